```python
import jax, jax.numpy as jnp
from jax import lax
import numpy as np

D_MODEL = 1024
BATCH = 8
SEQ = 2048
DEPTH = 2

CHUNK = 64
Q_BLOCK = 128
A_HEADS = 8
A_HEAD_DIM = D_MODEL // 16
A_WIDTH = A_HEADS * A_HEAD_DIM
POOL_WINDOWS = (2, 4, 8, 16)
POOL_GROUPS = len(POOL_WINDOWS)
POOL_WIDTH = D_MODEL // 4
POOL_GROUP_DIM = POOL_WIDTH // POOL_GROUPS
CONV_WIDTH = D_MODEL // 4
CONV_K = 3
N_BRANCH = 3
MIX_WIDTH = A_WIDTH + POOL_WIDTH + CONV_WIDTH
D_FF = 4 * D_MODEL
RMS_EPS = 1e-6
NEG_INF = -1e30
IN_COLS = 3 * A_WIDTH + A_HEADS + POOL_WIDTH + 3 * CONV_WIDTH + N_BRANCH * D_MODEL

kernel_name = 'hybrid_fox_pool_conv_block'


def rmsnorm(x, g):
    xf = x.astype(jnp.float32)
    y = xf * lax.rsqrt(jnp.mean(xf * xf, axis=-1, keepdims=True) + RMS_EPS)
    return (y * g.astype(jnp.float32)).astype(x.dtype)


def forgetting_attention(q, k, v, f_logit):
    b, s, h, dh = q.shape
    q = q.transpose(0, 2, 1, 3)
    k = k.transpose(0, 2, 1, 3)
    v = v.transpose(0, 2, 1, 3)
    log_f = jax.nn.log_sigmoid(f_logit.astype(jnp.float32))
    cum_f = jnp.cumsum(log_f, axis=1).transpose(0, 2, 1)
    scale = dh ** -0.5
    outs = []
    for i in range(s // Q_BLOCK):
        qs, qe = i * Q_BLOCK, (i + 1) * Q_BLOCK
        qb = q[:, :, qs:qe]
        kb = k[:, :, :qe]
        vb = v[:, :, :qe]
        logits = jnp.einsum('bhqd,bhkd->bhqk', qb, kb).astype(jnp.float32) * scale
        logits = logits + cum_f[:, :, qs:qe, None] - cum_f[:, :, None, :qe]
        causal = jnp.arange(qs, qe)[:, None] >= jnp.arange(qe)[None, :]
        logits = jnp.where(causal[None, None], logits, NEG_INF)
        p = jax.nn.softmax(logits, axis=-1).astype(v.dtype)
        outs.append(jnp.einsum('bhqk,bhkd->bhqd', p, vb))
    o = jnp.concatenate(outs, axis=2)
    return o.transpose(0, 2, 1, 3).reshape(b, s, h * dh)


def pool_mixer(u, w_pool, pool_scale):
    b, s, _ = u.shape
    uf = u.astype(jnp.float32)
    cs = jnp.cumsum(uf, axis=1)
    groups = []
    for g, w in enumerate(POOL_WINDOWS):
        sl = slice(g * POOL_GROUP_DIM, (g + 1) * POOL_GROUP_DIM)
        cs_g = cs[..., sl]
        lagged = jnp.pad(cs_g, ((0, 0), (w, 0), (0, 0)))[:, :s]
        count = jnp.minimum(jnp.arange(1, s + 1, dtype=jnp.float32), float(w))
        groups.append((cs_g - lagged) / count[None, :, None] - uf[..., sl])
    p = jnp.stack(groups, axis=2).astype(u.dtype)
    y = jnp.einsum('bsgc,gcd->bsgd', p, w_pool).reshape(b, s, POOL_WIDTH)
    return y * pool_scale


def short_conv(h, b_gate, c_gate, conv_w):
    u = c_gate * h
    y = lax.conv_general_dilated(
        u, conv_w[:, None, :].astype(u.dtype), window_strides=(1,), padding=[(CONV_K - 1, 0)],
        dimension_numbers=('NWC', 'WIO', 'NWC'), feature_group_count=CONV_WIDTH)
    return b_gate * y


def hybrid_layer(x, c, w_ada, b_ada, g_mix_pre, g_mix_post, g_ff_pre, g_ff_post, w_in, b_f,
                 w_pool, pool_scale, conv_w, w_branch, w_out, w_ff1, w_ff2):
    b, s, d = x.shape
    mod = jax.nn.silu(c) @ w_ada + b_ada
    shift_m, scale_m, gate_m, shift_f, scale_f, gate_f = jnp.split(mod, 6, axis=-1)

    h = rmsnorm(x, g_mix_pre) * (1.0 + scale_m[:, None]) + shift_m[:, None]
    z = h @ w_in
    sizes = [A_WIDTH, A_WIDTH, A_WIDTH, A_HEADS, POOL_WIDTH, CONV_WIDTH, CONV_WIDTH, CONV_WIDTH]
    cuts = [int(v) for v in np.cumsum(sizes)]
    q, k, v, fl, pu, ch, cb, cc, gl = jnp.split(z, cuts, axis=-1)
    heads = (b, s, A_HEADS, A_HEAD_DIM)
    br_a = forgetting_attention(q.reshape(heads), k.reshape(heads), v.reshape(heads), fl + b_f)
    br_b = pool_mixer(pu, w_pool, pool_scale)
    br_c = short_conv(ch, cb, cc, conv_w)
    gates = jax.nn.sigmoid(gl).reshape(b, s, N_BRANCH, d)
    wa = w_branch[:A_WIDTH]
    wb = w_branch[A_WIDTH:A_WIDTH + POOL_WIDTH]
    wc = w_branch[A_WIDTH + POOL_WIDTH:]
    merged = gates[:, :, 0] * (br_a @ wa) + gates[:, :, 1] * (br_b @ wb) + gates[:, :, 2] * (br_c @ wc)
    y = merged @ w_out
    x = x + gate_m[:, None] * rmsnorm(y, g_mix_post)

    h2 = rmsnorm(x, g_ff_pre) * (1.0 + scale_f[:, None]) + shift_f[:, None]
    y2 = jnp.square(jax.nn.relu(h2 @ w_ff1)) @ w_ff2
    return x + gate_f[:, None] * rmsnorm(y2, g_ff_post)


def _fwd_setup_inputs(seed: int = 0) -> dict:
    key = jax.random.key(seed)
    ks = jax.random.split(key, 20)

    def nrm(k, shape, std):
        return jax.random.normal(k, shape, jnp.float32) * std

    d = D_MODEL
    branch_row_scale = jnp.concatenate([
        jnp.full((A_WIDTH,), A_WIDTH ** -0.5, jnp.float32),
        jnp.full((POOL_WIDTH,), POOL_WIDTH ** -0.5, jnp.float32),
        jnp.full((CONV_WIDTH,), CONV_WIDTH ** -0.5, jnp.float32)])
    return {
        'x': nrm(ks[0], (BATCH, SEQ, d), 1.0),
        'c': nrm(ks[1], (BATCH, d), 1.0),
        'w_ada': nrm(ks[2], (DEPTH, d, 6 * d), 0.5 * d ** -0.5),
        'b_ada': nrm(ks[3], (DEPTH, 6 * d), 0.02),
        'g_mix_pre': 1.0 + nrm(ks[4], (DEPTH, d), 0.05),
        'g_mix_post': 1.0 + nrm(ks[5], (DEPTH, d), 0.05),
        'g_ff_pre': 1.0 + nrm(ks[6], (DEPTH, d), 0.05),
        'g_ff_post': 1.0 + nrm(ks[7], (DEPTH, d), 0.05),
        'w_in': nrm(ks[8], (DEPTH, d, IN_COLS), d ** -0.5),
        'b_f': 3.0 + nrm(ks[9], (DEPTH, A_HEADS), 0.1),
        'w_pool': nrm(ks[10], (DEPTH, POOL_GROUPS, POOL_GROUP_DIM, POOL_GROUP_DIM), POOL_GROUP_DIM ** -0.5),
        'pool_scale': 1.0 + nrm(ks[11], (DEPTH, POOL_WIDTH), 0.1),
        'conv_w': nrm(ks[12], (DEPTH, CONV_K, CONV_WIDTH), CONV_K ** -0.5),
        'w_branch': nrm(ks[13], (DEPTH, MIX_WIDTH, d), 1.0) * branch_row_scale[None, :, None],
        'w_out': nrm(ks[14], (DEPTH, d, d), d ** -0.5),
        'w_ff1': nrm(ks[15], (DEPTH, d, D_FF), d ** -0.5),
        'w_ff2': nrm(ks[16], (DEPTH, D_FF, d), D_FF ** -0.5),
    }


def _fwd_reference(x, c, w_ada, b_ada, g_mix_pre, g_mix_post, g_ff_pre, g_ff_post, w_in, b_f,
              w_pool, pool_scale, conv_w, w_branch, w_out, w_ff1, w_ff2):
    for l in range(DEPTH):
        x = hybrid_layer(x, c, w_ada[l], b_ada[l], g_mix_pre[l], g_mix_post[l], g_ff_pre[l], g_ff_post[l],
                         w_in[l], b_f[l], w_pool[l], pool_scale[l], conv_w[l], w_branch[l], w_out[l],
                         w_ff1[l], w_ff2[l])
    return x


import jax as _jax
import jax.numpy as _jnp

TWIN_FORMAT = 'train_step'
FWD_PARAMS = ['x', 'c', 'w_ada', 'b_ada', 'g_mix_pre', 'g_mix_post', 'g_ff_pre', 'g_ff_post', 'w_in', 'b_f', 'w_pool', 'pool_scale', 'conv_w', 'w_branch', 'w_out', 'w_ff1', 'w_ff2']
TWIN_WEIGHTS = ['w_ada', 'b_ada', 'g_mix_pre', 'g_mix_post', 'g_ff_pre', 'g_ff_post', 'w_in', 'b_f', 'w_pool', 'pool_scale', 'conv_w', 'w_branch', 'w_out', 'w_ff1', 'w_ff2']
TWIN_DIFF_INPUT = 'x'
TWIN_INPUTS = ['x', 'c', 'w_ada', 'b_ada', 'g_mix_pre', 'g_mix_post', 'g_ff_pre', 'g_ff_post', 'w_in', 'b_f', 'w_pool', 'pool_scale', 'conv_w', 'w_branch', 'w_out', 'w_ff1', 'w_ff2', 'loss_target', 'm_w_ada', 'm_b_ada', 'm_g_mix_pre', 'm_g_mix_post', 'm_g_ff_pre', 'm_g_ff_post', 'm_w_in', 'm_b_f', 'm_w_pool', 'm_pool_scale', 'm_conv_w', 'm_w_branch', 'm_w_out', 'm_w_ff1', 'm_w_ff2', 'v_w_ada', 'v_b_ada', 'v_g_mix_pre', 'v_g_mix_post', 'v_g_ff_pre', 'v_g_ff_post', 'v_w_in', 'v_b_f', 'v_w_pool', 'v_pool_scale', 'v_conv_w', 'v_w_branch', 'v_w_out', 'v_w_ff1', 'v_w_ff2']
TWIN_OUTPUTS = ['loss', 'grad_x', 'grad_w_ada', 'grad_b_ada', 'grad_g_mix_pre', 'grad_g_mix_post', 'grad_g_ff_pre', 'grad_g_ff_post', 'grad_w_in', 'grad_b_f', 'grad_w_pool', 'grad_pool_scale', 'grad_conv_w', 'grad_w_branch', 'grad_w_out', 'grad_w_ff1', 'grad_w_ff2', 'delta_w_ada', 'delta_b_ada', 'delta_g_mix_pre', 'delta_g_mix_post', 'delta_g_ff_pre', 'delta_g_ff_post', 'delta_w_in', 'delta_b_f', 'delta_w_pool', 'delta_pool_scale', 'delta_conv_w', 'delta_w_branch', 'delta_w_out', 'delta_w_ff1', 'delta_w_ff2', 'new_m_w_ada', 'new_m_b_ada', 'new_m_g_mix_pre', 'new_m_g_mix_post', 'new_m_g_ff_pre', 'new_m_g_ff_post', 'new_m_w_in', 'new_m_b_f', 'new_m_w_pool', 'new_m_pool_scale', 'new_m_conv_w', 'new_m_w_branch', 'new_m_w_out', 'new_m_w_ff1', 'new_m_w_ff2', 'new_v_w_ada', 'new_v_b_ada', 'new_v_g_mix_pre', 'new_v_g_mix_post', 'new_v_g_ff_pre', 'new_v_g_ff_post', 'new_v_w_in', 'new_v_b_f', 'new_v_w_pool', 'new_v_pool_scale', 'new_v_conv_w', 'new_v_w_branch', 'new_v_w_out', 'new_v_w_ff1', 'new_v_w_ff2']
TWIN_LEAF_KINDS = {'loss': 'loss', 'grad_x': 'grad_x', 'grad_w_ada': 'grad_w', 'grad_b_ada': 'grad_w', 'grad_g_mix_pre': 'grad_w', 'grad_g_mix_post': 'grad_w', 'grad_g_ff_pre': 'grad_w', 'grad_g_ff_post': 'grad_w', 'grad_w_in': 'grad_w', 'grad_b_f': 'grad_w', 'grad_w_pool': 'grad_w', 'grad_pool_scale': 'grad_w', 'grad_conv_w': 'grad_w', 'grad_w_branch': 'grad_w', 'grad_w_out': 'grad_w', 'grad_w_ff1': 'grad_w', 'grad_w_ff2': 'grad_w', 'delta_w_ada': 'delta_w', 'delta_b_ada': 'delta_w', 'delta_g_mix_pre': 'delta_w', 'delta_g_mix_post': 'delta_w', 'delta_g_ff_pre': 'delta_w', 'delta_g_ff_post': 'delta_w', 'delta_w_in': 'delta_w', 'delta_b_f': 'delta_w', 'delta_w_pool': 'delta_w', 'delta_pool_scale': 'delta_w', 'delta_conv_w': 'delta_w', 'delta_w_branch': 'delta_w', 'delta_w_out': 'delta_w', 'delta_w_ff1': 'delta_w', 'delta_w_ff2': 'delta_w', 'new_m_w_ada': 'new_m', 'new_m_b_ada': 'new_m', 'new_m_g_mix_pre': 'new_m', 'new_m_g_mix_post': 'new_m', 'new_m_g_ff_pre': 'new_m', 'new_m_g_ff_post': 'new_m', 'new_m_w_in': 'new_m', 'new_m_b_f': 'new_m', 'new_m_w_pool': 'new_m', 'new_m_pool_scale': 'new_m', 'new_m_conv_w': 'new_m', 'new_m_w_branch': 'new_m', 'new_m_w_out': 'new_m', 'new_m_w_ff1': 'new_m', 'new_m_w_ff2': 'new_m', 'new_v_w_ada': 'new_v', 'new_v_b_ada': 'new_v', 'new_v_g_mix_pre': 'new_v', 'new_v_g_mix_post': 'new_v', 'new_v_g_ff_pre': 'new_v', 'new_v_g_ff_post': 'new_v', 'new_v_w_in': 'new_v', 'new_v_b_f': 'new_v', 'new_v_w_pool': 'new_v', 'new_v_pool_scale': 'new_v', 'new_v_conv_w': 'new_v', 'new_v_w_branch': 'new_v', 'new_v_w_out': 'new_v', 'new_v_w_ff1': 'new_v', 'new_v_w_ff2': 'new_v'}


def _forward(args):
    return _fwd_reference(*[args[k] for k in FWD_PARAMS])


def _output_shape():
    out = _jax.eval_shape(lambda: _forward(_fwd_setup_inputs(0)))
    return out.shape, out.dtype

N_MICROBATCH = 1
ADAM_LR = 0.001
ADAM_B1 = 0.9
ADAM_B2 = 0.999
ADAM_EPS = 1e-08
ADAM_WD = 0.01
ADAM_STEP = 10
PER_EXAMPLE_BATCH_AXIS = {'x': 0, 'c': 0, 'loss_target': 0}
SHARED_INPUTS = []
_WEIGHT_DTYPES = {'w_ada': _jnp.float32, 'b_ada': _jnp.float32, 'g_mix_pre': _jnp.float32, 'g_mix_post': _jnp.float32, 'g_ff_pre': _jnp.float32, 'g_ff_post': _jnp.float32, 'w_in': _jnp.float32, 'b_f': _jnp.float32, 'w_pool': _jnp.float32, 'pool_scale': _jnp.float32, 'conv_w': _jnp.float32, 'w_branch': _jnp.float32, 'w_out': _jnp.float32, 'w_ff1': _jnp.float32, 'w_ff2': _jnp.float32}
MOMENT_SCALE = {'w_ada': 9.199897e-01, 'b_ada': 1.708763e+00, 'g_mix_pre': 1.061313e-01, 'g_mix_post': 1.906031e+00, 'g_ff_pre': 9.341143e-02, 'g_ff_post': 1.890727e+00, 'w_in': 5.719582e-02, 'b_f': 8.847063e-02, 'w_pool': 1.077194e-01, 'pool_scale': 1.253718e-01, 'conv_w': 1.141726e-01, 'w_branch': 6.850270e-02, 'w_out': 1.152995e-01, 'w_ff1': 6.654309e-02, 'w_ff2': 2.512386e-01}


def _to_microbatches(a, axis):
    t = _jnp.moveaxis(a, axis, 0)
    t = t.reshape((N_MICROBATCH, t.shape[0] // N_MICROBATCH) + t.shape[1:])
    return _jnp.moveaxis(t, 1, axis + 1)


def setup_inputs(seed: int = 0) -> dict:
    inp = _fwd_setup_inputs(seed)
    key = _jax.random.fold_in(_jax.random.key(seed), 7919)
    shape, _ = _output_shape()
    out = dict(inp)
    out["loss_target"] = _jax.random.normal(_jax.random.fold_in(key, 0), shape, _jnp.float32)
    for i, name in enumerate(TWIN_WEIGHTS):
        w = inp[name].astype(_jnp.float32)
        if MOMENT_SCALE is None:
            s = _jnp.sqrt(_jnp.mean(_jnp.square(w)) + 1e-30)
        else:
            s = MOMENT_SCALE[name]
        km, kv = _jax.random.split(_jax.random.fold_in(key, i + 1))
        out[name] = w
        out["m_" + name] = s * _jax.random.normal(km, w.shape, _jnp.float32)
        out["v_" + name] = (s * s) * _jax.random.uniform(kv, w.shape, _jnp.float32, 0.5, 1.5)
    if N_MICROBATCH > 1:
        for name, axis in PER_EXAMPLE_BATCH_AXIS.items():
            out[name] = _to_microbatches(out[name], axis)
    return {'x': out['x'], 'c': out['c'], 'w_ada': out['w_ada'], 'b_ada': out['b_ada'], 'g_mix_pre': out['g_mix_pre'], 'g_mix_post': out['g_mix_post'], 'g_ff_pre': out['g_ff_pre'], 'g_ff_post': out['g_ff_post'], 'w_in': out['w_in'], 'b_f': out['b_f'], 'w_pool': out['w_pool'], 'pool_scale': out['pool_scale'], 'conv_w': out['conv_w'], 'w_branch': out['w_branch'], 'w_out': out['w_out'], 'w_ff1': out['w_ff1'], 'w_ff2': out['w_ff2'], 'loss_target': out['loss_target'], 'm_w_ada': out['m_w_ada'], 'm_b_ada': out['m_b_ada'], 'm_g_mix_pre': out['m_g_mix_pre'], 'm_g_mix_post': out['m_g_mix_post'], 'm_g_ff_pre': out['m_g_ff_pre'], 'm_g_ff_post': out['m_g_ff_post'], 'm_w_in': out['m_w_in'], 'm_b_f': out['m_b_f'], 'm_w_pool': out['m_w_pool'], 'm_pool_scale': out['m_pool_scale'], 'm_conv_w': out['m_conv_w'], 'm_w_branch': out['m_w_branch'], 'm_w_out': out['m_w_out'], 'm_w_ff1': out['m_w_ff1'], 'm_w_ff2': out['m_w_ff2'], 'v_w_ada': out['v_w_ada'], 'v_b_ada': out['v_b_ada'], 'v_g_mix_pre': out['v_g_mix_pre'], 'v_g_mix_post': out['v_g_mix_post'], 'v_g_ff_pre': out['v_g_ff_pre'], 'v_g_ff_post': out['v_g_ff_post'], 'v_w_in': out['v_w_in'], 'v_b_f': out['v_b_f'], 'v_w_pool': out['v_w_pool'], 'v_pool_scale': out['v_pool_scale'], 'v_conv_w': out['v_conv_w'], 'v_w_branch': out['v_w_branch'], 'v_w_out': out['v_w_out'], 'v_w_ff1': out['v_w_ff1'], 'v_w_ff2': out['v_w_ff2']}


def _loss(weights, diff, rest, loss_target):
    with _jax.named_scope("forward"):
        args = {**rest, TWIN_DIFF_INPUT: diff, **{k: w.astype(_WEIGHT_DTYPES[k]) for k, w in weights.items()}}
        y = _forward(args)
    with _jax.named_scope("loss_head"):
        err = _jnp.square(y.astype(_jnp.float32) - loss_target)
        return 0.5 * _jnp.sum(_jnp.mean(err, axis=-1)) if err.ndim else 0.5 * err


def _adamw(w, g, m, v):
    m = ADAM_B1 * m + (1.0 - ADAM_B1) * g
    v = ADAM_B2 * v + (1.0 - ADAM_B2) * _jnp.square(g)
    m_hat = m / (1.0 - ADAM_B1 ** ADAM_STEP)
    v_hat = v / (1.0 - ADAM_B2 ** ADAM_STEP)
    delta = -ADAM_LR * (m_hat / (_jnp.sqrt(v_hat) + ADAM_EPS) + ADAM_WD * w)
    return delta, m, v


def reference(x, c, w_ada, b_ada, g_mix_pre, g_mix_post, g_ff_pre, g_ff_post, w_in, b_f, w_pool, pool_scale, conv_w, w_branch, w_out, w_ff1, w_ff2, loss_target, m_w_ada, m_b_ada, m_g_mix_pre, m_g_mix_post, m_g_ff_pre, m_g_ff_post, m_w_in, m_b_f, m_w_pool, m_pool_scale, m_conv_w, m_w_branch, m_w_out, m_w_ff1, m_w_ff2, v_w_ada, v_b_ada, v_g_mix_pre, v_g_mix_post, v_g_ff_pre, v_g_ff_post, v_w_in, v_b_f, v_w_pool, v_pool_scale, v_conv_w, v_w_branch, v_w_out, v_w_ff1, v_w_ff2):
    given = dict(x=x, c=c, w_ada=w_ada, b_ada=b_ada, g_mix_pre=g_mix_pre, g_mix_post=g_mix_post, g_ff_pre=g_ff_pre, g_ff_post=g_ff_post, w_in=w_in, b_f=b_f, w_pool=w_pool, pool_scale=pool_scale, conv_w=conv_w, w_branch=w_branch, w_out=w_out, w_ff1=w_ff1, w_ff2=w_ff2, loss_target=loss_target, m_w_ada=m_w_ada, m_b_ada=m_b_ada, m_g_mix_pre=m_g_mix_pre, m_g_mix_post=m_g_mix_post, m_g_ff_pre=m_g_ff_pre, m_g_ff_post=m_g_ff_post, m_w_in=m_w_in, m_b_f=m_b_f, m_w_pool=m_w_pool, m_pool_scale=m_pool_scale, m_conv_w=m_conv_w, m_w_branch=m_w_branch, m_w_out=m_w_out, m_w_ff1=m_w_ff1, m_w_ff2=m_w_ff2, v_w_ada=v_w_ada, v_b_ada=v_b_ada, v_g_mix_pre=v_g_mix_pre, v_g_mix_post=v_g_mix_post, v_g_ff_pre=v_g_ff_pre, v_g_ff_post=v_g_ff_post, v_w_in=v_w_in, v_b_f=v_b_f, v_w_pool=v_w_pool, v_pool_scale=v_pool_scale, v_conv_w=v_conv_w, v_w_branch=v_w_branch, v_w_out=v_w_out, v_w_ff1=v_w_ff1, v_w_ff2=v_w_ff2)
    weights = {n: given[n] for n in TWIN_WEIGHTS}
    shared = {n: given[n] for n in SHARED_INPUTS}
    per_example = {n: given[n] for n in ['x', 'c']}
    grad_fn = _jax.value_and_grad(_loss, argnums=(0, 1))

    def one_microbatch(ex, loss_target):
        ex = dict(ex)
        diff = ex.pop(TWIN_DIFF_INPUT)
        return grad_fn(weights, diff, {**shared, **ex}, loss_target)

    if N_MICROBATCH == 1:
        loss, (grad_w, grad_x) = one_microbatch(per_example, given["loss_target"])
    else:
        def body(carry, xs):
            loss_sum, grad_sum = carry
            l_k, (gw_k, gx_k) = one_microbatch(xs[0], xs[1])
            with _jax.named_scope("update"):
                return (loss_sum + l_k, _jax.tree.map(_jnp.add, grad_sum, gw_k)), gx_k

        init = (_jnp.zeros((), _jnp.float32), _jax.tree.map(_jnp.zeros_like, weights))
        (loss, grad_w), grad_x = _jax.lax.scan(body, init, (per_example, given["loss_target"]))
    with _jax.named_scope("update"):
        delta_w, new_m, new_v = {}, {}, {}
        for n in TWIN_WEIGHTS:
            delta_w[n], new_m[n], new_v[n] = _adamw(weights[n], grad_w[n], given["m_" + n], given["v_" + n])
    return (loss, grad_x, *[grad_w[n] for n in TWIN_WEIGHTS], *[delta_w[n] for n in TWIN_WEIGHTS],
            *[new_m[n] for n in TWIN_WEIGHTS], *[new_v[n] for n in TWIN_WEIGHTS])
```

```python
import functools

import jax
import jax.numpy as jnp
from jax import lax
from jax.experimental import pallas as pl
from jax.experimental.pallas import tpu as pltpu

F32 = jnp.float32
BF16 = jnp.bfloat16

N_DEV = 8
D = 1024
S = 2048
DEPTH = 2
D_FF = 4 * D
A_WIDTH = 512
HEAD_DIM = 64
N_PAIR = 4
POOL_W = 256
CONV_W = 256
IN_COLS = 5640
ADA_COLS = 6 * D // N_DEV
IN_SHARD = IN_COLS // N_DEV
RMS_EPS = 1e-6
NEG_INF = -1e30
ATT_SCALE = HEAD_DIM ** -0.5

NZ = 5760
Z_PC = 0
Z_G = 1024
Z_Q = 4096
Z_K = 4608
Z_V = 5120
Z_F = 5632

LR, B1, B2, EPS, WD, STEP = 0.001, 0.9, 0.999, 1e-08, 0.01, 10

LANE = 128
VMEM_LIMIT_BYTES = 48 * 1024 * 1024
TS = 256
TQ = 256


def _params(sem=None):
    return pltpu.CompilerParams(dimension_semantics=sem, vmem_limit_bytes=VMEM_LIMIT_BYTES)


def _pick(n, target):
    best = None
    for t in range(LANE, min(n, target) + 1, LANE):
        if n % t == 0:
            best = t
    return n if best is None else best


def _matmul(a, b, mode, name, out_dtype=F32, tm=512, tn=1024, tk=512, b_col_shards=False, out_col_shards=False):
    if b_col_shards:
        shards, b_rows, shard_cols = b.shape
        b_shape = (b_rows, shards * shard_cols)
    else:
        b_shape = b.shape
    if mode == "nn":
        (m, k), (k2, n) = a.shape, b_shape
    elif mode == "nt":
        (m, k), (n, k2) = a.shape, b_shape
    else:
        (k, m), (k2, n) = a.shape, b_shape
    assert k == k2, (a.shape, b.shape, mode)
    tm, tn, tk = _pick(m, tm), _pick(n, tn), _pick(k, tk)
    if b_col_shards and mode == "nn":
        tn = shard_cols
    if b_col_shards and mode == "nt":
        tk = shard_cols
    if out_col_shards:
        tn = n // N_DEV
    nk = k // tk
    if mode == "nn":
        a_spec = pl.BlockSpec((tm, tk), lambda i, j, kk: (i, kk))
        b_spec = (pl.BlockSpec((None, tk, tn), lambda i, j, kk: (j, kk, 0)) if b_col_shards else
                  pl.BlockSpec((tk, tn), lambda i, j, kk: (kk, j)))
        dims = (((1,), (0,)), ((), ()))
    elif mode == "nt":
        a_spec = pl.BlockSpec((tm, tk), lambda i, j, kk: (i, kk))
        b_spec = (pl.BlockSpec((None, tn, tk), lambda i, j, kk: (kk, j, 0)) if b_col_shards else
                  pl.BlockSpec((tn, tk), lambda i, j, kk: (j, kk)))
        dims = (((1,), (1,)), ((), ()))
    else:
        assert not b_col_shards
        a_spec = pl.BlockSpec((tk, tm), lambda i, j, kk: (kk, i))
        b_spec = pl.BlockSpec((tk, tn), lambda i, j, kk: (kk, j))
        dims = (((0,), (0,)), ((), ()))
    if out_col_shards:
        out_shape = jax.ShapeDtypeStruct((N_DEV, m, tn), out_dtype)
        out_spec = pl.BlockSpec((None, tm, tn), lambda i, j, kk: (j, i, 0))
    else:
        out_shape = jax.ShapeDtypeStruct((m, n), out_dtype)
        out_spec = pl.BlockSpec((tm, tn), lambda i, j, kk: (i, j))

    def body(a_ref, b_ref, o_ref, acc_ref):
        kk = pl.program_id(2)

        @pl.when(kk == 0)
        def _():
            acc_ref[...] = jnp.zeros_like(acc_ref)

        acc_ref[...] += lax.dot_general(a_ref[...].astype(BF16), b_ref[...].astype(BF16), dims,
                                        preferred_element_type=F32)

        @pl.when(kk == nk - 1)
        def _():
            o_ref[...] = acc_ref[...].astype(out_dtype)

    return pl.pallas_call(
        body, name=name,
        out_shape=out_shape,
        grid=(m // tm, n // tn, nk),
        in_specs=[a_spec, b_spec],
        out_specs=out_spec,
        scratch_shapes=[pltpu.VMEM((tm, tn), F32)],
        compiler_params=_params(("parallel", "parallel", "arbitrary")),
    )(a, b)


def _row_spec(width=D, col=0):
    return pl.BlockSpec((TS, width), lambda i: (i, col))


def _vec_spec(rows=8, width=D):
    return pl.BlockSpec((rows, width), lambda i: (0, 0))


def _rms(x):
    return lax.rsqrt(jnp.mean(x * x, axis=-1, keepdims=True) + RMS_EPS)


def _prenorm_fwd(x, gvec, mod, g_row, shift_row, scale_row, name):
    def body(x_ref, g_ref, mod_ref, h_ref):
        xv = x_ref[...]
        y = xv * _rms(xv) * g_ref[g_row:g_row + 1, :]
        h = y * (1.0 + mod_ref[scale_row:scale_row + 1, :]) + mod_ref[shift_row:shift_row + 1, :]
        h_ref[...] = h.astype(BF16)

    return pl.pallas_call(
        body, name=name, out_shape=jax.ShapeDtypeStruct((S, D), BF16), grid=(S // TS,),
        in_specs=[_row_spec(), _vec_spec(), _vec_spec()], out_specs=_row_spec(),
        compiler_params=_params(("parallel",)),
    )(x, gvec, mod)


def _prenorm_bwd(x, gvec, mod, dh, dres, g_row, scale_row, name):
    def body(x_ref, g_ref, mod_ref, dh_ref, dres_ref, dx_ref, red_ref):
        i = pl.program_id(0)

        @pl.when(i == 0)
        def _():
            red_ref[...] = jnp.zeros_like(red_ref)

        xv = x_ref[...]
        g = g_ref[g_row:g_row + 1, :]
        r = _rms(xv)
        n = xv * r
        yg = n * g
        dhv = dh_ref[...]
        dyg = dhv * (1.0 + mod_ref[scale_row:scale_row + 1, :])
        dn = dyg * g
        dx = r * (dn - n * jnp.mean(dn * n, axis=-1, keepdims=True))
        dx_ref[...] = dres_ref[...] + dx
        red_ref[0:1, :] += jnp.sum(dhv, axis=0, keepdims=True)
        red_ref[1:2, :] += jnp.sum(dhv * yg, axis=0, keepdims=True)
        red_ref[2:3, :] += jnp.sum(dyg * n, axis=0, keepdims=True)

    return pl.pallas_call(
        body, name=name,
        out_shape=(jax.ShapeDtypeStruct((S, D), F32), jax.ShapeDtypeStruct((8, D), F32)),
        grid=(S // TS,),
        in_specs=[_row_spec(), _vec_spec(), _vec_spec(), _row_spec(), _row_spec()],
        out_specs=(_row_spec(), _vec_spec()),
        compiler_params=_params(("arbitrary",)),
    )(x, gvec, mod, dh, dres)


def _postnorm_fwd(x, y, gvec, mod, g_row, gate_row, name):
    def body(x_ref, y_ref, g_ref, mod_ref, o_ref):
        yv = y_ref[...]
        yn = yv * _rms(yv) * g_ref[g_row:g_row + 1, :]
        o_ref[...] = x_ref[...] + mod_ref[gate_row:gate_row + 1, :] * yn

    return pl.pallas_call(
        body, name=name, out_shape=jax.ShapeDtypeStruct((S, D), F32), grid=(S // TS,),
        in_specs=[_row_spec(), _row_spec(), _vec_spec(), _vec_spec()], out_specs=_row_spec(),
        compiler_params=_params(("parallel",)),
    )(x, y, gvec, mod)


def _postnorm_bwd(y, gvec, mod, dxo, g_row, gate_row, name):
    def body(y_ref, g_ref, mod_ref, dxo_ref, dy_ref, red_ref):
        i = pl.program_id(0)

        @pl.when(i == 0)
        def _():
            red_ref[...] = jnp.zeros_like(red_ref)

        yv = y_ref[...]
        g = g_ref[g_row:g_row + 1, :]
        r = _rms(yv)
        n = yv * r
        dxo = dxo_ref[...]
        dyn = dxo * mod_ref[gate_row:gate_row + 1, :]
        dn = dyn * g
        dy = r * (dn - n * jnp.mean(dn * n, axis=-1, keepdims=True))
        dy_ref[...] = dy.astype(BF16)
        red_ref[0:1, :] += jnp.sum(dxo * (n * g), axis=0, keepdims=True)
        red_ref[1:2, :] += jnp.sum(dyn * n, axis=0, keepdims=True)

    return pl.pallas_call(
        body, name=name,
        out_shape=(jax.ShapeDtypeStruct((S, D), BF16), jax.ShapeDtypeStruct((8, D), F32)),
        grid=(S // TS,),
        in_specs=[_row_spec(), _vec_spec(), _vec_spec(), _row_spec()],
        out_specs=(_row_spec(), _vec_spec()),
        compiler_params=_params(("arbitrary",)),
    )(y, gvec, mod, dxo)


def _loss_head(xf, target, name):
    def body(x_ref, t_ref, dx_ref, loss_ref):
        i = pl.program_id(0)

        @pl.when(i == 0)
        def _():
            loss_ref[...] = jnp.zeros_like(loss_ref)

        e = x_ref[...] - t_ref[...]
        dx_ref[...] = e / float(D)
        per_tok = jnp.mean(e * e, axis=-1, keepdims=True)
        loss_ref[0:1, 0:1] += 0.5 * jnp.sum(per_tok, axis=0, keepdims=True)

    return pl.pallas_call(
        body, name=name,
        out_shape=(jax.ShapeDtypeStruct((S, D), F32), jax.ShapeDtypeStruct((8, LANE), F32)),
        grid=(S // TS,),
        in_specs=[_row_spec(), _row_spec()],
        out_specs=(_row_spec(), pl.BlockSpec((8, LANE), lambda i: (0, 0))),
        compiler_params=_params(("arbitrary",)),
    )(xf, target)


def _relu2_fwd(a, name):
    def body(a_ref, r_ref):
        t = jnp.maximum(a_ref[...], 0.0)
        r_ref[...] = (t * t).astype(BF16)

    return pl.pallas_call(
        body, name=name, out_shape=jax.ShapeDtypeStruct((S, D_FF), BF16), grid=(S // TS,),
        in_specs=[_row_spec(D_FF)], out_specs=_row_spec(D_FF),
        compiler_params=_params(("parallel",)),
    )(a)


def _relu2_bwd(a, dr, name):
    def body(a_ref, dr_ref, da_ref):
        da_ref[...] = (dr_ref[...] * (2.0 * jnp.maximum(a_ref[...], 0.0))).astype(BF16)

    return pl.pallas_call(
        body, name=name, out_shape=jax.ShapeDtypeStruct((S, D_FF), BF16), grid=(S // TS,),
        in_specs=[_row_spec(D_FF), _row_spec(D_FF)], out_specs=_row_spec(D_FF),
        compiler_params=_params(("parallel",)),
    )(a, dr)


def _merge_fwd(z, pa, pb, pc, name):
    def body(g0_ref, g1_ref, g2_ref, pa_ref, pb_ref, pc_ref, o_ref):
        m = (jax.nn.sigmoid(g0_ref[...]) * pa_ref[...] + jax.nn.sigmoid(g1_ref[...]) * pb_ref[...]
             + jax.nn.sigmoid(g2_ref[...]) * pc_ref[...])
        o_ref[...] = m.astype(BF16)

    gb = Z_G // D
    return pl.pallas_call(
        body, name=name, out_shape=jax.ShapeDtypeStruct((S, D), BF16), grid=(S // TS,),
        in_specs=[_row_spec(D, gb), _row_spec(D, gb + 1), _row_spec(D, gb + 2), _row_spec(), _row_spec(), _row_spec()],
        out_specs=_row_spec(),
        compiler_params=_params(("parallel",)),
    )(z, z, z, pa, pb, pc)


def _merge_bwd(z, pa, pb, pc, dm, name):
    def body(g0_ref, g1_ref, g2_ref, pa_ref, pb_ref, pc_ref, dm_ref, da_ref, db_ref, dc_ref, dgl_ref):
        dmv = dm_ref[...]
        for k, (g_ref, p_ref, d_ref) in enumerate(((g0_ref, pa_ref, da_ref), (g1_ref, pb_ref, db_ref),
                                                   (g2_ref, pc_ref, dc_ref))):
            sg = jax.nn.sigmoid(g_ref[...])
            d_ref[...] = (dmv * sg).astype(BF16)
            dgl_ref[:, k * D:(k + 1) * D] = (dmv * p_ref[...] * (sg * (1.0 - sg))).astype(BF16)

    gb = Z_G // D
    proj = jax.ShapeDtypeStruct((S, D), BF16)
    return pl.pallas_call(
        body, name=name,
        out_shape=(proj, proj, proj, jax.ShapeDtypeStruct((S, 3 * D), BF16)),
        grid=(S // TS,),
        in_specs=[_row_spec(D, gb), _row_spec(D, gb + 1), _row_spec(D, gb + 2), _row_spec(), _row_spec(), _row_spec(),
                  _row_spec()],
        out_specs=(_row_spec(), _row_spec(), _row_spec(), _row_spec(3 * D)),
        compiler_params=_params(("parallel",)),
    )(z, z, z, pa, pb, pc, dm)


def _shift_down(x, k, row):
    return jnp.where(row >= k, pltpu.roll(x, k, axis=0), 0.0)


def _shift_up(x, k, row):
    n = x.shape[0]
    return jnp.where(row < n - k, pltpu.roll(x, n - k, axis=0), 0.0)


def _cumsum_rows(x, row, reverse=False):
    shift = _shift_up if reverse else _shift_down
    k = 1
    while k < x.shape[0]:
        x = x + shift(x, k, row)
        k *= 2
    return x


def _full_spec(shape, idx=(0, 0)):
    return pl.BlockSpec(shape, lambda i: idx)


def _pool_window_select(lane, a2, a4, a8, a16):
    return jnp.where(lane < 64, a2, jnp.where(lane < 128, a4, jnp.where(lane < 192, a8, a16)))


def _pool_p(u, row, lane):
    t2 = u + _shift_down(u, 1, row)
    t4 = t2 + _shift_down(t2, 2, row)
    t8 = t4 + _shift_down(t4, 4, row)
    t16 = t8 + _shift_down(t8, 8, row)
    tw = _pool_window_select(lane, t2, t4, t8, t16)
    cnt = jnp.minimum((row + 1).astype(F32), _pool_window_select(lane, 2.0, 4.0, 8.0, 16.0))
    return tw / cnt - u, cnt


def _pool_fwd(z, wp_bd, pscale, name):
    def body(u_ref, w_ref, s_ref, o_ref):
        row = lax.broadcasted_iota(jnp.int32, (S, POOL_W), 0)
        lane = lax.broadcasted_iota(jnp.int32, (S, POOL_W), 1)
        p, _ = _pool_p(u_ref[...], row, lane)
        y = jnp.dot(p.astype(BF16), w_ref[...], preferred_element_type=F32)
        o_ref[...] = y * s_ref[0:1, :]

    return pl.pallas_call(
        body, name=name, out_shape=jax.ShapeDtypeStruct((S, POOL_W), F32), grid=(1,),
        in_specs=[_full_spec((S, POOL_W), (0, Z_PC // POOL_W)), _full_spec((POOL_W, POOL_W)), _full_spec((8, POOL_W))],
        out_specs=_full_spec((S, POOL_W)),
        compiler_params=_params(("arbitrary",)),
    )(z, wp_bd, pscale)


def _pool_bwd(z, wp_bd, pscale, dbr, name):
    def body(u_ref, w_ref, s_ref, dbr_ref, du_ref, dw_ref, red_ref):
        row = lax.broadcasted_iota(jnp.int32, (S, POOL_W), 0)
        lane = lax.broadcasted_iota(jnp.int32, (S, POOL_W), 1)
        p, cnt = _pool_p(u_ref[...], row, lane)
        pb = p.astype(BF16)
        y = jnp.dot(pb, w_ref[...], preferred_element_type=F32)
        dbr = dbr_ref[...]
        red_ref[...] = jnp.zeros_like(red_ref)
        red_ref[0:1, :] = jnp.sum(dbr * y, axis=0, keepdims=True)
        dy = (dbr * s_ref[0:1, :]).astype(BF16)
        dw_ref[...] = lax.dot_general(pb, dy, (((0,), (0,)), ((), ())), preferred_element_type=F32)
        dp = lax.dot_general(dy, w_ref[...], (((1,), (1,)), ((), ())), preferred_element_type=F32)
        g = dp / cnt
        a2 = g + _shift_up(g, 1, row)
        a4 = a2 + _shift_up(a2, 2, row)
        a8 = a4 + _shift_up(a4, 4, row)
        a16 = a8 + _shift_up(a8, 8, row)
        du_ref[...] = (_pool_window_select(lane, a2, a4, a8, a16) - dp).astype(BF16)

    return pl.pallas_call(
        body, name=name,
        out_shape=(jax.ShapeDtypeStruct((S, POOL_W), BF16), jax.ShapeDtypeStruct((POOL_W, POOL_W), F32),
                   jax.ShapeDtypeStruct((8, POOL_W), F32)),
        grid=(1,),
        in_specs=[_full_spec((S, POOL_W), (0, Z_PC // POOL_W)), _full_spec((POOL_W, POOL_W)), _full_spec((8, POOL_W)),
                  _full_spec((S, POOL_W))],
        out_specs=(_full_spec((S, POOL_W)), _full_spec((POOL_W, POOL_W)), _full_spec((8, POOL_W))),
        compiler_params=_params(("arbitrary",)),
    )(z, wp_bd, pscale, dbr)


def _conv_specs():
    base = Z_PC // CONV_W
    return [_full_spec((S, CONV_W), (0, base + 1)), _full_spec((S, CONV_W), (0, base + 2)),
            _full_spec((S, CONV_W), (0, base + 3)), _full_spec((8, CONV_W))]


def _conv_fwd(z, cw, name):
    def body(h_ref, b_ref, c_ref, w_ref, o_ref):
        row = lax.broadcasted_iota(jnp.int32, (S, CONV_W), 0)
        u = c_ref[...] * h_ref[...]
        y = (w_ref[0:1, :] * _shift_down(u, 2, row) + w_ref[1:2, :] * _shift_down(u, 1, row) + w_ref[2:3, :] * u)
        o_ref[...] = b_ref[...] * y

    return pl.pallas_call(
        body, name=name, out_shape=jax.ShapeDtypeStruct((S, CONV_W), F32), grid=(1,),
        in_specs=_conv_specs(), out_specs=_full_spec((S, CONV_W)),
        compiler_params=_params(("arbitrary",)),
    )(z, z, z, cw)


def _conv_bwd(z, cw, dbr, name):
    def body(h_ref, b_ref, c_ref, w_ref, dbr_ref, d_ref, red_ref):
        row = lax.broadcasted_iota(jnp.int32, (S, CONV_W), 0)
        h, cg = h_ref[...], c_ref[...]
        u = cg * h
        u1 = _shift_down(u, 1, row)
        u2 = _shift_down(u, 2, row)
        y = w_ref[0:1, :] * u2 + w_ref[1:2, :] * u1 + w_ref[2:3, :] * u
        dbr = dbr_ref[...]
        dy = dbr * b_ref[...]
        du = w_ref[2:3, :] * dy + w_ref[1:2, :] * _shift_up(dy, 1, row) + w_ref[0:1, :] * _shift_up(dy, 2, row)
        d_ref[:, 0:CONV_W] = (du * cg).astype(BF16)
        d_ref[:, CONV_W:2 * CONV_W] = (dbr * y).astype(BF16)
        d_ref[:, 2 * CONV_W:3 * CONV_W] = (du * h).astype(BF16)
        red_ref[...] = jnp.zeros_like(red_ref)
        red_ref[0:1, :] = jnp.sum(dy * u2, axis=0, keepdims=True)
        red_ref[1:2, :] = jnp.sum(dy * u1, axis=0, keepdims=True)
        red_ref[2:3, :] = jnp.sum(dy * u, axis=0, keepdims=True)

    return pl.pallas_call(
        body, name=name,
        out_shape=(jax.ShapeDtypeStruct((S, 3 * CONV_W), BF16), jax.ShapeDtypeStruct((8, CONV_W), F32)),
        grid=(1,),
        in_specs=_conv_specs() + [_full_spec((S, CONV_W))],
        out_specs=(_full_spec((S, 3 * CONV_W)), _full_spec((8, CONV_W))),
        compiler_params=_params(("arbitrary",)),
    )(z, z, z, cw, dbr)


def _pair_lane(h):
    return 8 * (h // 2) + h % 2


def _forget_fwd(z, bf, name):
    def body(f_ref, b_ref, fq_ref, fr_ref):
        row = lax.broadcasted_iota(jnp.int32, (S, LANE), 0)
        lane = lax.broadcasted_iota(jnp.int32, (S, LANE), 1)
        xv = f_ref[...] + b_ref[0:1, :]
        ls = jnp.minimum(xv, 0.0) - jnp.log(1.0 + jnp.exp(-jnp.abs(xv)))
        cum = _cumsum_rows(jnp.where(lane < 8, ls, 0.0), row)
        cols = [cum[:, h:h + 1] for h in range(8)]
        for p in range(N_PAIR):
            fq_ref[:, p * LANE:(p + 1) * LANE] = jnp.where(lane < 64, cols[2 * p], cols[2 * p + 1])
        g = jnp.zeros((S, LANE), F32)
        for h in range(8):
            g = jnp.where(lane == _pair_lane(h), cols[h], g)
        fr_ref[...] = g.T

    return pl.pallas_call(
        body, name=name,
        out_shape=(jax.ShapeDtypeStruct((S, N_PAIR * LANE), F32), jax.ShapeDtypeStruct((LANE, S), F32)),
        grid=(1,),
        in_specs=[_full_spec((S, LANE), (0, Z_F // LANE)), _full_spec((8, LANE))],
        out_specs=(_full_spec((S, N_PAIR * LANE)), _full_spec((LANE, S))),
        compiler_params=_params(("arbitrary",)),
    )(z, bf)


def _forget_bwd(z, bf, dfrow, dfq, name):
    def body(f_ref, b_ref, dfr_ref, dfq_ref, dfl_ref, red_ref):
        row = lax.broadcasted_iota(jnp.int32, (S, LANE), 0)
        lane = lax.broadcasted_iota(jnp.int32, (S, LANE), 1)
        t = jnp.concatenate([dfr_ref[...], jnp.zeros((LANE - 8 * N_PAIR, S), F32)], axis=0).T
        dcum = jnp.zeros((S, LANE), F32)
        for h in range(8):
            at = LANE * (h // 2) + 64 * (h % 2)
            dcum = jnp.where(lane == h, t[:, _pair_lane(h):_pair_lane(h) + 1] + dfq_ref[:, at:at + 1], dcum)
        dls = _cumsum_rows(dcum, row, reverse=True)
        xv = f_ref[...] + b_ref[0:1, :]
        dx = jnp.where(lane < 8, dls * jax.nn.sigmoid(-xv), 0.0)
        dfl_ref[...] = dx.astype(BF16)
        red_ref[...] = jnp.zeros_like(red_ref)
        red_ref[0:1, :] = jnp.sum(dx, axis=0, keepdims=True)

    return pl.pallas_call(
        body, name=name,
        out_shape=(jax.ShapeDtypeStruct((S, LANE), BF16), jax.ShapeDtypeStruct((8, LANE), F32)),
        grid=(1,),
        in_specs=[_full_spec((S, LANE), (0, Z_F // LANE)), _full_spec((8, LANE)), _full_spec((8 * N_PAIR, S)),
                  _full_spec((S, N_PAIR * LANE))],
        out_specs=(_full_spec((S, LANE)), _full_spec((8, LANE))),
        compiler_params=_params(("arbitrary",)),
    )(z, bf, dfrow, dfq)


_NT = (((1,), (1,)), ((), ()))
_TN = (((0,), (0,)), ((), ()))


def _attn_fwd(z, fq, frow, name):
    nq = S // TQ

    def body(q_ref, k_ref, v_ref, fq_ref, fr_ref, o_ref, lse_ref):
        i = pl.program_id(1)
        lane = lax.broadcasted_iota(jnp.int32, (TQ, LANE), 1)
        row = lax.broadcasted_iota(jnp.int32, (TQ, TQ), 0) + i * TQ
        col = lax.broadcasted_iota(jnp.int32, (TQ, TQ), 1)
        q = q_ref[...]
        res = []
        for e in range(2):
            head = (lane >= 64) if e else (lane < 64)
            qm = jnp.where(head, q, 0.0).astype(BF16)
            fqc = fq_ref[:, 64 * e:64 * e + 1]

            def step(j, carry, qm=qm, fqc=fqc, e=e):
                m, l, acc = carry
                off = pl.multiple_of(j * TQ, TQ)
                kb = k_ref[pl.ds(off, TQ), :].astype(BF16)
                vb = v_ref[pl.ds(off, TQ), :].astype(BF16)
                s = lax.dot_general(qm, kb, _NT, preferred_element_type=F32) * ATT_SCALE
                s = s + fqc - fr_ref[e:e + 1, pl.ds(off, TQ)]
                s = jnp.where(col + j * TQ > row, NEG_INF, s)
                mn = jnp.maximum(m, jnp.max(s, axis=1, keepdims=True))
                p = jnp.exp(s - mn)
                alpha = jnp.exp(m - mn)
                l = alpha * l + jnp.sum(p, axis=1, keepdims=True)
                acc = alpha * acc + jnp.dot(p.astype(BF16), vb, preferred_element_type=F32)
                return mn, l, acc

            init = (jnp.full((TQ, 1), NEG_INF, F32), jnp.zeros((TQ, 1), F32), jnp.zeros((TQ, LANE), F32))
            m, l, acc = lax.fori_loop(0, i + 1, step, init)
            res.append((acc / l, m + jnp.log(l)))
        o_ref[...] = jnp.where(lane < 64, res[0][0], res[1][0])
        lse_ref[...] = jnp.where(lane < 64, res[0][1], res[1][1])

    qb, kb_, vb_ = Z_Q // LANE, Z_K // LANE, Z_V // LANE
    out = jax.ShapeDtypeStruct((S, N_PAIR * LANE), F32)
    return pl.pallas_call(
        body, name=name, out_shape=(out, out), grid=(N_PAIR, nq),
        in_specs=[pl.BlockSpec((TQ, LANE), lambda p, i: (i, qb + p)),
                  pl.BlockSpec((S, LANE), lambda p, i: (0, kb_ + p)),
                  pl.BlockSpec((S, LANE), lambda p, i: (0, vb_ + p)),
                  pl.BlockSpec((TQ, LANE), lambda p, i: (i, p)),
                  pl.BlockSpec((8, S), lambda p, i: (p, 0))],
        out_specs=(pl.BlockSpec((TQ, LANE), lambda p, i: (i, p)), pl.BlockSpec((TQ, LANE), lambda p, i: (i, p))),
        compiler_params=_params(("parallel", "parallel")),
    )(z, z, z, fq, frow)


def _attn_bwd(z, fq, frow, o, lse, do, name):
    nq = S // TQ

    def body(q_ref, k_ref, v_ref, fq_ref, fr_ref, o_ref, lse_ref, do_ref, dq_ref, dk_ref, dv_ref, dfr_ref,
             dfq_ref):
        j = pl.program_id(1)

        @pl.when(j == 0)
        def _():
            dq_ref[...] = jnp.zeros_like(dq_ref)
            dfq_ref[...] = jnp.zeros_like(dfq_ref)

        lane = lax.broadcasted_iota(jnp.int32, (TQ, LANE), 1)
        row = lax.broadcasted_iota(jnp.int32, (TQ, TQ), 0)
        col = lax.broadcasted_iota(jnp.int32, (TQ, TQ), 1) + j * TQ
        kb = k_ref[...].astype(BF16)
        vb = v_ref[...].astype(BF16)
        res = []
        for e in range(2):
            head = (lane >= 64) if e else (lane < 64)
            frk = fr_ref[e:e + 1, :]

            def step(i, carry, head=head, frk=frk, e=e):
                dk_acc, dv_acc, df_acc = carry
                off = pl.multiple_of(i * TQ, TQ)
                q = q_ref[pl.ds(off, TQ), :]
                dov = do_ref[pl.ds(off, TQ), :]
                dom = jnp.where(head, dov, 0.0)
                dsum = jnp.sum(dom * o_ref[pl.ds(off, TQ), :], axis=1, keepdims=True)
                qm = jnp.where(head, q, 0.0).astype(BF16)
                s = lax.dot_general(qm, kb, _NT, preferred_element_type=F32) * ATT_SCALE
                s = s + fq_ref[pl.ds(off, TQ), 64 * e:64 * e + 1] - frk
                s = jnp.where(col > row + i * TQ, NEG_INF, s)
                p = jnp.exp(s - lse_ref[pl.ds(off, TQ), 64 * e:64 * e + 1])
                dp = lax.dot_general(dom.astype(BF16), vb, _NT, preferred_element_type=F32)
                ds = p * (dp - dsum)
                dsb = ds.astype(BF16)
                dv_acc = dv_acc + lax.dot_general(p.astype(BF16), dov.astype(BF16), _TN, preferred_element_type=F32)
                dk_acc = dk_acc + lax.dot_general(dsb, q.astype(BF16), _TN, preferred_element_type=F32)
                dqe = jnp.dot(dsb, kb, preferred_element_type=F32) * ATT_SCALE
                dq_ref[pl.ds(off, TQ), :] += jnp.where(head, dqe, 0.0)
                dfq_ref[pl.ds(off, TQ), :] += jnp.where(head, jnp.sum(ds, axis=1, keepdims=True), 0.0)
                df_acc = df_acc - jnp.sum(ds, axis=0, keepdims=True)
                return dk_acc, dv_acc, df_acc

            init = (jnp.zeros((TQ, LANE), F32), jnp.zeros((TQ, LANE), F32), jnp.zeros((1, TQ), F32))
            res.append(lax.fori_loop(j, nq, step, init))
        dk_ref[...] = (jnp.where(lane < 64, res[0][0], res[1][0]) * ATT_SCALE).astype(BF16)
        dv_ref[...] = jnp.where(lane < 64, res[0][1], res[1][1]).astype(BF16)
        sub = lax.broadcasted_iota(jnp.int32, (8, TQ), 0)
        dfr_ref[...] = jnp.where(sub == 0, res[0][2], jnp.where(sub == 1, res[1][2], 0.0))

    qb, kb_, vb_ = Z_Q // LANE, Z_K // LANE, Z_V // LANE
    full = lambda p, j: (0, p)
    width = N_PAIR * LANE
    return pl.pallas_call(
        body, name=name,
        out_shape=(jax.ShapeDtypeStruct((S, width), F32), jax.ShapeDtypeStruct((S, width), BF16),
                   jax.ShapeDtypeStruct((S, width), BF16), jax.ShapeDtypeStruct((8 * N_PAIR, S), F32),
                   jax.ShapeDtypeStruct((S, width), F32)),
        grid=(N_PAIR, nq),
        in_specs=[pl.BlockSpec((S, LANE), lambda p, j: (0, qb + p)),
                  pl.BlockSpec((TQ, LANE), lambda p, j: (j, kb_ + p)),
                  pl.BlockSpec((TQ, LANE), lambda p, j: (j, vb_ + p)),
                  pl.BlockSpec((S, LANE), full),
                  pl.BlockSpec((8, TQ), lambda p, j: (p, j)),
                  pl.BlockSpec((S, LANE), full), pl.BlockSpec((S, LANE), full), pl.BlockSpec((S, LANE), full)],
        out_specs=(pl.BlockSpec((S, LANE), full), pl.BlockSpec((TQ, LANE), lambda p, j: (j, p)),
                   pl.BlockSpec((TQ, LANE), lambda p, j: (j, p)), pl.BlockSpec((8, TQ), lambda p, j: (p, j)),
                   pl.BlockSpec((S, LANE), full)),
        compiler_params=_params(("arbitrary", "arbitrary")),
    )(z, z, z, fq, frow, o, lse, do)


ADA_ROWS = 16


def _ada_fwd(c_pad, w_ada, b_cols, name):
    def body(c_ref, w_ref, b_ref, o_ref):
        cv = c_ref[...]
        sc = (cv * jax.nn.sigmoid(cv)).astype(BF16)
        o_ref[0] = jnp.dot(sc, w_ref[0].astype(BF16), preferred_element_type=F32) + b_ref[0, 0:1, :]

    return pl.pallas_call(
        body, name=name, out_shape=jax.ShapeDtypeStruct((DEPTH, ADA_ROWS, ADA_COLS), F32), grid=(DEPTH,),
        in_specs=[pl.BlockSpec((ADA_ROWS, D), lambda l: (0, 0)), pl.BlockSpec((1, D, ADA_COLS), lambda l: (l, 0, 0)),
                  pl.BlockSpec((1, 8, ADA_COLS), lambda l: (l, 0, 0))],
        out_specs=pl.BlockSpec((1, ADA_ROWS, ADA_COLS), lambda l: (l, 0, 0)),
        compiler_params=_params(("parallel",)),
    )(c_pad, w_ada, b_cols)


def _ada_bwd(c_pad, dmod_cols, name):
    def body(c_ref, d_ref, o_ref):
        cv = c_ref[...]
        sc = (cv * jax.nn.sigmoid(cv)).astype(BF16)
        o_ref[0] = lax.dot_general(sc, d_ref[0].astype(BF16), _TN, preferred_element_type=F32)

    return pl.pallas_call(
        body, name=name, out_shape=jax.ShapeDtypeStruct((DEPTH, D, ADA_COLS), F32), grid=(DEPTH,),
        in_specs=[pl.BlockSpec((ADA_ROWS, D), lambda l: (0, 0)), pl.BlockSpec((1, ADA_ROWS, ADA_COLS), lambda l: (l, 0, 0))],
        out_specs=pl.BlockSpec((1, D, ADA_COLS), lambda l: (l, 0, 0)),
        compiler_params=_params(("parallel",)),
    )(c_pad, dmod_cols)


def _adamw_math(w, g, m, v):
    m = B1 * m + (1.0 - B1) * g
    v = B2 * v + (1.0 - B2) * (g * g)
    m_hat = m / (1.0 - B1 ** STEP)
    v_hat = v / (1.0 - B2 ** STEP)
    delta = -LR * (m_hat / (jnp.sqrt(v_hat) + EPS) + WD * w)
    return delta, m, v


def _row_tile(rows, target=256):
    best = 8
    for t in range(8, min(rows, target) + 1, 8):
        if rows % t == 0:
            best = t
    return best


def _adamw(w, g, m, v, name):
    layers, rows, cols = w.shape
    tr = _row_tile(rows)
    spec = pl.BlockSpec((1, tr, cols), lambda l, i: (l, i, 0))

    def body(w_ref, g_ref, m_ref, v_ref, d_ref, nm_ref, nv_ref):
        d_ref[...], nm_ref[...], nv_ref[...] = _adamw_math(w_ref[...], g_ref[...], m_ref[...], v_ref[...])

    out = jax.ShapeDtypeStruct(w.shape, F32)
    return pl.pallas_call(
        body, name=name, out_shape=(out, out, out), grid=(layers, rows // tr),
        in_specs=[spec] * 4, out_specs=(spec,) * 3, compiler_params=_params(("parallel", "parallel")),
    )(w, g, m, v)


def _sum_slabs(x, name):
    n, rows, _ = x.shape
    tr = _row_tile(rows)

    def body(x_ref, o_ref):
        acc = x_ref[0]
        for k in range(1, n):
            acc = acc + x_ref[k]
        o_ref[...] = acc

    return pl.pallas_call(
        body, name=name, out_shape=jax.ShapeDtypeStruct((rows, D), F32), grid=(rows // tr,),
        in_specs=[pl.BlockSpec((n, tr, D), lambda i: (0, i, 0))], out_specs=pl.BlockSpec((tr, D), lambda i: (i, 0)),
        compiler_params=_params(("parallel",)),
    )(x)


_ANY = pl.BlockSpec(memory_space=pl.ANY)
MESH = pl.DeviceIdType.MESH


def _all_gather(xs, name):
    n = len(xs)

    def body(*refs):
        x_refs, out_refs = refs[:n], refs[n:2 * n]
        send_sems, recv_sems, local_sems = refs[2 * n:]
        x_, y_, c_ = lax.axis_index("x"), lax.axis_index("y"), lax.axis_index("c")
        me, sibling = (x_, y_, c_), (x_, y_, 1 - c_)
        chips = [(1 - x_, y_), (x_, 1 - y_), (1 - x_, 1 - y_)]

        def slot(a, px, py, pc):
            return out_refs[a].at[4 * px + 2 * py + pc]

        def copy(a, k, block, to, src=None):
            return pltpu.make_async_remote_copy(
                src_ref=slot(a, *block) if src is None else src, dst_ref=slot(a, *block),
                send_sem=send_sems.at[7 * a + k], recv_sem=recv_sems.at[7 * a + k], device_id=to, device_id_type=MESH)

        mine = [pltpu.make_async_copy(x_refs[a], slot(a, *me), local_sems.at[a]) for a in range(n)]
        for cp in mine:
            cp.start()
        first = []
        for a in range(n):
            first.append(copy(a, 0, me, sibling, src=x_refs[a]))
            first += [copy(a, 1 + j, me, (*chip, c_), src=x_refs[a]) for j, chip in enumerate(chips)]
        for cp in first:
            cp.start()
        passed = []
        for j, chip in enumerate(chips):
            for a in range(n):
                copy(a, 1 + j, (*chip, c_), me).wait_recv()
                passed.append(copy(a, 4 + j, (*chip, c_), sibling))
                passed[-1].start()
        for a in range(n):
            copy(a, 0, sibling, me).wait_recv()
        for j, chip in enumerate(chips):
            for a in range(n):
                copy(a, 4 + j, (*chip, 1 - c_), me).wait_recv()
        for cp in first + passed:
            cp.wait_send()
        for cp in mine:
            cp.wait()

    return pl.pallas_call(
        body, name=name, out_shape=[jax.ShapeDtypeStruct((N_DEV,) + x.shape, x.dtype) for x in xs],
        in_specs=[_ANY] * n, out_specs=[_ANY] * n,
        scratch_shapes=[pltpu.SemaphoreType.DMA((7 * n,)), pltpu.SemaphoreType.DMA((7 * n,)),
                        pltpu.SemaphoreType.DMA((n,))],
    )(*xs)


def _sibling_exchange(gs, name):
    n = len(gs)

    def body(*refs):
        g_refs, p_refs = refs[:n], refs[n:2 * n]
        send_sems, recv_sems = refs[2 * n:]
        x_, y_, c_ = lax.axis_index("x"), lax.axis_index("y"), lax.axis_index("c")
        copies = [pltpu.make_async_remote_copy(
            src_ref=g_refs[a].at[2 * k + (1 - c_)], dst_ref=p_refs[a].at[k], send_sem=send_sems.at[4 * a + k],
            recv_sem=recv_sems.at[4 * a + k], device_id=(x_, y_, 1 - c_), device_id_type=MESH)
            for a in range(n) for k in range(4)]
        for cp in copies:
            cp.start()
        for cp in copies:
            cp.wait()

    return pl.pallas_call(
        body, name=name, out_shape=[jax.ShapeDtypeStruct((4,) + g.shape[1:], g.dtype) for g in gs],
        in_specs=[_ANY] * n, out_specs=[_ANY] * n,
        scratch_shapes=[pltpu.SemaphoreType.DMA((4 * n,)), pltpu.SemaphoreType.DMA((4 * n,))],
    )(*gs)


def _pair_add(g, p, core, name):
    _, rows, cols = g.shape
    tr = _row_tile(rows)

    def body(core_ref, g_ref, p_ref, t_ref):
        t_ref[...] = g_ref[...] + p_ref[...]

    return pl.pallas_call(
        body, name=name, out_shape=jax.ShapeDtypeStruct((4, rows, cols), F32),
        grid_spec=pltpu.PrefetchScalarGridSpec(
            num_scalar_prefetch=1, grid=(4, rows // tr),
            in_specs=[pl.BlockSpec((1, tr, cols), lambda k, i, core_ref: (2 * k + core_ref[0], i, 0)),
                      pl.BlockSpec((1, tr, cols), lambda k, i, core_ref: (k, i, 0))],
            out_specs=pl.BlockSpec((1, tr, cols), lambda k, i, core_ref: (k, i, 0))),
        compiler_params=_params(("parallel", "parallel")),
    )(core, g, p)


def _chip_exchange(ts, name):
    n = len(ts)

    def body(*refs):
        t_refs, l_refs = refs[:n], refs[n:2 * n]
        send_sems, recv_sems = refs[2 * n:]
        x_, y_, c_ = lax.axis_index("x"), lax.axis_index("y"), lax.axis_index("c")
        chips = [(1 - x_, y_), (x_, 1 - y_), (1 - x_, 1 - y_)]
        copies = [pltpu.make_async_remote_copy(
            src_ref=t_refs[a].at[2 * px + py], dst_ref=l_refs[a].at[r], send_sem=send_sems.at[3 * a + r],
            recv_sem=recv_sems.at[3 * a + r], device_id=(px, py, c_), device_id_type=MESH)
            for a in range(n) for r, (px, py) in enumerate(chips)]
        for cp in copies:
            cp.start()
        for cp in copies:
            cp.wait()

    return pl.pallas_call(
        body, name=name, out_shape=[jax.ShapeDtypeStruct((3,) + t.shape[1:], t.dtype) for t in ts],
        in_specs=[_ANY] * n, out_specs=[_ANY] * n,
        scratch_shapes=[pltpu.SemaphoreType.DMA((3 * n,)), pltpu.SemaphoreType.DMA((3 * n,))],
    )(*ts)


def _reduce_adamw(sums, landed, chip, w, m, v, name):
    layers, rows, cols = w.shape
    assert layers == DEPTH == 2
    tr = _row_tile(rows)
    nr = rows // tr
    spec = pl.BlockSpec((1, tr, cols), lambda l, i, chip_ref: (l, i, 0))

    def own(layer, blocks):
        park = nr - 1 if layer == 0 else 0
        return pl.BlockSpec((blocks, tr, cols), lambda l, i, chip_ref: (
            chip_ref[0] if blocks == 1 else 0, jnp.where(l == layer, i, park), 0))

    def body(chip_ref, t0_ref, l0_ref, t1_ref, l1_ref, w_ref, m_ref, v_ref, g_ref, d_ref, nm_ref, nv_ref):
        def update(t_ref, l_ref):
            g = t_ref[0] + l_ref[0] + l_ref[1] + l_ref[2]
            g_ref[0] = g
            d_ref[0], nm_ref[0], nv_ref[0] = _adamw_math(w_ref[0], g, m_ref[0], v_ref[0])

        @pl.when(pl.program_id(0) == 0)
        def _():
            update(t0_ref, l0_ref)

        @pl.when(pl.program_id(0) == 1)
        def _():
            update(t1_ref, l1_ref)

    out = jax.ShapeDtypeStruct(w.shape, F32)
    return pl.pallas_call(
        body, name=name, out_shape=(out, out, out, out),
        grid_spec=pltpu.PrefetchScalarGridSpec(
            num_scalar_prefetch=1, grid=(DEPTH, nr),
            in_specs=[own(0, 1), own(0, 3), own(1, 1), own(1, 3), spec, spec, spec],
            out_specs=(spec, spec, spec, spec)),
        compiler_params=_params(("arbitrary", "arbitrary")),
    )(chip, sums[0], landed[0], sums[1], landed[1], w, m, v)


def _pack(pieces, row_multiple, dtype, cols=D, rows=None):
    flat = jnp.concatenate([p.astype(dtype).reshape(-1) for p in pieces])
    if rows is None:
        rows = -(-flat.shape[0] // cols)
        rows = -(-rows // row_multiple) * row_multiple
    flat = jnp.pad(flat, (0, rows * cols - flat.shape[0]))
    return flat.reshape(rows, cols)


def _unpack(flat, shapes, lead=()):
    out, off = [], 0
    for shp in shapes:
        n = 1
        for s_ in shp:
            n *= s_
        out.append(lax.slice_in_dim(flat, off, off + n, axis=len(lead)).reshape(lead + tuple(shp)))
        off += n
    return out


def _z_from_in(w):
    pad = jnp.zeros(w.shape[:-1] + (NZ - IN_COLS,), w.dtype)
    return jnp.concatenate([w[..., 1544:2568], w[..., 2568:5640], w[..., 0:1536], w[..., 1536:1544], pad], axis=-1)


def _in_from_z(w):
    return jnp.concatenate([w[..., Z_Q:Z_Q + 1536], w[..., Z_F:Z_F + 8], w[..., Z_PC:Z_PC + 1024], w[..., Z_G:Z_G + 3072]],
                           axis=-1)


def _pad_rows(v, rows=8):
    return jnp.pad(v, ((0, rows - v.shape[0]), (0, 0)))


def _layer_fwd(l, x, wts, gvec, mod):
    tag = f"l{l}"
    h = _prenorm_fwd(x, gvec, mod, 0, 0, 1, f"prenorm_mix_{tag}")
    z = _matmul(h, wts["w_in"], "nn", f"in_proj_{tag}", tn=1152)
    fq, frow = _forget_fwd(z, wts["b_f"], f"forget_{tag}")
    o, lse = _attn_fwd(z, fq, frow, f"attn_{tag}")
    br_b = _pool_fwd(z, wts["wp_bd"], wts["pool_scale"], f"pool_{tag}")
    br_c = _conv_fwd(z, wts["conv_w"], f"conv_{tag}")
    pa = _matmul(o, wts["wa"], "nn", f"proj_a_{tag}")
    pb = _matmul(br_b, wts["wb"], "nn", f"proj_b_{tag}")
    pc = _matmul(br_c, wts["wc"], "nn", f"proj_c_{tag}")
    merged = _merge_fwd(z, pa, pb, pc, f"merge_{tag}")
    y = _matmul(merged, wts["w_out"], "nn", f"out_proj_{tag}")
    x1 = _postnorm_fwd(x, y, gvec, mod, 1, 2, f"postnorm_mix_{tag}")
    h2 = _prenorm_fwd(x1, gvec, mod, 2, 3, 4, f"prenorm_ff_{tag}")
    a = _matmul(h2, wts["w_ff1"], "nn", f"ff1_{tag}", b_col_shards=True)
    r = _relu2_fwd(a, f"relu2_{tag}")
    y2 = _matmul(r, wts["w_ff2"], "nn", f"ff2_{tag}")
    x2 = _postnorm_fwd(x1, y2, gvec, mod, 3, 5, f"postnorm_ff_{tag}")
    saved = dict(x=x, h=h, z=z, fq=fq, frow=frow, o=o, lse=lse, br_b=br_b, br_c=br_c, pa=pa, pb=pb, pc=pc,
                 merged=merged, y=y, x1=x1, h2=h2, a=a, r=r, y2=y2)
    return x2, saved


def _layer_bwd(l, dx2, sv, wts, gvec, mod):
    tag = f"l{l}"
    dy2, red_post_ff = _postnorm_bwd(sv["y2"], gvec, mod, dx2, 3, 5, f"postnorm_ff_bwd_{tag}")
    dr = _matmul(dy2, wts["w_ff2"], "nt", f"ff2_dx_{tag}")
    d_w_ff2 = _matmul(sv["r"], dy2, "tn", f"ff2_dw_{tag}")
    da = _relu2_bwd(sv["a"], dr, f"relu2_bwd_{tag}")
    dh2 = _matmul(da, wts["w_ff1"], "nt", f"ff1_dx_{tag}", b_col_shards=True)
    d_w_ff1 = _matmul(sv["h2"], da, "tn", f"ff1_dw_{tag}", out_col_shards=True)
    dx1, red_pre_ff = _prenorm_bwd(sv["x1"], gvec, mod, dh2, dx2, 2, 4, f"prenorm_ff_bwd_{tag}")

    dy, red_post_mix = _postnorm_bwd(sv["y"], gvec, mod, dx1, 1, 2, f"postnorm_mix_bwd_{tag}")
    dmerged = _matmul(dy, wts["w_out"], "nt", f"out_proj_dx_{tag}")
    d_w_out = _matmul(sv["merged"], dy, "tn", f"out_proj_dw_{tag}")
    dpa, dpb, dpc, dgl = _merge_bwd(sv["z"], sv["pa"], sv["pb"], sv["pc"], dmerged, f"merge_bwd_{tag}")
    do = _matmul(dpa, wts["wa"], "nt", f"proj_a_dx_{tag}")
    dbr_b = _matmul(dpb, wts["wb"], "nt", f"proj_b_dx_{tag}")
    dbr_c = _matmul(dpc, wts["wc"], "nt", f"proj_c_dx_{tag}")
    d_wa = _matmul(sv["o"], dpa, "tn", f"proj_a_dw_{tag}")
    d_wb = _matmul(sv["br_b"], dpb, "tn", f"proj_b_dw_{tag}")
    d_wc = _matmul(sv["br_c"], dpc, "tn", f"proj_c_dw_{tag}")
    d_w_branch = jnp.concatenate([d_wa, d_wb, d_wc], axis=0)

    dpu, d_wp_bd, red_pool = _pool_bwd(sv["z"], wts["wp_bd"], wts["pool_scale"], dbr_b, f"pool_bwd_{tag}")
    dconv, red_conv = _conv_bwd(sv["z"], wts["conv_w"], dbr_c, f"conv_bwd_{tag}")
    dq, dk, dv, dfrow, dfq = _attn_bwd(sv["z"], sv["fq"], sv["frow"], sv["o"], sv["lse"], do, f"attn_bwd_{tag}")
    dfl, red_f = _forget_bwd(sv["z"], wts["b_f"], dfrow, dfq, f"forget_bwd_{tag}")
    dz = jnp.concatenate([dpu, dconv, dgl, dq.astype(BF16), dk, dv, dfl], axis=1)
    dh = _matmul(dz, wts["w_in"], "nt", f"in_proj_dx_{tag}", tk=1152)
    d_w_in = _matmul(sv["h"], dz, "tn", f"in_proj_dw_{tag}", tn=1152)
    dx0, red_pre_mix = _prenorm_bwd(sv["x"], gvec, mod, dh, dx1, 0, 1, f"prenorm_mix_bwd_{tag}")

    rows = D // N_DEV
    big = [jnp.transpose(_in_from_z(d_w_in).reshape(D, N_DEV, IN_SHARD), (1, 0, 2)), d_w_branch.reshape(N_DEV, rows, D),
           d_w_out.reshape(N_DEV, rows, D), d_w_ff1, d_w_ff2.reshape(N_DEV, D_FF // N_DEV, D)]
    d_w_pool = jnp.stack([d_wp_bd[64 * g:64 * (g + 1), 64 * g:64 * (g + 1)] for g in range(4)])
    small = dict(
        mod=jnp.stack([red_pre_mix[0], red_pre_mix[1], red_post_mix[0], red_pre_ff[0], red_pre_ff[1], red_post_ff[0]]),
        g_mix_pre=red_pre_mix[2], g_mix_post=red_post_mix[1], g_ff_pre=red_pre_ff[2], g_ff_post=red_post_ff[1],
        b_f=red_f[0, 0:8], w_pool=d_w_pool, pool_scale=red_pool[0], conv_w=red_conv[0:3])
    return dx0, big, small


SMALL_KEYS = ["mod", "g_mix_pre", "g_mix_post", "g_ff_pre", "g_ff_post", "b_f", "w_pool", "pool_scale", "conv_w"]
SMALL_SHAPES = [(DEPTH, 6 * D), (DEPTH, D), (DEPTH, D), (DEPTH, D), (DEPTH, D), (DEPTH, 8), (DEPTH, 4, 64, 64),
                (DEPTH, POOL_W), (DEPTH, 3, CONV_W)]


def kernel(x, c, w_ada, b_ada, g_mix_pre, g_mix_post, g_ff_pre, g_ff_post, w_in, b_f, w_pool, pool_scale, conv_w, w_branch, w_out, w_ff1, w_ff2, loss_target, m_w_ada, m_b_ada, m_g_mix_pre, m_g_mix_post, m_g_ff_pre, m_g_ff_post, m_w_in, m_b_f, m_w_pool, m_pool_scale, m_conv_w, m_w_branch, m_w_out, m_w_ff1, m_w_ff2, v_w_ada, v_b_ada, v_g_mix_pre, v_g_mix_post, v_g_ff_pre, v_g_ff_post, v_w_in, v_b_f, v_w_pool, v_pool_scale, v_conv_w, v_w_branch, v_w_out, v_w_ff1, v_w_ff2):
    ix, iy, ic = lax.axis_index("x"), lax.axis_index("y"), lax.axis_index("c")
    me = 4 * ix + 2 * iy + ic
    core = jnp.reshape(ic, (1,)).astype(jnp.int32)
    chip = jnp.reshape(2 * ix + iy, (1,)).astype(jnp.int32)

    c_all = _all_gather([_pad_rows(c)], "gather_c")[0][:, 0, :]
    c_pad = _pad_rows(c_all, ADA_ROWS)
    b_cols = lax.dynamic_slice_in_dim(b_ada, me * ADA_COLS, ADA_COLS, axis=1)
    b_cols = jnp.broadcast_to(b_cols[:, None, :], (DEPTH, 8, ADA_COLS))
    mod_part = _ada_fwd(c_pad, w_ada, b_cols, "ada_fwd")
    mod_all = _all_gather([mod_part.reshape(DEPTH * ADA_ROWS, ADA_COLS)], "gather_mod")[0]
    mod_all = mod_all.reshape(N_DEV, DEPTH, ADA_ROWS, ADA_COLS)
    mod_mine = lax.dynamic_index_in_dim(mod_all, me, axis=2, keepdims=False)
    mod_mine = jnp.transpose(mod_mine, (1, 0, 2)).reshape(DEPTH, 6, D)

    send = [w[l].astype(BF16) for l in range(DEPTH) for w in (w_in, w_branch, w_out, w_ff1, w_ff2)]
    cw_cols = CONV_W // N_DEV
    cw_send = jnp.pad(conv_w.reshape(DEPTH * 3, cw_cols), ((0, 8 - DEPTH * 3), (0, LANE - cw_cols)))
    gathered = _all_gather(send + [cw_send], "gather_weights")
    cw_all = gathered[-1][:, :DEPTH * 3, :cw_cols].reshape(N_DEV, DEPTH, 3, cw_cols)

    layers = []
    for l in range(DEPTH):
        p_in, p_br, p_out, p_ff1, p_ff2 = gathered[5 * l:5 * l + 5]
        w_in_full = jnp.transpose(p_in, (1, 0, 2)).reshape(D, IN_COLS)
        w_br_full = p_br.reshape(D, D)
        cw_full = jnp.transpose(cw_all[:, l], (1, 0, 2)).reshape(3, CONV_W)
        wp_bd = jnp.zeros((POOL_W, POOL_W), F32)
        for g in range(4):
            wp_bd = wp_bd.at[64 * g:64 * (g + 1), 64 * g:64 * (g + 1)].set(w_pool[l, g])
        wts = dict(
            w_in=_z_from_in(w_in_full), wa=w_br_full[0:A_WIDTH], wb=w_br_full[A_WIDTH:A_WIDTH + POOL_W],
            wc=w_br_full[A_WIDTH + POOL_W:], w_out=p_out.reshape(D, D),
            w_ff1=p_ff1, w_ff2=p_ff2.reshape(D_FF, D),
            conv_w=_pad_rows(cw_full), wp_bd=wp_bd.astype(BF16), pool_scale=_pad_rows(pool_scale[l][None, :]),
            b_f=_pad_rows(jnp.pad(b_f[l], (0, LANE - 8))[None, :]))
        gvec = _pad_rows(jnp.stack([g_mix_pre[l], g_mix_post[l], g_ff_pre[l], g_ff_post[l]]))
        layers.append((wts, gvec, _pad_rows(mod_mine[l])))

    xs = x[0]
    saved = []
    for l in range(DEPTH):
        xs, sv = _layer_fwd(l, xs, *layers[l])
        saved.append(sv)
    dx, loss_part = _loss_head(xs, loss_target[0], "loss_head")
    loss = lax.psum(loss_part[0, 0], ("x", "y", "c"))
    small_grads, sums, landed = [None] * DEPTH, [None] * DEPTH, [None] * DEPTH
    for l in reversed(range(DEPTH)):
        dx, big, small_grads[l] = _layer_bwd(l, dx, saved[l], *layers[l])
        from_sibling = _sibling_exchange(big, f"rs_sibling_l{l}")
        sums[l] = [_pair_add(g, p, core, f"rs_pair_add_{k}_l{l}") for k, (g, p) in enumerate(zip(big, from_sibling))]
        landed[l] = _chip_exchange(sums[l], f"rs_chips_l{l}")
    grad_x = dx[None]

    big_w = [w_in, w_branch, w_out, w_ff1, w_ff2]
    big_m = [m_w_in, m_w_branch, m_w_out, m_w_ff1, m_w_ff2]
    big_v = [v_w_in, v_w_branch, v_w_out, v_w_ff1, v_w_ff2]
    big_out = [[], [], [], []]
    for k in range(5):
        res = _reduce_adamw([sums[l][k] for l in range(DEPTH)], [landed[l][k] for l in range(DEPTH)], chip,
                            big_w[k], big_m[k], big_v[k], f"rs_sum_adamw_{k}")
        for which in range(4):
            big_out[which].append(res[which])

    small = {k: jnp.stack([small_grads[l][k] for l in range(DEPTH)]) for k in SMALL_KEYS}
    small_all = _all_gather([_pack([small[k] for k in SMALL_KEYS], 8, F32)], "gather_small")[0]
    dmod_all = small_all[:, 0:DEPTH * 6, :].reshape(N_DEV, DEPTH, 6 * D)
    summed = _unpack(_sum_slabs(small_all, "sum_small").reshape(-1), SMALL_SHAPES)
    sg = dict(zip(SMALL_KEYS, summed))
    dmod_cols = lax.dynamic_slice_in_dim(dmod_all, me * ADA_COLS, ADA_COLS, axis=2)
    dmod_cols = jnp.pad(jnp.transpose(dmod_cols, (1, 0, 2)), ((0, 0), (0, ADA_ROWS - N_DEV), (0, 0)))
    g_w_ada = _ada_bwd(c_pad, dmod_cols, "ada_bwd")
    g_conv_w = lax.dynamic_slice_in_dim(sg["conv_w"], me * (CONV_W // N_DEV), CONV_W // N_DEV, axis=2)

    ada_out = [g_w_ada] + list(_adamw(w_ada, g_w_ada, m_w_ada, v_w_ada, "adamw_ada"))
    rest_w = [b_ada, g_mix_pre, g_mix_post, g_ff_pre, g_ff_post, b_f, w_pool, pool_scale, conv_w]
    rest_m = [m_b_ada, m_g_mix_pre, m_g_mix_post, m_g_ff_pre, m_g_ff_post, m_b_f, m_w_pool, m_pool_scale, m_conv_w]
    rest_v = [v_b_ada, v_g_mix_pre, v_g_mix_post, v_g_ff_pre, v_g_ff_post, v_b_f, v_w_pool, v_pool_scale, v_conv_w]
    rest_g = [sg["mod"], sg["g_mix_pre"], sg["g_mix_post"], sg["g_ff_pre"], sg["g_ff_post"], sg["b_f"],
              sg["w_pool"], sg["pool_scale"], g_conv_w]
    rest_shapes = [a.shape for a in rest_w]
    upd = _adamw(_pack(rest_w, 8, F32)[None], _pack(rest_g, 8, F32)[None], _pack(rest_m, 8, F32)[None],
                 _pack(rest_v, 8, F32)[None], "adamw_rest")
    rest_out = [rest_g] + [_unpack(arr.reshape(-1), rest_shapes) for arr in upd]
    rest_out = [[ada_out[which]] + rest_out[which] for which in range(4)]

    def ordered(k):
        r, b = rest_out[k], big_out[k]
        return [r[0], r[1], r[2], r[3], r[4], r[5], b[0], r[6], r[7], r[8], r[9], b[1], b[2], b[3], b[4]]

    return (loss, grad_x, *ordered(0), *ordered(1), *ordered(2), *ordered(3))
```

```python
import functools

import jax
import jax.numpy as jnp
from jax import lax
from jax.experimental import pallas as pl
from jax.experimental.pallas import tpu as pltpu

F32 = jnp.float32
BF16 = jnp.bfloat16

N_DEV = 8
D = 1024
S = 2048
DEPTH = 2
D_FF = 4 * D
A_WIDTH = 512
HEAD_DIM = 64
N_PAIR = 4
POOL_W = 256
CONV_W = 256
IN_COLS = 5640
ADA_COLS = 6 * D // N_DEV
IN_SHARD = IN_COLS // N_DEV
RMS_EPS = 1e-6
NEG_INF = -1e30
ATT_SCALE = HEAD_DIM ** -0.5

NZ = 5760
Z_PC = 0
Z_G = 1024
Z_Q = 4096
Z_K = 4608
Z_V = 5120
Z_F = 5632

LR, B1, B2, EPS, WD, STEP = 0.001, 0.9, 0.999, 1e-08, 0.01, 10

LANE = 128
VMEM_LIMIT_BYTES = 48 * 1024 * 1024
TS = 256
TQ = 256


def _params(sem=None):
    return pltpu.CompilerParams(dimension_semantics=sem, vmem_limit_bytes=VMEM_LIMIT_BYTES)


def _pick(n, target):
    best = None
    for t in range(LANE, min(n, target) + 1, LANE):
        if n % t == 0:
            best = t
    return n if best is None else best


def _matmul(a, b, mode, name, out_dtype=F32, tm=1024, tn=1024, tk=1024, b_col_shards=False, out_col_shards=False):
    if b_col_shards:
        shards, b_rows, shard_cols = b.shape
        b_shape = (b_rows, shards * shard_cols)
    else:
        b_shape = b.shape
    if mode == "nn":
        (m, k), (k2, n) = a.shape, b_shape
    elif mode == "nt":
        (m, k), (n, k2) = a.shape, b_shape
    else:
        (k, m), (k2, n) = a.shape, b_shape
    assert k == k2, (a.shape, b.shape, mode)
    tm, tn, tk = _pick(m, tm), _pick(n, tn), _pick(k, tk)
    if b_col_shards and mode == "nn":
        tn = shard_cols
    if b_col_shards and mode == "nt":
        tk = shard_cols
    if out_col_shards:
        tn = n // N_DEV
    nk = k // tk
    if mode == "nn":
        a_spec = pl.BlockSpec((tm, tk), lambda i, j, kk: (i, kk))
        b_spec = (pl.BlockSpec((None, tk, tn), lambda i, j, kk: (j, kk, 0)) if b_col_shards else
                  pl.BlockSpec((tk, tn), lambda i, j, kk: (kk, j)))
        dims = (((1,), (0,)), ((), ()))
    elif mode == "nt":
        a_spec = pl.BlockSpec((tm, tk), lambda i, j, kk: (i, kk))
        b_spec = (pl.BlockSpec((None, tn, tk), lambda i, j, kk: (kk, j, 0)) if b_col_shards else
                  pl.BlockSpec((tn, tk), lambda i, j, kk: (j, kk)))
        dims = (((1,), (1,)), ((), ()))
    else:
        assert not b_col_shards
        a_spec = pl.BlockSpec((tk, tm), lambda i, j, kk: (kk, i))
        b_spec = pl.BlockSpec((tk, tn), lambda i, j, kk: (kk, j))
        dims = (((0,), (0,)), ((), ()))
    if out_col_shards:
        out_shape = jax.ShapeDtypeStruct((N_DEV, m, tn), out_dtype)
        out_spec = pl.BlockSpec((None, tm, tn), lambda i, j, kk: (j, i, 0))
    else:
        out_shape = jax.ShapeDtypeStruct((m, n), out_dtype)
        out_spec = pl.BlockSpec((tm, tn), lambda i, j, kk: (i, j))

    def product(a_ref, b_ref):
        return lax.dot_general(a_ref[...].astype(BF16), b_ref[...].astype(BF16), dims, preferred_element_type=F32)

    def body_one_pass(a_ref, b_ref, o_ref):
        o_ref[...] = product(a_ref, b_ref).astype(out_dtype)

    def body(a_ref, b_ref, o_ref, acc_ref):
        kk = pl.program_id(2)

        @pl.when(kk == 0)
        def _():
            acc_ref[...] = product(a_ref, b_ref)

        @pl.when(kk > 0)
        def _():
            acc_ref[...] += product(a_ref, b_ref)

        @pl.when(kk == nk - 1)
        def _():
            o_ref[...] = acc_ref[...].astype(out_dtype)

    return pl.pallas_call(
        body_one_pass if nk == 1 else body, name=name,
        out_shape=out_shape,
        grid=(m // tm, n // tn, nk),
        in_specs=[a_spec, b_spec],
        out_specs=out_spec,
        scratch_shapes=[] if nk == 1 else [pltpu.VMEM((tm, tn), F32)],
        compiler_params=_params(("parallel", "parallel", "arbitrary")),
    )(a, b)


def _row_spec(width=D, col=0):
    return pl.BlockSpec((TS, width), lambda i: (i, col))


def _vec_spec(rows=8, width=D):
    return pl.BlockSpec((rows, width), lambda i: (0, 0))


def _rms(x):
    return lax.rsqrt(jnp.mean(x * x, axis=-1, keepdims=True) + RMS_EPS)


def _prenorm_fwd(x, gvec, mod, g_row, shift_row, scale_row, name):
    def body(x_ref, g_ref, mod_ref, h_ref):
        xv = x_ref[...]
        y = xv * _rms(xv) * g_ref[g_row:g_row + 1, :]
        h = y * (1.0 + mod_ref[scale_row:scale_row + 1, :]) + mod_ref[shift_row:shift_row + 1, :]
        h_ref[...] = h.astype(BF16)

    return pl.pallas_call(
        body, name=name, out_shape=jax.ShapeDtypeStruct((S, D), BF16), grid=(S // TS,),
        in_specs=[_row_spec(), _vec_spec(), _vec_spec()], out_specs=_row_spec(),
        compiler_params=_params(("parallel",)),
    )(x, gvec, mod)


def _prenorm_bwd(x, gvec, mod, dh, dres, g_row, scale_row, name):
    def body(x_ref, g_ref, mod_ref, dh_ref, dres_ref, dx_ref, red_ref):
        i = pl.program_id(0)

        @pl.when(i == 0)
        def _():
            red_ref[...] = jnp.zeros_like(red_ref)

        xv = x_ref[...]
        g = g_ref[g_row:g_row + 1, :]
        r = _rms(xv)
        n = xv * r
        yg = n * g
        dhv = dh_ref[...]
        dyg = dhv * (1.0 + mod_ref[scale_row:scale_row + 1, :])
        dn = dyg * g
        dx = r * (dn - n * jnp.mean(dn * n, axis=-1, keepdims=True))
        dx_ref[...] = dres_ref[...] + dx
        red_ref[0:1, :] += jnp.sum(dhv, axis=0, keepdims=True)
        red_ref[1:2, :] += jnp.sum(dhv * yg, axis=0, keepdims=True)
        red_ref[2:3, :] += jnp.sum(dyg * n, axis=0, keepdims=True)

    return pl.pallas_call(
        body, name=name,
        out_shape=(jax.ShapeDtypeStruct((S, D), F32), jax.ShapeDtypeStruct((8, D), F32)),
        grid=(S // TS,),
        in_specs=[_row_spec(), _vec_spec(), _vec_spec(), _row_spec(), _row_spec()],
        out_specs=(_row_spec(), _vec_spec()),
        compiler_params=_params(("arbitrary",)),
    )(x, gvec, mod, dh, dres)


def _postnorm_fwd(x, y, gvec, mod, g_row, gate_row, name):
    def body(x_ref, y_ref, g_ref, mod_ref, o_ref):
        yv = y_ref[...]
        yn = yv * _rms(yv) * g_ref[g_row:g_row + 1, :]
        o_ref[...] = x_ref[...] + mod_ref[gate_row:gate_row + 1, :] * yn

    return pl.pallas_call(
        body, name=name, out_shape=jax.ShapeDtypeStruct((S, D), F32), grid=(S // TS,),
        in_specs=[_row_spec(), _row_spec(), _vec_spec(), _vec_spec()], out_specs=_row_spec(),
        compiler_params=_params(("parallel",)),
    )(x, y, gvec, mod)


def _postnorm_bwd(y, gvec, mod, dxo, g_row, gate_row, name):
    def body(y_ref, g_ref, mod_ref, dxo_ref, dy_ref, red_ref):
        i = pl.program_id(0)

        @pl.when(i == 0)
        def _():
            red_ref[...] = jnp.zeros_like(red_ref)

        yv = y_ref[...]
        g = g_ref[g_row:g_row + 1, :]
        r = _rms(yv)
        n = yv * r
        dxo = dxo_ref[...]
        dyn = dxo * mod_ref[gate_row:gate_row + 1, :]
        dn = dyn * g
        dy = r * (dn - n * jnp.mean(dn * n, axis=-1, keepdims=True))
        dy_ref[...] = dy.astype(BF16)
        red_ref[0:1, :] += jnp.sum(dxo * (n * g), axis=0, keepdims=True)
        red_ref[1:2, :] += jnp.sum(dyn * n, axis=0, keepdims=True)

    return pl.pallas_call(
        body, name=name,
        out_shape=(jax.ShapeDtypeStruct((S, D), BF16), jax.ShapeDtypeStruct((8, D), F32)),
        grid=(S // TS,),
        in_specs=[_row_spec(), _vec_spec(), _vec_spec(), _row_spec()],
        out_specs=(_row_spec(), _vec_spec()),
        compiler_params=_params(("arbitrary",)),
    )(y, gvec, mod, dxo)


def _loss_head(xf, target, name):
    def body(x_ref, t_ref, dx_ref, loss_ref):
        i = pl.program_id(0)

        @pl.when(i == 0)
        def _():
            loss_ref[...] = jnp.zeros_like(loss_ref)

        e = x_ref[...] - t_ref[...]
        dx_ref[...] = e / float(D)
        per_tok = jnp.mean(e * e, axis=-1, keepdims=True)
        loss_ref[0:1, 0:1] += 0.5 * jnp.sum(per_tok, axis=0, keepdims=True)

    return pl.pallas_call(
        body, name=name,
        out_shape=(jax.ShapeDtypeStruct((S, D), F32), jax.ShapeDtypeStruct((8, LANE), F32)),
        grid=(S // TS,),
        in_specs=[_row_spec(), _row_spec()],
        out_specs=(_row_spec(), pl.BlockSpec((8, LANE), lambda i: (0, 0))),
        compiler_params=_params(("arbitrary",)),
    )(xf, target)


def _relu2_fwd(a, name):
    def body(a_ref, r_ref):
        t = jnp.maximum(a_ref[...], 0.0)
        r_ref[...] = (t * t).astype(BF16)

    return pl.pallas_call(
        body, name=name, out_shape=jax.ShapeDtypeStruct((S, D_FF), BF16), grid=(S // TS,),
        in_specs=[_row_spec(D_FF)], out_specs=_row_spec(D_FF),
        compiler_params=_params(("parallel",)),
    )(a)


def _relu2_bwd(a, dr, name):
    def body(a_ref, dr_ref, da_ref):
        da_ref[...] = (dr_ref[...] * (2.0 * jnp.maximum(a_ref[...], 0.0))).astype(BF16)

    return pl.pallas_call(
        body, name=name, out_shape=jax.ShapeDtypeStruct((S, D_FF), BF16), grid=(S // TS,),
        in_specs=[_row_spec(D_FF), _row_spec(D_FF)], out_specs=_row_spec(D_FF),
        compiler_params=_params(("parallel",)),
    )(a, dr)


def _merge_fwd(z, pa, pb, pc, name):
    def body(g0_ref, g1_ref, g2_ref, pa_ref, pb_ref, pc_ref, o_ref):
        m = (jax.nn.sigmoid(g0_ref[...]) * pa_ref[...] + jax.nn.sigmoid(g1_ref[...]) * pb_ref[...]
             + jax.nn.sigmoid(g2_ref[...]) * pc_ref[...])
        o_ref[...] = m.astype(BF16)

    gb = Z_G // D
    return pl.pallas_call(
        body, name=name, out_shape=jax.ShapeDtypeStruct((S, D), BF16), grid=(S // TS,),
        in_specs=[_row_spec(D, gb), _row_spec(D, gb + 1), _row_spec(D, gb + 2), _row_spec(), _row_spec(), _row_spec()],
        out_specs=_row_spec(),
        compiler_params=_params(("parallel",)),
    )(z, z, z, pa, pb, pc)


def _merge_bwd(z, pa, pb, pc, dm, name):
    def body(g0_ref, g1_ref, g2_ref, pa_ref, pb_ref, pc_ref, dm_ref, da_ref, db_ref, dc_ref, dgl_ref):
        dmv = dm_ref[...]
        for k, (g_ref, p_ref, d_ref) in enumerate(((g0_ref, pa_ref, da_ref), (g1_ref, pb_ref, db_ref),
                                                   (g2_ref, pc_ref, dc_ref))):
            sg = jax.nn.sigmoid(g_ref[...])
            d_ref[...] = (dmv * sg).astype(BF16)
            dgl_ref[:, k * D:(k + 1) * D] = (dmv * p_ref[...] * (sg * (1.0 - sg))).astype(BF16)

    gb = Z_G // D
    proj = jax.ShapeDtypeStruct((S, D), BF16)
    return pl.pallas_call(
        body, name=name,
        out_shape=(proj, proj, proj, jax.ShapeDtypeStruct((S, 3 * D), BF16)),
        grid=(S // TS,),
        in_specs=[_row_spec(D, gb), _row_spec(D, gb + 1), _row_spec(D, gb + 2), _row_spec(), _row_spec(), _row_spec(),
                  _row_spec()],
        out_specs=(_row_spec(), _row_spec(), _row_spec(), _row_spec(3 * D)),
        compiler_params=_params(("parallel",)),
    )(z, z, z, pa, pb, pc, dm)


def _shift_down(x, k, row):
    return jnp.where(row >= k, pltpu.roll(x, k, axis=0), 0.0)


def _shift_up(x, k, row):
    n = x.shape[0]
    return jnp.where(row < n - k, pltpu.roll(x, n - k, axis=0), 0.0)


def _cumsum_rows(x, row, reverse=False):
    shift = _shift_up if reverse else _shift_down
    k = 1
    while k < x.shape[0]:
        x = x + shift(x, k, row)
        k *= 2
    return x


def _full_spec(shape, idx=(0, 0)):
    return pl.BlockSpec(shape, lambda i: idx)


def _pool_window_select(lane, a2, a4, a8, a16):
    return jnp.where(lane < 64, a2, jnp.where(lane < 128, a4, jnp.where(lane < 192, a8, a16)))


def _pool_p(u, row, lane):
    t2 = u + _shift_down(u, 1, row)
    t4 = t2 + _shift_down(t2, 2, row)
    t8 = t4 + _shift_down(t4, 4, row)
    t16 = t8 + _shift_down(t8, 8, row)
    tw = _pool_window_select(lane, t2, t4, t8, t16)
    cnt = jnp.minimum((row + 1).astype(F32), _pool_window_select(lane, 2.0, 4.0, 8.0, 16.0))
    return tw / cnt - u, cnt


def _pool_fwd(z, wp_bd, pscale, name):
    def body(u_ref, w_ref, s_ref, o_ref):
        row = lax.broadcasted_iota(jnp.int32, (S, POOL_W), 0)
        lane = lax.broadcasted_iota(jnp.int32, (S, POOL_W), 1)
        p, _ = _pool_p(u_ref[...], row, lane)
        y = jnp.dot(p.astype(BF16), w_ref[...], preferred_element_type=F32)
        o_ref[...] = y * s_ref[0:1, :]

    return pl.pallas_call(
        body, name=name, out_shape=jax.ShapeDtypeStruct((S, POOL_W), F32), grid=(1,),
        in_specs=[_full_spec((S, POOL_W), (0, Z_PC // POOL_W)), _full_spec((POOL_W, POOL_W)), _full_spec((8, POOL_W))],
        out_specs=_full_spec((S, POOL_W)),
        compiler_params=_params(("arbitrary",)),
    )(z, wp_bd, pscale)


def _pool_bwd(z, wp_bd, pscale, dbr, name):
    def body(u_ref, w_ref, s_ref, dbr_ref, du_ref, dw_ref, red_ref):
        row = lax.broadcasted_iota(jnp.int32, (S, POOL_W), 0)
        lane = lax.broadcasted_iota(jnp.int32, (S, POOL_W), 1)
        p, cnt = _pool_p(u_ref[...], row, lane)
        pb = p.astype(BF16)
        y = jnp.dot(pb, w_ref[...], preferred_element_type=F32)
        dbr = dbr_ref[...]
        red_ref[...] = jnp.zeros_like(red_ref)
        red_ref[0:1, :] = jnp.sum(dbr * y, axis=0, keepdims=True)
        dy = (dbr * s_ref[0:1, :]).astype(BF16)
        dw_ref[...] = lax.dot_general(pb, dy, (((0,), (0,)), ((), ())), preferred_element_type=F32)
        dp = lax.dot_general(dy, w_ref[...], (((1,), (1,)), ((), ())), preferred_element_type=F32)
        g = dp / cnt
        a2 = g + _shift_up(g, 1, row)
        a4 = a2 + _shift_up(a2, 2, row)
        a8 = a4 + _shift_up(a4, 4, row)
        a16 = a8 + _shift_up(a8, 8, row)
        du_ref[...] = (_pool_window_select(lane, a2, a4, a8, a16) - dp).astype(BF16)

    return pl.pallas_call(
        body, name=name,
        out_shape=(jax.ShapeDtypeStruct((S, POOL_W), BF16), jax.ShapeDtypeStruct((POOL_W, POOL_W), F32),
                   jax.ShapeDtypeStruct((8, POOL_W), F32)),
        grid=(1,),
        in_specs=[_full_spec((S, POOL_W), (0, Z_PC // POOL_W)), _full_spec((POOL_W, POOL_W)), _full_spec((8, POOL_W)),
                  _full_spec((S, POOL_W))],
        out_specs=(_full_spec((S, POOL_W)), _full_spec((POOL_W, POOL_W)), _full_spec((8, POOL_W))),
        compiler_params=_params(("arbitrary",)),
    )(z, wp_bd, pscale, dbr)


def _conv_specs():
    base = Z_PC // CONV_W
    return [_full_spec((S, CONV_W), (0, base + 1)), _full_spec((S, CONV_W), (0, base + 2)),
            _full_spec((S, CONV_W), (0, base + 3)), _full_spec((8, CONV_W))]


def _conv_fwd(z, cw, name):
    def body(h_ref, b_ref, c_ref, w_ref, o_ref):
        row = lax.broadcasted_iota(jnp.int32, (S, CONV_W), 0)
        u = c_ref[...] * h_ref[...]
        y = (w_ref[0:1, :] * _shift_down(u, 2, row) + w_ref[1:2, :] * _shift_down(u, 1, row) + w_ref[2:3, :] * u)
        o_ref[...] = b_ref[...] * y

    return pl.pallas_call(
        body, name=name, out_shape=jax.ShapeDtypeStruct((S, CONV_W), F32), grid=(1,),
        in_specs=_conv_specs(), out_specs=_full_spec((S, CONV_W)),
        compiler_params=_params(("arbitrary",)),
    )(z, z, z, cw)


def _conv_bwd(z, cw, dbr, name):
    def body(h_ref, b_ref, c_ref, w_ref, dbr_ref, d_ref, red_ref):
        row = lax.broadcasted_iota(jnp.int32, (S, CONV_W), 0)
        h, cg = h_ref[...], c_ref[...]
        u = cg * h
        u1 = _shift_down(u, 1, row)
        u2 = _shift_down(u, 2, row)
        y = w_ref[0:1, :] * u2 + w_ref[1:2, :] * u1 + w_ref[2:3, :] * u
        dbr = dbr_ref[...]
        dy = dbr * b_ref[...]
        du = w_ref[2:3, :] * dy + w_ref[1:2, :] * _shift_up(dy, 1, row) + w_ref[0:1, :] * _shift_up(dy, 2, row)
        d_ref[:, 0:CONV_W] = (du * cg).astype(BF16)
        d_ref[:, CONV_W:2 * CONV_W] = (dbr * y).astype(BF16)
        d_ref[:, 2 * CONV_W:3 * CONV_W] = (du * h).astype(BF16)
        red_ref[...] = jnp.zeros_like(red_ref)
        red_ref[0:1, :] = jnp.sum(dy * u2, axis=0, keepdims=True)
        red_ref[1:2, :] = jnp.sum(dy * u1, axis=0, keepdims=True)
        red_ref[2:3, :] = jnp.sum(dy * u, axis=0, keepdims=True)

    return pl.pallas_call(
        body, name=name,
        out_shape=(jax.ShapeDtypeStruct((S, 3 * CONV_W), BF16), jax.ShapeDtypeStruct((8, CONV_W), F32)),
        grid=(1,),
        in_specs=_conv_specs() + [_full_spec((S, CONV_W))],
        out_specs=(_full_spec((S, 3 * CONV_W)), _full_spec((8, CONV_W))),
        compiler_params=_params(("arbitrary",)),
    )(z, z, z, cw, dbr)


def _pair_lane(h):
    return 8 * (h // 2) + h % 2


def _forget_fwd(z, bf, name):
    def body(f_ref, b_ref, fq_ref, fr_ref):
        row = lax.broadcasted_iota(jnp.int32, (S, LANE), 0)
        lane = lax.broadcasted_iota(jnp.int32, (S, LANE), 1)
        xv = f_ref[...] + b_ref[0:1, :]
        ls = jnp.minimum(xv, 0.0) - jnp.log(1.0 + jnp.exp(-jnp.abs(xv)))
        cum = _cumsum_rows(jnp.where(lane < 8, ls, 0.0), row)
        cols = [cum[:, h:h + 1] for h in range(8)]
        for p in range(N_PAIR):
            fq_ref[:, p * LANE:(p + 1) * LANE] = jnp.where(lane < 64, cols[2 * p], cols[2 * p + 1])
        g = jnp.zeros((S, LANE), F32)
        for h in range(8):
            g = jnp.where(lane == _pair_lane(h), cols[h], g)
        fr_ref[...] = g.T

    return pl.pallas_call(
        body, name=name,
        out_shape=(jax.ShapeDtypeStruct((S, N_PAIR * LANE), F32), jax.ShapeDtypeStruct((LANE, S), F32)),
        grid=(1,),
        in_specs=[_full_spec((S, LANE), (0, Z_F // LANE)), _full_spec((8, LANE))],
        out_specs=(_full_spec((S, N_PAIR * LANE)), _full_spec((LANE, S))),
        compiler_params=_params(("arbitrary",)),
    )(z, bf)


def _forget_bwd(z, bf, dfrow, dfq, name):
    def body(f_ref, b_ref, dfr_ref, dfq_ref, dfl_ref, red_ref):
        row = lax.broadcasted_iota(jnp.int32, (S, LANE), 0)
        lane = lax.broadcasted_iota(jnp.int32, (S, LANE), 1)
        t = jnp.concatenate([dfr_ref[...], jnp.zeros((LANE - 8 * N_PAIR, S), F32)], axis=0).T
        dcum = jnp.zeros((S, LANE), F32)
        for h in range(8):
            at = LANE * (h // 2) + 64 * (h % 2)
            dcum = jnp.where(lane == h, t[:, _pair_lane(h):_pair_lane(h) + 1] + dfq_ref[:, at:at + 1], dcum)
        dls = _cumsum_rows(dcum, row, reverse=True)
        xv = f_ref[...] + b_ref[0:1, :]
        dx = jnp.where(lane < 8, dls * jax.nn.sigmoid(-xv), 0.0)
        dfl_ref[...] = dx.astype(BF16)
        red_ref[...] = jnp.zeros_like(red_ref)
        red_ref[0:1, :] = jnp.sum(dx, axis=0, keepdims=True)

    return pl.pallas_call(
        body, name=name,
        out_shape=(jax.ShapeDtypeStruct((S, LANE), BF16), jax.ShapeDtypeStruct((8, LANE), F32)),
        grid=(1,),
        in_specs=[_full_spec((S, LANE), (0, Z_F // LANE)), _full_spec((8, LANE)), _full_spec((8 * N_PAIR, S)),
                  _full_spec((S, N_PAIR * LANE))],
        out_specs=(_full_spec((S, LANE)), _full_spec((8, LANE))),
        compiler_params=_params(("arbitrary",)),
    )(z, bf, dfrow, dfq)


_NT = (((1,), (1,)), ((), ()))
_TN = (((0,), (0,)), ((), ()))


def _attn_fwd(z, fq, frow, name):
    nq = S // TQ

    def body(q_ref, k_ref, v_ref, fq_ref, fr_ref, o_ref, lse_ref):
        i = pl.program_id(1)
        lane = lax.broadcasted_iota(jnp.int32, (TQ, LANE), 1)
        row = lax.broadcasted_iota(jnp.int32, (TQ, TQ), 0) + i * TQ
        col = lax.broadcasted_iota(jnp.int32, (TQ, TQ), 1)
        q = q_ref[...]
        res = []
        for e in range(2):
            head = (lane >= 64) if e else (lane < 64)
            qm = jnp.where(head, q, 0.0).astype(BF16)
            fqc = fq_ref[:, 64 * e:64 * e + 1]

            def step(j, carry, qm=qm, fqc=fqc, e=e):
                m, l, acc = carry
                off = pl.multiple_of(j * TQ, TQ)
                kb = k_ref[pl.ds(off, TQ), :].astype(BF16)
                vb = v_ref[pl.ds(off, TQ), :].astype(BF16)
                s = lax.dot_general(qm, kb, _NT, preferred_element_type=F32) * ATT_SCALE
                s = s + fqc - fr_ref[e:e + 1, pl.ds(off, TQ)]
                s = jnp.where(col + j * TQ > row, NEG_INF, s)
                mn = jnp.maximum(m, jnp.max(s, axis=1, keepdims=True))
                p = jnp.exp(s - mn)
                alpha = jnp.exp(m - mn)
                l = alpha * l + jnp.sum(p, axis=1, keepdims=True)
                acc = alpha * acc + jnp.dot(p.astype(BF16), vb, preferred_element_type=F32)
                return mn, l, acc

            init = (jnp.full((TQ, 1), NEG_INF, F32), jnp.zeros((TQ, 1), F32), jnp.zeros((TQ, LANE), F32))
            m, l, acc = lax.fori_loop(0, i + 1, step, init)
            res.append((acc / l, m + jnp.log(l)))
        o_ref[...] = jnp.where(lane < 64, res[0][0], res[1][0])
        lse_ref[...] = jnp.where(lane < 64, res[0][1], res[1][1])

    qb, kb_, vb_ = Z_Q // LANE, Z_K // LANE, Z_V // LANE
    out = jax.ShapeDtypeStruct((S, N_PAIR * LANE), F32)
    return pl.pallas_call(
        body, name=name, out_shape=(out, out), grid=(N_PAIR, nq),
        in_specs=[pl.BlockSpec((TQ, LANE), lambda p, i: (i, qb + p)),
                  pl.BlockSpec((S, LANE), lambda p, i: (0, kb_ + p)),
                  pl.BlockSpec((S, LANE), lambda p, i: (0, vb_ + p)),
                  pl.BlockSpec((TQ, LANE), lambda p, i: (i, p)),
                  pl.BlockSpec((8, S), lambda p, i: (p, 0))],
        out_specs=(pl.BlockSpec((TQ, LANE), lambda p, i: (i, p)), pl.BlockSpec((TQ, LANE), lambda p, i: (i, p))),
        compiler_params=_params(("parallel", "parallel")),
    )(z, z, z, fq, frow)


def _attn_bwd(z, fq, frow, o, lse, do, name):
    nq = S // TQ

    def body(q_ref, k_ref, v_ref, fq_ref, fr_ref, o_ref, lse_ref, do_ref, dq_ref, dk_ref, dv_ref, dfr_ref,
             dfq_ref):
        j = pl.program_id(1)

        @pl.when(j == 0)
        def _():
            dq_ref[...] = jnp.zeros_like(dq_ref)
            dfq_ref[...] = jnp.zeros_like(dfq_ref)

        lane = lax.broadcasted_iota(jnp.int32, (TQ, LANE), 1)
        row = lax.broadcasted_iota(jnp.int32, (TQ, TQ), 0)
        col = lax.broadcasted_iota(jnp.int32, (TQ, TQ), 1) + j * TQ
        kb = k_ref[...].astype(BF16)
        vb = v_ref[...].astype(BF16)
        res = []
        for e in range(2):
            head = (lane >= 64) if e else (lane < 64)
            frk = fr_ref[e:e + 1, :]

            def step(i, carry, head=head, frk=frk, e=e):
                dk_acc, dv_acc, df_acc = carry
                off = pl.multiple_of(i * TQ, TQ)
                q = q_ref[pl.ds(off, TQ), :]
                dov = do_ref[pl.ds(off, TQ), :]
                dom = jnp.where(head, dov, 0.0)
                dsum = jnp.sum(dom * o_ref[pl.ds(off, TQ), :], axis=1, keepdims=True)
                qm = jnp.where(head, q, 0.0).astype(BF16)
                s = lax.dot_general(qm, kb, _NT, preferred_element_type=F32) * ATT_SCALE
                s = s + fq_ref[pl.ds(off, TQ), 64 * e:64 * e + 1] - frk
                s = jnp.where(col > row + i * TQ, NEG_INF, s)
                p = jnp.exp(s - lse_ref[pl.ds(off, TQ), 64 * e:64 * e + 1])
                dp = lax.dot_general(dom.astype(BF16), vb, _NT, preferred_element_type=F32)
                ds = p * (dp - dsum)
                dsb = ds.astype(BF16)
                dv_acc = dv_acc + lax.dot_general(p.astype(BF16), dov.astype(BF16), _TN, preferred_element_type=F32)
                dk_acc = dk_acc + lax.dot_general(dsb, q.astype(BF16), _TN, preferred_element_type=F32)
                dqe = jnp.dot(dsb, kb, preferred_element_type=F32) * ATT_SCALE
                dq_ref[pl.ds(off, TQ), :] += jnp.where(head, dqe, 0.0)
                dfq_ref[pl.ds(off, TQ), :] += jnp.where(head, jnp.sum(ds, axis=1, keepdims=True), 0.0)
                df_acc = df_acc - jnp.sum(ds, axis=0, keepdims=True)
                return dk_acc, dv_acc, df_acc

            init = (jnp.zeros((TQ, LANE), F32), jnp.zeros((TQ, LANE), F32), jnp.zeros((1, TQ), F32))
            res.append(lax.fori_loop(j, nq, step, init))
        dk_ref[...] = (jnp.where(lane < 64, res[0][0], res[1][0]) * ATT_SCALE).astype(BF16)
        dv_ref[...] = jnp.where(lane < 64, res[0][1], res[1][1]).astype(BF16)
        sub = lax.broadcasted_iota(jnp.int32, (8, TQ), 0)
        dfr_ref[...] = jnp.where(sub == 0, res[0][2], jnp.where(sub == 1, res[1][2], 0.0))

    qb, kb_, vb_ = Z_Q // LANE, Z_K // LANE, Z_V // LANE
    full = lambda p, j: (0, p)
    width = N_PAIR * LANE
    return pl.pallas_call(
        body, name=name,
        out_shape=(jax.ShapeDtypeStruct((S, width), F32), jax.ShapeDtypeStruct((S, width), BF16),
                   jax.ShapeDtypeStruct((S, width), BF16), jax.ShapeDtypeStruct((8 * N_PAIR, S), F32),
                   jax.ShapeDtypeStruct((S, width), F32)),
        grid=(N_PAIR, nq),
        in_specs=[pl.BlockSpec((S, LANE), lambda p, j: (0, qb + p)),
                  pl.BlockSpec((TQ, LANE), lambda p, j: (j, kb_ + p)),
                  pl.BlockSpec((TQ, LANE), lambda p, j: (j, vb_ + p)),
                  pl.BlockSpec((S, LANE), full),
                  pl.BlockSpec((8, TQ), lambda p, j: (p, j)),
                  pl.BlockSpec((S, LANE), full), pl.BlockSpec((S, LANE), full), pl.BlockSpec((S, LANE), full)],
        out_specs=(pl.BlockSpec((S, LANE), full), pl.BlockSpec((TQ, LANE), lambda p, j: (j, p)),
                   pl.BlockSpec((TQ, LANE), lambda p, j: (j, p)), pl.BlockSpec((8, TQ), lambda p, j: (p, j)),
                   pl.BlockSpec((S, LANE), full)),
        compiler_params=_params(("arbitrary", "arbitrary")),
    )(z, z, z, fq, frow, o, lse, do)


ADA_ROWS = 16


def _ada_fwd(c_pad, w_ada, b_cols, name):
    def body(c_ref, w_ref, b_ref, o_ref):
        cv = c_ref[...]
        sc = (cv * jax.nn.sigmoid(cv)).astype(BF16)
        o_ref[0] = jnp.dot(sc, w_ref[0].astype(BF16), preferred_element_type=F32) + b_ref[0, 0:1, :]

    return pl.pallas_call(
        body, name=name, out_shape=jax.ShapeDtypeStruct((DEPTH, ADA_ROWS, ADA_COLS), F32), grid=(DEPTH,),
        in_specs=[pl.BlockSpec((ADA_ROWS, D), lambda l: (0, 0)), pl.BlockSpec((1, D, ADA_COLS), lambda l: (l, 0, 0)),
                  pl.BlockSpec((1, 8, ADA_COLS), lambda l: (l, 0, 0))],
        out_specs=pl.BlockSpec((1, ADA_ROWS, ADA_COLS), lambda l: (l, 0, 0)),
        compiler_params=_params(("parallel",)),
    )(c_pad, w_ada, b_cols)


def _ada_bwd(c_pad, dmod_cols, name):
    def body(c_ref, d_ref, o_ref):
        cv = c_ref[...]
        sc = (cv * jax.nn.sigmoid(cv)).astype(BF16)
        o_ref[0] = lax.dot_general(sc, d_ref[0].astype(BF16), _TN, preferred_element_type=F32)

    return pl.pallas_call(
        body, name=name, out_shape=jax.ShapeDtypeStruct((DEPTH, D, ADA_COLS), F32), grid=(DEPTH,),
        in_specs=[pl.BlockSpec((ADA_ROWS, D), lambda l: (0, 0)), pl.BlockSpec((1, ADA_ROWS, ADA_COLS), lambda l: (l, 0, 0))],
        out_specs=pl.BlockSpec((1, D, ADA_COLS), lambda l: (l, 0, 0)),
        compiler_params=_params(("parallel",)),
    )(c_pad, dmod_cols)


def _adamw_math(w, g, m, v):
    m = B1 * m + (1.0 - B1) * g
    v = B2 * v + (1.0 - B2) * (g * g)
    m_hat = m / (1.0 - B1 ** STEP)
    v_hat = v / (1.0 - B2 ** STEP)
    delta = -LR * (m_hat / (jnp.sqrt(v_hat) + EPS) + WD * w)
    return delta, m, v


def _row_tile(rows, target=256):
    best = 8
    for t in range(8, min(rows, target) + 1, 8):
        if rows % t == 0:
            best = t
    return best


def _adamw(w, g, m, v, name):
    layers, rows, cols = w.shape
    tr = _row_tile(rows)
    spec = pl.BlockSpec((1, tr, cols), lambda l, i: (l, i, 0))

    def body(w_ref, g_ref, m_ref, v_ref, d_ref, nm_ref, nv_ref):
        d_ref[...], nm_ref[...], nv_ref[...] = _adamw_math(w_ref[...], g_ref[...], m_ref[...], v_ref[...])

    out = jax.ShapeDtypeStruct(w.shape, F32)
    return pl.pallas_call(
        body, name=name, out_shape=(out, out, out), grid=(layers, rows // tr),
        in_specs=[spec] * 4, out_specs=(spec,) * 3, compiler_params=_params(("parallel", "parallel")),
    )(w, g, m, v)


def _sum_slabs(x, name):
    n, rows, _ = x.shape
    tr = _row_tile(rows)

    def body(x_ref, o_ref):
        acc = x_ref[0]
        for k in range(1, n):
            acc = acc + x_ref[k]
        o_ref[...] = acc

    return pl.pallas_call(
        body, name=name, out_shape=jax.ShapeDtypeStruct((rows, D), F32), grid=(rows // tr,),
        in_specs=[pl.BlockSpec((n, tr, D), lambda i: (0, i, 0))], out_specs=pl.BlockSpec((tr, D), lambda i: (i, 0)),
        compiler_params=_params(("parallel",)),
    )(x)


_ANY = pl.BlockSpec(memory_space=pl.ANY)
MESH = pl.DeviceIdType.MESH


def _all_gather(xs, name):
    n = len(xs)

    def body(*refs):
        x_refs, out_refs = refs[:n], refs[n:2 * n]
        send_sems, recv_sems, local_sems = refs[2 * n:]
        x_, y_, c_ = lax.axis_index("x"), lax.axis_index("y"), lax.axis_index("c")
        me, sibling = (x_, y_, c_), (x_, y_, 1 - c_)
        chips = [(1 - x_, y_), (x_, 1 - y_), (1 - x_, 1 - y_)]

        def slot(a, px, py, pc):
            return out_refs[a].at[4 * px + 2 * py + pc]

        def copy(a, k, block, to, src=None):
            return pltpu.make_async_remote_copy(
                src_ref=slot(a, *block) if src is None else src, dst_ref=slot(a, *block),
                send_sem=send_sems.at[7 * a + k], recv_sem=recv_sems.at[7 * a + k], device_id=to, device_id_type=MESH)

        mine = [pltpu.make_async_copy(x_refs[a], slot(a, *me), local_sems.at[a]) for a in range(n)]
        for cp in mine:
            cp.start()
        first = []
        for a in range(n):
            first.append(copy(a, 0, me, sibling, src=x_refs[a]))
            first += [copy(a, 1 + j, me, (*chip, c_), src=x_refs[a]) for j, chip in enumerate(chips)]
        for cp in first:
            cp.start()
        passed = []
        for j, chip in enumerate(chips):
            for a in range(n):
                copy(a, 1 + j, (*chip, c_), me).wait_recv()
                passed.append(copy(a, 4 + j, (*chip, c_), sibling))
                passed[-1].start()
        for a in range(n):
            copy(a, 0, sibling, me).wait_recv()
        for j, chip in enumerate(chips):
            for a in range(n):
                copy(a, 4 + j, (*chip, 1 - c_), me).wait_recv()
        for cp in first + passed:
            cp.wait_send()
        for cp in mine:
            cp.wait()

    return pl.pallas_call(
        body, name=name, out_shape=[jax.ShapeDtypeStruct((N_DEV,) + x.shape, x.dtype) for x in xs],
        in_specs=[_ANY] * n, out_specs=[_ANY] * n,
        scratch_shapes=[pltpu.SemaphoreType.DMA((7 * n,)), pltpu.SemaphoreType.DMA((7 * n,)),
                        pltpu.SemaphoreType.DMA((n,))],
    )(*xs)


def _sibling_exchange(gs, name):
    n = len(gs)

    def body(*refs):
        g_refs, p_refs = refs[:n], refs[n:2 * n]
        send_sems, recv_sems = refs[2 * n:]
        x_, y_, c_ = lax.axis_index("x"), lax.axis_index("y"), lax.axis_index("c")
        copies = [pltpu.make_async_remote_copy(
            src_ref=g_refs[a].at[2 * k + (1 - c_)], dst_ref=p_refs[a].at[k], send_sem=send_sems.at[4 * a + k],
            recv_sem=recv_sems.at[4 * a + k], device_id=(x_, y_, 1 - c_), device_id_type=MESH)
            for a in range(n) for k in range(4)]
        for cp in copies:
            cp.start()
        for cp in copies:
            cp.wait()

    return pl.pallas_call(
        body, name=name, out_shape=[jax.ShapeDtypeStruct((4,) + g.shape[1:], g.dtype) for g in gs],
        in_specs=[_ANY] * n, out_specs=[_ANY] * n,
        scratch_shapes=[pltpu.SemaphoreType.DMA((4 * n,)), pltpu.SemaphoreType.DMA((4 * n,))],
    )(*gs)


def _slab_tiles(rows, cols):
    if rows % 8 == 0:
        return _row_tile(rows), cols
    return rows, 2 * LANE


def _pair_sums(g, p, route, name):
    _, rows, cols = g.shape
    tr, tc = _slab_tiles(rows, cols)

    def body(route_ref, g_ref, p_ref, t_ref):
        t_ref[...] = (g_ref[...] + p_ref[...]).astype(BF16)

    return pl.pallas_call(
        body, name=name, out_shape=jax.ShapeDtypeStruct((3, rows, cols), BF16),
        grid_spec=pltpu.PrefetchScalarGridSpec(
            num_scalar_prefetch=1, grid=(3, rows // tr, cols // tc),
            in_specs=[pl.BlockSpec((1, tr, tc), lambda r, i, j, route_ref: (2 * route_ref[1 + r] + route_ref[0], i, j)),
                      pl.BlockSpec((1, tr, tc), lambda r, i, j, route_ref: (route_ref[1 + r], i, j))],
            out_specs=pl.BlockSpec((1, tr, tc), lambda r, i, j, route_ref: (r, i, j))),
        compiler_params=_params(("parallel", "parallel", "parallel")),
    )(route, g, p)


def _chip_exchange(ts, name):
    n = len(ts)

    def body(*refs):
        t_refs, l_refs = refs[:n], refs[n:2 * n]
        send_sems, recv_sems = refs[2 * n:]
        x_, y_, c_ = lax.axis_index("x"), lax.axis_index("y"), lax.axis_index("c")
        chips = [(1 - x_, y_), (x_, 1 - y_), (1 - x_, 1 - y_)]
        copies = [pltpu.make_async_remote_copy(
            src_ref=t_refs[a].at[r], dst_ref=l_refs[a].at[r], send_sem=send_sems.at[3 * a + r],
            recv_sem=recv_sems.at[3 * a + r], device_id=(px, py, c_), device_id_type=MESH)
            for a in range(n) for r, (px, py) in enumerate(chips)]
        for cp in copies:
            cp.start()
        for cp in copies:
            cp.wait()

    return pl.pallas_call(
        body, name=name, out_shape=[jax.ShapeDtypeStruct((3,) + t.shape[1:], t.dtype) for t in ts],
        in_specs=[_ANY] * n, out_specs=[_ANY] * n,
        scratch_shapes=[pltpu.SemaphoreType.DMA((3 * n,)), pltpu.SemaphoreType.DMA((3 * n,))],
    )(*ts)


def _reduce_adamw(gs, ps, landed, place, w, m, v, name):
    layers, rows, cols = w.shape
    assert layers == DEPTH == 2
    tr, tc = _slab_tiles(rows, cols)
    nr, nc = rows // tr, cols // tc
    spec = pl.BlockSpec((1, tr, tc), lambda l, i, j, place_ref: (l, i, j))

    def own(layer, which):
        pi, pj = (nr - 1, nc - 1) if layer == 0 else (0, 0)

        def index(l, i, j, place_ref):
            lead = 0 if which is None else place_ref[which]
            return lead, jnp.where(l == layer, i, pi), jnp.where(l == layer, j, pj)

        return pl.BlockSpec((3 if which is None else 1, tr, tc), index)

    def body(place_ref, g0_ref, p0_ref, l0_ref, g1_ref, p1_ref, l1_ref, w_ref, m_ref, v_ref,
             g_ref, d_ref, nm_ref, nv_ref):
        def update(own_ref, sib_ref, l_ref):
            g = own_ref[0] + sib_ref[0] + l_ref[0].astype(F32) + l_ref[1].astype(F32) + l_ref[2].astype(F32)
            g_ref[0] = g
            d_ref[0], nm_ref[0], nv_ref[0] = _adamw_math(w_ref[0], g, m_ref[0], v_ref[0])

        @pl.when(pl.program_id(0) == 0)
        def _():
            update(g0_ref, p0_ref, l0_ref)

        @pl.when(pl.program_id(0) == 1)
        def _():
            update(g1_ref, p1_ref, l1_ref)

    out = jax.ShapeDtypeStruct(w.shape, F32)
    return pl.pallas_call(
        body, name=name, out_shape=(out, out, out, out),
        grid_spec=pltpu.PrefetchScalarGridSpec(
            num_scalar_prefetch=1, grid=(DEPTH, nr, nc),
            in_specs=[own(0, 0), own(0, 1), own(0, None), own(1, 0), own(1, 1), own(1, None), spec, spec, spec],
            out_specs=(spec, spec, spec, spec)),
        compiler_params=_params(("arbitrary", "arbitrary", "arbitrary")),
    )(place, gs[0], ps[0], landed[0], gs[1], ps[1], landed[1], w, m, v)


def _pack(pieces, row_multiple, dtype, cols=D, rows=None):
    flat = jnp.concatenate([p.astype(dtype).reshape(-1) for p in pieces])
    if rows is None:
        rows = -(-flat.shape[0] // cols)
        rows = -(-rows // row_multiple) * row_multiple
    flat = jnp.pad(flat, (0, rows * cols - flat.shape[0]))
    return flat.reshape(rows, cols)


def _unpack(flat, shapes, lead=()):
    out, off = [], 0
    for shp in shapes:
        n = 1
        for s_ in shp:
            n *= s_
        out.append(lax.slice_in_dim(flat, off, off + n, axis=len(lead)).reshape(lead + tuple(shp)))
        off += n
    return out


def _z_rows_from_in(wt):
    pad = jnp.zeros((NZ - IN_COLS, wt.shape[1]), wt.dtype)
    return jnp.concatenate([wt[1544:2568], wt[2568:5640], wt[0:1536], wt[1536:1544], pad], axis=0)


def _in_rows_from_z(wt):
    return jnp.concatenate([wt[Z_Q:Z_Q + 1536], wt[Z_F:Z_F + 8], wt[Z_PC:Z_PC + 1024], wt[Z_G:Z_G + 3072]], axis=0)


def _pad_rows(v, rows=8):
    return jnp.pad(v, ((0, rows - v.shape[0]), (0, 0)))


def _layer_fwd(l, x, wts, gvec, mod):
    tag = f"l{l}"
    h = _prenorm_fwd(x, gvec, mod, 0, 0, 1, f"prenorm_mix_{tag}")
    z = _matmul(h, wts["w_in_t"], "nt", f"in_proj_{tag}", tn=1152)
    fq, frow = _forget_fwd(z, wts["b_f"], f"forget_{tag}")
    o, lse = _attn_fwd(z, fq, frow, f"attn_{tag}")
    br_b = _pool_fwd(z, wts["wp_bd"], wts["pool_scale"], f"pool_{tag}")
    br_c = _conv_fwd(z, wts["conv_w"], f"conv_{tag}")
    pa = _matmul(o, wts["wa"], "nn", f"proj_a_{tag}")
    pb = _matmul(br_b, wts["wb"], "nn", f"proj_b_{tag}")
    pc = _matmul(br_c, wts["wc"], "nn", f"proj_c_{tag}")
    merged = _merge_fwd(z, pa, pb, pc, f"merge_{tag}")
    y = _matmul(merged, wts["w_out"], "nn", f"out_proj_{tag}")
    x1 = _postnorm_fwd(x, y, gvec, mod, 1, 2, f"postnorm_mix_{tag}")
    h2 = _prenorm_fwd(x1, gvec, mod, 2, 3, 4, f"prenorm_ff_{tag}")
    a = _matmul(h2, wts["w_ff1"], "nn", f"ff1_{tag}", b_col_shards=True)
    r = _relu2_fwd(a, f"relu2_{tag}")
    y2 = _matmul(r, wts["w_ff2"], "nn", f"ff2_{tag}")
    x2 = _postnorm_fwd(x1, y2, gvec, mod, 3, 5, f"postnorm_ff_{tag}")
    saved = dict(x=x, h=h, z=z, fq=fq, frow=frow, o=o, lse=lse, br_b=br_b, br_c=br_c, pa=pa, pb=pb, pc=pc,
                 merged=merged, y=y, x1=x1, h2=h2, a=a, r=r, y2=y2)
    return x2, saved


def _layer_bwd(l, dx2, sv, wts, gvec, mod):
    tag = f"l{l}"
    dy2, red_post_ff = _postnorm_bwd(sv["y2"], gvec, mod, dx2, 3, 5, f"postnorm_ff_bwd_{tag}")
    dr = _matmul(dy2, wts["w_ff2"], "nt", f"ff2_dx_{tag}")
    d_w_ff2 = _matmul(sv["r"], dy2, "tn", f"ff2_dw_{tag}")
    da = _relu2_bwd(sv["a"], dr, f"relu2_bwd_{tag}")
    dh2 = _matmul(da, wts["w_ff1"], "nt", f"ff1_dx_{tag}", b_col_shards=True)
    d_w_ff1 = _matmul(sv["h2"], da, "tn", f"ff1_dw_{tag}", out_col_shards=True)
    dx1, red_pre_ff = _prenorm_bwd(sv["x1"], gvec, mod, dh2, dx2, 2, 4, f"prenorm_ff_bwd_{tag}")

    dy, red_post_mix = _postnorm_bwd(sv["y"], gvec, mod, dx1, 1, 2, f"postnorm_mix_bwd_{tag}")
    dmerged = _matmul(dy, wts["w_out"], "nt", f"out_proj_dx_{tag}")
    d_w_out = _matmul(sv["merged"], dy, "tn", f"out_proj_dw_{tag}")
    dpa, dpb, dpc, dgl = _merge_bwd(sv["z"], sv["pa"], sv["pb"], sv["pc"], dmerged, f"merge_bwd_{tag}")
    do = _matmul(dpa, wts["wa"], "nt", f"proj_a_dx_{tag}")
    dbr_b = _matmul(dpb, wts["wb"], "nt", f"proj_b_dx_{tag}")
    dbr_c = _matmul(dpc, wts["wc"], "nt", f"proj_c_dx_{tag}")
    d_wa = _matmul(sv["o"], dpa, "tn", f"proj_a_dw_{tag}")
    d_wb = _matmul(sv["br_b"], dpb, "tn", f"proj_b_dw_{tag}")
    d_wc = _matmul(sv["br_c"], dpc, "tn", f"proj_c_dw_{tag}")
    d_w_branch = jnp.concatenate([d_wa, d_wb, d_wc], axis=0)

    dpu, d_wp_bd, red_pool = _pool_bwd(sv["z"], wts["wp_bd"], wts["pool_scale"], dbr_b, f"pool_bwd_{tag}")
    dconv, red_conv = _conv_bwd(sv["z"], wts["conv_w"], dbr_c, f"conv_bwd_{tag}")
    dq, dk, dv, dfrow, dfq = _attn_bwd(sv["z"], sv["fq"], sv["frow"], sv["o"], sv["lse"], do, f"attn_bwd_{tag}")
    dfl, red_f = _forget_bwd(sv["z"], wts["b_f"], dfrow, dfq, f"forget_bwd_{tag}")
    dz = jnp.concatenate([dpu, dconv, dgl, dq.astype(BF16), dk, dv, dfl], axis=1)
    dh = _matmul(dz, wts["w_in_t"], "nn", f"in_proj_dx_{tag}", tk=1152)
    d_w_in_t = _matmul(dz, sv["h"], "tn", f"in_proj_dw_{tag}", tm=1152)
    dx0, red_pre_mix = _prenorm_bwd(sv["x"], gvec, mod, dh, dx1, 0, 1, f"prenorm_mix_bwd_{tag}")

    rows = D // N_DEV
    big = [_in_rows_from_z(d_w_in_t).reshape(N_DEV, IN_SHARD, D), d_w_branch.reshape(N_DEV, rows, D),
           d_w_out.reshape(N_DEV, rows, D), d_w_ff1, d_w_ff2.reshape(N_DEV, D_FF // N_DEV, D)]
    d_w_pool = jnp.stack([d_wp_bd[64 * g:64 * (g + 1), 64 * g:64 * (g + 1)] for g in range(4)])
    small = dict(
        mod=jnp.stack([red_pre_mix[0], red_pre_mix[1], red_post_mix[0], red_pre_ff[0], red_pre_ff[1], red_post_ff[0]]),
        g_mix_pre=red_pre_mix[2], g_mix_post=red_post_mix[1], g_ff_pre=red_pre_ff[2], g_ff_post=red_post_ff[1],
        b_f=red_f[0, 0:8], w_pool=d_w_pool, pool_scale=red_pool[0], conv_w=red_conv[0:3])
    return dx0, big, small


SMALL_KEYS = ["mod", "g_mix_pre", "g_mix_post", "g_ff_pre", "g_ff_post", "b_f", "w_pool", "pool_scale", "conv_w"]
SMALL_SHAPES = [(DEPTH, 6 * D), (DEPTH, D), (DEPTH, D), (DEPTH, D), (DEPTH, D), (DEPTH, 8), (DEPTH, 4, 64, 64),
                (DEPTH, POOL_W), (DEPTH, 3, CONV_W)]


def kernel(x, c, w_ada, b_ada, g_mix_pre, g_mix_post, g_ff_pre, g_ff_post, w_in, b_f, w_pool, pool_scale, conv_w, w_branch, w_out, w_ff1, w_ff2, loss_target, m_w_ada, m_b_ada, m_g_mix_pre, m_g_mix_post, m_g_ff_pre, m_g_ff_post, m_w_in, m_b_f, m_w_pool, m_pool_scale, m_conv_w, m_w_branch, m_w_out, m_w_ff1, m_w_ff2, v_w_ada, v_b_ada, v_g_mix_pre, v_g_mix_post, v_g_ff_pre, v_g_ff_post, v_w_in, v_b_f, v_w_pool, v_pool_scale, v_conv_w, v_w_branch, v_w_out, v_w_ff1, v_w_ff2):
    ix, iy, ic = lax.axis_index("x"), lax.axis_index("y"), lax.axis_index("c")
    me = 4 * ix + 2 * iy + ic
    route = jnp.stack([ic, 2 * (1 - ix) + iy, 2 * ix + (1 - iy), 2 * (1 - ix) + (1 - iy)]).astype(jnp.int32)
    place = jnp.stack([me, 2 * ix + iy]).astype(jnp.int32)
    wt_in, mt_in, vt_in = (jnp.transpose(a, (0, 2, 1)) for a in (w_in, m_w_in, v_w_in))

    c_all = _all_gather([_pad_rows(c)], "gather_c")[0][:, 0, :]
    c_pad = _pad_rows(c_all, ADA_ROWS)
    b_cols = lax.dynamic_slice_in_dim(b_ada, me * ADA_COLS, ADA_COLS, axis=1)
    b_cols = jnp.broadcast_to(b_cols[:, None, :], (DEPTH, 8, ADA_COLS))
    mod_part = _ada_fwd(c_pad, w_ada, b_cols, "ada_fwd")
    mod_all = _all_gather([mod_part.reshape(DEPTH * ADA_ROWS, ADA_COLS)], "gather_mod")[0]
    mod_all = mod_all.reshape(N_DEV, DEPTH, ADA_ROWS, ADA_COLS)
    mod_mine = lax.dynamic_index_in_dim(mod_all, me, axis=2, keepdims=False)
    mod_mine = jnp.transpose(mod_mine, (1, 0, 2)).reshape(DEPTH, 6, D)

    send = [w[l].astype(BF16) for l in range(DEPTH) for w in (wt_in, w_branch, w_out, w_ff1, w_ff2)]
    cw_cols = CONV_W // N_DEV
    cw_send = jnp.pad(conv_w.reshape(DEPTH * 3, cw_cols), ((0, 8 - DEPTH * 3), (0, LANE - cw_cols)))
    gathered = _all_gather(send + [cw_send], "gather_weights")
    cw_all = gathered[-1][:, :DEPTH * 3, :cw_cols].reshape(N_DEV, DEPTH, 3, cw_cols)

    layers = []
    for l in range(DEPTH):
        p_in, p_br, p_out, p_ff1, p_ff2 = gathered[5 * l:5 * l + 5]
        w_br_full = p_br.reshape(D, D)
        cw_full = jnp.transpose(cw_all[:, l], (1, 0, 2)).reshape(3, CONV_W)
        wp_bd = jnp.zeros((POOL_W, POOL_W), F32)
        for g in range(4):
            wp_bd = wp_bd.at[64 * g:64 * (g + 1), 64 * g:64 * (g + 1)].set(w_pool[l, g])
        wts = dict(
            w_in_t=_z_rows_from_in(p_in.reshape(IN_COLS, D)), wa=w_br_full[0:A_WIDTH], wb=w_br_full[A_WIDTH:A_WIDTH + POOL_W],
            wc=w_br_full[A_WIDTH + POOL_W:], w_out=p_out.reshape(D, D),
            w_ff1=p_ff1, w_ff2=p_ff2.reshape(D_FF, D),
            conv_w=_pad_rows(cw_full), wp_bd=wp_bd.astype(BF16), pool_scale=_pad_rows(pool_scale[l][None, :]),
            b_f=_pad_rows(jnp.pad(b_f[l], (0, LANE - 8))[None, :]))
        gvec = _pad_rows(jnp.stack([g_mix_pre[l], g_mix_post[l], g_ff_pre[l], g_ff_post[l]]))
        layers.append((wts, gvec, _pad_rows(mod_mine[l])))

    xs = x[0]
    saved = []
    for l in range(DEPTH):
        xs, sv = _layer_fwd(l, xs, *layers[l])
        saved.append(sv)
    dx, loss_part = _loss_head(xs, loss_target[0], "loss_head")
    loss = lax.psum(loss_part[0, 0], ("x", "y", "c"))
    small_grads, mine, sibs, landed = [None] * DEPTH, [None] * DEPTH, [None] * DEPTH, [None] * DEPTH
    for l in reversed(range(DEPTH)):
        dx, mine[l], small_grads[l] = _layer_bwd(l, dx, saved[l], *layers[l])
        sibs[l] = _sibling_exchange(mine[l], f"rs_sibling_l{l}")
        sends = [_pair_sums(g, p, route, f"rs_pair_sums_{k}_l{l}") for k, (g, p) in enumerate(zip(mine[l], sibs[l]))]
        landed[l] = _chip_exchange(sends, f"rs_chips_l{l}")
    grad_x = dx[None]

    big_w = [wt_in, w_branch, w_out, w_ff1, w_ff2]
    big_m = [mt_in, m_w_branch, m_w_out, m_w_ff1, m_w_ff2]
    big_v = [vt_in, v_w_branch, v_w_out, v_w_ff1, v_w_ff2]
    big_out = [[], [], [], []]
    for k in range(5):
        res = _reduce_adamw([mine[l][k] for l in range(DEPTH)], [sibs[l][k] for l in range(DEPTH)],
                            [landed[l][k] for l in range(DEPTH)], place, big_w[k], big_m[k], big_v[k],
                            f"rs_sum_adamw_{k}")
        for which in range(4):
            big_out[which].append(jnp.transpose(res[which], (0, 2, 1)) if k == 0 else res[which])

    small = {k: jnp.stack([small_grads[l][k] for l in range(DEPTH)]) for k in SMALL_KEYS}
    small_all = _all_gather([_pack([small[k] for k in SMALL_KEYS], 8, F32)], "gather_small")[0]
    dmod_all = small_all[:, 0:DEPTH * 6, :].reshape(N_DEV, DEPTH, 6 * D)
    summed = _unpack(_sum_slabs(small_all, "sum_small").reshape(-1), SMALL_SHAPES)
    sg = dict(zip(SMALL_KEYS, summed))
    dmod_cols = lax.dynamic_slice_in_dim(dmod_all, me * ADA_COLS, ADA_COLS, axis=2)
    dmod_cols = jnp.pad(jnp.transpose(dmod_cols, (1, 0, 2)), ((0, 0), (0, ADA_ROWS - N_DEV), (0, 0)))
    g_w_ada = _ada_bwd(c_pad, dmod_cols, "ada_bwd")
    g_conv_w = lax.dynamic_slice_in_dim(sg["conv_w"], me * (CONV_W // N_DEV), CONV_W // N_DEV, axis=2)

    ada_out = [g_w_ada] + list(_adamw(w_ada, g_w_ada, m_w_ada, v_w_ada, "adamw_ada"))
    rest_w = [b_ada, g_mix_pre, g_mix_post, g_ff_pre, g_ff_post, b_f, w_pool, pool_scale, conv_w]
    rest_m = [m_b_ada, m_g_mix_pre, m_g_mix_post, m_g_ff_pre, m_g_ff_post, m_b_f, m_w_pool, m_pool_scale, m_conv_w]
    rest_v = [v_b_ada, v_g_mix_pre, v_g_mix_post, v_g_ff_pre, v_g_ff_post, v_b_f, v_w_pool, v_pool_scale, v_conv_w]
    rest_g = [sg["mod"], sg["g_mix_pre"], sg["g_mix_post"], sg["g_ff_pre"], sg["g_ff_post"], sg["b_f"],
              sg["w_pool"], sg["pool_scale"], g_conv_w]
    rest_shapes = [a.shape for a in rest_w]
    upd = _adamw(_pack(rest_w, 8, F32)[None], _pack(rest_g, 8, F32)[None], _pack(rest_m, 8, F32)[None],
                 _pack(rest_v, 8, F32)[None], "adamw_rest")
    rest_out = [rest_g] + [_unpack(arr.reshape(-1), rest_shapes) for arr in upd]
    rest_out = [[ada_out[which]] + rest_out[which] for which in range(4)]

    def ordered(k):
        r, b = rest_out[k], big_out[k]
        return [r[0], r[1], r[2], r[3], r[4], r[5], b[0], r[6], r[7], r[8], r[9], b[1], b[2], b[3], b[4]]

    return (loss, grad_x, *ordered(0), *ordered(1), *ordered(2), *ordered(3))
```

```python
import functools

import jax
import jax.numpy as jnp
from jax import lax
from jax.experimental import pallas as pl
from jax.experimental.pallas import tpu as pltpu

F32 = jnp.float32
BF16 = jnp.bfloat16

N_DEV = 8
D = 1024
S = 2048
DEPTH = 2
D_FF = 4 * D
A_WIDTH = 512
HEAD_DIM = 64
N_PAIR = 4
POOL_W = 256
CONV_W = 256
IN_COLS = 5640
ADA_COLS = 6 * D // N_DEV
IN_SHARD = IN_COLS // N_DEV
RMS_EPS = 1e-6
NEG_INF = -1e30
ATT_SCALE = HEAD_DIM ** -0.5

NZ = 5760
Z_PC = 0
Z_G = 1024
Z_Q = 4096
Z_K = 4608
Z_V = 5120
Z_F = 5632

LR, B1, B2, EPS, WD, STEP = 0.001, 0.9, 0.999, 1e-08, 0.01, 10

LANE = 128
VMEM_LIMIT_BYTES = 48 * 1024 * 1024
TS = 256
TQ = 256
TQ_FWD = 512


def _params(sem=None):
    return pltpu.CompilerParams(dimension_semantics=sem, vmem_limit_bytes=VMEM_LIMIT_BYTES)


def _pick(n, target):
    best = None
    for t in range(LANE, min(n, target) + 1, LANE):
        if n % t == 0:
            best = t
    return n if best is None else best


def _matmul(a, b, mode, name, out_dtype=F32, tm=1024, tn=1024, tk=1024, b_col_shards=False, out_col_shards=False):
    if b_col_shards:
        shards, b_rows, shard_cols = b.shape
        b_shape = (b_rows, shards * shard_cols)
    else:
        b_shape = b.shape
    if mode == "nn":
        (m, k), (k2, n) = a.shape, b_shape
    elif mode == "nt":
        (m, k), (n, k2) = a.shape, b_shape
    else:
        (k, m), (k2, n) = a.shape, b_shape
    assert k == k2, (a.shape, b.shape, mode)
    tm, tn, tk = _pick(m, tm), _pick(n, tn), _pick(k, tk)
    if b_col_shards and mode == "nn":
        tn = shard_cols
    if b_col_shards and mode == "nt":
        tk = shard_cols
    if out_col_shards:
        tn = n // N_DEV
    nk = k // tk
    if mode == "nn":
        a_spec = pl.BlockSpec((tm, tk), lambda i, j, kk: (i, kk))
        b_spec = (pl.BlockSpec((None, tk, tn), lambda i, j, kk: (j, kk, 0)) if b_col_shards else
                  pl.BlockSpec((tk, tn), lambda i, j, kk: (kk, j)))
        dims = (((1,), (0,)), ((), ()))
    elif mode == "nt":
        a_spec = pl.BlockSpec((tm, tk), lambda i, j, kk: (i, kk))
        b_spec = (pl.BlockSpec((None, tn, tk), lambda i, j, kk: (kk, j, 0)) if b_col_shards else
                  pl.BlockSpec((tn, tk), lambda i, j, kk: (j, kk)))
        dims = (((1,), (1,)), ((), ()))
    else:
        assert not b_col_shards
        a_spec = pl.BlockSpec((tk, tm), lambda i, j, kk: (kk, i))
        b_spec = pl.BlockSpec((tk, tn), lambda i, j, kk: (kk, j))
        dims = (((0,), (0,)), ((), ()))
    if out_col_shards:
        out_shape = jax.ShapeDtypeStruct((N_DEV, m, tn), out_dtype)
        out_spec = pl.BlockSpec((None, tm, tn), lambda i, j, kk: (j, i, 0))
    else:
        out_shape = jax.ShapeDtypeStruct((m, n), out_dtype)
        out_spec = pl.BlockSpec((tm, tn), lambda i, j, kk: (i, j))

    def product(a_ref, b_ref):
        return lax.dot_general(a_ref[...].astype(BF16), b_ref[...].astype(BF16), dims, preferred_element_type=F32)

    def body_one_pass(a_ref, b_ref, o_ref):
        o_ref[...] = product(a_ref, b_ref).astype(out_dtype)

    def body(a_ref, b_ref, o_ref, acc_ref):
        kk = pl.program_id(2)

        @pl.when(kk == 0)
        def _():
            acc_ref[...] = product(a_ref, b_ref)

        @pl.when(kk > 0)
        def _():
            acc_ref[...] += product(a_ref, b_ref)

        @pl.when(kk == nk - 1)
        def _():
            o_ref[...] = acc_ref[...].astype(out_dtype)

    return pl.pallas_call(
        body_one_pass if nk == 1 else body, name=name,
        out_shape=out_shape,
        grid=(m // tm, n // tn, nk),
        in_specs=[a_spec, b_spec],
        out_specs=out_spec,
        scratch_shapes=[] if nk == 1 else [pltpu.VMEM((tm, tn), F32)],
        compiler_params=_params(("parallel", "parallel", "arbitrary")),
    )(a, b)


def _row_spec(width=D, col=0):
    return pl.BlockSpec((TS, width), lambda i: (i, col))


def _vec_spec(rows=8, width=D):
    return pl.BlockSpec((rows, width), lambda i: (0, 0))


def _rms(x):
    return lax.rsqrt(jnp.mean(x * x, axis=-1, keepdims=True) + RMS_EPS)


def _prenorm_fwd(x, gvec, mod, g_row, shift_row, scale_row, name):
    def body(x_ref, g_ref, mod_ref, h_ref):
        xv = x_ref[...]
        y = xv * _rms(xv) * g_ref[g_row:g_row + 1, :]
        h = y * (1.0 + mod_ref[scale_row:scale_row + 1, :]) + mod_ref[shift_row:shift_row + 1, :]
        h_ref[...] = h.astype(BF16)

    return pl.pallas_call(
        body, name=name, out_shape=jax.ShapeDtypeStruct((S, D), BF16), grid=(S // TS,),
        in_specs=[_row_spec(), _vec_spec(), _vec_spec()], out_specs=_row_spec(),
        compiler_params=_params(("parallel",)),
    )(x, gvec, mod)


def _prenorm_bwd(x, gvec, mod, dh, dres, g_row, scale_row, name):
    def body(x_ref, g_ref, mod_ref, dh_ref, dres_ref, dx_ref, red_ref):
        i = pl.program_id(0)

        @pl.when(i == 0)
        def _():
            red_ref[...] = jnp.zeros_like(red_ref)

        xv = x_ref[...]
        g = g_ref[g_row:g_row + 1, :]
        r = _rms(xv)
        n = xv * r
        yg = n * g
        dhv = dh_ref[...]
        dyg = dhv * (1.0 + mod_ref[scale_row:scale_row + 1, :])
        dn = dyg * g
        dx = r * (dn - n * jnp.mean(dn * n, axis=-1, keepdims=True))
        dx_ref[...] = dres_ref[...] + dx
        red_ref[0:1, :] += jnp.sum(dhv, axis=0, keepdims=True)
        red_ref[1:2, :] += jnp.sum(dhv * yg, axis=0, keepdims=True)
        red_ref[2:3, :] += jnp.sum(dyg * n, axis=0, keepdims=True)

    return pl.pallas_call(
        body, name=name,
        out_shape=(jax.ShapeDtypeStruct((S, D), F32), jax.ShapeDtypeStruct((8, D), F32)),
        grid=(S // TS,),
        in_specs=[_row_spec(), _vec_spec(), _vec_spec(), _row_spec(), _row_spec()],
        out_specs=(_row_spec(), _vec_spec()),
        compiler_params=_params(("arbitrary",)),
    )(x, gvec, mod, dh, dres)


def _postnorm_fwd(x, y, gvec, mod, g_row, gate_row, name):
    def body(x_ref, y_ref, g_ref, mod_ref, o_ref):
        yv = y_ref[...]
        yn = yv * _rms(yv) * g_ref[g_row:g_row + 1, :]
        o_ref[...] = x_ref[...] + mod_ref[gate_row:gate_row + 1, :] * yn

    return pl.pallas_call(
        body, name=name, out_shape=jax.ShapeDtypeStruct((S, D), F32), grid=(S // TS,),
        in_specs=[_row_spec(), _row_spec(), _vec_spec(), _vec_spec()], out_specs=_row_spec(),
        compiler_params=_params(("parallel",)),
    )(x, y, gvec, mod)


def _postnorm_bwd(y, gvec, mod, dxo, g_row, gate_row, name):
    def body(y_ref, g_ref, mod_ref, dxo_ref, dy_ref, red_ref):
        i = pl.program_id(0)

        @pl.when(i == 0)
        def _():
            red_ref[...] = jnp.zeros_like(red_ref)

        yv = y_ref[...]
        g = g_ref[g_row:g_row + 1, :]
        r = _rms(yv)
        n = yv * r
        dxo = dxo_ref[...]
        dyn = dxo * mod_ref[gate_row:gate_row + 1, :]
        dn = dyn * g
        dy = r * (dn - n * jnp.mean(dn * n, axis=-1, keepdims=True))
        dy_ref[...] = dy.astype(BF16)
        red_ref[0:1, :] += jnp.sum(dxo * (n * g), axis=0, keepdims=True)
        red_ref[1:2, :] += jnp.sum(dyn * n, axis=0, keepdims=True)

    return pl.pallas_call(
        body, name=name,
        out_shape=(jax.ShapeDtypeStruct((S, D), BF16), jax.ShapeDtypeStruct((8, D), F32)),
        grid=(S // TS,),
        in_specs=[_row_spec(), _vec_spec(), _vec_spec(), _row_spec()],
        out_specs=(_row_spec(), _vec_spec()),
        compiler_params=_params(("arbitrary",)),
    )(y, gvec, mod, dxo)


def _loss_head(xf, target, name):
    def body(x_ref, t_ref, dx_ref, loss_ref):
        i = pl.program_id(0)

        @pl.when(i == 0)
        def _():
            loss_ref[...] = jnp.zeros_like(loss_ref)

        e = x_ref[...] - t_ref[...]
        dx_ref[...] = e / float(D)
        per_tok = jnp.mean(e * e, axis=-1, keepdims=True)
        loss_ref[0:1, 0:1] += 0.5 * jnp.sum(per_tok, axis=0, keepdims=True)

    return pl.pallas_call(
        body, name=name,
        out_shape=(jax.ShapeDtypeStruct((S, D), F32), jax.ShapeDtypeStruct((8, LANE), F32)),
        grid=(S // TS,),
        in_specs=[_row_spec(), _row_spec()],
        out_specs=(_row_spec(), pl.BlockSpec((8, LANE), lambda i: (0, 0))),
        compiler_params=_params(("arbitrary",)),
    )(xf, target)


def _relu2_fwd(a, name):
    def body(a_ref, r_ref):
        t = jnp.maximum(a_ref[...], 0.0)
        r_ref[...] = (t * t).astype(BF16)

    return pl.pallas_call(
        body, name=name, out_shape=jax.ShapeDtypeStruct((S, D_FF), BF16), grid=(S // TS,),
        in_specs=[_row_spec(D_FF)], out_specs=_row_spec(D_FF),
        compiler_params=_params(("parallel",)),
    )(a)


def _relu2_bwd(a, dr, name):
    def body(a_ref, dr_ref, da_ref):
        da_ref[...] = (dr_ref[...] * (2.0 * jnp.maximum(a_ref[...], 0.0))).astype(BF16)

    return pl.pallas_call(
        body, name=name, out_shape=jax.ShapeDtypeStruct((S, D_FF), BF16), grid=(S // TS,),
        in_specs=[_row_spec(D_FF), _row_spec(D_FF)], out_specs=_row_spec(D_FF),
        compiler_params=_params(("parallel",)),
    )(a, dr)


def _merge_fwd(z, pa, pb, pc, name):
    def body(g0_ref, g1_ref, g2_ref, pa_ref, pb_ref, pc_ref, o_ref):
        m = (jax.nn.sigmoid(g0_ref[...]) * pa_ref[...] + jax.nn.sigmoid(g1_ref[...]) * pb_ref[...]
             + jax.nn.sigmoid(g2_ref[...]) * pc_ref[...])
        o_ref[...] = m.astype(BF16)

    gb = Z_G // D
    return pl.pallas_call(
        body, name=name, out_shape=jax.ShapeDtypeStruct((S, D), BF16), grid=(S // TS,),
        in_specs=[_row_spec(D, gb), _row_spec(D, gb + 1), _row_spec(D, gb + 2), _row_spec(), _row_spec(), _row_spec()],
        out_specs=_row_spec(),
        compiler_params=_params(("parallel",)),
    )(z, z, z, pa, pb, pc)


def _merge_bwd(z, pa, pb, pc, dm, name):
    def body(g0_ref, g1_ref, g2_ref, pa_ref, pb_ref, pc_ref, dm_ref, da_ref, db_ref, dc_ref, dgl_ref):
        dmv = dm_ref[...]
        for k, (g_ref, p_ref, d_ref) in enumerate(((g0_ref, pa_ref, da_ref), (g1_ref, pb_ref, db_ref),
                                                   (g2_ref, pc_ref, dc_ref))):
            sg = jax.nn.sigmoid(g_ref[...])
            d_ref[...] = (dmv * sg).astype(BF16)
            dgl_ref[:, k * D:(k + 1) * D] = (dmv * p_ref[...] * (sg * (1.0 - sg))).astype(BF16)

    gb = Z_G // D
    proj = jax.ShapeDtypeStruct((S, D), BF16)
    return pl.pallas_call(
        body, name=name,
        out_shape=(proj, proj, proj, jax.ShapeDtypeStruct((S, 3 * D), BF16)),
        grid=(S // TS,),
        in_specs=[_row_spec(D, gb), _row_spec(D, gb + 1), _row_spec(D, gb + 2), _row_spec(), _row_spec(), _row_spec(),
                  _row_spec()],
        out_specs=(_row_spec(), _row_spec(), _row_spec(), _row_spec(3 * D)),
        compiler_params=_params(("parallel",)),
    )(z, z, z, pa, pb, pc, dm)


def _shift_down(x, k, row):
    return jnp.where(row >= k, pltpu.roll(x, k, axis=0), 0.0)


def _shift_up(x, k, row):
    n = x.shape[0]
    return jnp.where(row < n - k, pltpu.roll(x, n - k, axis=0), 0.0)


def _cumsum_rows(x, row, reverse=False):
    shift = _shift_up if reverse else _shift_down
    k = 1
    while k < x.shape[0]:
        x = x + shift(x, k, row)
        k *= 2
    return x


def _full_spec(shape, idx=(0, 0)):
    return pl.BlockSpec(shape, lambda i: idx)


def _pool_window_select(lane, a2, a4, a8, a16):
    return jnp.where(lane < 64, a2, jnp.where(lane < 128, a4, jnp.where(lane < 192, a8, a16)))


def _pool_p(u, row, lane):
    t2 = u + _shift_down(u, 1, row)
    t4 = t2 + _shift_down(t2, 2, row)
    t8 = t4 + _shift_down(t4, 4, row)
    t16 = t8 + _shift_down(t8, 8, row)
    tw = _pool_window_select(lane, t2, t4, t8, t16)
    cnt = jnp.minimum((row + 1).astype(F32), _pool_window_select(lane, 2.0, 4.0, 8.0, 16.0))
    return tw / cnt - u, cnt


def _pool_fwd(z, wp_bd, pscale, name):
    def body(u_ref, w_ref, s_ref, o_ref):
        row = lax.broadcasted_iota(jnp.int32, (S, POOL_W), 0)
        lane = lax.broadcasted_iota(jnp.int32, (S, POOL_W), 1)
        p, _ = _pool_p(u_ref[...], row, lane)
        y = jnp.dot(p.astype(BF16), w_ref[...], preferred_element_type=F32)
        o_ref[...] = y * s_ref[0:1, :]

    return pl.pallas_call(
        body, name=name, out_shape=jax.ShapeDtypeStruct((S, POOL_W), F32), grid=(1,),
        in_specs=[_full_spec((S, POOL_W), (0, Z_PC // POOL_W)), _full_spec((POOL_W, POOL_W)), _full_spec((8, POOL_W))],
        out_specs=_full_spec((S, POOL_W)),
        compiler_params=_params(("arbitrary",)),
    )(z, wp_bd, pscale)


def _pool_bwd(z, wp_bd, pscale, dbr, name):
    def body(u_ref, w_ref, s_ref, dbr_ref, du_ref, dw_ref, red_ref):
        row = lax.broadcasted_iota(jnp.int32, (S, POOL_W), 0)
        lane = lax.broadcasted_iota(jnp.int32, (S, POOL_W), 1)
        p, cnt = _pool_p(u_ref[...], row, lane)
        pb = p.astype(BF16)
        y = jnp.dot(pb, w_ref[...], preferred_element_type=F32)
        dbr = dbr_ref[...]
        red_ref[...] = jnp.zeros_like(red_ref)
        red_ref[0:1, :] = jnp.sum(dbr * y, axis=0, keepdims=True)
        dy = (dbr * s_ref[0:1, :]).astype(BF16)
        dw_ref[...] = lax.dot_general(pb, dy, (((0,), (0,)), ((), ())), preferred_element_type=F32)
        dp = lax.dot_general(dy, w_ref[...], (((1,), (1,)), ((), ())), preferred_element_type=F32)
        g = dp / cnt
        a2 = g + _shift_up(g, 1, row)
        a4 = a2 + _shift_up(a2, 2, row)
        a8 = a4 + _shift_up(a4, 4, row)
        a16 = a8 + _shift_up(a8, 8, row)
        du_ref[...] = (_pool_window_select(lane, a2, a4, a8, a16) - dp).astype(BF16)

    return pl.pallas_call(
        body, name=name,
        out_shape=(jax.ShapeDtypeStruct((S, POOL_W), BF16), jax.ShapeDtypeStruct((POOL_W, POOL_W), F32),
                   jax.ShapeDtypeStruct((8, POOL_W), F32)),
        grid=(1,),
        in_specs=[_full_spec((S, POOL_W), (0, Z_PC // POOL_W)), _full_spec((POOL_W, POOL_W)), _full_spec((8, POOL_W)),
                  _full_spec((S, POOL_W))],
        out_specs=(_full_spec((S, POOL_W)), _full_spec((POOL_W, POOL_W)), _full_spec((8, POOL_W))),
        compiler_params=_params(("arbitrary",)),
    )(z, wp_bd, pscale, dbr)


def _conv_specs():
    base = Z_PC // CONV_W
    return [_full_spec((S, CONV_W), (0, base + 1)), _full_spec((S, CONV_W), (0, base + 2)),
            _full_spec((S, CONV_W), (0, base + 3)), _full_spec((8, CONV_W))]


def _conv_fwd(z, cw, name):
    def body(h_ref, b_ref, c_ref, w_ref, o_ref):
        row = lax.broadcasted_iota(jnp.int32, (S, CONV_W), 0)
        u = c_ref[...] * h_ref[...]
        y = (w_ref[0:1, :] * _shift_down(u, 2, row) + w_ref[1:2, :] * _shift_down(u, 1, row) + w_ref[2:3, :] * u)
        o_ref[...] = b_ref[...] * y

    return pl.pallas_call(
        body, name=name, out_shape=jax.ShapeDtypeStruct((S, CONV_W), F32), grid=(1,),
        in_specs=_conv_specs(), out_specs=_full_spec((S, CONV_W)),
        compiler_params=_params(("arbitrary",)),
    )(z, z, z, cw)


def _conv_bwd(z, cw, dbr, name):
    def body(h_ref, b_ref, c_ref, w_ref, dbr_ref, d_ref, red_ref):
        row = lax.broadcasted_iota(jnp.int32, (S, CONV_W), 0)
        h, cg = h_ref[...], c_ref[...]
        u = cg * h
        u1 = _shift_down(u, 1, row)
        u2 = _shift_down(u, 2, row)
        y = w_ref[0:1, :] * u2 + w_ref[1:2, :] * u1 + w_ref[2:3, :] * u
        dbr = dbr_ref[...]
        dy = dbr * b_ref[...]
        du = w_ref[2:3, :] * dy + w_ref[1:2, :] * _shift_up(dy, 1, row) + w_ref[0:1, :] * _shift_up(dy, 2, row)
        d_ref[:, 0:CONV_W] = (du * cg).astype(BF16)
        d_ref[:, CONV_W:2 * CONV_W] = (dbr * y).astype(BF16)
        d_ref[:, 2 * CONV_W:3 * CONV_W] = (du * h).astype(BF16)
        red_ref[...] = jnp.zeros_like(red_ref)
        red_ref[0:1, :] = jnp.sum(dy * u2, axis=0, keepdims=True)
        red_ref[1:2, :] = jnp.sum(dy * u1, axis=0, keepdims=True)
        red_ref[2:3, :] = jnp.sum(dy * u, axis=0, keepdims=True)

    return pl.pallas_call(
        body, name=name,
        out_shape=(jax.ShapeDtypeStruct((S, 3 * CONV_W), BF16), jax.ShapeDtypeStruct((8, CONV_W), F32)),
        grid=(1,),
        in_specs=_conv_specs() + [_full_spec((S, CONV_W))],
        out_specs=(_full_spec((S, 3 * CONV_W)), _full_spec((8, CONV_W))),
        compiler_params=_params(("arbitrary",)),
    )(z, z, z, cw, dbr)


_NT = (((1,), (1,)), ((), ()))
_TN = (((0,), (0,)), ((), ()))
N_HEAD = 2 * N_PAIR


def _split3(x):
    hi = x.astype(BF16).astype(F32)
    mid = (x - hi).astype(BF16).astype(F32)
    lo = (x - hi - mid).astype(BF16).astype(F32)
    return hi, mid, lo


def _spare(lane, e, k):
    return lane == 64 * (1 - e) + k


def _spare3(lane, e, k):
    base = 64 * (1 - e) + k
    return (lane >= base) & (lane < base + 3)


def _put3(lane, e, k, pieces, rest):
    out = rest
    for n, piece in enumerate(pieces):
        out = jnp.where(_spare(lane, e, k + n), piece, out)
    return out


def _attn_prep(z, bf, name):
    def body(q_ref, k_ref, v_ref, f_ref, b_ref, qa_ref, ka_ref, va_ref, kat_ref):
        p = pl.program_id(0)
        row = lax.broadcasted_iota(jnp.int32, (S, LANE), 0)
        lane = lax.broadcasted_iota(jnp.int32, (S, LANE), 1)
        xv = f_ref[...] + b_ref[0:1, :]
        ls = jnp.minimum(xv, 0.0) - jnp.log(1.0 + jnp.exp(-jnp.abs(xv)))
        cum = _cumsum_rows(jnp.where(lane < N_HEAD, ls, 0.0), row)
        q, k, v = q_ref[...], k_ref[...], v_ref[...]
        for e in range(2):
            head = (lane >= 64) if e else (lane < 64)
            f = jnp.sum(jnp.where(lane == 2 * p + e, cum, 0.0), axis=1, keepdims=True)
            pieces = _split3(f)
            qa = jnp.where(head, q * ATT_SCALE, _put3(lane, e, 0, pieces, jnp.where(_spare3(lane, e, 3), 1.0, 0.0)))
            ones = jnp.where(_spare3(lane, e, 0) | _spare3(lane, e, 6), 1.0, 0.0)
            ka = jnp.where(head, k, _put3(lane, e, 3, [-x for x in pieces], ones))
            va = jnp.where(head, v, jnp.where(_spare3(lane, e, 0), 1.0, 0.0))
            qa_ref[e] = qa.astype(BF16)
            ka_ref[e] = ka.astype(BF16)
            va_ref[e] = va.astype(BF16)
            kat_ref[e] = ka.T.astype(BF16)

    qb, kb, vb = Z_Q // LANE, Z_K // LANE, Z_V // LANE
    heads = jax.ShapeDtypeStruct((N_HEAD, S, LANE), BF16)
    pair = pl.BlockSpec((2, S, LANE), lambda p: (p, 0, 0))
    return pl.pallas_call(
        body, name=name,
        out_shape=(heads, heads, heads, jax.ShapeDtypeStruct((N_HEAD, LANE, S), BF16)),
        grid=(N_PAIR,),
        in_specs=[pl.BlockSpec((S, LANE), lambda p: (0, qb + p)), pl.BlockSpec((S, LANE), lambda p: (0, kb + p)),
                  pl.BlockSpec((S, LANE), lambda p: (0, vb + p)), pl.BlockSpec((S, LANE), lambda p: (0, Z_F // LANE)),
                  pl.BlockSpec((8, LANE), lambda p: (0, 0))],
        out_specs=(pair, pair, pair, pl.BlockSpec((2, LANE, S), lambda p: (p, 0, 0))),
        compiler_params=_params(("parallel",)),
    )(z, z, z, z, bf)


def _attn_bwd_prep(qa, o, lse, do, name):
    def body(qa_ref, o_ref, lse_ref, do_ref, qa2_ref, doa_ref):
        lane = lax.broadcasted_iota(jnp.int32, (S, LANE), 1)
        dov, ov, lsev = do_ref[...], o_ref[...], lse_ref[...]
        for e in range(2):
            head = (lane >= 64) if e else (lane < 64)
            dsum = jnp.sum(jnp.where(head, dov * ov, 0.0), axis=1, keepdims=True)
            doa_ref[e] = jnp.where(head, dov, _put3(lane, e, 0, [-x for x in _split3(dsum)], 0.0)).astype(BF16)
            lse_col = lsev[:, 64 * e:64 * e + 1]
            qa2_ref[e] = _put3(lane, e, 6, [-x for x in _split3(lse_col)], qa_ref[e].astype(F32)).astype(BF16)

    heads = jax.ShapeDtypeStruct((N_HEAD, S, LANE), BF16)
    pair = pl.BlockSpec((2, S, LANE), lambda p: (p, 0, 0))
    cols = pl.BlockSpec((S, LANE), lambda p: (0, p))
    return pl.pallas_call(
        body, name=name, out_shape=(heads, heads), grid=(N_PAIR,),
        in_specs=[pair, cols, cols, cols], out_specs=(pair, pair),
        compiler_params=_params(("parallel",)),
    )(qa, o, lse, do)


def _attn_bwd_post(z, bf, dqt, dka, dva, name):
    def body(f_ref, b_ref, dqt_ref, dk_ref, dv_ref, dq_out, dk_out, dv_out, dfl_ref, red_ref, dcum_ref):
        p = pl.program_id(0)

        @pl.when(p == 0)
        def _():
            dcum_ref[...] = jnp.zeros_like(dcum_ref)

        row = lax.broadcasted_iota(jnp.int32, (S, LANE), 0)
        lane = lax.broadcasted_iota(jnp.int32, (S, LANE), 1)
        dqa = [dqt_ref[e].T for e in range(2)]
        dq_out[...] = (jnp.where(lane < 64, dqa[0], dqa[1]) * ATT_SCALE).astype(BF16)
        dk_out[...] = jnp.where(lane < 64, dk_ref[0], dk_ref[1]).astype(BF16)
        dv_out[...] = jnp.where(lane < 64, dv_ref[0], dv_ref[1]).astype(BF16)
        for e in range(2):
            d_query = jnp.sum(jnp.where(_spare(lane, e, 0), dqa[e], 0.0), axis=1, keepdims=True)
            d_key = jnp.sum(jnp.where(_spare(lane, e, 3), dk_ref[e], 0.0), axis=1, keepdims=True)
            dcum_ref[...] += jnp.where(lane == 2 * p + e, d_query - d_key, 0.0)

        @pl.when(p == N_PAIR - 1)
        def _():
            dls = _cumsum_rows(dcum_ref[...], row, reverse=True)
            xv = f_ref[...] + b_ref[0:1, :]
            dx = jnp.where(lane < N_HEAD, dls * jax.nn.sigmoid(-xv), 0.0)
            dfl_ref[...] = dx.astype(BF16)
            red_ref[...] = jnp.zeros_like(red_ref)
            red_ref[0:1, :] = jnp.sum(dx, axis=0, keepdims=True)

    wide = jax.ShapeDtypeStruct((S, N_PAIR * LANE), BF16)
    cols = pl.BlockSpec((S, LANE), lambda p: (0, p))
    pair = pl.BlockSpec((2, S, LANE), lambda p: (p, 0, 0))
    return pl.pallas_call(
        body, name=name,
        out_shape=(wide, wide, wide, jax.ShapeDtypeStruct((S, LANE), BF16), jax.ShapeDtypeStruct((8, LANE), F32)),
        grid=(N_PAIR,),
        in_specs=[pl.BlockSpec((S, LANE), lambda p: (0, Z_F // LANE)), pl.BlockSpec((8, LANE), lambda p: (0, 0)),
                  pl.BlockSpec((2, LANE, S), lambda p: (p, 0, 0)), pair, pair],
        out_specs=(cols, cols, cols, pl.BlockSpec((S, LANE), lambda p: (0, 0)), pl.BlockSpec((8, LANE), lambda p: (0, 0))),
        scratch_shapes=[pltpu.VMEM((S, LANE), F32)],
        compiler_params=_params(("arbitrary",)),
    )(z, bf, dqt, dka, dva)


def _attn_fwd(qa, ka, va, name):
    tq, tk = TQ_FWD, TQ
    ratio = tq // tk

    def body(qa_ref, ka_ref, va_ref, o_ref, lse_ref):
        i = pl.program_id(1)
        lane = lax.broadcasted_iota(jnp.int32, (tq, LANE), 1)
        row = lax.broadcasted_iota(jnp.int32, (tq, tk), 0)
        col = lax.broadcasted_iota(jnp.int32, (tq, tk), 1)
        qs = [qa_ref[0], qa_ref[1]]

        def block(j, carry, masked):
            off = pl.multiple_of(j * tk, tk)
            out = []
            for e in range(2):
                m, acc = carry[e]
                s = lax.dot_general(qs[e], ka_ref[e, pl.ds(off, tk), :], _NT, preferred_element_type=F32)
                if masked:
                    s = jnp.where(col + (j - ratio * i) * tk > row, NEG_INF, s)
                mn = jnp.maximum(m, jnp.max(s, axis=1, keepdims=True))
                p = jnp.exp(s - mn).astype(BF16)
                acc = jnp.exp(m - mn) * acc + jnp.dot(p, va_ref[e, pl.ds(off, tk), :], preferred_element_type=F32)
                out.append((mn, acc))
            return tuple(out)

        init = (jnp.full((tq, 1), NEG_INF, F32), jnp.zeros((tq, LANE), F32))
        carry = lax.fori_loop(0, ratio * i, lambda j, c: block(j, c, False), (init, init))
        for d in range(ratio):
            carry = block(ratio * i + d, carry, True)
        res = []
        for e in range(2):
            m, acc = carry[e]
            l = jnp.sum(jnp.where(_spare(lane, e, 0), acc, 0.0), axis=1, keepdims=True)
            res.append((acc / l, m + jnp.log(l)))
        o_ref[...] = jnp.where(lane < 64, res[0][0], res[1][0])
        lse_ref[...] = jnp.where(lane < 64, res[0][1], res[1][1])

    out = jax.ShapeDtypeStruct((S, N_PAIR * LANE), F32)
    return pl.pallas_call(
        body, name=name, out_shape=(out, out), grid=(N_PAIR, S // tq),
        in_specs=[pl.BlockSpec((2, tq, LANE), lambda p, i: (p, i, 0)), pl.BlockSpec((2, S, LANE), lambda p, i: (p, 0, 0)),
                  pl.BlockSpec((2, S, LANE), lambda p, i: (p, 0, 0))],
        out_specs=(pl.BlockSpec((tq, LANE), lambda p, i: (i, p)), pl.BlockSpec((tq, LANE), lambda p, i: (i, p))),
        compiler_params=_params(("parallel", "parallel")),
    )(qa, ka, va)


def _attn_bwd(qa2, ka, va, kat, doa, name):
    nq = S // TQ

    def body(qa_ref, ka_ref, va_ref, kat_ref, doa_ref, dqt_ref, dk_ref, dv_ref):
        j = pl.program_id(1)

        @pl.when(j == 0)
        def _():
            dqt_ref[...] = jnp.zeros_like(dqt_ref)

        key = lax.broadcasted_iota(jnp.int32, (TQ, TQ), 0)
        qry = lax.broadcasted_iota(jnp.int32, (TQ, TQ), 1)
        kav, vav, katv = [ka_ref[0], ka_ref[1]], [va_ref[0], va_ref[1]], [kat_ref[0], kat_ref[1]]

        def block(i, carry, masked):
            off = pl.multiple_of(i * TQ, TQ)
            out = []
            for e in range(2):
                dk_acc, dv_acc = carry[e]
                qav = qa_ref[e, pl.ds(off, TQ), :]
                doav = doa_ref[e, pl.ds(off, TQ), :]
                s_t = lax.dot_general(kav[e], qav, _NT, preferred_element_type=F32)
                if masked:
                    s_t = jnp.where(key > qry, NEG_INF, s_t)
                p_t = jnp.exp(s_t)
                ds_t = p_t * lax.dot_general(vav[e], doav, _NT, preferred_element_type=F32)
                dsb = ds_t.astype(BF16)
                dv_acc = dv_acc + jnp.dot(p_t.astype(BF16), doav, preferred_element_type=F32)
                dk_acc = dk_acc + jnp.dot(dsb, qav, preferred_element_type=F32)
                dqt_ref[e, :, pl.ds(off, TQ)] += jnp.dot(katv[e], dsb, preferred_element_type=F32)
                out.append((dk_acc, dv_acc))
            return tuple(out)

        zero = (jnp.zeros((TQ, LANE), F32), jnp.zeros((TQ, LANE), F32))
        carry = block(j, (zero, zero), True)
        carry = lax.fori_loop(j + 1, nq, lambda i, c: block(i, c, False), carry)
        for e in range(2):
            dk_ref[e], dv_ref[e] = carry[e]

    full = pl.BlockSpec((2, S, LANE), lambda p, j: (p, 0, 0))
    blk = pl.BlockSpec((2, TQ, LANE), lambda p, j: (p, j, 0))
    acc = jax.ShapeDtypeStruct((N_HEAD, S, LANE), F32)
    return pl.pallas_call(
        body, name=name,
        out_shape=(jax.ShapeDtypeStruct((N_HEAD, LANE, S), F32), acc, acc),
        grid=(N_PAIR, nq),
        in_specs=[full, blk, blk, pl.BlockSpec((2, LANE, TQ), lambda p, j: (p, 0, j)), full],
        out_specs=(pl.BlockSpec((2, LANE, S), lambda p, j: (p, 0, 0)), blk, blk),
        compiler_params=_params(("arbitrary", "arbitrary")),
    )(qa2, ka, va, kat, doa)


ADA_ROWS = 16


def _ada_fwd(c_pad, w_ada, b_cols, name):
    def body(c_ref, w_ref, b_ref, o_ref):
        cv = c_ref[...]
        sc = (cv * jax.nn.sigmoid(cv)).astype(BF16)
        o_ref[0] = jnp.dot(sc, w_ref[0].astype(BF16), preferred_element_type=F32) + b_ref[0, 0:1, :]

    return pl.pallas_call(
        body, name=name, out_shape=jax.ShapeDtypeStruct((DEPTH, ADA_ROWS, ADA_COLS), F32), grid=(DEPTH,),
        in_specs=[pl.BlockSpec((ADA_ROWS, D), lambda l: (0, 0)), pl.BlockSpec((1, D, ADA_COLS), lambda l: (l, 0, 0)),
                  pl.BlockSpec((1, 8, ADA_COLS), lambda l: (l, 0, 0))],
        out_specs=pl.BlockSpec((1, ADA_ROWS, ADA_COLS), lambda l: (l, 0, 0)),
        compiler_params=_params(("parallel",)),
    )(c_pad, w_ada, b_cols)


def _ada_bwd(c_pad, dmod_cols, name):
    def body(c_ref, d_ref, o_ref):
        cv = c_ref[...]
        sc = (cv * jax.nn.sigmoid(cv)).astype(BF16)
        o_ref[0] = lax.dot_general(sc, d_ref[0].astype(BF16), _TN, preferred_element_type=F32)

    return pl.pallas_call(
        body, name=name, out_shape=jax.ShapeDtypeStruct((DEPTH, D, ADA_COLS), F32), grid=(DEPTH,),
        in_specs=[pl.BlockSpec((ADA_ROWS, D), lambda l: (0, 0)), pl.BlockSpec((1, ADA_ROWS, ADA_COLS), lambda l: (l, 0, 0))],
        out_specs=pl.BlockSpec((1, D, ADA_COLS), lambda l: (l, 0, 0)),
        compiler_params=_params(("parallel",)),
    )(c_pad, dmod_cols)


def _adamw_math(w, g, m, v):
    m = B1 * m + (1.0 - B1) * g
    v = B2 * v + (1.0 - B2) * (g * g)
    m_hat = m / (1.0 - B1 ** STEP)
    v_hat = v / (1.0 - B2 ** STEP)
    delta = -LR * (m_hat / (jnp.sqrt(v_hat) + EPS) + WD * w)
    return delta, m, v


def _row_tile(rows, target=256):
    best = 8
    for t in range(8, min(rows, target) + 1, 8):
        if rows % t == 0:
            best = t
    return best


def _adamw(w, g, m, v, name):
    layers, rows, cols = w.shape
    tr = _row_tile(rows)
    spec = pl.BlockSpec((1, tr, cols), lambda l, i: (l, i, 0))

    def body(w_ref, g_ref, m_ref, v_ref, d_ref, nm_ref, nv_ref):
        d_ref[...], nm_ref[...], nv_ref[...] = _adamw_math(w_ref[...], g_ref[...], m_ref[...], v_ref[...])

    out = jax.ShapeDtypeStruct(w.shape, F32)
    return pl.pallas_call(
        body, name=name, out_shape=(out, out, out), grid=(layers, rows // tr),
        in_specs=[spec] * 4, out_specs=(spec,) * 3, compiler_params=_params(("parallel", "parallel")),
    )(w, g, m, v)


def _sum_slabs(x, name):
    n, rows, _ = x.shape
    tr = _row_tile(rows)

    def body(x_ref, o_ref):
        acc = x_ref[0]
        for k in range(1, n):
            acc = acc + x_ref[k]
        o_ref[...] = acc

    return pl.pallas_call(
        body, name=name, out_shape=jax.ShapeDtypeStruct((rows, D), F32), grid=(rows // tr,),
        in_specs=[pl.BlockSpec((n, tr, D), lambda i: (0, i, 0))], out_specs=pl.BlockSpec((tr, D), lambda i: (i, 0)),
        compiler_params=_params(("parallel",)),
    )(x)


_ANY = pl.BlockSpec(memory_space=pl.ANY)
MESH = pl.DeviceIdType.MESH


def _all_gather(xs, name):
    n = len(xs)

    def body(*refs):
        x_refs, out_refs = refs[:n], refs[n:2 * n]
        send_sems, recv_sems, local_sems = refs[2 * n:]
        x_, y_, c_ = lax.axis_index("x"), lax.axis_index("y"), lax.axis_index("c")
        me, sibling = (x_, y_, c_), (x_, y_, 1 - c_)
        chips = [(1 - x_, y_), (x_, 1 - y_), (1 - x_, 1 - y_)]

        def slot(a, px, py, pc):
            return out_refs[a].at[4 * px + 2 * py + pc]

        def copy(a, k, block, to, src=None):
            return pltpu.make_async_remote_copy(
                src_ref=slot(a, *block) if src is None else src, dst_ref=slot(a, *block),
                send_sem=send_sems.at[7 * a + k], recv_sem=recv_sems.at[7 * a + k], device_id=to, device_id_type=MESH)

        mine = [pltpu.make_async_copy(x_refs[a], slot(a, *me), local_sems.at[a]) for a in range(n)]
        for cp in mine:
            cp.start()
        first = []
        for a in range(n):
            first.append(copy(a, 0, me, sibling, src=x_refs[a]))
            first += [copy(a, 1 + j, me, (*chip, c_), src=x_refs[a]) for j, chip in enumerate(chips)]
        for cp in first:
            cp.start()
        passed = []
        for j, chip in enumerate(chips):
            for a in range(n):
                copy(a, 1 + j, (*chip, c_), me).wait_recv()
                passed.append(copy(a, 4 + j, (*chip, c_), sibling))
                passed[-1].start()
        for a in range(n):
            copy(a, 0, sibling, me).wait_recv()
        for j, chip in enumerate(chips):
            for a in range(n):
                copy(a, 4 + j, (*chip, 1 - c_), me).wait_recv()
        for cp in first + passed:
            cp.wait_send()
        for cp in mine:
            cp.wait()

    return pl.pallas_call(
        body, name=name, out_shape=[jax.ShapeDtypeStruct((N_DEV,) + x.shape, x.dtype) for x in xs],
        in_specs=[_ANY] * n, out_specs=[_ANY] * n,
        scratch_shapes=[pltpu.SemaphoreType.DMA((7 * n,)), pltpu.SemaphoreType.DMA((7 * n,)),
                        pltpu.SemaphoreType.DMA((n,))],
    )(*xs)


def _sibling_exchange(gs, name):
    n = len(gs)

    def body(*refs):
        g_refs, p_refs = refs[:n], refs[n:2 * n]
        send_sems, recv_sems = refs[2 * n:]
        x_, y_, c_ = lax.axis_index("x"), lax.axis_index("y"), lax.axis_index("c")
        copies = [pltpu.make_async_remote_copy(
            src_ref=g_refs[a].at[2 * k + (1 - c_)], dst_ref=p_refs[a].at[k], send_sem=send_sems.at[4 * a + k],
            recv_sem=recv_sems.at[4 * a + k], device_id=(x_, y_, 1 - c_), device_id_type=MESH)
            for a in range(n) for k in range(4)]
        for cp in copies:
            cp.start()
        for cp in copies:
            cp.wait()

    return pl.pallas_call(
        body, name=name, out_shape=[jax.ShapeDtypeStruct((4,) + g.shape[1:], g.dtype) for g in gs],
        in_specs=[_ANY] * n, out_specs=[_ANY] * n,
        scratch_shapes=[pltpu.SemaphoreType.DMA((4 * n,)), pltpu.SemaphoreType.DMA((4 * n,))],
    )(*gs)


def _slab_tiles(rows, cols):
    if rows % 8 == 0:
        return _row_tile(rows), cols
    return rows, 2 * LANE


def _pair_sums(g, p, route, name):
    _, rows, cols = g.shape
    tr, tc = _slab_tiles(rows, cols)

    def body(route_ref, g_ref, p_ref, t_ref):
        t_ref[...] = (g_ref[...] + p_ref[...]).astype(BF16)

    return pl.pallas_call(
        body, name=name, out_shape=jax.ShapeDtypeStruct((3, rows, cols), BF16),
        grid_spec=pltpu.PrefetchScalarGridSpec(
            num_scalar_prefetch=1, grid=(3, rows // tr, cols // tc),
            in_specs=[pl.BlockSpec((1, tr, tc), lambda r, i, j, route_ref: (2 * route_ref[1 + r] + route_ref[0], i, j)),
                      pl.BlockSpec((1, tr, tc), lambda r, i, j, route_ref: (route_ref[1 + r], i, j))],
            out_specs=pl.BlockSpec((1, tr, tc), lambda r, i, j, route_ref: (r, i, j))),
        compiler_params=_params(("parallel", "parallel", "parallel")),
    )(route, g, p)


def _chip_exchange(ts, name):
    n = len(ts)

    def body(*refs):
        t_refs, l_refs = refs[:n], refs[n:2 * n]
        send_sems, recv_sems = refs[2 * n:]
        x_, y_, c_ = lax.axis_index("x"), lax.axis_index("y"), lax.axis_index("c")
        chips = [(1 - x_, y_), (x_, 1 - y_), (1 - x_, 1 - y_)]
        copies = [pltpu.make_async_remote_copy(
            src_ref=t_refs[a].at[r], dst_ref=l_refs[a].at[r], send_sem=send_sems.at[3 * a + r],
            recv_sem=recv_sems.at[3 * a + r], device_id=(px, py, c_), device_id_type=MESH)
            for a in range(n) for r, (px, py) in enumerate(chips)]
        for cp in copies:
            cp.start()
        for cp in copies:
            cp.wait()

    return pl.pallas_call(
        body, name=name, out_shape=[jax.ShapeDtypeStruct((3,) + t.shape[1:], t.dtype) for t in ts],
        in_specs=[_ANY] * n, out_specs=[_ANY] * n,
        scratch_shapes=[pltpu.SemaphoreType.DMA((3 * n,)), pltpu.SemaphoreType.DMA((3 * n,))],
    )(*ts)


def _reduce_adamw(gs, ps, landed, place, w, m, v, name):
    layers, rows, cols = w.shape
    assert layers == DEPTH == 2
    tr, tc = _slab_tiles(rows, cols)
    nr, nc = rows // tr, cols // tc
    spec = pl.BlockSpec((1, tr, tc), lambda l, i, j, place_ref: (l, i, j))

    def own(layer, which):
        pi, pj = (nr - 1, nc - 1) if layer == 0 else (0, 0)

        def index(l, i, j, place_ref):
            lead = 0 if which is None else place_ref[which]
            return lead, jnp.where(l == layer, i, pi), jnp.where(l == layer, j, pj)

        return pl.BlockSpec((3 if which is None else 1, tr, tc), index)

    def body(place_ref, g0_ref, p0_ref, l0_ref, g1_ref, p1_ref, l1_ref, w_ref, m_ref, v_ref,
             g_ref, d_ref, nm_ref, nv_ref):
        def update(own_ref, sib_ref, l_ref):
            g = own_ref[0] + sib_ref[0] + l_ref[0].astype(F32) + l_ref[1].astype(F32) + l_ref[2].astype(F32)
            g_ref[0] = g
            d_ref[0], nm_ref[0], nv_ref[0] = _adamw_math(w_ref[0], g, m_ref[0], v_ref[0])

        @pl.when(pl.program_id(0) == 0)
        def _():
            update(g0_ref, p0_ref, l0_ref)

        @pl.when(pl.program_id(0) == 1)
        def _():
            update(g1_ref, p1_ref, l1_ref)

    out = jax.ShapeDtypeStruct(w.shape, F32)
    return pl.pallas_call(
        body, name=name, out_shape=(out, out, out, out),
        grid_spec=pltpu.PrefetchScalarGridSpec(
            num_scalar_prefetch=1, grid=(DEPTH, nr, nc),
            in_specs=[own(0, 0), own(0, 1), own(0, None), own(1, 0), own(1, 1), own(1, None), spec, spec, spec],
            out_specs=(spec, spec, spec, spec)),
        compiler_params=_params(("arbitrary", "arbitrary", "arbitrary")),
    )(place, gs[0], ps[0], landed[0], gs[1], ps[1], landed[1], w, m, v)


def _pack(pieces, row_multiple, dtype, cols=D, rows=None):
    flat = jnp.concatenate([p.astype(dtype).reshape(-1) for p in pieces])
    if rows is None:
        rows = -(-flat.shape[0] // cols)
        rows = -(-rows // row_multiple) * row_multiple
    flat = jnp.pad(flat, (0, rows * cols - flat.shape[0]))
    return flat.reshape(rows, cols)


def _unpack(flat, shapes, lead=()):
    out, off = [], 0
    for shp in shapes:
        n = 1
        for s_ in shp:
            n *= s_
        out.append(lax.slice_in_dim(flat, off, off + n, axis=len(lead)).reshape(lead + tuple(shp)))
        off += n
    return out


def _z_rows_from_in(wt):
    pad = jnp.zeros((NZ - IN_COLS, wt.shape[1]), wt.dtype)
    return jnp.concatenate([wt[1544:2568], wt[2568:5640], wt[0:1536], wt[1536:1544], pad], axis=0)


def _in_rows_from_z(wt):
    return jnp.concatenate([wt[Z_Q:Z_Q + 1536], wt[Z_F:Z_F + 8], wt[Z_PC:Z_PC + 1024], wt[Z_G:Z_G + 3072]], axis=0)


def _pad_rows(v, rows=8):
    return jnp.pad(v, ((0, rows - v.shape[0]), (0, 0)))


def _layer_fwd(l, x, wts, gvec, mod):
    tag = f"l{l}"
    h = _prenorm_fwd(x, gvec, mod, 0, 0, 1, f"prenorm_mix_{tag}")
    z = _matmul(h, wts["w_in_t"], "nt", f"in_proj_{tag}", tn=1152)
    qa, ka, va, kat = _attn_prep(z, wts["b_f"], f"attn_prep_{tag}")
    o, lse = _attn_fwd(qa, ka, va, f"attn_{tag}")
    br_b = _pool_fwd(z, wts["wp_bd"], wts["pool_scale"], f"pool_{tag}")
    br_c = _conv_fwd(z, wts["conv_w"], f"conv_{tag}")
    pa = _matmul(o, wts["wa"], "nn", f"proj_a_{tag}")
    pb = _matmul(br_b, wts["wb"], "nn", f"proj_b_{tag}")
    pc = _matmul(br_c, wts["wc"], "nn", f"proj_c_{tag}")
    merged = _merge_fwd(z, pa, pb, pc, f"merge_{tag}")
    y = _matmul(merged, wts["w_out"], "nn", f"out_proj_{tag}")
    x1 = _postnorm_fwd(x, y, gvec, mod, 1, 2, f"postnorm_mix_{tag}")
    h2 = _prenorm_fwd(x1, gvec, mod, 2, 3, 4, f"prenorm_ff_{tag}")
    a = _matmul(h2, wts["w_ff1"], "nn", f"ff1_{tag}", b_col_shards=True)
    r = _relu2_fwd(a, f"relu2_{tag}")
    y2 = _matmul(r, wts["w_ff2"], "nn", f"ff2_{tag}")
    x2 = _postnorm_fwd(x1, y2, gvec, mod, 3, 5, f"postnorm_ff_{tag}")
    saved = dict(x=x, h=h, z=z, qa=qa, ka=ka, va=va, kat=kat, o=o, lse=lse, br_b=br_b, br_c=br_c, pa=pa, pb=pb, pc=pc,
                 merged=merged, y=y, x1=x1, h2=h2, a=a, r=r, y2=y2)
    return x2, saved


def _layer_bwd(l, dx2, sv, wts, gvec, mod):
    tag = f"l{l}"
    dy2, red_post_ff = _postnorm_bwd(sv["y2"], gvec, mod, dx2, 3, 5, f"postnorm_ff_bwd_{tag}")
    dr = _matmul(dy2, wts["w_ff2"], "nt", f"ff2_dx_{tag}")
    d_w_ff2 = _matmul(sv["r"], dy2, "tn", f"ff2_dw_{tag}")
    da = _relu2_bwd(sv["a"], dr, f"relu2_bwd_{tag}")
    dh2 = _matmul(da, wts["w_ff1"], "nt", f"ff1_dx_{tag}", b_col_shards=True)
    d_w_ff1 = _matmul(sv["h2"], da, "tn", f"ff1_dw_{tag}", out_col_shards=True)
    dx1, red_pre_ff = _prenorm_bwd(sv["x1"], gvec, mod, dh2, dx2, 2, 4, f"prenorm_ff_bwd_{tag}")

    dy, red_post_mix = _postnorm_bwd(sv["y"], gvec, mod, dx1, 1, 2, f"postnorm_mix_bwd_{tag}")
    dmerged = _matmul(dy, wts["w_out"], "nt", f"out_proj_dx_{tag}")
    d_w_out = _matmul(sv["merged"], dy, "tn", f"out_proj_dw_{tag}")
    dpa, dpb, dpc, dgl = _merge_bwd(sv["z"], sv["pa"], sv["pb"], sv["pc"], dmerged, f"merge_bwd_{tag}")
    do = _matmul(dpa, wts["wa"], "nt", f"proj_a_dx_{tag}")
    dbr_b = _matmul(dpb, wts["wb"], "nt", f"proj_b_dx_{tag}")
    dbr_c = _matmul(dpc, wts["wc"], "nt", f"proj_c_dx_{tag}")
    d_wa = _matmul(sv["o"], dpa, "tn", f"proj_a_dw_{tag}")
    d_wb = _matmul(sv["br_b"], dpb, "tn", f"proj_b_dw_{tag}")
    d_wc = _matmul(sv["br_c"], dpc, "tn", f"proj_c_dw_{tag}")
    d_w_branch = jnp.concatenate([d_wa, d_wb, d_wc], axis=0)

    dpu, d_wp_bd, red_pool = _pool_bwd(sv["z"], wts["wp_bd"], wts["pool_scale"], dbr_b, f"pool_bwd_{tag}")
    dconv, red_conv = _conv_bwd(sv["z"], wts["conv_w"], dbr_c, f"conv_bwd_{tag}")
    qa2, doa = _attn_bwd_prep(sv["qa"], sv["o"], sv["lse"], do, f"attn_bwd_prep_{tag}")
    dqt, dka, dva = _attn_bwd(qa2, sv["ka"], sv["va"], sv["kat"], doa, f"attn_bwd_{tag}")
    dq, dk, dv, dfl, red_f = _attn_bwd_post(sv["z"], wts["b_f"], dqt, dka, dva, f"attn_bwd_post_{tag}")
    dz = jnp.concatenate([dpu, dconv, dgl, dq, dk, dv, dfl], axis=1)
    dh = _matmul(dz, wts["w_in_t"], "nn", f"in_proj_dx_{tag}", tk=1152)
    d_w_in_t = _matmul(dz, sv["h"], "tn", f"in_proj_dw_{tag}", tm=1152)
    dx0, red_pre_mix = _prenorm_bwd(sv["x"], gvec, mod, dh, dx1, 0, 1, f"prenorm_mix_bwd_{tag}")

    rows = D // N_DEV
    big = [_in_rows_from_z(d_w_in_t).reshape(N_DEV, IN_SHARD, D), d_w_branch.reshape(N_DEV, rows, D),
           d_w_out.reshape(N_DEV, rows, D), d_w_ff1, d_w_ff2.reshape(N_DEV, D_FF // N_DEV, D)]
    d_w_pool = jnp.stack([d_wp_bd[64 * g:64 * (g + 1), 64 * g:64 * (g + 1)] for g in range(4)])
    small = dict(
        mod=jnp.stack([red_pre_mix[0], red_pre_mix[1], red_post_mix[0], red_pre_ff[0], red_pre_ff[1], red_post_ff[0]]),
        g_mix_pre=red_pre_mix[2], g_mix_post=red_post_mix[1], g_ff_pre=red_pre_ff[2], g_ff_post=red_post_ff[1],
        b_f=red_f[0, 0:8], w_pool=d_w_pool, pool_scale=red_pool[0], conv_w=red_conv[0:3])
    return dx0, big, small


SMALL_KEYS = ["mod", "g_mix_pre", "g_mix_post", "g_ff_pre", "g_ff_post", "b_f", "w_pool", "pool_scale", "conv_w"]
SMALL_SHAPES = [(DEPTH, 6 * D), (DEPTH, D), (DEPTH, D), (DEPTH, D), (DEPTH, D), (DEPTH, 8), (DEPTH, 4, 64, 64),
                (DEPTH, POOL_W), (DEPTH, 3, CONV_W)]


def kernel(x, c, w_ada, b_ada, g_mix_pre, g_mix_post, g_ff_pre, g_ff_post, w_in, b_f, w_pool, pool_scale, conv_w, w_branch, w_out, w_ff1, w_ff2, loss_target, m_w_ada, m_b_ada, m_g_mix_pre, m_g_mix_post, m_g_ff_pre, m_g_ff_post, m_w_in, m_b_f, m_w_pool, m_pool_scale, m_conv_w, m_w_branch, m_w_out, m_w_ff1, m_w_ff2, v_w_ada, v_b_ada, v_g_mix_pre, v_g_mix_post, v_g_ff_pre, v_g_ff_post, v_w_in, v_b_f, v_w_pool, v_pool_scale, v_conv_w, v_w_branch, v_w_out, v_w_ff1, v_w_ff2):
    ix, iy, ic = lax.axis_index("x"), lax.axis_index("y"), lax.axis_index("c")
    me = 4 * ix + 2 * iy + ic
    route = jnp.stack([ic, 2 * (1 - ix) + iy, 2 * ix + (1 - iy), 2 * (1 - ix) + (1 - iy)]).astype(jnp.int32)
    place = jnp.stack([me, 2 * ix + iy]).astype(jnp.int32)
    wt_in, mt_in, vt_in = (jnp.transpose(a, (0, 2, 1)) for a in (w_in, m_w_in, v_w_in))

    c_all = _all_gather([_pad_rows(c)], "gather_c")[0][:, 0, :]
    c_pad = _pad_rows(c_all, ADA_ROWS)
    b_cols = lax.dynamic_slice_in_dim(b_ada, me * ADA_COLS, ADA_COLS, axis=1)
    b_cols = jnp.broadcast_to(b_cols[:, None, :], (DEPTH, 8, ADA_COLS))
    mod_part = _ada_fwd(c_pad, w_ada, b_cols, "ada_fwd")
    mod_all = _all_gather([mod_part.reshape(DEPTH * ADA_ROWS, ADA_COLS)], "gather_mod")[0]
    mod_all = mod_all.reshape(N_DEV, DEPTH, ADA_ROWS, ADA_COLS)
    mod_mine = lax.dynamic_index_in_dim(mod_all, me, axis=2, keepdims=False)
    mod_mine = jnp.transpose(mod_mine, (1, 0, 2)).reshape(DEPTH, 6, D)

    send = [w[l].astype(BF16) for l in range(DEPTH) for w in (wt_in, w_branch, w_out, w_ff1, w_ff2)]
    cw_cols = CONV_W // N_DEV
    cw_send = jnp.pad(conv_w.reshape(DEPTH * 3, cw_cols), ((0, 8 - DEPTH * 3), (0, LANE - cw_cols)))
    gathered = _all_gather(send + [cw_send], "gather_weights")
    cw_all = gathered[-1][:, :DEPTH * 3, :cw_cols].reshape(N_DEV, DEPTH, 3, cw_cols)

    layers = []
    for l in range(DEPTH):
        p_in, p_br, p_out, p_ff1, p_ff2 = gathered[5 * l:5 * l + 5]
        w_br_full = p_br.reshape(D, D)
        cw_full = jnp.transpose(cw_all[:, l], (1, 0, 2)).reshape(3, CONV_W)
        wp_bd = jnp.zeros((POOL_W, POOL_W), F32)
        for g in range(4):
            wp_bd = wp_bd.at[64 * g:64 * (g + 1), 64 * g:64 * (g + 1)].set(w_pool[l, g])
        wts = dict(
            w_in_t=_z_rows_from_in(p_in.reshape(IN_COLS, D)), wa=w_br_full[0:A_WIDTH], wb=w_br_full[A_WIDTH:A_WIDTH + POOL_W],
            wc=w_br_full[A_WIDTH + POOL_W:], w_out=p_out.reshape(D, D),
            w_ff1=p_ff1, w_ff2=p_ff2.reshape(D_FF, D),
            conv_w=_pad_rows(cw_full), wp_bd=wp_bd.astype(BF16), pool_scale=_pad_rows(pool_scale[l][None, :]),
            b_f=_pad_rows(jnp.pad(b_f[l], (0, LANE - 8))[None, :]))
        gvec = _pad_rows(jnp.stack([g_mix_pre[l], g_mix_post[l], g_ff_pre[l], g_ff_post[l]]))
        layers.append((wts, gvec, _pad_rows(mod_mine[l])))

    xs = x[0]
    saved = []
    for l in range(DEPTH):
        xs, sv = _layer_fwd(l, xs, *layers[l])
        saved.append(sv)
    dx, loss_part = _loss_head(xs, loss_target[0], "loss_head")
    loss = lax.psum(loss_part[0, 0], ("x", "y", "c"))
    small_grads, mine, sibs, landed = [None] * DEPTH, [None] * DEPTH, [None] * DEPTH, [None] * DEPTH
    for l in reversed(range(DEPTH)):
        dx, mine[l], small_grads[l] = _layer_bwd(l, dx, saved[l], *layers[l])
        sibs[l] = _sibling_exchange(mine[l], f"rs_sibling_l{l}")
        sends = [_pair_sums(g, p, route, f"rs_pair_sums_{k}_l{l}") for k, (g, p) in enumerate(zip(mine[l], sibs[l]))]
        landed[l] = _chip_exchange(sends, f"rs_chips_l{l}")
    grad_x = dx[None]

    big_w = [wt_in, w_branch, w_out, w_ff1, w_ff2]
    big_m = [mt_in, m_w_branch, m_w_out, m_w_ff1, m_w_ff2]
    big_v = [vt_in, v_w_branch, v_w_out, v_w_ff1, v_w_ff2]
    big_out = [[], [], [], []]
    for k in range(5):
        res = _reduce_adamw([mine[l][k] for l in range(DEPTH)], [sibs[l][k] for l in range(DEPTH)],
                            [landed[l][k] for l in range(DEPTH)], place, big_w[k], big_m[k], big_v[k],
                            f"rs_sum_adamw_{k}")
        for which in range(4):
            big_out[which].append(jnp.transpose(res[which], (0, 2, 1)) if k == 0 else res[which])

    small = {k: jnp.stack([small_grads[l][k] for l in range(DEPTH)]) for k in SMALL_KEYS}
    small_all = _all_gather([_pack([small[k] for k in SMALL_KEYS], 8, F32)], "gather_small")[0]
    dmod_all = small_all[:, 0:DEPTH * 6, :].reshape(N_DEV, DEPTH, 6 * D)
    summed = _unpack(_sum_slabs(small_all, "sum_small").reshape(-1), SMALL_SHAPES)
    sg = dict(zip(SMALL_KEYS, summed))
    dmod_cols = lax.dynamic_slice_in_dim(dmod_all, me * ADA_COLS, ADA_COLS, axis=2)
    dmod_cols = jnp.pad(jnp.transpose(dmod_cols, (1, 0, 2)), ((0, 0), (0, ADA_ROWS - N_DEV), (0, 0)))
    g_w_ada = _ada_bwd(c_pad, dmod_cols, "ada_bwd")
    g_conv_w = lax.dynamic_slice_in_dim(sg["conv_w"], me * (CONV_W // N_DEV), CONV_W // N_DEV, axis=2)

    ada_out = [g_w_ada] + list(_adamw(w_ada, g_w_ada, m_w_ada, v_w_ada, "adamw_ada"))
    rest_w = [b_ada, g_mix_pre, g_mix_post, g_ff_pre, g_ff_post, b_f, w_pool, pool_scale, conv_w]
    rest_m = [m_b_ada, m_g_mix_pre, m_g_mix_post, m_g_ff_pre, m_g_ff_post, m_b_f, m_w_pool, m_pool_scale, m_conv_w]
    rest_v = [v_b_ada, v_g_mix_pre, v_g_mix_post, v_g_ff_pre, v_g_ff_post, v_b_f, v_w_pool, v_pool_scale, v_conv_w]
    rest_g = [sg["mod"], sg["g_mix_pre"], sg["g_mix_post"], sg["g_ff_pre"], sg["g_ff_post"], sg["b_f"],
              sg["w_pool"], sg["pool_scale"], g_conv_w]
    rest_shapes = [a.shape for a in rest_w]
    upd = _adamw(_pack(rest_w, 8, F32)[None], _pack(rest_g, 8, F32)[None], _pack(rest_m, 8, F32)[None],
                 _pack(rest_v, 8, F32)[None], "adamw_rest")
    rest_out = [rest_g] + [_unpack(arr.reshape(-1), rest_shapes) for arr in upd]
    rest_out = [[ada_out[which]] + rest_out[which] for which in range(4)]

    def ordered(k):
        r, b = rest_out[k], big_out[k]
        return [r[0], r[1], r[2], r[3], r[4], r[5], b[0], r[6], r[7], r[8], r[9], b[1], b[2], b[3], b[4]]

    return (loss, grad_x, *ordered(0), *ordered(1), *ordered(2), *ordered(3))
```

```python
import functools

import jax
import jax.numpy as jnp
from jax import lax
from jax.experimental import pallas as pl
from jax.experimental.pallas import tpu as pltpu
from jax.experimental.pallas import tpu_sc as plsc

F32 = jnp.float32
BF16 = jnp.bfloat16

N_DEV = 8
D = 1024
S = 2048
DEPTH = 2
D_FF = 4 * D
A_WIDTH = 512
HEAD_DIM = 64
N_PAIR = 4
POOL_W = 256
CONV_W = 256
IN_COLS = 5640
ADA_COLS = 6 * D // N_DEV
IN_SHARD = IN_COLS // N_DEV
RMS_EPS = 1e-6
NEG_INF = -1e30
ATT_SCALE = HEAD_DIM ** -0.5

NZ = 5760
Z_PC = 0
Z_G = 1024
Z_Q = 4096
Z_K = 4608
Z_V = 5120
Z_F = 5632

LR, B1, B2, EPS, WD, STEP = 0.001, 0.9, 0.999, 1e-08, 0.01, 10

LANE = 128
VMEM_LIMIT_BYTES = 48 * 1024 * 1024
TS = 256
TQ = 256
TQ_FWD = 512


def _params(sem=None):
    return pltpu.CompilerParams(dimension_semantics=sem, vmem_limit_bytes=VMEM_LIMIT_BYTES)


def _pick(n, target):
    best = None
    for t in range(LANE, min(n, target) + 1, LANE):
        if n % t == 0:
            best = t
    return n if best is None else best


def _matmul(a, b, mode, name, out_dtype=F32, tm=1024, tn=1024, tk=1024, b_col_shards=False, out_col_shards=False):
    if b_col_shards:
        shards, b_rows, shard_cols = b.shape
        b_shape = (b_rows, shards * shard_cols)
    else:
        b_shape = b.shape
    if mode == "nn":
        (m, k), (k2, n) = a.shape, b_shape
    elif mode == "nt":
        (m, k), (n, k2) = a.shape, b_shape
    else:
        (k, m), (k2, n) = a.shape, b_shape
    assert k == k2, (a.shape, b.shape, mode)
    tm, tn, tk = _pick(m, tm), _pick(n, tn), _pick(k, tk)
    if b_col_shards and mode == "nn":
        tn = shard_cols
    if b_col_shards and mode == "nt":
        tk = shard_cols
    if out_col_shards:
        tn = n // N_DEV
    nk = k // tk
    if mode == "nn":
        a_spec = pl.BlockSpec((tm, tk), lambda i, j, kk: (i, kk))
        b_spec = (pl.BlockSpec((None, tk, tn), lambda i, j, kk: (j, kk, 0)) if b_col_shards else
                  pl.BlockSpec((tk, tn), lambda i, j, kk: (kk, j)))
        dims = (((1,), (0,)), ((), ()))
    elif mode == "nt":
        a_spec = pl.BlockSpec((tm, tk), lambda i, j, kk: (i, kk))
        b_spec = (pl.BlockSpec((None, tn, tk), lambda i, j, kk: (kk, j, 0)) if b_col_shards else
                  pl.BlockSpec((tn, tk), lambda i, j, kk: (j, kk)))
        dims = (((1,), (1,)), ((), ()))
    else:
        assert not b_col_shards
        a_spec = pl.BlockSpec((tk, tm), lambda i, j, kk: (kk, i))
        b_spec = pl.BlockSpec((tk, tn), lambda i, j, kk: (kk, j))
        dims = (((0,), (0,)), ((), ()))
    if out_col_shards:
        out_shape = jax.ShapeDtypeStruct((N_DEV, m, tn), out_dtype)
        out_spec = pl.BlockSpec((None, tm, tn), lambda i, j, kk: (j, i, 0))
    else:
        out_shape = jax.ShapeDtypeStruct((m, n), out_dtype)
        out_spec = pl.BlockSpec((tm, tn), lambda i, j, kk: (i, j))

    def product(a_ref, b_ref):
        return lax.dot_general(a_ref[...].astype(BF16), b_ref[...].astype(BF16), dims, preferred_element_type=F32)

    def body_one_pass(a_ref, b_ref, o_ref):
        o_ref[...] = product(a_ref, b_ref).astype(out_dtype)

    def body(a_ref, b_ref, o_ref, acc_ref):
        kk = pl.program_id(2)

        @pl.when(kk == 0)
        def _():
            acc_ref[...] = product(a_ref, b_ref)

        @pl.when(kk > 0)
        def _():
            acc_ref[...] += product(a_ref, b_ref)

        @pl.when(kk == nk - 1)
        def _():
            o_ref[...] = acc_ref[...].astype(out_dtype)

    return pl.pallas_call(
        body_one_pass if nk == 1 else body, name=name,
        out_shape=out_shape,
        grid=(m // tm, n // tn, nk),
        in_specs=[a_spec, b_spec],
        out_specs=out_spec,
        scratch_shapes=[] if nk == 1 else [pltpu.VMEM((tm, tn), F32)],
        compiler_params=_params(("parallel", "parallel", "arbitrary")),
    )(a, b)


def _row_spec(width=D, col=0):
    return pl.BlockSpec((TS, width), lambda i: (i, col))


def _vec_spec(rows=8, width=D):
    return pl.BlockSpec((rows, width), lambda i: (0, 0))


def _rms(x):
    return lax.rsqrt(jnp.mean(x * x, axis=-1, keepdims=True) + RMS_EPS)


def _prenorm_fwd(x, gvec, mod, g_row, shift_row, scale_row, name):
    def body(x_ref, g_ref, mod_ref, h_ref):
        xv = x_ref[...]
        y = xv * _rms(xv) * g_ref[g_row:g_row + 1, :]
        h = y * (1.0 + mod_ref[scale_row:scale_row + 1, :]) + mod_ref[shift_row:shift_row + 1, :]
        h_ref[...] = h.astype(BF16)

    return pl.pallas_call(
        body, name=name, out_shape=jax.ShapeDtypeStruct((S, D), BF16), grid=(S // TS,),
        in_specs=[_row_spec(), _vec_spec(), _vec_spec()], out_specs=_row_spec(),
        compiler_params=_params(("parallel",)),
    )(x, gvec, mod)


def _prenorm_bwd(x, gvec, mod, dh, dres, g_row, scale_row, name):
    def body(x_ref, g_ref, mod_ref, dh_ref, dres_ref, dx_ref, red_ref):
        i = pl.program_id(0)

        @pl.when(i == 0)
        def _():
            red_ref[...] = jnp.zeros_like(red_ref)

        xv = x_ref[...]
        g = g_ref[g_row:g_row + 1, :]
        r = _rms(xv)
        n = xv * r
        yg = n * g
        dhv = dh_ref[...]
        dyg = dhv * (1.0 + mod_ref[scale_row:scale_row + 1, :])
        dn = dyg * g
        dx = r * (dn - n * jnp.mean(dn * n, axis=-1, keepdims=True))
        dx_ref[...] = dres_ref[...] + dx
        red_ref[0:1, :] += jnp.sum(dhv, axis=0, keepdims=True)
        red_ref[1:2, :] += jnp.sum(dhv * yg, axis=0, keepdims=True)
        red_ref[2:3, :] += jnp.sum(dyg * n, axis=0, keepdims=True)

    return pl.pallas_call(
        body, name=name,
        out_shape=(jax.ShapeDtypeStruct((S, D), F32), jax.ShapeDtypeStruct((8, D), F32)),
        grid=(S // TS,),
        in_specs=[_row_spec(), _vec_spec(), _vec_spec(), _row_spec(), _row_spec()],
        out_specs=(_row_spec(), _vec_spec()),
        compiler_params=_params(("arbitrary",)),
    )(x, gvec, mod, dh, dres)


def _postnorm_fwd(x, y, gvec, mod, g_row, gate_row, name):
    def body(x_ref, y_ref, g_ref, mod_ref, o_ref):
        yv = y_ref[...]
        yn = yv * _rms(yv) * g_ref[g_row:g_row + 1, :]
        o_ref[...] = x_ref[...] + mod_ref[gate_row:gate_row + 1, :] * yn

    return pl.pallas_call(
        body, name=name, out_shape=jax.ShapeDtypeStruct((S, D), F32), grid=(S // TS,),
        in_specs=[_row_spec(), _row_spec(), _vec_spec(), _vec_spec()], out_specs=_row_spec(),
        compiler_params=_params(("parallel",)),
    )(x, y, gvec, mod)


def _postnorm_bwd(y, gvec, mod, dxo, g_row, gate_row, name):
    def body(y_ref, g_ref, mod_ref, dxo_ref, dy_ref, red_ref):
        i = pl.program_id(0)

        @pl.when(i == 0)
        def _():
            red_ref[...] = jnp.zeros_like(red_ref)

        yv = y_ref[...]
        g = g_ref[g_row:g_row + 1, :]
        r = _rms(yv)
        n = yv * r
        dxo = dxo_ref[...]
        dyn = dxo * mod_ref[gate_row:gate_row + 1, :]
        dn = dyn * g
        dy = r * (dn - n * jnp.mean(dn * n, axis=-1, keepdims=True))
        dy_ref[...] = dy.astype(BF16)
        red_ref[0:1, :] += jnp.sum(dxo * (n * g), axis=0, keepdims=True)
        red_ref[1:2, :] += jnp.sum(dyn * n, axis=0, keepdims=True)

    return pl.pallas_call(
        body, name=name,
        out_shape=(jax.ShapeDtypeStruct((S, D), BF16), jax.ShapeDtypeStruct((8, D), F32)),
        grid=(S // TS,),
        in_specs=[_row_spec(), _vec_spec(), _vec_spec(), _row_spec()],
        out_specs=(_row_spec(), _vec_spec()),
        compiler_params=_params(("arbitrary",)),
    )(y, gvec, mod, dxo)


def _loss_head(xf, target, name):
    def body(x_ref, t_ref, dx_ref, loss_ref):
        i = pl.program_id(0)

        @pl.when(i == 0)
        def _():
            loss_ref[...] = jnp.zeros_like(loss_ref)

        e = x_ref[...] - t_ref[...]
        dx_ref[...] = e / float(D)
        per_tok = jnp.mean(e * e, axis=-1, keepdims=True)
        loss_ref[0:1, 0:1] += 0.5 * jnp.sum(per_tok, axis=0, keepdims=True)

    return pl.pallas_call(
        body, name=name,
        out_shape=(jax.ShapeDtypeStruct((S, D), F32), jax.ShapeDtypeStruct((8, LANE), F32)),
        grid=(S // TS,),
        in_specs=[_row_spec(), _row_spec()],
        out_specs=(_row_spec(), pl.BlockSpec((8, LANE), lambda i: (0, 0))),
        compiler_params=_params(("arbitrary",)),
    )(xf, target)


def _relu2_fwd(a, name):
    def body(a_ref, r_ref):
        t = jnp.maximum(a_ref[...], 0.0)
        r_ref[...] = (t * t).astype(BF16)

    return pl.pallas_call(
        body, name=name, out_shape=jax.ShapeDtypeStruct((S, D_FF), BF16), grid=(S // TS,),
        in_specs=[_row_spec(D_FF)], out_specs=_row_spec(D_FF),
        compiler_params=_params(("parallel",)),
    )(a)


def _relu2_bwd(a, dr, name):
    def body(a_ref, dr_ref, da_ref):
        da_ref[...] = (dr_ref[...] * (2.0 * jnp.maximum(a_ref[...], 0.0))).astype(BF16)

    return pl.pallas_call(
        body, name=name, out_shape=jax.ShapeDtypeStruct((S, D_FF), BF16), grid=(S // TS,),
        in_specs=[_row_spec(D_FF), _row_spec(D_FF)], out_specs=_row_spec(D_FF),
        compiler_params=_params(("parallel",)),
    )(a, dr)


def _merge_fwd(z, pa, pb, pc, name):
    def body(g0_ref, g1_ref, g2_ref, pa_ref, pb_ref, pc_ref, o_ref):
        m = (jax.nn.sigmoid(g0_ref[...]) * pa_ref[...] + jax.nn.sigmoid(g1_ref[...]) * pb_ref[...]
             + jax.nn.sigmoid(g2_ref[...]) * pc_ref[...])
        o_ref[...] = m.astype(BF16)

    gb = Z_G // D
    return pl.pallas_call(
        body, name=name, out_shape=jax.ShapeDtypeStruct((S, D), BF16), grid=(S // TS,),
        in_specs=[_row_spec(D, gb), _row_spec(D, gb + 1), _row_spec(D, gb + 2), _row_spec(), _row_spec(), _row_spec()],
        out_specs=_row_spec(),
        compiler_params=_params(("parallel",)),
    )(z, z, z, pa, pb, pc)


def _merge_bwd(z, pa, pb, pc, dm, name):
    def body(g0_ref, g1_ref, g2_ref, pa_ref, pb_ref, pc_ref, dm_ref, da_ref, db_ref, dc_ref, dgl_ref):
        dmv = dm_ref[...]
        for k, (g_ref, p_ref, d_ref) in enumerate(((g0_ref, pa_ref, da_ref), (g1_ref, pb_ref, db_ref),
                                                   (g2_ref, pc_ref, dc_ref))):
            sg = jax.nn.sigmoid(g_ref[...])
            d_ref[...] = (dmv * sg).astype(BF16)
            dgl_ref[:, k * D:(k + 1) * D] = (dmv * p_ref[...] * (sg * (1.0 - sg))).astype(BF16)

    gb = Z_G // D
    proj = jax.ShapeDtypeStruct((S, D), BF16)
    return pl.pallas_call(
        body, name=name,
        out_shape=(proj, proj, proj, jax.ShapeDtypeStruct((S, 3 * D), BF16)),
        grid=(S // TS,),
        in_specs=[_row_spec(D, gb), _row_spec(D, gb + 1), _row_spec(D, gb + 2), _row_spec(), _row_spec(), _row_spec(),
                  _row_spec()],
        out_specs=(_row_spec(), _row_spec(), _row_spec(), _row_spec(3 * D)),
        compiler_params=_params(("parallel",)),
    )(z, z, z, pa, pb, pc, dm)


def _shift_down(x, k, row):
    return jnp.where(row >= k, pltpu.roll(x, k, axis=0), 0.0)


def _shift_up(x, k, row):
    n = x.shape[0]
    return jnp.where(row < n - k, pltpu.roll(x, n - k, axis=0), 0.0)


def _cumsum_rows(x, row, reverse=False):
    shift = _shift_up if reverse else _shift_down
    k = 1
    while k < x.shape[0]:
        x = x + shift(x, k, row)
        k *= 2
    return x


def _full_spec(shape, idx=(0, 0)):
    return pl.BlockSpec(shape, lambda i: idx)


def _pool_window_select(lane, a2, a4, a8, a16):
    return jnp.where(lane < 64, a2, jnp.where(lane < 128, a4, jnp.where(lane < 192, a8, a16)))


def _pool_p(u, row, lane):
    t2 = u + _shift_down(u, 1, row)
    t4 = t2 + _shift_down(t2, 2, row)
    t8 = t4 + _shift_down(t4, 4, row)
    t16 = t8 + _shift_down(t8, 8, row)
    tw = _pool_window_select(lane, t2, t4, t8, t16)
    cnt = jnp.minimum((row + 1).astype(F32), _pool_window_select(lane, 2.0, 4.0, 8.0, 16.0))
    return tw / cnt - u, cnt


def _pool_fwd(z, wp_bd, pscale, name):
    def body(u_ref, w_ref, s_ref, o_ref):
        row = lax.broadcasted_iota(jnp.int32, (S, POOL_W), 0)
        lane = lax.broadcasted_iota(jnp.int32, (S, POOL_W), 1)
        p, _ = _pool_p(u_ref[...], row, lane)
        y = jnp.dot(p.astype(BF16), w_ref[...], preferred_element_type=F32)
        o_ref[...] = y * s_ref[0:1, :]

    return pl.pallas_call(
        body, name=name, out_shape=jax.ShapeDtypeStruct((S, POOL_W), F32), grid=(1,),
        in_specs=[_full_spec((S, POOL_W), (0, Z_PC // POOL_W)), _full_spec((POOL_W, POOL_W)), _full_spec((8, POOL_W))],
        out_specs=_full_spec((S, POOL_W)),
        compiler_params=_params(("arbitrary",)),
    )(z, wp_bd, pscale)


def _pool_bwd(z, wp_bd, pscale, dbr, name):
    def body(u_ref, w_ref, s_ref, dbr_ref, du_ref, dw_ref, red_ref):
        row = lax.broadcasted_iota(jnp.int32, (S, POOL_W), 0)
        lane = lax.broadcasted_iota(jnp.int32, (S, POOL_W), 1)
        p, cnt = _pool_p(u_ref[...], row, lane)
        pb = p.astype(BF16)
        y = jnp.dot(pb, w_ref[...], preferred_element_type=F32)
        dbr = dbr_ref[...]
        red_ref[...] = jnp.zeros_like(red_ref)
        red_ref[0:1, :] = jnp.sum(dbr * y, axis=0, keepdims=True)
        dy = (dbr * s_ref[0:1, :]).astype(BF16)
        dw_ref[...] = lax.dot_general(pb, dy, (((0,), (0,)), ((), ())), preferred_element_type=F32)
        dp = lax.dot_general(dy, w_ref[...], (((1,), (1,)), ((), ())), preferred_element_type=F32)
        g = dp / cnt
        a2 = g + _shift_up(g, 1, row)
        a4 = a2 + _shift_up(a2, 2, row)
        a8 = a4 + _shift_up(a4, 4, row)
        a16 = a8 + _shift_up(a8, 8, row)
        du_ref[...] = (_pool_window_select(lane, a2, a4, a8, a16) - dp).astype(BF16)

    return pl.pallas_call(
        body, name=name,
        out_shape=(jax.ShapeDtypeStruct((S, POOL_W), BF16), jax.ShapeDtypeStruct((POOL_W, POOL_W), F32),
                   jax.ShapeDtypeStruct((8, POOL_W), F32)),
        grid=(1,),
        in_specs=[_full_spec((S, POOL_W), (0, Z_PC // POOL_W)), _full_spec((POOL_W, POOL_W)), _full_spec((8, POOL_W)),
                  _full_spec((S, POOL_W))],
        out_specs=(_full_spec((S, POOL_W)), _full_spec((POOL_W, POOL_W)), _full_spec((8, POOL_W))),
        compiler_params=_params(("arbitrary",)),
    )(z, wp_bd, pscale, dbr)


def _conv_specs():
    base = Z_PC // CONV_W
    return [_full_spec((S, CONV_W), (0, base + 1)), _full_spec((S, CONV_W), (0, base + 2)),
            _full_spec((S, CONV_W), (0, base + 3)), _full_spec((8, CONV_W))]


def _conv_fwd(z, cw, name):
    def body(h_ref, b_ref, c_ref, w_ref, o_ref):
        row = lax.broadcasted_iota(jnp.int32, (S, CONV_W), 0)
        u = c_ref[...] * h_ref[...]
        y = (w_ref[0:1, :] * _shift_down(u, 2, row) + w_ref[1:2, :] * _shift_down(u, 1, row) + w_ref[2:3, :] * u)
        o_ref[...] = b_ref[...] * y

    return pl.pallas_call(
        body, name=name, out_shape=jax.ShapeDtypeStruct((S, CONV_W), F32), grid=(1,),
        in_specs=_conv_specs(), out_specs=_full_spec((S, CONV_W)),
        compiler_params=_params(("arbitrary",)),
    )(z, z, z, cw)


def _conv_bwd(z, cw, dbr, name):
    def body(h_ref, b_ref, c_ref, w_ref, dbr_ref, d_ref, red_ref):
        row = lax.broadcasted_iota(jnp.int32, (S, CONV_W), 0)
        h, cg = h_ref[...], c_ref[...]
        u = cg * h
        u1 = _shift_down(u, 1, row)
        u2 = _shift_down(u, 2, row)
        y = w_ref[0:1, :] * u2 + w_ref[1:2, :] * u1 + w_ref[2:3, :] * u
        dbr = dbr_ref[...]
        dy = dbr * b_ref[...]
        du = w_ref[2:3, :] * dy + w_ref[1:2, :] * _shift_up(dy, 1, row) + w_ref[0:1, :] * _shift_up(dy, 2, row)
        d_ref[:, 0:CONV_W] = (du * cg).astype(BF16)
        d_ref[:, CONV_W:2 * CONV_W] = (dbr * y).astype(BF16)
        d_ref[:, 2 * CONV_W:3 * CONV_W] = (du * h).astype(BF16)
        red_ref[...] = jnp.zeros_like(red_ref)
        red_ref[0:1, :] = jnp.sum(dy * u2, axis=0, keepdims=True)
        red_ref[1:2, :] = jnp.sum(dy * u1, axis=0, keepdims=True)
        red_ref[2:3, :] = jnp.sum(dy * u, axis=0, keepdims=True)

    return pl.pallas_call(
        body, name=name,
        out_shape=(jax.ShapeDtypeStruct((S, 3 * CONV_W), BF16), jax.ShapeDtypeStruct((8, CONV_W), F32)),
        grid=(1,),
        in_specs=_conv_specs() + [_full_spec((S, CONV_W))],
        out_specs=(_full_spec((S, 3 * CONV_W)), _full_spec((8, CONV_W))),
        compiler_params=_params(("arbitrary",)),
    )(z, z, z, cw, dbr)


_NT = (((1,), (1,)), ((), ()))
_TN = (((0,), (0,)), ((), ()))
N_HEAD = 2 * N_PAIR


def _split3(x):
    hi = x.astype(BF16).astype(F32)
    mid = (x - hi).astype(BF16).astype(F32)
    lo = (x - hi - mid).astype(BF16).astype(F32)
    return hi, mid, lo


def _spare(lane, e, k):
    return lane == 64 * (1 - e) + k


def _spare3(lane, e, k):
    base = 64 * (1 - e) + k
    return (lane >= base) & (lane < base + 3)


def _put3(lane, e, k, pieces, rest):
    out = rest
    for n, piece in enumerate(pieces):
        out = jnp.where(_spare(lane, e, k + n), piece, out)
    return out


def _attn_prep(z, bf, name):
    def body(q_ref, k_ref, v_ref, f_ref, b_ref, qa_ref, ka_ref, va_ref, kat_ref):
        p = pl.program_id(0)
        row = lax.broadcasted_iota(jnp.int32, (S, LANE), 0)
        lane = lax.broadcasted_iota(jnp.int32, (S, LANE), 1)
        xv = f_ref[...] + b_ref[0:1, :]
        ls = jnp.minimum(xv, 0.0) - jnp.log(1.0 + jnp.exp(-jnp.abs(xv)))
        cum = _cumsum_rows(jnp.where(lane < N_HEAD, ls, 0.0), row)
        q, k, v = q_ref[...], k_ref[...], v_ref[...]
        for e in range(2):
            head = (lane >= 64) if e else (lane < 64)
            f = jnp.sum(jnp.where(lane == 2 * p + e, cum, 0.0), axis=1, keepdims=True)
            pieces = _split3(f)
            qa = jnp.where(head, q * ATT_SCALE, _put3(lane, e, 0, pieces, jnp.where(_spare3(lane, e, 3), 1.0, 0.0)))
            ones = jnp.where(_spare3(lane, e, 0) | _spare3(lane, e, 6), 1.0, 0.0)
            ka = jnp.where(head, k, _put3(lane, e, 3, [-x for x in pieces], ones))
            va = jnp.where(head, v, jnp.where(_spare3(lane, e, 0), 1.0, 0.0))
            qa_ref[e] = qa.astype(BF16)
            ka_ref[e] = ka.astype(BF16)
            va_ref[e] = va.astype(BF16)
            kat_ref[e] = ka.T.astype(BF16)

    qb, kb, vb = Z_Q // LANE, Z_K // LANE, Z_V // LANE
    heads = jax.ShapeDtypeStruct((N_HEAD, S, LANE), BF16)
    pair = pl.BlockSpec((2, S, LANE), lambda p: (p, 0, 0))
    return pl.pallas_call(
        body, name=name,
        out_shape=(heads, heads, heads, jax.ShapeDtypeStruct((N_HEAD, LANE, S), BF16)),
        grid=(N_PAIR,),
        in_specs=[pl.BlockSpec((S, LANE), lambda p: (0, qb + p)), pl.BlockSpec((S, LANE), lambda p: (0, kb + p)),
                  pl.BlockSpec((S, LANE), lambda p: (0, vb + p)), pl.BlockSpec((S, LANE), lambda p: (0, Z_F // LANE)),
                  pl.BlockSpec((8, LANE), lambda p: (0, 0))],
        out_specs=(pair, pair, pair, pl.BlockSpec((2, LANE, S), lambda p: (p, 0, 0))),
        compiler_params=_params(("parallel",)),
    )(z, z, z, z, bf)


def _attn_bwd_prep(qa, o, lse, do, name):
    def body(qa_ref, o_ref, lse_ref, do_ref, qa2_ref, doa_ref):
        lane = lax.broadcasted_iota(jnp.int32, (S, LANE), 1)
        dov, ov, lsev = do_ref[...], o_ref[...], lse_ref[...]
        for e in range(2):
            head = (lane >= 64) if e else (lane < 64)
            dsum = jnp.sum(jnp.where(head, dov * ov, 0.0), axis=1, keepdims=True)
            doa_ref[e] = jnp.where(head, dov, _put3(lane, e, 0, [-x for x in _split3(dsum)], 0.0)).astype(BF16)
            lse_col = lsev[:, 64 * e:64 * e + 1]
            qa2_ref[e] = _put3(lane, e, 6, [-x for x in _split3(lse_col)], qa_ref[e].astype(F32)).astype(BF16)

    heads = jax.ShapeDtypeStruct((N_HEAD, S, LANE), BF16)
    pair = pl.BlockSpec((2, S, LANE), lambda p: (p, 0, 0))
    cols = pl.BlockSpec((S, LANE), lambda p: (0, p))
    return pl.pallas_call(
        body, name=name, out_shape=(heads, heads), grid=(N_PAIR,),
        in_specs=[pair, cols, cols, cols], out_specs=(pair, pair),
        compiler_params=_params(("parallel",)),
    )(qa, o, lse, do)


def _attn_bwd_post(z, bf, dqt, dka, dva, name):
    def body(f_ref, b_ref, dqt_ref, dk_ref, dv_ref, dq_out, dk_out, dv_out, dfl_ref, red_ref, dcum_ref):
        p = pl.program_id(0)

        @pl.when(p == 0)
        def _():
            dcum_ref[...] = jnp.zeros_like(dcum_ref)

        row = lax.broadcasted_iota(jnp.int32, (S, LANE), 0)
        lane = lax.broadcasted_iota(jnp.int32, (S, LANE), 1)
        dqa = [dqt_ref[e].T for e in range(2)]
        dq_out[...] = (jnp.where(lane < 64, dqa[0], dqa[1]) * ATT_SCALE).astype(BF16)
        dk_out[...] = jnp.where(lane < 64, dk_ref[0], dk_ref[1]).astype(BF16)
        dv_out[...] = jnp.where(lane < 64, dv_ref[0], dv_ref[1]).astype(BF16)
        for e in range(2):
            d_query = jnp.sum(jnp.where(_spare(lane, e, 0), dqa[e], 0.0), axis=1, keepdims=True)
            d_key = jnp.sum(jnp.where(_spare(lane, e, 3), dk_ref[e], 0.0), axis=1, keepdims=True)
            dcum_ref[...] += jnp.where(lane == 2 * p + e, d_query - d_key, 0.0)

        @pl.when(p == N_PAIR - 1)
        def _():
            dls = _cumsum_rows(dcum_ref[...], row, reverse=True)
            xv = f_ref[...] + b_ref[0:1, :]
            dx = jnp.where(lane < N_HEAD, dls * jax.nn.sigmoid(-xv), 0.0)
            dfl_ref[...] = dx.astype(BF16)
            red_ref[...] = jnp.zeros_like(red_ref)
            red_ref[0:1, :] = jnp.sum(dx, axis=0, keepdims=True)

    wide = jax.ShapeDtypeStruct((S, N_PAIR * LANE), BF16)
    cols = pl.BlockSpec((S, LANE), lambda p: (0, p))
    pair = pl.BlockSpec((2, S, LANE), lambda p: (p, 0, 0))
    return pl.pallas_call(
        body, name=name,
        out_shape=(wide, wide, wide, jax.ShapeDtypeStruct((S, LANE), BF16), jax.ShapeDtypeStruct((8, LANE), F32)),
        grid=(N_PAIR,),
        in_specs=[pl.BlockSpec((S, LANE), lambda p: (0, Z_F // LANE)), pl.BlockSpec((8, LANE), lambda p: (0, 0)),
                  pl.BlockSpec((2, LANE, S), lambda p: (p, 0, 0)), pair, pair],
        out_specs=(cols, cols, cols, pl.BlockSpec((S, LANE), lambda p: (0, 0)), pl.BlockSpec((8, LANE), lambda p: (0, 0))),
        scratch_shapes=[pltpu.VMEM((S, LANE), F32)],
        compiler_params=_params(("arbitrary",)),
    )(z, bf, dqt, dka, dva)


def _attn_fwd(qa, ka, va, name):
    tq, tk = TQ_FWD, TQ
    ratio = tq // tk

    def body(qa_ref, ka_ref, va_ref, o_ref, lse_ref):
        i = pl.program_id(1)
        lane = lax.broadcasted_iota(jnp.int32, (tq, LANE), 1)
        row = lax.broadcasted_iota(jnp.int32, (tq, tk), 0)
        col = lax.broadcasted_iota(jnp.int32, (tq, tk), 1)
        qs = [qa_ref[0], qa_ref[1]]

        def block(j, carry, masked):
            off = pl.multiple_of(j * tk, tk)
            out = []
            for e in range(2):
                m, acc = carry[e]
                s = lax.dot_general(qs[e], ka_ref[e, pl.ds(off, tk), :], _NT, preferred_element_type=F32)
                if masked:
                    s = jnp.where(col + (j - ratio * i) * tk > row, NEG_INF, s)
                mn = jnp.maximum(m, jnp.max(s, axis=1, keepdims=True))
                p = jnp.exp(s - mn).astype(BF16)
                acc = jnp.exp(m - mn) * acc + jnp.dot(p, va_ref[e, pl.ds(off, tk), :], preferred_element_type=F32)
                out.append((mn, acc))
            return tuple(out)

        init = (jnp.full((tq, 1), NEG_INF, F32), jnp.zeros((tq, LANE), F32))
        carry = lax.fori_loop(0, ratio * i, lambda j, c: block(j, c, False), (init, init))
        for d in range(ratio):
            carry = block(ratio * i + d, carry, True)
        res = []
        for e in range(2):
            m, acc = carry[e]
            l = jnp.sum(jnp.where(_spare(lane, e, 0), acc, 0.0), axis=1, keepdims=True)
            res.append((acc / l, m + jnp.log(l)))
        o_ref[...] = jnp.where(lane < 64, res[0][0], res[1][0])
        lse_ref[...] = jnp.where(lane < 64, res[0][1], res[1][1])

    out = jax.ShapeDtypeStruct((S, N_PAIR * LANE), F32)
    return pl.pallas_call(
        body, name=name, out_shape=(out, out), grid=(N_PAIR, S // tq),
        in_specs=[pl.BlockSpec((2, tq, LANE), lambda p, i: (p, i, 0)), pl.BlockSpec((2, S, LANE), lambda p, i: (p, 0, 0)),
                  pl.BlockSpec((2, S, LANE), lambda p, i: (p, 0, 0))],
        out_specs=(pl.BlockSpec((tq, LANE), lambda p, i: (i, p)), pl.BlockSpec((tq, LANE), lambda p, i: (i, p))),
        compiler_params=_params(("parallel", "parallel")),
    )(qa, ka, va)


def _attn_bwd(qa2, ka, va, kat, doa, name):
    nq = S // TQ

    def body(qa_ref, ka_ref, va_ref, kat_ref, doa_ref, dqt_ref, dk_ref, dv_ref):
        j = pl.program_id(1)

        @pl.when(j == 0)
        def _():
            dqt_ref[...] = jnp.zeros_like(dqt_ref)

        key = lax.broadcasted_iota(jnp.int32, (TQ, TQ), 0)
        qry = lax.broadcasted_iota(jnp.int32, (TQ, TQ), 1)
        kav, vav, katv = [ka_ref[0], ka_ref[1]], [va_ref[0], va_ref[1]], [kat_ref[0], kat_ref[1]]

        def block(i, carry, masked):
            off = pl.multiple_of(i * TQ, TQ)
            out = []
            for e in range(2):
                dk_acc, dv_acc = carry[e]
                qav = qa_ref[e, pl.ds(off, TQ), :]
                doav = doa_ref[e, pl.ds(off, TQ), :]
                s_t = lax.dot_general(kav[e], qav, _NT, preferred_element_type=F32)
                if masked:
                    s_t = jnp.where(key > qry, NEG_INF, s_t)
                p_t = jnp.exp(s_t)
                ds_t = p_t * lax.dot_general(vav[e], doav, _NT, preferred_element_type=F32)
                dsb = ds_t.astype(BF16)
                dv_acc = dv_acc + jnp.dot(p_t.astype(BF16), doav, preferred_element_type=F32)
                dk_acc = dk_acc + jnp.dot(dsb, qav, preferred_element_type=F32)
                dqt_ref[e, :, pl.ds(off, TQ)] += jnp.dot(katv[e], dsb, preferred_element_type=F32)
                out.append((dk_acc, dv_acc))
            return tuple(out)

        zero = (jnp.zeros((TQ, LANE), F32), jnp.zeros((TQ, LANE), F32))
        carry = block(j, (zero, zero), True)
        carry = lax.fori_loop(j + 1, nq, lambda i, c: block(i, c, False), carry)
        for e in range(2):
            dk_ref[e], dv_ref[e] = carry[e]

    full = pl.BlockSpec((2, S, LANE), lambda p, j: (p, 0, 0))
    blk = pl.BlockSpec((2, TQ, LANE), lambda p, j: (p, j, 0))
    acc = jax.ShapeDtypeStruct((N_HEAD, S, LANE), F32)
    return pl.pallas_call(
        body, name=name,
        out_shape=(jax.ShapeDtypeStruct((N_HEAD, LANE, S), F32), acc, acc),
        grid=(N_PAIR, nq),
        in_specs=[full, blk, blk, pl.BlockSpec((2, LANE, TQ), lambda p, j: (p, 0, j)), full],
        out_specs=(pl.BlockSpec((2, LANE, S), lambda p, j: (p, 0, 0)), blk, blk),
        compiler_params=_params(("arbitrary", "arbitrary")),
    )(qa2, ka, va, kat, doa)


ADA_ROWS = 16


def _ada_fwd(c_pad, w_ada, b_cols, name):
    def body(c_ref, w_ref, b_ref, o_ref):
        cv = c_ref[...]
        sc = (cv * jax.nn.sigmoid(cv)).astype(BF16)
        o_ref[0] = jnp.dot(sc, w_ref[0].astype(BF16), preferred_element_type=F32) + b_ref[0, 0:1, :]

    return pl.pallas_call(
        body, name=name, out_shape=jax.ShapeDtypeStruct((DEPTH, ADA_ROWS, ADA_COLS), F32), grid=(DEPTH,),
        in_specs=[pl.BlockSpec((ADA_ROWS, D), lambda l: (0, 0)), pl.BlockSpec((1, D, ADA_COLS), lambda l: (l, 0, 0)),
                  pl.BlockSpec((1, 8, ADA_COLS), lambda l: (l, 0, 0))],
        out_specs=pl.BlockSpec((1, ADA_ROWS, ADA_COLS), lambda l: (l, 0, 0)),
        compiler_params=_params(("parallel",)),
    )(c_pad, w_ada, b_cols)


def _ada_bwd(c_pad, dmod_cols, name):
    def body(c_ref, d_ref, o_ref):
        cv = c_ref[...]
        sc = (cv * jax.nn.sigmoid(cv)).astype(BF16)
        o_ref[0] = lax.dot_general(sc, d_ref[0].astype(BF16), _TN, preferred_element_type=F32)

    return pl.pallas_call(
        body, name=name, out_shape=jax.ShapeDtypeStruct((DEPTH, D, ADA_COLS), F32), grid=(DEPTH,),
        in_specs=[pl.BlockSpec((ADA_ROWS, D), lambda l: (0, 0)), pl.BlockSpec((1, ADA_ROWS, ADA_COLS), lambda l: (l, 0, 0))],
        out_specs=pl.BlockSpec((1, D, ADA_COLS), lambda l: (l, 0, 0)),
        compiler_params=_params(("parallel",)),
    )(c_pad, dmod_cols)


def _adamw_math(w, g, m, v):
    m = B1 * m + (1.0 - B1) * g
    v = B2 * v + (1.0 - B2) * (g * g)
    m_hat = m / (1.0 - B1 ** STEP)
    v_hat = v / (1.0 - B2 ** STEP)
    delta = -LR * (m_hat / (jnp.sqrt(v_hat) + EPS) + WD * w)
    return delta, m, v


def _row_tile(rows, target=256):
    best = 8
    for t in range(8, min(rows, target) + 1, 8):
        if rows % t == 0:
            best = t
    return best


def _adamw(w, g, m, v, name):
    layers, rows, cols = w.shape
    tr = _row_tile(rows)
    spec = pl.BlockSpec((1, tr, cols), lambda l, i: (l, i, 0))

    def body(w_ref, g_ref, m_ref, v_ref, d_ref, nm_ref, nv_ref):
        d_ref[...], nm_ref[...], nv_ref[...] = _adamw_math(w_ref[...], g_ref[...], m_ref[...], v_ref[...])

    out = jax.ShapeDtypeStruct(w.shape, F32)
    return pl.pallas_call(
        body, name=name, out_shape=(out, out, out), grid=(layers, rows // tr),
        in_specs=[spec] * 4, out_specs=(spec,) * 3, compiler_params=_params(("parallel", "parallel")),
    )(w, g, m, v)


def _sum_slabs(x, name):
    n, rows, _ = x.shape
    tr = _row_tile(rows)

    def body(x_ref, o_ref):
        acc = x_ref[0]
        for k in range(1, n):
            acc = acc + x_ref[k]
        o_ref[...] = acc

    return pl.pallas_call(
        body, name=name, out_shape=jax.ShapeDtypeStruct((rows, D), F32), grid=(rows // tr,),
        in_specs=[pl.BlockSpec((n, tr, D), lambda i: (0, i, 0))], out_specs=pl.BlockSpec((tr, D), lambda i: (i, 0)),
        compiler_params=_params(("parallel",)),
    )(x)


_ANY = pl.BlockSpec(memory_space=pl.ANY)
MESH = pl.DeviceIdType.MESH


def _all_gather(xs, name, sequencer_id=None):
    n = len(xs)

    def body(*refs):
        x_refs, out_refs = refs[:n], refs[n:2 * n]
        send_sems, recv_sems, local_sems = refs[2 * n:]
        x_, y_, c_ = lax.axis_index("x"), lax.axis_index("y"), lax.axis_index("c")
        me, sibling = (x_, y_, c_), (x_, y_, 1 - c_)
        chips = [(1 - x_, y_), (x_, 1 - y_), (1 - x_, 1 - y_)]
        if sequencer_id is not None:
            barrier = pltpu.get_barrier_semaphore()
            peers = [sibling] + [(*chip, pc) for chip in chips for pc in (c_, 1 - c_)]
            for peer in peers:
                pl.semaphore_signal(barrier, inc=1, device_id=peer, device_id_type=MESH)
            pl.semaphore_wait(barrier, len(peers))

        def slot(a, px, py, pc):
            return out_refs[a].at[4 * px + 2 * py + pc]

        def copy(a, k, block, to, src=None):
            return pltpu.make_async_remote_copy(
                src_ref=slot(a, *block) if src is None else src, dst_ref=slot(a, *block),
                send_sem=send_sems.at[7 * a + k], recv_sem=recv_sems.at[7 * a + k], device_id=to, device_id_type=MESH)

        mine = [pltpu.make_async_copy(x_refs[a], slot(a, *me), local_sems.at[a]) for a in range(n)]
        for cp in mine:
            cp.start()
        first = []
        for a in range(n):
            first.append(copy(a, 0, me, sibling, src=x_refs[a]))
            first += [copy(a, 1 + j, me, (*chip, c_), src=x_refs[a]) for j, chip in enumerate(chips)]
        for cp in first:
            cp.start()
        passed = []
        for j, chip in enumerate(chips):
            for a in range(n):
                copy(a, 1 + j, (*chip, c_), me).wait_recv()
                passed.append(copy(a, 4 + j, (*chip, c_), sibling))
                passed[-1].start()
        for a in range(n):
            copy(a, 0, sibling, me).wait_recv()
        for j, chip in enumerate(chips):
            for a in range(n):
                copy(a, 4 + j, (*chip, 1 - c_), me).wait_recv()
        for cp in first + passed:
            cp.wait_send()
        for cp in mine:
            cp.wait()

    out_shape = [jax.ShapeDtypeStruct((N_DEV,) + x.shape, x.dtype) for x in xs]
    sems = [pltpu.SemaphoreType.DMA((7 * n,)), pltpu.SemaphoreType.DMA((7 * n,)), pltpu.SemaphoreType.DMA((n,))]
    if sequencer_id is not None:
        return pl.kernel(
            body, out_type=out_shape, mesh=plsc.ScalarSubcoreMesh(axis_name="sequencer", num_cores=1),
            scratch_types=sems, compiler_params=pltpu.CompilerParams(collective_id=sequencer_id), name=name)(*xs)
    return pl.pallas_call(
        body, name=name, out_shape=out_shape, in_specs=[_ANY] * n, out_specs=[_ANY] * n, scratch_shapes=sems)(*xs)


def _sibling_exchange(gs, name):
    n = len(gs)

    def body(*refs):
        g_refs, p_refs = refs[:n], refs[n:2 * n]
        send_sems, recv_sems = refs[2 * n:]
        x_, y_, c_ = lax.axis_index("x"), lax.axis_index("y"), lax.axis_index("c")
        copies = [pltpu.make_async_remote_copy(
            src_ref=g_refs[a].at[2 * k + (1 - c_)], dst_ref=p_refs[a].at[k], send_sem=send_sems.at[4 * a + k],
            recv_sem=recv_sems.at[4 * a + k], device_id=(x_, y_, 1 - c_), device_id_type=MESH)
            for a in range(n) for k in range(4)]
        for cp in copies:
            cp.start()
        for cp in copies:
            cp.wait()

    return pl.pallas_call(
        body, name=name, out_shape=[jax.ShapeDtypeStruct((4,) + g.shape[1:], g.dtype) for g in gs],
        in_specs=[_ANY] * n, out_specs=[_ANY] * n,
        scratch_shapes=[pltpu.SemaphoreType.DMA((4 * n,)), pltpu.SemaphoreType.DMA((4 * n,))],
    )(*gs)


def _slab_tiles(rows, cols):
    if rows % 8 == 0:
        return _row_tile(rows), cols
    return rows, 2 * LANE


def _pair_sums(g, p, route, name):
    _, rows, cols = g.shape
    tr, tc = _slab_tiles(rows, cols)

    def body(route_ref, g_ref, p_ref, t_ref):
        t_ref[...] = (g_ref[...] + p_ref[...]).astype(BF16)

    return pl.pallas_call(
        body, name=name, out_shape=jax.ShapeDtypeStruct((3, rows, cols), BF16),
        grid_spec=pltpu.PrefetchScalarGridSpec(
            num_scalar_prefetch=1, grid=(3, rows // tr, cols // tc),
            in_specs=[pl.BlockSpec((1, tr, tc), lambda r, i, j, route_ref: (2 * route_ref[1 + r] + route_ref[0], i, j)),
                      pl.BlockSpec((1, tr, tc), lambda r, i, j, route_ref: (route_ref[1 + r], i, j))],
            out_specs=pl.BlockSpec((1, tr, tc), lambda r, i, j, route_ref: (r, i, j))),
        compiler_params=_params(("parallel", "parallel", "parallel")),
    )(route, g, p)


def _chip_exchange(ts, name, sequencer_id=None):
    n = len(ts)

    def body(*refs):
        t_refs, l_refs = refs[:n], refs[n:2 * n]
        send_sems, recv_sems = refs[2 * n:]
        x_, y_, c_ = lax.axis_index("x"), lax.axis_index("y"), lax.axis_index("c")
        chips = [(1 - x_, y_), (x_, 1 - y_), (1 - x_, 1 - y_)]
        if sequencer_id is not None:
            barrier = pltpu.get_barrier_semaphore()
            for px, py in chips:
                pl.semaphore_signal(barrier, inc=1, device_id=(px, py, c_), device_id_type=MESH)
            pl.semaphore_wait(barrier, len(chips))
        copies = [pltpu.make_async_remote_copy(
            src_ref=t_refs[a].at[r], dst_ref=l_refs[a].at[r], send_sem=send_sems.at[3 * a + r],
            recv_sem=recv_sems.at[3 * a + r], device_id=(px, py, c_), device_id_type=MESH)
            for a in range(n) for r, (px, py) in enumerate(chips)]
        for cp in copies:
            cp.start()
        for cp in copies:
            cp.wait()

    out_shape = [jax.ShapeDtypeStruct((3,) + t.shape[1:], t.dtype) for t in ts]
    sems = [pltpu.SemaphoreType.DMA((3 * n,)), pltpu.SemaphoreType.DMA((3 * n,))]
    if sequencer_id is not None:
        return pl.kernel(
            body, out_type=out_shape, mesh=plsc.ScalarSubcoreMesh(axis_name="sequencer", num_cores=1),
            scratch_types=sems, compiler_params=pltpu.CompilerParams(collective_id=sequencer_id), name=name)(*ts)
    return pl.pallas_call(
        body, name=name, out_shape=out_shape, in_specs=[_ANY] * n, out_specs=[_ANY] * n, scratch_shapes=sems)(*ts)


def _reduce_adamw(gs, ps, landed, place, w, m, v, name):
    layers, rows, cols = w.shape
    assert layers == DEPTH == 2
    tr, tc = _slab_tiles(rows, cols)
    nr, nc = rows // tr, cols // tc
    spec = pl.BlockSpec((1, tr, tc), lambda l, i, j, place_ref: (l, i, j))

    def own(layer, which):
        pi, pj = (nr - 1, nc - 1) if layer == 0 else (0, 0)

        def index(l, i, j, place_ref):
            lead = 0 if which is None else place_ref[which]
            return lead, jnp.where(l == layer, i, pi), jnp.where(l == layer, j, pj)

        return pl.BlockSpec((3 if which is None else 1, tr, tc), index)

    def body(place_ref, g0_ref, p0_ref, l0_ref, g1_ref, p1_ref, l1_ref, w_ref, m_ref, v_ref,
             g_ref, d_ref, nm_ref, nv_ref):
        def update(own_ref, sib_ref, l_ref):
            g = own_ref[0] + sib_ref[0] + l_ref[0].astype(F32) + l_ref[1].astype(F32) + l_ref[2].astype(F32)
            g_ref[0] = g
            d_ref[0], nm_ref[0], nv_ref[0] = _adamw_math(w_ref[0], g, m_ref[0], v_ref[0])

        @pl.when(pl.program_id(0) == 0)
        def _():
            update(g0_ref, p0_ref, l0_ref)

        @pl.when(pl.program_id(0) == 1)
        def _():
            update(g1_ref, p1_ref, l1_ref)

    out = jax.ShapeDtypeStruct(w.shape, F32)
    return pl.pallas_call(
        body, name=name, out_shape=(out, out, out, out),
        grid_spec=pltpu.PrefetchScalarGridSpec(
            num_scalar_prefetch=1, grid=(DEPTH, nr, nc),
            in_specs=[own(0, 0), own(0, 1), own(0, None), own(1, 0), own(1, 1), own(1, None), spec, spec, spec],
            out_specs=(spec, spec, spec, spec)),
        compiler_params=_params(("arbitrary", "arbitrary", "arbitrary")),
    )(place, gs[0], ps[0], landed[0], gs[1], ps[1], landed[1], w, m, v)


def _pack(pieces, row_multiple, dtype, cols=D, rows=None):
    flat = jnp.concatenate([p.astype(dtype).reshape(-1) for p in pieces])
    if rows is None:
        rows = -(-flat.shape[0] // cols)
        rows = -(-rows // row_multiple) * row_multiple
    flat = jnp.pad(flat, (0, rows * cols - flat.shape[0]))
    return flat.reshape(rows, cols)


def _unpack(flat, shapes, lead=()):
    out, off = [], 0
    for shp in shapes:
        n = 1
        for s_ in shp:
            n *= s_
        out.append(lax.slice_in_dim(flat, off, off + n, axis=len(lead)).reshape(lead + tuple(shp)))
        off += n
    return out


def _z_rows_from_in(wt):
    pad = jnp.zeros((NZ - IN_COLS, wt.shape[1]), wt.dtype)
    return jnp.concatenate([wt[1544:2568], wt[2568:5640], wt[0:1536], wt[1536:1544], pad], axis=0)


def _in_rows_from_z(wt):
    return jnp.concatenate([wt[Z_Q:Z_Q + 1536], wt[Z_F:Z_F + 8], wt[Z_PC:Z_PC + 1024], wt[Z_G:Z_G + 3072]], axis=0)


def _pad_rows(v, rows=8):
    return jnp.pad(v, ((0, rows - v.shape[0]), (0, 0)))


def _layer_fwd(l, x, wts, gvec, mod):
    tag = f"l{l}"
    h = _prenorm_fwd(x, gvec, mod, 0, 0, 1, f"prenorm_mix_{tag}")
    z = _matmul(h, wts["w_in_t"], "nt", f"in_proj_{tag}", tn=1152)
    qa, ka, va, kat = _attn_prep(z, wts["b_f"], f"attn_prep_{tag}")
    o, lse = _attn_fwd(qa, ka, va, f"attn_{tag}")
    br_b = _pool_fwd(z, wts["wp_bd"], wts["pool_scale"], f"pool_{tag}")
    br_c = _conv_fwd(z, wts["conv_w"], f"conv_{tag}")
    pa = _matmul(o, wts["wa"], "nn", f"proj_a_{tag}")
    pb = _matmul(br_b, wts["wb"], "nn", f"proj_b_{tag}")
    pc = _matmul(br_c, wts["wc"], "nn", f"proj_c_{tag}")
    merged = _merge_fwd(z, pa, pb, pc, f"merge_{tag}")
    y = _matmul(merged, wts["w_out"], "nn", f"out_proj_{tag}")
    x1 = _postnorm_fwd(x, y, gvec, mod, 1, 2, f"postnorm_mix_{tag}")
    h2 = _prenorm_fwd(x1, gvec, mod, 2, 3, 4, f"prenorm_ff_{tag}")
    a = _matmul(h2, wts["w_ff1"], "nn", f"ff1_{tag}", b_col_shards=True)
    r = _relu2_fwd(a, f"relu2_{tag}")
    y2 = _matmul(r, wts["w_ff2"], "nn", f"ff2_{tag}")
    x2 = _postnorm_fwd(x1, y2, gvec, mod, 3, 5, f"postnorm_ff_{tag}")
    saved = dict(x=x, h=h, z=z, qa=qa, ka=ka, va=va, kat=kat, o=o, lse=lse, br_b=br_b, br_c=br_c, pa=pa, pb=pb, pc=pc,
                 merged=merged, y=y, x1=x1, h2=h2, a=a, r=r, y2=y2)
    return x2, saved


def _layer_bwd(l, dx2, sv, wts, gvec, mod):
    tag = f"l{l}"
    dy2, red_post_ff = _postnorm_bwd(sv["y2"], gvec, mod, dx2, 3, 5, f"postnorm_ff_bwd_{tag}")
    dr = _matmul(dy2, wts["w_ff2"], "nt", f"ff2_dx_{tag}")
    d_w_ff2 = _matmul(sv["r"], dy2, "tn", f"ff2_dw_{tag}")
    da = _relu2_bwd(sv["a"], dr, f"relu2_bwd_{tag}")
    dh2 = _matmul(da, wts["w_ff1"], "nt", f"ff1_dx_{tag}", b_col_shards=True)
    d_w_ff1 = _matmul(sv["h2"], da, "tn", f"ff1_dw_{tag}", out_col_shards=True)
    dx1, red_pre_ff = _prenorm_bwd(sv["x1"], gvec, mod, dh2, dx2, 2, 4, f"prenorm_ff_bwd_{tag}")

    dy, red_post_mix = _postnorm_bwd(sv["y"], gvec, mod, dx1, 1, 2, f"postnorm_mix_bwd_{tag}")
    dmerged = _matmul(dy, wts["w_out"], "nt", f"out_proj_dx_{tag}")
    d_w_out = _matmul(sv["merged"], dy, "tn", f"out_proj_dw_{tag}")
    dpa, dpb, dpc, dgl = _merge_bwd(sv["z"], sv["pa"], sv["pb"], sv["pc"], dmerged, f"merge_bwd_{tag}")
    do = _matmul(dpa, wts["wa"], "nt", f"proj_a_dx_{tag}")
    dbr_b = _matmul(dpb, wts["wb"], "nt", f"proj_b_dx_{tag}")
    dbr_c = _matmul(dpc, wts["wc"], "nt", f"proj_c_dx_{tag}")
    d_wa = _matmul(sv["o"], dpa, "tn", f"proj_a_dw_{tag}")
    d_wb = _matmul(sv["br_b"], dpb, "tn", f"proj_b_dw_{tag}")
    d_wc = _matmul(sv["br_c"], dpc, "tn", f"proj_c_dw_{tag}")
    d_w_branch = jnp.concatenate([d_wa, d_wb, d_wc], axis=0)

    dpu, d_wp_bd, red_pool = _pool_bwd(sv["z"], wts["wp_bd"], wts["pool_scale"], dbr_b, f"pool_bwd_{tag}")
    dconv, red_conv = _conv_bwd(sv["z"], wts["conv_w"], dbr_c, f"conv_bwd_{tag}")
    qa2, doa = _attn_bwd_prep(sv["qa"], sv["o"], sv["lse"], do, f"attn_bwd_prep_{tag}")
    dqt, dka, dva = _attn_bwd(qa2, sv["ka"], sv["va"], sv["kat"], doa, f"attn_bwd_{tag}")
    dq, dk, dv, dfl, red_f = _attn_bwd_post(sv["z"], wts["b_f"], dqt, dka, dva, f"attn_bwd_post_{tag}")
    dz = jnp.concatenate([dpu, dconv, dgl, dq, dk, dv, dfl], axis=1)
    dh = _matmul(dz, wts["w_in_t"], "nn", f"in_proj_dx_{tag}", tk=1152)
    d_w_in_t = _matmul(dz, sv["h"], "tn", f"in_proj_dw_{tag}", tm=1152)
    dx0, red_pre_mix = _prenorm_bwd(sv["x"], gvec, mod, dh, dx1, 0, 1, f"prenorm_mix_bwd_{tag}")

    rows = D // N_DEV
    big = [_in_rows_from_z(d_w_in_t).reshape(N_DEV, IN_SHARD, D), d_w_branch.reshape(N_DEV, rows, D),
           d_w_out.reshape(N_DEV, rows, D), d_w_ff1, d_w_ff2.reshape(N_DEV, D_FF // N_DEV, D)]
    d_w_pool = jnp.stack([d_wp_bd[64 * g:64 * (g + 1), 64 * g:64 * (g + 1)] for g in range(4)])
    small = dict(
        mod=jnp.stack([red_pre_mix[0], red_pre_mix[1], red_post_mix[0], red_pre_ff[0], red_pre_ff[1], red_post_ff[0]]),
        g_mix_pre=red_pre_mix[2], g_mix_post=red_post_mix[1], g_ff_pre=red_pre_ff[2], g_ff_post=red_post_ff[1],
        b_f=red_f[0, 0:8], w_pool=d_w_pool, pool_scale=red_pool[0], conv_w=red_conv[0:3])
    return dx0, big, small


SMALL_KEYS = ["mod", "g_mix_pre", "g_mix_post", "g_ff_pre", "g_ff_post", "b_f", "w_pool", "pool_scale", "conv_w"]
SMALL_SHAPES = [(DEPTH, 6 * D), (DEPTH, D), (DEPTH, D), (DEPTH, D), (DEPTH, D), (DEPTH, 8), (DEPTH, 4, 64, 64),
                (DEPTH, POOL_W), (DEPTH, 3, CONV_W)]


def kernel(x, c, w_ada, b_ada, g_mix_pre, g_mix_post, g_ff_pre, g_ff_post, w_in, b_f, w_pool, pool_scale, conv_w, w_branch, w_out, w_ff1, w_ff2, loss_target, m_w_ada, m_b_ada, m_g_mix_pre, m_g_mix_post, m_g_ff_pre, m_g_ff_post, m_w_in, m_b_f, m_w_pool, m_pool_scale, m_conv_w, m_w_branch, m_w_out, m_w_ff1, m_w_ff2, v_w_ada, v_b_ada, v_g_mix_pre, v_g_mix_post, v_g_ff_pre, v_g_ff_post, v_w_in, v_b_f, v_w_pool, v_pool_scale, v_conv_w, v_w_branch, v_w_out, v_w_ff1, v_w_ff2):
    ix, iy, ic = lax.axis_index("x"), lax.axis_index("y"), lax.axis_index("c")
    me = 4 * ix + 2 * iy + ic
    route = jnp.stack([ic, 2 * (1 - ix) + iy, 2 * ix + (1 - iy), 2 * (1 - ix) + (1 - iy)]).astype(jnp.int32)
    place = jnp.stack([me, 2 * ix + iy]).astype(jnp.int32)
    wt_in, mt_in, vt_in = (jnp.transpose(a, (0, 2, 1)) for a in (w_in, m_w_in, v_w_in))

    c_all = _all_gather([_pad_rows(c)], "gather_c")[0][:, 0, :]
    c_pad = _pad_rows(c_all, ADA_ROWS)
    b_cols = lax.dynamic_slice_in_dim(b_ada, me * ADA_COLS, ADA_COLS, axis=1)
    b_cols = jnp.broadcast_to(b_cols[:, None, :], (DEPTH, 8, ADA_COLS))
    mod_part = _ada_fwd(c_pad, w_ada, b_cols, "ada_fwd")
    mod_all = _all_gather([mod_part.reshape(DEPTH * ADA_ROWS, ADA_COLS)], "gather_mod")[0]
    mod_all = mod_all.reshape(N_DEV, DEPTH, ADA_ROWS, ADA_COLS)
    mod_mine = lax.dynamic_index_in_dim(mod_all, me, axis=2, keepdims=False)
    mod_mine = jnp.transpose(mod_mine, (1, 0, 2)).reshape(DEPTH, 6, D)

    cw_cols = CONV_W // N_DEV
    cw_send = jnp.pad(conv_w.reshape(DEPTH * 3, cw_cols), ((0, 8 - DEPTH * 3), (0, LANE - cw_cols)))
    gathered = []
    for l in range(DEPTH):
        send = [w[l].astype(BF16) for w in (wt_in, w_branch, w_out, w_ff1, w_ff2)] + ([cw_send] if l == 0 else [])
        gathered.append(_all_gather(send, f"gather_weights_l{l}", sequencer_id=1 + l))
    cw_all = gathered[0][-1][:, :DEPTH * 3, :cw_cols].reshape(N_DEV, DEPTH, 3, cw_cols)

    def layer_operands(l, weights):
        p_in, p_br, p_out, p_ff1, p_ff2 = weights[:5]
        w_br_full = p_br.reshape(D, D)
        cw_full = jnp.transpose(cw_all[:, l], (1, 0, 2)).reshape(3, CONV_W)
        wp_bd = jnp.zeros((POOL_W, POOL_W), F32)
        for g in range(4):
            wp_bd = wp_bd.at[64 * g:64 * (g + 1), 64 * g:64 * (g + 1)].set(w_pool[l, g])
        wts = dict(
            w_in_t=_z_rows_from_in(p_in.reshape(IN_COLS, D)), wa=w_br_full[0:A_WIDTH], wb=w_br_full[A_WIDTH:A_WIDTH + POOL_W],
            wc=w_br_full[A_WIDTH + POOL_W:], w_out=p_out.reshape(D, D),
            w_ff1=p_ff1, w_ff2=p_ff2.reshape(D_FF, D),
            conv_w=_pad_rows(cw_full), wp_bd=wp_bd.astype(BF16), pool_scale=_pad_rows(pool_scale[l][None, :]),
            b_f=_pad_rows(jnp.pad(b_f[l], (0, LANE - 8))[None, :]))
        gvec = _pad_rows(jnp.stack([g_mix_pre[l], g_mix_post[l], g_ff_pre[l], g_ff_post[l]]))
        return wts, gvec, _pad_rows(mod_mine[l])

    xs = x[0]
    saved, layers = [], []
    for l in range(DEPTH):
        weights = gathered[l]
        if l > 0:
            xs, weights = lax.optimization_barrier((xs, weights))
        layers.append(layer_operands(l, weights))
        xs, sv = _layer_fwd(l, xs, *layers[l])
        saved.append(sv)
    dx, loss_part = _loss_head(xs, loss_target[0], "loss_head")
    loss = lax.psum(loss_part[0, 0], ("x", "y", "c"))
    small_grads, mine, sibs, landed = [None] * DEPTH, [None] * DEPTH, [None] * DEPTH, [None] * DEPTH
    for l in reversed(range(DEPTH)):
        dx, mine[l], small_grads[l] = _layer_bwd(l, dx, saved[l], *layers[l])
        sibs[l] = _sibling_exchange(mine[l], f"rs_sibling_l{l}")
        sends = [_pair_sums(g, p, route, f"rs_pair_sums_{k}_l{l}") for k, (g, p) in enumerate(zip(mine[l], sibs[l]))]
        if l > 0:
            dx, sends = lax.optimization_barrier((dx, sends))
        landed[l] = _chip_exchange(sends, f"rs_chips_l{l}", sequencer_id=3 + l)
    grad_x = dx[None]

    big_w = [wt_in, w_branch, w_out, w_ff1, w_ff2]
    big_m = [mt_in, m_w_branch, m_w_out, m_w_ff1, m_w_ff2]
    big_v = [vt_in, v_w_branch, v_w_out, v_w_ff1, v_w_ff2]
    big_out = [[], [], [], []]
    for k in range(5):
        res = _reduce_adamw([mine[l][k] for l in range(DEPTH)], [sibs[l][k] for l in range(DEPTH)],
                            [landed[l][k] for l in range(DEPTH)], place, big_w[k], big_m[k], big_v[k],
                            f"rs_sum_adamw_{k}")
        for which in range(4):
            big_out[which].append(jnp.transpose(res[which], (0, 2, 1)) if k == 0 else res[which])

    small = {k: jnp.stack([small_grads[l][k] for l in range(DEPTH)]) for k in SMALL_KEYS}
    small_all = _all_gather([_pack([small[k] for k in SMALL_KEYS], 8, F32)], "gather_small")[0]
    dmod_all = small_all[:, 0:DEPTH * 6, :].reshape(N_DEV, DEPTH, 6 * D)
    summed = _unpack(_sum_slabs(small_all, "sum_small").reshape(-1), SMALL_SHAPES)
    sg = dict(zip(SMALL_KEYS, summed))
    dmod_cols = lax.dynamic_slice_in_dim(dmod_all, me * ADA_COLS, ADA_COLS, axis=2)
    dmod_cols = jnp.pad(jnp.transpose(dmod_cols, (1, 0, 2)), ((0, 0), (0, ADA_ROWS - N_DEV), (0, 0)))
    g_w_ada = _ada_bwd(c_pad, dmod_cols, "ada_bwd")
    g_conv_w = lax.dynamic_slice_in_dim(sg["conv_w"], me * (CONV_W // N_DEV), CONV_W // N_DEV, axis=2)

    ada_out = [g_w_ada] + list(_adamw(w_ada, g_w_ada, m_w_ada, v_w_ada, "adamw_ada"))
    rest_w = [b_ada, g_mix_pre, g_mix_post, g_ff_pre, g_ff_post, b_f, w_pool, pool_scale, conv_w]
    rest_m = [m_b_ada, m_g_mix_pre, m_g_mix_post, m_g_ff_pre, m_g_ff_post, m_b_f, m_w_pool, m_pool_scale, m_conv_w]
    rest_v = [v_b_ada, v_g_mix_pre, v_g_mix_post, v_g_ff_pre, v_g_ff_post, v_b_f, v_w_pool, v_pool_scale, v_conv_w]
    rest_g = [sg["mod"], sg["g_mix_pre"], sg["g_mix_post"], sg["g_ff_pre"], sg["g_ff_post"], sg["b_f"],
              sg["w_pool"], sg["pool_scale"], g_conv_w]
    rest_shapes = [a.shape for a in rest_w]
    upd = _adamw(_pack(rest_w, 8, F32)[None], _pack(rest_g, 8, F32)[None], _pack(rest_m, 8, F32)[None],
                 _pack(rest_v, 8, F32)[None], "adamw_rest")
    rest_out = [rest_g] + [_unpack(arr.reshape(-1), rest_shapes) for arr in upd]
    rest_out = [[ada_out[which]] + rest_out[which] for which in range(4)]

    def ordered(k):
        r, b = rest_out[k], big_out[k]
        return [r[0], r[1], r[2], r[3], r[4], r[5], b[0], r[6], r[7], r[8], r[9], b[1], b[2], b[3], b[4]]

    return (loss, grad_x, *ordered(0), *ordered(1), *ordered(2), *ordered(3))
```

```python
import functools

import jax
import jax.numpy as jnp
from jax import lax
from jax.experimental import pallas as pl
from jax.experimental.pallas import tpu as pltpu
from jax.experimental.pallas import tpu_sc as plsc

F32 = jnp.float32
BF16 = jnp.bfloat16

N_DEV = 8
D = 1024
S = 2048
DEPTH = 2
D_FF = 4 * D
A_WIDTH = 512
HEAD_DIM = 64
N_PAIR = 4
POOL_W = 256
CONV_W = 256
IN_COLS = 5640
ADA_COLS = 6 * D // N_DEV
IN_SHARD = IN_COLS // N_DEV
RMS_EPS = 1e-6
NEG_INF = -1e30
ATT_SCALE = HEAD_DIM ** -0.5

NZ = 5760
Z_PC = 0
Z_G = 1024
Z_Q = 4096
Z_K = 4608
Z_V = 5120
Z_F = 5632

LR, B1, B2, EPS, WD, STEP = 0.001, 0.9, 0.999, 1e-08, 0.01, 10

LANE = 128
VMEM_LIMIT_BYTES = 48 * 1024 * 1024
TS = 256
TQ = 256
TQ_FWD = 512


def _params(sem=None):
    return pltpu.CompilerParams(dimension_semantics=sem, vmem_limit_bytes=VMEM_LIMIT_BYTES)


def _pick(n, target):
    best = None
    for t in range(LANE, min(n, target) + 1, LANE):
        if n % t == 0:
            best = t
    return n if best is None else best


def _matmul(a, b, mode, name, out_dtype=F32, tm=1024, tn=1024, tk=1024, b_col_shards=False, out_col_shards=False):
    if b_col_shards:
        shards, b_rows, shard_cols = b.shape
        b_shape = (b_rows, shards * shard_cols)
    else:
        b_shape = b.shape
    if mode == "nn":
        (m, k), (k2, n) = a.shape, b_shape
    elif mode == "nt":
        (m, k), (n, k2) = a.shape, b_shape
    else:
        (k, m), (k2, n) = a.shape, b_shape
    assert k == k2, (a.shape, b.shape, mode)
    tm, tn, tk = _pick(m, tm), _pick(n, tn), _pick(k, tk)
    if b_col_shards and mode == "nn":
        tn = shard_cols
    if b_col_shards and mode == "nt":
        tk = shard_cols
    if out_col_shards:
        tn = n // N_DEV
    nk = k // tk
    if mode == "nn":
        a_spec = pl.BlockSpec((tm, tk), lambda i, j, kk: (i, kk))
        b_spec = (pl.BlockSpec((None, tk, tn), lambda i, j, kk: (j, kk, 0)) if b_col_shards else
                  pl.BlockSpec((tk, tn), lambda i, j, kk: (kk, j)))
        dims = (((1,), (0,)), ((), ()))
    elif mode == "nt":
        a_spec = pl.BlockSpec((tm, tk), lambda i, j, kk: (i, kk))
        b_spec = (pl.BlockSpec((None, tn, tk), lambda i, j, kk: (kk, j, 0)) if b_col_shards else
                  pl.BlockSpec((tn, tk), lambda i, j, kk: (j, kk)))
        dims = (((1,), (1,)), ((), ()))
    else:
        assert not b_col_shards
        a_spec = pl.BlockSpec((tk, tm), lambda i, j, kk: (kk, i))
        b_spec = pl.BlockSpec((tk, tn), lambda i, j, kk: (kk, j))
        dims = (((0,), (0,)), ((), ()))
    if out_col_shards:
        out_shape = jax.ShapeDtypeStruct((N_DEV, m, tn), out_dtype)
        out_spec = pl.BlockSpec((None, tm, tn), lambda i, j, kk: (j, i, 0))
    else:
        out_shape = jax.ShapeDtypeStruct((m, n), out_dtype)
        out_spec = pl.BlockSpec((tm, tn), lambda i, j, kk: (i, j))

    def product(a_ref, b_ref):
        return lax.dot_general(a_ref[...].astype(BF16), b_ref[...].astype(BF16), dims, preferred_element_type=F32)

    def body_one_pass(a_ref, b_ref, o_ref):
        o_ref[...] = product(a_ref, b_ref).astype(out_dtype)

    def body(a_ref, b_ref, o_ref, acc_ref):
        kk = pl.program_id(2)

        @pl.when(kk == 0)
        def _():
            acc_ref[...] = product(a_ref, b_ref)

        @pl.when(kk > 0)
        def _():
            acc_ref[...] += product(a_ref, b_ref)

        @pl.when(kk == nk - 1)
        def _():
            o_ref[...] = acc_ref[...].astype(out_dtype)

    return pl.pallas_call(
        body_one_pass if nk == 1 else body, name=name,
        out_shape=out_shape,
        grid=(m // tm, n // tn, nk),
        in_specs=[a_spec, b_spec],
        out_specs=out_spec,
        scratch_shapes=[] if nk == 1 else [pltpu.VMEM((tm, tn), F32)],
        compiler_params=_params(("parallel", "parallel", "arbitrary")),
    )(a, b)


def _row_spec(width=D, col=0):
    return pl.BlockSpec((TS, width), lambda i: (i, col))


def _vec_spec(rows=8, width=D):
    return pl.BlockSpec((rows, width), lambda i: (0, 0))


def _rms(x):
    return lax.rsqrt(jnp.mean(x * x, axis=-1, keepdims=True) + RMS_EPS)


def _prenorm_fwd(x, gvec, mod, g_row, shift_row, scale_row, name):
    def body(x_ref, g_ref, mod_ref, h_ref):
        xv = x_ref[...]
        y = xv * _rms(xv) * g_ref[g_row:g_row + 1, :]
        h = y * (1.0 + mod_ref[scale_row:scale_row + 1, :]) + mod_ref[shift_row:shift_row + 1, :]
        h_ref[...] = h.astype(BF16)

    return pl.pallas_call(
        body, name=name, out_shape=jax.ShapeDtypeStruct((S, D), BF16), grid=(S // TS,),
        in_specs=[_row_spec(), _vec_spec(), _vec_spec()], out_specs=_row_spec(),
        compiler_params=_params(("parallel",)),
    )(x, gvec, mod)


def _prenorm_bwd(x, gvec, mod, dh, dres, g_row, scale_row, name):
    def body(x_ref, g_ref, mod_ref, dh_ref, dres_ref, dx_ref, red_ref):
        i = pl.program_id(0)

        @pl.when(i == 0)
        def _():
            red_ref[...] = jnp.zeros_like(red_ref)

        xv = x_ref[...]
        g = g_ref[g_row:g_row + 1, :]
        r = _rms(xv)
        n = xv * r
        yg = n * g
        dhv = dh_ref[...]
        dyg = dhv * (1.0 + mod_ref[scale_row:scale_row + 1, :])
        dn = dyg * g
        dx = r * (dn - n * jnp.mean(dn * n, axis=-1, keepdims=True))
        dx_ref[...] = dres_ref[...] + dx
        red_ref[0:1, :] += jnp.sum(dhv, axis=0, keepdims=True)
        red_ref[1:2, :] += jnp.sum(dhv * yg, axis=0, keepdims=True)
        red_ref[2:3, :] += jnp.sum(dyg * n, axis=0, keepdims=True)

    return pl.pallas_call(
        body, name=name,
        out_shape=(jax.ShapeDtypeStruct((S, D), F32), jax.ShapeDtypeStruct((8, D), F32)),
        grid=(S // TS,),
        in_specs=[_row_spec(), _vec_spec(), _vec_spec(), _row_spec(), _row_spec()],
        out_specs=(_row_spec(), _vec_spec()),
        compiler_params=_params(("arbitrary",)),
    )(x, gvec, mod, dh, dres)


def _postnorm_fwd(x, y, gvec, mod, g_row, gate_row, name):
    def body(x_ref, y_ref, g_ref, mod_ref, o_ref):
        yv = y_ref[...]
        yn = yv * _rms(yv) * g_ref[g_row:g_row + 1, :]
        o_ref[...] = x_ref[...] + mod_ref[gate_row:gate_row + 1, :] * yn

    return pl.pallas_call(
        body, name=name, out_shape=jax.ShapeDtypeStruct((S, D), F32), grid=(S // TS,),
        in_specs=[_row_spec(), _row_spec(), _vec_spec(), _vec_spec()], out_specs=_row_spec(),
        compiler_params=_params(("parallel",)),
    )(x, y, gvec, mod)


def _postnorm_bwd(y, gvec, mod, dxo, g_row, gate_row, name):
    def body(y_ref, g_ref, mod_ref, dxo_ref, dy_ref, red_ref):
        i = pl.program_id(0)

        @pl.when(i == 0)
        def _():
            red_ref[...] = jnp.zeros_like(red_ref)

        yv = y_ref[...]
        g = g_ref[g_row:g_row + 1, :]
        r = _rms(yv)
        n = yv * r
        dxo = dxo_ref[...]
        dyn = dxo * mod_ref[gate_row:gate_row + 1, :]
        dn = dyn * g
        dy = r * (dn - n * jnp.mean(dn * n, axis=-1, keepdims=True))
        dy_ref[...] = dy.astype(BF16)
        red_ref[0:1, :] += jnp.sum(dxo * (n * g), axis=0, keepdims=True)
        red_ref[1:2, :] += jnp.sum(dyn * n, axis=0, keepdims=True)

    return pl.pallas_call(
        body, name=name,
        out_shape=(jax.ShapeDtypeStruct((S, D), BF16), jax.ShapeDtypeStruct((8, D), F32)),
        grid=(S // TS,),
        in_specs=[_row_spec(), _vec_spec(), _vec_spec(), _row_spec()],
        out_specs=(_row_spec(), _vec_spec()),
        compiler_params=_params(("arbitrary",)),
    )(y, gvec, mod, dxo)


def _loss_head(xf, target, name):
    def body(x_ref, t_ref, dx_ref, loss_ref):
        i = pl.program_id(0)

        @pl.when(i == 0)
        def _():
            loss_ref[...] = jnp.zeros_like(loss_ref)

        e = x_ref[...] - t_ref[...]
        dx_ref[...] = e / float(D)
        per_tok = jnp.mean(e * e, axis=-1, keepdims=True)
        loss_ref[0:1, 0:1] += 0.5 * jnp.sum(per_tok, axis=0, keepdims=True)

    return pl.pallas_call(
        body, name=name,
        out_shape=(jax.ShapeDtypeStruct((S, D), F32), jax.ShapeDtypeStruct((8, LANE), F32)),
        grid=(S // TS,),
        in_specs=[_row_spec(), _row_spec()],
        out_specs=(_row_spec(), pl.BlockSpec((8, LANE), lambda i: (0, 0))),
        compiler_params=_params(("arbitrary",)),
    )(xf, target)


def _relu2_fwd(a, name):
    def body(a_ref, r_ref):
        t = jnp.maximum(a_ref[...], 0.0)
        r_ref[...] = (t * t).astype(BF16)

    return pl.pallas_call(
        body, name=name, out_shape=jax.ShapeDtypeStruct((S, D_FF), BF16), grid=(S // TS,),
        in_specs=[_row_spec(D_FF)], out_specs=_row_spec(D_FF),
        compiler_params=_params(("parallel",)),
    )(a)


def _relu2_bwd(a, dr, name):
    def body(a_ref, dr_ref, da_ref):
        da_ref[...] = (dr_ref[...] * (2.0 * jnp.maximum(a_ref[...], 0.0))).astype(BF16)

    return pl.pallas_call(
        body, name=name, out_shape=jax.ShapeDtypeStruct((S, D_FF), BF16), grid=(S // TS,),
        in_specs=[_row_spec(D_FF), _row_spec(D_FF)], out_specs=_row_spec(D_FF),
        compiler_params=_params(("parallel",)),
    )(a, dr)


def _merge_fwd(z, pa, pb, pc, name):
    def body(g0_ref, g1_ref, g2_ref, pa_ref, pb_ref, pc_ref, o_ref):
        m = (jax.nn.sigmoid(g0_ref[...]) * pa_ref[...] + jax.nn.sigmoid(g1_ref[...]) * pb_ref[...]
             + jax.nn.sigmoid(g2_ref[...]) * pc_ref[...])
        o_ref[...] = m.astype(BF16)

    gb = Z_G // D
    return pl.pallas_call(
        body, name=name, out_shape=jax.ShapeDtypeStruct((S, D), BF16), grid=(S // TS,),
        in_specs=[_row_spec(D, gb), _row_spec(D, gb + 1), _row_spec(D, gb + 2), _row_spec(), _row_spec(), _row_spec()],
        out_specs=_row_spec(),
        compiler_params=_params(("parallel",)),
    )(z, z, z, pa, pb, pc)


def _merge_bwd(z, pa, pb, pc, dm, name):
    def body(g0_ref, g1_ref, g2_ref, pa_ref, pb_ref, pc_ref, dm_ref, da_ref, db_ref, dc_ref, dgl_ref):
        dmv = dm_ref[...]
        for k, (g_ref, p_ref, d_ref) in enumerate(((g0_ref, pa_ref, da_ref), (g1_ref, pb_ref, db_ref),
                                                   (g2_ref, pc_ref, dc_ref))):
            sg = jax.nn.sigmoid(g_ref[...])
            d_ref[...] = (dmv * sg).astype(BF16)
            dgl_ref[:, k * D:(k + 1) * D] = (dmv * p_ref[...] * (sg * (1.0 - sg))).astype(BF16)

    gb = Z_G // D
    proj = jax.ShapeDtypeStruct((S, D), BF16)
    return pl.pallas_call(
        body, name=name,
        out_shape=(proj, proj, proj, jax.ShapeDtypeStruct((S, 3 * D), BF16)),
        grid=(S // TS,),
        in_specs=[_row_spec(D, gb), _row_spec(D, gb + 1), _row_spec(D, gb + 2), _row_spec(), _row_spec(), _row_spec(),
                  _row_spec()],
        out_specs=(_row_spec(), _row_spec(), _row_spec(), _row_spec(3 * D)),
        compiler_params=_params(("parallel",)),
    )(z, z, z, pa, pb, pc, dm)


def _shift_down(x, k, row):
    return jnp.where(row >= k, pltpu.roll(x, k, axis=0), 0.0)


def _shift_up(x, k, row):
    n = x.shape[0]
    return jnp.where(row < n - k, pltpu.roll(x, n - k, axis=0), 0.0)


def _cumsum_rows(x, row, reverse=False):
    shift = _shift_up if reverse else _shift_down
    k = 1
    while k < x.shape[0]:
        x = x + shift(x, k, row)
        k *= 2
    return x


def _full_spec(shape, idx=(0, 0)):
    return pl.BlockSpec(shape, lambda i: idx)


def _pool_window_select(lane, a2, a4, a8, a16):
    return jnp.where(lane < 64, a2, jnp.where(lane < 128, a4, jnp.where(lane < 192, a8, a16)))


def _pool_p(u, row, lane):
    t2 = u + _shift_down(u, 1, row)
    t4 = t2 + _shift_down(t2, 2, row)
    t8 = t4 + _shift_down(t4, 4, row)
    t16 = t8 + _shift_down(t8, 8, row)
    tw = _pool_window_select(lane, t2, t4, t8, t16)
    cnt = jnp.minimum((row + 1).astype(F32), _pool_window_select(lane, 2.0, 4.0, 8.0, 16.0))
    return tw / cnt - u, cnt


def _pool_fwd(z, wp_bd, pscale, name):
    def body(u_ref, w_ref, s_ref, o_ref):
        row = lax.broadcasted_iota(jnp.int32, (S, POOL_W), 0)
        lane = lax.broadcasted_iota(jnp.int32, (S, POOL_W), 1)
        p, _ = _pool_p(u_ref[...], row, lane)
        y = jnp.dot(p.astype(BF16), w_ref[...], preferred_element_type=F32)
        o_ref[...] = y * s_ref[0:1, :]

    return pl.pallas_call(
        body, name=name, out_shape=jax.ShapeDtypeStruct((S, POOL_W), F32), grid=(1,),
        in_specs=[_full_spec((S, POOL_W), (0, Z_PC // POOL_W)), _full_spec((POOL_W, POOL_W)), _full_spec((8, POOL_W))],
        out_specs=_full_spec((S, POOL_W)),
        compiler_params=_params(("arbitrary",)),
    )(z, wp_bd, pscale)


def _pool_bwd(z, wp_bd, pscale, dbr, name):
    def body(u_ref, w_ref, s_ref, dbr_ref, du_ref, dw_ref, red_ref):
        row = lax.broadcasted_iota(jnp.int32, (S, POOL_W), 0)
        lane = lax.broadcasted_iota(jnp.int32, (S, POOL_W), 1)
        p, cnt = _pool_p(u_ref[...], row, lane)
        pb = p.astype(BF16)
        y = jnp.dot(pb, w_ref[...], preferred_element_type=F32)
        dbr = dbr_ref[...]
        red_ref[...] = jnp.zeros_like(red_ref)
        red_ref[0:1, :] = jnp.sum(dbr * y, axis=0, keepdims=True)
        dy = (dbr * s_ref[0:1, :]).astype(BF16)
        dw_ref[...] = lax.dot_general(pb, dy, (((0,), (0,)), ((), ())), preferred_element_type=F32)
        dp = lax.dot_general(dy, w_ref[...], (((1,), (1,)), ((), ())), preferred_element_type=F32)
        g = dp / cnt
        a2 = g + _shift_up(g, 1, row)
        a4 = a2 + _shift_up(a2, 2, row)
        a8 = a4 + _shift_up(a4, 4, row)
        a16 = a8 + _shift_up(a8, 8, row)
        du_ref[...] = (_pool_window_select(lane, a2, a4, a8, a16) - dp).astype(BF16)

    return pl.pallas_call(
        body, name=name,
        out_shape=(jax.ShapeDtypeStruct((S, POOL_W), BF16), jax.ShapeDtypeStruct((POOL_W, POOL_W), F32),
                   jax.ShapeDtypeStruct((8, POOL_W), F32)),
        grid=(1,),
        in_specs=[_full_spec((S, POOL_W), (0, Z_PC // POOL_W)), _full_spec((POOL_W, POOL_W)), _full_spec((8, POOL_W)),
                  _full_spec((S, POOL_W))],
        out_specs=(_full_spec((S, POOL_W)), _full_spec((POOL_W, POOL_W)), _full_spec((8, POOL_W))),
        compiler_params=_params(("arbitrary",)),
    )(z, wp_bd, pscale, dbr)


def _conv_specs():
    base = Z_PC // CONV_W
    return [_full_spec((S, CONV_W), (0, base + 1)), _full_spec((S, CONV_W), (0, base + 2)),
            _full_spec((S, CONV_W), (0, base + 3)), _full_spec((8, CONV_W))]


def _conv_fwd(z, cw, name):
    def body(h_ref, b_ref, c_ref, w_ref, o_ref):
        row = lax.broadcasted_iota(jnp.int32, (S, CONV_W), 0)
        u = c_ref[...] * h_ref[...]
        y = (w_ref[0:1, :] * _shift_down(u, 2, row) + w_ref[1:2, :] * _shift_down(u, 1, row) + w_ref[2:3, :] * u)
        o_ref[...] = b_ref[...] * y

    return pl.pallas_call(
        body, name=name, out_shape=jax.ShapeDtypeStruct((S, CONV_W), F32), grid=(1,),
        in_specs=_conv_specs(), out_specs=_full_spec((S, CONV_W)),
        compiler_params=_params(("arbitrary",)),
    )(z, z, z, cw)


def _conv_bwd(z, cw, dbr, name):
    def body(h_ref, b_ref, c_ref, w_ref, dbr_ref, d_ref, red_ref):
        row = lax.broadcasted_iota(jnp.int32, (S, CONV_W), 0)
        h, cg = h_ref[...], c_ref[...]
        u = cg * h
        u1 = _shift_down(u, 1, row)
        u2 = _shift_down(u, 2, row)
        y = w_ref[0:1, :] * u2 + w_ref[1:2, :] * u1 + w_ref[2:3, :] * u
        dbr = dbr_ref[...]
        dy = dbr * b_ref[...]
        du = w_ref[2:3, :] * dy + w_ref[1:2, :] * _shift_up(dy, 1, row) + w_ref[0:1, :] * _shift_up(dy, 2, row)
        d_ref[:, 0:CONV_W] = (du * cg).astype(BF16)
        d_ref[:, CONV_W:2 * CONV_W] = (dbr * y).astype(BF16)
        d_ref[:, 2 * CONV_W:3 * CONV_W] = (du * h).astype(BF16)
        red_ref[...] = jnp.zeros_like(red_ref)
        red_ref[0:1, :] = jnp.sum(dy * u2, axis=0, keepdims=True)
        red_ref[1:2, :] = jnp.sum(dy * u1, axis=0, keepdims=True)
        red_ref[2:3, :] = jnp.sum(dy * u, axis=0, keepdims=True)

    return pl.pallas_call(
        body, name=name,
        out_shape=(jax.ShapeDtypeStruct((S, 3 * CONV_W), BF16), jax.ShapeDtypeStruct((8, CONV_W), F32)),
        grid=(1,),
        in_specs=_conv_specs() + [_full_spec((S, CONV_W))],
        out_specs=(_full_spec((S, 3 * CONV_W)), _full_spec((8, CONV_W))),
        compiler_params=_params(("arbitrary",)),
    )(z, z, z, cw, dbr)


_NT = (((1,), (1,)), ((), ()))
_TN = (((0,), (0,)), ((), ()))
N_HEAD = 2 * N_PAIR


def _split3(x):
    hi = x.astype(BF16).astype(F32)
    mid = (x - hi).astype(BF16).astype(F32)
    lo = (x - hi - mid).astype(BF16).astype(F32)
    return hi, mid, lo


def _spare(lane, e, k):
    return lane == 64 * (1 - e) + k


def _spare3(lane, e, k):
    base = 64 * (1 - e) + k
    return (lane >= base) & (lane < base + 3)


def _put3(lane, e, k, pieces, rest):
    out = rest
    for n, piece in enumerate(pieces):
        out = jnp.where(_spare(lane, e, k + n), piece, out)
    return out


def _attn_prep(z, bf, name):
    def body(q_ref, k_ref, v_ref, f_ref, b_ref, qa_ref, ka_ref, va_ref, kat_ref):
        p = pl.program_id(0)
        row = lax.broadcasted_iota(jnp.int32, (S, LANE), 0)
        lane = lax.broadcasted_iota(jnp.int32, (S, LANE), 1)
        xv = f_ref[...] + b_ref[0:1, :]
        ls = jnp.minimum(xv, 0.0) - jnp.log(1.0 + jnp.exp(-jnp.abs(xv)))
        cum = _cumsum_rows(jnp.where(lane < N_HEAD, ls, 0.0), row)
        q, k, v = q_ref[...], k_ref[...], v_ref[...]
        for e in range(2):
            head = (lane >= 64) if e else (lane < 64)
            f = jnp.sum(jnp.where(lane == 2 * p + e, cum, 0.0), axis=1, keepdims=True)
            pieces = _split3(f)
            qa = jnp.where(head, q * ATT_SCALE, _put3(lane, e, 0, pieces, jnp.where(_spare3(lane, e, 3), 1.0, 0.0)))
            ones = jnp.where(_spare3(lane, e, 0) | _spare3(lane, e, 6), 1.0, 0.0)
            ka = jnp.where(head, k, _put3(lane, e, 3, [-x for x in pieces], ones))
            va = jnp.where(head, v, jnp.where(_spare3(lane, e, 0), 1.0, 0.0))
            qa_ref[e] = qa.astype(BF16)
            ka_ref[e] = ka.astype(BF16)
            va_ref[e] = va.astype(BF16)
            kat_ref[e] = ka.T.astype(BF16)

    qb, kb, vb = Z_Q // LANE, Z_K // LANE, Z_V // LANE
    heads = jax.ShapeDtypeStruct((N_HEAD, S, LANE), BF16)
    pair = pl.BlockSpec((2, S, LANE), lambda p: (p, 0, 0))
    return pl.pallas_call(
        body, name=name,
        out_shape=(heads, heads, heads, jax.ShapeDtypeStruct((N_HEAD, LANE, S), BF16)),
        grid=(N_PAIR,),
        in_specs=[pl.BlockSpec((S, LANE), lambda p: (0, qb + p)), pl.BlockSpec((S, LANE), lambda p: (0, kb + p)),
                  pl.BlockSpec((S, LANE), lambda p: (0, vb + p)), pl.BlockSpec((S, LANE), lambda p: (0, Z_F // LANE)),
                  pl.BlockSpec((8, LANE), lambda p: (0, 0))],
        out_specs=(pair, pair, pair, pl.BlockSpec((2, LANE, S), lambda p: (p, 0, 0))),
        compiler_params=_params(("parallel",)),
    )(z, z, z, z, bf)


def _attn_bwd_prep(qa, o, lse, do, name):
    def body(qa_ref, o_ref, lse_ref, do_ref, qa2_ref, doa_ref):
        lane = lax.broadcasted_iota(jnp.int32, (S, LANE), 1)
        dov, ov, lsev = do_ref[...], o_ref[...], lse_ref[...]
        for e in range(2):
            head = (lane >= 64) if e else (lane < 64)
            dsum = jnp.sum(jnp.where(head, dov * ov, 0.0), axis=1, keepdims=True)
            doa_ref[e] = jnp.where(head, dov, _put3(lane, e, 0, [-x for x in _split3(dsum)], 0.0)).astype(BF16)
            lse_col = lsev[:, 64 * e:64 * e + 1]
            qa2_ref[e] = _put3(lane, e, 6, [-x for x in _split3(lse_col)], qa_ref[e].astype(F32)).astype(BF16)

    heads = jax.ShapeDtypeStruct((N_HEAD, S, LANE), BF16)
    pair = pl.BlockSpec((2, S, LANE), lambda p: (p, 0, 0))
    cols = pl.BlockSpec((S, LANE), lambda p: (0, p))
    return pl.pallas_call(
        body, name=name, out_shape=(heads, heads), grid=(N_PAIR,),
        in_specs=[pair, cols, cols, cols], out_specs=(pair, pair),
        compiler_params=_params(("parallel",)),
    )(qa, o, lse, do)


def _attn_bwd_post(z, bf, dqt, dka, dva, name):
    def body(f_ref, b_ref, dqt_ref, dk_ref, dv_ref, dq_out, dk_out, dv_out, dfl_ref, red_ref, dcum_ref):
        p = pl.program_id(0)

        @pl.when(p == 0)
        def _():
            dcum_ref[...] = jnp.zeros_like(dcum_ref)

        row = lax.broadcasted_iota(jnp.int32, (S, LANE), 0)
        lane = lax.broadcasted_iota(jnp.int32, (S, LANE), 1)
        dqa = [dqt_ref[e].T for e in range(2)]
        dq_out[...] = (jnp.where(lane < 64, dqa[0], dqa[1]) * ATT_SCALE).astype(BF16)
        dk_out[...] = jnp.where(lane < 64, dk_ref[0], dk_ref[1]).astype(BF16)
        dv_out[...] = jnp.where(lane < 64, dv_ref[0], dv_ref[1]).astype(BF16)
        for e in range(2):
            d_query = jnp.sum(jnp.where(_spare(lane, e, 0), dqa[e], 0.0), axis=1, keepdims=True)
            d_key = jnp.sum(jnp.where(_spare(lane, e, 3), dk_ref[e], 0.0), axis=1, keepdims=True)
            dcum_ref[...] += jnp.where(lane == 2 * p + e, d_query - d_key, 0.0)

        @pl.when(p == N_PAIR - 1)
        def _():
            dls = _cumsum_rows(dcum_ref[...], row, reverse=True)
            xv = f_ref[...] + b_ref[0:1, :]
            dx = jnp.where(lane < N_HEAD, dls * jax.nn.sigmoid(-xv), 0.0)
            dfl_ref[...] = dx.astype(BF16)
            red_ref[...] = jnp.zeros_like(red_ref)
            red_ref[0:1, :] = jnp.sum(dx, axis=0, keepdims=True)

    wide = jax.ShapeDtypeStruct((S, N_PAIR * LANE), BF16)
    cols = pl.BlockSpec((S, LANE), lambda p: (0, p))
    pair = pl.BlockSpec((2, S, LANE), lambda p: (p, 0, 0))
    return pl.pallas_call(
        body, name=name,
        out_shape=(wide, wide, wide, jax.ShapeDtypeStruct((S, LANE), BF16), jax.ShapeDtypeStruct((8, LANE), F32)),
        grid=(N_PAIR,),
        in_specs=[pl.BlockSpec((S, LANE), lambda p: (0, Z_F // LANE)), pl.BlockSpec((8, LANE), lambda p: (0, 0)),
                  pl.BlockSpec((2, LANE, S), lambda p: (p, 0, 0)), pair, pair],
        out_specs=(cols, cols, cols, pl.BlockSpec((S, LANE), lambda p: (0, 0)), pl.BlockSpec((8, LANE), lambda p: (0, 0))),
        scratch_shapes=[pltpu.VMEM((S, LANE), F32)],
        compiler_params=_params(("arbitrary",)),
    )(z, bf, dqt, dka, dva)


def _attn_fwd(qa, ka, va, name):
    tq, tk = TQ_FWD, TQ
    ratio = tq // tk

    def body(qa_ref, ka_ref, va_ref, o_ref, lse_ref):
        i = pl.program_id(1)
        lane = lax.broadcasted_iota(jnp.int32, (tq, LANE), 1)
        row = lax.broadcasted_iota(jnp.int32, (tq, tk), 0)
        col = lax.broadcasted_iota(jnp.int32, (tq, tk), 1)
        qs = [qa_ref[0], qa_ref[1]]

        def block(j, carry, masked):
            off = pl.multiple_of(j * tk, tk)
            out = []
            for e in range(2):
                m, acc = carry[e]
                s = lax.dot_general(qs[e], ka_ref[e, pl.ds(off, tk), :], _NT, preferred_element_type=F32)
                if masked:
                    s = jnp.where(col + (j - ratio * i) * tk > row, NEG_INF, s)
                mn = jnp.maximum(m, jnp.max(s, axis=1, keepdims=True))
                p = jnp.exp(s - mn).astype(BF16)
                acc = jnp.exp(m - mn) * acc + jnp.dot(p, va_ref[e, pl.ds(off, tk), :], preferred_element_type=F32)
                out.append((mn, acc))
            return tuple(out)

        init = (jnp.full((tq, 1), NEG_INF, F32), jnp.zeros((tq, LANE), F32))
        carry = lax.fori_loop(0, ratio * i, lambda j, c: block(j, c, False), (init, init))
        for d in range(ratio):
            carry = block(ratio * i + d, carry, True)
        res = []
        for e in range(2):
            m, acc = carry[e]
            l = jnp.sum(jnp.where(_spare(lane, e, 0), acc, 0.0), axis=1, keepdims=True)
            res.append((acc / l, m + jnp.log(l)))
        o_ref[...] = jnp.where(lane < 64, res[0][0], res[1][0])
        lse_ref[...] = jnp.where(lane < 64, res[0][1], res[1][1])

    out = jax.ShapeDtypeStruct((S, N_PAIR * LANE), F32)
    return pl.pallas_call(
        body, name=name, out_shape=(out, out), grid=(N_PAIR, S // tq),
        in_specs=[pl.BlockSpec((2, tq, LANE), lambda p, i: (p, i, 0)), pl.BlockSpec((2, S, LANE), lambda p, i: (p, 0, 0)),
                  pl.BlockSpec((2, S, LANE), lambda p, i: (p, 0, 0))],
        out_specs=(pl.BlockSpec((tq, LANE), lambda p, i: (i, p)), pl.BlockSpec((tq, LANE), lambda p, i: (i, p))),
        compiler_params=_params(("parallel", "parallel")),
    )(qa, ka, va)


def _attn_bwd(qa2, ka, va, kat, doa, name):
    nq = S // TQ

    def body(qa_ref, ka_ref, va_ref, kat_ref, doa_ref, dqt_ref, dk_ref, dv_ref):
        j = pl.program_id(1)

        @pl.when(j == 0)
        def _():
            dqt_ref[...] = jnp.zeros_like(dqt_ref)

        key = lax.broadcasted_iota(jnp.int32, (TQ, TQ), 0)
        qry = lax.broadcasted_iota(jnp.int32, (TQ, TQ), 1)
        kav, vav, katv = [ka_ref[0], ka_ref[1]], [va_ref[0], va_ref[1]], [kat_ref[0], kat_ref[1]]

        def block(i, carry, masked):
            off = pl.multiple_of(i * TQ, TQ)
            out = []
            for e in range(2):
                dk_acc, dv_acc = carry[e]
                qav = qa_ref[e, pl.ds(off, TQ), :]
                doav = doa_ref[e, pl.ds(off, TQ), :]
                s_t = lax.dot_general(kav[e], qav, _NT, preferred_element_type=F32)
                if masked:
                    s_t = jnp.where(key > qry, NEG_INF, s_t)
                p_t = jnp.exp(s_t)
                ds_t = p_t * lax.dot_general(vav[e], doav, _NT, preferred_element_type=F32)
                dsb = ds_t.astype(BF16)
                dv_acc = dv_acc + jnp.dot(p_t.astype(BF16), doav, preferred_element_type=F32)
                dk_acc = dk_acc + jnp.dot(dsb, qav, preferred_element_type=F32)
                dqt_ref[e, :, pl.ds(off, TQ)] += jnp.dot(katv[e], dsb, preferred_element_type=F32)
                out.append((dk_acc, dv_acc))
            return tuple(out)

        zero = (jnp.zeros((TQ, LANE), F32), jnp.zeros((TQ, LANE), F32))
        carry = block(j, (zero, zero), True)
        carry = lax.fori_loop(j + 1, nq, lambda i, c: block(i, c, False), carry)
        for e in range(2):
            dk_ref[e], dv_ref[e] = carry[e]

    full = pl.BlockSpec((2, S, LANE), lambda p, j: (p, 0, 0))
    blk = pl.BlockSpec((2, TQ, LANE), lambda p, j: (p, j, 0))
    acc = jax.ShapeDtypeStruct((N_HEAD, S, LANE), F32)
    return pl.pallas_call(
        body, name=name,
        out_shape=(jax.ShapeDtypeStruct((N_HEAD, LANE, S), F32), acc, acc),
        grid=(N_PAIR, nq),
        in_specs=[full, blk, blk, pl.BlockSpec((2, LANE, TQ), lambda p, j: (p, 0, j)), full],
        out_specs=(pl.BlockSpec((2, LANE, S), lambda p, j: (p, 0, 0)), blk, blk),
        compiler_params=_params(("arbitrary", "arbitrary")),
    )(qa2, ka, va, kat, doa)


ADA_ROWS = 16


def _ada_fwd(c_pad, w_ada, b_cols, name):
    def body(c_ref, w_ref, b_ref, o_ref):
        cv = c_ref[...]
        sc = (cv * jax.nn.sigmoid(cv)).astype(BF16)
        o_ref[0] = jnp.dot(sc, w_ref[0].astype(BF16), preferred_element_type=F32) + b_ref[0, 0:1, :]

    return pl.pallas_call(
        body, name=name, out_shape=jax.ShapeDtypeStruct((DEPTH, ADA_ROWS, ADA_COLS), F32), grid=(DEPTH,),
        in_specs=[pl.BlockSpec((ADA_ROWS, D), lambda l: (0, 0)), pl.BlockSpec((1, D, ADA_COLS), lambda l: (l, 0, 0)),
                  pl.BlockSpec((1, 8, ADA_COLS), lambda l: (l, 0, 0))],
        out_specs=pl.BlockSpec((1, ADA_ROWS, ADA_COLS), lambda l: (l, 0, 0)),
        compiler_params=_params(("parallel",)),
    )(c_pad, w_ada, b_cols)


def _ada_bwd(c_pad, dmod_cols, name):
    def body(c_ref, d_ref, o_ref):
        cv = c_ref[...]
        sc = (cv * jax.nn.sigmoid(cv)).astype(BF16)
        o_ref[0] = lax.dot_general(sc, d_ref[0].astype(BF16), _TN, preferred_element_type=F32)

    return pl.pallas_call(
        body, name=name, out_shape=jax.ShapeDtypeStruct((DEPTH, D, ADA_COLS), F32), grid=(DEPTH,),
        in_specs=[pl.BlockSpec((ADA_ROWS, D), lambda l: (0, 0)), pl.BlockSpec((1, ADA_ROWS, ADA_COLS), lambda l: (l, 0, 0))],
        out_specs=pl.BlockSpec((1, D, ADA_COLS), lambda l: (l, 0, 0)),
        compiler_params=_params(("parallel",)),
    )(c_pad, dmod_cols)


def _adamw_math(w, g, m, v):
    m = B1 * m + (1.0 - B1) * g
    v = B2 * v + (1.0 - B2) * (g * g)
    m_hat = m / (1.0 - B1 ** STEP)
    v_hat = v / (1.0 - B2 ** STEP)
    delta = -LR * (m_hat / (jnp.sqrt(v_hat) + EPS) + WD * w)
    return delta, m, v


def _row_tile(rows, target=256):
    best = 8
    for t in range(8, min(rows, target) + 1, 8):
        if rows % t == 0:
            best = t
    return best


def _adamw(w, g, m, v, name):
    layers, rows, cols = w.shape
    tr = _row_tile(rows)
    spec = pl.BlockSpec((1, tr, cols), lambda l, i: (l, i, 0))

    def body(w_ref, g_ref, m_ref, v_ref, d_ref, nm_ref, nv_ref):
        d_ref[...], nm_ref[...], nv_ref[...] = _adamw_math(w_ref[...], g_ref[...], m_ref[...], v_ref[...])

    out = jax.ShapeDtypeStruct(w.shape, F32)
    return pl.pallas_call(
        body, name=name, out_shape=(out, out, out), grid=(layers, rows // tr),
        in_specs=[spec] * 4, out_specs=(spec,) * 3, compiler_params=_params(("parallel", "parallel")),
    )(w, g, m, v)


def _sum_slabs(x, name):
    n, rows, _ = x.shape
    tr = _row_tile(rows)

    def body(x_ref, o_ref):
        acc = x_ref[0]
        for k in range(1, n):
            acc = acc + x_ref[k]
        o_ref[...] = acc

    return pl.pallas_call(
        body, name=name, out_shape=jax.ShapeDtypeStruct((rows, D), F32), grid=(rows // tr,),
        in_specs=[pl.BlockSpec((n, tr, D), lambda i: (0, i, 0))], out_specs=pl.BlockSpec((tr, D), lambda i: (i, 0)),
        compiler_params=_params(("parallel",)),
    )(x)


_ANY = pl.BlockSpec(memory_space=pl.ANY)
MESH = pl.DeviceIdType.MESH


def _on_sequencer(body, out_shape, sems, operands, after, sequencer_id, name):
    n = len(operands)

    def ordered_body(*refs):
        body(*refs[:n], *refs[n + 1:])

    extra = [] if after is None else [after]
    return pl.kernel(
        body if after is None else ordered_body, out_type=out_shape,
        mesh=plsc.ScalarSubcoreMesh(axis_name="sequencer", num_cores=1), scratch_types=sems,
        compiler_params=pltpu.CompilerParams(collective_id=sequencer_id), name=name)(*operands, *extra)


def _all_gather(xs, name, sequencer_id=None, after=None):
    n = len(xs)

    def body(*refs):
        x_refs, out_refs = refs[:n], refs[n:2 * n]
        send_sems, recv_sems, local_sems = refs[2 * n:]
        x_, y_, c_ = lax.axis_index("x"), lax.axis_index("y"), lax.axis_index("c")
        me, sibling = (x_, y_, c_), (x_, y_, 1 - c_)
        chips = [(1 - x_, y_), (x_, 1 - y_), (1 - x_, 1 - y_)]
        if sequencer_id is not None:
            barrier = pltpu.get_barrier_semaphore()
            peers = [sibling] + [(*chip, pc) for chip in chips for pc in (c_, 1 - c_)]
            for peer in peers:
                pl.semaphore_signal(barrier, inc=1, device_id=peer, device_id_type=MESH)
            pl.semaphore_wait(barrier, len(peers))

        def slot(a, px, py, pc):
            return out_refs[a].at[4 * px + 2 * py + pc]

        def copy(a, k, block, to, src=None):
            return pltpu.make_async_remote_copy(
                src_ref=slot(a, *block) if src is None else src, dst_ref=slot(a, *block),
                send_sem=send_sems.at[7 * a + k], recv_sem=recv_sems.at[7 * a + k], device_id=to, device_id_type=MESH)

        mine = [pltpu.make_async_copy(x_refs[a], slot(a, *me), local_sems.at[a]) for a in range(n)]
        for cp in mine:
            cp.start()
        first = []
        for a in range(n):
            first.append(copy(a, 0, me, sibling, src=x_refs[a]))
            first += [copy(a, 1 + j, me, (*chip, c_), src=x_refs[a]) for j, chip in enumerate(chips)]
        for cp in first:
            cp.start()
        passed = []
        for j, chip in enumerate(chips):
            for a in range(n):
                copy(a, 1 + j, (*chip, c_), me).wait_recv()
                passed.append(copy(a, 4 + j, (*chip, c_), sibling))
                passed[-1].start()
        for a in range(n):
            copy(a, 0, sibling, me).wait_recv()
        for j, chip in enumerate(chips):
            for a in range(n):
                copy(a, 4 + j, (*chip, 1 - c_), me).wait_recv()
        for cp in first + passed:
            cp.wait_send()
        for cp in mine:
            cp.wait()

    out_shape = [jax.ShapeDtypeStruct((N_DEV,) + x.shape, x.dtype) for x in xs]
    sems = [pltpu.SemaphoreType.DMA((7 * n,)), pltpu.SemaphoreType.DMA((7 * n,)), pltpu.SemaphoreType.DMA((n,))]
    if sequencer_id is not None:
        return _on_sequencer(body, out_shape, sems, xs, after, sequencer_id, name)
    return pl.pallas_call(
        body, name=name, out_shape=out_shape, in_specs=[_ANY] * n, out_specs=[_ANY] * n, scratch_shapes=sems)(*xs)


def _sibling_exchange(gs, name, sequencer_id=None, after=None):
    n = len(gs)

    def body(*refs):
        g_refs, p_refs = refs[:n], refs[n:2 * n]
        send_sems, recv_sems = refs[2 * n:]
        x_, y_, c_ = lax.axis_index("x"), lax.axis_index("y"), lax.axis_index("c")
        if sequencer_id is not None:
            barrier = pltpu.get_barrier_semaphore()
            pl.semaphore_signal(barrier, inc=1, device_id=(x_, y_, 1 - c_), device_id_type=MESH)
            pl.semaphore_wait(barrier, 1)
        copies = [pltpu.make_async_remote_copy(
            src_ref=g_refs[a].at[2 * k + (1 - c_)], dst_ref=p_refs[a].at[k], send_sem=send_sems.at[4 * a + k],
            recv_sem=recv_sems.at[4 * a + k], device_id=(x_, y_, 1 - c_), device_id_type=MESH)
            for a in range(n) for k in range(4)]
        for cp in copies:
            cp.start()
        for cp in copies:
            cp.wait()

    out_shape = [jax.ShapeDtypeStruct((4,) + g.shape[1:], g.dtype) for g in gs]
    sems = [pltpu.SemaphoreType.DMA((4 * n,)), pltpu.SemaphoreType.DMA((4 * n,))]
    if sequencer_id is not None:
        return _on_sequencer(body, out_shape, sems, gs, after, sequencer_id, name)
    return pl.pallas_call(
        body, name=name, out_shape=out_shape, in_specs=[_ANY] * n, out_specs=[_ANY] * n, scratch_shapes=sems)(*gs)


def _slab_tiles(rows, cols):
    if rows % 8 == 0:
        return _row_tile(rows), cols
    return rows, 2 * LANE


def _pair_sums(g, p, route, name):
    _, rows, cols = g.shape
    tr, tc = _slab_tiles(rows, cols)

    def body(route_ref, g_ref, p_ref, t_ref):
        t_ref[...] = (g_ref[...] + p_ref[...]).astype(BF16)

    return pl.pallas_call(
        body, name=name, out_shape=jax.ShapeDtypeStruct((3, rows, cols), BF16),
        grid_spec=pltpu.PrefetchScalarGridSpec(
            num_scalar_prefetch=1, grid=(3, rows // tr, cols // tc),
            in_specs=[pl.BlockSpec((1, tr, tc), lambda r, i, j, route_ref: (2 * route_ref[1 + r] + route_ref[0], i, j)),
                      pl.BlockSpec((1, tr, tc), lambda r, i, j, route_ref: (route_ref[1 + r], i, j))],
            out_specs=pl.BlockSpec((1, tr, tc), lambda r, i, j, route_ref: (r, i, j))),
        compiler_params=_params(("parallel", "parallel", "parallel")),
    )(route, g, p)


def _chip_exchange(ts, name, sequencer_id=None, after=None):
    n = len(ts)

    def body(*refs):
        t_refs, l_refs = refs[:n], refs[n:2 * n]
        send_sems, recv_sems = refs[2 * n:]
        x_, y_, c_ = lax.axis_index("x"), lax.axis_index("y"), lax.axis_index("c")
        chips = [(1 - x_, y_), (x_, 1 - y_), (1 - x_, 1 - y_)]
        if sequencer_id is not None:
            barrier = pltpu.get_barrier_semaphore()
            for px, py in chips:
                pl.semaphore_signal(barrier, inc=1, device_id=(px, py, c_), device_id_type=MESH)
            pl.semaphore_wait(barrier, len(chips))
        copies = [pltpu.make_async_remote_copy(
            src_ref=t_refs[a].at[r], dst_ref=l_refs[a].at[r], send_sem=send_sems.at[3 * a + r],
            recv_sem=recv_sems.at[3 * a + r], device_id=(px, py, c_), device_id_type=MESH)
            for a in range(n) for r, (px, py) in enumerate(chips)]
        for cp in copies:
            cp.start()
        for cp in copies:
            cp.wait()

    out_shape = [jax.ShapeDtypeStruct((3,) + t.shape[1:], t.dtype) for t in ts]
    sems = [pltpu.SemaphoreType.DMA((3 * n,)), pltpu.SemaphoreType.DMA((3 * n,))]
    if sequencer_id is not None:
        return _on_sequencer(body, out_shape, sems, ts, after, sequencer_id, name)
    return pl.pallas_call(
        body, name=name, out_shape=out_shape, in_specs=[_ANY] * n, out_specs=[_ANY] * n, scratch_shapes=sems)(*ts)


def _reduce_adamw(gs, ps, landed, place, w, m, v, name):
    layers, rows, cols = w.shape
    assert layers == DEPTH == 2
    tr, tc = _slab_tiles(rows, cols)
    nr, nc = rows // tr, cols // tc
    spec = pl.BlockSpec((1, tr, tc), lambda l, i, j, place_ref: (l, i, j))

    def own(layer, which):
        pi, pj = (nr - 1, nc - 1) if layer == 0 else (0, 0)

        def index(l, i, j, place_ref):
            lead = 0 if which is None else place_ref[which]
            return lead, jnp.where(l == layer, i, pi), jnp.where(l == layer, j, pj)

        return pl.BlockSpec((3 if which is None else 1, tr, tc), index)

    def body(place_ref, g0_ref, p0_ref, l0_ref, g1_ref, p1_ref, l1_ref, w_ref, m_ref, v_ref,
             g_ref, d_ref, nm_ref, nv_ref):
        def update(own_ref, sib_ref, l_ref):
            g = own_ref[0] + sib_ref[0] + l_ref[0].astype(F32) + l_ref[1].astype(F32) + l_ref[2].astype(F32)
            g_ref[0] = g
            d_ref[0], nm_ref[0], nv_ref[0] = _adamw_math(w_ref[0], g, m_ref[0], v_ref[0])

        @pl.when(pl.program_id(0) == 0)
        def _():
            update(g0_ref, p0_ref, l0_ref)

        @pl.when(pl.program_id(0) == 1)
        def _():
            update(g1_ref, p1_ref, l1_ref)

    out = jax.ShapeDtypeStruct(w.shape, F32)
    return pl.pallas_call(
        body, name=name, out_shape=(out, out, out, out),
        grid_spec=pltpu.PrefetchScalarGridSpec(
            num_scalar_prefetch=1, grid=(DEPTH, nr, nc),
            in_specs=[own(0, 0), own(0, 1), own(0, None), own(1, 0), own(1, 1), own(1, None), spec, spec, spec],
            out_specs=(spec, spec, spec, spec)),
        compiler_params=_params(("arbitrary", "arbitrary", "arbitrary")),
    )(place, gs[0], ps[0], landed[0], gs[1], ps[1], landed[1], w, m, v)


def _pack(pieces, row_multiple, dtype, cols=D, rows=None):
    flat = jnp.concatenate([p.astype(dtype).reshape(-1) for p in pieces])
    if rows is None:
        rows = -(-flat.shape[0] // cols)
        rows = -(-rows // row_multiple) * row_multiple
    flat = jnp.pad(flat, (0, rows * cols - flat.shape[0]))
    return flat.reshape(rows, cols)


def _unpack(flat, shapes, lead=()):
    out, off = [], 0
    for shp in shapes:
        n = 1
        for s_ in shp:
            n *= s_
        out.append(lax.slice_in_dim(flat, off, off + n, axis=len(lead)).reshape(lead + tuple(shp)))
        off += n
    return out


def _z_rows_from_in(wt):
    pad = jnp.zeros((NZ - IN_COLS, wt.shape[1]), wt.dtype)
    return jnp.concatenate([wt[1544:2568], wt[2568:5640], wt[0:1536], wt[1536:1544], pad], axis=0)


def _in_rows_from_z(wt):
    return jnp.concatenate([wt[Z_Q:Z_Q + 1536], wt[Z_F:Z_F + 8], wt[Z_PC:Z_PC + 1024], wt[Z_G:Z_G + 3072]], axis=0)


def _pad_rows(v, rows=8):
    return jnp.pad(v, ((0, rows - v.shape[0]), (0, 0)))


def _layer_fwd(l, x, wts, gvec, mod):
    tag = f"l{l}"
    h = _prenorm_fwd(x, gvec, mod, 0, 0, 1, f"prenorm_mix_{tag}")
    z = _matmul(h, wts["w_in_t"], "nt", f"in_proj_{tag}", tn=1152)
    qa, ka, va, kat = _attn_prep(z, wts["b_f"], f"attn_prep_{tag}")
    o, lse = _attn_fwd(qa, ka, va, f"attn_{tag}")
    br_b = _pool_fwd(z, wts["wp_bd"], wts["pool_scale"], f"pool_{tag}")
    br_c = _conv_fwd(z, wts["conv_w"], f"conv_{tag}")
    pa = _matmul(o, wts["wa"], "nn", f"proj_a_{tag}")
    pb = _matmul(br_b, wts["wb"], "nn", f"proj_b_{tag}")
    pc = _matmul(br_c, wts["wc"], "nn", f"proj_c_{tag}")
    merged = _merge_fwd(z, pa, pb, pc, f"merge_{tag}")
    y = _matmul(merged, wts["w_out"], "nn", f"out_proj_{tag}")
    x1 = _postnorm_fwd(x, y, gvec, mod, 1, 2, f"postnorm_mix_{tag}")
    h2 = _prenorm_fwd(x1, gvec, mod, 2, 3, 4, f"prenorm_ff_{tag}")
    a = _matmul(h2, wts["w_ff1"], "nn", f"ff1_{tag}", b_col_shards=True)
    r = _relu2_fwd(a, f"relu2_{tag}")
    y2 = _matmul(r, wts["w_ff2"], "nn", f"ff2_{tag}")
    x2 = _postnorm_fwd(x1, y2, gvec, mod, 3, 5, f"postnorm_ff_{tag}")
    saved = dict(x=x, h=h, z=z, qa=qa, ka=ka, va=va, kat=kat, o=o, lse=lse, br_b=br_b, br_c=br_c, pa=pa, pb=pb, pc=pc,
                 merged=merged, y=y, x1=x1, h2=h2, a=a, r=r, y2=y2)
    return x2, saved


def _ffn_bwd(l, dx2, sv, wts, gvec, mod, midpoint):
    tag = f"l{l}"
    dy2, red_post_ff = _postnorm_bwd(sv["y2"], gvec, mod, dx2, 3, 5, f"postnorm_ff_bwd_{tag}")
    dr = _matmul(dy2, wts["w_ff2"], "nt", f"ff2_dx_{tag}")
    d_w_ff2 = _matmul(sv["r"], dy2, "tn", f"ff2_dw_{tag}")
    da = _relu2_bwd(sv["a"], midpoint(dr), f"relu2_bwd_{tag}")
    dh2 = _matmul(da, wts["w_ff1"], "nt", f"ff1_dx_{tag}", b_col_shards=True)
    d_w_ff1 = _matmul(sv["h2"], da, "tn", f"ff1_dw_{tag}", out_col_shards=True)
    dx1, red_pre_ff = _prenorm_bwd(sv["x1"], gvec, mod, dh2, dx2, 2, 4, f"prenorm_ff_bwd_{tag}")
    return dx1, [d_w_ff1, d_w_ff2.reshape(N_DEV, D_FF // N_DEV, D)], (red_pre_ff, red_post_ff)


def _mixer_bwd(l, dx1, sv, wts, gvec, mod, ffn_reds, midpoint):
    tag = f"l{l}"
    red_pre_ff, red_post_ff = ffn_reds
    dy, red_post_mix = _postnorm_bwd(sv["y"], gvec, mod, dx1, 1, 2, f"postnorm_mix_bwd_{tag}")
    dmerged = _matmul(dy, wts["w_out"], "nt", f"out_proj_dx_{tag}")
    d_w_out = _matmul(sv["merged"], dy, "tn", f"out_proj_dw_{tag}")
    dmerged = midpoint(dmerged)
    dpa, dpb, dpc, dgl = _merge_bwd(sv["z"], sv["pa"], sv["pb"], sv["pc"], dmerged, f"merge_bwd_{tag}")
    do = _matmul(dpa, wts["wa"], "nt", f"proj_a_dx_{tag}")
    dbr_b = _matmul(dpb, wts["wb"], "nt", f"proj_b_dx_{tag}")
    dbr_c = _matmul(dpc, wts["wc"], "nt", f"proj_c_dx_{tag}")
    d_wa = _matmul(sv["o"], dpa, "tn", f"proj_a_dw_{tag}")
    d_wb = _matmul(sv["br_b"], dpb, "tn", f"proj_b_dw_{tag}")
    d_wc = _matmul(sv["br_c"], dpc, "tn", f"proj_c_dw_{tag}")
    d_w_branch = jnp.concatenate([d_wa, d_wb, d_wc], axis=0)

    dpu, d_wp_bd, red_pool = _pool_bwd(sv["z"], wts["wp_bd"], wts["pool_scale"], dbr_b, f"pool_bwd_{tag}")
    dconv, red_conv = _conv_bwd(sv["z"], wts["conv_w"], dbr_c, f"conv_bwd_{tag}")
    qa2, doa = _attn_bwd_prep(sv["qa"], sv["o"], sv["lse"], do, f"attn_bwd_prep_{tag}")
    dqt, dka, dva = _attn_bwd(qa2, sv["ka"], sv["va"], sv["kat"], doa, f"attn_bwd_{tag}")
    dq, dk, dv, dfl, red_f = _attn_bwd_post(sv["z"], wts["b_f"], dqt, dka, dva, f"attn_bwd_post_{tag}")
    dz = jnp.concatenate([dpu, dconv, dgl, dq, dk, dv, dfl], axis=1)
    dh = _matmul(dz, wts["w_in_t"], "nn", f"in_proj_dx_{tag}", tk=1152)
    d_w_in_t = _matmul(dz, sv["h"], "tn", f"in_proj_dw_{tag}", tm=1152)
    dx0, red_pre_mix = _prenorm_bwd(sv["x"], gvec, mod, dh, dx1, 0, 1, f"prenorm_mix_bwd_{tag}")

    rows = D // N_DEV
    big = [_in_rows_from_z(d_w_in_t).reshape(N_DEV, IN_SHARD, D), d_w_branch.reshape(N_DEV, rows, D),
           d_w_out.reshape(N_DEV, rows, D)]
    d_w_pool = jnp.stack([d_wp_bd[64 * g:64 * (g + 1), 64 * g:64 * (g + 1)] for g in range(4)])
    small = dict(
        mod=jnp.stack([red_pre_mix[0], red_pre_mix[1], red_post_mix[0], red_pre_ff[0], red_pre_ff[1], red_post_ff[0]]),
        g_mix_pre=red_pre_mix[2], g_mix_post=red_post_mix[1], g_ff_pre=red_pre_ff[2], g_ff_post=red_post_ff[1],
        b_f=red_f[0, 0:8], w_pool=d_w_pool, pool_scale=red_pool[0], conv_w=red_conv[0:3])
    return dx0, big, small


SMALL_KEYS = ["mod", "g_mix_pre", "g_mix_post", "g_ff_pre", "g_ff_post", "b_f", "w_pool", "pool_scale", "conv_w"]
SMALL_SHAPES = [(DEPTH, 6 * D), (DEPTH, D), (DEPTH, D), (DEPTH, D), (DEPTH, D), (DEPTH, 8), (DEPTH, 4, 64, 64),
                (DEPTH, POOL_W), (DEPTH, 3, CONV_W)]


def kernel(x, c, w_ada, b_ada, g_mix_pre, g_mix_post, g_ff_pre, g_ff_post, w_in, b_f, w_pool, pool_scale, conv_w, w_branch, w_out, w_ff1, w_ff2, loss_target, m_w_ada, m_b_ada, m_g_mix_pre, m_g_mix_post, m_g_ff_pre, m_g_ff_post, m_w_in, m_b_f, m_w_pool, m_pool_scale, m_conv_w, m_w_branch, m_w_out, m_w_ff1, m_w_ff2, v_w_ada, v_b_ada, v_g_mix_pre, v_g_mix_post, v_g_ff_pre, v_g_ff_post, v_w_in, v_b_f, v_w_pool, v_pool_scale, v_conv_w, v_w_branch, v_w_out, v_w_ff1, v_w_ff2):
    ix, iy, ic = lax.axis_index("x"), lax.axis_index("y"), lax.axis_index("c")
    me = 4 * ix + 2 * iy + ic
    route = jnp.stack([ic, 2 * (1 - ix) + iy, 2 * ix + (1 - iy), 2 * (1 - ix) + (1 - iy)]).astype(jnp.int32)
    place = jnp.stack([me, 2 * ix + iy]).astype(jnp.int32)
    wt_in, mt_in, vt_in = (jnp.transpose(a, (0, 2, 1)) for a in (w_in, m_w_in, v_w_in))

    c_all = _all_gather([_pad_rows(c)], "gather_c")[0][:, 0, :]
    c_pad = _pad_rows(c_all, ADA_ROWS)
    b_cols = lax.dynamic_slice_in_dim(b_ada, me * ADA_COLS, ADA_COLS, axis=1)
    b_cols = jnp.broadcast_to(b_cols[:, None, :], (DEPTH, 8, ADA_COLS))
    mod_part = _ada_fwd(c_pad, w_ada, b_cols, "ada_fwd")
    mod_all = _all_gather([mod_part.reshape(DEPTH * ADA_ROWS, ADA_COLS)], "gather_mod")[0]
    mod_all = mod_all.reshape(N_DEV, DEPTH, ADA_ROWS, ADA_COLS)
    mod_mine = lax.dynamic_index_in_dim(mod_all, me, axis=2, keepdims=False)
    mod_mine = jnp.transpose(mod_mine, (1, 0, 2)).reshape(DEPTH, 6, D)

    cw_cols = CONV_W // N_DEV
    cw_send = jnp.pad(conv_w.reshape(DEPTH * 3, cw_cols), ((0, 8 - DEPTH * 3), (0, LANE - cw_cols)))
    gathered = []
    for l in range(DEPTH):
        send = [w[l].astype(BF16) for w in (wt_in, w_branch, w_out, w_ff1, w_ff2)] + ([cw_send] if l == 0 else [])
        gathered.append(_all_gather(send, f"gather_weights_l{l}", sequencer_id=1 + l))
    cw_all = gathered[0][-1][:, :DEPTH * 3, :cw_cols].reshape(N_DEV, DEPTH, 3, cw_cols)

    def layer_operands(l, weights):
        p_in, p_br, p_out, p_ff1, p_ff2 = weights[:5]
        w_br_full = p_br.reshape(D, D)
        cw_full = jnp.transpose(cw_all[:, l], (1, 0, 2)).reshape(3, CONV_W)
        wp_bd = jnp.zeros((POOL_W, POOL_W), F32)
        for g in range(4):
            wp_bd = wp_bd.at[64 * g:64 * (g + 1), 64 * g:64 * (g + 1)].set(w_pool[l, g])
        wts = dict(
            w_in_t=_z_rows_from_in(p_in.reshape(IN_COLS, D)), wa=w_br_full[0:A_WIDTH], wb=w_br_full[A_WIDTH:A_WIDTH + POOL_W],
            wc=w_br_full[A_WIDTH + POOL_W:], w_out=p_out.reshape(D, D),
            w_ff1=p_ff1, w_ff2=p_ff2.reshape(D_FF, D),
            conv_w=_pad_rows(cw_full), wp_bd=wp_bd.astype(BF16), pool_scale=_pad_rows(pool_scale[l][None, :]),
            b_f=_pad_rows(jnp.pad(b_f[l], (0, LANE - 8))[None, :]))
        gvec = _pad_rows(jnp.stack([g_mix_pre[l], g_mix_post[l], g_ff_pre[l], g_ff_post[l]]))
        return wts, gvec, _pad_rows(mod_mine[l])

    xs = x[0]
    saved, layers = [], []
    for l in range(DEPTH):
        weights = gathered[l]
        if l > 0:
            xs, weights = lax.optimization_barrier((xs, weights))
        layers.append(layer_operands(l, weights))
        xs, sv = _layer_fwd(l, xs, *layers[l])
        saved.append(sv)
    dx, loss_part = _loss_head(xs, loss_target[0], "loss_head")
    loss = lax.psum(loss_part[0, 0], ("x", "y", "c"))
    small_grads = [None] * DEPTH
    mine, sibs, landed = ({} for _ in range(3))
    seq_id = iter(range(3, 3 + 4 * DEPTH))
    last = [gathered[DEPTH - 1][0]]

    def start(group, grads):
        mine[group] = grads
        sibs[group] = _sibling_exchange(grads, f"rs_sibling_{group}", sequencer_id=next(seq_id), after=last[0])
        last[0] = sibs[group][0]

    def finish(group, later):
        later, (grads, sib) = lax.optimization_barrier((later, (mine[group], sibs[group])))
        sends = [_pair_sums(g, p, route, f"rs_pair_sums_{group}_{k}") for k, (g, p) in enumerate(zip(grads, sib))]
        later, sends = lax.optimization_barrier((later, sends))
        landed[group] = _chip_exchange(sends, f"rs_chips_{group}", sequencer_id=next(seq_id), after=last[0])
        last[0] = landed[group][0]
        return later

    pending = None
    for l in reversed(range(DEPTH)):
        hook = (lambda da: da) if pending is None else functools.partial(finish, pending)
        dx, ffn_grads, ffn_reds = _ffn_bwd(l, dx, saved[l], *layers[l], hook)
        start(f"ffn_l{l}", ffn_grads)
        dx, mix_grads, small_grads[l] = _mixer_bwd(l, dx, saved[l], *layers[l], ffn_reds,
                                                   functools.partial(finish, f"ffn_l{l}"))
        start(f"mix_l{l}", mix_grads)
        pending = f"mix_l{l}"
    grad_x = dx[None]

    big_w = [wt_in, w_branch, w_out, w_ff1, w_ff2]
    big_m = [mt_in, m_w_branch, m_w_out, m_w_ff1, m_w_ff2]
    big_v = [vt_in, v_w_branch, v_w_out, v_w_ff1, v_w_ff2]
    where = [("mix", 0), ("mix", 1), ("mix", 2), ("ffn", 0), ("ffn", 1)]

    def reduce_and_update(k):
        group, at = where[k]
        return _reduce_adamw([mine[f"{group}_l{l}"][at] for l in range(DEPTH)],
                             [sibs[f"{group}_l{l}"][at] for l in range(DEPTH)],
                             [landed[f"{group}_l{l}"][at] for l in range(DEPTH)], place, big_w[k], big_m[k], big_v[k],
                             f"rs_sum_adamw_{k}")

    big_res = {k: list(reduce_and_update(k)) for k in (3, 4)}
    big_res[3][0] = finish(pending, big_res[3][0])

    small = {k: jnp.stack([small_grads[l][k] for l in range(DEPTH)]) for k in SMALL_KEYS}
    small_all = _all_gather([_pack([small[k] for k in SMALL_KEYS], 8, F32)], "gather_small")[0]
    dmod_all = small_all[:, 0:DEPTH * 6, :].reshape(N_DEV, DEPTH, 6 * D)
    summed = _unpack(_sum_slabs(small_all, "sum_small").reshape(-1), SMALL_SHAPES)
    sg = dict(zip(SMALL_KEYS, summed))
    dmod_cols = lax.dynamic_slice_in_dim(dmod_all, me * ADA_COLS, ADA_COLS, axis=2)
    dmod_cols = jnp.pad(jnp.transpose(dmod_cols, (1, 0, 2)), ((0, 0), (0, ADA_ROWS - N_DEV), (0, 0)))
    g_w_ada = _ada_bwd(c_pad, dmod_cols, "ada_bwd")
    g_conv_w = lax.dynamic_slice_in_dim(sg["conv_w"], me * (CONV_W // N_DEV), CONV_W // N_DEV, axis=2)

    ada_out = [g_w_ada] + list(_adamw(w_ada, g_w_ada, m_w_ada, v_w_ada, "adamw_ada"))
    rest_w = [b_ada, g_mix_pre, g_mix_post, g_ff_pre, g_ff_post, b_f, w_pool, pool_scale, conv_w]
    rest_m = [m_b_ada, m_g_mix_pre, m_g_mix_post, m_g_ff_pre, m_g_ff_post, m_b_f, m_w_pool, m_pool_scale, m_conv_w]
    rest_v = [v_b_ada, v_g_mix_pre, v_g_mix_post, v_g_ff_pre, v_g_ff_post, v_b_f, v_w_pool, v_pool_scale, v_conv_w]
    rest_g = [sg["mod"], sg["g_mix_pre"], sg["g_mix_post"], sg["g_ff_pre"], sg["g_ff_post"], sg["b_f"],
              sg["w_pool"], sg["pool_scale"], g_conv_w]
    rest_shapes = [a.shape for a in rest_w]
    upd = _adamw(_pack(rest_w, 8, F32)[None], _pack(rest_g, 8, F32)[None], _pack(rest_m, 8, F32)[None],
                 _pack(rest_v, 8, F32)[None], "adamw_rest")
    rest_out = [rest_g] + [_unpack(arr.reshape(-1), rest_shapes) for arr in upd]
    rest_out = [[ada_out[which]] + rest_out[which] for which in range(4)]

    landed[pending], rest_out = lax.optimization_barrier((landed[pending], rest_out))
    big_res.update({k: reduce_and_update(k) for k in (0, 1, 2)})
    big_out = [[jnp.transpose(big_res[k][which], (0, 2, 1)) if k == 0 else big_res[k][which] for k in range(5)]
               for which in range(4)]

    def ordered(k):
        r, b = rest_out[k], big_out[k]
        return [r[0], r[1], r[2], r[3], r[4], r[5], b[0], r[6], r[7], r[8], r[9], b[1], b[2], b[3], b[4]]

    return (loss, grad_x, *ordered(0), *ordered(1), *ordered(2), *ordered(3))
```

```python
import functools

import jax
import jax.numpy as jnp
from jax import lax
from jax.experimental import pallas as pl
from jax.experimental.pallas import tpu as pltpu
from jax.experimental.pallas import tpu_sc as plsc

F32 = jnp.float32
BF16 = jnp.bfloat16

N_DEV = 8
D = 1024
S = 2048
DEPTH = 2
D_FF = 4 * D
A_WIDTH = 512
HEAD_DIM = 64
N_PAIR = 4
POOL_W = 256
CONV_W = 256
IN_COLS = 5640
ADA_COLS = 6 * D // N_DEV
IN_SHARD = IN_COLS // N_DEV
RMS_EPS = 1e-6
NEG_INF = -1e30
ATT_SCALE = HEAD_DIM ** -0.5

NZ = 5760
Z_PC = 0
Z_G = 1024
Z_Q = 4096
Z_K = 4608
Z_V = 5120
Z_F = 5632

LR, B1, B2, EPS, WD, STEP = 0.001, 0.9, 0.999, 1e-08, 0.01, 10

LANE = 128
VMEM_LIMIT_BYTES = 48 * 1024 * 1024
TS = 256
TQ = 256
TQ_FWD = 512


def _params(sem=None):
    return pltpu.CompilerParams(dimension_semantics=sem, vmem_limit_bytes=VMEM_LIMIT_BYTES)


def _pick(n, target):
    best = None
    for t in range(LANE, min(n, target) + 1, LANE):
        if n % t == 0:
            best = t
    return n if best is None else best


def _matmul(a, b, mode, name, out_dtype=F32, tm=1024, tn=1024, tk=1024, b_col_shards=False, out_col_shards=False):
    if b_col_shards:
        shards, b_rows, shard_cols = b.shape
        b_shape = (b_rows, shards * shard_cols)
    else:
        b_shape = b.shape
    if mode == "nn":
        (m, k), (k2, n) = a.shape, b_shape
    elif mode == "nt":
        (m, k), (n, k2) = a.shape, b_shape
    else:
        (k, m), (k2, n) = a.shape, b_shape
    assert k == k2, (a.shape, b.shape, mode)
    tm, tn, tk = _pick(m, tm), _pick(n, tn), _pick(k, tk)
    if b_col_shards and mode == "nn":
        tn = shard_cols
    if b_col_shards and mode == "nt":
        tk = shard_cols
    if out_col_shards:
        tn = n // N_DEV
    nk = k // tk
    if mode == "nn":
        a_spec = pl.BlockSpec((tm, tk), lambda i, j, kk: (i, kk))
        b_spec = (pl.BlockSpec((None, tk, tn), lambda i, j, kk: (j, kk, 0)) if b_col_shards else
                  pl.BlockSpec((tk, tn), lambda i, j, kk: (kk, j)))
        dims = (((1,), (0,)), ((), ()))
    elif mode == "nt":
        a_spec = pl.BlockSpec((tm, tk), lambda i, j, kk: (i, kk))
        b_spec = (pl.BlockSpec((None, tn, tk), lambda i, j, kk: (kk, j, 0)) if b_col_shards else
                  pl.BlockSpec((tn, tk), lambda i, j, kk: (j, kk)))
        dims = (((1,), (1,)), ((), ()))
    else:
        assert not b_col_shards
        a_spec = pl.BlockSpec((tk, tm), lambda i, j, kk: (kk, i))
        b_spec = pl.BlockSpec((tk, tn), lambda i, j, kk: (kk, j))
        dims = (((0,), (0,)), ((), ()))
    if out_col_shards:
        out_shape = jax.ShapeDtypeStruct((N_DEV, m, tn), out_dtype)
        out_spec = pl.BlockSpec((None, tm, tn), lambda i, j, kk: (j, i, 0))
    else:
        out_shape = jax.ShapeDtypeStruct((m, n), out_dtype)
        out_spec = pl.BlockSpec((tm, tn), lambda i, j, kk: (i, j))

    def product(a_ref, b_ref):
        return lax.dot_general(a_ref[...].astype(BF16), b_ref[...].astype(BF16), dims, preferred_element_type=F32)

    def body_one_pass(a_ref, b_ref, o_ref):
        o_ref[...] = product(a_ref, b_ref).astype(out_dtype)

    def body(a_ref, b_ref, o_ref, acc_ref):
        kk = pl.program_id(2)

        @pl.when(kk == 0)
        def _():
            acc_ref[...] = product(a_ref, b_ref)

        @pl.when(kk > 0)
        def _():
            acc_ref[...] += product(a_ref, b_ref)

        @pl.when(kk == nk - 1)
        def _():
            o_ref[...] = acc_ref[...].astype(out_dtype)

    return pl.pallas_call(
        body_one_pass if nk == 1 else body, name=name,
        out_shape=out_shape,
        grid=(m // tm, n // tn, nk),
        in_specs=[a_spec, b_spec],
        out_specs=out_spec,
        scratch_shapes=[] if nk == 1 else [pltpu.VMEM((tm, tn), F32)],
        compiler_params=_params(("parallel", "parallel", "arbitrary")),
    )(a, b)


def _row_spec(width=D, col=0):
    return pl.BlockSpec((TS, width), lambda i: (i, col))


def _vec_spec(rows=8, width=D):
    return pl.BlockSpec((rows, width), lambda i: (0, 0))


def _rms(x):
    return lax.rsqrt(jnp.mean(x * x, axis=-1, keepdims=True) + RMS_EPS)


def _prenorm_fwd(x, gvec, mod, g_row, shift_row, scale_row, name):
    def body(x_ref, g_ref, mod_ref, h_ref):
        xv = x_ref[...]
        y = xv * _rms(xv) * g_ref[g_row:g_row + 1, :]
        h = y * (1.0 + mod_ref[scale_row:scale_row + 1, :]) + mod_ref[shift_row:shift_row + 1, :]
        h_ref[...] = h.astype(BF16)

    return pl.pallas_call(
        body, name=name, out_shape=jax.ShapeDtypeStruct((S, D), BF16), grid=(S // TS,),
        in_specs=[_row_spec(), _vec_spec(), _vec_spec()], out_specs=_row_spec(),
        compiler_params=_params(("parallel",)),
    )(x, gvec, mod)


def _prenorm_bwd(x, gvec, mod, dh, dres, g_row, scale_row, name):
    def body(x_ref, g_ref, mod_ref, dh_ref, dres_ref, dx_ref, red_ref):
        i = pl.program_id(0)

        @pl.when(i == 0)
        def _():
            red_ref[...] = jnp.zeros_like(red_ref)

        xv = x_ref[...]
        g = g_ref[g_row:g_row + 1, :]
        r = _rms(xv)
        n = xv * r
        yg = n * g
        dhv = dh_ref[...]
        dyg = dhv * (1.0 + mod_ref[scale_row:scale_row + 1, :])
        dn = dyg * g
        dx = r * (dn - n * jnp.mean(dn * n, axis=-1, keepdims=True))
        dx_ref[...] = dres_ref[...] + dx
        red_ref[0:1, :] += jnp.sum(dhv, axis=0, keepdims=True)
        red_ref[1:2, :] += jnp.sum(dhv * yg, axis=0, keepdims=True)
        red_ref[2:3, :] += jnp.sum(dyg * n, axis=0, keepdims=True)

    return pl.pallas_call(
        body, name=name,
        out_shape=(jax.ShapeDtypeStruct((S, D), F32), jax.ShapeDtypeStruct((8, D), F32)),
        grid=(S // TS,),
        in_specs=[_row_spec(), _vec_spec(), _vec_spec(), _row_spec(), _row_spec()],
        out_specs=(_row_spec(), _vec_spec()),
        compiler_params=_params(("arbitrary",)),
    )(x, gvec, mod, dh, dres)


def _postnorm_fwd(x, y, gvec, mod, g_row, gate_row, name):
    def body(x_ref, y_ref, g_ref, mod_ref, o_ref):
        yv = y_ref[...]
        yn = yv * _rms(yv) * g_ref[g_row:g_row + 1, :]
        o_ref[...] = x_ref[...] + mod_ref[gate_row:gate_row + 1, :] * yn

    return pl.pallas_call(
        body, name=name, out_shape=jax.ShapeDtypeStruct((S, D), F32), grid=(S // TS,),
        in_specs=[_row_spec(), _row_spec(), _vec_spec(), _vec_spec()], out_specs=_row_spec(),
        compiler_params=_params(("parallel",)),
    )(x, y, gvec, mod)


def _postnorm_bwd(y, gvec, mod, dxo, g_row, gate_row, name):
    def body(y_ref, g_ref, mod_ref, dxo_ref, dy_ref, red_ref):
        i = pl.program_id(0)

        @pl.when(i == 0)
        def _():
            red_ref[...] = jnp.zeros_like(red_ref)

        yv = y_ref[...]
        g = g_ref[g_row:g_row + 1, :]
        r = _rms(yv)
        n = yv * r
        dxo = dxo_ref[...]
        dyn = dxo * mod_ref[gate_row:gate_row + 1, :]
        dn = dyn * g
        dy = r * (dn - n * jnp.mean(dn * n, axis=-1, keepdims=True))
        dy_ref[...] = dy.astype(BF16)
        red_ref[0:1, :] += jnp.sum(dxo * (n * g), axis=0, keepdims=True)
        red_ref[1:2, :] += jnp.sum(dyn * n, axis=0, keepdims=True)

    return pl.pallas_call(
        body, name=name,
        out_shape=(jax.ShapeDtypeStruct((S, D), BF16), jax.ShapeDtypeStruct((8, D), F32)),
        grid=(S // TS,),
        in_specs=[_row_spec(), _vec_spec(), _vec_spec(), _row_spec()],
        out_specs=(_row_spec(), _vec_spec()),
        compiler_params=_params(("arbitrary",)),
    )(y, gvec, mod, dxo)


def _loss_head(xf, target, name):
    def body(x_ref, t_ref, dx_ref, loss_ref):
        i = pl.program_id(0)

        @pl.when(i == 0)
        def _():
            loss_ref[...] = jnp.zeros_like(loss_ref)

        e = x_ref[...] - t_ref[...]
        dx_ref[...] = e / float(D)
        per_tok = jnp.mean(e * e, axis=-1, keepdims=True)
        loss_ref[0:1, 0:1] += 0.5 * jnp.sum(per_tok, axis=0, keepdims=True)

    return pl.pallas_call(
        body, name=name,
        out_shape=(jax.ShapeDtypeStruct((S, D), F32), jax.ShapeDtypeStruct((8, LANE), F32)),
        grid=(S // TS,),
        in_specs=[_row_spec(), _row_spec()],
        out_specs=(_row_spec(), pl.BlockSpec((8, LANE), lambda i: (0, 0))),
        compiler_params=_params(("arbitrary",)),
    )(xf, target)


def _relu2_fwd(a, name):
    def body(a_ref, r_ref):
        t = jnp.maximum(a_ref[...], 0.0)
        r_ref[...] = (t * t).astype(BF16)

    return pl.pallas_call(
        body, name=name, out_shape=jax.ShapeDtypeStruct((S, D_FF), BF16), grid=(S // TS,),
        in_specs=[_row_spec(D_FF)], out_specs=_row_spec(D_FF),
        compiler_params=_params(("parallel",)),
    )(a)


def _relu2_bwd(a, dr, name):
    def body(a_ref, dr_ref, da_ref):
        da_ref[...] = (dr_ref[...] * (2.0 * jnp.maximum(a_ref[...], 0.0))).astype(BF16)

    return pl.pallas_call(
        body, name=name, out_shape=jax.ShapeDtypeStruct((S, D_FF), BF16), grid=(S // TS,),
        in_specs=[_row_spec(D_FF), _row_spec(D_FF)], out_specs=_row_spec(D_FF),
        compiler_params=_params(("parallel",)),
    )(a, dr)


def _merge_fwd(z, pa, pb, pc, name):
    def body(g0_ref, g1_ref, g2_ref, pa_ref, pb_ref, pc_ref, o_ref):
        m = (jax.nn.sigmoid(g0_ref[...]) * pa_ref[...] + jax.nn.sigmoid(g1_ref[...]) * pb_ref[...]
             + jax.nn.sigmoid(g2_ref[...]) * pc_ref[...])
        o_ref[...] = m.astype(BF16)

    gb = Z_G // D
    return pl.pallas_call(
        body, name=name, out_shape=jax.ShapeDtypeStruct((S, D), BF16), grid=(S // TS,),
        in_specs=[_row_spec(D, gb), _row_spec(D, gb + 1), _row_spec(D, gb + 2), _row_spec(), _row_spec(), _row_spec()],
        out_specs=_row_spec(),
        compiler_params=_params(("parallel",)),
    )(z, z, z, pa, pb, pc)


def _merge_bwd(z, pa, pb, pc, dm, name):
    def body(g0_ref, g1_ref, g2_ref, pa_ref, pb_ref, pc_ref, dm_ref, da_ref, db_ref, dc_ref, dgl_ref):
        dmv = dm_ref[...]
        for k, (g_ref, p_ref, d_ref) in enumerate(((g0_ref, pa_ref, da_ref), (g1_ref, pb_ref, db_ref),
                                                   (g2_ref, pc_ref, dc_ref))):
            sg = jax.nn.sigmoid(g_ref[...])
            d_ref[...] = (dmv * sg).astype(BF16)
            dgl_ref[:, k * D:(k + 1) * D] = (dmv * p_ref[...] * (sg * (1.0 - sg))).astype(BF16)

    gb = Z_G // D
    proj = jax.ShapeDtypeStruct((S, D), BF16)
    return pl.pallas_call(
        body, name=name,
        out_shape=(proj, proj, proj, jax.ShapeDtypeStruct((S, 3 * D), BF16)),
        grid=(S // TS,),
        in_specs=[_row_spec(D, gb), _row_spec(D, gb + 1), _row_spec(D, gb + 2), _row_spec(), _row_spec(), _row_spec(),
                  _row_spec()],
        out_specs=(_row_spec(), _row_spec(), _row_spec(), _row_spec(3 * D)),
        compiler_params=_params(("parallel",)),
    )(z, z, z, pa, pb, pc, dm)


def _shift_down(x, k, row):
    return jnp.where(row >= k, pltpu.roll(x, k, axis=0), 0.0)


def _shift_up(x, k, row):
    n = x.shape[0]
    return jnp.where(row < n - k, pltpu.roll(x, n - k, axis=0), 0.0)


def _cumsum_rows(x, row, reverse=False):
    shift = _shift_up if reverse else _shift_down
    k = 1
    while k < x.shape[0]:
        x = x + shift(x, k, row)
        k *= 2
    return x


def _full_spec(shape, idx=(0, 0)):
    return pl.BlockSpec(shape, lambda i: idx)


def _pool_window_select(lane, a2, a4, a8, a16):
    return jnp.where(lane < 64, a2, jnp.where(lane < 128, a4, jnp.where(lane < 192, a8, a16)))


def _pool_p(u, row, lane):
    t2 = u + _shift_down(u, 1, row)
    t4 = t2 + _shift_down(t2, 2, row)
    t8 = t4 + _shift_down(t4, 4, row)
    t16 = t8 + _shift_down(t8, 8, row)
    tw = _pool_window_select(lane, t2, t4, t8, t16)
    cnt = jnp.minimum((row + 1).astype(F32), _pool_window_select(lane, 2.0, 4.0, 8.0, 16.0))
    return tw / cnt - u, cnt


def _pool_fwd(z, wp_bd, pscale, name):
    def body(u_ref, w_ref, s_ref, o_ref):
        row = lax.broadcasted_iota(jnp.int32, (S, POOL_W), 0)
        lane = lax.broadcasted_iota(jnp.int32, (S, POOL_W), 1)
        p, _ = _pool_p(u_ref[...], row, lane)
        y = jnp.dot(p.astype(BF16), w_ref[...], preferred_element_type=F32)
        o_ref[...] = y * s_ref[0:1, :]

    return pl.pallas_call(
        body, name=name, out_shape=jax.ShapeDtypeStruct((S, POOL_W), F32), grid=(1,),
        in_specs=[_full_spec((S, POOL_W), (0, Z_PC // POOL_W)), _full_spec((POOL_W, POOL_W)), _full_spec((8, POOL_W))],
        out_specs=_full_spec((S, POOL_W)),
        compiler_params=_params(("arbitrary",)),
    )(z, wp_bd, pscale)


def _pool_bwd(z, wp_bd, pscale, dbr, name):
    def body(u_ref, w_ref, s_ref, dbr_ref, du_ref, dw_ref, red_ref):
        row = lax.broadcasted_iota(jnp.int32, (S, POOL_W), 0)
        lane = lax.broadcasted_iota(jnp.int32, (S, POOL_W), 1)
        p, cnt = _pool_p(u_ref[...], row, lane)
        pb = p.astype(BF16)
        y = jnp.dot(pb, w_ref[...], preferred_element_type=F32)
        dbr = dbr_ref[...]
        red_ref[...] = jnp.zeros_like(red_ref)
        red_ref[0:1, :] = jnp.sum(dbr * y, axis=0, keepdims=True)
        dy = (dbr * s_ref[0:1, :]).astype(BF16)
        dw_ref[...] = lax.dot_general(pb, dy, (((0,), (0,)), ((), ())), preferred_element_type=F32)
        dp = lax.dot_general(dy, w_ref[...], (((1,), (1,)), ((), ())), preferred_element_type=F32)
        g = dp / cnt
        a2 = g + _shift_up(g, 1, row)
        a4 = a2 + _shift_up(a2, 2, row)
        a8 = a4 + _shift_up(a4, 4, row)
        a16 = a8 + _shift_up(a8, 8, row)
        du_ref[...] = (_pool_window_select(lane, a2, a4, a8, a16) - dp).astype(BF16)

    return pl.pallas_call(
        body, name=name,
        out_shape=(jax.ShapeDtypeStruct((S, POOL_W), BF16), jax.ShapeDtypeStruct((POOL_W, POOL_W), F32),
                   jax.ShapeDtypeStruct((8, POOL_W), F32)),
        grid=(1,),
        in_specs=[_full_spec((S, POOL_W), (0, Z_PC // POOL_W)), _full_spec((POOL_W, POOL_W)), _full_spec((8, POOL_W)),
                  _full_spec((S, POOL_W))],
        out_specs=(_full_spec((S, POOL_W)), _full_spec((POOL_W, POOL_W)), _full_spec((8, POOL_W))),
        compiler_params=_params(("arbitrary",)),
    )(z, wp_bd, pscale, dbr)


def _conv_specs():
    base = Z_PC // CONV_W
    return [_full_spec((S, CONV_W), (0, base + 1)), _full_spec((S, CONV_W), (0, base + 2)),
            _full_spec((S, CONV_W), (0, base + 3)), _full_spec((8, CONV_W))]


def _conv_fwd(z, cw, name):
    def body(h_ref, b_ref, c_ref, w_ref, o_ref):
        row = lax.broadcasted_iota(jnp.int32, (S, CONV_W), 0)
        u = c_ref[...] * h_ref[...]
        y = (w_ref[0:1, :] * _shift_down(u, 2, row) + w_ref[1:2, :] * _shift_down(u, 1, row) + w_ref[2:3, :] * u)
        o_ref[...] = b_ref[...] * y

    return pl.pallas_call(
        body, name=name, out_shape=jax.ShapeDtypeStruct((S, CONV_W), F32), grid=(1,),
        in_specs=_conv_specs(), out_specs=_full_spec((S, CONV_W)),
        compiler_params=_params(("arbitrary",)),
    )(z, z, z, cw)


def _conv_bwd(z, cw, dbr, name):
    def body(h_ref, b_ref, c_ref, w_ref, dbr_ref, d_ref, red_ref):
        row = lax.broadcasted_iota(jnp.int32, (S, CONV_W), 0)
        h, cg = h_ref[...], c_ref[...]
        u = cg * h
        u1 = _shift_down(u, 1, row)
        u2 = _shift_down(u, 2, row)
        y = w_ref[0:1, :] * u2 + w_ref[1:2, :] * u1 + w_ref[2:3, :] * u
        dbr = dbr_ref[...]
        dy = dbr * b_ref[...]
        du = w_ref[2:3, :] * dy + w_ref[1:2, :] * _shift_up(dy, 1, row) + w_ref[0:1, :] * _shift_up(dy, 2, row)
        d_ref[:, 0:CONV_W] = (du * cg).astype(BF16)
        d_ref[:, CONV_W:2 * CONV_W] = (dbr * y).astype(BF16)
        d_ref[:, 2 * CONV_W:3 * CONV_W] = (du * h).astype(BF16)
        red_ref[...] = jnp.zeros_like(red_ref)
        red_ref[0:1, :] = jnp.sum(dy * u2, axis=0, keepdims=True)
        red_ref[1:2, :] = jnp.sum(dy * u1, axis=0, keepdims=True)
        red_ref[2:3, :] = jnp.sum(dy * u, axis=0, keepdims=True)

    return pl.pallas_call(
        body, name=name,
        out_shape=(jax.ShapeDtypeStruct((S, 3 * CONV_W), BF16), jax.ShapeDtypeStruct((8, CONV_W), F32)),
        grid=(1,),
        in_specs=_conv_specs() + [_full_spec((S, CONV_W))],
        out_specs=(_full_spec((S, 3 * CONV_W)), _full_spec((8, CONV_W))),
        compiler_params=_params(("arbitrary",)),
    )(z, z, z, cw, dbr)


_NT = (((1,), (1,)), ((), ()))
_TN = (((0,), (0,)), ((), ()))
N_HEAD = 2 * N_PAIR


def _split3(x):
    hi = x.astype(BF16).astype(F32)
    mid = (x - hi).astype(BF16).astype(F32)
    lo = (x - hi - mid).astype(BF16).astype(F32)
    return hi, mid, lo


def _spare(lane, e, k):
    return lane == 64 * (1 - e) + k


def _spare3(lane, e, k):
    base = 64 * (1 - e) + k
    return (lane >= base) & (lane < base + 3)


def _put3(lane, e, k, pieces, rest):
    out = rest
    for n, piece in enumerate(pieces):
        out = jnp.where(_spare(lane, e, k + n), piece, out)
    return out


def _attn_prep(z, bf, name):
    def body(q_ref, k_ref, v_ref, f_ref, b_ref, qa_ref, ka_ref, va_ref, kat_ref):
        p = pl.program_id(0)
        row = lax.broadcasted_iota(jnp.int32, (S, LANE), 0)
        lane = lax.broadcasted_iota(jnp.int32, (S, LANE), 1)
        xv = f_ref[...] + b_ref[0:1, :]
        ls = jnp.minimum(xv, 0.0) - jnp.log(1.0 + jnp.exp(-jnp.abs(xv)))
        cum = _cumsum_rows(jnp.where(lane < N_HEAD, ls, 0.0), row)
        q, k, v = q_ref[...], k_ref[...], v_ref[...]
        for e in range(2):
            head = (lane >= 64) if e else (lane < 64)
            f = jnp.sum(jnp.where(lane == 2 * p + e, cum, 0.0), axis=1, keepdims=True)
            pieces = _split3(f)
            qa = jnp.where(head, q * ATT_SCALE, _put3(lane, e, 0, pieces, jnp.where(_spare3(lane, e, 3), 1.0, 0.0)))
            ones = jnp.where(_spare3(lane, e, 0) | _spare3(lane, e, 6), 1.0, 0.0)
            ka = jnp.where(head, k, _put3(lane, e, 3, [-x for x in pieces], ones))
            va = jnp.where(head, v, jnp.where(_spare3(lane, e, 0), 1.0, 0.0))
            qa_ref[e] = qa.astype(BF16)
            ka_ref[e] = ka.astype(BF16)
            va_ref[e] = va.astype(BF16)
            kat_ref[e] = ka.T.astype(BF16)

    qb, kb, vb = Z_Q // LANE, Z_K // LANE, Z_V // LANE
    heads = jax.ShapeDtypeStruct((N_HEAD, S, LANE), BF16)
    pair = pl.BlockSpec((2, S, LANE), lambda p: (p, 0, 0))
    return pl.pallas_call(
        body, name=name,
        out_shape=(heads, heads, heads, jax.ShapeDtypeStruct((N_HEAD, LANE, S), BF16)),
        grid=(N_PAIR,),
        in_specs=[pl.BlockSpec((S, LANE), lambda p: (0, qb + p)), pl.BlockSpec((S, LANE), lambda p: (0, kb + p)),
                  pl.BlockSpec((S, LANE), lambda p: (0, vb + p)), pl.BlockSpec((S, LANE), lambda p: (0, Z_F // LANE)),
                  pl.BlockSpec((8, LANE), lambda p: (0, 0))],
        out_specs=(pair, pair, pair, pl.BlockSpec((2, LANE, S), lambda p: (p, 0, 0))),
        compiler_params=_params(("parallel",)),
    )(z, z, z, z, bf)


def _attn_bwd_prep(qa, o, lse, do, name):
    def body(qa_ref, o_ref, lse_ref, do_ref, qa2_ref, doa_ref):
        lane = lax.broadcasted_iota(jnp.int32, (S, LANE), 1)
        dov, ov, lsev = do_ref[...], o_ref[...], lse_ref[...]
        for e in range(2):
            head = (lane >= 64) if e else (lane < 64)
            dsum = jnp.sum(jnp.where(head, dov * ov, 0.0), axis=1, keepdims=True)
            doa_ref[e] = jnp.where(head, dov, _put3(lane, e, 0, [-x for x in _split3(dsum)], 0.0)).astype(BF16)
            lse_col = lsev[:, 64 * e:64 * e + 1]
            qa2_ref[e] = _put3(lane, e, 6, [-x for x in _split3(lse_col)], qa_ref[e].astype(F32)).astype(BF16)

    heads = jax.ShapeDtypeStruct((N_HEAD, S, LANE), BF16)
    pair = pl.BlockSpec((2, S, LANE), lambda p: (p, 0, 0))
    cols = pl.BlockSpec((S, LANE), lambda p: (0, p))
    return pl.pallas_call(
        body, name=name, out_shape=(heads, heads), grid=(N_PAIR,),
        in_specs=[pair, cols, cols, cols], out_specs=(pair, pair),
        compiler_params=_params(("parallel",)),
    )(qa, o, lse, do)


def _attn_bwd_post(z, bf, dqt, dka, dva, name):
    def body(f_ref, b_ref, dqt_ref, dk_ref, dv_ref, dq_out, dk_out, dv_out, dfl_ref, red_ref, dcum_ref):
        p = pl.program_id(0)

        @pl.when(p == 0)
        def _():
            dcum_ref[...] = jnp.zeros_like(dcum_ref)

        row = lax.broadcasted_iota(jnp.int32, (S, LANE), 0)
        lane = lax.broadcasted_iota(jnp.int32, (S, LANE), 1)
        dqa = [dqt_ref[e].T for e in range(2)]
        dq_out[...] = (jnp.where(lane < 64, dqa[0], dqa[1]) * ATT_SCALE).astype(BF16)
        dk_out[...] = jnp.where(lane < 64, dk_ref[0], dk_ref[1]).astype(BF16)
        dv_out[...] = jnp.where(lane < 64, dv_ref[0], dv_ref[1]).astype(BF16)
        for e in range(2):
            d_query = jnp.sum(jnp.where(_spare(lane, e, 0), dqa[e], 0.0), axis=1, keepdims=True)
            d_key = jnp.sum(jnp.where(_spare(lane, e, 3), dk_ref[e], 0.0), axis=1, keepdims=True)
            dcum_ref[...] += jnp.where(lane == 2 * p + e, d_query - d_key, 0.0)

        @pl.when(p == N_PAIR - 1)
        def _():
            dls = _cumsum_rows(dcum_ref[...], row, reverse=True)
            xv = f_ref[...] + b_ref[0:1, :]
            dx = jnp.where(lane < N_HEAD, dls * jax.nn.sigmoid(-xv), 0.0)
            dfl_ref[...] = dx.astype(BF16)
            red_ref[...] = jnp.zeros_like(red_ref)
            red_ref[0:1, :] = jnp.sum(dx, axis=0, keepdims=True)

    wide = jax.ShapeDtypeStruct((S, N_PAIR * LANE), BF16)
    cols = pl.BlockSpec((S, LANE), lambda p: (0, p))
    pair = pl.BlockSpec((2, S, LANE), lambda p: (p, 0, 0))
    return pl.pallas_call(
        body, name=name,
        out_shape=(wide, wide, wide, jax.ShapeDtypeStruct((S, LANE), BF16), jax.ShapeDtypeStruct((8, LANE), F32)),
        grid=(N_PAIR,),
        in_specs=[pl.BlockSpec((S, LANE), lambda p: (0, Z_F // LANE)), pl.BlockSpec((8, LANE), lambda p: (0, 0)),
                  pl.BlockSpec((2, LANE, S), lambda p: (p, 0, 0)), pair, pair],
        out_specs=(cols, cols, cols, pl.BlockSpec((S, LANE), lambda p: (0, 0)), pl.BlockSpec((8, LANE), lambda p: (0, 0))),
        scratch_shapes=[pltpu.VMEM((S, LANE), F32)],
        compiler_params=_params(("arbitrary",)),
    )(z, bf, dqt, dka, dva)


def _attn_fwd(qa, ka, va, name):
    tq, tk = TQ_FWD, TQ
    ratio = tq // tk

    def body(qa_ref, ka_ref, va_ref, o_ref, lse_ref):
        i = pl.program_id(1)
        lane = lax.broadcasted_iota(jnp.int32, (tq, LANE), 1)
        row = lax.broadcasted_iota(jnp.int32, (tq, tk), 0)
        col = lax.broadcasted_iota(jnp.int32, (tq, tk), 1)
        qs = [qa_ref[0], qa_ref[1]]

        def block(j, carry, masked):
            off = pl.multiple_of(j * tk, tk)
            out = []
            for e in range(2):
                m, acc = carry[e]
                s = lax.dot_general(qs[e], ka_ref[e, pl.ds(off, tk), :], _NT, preferred_element_type=F32)
                if masked:
                    s = jnp.where(col + (j - ratio * i) * tk > row, NEG_INF, s)
                mn = jnp.maximum(m, jnp.max(s, axis=1, keepdims=True))
                p = jnp.exp(s - mn).astype(BF16)
                acc = jnp.exp(m - mn) * acc + jnp.dot(p, va_ref[e, pl.ds(off, tk), :], preferred_element_type=F32)
                out.append((mn, acc))
            return tuple(out)

        init = (jnp.full((tq, 1), NEG_INF, F32), jnp.zeros((tq, LANE), F32))
        carry = lax.fori_loop(0, ratio * i, lambda j, c: block(j, c, False), (init, init))
        for d in range(ratio):
            carry = block(ratio * i + d, carry, True)
        res = []
        for e in range(2):
            m, acc = carry[e]
            l = jnp.sum(jnp.where(_spare(lane, e, 0), acc, 0.0), axis=1, keepdims=True)
            res.append((acc / l, m + jnp.log(l)))
        o_ref[...] = jnp.where(lane < 64, res[0][0], res[1][0])
        lse_ref[...] = jnp.where(lane < 64, res[0][1], res[1][1])

    out = jax.ShapeDtypeStruct((S, N_PAIR * LANE), F32)
    return pl.pallas_call(
        body, name=name, out_shape=(out, out), grid=(N_PAIR, S // tq),
        in_specs=[pl.BlockSpec((2, tq, LANE), lambda p, i: (p, i, 0)), pl.BlockSpec((2, S, LANE), lambda p, i: (p, 0, 0)),
                  pl.BlockSpec((2, S, LANE), lambda p, i: (p, 0, 0))],
        out_specs=(pl.BlockSpec((tq, LANE), lambda p, i: (i, p)), pl.BlockSpec((tq, LANE), lambda p, i: (i, p))),
        compiler_params=_params(("parallel", "parallel")),
    )(qa, ka, va)


def _attn_bwd(qa2, ka, va, kat, doa, name):
    nq = S // TQ

    def body(qa_ref, ka_ref, va_ref, kat_ref, doa_ref, dqt_ref, dk_ref, dv_ref):
        j = pl.program_id(1)

        @pl.when(j == 0)
        def _():
            dqt_ref[...] = jnp.zeros_like(dqt_ref)

        key = lax.broadcasted_iota(jnp.int32, (TQ, TQ), 0)
        qry = lax.broadcasted_iota(jnp.int32, (TQ, TQ), 1)
        kav, vav, katv = [ka_ref[0], ka_ref[1]], [va_ref[0], va_ref[1]], [kat_ref[0], kat_ref[1]]

        def block(i, carry, masked):
            off = pl.multiple_of(i * TQ, TQ)
            out = []
            for e in range(2):
                dk_acc, dv_acc = carry[e]
                qav = qa_ref[e, pl.ds(off, TQ), :]
                doav = doa_ref[e, pl.ds(off, TQ), :]
                s_t = lax.dot_general(kav[e], qav, _NT, preferred_element_type=F32)
                if masked:
                    s_t = jnp.where(key > qry, NEG_INF, s_t)
                p_t = jnp.exp(s_t)
                ds_t = p_t * lax.dot_general(vav[e], doav, _NT, preferred_element_type=F32)
                dsb = ds_t.astype(BF16)
                dv_acc = dv_acc + jnp.dot(p_t.astype(BF16), doav, preferred_element_type=F32)
                dk_acc = dk_acc + jnp.dot(dsb, qav, preferred_element_type=F32)
                dqt_ref[e, :, pl.ds(off, TQ)] += jnp.dot(katv[e], dsb, preferred_element_type=F32)
                out.append((dk_acc, dv_acc))
            return tuple(out)

        zero = (jnp.zeros((TQ, LANE), F32), jnp.zeros((TQ, LANE), F32))
        carry = block(j, (zero, zero), True)
        carry = lax.fori_loop(j + 1, nq, lambda i, c: block(i, c, False), carry)
        for e in range(2):
            dk_ref[e], dv_ref[e] = carry[e]

    full = pl.BlockSpec((2, S, LANE), lambda p, j: (p, 0, 0))
    blk = pl.BlockSpec((2, TQ, LANE), lambda p, j: (p, j, 0))
    acc = jax.ShapeDtypeStruct((N_HEAD, S, LANE), F32)
    return pl.pallas_call(
        body, name=name,
        out_shape=(jax.ShapeDtypeStruct((N_HEAD, LANE, S), F32), acc, acc),
        grid=(N_PAIR, nq),
        in_specs=[full, blk, blk, pl.BlockSpec((2, LANE, TQ), lambda p, j: (p, 0, j)), full],
        out_specs=(pl.BlockSpec((2, LANE, S), lambda p, j: (p, 0, 0)), blk, blk),
        compiler_params=_params(("arbitrary", "arbitrary")),
    )(qa2, ka, va, kat, doa)


ADA_ROWS = 16


def _ada_fwd(c_pad, w_ada, b_cols, name):
    def body(c_ref, w_ref, b_ref, o_ref):
        cv = c_ref[...]
        sc = (cv * jax.nn.sigmoid(cv)).astype(BF16)
        o_ref[0] = jnp.dot(sc, w_ref[0].astype(BF16), preferred_element_type=F32) + b_ref[0, 0:1, :]

    return pl.pallas_call(
        body, name=name, out_shape=jax.ShapeDtypeStruct((DEPTH, ADA_ROWS, ADA_COLS), F32), grid=(DEPTH,),
        in_specs=[pl.BlockSpec((ADA_ROWS, D), lambda l: (0, 0)), pl.BlockSpec((1, D, ADA_COLS), lambda l: (l, 0, 0)),
                  pl.BlockSpec((1, 8, ADA_COLS), lambda l: (l, 0, 0))],
        out_specs=pl.BlockSpec((1, ADA_ROWS, ADA_COLS), lambda l: (l, 0, 0)),
        compiler_params=_params(("parallel",)),
    )(c_pad, w_ada, b_cols)


def _ada_bwd(c_pad, dmod_cols, name):
    def body(c_ref, d_ref, o_ref):
        cv = c_ref[...]
        sc = (cv * jax.nn.sigmoid(cv)).astype(BF16)
        o_ref[0] = lax.dot_general(sc, d_ref[0].astype(BF16), _TN, preferred_element_type=F32)

    return pl.pallas_call(
        body, name=name, out_shape=jax.ShapeDtypeStruct((DEPTH, D, ADA_COLS), F32), grid=(DEPTH,),
        in_specs=[pl.BlockSpec((ADA_ROWS, D), lambda l: (0, 0)), pl.BlockSpec((1, ADA_ROWS, ADA_COLS), lambda l: (l, 0, 0))],
        out_specs=pl.BlockSpec((1, D, ADA_COLS), lambda l: (l, 0, 0)),
        compiler_params=_params(("parallel",)),
    )(c_pad, dmod_cols)


def _adamw_math(w, g, m, v):
    m = B1 * m + (1.0 - B1) * g
    v = B2 * v + (1.0 - B2) * (g * g)
    m_hat = m / (1.0 - B1 ** STEP)
    v_hat = v / (1.0 - B2 ** STEP)
    delta = -LR * (m_hat / (jnp.sqrt(v_hat) + EPS) + WD * w)
    return delta, m, v


def _row_tile(rows, target=256):
    best = 8
    for t in range(8, min(rows, target) + 1, 8):
        if rows % t == 0:
            best = t
    return best


def _adamw(w, g, m, v, name):
    layers, rows, cols = w.shape
    tr = _row_tile(rows)
    spec = pl.BlockSpec((1, tr, cols), lambda l, i: (l, i, 0))

    def body(w_ref, g_ref, m_ref, v_ref, d_ref, nm_ref, nv_ref):
        d_ref[...], nm_ref[...], nv_ref[...] = _adamw_math(w_ref[...], g_ref[...], m_ref[...], v_ref[...])

    out = jax.ShapeDtypeStruct(w.shape, F32)
    return pl.pallas_call(
        body, name=name, out_shape=(out, out, out), grid=(layers, rows // tr),
        in_specs=[spec] * 4, out_specs=(spec,) * 3, compiler_params=_params(("parallel", "parallel")),
    )(w, g, m, v)


def _sum_slabs(x, name):
    n, rows, _ = x.shape
    tr = _row_tile(rows)

    def body(x_ref, o_ref):
        acc = x_ref[0]
        for k in range(1, n):
            acc = acc + x_ref[k]
        o_ref[...] = acc

    return pl.pallas_call(
        body, name=name, out_shape=jax.ShapeDtypeStruct((rows, D), F32), grid=(rows // tr,),
        in_specs=[pl.BlockSpec((n, tr, D), lambda i: (0, i, 0))], out_specs=pl.BlockSpec((tr, D), lambda i: (i, 0)),
        compiler_params=_params(("parallel",)),
    )(x)


_ANY = pl.BlockSpec(memory_space=pl.ANY)
MESH = pl.DeviceIdType.MESH


def _on_sequencer(body, out_shape, sems, operands, after, sequencer_id, name):
    n = len(operands)

    def ordered_body(*refs):
        body(*refs[:n], *refs[n + 1:])

    extra = [] if after is None else [after]
    return pl.kernel(
        body if after is None else ordered_body, out_type=out_shape,
        mesh=plsc.ScalarSubcoreMesh(axis_name="sequencer", num_cores=1), scratch_types=sems,
        compiler_params=pltpu.CompilerParams(collective_id=sequencer_id), name=name)(*operands, *extra)


def _all_gather(xs, name, sequencer_id=None, after=None):
    n = len(xs)

    def body(*refs):
        x_refs, out_refs = refs[:n], refs[n:2 * n]
        send_sems, recv_sems, local_sems = refs[2 * n:]
        x_, y_, c_ = lax.axis_index("x"), lax.axis_index("y"), lax.axis_index("c")
        me, sibling = (x_, y_, c_), (x_, y_, 1 - c_)
        chips = [(1 - x_, y_), (x_, 1 - y_), (1 - x_, 1 - y_)]
        if sequencer_id is not None:
            barrier = pltpu.get_barrier_semaphore()
            peers = [sibling] + [(*chip, pc) for chip in chips for pc in (c_, 1 - c_)]
            for peer in peers:
                pl.semaphore_signal(barrier, inc=1, device_id=peer, device_id_type=MESH)
            pl.semaphore_wait(barrier, len(peers))

        def slot(a, px, py, pc):
            return out_refs[a].at[4 * px + 2 * py + pc]

        def copy(a, k, block, to, src=None):
            return pltpu.make_async_remote_copy(
                src_ref=slot(a, *block) if src is None else src, dst_ref=slot(a, *block),
                send_sem=send_sems.at[7 * a + k], recv_sem=recv_sems.at[7 * a + k], device_id=to, device_id_type=MESH)

        mine = [pltpu.make_async_copy(x_refs[a], slot(a, *me), local_sems.at[a]) for a in range(n)]
        for cp in mine:
            cp.start()
        first = []
        for a in range(n):
            first.append(copy(a, 0, me, sibling, src=x_refs[a]))
            first += [copy(a, 1 + j, me, (*chip, c_), src=x_refs[a]) for j, chip in enumerate(chips)]
        for cp in first:
            cp.start()
        passed = []
        for j, chip in enumerate(chips):
            for a in range(n):
                copy(a, 1 + j, (*chip, c_), me).wait_recv()
                passed.append(copy(a, 4 + j, (*chip, c_), sibling))
                passed[-1].start()
        for a in range(n):
            copy(a, 0, sibling, me).wait_recv()
        for j, chip in enumerate(chips):
            for a in range(n):
                copy(a, 4 + j, (*chip, 1 - c_), me).wait_recv()
        for cp in first + passed:
            cp.wait_send()
        for cp in mine:
            cp.wait()

    out_shape = [jax.ShapeDtypeStruct((N_DEV,) + x.shape, x.dtype) for x in xs]
    sems = [pltpu.SemaphoreType.DMA((7 * n,)), pltpu.SemaphoreType.DMA((7 * n,)), pltpu.SemaphoreType.DMA((n,))]
    if sequencer_id is not None:
        return _on_sequencer(body, out_shape, sems, xs, after, sequencer_id, name)
    return pl.pallas_call(
        body, name=name, out_shape=out_shape, in_specs=[_ANY] * n, out_specs=[_ANY] * n, scratch_shapes=sems)(*xs)


def _sibling_exchange(gs, name, sequencer_id=None, after=None):
    n = len(gs)

    def body(*refs):
        g_refs, p_refs = refs[:n], refs[n:2 * n]
        send_sems, recv_sems = refs[2 * n:]
        x_, y_, c_ = lax.axis_index("x"), lax.axis_index("y"), lax.axis_index("c")
        if sequencer_id is not None:
            barrier = pltpu.get_barrier_semaphore()
            pl.semaphore_signal(barrier, inc=1, device_id=(x_, y_, 1 - c_), device_id_type=MESH)
            pl.semaphore_wait(barrier, 1)
        copies = [pltpu.make_async_remote_copy(
            src_ref=g_refs[a].at[2 * k + (1 - c_)], dst_ref=p_refs[a].at[k], send_sem=send_sems.at[4 * a + k],
            recv_sem=recv_sems.at[4 * a + k], device_id=(x_, y_, 1 - c_), device_id_type=MESH)
            for a in range(n) for k in range(4)]
        for cp in copies:
            cp.start()
        for cp in copies:
            cp.wait()

    out_shape = [jax.ShapeDtypeStruct((4,) + g.shape[1:], g.dtype) for g in gs]
    sems = [pltpu.SemaphoreType.DMA((4 * n,)), pltpu.SemaphoreType.DMA((4 * n,))]
    if sequencer_id is not None:
        return _on_sequencer(body, out_shape, sems, gs, after, sequencer_id, name)
    return pl.pallas_call(
        body, name=name, out_shape=out_shape, in_specs=[_ANY] * n, out_specs=[_ANY] * n, scratch_shapes=sems)(*gs)


def _slab_tiles(rows, cols):
    if rows % 8 == 0:
        return _row_tile(rows), cols
    return rows, 2 * LANE


def _pair_sums(g, p, route, name):
    _, rows, cols = g.shape
    tr, tc = _slab_tiles(rows, cols)

    def body(route_ref, g_ref, p_ref, t_ref):
        t_ref[...] = (g_ref[...] + p_ref[...]).astype(BF16)

    return pl.pallas_call(
        body, name=name, out_shape=jax.ShapeDtypeStruct((3, rows, cols), BF16),
        grid_spec=pltpu.PrefetchScalarGridSpec(
            num_scalar_prefetch=1, grid=(3, rows // tr, cols // tc),
            in_specs=[pl.BlockSpec((1, tr, tc), lambda r, i, j, route_ref: (2 * route_ref[1 + r] + route_ref[0], i, j)),
                      pl.BlockSpec((1, tr, tc), lambda r, i, j, route_ref: (route_ref[1 + r], i, j))],
            out_specs=pl.BlockSpec((1, tr, tc), lambda r, i, j, route_ref: (r, i, j))),
        compiler_params=_params(("parallel", "parallel", "parallel")),
    )(route, g, p)


def _chip_exchange(ts, name, sequencer_id=None, after=None):
    n = len(ts)

    def body(*refs):
        t_refs, l_refs = refs[:n], refs[n:2 * n]
        send_sems, recv_sems = refs[2 * n:]
        x_, y_, c_ = lax.axis_index("x"), lax.axis_index("y"), lax.axis_index("c")
        chips = [(1 - x_, y_), (x_, 1 - y_), (1 - x_, 1 - y_)]
        if sequencer_id is not None:
            barrier = pltpu.get_barrier_semaphore()
            for px, py in chips:
                pl.semaphore_signal(barrier, inc=1, device_id=(px, py, c_), device_id_type=MESH)
            pl.semaphore_wait(barrier, len(chips))
        copies = [pltpu.make_async_remote_copy(
            src_ref=t_refs[a].at[r], dst_ref=l_refs[a].at[r], send_sem=send_sems.at[3 * a + r],
            recv_sem=recv_sems.at[3 * a + r], device_id=(px, py, c_), device_id_type=MESH)
            for a in range(n) for r, (px, py) in enumerate(chips)]
        for cp in copies:
            cp.start()
        for cp in copies:
            cp.wait()

    out_shape = [jax.ShapeDtypeStruct((3,) + t.shape[1:], t.dtype) for t in ts]
    sems = [pltpu.SemaphoreType.DMA((3 * n,)), pltpu.SemaphoreType.DMA((3 * n,))]
    if sequencer_id is not None:
        return _on_sequencer(body, out_shape, sems, ts, after, sequencer_id, name)
    return pl.pallas_call(
        body, name=name, out_shape=out_shape, in_specs=[_ANY] * n, out_specs=[_ANY] * n, scratch_shapes=sems)(*ts)


def _reduce_adamw(gs, ps, landed, place, w, m, v, name):
    layers, rows, cols = w.shape
    assert layers == DEPTH == 2
    tr, tc = _slab_tiles(rows, cols)
    nr, nc = rows // tr, cols // tc
    spec = pl.BlockSpec((1, tr, tc), lambda l, i, j, place_ref: (l, i, j))

    def own(layer, which):
        pi, pj = (nr - 1, nc - 1) if layer == 0 else (0, 0)

        def index(l, i, j, place_ref):
            lead = 0 if which is None else place_ref[which]
            return lead, jnp.where(l == layer, i, pi), jnp.where(l == layer, j, pj)

        return pl.BlockSpec((3 if which is None else 1, tr, tc), index)

    def body(place_ref, g0_ref, p0_ref, l0_ref, g1_ref, p1_ref, l1_ref, w_ref, m_ref, v_ref,
             g_ref, d_ref, nm_ref, nv_ref):
        def update(own_ref, sib_ref, l_ref):
            g = own_ref[0] + sib_ref[0] + l_ref[0].astype(F32) + l_ref[1].astype(F32) + l_ref[2].astype(F32)
            g_ref[0] = g
            d_ref[0], nm_ref[0], nv_ref[0] = _adamw_math(w_ref[0], g, m_ref[0], v_ref[0])

        @pl.when(pl.program_id(0) == 0)
        def _():
            update(g0_ref, p0_ref, l0_ref)

        @pl.when(pl.program_id(0) == 1)
        def _():
            update(g1_ref, p1_ref, l1_ref)

    out = jax.ShapeDtypeStruct(w.shape, F32)
    return pl.pallas_call(
        body, name=name, out_shape=(out, out, out, out),
        grid_spec=pltpu.PrefetchScalarGridSpec(
            num_scalar_prefetch=1, grid=(DEPTH, nr, nc),
            in_specs=[own(0, 0), own(0, 1), own(0, None), own(1, 0), own(1, 1), own(1, None), spec, spec, spec],
            out_specs=(spec, spec, spec, spec)),
        compiler_params=_params(("arbitrary", "arbitrary", "arbitrary")),
    )(place, gs[0], ps[0], landed[0], gs[1], ps[1], landed[1], w, m, v)


def _pack(pieces, row_multiple, dtype, cols=D, rows=None):
    flat = jnp.concatenate([p.astype(dtype).reshape(-1) for p in pieces])
    if rows is None:
        rows = -(-flat.shape[0] // cols)
        rows = -(-rows // row_multiple) * row_multiple
    flat = jnp.pad(flat, (0, rows * cols - flat.shape[0]))
    return flat.reshape(rows, cols)


def _unpack(flat, shapes, lead=()):
    out, off = [], 0
    for shp in shapes:
        n = 1
        for s_ in shp:
            n *= s_
        out.append(lax.slice_in_dim(flat, off, off + n, axis=len(lead)).reshape(lead + tuple(shp)))
        off += n
    return out


def _z_rows_from_in(wt):
    pad = jnp.zeros((NZ - IN_COLS, wt.shape[1]), wt.dtype)
    return jnp.concatenate([wt[1544:2568], wt[2568:5640], wt[0:1536], wt[1536:1544], pad], axis=0)


def _in_rows_from_z(wt):
    return jnp.concatenate([wt[Z_Q:Z_Q + 1536], wt[Z_F:Z_F + 8], wt[Z_PC:Z_PC + 1024], wt[Z_G:Z_G + 3072]], axis=0)


def _pad_rows(v, rows=8):
    return jnp.pad(v, ((0, rows - v.shape[0]), (0, 0)))


def _layer_fwd(l, x, wts, gvec, mod):
    tag = f"l{l}"
    h = _prenorm_fwd(x, gvec, mod, 0, 0, 1, f"prenorm_mix_{tag}")
    z = _matmul(h, wts["w_in_t"], "nt", f"in_proj_{tag}", tn=1152)
    qa, ka, va, kat = _attn_prep(z, wts["b_f"], f"attn_prep_{tag}")
    o, lse = _attn_fwd(qa, ka, va, f"attn_{tag}")
    br_b = _pool_fwd(z, wts["wp_bd"], wts["pool_scale"], f"pool_{tag}")
    br_c = _conv_fwd(z, wts["conv_w"], f"conv_{tag}")
    pa = _matmul(o, wts["wa"], "nn", f"proj_a_{tag}")
    pb = _matmul(br_b, wts["wb"], "nn", f"proj_b_{tag}")
    pc = _matmul(br_c, wts["wc"], "nn", f"proj_c_{tag}")
    merged = _merge_fwd(z, pa, pb, pc, f"merge_{tag}")
    y = _matmul(merged, wts["w_out"], "nn", f"out_proj_{tag}")
    x1 = _postnorm_fwd(x, y, gvec, mod, 1, 2, f"postnorm_mix_{tag}")
    h2 = _prenorm_fwd(x1, gvec, mod, 2, 3, 4, f"prenorm_ff_{tag}")
    a = _matmul(h2, wts["w_ff1"], "nn", f"ff1_{tag}", b_col_shards=True)
    r = _relu2_fwd(a, f"relu2_{tag}")
    y2 = _matmul(r, wts["w_ff2"], "nn", f"ff2_{tag}")
    x2 = _postnorm_fwd(x1, y2, gvec, mod, 3, 5, f"postnorm_ff_{tag}")
    saved = dict(x=x, h=h, z=z, qa=qa, ka=ka, va=va, kat=kat, o=o, lse=lse, br_b=br_b, br_c=br_c, pa=pa, pb=pb, pc=pc,
                 merged=merged, y=y, x1=x1, h2=h2, a=a, r=r, y2=y2)
    return x2, saved


def _ffn_bwd(l, dx2, sv, wts, gvec, mod, midpoint):
    tag = f"l{l}"
    dy2, red_post_ff = _postnorm_bwd(sv["y2"], gvec, mod, dx2, 3, 5, f"postnorm_ff_bwd_{tag}")
    dr = _matmul(dy2, wts["w_ff2"], "nt", f"ff2_dx_{tag}")
    d_w_ff2 = _matmul(sv["r"], dy2, "tn", f"ff2_dw_{tag}")
    da = _relu2_bwd(sv["a"], midpoint(dr), f"relu2_bwd_{tag}")
    dh2 = _matmul(da, wts["w_ff1"], "nt", f"ff1_dx_{tag}", b_col_shards=True)
    d_w_ff1 = _matmul(sv["h2"], da, "tn", f"ff1_dw_{tag}", out_col_shards=True)
    dx1, red_pre_ff = _prenorm_bwd(sv["x1"], gvec, mod, dh2, dx2, 2, 4, f"prenorm_ff_bwd_{tag}")
    return dx1, [d_w_ff1, d_w_ff2.reshape(N_DEV, D_FF // N_DEV, D)], (red_pre_ff, red_post_ff)


def _mixer_bwd(l, dx1, sv, wts, gvec, mod, ffn_reds, midpoint):
    tag = f"l{l}"
    red_pre_ff, red_post_ff = ffn_reds
    dy, red_post_mix = _postnorm_bwd(sv["y"], gvec, mod, dx1, 1, 2, f"postnorm_mix_bwd_{tag}")
    dmerged = _matmul(dy, wts["w_out"], "nt", f"out_proj_dx_{tag}")
    d_w_out = _matmul(sv["merged"], dy, "tn", f"out_proj_dw_{tag}")
    dmerged = midpoint(dmerged)
    dpa, dpb, dpc, dgl = _merge_bwd(sv["z"], sv["pa"], sv["pb"], sv["pc"], dmerged, f"merge_bwd_{tag}")
    do = _matmul(dpa, wts["wa"], "nt", f"proj_a_dx_{tag}")
    dbr_b = _matmul(dpb, wts["wb"], "nt", f"proj_b_dx_{tag}")
    dbr_c = _matmul(dpc, wts["wc"], "nt", f"proj_c_dx_{tag}")
    d_wa = _matmul(sv["o"], dpa, "tn", f"proj_a_dw_{tag}")
    d_wb = _matmul(sv["br_b"], dpb, "tn", f"proj_b_dw_{tag}")
    d_wc = _matmul(sv["br_c"], dpc, "tn", f"proj_c_dw_{tag}")
    d_w_branch = jnp.concatenate([d_wa, d_wb, d_wc], axis=0)

    dpu, d_wp_bd, red_pool = _pool_bwd(sv["z"], wts["wp_bd"], wts["pool_scale"], dbr_b, f"pool_bwd_{tag}")
    dconv, red_conv = _conv_bwd(sv["z"], wts["conv_w"], dbr_c, f"conv_bwd_{tag}")
    qa2, doa = _attn_bwd_prep(sv["qa"], sv["o"], sv["lse"], do, f"attn_bwd_prep_{tag}")
    dqt, dka, dva = _attn_bwd(qa2, sv["ka"], sv["va"], sv["kat"], doa, f"attn_bwd_{tag}")
    dq, dk, dv, dfl, red_f = _attn_bwd_post(sv["z"], wts["b_f"], dqt, dka, dva, f"attn_bwd_post_{tag}")
    dz = jnp.concatenate([dpu, dconv, dgl, dq, dk, dv, dfl], axis=1)
    dh = _matmul(dz, wts["w_in_t"], "nn", f"in_proj_dx_{tag}", tk=1152)
    d_w_in_t = _matmul(dz, sv["h"], "tn", f"in_proj_dw_{tag}", tm=1152)
    dx0, red_pre_mix = _prenorm_bwd(sv["x"], gvec, mod, dh, dx1, 0, 1, f"prenorm_mix_bwd_{tag}")

    rows = D // N_DEV
    big = [_in_rows_from_z(d_w_in_t).reshape(N_DEV, IN_SHARD, D), d_w_branch.reshape(N_DEV, rows, D),
           d_w_out.reshape(N_DEV, rows, D)]
    d_w_pool = jnp.stack([d_wp_bd[64 * g:64 * (g + 1), 64 * g:64 * (g + 1)] for g in range(4)])
    small = dict(
        mod=jnp.stack([red_pre_mix[0], red_pre_mix[1], red_post_mix[0], red_pre_ff[0], red_pre_ff[1], red_post_ff[0]]),
        g_mix_pre=red_pre_mix[2], g_mix_post=red_post_mix[1], g_ff_pre=red_pre_ff[2], g_ff_post=red_post_ff[1],
        b_f=red_f[0, 0:8], w_pool=d_w_pool, pool_scale=red_pool[0], conv_w=red_conv[0:3])
    return dx0, big, small


SMALL_KEYS = ["mod", "g_mix_pre", "g_mix_post", "g_ff_pre", "g_ff_post", "b_f", "w_pool", "pool_scale", "conv_w"]
SMALL_SHAPES = [(DEPTH, 6 * D), (DEPTH, D), (DEPTH, D), (DEPTH, D), (DEPTH, D), (DEPTH, 8), (DEPTH, 4, 64, 64),
                (DEPTH, POOL_W), (DEPTH, 3, CONV_W)]


def kernel(x, c, w_ada, b_ada, g_mix_pre, g_mix_post, g_ff_pre, g_ff_post, w_in, b_f, w_pool, pool_scale, conv_w, w_branch, w_out, w_ff1, w_ff2, loss_target, m_w_ada, m_b_ada, m_g_mix_pre, m_g_mix_post, m_g_ff_pre, m_g_ff_post, m_w_in, m_b_f, m_w_pool, m_pool_scale, m_conv_w, m_w_branch, m_w_out, m_w_ff1, m_w_ff2, v_w_ada, v_b_ada, v_g_mix_pre, v_g_mix_post, v_g_ff_pre, v_g_ff_post, v_w_in, v_b_f, v_w_pool, v_pool_scale, v_conv_w, v_w_branch, v_w_out, v_w_ff1, v_w_ff2):
    ix, iy, ic = lax.axis_index("x"), lax.axis_index("y"), lax.axis_index("c")
    me = 4 * ix + 2 * iy + ic
    route = jnp.stack([ic, 2 * (1 - ix) + iy, 2 * ix + (1 - iy), 2 * (1 - ix) + (1 - iy)]).astype(jnp.int32)
    place = jnp.stack([me, 2 * ix + iy]).astype(jnp.int32)
    wt_in, mt_in, vt_in = (jnp.transpose(a, (0, 2, 1)) for a in (w_in, m_w_in, v_w_in))

    c_all = _all_gather([_pad_rows(c)], "gather_c")[0][:, 0, :]
    c_pad = _pad_rows(c_all, ADA_ROWS)
    b_cols = lax.dynamic_slice_in_dim(b_ada, me * ADA_COLS, ADA_COLS, axis=1)
    b_cols = jnp.broadcast_to(b_cols[:, None, :], (DEPTH, 8, ADA_COLS))
    mod_part = _ada_fwd(c_pad, w_ada, b_cols, "ada_fwd")
    mod_all = _all_gather([mod_part.reshape(DEPTH * ADA_ROWS, ADA_COLS)], "gather_mod")[0]
    mod_all = mod_all.reshape(N_DEV, DEPTH, ADA_ROWS, ADA_COLS)
    mod_mine = lax.dynamic_index_in_dim(mod_all, me, axis=2, keepdims=False)
    mod_mine = jnp.transpose(mod_mine, (1, 0, 2)).reshape(DEPTH, 6, D)

    cw_cols = CONV_W // N_DEV
    cw_send = jnp.pad(conv_w.reshape(DEPTH * 3, cw_cols), ((0, 8 - DEPTH * 3), (0, LANE - cw_cols)))
    send = [[w[l].astype(BF16) for w in (wt_in, w_branch, w_out, w_ff1, w_ff2)] for l in range(DEPTH)]
    first = _all_gather(send[0][:1], "gather_weights_l0_in", sequencer_id=1, after=mod_all)
    rest = _all_gather(send[0][1:] + [cw_send], "gather_weights_l0_rest", sequencer_id=2, after=first[0])
    gathered = [first + rest[:4], _all_gather(send[1], "gather_weights_l1", sequencer_id=3, after=first[0])]
    cw_all = rest[4][:, :DEPTH * 3, :cw_cols].reshape(N_DEV, DEPTH, 3, cw_cols)

    def layer_operands(l, weights):
        p_in, p_br, p_out, p_ff1, p_ff2 = weights[:5]
        w_br_full = p_br.reshape(D, D)
        cw_full = jnp.transpose(cw_all[:, l], (1, 0, 2)).reshape(3, CONV_W)
        wp_bd = jnp.zeros((POOL_W, POOL_W), F32)
        for g in range(4):
            wp_bd = wp_bd.at[64 * g:64 * (g + 1), 64 * g:64 * (g + 1)].set(w_pool[l, g])
        wts = dict(
            w_in_t=_z_rows_from_in(p_in.reshape(IN_COLS, D)), wa=w_br_full[0:A_WIDTH], wb=w_br_full[A_WIDTH:A_WIDTH + POOL_W],
            wc=w_br_full[A_WIDTH + POOL_W:], w_out=p_out.reshape(D, D),
            w_ff1=p_ff1, w_ff2=p_ff2.reshape(D_FF, D),
            conv_w=_pad_rows(cw_full), wp_bd=wp_bd.astype(BF16), pool_scale=_pad_rows(pool_scale[l][None, :]),
            b_f=_pad_rows(jnp.pad(b_f[l], (0, LANE - 8))[None, :]))
        gvec = _pad_rows(jnp.stack([g_mix_pre[l], g_mix_post[l], g_ff_pre[l], g_ff_post[l]]))
        return wts, gvec, _pad_rows(mod_mine[l])

    xs = x[0]
    saved, layers = [], []
    for l in range(DEPTH):
        weights = gathered[l]
        if l > 0:
            xs, weights = lax.optimization_barrier((xs, weights))
        layers.append(layer_operands(l, weights))
        xs, sv = _layer_fwd(l, xs, *layers[l])
        saved.append(sv)
    dx, loss_part = _loss_head(xs, loss_target[0], "loss_head")
    loss = lax.psum(loss_part[0, 0], ("x", "y", "c"))
    small_grads = [None] * DEPTH
    mine, sibs, landed = ({} for _ in range(3))
    seq_id = iter(range(4, 4 + 4 * DEPTH))
    last = [gathered[DEPTH - 1][0]]

    def start(group, grads):
        mine[group] = grads
        sibs[group] = _sibling_exchange(grads, f"rs_sibling_{group}", sequencer_id=next(seq_id), after=last[0])
        last[0] = sibs[group][0]

    def finish(group, later):
        later, (grads, sib) = lax.optimization_barrier((later, (mine[group], sibs[group])))
        sends = [_pair_sums(g, p, route, f"rs_pair_sums_{group}_{k}") for k, (g, p) in enumerate(zip(grads, sib))]
        later, sends = lax.optimization_barrier((later, sends))
        landed[group] = _chip_exchange(sends, f"rs_chips_{group}", sequencer_id=next(seq_id), after=last[0])
        last[0] = landed[group][0]
        return later

    pending = None
    for l in reversed(range(DEPTH)):
        hook = (lambda da: da) if pending is None else functools.partial(finish, pending)
        dx, ffn_grads, ffn_reds = _ffn_bwd(l, dx, saved[l], *layers[l], hook)
        start(f"ffn_l{l}", ffn_grads)
        dx, mix_grads, small_grads[l] = _mixer_bwd(l, dx, saved[l], *layers[l], ffn_reds,
                                                   functools.partial(finish, f"ffn_l{l}"))
        start(f"mix_l{l}", mix_grads)
        pending = f"mix_l{l}"
    grad_x = dx[None]

    big_w = [wt_in, w_branch, w_out, w_ff1, w_ff2]
    big_m = [mt_in, m_w_branch, m_w_out, m_w_ff1, m_w_ff2]
    big_v = [vt_in, v_w_branch, v_w_out, v_w_ff1, v_w_ff2]
    where = [("mix", 0), ("mix", 1), ("mix", 2), ("ffn", 0), ("ffn", 1)]

    def reduce_and_update(k):
        group, at = where[k]
        return _reduce_adamw([mine[f"{group}_l{l}"][at] for l in range(DEPTH)],
                             [sibs[f"{group}_l{l}"][at] for l in range(DEPTH)],
                             [landed[f"{group}_l{l}"][at] for l in range(DEPTH)], place, big_w[k], big_m[k], big_v[k],
                             f"rs_sum_adamw_{k}")

    big_res = {k: list(reduce_and_update(k)) for k in (3, 4)}
    big_res[3][0] = finish(pending, big_res[3][0])

    small = {k: jnp.stack([small_grads[l][k] for l in range(DEPTH)]) for k in SMALL_KEYS}
    small_all = _all_gather([_pack([small[k] for k in SMALL_KEYS], 8, F32)], "gather_small")[0]
    dmod_all = small_all[:, 0:DEPTH * 6, :].reshape(N_DEV, DEPTH, 6 * D)
    summed = _unpack(_sum_slabs(small_all, "sum_small").reshape(-1), SMALL_SHAPES)
    sg = dict(zip(SMALL_KEYS, summed))
    dmod_cols = lax.dynamic_slice_in_dim(dmod_all, me * ADA_COLS, ADA_COLS, axis=2)
    dmod_cols = jnp.pad(jnp.transpose(dmod_cols, (1, 0, 2)), ((0, 0), (0, ADA_ROWS - N_DEV), (0, 0)))
    g_w_ada = _ada_bwd(c_pad, dmod_cols, "ada_bwd")
    g_conv_w = lax.dynamic_slice_in_dim(sg["conv_w"], me * (CONV_W // N_DEV), CONV_W // N_DEV, axis=2)

    ada_out = [g_w_ada] + list(_adamw(w_ada, g_w_ada, m_w_ada, v_w_ada, "adamw_ada"))
    rest_w = [b_ada, g_mix_pre, g_mix_post, g_ff_pre, g_ff_post, b_f, w_pool, pool_scale, conv_w]
    rest_m = [m_b_ada, m_g_mix_pre, m_g_mix_post, m_g_ff_pre, m_g_ff_post, m_b_f, m_w_pool, m_pool_scale, m_conv_w]
    rest_v = [v_b_ada, v_g_mix_pre, v_g_mix_post, v_g_ff_pre, v_g_ff_post, v_b_f, v_w_pool, v_pool_scale, v_conv_w]
    rest_g = [sg["mod"], sg["g_mix_pre"], sg["g_mix_post"], sg["g_ff_pre"], sg["g_ff_post"], sg["b_f"],
              sg["w_pool"], sg["pool_scale"], g_conv_w]
    rest_shapes = [a.shape for a in rest_w]
    upd = _adamw(_pack(rest_w, 8, F32)[None], _pack(rest_g, 8, F32)[None], _pack(rest_m, 8, F32)[None],
                 _pack(rest_v, 8, F32)[None], "adamw_rest")
    rest_out = [rest_g] + [_unpack(arr.reshape(-1), rest_shapes) for arr in upd]
    rest_out = [[ada_out[which]] + rest_out[which] for which in range(4)]

    landed[pending], rest_out = lax.optimization_barrier((landed[pending], rest_out))
    big_res.update({k: reduce_and_update(k) for k in (0, 1, 2)})
    big_out = [[jnp.transpose(big_res[k][which], (0, 2, 1)) if k == 0 else big_res[k][which] for k in range(5)]
               for which in range(4)]

    def ordered(k):
        r, b = rest_out[k], big_out[k]
        return [r[0], r[1], r[2], r[3], r[4], r[5], b[0], r[6], r[7], r[8], r[9], b[1], b[2], b[3], b[4]]

    return (loss, grad_x, *ordered(0), *ordered(1), *ordered(2), *ordered(3))
```

```python
import functools

import jax
import jax.numpy as jnp
from jax import lax
from jax.experimental import pallas as pl
from jax.experimental.pallas import tpu as pltpu
from jax.experimental.pallas import tpu_sc as plsc

F32 = jnp.float32
BF16 = jnp.bfloat16

N_DEV = 8
D = 1024
S = 2048
DEPTH = 2
D_FF = 4 * D
A_WIDTH = 512
HEAD_DIM = 64
N_PAIR = 4
POOL_W = 256
CONV_W = 256
IN_COLS = 5640
ADA_COLS = 6 * D // N_DEV
IN_SHARD = IN_COLS // N_DEV
RMS_EPS = 1e-6
NEG_INF = -1e30
ATT_SCALE = HEAD_DIM ** -0.5

NZ = 5760
Z_PC = 0
Z_G = 1024
Z_Q = 4096
Z_K = 4608
Z_V = 5120
Z_F = 5632

LR, B1, B2, EPS, WD, STEP = 0.001, 0.9, 0.999, 1e-08, 0.01, 10

LANE = 128
VMEM_LIMIT_BYTES = 48 * 1024 * 1024
TS = 256
TQ = 256
TQ_FWD = 512


def _params(sem=None):
    return pltpu.CompilerParams(dimension_semantics=sem, vmem_limit_bytes=VMEM_LIMIT_BYTES)


def _pick(n, target):
    best = None
    for t in range(LANE, min(n, target) + 1, LANE):
        if n % t == 0:
            best = t
    return n if best is None else best


def _matmul(a, b, mode, name, out_dtype=F32, tm=1024, tn=1024, tk=1024, b_col_shards=False, out_col_shards=False,
            extra=(), epilogue=None, out_dtypes=None):
    if b_col_shards:
        shards, b_rows, shard_cols = b.shape
        b_shape = (b_rows, shards * shard_cols)
    else:
        b_shape = b.shape
    if mode == "nn":
        (m, k), (k2, n) = a.shape, b_shape
    elif mode == "nt":
        (m, k), (n, k2) = a.shape, b_shape
    else:
        (k, m), (k2, n) = a.shape, b_shape
    assert k == k2, (a.shape, b.shape, mode)
    tm, tn, tk = _pick(m, tm), _pick(n, tn), _pick(k, tk)
    if b_col_shards and mode == "nn":
        tn = shard_cols
    if b_col_shards and mode == "nt":
        tk = shard_cols
    if out_col_shards:
        tn = n // N_DEV
    nk = k // tk
    if mode == "nn":
        a_spec = pl.BlockSpec((tm, tk), lambda i, j, kk: (i, kk))
        b_spec = (pl.BlockSpec((None, tk, tn), lambda i, j, kk: (j, kk, 0)) if b_col_shards else
                  pl.BlockSpec((tk, tn), lambda i, j, kk: (kk, j)))
        dims = (((1,), (0,)), ((), ()))
    elif mode == "nt":
        a_spec = pl.BlockSpec((tm, tk), lambda i, j, kk: (i, kk))
        b_spec = (pl.BlockSpec((None, tn, tk), lambda i, j, kk: (kk, j, 0)) if b_col_shards else
                  pl.BlockSpec((tn, tk), lambda i, j, kk: (j, kk)))
        dims = (((1,), (1,)), ((), ()))
    else:
        assert not b_col_shards
        a_spec = pl.BlockSpec((tk, tm), lambda i, j, kk: (kk, i))
        b_spec = pl.BlockSpec((tk, tn), lambda i, j, kk: (kk, j))
        dims = (((0,), (0,)), ((), ()))
    if out_col_shards:
        out_shape = jax.ShapeDtypeStruct((N_DEV, m, tn), out_dtype)
        out_spec = pl.BlockSpec((None, tm, tn), lambda i, j, kk: (j, i, 0))
    else:
        out_shape = jax.ShapeDtypeStruct((m, n), out_dtype)
        out_spec = pl.BlockSpec((tm, tn), lambda i, j, kk: (i, j))

    n_extra = len(extra)
    extra_specs = [pl.BlockSpec((tm, tn), lambda i, j, kk, off=off: (i, j + off // tn)) for _, off in extra]
    if epilogue is not None:
        assert not out_col_shards and all(off % tn == 0 for _, off in extra)
        out_shape = [jax.ShapeDtypeStruct((m, n), dt) for dt in out_dtypes]
        out_spec = [pl.BlockSpec((tm, tn), lambda i, j, kk: (i, j)) for _ in out_dtypes]

    def product(a_ref, b_ref):
        return lax.dot_general(a_ref[...].astype(BF16), b_ref[...].astype(BF16), dims, preferred_element_type=F32)

    def write(acc, extra_refs, o_refs):
        if epilogue is None:
            o_refs[0][...] = acc.astype(out_dtype)
        else:
            for o_ref, tile in zip(o_refs, epilogue(acc, *[r[...] for r in extra_refs])):
                o_ref[...] = tile.astype(o_ref.dtype)

    def body_one_pass(a_ref, b_ref, *refs):
        write(product(a_ref, b_ref), refs[:n_extra], refs[n_extra:])

    def body(a_ref, b_ref, *refs):
        acc_ref = refs[-1]
        kk = pl.program_id(2)

        @pl.when(kk == 0)
        def _():
            acc_ref[...] = product(a_ref, b_ref)

        @pl.when(kk > 0)
        def _():
            acc_ref[...] += product(a_ref, b_ref)

        @pl.when(kk == nk - 1)
        def _():
            write(acc_ref[...], refs[:n_extra], refs[n_extra:-1])

    return pl.pallas_call(
        body_one_pass if nk == 1 else body, name=name,
        out_shape=out_shape,
        grid=(m // tm, n // tn, nk),
        in_specs=[a_spec, b_spec] + extra_specs,
        out_specs=out_spec,
        scratch_shapes=[] if nk == 1 else [pltpu.VMEM((tm, tn), F32)],
        compiler_params=_params(("parallel", "parallel", "arbitrary")),
    )(a, b, *[x for x, _ in extra])


def _row_spec(width=D, col=0):
    return pl.BlockSpec((TS, width), lambda i: (i, col))


def _vec_spec(rows=8, width=D):
    return pl.BlockSpec((rows, width), lambda i: (0, 0))


def _rms(x):
    return lax.rsqrt(jnp.mean(x * x, axis=-1, keepdims=True) + RMS_EPS)


def _prenorm_fwd(x, gvec, mod, g_row, shift_row, scale_row, name):
    def body(x_ref, g_ref, mod_ref, h_ref):
        xv = x_ref[...]
        y = xv * _rms(xv) * g_ref[g_row:g_row + 1, :]
        h = y * (1.0 + mod_ref[scale_row:scale_row + 1, :]) + mod_ref[shift_row:shift_row + 1, :]
        h_ref[...] = h.astype(BF16)

    return pl.pallas_call(
        body, name=name, out_shape=jax.ShapeDtypeStruct((S, D), BF16), grid=(S // TS,),
        in_specs=[_row_spec(), _vec_spec(), _vec_spec()], out_specs=_row_spec(),
        compiler_params=_params(("parallel",)),
    )(x, gvec, mod)


def _prenorm_bwd(x, gvec, mod, dh, dres, g_row, scale_row, name):
    def body(x_ref, g_ref, mod_ref, dh_ref, dres_ref, dx_ref, red_ref):
        i = pl.program_id(0)

        @pl.when(i == 0)
        def _():
            red_ref[...] = jnp.zeros_like(red_ref)

        xv = x_ref[...]
        g = g_ref[g_row:g_row + 1, :]
        r = _rms(xv)
        n = xv * r
        yg = n * g
        dhv = dh_ref[...]
        dyg = dhv * (1.0 + mod_ref[scale_row:scale_row + 1, :])
        dn = dyg * g
        dx = r * (dn - n * jnp.mean(dn * n, axis=-1, keepdims=True))
        dx_ref[...] = dres_ref[...] + dx
        red_ref[0:1, :] += jnp.sum(dhv, axis=0, keepdims=True)
        red_ref[1:2, :] += jnp.sum(dhv * yg, axis=0, keepdims=True)
        red_ref[2:3, :] += jnp.sum(dyg * n, axis=0, keepdims=True)

    return pl.pallas_call(
        body, name=name,
        out_shape=(jax.ShapeDtypeStruct((S, D), F32), jax.ShapeDtypeStruct((8, D), F32)),
        grid=(S // TS,),
        in_specs=[_row_spec(), _vec_spec(), _vec_spec(), _row_spec(), _row_spec()],
        out_specs=(_row_spec(), _vec_spec()),
        compiler_params=_params(("arbitrary",)),
    )(x, gvec, mod, dh, dres)


def _postnorm_fwd(x, y, gvec, mod, g_row, gate_row, name):
    def body(x_ref, y_ref, g_ref, mod_ref, o_ref):
        yv = y_ref[...]
        yn = yv * _rms(yv) * g_ref[g_row:g_row + 1, :]
        o_ref[...] = x_ref[...] + mod_ref[gate_row:gate_row + 1, :] * yn

    return pl.pallas_call(
        body, name=name, out_shape=jax.ShapeDtypeStruct((S, D), F32), grid=(S // TS,),
        in_specs=[_row_spec(), _row_spec(), _vec_spec(), _vec_spec()], out_specs=_row_spec(),
        compiler_params=_params(("parallel",)),
    )(x, y, gvec, mod)


def _postnorm_bwd(y, gvec, mod, dxo, g_row, gate_row, name):
    def body(y_ref, g_ref, mod_ref, dxo_ref, dy_ref, red_ref):
        i = pl.program_id(0)

        @pl.when(i == 0)
        def _():
            red_ref[...] = jnp.zeros_like(red_ref)

        yv = y_ref[...]
        g = g_ref[g_row:g_row + 1, :]
        r = _rms(yv)
        n = yv * r
        dxo = dxo_ref[...]
        dyn = dxo * mod_ref[gate_row:gate_row + 1, :]
        dn = dyn * g
        dy = r * (dn - n * jnp.mean(dn * n, axis=-1, keepdims=True))
        dy_ref[...] = dy.astype(BF16)
        red_ref[0:1, :] += jnp.sum(dxo * (n * g), axis=0, keepdims=True)
        red_ref[1:2, :] += jnp.sum(dyn * n, axis=0, keepdims=True)

    return pl.pallas_call(
        body, name=name,
        out_shape=(jax.ShapeDtypeStruct((S, D), BF16), jax.ShapeDtypeStruct((8, D), F32)),
        grid=(S // TS,),
        in_specs=[_row_spec(), _vec_spec(), _vec_spec(), _row_spec()],
        out_specs=(_row_spec(), _vec_spec()),
        compiler_params=_params(("arbitrary",)),
    )(y, gvec, mod, dxo)


def _loss_head(xf, target, name):
    def body(x_ref, t_ref, dx_ref, loss_ref):
        i = pl.program_id(0)

        @pl.when(i == 0)
        def _():
            loss_ref[...] = jnp.zeros_like(loss_ref)

        e = x_ref[...] - t_ref[...]
        dx_ref[...] = e / float(D)
        per_tok = jnp.mean(e * e, axis=-1, keepdims=True)
        loss_ref[0:1, 0:1] += 0.5 * jnp.sum(per_tok, axis=0, keepdims=True)

    return pl.pallas_call(
        body, name=name,
        out_shape=(jax.ShapeDtypeStruct((S, D), F32), jax.ShapeDtypeStruct((8, LANE), F32)),
        grid=(S // TS,),
        in_specs=[_row_spec(), _row_spec()],
        out_specs=(_row_spec(), pl.BlockSpec((8, LANE), lambda i: (0, 0))),
        compiler_params=_params(("arbitrary",)),
    )(xf, target)


def _relu2_epilogue(a):
    t = jnp.maximum(a, 0.0)
    return a, t * t


def _relu2_bwd_epilogue(dr, a):
    return (dr * (2.0 * jnp.maximum(a, 0.0)),)


def _merge_epilogue(pc, g0, g1, g2, pa, pb):
    return pc, jax.nn.sigmoid(g0) * pa + jax.nn.sigmoid(g1) * pb + jax.nn.sigmoid(g2) * pc


def _merge_bwd_epilogue(dm, g0, g1, g2, pa, pb, pc):
    sg = [jax.nn.sigmoid(g) for g in (g0, g1, g2)]
    return tuple(dm * s for s in sg) + tuple(dm * p * (s * (1.0 - s)) for p, s in zip((pa, pb, pc), sg))


def _shift_down(x, k, row):
    return jnp.where(row >= k, pltpu.roll(x, k, axis=0), 0.0)


def _shift_up(x, k, row):
    n = x.shape[0]
    return jnp.where(row < n - k, pltpu.roll(x, n - k, axis=0), 0.0)


def _cumsum_rows(x, row, reverse=False):
    shift = _shift_up if reverse else _shift_down
    k = 1
    while k < x.shape[0]:
        x = x + shift(x, k, row)
        k *= 2
    return x


def _full_spec(shape, idx=(0, 0)):
    return pl.BlockSpec(shape, lambda i: idx)


def _pool_window_select(lane, a2, a4, a8, a16):
    return jnp.where(lane < 64, a2, jnp.where(lane < 128, a4, jnp.where(lane < 192, a8, a16)))


def _pool_p(u, row, lane):
    t2 = u + _shift_down(u, 1, row)
    t4 = t2 + _shift_down(t2, 2, row)
    t8 = t4 + _shift_down(t4, 4, row)
    t16 = t8 + _shift_down(t8, 8, row)
    tw = _pool_window_select(lane, t2, t4, t8, t16)
    cnt = jnp.minimum((row + 1).astype(F32), _pool_window_select(lane, 2.0, 4.0, 8.0, 16.0))
    return tw / cnt - u, cnt


def _pool_fwd(z, wp_bd, pscale, name):
    def body(u_ref, w_ref, s_ref, o_ref):
        row = lax.broadcasted_iota(jnp.int32, (S, POOL_W), 0)
        lane = lax.broadcasted_iota(jnp.int32, (S, POOL_W), 1)
        p, _ = _pool_p(u_ref[...], row, lane)
        y = jnp.dot(p.astype(BF16), w_ref[...], preferred_element_type=F32)
        o_ref[...] = y * s_ref[0:1, :]

    return pl.pallas_call(
        body, name=name, out_shape=jax.ShapeDtypeStruct((S, POOL_W), F32), grid=(1,),
        in_specs=[_full_spec((S, POOL_W), (0, Z_PC // POOL_W)), _full_spec((POOL_W, POOL_W)), _full_spec((8, POOL_W))],
        out_specs=_full_spec((S, POOL_W)),
        compiler_params=_params(("arbitrary",)),
    )(z, wp_bd, pscale)


def _pool_bwd(z, wp_bd, pscale, dbr, name):
    def body(u_ref, w_ref, s_ref, dbr_ref, du_ref, dw_ref, red_ref):
        row = lax.broadcasted_iota(jnp.int32, (S, POOL_W), 0)
        lane = lax.broadcasted_iota(jnp.int32, (S, POOL_W), 1)
        p, cnt = _pool_p(u_ref[...], row, lane)
        pb = p.astype(BF16)
        y = jnp.dot(pb, w_ref[...], preferred_element_type=F32)
        dbr = dbr_ref[...]
        red_ref[...] = jnp.zeros_like(red_ref)
        red_ref[0:1, :] = jnp.sum(dbr * y, axis=0, keepdims=True)
        dy = (dbr * s_ref[0:1, :]).astype(BF16)
        dw_ref[...] = lax.dot_general(pb, dy, (((0,), (0,)), ((), ())), preferred_element_type=F32)
        dp = lax.dot_general(dy, w_ref[...], (((1,), (1,)), ((), ())), preferred_element_type=F32)
        g = dp / cnt
        a2 = g + _shift_up(g, 1, row)
        a4 = a2 + _shift_up(a2, 2, row)
        a8 = a4 + _shift_up(a4, 4, row)
        a16 = a8 + _shift_up(a8, 8, row)
        du_ref[...] = (_pool_window_select(lane, a2, a4, a8, a16) - dp).astype(BF16)

    return pl.pallas_call(
        body, name=name,
        out_shape=(jax.ShapeDtypeStruct((S, POOL_W), BF16), jax.ShapeDtypeStruct((POOL_W, POOL_W), F32),
                   jax.ShapeDtypeStruct((8, POOL_W), F32)),
        grid=(1,),
        in_specs=[_full_spec((S, POOL_W), (0, Z_PC // POOL_W)), _full_spec((POOL_W, POOL_W)), _full_spec((8, POOL_W)),
                  _full_spec((S, POOL_W))],
        out_specs=(_full_spec((S, POOL_W)), _full_spec((POOL_W, POOL_W)), _full_spec((8, POOL_W))),
        compiler_params=_params(("arbitrary",)),
    )(z, wp_bd, pscale, dbr)


def _conv_specs():
    base = Z_PC // CONV_W
    return [_full_spec((S, CONV_W), (0, base + 1)), _full_spec((S, CONV_W), (0, base + 2)),
            _full_spec((S, CONV_W), (0, base + 3)), _full_spec((8, CONV_W))]


def _conv_fwd(z, cw, name):
    def body(h_ref, b_ref, c_ref, w_ref, o_ref):
        row = lax.broadcasted_iota(jnp.int32, (S, CONV_W), 0)
        u = c_ref[...] * h_ref[...]
        y = (w_ref[0:1, :] * _shift_down(u, 2, row) + w_ref[1:2, :] * _shift_down(u, 1, row) + w_ref[2:3, :] * u)
        o_ref[...] = b_ref[...] * y

    return pl.pallas_call(
        body, name=name, out_shape=jax.ShapeDtypeStruct((S, CONV_W), F32), grid=(1,),
        in_specs=_conv_specs(), out_specs=_full_spec((S, CONV_W)),
        compiler_params=_params(("arbitrary",)),
    )(z, z, z, cw)


def _conv_bwd(z, cw, dbr, name):
    def body(h_ref, b_ref, c_ref, w_ref, dbr_ref, d_ref, red_ref):
        row = lax.broadcasted_iota(jnp.int32, (S, CONV_W), 0)
        h, cg = h_ref[...], c_ref[...]
        u = cg * h
        u1 = _shift_down(u, 1, row)
        u2 = _shift_down(u, 2, row)
        y = w_ref[0:1, :] * u2 + w_ref[1:2, :] * u1 + w_ref[2:3, :] * u
        dbr = dbr_ref[...]
        dy = dbr * b_ref[...]
        du = w_ref[2:3, :] * dy + w_ref[1:2, :] * _shift_up(dy, 1, row) + w_ref[0:1, :] * _shift_up(dy, 2, row)
        d_ref[:, 0:CONV_W] = (du * cg).astype(BF16)
        d_ref[:, CONV_W:2 * CONV_W] = (dbr * y).astype(BF16)
        d_ref[:, 2 * CONV_W:3 * CONV_W] = (du * h).astype(BF16)
        red_ref[...] = jnp.zeros_like(red_ref)
        red_ref[0:1, :] = jnp.sum(dy * u2, axis=0, keepdims=True)
        red_ref[1:2, :] = jnp.sum(dy * u1, axis=0, keepdims=True)
        red_ref[2:3, :] = jnp.sum(dy * u, axis=0, keepdims=True)

    return pl.pallas_call(
        body, name=name,
        out_shape=(jax.ShapeDtypeStruct((S, 3 * CONV_W), BF16), jax.ShapeDtypeStruct((8, CONV_W), F32)),
        grid=(1,),
        in_specs=_conv_specs() + [_full_spec((S, CONV_W))],
        out_specs=(_full_spec((S, 3 * CONV_W)), _full_spec((8, CONV_W))),
        compiler_params=_params(("arbitrary",)),
    )(z, z, z, cw, dbr)


_NT = (((1,), (1,)), ((), ()))
_TN = (((0,), (0,)), ((), ()))
N_HEAD = 2 * N_PAIR


def _split3(x):
    hi = x.astype(BF16).astype(F32)
    mid = (x - hi).astype(BF16).astype(F32)
    lo = (x - hi - mid).astype(BF16).astype(F32)
    return hi, mid, lo


def _spare(lane, e, k):
    return lane == 64 * (1 - e) + k


def _spare3(lane, e, k):
    base = 64 * (1 - e) + k
    return (lane >= base) & (lane < base + 3)


def _put3(lane, e, k, pieces, rest):
    out = rest
    for n, piece in enumerate(pieces):
        out = jnp.where(_spare(lane, e, k + n), piece, out)
    return out


def _attn_prep(z, bf, name):
    def body(q_ref, k_ref, v_ref, f_ref, b_ref, qa_ref, ka_ref, va_ref, kat_ref):
        p = pl.program_id(0)
        row = lax.broadcasted_iota(jnp.int32, (S, LANE), 0)
        lane = lax.broadcasted_iota(jnp.int32, (S, LANE), 1)
        xv = f_ref[...] + b_ref[0:1, :]
        ls = jnp.minimum(xv, 0.0) - jnp.log(1.0 + jnp.exp(-jnp.abs(xv)))
        cum = _cumsum_rows(jnp.where(lane < N_HEAD, ls, 0.0), row)
        q, k, v = q_ref[...], k_ref[...], v_ref[...]
        for e in range(2):
            head = (lane >= 64) if e else (lane < 64)
            f = jnp.sum(jnp.where(lane == 2 * p + e, cum, 0.0), axis=1, keepdims=True)
            pieces = _split3(f)
            qa = jnp.where(head, q * ATT_SCALE, _put3(lane, e, 0, pieces, jnp.where(_spare3(lane, e, 3), 1.0, 0.0)))
            ones = jnp.where(_spare3(lane, e, 0) | _spare3(lane, e, 6), 1.0, 0.0)
            ka = jnp.where(head, k, _put3(lane, e, 3, [-x for x in pieces], ones))
            va = jnp.where(head, v, jnp.where(_spare3(lane, e, 0), 1.0, 0.0))
            qa_ref[e] = qa.astype(BF16)
            ka_ref[e] = ka.astype(BF16)
            va_ref[e] = va.astype(BF16)
            kat_ref[e] = ka.T.astype(BF16)

    qb, kb, vb = Z_Q // LANE, Z_K // LANE, Z_V // LANE
    heads = jax.ShapeDtypeStruct((N_HEAD, S, LANE), BF16)
    pair = pl.BlockSpec((2, S, LANE), lambda p: (p, 0, 0))
    return pl.pallas_call(
        body, name=name,
        out_shape=(heads, heads, heads, jax.ShapeDtypeStruct((N_HEAD, LANE, S), BF16)),
        grid=(N_PAIR,),
        in_specs=[pl.BlockSpec((S, LANE), lambda p: (0, qb + p)), pl.BlockSpec((S, LANE), lambda p: (0, kb + p)),
                  pl.BlockSpec((S, LANE), lambda p: (0, vb + p)), pl.BlockSpec((S, LANE), lambda p: (0, Z_F // LANE)),
                  pl.BlockSpec((8, LANE), lambda p: (0, 0))],
        out_specs=(pair, pair, pair, pl.BlockSpec((2, LANE, S), lambda p: (p, 0, 0))),
        compiler_params=_params(("parallel",)),
    )(z, z, z, z, bf)


def _attn_bwd_prep(qa, o, lse, do, name):
    def body(qa_ref, o_ref, lse_ref, do_ref, qa2_ref, doa_ref):
        lane = lax.broadcasted_iota(jnp.int32, (S, LANE), 1)
        dov, ov, lsev = do_ref[...], o_ref[...], lse_ref[...]
        for e in range(2):
            head = (lane >= 64) if e else (lane < 64)
            dsum = jnp.sum(jnp.where(head, dov * ov, 0.0), axis=1, keepdims=True)
            doa_ref[e] = jnp.where(head, dov, _put3(lane, e, 0, [-x for x in _split3(dsum)], 0.0)).astype(BF16)
            lse_col = lsev[:, 64 * e:64 * e + 1]
            qa2_ref[e] = _put3(lane, e, 6, [-x for x in _split3(lse_col)], qa_ref[e].astype(F32)).astype(BF16)

    heads = jax.ShapeDtypeStruct((N_HEAD, S, LANE), BF16)
    pair = pl.BlockSpec((2, S, LANE), lambda p: (p, 0, 0))
    cols = pl.BlockSpec((S, LANE), lambda p: (0, p))
    return pl.pallas_call(
        body, name=name, out_shape=(heads, heads), grid=(N_PAIR,),
        in_specs=[pair, cols, cols, cols], out_specs=(pair, pair),
        compiler_params=_params(("parallel",)),
    )(qa, o, lse, do)


def _attn_bwd_post(z, bf, dqt, dka, dva, name):
    def body(f_ref, b_ref, dqt_ref, dk_ref, dv_ref, dq_out, dk_out, dv_out, dfl_ref, red_ref, dcum_ref):
        p = pl.program_id(0)

        @pl.when(p == 0)
        def _():
            dcum_ref[...] = jnp.zeros_like(dcum_ref)

        row = lax.broadcasted_iota(jnp.int32, (S, LANE), 0)
        lane = lax.broadcasted_iota(jnp.int32, (S, LANE), 1)
        dqa = [dqt_ref[e].T for e in range(2)]
        dq_out[...] = (jnp.where(lane < 64, dqa[0], dqa[1]) * ATT_SCALE).astype(BF16)
        dk_out[...] = jnp.where(lane < 64, dk_ref[0], dk_ref[1]).astype(BF16)
        dv_out[...] = jnp.where(lane < 64, dv_ref[0], dv_ref[1]).astype(BF16)
        for e in range(2):
            d_query = jnp.sum(jnp.where(_spare(lane, e, 0), dqa[e], 0.0), axis=1, keepdims=True)
            d_key = jnp.sum(jnp.where(_spare(lane, e, 3), dk_ref[e], 0.0), axis=1, keepdims=True)
            dcum_ref[...] += jnp.where(lane == 2 * p + e, d_query - d_key, 0.0)

        @pl.when(p == N_PAIR - 1)
        def _():
            dls = _cumsum_rows(dcum_ref[...], row, reverse=True)
            xv = f_ref[...] + b_ref[0:1, :]
            dx = jnp.where(lane < N_HEAD, dls * jax.nn.sigmoid(-xv), 0.0)
            dfl_ref[...] = dx.astype(BF16)
            red_ref[...] = jnp.zeros_like(red_ref)
            red_ref[0:1, :] = jnp.sum(dx, axis=0, keepdims=True)

    wide = jax.ShapeDtypeStruct((S, N_PAIR * LANE), BF16)
    cols = pl.BlockSpec((S, LANE), lambda p: (0, p))
    pair = pl.BlockSpec((2, S, LANE), lambda p: (p, 0, 0))
    return pl.pallas_call(
        body, name=name,
        out_shape=(wide, wide, wide, jax.ShapeDtypeStruct((S, LANE), BF16), jax.ShapeDtypeStruct((8, LANE), F32)),
        grid=(N_PAIR,),
        in_specs=[pl.BlockSpec((S, LANE), lambda p: (0, Z_F // LANE)), pl.BlockSpec((8, LANE), lambda p: (0, 0)),
                  pl.BlockSpec((2, LANE, S), lambda p: (p, 0, 0)), pair, pair],
        out_specs=(cols, cols, cols, pl.BlockSpec((S, LANE), lambda p: (0, 0)), pl.BlockSpec((8, LANE), lambda p: (0, 0))),
        scratch_shapes=[pltpu.VMEM((S, LANE), F32)],
        compiler_params=_params(("arbitrary",)),
    )(z, bf, dqt, dka, dva)


def _attn_fwd(qa, ka, va, name):
    tq, tk = TQ_FWD, TQ
    ratio = tq // tk

    def body(qa_ref, ka_ref, va_ref, o_ref, lse_ref):
        i = pl.program_id(1)
        lane = lax.broadcasted_iota(jnp.int32, (tq, LANE), 1)
        row = lax.broadcasted_iota(jnp.int32, (tq, tk), 0)
        col = lax.broadcasted_iota(jnp.int32, (tq, tk), 1)
        qs = [qa_ref[0], qa_ref[1]]

        def block(j, carry, masked):
            off = pl.multiple_of(j * tk, tk)
            out = []
            for e in range(2):
                m, acc = carry[e]
                s = lax.dot_general(qs[e], ka_ref[e, pl.ds(off, tk), :], _NT, preferred_element_type=F32)
                if masked:
                    s = jnp.where(col + (j - ratio * i) * tk > row, NEG_INF, s)
                mn = jnp.maximum(m, jnp.max(s, axis=1, keepdims=True))
                p = jnp.exp(s - mn).astype(BF16)
                acc = jnp.exp(m - mn) * acc + jnp.dot(p, va_ref[e, pl.ds(off, tk), :], preferred_element_type=F32)
                out.append((mn, acc))
            return tuple(out)

        init = (jnp.full((tq, 1), NEG_INF, F32), jnp.zeros((tq, LANE), F32))
        carry = lax.fori_loop(0, ratio * i, lambda j, c: block(j, c, False), (init, init))
        for d in range(ratio):
            carry = block(ratio * i + d, carry, True)
        res = []
        for e in range(2):
            m, acc = carry[e]
            l = jnp.sum(jnp.where(_spare(lane, e, 0), acc, 0.0), axis=1, keepdims=True)
            res.append((acc / l, m + jnp.log(l)))
        o_ref[...] = jnp.where(lane < 64, res[0][0], res[1][0])
        lse_ref[...] = jnp.where(lane < 64, res[0][1], res[1][1])

    out = jax.ShapeDtypeStruct((S, N_PAIR * LANE), F32)
    return pl.pallas_call(
        body, name=name, out_shape=(out, out), grid=(N_PAIR, S // tq),
        in_specs=[pl.BlockSpec((2, tq, LANE), lambda p, i: (p, i, 0)), pl.BlockSpec((2, S, LANE), lambda p, i: (p, 0, 0)),
                  pl.BlockSpec((2, S, LANE), lambda p, i: (p, 0, 0))],
        out_specs=(pl.BlockSpec((tq, LANE), lambda p, i: (i, p)), pl.BlockSpec((tq, LANE), lambda p, i: (i, p))),
        compiler_params=_params(("parallel", "parallel")),
    )(qa, ka, va)


def _attn_bwd(qa2, ka, va, kat, doa, name):
    nq = S // TQ

    def body(qa_ref, ka_ref, va_ref, kat_ref, doa_ref, dqt_ref, dk_ref, dv_ref):
        j = pl.program_id(1)

        @pl.when(j == 0)
        def _():
            dqt_ref[...] = jnp.zeros_like(dqt_ref)

        key = lax.broadcasted_iota(jnp.int32, (TQ, TQ), 0)
        qry = lax.broadcasted_iota(jnp.int32, (TQ, TQ), 1)
        kav, vav, katv = [ka_ref[0], ka_ref[1]], [va_ref[0], va_ref[1]], [kat_ref[0], kat_ref[1]]

        def block(i, carry, masked):
            off = pl.multiple_of(i * TQ, TQ)
            out = []
            for e in range(2):
                dk_acc, dv_acc = carry[e]
                qav = qa_ref[e, pl.ds(off, TQ), :]
                doav = doa_ref[e, pl.ds(off, TQ), :]
                s_t = lax.dot_general(kav[e], qav, _NT, preferred_element_type=F32)
                if masked:
                    s_t = jnp.where(key > qry, NEG_INF, s_t)
                p_t = jnp.exp(s_t)
                ds_t = p_t * lax.dot_general(vav[e], doav, _NT, preferred_element_type=F32)
                dsb = ds_t.astype(BF16)
                dv_acc = dv_acc + jnp.dot(p_t.astype(BF16), doav, preferred_element_type=F32)
                dk_acc = dk_acc + jnp.dot(dsb, qav, preferred_element_type=F32)
                dqt_ref[e, :, pl.ds(off, TQ)] += jnp.dot(katv[e], dsb, preferred_element_type=F32)
                out.append((dk_acc, dv_acc))
            return tuple(out)

        zero = (jnp.zeros((TQ, LANE), F32), jnp.zeros((TQ, LANE), F32))
        carry = block(j, (zero, zero), True)
        carry = lax.fori_loop(j + 1, nq, lambda i, c: block(i, c, False), carry)
        for e in range(2):
            dk_ref[e], dv_ref[e] = carry[e]

    full = pl.BlockSpec((2, S, LANE), lambda p, j: (p, 0, 0))
    blk = pl.BlockSpec((2, TQ, LANE), lambda p, j: (p, j, 0))
    acc = jax.ShapeDtypeStruct((N_HEAD, S, LANE), F32)
    return pl.pallas_call(
        body, name=name,
        out_shape=(jax.ShapeDtypeStruct((N_HEAD, LANE, S), F32), acc, acc),
        grid=(N_PAIR, nq),
        in_specs=[full, blk, blk, pl.BlockSpec((2, LANE, TQ), lambda p, j: (p, 0, j)), full],
        out_specs=(pl.BlockSpec((2, LANE, S), lambda p, j: (p, 0, 0)), blk, blk),
        compiler_params=_params(("arbitrary", "arbitrary")),
    )(qa2, ka, va, kat, doa)


ADA_ROWS = 16


def _ada_fwd(c_pad, w_ada, b_cols, name):
    def body(c_ref, w_ref, b_ref, o_ref):
        cv = c_ref[...]
        sc = (cv * jax.nn.sigmoid(cv)).astype(BF16)
        o_ref[0] = jnp.dot(sc, w_ref[0].astype(BF16), preferred_element_type=F32) + b_ref[0, 0:1, :]

    return pl.pallas_call(
        body, name=name, out_shape=jax.ShapeDtypeStruct((DEPTH, ADA_ROWS, ADA_COLS), F32), grid=(DEPTH,),
        in_specs=[pl.BlockSpec((ADA_ROWS, D), lambda l: (0, 0)), pl.BlockSpec((1, D, ADA_COLS), lambda l: (l, 0, 0)),
                  pl.BlockSpec((1, 8, ADA_COLS), lambda l: (l, 0, 0))],
        out_specs=pl.BlockSpec((1, ADA_ROWS, ADA_COLS), lambda l: (l, 0, 0)),
        compiler_params=_params(("parallel",)),
    )(c_pad, w_ada, b_cols)


def _ada_bwd(c_pad, dmod_cols, name):
    def body(c_ref, d_ref, o_ref):
        cv = c_ref[...]
        sc = (cv * jax.nn.sigmoid(cv)).astype(BF16)
        o_ref[0] = lax.dot_general(sc, d_ref[0].astype(BF16), _TN, preferred_element_type=F32)

    return pl.pallas_call(
        body, name=name, out_shape=jax.ShapeDtypeStruct((DEPTH, D, ADA_COLS), F32), grid=(DEPTH,),
        in_specs=[pl.BlockSpec((ADA_ROWS, D), lambda l: (0, 0)), pl.BlockSpec((1, ADA_ROWS, ADA_COLS), lambda l: (l, 0, 0))],
        out_specs=pl.BlockSpec((1, D, ADA_COLS), lambda l: (l, 0, 0)),
        compiler_params=_params(("parallel",)),
    )(c_pad, dmod_cols)


def _adamw_math(w, g, m, v):
    m = B1 * m + (1.0 - B1) * g
    v = B2 * v + (1.0 - B2) * (g * g)
    m_hat = m / (1.0 - B1 ** STEP)
    v_hat = v / (1.0 - B2 ** STEP)
    delta = -LR * (m_hat / (jnp.sqrt(v_hat) + EPS) + WD * w)
    return delta, m, v


def _row_tile(rows, target=256):
    best = 8
    for t in range(8, min(rows, target) + 1, 8):
        if rows % t == 0:
            best = t
    return best


def _adamw(w, g, m, v, name):
    layers, rows, cols = w.shape
    tr = _row_tile(rows)
    spec = pl.BlockSpec((1, tr, cols), lambda l, i: (l, i, 0))

    def body(w_ref, g_ref, m_ref, v_ref, d_ref, nm_ref, nv_ref):
        d_ref[...], nm_ref[...], nv_ref[...] = _adamw_math(w_ref[...], g_ref[...], m_ref[...], v_ref[...])

    out = jax.ShapeDtypeStruct(w.shape, F32)
    return pl.pallas_call(
        body, name=name, out_shape=(out, out, out), grid=(layers, rows // tr),
        in_specs=[spec] * 4, out_specs=(spec,) * 3, compiler_params=_params(("parallel", "parallel")),
    )(w, g, m, v)


def _sum_slabs(x, name):
    n, rows, _ = x.shape
    tr = _row_tile(rows)

    def body(x_ref, o_ref):
        acc = x_ref[0]
        for k in range(1, n):
            acc = acc + x_ref[k]
        o_ref[...] = acc

    return pl.pallas_call(
        body, name=name, out_shape=jax.ShapeDtypeStruct((rows, D), F32), grid=(rows // tr,),
        in_specs=[pl.BlockSpec((n, tr, D), lambda i: (0, i, 0))], out_specs=pl.BlockSpec((tr, D), lambda i: (i, 0)),
        compiler_params=_params(("parallel",)),
    )(x)


_ANY = pl.BlockSpec(memory_space=pl.ANY)
MESH = pl.DeviceIdType.MESH


def _on_sequencer(body, out_shape, sems, operands, after, sequencer_id, name):
    n = len(operands)

    def ordered_body(*refs):
        body(*refs[:n], *refs[n + 1:])

    extra = [] if after is None else [after]
    return pl.kernel(
        body if after is None else ordered_body, out_type=out_shape,
        mesh=plsc.ScalarSubcoreMesh(axis_name="sequencer", num_cores=1), scratch_types=sems,
        compiler_params=pltpu.CompilerParams(collective_id=sequencer_id), name=name)(*operands, *extra)


def _all_gather(xs, name, sequencer_id=None, after=None):
    n = len(xs)

    def body(*refs):
        x_refs, out_refs = refs[:n], refs[n:2 * n]
        send_sems, recv_sems, local_sems = refs[2 * n:]
        x_, y_, c_ = lax.axis_index("x"), lax.axis_index("y"), lax.axis_index("c")
        me, sibling = (x_, y_, c_), (x_, y_, 1 - c_)
        chips = [(1 - x_, y_), (x_, 1 - y_), (1 - x_, 1 - y_)]
        if sequencer_id is not None:
            barrier = pltpu.get_barrier_semaphore()
            peers = [sibling] + [(*chip, pc) for chip in chips for pc in (c_, 1 - c_)]
            for peer in peers:
                pl.semaphore_signal(barrier, inc=1, device_id=peer, device_id_type=MESH)
            pl.semaphore_wait(barrier, len(peers))

        def slot(a, px, py, pc):
            return out_refs[a].at[4 * px + 2 * py + pc]

        def copy(a, k, block, to, src=None):
            return pltpu.make_async_remote_copy(
                src_ref=slot(a, *block) if src is None else src, dst_ref=slot(a, *block),
                send_sem=send_sems.at[7 * a + k], recv_sem=recv_sems.at[7 * a + k], device_id=to, device_id_type=MESH)

        mine = [pltpu.make_async_copy(x_refs[a], slot(a, *me), local_sems.at[a]) for a in range(n)]
        for cp in mine:
            cp.start()
        first = []
        for a in range(n):
            first.append(copy(a, 0, me, sibling, src=x_refs[a]))
            first += [copy(a, 1 + j, me, (*chip, c_), src=x_refs[a]) for j, chip in enumerate(chips)]
        for cp in first:
            cp.start()
        passed = []
        for j, chip in enumerate(chips):
            for a in range(n):
                copy(a, 1 + j, (*chip, c_), me).wait_recv()
                passed.append(copy(a, 4 + j, (*chip, c_), sibling))
                passed[-1].start()
        for a in range(n):
            copy(a, 0, sibling, me).wait_recv()
        for j, chip in enumerate(chips):
            for a in range(n):
                copy(a, 4 + j, (*chip, 1 - c_), me).wait_recv()
        for cp in first + passed:
            cp.wait_send()
        for cp in mine:
            cp.wait()

    out_shape = [jax.ShapeDtypeStruct((N_DEV,) + x.shape, x.dtype) for x in xs]
    sems = [pltpu.SemaphoreType.DMA((7 * n,)), pltpu.SemaphoreType.DMA((7 * n,)), pltpu.SemaphoreType.DMA((n,))]
    if sequencer_id is not None:
        return _on_sequencer(body, out_shape, sems, xs, after, sequencer_id, name)
    return pl.pallas_call(
        body, name=name, out_shape=out_shape, in_specs=[_ANY] * n, out_specs=[_ANY] * n, scratch_shapes=sems)(*xs)


def _sibling_exchange(gs, name, sequencer_id=None, after=None):
    n = len(gs)

    def body(*refs):
        g_refs, p_refs = refs[:n], refs[n:2 * n]
        send_sems, recv_sems = refs[2 * n:]
        x_, y_, c_ = lax.axis_index("x"), lax.axis_index("y"), lax.axis_index("c")
        if sequencer_id is not None:
            barrier = pltpu.get_barrier_semaphore()
            pl.semaphore_signal(barrier, inc=1, device_id=(x_, y_, 1 - c_), device_id_type=MESH)
            pl.semaphore_wait(barrier, 1)
        copies = [pltpu.make_async_remote_copy(
            src_ref=g_refs[a].at[2 * k + (1 - c_)], dst_ref=p_refs[a].at[k], send_sem=send_sems.at[4 * a + k],
            recv_sem=recv_sems.at[4 * a + k], device_id=(x_, y_, 1 - c_), device_id_type=MESH)
            for a in range(n) for k in range(4)]
        for cp in copies:
            cp.start()
        for cp in copies:
            cp.wait()

    out_shape = [jax.ShapeDtypeStruct((4,) + g.shape[1:], g.dtype) for g in gs]
    sems = [pltpu.SemaphoreType.DMA((4 * n,)), pltpu.SemaphoreType.DMA((4 * n,))]
    if sequencer_id is not None:
        return _on_sequencer(body, out_shape, sems, gs, after, sequencer_id, name)
    return pl.pallas_call(
        body, name=name, out_shape=out_shape, in_specs=[_ANY] * n, out_specs=[_ANY] * n, scratch_shapes=sems)(*gs)


def _slab_tiles(rows, cols):
    if rows % 8 == 0:
        return _row_tile(rows), cols
    return rows, 2 * LANE


def _pair_sums(g, p, route, name):
    _, rows, cols = g.shape
    tr, tc = _slab_tiles(rows, cols)

    def body(route_ref, g_ref, p_ref, t_ref):
        t_ref[...] = (g_ref[...] + p_ref[...]).astype(BF16)

    return pl.pallas_call(
        body, name=name, out_shape=jax.ShapeDtypeStruct((3, rows, cols), BF16),
        grid_spec=pltpu.PrefetchScalarGridSpec(
            num_scalar_prefetch=1, grid=(3, rows // tr, cols // tc),
            in_specs=[pl.BlockSpec((1, tr, tc), lambda r, i, j, route_ref: (2 * route_ref[1 + r] + route_ref[0], i, j)),
                      pl.BlockSpec((1, tr, tc), lambda r, i, j, route_ref: (route_ref[1 + r], i, j))],
            out_specs=pl.BlockSpec((1, tr, tc), lambda r, i, j, route_ref: (r, i, j))),
        compiler_params=_params(("parallel", "parallel", "parallel")),
    )(route, g, p)


def _chip_exchange(ts, name, sequencer_id=None, after=None):
    n = len(ts)

    def body(*refs):
        t_refs, l_refs = refs[:n], refs[n:2 * n]
        send_sems, recv_sems = refs[2 * n:]
        x_, y_, c_ = lax.axis_index("x"), lax.axis_index("y"), lax.axis_index("c")
        chips = [(1 - x_, y_), (x_, 1 - y_), (1 - x_, 1 - y_)]
        if sequencer_id is not None:
            barrier = pltpu.get_barrier_semaphore()
            for px, py in chips:
                pl.semaphore_signal(barrier, inc=1, device_id=(px, py, c_), device_id_type=MESH)
            pl.semaphore_wait(barrier, len(chips))
        copies = [pltpu.make_async_remote_copy(
            src_ref=t_refs[a].at[r], dst_ref=l_refs[a].at[r], send_sem=send_sems.at[3 * a + r],
            recv_sem=recv_sems.at[3 * a + r], device_id=(px, py, c_), device_id_type=MESH)
            for a in range(n) for r, (px, py) in enumerate(chips)]
        for cp in copies:
            cp.start()
        for cp in copies:
            cp.wait()

    out_shape = [jax.ShapeDtypeStruct((3,) + t.shape[1:], t.dtype) for t in ts]
    sems = [pltpu.SemaphoreType.DMA((3 * n,)), pltpu.SemaphoreType.DMA((3 * n,))]
    if sequencer_id is not None:
        return _on_sequencer(body, out_shape, sems, ts, after, sequencer_id, name)
    return pl.pallas_call(
        body, name=name, out_shape=out_shape, in_specs=[_ANY] * n, out_specs=[_ANY] * n, scratch_shapes=sems)(*ts)


def _reduce_adamw(gs, ps, landed, place, w, m, v, name):
    layers, rows, cols = w.shape
    assert layers == DEPTH == 2
    tr, tc = _slab_tiles(rows, cols)
    nr, nc = rows // tr, cols // tc
    spec = pl.BlockSpec((1, tr, tc), lambda l, i, j, place_ref: (l, i, j))

    def own(layer, which):
        pi, pj = (nr - 1, nc - 1) if layer == 0 else (0, 0)

        def index(l, i, j, place_ref):
            lead = 0 if which is None else place_ref[which]
            return lead, jnp.where(l == layer, i, pi), jnp.where(l == layer, j, pj)

        return pl.BlockSpec((3 if which is None else 1, tr, tc), index)

    def body(place_ref, g0_ref, p0_ref, l0_ref, g1_ref, p1_ref, l1_ref, w_ref, m_ref, v_ref,
             g_ref, d_ref, nm_ref, nv_ref):
        def update(own_ref, sib_ref, l_ref):
            g = own_ref[0] + sib_ref[0] + l_ref[0].astype(F32) + l_ref[1].astype(F32) + l_ref[2].astype(F32)
            g_ref[0] = g
            d_ref[0], nm_ref[0], nv_ref[0] = _adamw_math(w_ref[0], g, m_ref[0], v_ref[0])

        @pl.when(pl.program_id(0) == 0)
        def _():
            update(g0_ref, p0_ref, l0_ref)

        @pl.when(pl.program_id(0) == 1)
        def _():
            update(g1_ref, p1_ref, l1_ref)

    out = jax.ShapeDtypeStruct(w.shape, F32)
    return pl.pallas_call(
        body, name=name, out_shape=(out, out, out, out),
        grid_spec=pltpu.PrefetchScalarGridSpec(
            num_scalar_prefetch=1, grid=(DEPTH, nr, nc),
            in_specs=[own(0, 0), own(0, 1), own(0, None), own(1, 0), own(1, 1), own(1, None), spec, spec, spec],
            out_specs=(spec, spec, spec, spec)),
        compiler_params=_params(("arbitrary", "arbitrary", "arbitrary")),
    )(place, gs[0], ps[0], landed[0], gs[1], ps[1], landed[1], w, m, v)


def _pack(pieces, row_multiple, dtype, cols=D, rows=None):
    flat = jnp.concatenate([p.astype(dtype).reshape(-1) for p in pieces])
    if rows is None:
        rows = -(-flat.shape[0] // cols)
        rows = -(-rows // row_multiple) * row_multiple
    flat = jnp.pad(flat, (0, rows * cols - flat.shape[0]))
    return flat.reshape(rows, cols)


def _unpack(flat, shapes, lead=()):
    out, off = [], 0
    for shp in shapes:
        n = 1
        for s_ in shp:
            n *= s_
        out.append(lax.slice_in_dim(flat, off, off + n, axis=len(lead)).reshape(lead + tuple(shp)))
        off += n
    return out


def _z_rows_from_in(wt):
    pad = jnp.zeros((NZ - IN_COLS, wt.shape[1]), wt.dtype)
    return jnp.concatenate([wt[1544:2568], wt[2568:5640], wt[0:1536], wt[1536:1544], pad], axis=0)


def _in_rows_from_z(wt):
    return jnp.concatenate([wt[Z_Q:Z_Q + 1536], wt[Z_F:Z_F + 8], wt[Z_PC:Z_PC + 1024], wt[Z_G:Z_G + 3072]], axis=0)


def _pad_rows(v, rows=8):
    return jnp.pad(v, ((0, rows - v.shape[0]), (0, 0)))


def _layer_fwd(l, x, wts, gvec, mod):
    tag = f"l{l}"
    h = _prenorm_fwd(x, gvec, mod, 0, 0, 1, f"prenorm_mix_{tag}")
    z = _matmul(h, wts["w_in_t"], "nt", f"in_proj_{tag}", tn=1152)
    qa, ka, va, kat = _attn_prep(z, wts["b_f"], f"attn_prep_{tag}")
    o, lse = _attn_fwd(qa, ka, va, f"attn_{tag}")
    br_b = _pool_fwd(z, wts["wp_bd"], wts["pool_scale"], f"pool_{tag}")
    br_c = _conv_fwd(z, wts["conv_w"], f"conv_{tag}")
    pa = _matmul(o, wts["wa"], "nn", f"proj_a_{tag}")
    pb = _matmul(br_b, wts["wb"], "nn", f"proj_b_{tag}")
    gates = [(z, Z_G + k * D) for k in range(3)]
    pc, merged = _matmul(br_c, wts["wc"], "nn", f"proj_c_merge_{tag}", tm=512, tn=512,
                         extra=gates + [(pa, 0), (pb, 0)], epilogue=_merge_epilogue, out_dtypes=(F32, BF16))
    y = _matmul(merged, wts["w_out"], "nn", f"out_proj_{tag}")
    x1 = _postnorm_fwd(x, y, gvec, mod, 1, 2, f"postnorm_mix_{tag}")
    h2 = _prenorm_fwd(x1, gvec, mod, 2, 3, 4, f"prenorm_ff_{tag}")
    a, r = _matmul(h2, wts["w_ff1"], "nn", f"ff1_{tag}", b_col_shards=True, epilogue=_relu2_epilogue,
                   out_dtypes=(F32, BF16))
    y2 = _matmul(r, wts["w_ff2"], "nn", f"ff2_{tag}")
    x2 = _postnorm_fwd(x1, y2, gvec, mod, 3, 5, f"postnorm_ff_{tag}")
    saved = dict(x=x, h=h, z=z, qa=qa, ka=ka, va=va, kat=kat, o=o, lse=lse, br_b=br_b, br_c=br_c, pa=pa, pb=pb, pc=pc,
                 merged=merged, y=y, x1=x1, h2=h2, a=a, r=r, y2=y2)
    return x2, saved


def _ffn_bwd(l, dx2, sv, wts, gvec, mod, midpoint):
    tag = f"l{l}"
    dy2, red_post_ff = _postnorm_bwd(sv["y2"], gvec, mod, dx2, 3, 5, f"postnorm_ff_bwd_{tag}")
    da = midpoint(_matmul(dy2, wts["w_ff2"], "nt", f"ff2_dx_{tag}", extra=[(sv["a"], 0)],
                          epilogue=_relu2_bwd_epilogue, out_dtypes=(BF16,))[0])
    d_w_ff2 = _matmul(sv["r"], dy2, "tn", f"ff2_dw_{tag}")
    dh2 = _matmul(da, wts["w_ff1"], "nt", f"ff1_dx_{tag}", b_col_shards=True)
    d_w_ff1 = _matmul(sv["h2"], da, "tn", f"ff1_dw_{tag}", out_col_shards=True)
    dx1, red_pre_ff = _prenorm_bwd(sv["x1"], gvec, mod, dh2, dx2, 2, 4, f"prenorm_ff_bwd_{tag}")
    return dx1, [d_w_ff1, d_w_ff2.reshape(N_DEV, D_FF // N_DEV, D)], (red_pre_ff, red_post_ff)


def _mixer_bwd(l, dx1, sv, wts, gvec, mod, ffn_reds, midpoint):
    tag = f"l{l}"
    red_pre_ff, red_post_ff = ffn_reds
    dy, red_post_mix = _postnorm_bwd(sv["y"], gvec, mod, dx1, 1, 2, f"postnorm_mix_bwd_{tag}")
    gates = [(sv["z"], Z_G + k * D) for k in range(3)]
    dpa, dpb, dpc, *dgl = _matmul(dy, wts["w_out"], "nt", f"out_proj_dx_{tag}", tm=512, tn=512,
                                  extra=gates + [(sv["pa"], 0), (sv["pb"], 0), (sv["pc"], 0)],
                                  epilogue=_merge_bwd_epilogue, out_dtypes=(BF16,) * 6)
    d_w_out = _matmul(sv["merged"], dy, "tn", f"out_proj_dw_{tag}")
    dpa = midpoint(dpa)
    do = _matmul(dpa, wts["wa"], "nt", f"proj_a_dx_{tag}")
    dbr_b = _matmul(dpb, wts["wb"], "nt", f"proj_b_dx_{tag}")
    dbr_c = _matmul(dpc, wts["wc"], "nt", f"proj_c_dx_{tag}")
    d_wa = _matmul(sv["o"], dpa, "tn", f"proj_a_dw_{tag}")
    d_wb = _matmul(sv["br_b"], dpb, "tn", f"proj_b_dw_{tag}")
    d_wc = _matmul(sv["br_c"], dpc, "tn", f"proj_c_dw_{tag}")
    d_w_branch = jnp.concatenate([d_wa, d_wb, d_wc], axis=0)

    dpu, d_wp_bd, red_pool = _pool_bwd(sv["z"], wts["wp_bd"], wts["pool_scale"], dbr_b, f"pool_bwd_{tag}")
    dconv, red_conv = _conv_bwd(sv["z"], wts["conv_w"], dbr_c, f"conv_bwd_{tag}")
    qa2, doa = _attn_bwd_prep(sv["qa"], sv["o"], sv["lse"], do, f"attn_bwd_prep_{tag}")
    dqt, dka, dva = _attn_bwd(qa2, sv["ka"], sv["va"], sv["kat"], doa, f"attn_bwd_{tag}")
    dq, dk, dv, dfl, red_f = _attn_bwd_post(sv["z"], wts["b_f"], dqt, dka, dva, f"attn_bwd_post_{tag}")
    dz = jnp.concatenate([dpu, dconv, *dgl, dq, dk, dv, dfl], axis=1)
    dh = _matmul(dz, wts["w_in_t"], "nn", f"in_proj_dx_{tag}", tk=1152)
    d_w_in_t = _matmul(dz, sv["h"], "tn", f"in_proj_dw_{tag}", tm=1152)
    dx0, red_pre_mix = _prenorm_bwd(sv["x"], gvec, mod, dh, dx1, 0, 1, f"prenorm_mix_bwd_{tag}")

    rows = D // N_DEV
    big = [_in_rows_from_z(d_w_in_t).reshape(N_DEV, IN_SHARD, D), d_w_branch.reshape(N_DEV, rows, D),
           d_w_out.reshape(N_DEV, rows, D)]
    d_w_pool = jnp.stack([d_wp_bd[64 * g:64 * (g + 1), 64 * g:64 * (g + 1)] for g in range(4)])
    small = dict(
        mod=jnp.stack([red_pre_mix[0], red_pre_mix[1], red_post_mix[0], red_pre_ff[0], red_pre_ff[1], red_post_ff[0]]),
        g_mix_pre=red_pre_mix[2], g_mix_post=red_post_mix[1], g_ff_pre=red_pre_ff[2], g_ff_post=red_post_ff[1],
        b_f=red_f[0, 0:8], w_pool=d_w_pool, pool_scale=red_pool[0], conv_w=red_conv[0:3])
    return dx0, big, small


SMALL_KEYS = ["mod", "g_mix_pre", "g_mix_post", "g_ff_pre", "g_ff_post", "b_f", "w_pool", "pool_scale", "conv_w"]
SMALL_SHAPES = [(DEPTH, 6 * D), (DEPTH, D), (DEPTH, D), (DEPTH, D), (DEPTH, D), (DEPTH, 8), (DEPTH, 4, 64, 64),
                (DEPTH, POOL_W), (DEPTH, 3, CONV_W)]


def kernel(x, c, w_ada, b_ada, g_mix_pre, g_mix_post, g_ff_pre, g_ff_post, w_in, b_f, w_pool, pool_scale, conv_w, w_branch, w_out, w_ff1, w_ff2, loss_target, m_w_ada, m_b_ada, m_g_mix_pre, m_g_mix_post, m_g_ff_pre, m_g_ff_post, m_w_in, m_b_f, m_w_pool, m_pool_scale, m_conv_w, m_w_branch, m_w_out, m_w_ff1, m_w_ff2, v_w_ada, v_b_ada, v_g_mix_pre, v_g_mix_post, v_g_ff_pre, v_g_ff_post, v_w_in, v_b_f, v_w_pool, v_pool_scale, v_conv_w, v_w_branch, v_w_out, v_w_ff1, v_w_ff2):
    ix, iy, ic = lax.axis_index("x"), lax.axis_index("y"), lax.axis_index("c")
    me = 4 * ix + 2 * iy + ic
    route = jnp.stack([ic, 2 * (1 - ix) + iy, 2 * ix + (1 - iy), 2 * (1 - ix) + (1 - iy)]).astype(jnp.int32)
    place = jnp.stack([me, 2 * ix + iy]).astype(jnp.int32)
    wt_in, mt_in, vt_in = (jnp.transpose(a, (0, 2, 1)) for a in (w_in, m_w_in, v_w_in))

    c_all = _all_gather([_pad_rows(c)], "gather_c")[0][:, 0, :]
    c_pad = _pad_rows(c_all, ADA_ROWS)
    b_cols = lax.dynamic_slice_in_dim(b_ada, me * ADA_COLS, ADA_COLS, axis=1)
    b_cols = jnp.broadcast_to(b_cols[:, None, :], (DEPTH, 8, ADA_COLS))
    mod_part = _ada_fwd(c_pad, w_ada, b_cols, "ada_fwd")
    mod_all = _all_gather([mod_part.reshape(DEPTH * ADA_ROWS, ADA_COLS)], "gather_mod")[0]
    mod_all = mod_all.reshape(N_DEV, DEPTH, ADA_ROWS, ADA_COLS)
    mod_mine = lax.dynamic_index_in_dim(mod_all, me, axis=2, keepdims=False)
    mod_mine = jnp.transpose(mod_mine, (1, 0, 2)).reshape(DEPTH, 6, D)

    cw_cols = CONV_W // N_DEV
    cw_send = jnp.pad(conv_w.reshape(DEPTH * 3, cw_cols), ((0, 8 - DEPTH * 3), (0, LANE - cw_cols)))
    send = [[w[l].astype(BF16) for w in (wt_in, w_branch, w_out, w_ff1, w_ff2)] for l in range(DEPTH)]
    first = _all_gather(send[0][:1], "gather_weights_l0_in", sequencer_id=1, after=mod_all)
    rest = _all_gather(send[0][1:] + [cw_send], "gather_weights_l0_rest", sequencer_id=2, after=first[0])
    gathered = [first + rest[:4], _all_gather(send[1], "gather_weights_l1", sequencer_id=3, after=first[0])]
    cw_all = rest[4][:, :DEPTH * 3, :cw_cols].reshape(N_DEV, DEPTH, 3, cw_cols)

    def layer_operands(l, weights):
        p_in, p_br, p_out, p_ff1, p_ff2 = weights[:5]
        w_br_full = p_br.reshape(D, D)
        cw_full = jnp.transpose(cw_all[:, l], (1, 0, 2)).reshape(3, CONV_W)
        wp_bd = jnp.zeros((POOL_W, POOL_W), F32)
        for g in range(4):
            wp_bd = wp_bd.at[64 * g:64 * (g + 1), 64 * g:64 * (g + 1)].set(w_pool[l, g])
        wts = dict(
            w_in_t=_z_rows_from_in(p_in.reshape(IN_COLS, D)), wa=w_br_full[0:A_WIDTH], wb=w_br_full[A_WIDTH:A_WIDTH + POOL_W],
            wc=w_br_full[A_WIDTH + POOL_W:], w_out=p_out.reshape(D, D),
            w_ff1=p_ff1, w_ff2=p_ff2.reshape(D_FF, D),
            conv_w=_pad_rows(cw_full), wp_bd=wp_bd.astype(BF16), pool_scale=_pad_rows(pool_scale[l][None, :]),
            b_f=_pad_rows(jnp.pad(b_f[l], (0, LANE - 8))[None, :]))
        gvec = _pad_rows(jnp.stack([g_mix_pre[l], g_mix_post[l], g_ff_pre[l], g_ff_post[l]]))
        return wts, gvec, _pad_rows(mod_mine[l])

    xs = x[0]
    saved, layers = [], []
    for l in range(DEPTH):
        weights = gathered[l]
        if l > 0:
            xs, weights = lax.optimization_barrier((xs, weights))
        layers.append(layer_operands(l, weights))
        xs, sv = _layer_fwd(l, xs, *layers[l])
        saved.append(sv)
    dx, loss_part = _loss_head(xs, loss_target[0], "loss_head")
    loss = lax.psum(loss_part[0, 0], ("x", "y", "c"))
    small_grads = [None] * DEPTH
    mine, sibs, landed = ({} for _ in range(3))
    seq_id = iter(range(4, 4 + 4 * DEPTH))
    last = [gathered[DEPTH - 1][0]]

    def start(group, grads):
        mine[group] = grads
        sibs[group] = _sibling_exchange(grads, f"rs_sibling_{group}", sequencer_id=next(seq_id), after=last[0])
        last[0] = sibs[group][0]

    def finish(group, later):
        later, (grads, sib) = lax.optimization_barrier((later, (mine[group], sibs[group])))
        sends = [_pair_sums(g, p, route, f"rs_pair_sums_{group}_{k}") for k, (g, p) in enumerate(zip(grads, sib))]
        later, sends = lax.optimization_barrier((later, sends))
        landed[group] = _chip_exchange(sends, f"rs_chips_{group}", sequencer_id=next(seq_id), after=last[0])
        last[0] = landed[group][0]
        return later

    pending = None
    for l in reversed(range(DEPTH)):
        hook = (lambda da: da) if pending is None else functools.partial(finish, pending)
        dx, ffn_grads, ffn_reds = _ffn_bwd(l, dx, saved[l], *layers[l], hook)
        start(f"ffn_l{l}", ffn_grads)
        dx, mix_grads, small_grads[l] = _mixer_bwd(l, dx, saved[l], *layers[l], ffn_reds,
                                                   functools.partial(finish, f"ffn_l{l}"))
        start(f"mix_l{l}", mix_grads)
        pending = f"mix_l{l}"
    grad_x = dx[None]

    big_w = [wt_in, w_branch, w_out, w_ff1, w_ff2]
    big_m = [mt_in, m_w_branch, m_w_out, m_w_ff1, m_w_ff2]
    big_v = [vt_in, v_w_branch, v_w_out, v_w_ff1, v_w_ff2]
    where = [("mix", 0), ("mix", 1), ("mix", 2), ("ffn", 0), ("ffn", 1)]

    def reduce_and_update(k):
        group, at = where[k]
        return _reduce_adamw([mine[f"{group}_l{l}"][at] for l in range(DEPTH)],
                             [sibs[f"{group}_l{l}"][at] for l in range(DEPTH)],
                             [landed[f"{group}_l{l}"][at] for l in range(DEPTH)], place, big_w[k], big_m[k], big_v[k],
                             f"rs_sum_adamw_{k}")

    big_res = {k: list(reduce_and_update(k)) for k in (3, 4)}
    big_res[3][0] = finish(pending, big_res[3][0])

    small = {k: jnp.stack([small_grads[l][k] for l in range(DEPTH)]) for k in SMALL_KEYS}
    small_all = _all_gather([_pack([small[k] for k in SMALL_KEYS], 8, F32)], "gather_small")[0]
    dmod_all = small_all[:, 0:DEPTH * 6, :].reshape(N_DEV, DEPTH, 6 * D)
    summed = _unpack(_sum_slabs(small_all, "sum_small").reshape(-1), SMALL_SHAPES)
    sg = dict(zip(SMALL_KEYS, summed))
    dmod_cols = lax.dynamic_slice_in_dim(dmod_all, me * ADA_COLS, ADA_COLS, axis=2)
    dmod_cols = jnp.pad(jnp.transpose(dmod_cols, (1, 0, 2)), ((0, 0), (0, ADA_ROWS - N_DEV), (0, 0)))
    g_w_ada = _ada_bwd(c_pad, dmod_cols, "ada_bwd")
    g_conv_w = lax.dynamic_slice_in_dim(sg["conv_w"], me * (CONV_W // N_DEV), CONV_W // N_DEV, axis=2)

    ada_out = [g_w_ada] + list(_adamw(w_ada, g_w_ada, m_w_ada, v_w_ada, "adamw_ada"))
    rest_w = [b_ada, g_mix_pre, g_mix_post, g_ff_pre, g_ff_post, b_f, w_pool, pool_scale, conv_w]
    rest_m = [m_b_ada, m_g_mix_pre, m_g_mix_post, m_g_ff_pre, m_g_ff_post, m_b_f, m_w_pool, m_pool_scale, m_conv_w]
    rest_v = [v_b_ada, v_g_mix_pre, v_g_mix_post, v_g_ff_pre, v_g_ff_post, v_b_f, v_w_pool, v_pool_scale, v_conv_w]
    rest_g = [sg["mod"], sg["g_mix_pre"], sg["g_mix_post"], sg["g_ff_pre"], sg["g_ff_post"], sg["b_f"],
              sg["w_pool"], sg["pool_scale"], g_conv_w]
    rest_shapes = [a.shape for a in rest_w]
    upd = _adamw(_pack(rest_w, 8, F32)[None], _pack(rest_g, 8, F32)[None], _pack(rest_m, 8, F32)[None],
                 _pack(rest_v, 8, F32)[None], "adamw_rest")
    rest_out = [rest_g] + [_unpack(arr.reshape(-1), rest_shapes) for arr in upd]
    rest_out = [[ada_out[which]] + rest_out[which] for which in range(4)]

    landed[pending], rest_out = lax.optimization_barrier((landed[pending], rest_out))
    big_res.update({k: reduce_and_update(k) for k in (0, 1, 2)})
    big_out = [[jnp.transpose(big_res[k][which], (0, 2, 1)) if k == 0 else big_res[k][which] for k in range(5)]
               for which in range(4)]

    def ordered(k):
        r, b = rest_out[k], big_out[k]
        return [r[0], r[1], r[2], r[3], r[4], r[5], b[0], r[6], r[7], r[8], r[9], b[1], b[2], b[3], b[4]]

    return (loss, grad_x, *ordered(0), *ordered(1), *ordered(2), *ordered(3))
```

```python
import functools

import jax
import jax.numpy as jnp
from jax import lax
from jax.experimental import pallas as pl
from jax.experimental.pallas import tpu as pltpu
from jax.experimental.pallas import tpu_sc as plsc

F32 = jnp.float32
BF16 = jnp.bfloat16

N_DEV = 8
D = 1024
S = 2048
DEPTH = 2
D_FF = 4 * D
A_WIDTH = 512
HEAD_DIM = 64
N_PAIR = 4
POOL_W = 256
CONV_W = 256
IN_COLS = 5640
ADA_COLS = 6 * D // N_DEV
IN_SHARD = IN_COLS // N_DEV
RMS_EPS = 1e-6
NEG_INF = -1e30
ATT_SCALE = HEAD_DIM ** -0.5

NZ = 5760
Z_PC = 0
Z_G = 1024
Z_Q = 4096
Z_K = 4608
Z_V = 5120
Z_F = 5632

LR, B1, B2, EPS, WD, STEP = 0.001, 0.9, 0.999, 1e-08, 0.01, 10

LANE = 128
VMEM_LIMIT_BYTES = 48 * 1024 * 1024
TS = 512
TQ = 256
TQ_FWD = 512


def _params(sem=None):
    return pltpu.CompilerParams(dimension_semantics=sem, vmem_limit_bytes=VMEM_LIMIT_BYTES)


def _pick(n, target):
    best = None
    for t in range(LANE, min(n, target) + 1, LANE):
        if n % t == 0:
            best = t
    return n if best is None else best


def _matmul(a, b, mode, name, out_dtype=F32, tm=1024, tn=1024, tk=1024, b_col_shards=False, out_col_shards=False,
            extra=(), epilogue=None, out_dtypes=None):
    if b_col_shards:
        shards, b_rows, shard_cols = b.shape
        b_shape = (b_rows, shards * shard_cols)
    else:
        b_shape = b.shape
    if mode == "nn":
        (m, k), (k2, n) = a.shape, b_shape
    elif mode == "nt":
        (m, k), (n, k2) = a.shape, b_shape
    else:
        (k, m), (k2, n) = a.shape, b_shape
    assert k == k2, (a.shape, b.shape, mode)
    tm, tn, tk = _pick(m, tm), _pick(n, tn), _pick(k, tk)
    if b_col_shards and mode == "nn":
        tn = shard_cols
    if b_col_shards and mode == "nt":
        tk = shard_cols
    if out_col_shards:
        tn = n // N_DEV
    nk = k // tk
    if mode == "nn":
        a_spec = pl.BlockSpec((tm, tk), lambda i, j, kk: (i, kk))
        b_spec = (pl.BlockSpec((None, tk, tn), lambda i, j, kk: (j, kk, 0)) if b_col_shards else
                  pl.BlockSpec((tk, tn), lambda i, j, kk: (kk, j)))
        dims = (((1,), (0,)), ((), ()))
    elif mode == "nt":
        a_spec = pl.BlockSpec((tm, tk), lambda i, j, kk: (i, kk))
        b_spec = (pl.BlockSpec((None, tn, tk), lambda i, j, kk: (kk, j, 0)) if b_col_shards else
                  pl.BlockSpec((tn, tk), lambda i, j, kk: (j, kk)))
        dims = (((1,), (1,)), ((), ()))
    else:
        assert not b_col_shards
        a_spec = pl.BlockSpec((tk, tm), lambda i, j, kk: (kk, i))
        b_spec = pl.BlockSpec((tk, tn), lambda i, j, kk: (kk, j))
        dims = (((0,), (0,)), ((), ()))
    if out_col_shards:
        out_shape = jax.ShapeDtypeStruct((N_DEV, m, tn), out_dtype)
        out_spec = pl.BlockSpec((None, tm, tn), lambda i, j, kk: (j, i, 0))
    else:
        out_shape = jax.ShapeDtypeStruct((m, n), out_dtype)
        out_spec = pl.BlockSpec((tm, tn), lambda i, j, kk: (i, j))

    n_extra = len(extra)
    extra_specs = [pl.BlockSpec((tm, tn), lambda i, j, kk, off=off: (i, j + off // tn)) for _, off in extra]
    if epilogue is not None:
        assert not out_col_shards and all(off % tn == 0 for _, off in extra)
        out_shape = [jax.ShapeDtypeStruct((m, n), dt) for dt in out_dtypes]
        out_spec = [pl.BlockSpec((tm, tn), lambda i, j, kk: (i, j)) for _ in out_dtypes]

    def product(a_ref, b_ref):
        return lax.dot_general(a_ref[...].astype(BF16), b_ref[...].astype(BF16), dims, preferred_element_type=F32)

    def write(acc, extra_refs, o_refs):
        if epilogue is None:
            o_refs[0][...] = acc.astype(out_dtype)
        else:
            for o_ref, tile in zip(o_refs, epilogue(acc, *[r[...] for r in extra_refs])):
                o_ref[...] = tile.astype(o_ref.dtype)

    def body_one_pass(a_ref, b_ref, *refs):
        write(product(a_ref, b_ref), refs[:n_extra], refs[n_extra:])

    def body(a_ref, b_ref, *refs):
        acc_ref = refs[-1]
        kk = pl.program_id(2)

        @pl.when(kk == 0)
        def _():
            acc_ref[...] = product(a_ref, b_ref)

        @pl.when(kk > 0)
        def _():
            acc_ref[...] += product(a_ref, b_ref)

        @pl.when(kk == nk - 1)
        def _():
            write(acc_ref[...], refs[:n_extra], refs[n_extra:-1])

    return pl.pallas_call(
        body_one_pass if nk == 1 else body, name=name,
        out_shape=out_shape,
        grid=(m // tm, n // tn, nk),
        in_specs=[a_spec, b_spec] + extra_specs,
        out_specs=out_spec,
        scratch_shapes=[] if nk == 1 else [pltpu.VMEM((tm, tn), F32)],
        compiler_params=_params(("parallel", "parallel", "arbitrary")),
    )(a, b, *[x for x, _ in extra])


def _row_spec(width=D, col=0):
    return pl.BlockSpec((TS, width), lambda i: (i, col))


def _vec_spec(rows=8, width=D):
    return pl.BlockSpec((rows, width), lambda i: (0, 0))


def _rms(x):
    return lax.rsqrt(jnp.mean(x * x, axis=-1, keepdims=True) + RMS_EPS)


def _prenorm_fwd(x, gvec, mod, g_row, shift_row, scale_row, name):
    def body(x_ref, g_ref, mod_ref, h_ref):
        xv = x_ref[...]
        y = xv * _rms(xv) * g_ref[g_row:g_row + 1, :]
        h = y * (1.0 + mod_ref[scale_row:scale_row + 1, :]) + mod_ref[shift_row:shift_row + 1, :]
        h_ref[...] = h.astype(BF16)

    return pl.pallas_call(
        body, name=name, out_shape=jax.ShapeDtypeStruct((S, D), BF16), grid=(S // TS,),
        in_specs=[_row_spec(), _vec_spec(), _vec_spec()], out_specs=_row_spec(),
        compiler_params=_params(("parallel",)),
    )(x, gvec, mod)


def _prenorm_bwd(x, gvec, mod, dh, dres, g_row, scale_row, name):
    def body(x_ref, g_ref, mod_ref, dh_ref, dres_ref, dx_ref, red_ref):
        i = pl.program_id(0)

        @pl.when(i == 0)
        def _():
            red_ref[...] = jnp.zeros_like(red_ref)

        xv = x_ref[...]
        g = g_ref[g_row:g_row + 1, :]
        r = _rms(xv)
        n = xv * r
        yg = n * g
        dhv = dh_ref[...]
        dyg = dhv * (1.0 + mod_ref[scale_row:scale_row + 1, :])
        dn = dyg * g
        dx = r * (dn - n * jnp.mean(dn * n, axis=-1, keepdims=True))
        dx_ref[...] = dres_ref[...] + dx
        red_ref[0:1, :] += jnp.sum(dhv, axis=0, keepdims=True)
        red_ref[1:2, :] += jnp.sum(dhv * yg, axis=0, keepdims=True)
        red_ref[2:3, :] += jnp.sum(dyg * n, axis=0, keepdims=True)

    return pl.pallas_call(
        body, name=name,
        out_shape=(jax.ShapeDtypeStruct((S, D), F32), jax.ShapeDtypeStruct((8, D), F32)),
        grid=(S // TS,),
        in_specs=[_row_spec(), _vec_spec(), _vec_spec(), _row_spec(), _row_spec()],
        out_specs=(_row_spec(), _vec_spec()),
        compiler_params=_params(("arbitrary",)),
    )(x, gvec, mod, dh, dres)


def _postnorm_fwd(x, y, gvec, mod, g_row, gate_row, name):
    def body(x_ref, y_ref, g_ref, mod_ref, o_ref):
        yv = y_ref[...]
        yn = yv * _rms(yv) * g_ref[g_row:g_row + 1, :]
        o_ref[...] = x_ref[...] + mod_ref[gate_row:gate_row + 1, :] * yn

    return pl.pallas_call(
        body, name=name, out_shape=jax.ShapeDtypeStruct((S, D), F32), grid=(S // TS,),
        in_specs=[_row_spec(), _row_spec(), _vec_spec(), _vec_spec()], out_specs=_row_spec(),
        compiler_params=_params(("parallel",)),
    )(x, y, gvec, mod)


def _postnorm_bwd(y, gvec, mod, dxo, g_row, gate_row, name):
    def body(y_ref, g_ref, mod_ref, dxo_ref, dy_ref, red_ref):
        i = pl.program_id(0)

        @pl.when(i == 0)
        def _():
            red_ref[...] = jnp.zeros_like(red_ref)

        yv = y_ref[...]
        g = g_ref[g_row:g_row + 1, :]
        r = _rms(yv)
        n = yv * r
        dxo = dxo_ref[...]
        dyn = dxo * mod_ref[gate_row:gate_row + 1, :]
        dn = dyn * g
        dy = r * (dn - n * jnp.mean(dn * n, axis=-1, keepdims=True))
        dy_ref[...] = dy.astype(BF16)
        red_ref[0:1, :] += jnp.sum(dxo * (n * g), axis=0, keepdims=True)
        red_ref[1:2, :] += jnp.sum(dyn * n, axis=0, keepdims=True)

    return pl.pallas_call(
        body, name=name,
        out_shape=(jax.ShapeDtypeStruct((S, D), BF16), jax.ShapeDtypeStruct((8, D), F32)),
        grid=(S // TS,),
        in_specs=[_row_spec(), _vec_spec(), _vec_spec(), _row_spec()],
        out_specs=(_row_spec(), _vec_spec()),
        compiler_params=_params(("arbitrary",)),
    )(y, gvec, mod, dxo)


def _loss_head(xf, target, name):
    def body(x_ref, t_ref, dx_ref, loss_ref):
        i = pl.program_id(0)

        @pl.when(i == 0)
        def _():
            loss_ref[...] = jnp.zeros_like(loss_ref)

        e = x_ref[...] - t_ref[...]
        dx_ref[...] = e / float(D)
        per_tok = jnp.mean(e * e, axis=-1, keepdims=True)
        loss_ref[0:1, 0:1] += 0.5 * jnp.sum(per_tok, axis=0, keepdims=True)

    return pl.pallas_call(
        body, name=name,
        out_shape=(jax.ShapeDtypeStruct((S, D), F32), jax.ShapeDtypeStruct((8, LANE), F32)),
        grid=(S // TS,),
        in_specs=[_row_spec(), _row_spec()],
        out_specs=(_row_spec(), pl.BlockSpec((8, LANE), lambda i: (0, 0))),
        compiler_params=_params(("arbitrary",)),
    )(xf, target)


def _relu2_epilogue(a):
    t = jnp.maximum(a, 0.0)
    return a, t * t


def _relu2_bwd_epilogue(dr, a):
    return (dr * (2.0 * jnp.maximum(a, 0.0)),)


def _merge_epilogue(pc, g0, g1, g2, pa, pb):
    return pc, jax.nn.sigmoid(g0) * pa + jax.nn.sigmoid(g1) * pb + jax.nn.sigmoid(g2) * pc


def _merge_bwd_epilogue(dm, g0, g1, g2, pa, pb, pc):
    sg = [jax.nn.sigmoid(g) for g in (g0, g1, g2)]
    return tuple(dm * s for s in sg) + tuple(dm * p * (s * (1.0 - s)) for p, s in zip((pa, pb, pc), sg))


def _shift_down(x, k, row):
    return jnp.where(row >= k, pltpu.roll(x, k, axis=0), 0.0)


def _shift_up(x, k, row):
    n = x.shape[0]
    return jnp.where(row < n - k, pltpu.roll(x, n - k, axis=0), 0.0)


def _cumsum_rows(x, row, reverse=False):
    shift = _shift_up if reverse else _shift_down
    k = 1
    while k < x.shape[0]:
        x = x + shift(x, k, row)
        k *= 2
    return x


def _full_spec(shape, idx=(0, 0)):
    return pl.BlockSpec(shape, lambda i: idx)


def _pool_window_select(lane, a2, a4, a8, a16):
    return jnp.where(lane < 64, a2, jnp.where(lane < 128, a4, jnp.where(lane < 192, a8, a16)))


def _pool_p(u, row, lane):
    t2 = u + _shift_down(u, 1, row)
    t4 = t2 + _shift_down(t2, 2, row)
    t8 = t4 + _shift_down(t4, 4, row)
    t16 = t8 + _shift_down(t8, 8, row)
    tw = _pool_window_select(lane, t2, t4, t8, t16)
    cnt = jnp.minimum((row + 1).astype(F32), _pool_window_select(lane, 2.0, 4.0, 8.0, 16.0))
    return tw / cnt - u, cnt


def _pool_fwd(z, wp_bd, pscale, name):
    def body(u_ref, w_ref, s_ref, o_ref):
        row = lax.broadcasted_iota(jnp.int32, (S, POOL_W), 0)
        lane = lax.broadcasted_iota(jnp.int32, (S, POOL_W), 1)
        p, _ = _pool_p(u_ref[...], row, lane)
        y = jnp.dot(p.astype(BF16), w_ref[...], preferred_element_type=F32)
        o_ref[...] = y * s_ref[0:1, :]

    return pl.pallas_call(
        body, name=name, out_shape=jax.ShapeDtypeStruct((S, POOL_W), F32), grid=(1,),
        in_specs=[_full_spec((S, POOL_W), (0, Z_PC // POOL_W)), _full_spec((POOL_W, POOL_W)), _full_spec((8, POOL_W))],
        out_specs=_full_spec((S, POOL_W)),
        compiler_params=_params(("arbitrary",)),
    )(z, wp_bd, pscale)


def _pool_bwd(z, wp_bd, pscale, dbr, name):
    def body(u_ref, w_ref, s_ref, dbr_ref, du_ref, dw_ref, red_ref):
        row = lax.broadcasted_iota(jnp.int32, (S, POOL_W), 0)
        lane = lax.broadcasted_iota(jnp.int32, (S, POOL_W), 1)
        p, cnt = _pool_p(u_ref[...], row, lane)
        pb = p.astype(BF16)
        y = jnp.dot(pb, w_ref[...], preferred_element_type=F32)
        dbr = dbr_ref[...]
        red_ref[...] = jnp.zeros_like(red_ref)
        red_ref[0:1, :] = jnp.sum(dbr * y, axis=0, keepdims=True)
        dy = (dbr * s_ref[0:1, :]).astype(BF16)
        dw_ref[...] = lax.dot_general(pb, dy, (((0,), (0,)), ((), ())), preferred_element_type=F32)
        dp = lax.dot_general(dy, w_ref[...], (((1,), (1,)), ((), ())), preferred_element_type=F32)
        g = dp / cnt
        a2 = g + _shift_up(g, 1, row)
        a4 = a2 + _shift_up(a2, 2, row)
        a8 = a4 + _shift_up(a4, 4, row)
        a16 = a8 + _shift_up(a8, 8, row)
        du_ref[...] = (_pool_window_select(lane, a2, a4, a8, a16) - dp).astype(BF16)

    return pl.pallas_call(
        body, name=name,
        out_shape=(jax.ShapeDtypeStruct((S, POOL_W), BF16), jax.ShapeDtypeStruct((POOL_W, POOL_W), F32),
                   jax.ShapeDtypeStruct((8, POOL_W), F32)),
        grid=(1,),
        in_specs=[_full_spec((S, POOL_W), (0, Z_PC // POOL_W)), _full_spec((POOL_W, POOL_W)), _full_spec((8, POOL_W)),
                  _full_spec((S, POOL_W))],
        out_specs=(_full_spec((S, POOL_W)), _full_spec((POOL_W, POOL_W)), _full_spec((8, POOL_W))),
        compiler_params=_params(("arbitrary",)),
    )(z, wp_bd, pscale, dbr)


def _conv_specs():
    base = Z_PC // CONV_W
    return [_full_spec((S, CONV_W), (0, base + 1)), _full_spec((S, CONV_W), (0, base + 2)),
            _full_spec((S, CONV_W), (0, base + 3)), _full_spec((8, CONV_W))]


def _conv_fwd(z, cw, name):
    def body(h_ref, b_ref, c_ref, w_ref, o_ref):
        row = lax.broadcasted_iota(jnp.int32, (S, CONV_W), 0)
        u = c_ref[...] * h_ref[...]
        y = (w_ref[0:1, :] * _shift_down(u, 2, row) + w_ref[1:2, :] * _shift_down(u, 1, row) + w_ref[2:3, :] * u)
        o_ref[...] = b_ref[...] * y

    return pl.pallas_call(
        body, name=name, out_shape=jax.ShapeDtypeStruct((S, CONV_W), F32), grid=(1,),
        in_specs=_conv_specs(), out_specs=_full_spec((S, CONV_W)),
        compiler_params=_params(("arbitrary",)),
    )(z, z, z, cw)


def _conv_bwd(z, cw, dbr, name):
    def body(h_ref, b_ref, c_ref, w_ref, dbr_ref, d_ref, red_ref):
        row = lax.broadcasted_iota(jnp.int32, (S, CONV_W), 0)
        h, cg = h_ref[...], c_ref[...]
        u = cg * h
        u1 = _shift_down(u, 1, row)
        u2 = _shift_down(u, 2, row)
        y = w_ref[0:1, :] * u2 + w_ref[1:2, :] * u1 + w_ref[2:3, :] * u
        dbr = dbr_ref[...]
        dy = dbr * b_ref[...]
        du = w_ref[2:3, :] * dy + w_ref[1:2, :] * _shift_up(dy, 1, row) + w_ref[0:1, :] * _shift_up(dy, 2, row)
        d_ref[:, 0:CONV_W] = (du * cg).astype(BF16)
        d_ref[:, CONV_W:2 * CONV_W] = (dbr * y).astype(BF16)
        d_ref[:, 2 * CONV_W:3 * CONV_W] = (du * h).astype(BF16)
        red_ref[...] = jnp.zeros_like(red_ref)
        red_ref[0:1, :] = jnp.sum(dy * u2, axis=0, keepdims=True)
        red_ref[1:2, :] = jnp.sum(dy * u1, axis=0, keepdims=True)
        red_ref[2:3, :] = jnp.sum(dy * u, axis=0, keepdims=True)

    return pl.pallas_call(
        body, name=name,
        out_shape=(jax.ShapeDtypeStruct((S, 3 * CONV_W), BF16), jax.ShapeDtypeStruct((8, CONV_W), F32)),
        grid=(1,),
        in_specs=_conv_specs() + [_full_spec((S, CONV_W))],
        out_specs=(_full_spec((S, 3 * CONV_W)), _full_spec((8, CONV_W))),
        compiler_params=_params(("arbitrary",)),
    )(z, z, z, cw, dbr)


_NT = (((1,), (1,)), ((), ()))
_TN = (((0,), (0,)), ((), ()))
N_HEAD = 2 * N_PAIR


def _split3(x):
    hi = x.astype(BF16).astype(F32)
    mid = (x - hi).astype(BF16).astype(F32)
    lo = (x - hi - mid).astype(BF16).astype(F32)
    return hi, mid, lo


def _spare(lane, e, k):
    return lane == 64 * (1 - e) + k


def _spare3(lane, e, k):
    base = 64 * (1 - e) + k
    return (lane >= base) & (lane < base + 3)


def _put3(lane, e, k, pieces, rest):
    out = rest
    for n, piece in enumerate(pieces):
        out = jnp.where(_spare(lane, e, k + n), piece, out)
    return out


def _attn_prep(z, bf, name):
    def body(q_ref, k_ref, v_ref, f_ref, b_ref, qa_ref, ka_ref, va_ref, kat_ref):
        p = pl.program_id(0)
        row = lax.broadcasted_iota(jnp.int32, (S, LANE), 0)
        lane = lax.broadcasted_iota(jnp.int32, (S, LANE), 1)
        xv = f_ref[...] + b_ref[0:1, :]
        ls = jnp.minimum(xv, 0.0) - jnp.log(1.0 + jnp.exp(-jnp.abs(xv)))
        cum = _cumsum_rows(jnp.where(lane < N_HEAD, ls, 0.0), row)
        q, k, v = q_ref[...], k_ref[...], v_ref[...]
        for e in range(2):
            head = (lane >= 64) if e else (lane < 64)
            f = jnp.sum(jnp.where(lane == 2 * p + e, cum, 0.0), axis=1, keepdims=True)
            pieces = _split3(f)
            qa = jnp.where(head, q * ATT_SCALE, _put3(lane, e, 0, pieces, jnp.where(_spare3(lane, e, 3), 1.0, 0.0)))
            ones = jnp.where(_spare3(lane, e, 0) | _spare3(lane, e, 6), 1.0, 0.0)
            ka = jnp.where(head, k, _put3(lane, e, 3, [-x for x in pieces], ones))
            va = jnp.where(head, v, jnp.where(_spare3(lane, e, 0), 1.0, 0.0))
            qa_ref[e] = qa.astype(BF16)
            ka_ref[e] = ka.astype(BF16)
            va_ref[e] = va.astype(BF16)
            kat_ref[e] = ka.T.astype(BF16)

    qb, kb, vb = Z_Q // LANE, Z_K // LANE, Z_V // LANE
    heads = jax.ShapeDtypeStruct((N_HEAD, S, LANE), BF16)
    pair = pl.BlockSpec((2, S, LANE), lambda p: (p, 0, 0))
    return pl.pallas_call(
        body, name=name,
        out_shape=(heads, heads, heads, jax.ShapeDtypeStruct((N_HEAD, LANE, S), BF16)),
        grid=(N_PAIR,),
        in_specs=[pl.BlockSpec((S, LANE), lambda p: (0, qb + p)), pl.BlockSpec((S, LANE), lambda p: (0, kb + p)),
                  pl.BlockSpec((S, LANE), lambda p: (0, vb + p)), pl.BlockSpec((S, LANE), lambda p: (0, Z_F // LANE)),
                  pl.BlockSpec((8, LANE), lambda p: (0, 0))],
        out_specs=(pair, pair, pair, pl.BlockSpec((2, LANE, S), lambda p: (p, 0, 0))),
        compiler_params=_params(("parallel",)),
    )(z, z, z, z, bf)


def _attn_bwd_prep(qa, o, lse, do, name):
    def body(qa_ref, o_ref, lse_ref, do_ref, qa2_ref, doa_ref):
        lane = lax.broadcasted_iota(jnp.int32, (S, LANE), 1)
        dov, ov, lsev = do_ref[...], o_ref[...], lse_ref[...]
        for e in range(2):
            head = (lane >= 64) if e else (lane < 64)
            dsum = jnp.sum(jnp.where(head, dov * ov, 0.0), axis=1, keepdims=True)
            doa_ref[e] = jnp.where(head, dov, _put3(lane, e, 0, [-x for x in _split3(dsum)], 0.0)).astype(BF16)
            lse_col = lsev[:, 64 * e:64 * e + 1]
            qa2_ref[e] = _put3(lane, e, 6, [-x for x in _split3(lse_col)], qa_ref[e].astype(F32)).astype(BF16)

    heads = jax.ShapeDtypeStruct((N_HEAD, S, LANE), BF16)
    pair = pl.BlockSpec((2, S, LANE), lambda p: (p, 0, 0))
    cols = pl.BlockSpec((S, LANE), lambda p: (0, p))
    return pl.pallas_call(
        body, name=name, out_shape=(heads, heads), grid=(N_PAIR,),
        in_specs=[pair, cols, cols, cols], out_specs=(pair, pair),
        compiler_params=_params(("parallel",)),
    )(qa, o, lse, do)


def _attn_bwd_post(z, bf, dqt, dka, dva, name):
    def body(f_ref, b_ref, dqt_ref, dk_ref, dv_ref, dq_out, dk_out, dv_out, dfl_ref, red_ref, dcum_ref):
        p = pl.program_id(0)

        @pl.when(p == 0)
        def _():
            dcum_ref[...] = jnp.zeros_like(dcum_ref)

        row = lax.broadcasted_iota(jnp.int32, (S, LANE), 0)
        lane = lax.broadcasted_iota(jnp.int32, (S, LANE), 1)
        dqa = [dqt_ref[e].T for e in range(2)]
        dq_out[...] = (jnp.where(lane < 64, dqa[0], dqa[1]) * ATT_SCALE).astype(BF16)
        dk_out[...] = jnp.where(lane < 64, dk_ref[0], dk_ref[1]).astype(BF16)
        dv_out[...] = jnp.where(lane < 64, dv_ref[0], dv_ref[1]).astype(BF16)
        for e in range(2):
            d_query = jnp.sum(jnp.where(_spare(lane, e, 0), dqa[e], 0.0), axis=1, keepdims=True)
            d_key = jnp.sum(jnp.where(_spare(lane, e, 3), dk_ref[e], 0.0), axis=1, keepdims=True)
            dcum_ref[...] += jnp.where(lane == 2 * p + e, d_query - d_key, 0.0)

        @pl.when(p == N_PAIR - 1)
        def _():
            dls = _cumsum_rows(dcum_ref[...], row, reverse=True)
            xv = f_ref[...] + b_ref[0:1, :]
            dx = jnp.where(lane < N_HEAD, dls * jax.nn.sigmoid(-xv), 0.0)
            dfl_ref[...] = dx.astype(BF16)
            red_ref[...] = jnp.zeros_like(red_ref)
            red_ref[0:1, :] = jnp.sum(dx, axis=0, keepdims=True)

    wide = jax.ShapeDtypeStruct((S, N_PAIR * LANE), BF16)
    cols = pl.BlockSpec((S, LANE), lambda p: (0, p))
    pair = pl.BlockSpec((2, S, LANE), lambda p: (p, 0, 0))
    return pl.pallas_call(
        body, name=name,
        out_shape=(wide, wide, wide, jax.ShapeDtypeStruct((S, LANE), BF16), jax.ShapeDtypeStruct((8, LANE), F32)),
        grid=(N_PAIR,),
        in_specs=[pl.BlockSpec((S, LANE), lambda p: (0, Z_F // LANE)), pl.BlockSpec((8, LANE), lambda p: (0, 0)),
                  pl.BlockSpec((2, LANE, S), lambda p: (p, 0, 0)), pair, pair],
        out_specs=(cols, cols, cols, pl.BlockSpec((S, LANE), lambda p: (0, 0)), pl.BlockSpec((8, LANE), lambda p: (0, 0))),
        scratch_shapes=[pltpu.VMEM((S, LANE), F32)],
        compiler_params=_params(("arbitrary",)),
    )(z, bf, dqt, dka, dva)


def _attn_fwd(qa, ka, va, name):
    tq, tk = TQ_FWD, TQ
    ratio = tq // tk

    def body(qa_ref, ka_ref, va_ref, o_ref, lse_ref):
        i = pl.program_id(1)
        lane = lax.broadcasted_iota(jnp.int32, (tq, LANE), 1)
        row = lax.broadcasted_iota(jnp.int32, (tq, tk), 0)
        col = lax.broadcasted_iota(jnp.int32, (tq, tk), 1)
        qs = [qa_ref[0], qa_ref[1]]

        def block(j, carry, masked):
            off = pl.multiple_of(j * tk, tk)
            out = []
            for e in range(2):
                m, acc = carry[e]
                s = lax.dot_general(qs[e], ka_ref[e, pl.ds(off, tk), :], _NT, preferred_element_type=F32)
                if masked:
                    s = jnp.where(col + (j - ratio * i) * tk > row, NEG_INF, s)
                mn = jnp.maximum(m, jnp.max(s, axis=1, keepdims=True))
                p = jnp.exp(s - mn).astype(BF16)
                acc = jnp.exp(m - mn) * acc + jnp.dot(p, va_ref[e, pl.ds(off, tk), :], preferred_element_type=F32)
                out.append((mn, acc))
            return tuple(out)

        init = (jnp.full((tq, 1), NEG_INF, F32), jnp.zeros((tq, LANE), F32))
        carry = lax.fori_loop(0, ratio * i, lambda j, c: block(j, c, False), (init, init))
        for d in range(ratio):
            carry = block(ratio * i + d, carry, True)
        res = []
        for e in range(2):
            m, acc = carry[e]
            l = jnp.sum(jnp.where(_spare(lane, e, 0), acc, 0.0), axis=1, keepdims=True)
            res.append((acc / l, m + jnp.log(l)))
        o_ref[...] = jnp.where(lane < 64, res[0][0], res[1][0])
        lse_ref[...] = jnp.where(lane < 64, res[0][1], res[1][1])

    out = jax.ShapeDtypeStruct((S, N_PAIR * LANE), F32)
    return pl.pallas_call(
        body, name=name, out_shape=(out, out), grid=(N_PAIR, S // tq),
        in_specs=[pl.BlockSpec((2, tq, LANE), lambda p, i: (p, i, 0)), pl.BlockSpec((2, S, LANE), lambda p, i: (p, 0, 0)),
                  pl.BlockSpec((2, S, LANE), lambda p, i: (p, 0, 0))],
        out_specs=(pl.BlockSpec((tq, LANE), lambda p, i: (i, p)), pl.BlockSpec((tq, LANE), lambda p, i: (i, p))),
        compiler_params=_params(("parallel", "parallel")),
    )(qa, ka, va)


def _attn_bwd(qa2, ka, va, kat, doa, name):
    nq = S // TQ

    def body(qa_ref, ka_ref, va_ref, kat_ref, doa_ref, dqt_ref, dk_ref, dv_ref):
        j = pl.program_id(1)

        @pl.when(j == 0)
        def _():
            dqt_ref[...] = jnp.zeros_like(dqt_ref)

        key = lax.broadcasted_iota(jnp.int32, (TQ, TQ), 0)
        qry = lax.broadcasted_iota(jnp.int32, (TQ, TQ), 1)
        kav, vav, katv = [ka_ref[0], ka_ref[1]], [va_ref[0], va_ref[1]], [kat_ref[0], kat_ref[1]]

        def block(i, carry, masked):
            off = pl.multiple_of(i * TQ, TQ)
            out = []
            for e in range(2):
                dk_acc, dv_acc = carry[e]
                qav = qa_ref[e, pl.ds(off, TQ), :]
                doav = doa_ref[e, pl.ds(off, TQ), :]
                s_t = lax.dot_general(kav[e], qav, _NT, preferred_element_type=F32)
                if masked:
                    s_t = jnp.where(key > qry, NEG_INF, s_t)
                p_t = jnp.exp(s_t)
                ds_t = p_t * lax.dot_general(vav[e], doav, _NT, preferred_element_type=F32)
                dsb = ds_t.astype(BF16)
                dv_acc = dv_acc + jnp.dot(p_t.astype(BF16), doav, preferred_element_type=F32)
                dk_acc = dk_acc + jnp.dot(dsb, qav, preferred_element_type=F32)
                dqt_ref[e, :, pl.ds(off, TQ)] += jnp.dot(katv[e], dsb, preferred_element_type=F32)
                out.append((dk_acc, dv_acc))
            return tuple(out)

        zero = (jnp.zeros((TQ, LANE), F32), jnp.zeros((TQ, LANE), F32))
        carry = block(j, (zero, zero), True)
        carry = lax.fori_loop(j + 1, nq, lambda i, c: block(i, c, False), carry)
        for e in range(2):
            dk_ref[e], dv_ref[e] = carry[e]

    full = pl.BlockSpec((2, S, LANE), lambda p, j: (p, 0, 0))
    blk = pl.BlockSpec((2, TQ, LANE), lambda p, j: (p, j, 0))
    acc = jax.ShapeDtypeStruct((N_HEAD, S, LANE), F32)
    return pl.pallas_call(
        body, name=name,
        out_shape=(jax.ShapeDtypeStruct((N_HEAD, LANE, S), F32), acc, acc),
        grid=(N_PAIR, nq),
        in_specs=[full, blk, blk, pl.BlockSpec((2, LANE, TQ), lambda p, j: (p, 0, j)), full],
        out_specs=(pl.BlockSpec((2, LANE, S), lambda p, j: (p, 0, 0)), blk, blk),
        compiler_params=_params(("arbitrary", "arbitrary")),
    )(qa2, ka, va, kat, doa)


ADA_ROWS = 16


def _ada_fwd(c_pad, w_ada, b_cols, name):
    def body(c_ref, w_ref, b_ref, o_ref):
        cv = c_ref[...]
        sc = (cv * jax.nn.sigmoid(cv)).astype(BF16)
        o_ref[0] = jnp.dot(sc, w_ref[0].astype(BF16), preferred_element_type=F32) + b_ref[0, 0:1, :]

    return pl.pallas_call(
        body, name=name, out_shape=jax.ShapeDtypeStruct((DEPTH, ADA_ROWS, ADA_COLS), F32), grid=(DEPTH,),
        in_specs=[pl.BlockSpec((ADA_ROWS, D), lambda l: (0, 0)), pl.BlockSpec((1, D, ADA_COLS), lambda l: (l, 0, 0)),
                  pl.BlockSpec((1, 8, ADA_COLS), lambda l: (l, 0, 0))],
        out_specs=pl.BlockSpec((1, ADA_ROWS, ADA_COLS), lambda l: (l, 0, 0)),
        compiler_params=_params(("parallel",)),
    )(c_pad, w_ada, b_cols)


def _ada_bwd(c_pad, dmod_cols, name):
    def body(c_ref, d_ref, o_ref):
        cv = c_ref[...]
        sc = (cv * jax.nn.sigmoid(cv)).astype(BF16)
        o_ref[0] = lax.dot_general(sc, d_ref[0].astype(BF16), _TN, preferred_element_type=F32)

    return pl.pallas_call(
        body, name=name, out_shape=jax.ShapeDtypeStruct((DEPTH, D, ADA_COLS), F32), grid=(DEPTH,),
        in_specs=[pl.BlockSpec((ADA_ROWS, D), lambda l: (0, 0)), pl.BlockSpec((1, ADA_ROWS, ADA_COLS), lambda l: (l, 0, 0))],
        out_specs=pl.BlockSpec((1, D, ADA_COLS), lambda l: (l, 0, 0)),
        compiler_params=_params(("parallel",)),
    )(c_pad, dmod_cols)


def _adamw_math(w, g, m, v):
    m = B1 * m + (1.0 - B1) * g
    v = B2 * v + (1.0 - B2) * (g * g)
    m_hat = m / (1.0 - B1 ** STEP)
    v_hat = v / (1.0 - B2 ** STEP)
    delta = -LR * (m_hat / (jnp.sqrt(v_hat) + EPS) + WD * w)
    return delta, m, v


def _row_tile(rows, target=256):
    best = 8
    for t in range(8, min(rows, target) + 1, 8):
        if rows % t == 0:
            best = t
    return best


def _adamw(w, g, m, v, name):
    layers, rows, cols = w.shape
    tr = _row_tile(rows)
    spec = pl.BlockSpec((1, tr, cols), lambda l, i: (l, i, 0))

    def body(w_ref, g_ref, m_ref, v_ref, d_ref, nm_ref, nv_ref):
        d_ref[...], nm_ref[...], nv_ref[...] = _adamw_math(w_ref[...], g_ref[...], m_ref[...], v_ref[...])

    out = jax.ShapeDtypeStruct(w.shape, F32)
    return pl.pallas_call(
        body, name=name, out_shape=(out, out, out), grid=(layers, rows // tr),
        in_specs=[spec] * 4, out_specs=(spec,) * 3, compiler_params=_params(("parallel", "parallel")),
    )(w, g, m, v)


def _sum_slabs(x, name):
    n, rows, _ = x.shape
    tr = _row_tile(rows)

    def body(x_ref, o_ref):
        acc = x_ref[0]
        for k in range(1, n):
            acc = acc + x_ref[k]
        o_ref[...] = acc

    return pl.pallas_call(
        body, name=name, out_shape=jax.ShapeDtypeStruct((rows, D), F32), grid=(rows // tr,),
        in_specs=[pl.BlockSpec((n, tr, D), lambda i: (0, i, 0))], out_specs=pl.BlockSpec((tr, D), lambda i: (i, 0)),
        compiler_params=_params(("parallel",)),
    )(x)


_ANY = pl.BlockSpec(memory_space=pl.ANY)
MESH = pl.DeviceIdType.MESH


def _on_sequencer(body, out_shape, sems, operands, after, sequencer_id, name):
    n = len(operands)

    def ordered_body(*refs):
        body(*refs[:n], *refs[n + 1:])

    extra = [] if after is None else [after]
    return pl.kernel(
        body if after is None else ordered_body, out_type=out_shape,
        mesh=plsc.ScalarSubcoreMesh(axis_name="sequencer", num_cores=1), scratch_types=sems,
        compiler_params=pltpu.CompilerParams(collective_id=sequencer_id), name=name)(*operands, *extra)


def _all_gather(xs, name, sequencer_id=None, after=None):
    n = len(xs)

    def body(*refs):
        x_refs, out_refs = refs[:n], refs[n:2 * n]
        send_sems, recv_sems, local_sems = refs[2 * n:]
        x_, y_, c_ = lax.axis_index("x"), lax.axis_index("y"), lax.axis_index("c")
        me, sibling = (x_, y_, c_), (x_, y_, 1 - c_)
        chips = [(1 - x_, y_), (x_, 1 - y_), (1 - x_, 1 - y_)]
        if sequencer_id is not None:
            barrier = pltpu.get_barrier_semaphore()
            peers = [sibling] + [(*chip, pc) for chip in chips for pc in (c_, 1 - c_)]
            for peer in peers:
                pl.semaphore_signal(barrier, inc=1, device_id=peer, device_id_type=MESH)
            pl.semaphore_wait(barrier, len(peers))

        def slot(a, px, py, pc):
            return out_refs[a].at[4 * px + 2 * py + pc]

        def copy(a, k, block, to, src=None):
            return pltpu.make_async_remote_copy(
                src_ref=slot(a, *block) if src is None else src, dst_ref=slot(a, *block),
                send_sem=send_sems.at[7 * a + k], recv_sem=recv_sems.at[7 * a + k], device_id=to, device_id_type=MESH)

        mine = [pltpu.make_async_copy(x_refs[a], slot(a, *me), local_sems.at[a]) for a in range(n)]
        for cp in mine:
            cp.start()
        first = []
        for a in range(n):
            first.append(copy(a, 0, me, sibling, src=x_refs[a]))
            first += [copy(a, 1 + j, me, (*chip, c_), src=x_refs[a]) for j, chip in enumerate(chips)]
        for cp in first:
            cp.start()
        passed = []
        for j, chip in enumerate(chips):
            for a in range(n):
                copy(a, 1 + j, (*chip, c_), me).wait_recv()
                passed.append(copy(a, 4 + j, (*chip, c_), sibling))
                passed[-1].start()
        for a in range(n):
            copy(a, 0, sibling, me).wait_recv()
        for j, chip in enumerate(chips):
            for a in range(n):
                copy(a, 4 + j, (*chip, 1 - c_), me).wait_recv()
        for cp in first + passed:
            cp.wait_send()
        for cp in mine:
            cp.wait()

    out_shape = [jax.ShapeDtypeStruct((N_DEV,) + x.shape, x.dtype) for x in xs]
    sems = [pltpu.SemaphoreType.DMA((7 * n,)), pltpu.SemaphoreType.DMA((7 * n,)), pltpu.SemaphoreType.DMA((n,))]
    if sequencer_id is not None:
        return _on_sequencer(body, out_shape, sems, xs, after, sequencer_id, name)
    return pl.pallas_call(
        body, name=name, out_shape=out_shape, in_specs=[_ANY] * n, out_specs=[_ANY] * n, scratch_shapes=sems)(*xs)


def _sibling_exchange(gs, name, sequencer_id=None, after=None):
    n = len(gs)

    def body(*refs):
        g_refs, p_refs = refs[:n], refs[n:2 * n]
        send_sems, recv_sems = refs[2 * n:]
        x_, y_, c_ = lax.axis_index("x"), lax.axis_index("y"), lax.axis_index("c")
        if sequencer_id is not None:
            barrier = pltpu.get_barrier_semaphore()
            pl.semaphore_signal(barrier, inc=1, device_id=(x_, y_, 1 - c_), device_id_type=MESH)
            pl.semaphore_wait(barrier, 1)
        copies = [pltpu.make_async_remote_copy(
            src_ref=g_refs[a].at[2 * k + (1 - c_)], dst_ref=p_refs[a].at[k], send_sem=send_sems.at[4 * a + k],
            recv_sem=recv_sems.at[4 * a + k], device_id=(x_, y_, 1 - c_), device_id_type=MESH)
            for a in range(n) for k in range(4)]
        for cp in copies:
            cp.start()
        for cp in copies:
            cp.wait()

    out_shape = [jax.ShapeDtypeStruct((4,) + g.shape[1:], g.dtype) for g in gs]
    sems = [pltpu.SemaphoreType.DMA((4 * n,)), pltpu.SemaphoreType.DMA((4 * n,))]
    if sequencer_id is not None:
        return _on_sequencer(body, out_shape, sems, gs, after, sequencer_id, name)
    return pl.pallas_call(
        body, name=name, out_shape=out_shape, in_specs=[_ANY] * n, out_specs=[_ANY] * n, scratch_shapes=sems)(*gs)


def _slab_tiles(rows, cols):
    if rows % 8 == 0:
        return _row_tile(rows), cols
    return rows, 2 * LANE


def _pair_sums(g, p, route, name):
    _, rows, cols = g.shape
    tr, tc = _slab_tiles(rows, cols)

    def body(route_ref, g_ref, p_ref, t_ref):
        t_ref[...] = (g_ref[...] + p_ref[...]).astype(BF16)

    return pl.pallas_call(
        body, name=name, out_shape=jax.ShapeDtypeStruct((3, rows, cols), BF16),
        grid_spec=pltpu.PrefetchScalarGridSpec(
            num_scalar_prefetch=1, grid=(3, rows // tr, cols // tc),
            in_specs=[pl.BlockSpec((1, tr, tc), lambda r, i, j, route_ref: (2 * route_ref[1 + r] + route_ref[0], i, j)),
                      pl.BlockSpec((1, tr, tc), lambda r, i, j, route_ref: (route_ref[1 + r], i, j))],
            out_specs=pl.BlockSpec((1, tr, tc), lambda r, i, j, route_ref: (r, i, j))),
        compiler_params=_params(("parallel", "parallel", "parallel")),
    )(route, g, p)


def _chip_exchange(ts, name, sequencer_id=None, after=None):
    n = len(ts)

    def body(*refs):
        t_refs, l_refs = refs[:n], refs[n:2 * n]
        send_sems, recv_sems = refs[2 * n:]
        x_, y_, c_ = lax.axis_index("x"), lax.axis_index("y"), lax.axis_index("c")
        chips = [(1 - x_, y_), (x_, 1 - y_), (1 - x_, 1 - y_)]
        if sequencer_id is not None:
            barrier = pltpu.get_barrier_semaphore()
            for px, py in chips:
                pl.semaphore_signal(barrier, inc=1, device_id=(px, py, c_), device_id_type=MESH)
            pl.semaphore_wait(barrier, len(chips))
        copies = [pltpu.make_async_remote_copy(
            src_ref=t_refs[a].at[r], dst_ref=l_refs[a].at[r], send_sem=send_sems.at[3 * a + r],
            recv_sem=recv_sems.at[3 * a + r], device_id=(px, py, c_), device_id_type=MESH)
            for a in range(n) for r, (px, py) in enumerate(chips)]
        for cp in copies:
            cp.start()
        for cp in copies:
            cp.wait()

    out_shape = [jax.ShapeDtypeStruct((3,) + t.shape[1:], t.dtype) for t in ts]
    sems = [pltpu.SemaphoreType.DMA((3 * n,)), pltpu.SemaphoreType.DMA((3 * n,))]
    if sequencer_id is not None:
        return _on_sequencer(body, out_shape, sems, ts, after, sequencer_id, name)
    return pl.pallas_call(
        body, name=name, out_shape=out_shape, in_specs=[_ANY] * n, out_specs=[_ANY] * n, scratch_shapes=sems)(*ts)


def _reduce_adamw(gs, ps, landed, place, w, m, v, name):
    layers, rows, cols = w.shape
    assert layers == DEPTH == 2
    tr, tc = _slab_tiles(rows, cols)
    nr, nc = rows // tr, cols // tc
    spec = pl.BlockSpec((1, tr, tc), lambda l, i, j, place_ref: (l, i, j))

    def own(layer, which):
        pi, pj = (nr - 1, nc - 1) if layer == 0 else (0, 0)

        def index(l, i, j, place_ref):
            lead = 0 if which is None else place_ref[which]
            return lead, jnp.where(l == layer, i, pi), jnp.where(l == layer, j, pj)

        return pl.BlockSpec((3 if which is None else 1, tr, tc), index)

    def body(place_ref, g0_ref, p0_ref, l0_ref, g1_ref, p1_ref, l1_ref, w_ref, m_ref, v_ref,
             g_ref, d_ref, nm_ref, nv_ref):
        def update(own_ref, sib_ref, l_ref):
            g = own_ref[0] + sib_ref[0] + l_ref[0].astype(F32) + l_ref[1].astype(F32) + l_ref[2].astype(F32)
            g_ref[0] = g
            d_ref[0], nm_ref[0], nv_ref[0] = _adamw_math(w_ref[0], g, m_ref[0], v_ref[0])

        @pl.when(pl.program_id(0) == 0)
        def _():
            update(g0_ref, p0_ref, l0_ref)

        @pl.when(pl.program_id(0) == 1)
        def _():
            update(g1_ref, p1_ref, l1_ref)

    out = jax.ShapeDtypeStruct(w.shape, F32)
    return pl.pallas_call(
        body, name=name, out_shape=(out, out, out, out),
        grid_spec=pltpu.PrefetchScalarGridSpec(
            num_scalar_prefetch=1, grid=(DEPTH, nr, nc),
            in_specs=[own(0, 0), own(0, 1), own(0, None), own(1, 0), own(1, 1), own(1, None), spec, spec, spec],
            out_specs=(spec, spec, spec, spec)),
        compiler_params=_params(("arbitrary", "arbitrary", "arbitrary")),
    )(place, gs[0], ps[0], landed[0], gs[1], ps[1], landed[1], w, m, v)


def _pack(pieces, row_multiple, dtype, cols=D, rows=None):
    flat = jnp.concatenate([p.astype(dtype).reshape(-1) for p in pieces])
    if rows is None:
        rows = -(-flat.shape[0] // cols)
        rows = -(-rows // row_multiple) * row_multiple
    flat = jnp.pad(flat, (0, rows * cols - flat.shape[0]))
    return flat.reshape(rows, cols)


def _unpack(flat, shapes, lead=()):
    out, off = [], 0
    for shp in shapes:
        n = 1
        for s_ in shp:
            n *= s_
        out.append(lax.slice_in_dim(flat, off, off + n, axis=len(lead)).reshape(lead + tuple(shp)))
        off += n
    return out


def _z_rows_from_in(wt):
    pad = jnp.zeros((NZ - IN_COLS, wt.shape[1]), wt.dtype)
    return jnp.concatenate([wt[1544:2568], wt[2568:5640], wt[0:1536], wt[1536:1544], pad], axis=0)


def _in_rows_from_z(wt):
    return jnp.concatenate([wt[Z_Q:Z_Q + 1536], wt[Z_F:Z_F + 8], wt[Z_PC:Z_PC + 1024], wt[Z_G:Z_G + 3072]], axis=0)


def _pad_rows(v, rows=8):
    return jnp.pad(v, ((0, rows - v.shape[0]), (0, 0)))


def _layer_fwd(l, x, wts, gvec, mod):
    tag = f"l{l}"
    h = _prenorm_fwd(x, gvec, mod, 0, 0, 1, f"prenorm_mix_{tag}")
    z = _matmul(h, wts["w_in_t"], "nt", f"in_proj_{tag}", tn=1152)
    qa, ka, va, kat = _attn_prep(z, wts["b_f"], f"attn_prep_{tag}")
    o, lse = _attn_fwd(qa, ka, va, f"attn_{tag}")
    br_b = _pool_fwd(z, wts["wp_bd"], wts["pool_scale"], f"pool_{tag}")
    br_c = _conv_fwd(z, wts["conv_w"], f"conv_{tag}")
    pa = _matmul(o, wts["wa"], "nn", f"proj_a_{tag}")
    pb = _matmul(br_b, wts["wb"], "nn", f"proj_b_{tag}")
    gates = [(z, Z_G + k * D) for k in range(3)]
    pc, merged = _matmul(br_c, wts["wc"], "nn", f"proj_c_merge_{tag}", tm=512, tn=512,
                         extra=gates + [(pa, 0), (pb, 0)], epilogue=_merge_epilogue, out_dtypes=(F32, BF16))
    y = _matmul(merged, wts["w_out"], "nn", f"out_proj_{tag}")
    x1 = _postnorm_fwd(x, y, gvec, mod, 1, 2, f"postnorm_mix_{tag}")
    h2 = _prenorm_fwd(x1, gvec, mod, 2, 3, 4, f"prenorm_ff_{tag}")
    a, r = _matmul(h2, wts["w_ff1"], "nn", f"ff1_{tag}", b_col_shards=True, epilogue=_relu2_epilogue,
                   out_dtypes=(F32, BF16))
    y2 = _matmul(r, wts["w_ff2"], "nn", f"ff2_{tag}")
    x2 = _postnorm_fwd(x1, y2, gvec, mod, 3, 5, f"postnorm_ff_{tag}")
    saved = dict(x=x, h=h, z=z, qa=qa, ka=ka, va=va, kat=kat, o=o, lse=lse, br_b=br_b, br_c=br_c, pa=pa, pb=pb, pc=pc,
                 merged=merged, y=y, x1=x1, h2=h2, a=a, r=r, y2=y2)
    return x2, saved


def _ffn_bwd(l, dx2, sv, wts, gvec, mod, midpoint):
    tag = f"l{l}"
    dy2, red_post_ff = _postnorm_bwd(sv["y2"], gvec, mod, dx2, 3, 5, f"postnorm_ff_bwd_{tag}")
    da = midpoint(_matmul(dy2, wts["w_ff2"], "nt", f"ff2_dx_{tag}", extra=[(sv["a"], 0)],
                          epilogue=_relu2_bwd_epilogue, out_dtypes=(BF16,))[0])
    d_w_ff2 = _matmul(sv["r"], dy2, "tn", f"ff2_dw_{tag}")
    dh2 = _matmul(da, wts["w_ff1"], "nt", f"ff1_dx_{tag}", b_col_shards=True)
    d_w_ff1 = _matmul(sv["h2"], da, "tn", f"ff1_dw_{tag}", out_col_shards=True)
    dx1, red_pre_ff = _prenorm_bwd(sv["x1"], gvec, mod, dh2, dx2, 2, 4, f"prenorm_ff_bwd_{tag}")
    return dx1, [d_w_ff1, d_w_ff2.reshape(N_DEV, D_FF // N_DEV, D)], (red_pre_ff, red_post_ff)


def _mixer_bwd(l, dx1, sv, wts, gvec, mod, ffn_reds, midpoint):
    tag = f"l{l}"
    red_pre_ff, red_post_ff = ffn_reds
    dy, red_post_mix = _postnorm_bwd(sv["y"], gvec, mod, dx1, 1, 2, f"postnorm_mix_bwd_{tag}")
    gates = [(sv["z"], Z_G + k * D) for k in range(3)]
    dpa, dpb, dpc, *dgl = _matmul(dy, wts["w_out"], "nt", f"out_proj_dx_{tag}", tm=512, tn=512,
                                  extra=gates + [(sv["pa"], 0), (sv["pb"], 0), (sv["pc"], 0)],
                                  epilogue=_merge_bwd_epilogue, out_dtypes=(BF16,) * 6)
    d_w_out = _matmul(sv["merged"], dy, "tn", f"out_proj_dw_{tag}")
    dpa = midpoint(dpa)
    do = _matmul(dpa, wts["wa"], "nt", f"proj_a_dx_{tag}")
    dbr_b = _matmul(dpb, wts["wb"], "nt", f"proj_b_dx_{tag}")
    dbr_c = _matmul(dpc, wts["wc"], "nt", f"proj_c_dx_{tag}")
    d_wa = _matmul(sv["o"], dpa, "tn", f"proj_a_dw_{tag}")
    d_wb = _matmul(sv["br_b"], dpb, "tn", f"proj_b_dw_{tag}")
    d_wc = _matmul(sv["br_c"], dpc, "tn", f"proj_c_dw_{tag}")
    d_w_branch = jnp.concatenate([d_wa, d_wb, d_wc], axis=0)

    dpu, d_wp_bd, red_pool = _pool_bwd(sv["z"], wts["wp_bd"], wts["pool_scale"], dbr_b, f"pool_bwd_{tag}")
    dconv, red_conv = _conv_bwd(sv["z"], wts["conv_w"], dbr_c, f"conv_bwd_{tag}")
    qa2, doa = _attn_bwd_prep(sv["qa"], sv["o"], sv["lse"], do, f"attn_bwd_prep_{tag}")
    dqt, dka, dva = _attn_bwd(qa2, sv["ka"], sv["va"], sv["kat"], doa, f"attn_bwd_{tag}")
    dq, dk, dv, dfl, red_f = _attn_bwd_post(sv["z"], wts["b_f"], dqt, dka, dva, f"attn_bwd_post_{tag}")
    dz = jnp.concatenate([dpu, dconv, *dgl, dq, dk, dv, dfl], axis=1)
    dh = _matmul(dz, wts["w_in_t"], "nn", f"in_proj_dx_{tag}", tk=1152)
    d_w_in_t = _matmul(dz, sv["h"], "tn", f"in_proj_dw_{tag}", tm=1152)
    dx0, red_pre_mix = _prenorm_bwd(sv["x"], gvec, mod, dh, dx1, 0, 1, f"prenorm_mix_bwd_{tag}")

    rows = D // N_DEV
    big = [_in_rows_from_z(d_w_in_t).reshape(N_DEV, IN_SHARD, D), d_w_branch.reshape(N_DEV, rows, D),
           d_w_out.reshape(N_DEV, rows, D)]
    d_w_pool = jnp.stack([d_wp_bd[64 * g:64 * (g + 1), 64 * g:64 * (g + 1)] for g in range(4)])
    small = dict(
        mod=jnp.stack([red_pre_mix[0], red_pre_mix[1], red_post_mix[0], red_pre_ff[0], red_pre_ff[1], red_post_ff[0]]),
        g_mix_pre=red_pre_mix[2], g_mix_post=red_post_mix[1], g_ff_pre=red_pre_ff[2], g_ff_post=red_post_ff[1],
        b_f=red_f[0, 0:8], w_pool=d_w_pool, pool_scale=red_pool[0], conv_w=red_conv[0:3])
    return dx0, big, small


SMALL_KEYS = ["mod", "g_mix_pre", "g_mix_post", "g_ff_pre", "g_ff_post", "b_f", "w_pool", "pool_scale", "conv_w"]
SMALL_SHAPES = [(DEPTH, 6 * D), (DEPTH, D), (DEPTH, D), (DEPTH, D), (DEPTH, D), (DEPTH, 8), (DEPTH, 4, 64, 64),
                (DEPTH, POOL_W), (DEPTH, 3, CONV_W)]


def kernel(x, c, w_ada, b_ada, g_mix_pre, g_mix_post, g_ff_pre, g_ff_post, w_in, b_f, w_pool, pool_scale, conv_w, w_branch, w_out, w_ff1, w_ff2, loss_target, m_w_ada, m_b_ada, m_g_mix_pre, m_g_mix_post, m_g_ff_pre, m_g_ff_post, m_w_in, m_b_f, m_w_pool, m_pool_scale, m_conv_w, m_w_branch, m_w_out, m_w_ff1, m_w_ff2, v_w_ada, v_b_ada, v_g_mix_pre, v_g_mix_post, v_g_ff_pre, v_g_ff_post, v_w_in, v_b_f, v_w_pool, v_pool_scale, v_conv_w, v_w_branch, v_w_out, v_w_ff1, v_w_ff2):
    ix, iy, ic = lax.axis_index("x"), lax.axis_index("y"), lax.axis_index("c")
    me = 4 * ix + 2 * iy + ic
    route = jnp.stack([ic, 2 * (1 - ix) + iy, 2 * ix + (1 - iy), 2 * (1 - ix) + (1 - iy)]).astype(jnp.int32)
    place = jnp.stack([me, 2 * ix + iy]).astype(jnp.int32)
    wt_in, mt_in, vt_in = (jnp.transpose(a, (0, 2, 1)) for a in (w_in, m_w_in, v_w_in))

    c_all = _all_gather([_pad_rows(c)], "gather_c")[0][:, 0, :]
    c_pad = _pad_rows(c_all, ADA_ROWS)
    b_cols = lax.dynamic_slice_in_dim(b_ada, me * ADA_COLS, ADA_COLS, axis=1)
    b_cols = jnp.broadcast_to(b_cols[:, None, :], (DEPTH, 8, ADA_COLS))
    mod_part = _ada_fwd(c_pad, w_ada, b_cols, "ada_fwd")
    mod_all = _all_gather([mod_part.reshape(DEPTH * ADA_ROWS, ADA_COLS)], "gather_mod")[0]
    mod_all = mod_all.reshape(N_DEV, DEPTH, ADA_ROWS, ADA_COLS)
    mod_mine = lax.dynamic_index_in_dim(mod_all, me, axis=2, keepdims=False)
    mod_mine = jnp.transpose(mod_mine, (1, 0, 2)).reshape(DEPTH, 6, D)

    cw_cols = CONV_W // N_DEV
    cw_send = jnp.pad(conv_w.reshape(DEPTH * 3, cw_cols), ((0, 8 - DEPTH * 3), (0, LANE - cw_cols)))
    send = [[w[l].astype(BF16) for w in (wt_in, w_branch, w_out, w_ff1, w_ff2)] for l in range(DEPTH)]
    first = _all_gather(send[0][:1], "gather_weights_l0_in", sequencer_id=1, after=mod_all)
    rest = _all_gather(send[0][1:] + [cw_send], "gather_weights_l0_rest", sequencer_id=2, after=first[0])
    gathered = [first + rest[:4], _all_gather(send[1], "gather_weights_l1", sequencer_id=3, after=first[0])]
    cw_all = rest[4][:, :DEPTH * 3, :cw_cols].reshape(N_DEV, DEPTH, 3, cw_cols)

    def layer_operands(l, weights):
        p_in, p_br, p_out, p_ff1, p_ff2 = weights[:5]
        w_br_full = p_br.reshape(D, D)
        cw_full = jnp.transpose(cw_all[:, l], (1, 0, 2)).reshape(3, CONV_W)
        wp_bd = jnp.zeros((POOL_W, POOL_W), F32)
        for g in range(4):
            wp_bd = wp_bd.at[64 * g:64 * (g + 1), 64 * g:64 * (g + 1)].set(w_pool[l, g])
        wts = dict(
            w_in_t=_z_rows_from_in(p_in.reshape(IN_COLS, D)), wa=w_br_full[0:A_WIDTH], wb=w_br_full[A_WIDTH:A_WIDTH + POOL_W],
            wc=w_br_full[A_WIDTH + POOL_W:], w_out=p_out.reshape(D, D),
            w_ff1=p_ff1, w_ff2=p_ff2.reshape(D_FF, D),
            conv_w=_pad_rows(cw_full), wp_bd=wp_bd.astype(BF16), pool_scale=_pad_rows(pool_scale[l][None, :]),
            b_f=_pad_rows(jnp.pad(b_f[l], (0, LANE - 8))[None, :]))
        gvec = _pad_rows(jnp.stack([g_mix_pre[l], g_mix_post[l], g_ff_pre[l], g_ff_post[l]]))
        return wts, gvec, _pad_rows(mod_mine[l])

    xs = x[0]
    saved, layers = [], []
    for l in range(DEPTH):
        weights = gathered[l]
        if l > 0:
            xs, weights = lax.optimization_barrier((xs, weights))
        layers.append(layer_operands(l, weights))
        xs, sv = _layer_fwd(l, xs, *layers[l])
        saved.append(sv)
    dx, loss_part = _loss_head(xs, loss_target[0], "loss_head")
    loss = lax.psum(loss_part[0, 0], ("x", "y", "c"))
    small_grads = [None] * DEPTH
    mine, sibs, landed = ({} for _ in range(3))
    seq_id = iter(range(4, 4 + 4 * DEPTH))
    last = [gathered[DEPTH - 1][0]]

    def start(group, grads):
        mine[group] = grads
        sibs[group] = _sibling_exchange(grads, f"rs_sibling_{group}", sequencer_id=next(seq_id), after=last[0])
        last[0] = sibs[group][0]

    def finish(group, later):
        later, (grads, sib) = lax.optimization_barrier((later, (mine[group], sibs[group])))
        sends = [_pair_sums(g, p, route, f"rs_pair_sums_{group}_{k}") for k, (g, p) in enumerate(zip(grads, sib))]
        later, sends = lax.optimization_barrier((later, sends))
        landed[group] = _chip_exchange(sends, f"rs_chips_{group}", sequencer_id=next(seq_id), after=last[0])
        last[0] = landed[group][0]
        return later

    pending = None
    for l in reversed(range(DEPTH)):
        hook = (lambda da: da) if pending is None else functools.partial(finish, pending)
        dx, ffn_grads, ffn_reds = _ffn_bwd(l, dx, saved[l], *layers[l], hook)
        start(f"ffn_l{l}", ffn_grads)
        dx, mix_grads, small_grads[l] = _mixer_bwd(l, dx, saved[l], *layers[l], ffn_reds,
                                                   functools.partial(finish, f"ffn_l{l}"))
        start(f"mix_l{l}", mix_grads)
        pending = f"mix_l{l}"
    grad_x = dx[None]

    big_w = [wt_in, w_branch, w_out, w_ff1, w_ff2]
    big_m = [mt_in, m_w_branch, m_w_out, m_w_ff1, m_w_ff2]
    big_v = [vt_in, v_w_branch, v_w_out, v_w_ff1, v_w_ff2]
    where = [("mix", 0), ("mix", 1), ("mix", 2), ("ffn", 0), ("ffn", 1)]

    def reduce_and_update(k):
        group, at = where[k]
        return _reduce_adamw([mine[f"{group}_l{l}"][at] for l in range(DEPTH)],
                             [sibs[f"{group}_l{l}"][at] for l in range(DEPTH)],
                             [landed[f"{group}_l{l}"][at] for l in range(DEPTH)], place, big_w[k], big_m[k], big_v[k],
                             f"rs_sum_adamw_{k}")

    big_res = {k: list(reduce_and_update(k)) for k in (3, 4)}
    big_res[3][0] = finish(pending, big_res[3][0])

    small = {k: jnp.stack([small_grads[l][k] for l in range(DEPTH)]) for k in SMALL_KEYS}
    small_all = _all_gather([_pack([small[k] for k in SMALL_KEYS], 8, F32)], "gather_small")[0]
    dmod_all = small_all[:, 0:DEPTH * 6, :].reshape(N_DEV, DEPTH, 6 * D)
    summed = _unpack(_sum_slabs(small_all, "sum_small").reshape(-1), SMALL_SHAPES)
    sg = dict(zip(SMALL_KEYS, summed))
    dmod_cols = lax.dynamic_slice_in_dim(dmod_all, me * ADA_COLS, ADA_COLS, axis=2)
    dmod_cols = jnp.pad(jnp.transpose(dmod_cols, (1, 0, 2)), ((0, 0), (0, ADA_ROWS - N_DEV), (0, 0)))
    g_w_ada = _ada_bwd(c_pad, dmod_cols, "ada_bwd")
    g_conv_w = lax.dynamic_slice_in_dim(sg["conv_w"], me * (CONV_W // N_DEV), CONV_W // N_DEV, axis=2)

    ada_out = [g_w_ada] + list(_adamw(w_ada, g_w_ada, m_w_ada, v_w_ada, "adamw_ada"))
    rest_w = [b_ada, g_mix_pre, g_mix_post, g_ff_pre, g_ff_post, b_f, w_pool, pool_scale, conv_w]
    rest_m = [m_b_ada, m_g_mix_pre, m_g_mix_post, m_g_ff_pre, m_g_ff_post, m_b_f, m_w_pool, m_pool_scale, m_conv_w]
    rest_v = [v_b_ada, v_g_mix_pre, v_g_mix_post, v_g_ff_pre, v_g_ff_post, v_b_f, v_w_pool, v_pool_scale, v_conv_w]
    rest_g = [sg["mod"], sg["g_mix_pre"], sg["g_mix_post"], sg["g_ff_pre"], sg["g_ff_post"], sg["b_f"],
              sg["w_pool"], sg["pool_scale"], g_conv_w]
    rest_shapes = [a.shape for a in rest_w]
    upd = _adamw(_pack(rest_w, 8, F32)[None], _pack(rest_g, 8, F32)[None], _pack(rest_m, 8, F32)[None],
                 _pack(rest_v, 8, F32)[None], "adamw_rest")
    rest_out = [rest_g] + [_unpack(arr.reshape(-1), rest_shapes) for arr in upd]
    rest_out = [[ada_out[which]] + rest_out[which] for which in range(4)]

    landed[pending], rest_out = lax.optimization_barrier((landed[pending], rest_out))
    big_res.update({k: reduce_and_update(k) for k in (0, 1, 2)})
    big_out = [[jnp.transpose(big_res[k][which], (0, 2, 1)) if k == 0 else big_res[k][which] for k in range(5)]
               for which in range(4)]

    def ordered(k):
        r, b = rest_out[k], big_out[k]
        return [r[0], r[1], r[2], r[3], r[4], r[5], b[0], r[6], r[7], r[8], r[9], b[1], b[2], b[3], b[4]]

    return (loss, grad_x, *ordered(0), *ordered(1), *ordered(2), *ordered(3))
```

```python
import functools

import jax
import jax.numpy as jnp
from jax import lax
from jax.experimental import pallas as pl
from jax.experimental.pallas import tpu as pltpu
from jax.experimental.pallas import tpu_sc as plsc

F32 = jnp.float32
BF16 = jnp.bfloat16

N_DEV = 8
D = 1024
S = 2048
DEPTH = 2
D_FF = 4 * D
A_WIDTH = 512
HEAD_DIM = 64
N_PAIR = 4
POOL_W = 256
CONV_W = 256
IN_COLS = 5640
ADA_COLS = 6 * D // N_DEV
IN_SHARD = IN_COLS // N_DEV
RMS_EPS = 1e-6
NEG_INF = -1e30
ATT_SCALE = HEAD_DIM ** -0.5

NZ = 5760
Z_PC = 0
Z_G = 1024
Z_Q = 4096
Z_K = 4608
Z_V = 5120
Z_F = 5632

LR, B1, B2, EPS, WD, STEP = 0.001, 0.9, 0.999, 1e-08, 0.01, 10

LANE = 128
VMEM_LIMIT_BYTES = 48 * 1024 * 1024
TS = 512
TQ = 256
TQ_FWD = 512
HEADS_PER_STEP = 4


def _params(sem=None):
    return pltpu.CompilerParams(dimension_semantics=sem, vmem_limit_bytes=VMEM_LIMIT_BYTES)


def _pick(n, target):
    best = None
    for t in range(LANE, min(n, target) + 1, LANE):
        if n % t == 0:
            best = t
    return n if best is None else best


def _matmul(a, b, mode, name, out_dtype=F32, tm=1024, tn=1024, tk=1024, b_col_shards=False, out_col_shards=False,
            extra=(), epilogue=None, out_dtypes=None):
    if b_col_shards:
        shards, b_rows, shard_cols = b.shape
        b_shape = (b_rows, shards * shard_cols)
    else:
        b_shape = b.shape
    if mode == "nn":
        (m, k), (k2, n) = a.shape, b_shape
    elif mode == "nt":
        (m, k), (n, k2) = a.shape, b_shape
    else:
        (k, m), (k2, n) = a.shape, b_shape
    assert k == k2, (a.shape, b.shape, mode)
    tm, tn, tk = _pick(m, tm), _pick(n, tn), _pick(k, tk)
    if b_col_shards and mode == "nn":
        tn = shard_cols
    if b_col_shards and mode == "nt":
        tk = shard_cols
    if out_col_shards:
        tn = n // N_DEV
    nk = k // tk
    if mode == "nn":
        a_spec = pl.BlockSpec((tm, tk), lambda i, j, kk: (i, kk))
        b_spec = (pl.BlockSpec((None, tk, tn), lambda i, j, kk: (j, kk, 0)) if b_col_shards else
                  pl.BlockSpec((tk, tn), lambda i, j, kk: (kk, j)))
        dims = (((1,), (0,)), ((), ()))
    elif mode == "nt":
        a_spec = pl.BlockSpec((tm, tk), lambda i, j, kk: (i, kk))
        b_spec = (pl.BlockSpec((None, tn, tk), lambda i, j, kk: (kk, j, 0)) if b_col_shards else
                  pl.BlockSpec((tn, tk), lambda i, j, kk: (j, kk)))
        dims = (((1,), (1,)), ((), ()))
    else:
        assert not b_col_shards
        a_spec = pl.BlockSpec((tk, tm), lambda i, j, kk: (kk, i))
        b_spec = pl.BlockSpec((tk, tn), lambda i, j, kk: (kk, j))
        dims = (((0,), (0,)), ((), ()))
    if out_col_shards:
        out_shape = jax.ShapeDtypeStruct((N_DEV, m, tn), out_dtype)
        out_spec = pl.BlockSpec((None, tm, tn), lambda i, j, kk: (j, i, 0))
    else:
        out_shape = jax.ShapeDtypeStruct((m, n), out_dtype)
        out_spec = pl.BlockSpec((tm, tn), lambda i, j, kk: (i, j))

    n_extra = len(extra)
    extra_specs = [pl.BlockSpec((tm, tn), lambda i, j, kk, off=off: (i, j + off // tn)) for _, off in extra]
    if epilogue is not None:
        assert not out_col_shards and all(off % tn == 0 for _, off in extra)
        out_shape = [jax.ShapeDtypeStruct((m, n), dt) for dt in out_dtypes]
        out_spec = [pl.BlockSpec((tm, tn), lambda i, j, kk: (i, j)) for _ in out_dtypes]

    def product(a_ref, b_ref):
        return lax.dot_general(a_ref[...].astype(BF16), b_ref[...].astype(BF16), dims, preferred_element_type=F32)

    def write(acc, extra_refs, o_refs):
        if epilogue is None:
            o_refs[0][...] = acc.astype(out_dtype)
        else:
            for o_ref, tile in zip(o_refs, epilogue(acc, *[r[...] for r in extra_refs])):
                o_ref[...] = tile.astype(o_ref.dtype)

    def body_one_pass(a_ref, b_ref, *refs):
        write(product(a_ref, b_ref), refs[:n_extra], refs[n_extra:])

    def body(a_ref, b_ref, *refs):
        acc_ref = refs[-1]
        kk = pl.program_id(2)

        @pl.when(kk == 0)
        def _():
            acc_ref[...] = product(a_ref, b_ref)

        @pl.when(kk > 0)
        def _():
            acc_ref[...] += product(a_ref, b_ref)

        @pl.when(kk == nk - 1)
        def _():
            write(acc_ref[...], refs[:n_extra], refs[n_extra:-1])

    return pl.pallas_call(
        body_one_pass if nk == 1 else body, name=name,
        out_shape=out_shape,
        grid=(m // tm, n // tn, nk),
        in_specs=[a_spec, b_spec] + extra_specs,
        out_specs=out_spec,
        scratch_shapes=[] if nk == 1 else [pltpu.VMEM((tm, tn), F32)],
        compiler_params=_params(("parallel", "parallel", "arbitrary")),
    )(a, b, *[x for x, _ in extra])


def _row_spec(width=D, col=0):
    return pl.BlockSpec((TS, width), lambda i: (i, col))


def _vec_spec(rows=8, width=D):
    return pl.BlockSpec((rows, width), lambda i: (0, 0))


def _rms(x):
    return lax.rsqrt(jnp.mean(x * x, axis=-1, keepdims=True) + RMS_EPS)


def _prenorm_fwd(x, gvec, mod, g_row, shift_row, scale_row, name):
    def body(x_ref, g_ref, mod_ref, h_ref):
        xv = x_ref[...]
        y = xv * _rms(xv) * g_ref[g_row:g_row + 1, :]
        h = y * (1.0 + mod_ref[scale_row:scale_row + 1, :]) + mod_ref[shift_row:shift_row + 1, :]
        h_ref[...] = h.astype(BF16)

    return pl.pallas_call(
        body, name=name, out_shape=jax.ShapeDtypeStruct((S, D), BF16), grid=(S // TS,),
        in_specs=[_row_spec(), _vec_spec(), _vec_spec()], out_specs=_row_spec(),
        compiler_params=_params(("parallel",)),
    )(x, gvec, mod)


def _prenorm_bwd(x, gvec, mod, dh, dres, g_row, scale_row, name):
    def body(x_ref, g_ref, mod_ref, dh_ref, dres_ref, dx_ref, red_ref):
        i = pl.program_id(0)

        @pl.when(i == 0)
        def _():
            red_ref[...] = jnp.zeros_like(red_ref)

        xv = x_ref[...]
        g = g_ref[g_row:g_row + 1, :]
        r = _rms(xv)
        n = xv * r
        yg = n * g
        dhv = dh_ref[...]
        dyg = dhv * (1.0 + mod_ref[scale_row:scale_row + 1, :])
        dn = dyg * g
        dx = r * (dn - n * jnp.mean(dn * n, axis=-1, keepdims=True))
        dx_ref[...] = dres_ref[...] + dx
        red_ref[0:1, :] += jnp.sum(dhv, axis=0, keepdims=True)
        red_ref[1:2, :] += jnp.sum(dhv * yg, axis=0, keepdims=True)
        red_ref[2:3, :] += jnp.sum(dyg * n, axis=0, keepdims=True)

    return pl.pallas_call(
        body, name=name,
        out_shape=(jax.ShapeDtypeStruct((S, D), F32), jax.ShapeDtypeStruct((8, D), F32)),
        grid=(S // TS,),
        in_specs=[_row_spec(), _vec_spec(), _vec_spec(), _row_spec(), _row_spec()],
        out_specs=(_row_spec(), _vec_spec()),
        compiler_params=_params(("arbitrary",)),
    )(x, gvec, mod, dh, dres)


def _postnorm_fwd(x, y, gvec, mod, g_row, gate_row, name):
    def body(x_ref, y_ref, g_ref, mod_ref, o_ref):
        yv = y_ref[...]
        yn = yv * _rms(yv) * g_ref[g_row:g_row + 1, :]
        o_ref[...] = x_ref[...] + mod_ref[gate_row:gate_row + 1, :] * yn

    return pl.pallas_call(
        body, name=name, out_shape=jax.ShapeDtypeStruct((S, D), F32), grid=(S // TS,),
        in_specs=[_row_spec(), _row_spec(), _vec_spec(), _vec_spec()], out_specs=_row_spec(),
        compiler_params=_params(("parallel",)),
    )(x, y, gvec, mod)


def _postnorm_bwd(y, gvec, mod, dxo, g_row, gate_row, name):
    def body(y_ref, g_ref, mod_ref, dxo_ref, dy_ref, red_ref):
        i = pl.program_id(0)

        @pl.when(i == 0)
        def _():
            red_ref[...] = jnp.zeros_like(red_ref)

        yv = y_ref[...]
        g = g_ref[g_row:g_row + 1, :]
        r = _rms(yv)
        n = yv * r
        dxo = dxo_ref[...]
        dyn = dxo * mod_ref[gate_row:gate_row + 1, :]
        dn = dyn * g
        dy = r * (dn - n * jnp.mean(dn * n, axis=-1, keepdims=True))
        dy_ref[...] = dy.astype(BF16)
        red_ref[0:1, :] += jnp.sum(dxo * (n * g), axis=0, keepdims=True)
        red_ref[1:2, :] += jnp.sum(dyn * n, axis=0, keepdims=True)

    return pl.pallas_call(
        body, name=name,
        out_shape=(jax.ShapeDtypeStruct((S, D), BF16), jax.ShapeDtypeStruct((8, D), F32)),
        grid=(S // TS,),
        in_specs=[_row_spec(), _vec_spec(), _vec_spec(), _row_spec()],
        out_specs=(_row_spec(), _vec_spec()),
        compiler_params=_params(("arbitrary",)),
    )(y, gvec, mod, dxo)


def _loss_head(xf, target, name):
    def body(x_ref, t_ref, dx_ref, loss_ref):
        i = pl.program_id(0)

        @pl.when(i == 0)
        def _():
            loss_ref[...] = jnp.zeros_like(loss_ref)

        e = x_ref[...] - t_ref[...]
        dx_ref[...] = e / float(D)
        per_tok = jnp.mean(e * e, axis=-1, keepdims=True)
        loss_ref[0:1, 0:1] += 0.5 * jnp.sum(per_tok, axis=0, keepdims=True)

    return pl.pallas_call(
        body, name=name,
        out_shape=(jax.ShapeDtypeStruct((S, D), F32), jax.ShapeDtypeStruct((8, LANE), F32)),
        grid=(S // TS,),
        in_specs=[_row_spec(), _row_spec()],
        out_specs=(_row_spec(), pl.BlockSpec((8, LANE), lambda i: (0, 0))),
        compiler_params=_params(("arbitrary",)),
    )(xf, target)


def _relu2_epilogue(a):
    t = jnp.maximum(a, 0.0)
    return a, t * t


def _relu2_bwd_epilogue(dr, a):
    return (dr * (2.0 * jnp.maximum(a, 0.0)),)


def _merge_epilogue(pc, g0, g1, g2, pa, pb):
    return pc, jax.nn.sigmoid(g0) * pa + jax.nn.sigmoid(g1) * pb + jax.nn.sigmoid(g2) * pc


def _merge_bwd_epilogue(dm, g0, g1, g2, pa, pb, pc):
    sg = [jax.nn.sigmoid(g) for g in (g0, g1, g2)]
    return tuple(dm * s for s in sg) + tuple(dm * p * (s * (1.0 - s)) for p, s in zip((pa, pb, pc), sg))


def _shift_down(x, k, row):
    return jnp.where(row >= k, pltpu.roll(x, k, axis=0), 0.0)


def _shift_up(x, k, row):
    n = x.shape[0]
    return jnp.where(row < n - k, pltpu.roll(x, n - k, axis=0), 0.0)


def _cumsum_rows(x, row, reverse=False):
    shift = _shift_up if reverse else _shift_down
    k = 1
    while k < x.shape[0]:
        x = x + shift(x, k, row)
        k *= 2
    return x


def _full_spec(shape, idx=(0, 0)):
    return pl.BlockSpec(shape, lambda i: idx)


def _pool_window_select(lane, a2, a4, a8, a16):
    return jnp.where(lane < 64, a2, jnp.where(lane < 128, a4, jnp.where(lane < 192, a8, a16)))


def _pool_p(u, row, lane):
    t2 = u + _shift_down(u, 1, row)
    t4 = t2 + _shift_down(t2, 2, row)
    t8 = t4 + _shift_down(t4, 4, row)
    t16 = t8 + _shift_down(t8, 8, row)
    tw = _pool_window_select(lane, t2, t4, t8, t16)
    cnt = jnp.minimum((row + 1).astype(F32), _pool_window_select(lane, 2.0, 4.0, 8.0, 16.0))
    return tw / cnt - u, cnt


def _pool_fwd(z, wp_bd, pscale, name):
    def body(u_ref, w_ref, s_ref, o_ref):
        row = lax.broadcasted_iota(jnp.int32, (S, POOL_W), 0)
        lane = lax.broadcasted_iota(jnp.int32, (S, POOL_W), 1)
        p, _ = _pool_p(u_ref[...], row, lane)
        y = jnp.dot(p.astype(BF16), w_ref[...], preferred_element_type=F32)
        o_ref[...] = y * s_ref[0:1, :]

    return pl.pallas_call(
        body, name=name, out_shape=jax.ShapeDtypeStruct((S, POOL_W), F32), grid=(1,),
        in_specs=[_full_spec((S, POOL_W), (0, Z_PC // POOL_W)), _full_spec((POOL_W, POOL_W)), _full_spec((8, POOL_W))],
        out_specs=_full_spec((S, POOL_W)),
        compiler_params=_params(("arbitrary",)),
    )(z, wp_bd, pscale)


def _pool_bwd(z, wp_bd, pscale, dbr, name):
    def body(u_ref, w_ref, s_ref, dbr_ref, du_ref, dw_ref, red_ref):
        row = lax.broadcasted_iota(jnp.int32, (S, POOL_W), 0)
        lane = lax.broadcasted_iota(jnp.int32, (S, POOL_W), 1)
        p, cnt = _pool_p(u_ref[...], row, lane)
        pb = p.astype(BF16)
        y = jnp.dot(pb, w_ref[...], preferred_element_type=F32)
        dbr = dbr_ref[...]
        red_ref[...] = jnp.zeros_like(red_ref)
        red_ref[0:1, :] = jnp.sum(dbr * y, axis=0, keepdims=True)
        dy = (dbr * s_ref[0:1, :]).astype(BF16)
        dw_ref[...] = lax.dot_general(pb, dy, (((0,), (0,)), ((), ())), preferred_element_type=F32)
        dp = lax.dot_general(dy, w_ref[...], (((1,), (1,)), ((), ())), preferred_element_type=F32)
        g = dp / cnt
        a2 = g + _shift_up(g, 1, row)
        a4 = a2 + _shift_up(a2, 2, row)
        a8 = a4 + _shift_up(a4, 4, row)
        a16 = a8 + _shift_up(a8, 8, row)
        du_ref[...] = (_pool_window_select(lane, a2, a4, a8, a16) - dp).astype(BF16)

    return pl.pallas_call(
        body, name=name,
        out_shape=(jax.ShapeDtypeStruct((S, POOL_W), BF16), jax.ShapeDtypeStruct((POOL_W, POOL_W), F32),
                   jax.ShapeDtypeStruct((8, POOL_W), F32)),
        grid=(1,),
        in_specs=[_full_spec((S, POOL_W), (0, Z_PC // POOL_W)), _full_spec((POOL_W, POOL_W)), _full_spec((8, POOL_W)),
                  _full_spec((S, POOL_W))],
        out_specs=(_full_spec((S, POOL_W)), _full_spec((POOL_W, POOL_W)), _full_spec((8, POOL_W))),
        compiler_params=_params(("arbitrary",)),
    )(z, wp_bd, pscale, dbr)


def _conv_specs():
    base = Z_PC // CONV_W
    return [_full_spec((S, CONV_W), (0, base + 1)), _full_spec((S, CONV_W), (0, base + 2)),
            _full_spec((S, CONV_W), (0, base + 3)), _full_spec((8, CONV_W))]


def _conv_fwd(z, cw, name):
    def body(h_ref, b_ref, c_ref, w_ref, o_ref):
        row = lax.broadcasted_iota(jnp.int32, (S, CONV_W), 0)
        u = c_ref[...] * h_ref[...]
        y = (w_ref[0:1, :] * _shift_down(u, 2, row) + w_ref[1:2, :] * _shift_down(u, 1, row) + w_ref[2:3, :] * u)
        o_ref[...] = b_ref[...] * y

    return pl.pallas_call(
        body, name=name, out_shape=jax.ShapeDtypeStruct((S, CONV_W), F32), grid=(1,),
        in_specs=_conv_specs(), out_specs=_full_spec((S, CONV_W)),
        compiler_params=_params(("arbitrary",)),
    )(z, z, z, cw)


def _conv_bwd(z, cw, dbr, name):
    def body(h_ref, b_ref, c_ref, w_ref, dbr_ref, d_ref, red_ref):
        row = lax.broadcasted_iota(jnp.int32, (S, CONV_W), 0)
        h, cg = h_ref[...], c_ref[...]
        u = cg * h
        u1 = _shift_down(u, 1, row)
        u2 = _shift_down(u, 2, row)
        y = w_ref[0:1, :] * u2 + w_ref[1:2, :] * u1 + w_ref[2:3, :] * u
        dbr = dbr_ref[...]
        dy = dbr * b_ref[...]
        du = w_ref[2:3, :] * dy + w_ref[1:2, :] * _shift_up(dy, 1, row) + w_ref[0:1, :] * _shift_up(dy, 2, row)
        d_ref[:, 0:CONV_W] = (du * cg).astype(BF16)
        d_ref[:, CONV_W:2 * CONV_W] = (dbr * y).astype(BF16)
        d_ref[:, 2 * CONV_W:3 * CONV_W] = (du * h).astype(BF16)
        red_ref[...] = jnp.zeros_like(red_ref)
        red_ref[0:1, :] = jnp.sum(dy * u2, axis=0, keepdims=True)
        red_ref[1:2, :] = jnp.sum(dy * u1, axis=0, keepdims=True)
        red_ref[2:3, :] = jnp.sum(dy * u, axis=0, keepdims=True)

    return pl.pallas_call(
        body, name=name,
        out_shape=(jax.ShapeDtypeStruct((S, 3 * CONV_W), BF16), jax.ShapeDtypeStruct((8, CONV_W), F32)),
        grid=(1,),
        in_specs=_conv_specs() + [_full_spec((S, CONV_W))],
        out_specs=(_full_spec((S, 3 * CONV_W)), _full_spec((8, CONV_W))),
        compiler_params=_params(("arbitrary",)),
    )(z, z, z, cw, dbr)


_NT = (((1,), (1,)), ((), ()))
_TN = (((0,), (0,)), ((), ()))
N_HEAD = 2 * N_PAIR


def _split3(x):
    hi = x.astype(BF16).astype(F32)
    mid = (x - hi).astype(BF16).astype(F32)
    lo = (x - hi - mid).astype(BF16).astype(F32)
    return hi, mid, lo


def _spare(lane, e, k):
    return lane == 64 * (1 - e) + k


def _spare3(lane, e, k):
    base = 64 * (1 - e) + k
    return (lane >= base) & (lane < base + 3)


def _put3(lane, e, k, pieces, rest):
    out = rest
    for n, piece in enumerate(pieces):
        out = jnp.where(_spare(lane, e, k + n), piece, out)
    return out


def _attn_prep(z, bf, name):
    def body(q_ref, k_ref, v_ref, f_ref, b_ref, qa_ref, ka_ref, va_ref, kat_ref):
        p = pl.program_id(0)
        row = lax.broadcasted_iota(jnp.int32, (S, LANE), 0)
        lane = lax.broadcasted_iota(jnp.int32, (S, LANE), 1)
        xv = f_ref[...] + b_ref[0:1, :]
        ls = jnp.minimum(xv, 0.0) - jnp.log(1.0 + jnp.exp(-jnp.abs(xv)))
        cum = _cumsum_rows(jnp.where(lane < N_HEAD, ls, 0.0), row)
        q, k, v = q_ref[...], k_ref[...], v_ref[...]
        for e in range(2):
            head = (lane >= 64) if e else (lane < 64)
            f = jnp.sum(jnp.where(lane == 2 * p + e, cum, 0.0), axis=1, keepdims=True)
            pieces = _split3(f)
            qa = jnp.where(head, q * ATT_SCALE, _put3(lane, e, 0, pieces, jnp.where(_spare3(lane, e, 3), 1.0, 0.0)))
            ones = jnp.where(_spare3(lane, e, 0) | _spare3(lane, e, 6), 1.0, 0.0)
            ka = jnp.where(head, k, _put3(lane, e, 3, [-x for x in pieces], ones))
            va = jnp.where(head, v, jnp.where(_spare3(lane, e, 0), 1.0, 0.0))
            qa_ref[e] = qa.astype(BF16)
            ka_ref[e] = ka.astype(BF16)
            va_ref[e] = va.astype(BF16)
            kat_ref[e] = ka.T.astype(BF16)

    qb, kb, vb = Z_Q // LANE, Z_K // LANE, Z_V // LANE
    heads = jax.ShapeDtypeStruct((N_HEAD, S, LANE), BF16)
    pair = pl.BlockSpec((2, S, LANE), lambda p: (p, 0, 0))
    return pl.pallas_call(
        body, name=name,
        out_shape=(heads, heads, heads, jax.ShapeDtypeStruct((N_HEAD, LANE, S), BF16)),
        grid=(N_PAIR,),
        in_specs=[pl.BlockSpec((S, LANE), lambda p: (0, qb + p)), pl.BlockSpec((S, LANE), lambda p: (0, kb + p)),
                  pl.BlockSpec((S, LANE), lambda p: (0, vb + p)), pl.BlockSpec((S, LANE), lambda p: (0, Z_F // LANE)),
                  pl.BlockSpec((8, LANE), lambda p: (0, 0))],
        out_specs=(pair, pair, pair, pl.BlockSpec((2, LANE, S), lambda p: (p, 0, 0))),
        compiler_params=_params(("parallel",)),
    )(z, z, z, z, bf)


def _attn_bwd_prep(qa, o, lse, do, name):
    def body(qa_ref, o_ref, lse_ref, do_ref, qa2_ref, doa_ref):
        lane = lax.broadcasted_iota(jnp.int32, (S, LANE), 1)
        dov, ov, lsev = do_ref[...], o_ref[...], lse_ref[...]
        for e in range(2):
            head = (lane >= 64) if e else (lane < 64)
            dsum = jnp.sum(jnp.where(head, dov * ov, 0.0), axis=1, keepdims=True)
            doa_ref[e] = jnp.where(head, dov, _put3(lane, e, 0, [-x for x in _split3(dsum)], 0.0)).astype(BF16)
            lse_col = lsev[:, 64 * e:64 * e + 1]
            qa2_ref[e] = _put3(lane, e, 6, [-x for x in _split3(lse_col)], qa_ref[e].astype(F32)).astype(BF16)

    heads = jax.ShapeDtypeStruct((N_HEAD, S, LANE), BF16)
    pair = pl.BlockSpec((2, S, LANE), lambda p: (p, 0, 0))
    cols = pl.BlockSpec((S, LANE), lambda p: (0, p))
    return pl.pallas_call(
        body, name=name, out_shape=(heads, heads), grid=(N_PAIR,),
        in_specs=[pair, cols, cols, cols], out_specs=(pair, pair),
        compiler_params=_params(("parallel",)),
    )(qa, o, lse, do)


def _attn_bwd_post(z, bf, dqt, dka, dva, name):
    def body(f_ref, b_ref, dqt_ref, dk_ref, dv_ref, dq_out, dk_out, dv_out, dfl_ref, red_ref, dcum_ref):
        p = pl.program_id(0)

        @pl.when(p == 0)
        def _():
            dcum_ref[...] = jnp.zeros_like(dcum_ref)

        row = lax.broadcasted_iota(jnp.int32, (S, LANE), 0)
        lane = lax.broadcasted_iota(jnp.int32, (S, LANE), 1)
        dqa = [dqt_ref[e].T for e in range(2)]
        dq_out[...] = (jnp.where(lane < 64, dqa[0], dqa[1]) * ATT_SCALE).astype(BF16)
        dk_out[...] = jnp.where(lane < 64, dk_ref[0], dk_ref[1]).astype(BF16)
        dv_out[...] = jnp.where(lane < 64, dv_ref[0], dv_ref[1]).astype(BF16)
        for e in range(2):
            d_query = jnp.sum(jnp.where(_spare(lane, e, 0), dqa[e], 0.0), axis=1, keepdims=True)
            d_key = jnp.sum(jnp.where(_spare(lane, e, 3), dk_ref[e], 0.0), axis=1, keepdims=True)
            dcum_ref[...] += jnp.where(lane == 2 * p + e, d_query - d_key, 0.0)

        @pl.when(p == N_PAIR - 1)
        def _():
            dls = _cumsum_rows(dcum_ref[...], row, reverse=True)
            xv = f_ref[...] + b_ref[0:1, :]
            dx = jnp.where(lane < N_HEAD, dls * jax.nn.sigmoid(-xv), 0.0)
            dfl_ref[...] = dx.astype(BF16)
            red_ref[...] = jnp.zeros_like(red_ref)
            red_ref[0:1, :] = jnp.sum(dx, axis=0, keepdims=True)

    wide = jax.ShapeDtypeStruct((S, N_PAIR * LANE), BF16)
    cols = pl.BlockSpec((S, LANE), lambda p: (0, p))
    pair = pl.BlockSpec((2, S, LANE), lambda p: (p, 0, 0))
    return pl.pallas_call(
        body, name=name,
        out_shape=(wide, wide, wide, jax.ShapeDtypeStruct((S, LANE), BF16), jax.ShapeDtypeStruct((8, LANE), F32)),
        grid=(N_PAIR,),
        in_specs=[pl.BlockSpec((S, LANE), lambda p: (0, Z_F // LANE)), pl.BlockSpec((8, LANE), lambda p: (0, 0)),
                  pl.BlockSpec((2, LANE, S), lambda p: (p, 0, 0)), pair, pair],
        out_specs=(cols, cols, cols, pl.BlockSpec((S, LANE), lambda p: (0, 0)), pl.BlockSpec((8, LANE), lambda p: (0, 0))),
        scratch_shapes=[pltpu.VMEM((S, LANE), F32)],
        compiler_params=_params(("arbitrary",)),
    )(z, bf, dqt, dka, dva)


def _attn_fwd(qa, ka, va, name):
    tq, tk = TQ_FWD, TQ
    ratio = tq // tk

    def body(qa_ref, ka_ref, va_ref, o_ref, lse_ref):
        i = pl.program_id(1)
        lane = lax.broadcasted_iota(jnp.int32, (tq, LANE), 1)
        row = lax.broadcasted_iota(jnp.int32, (tq, tk), 0)
        col = lax.broadcasted_iota(jnp.int32, (tq, tk), 1)
        nh = HEADS_PER_STEP
        qs = [qa_ref[h] for h in range(nh)]

        def block(j, carry, masked):
            off = pl.multiple_of(j * tk, tk)
            out = []
            for h in range(nh):
                m, acc = carry[h]
                s = lax.dot_general(qs[h], ka_ref[h, pl.ds(off, tk), :], _NT, preferred_element_type=F32)
                if masked:
                    s = jnp.where(col + (j - ratio * i) * tk > row, NEG_INF, s)
                mn = jnp.maximum(m, jnp.max(s, axis=1, keepdims=True))
                p = jnp.exp(s - mn).astype(BF16)
                acc = jnp.exp(m - mn) * acc + jnp.dot(p, va_ref[h, pl.ds(off, tk), :], preferred_element_type=F32)
                out.append((mn, acc))
            return tuple(out)

        init = (jnp.full((tq, 1), NEG_INF, F32), jnp.zeros((tq, LANE), F32))
        carry = lax.fori_loop(0, ratio * i, lambda j, c: block(j, c, False), (init,) * nh)
        for d in range(ratio):
            carry = block(ratio * i + d, carry, True)
        res = []
        for h in range(nh):
            m, acc = carry[h]
            l = jnp.sum(jnp.where(_spare(lane, h % 2, 0), acc, 0.0), axis=1, keepdims=True)
            res.append((acc / l, m + jnp.log(l)))
        for g in range(nh // 2):
            o_ref[:, g * LANE:(g + 1) * LANE] = jnp.where(lane < 64, res[2 * g][0], res[2 * g + 1][0])
            lse_ref[:, g * LANE:(g + 1) * LANE] = jnp.where(lane < 64, res[2 * g][1], res[2 * g + 1][1])

    nh = HEADS_PER_STEP
    out = jax.ShapeDtypeStruct((S, N_PAIR * LANE), F32)
    wide = pl.BlockSpec((tq, 64 * nh), lambda p, i: (i, p))
    return pl.pallas_call(
        body, name=name, out_shape=(out, out), grid=(N_HEAD // nh, S // tq),
        in_specs=[pl.BlockSpec((nh, tq, LANE), lambda p, i: (p, i, 0)), pl.BlockSpec((nh, S, LANE), lambda p, i: (p, 0, 0)),
                  pl.BlockSpec((nh, S, LANE), lambda p, i: (p, 0, 0))],
        out_specs=(wide, wide),
        compiler_params=_params(("parallel", "parallel")),
    )(qa, ka, va)


def _attn_bwd(qa2, ka, va, kat, doa, name):
    nq = S // TQ

    def body(qa_ref, ka_ref, va_ref, kat_ref, doa_ref, dqt_ref, dk_ref, dv_ref):
        j = pl.program_id(1)

        @pl.when(j == 0)
        def _():
            dqt_ref[...] = jnp.zeros_like(dqt_ref)

        key = lax.broadcasted_iota(jnp.int32, (TQ, TQ), 0)
        qry = lax.broadcasted_iota(jnp.int32, (TQ, TQ), 1)
        nh = HEADS_PER_STEP
        kav, vav, katv = ([ref[h] for h in range(nh)] for ref in (ka_ref, va_ref, kat_ref))

        def block(i, carry, masked):
            off = pl.multiple_of(i * TQ, TQ)
            out = []
            for h in range(nh):
                dk_acc, dv_acc = carry[h]
                qav = qa_ref[h, pl.ds(off, TQ), :]
                doav = doa_ref[h, pl.ds(off, TQ), :]
                s_t = lax.dot_general(kav[h], qav, _NT, preferred_element_type=F32)
                if masked:
                    s_t = jnp.where(key > qry, NEG_INF, s_t)
                p_t = jnp.exp(s_t)
                ds_t = p_t * lax.dot_general(vav[h], doav, _NT, preferred_element_type=F32)
                dsb = ds_t.astype(BF16)
                dv_acc = dv_acc + jnp.dot(p_t.astype(BF16), doav, preferred_element_type=F32)
                dk_acc = dk_acc + jnp.dot(dsb, qav, preferred_element_type=F32)
                dqt_ref[h, :, pl.ds(off, TQ)] += jnp.dot(katv[h], dsb, preferred_element_type=F32)
                out.append((dk_acc, dv_acc))
            return tuple(out)

        zero = (jnp.zeros((TQ, LANE), F32), jnp.zeros((TQ, LANE), F32))
        carry = block(j, (zero,) * nh, True)
        carry = lax.fori_loop(j + 1, nq, lambda i, c: block(i, c, False), carry)
        for h in range(nh):
            dk_ref[h], dv_ref[h] = carry[h]

    nh = HEADS_PER_STEP
    full = pl.BlockSpec((nh, S, LANE), lambda p, j: (p, 0, 0))
    blk = pl.BlockSpec((nh, TQ, LANE), lambda p, j: (p, j, 0))
    acc = jax.ShapeDtypeStruct((N_HEAD, S, LANE), F32)
    return pl.pallas_call(
        body, name=name,
        out_shape=(jax.ShapeDtypeStruct((N_HEAD, LANE, S), F32), acc, acc),
        grid=(N_HEAD // nh, nq),
        in_specs=[full, blk, blk, pl.BlockSpec((nh, LANE, TQ), lambda p, j: (p, 0, j)), full],
        out_specs=(pl.BlockSpec((nh, LANE, S), lambda p, j: (p, 0, 0)), blk, blk),
        compiler_params=_params(("arbitrary", "arbitrary")),
    )(qa2, ka, va, kat, doa)


ADA_ROWS = 16


def _ada_fwd(c_pad, w_ada, b_cols, name):
    def body(c_ref, w_ref, b_ref, o_ref):
        cv = c_ref[...]
        sc = (cv * jax.nn.sigmoid(cv)).astype(BF16)
        o_ref[0] = jnp.dot(sc, w_ref[0].astype(BF16), preferred_element_type=F32) + b_ref[0, 0:1, :]

    return pl.pallas_call(
        body, name=name, out_shape=jax.ShapeDtypeStruct((DEPTH, ADA_ROWS, ADA_COLS), F32), grid=(DEPTH,),
        in_specs=[pl.BlockSpec((ADA_ROWS, D), lambda l: (0, 0)), pl.BlockSpec((1, D, ADA_COLS), lambda l: (l, 0, 0)),
                  pl.BlockSpec((1, 8, ADA_COLS), lambda l: (l, 0, 0))],
        out_specs=pl.BlockSpec((1, ADA_ROWS, ADA_COLS), lambda l: (l, 0, 0)),
        compiler_params=_params(("parallel",)),
    )(c_pad, w_ada, b_cols)


def _ada_bwd(c_pad, dmod_cols, name):
    def body(c_ref, d_ref, o_ref):
        cv = c_ref[...]
        sc = (cv * jax.nn.sigmoid(cv)).astype(BF16)
        o_ref[0] = lax.dot_general(sc, d_ref[0].astype(BF16), _TN, preferred_element_type=F32)

    return pl.pallas_call(
        body, name=name, out_shape=jax.ShapeDtypeStruct((DEPTH, D, ADA_COLS), F32), grid=(DEPTH,),
        in_specs=[pl.BlockSpec((ADA_ROWS, D), lambda l: (0, 0)), pl.BlockSpec((1, ADA_ROWS, ADA_COLS), lambda l: (l, 0, 0))],
        out_specs=pl.BlockSpec((1, D, ADA_COLS), lambda l: (l, 0, 0)),
        compiler_params=_params(("parallel",)),
    )(c_pad, dmod_cols)


def _adamw_math(w, g, m, v):
    m = B1 * m + (1.0 - B1) * g
    v = B2 * v + (1.0 - B2) * (g * g)
    m_hat = m / (1.0 - B1 ** STEP)
    v_hat = v / (1.0 - B2 ** STEP)
    delta = -LR * (m_hat / (jnp.sqrt(v_hat) + EPS) + WD * w)
    return delta, m, v


def _row_tile(rows, target=256):
    best = 8
    for t in range(8, min(rows, target) + 1, 8):
        if rows % t == 0:
            best = t
    return best


def _adamw(w, g, m, v, name):
    layers, rows, cols = w.shape
    tr = _row_tile(rows)
    spec = pl.BlockSpec((1, tr, cols), lambda l, i: (l, i, 0))

    def body(w_ref, g_ref, m_ref, v_ref, d_ref, nm_ref, nv_ref):
        d_ref[...], nm_ref[...], nv_ref[...] = _adamw_math(w_ref[...], g_ref[...], m_ref[...], v_ref[...])

    out = jax.ShapeDtypeStruct(w.shape, F32)
    return pl.pallas_call(
        body, name=name, out_shape=(out, out, out), grid=(layers, rows // tr),
        in_specs=[spec] * 4, out_specs=(spec,) * 3, compiler_params=_params(("parallel", "parallel")),
    )(w, g, m, v)


def _sum_slabs(x, name):
    n, rows, _ = x.shape
    tr = _row_tile(rows)

    def body(x_ref, o_ref):
        acc = x_ref[0]
        for k in range(1, n):
            acc = acc + x_ref[k]
        o_ref[...] = acc

    return pl.pallas_call(
        body, name=name, out_shape=jax.ShapeDtypeStruct((rows, D), F32), grid=(rows // tr,),
        in_specs=[pl.BlockSpec((n, tr, D), lambda i: (0, i, 0))], out_specs=pl.BlockSpec((tr, D), lambda i: (i, 0)),
        compiler_params=_params(("parallel",)),
    )(x)


_ANY = pl.BlockSpec(memory_space=pl.ANY)
MESH = pl.DeviceIdType.MESH


def _on_sequencer(body, out_shape, sems, operands, after, sequencer_id, name):
    n = len(operands)

    def ordered_body(*refs):
        body(*refs[:n], *refs[n + 1:])

    extra = [] if after is None else [after]
    return pl.kernel(
        body if after is None else ordered_body, out_type=out_shape,
        mesh=plsc.ScalarSubcoreMesh(axis_name="sequencer", num_cores=1), scratch_types=sems,
        compiler_params=pltpu.CompilerParams(collective_id=sequencer_id), name=name)(*operands, *extra)


def _all_gather(xs, name, sequencer_id=None, after=None):
    n = len(xs)

    def body(*refs):
        x_refs, out_refs = refs[:n], refs[n:2 * n]
        send_sems, recv_sems, local_sems = refs[2 * n:]
        x_, y_, c_ = lax.axis_index("x"), lax.axis_index("y"), lax.axis_index("c")
        me, sibling = (x_, y_, c_), (x_, y_, 1 - c_)
        chips = [(1 - x_, y_), (x_, 1 - y_), (1 - x_, 1 - y_)]
        if sequencer_id is not None:
            barrier = pltpu.get_barrier_semaphore()
            peers = [sibling] + [(*chip, pc) for chip in chips for pc in (c_, 1 - c_)]
            for peer in peers:
                pl.semaphore_signal(barrier, inc=1, device_id=peer, device_id_type=MESH)
            pl.semaphore_wait(barrier, len(peers))

        def slot(a, px, py, pc):
            return out_refs[a].at[4 * px + 2 * py + pc]

        def copy(a, k, block, to, src=None):
            return pltpu.make_async_remote_copy(
                src_ref=slot(a, *block) if src is None else src, dst_ref=slot(a, *block),
                send_sem=send_sems.at[7 * a + k], recv_sem=recv_sems.at[7 * a + k], device_id=to, device_id_type=MESH)

        mine = [pltpu.make_async_copy(x_refs[a], slot(a, *me), local_sems.at[a]) for a in range(n)]
        for cp in mine:
            cp.start()
        first = []
        for a in range(n):
            first.append(copy(a, 0, me, sibling, src=x_refs[a]))
            first += [copy(a, 1 + j, me, (*chip, c_), src=x_refs[a]) for j, chip in enumerate(chips)]
        for cp in first:
            cp.start()
        passed = []
        for j, chip in enumerate(chips):
            for a in range(n):
                copy(a, 1 + j, (*chip, c_), me).wait_recv()
                passed.append(copy(a, 4 + j, (*chip, c_), sibling))
                passed[-1].start()
        for a in range(n):
            copy(a, 0, sibling, me).wait_recv()
        for j, chip in enumerate(chips):
            for a in range(n):
                copy(a, 4 + j, (*chip, 1 - c_), me).wait_recv()
        for cp in first + passed:
            cp.wait_send()
        for cp in mine:
            cp.wait()

    out_shape = [jax.ShapeDtypeStruct((N_DEV,) + x.shape, x.dtype) for x in xs]
    sems = [pltpu.SemaphoreType.DMA((7 * n,)), pltpu.SemaphoreType.DMA((7 * n,)), pltpu.SemaphoreType.DMA((n,))]
    if sequencer_id is not None:
        return _on_sequencer(body, out_shape, sems, xs, after, sequencer_id, name)
    return pl.pallas_call(
        body, name=name, out_shape=out_shape, in_specs=[_ANY] * n, out_specs=[_ANY] * n, scratch_shapes=sems)(*xs)


def _sibling_exchange(gs, name, sequencer_id=None, after=None):
    n = len(gs)

    def body(*refs):
        g_refs, p_refs = refs[:n], refs[n:2 * n]
        send_sems, recv_sems = refs[2 * n:]
        x_, y_, c_ = lax.axis_index("x"), lax.axis_index("y"), lax.axis_index("c")
        if sequencer_id is not None:
            barrier = pltpu.get_barrier_semaphore()
            pl.semaphore_signal(barrier, inc=1, device_id=(x_, y_, 1 - c_), device_id_type=MESH)
            pl.semaphore_wait(barrier, 1)
        copies = [pltpu.make_async_remote_copy(
            src_ref=g_refs[a].at[2 * k + (1 - c_)], dst_ref=p_refs[a].at[k], send_sem=send_sems.at[4 * a + k],
            recv_sem=recv_sems.at[4 * a + k], device_id=(x_, y_, 1 - c_), device_id_type=MESH)
            for a in range(n) for k in range(4)]
        for cp in copies:
            cp.start()
        for cp in copies:
            cp.wait()

    out_shape = [jax.ShapeDtypeStruct((4,) + g.shape[1:], g.dtype) for g in gs]
    sems = [pltpu.SemaphoreType.DMA((4 * n,)), pltpu.SemaphoreType.DMA((4 * n,))]
    if sequencer_id is not None:
        return _on_sequencer(body, out_shape, sems, gs, after, sequencer_id, name)
    return pl.pallas_call(
        body, name=name, out_shape=out_shape, in_specs=[_ANY] * n, out_specs=[_ANY] * n, scratch_shapes=sems)(*gs)


def _slab_tiles(rows, cols):
    if rows % 8 == 0:
        return _row_tile(rows), cols
    return rows, 2 * LANE


def _pair_sums(g, p, route, name):
    _, rows, cols = g.shape
    tr, tc = _slab_tiles(rows, cols)

    def body(route_ref, g_ref, p_ref, t_ref):
        t_ref[...] = (g_ref[...] + p_ref[...]).astype(BF16)

    return pl.pallas_call(
        body, name=name, out_shape=jax.ShapeDtypeStruct((3, rows, cols), BF16),
        grid_spec=pltpu.PrefetchScalarGridSpec(
            num_scalar_prefetch=1, grid=(3, rows // tr, cols // tc),
            in_specs=[pl.BlockSpec((1, tr, tc), lambda r, i, j, route_ref: (2 * route_ref[1 + r] + route_ref[0], i, j)),
                      pl.BlockSpec((1, tr, tc), lambda r, i, j, route_ref: (route_ref[1 + r], i, j))],
            out_specs=pl.BlockSpec((1, tr, tc), lambda r, i, j, route_ref: (r, i, j))),
        compiler_params=_params(("parallel", "parallel", "parallel")),
    )(route, g, p)


def _chip_exchange(ts, name, sequencer_id=None, after=None):
    n = len(ts)

    def body(*refs):
        t_refs, l_refs = refs[:n], refs[n:2 * n]
        send_sems, recv_sems = refs[2 * n:]
        x_, y_, c_ = lax.axis_index("x"), lax.axis_index("y"), lax.axis_index("c")
        chips = [(1 - x_, y_), (x_, 1 - y_), (1 - x_, 1 - y_)]
        if sequencer_id is not None:
            barrier = pltpu.get_barrier_semaphore()
            for px, py in chips:
                pl.semaphore_signal(barrier, inc=1, device_id=(px, py, c_), device_id_type=MESH)
            pl.semaphore_wait(barrier, len(chips))
        copies = [pltpu.make_async_remote_copy(
            src_ref=t_refs[a].at[r], dst_ref=l_refs[a].at[r], send_sem=send_sems.at[3 * a + r],
            recv_sem=recv_sems.at[3 * a + r], device_id=(px, py, c_), device_id_type=MESH)
            for a in range(n) for r, (px, py) in enumerate(chips)]
        for cp in copies:
            cp.start()
        for cp in copies:
            cp.wait()

    out_shape = [jax.ShapeDtypeStruct((3,) + t.shape[1:], t.dtype) for t in ts]
    sems = [pltpu.SemaphoreType.DMA((3 * n,)), pltpu.SemaphoreType.DMA((3 * n,))]
    if sequencer_id is not None:
        return _on_sequencer(body, out_shape, sems, ts, after, sequencer_id, name)
    return pl.pallas_call(
        body, name=name, out_shape=out_shape, in_specs=[_ANY] * n, out_specs=[_ANY] * n, scratch_shapes=sems)(*ts)


def _reduce_adamw(gs, ps, landed, place, w, m, v, name):
    layers, rows, cols = w.shape
    assert layers == DEPTH == 2
    tr, tc = _slab_tiles(rows, cols)
    nr, nc = rows // tr, cols // tc
    spec = pl.BlockSpec((1, tr, tc), lambda l, i, j, place_ref: (l, i, j))

    def own(layer, which):
        pi, pj = (nr - 1, nc - 1) if layer == 0 else (0, 0)

        def index(l, i, j, place_ref):
            lead = 0 if which is None else place_ref[which]
            return lead, jnp.where(l == layer, i, pi), jnp.where(l == layer, j, pj)

        return pl.BlockSpec((3 if which is None else 1, tr, tc), index)

    def body(place_ref, g0_ref, p0_ref, l0_ref, g1_ref, p1_ref, l1_ref, w_ref, m_ref, v_ref,
             g_ref, d_ref, nm_ref, nv_ref):
        def update(own_ref, sib_ref, l_ref):
            g = own_ref[0] + sib_ref[0] + l_ref[0].astype(F32) + l_ref[1].astype(F32) + l_ref[2].astype(F32)
            g_ref[0] = g
            d_ref[0], nm_ref[0], nv_ref[0] = _adamw_math(w_ref[0], g, m_ref[0], v_ref[0])

        @pl.when(pl.program_id(0) == 0)
        def _():
            update(g0_ref, p0_ref, l0_ref)

        @pl.when(pl.program_id(0) == 1)
        def _():
            update(g1_ref, p1_ref, l1_ref)

    out = jax.ShapeDtypeStruct(w.shape, F32)
    return pl.pallas_call(
        body, name=name, out_shape=(out, out, out, out),
        grid_spec=pltpu.PrefetchScalarGridSpec(
            num_scalar_prefetch=1, grid=(DEPTH, nr, nc),
            in_specs=[own(0, 0), own(0, 1), own(0, None), own(1, 0), own(1, 1), own(1, None), spec, spec, spec],
            out_specs=(spec, spec, spec, spec)),
        compiler_params=_params(("arbitrary", "arbitrary", "arbitrary")),
    )(place, gs[0], ps[0], landed[0], gs[1], ps[1], landed[1], w, m, v)


def _pack(pieces, row_multiple, dtype, cols=D, rows=None):
    flat = jnp.concatenate([p.astype(dtype).reshape(-1) for p in pieces])
    if rows is None:
        rows = -(-flat.shape[0] // cols)
        rows = -(-rows // row_multiple) * row_multiple
    flat = jnp.pad(flat, (0, rows * cols - flat.shape[0]))
    return flat.reshape(rows, cols)


def _unpack(flat, shapes, lead=()):
    out, off = [], 0
    for shp in shapes:
        n = 1
        for s_ in shp:
            n *= s_
        out.append(lax.slice_in_dim(flat, off, off + n, axis=len(lead)).reshape(lead + tuple(shp)))
        off += n
    return out


def _z_rows_from_in(wt):
    pad = jnp.zeros((NZ - IN_COLS, wt.shape[1]), wt.dtype)
    return jnp.concatenate([wt[1544:2568], wt[2568:5640], wt[0:1536], wt[1536:1544], pad], axis=0)


def _in_rows_from_z(wt):
    return jnp.concatenate([wt[Z_Q:Z_Q + 1536], wt[Z_F:Z_F + 8], wt[Z_PC:Z_PC + 1024], wt[Z_G:Z_G + 3072]], axis=0)


def _pad_rows(v, rows=8):
    return jnp.pad(v, ((0, rows - v.shape[0]), (0, 0)))


def _layer_fwd(l, x, wts, gvec, mod):
    tag = f"l{l}"
    h = _prenorm_fwd(x, gvec, mod, 0, 0, 1, f"prenorm_mix_{tag}")
    z = _matmul(h, wts["w_in_t"], "nt", f"in_proj_{tag}", tn=1152)
    qa, ka, va, kat = _attn_prep(z, wts["b_f"], f"attn_prep_{tag}")
    o, lse = _attn_fwd(qa, ka, va, f"attn_{tag}")
    br_b = _pool_fwd(z, wts["wp_bd"], wts["pool_scale"], f"pool_{tag}")
    br_c = _conv_fwd(z, wts["conv_w"], f"conv_{tag}")
    pa = _matmul(o, wts["wa"], "nn", f"proj_a_{tag}")
    pb = _matmul(br_b, wts["wb"], "nn", f"proj_b_{tag}")
    gates = [(z, Z_G + k * D) for k in range(3)]
    pc, merged = _matmul(br_c, wts["wc"], "nn", f"proj_c_merge_{tag}", tm=512, tn=512,
                         extra=gates + [(pa, 0), (pb, 0)], epilogue=_merge_epilogue, out_dtypes=(F32, BF16))
    y = _matmul(merged, wts["w_out"], "nn", f"out_proj_{tag}")
    x1 = _postnorm_fwd(x, y, gvec, mod, 1, 2, f"postnorm_mix_{tag}")
    h2 = _prenorm_fwd(x1, gvec, mod, 2, 3, 4, f"prenorm_ff_{tag}")
    a, r = _matmul(h2, wts["w_ff1"], "nn", f"ff1_{tag}", b_col_shards=True, epilogue=_relu2_epilogue,
                   out_dtypes=(F32, BF16))
    y2 = _matmul(r, wts["w_ff2"], "nn", f"ff2_{tag}")
    x2 = _postnorm_fwd(x1, y2, gvec, mod, 3, 5, f"postnorm_ff_{tag}")
    saved = dict(x=x, h=h, z=z, qa=qa, ka=ka, va=va, kat=kat, o=o, lse=lse, br_b=br_b, br_c=br_c, pa=pa, pb=pb, pc=pc,
                 merged=merged, y=y, x1=x1, h2=h2, a=a, r=r, y2=y2)
    return x2, saved


def _ffn_bwd(l, dx2, sv, wts, gvec, mod, midpoint):
    tag = f"l{l}"
    dy2, red_post_ff = _postnorm_bwd(sv["y2"], gvec, mod, dx2, 3, 5, f"postnorm_ff_bwd_{tag}")
    dy2 = midpoint(dy2)
    da = _matmul(dy2, wts["w_ff2"], "nt", f"ff2_dx_{tag}", extra=[(sv["a"], 0)], epilogue=_relu2_bwd_epilogue,
                 out_dtypes=(BF16,))[0]
    d_w_ff2 = _matmul(sv["r"], dy2, "tn", f"ff2_dw_{tag}")
    dh2 = _matmul(da, wts["w_ff1"], "nt", f"ff1_dx_{tag}", b_col_shards=True)
    d_w_ff1 = _matmul(sv["h2"], da, "tn", f"ff1_dw_{tag}", out_col_shards=True)
    dx1, red_pre_ff = _prenorm_bwd(sv["x1"], gvec, mod, dh2, dx2, 2, 4, f"prenorm_ff_bwd_{tag}")
    return dx1, [d_w_ff1, d_w_ff2.reshape(N_DEV, D_FF // N_DEV, D)], (red_pre_ff, red_post_ff)


def _mixer_bwd(l, dx1, sv, wts, gvec, mod, ffn_reds, midpoint):
    tag = f"l{l}"
    red_pre_ff, red_post_ff = ffn_reds
    dy, red_post_mix = _postnorm_bwd(sv["y"], gvec, mod, dx1, 1, 2, f"postnorm_mix_bwd_{tag}")
    gates = [(sv["z"], Z_G + k * D) for k in range(3)]
    dpa, dpb, dpc, *dgl = _matmul(dy, wts["w_out"], "nt", f"out_proj_dx_{tag}", tm=512, tn=512,
                                  extra=gates + [(sv["pa"], 0), (sv["pb"], 0), (sv["pc"], 0)],
                                  epilogue=_merge_bwd_epilogue, out_dtypes=(BF16,) * 6)
    d_w_out = _matmul(sv["merged"], dy, "tn", f"out_proj_dw_{tag}")
    dpa = midpoint(dpa)
    do = _matmul(dpa, wts["wa"], "nt", f"proj_a_dx_{tag}")
    dbr_b = _matmul(dpb, wts["wb"], "nt", f"proj_b_dx_{tag}")
    dbr_c = _matmul(dpc, wts["wc"], "nt", f"proj_c_dx_{tag}")
    d_wa = _matmul(sv["o"], dpa, "tn", f"proj_a_dw_{tag}")
    d_wb = _matmul(sv["br_b"], dpb, "tn", f"proj_b_dw_{tag}")
    d_wc = _matmul(sv["br_c"], dpc, "tn", f"proj_c_dw_{tag}")
    d_w_branch = jnp.concatenate([d_wa, d_wb, d_wc], axis=0)

    dpu, d_wp_bd, red_pool = _pool_bwd(sv["z"], wts["wp_bd"], wts["pool_scale"], dbr_b, f"pool_bwd_{tag}")
    dconv, red_conv = _conv_bwd(sv["z"], wts["conv_w"], dbr_c, f"conv_bwd_{tag}")
    qa2, doa = _attn_bwd_prep(sv["qa"], sv["o"], sv["lse"], do, f"attn_bwd_prep_{tag}")
    dqt, dka, dva = _attn_bwd(qa2, sv["ka"], sv["va"], sv["kat"], doa, f"attn_bwd_{tag}")
    dq, dk, dv, dfl, red_f = _attn_bwd_post(sv["z"], wts["b_f"], dqt, dka, dva, f"attn_bwd_post_{tag}")
    dz = jnp.concatenate([dpu, dconv, *dgl, dq, dk, dv, dfl], axis=1)
    dh = _matmul(dz, wts["w_in_t"], "nn", f"in_proj_dx_{tag}", tk=1152)
    d_w_in_t = _matmul(dz, sv["h"], "tn", f"in_proj_dw_{tag}", tm=1152)
    dx0, red_pre_mix = _prenorm_bwd(sv["x"], gvec, mod, dh, dx1, 0, 1, f"prenorm_mix_bwd_{tag}")

    rows = D // N_DEV
    big = [_in_rows_from_z(d_w_in_t).reshape(N_DEV, IN_SHARD, D), d_w_branch.reshape(N_DEV, rows, D),
           d_w_out.reshape(N_DEV, rows, D)]
    d_w_pool = jnp.stack([d_wp_bd[64 * g:64 * (g + 1), 64 * g:64 * (g + 1)] for g in range(4)])
    small = dict(
        mod=jnp.stack([red_pre_mix[0], red_pre_mix[1], red_post_mix[0], red_pre_ff[0], red_pre_ff[1], red_post_ff[0]]),
        g_mix_pre=red_pre_mix[2], g_mix_post=red_post_mix[1], g_ff_pre=red_pre_ff[2], g_ff_post=red_post_ff[1],
        b_f=red_f[0, 0:8], w_pool=d_w_pool, pool_scale=red_pool[0], conv_w=red_conv[0:3])
    return dx0, big, small


SMALL_KEYS = ["mod", "g_mix_pre", "g_mix_post", "g_ff_pre", "g_ff_post", "b_f", "w_pool", "pool_scale", "conv_w"]
SMALL_SHAPES = [(DEPTH, 6 * D), (DEPTH, D), (DEPTH, D), (DEPTH, D), (DEPTH, D), (DEPTH, 8), (DEPTH, 4, 64, 64),
                (DEPTH, POOL_W), (DEPTH, 3, CONV_W)]


def kernel(x, c, w_ada, b_ada, g_mix_pre, g_mix_post, g_ff_pre, g_ff_post, w_in, b_f, w_pool, pool_scale, conv_w, w_branch, w_out, w_ff1, w_ff2, loss_target, m_w_ada, m_b_ada, m_g_mix_pre, m_g_mix_post, m_g_ff_pre, m_g_ff_post, m_w_in, m_b_f, m_w_pool, m_pool_scale, m_conv_w, m_w_branch, m_w_out, m_w_ff1, m_w_ff2, v_w_ada, v_b_ada, v_g_mix_pre, v_g_mix_post, v_g_ff_pre, v_g_ff_post, v_w_in, v_b_f, v_w_pool, v_pool_scale, v_conv_w, v_w_branch, v_w_out, v_w_ff1, v_w_ff2):
    ix, iy, ic = lax.axis_index("x"), lax.axis_index("y"), lax.axis_index("c")
    me = 4 * ix + 2 * iy + ic
    route = jnp.stack([ic, 2 * (1 - ix) + iy, 2 * ix + (1 - iy), 2 * (1 - ix) + (1 - iy)]).astype(jnp.int32)
    place = jnp.stack([me, 2 * ix + iy]).astype(jnp.int32)
    wt_in, mt_in, vt_in = (jnp.transpose(a, (0, 2, 1)) for a in (w_in, m_w_in, v_w_in))

    c_all = _all_gather([_pad_rows(c)], "gather_c")[0][:, 0, :]
    c_pad = _pad_rows(c_all, ADA_ROWS)
    b_cols = lax.dynamic_slice_in_dim(b_ada, me * ADA_COLS, ADA_COLS, axis=1)
    b_cols = jnp.broadcast_to(b_cols[:, None, :], (DEPTH, 8, ADA_COLS))
    mod_part = _ada_fwd(c_pad, w_ada, b_cols, "ada_fwd")
    mod_all = _all_gather([mod_part.reshape(DEPTH * ADA_ROWS, ADA_COLS)], "gather_mod")[0]
    mod_all = mod_all.reshape(N_DEV, DEPTH, ADA_ROWS, ADA_COLS)
    mod_mine = lax.dynamic_index_in_dim(mod_all, me, axis=2, keepdims=False)
    mod_mine = jnp.transpose(mod_mine, (1, 0, 2)).reshape(DEPTH, 6, D)

    cw_cols = CONV_W // N_DEV
    cw_send = jnp.pad(conv_w.reshape(DEPTH * 3, cw_cols), ((0, 8 - DEPTH * 3), (0, LANE - cw_cols)))
    send = [[w[l].astype(BF16) for w in (wt_in, w_branch, w_out, w_ff1, w_ff2)] for l in range(DEPTH)]
    first = _all_gather(send[0][:1], "gather_weights_l0_in", sequencer_id=1, after=mod_all)
    rest = _all_gather(send[0][1:] + [cw_send], "gather_weights_l0_rest", sequencer_id=2, after=first[0])
    gathered = [first + rest[:4], _all_gather(send[1], "gather_weights_l1", sequencer_id=3, after=first[0])]
    cw_all = rest[4][:, :DEPTH * 3, :cw_cols].reshape(N_DEV, DEPTH, 3, cw_cols)

    def layer_operands(l, weights):
        p_in, p_br, p_out, p_ff1, p_ff2 = weights[:5]
        w_br_full = p_br.reshape(D, D)
        cw_full = jnp.transpose(cw_all[:, l], (1, 0, 2)).reshape(3, CONV_W)
        wp_bd = jnp.zeros((POOL_W, POOL_W), F32)
        for g in range(4):
            wp_bd = wp_bd.at[64 * g:64 * (g + 1), 64 * g:64 * (g + 1)].set(w_pool[l, g])
        wts = dict(
            w_in_t=_z_rows_from_in(p_in.reshape(IN_COLS, D)), wa=w_br_full[0:A_WIDTH], wb=w_br_full[A_WIDTH:A_WIDTH + POOL_W],
            wc=w_br_full[A_WIDTH + POOL_W:], w_out=p_out.reshape(D, D),
            w_ff1=p_ff1, w_ff2=p_ff2.reshape(D_FF, D),
            conv_w=_pad_rows(cw_full), wp_bd=wp_bd.astype(BF16), pool_scale=_pad_rows(pool_scale[l][None, :]),
            b_f=_pad_rows(jnp.pad(b_f[l], (0, LANE - 8))[None, :]))
        gvec = _pad_rows(jnp.stack([g_mix_pre[l], g_mix_post[l], g_ff_pre[l], g_ff_post[l]]))
        return wts, gvec, _pad_rows(mod_mine[l])

    xs = x[0]
    saved, layers = [], []
    for l in range(DEPTH):
        weights = gathered[l]
        if l > 0:
            xs, weights = lax.optimization_barrier((xs, weights))
        layers.append(layer_operands(l, weights))
        xs, sv = _layer_fwd(l, xs, *layers[l])
        saved.append(sv)
    dx, loss_part = _loss_head(xs, loss_target[0], "loss_head")
    loss = lax.psum(loss_part[0, 0], ("x", "y", "c"))
    small_grads = [None] * DEPTH
    mine, sibs, landed = ({} for _ in range(3))
    seq_id = iter(range(4, 4 + 4 * DEPTH))
    last = [gathered[DEPTH - 1][0]]

    def start(group, grads):
        mine[group] = grads
        sibs[group] = _sibling_exchange(grads, f"rs_sibling_{group}", sequencer_id=next(seq_id), after=last[0])
        last[0] = sibs[group][0]

    def finish(group, later):
        later, (grads, sib) = lax.optimization_barrier((later, (mine[group], sibs[group])))
        sends = [_pair_sums(g, p, route, f"rs_pair_sums_{group}_{k}") for k, (g, p) in enumerate(zip(grads, sib))]
        later, sends = lax.optimization_barrier((later, sends))
        landed[group] = _chip_exchange(sends, f"rs_chips_{group}", sequencer_id=next(seq_id), after=last[0])
        last[0] = landed[group][0]
        return later

    pending = None
    for l in reversed(range(DEPTH)):
        hook = (lambda da: da) if pending is None else functools.partial(finish, pending)
        dx, ffn_grads, ffn_reds = _ffn_bwd(l, dx, saved[l], *layers[l], hook)
        start(f"ffn_l{l}", ffn_grads)
        dx, mix_grads, small_grads[l] = _mixer_bwd(l, dx, saved[l], *layers[l], ffn_reds,
                                                   functools.partial(finish, f"ffn_l{l}"))
        start(f"mix_l{l}", mix_grads)
        pending = f"mix_l{l}"
    grad_x = dx[None]

    big_w = [wt_in, w_branch, w_out, w_ff1, w_ff2]
    big_m = [mt_in, m_w_branch, m_w_out, m_w_ff1, m_w_ff2]
    big_v = [vt_in, v_w_branch, v_w_out, v_w_ff1, v_w_ff2]
    where = [("mix", 0), ("mix", 1), ("mix", 2), ("ffn", 0), ("ffn", 1)]

    def reduce_and_update(k):
        group, at = where[k]
        return _reduce_adamw([mine[f"{group}_l{l}"][at] for l in range(DEPTH)],
                             [sibs[f"{group}_l{l}"][at] for l in range(DEPTH)],
                             [landed[f"{group}_l{l}"][at] for l in range(DEPTH)], place, big_w[k], big_m[k], big_v[k],
                             f"rs_sum_adamw_{k}")

    big_res = {k: list(reduce_and_update(k)) for k in (3, 4)}
    big_res[3][0] = finish(pending, big_res[3][0])

    small = {k: jnp.stack([small_grads[l][k] for l in range(DEPTH)]) for k in SMALL_KEYS}
    small_all = _all_gather([_pack([small[k] for k in SMALL_KEYS], 8, F32)], "gather_small")[0]
    dmod_all = small_all[:, 0:DEPTH * 6, :].reshape(N_DEV, DEPTH, 6 * D)
    summed = _unpack(_sum_slabs(small_all, "sum_small").reshape(-1), SMALL_SHAPES)
    sg = dict(zip(SMALL_KEYS, summed))
    dmod_cols = lax.dynamic_slice_in_dim(dmod_all, me * ADA_COLS, ADA_COLS, axis=2)
    dmod_cols = jnp.pad(jnp.transpose(dmod_cols, (1, 0, 2)), ((0, 0), (0, ADA_ROWS - N_DEV), (0, 0)))
    g_w_ada = _ada_bwd(c_pad, dmod_cols, "ada_bwd")
    g_conv_w = lax.dynamic_slice_in_dim(sg["conv_w"], me * (CONV_W // N_DEV), CONV_W // N_DEV, axis=2)

    ada_out = [g_w_ada] + list(_adamw(w_ada, g_w_ada, m_w_ada, v_w_ada, "adamw_ada"))
    rest_w = [b_ada, g_mix_pre, g_mix_post, g_ff_pre, g_ff_post, b_f, w_pool, pool_scale, conv_w]
    rest_m = [m_b_ada, m_g_mix_pre, m_g_mix_post, m_g_ff_pre, m_g_ff_post, m_b_f, m_w_pool, m_pool_scale, m_conv_w]
    rest_v = [v_b_ada, v_g_mix_pre, v_g_mix_post, v_g_ff_pre, v_g_ff_post, v_b_f, v_w_pool, v_pool_scale, v_conv_w]
    rest_g = [sg["mod"], sg["g_mix_pre"], sg["g_mix_post"], sg["g_ff_pre"], sg["g_ff_post"], sg["b_f"],
              sg["w_pool"], sg["pool_scale"], g_conv_w]
    rest_shapes = [a.shape for a in rest_w]
    upd = _adamw(_pack(rest_w, 8, F32)[None], _pack(rest_g, 8, F32)[None], _pack(rest_m, 8, F32)[None],
                 _pack(rest_v, 8, F32)[None], "adamw_rest")
    rest_out = [rest_g] + [_unpack(arr.reshape(-1), rest_shapes) for arr in upd]
    rest_out = [[ada_out[which]] + rest_out[which] for which in range(4)]

    landed[pending], rest_out = lax.optimization_barrier((landed[pending], rest_out))
    big_res.update({k: reduce_and_update(k) for k in (0, 1, 2)})
    big_out = [[jnp.transpose(big_res[k][which], (0, 2, 1)) if k == 0 else big_res[k][which] for k in range(5)]
               for which in range(4)]

    def ordered(k):
        r, b = rest_out[k], big_out[k]
        return [r[0], r[1], r[2], r[3], r[4], r[5], b[0], r[6], r[7], r[8], r[9], b[1], b[2], b[3], b[4]]

    return (loss, grad_x, *ordered(0), *ordered(1), *ordered(2), *ordered(3))
```

```python
import functools

import jax
import jax.numpy as jnp
from jax import lax
from jax.experimental import pallas as pl
from jax.experimental.pallas import tpu as pltpu
from jax.experimental.pallas import tpu_sc as plsc

F32 = jnp.float32
BF16 = jnp.bfloat16

N_DEV = 8
D = 1024
S = 2048
DEPTH = 2
D_FF = 4 * D
A_WIDTH = 512
HEAD_DIM = 64
N_PAIR = 4
POOL_W = 256
CONV_W = 256
IN_COLS = 5640
ADA_COLS = 6 * D // N_DEV
IN_SHARD = IN_COLS // N_DEV
RMS_EPS = 1e-6
NEG_INF = -1e30
ATT_SCALE = HEAD_DIM ** -0.5

NZ = 5760
Z_PC = 0
Z_G = 1024
Z_Q = 4096
Z_K = 4608
Z_V = 5120
Z_F = 5632

LR, B1, B2, EPS, WD, STEP = 0.001, 0.9, 0.999, 1e-08, 0.01, 10

LANE = 128
VMEM_LIMIT_BYTES = 48 * 1024 * 1024
TS = 512
TQ = 256
TQ_FWD = 512
HEADS_PER_STEP = 4


def _params(sem=None):
    return pltpu.CompilerParams(dimension_semantics=sem, vmem_limit_bytes=VMEM_LIMIT_BYTES)


def _pick(n, target):
    best = None
    for t in range(LANE, min(n, target) + 1, LANE):
        if n % t == 0:
            best = t
    return n if best is None else best


def _matmul(a, b, mode, name, out_dtype=F32, tm=1024, tn=1024, tk=1024, b_col_shards=False, out_col_shards=False,
            extra=(), epilogue=None, out_dtypes=None):
    if b_col_shards:
        shards, b_rows, shard_cols = b.shape
        b_shape = (b_rows, shards * shard_cols)
    else:
        b_shape = b.shape
    if mode == "nn":
        (m, k), (k2, n) = a.shape, b_shape
    elif mode == "nt":
        (m, k), (n, k2) = a.shape, b_shape
    else:
        (k, m), (k2, n) = a.shape, b_shape
    assert k == k2, (a.shape, b.shape, mode)
    tm, tn, tk = _pick(m, tm), _pick(n, tn), _pick(k, tk)
    if b_col_shards and mode == "nn":
        tn = shard_cols
    if b_col_shards and mode == "nt":
        tk = shard_cols
    if out_col_shards:
        tn = n // N_DEV
    nk = k // tk
    if mode == "nn":
        a_spec = pl.BlockSpec((tm, tk), lambda i, j, kk: (i, kk))
        b_spec = (pl.BlockSpec((None, tk, tn), lambda i, j, kk: (j, kk, 0)) if b_col_shards else
                  pl.BlockSpec((tk, tn), lambda i, j, kk: (kk, j)))
        dims = (((1,), (0,)), ((), ()))
    elif mode == "nt":
        a_spec = pl.BlockSpec((tm, tk), lambda i, j, kk: (i, kk))
        b_spec = (pl.BlockSpec((None, tn, tk), lambda i, j, kk: (kk, j, 0)) if b_col_shards else
                  pl.BlockSpec((tn, tk), lambda i, j, kk: (j, kk)))
        dims = (((1,), (1,)), ((), ()))
    else:
        assert not b_col_shards
        a_spec = pl.BlockSpec((tk, tm), lambda i, j, kk: (kk, i))
        b_spec = pl.BlockSpec((tk, tn), lambda i, j, kk: (kk, j))
        dims = (((0,), (0,)), ((), ()))
    if out_col_shards:
        out_shape = jax.ShapeDtypeStruct((N_DEV, m, tn), out_dtype)
        out_spec = pl.BlockSpec((None, tm, tn), lambda i, j, kk: (j, i, 0))
    else:
        out_shape = jax.ShapeDtypeStruct((m, n), out_dtype)
        out_spec = pl.BlockSpec((tm, tn), lambda i, j, kk: (i, j))

    n_extra = len(extra)
    extra_specs = [pl.BlockSpec((tm, tn), lambda i, j, kk, off=off: (i, j + off // tn)) for _, off in extra]
    if epilogue is not None:
        assert not out_col_shards and all(off % tn == 0 for _, off in extra)
        out_shape = [jax.ShapeDtypeStruct((m, n), dt) for dt in out_dtypes]
        out_spec = [pl.BlockSpec((tm, tn), lambda i, j, kk: (i, j)) for _ in out_dtypes]

    def product(a_ref, b_ref):
        return lax.dot_general(a_ref[...].astype(BF16), b_ref[...].astype(BF16), dims, preferred_element_type=F32)

    def write(acc, extra_refs, o_refs):
        if epilogue is None:
            o_refs[0][...] = acc.astype(out_dtype)
        else:
            for o_ref, tile in zip(o_refs, epilogue(acc, *[r[...] for r in extra_refs])):
                o_ref[...] = tile.astype(o_ref.dtype)

    def body_one_pass(a_ref, b_ref, *refs):
        write(product(a_ref, b_ref), refs[:n_extra], refs[n_extra:])

    def body(a_ref, b_ref, *refs):
        acc_ref = refs[-1]
        kk = pl.program_id(2)

        @pl.when(kk == 0)
        def _():
            acc_ref[...] = product(a_ref, b_ref)

        @pl.when(kk > 0)
        def _():
            acc_ref[...] += product(a_ref, b_ref)

        @pl.when(kk == nk - 1)
        def _():
            write(acc_ref[...], refs[:n_extra], refs[n_extra:-1])

    return pl.pallas_call(
        body_one_pass if nk == 1 else body, name=name,
        out_shape=out_shape,
        grid=(m // tm, n // tn, nk),
        in_specs=[a_spec, b_spec] + extra_specs,
        out_specs=out_spec,
        scratch_shapes=[] if nk == 1 else [pltpu.VMEM((tm, tn), F32)],
        compiler_params=_params(("parallel", "parallel", "arbitrary")),
    )(a, b, *[x for x, _ in extra])


def _row_spec(width=D, col=0):
    return pl.BlockSpec((TS, width), lambda i: (i, col))


def _vec_spec(rows=8, width=D):
    return pl.BlockSpec((rows, width), lambda i: (0, 0))


def _rms(x):
    return lax.rsqrt(jnp.mean(x * x, axis=-1, keepdims=True) + RMS_EPS)


def _prenorm_fwd(x, gvec, mod, g_row, shift_row, scale_row, name):
    def body(x_ref, g_ref, mod_ref, h_ref):
        xv = x_ref[...]
        y = xv * _rms(xv) * g_ref[g_row:g_row + 1, :]
        h = y * (1.0 + mod_ref[scale_row:scale_row + 1, :]) + mod_ref[shift_row:shift_row + 1, :]
        h_ref[...] = h.astype(BF16)

    return pl.pallas_call(
        body, name=name, out_shape=jax.ShapeDtypeStruct((S, D), BF16), grid=(S // TS,),
        in_specs=[_row_spec(), _vec_spec(), _vec_spec()], out_specs=_row_spec(),
        compiler_params=_params(("parallel",)),
    )(x, gvec, mod)


def _prenorm_bwd(x, gvec, mod, dh, dres, g_row, scale_row, name):
    def body(x_ref, g_ref, mod_ref, dh_ref, dres_ref, dx_ref, red_ref):
        i = pl.program_id(0)

        @pl.when(i == 0)
        def _():
            red_ref[...] = jnp.zeros_like(red_ref)

        xv = x_ref[...]
        g = g_ref[g_row:g_row + 1, :]
        r = _rms(xv)
        n = xv * r
        yg = n * g
        dhv = dh_ref[...]
        dyg = dhv * (1.0 + mod_ref[scale_row:scale_row + 1, :])
        dn = dyg * g
        dx = r * (dn - n * jnp.mean(dn * n, axis=-1, keepdims=True))
        dx_ref[...] = dres_ref[...] + dx
        red_ref[0:1, :] += jnp.sum(dhv, axis=0, keepdims=True)
        red_ref[1:2, :] += jnp.sum(dhv * yg, axis=0, keepdims=True)
        red_ref[2:3, :] += jnp.sum(dyg * n, axis=0, keepdims=True)

    return pl.pallas_call(
        body, name=name,
        out_shape=(jax.ShapeDtypeStruct((S, D), F32), jax.ShapeDtypeStruct((8, D), F32)),
        grid=(S // TS,),
        in_specs=[_row_spec(), _vec_spec(), _vec_spec(), _row_spec(), _row_spec()],
        out_specs=(_row_spec(), _vec_spec()),
        compiler_params=_params(("arbitrary",)),
    )(x, gvec, mod, dh, dres)


def _postnorm_fwd(x, y, gvec, mod, g_row, gate_row, name):
    def body(x_ref, y_ref, g_ref, mod_ref, o_ref):
        yv = y_ref[...]
        yn = yv * _rms(yv) * g_ref[g_row:g_row + 1, :]
        o_ref[...] = x_ref[...] + mod_ref[gate_row:gate_row + 1, :] * yn

    return pl.pallas_call(
        body, name=name, out_shape=jax.ShapeDtypeStruct((S, D), F32), grid=(S // TS,),
        in_specs=[_row_spec(), _row_spec(), _vec_spec(), _vec_spec()], out_specs=_row_spec(),
        compiler_params=_params(("parallel",)),
    )(x, y, gvec, mod)


def _postnorm_bwd(y, gvec, mod, dxo, g_row, gate_row, name):
    def body(y_ref, g_ref, mod_ref, dxo_ref, dy_ref, red_ref):
        i = pl.program_id(0)

        @pl.when(i == 0)
        def _():
            red_ref[...] = jnp.zeros_like(red_ref)

        yv = y_ref[...]
        g = g_ref[g_row:g_row + 1, :]
        r = _rms(yv)
        n = yv * r
        dxo = dxo_ref[...]
        dyn = dxo * mod_ref[gate_row:gate_row + 1, :]
        dn = dyn * g
        dy = r * (dn - n * jnp.mean(dn * n, axis=-1, keepdims=True))
        dy_ref[...] = dy.astype(BF16)
        red_ref[0:1, :] += jnp.sum(dxo * (n * g), axis=0, keepdims=True)
        red_ref[1:2, :] += jnp.sum(dyn * n, axis=0, keepdims=True)

    return pl.pallas_call(
        body, name=name,
        out_shape=(jax.ShapeDtypeStruct((S, D), BF16), jax.ShapeDtypeStruct((8, D), F32)),
        grid=(S // TS,),
        in_specs=[_row_spec(), _vec_spec(), _vec_spec(), _row_spec()],
        out_specs=(_row_spec(), _vec_spec()),
        compiler_params=_params(("arbitrary",)),
    )(y, gvec, mod, dxo)


def _loss_head(xf, target, name):
    def body(x_ref, t_ref, dx_ref, loss_ref):
        i = pl.program_id(0)

        @pl.when(i == 0)
        def _():
            loss_ref[...] = jnp.zeros_like(loss_ref)

        e = x_ref[...] - t_ref[...]
        dx_ref[...] = e / float(D)
        per_tok = jnp.mean(e * e, axis=-1, keepdims=True)
        loss_ref[0:1, 0:1] += 0.5 * jnp.sum(per_tok, axis=0, keepdims=True)

    return pl.pallas_call(
        body, name=name,
        out_shape=(jax.ShapeDtypeStruct((S, D), F32), jax.ShapeDtypeStruct((8, LANE), F32)),
        grid=(S // TS,),
        in_specs=[_row_spec(), _row_spec()],
        out_specs=(_row_spec(), pl.BlockSpec((8, LANE), lambda i: (0, 0))),
        compiler_params=_params(("arbitrary",)),
    )(xf, target)


def _relu2_epilogue(a):
    t = jnp.maximum(a, 0.0)
    return a, t * t


def _relu2_bwd_epilogue(dr, a):
    return (dr * (2.0 * jnp.maximum(a, 0.0)),)


def _merge_epilogue(pc, g0, g1, g2, pa, pb):
    return pc, jax.nn.sigmoid(g0) * pa + jax.nn.sigmoid(g1) * pb + jax.nn.sigmoid(g2) * pc


def _merge_bwd_epilogue(dm, g0, g1, g2, pa, pb, pc):
    sg = [jax.nn.sigmoid(g) for g in (g0, g1, g2)]
    return tuple(dm * s for s in sg) + tuple(dm * p * (s * (1.0 - s)) for p, s in zip((pa, pb, pc), sg))


def _shift_down(x, k, row):
    return jnp.where(row >= k, pltpu.roll(x, k, axis=0), 0.0)


def _shift_up(x, k, row):
    n = x.shape[0]
    return jnp.where(row < n - k, pltpu.roll(x, n - k, axis=0), 0.0)


def _cumsum_rows(x, row, reverse=False):
    shift = _shift_up if reverse else _shift_down
    k = 1
    while k < x.shape[0]:
        x = x + shift(x, k, row)
        k *= 2
    return x


def _full_spec(shape, idx=(0, 0)):
    return pl.BlockSpec(shape, lambda i: idx)


def _pool_window_select(lane, a2, a4, a8, a16):
    return jnp.where(lane < 64, a2, jnp.where(lane < 128, a4, jnp.where(lane < 192, a8, a16)))


def _pool_p(u, row, lane):
    t2 = u + _shift_down(u, 1, row)
    t4 = t2 + _shift_down(t2, 2, row)
    t8 = t4 + _shift_down(t4, 4, row)
    t16 = t8 + _shift_down(t8, 8, row)
    tw = _pool_window_select(lane, t2, t4, t8, t16)
    cnt = jnp.minimum((row + 1).astype(F32), _pool_window_select(lane, 2.0, 4.0, 8.0, 16.0))
    return tw / cnt - u, cnt


def _pool_fwd(z, wp_bd, pscale, name):
    def body(u_ref, w_ref, s_ref, o_ref):
        row = lax.broadcasted_iota(jnp.int32, (S, POOL_W), 0)
        lane = lax.broadcasted_iota(jnp.int32, (S, POOL_W), 1)
        p, _ = _pool_p(u_ref[...], row, lane)
        y = jnp.dot(p.astype(BF16), w_ref[...], preferred_element_type=F32)
        o_ref[...] = y * s_ref[0:1, :]

    return pl.pallas_call(
        body, name=name, out_shape=jax.ShapeDtypeStruct((S, POOL_W), F32), grid=(1,),
        in_specs=[_full_spec((S, POOL_W), (0, Z_PC // POOL_W)), _full_spec((POOL_W, POOL_W)), _full_spec((8, POOL_W))],
        out_specs=_full_spec((S, POOL_W)),
        compiler_params=_params(("arbitrary",)),
    )(z, wp_bd, pscale)


def _pool_bwd(z, wp_bd, pscale, dbr, name):
    def body(u_ref, w_ref, s_ref, dbr_ref, du_ref, dw_ref, red_ref):
        row = lax.broadcasted_iota(jnp.int32, (S, POOL_W), 0)
        lane = lax.broadcasted_iota(jnp.int32, (S, POOL_W), 1)
        p, cnt = _pool_p(u_ref[...], row, lane)
        pb = p.astype(BF16)
        y = jnp.dot(pb, w_ref[...], preferred_element_type=F32)
        dbr = dbr_ref[...]
        red_ref[...] = jnp.zeros_like(red_ref)
        red_ref[0:1, :] = jnp.sum(dbr * y, axis=0, keepdims=True)
        dy = (dbr * s_ref[0:1, :]).astype(BF16)
        dw_ref[...] = lax.dot_general(pb, dy, (((0,), (0,)), ((), ())), preferred_element_type=F32)
        dp = lax.dot_general(dy, w_ref[...], (((1,), (1,)), ((), ())), preferred_element_type=F32)
        g = dp / cnt
        a2 = g + _shift_up(g, 1, row)
        a4 = a2 + _shift_up(a2, 2, row)
        a8 = a4 + _shift_up(a4, 4, row)
        a16 = a8 + _shift_up(a8, 8, row)
        du_ref[...] = (_pool_window_select(lane, a2, a4, a8, a16) - dp).astype(BF16)

    return pl.pallas_call(
        body, name=name,
        out_shape=(jax.ShapeDtypeStruct((S, POOL_W), BF16), jax.ShapeDtypeStruct((POOL_W, POOL_W), F32),
                   jax.ShapeDtypeStruct((8, POOL_W), F32)),
        grid=(1,),
        in_specs=[_full_spec((S, POOL_W), (0, Z_PC // POOL_W)), _full_spec((POOL_W, POOL_W)), _full_spec((8, POOL_W)),
                  _full_spec((S, POOL_W))],
        out_specs=(_full_spec((S, POOL_W)), _full_spec((POOL_W, POOL_W)), _full_spec((8, POOL_W))),
        compiler_params=_params(("arbitrary",)),
    )(z, wp_bd, pscale, dbr)


def _conv_specs():
    base = Z_PC // CONV_W
    return [_full_spec((S, CONV_W), (0, base + 1)), _full_spec((S, CONV_W), (0, base + 2)),
            _full_spec((S, CONV_W), (0, base + 3)), _full_spec((8, CONV_W))]


def _conv_fwd(z, cw, name):
    def body(h_ref, b_ref, c_ref, w_ref, o_ref):
        row = lax.broadcasted_iota(jnp.int32, (S, CONV_W), 0)
        u = c_ref[...] * h_ref[...]
        y = (w_ref[0:1, :] * _shift_down(u, 2, row) + w_ref[1:2, :] * _shift_down(u, 1, row) + w_ref[2:3, :] * u)
        o_ref[...] = b_ref[...] * y

    return pl.pallas_call(
        body, name=name, out_shape=jax.ShapeDtypeStruct((S, CONV_W), F32), grid=(1,),
        in_specs=_conv_specs(), out_specs=_full_spec((S, CONV_W)),
        compiler_params=_params(("arbitrary",)),
    )(z, z, z, cw)


def _conv_bwd(z, cw, dbr, name):
    def body(h_ref, b_ref, c_ref, w_ref, dbr_ref, d_ref, red_ref):
        row = lax.broadcasted_iota(jnp.int32, (S, CONV_W), 0)
        h, cg = h_ref[...], c_ref[...]
        u = cg * h
        u1 = _shift_down(u, 1, row)
        u2 = _shift_down(u, 2, row)
        y = w_ref[0:1, :] * u2 + w_ref[1:2, :] * u1 + w_ref[2:3, :] * u
        dbr = dbr_ref[...]
        dy = dbr * b_ref[...]
        du = w_ref[2:3, :] * dy + w_ref[1:2, :] * _shift_up(dy, 1, row) + w_ref[0:1, :] * _shift_up(dy, 2, row)
        d_ref[:, 0:CONV_W] = (du * cg).astype(BF16)
        d_ref[:, CONV_W:2 * CONV_W] = (dbr * y).astype(BF16)
        d_ref[:, 2 * CONV_W:3 * CONV_W] = (du * h).astype(BF16)
        red_ref[...] = jnp.zeros_like(red_ref)
        red_ref[0:1, :] = jnp.sum(dy * u2, axis=0, keepdims=True)
        red_ref[1:2, :] = jnp.sum(dy * u1, axis=0, keepdims=True)
        red_ref[2:3, :] = jnp.sum(dy * u, axis=0, keepdims=True)

    return pl.pallas_call(
        body, name=name,
        out_shape=(jax.ShapeDtypeStruct((S, 3 * CONV_W), BF16), jax.ShapeDtypeStruct((8, CONV_W), F32)),
        grid=(1,),
        in_specs=_conv_specs() + [_full_spec((S, CONV_W))],
        out_specs=(_full_spec((S, 3 * CONV_W)), _full_spec((8, CONV_W))),
        compiler_params=_params(("arbitrary",)),
    )(z, z, z, cw, dbr)


_NT = (((1,), (1,)), ((), ()))
_TN = (((0,), (0,)), ((), ()))
N_HEAD = 2 * N_PAIR


def _split3(x):
    hi = x.astype(BF16).astype(F32)
    mid = (x - hi).astype(BF16).astype(F32)
    lo = (x - hi - mid).astype(BF16).astype(F32)
    return hi, mid, lo


def _spare(lane, e, k):
    return lane == 64 * (1 - e) + k


def _spare3(lane, e, k):
    base = 64 * (1 - e) + k
    return (lane >= base) & (lane < base + 3)


def _put3(lane, e, k, pieces, rest):
    out = rest
    for n, piece in enumerate(pieces):
        out = jnp.where(_spare(lane, e, k + n), piece, out)
    return out


def _attn_prep(z, bf, name):
    def body(q_ref, k_ref, v_ref, f_ref, b_ref, qa_ref, ka_ref, va_ref, kat_ref):
        p = pl.program_id(0)
        row = lax.broadcasted_iota(jnp.int32, (S, LANE), 0)
        lane = lax.broadcasted_iota(jnp.int32, (S, LANE), 1)
        xv = f_ref[...] + b_ref[0:1, :]
        ls = jnp.minimum(xv, 0.0) - jnp.log(1.0 + jnp.exp(-jnp.abs(xv)))
        cum = _cumsum_rows(jnp.where(lane < N_HEAD, ls, 0.0), row)
        q, k, v = q_ref[...], k_ref[...], v_ref[...]
        for e in range(2):
            head = (lane >= 64) if e else (lane < 64)
            f = jnp.sum(jnp.where(lane == 2 * p + e, cum, 0.0), axis=1, keepdims=True)
            pieces = _split3(f)
            qa = jnp.where(head, q * ATT_SCALE, _put3(lane, e, 0, pieces, jnp.where(_spare3(lane, e, 3), 1.0, 0.0)))
            ones = jnp.where(_spare3(lane, e, 0) | _spare3(lane, e, 6), 1.0, 0.0)
            ka = jnp.where(head, k, _put3(lane, e, 3, [-x for x in pieces], ones))
            va = jnp.where(head, v, jnp.where(_spare3(lane, e, 0), 1.0, 0.0))
            qa_ref[e] = qa.astype(BF16)
            ka_ref[e] = ka.astype(BF16)
            va_ref[e] = va.astype(BF16)
            kat_ref[e] = ka.T.astype(BF16)

    qb, kb, vb = Z_Q // LANE, Z_K // LANE, Z_V // LANE
    heads = jax.ShapeDtypeStruct((N_HEAD, S, LANE), BF16)
    pair = pl.BlockSpec((2, S, LANE), lambda p: (p, 0, 0))
    return pl.pallas_call(
        body, name=name,
        out_shape=(heads, heads, heads, jax.ShapeDtypeStruct((N_HEAD, LANE, S), BF16)),
        grid=(N_PAIR,),
        in_specs=[pl.BlockSpec((S, LANE), lambda p: (0, qb + p)), pl.BlockSpec((S, LANE), lambda p: (0, kb + p)),
                  pl.BlockSpec((S, LANE), lambda p: (0, vb + p)), pl.BlockSpec((S, LANE), lambda p: (0, Z_F // LANE)),
                  pl.BlockSpec((8, LANE), lambda p: (0, 0))],
        out_specs=(pair, pair, pair, pl.BlockSpec((2, LANE, S), lambda p: (p, 0, 0))),
        compiler_params=_params(("parallel",)),
    )(z, z, z, z, bf)


def _attn_bwd_prep(qa, o, lse, do, name):
    def body(qa_ref, o_ref, lse_ref, do_ref, qa2_ref, doa_ref):
        lane = lax.broadcasted_iota(jnp.int32, (S, LANE), 1)
        dov, ov, lsev = do_ref[...], o_ref[...], lse_ref[...]
        for e in range(2):
            head = (lane >= 64) if e else (lane < 64)
            dsum = jnp.sum(jnp.where(head, dov * ov, 0.0), axis=1, keepdims=True)
            doa_ref[e] = jnp.where(head, dov, _put3(lane, e, 0, [-x for x in _split3(dsum)], 0.0)).astype(BF16)
            lse_col = lsev[:, 64 * e:64 * e + 1]
            qa2_ref[e] = _put3(lane, e, 6, [-x for x in _split3(lse_col)], qa_ref[e].astype(F32)).astype(BF16)

    heads = jax.ShapeDtypeStruct((N_HEAD, S, LANE), BF16)
    pair = pl.BlockSpec((2, S, LANE), lambda p: (p, 0, 0))
    cols = pl.BlockSpec((S, LANE), lambda p: (0, p))
    return pl.pallas_call(
        body, name=name, out_shape=(heads, heads), grid=(N_PAIR,),
        in_specs=[pair, cols, cols, cols], out_specs=(pair, pair),
        compiler_params=_params(("parallel",)),
    )(qa, o, lse, do)


def _attn_bwd_post(z, bf, dqt, dka, dva, name):
    def body(f_ref, b_ref, dqt_ref, dk_ref, dv_ref, dq_out, dk_out, dv_out, dfl_ref, red_ref, dcum_ref):
        p = pl.program_id(0)

        @pl.when(p == 0)
        def _():
            dcum_ref[...] = jnp.zeros_like(dcum_ref)

        row = lax.broadcasted_iota(jnp.int32, (S, LANE), 0)
        lane = lax.broadcasted_iota(jnp.int32, (S, LANE), 1)
        dqa = [dqt_ref[e].T for e in range(2)]
        dq_out[...] = (jnp.where(lane < 64, dqa[0], dqa[1]) * ATT_SCALE).astype(BF16)
        dk_out[...] = jnp.where(lane < 64, dk_ref[0], dk_ref[1]).astype(BF16)
        dv_out[...] = jnp.where(lane < 64, dv_ref[0], dv_ref[1]).astype(BF16)
        for e in range(2):
            d_query = jnp.sum(jnp.where(_spare(lane, e, 0), dqa[e], 0.0), axis=1, keepdims=True)
            d_key = jnp.sum(jnp.where(_spare(lane, e, 3), dk_ref[e], 0.0), axis=1, keepdims=True)
            dcum_ref[...] += jnp.where(lane == 2 * p + e, d_query - d_key, 0.0)

        @pl.when(p == N_PAIR - 1)
        def _():
            dls = _cumsum_rows(dcum_ref[...], row, reverse=True)
            xv = f_ref[...] + b_ref[0:1, :]
            dx = jnp.where(lane < N_HEAD, dls * jax.nn.sigmoid(-xv), 0.0)
            dfl_ref[...] = dx.astype(BF16)
            red_ref[...] = jnp.zeros_like(red_ref)
            red_ref[0:1, :] = jnp.sum(dx, axis=0, keepdims=True)

    wide = jax.ShapeDtypeStruct((S, N_PAIR * LANE), BF16)
    cols = pl.BlockSpec((S, LANE), lambda p: (0, p))
    pair = pl.BlockSpec((2, S, LANE), lambda p: (p, 0, 0))
    return pl.pallas_call(
        body, name=name,
        out_shape=(wide, wide, wide, jax.ShapeDtypeStruct((S, LANE), BF16), jax.ShapeDtypeStruct((8, LANE), F32)),
        grid=(N_PAIR,),
        in_specs=[pl.BlockSpec((S, LANE), lambda p: (0, Z_F // LANE)), pl.BlockSpec((8, LANE), lambda p: (0, 0)),
                  pl.BlockSpec((2, LANE, S), lambda p: (p, 0, 0)), pair, pair],
        out_specs=(cols, cols, cols, pl.BlockSpec((S, LANE), lambda p: (0, 0)), pl.BlockSpec((8, LANE), lambda p: (0, 0))),
        scratch_shapes=[pltpu.VMEM((S, LANE), F32)],
        compiler_params=_params(("arbitrary",)),
    )(z, bf, dqt, dka, dva)


def _attn_fwd(qa, ka, va, name):
    tq, tk = TQ_FWD, TQ
    ratio = tq // tk

    def body(qa_ref, ka_ref, va_ref, o_ref, lse_ref):
        i = pl.program_id(1)
        lane = lax.broadcasted_iota(jnp.int32, (tq, LANE), 1)
        row = lax.broadcasted_iota(jnp.int32, (tq, tk), 0)
        col = lax.broadcasted_iota(jnp.int32, (tq, tk), 1)
        nh = HEADS_PER_STEP
        qs = [qa_ref[h] for h in range(nh)]

        def block(j, carry, masked):
            off = pl.multiple_of(j * tk, tk)
            out = []
            for h in range(nh):
                m, acc = carry[h]
                s = lax.dot_general(qs[h], ka_ref[h, pl.ds(off, tk), :], _NT, preferred_element_type=F32)
                if masked:
                    s = jnp.where(col + (j - ratio * i) * tk > row, NEG_INF, s)
                mn = jnp.maximum(m, jnp.max(s, axis=1, keepdims=True))
                p = jnp.exp(s - mn).astype(BF16)
                acc = jnp.exp(m - mn) * acc + jnp.dot(p, va_ref[h, pl.ds(off, tk), :], preferred_element_type=F32)
                out.append((mn, acc))
            return tuple(out)

        init = (jnp.full((tq, 1), NEG_INF, F32), jnp.zeros((tq, LANE), F32))
        carry = lax.fori_loop(0, ratio * i, lambda j, c: block(j, c, False), (init,) * nh)
        for d in range(ratio):
            carry = block(ratio * i + d, carry, True)
        res = []
        for h in range(nh):
            m, acc = carry[h]
            l = jnp.sum(jnp.where(_spare(lane, h % 2, 0), acc, 0.0), axis=1, keepdims=True)
            res.append((acc / l, m + jnp.log(l)))
        for g in range(nh // 2):
            o_ref[:, g * LANE:(g + 1) * LANE] = jnp.where(lane < 64, res[2 * g][0], res[2 * g + 1][0])
            lse_ref[:, g * LANE:(g + 1) * LANE] = jnp.where(lane < 64, res[2 * g][1], res[2 * g + 1][1])

    nh = HEADS_PER_STEP
    out = jax.ShapeDtypeStruct((S, N_PAIR * LANE), F32)
    wide = pl.BlockSpec((tq, 64 * nh), lambda p, i: (i, p))
    return pl.pallas_call(
        body, name=name, out_shape=(out, out), grid=(N_HEAD // nh, S // tq),
        in_specs=[pl.BlockSpec((nh, tq, LANE), lambda p, i: (p, i, 0)), pl.BlockSpec((nh, S, LANE), lambda p, i: (p, 0, 0)),
                  pl.BlockSpec((nh, S, LANE), lambda p, i: (p, 0, 0))],
        out_specs=(wide, wide),
        compiler_params=_params(("parallel", "parallel")),
    )(qa, ka, va)


def _attn_bwd(qa2, ka, va, kat, doa, name):
    nq = S // TQ

    def body(qa_ref, ka_ref, va_ref, kat_ref, doa_ref, dqt_ref, dk_ref, dv_ref):
        j = pl.program_id(1)

        @pl.when(j == 0)
        def _():
            dqt_ref[...] = jnp.zeros_like(dqt_ref)

        key = lax.broadcasted_iota(jnp.int32, (TQ, TQ), 0)
        qry = lax.broadcasted_iota(jnp.int32, (TQ, TQ), 1)
        nh = HEADS_PER_STEP
        kav, vav, katv = ([ref[h] for h in range(nh)] for ref in (ka_ref, va_ref, kat_ref))

        def block(i, carry, masked):
            off = pl.multiple_of(i * TQ, TQ)
            out = []
            for h in range(nh):
                dk_acc, dv_acc = carry[h]
                qav = qa_ref[h, pl.ds(off, TQ), :]
                doav = doa_ref[h, pl.ds(off, TQ), :]
                s_t = lax.dot_general(kav[h], qav, _NT, preferred_element_type=F32)
                if masked:
                    s_t = jnp.where(key > qry, NEG_INF, s_t)
                p_t = jnp.exp(s_t)
                ds_t = p_t * lax.dot_general(vav[h], doav, _NT, preferred_element_type=F32)
                dsb = ds_t.astype(BF16)
                dv_acc = dv_acc + jnp.dot(p_t.astype(BF16), doav, preferred_element_type=F32)
                dk_acc = dk_acc + jnp.dot(dsb, qav, preferred_element_type=F32)
                dqt_ref[h, :, pl.ds(off, TQ)] += jnp.dot(katv[h], dsb, preferred_element_type=F32)
                out.append((dk_acc, dv_acc))
            return tuple(out)

        zero = (jnp.zeros((TQ, LANE), F32), jnp.zeros((TQ, LANE), F32))
        carry = block(j, (zero,) * nh, True)
        carry = lax.fori_loop(j + 1, nq, lambda i, c: block(i, c, False), carry)
        for h in range(nh):
            dk_ref[h], dv_ref[h] = carry[h]

    nh = HEADS_PER_STEP
    full = pl.BlockSpec((nh, S, LANE), lambda p, j: (p, 0, 0))
    blk = pl.BlockSpec((nh, TQ, LANE), lambda p, j: (p, j, 0))
    acc = jax.ShapeDtypeStruct((N_HEAD, S, LANE), F32)
    return pl.pallas_call(
        body, name=name,
        out_shape=(jax.ShapeDtypeStruct((N_HEAD, LANE, S), F32), acc, acc),
        grid=(N_HEAD // nh, nq),
        in_specs=[full, blk, blk, pl.BlockSpec((nh, LANE, TQ), lambda p, j: (p, 0, j)), full],
        out_specs=(pl.BlockSpec((nh, LANE, S), lambda p, j: (p, 0, 0)), blk, blk),
        compiler_params=_params(("arbitrary", "arbitrary")),
    )(qa2, ka, va, kat, doa)


ADA_ROWS = 16


def _ada_fwd(c_pad, w_ada, b_cols, name):
    def body(c_ref, w_ref, b_ref, o_ref):
        cv = c_ref[...]
        sc = (cv * jax.nn.sigmoid(cv)).astype(BF16)
        o_ref[0] = jnp.dot(sc, w_ref[0].astype(BF16), preferred_element_type=F32) + b_ref[0, 0:1, :]

    return pl.pallas_call(
        body, name=name, out_shape=jax.ShapeDtypeStruct((DEPTH, ADA_ROWS, ADA_COLS), F32), grid=(DEPTH,),
        in_specs=[pl.BlockSpec((ADA_ROWS, D), lambda l: (0, 0)), pl.BlockSpec((1, D, ADA_COLS), lambda l: (l, 0, 0)),
                  pl.BlockSpec((1, 8, ADA_COLS), lambda l: (l, 0, 0))],
        out_specs=pl.BlockSpec((1, ADA_ROWS, ADA_COLS), lambda l: (l, 0, 0)),
        compiler_params=_params(("parallel",)),
    )(c_pad, w_ada, b_cols)


def _ada_bwd(c_pad, dmod_cols, name):
    def body(c_ref, d_ref, o_ref):
        cv = c_ref[...]
        sc = (cv * jax.nn.sigmoid(cv)).astype(BF16)
        o_ref[0] = lax.dot_general(sc, d_ref[0].astype(BF16), _TN, preferred_element_type=F32)

    return pl.pallas_call(
        body, name=name, out_shape=jax.ShapeDtypeStruct((DEPTH, D, ADA_COLS), F32), grid=(DEPTH,),
        in_specs=[pl.BlockSpec((ADA_ROWS, D), lambda l: (0, 0)), pl.BlockSpec((1, ADA_ROWS, ADA_COLS), lambda l: (l, 0, 0))],
        out_specs=pl.BlockSpec((1, D, ADA_COLS), lambda l: (l, 0, 0)),
        compiler_params=_params(("parallel",)),
    )(c_pad, dmod_cols)


def _adamw_math(w, g, m, v):
    m = B1 * m + (1.0 - B1) * g
    v = B2 * v + (1.0 - B2) * (g * g)
    m_hat = m / (1.0 - B1 ** STEP)
    v_hat = v / (1.0 - B2 ** STEP)
    delta = -LR * (m_hat / (jnp.sqrt(v_hat) + EPS) + WD * w)
    return delta, m, v


def _row_tile(rows, target=256):
    best = 8
    for t in range(8, min(rows, target) + 1, 8):
        if rows % t == 0:
            best = t
    return best


def _adamw(w, g, m, v, name):
    layers, rows, cols = w.shape
    tr = _row_tile(rows)
    spec = pl.BlockSpec((1, tr, cols), lambda l, i: (l, i, 0))

    def body(w_ref, g_ref, m_ref, v_ref, d_ref, nm_ref, nv_ref):
        d_ref[...], nm_ref[...], nv_ref[...] = _adamw_math(w_ref[...], g_ref[...], m_ref[...], v_ref[...])

    out = jax.ShapeDtypeStruct(w.shape, F32)
    return pl.pallas_call(
        body, name=name, out_shape=(out, out, out), grid=(layers, rows // tr),
        in_specs=[spec] * 4, out_specs=(spec,) * 3, compiler_params=_params(("parallel", "parallel")),
    )(w, g, m, v)


def _sum_slabs(x, name):
    n, rows, _ = x.shape
    tr = _row_tile(rows)

    def body(x_ref, o_ref):
        acc = x_ref[0]
        for k in range(1, n):
            acc = acc + x_ref[k]
        o_ref[...] = acc

    return pl.pallas_call(
        body, name=name, out_shape=jax.ShapeDtypeStruct((rows, D), F32), grid=(rows // tr,),
        in_specs=[pl.BlockSpec((n, tr, D), lambda i: (0, i, 0))], out_specs=pl.BlockSpec((tr, D), lambda i: (i, 0)),
        compiler_params=_params(("parallel",)),
    )(x)


_ANY = pl.BlockSpec(memory_space=pl.ANY)
MESH = pl.DeviceIdType.MESH


def _on_sequencer(body, out_shape, sems, operands, after, sequencer_id, name):
    n = len(operands)

    def ordered_body(*refs):
        body(*refs[:n], *refs[n + 1:])

    extra = [] if after is None else [after]
    return pl.kernel(
        body if after is None else ordered_body, out_type=out_shape,
        mesh=plsc.ScalarSubcoreMesh(axis_name="sequencer", num_cores=1), scratch_types=sems,
        compiler_params=pltpu.CompilerParams(collective_id=sequencer_id), name=name)(*operands, *extra)


def _all_gather(xs, name, sequencer_id=None, after=None):
    n = len(xs)

    def body(*refs):
        x_refs, out_refs = refs[:n], refs[n:2 * n]
        send_sems, recv_sems, local_sems = refs[2 * n:]
        x_, y_, c_ = lax.axis_index("x"), lax.axis_index("y"), lax.axis_index("c")
        me, sibling = (x_, y_, c_), (x_, y_, 1 - c_)
        chips = [(1 - x_, y_), (x_, 1 - y_), (1 - x_, 1 - y_)]
        if sequencer_id is not None:
            barrier = pltpu.get_barrier_semaphore()
            peers = [sibling] + [(*chip, pc) for chip in chips for pc in (c_, 1 - c_)]
            for peer in peers:
                pl.semaphore_signal(barrier, inc=1, device_id=peer, device_id_type=MESH)
            pl.semaphore_wait(barrier, len(peers))

        def slot(a, px, py, pc):
            return out_refs[a].at[4 * px + 2 * py + pc]

        def copy(a, k, block, to, src=None):
            return pltpu.make_async_remote_copy(
                src_ref=slot(a, *block) if src is None else src, dst_ref=slot(a, *block),
                send_sem=send_sems.at[7 * a + k], recv_sem=recv_sems.at[7 * a + k], device_id=to, device_id_type=MESH)

        mine = [pltpu.make_async_copy(x_refs[a], slot(a, *me), local_sems.at[a]) for a in range(n)]
        for cp in mine:
            cp.start()
        first = []
        for a in range(n):
            first.append(copy(a, 0, me, sibling, src=x_refs[a]))
            first += [copy(a, 1 + j, me, (*chip, c_), src=x_refs[a]) for j, chip in enumerate(chips)]
        for cp in first:
            cp.start()
        passed = []
        for j, chip in enumerate(chips):
            for a in range(n):
                copy(a, 1 + j, (*chip, c_), me).wait_recv()
                passed.append(copy(a, 4 + j, (*chip, c_), sibling))
                passed[-1].start()
        for a in range(n):
            copy(a, 0, sibling, me).wait_recv()
        for j, chip in enumerate(chips):
            for a in range(n):
                copy(a, 4 + j, (*chip, 1 - c_), me).wait_recv()
        for cp in first + passed:
            cp.wait_send()
        for cp in mine:
            cp.wait()

    out_shape = [jax.ShapeDtypeStruct((N_DEV,) + x.shape, x.dtype) for x in xs]
    sems = [pltpu.SemaphoreType.DMA((7 * n,)), pltpu.SemaphoreType.DMA((7 * n,)), pltpu.SemaphoreType.DMA((n,))]
    if sequencer_id is not None:
        return _on_sequencer(body, out_shape, sems, xs, after, sequencer_id, name)
    return pl.pallas_call(
        body, name=name, out_shape=out_shape, in_specs=[_ANY] * n, out_specs=[_ANY] * n, scratch_shapes=sems)(*xs)


def _sibling_exchange(gs, name, sequencer_id=None, after=None):
    n = len(gs)

    def body(*refs):
        g_refs, p_refs = refs[:n], refs[n:2 * n]
        send_sems, recv_sems = refs[2 * n:]
        x_, y_, c_ = lax.axis_index("x"), lax.axis_index("y"), lax.axis_index("c")
        if sequencer_id is not None:
            barrier = pltpu.get_barrier_semaphore()
            pl.semaphore_signal(barrier, inc=1, device_id=(x_, y_, 1 - c_), device_id_type=MESH)
            pl.semaphore_wait(barrier, 1)
        copies = [pltpu.make_async_remote_copy(
            src_ref=g_refs[a].at[2 * k + (1 - c_)], dst_ref=p_refs[a].at[k], send_sem=send_sems.at[4 * a + k],
            recv_sem=recv_sems.at[4 * a + k], device_id=(x_, y_, 1 - c_), device_id_type=MESH)
            for a in range(n) for k in range(4)]
        for cp in copies:
            cp.start()
        for cp in copies:
            cp.wait()

    out_shape = [jax.ShapeDtypeStruct((4,) + g.shape[1:], g.dtype) for g in gs]
    sems = [pltpu.SemaphoreType.DMA((4 * n,)), pltpu.SemaphoreType.DMA((4 * n,))]
    if sequencer_id is not None:
        return _on_sequencer(body, out_shape, sems, gs, after, sequencer_id, name)
    return pl.pallas_call(
        body, name=name, out_shape=out_shape, in_specs=[_ANY] * n, out_specs=[_ANY] * n, scratch_shapes=sems)(*gs)


def _slab_tiles(rows, cols):
    if rows % 8 == 0:
        return _row_tile(rows), cols
    return rows, 2 * LANE


def _pair_sums(g, p, route, name):
    _, rows, cols = g.shape
    tr, tc = _slab_tiles(rows, cols)

    def body(route_ref, g_ref, p_ref, t_ref):
        t_ref[...] = (g_ref[...] + p_ref[...]).astype(BF16)

    return pl.pallas_call(
        body, name=name, out_shape=jax.ShapeDtypeStruct((3, rows, cols), BF16),
        grid_spec=pltpu.PrefetchScalarGridSpec(
            num_scalar_prefetch=1, grid=(3, rows // tr, cols // tc),
            in_specs=[pl.BlockSpec((1, tr, tc), lambda r, i, j, route_ref: (2 * route_ref[1 + r] + route_ref[0], i, j)),
                      pl.BlockSpec((1, tr, tc), lambda r, i, j, route_ref: (route_ref[1 + r], i, j))],
            out_specs=pl.BlockSpec((1, tr, tc), lambda r, i, j, route_ref: (r, i, j))),
        compiler_params=_params(("parallel", "parallel", "parallel")),
    )(route, g, p)


def _chip_exchange(ts, name, sequencer_id=None, after=None):
    n = len(ts)

    def body(*refs):
        t_refs, l_refs = refs[:n], refs[n:2 * n]
        send_sems, recv_sems = refs[2 * n:]
        x_, y_, c_ = lax.axis_index("x"), lax.axis_index("y"), lax.axis_index("c")
        chips = [(1 - x_, y_), (x_, 1 - y_), (1 - x_, 1 - y_)]
        if sequencer_id is not None:
            barrier = pltpu.get_barrier_semaphore()
            for px, py in chips:
                pl.semaphore_signal(barrier, inc=1, device_id=(px, py, c_), device_id_type=MESH)
            pl.semaphore_wait(barrier, len(chips))
        copies = [pltpu.make_async_remote_copy(
            src_ref=t_refs[a].at[r], dst_ref=l_refs[a].at[r], send_sem=send_sems.at[3 * a + r],
            recv_sem=recv_sems.at[3 * a + r], device_id=(px, py, c_), device_id_type=MESH)
            for a in range(n) for r, (px, py) in enumerate(chips)]
        for cp in copies:
            cp.start()
        for cp in copies:
            cp.wait()

    out_shape = [jax.ShapeDtypeStruct((3,) + t.shape[1:], t.dtype) for t in ts]
    sems = [pltpu.SemaphoreType.DMA((3 * n,)), pltpu.SemaphoreType.DMA((3 * n,))]
    if sequencer_id is not None:
        return _on_sequencer(body, out_shape, sems, ts, after, sequencer_id, name)
    return pl.pallas_call(
        body, name=name, out_shape=out_shape, in_specs=[_ANY] * n, out_specs=[_ANY] * n, scratch_shapes=sems)(*ts)


def _reduce_adamw(gs, ps, landed, place, w, m, v, name):
    layers, rows, cols = w.shape
    assert layers == DEPTH == 2
    tr, tc = _slab_tiles(rows, cols)
    nr, nc = rows // tr, cols // tc
    spec = pl.BlockSpec((1, tr, tc), lambda l, i, j, place_ref: (l, i, j))

    def own(layer, which):
        pi, pj = (nr - 1, nc - 1) if layer == 0 else (0, 0)

        def index(l, i, j, place_ref):
            lead = 0 if which is None else place_ref[which]
            return lead, jnp.where(l == layer, i, pi), jnp.where(l == layer, j, pj)

        return pl.BlockSpec((3 if which is None else 1, tr, tc), index)

    def body(place_ref, g0_ref, p0_ref, l0_ref, g1_ref, p1_ref, l1_ref, w_ref, m_ref, v_ref,
             g_ref, d_ref, nm_ref, nv_ref):
        def update(own_ref, sib_ref, l_ref):
            g = own_ref[0] + sib_ref[0] + l_ref[0].astype(F32) + l_ref[1].astype(F32) + l_ref[2].astype(F32)
            g_ref[0] = g
            d_ref[0], nm_ref[0], nv_ref[0] = _adamw_math(w_ref[0], g, m_ref[0], v_ref[0])

        @pl.when(pl.program_id(0) == 0)
        def _():
            update(g0_ref, p0_ref, l0_ref)

        @pl.when(pl.program_id(0) == 1)
        def _():
            update(g1_ref, p1_ref, l1_ref)

    out = jax.ShapeDtypeStruct(w.shape, F32)
    return pl.pallas_call(
        body, name=name, out_shape=(out, out, out, out),
        grid_spec=pltpu.PrefetchScalarGridSpec(
            num_scalar_prefetch=1, grid=(DEPTH, nr, nc),
            in_specs=[own(0, 0), own(0, 1), own(0, None), own(1, 0), own(1, 1), own(1, None), spec, spec, spec],
            out_specs=(spec, spec, spec, spec)),
        compiler_params=_params(("arbitrary", "arbitrary", "arbitrary")),
    )(place, gs[0], ps[0], landed[0], gs[1], ps[1], landed[1], w, m, v)


def _pack(pieces, row_multiple, dtype, cols=D, rows=None):
    flat = jnp.concatenate([p.astype(dtype).reshape(-1) for p in pieces])
    if rows is None:
        rows = -(-flat.shape[0] // cols)
        rows = -(-rows // row_multiple) * row_multiple
    flat = jnp.pad(flat, (0, rows * cols - flat.shape[0]))
    return flat.reshape(rows, cols)


def _unpack(flat, shapes, lead=()):
    out, off = [], 0
    for shp in shapes:
        n = 1
        for s_ in shp:
            n *= s_
        out.append(lax.slice_in_dim(flat, off, off + n, axis=len(lead)).reshape(lead + tuple(shp)))
        off += n
    return out


WIN_STRIDE = 704
WIN_ROWS = 720
Z_TURN = 1544


def _window(wt, me):
    return lax.dynamic_update_slice(jnp.zeros((DEPTH, WIN_ROWS, wt.shape[-1]), wt.dtype), wt, (0, me, 0))


def _z_rows_from_windows(win):
    over = WIN_ROWS - WIN_STRIDE
    pieces = [(0, win[0][0:WIN_STRIDE])]
    for d in range(1, N_DEV):
        base = WIN_STRIDE * d
        pieces.append((base, win[d - 1][WIN_STRIDE:WIN_ROWS] + win[d][0:over]))
        pieces.append((base + over, win[d][over:WIN_STRIDE]))
    pieces.append((WIN_STRIDE * N_DEV, win[N_DEV - 1][WIN_STRIDE:WIN_ROWS]))

    def rows(a, b):
        out = []
        for start, arr in pieces:
            lo, hi = max(a, start), min(b, start + arr.shape[0])
            if lo < hi:
                out.append(arr[lo - start:hi - start])
        return out

    pad = jnp.zeros((NZ - IN_COLS, win.shape[-1]), win.dtype)
    return jnp.concatenate(rows(Z_TURN, IN_COLS) + rows(0, Z_TURN) + [pad], axis=0)


def _in_rows_from_z(wt):
    return jnp.concatenate([wt[Z_Q:Z_Q + 1536], wt[Z_F:Z_F + 8], wt[Z_PC:Z_PC + 1024], wt[Z_G:Z_G + 3072]], axis=0)


def _pad_rows(v, rows=8):
    return jnp.pad(v, ((0, rows - v.shape[0]), (0, 0)))


def _layer_fwd(l, x, wts, gvec, mod):
    tag = f"l{l}"
    h = _prenorm_fwd(x, gvec, mod, 0, 0, 1, f"prenorm_mix_{tag}")
    z = _matmul(h, wts["w_in_t"], "nt", f"in_proj_{tag}", tn=1152)
    qa, ka, va, kat = _attn_prep(z, wts["b_f"], f"attn_prep_{tag}")
    o, lse = _attn_fwd(qa, ka, va, f"attn_{tag}")
    br_b = _pool_fwd(z, wts["wp_bd"], wts["pool_scale"], f"pool_{tag}")
    br_c = _conv_fwd(z, wts["conv_w"], f"conv_{tag}")
    pa = _matmul(o, wts["wa"], "nn", f"proj_a_{tag}")
    pb = _matmul(br_b, wts["wb"], "nn", f"proj_b_{tag}")
    gates = [(z, Z_G + k * D) for k in range(3)]
    pc, merged = _matmul(br_c, wts["wc"], "nn", f"proj_c_merge_{tag}", tm=512, tn=512,
                         extra=gates + [(pa, 0), (pb, 0)], epilogue=_merge_epilogue, out_dtypes=(F32, BF16))
    y = _matmul(merged, wts["w_out"], "nn", f"out_proj_{tag}")
    x1 = _postnorm_fwd(x, y, gvec, mod, 1, 2, f"postnorm_mix_{tag}")
    h2 = _prenorm_fwd(x1, gvec, mod, 2, 3, 4, f"prenorm_ff_{tag}")
    a, r = _matmul(h2, wts["w_ff1"], "nn", f"ff1_{tag}", b_col_shards=True, epilogue=_relu2_epilogue,
                   out_dtypes=(F32, BF16))
    y2 = _matmul(r, wts["w_ff2"], "nn", f"ff2_{tag}")
    x2 = _postnorm_fwd(x1, y2, gvec, mod, 3, 5, f"postnorm_ff_{tag}")
    saved = dict(x=x, h=h, z=z, qa=qa, ka=ka, va=va, kat=kat, o=o, lse=lse, br_b=br_b, br_c=br_c, pa=pa, pb=pb, pc=pc,
                 merged=merged, y=y, x1=x1, h2=h2, a=a, r=r, y2=y2)
    return x2, saved


def _ffn_bwd(l, dx2, sv, wts, gvec, mod, midpoint):
    tag = f"l{l}"
    dy2, red_post_ff = _postnorm_bwd(sv["y2"], gvec, mod, dx2, 3, 5, f"postnorm_ff_bwd_{tag}")
    dy2 = midpoint(dy2)
    da = _matmul(dy2, wts["w_ff2"], "nt", f"ff2_dx_{tag}", extra=[(sv["a"], 0)], epilogue=_relu2_bwd_epilogue,
                 out_dtypes=(BF16,))[0]
    d_w_ff2 = _matmul(sv["r"], dy2, "tn", f"ff2_dw_{tag}")
    dh2 = _matmul(da, wts["w_ff1"], "nt", f"ff1_dx_{tag}", b_col_shards=True)
    d_w_ff1 = _matmul(sv["h2"], da, "tn", f"ff1_dw_{tag}", out_col_shards=True)
    dx1, red_pre_ff = _prenorm_bwd(sv["x1"], gvec, mod, dh2, dx2, 2, 4, f"prenorm_ff_bwd_{tag}")
    return dx1, [d_w_ff1, d_w_ff2.reshape(N_DEV, D_FF // N_DEV, D)], (red_pre_ff, red_post_ff)


def _mixer_bwd(l, dx1, sv, wts, gvec, mod, ffn_reds, midpoint):
    tag = f"l{l}"
    red_pre_ff, red_post_ff = ffn_reds
    dy, red_post_mix = _postnorm_bwd(sv["y"], gvec, mod, dx1, 1, 2, f"postnorm_mix_bwd_{tag}")
    gates = [(sv["z"], Z_G + k * D) for k in range(3)]
    dpa, dpb, dpc, *dgl = _matmul(dy, wts["w_out"], "nt", f"out_proj_dx_{tag}", tm=512, tn=512,
                                  extra=gates + [(sv["pa"], 0), (sv["pb"], 0), (sv["pc"], 0)],
                                  epilogue=_merge_bwd_epilogue, out_dtypes=(BF16,) * 6)
    d_w_out = _matmul(sv["merged"], dy, "tn", f"out_proj_dw_{tag}")
    dpa = midpoint(dpa)
    do = _matmul(dpa, wts["wa"], "nt", f"proj_a_dx_{tag}")
    dbr_b = _matmul(dpb, wts["wb"], "nt", f"proj_b_dx_{tag}")
    dbr_c = _matmul(dpc, wts["wc"], "nt", f"proj_c_dx_{tag}")
    d_wa = _matmul(sv["o"], dpa, "tn", f"proj_a_dw_{tag}")
    d_wb = _matmul(sv["br_b"], dpb, "tn", f"proj_b_dw_{tag}")
    d_wc = _matmul(sv["br_c"], dpc, "tn", f"proj_c_dw_{tag}")
    d_w_branch = jnp.concatenate([d_wa, d_wb, d_wc], axis=0)

    dpu, d_wp_bd, red_pool = _pool_bwd(sv["z"], wts["wp_bd"], wts["pool_scale"], dbr_b, f"pool_bwd_{tag}")
    dconv, red_conv = _conv_bwd(sv["z"], wts["conv_w"], dbr_c, f"conv_bwd_{tag}")
    qa2, doa = _attn_bwd_prep(sv["qa"], sv["o"], sv["lse"], do, f"attn_bwd_prep_{tag}")
    dqt, dka, dva = _attn_bwd(qa2, sv["ka"], sv["va"], sv["kat"], doa, f"attn_bwd_{tag}")
    dq, dk, dv, dfl, red_f = _attn_bwd_post(sv["z"], wts["b_f"], dqt, dka, dva, f"attn_bwd_post_{tag}")
    dz = jnp.concatenate([dpu, dconv, *dgl, dq, dk, dv, dfl], axis=1)
    dh = _matmul(dz, wts["w_in_t"], "nn", f"in_proj_dx_{tag}", tk=1152)
    d_w_in_t = _matmul(dz, sv["h"], "tn", f"in_proj_dw_{tag}", tm=1152)
    dx0, red_pre_mix = _prenorm_bwd(sv["x"], gvec, mod, dh, dx1, 0, 1, f"prenorm_mix_bwd_{tag}")

    rows = D // N_DEV
    big = [_in_rows_from_z(d_w_in_t).reshape(N_DEV, IN_SHARD, D), d_w_branch.reshape(N_DEV, rows, D),
           d_w_out.reshape(N_DEV, rows, D)]
    d_w_pool = jnp.stack([d_wp_bd[64 * g:64 * (g + 1), 64 * g:64 * (g + 1)] for g in range(4)])
    small = dict(
        mod=jnp.stack([red_pre_mix[0], red_pre_mix[1], red_post_mix[0], red_pre_ff[0], red_pre_ff[1], red_post_ff[0]]),
        g_mix_pre=red_pre_mix[2], g_mix_post=red_post_mix[1], g_ff_pre=red_pre_ff[2], g_ff_post=red_post_ff[1],
        b_f=red_f[0, 0:8], w_pool=d_w_pool, pool_scale=red_pool[0], conv_w=red_conv[0:3])
    return dx0, big, small


SMALL_KEYS = ["mod", "g_mix_pre", "g_mix_post", "g_ff_pre", "g_ff_post", "b_f", "w_pool", "pool_scale", "conv_w"]
SMALL_SHAPES = [(DEPTH, 6 * D), (DEPTH, D), (DEPTH, D), (DEPTH, D), (DEPTH, D), (DEPTH, 8), (DEPTH, 4, 64, 64),
                (DEPTH, POOL_W), (DEPTH, 3, CONV_W)]


def kernel(x, c, w_ada, b_ada, g_mix_pre, g_mix_post, g_ff_pre, g_ff_post, w_in, b_f, w_pool, pool_scale, conv_w, w_branch, w_out, w_ff1, w_ff2, loss_target, m_w_ada, m_b_ada, m_g_mix_pre, m_g_mix_post, m_g_ff_pre, m_g_ff_post, m_w_in, m_b_f, m_w_pool, m_pool_scale, m_conv_w, m_w_branch, m_w_out, m_w_ff1, m_w_ff2, v_w_ada, v_b_ada, v_g_mix_pre, v_g_mix_post, v_g_ff_pre, v_g_ff_post, v_w_in, v_b_f, v_w_pool, v_pool_scale, v_conv_w, v_w_branch, v_w_out, v_w_ff1, v_w_ff2):
    ix, iy, ic = lax.axis_index("x"), lax.axis_index("y"), lax.axis_index("c")
    me = 4 * ix + 2 * iy + ic
    route = jnp.stack([ic, 2 * (1 - ix) + iy, 2 * ix + (1 - iy), 2 * (1 - ix) + (1 - iy)]).astype(jnp.int32)
    place = jnp.stack([me, 2 * ix + iy]).astype(jnp.int32)
    wt_in, mt_in, vt_in = (jnp.transpose(a, (0, 2, 1)) for a in (w_in, m_w_in, v_w_in))

    c_all = _all_gather([_pad_rows(c)], "gather_c")[0][:, 0, :]
    c_pad = _pad_rows(c_all, ADA_ROWS)
    b_cols = lax.dynamic_slice_in_dim(b_ada, me * ADA_COLS, ADA_COLS, axis=1)
    b_cols = jnp.broadcast_to(b_cols[:, None, :], (DEPTH, 8, ADA_COLS))
    mod_part = _ada_fwd(c_pad, w_ada, b_cols, "ada_fwd")
    mod_all = _all_gather([mod_part.reshape(DEPTH * ADA_ROWS, ADA_COLS)], "gather_mod")[0]
    mod_all = mod_all.reshape(N_DEV, DEPTH, ADA_ROWS, ADA_COLS)
    mod_mine = lax.dynamic_index_in_dim(mod_all, me, axis=2, keepdims=False)
    mod_mine = jnp.transpose(mod_mine, (1, 0, 2)).reshape(DEPTH, 6, D)

    cw_cols = CONV_W // N_DEV
    cw_send = jnp.pad(conv_w.reshape(DEPTH * 3, cw_cols), ((0, 8 - DEPTH * 3), (0, LANE - cw_cols)))
    win_in = _window(wt_in, me)
    send = [[w[l].astype(BF16) for w in (win_in, w_branch, w_out, w_ff1, w_ff2)] for l in range(DEPTH)]
    first = _all_gather(send[0][:1], "gather_weights_l0_in", sequencer_id=1, after=mod_all)
    rest = _all_gather(send[0][1:] + [cw_send], "gather_weights_l0_rest", sequencer_id=2, after=first[0])
    gathered = [first + rest[:4], _all_gather(send[1], "gather_weights_l1", sequencer_id=3, after=first[0])]
    cw_all = rest[4][:, :DEPTH * 3, :cw_cols].reshape(N_DEV, DEPTH, 3, cw_cols)

    def layer_operands(l, weights):
        p_in, p_br, p_out, p_ff1, p_ff2 = weights[:5]
        w_br_full = p_br.reshape(D, D)
        cw_full = jnp.transpose(cw_all[:, l], (1, 0, 2)).reshape(3, CONV_W)
        wp_bd = jnp.zeros((POOL_W, POOL_W), F32)
        for g in range(4):
            wp_bd = wp_bd.at[64 * g:64 * (g + 1), 64 * g:64 * (g + 1)].set(w_pool[l, g])
        wts = dict(
            w_in_t=_z_rows_from_windows(p_in), wa=w_br_full[0:A_WIDTH], wb=w_br_full[A_WIDTH:A_WIDTH + POOL_W],
            wc=w_br_full[A_WIDTH + POOL_W:], w_out=p_out.reshape(D, D),
            w_ff1=p_ff1, w_ff2=p_ff2.reshape(D_FF, D),
            conv_w=_pad_rows(cw_full), wp_bd=wp_bd.astype(BF16), pool_scale=_pad_rows(pool_scale[l][None, :]),
            b_f=_pad_rows(jnp.pad(b_f[l], (0, LANE - 8))[None, :]))
        gvec = _pad_rows(jnp.stack([g_mix_pre[l], g_mix_post[l], g_ff_pre[l], g_ff_post[l]]))
        return wts, gvec, _pad_rows(mod_mine[l])

    xs = x[0]
    saved, layers = [], []
    for l in range(DEPTH):
        weights = gathered[l]
        if l > 0:
            xs, weights = lax.optimization_barrier((xs, weights))
        layers.append(layer_operands(l, weights))
        xs, sv = _layer_fwd(l, xs, *layers[l])
        saved.append(sv)
    dx, loss_part = _loss_head(xs, loss_target[0], "loss_head")
    loss = lax.psum(loss_part[0, 0], ("x", "y", "c"))
    small_grads = [None] * DEPTH
    mine, sibs, landed = ({} for _ in range(3))
    seq_id = iter(range(4, 4 + 4 * DEPTH))
    last = [gathered[DEPTH - 1][0]]

    def start(group, grads):
        mine[group] = grads
        sibs[group] = _sibling_exchange(grads, f"rs_sibling_{group}", sequencer_id=next(seq_id), after=last[0])
        last[0] = sibs[group][0]

    def finish(group, later):
        later, (grads, sib) = lax.optimization_barrier((later, (mine[group], sibs[group])))
        sends = [_pair_sums(g, p, route, f"rs_pair_sums_{group}_{k}") for k, (g, p) in enumerate(zip(grads, sib))]
        later, sends = lax.optimization_barrier((later, sends))
        landed[group] = _chip_exchange(sends, f"rs_chips_{group}", sequencer_id=next(seq_id), after=last[0])
        last[0] = landed[group][0]
        return later

    pending = None
    for l in reversed(range(DEPTH)):
        hook = (lambda da: da) if pending is None else functools.partial(finish, pending)
        dx, ffn_grads, ffn_reds = _ffn_bwd(l, dx, saved[l], *layers[l], hook)
        start(f"ffn_l{l}", ffn_grads)
        dx, mix_grads, small_grads[l] = _mixer_bwd(l, dx, saved[l], *layers[l], ffn_reds,
                                                   functools.partial(finish, f"ffn_l{l}"))
        start(f"mix_l{l}", mix_grads)
        pending = f"mix_l{l}"
    grad_x = dx[None]

    big_w = [wt_in, w_branch, w_out, w_ff1, w_ff2]
    big_m = [mt_in, m_w_branch, m_w_out, m_w_ff1, m_w_ff2]
    big_v = [vt_in, v_w_branch, v_w_out, v_w_ff1, v_w_ff2]
    where = [("mix", 0), ("mix", 1), ("mix", 2), ("ffn", 0), ("ffn", 1)]

    def reduce_and_update(k):
        group, at = where[k]
        return _reduce_adamw([mine[f"{group}_l{l}"][at] for l in range(DEPTH)],
                             [sibs[f"{group}_l{l}"][at] for l in range(DEPTH)],
                             [landed[f"{group}_l{l}"][at] for l in range(DEPTH)], place, big_w[k], big_m[k], big_v[k],
                             f"rs_sum_adamw_{k}")

    big_res = {k: list(reduce_and_update(k)) for k in (3, 4)}
    big_res[3][0] = finish(pending, big_res[3][0])

    small = {k: jnp.stack([small_grads[l][k] for l in range(DEPTH)]) for k in SMALL_KEYS}
    small_all = _all_gather([_pack([small[k] for k in SMALL_KEYS], 8, F32)], "gather_small")[0]
    dmod_all = small_all[:, 0:DEPTH * 6, :].reshape(N_DEV, DEPTH, 6 * D)
    summed = _unpack(_sum_slabs(small_all, "sum_small").reshape(-1), SMALL_SHAPES)
    sg = dict(zip(SMALL_KEYS, summed))
    dmod_cols = lax.dynamic_slice_in_dim(dmod_all, me * ADA_COLS, ADA_COLS, axis=2)
    dmod_cols = jnp.pad(jnp.transpose(dmod_cols, (1, 0, 2)), ((0, 0), (0, ADA_ROWS - N_DEV), (0, 0)))
    g_w_ada = _ada_bwd(c_pad, dmod_cols, "ada_bwd")
    g_conv_w = lax.dynamic_slice_in_dim(sg["conv_w"], me * (CONV_W // N_DEV), CONV_W // N_DEV, axis=2)

    ada_out = [g_w_ada] + list(_adamw(w_ada, g_w_ada, m_w_ada, v_w_ada, "adamw_ada"))
    rest_w = [b_ada, g_mix_pre, g_mix_post, g_ff_pre, g_ff_post, b_f, w_pool, pool_scale, conv_w]
    rest_m = [m_b_ada, m_g_mix_pre, m_g_mix_post, m_g_ff_pre, m_g_ff_post, m_b_f, m_w_pool, m_pool_scale, m_conv_w]
    rest_v = [v_b_ada, v_g_mix_pre, v_g_mix_post, v_g_ff_pre, v_g_ff_post, v_b_f, v_w_pool, v_pool_scale, v_conv_w]
    rest_g = [sg["mod"], sg["g_mix_pre"], sg["g_mix_post"], sg["g_ff_pre"], sg["g_ff_post"], sg["b_f"],
              sg["w_pool"], sg["pool_scale"], g_conv_w]
    rest_shapes = [a.shape for a in rest_w]
    upd = _adamw(_pack(rest_w, 8, F32)[None], _pack(rest_g, 8, F32)[None], _pack(rest_m, 8, F32)[None],
                 _pack(rest_v, 8, F32)[None], "adamw_rest")
    rest_out = [rest_g] + [_unpack(arr.reshape(-1), rest_shapes) for arr in upd]
    rest_out = [[ada_out[which]] + rest_out[which] for which in range(4)]

    landed[pending], rest_out = lax.optimization_barrier((landed[pending], rest_out))
    big_res.update({k: reduce_and_update(k) for k in (0, 1, 2)})
    big_out = [[jnp.transpose(big_res[k][which], (0, 2, 1)) if k == 0 else big_res[k][which] for k in range(5)]
               for which in range(4)]

    def ordered(k):
        r, b = rest_out[k], big_out[k]
        return [r[0], r[1], r[2], r[3], r[4], r[5], b[0], r[6], r[7], r[8], r[9], b[1], b[2], b[3], b[4]]

    return (loss, grad_x, *ordered(0), *ordered(1), *ordered(2), *ordered(3))
```

```python
import functools

import jax
import jax.numpy as jnp
from jax import lax
from jax.experimental import pallas as pl
from jax.experimental.pallas import tpu as pltpu
from jax.experimental.pallas import tpu_sc as plsc

F32 = jnp.float32
BF16 = jnp.bfloat16

N_DEV = 8
D = 1024
S = 2048
DEPTH = 2
D_FF = 4 * D
A_WIDTH = 512
HEAD_DIM = 64
N_PAIR = 4
POOL_W = 256
CONV_W = 256
IN_COLS = 5640
ADA_COLS = 6 * D // N_DEV
IN_SHARD = IN_COLS // N_DEV
RMS_EPS = 1e-6
NEG_INF = -1e30
ATT_SCALE = HEAD_DIM ** -0.5

NZ = 5760
Z_PC = 0
Z_G = 1024
Z_Q = 4096
Z_K = 4608
Z_V = 5120
Z_F = 5632

LR, B1, B2, EPS, WD, STEP = 0.001, 0.9, 0.999, 1e-08, 0.01, 10

LANE = 128
VMEM_LIMIT_BYTES = 48 * 1024 * 1024
TS = 512
TQ = 256
TQ_FWD = 512
HEADS_PER_STEP = 4


def _params(sem=None):
    return pltpu.CompilerParams(dimension_semantics=sem, vmem_limit_bytes=VMEM_LIMIT_BYTES)


def _pick(n, target):
    best = None
    for t in range(LANE, min(n, target) + 1, LANE):
        if n % t == 0:
            best = t
    return n if best is None else best


def _matmul(a, b, mode, name, out_dtype=F32, tm=1024, tn=1024, tk=1024, b_col_shards=False, out_col_shards=False,
            extra=(), epilogue=None, out_dtypes=None):
    if b_col_shards:
        shards, b_rows, shard_cols = b.shape
        b_shape = (b_rows, shards * shard_cols)
    else:
        b_shape = b.shape
    if mode == "nn":
        (m, k), (k2, n) = a.shape, b_shape
    elif mode == "nt":
        (m, k), (n, k2) = a.shape, b_shape
    else:
        (k, m), (k2, n) = a.shape, b_shape
    assert k == k2, (a.shape, b.shape, mode)
    tm, tn, tk = _pick(m, tm), _pick(n, tn), _pick(k, tk)
    if b_col_shards and mode == "nn":
        tn = shard_cols
    if b_col_shards and mode == "nt":
        tk = shard_cols
    if out_col_shards:
        tn = n // N_DEV
    nk = k // tk
    if mode == "nn":
        a_spec = pl.BlockSpec((tm, tk), lambda i, j, kk: (i, kk))
        b_spec = (pl.BlockSpec((None, tk, tn), lambda i, j, kk: (j, kk, 0)) if b_col_shards else
                  pl.BlockSpec((tk, tn), lambda i, j, kk: (kk, j)))
        dims = (((1,), (0,)), ((), ()))
    elif mode == "nt":
        a_spec = pl.BlockSpec((tm, tk), lambda i, j, kk: (i, kk))
        b_spec = (pl.BlockSpec((None, tn, tk), lambda i, j, kk: (kk, j, 0)) if b_col_shards else
                  pl.BlockSpec((tn, tk), lambda i, j, kk: (j, kk)))
        dims = (((1,), (1,)), ((), ()))
    else:
        assert not b_col_shards
        a_spec = pl.BlockSpec((tk, tm), lambda i, j, kk: (kk, i))
        b_spec = pl.BlockSpec((tk, tn), lambda i, j, kk: (kk, j))
        dims = (((0,), (0,)), ((), ()))
    if out_col_shards:
        out_shape = jax.ShapeDtypeStruct((N_DEV, m, tn), out_dtype)
        out_spec = pl.BlockSpec((None, tm, tn), lambda i, j, kk: (j, i, 0))
    else:
        out_shape = jax.ShapeDtypeStruct((m, n), out_dtype)
        out_spec = pl.BlockSpec((tm, tn), lambda i, j, kk: (i, j))

    n_extra = len(extra)
    extra_specs = [pl.BlockSpec((tm, tn), lambda i, j, kk, off=off: (i, j + off // tn)) for _, off in extra]
    if epilogue is not None:
        assert not out_col_shards and all(off % tn == 0 for _, off in extra)
        out_shape = [jax.ShapeDtypeStruct((m, n), dt) for dt in out_dtypes]
        out_spec = [pl.BlockSpec((tm, tn), lambda i, j, kk: (i, j)) for _ in out_dtypes]

    def product(a_ref, b_ref):
        return lax.dot_general(a_ref[...].astype(BF16), b_ref[...].astype(BF16), dims, preferred_element_type=F32)

    def write(acc, extra_refs, o_refs):
        if epilogue is None:
            o_refs[0][...] = acc.astype(out_dtype)
        else:
            for o_ref, tile in zip(o_refs, epilogue(acc, *[r[...] for r in extra_refs])):
                o_ref[...] = tile.astype(o_ref.dtype)

    def body_one_pass(a_ref, b_ref, *refs):
        write(product(a_ref, b_ref), refs[:n_extra], refs[n_extra:])

    def body(a_ref, b_ref, *refs):
        acc_ref = refs[-1]
        kk = pl.program_id(2)

        @pl.when(kk == 0)
        def _():
            acc_ref[...] = product(a_ref, b_ref)

        @pl.when(kk > 0)
        def _():
            acc_ref[...] += product(a_ref, b_ref)

        @pl.when(kk == nk - 1)
        def _():
            write(acc_ref[...], refs[:n_extra], refs[n_extra:-1])

    return pl.pallas_call(
        body_one_pass if nk == 1 else body, name=name,
        out_shape=out_shape,
        grid=(m // tm, n // tn, nk),
        in_specs=[a_spec, b_spec] + extra_specs,
        out_specs=out_spec,
        scratch_shapes=[] if nk == 1 else [pltpu.VMEM((tm, tn), F32)],
        compiler_params=_params(("parallel", "parallel", "arbitrary")),
    )(a, b, *[x for x, _ in extra])


def _row_spec(width=D, col=0):
    return pl.BlockSpec((TS, width), lambda i: (i, col))


def _vec_spec(rows=8, width=D):
    return pl.BlockSpec((rows, width), lambda i: (0, 0))


def _rms(x):
    return lax.rsqrt(jnp.mean(x * x, axis=-1, keepdims=True) + RMS_EPS)


def _prenorm_fwd(x, gvec, mod, g_row, shift_row, scale_row, name):
    def body(x_ref, g_ref, mod_ref, h_ref):
        xv = x_ref[...]
        y = xv * _rms(xv) * g_ref[g_row:g_row + 1, :]
        h = y * (1.0 + mod_ref[scale_row:scale_row + 1, :]) + mod_ref[shift_row:shift_row + 1, :]
        h_ref[...] = h.astype(BF16)

    return pl.pallas_call(
        body, name=name, out_shape=jax.ShapeDtypeStruct((S, D), BF16), grid=(S // TS,),
        in_specs=[_row_spec(), _vec_spec(), _vec_spec()], out_specs=_row_spec(),
        compiler_params=_params(("parallel",)),
    )(x, gvec, mod)


def _prenorm_bwd(x, gvec, mod, dh, dres, g_row, scale_row, name):
    def body(x_ref, g_ref, mod_ref, dh_ref, dres_ref, dx_ref, red_ref):
        i = pl.program_id(0)

        @pl.when(i == 0)
        def _():
            red_ref[...] = jnp.zeros_like(red_ref)

        xv = x_ref[...]
        g = g_ref[g_row:g_row + 1, :]
        r = _rms(xv)
        n = xv * r
        yg = n * g
        dhv = dh_ref[...]
        dyg = dhv * (1.0 + mod_ref[scale_row:scale_row + 1, :])
        dn = dyg * g
        dx = r * (dn - n * jnp.mean(dn * n, axis=-1, keepdims=True))
        dx_ref[...] = dres_ref[...] + dx
        red_ref[0:1, :] += jnp.sum(dhv, axis=0, keepdims=True)
        red_ref[1:2, :] += jnp.sum(dhv * yg, axis=0, keepdims=True)
        red_ref[2:3, :] += jnp.sum(dyg * n, axis=0, keepdims=True)

    return pl.pallas_call(
        body, name=name,
        out_shape=(jax.ShapeDtypeStruct((S, D), F32), jax.ShapeDtypeStruct((8, D), F32)),
        grid=(S // TS,),
        in_specs=[_row_spec(), _vec_spec(), _vec_spec(), _row_spec(), _row_spec()],
        out_specs=(_row_spec(), _vec_spec()),
        compiler_params=_params(("arbitrary",)),
    )(x, gvec, mod, dh, dres)


def _postnorm_fwd(x, y, gvec, mod, g_row, gate_row, name):
    def body(x_ref, y_ref, g_ref, mod_ref, o_ref):
        yv = y_ref[...]
        yn = yv * _rms(yv) * g_ref[g_row:g_row + 1, :]
        o_ref[...] = x_ref[...] + mod_ref[gate_row:gate_row + 1, :] * yn

    return pl.pallas_call(
        body, name=name, out_shape=jax.ShapeDtypeStruct((S, D), F32), grid=(S // TS,),
        in_specs=[_row_spec(), _row_spec(), _vec_spec(), _vec_spec()], out_specs=_row_spec(),
        compiler_params=_params(("parallel",)),
    )(x, y, gvec, mod)


def _postnorm_bwd(y, gvec, mod, dxo, g_row, gate_row, name):
    def body(y_ref, g_ref, mod_ref, dxo_ref, dy_ref, red_ref):
        i = pl.program_id(0)

        @pl.when(i == 0)
        def _():
            red_ref[...] = jnp.zeros_like(red_ref)

        yv = y_ref[...]
        g = g_ref[g_row:g_row + 1, :]
        r = _rms(yv)
        n = yv * r
        dxo = dxo_ref[...]
        dyn = dxo * mod_ref[gate_row:gate_row + 1, :]
        dn = dyn * g
        dy = r * (dn - n * jnp.mean(dn * n, axis=-1, keepdims=True))
        dy_ref[...] = dy.astype(BF16)
        red_ref[0:1, :] += jnp.sum(dxo * (n * g), axis=0, keepdims=True)
        red_ref[1:2, :] += jnp.sum(dyn * n, axis=0, keepdims=True)

    return pl.pallas_call(
        body, name=name,
        out_shape=(jax.ShapeDtypeStruct((S, D), BF16), jax.ShapeDtypeStruct((8, D), F32)),
        grid=(S // TS,),
        in_specs=[_row_spec(), _vec_spec(), _vec_spec(), _row_spec()],
        out_specs=(_row_spec(), _vec_spec()),
        compiler_params=_params(("arbitrary",)),
    )(y, gvec, mod, dxo)


def _loss_head(xf, target, name):
    def body(x_ref, t_ref, dx_ref, loss_ref):
        i = pl.program_id(0)

        @pl.when(i == 0)
        def _():
            loss_ref[...] = jnp.zeros_like(loss_ref)

        e = x_ref[...] - t_ref[...]
        dx_ref[...] = e / float(D)
        per_tok = jnp.mean(e * e, axis=-1, keepdims=True)
        loss_ref[0:1, 0:1] += 0.5 * jnp.sum(per_tok, axis=0, keepdims=True)

    return pl.pallas_call(
        body, name=name,
        out_shape=(jax.ShapeDtypeStruct((S, D), F32), jax.ShapeDtypeStruct((8, LANE), F32)),
        grid=(S // TS,),
        in_specs=[_row_spec(), _row_spec()],
        out_specs=(_row_spec(), pl.BlockSpec((8, LANE), lambda i: (0, 0))),
        compiler_params=_params(("arbitrary",)),
    )(xf, target)


def _relu2_epilogue(a):
    t = jnp.maximum(a, 0.0)
    return a, t * t


def _relu2_bwd_epilogue(dr, a):
    return (dr * (2.0 * jnp.maximum(a, 0.0)),)


def _merge_epilogue(pc, g0, g1, g2, pa, pb):
    return pc, jax.nn.sigmoid(g0) * pa + jax.nn.sigmoid(g1) * pb + jax.nn.sigmoid(g2) * pc


def _merge_bwd_epilogue(dm, g0, g1, g2, pa, pb, pc):
    sg = [jax.nn.sigmoid(g) for g in (g0, g1, g2)]
    return tuple(dm * s for s in sg) + tuple(dm * p * (s * (1.0 - s)) for p, s in zip((pa, pb, pc), sg))


def _shift_down(x, k, row):
    return jnp.where(row >= k, pltpu.roll(x, k, axis=0), 0.0)


def _shift_up(x, k, row):
    n = x.shape[0]
    return jnp.where(row < n - k, pltpu.roll(x, n - k, axis=0), 0.0)


def _cumsum_rows(x, row, reverse=False):
    shift = _shift_up if reverse else _shift_down
    k = 1
    while k < x.shape[0]:
        x = x + shift(x, k, row)
        k *= 2
    return x


def _full_spec(shape, idx=(0, 0)):
    return pl.BlockSpec(shape, lambda i: idx)


def _pool_window_select(lane, a2, a4, a8, a16):
    return jnp.where(lane < 64, a2, jnp.where(lane < 128, a4, jnp.where(lane < 192, a8, a16)))


def _pool_p(u, row, lane):
    t2 = u + _shift_down(u, 1, row)
    t4 = t2 + _shift_down(t2, 2, row)
    t8 = t4 + _shift_down(t4, 4, row)
    t16 = t8 + _shift_down(t8, 8, row)
    tw = _pool_window_select(lane, t2, t4, t8, t16)
    cnt = jnp.minimum((row + 1).astype(F32), _pool_window_select(lane, 2.0, 4.0, 8.0, 16.0))
    return tw / cnt - u, cnt


def _pool_fwd(z, wp_bd, pscale, name):
    def body(u_ref, w_ref, s_ref, o_ref):
        row = lax.broadcasted_iota(jnp.int32, (S, POOL_W), 0)
        lane = lax.broadcasted_iota(jnp.int32, (S, POOL_W), 1)
        p, _ = _pool_p(u_ref[...], row, lane)
        y = jnp.dot(p.astype(BF16), w_ref[...], preferred_element_type=F32)
        o_ref[...] = y * s_ref[0:1, :]

    return pl.pallas_call(
        body, name=name, out_shape=jax.ShapeDtypeStruct((S, POOL_W), F32), grid=(1,),
        in_specs=[_full_spec((S, POOL_W), (0, Z_PC // POOL_W)), _full_spec((POOL_W, POOL_W)), _full_spec((8, POOL_W))],
        out_specs=_full_spec((S, POOL_W)),
        compiler_params=_params(("arbitrary",)),
    )(z, wp_bd, pscale)


def _pool_bwd(z, wp_bd, pscale, dbr, name):
    def body(u_ref, w_ref, s_ref, dbr_ref, du_ref, dw_ref, red_ref):
        row = lax.broadcasted_iota(jnp.int32, (S, POOL_W), 0)
        lane = lax.broadcasted_iota(jnp.int32, (S, POOL_W), 1)
        p, cnt = _pool_p(u_ref[...], row, lane)
        pb = p.astype(BF16)
        y = jnp.dot(pb, w_ref[...], preferred_element_type=F32)
        dbr = dbr_ref[...]
        red_ref[...] = jnp.zeros_like(red_ref)
        red_ref[0:1, :] = jnp.sum(dbr * y, axis=0, keepdims=True)
        dy = (dbr * s_ref[0:1, :]).astype(BF16)
        dw_ref[...] = lax.dot_general(pb, dy, (((0,), (0,)), ((), ())), preferred_element_type=F32)
        dp = lax.dot_general(dy, w_ref[...], (((1,), (1,)), ((), ())), preferred_element_type=F32)
        g = dp / cnt
        a2 = g + _shift_up(g, 1, row)
        a4 = a2 + _shift_up(a2, 2, row)
        a8 = a4 + _shift_up(a4, 4, row)
        a16 = a8 + _shift_up(a8, 8, row)
        du_ref[...] = (_pool_window_select(lane, a2, a4, a8, a16) - dp).astype(BF16)

    return pl.pallas_call(
        body, name=name,
        out_shape=(jax.ShapeDtypeStruct((S, POOL_W), BF16), jax.ShapeDtypeStruct((POOL_W, POOL_W), F32),
                   jax.ShapeDtypeStruct((8, POOL_W), F32)),
        grid=(1,),
        in_specs=[_full_spec((S, POOL_W), (0, Z_PC // POOL_W)), _full_spec((POOL_W, POOL_W)), _full_spec((8, POOL_W)),
                  _full_spec((S, POOL_W))],
        out_specs=(_full_spec((S, POOL_W)), _full_spec((POOL_W, POOL_W)), _full_spec((8, POOL_W))),
        compiler_params=_params(("arbitrary",)),
    )(z, wp_bd, pscale, dbr)


def _conv_specs():
    base = Z_PC // CONV_W
    return [_full_spec((S, CONV_W), (0, base + 1)), _full_spec((S, CONV_W), (0, base + 2)),
            _full_spec((S, CONV_W), (0, base + 3)), _full_spec((8, CONV_W))]


def _conv_fwd(z, cw, name):
    def body(h_ref, b_ref, c_ref, w_ref, o_ref):
        row = lax.broadcasted_iota(jnp.int32, (S, CONV_W), 0)
        u = c_ref[...] * h_ref[...]
        y = (w_ref[0:1, :] * _shift_down(u, 2, row) + w_ref[1:2, :] * _shift_down(u, 1, row) + w_ref[2:3, :] * u)
        o_ref[...] = b_ref[...] * y

    return pl.pallas_call(
        body, name=name, out_shape=jax.ShapeDtypeStruct((S, CONV_W), F32), grid=(1,),
        in_specs=_conv_specs(), out_specs=_full_spec((S, CONV_W)),
        compiler_params=_params(("arbitrary",)),
    )(z, z, z, cw)


def _conv_bwd(z, cw, dbr, name):
    def body(h_ref, b_ref, c_ref, w_ref, dbr_ref, d_ref, red_ref):
        row = lax.broadcasted_iota(jnp.int32, (S, CONV_W), 0)
        h, cg = h_ref[...], c_ref[...]
        u = cg * h
        u1 = _shift_down(u, 1, row)
        u2 = _shift_down(u, 2, row)
        y = w_ref[0:1, :] * u2 + w_ref[1:2, :] * u1 + w_ref[2:3, :] * u
        dbr = dbr_ref[...]
        dy = dbr * b_ref[...]
        du = w_ref[2:3, :] * dy + w_ref[1:2, :] * _shift_up(dy, 1, row) + w_ref[0:1, :] * _shift_up(dy, 2, row)
        d_ref[:, 0:CONV_W] = (du * cg).astype(BF16)
        d_ref[:, CONV_W:2 * CONV_W] = (dbr * y).astype(BF16)
        d_ref[:, 2 * CONV_W:3 * CONV_W] = (du * h).astype(BF16)
        red_ref[...] = jnp.zeros_like(red_ref)
        red_ref[0:1, :] = jnp.sum(dy * u2, axis=0, keepdims=True)
        red_ref[1:2, :] = jnp.sum(dy * u1, axis=0, keepdims=True)
        red_ref[2:3, :] = jnp.sum(dy * u, axis=0, keepdims=True)

    return pl.pallas_call(
        body, name=name,
        out_shape=(jax.ShapeDtypeStruct((S, 3 * CONV_W), BF16), jax.ShapeDtypeStruct((8, CONV_W), F32)),
        grid=(1,),
        in_specs=_conv_specs() + [_full_spec((S, CONV_W))],
        out_specs=(_full_spec((S, 3 * CONV_W)), _full_spec((8, CONV_W))),
        compiler_params=_params(("arbitrary",)),
    )(z, z, z, cw, dbr)


_NT = (((1,), (1,)), ((), ()))
_TN = (((0,), (0,)), ((), ()))
N_HEAD = 2 * N_PAIR


def _split3(x):
    hi = x.astype(BF16).astype(F32)
    mid = (x - hi).astype(BF16).astype(F32)
    lo = (x - hi - mid).astype(BF16).astype(F32)
    return hi, mid, lo


def _spare(lane, e, k):
    return lane == 64 * (1 - e) + k


def _spare3(lane, e, k):
    base = 64 * (1 - e) + k
    return (lane >= base) & (lane < base + 3)


def _put3(lane, e, k, pieces, rest):
    out = rest
    for n, piece in enumerate(pieces):
        out = jnp.where(_spare(lane, e, k + n), piece, out)
    return out


def _attn_prep(z, bf, name):
    def body(q_ref, k_ref, v_ref, f_ref, b_ref, qa_ref, ka_ref, va_ref, kat_ref):
        p = pl.program_id(0)
        row = lax.broadcasted_iota(jnp.int32, (S, LANE), 0)
        lane = lax.broadcasted_iota(jnp.int32, (S, LANE), 1)
        xv = f_ref[...] + b_ref[0:1, :]
        ls = jnp.minimum(xv, 0.0) - jnp.log(1.0 + jnp.exp(-jnp.abs(xv)))
        cum = _cumsum_rows(jnp.where(lane < N_HEAD, ls, 0.0), row)
        q, k, v = q_ref[...], k_ref[...], v_ref[...]
        for e in range(2):
            head = (lane >= 64) if e else (lane < 64)
            f = jnp.sum(jnp.where(lane == 2 * p + e, cum, 0.0), axis=1, keepdims=True)
            pieces = _split3(f)
            qa = jnp.where(head, q * ATT_SCALE, _put3(lane, e, 0, pieces, jnp.where(_spare3(lane, e, 3), 1.0, 0.0)))
            ones = jnp.where(_spare3(lane, e, 0) | _spare3(lane, e, 6), 1.0, 0.0)
            ka = jnp.where(head, k, _put3(lane, e, 3, [-x for x in pieces], ones))
            va = jnp.where(head, v, jnp.where(_spare3(lane, e, 0), 1.0, 0.0))
            qa_ref[e] = qa.astype(BF16)
            ka_ref[e] = ka.astype(BF16)
            va_ref[e] = va.astype(BF16)
            kat_ref[e] = ka.T.astype(BF16)

    qb, kb, vb = Z_Q // LANE, Z_K // LANE, Z_V // LANE
    heads = jax.ShapeDtypeStruct((N_HEAD, S, LANE), BF16)
    pair = pl.BlockSpec((2, S, LANE), lambda p: (p, 0, 0))
    return pl.pallas_call(
        body, name=name,
        out_shape=(heads, heads, heads, jax.ShapeDtypeStruct((N_HEAD, LANE, S), BF16)),
        grid=(N_PAIR,),
        in_specs=[pl.BlockSpec((S, LANE), lambda p: (0, qb + p)), pl.BlockSpec((S, LANE), lambda p: (0, kb + p)),
                  pl.BlockSpec((S, LANE), lambda p: (0, vb + p)), pl.BlockSpec((S, LANE), lambda p: (0, Z_F // LANE)),
                  pl.BlockSpec((8, LANE), lambda p: (0, 0))],
        out_specs=(pair, pair, pair, pl.BlockSpec((2, LANE, S), lambda p: (p, 0, 0))),
        compiler_params=_params(("parallel",)),
    )(z, z, z, z, bf)


def _attn_bwd_prep(qa, o, lse, do, name):
    def body(qa_ref, o_ref, lse_ref, do_ref, qa2_ref, doa_ref):
        lane = lax.broadcasted_iota(jnp.int32, (S, LANE), 1)
        dov, ov, lsev = do_ref[...], o_ref[...], lse_ref[...]
        for e in range(2):
            head = (lane >= 64) if e else (lane < 64)
            dsum = jnp.sum(jnp.where(head, dov * ov, 0.0), axis=1, keepdims=True)
            doa_ref[e] = jnp.where(head, dov, _put3(lane, e, 0, [-x for x in _split3(dsum)], 0.0)).astype(BF16)
            lse_col = lsev[:, 64 * e:64 * e + 1]
            qa2_ref[e] = _put3(lane, e, 6, [-x for x in _split3(lse_col)], qa_ref[e].astype(F32)).astype(BF16)

    heads = jax.ShapeDtypeStruct((N_HEAD, S, LANE), BF16)
    pair = pl.BlockSpec((2, S, LANE), lambda p: (p, 0, 0))
    cols = pl.BlockSpec((S, LANE), lambda p: (0, p))
    return pl.pallas_call(
        body, name=name, out_shape=(heads, heads), grid=(N_PAIR,),
        in_specs=[pair, cols, cols, cols], out_specs=(pair, pair),
        compiler_params=_params(("parallel",)),
    )(qa, o, lse, do)


def _attn_bwd_post(z, bf, dqt, dka, dva, name):
    def body(f_ref, b_ref, dqt_ref, dk_ref, dv_ref, dq_out, dk_out, dv_out, dfl_ref, red_ref, dcum_ref):
        p = pl.program_id(0)

        @pl.when(p == 0)
        def _():
            dcum_ref[...] = jnp.zeros_like(dcum_ref)

        row = lax.broadcasted_iota(jnp.int32, (S, LANE), 0)
        lane = lax.broadcasted_iota(jnp.int32, (S, LANE), 1)
        dqa = [dqt_ref[e].T for e in range(2)]
        dq_out[...] = (jnp.where(lane < 64, dqa[0], dqa[1]) * ATT_SCALE).astype(BF16)
        dk_out[...] = jnp.where(lane < 64, dk_ref[0], dk_ref[1]).astype(BF16)
        dv_out[...] = jnp.where(lane < 64, dv_ref[0], dv_ref[1]).astype(BF16)
        for e in range(2):
            d_query = jnp.sum(jnp.where(_spare(lane, e, 0), dqa[e], 0.0), axis=1, keepdims=True)
            d_key = jnp.sum(jnp.where(_spare(lane, e, 3), dk_ref[e], 0.0), axis=1, keepdims=True)
            dcum_ref[...] += jnp.where(lane == 2 * p + e, d_query - d_key, 0.0)

        @pl.when(p == N_PAIR - 1)
        def _():
            dls = _cumsum_rows(dcum_ref[...], row, reverse=True)
            xv = f_ref[...] + b_ref[0:1, :]
            dx = jnp.where(lane < N_HEAD, dls * jax.nn.sigmoid(-xv), 0.0)
            dfl_ref[...] = dx.astype(BF16)
            red_ref[...] = jnp.zeros_like(red_ref)
            red_ref[0:1, :] = jnp.sum(dx, axis=0, keepdims=True)

    wide = jax.ShapeDtypeStruct((S, N_PAIR * LANE), BF16)
    cols = pl.BlockSpec((S, LANE), lambda p: (0, p))
    pair = pl.BlockSpec((2, S, LANE), lambda p: (p, 0, 0))
    return pl.pallas_call(
        body, name=name,
        out_shape=(wide, wide, wide, jax.ShapeDtypeStruct((S, LANE), BF16), jax.ShapeDtypeStruct((8, LANE), F32)),
        grid=(N_PAIR,),
        in_specs=[pl.BlockSpec((S, LANE), lambda p: (0, Z_F // LANE)), pl.BlockSpec((8, LANE), lambda p: (0, 0)),
                  pl.BlockSpec((2, LANE, S), lambda p: (p, 0, 0)), pair, pair],
        out_specs=(cols, cols, cols, pl.BlockSpec((S, LANE), lambda p: (0, 0)), pl.BlockSpec((8, LANE), lambda p: (0, 0))),
        scratch_shapes=[pltpu.VMEM((S, LANE), F32)],
        compiler_params=_params(("arbitrary",)),
    )(z, bf, dqt, dka, dva)


def _attn_fwd(qa, ka, va, name):
    tq, tk = TQ_FWD, TQ
    ratio = tq // tk

    def body(qa_ref, ka_ref, va_ref, o_ref, lse_ref):
        i = pl.program_id(1)
        lane = lax.broadcasted_iota(jnp.int32, (tq, LANE), 1)
        row = lax.broadcasted_iota(jnp.int32, (tq, tk), 0)
        col = lax.broadcasted_iota(jnp.int32, (tq, tk), 1)
        nh = HEADS_PER_STEP
        qs = [qa_ref[h] for h in range(nh)]

        def block(j, carry, masked):
            off = pl.multiple_of(j * tk, tk)
            out = []
            for h in range(nh):
                m, acc = carry[h]
                s = lax.dot_general(qs[h], ka_ref[h, pl.ds(off, tk), :], _NT, preferred_element_type=F32)
                if masked:
                    s = jnp.where(col + (j - ratio * i) * tk > row, NEG_INF, s)
                mn = jnp.maximum(m, jnp.max(s, axis=1, keepdims=True))
                p = jnp.exp(s - mn).astype(BF16)
                acc = jnp.exp(m - mn) * acc + jnp.dot(p, va_ref[h, pl.ds(off, tk), :], preferred_element_type=F32)
                out.append((mn, acc))
            return tuple(out)

        init = (jnp.full((tq, 1), NEG_INF, F32), jnp.zeros((tq, LANE), F32))
        carry = lax.fori_loop(0, ratio * i, lambda j, c: block(j, c, False), (init,) * nh)
        for d in range(ratio):
            carry = block(ratio * i + d, carry, True)
        res = []
        for h in range(nh):
            m, acc = carry[h]
            l = jnp.sum(jnp.where(_spare(lane, h % 2, 0), acc, 0.0), axis=1, keepdims=True)
            res.append((acc / l, m + jnp.log(l)))
        for g in range(nh // 2):
            o_ref[:, g * LANE:(g + 1) * LANE] = jnp.where(lane < 64, res[2 * g][0], res[2 * g + 1][0])
            lse_ref[:, g * LANE:(g + 1) * LANE] = jnp.where(lane < 64, res[2 * g][1], res[2 * g + 1][1])

    nh = HEADS_PER_STEP
    out = jax.ShapeDtypeStruct((S, N_PAIR * LANE), F32)
    wide = pl.BlockSpec((tq, 64 * nh), lambda p, i: (i, p))
    return pl.pallas_call(
        body, name=name, out_shape=(out, out), grid=(N_HEAD // nh, S // tq),
        in_specs=[pl.BlockSpec((nh, tq, LANE), lambda p, i: (p, i, 0)), pl.BlockSpec((nh, S, LANE), lambda p, i: (p, 0, 0)),
                  pl.BlockSpec((nh, S, LANE), lambda p, i: (p, 0, 0))],
        out_specs=(wide, wide),
        compiler_params=_params(("parallel", "parallel")),
    )(qa, ka, va)


def _attn_bwd(qa2, ka, va, kat, doa, name):
    nq = S // TQ

    def body(qa_ref, ka_ref, va_ref, kat_ref, doa_ref, dqt_ref, dk_ref, dv_ref):
        j = pl.program_id(1)

        @pl.when(j == 0)
        def _():
            dqt_ref[...] = jnp.zeros_like(dqt_ref)

        key = lax.broadcasted_iota(jnp.int32, (TQ, TQ), 0)
        qry = lax.broadcasted_iota(jnp.int32, (TQ, TQ), 1)
        nh = HEADS_PER_STEP
        kav, vav, katv = ([ref[h] for h in range(nh)] for ref in (ka_ref, va_ref, kat_ref))

        def block(i, carry, masked):
            off = pl.multiple_of(i * TQ, TQ)
            out = []
            for h in range(nh):
                dk_acc, dv_acc = carry[h]
                qav = qa_ref[h, pl.ds(off, TQ), :]
                doav = doa_ref[h, pl.ds(off, TQ), :]
                s_t = lax.dot_general(kav[h], qav, _NT, preferred_element_type=F32)
                if masked:
                    s_t = jnp.where(key > qry, NEG_INF, s_t)
                p_t = jnp.exp(s_t)
                ds_t = p_t * lax.dot_general(vav[h], doav, _NT, preferred_element_type=F32)
                dsb = ds_t.astype(BF16)
                dv_acc = dv_acc + jnp.dot(p_t.astype(BF16), doav, preferred_element_type=F32)
                dk_acc = dk_acc + jnp.dot(dsb, qav, preferred_element_type=F32)
                dqt_ref[h, :, pl.ds(off, TQ)] += jnp.dot(katv[h], dsb, preferred_element_type=F32)
                out.append((dk_acc, dv_acc))
            return tuple(out)

        zero = (jnp.zeros((TQ, LANE), F32), jnp.zeros((TQ, LANE), F32))
        carry = block(j, (zero,) * nh, True)
        carry = lax.fori_loop(j + 1, nq, lambda i, c: block(i, c, False), carry)
        for h in range(nh):
            dk_ref[h], dv_ref[h] = carry[h]

    nh = HEADS_PER_STEP
    full = pl.BlockSpec((nh, S, LANE), lambda p, j: (p, 0, 0))
    blk = pl.BlockSpec((nh, TQ, LANE), lambda p, j: (p, j, 0))
    acc = jax.ShapeDtypeStruct((N_HEAD, S, LANE), F32)
    return pl.pallas_call(
        body, name=name,
        out_shape=(jax.ShapeDtypeStruct((N_HEAD, LANE, S), F32), acc, acc),
        grid=(N_HEAD // nh, nq),
        in_specs=[full, blk, blk, pl.BlockSpec((nh, LANE, TQ), lambda p, j: (p, 0, j)), full],
        out_specs=(pl.BlockSpec((nh, LANE, S), lambda p, j: (p, 0, 0)), blk, blk),
        compiler_params=_params(("arbitrary", "arbitrary")),
    )(qa2, ka, va, kat, doa)


ADA_ROWS = 16


def _ada_fwd(c_pad, w_ada, b_cols, name):
    def body(c_ref, w_ref, b_ref, o_ref):
        cv = c_ref[...]
        sc = (cv * jax.nn.sigmoid(cv)).astype(BF16)
        o_ref[0] = jnp.dot(sc, w_ref[0].astype(BF16), preferred_element_type=F32) + b_ref[0, 0:1, :]

    return pl.pallas_call(
        body, name=name, out_shape=jax.ShapeDtypeStruct((DEPTH, ADA_ROWS, ADA_COLS), F32), grid=(DEPTH,),
        in_specs=[pl.BlockSpec((ADA_ROWS, D), lambda l: (0, 0)), pl.BlockSpec((1, D, ADA_COLS), lambda l: (l, 0, 0)),
                  pl.BlockSpec((1, 8, ADA_COLS), lambda l: (l, 0, 0))],
        out_specs=pl.BlockSpec((1, ADA_ROWS, ADA_COLS), lambda l: (l, 0, 0)),
        compiler_params=_params(("parallel",)),
    )(c_pad, w_ada, b_cols)


def _ada_bwd(c_pad, dmod_cols, name):
    def body(c_ref, d_ref, o_ref):
        cv = c_ref[...]
        sc = (cv * jax.nn.sigmoid(cv)).astype(BF16)
        o_ref[0] = lax.dot_general(sc, d_ref[0].astype(BF16), _TN, preferred_element_type=F32)

    return pl.pallas_call(
        body, name=name, out_shape=jax.ShapeDtypeStruct((DEPTH, D, ADA_COLS), F32), grid=(DEPTH,),
        in_specs=[pl.BlockSpec((ADA_ROWS, D), lambda l: (0, 0)), pl.BlockSpec((1, ADA_ROWS, ADA_COLS), lambda l: (l, 0, 0))],
        out_specs=pl.BlockSpec((1, D, ADA_COLS), lambda l: (l, 0, 0)),
        compiler_params=_params(("parallel",)),
    )(c_pad, dmod_cols)


def _adamw_math(w, g, m, v):
    m = B1 * m + (1.0 - B1) * g
    v = B2 * v + (1.0 - B2) * (g * g)
    m_hat = m / (1.0 - B1 ** STEP)
    v_hat = v / (1.0 - B2 ** STEP)
    delta = -LR * (m_hat / (jnp.sqrt(v_hat) + EPS) + WD * w)
    return delta, m, v


def _row_tile(rows, target=256):
    best = 8
    for t in range(8, min(rows, target) + 1, 8):
        if rows % t == 0:
            best = t
    return best


def _adamw(w, g, m, v, name):
    layers, rows, cols = w.shape
    tr = _row_tile(rows)
    spec = pl.BlockSpec((1, tr, cols), lambda l, i: (l, i, 0))

    def body(w_ref, g_ref, m_ref, v_ref, d_ref, nm_ref, nv_ref):
        d_ref[...], nm_ref[...], nv_ref[...] = _adamw_math(w_ref[...], g_ref[...], m_ref[...], v_ref[...])

    out = jax.ShapeDtypeStruct(w.shape, F32)
    return pl.pallas_call(
        body, name=name, out_shape=(out, out, out), grid=(layers, rows // tr),
        in_specs=[spec] * 4, out_specs=(spec,) * 3, compiler_params=_params(("parallel", "parallel")),
    )(w, g, m, v)


def _sum_slabs(x, name):
    n, rows, _ = x.shape
    tr = _row_tile(rows)

    def body(x_ref, o_ref):
        acc = x_ref[0]
        for k in range(1, n):
            acc = acc + x_ref[k]
        o_ref[...] = acc

    return pl.pallas_call(
        body, name=name, out_shape=jax.ShapeDtypeStruct((rows, D), F32), grid=(rows // tr,),
        in_specs=[pl.BlockSpec((n, tr, D), lambda i: (0, i, 0))], out_specs=pl.BlockSpec((tr, D), lambda i: (i, 0)),
        compiler_params=_params(("parallel",)),
    )(x)


_ANY = pl.BlockSpec(memory_space=pl.ANY)
MESH = pl.DeviceIdType.MESH


def _on_sequencer(body, out_shape, sems, operands, after, sequencer_id, name):
    n = len(operands)

    def ordered_body(*refs):
        body(*refs[:n], *refs[n + 1:])

    extra = [] if after is None else [after]
    return pl.kernel(
        body if after is None else ordered_body, out_type=out_shape,
        mesh=plsc.ScalarSubcoreMesh(axis_name="sequencer", num_cores=1), scratch_types=sems,
        compiler_params=pltpu.CompilerParams(collective_id=sequencer_id), name=name)(*operands, *extra)


def _all_gather(xs, name, sequencer_id=None, after=None):
    n = len(xs)

    def body(*refs):
        x_refs, out_refs = refs[:n], refs[n:2 * n]
        send_sems, recv_sems, local_sems = refs[2 * n:]
        x_, y_, c_ = lax.axis_index("x"), lax.axis_index("y"), lax.axis_index("c")
        me, sibling = (x_, y_, c_), (x_, y_, 1 - c_)
        chips = [(1 - x_, y_), (x_, 1 - y_), (1 - x_, 1 - y_)]
        if sequencer_id is not None:
            barrier = pltpu.get_barrier_semaphore()
            peers = [sibling] + [(*chip, pc) for chip in chips for pc in (c_, 1 - c_)]
            for peer in peers:
                pl.semaphore_signal(barrier, inc=1, device_id=peer, device_id_type=MESH)
            pl.semaphore_wait(barrier, len(peers))

        def slot(a, px, py, pc):
            return out_refs[a].at[4 * px + 2 * py + pc]

        def copy(a, k, block, to, src=None):
            return pltpu.make_async_remote_copy(
                src_ref=slot(a, *block) if src is None else src, dst_ref=slot(a, *block),
                send_sem=send_sems.at[7 * a + k], recv_sem=recv_sems.at[7 * a + k], device_id=to, device_id_type=MESH)

        mine = [pltpu.make_async_copy(x_refs[a], slot(a, *me), local_sems.at[a]) for a in range(n)]
        for cp in mine:
            cp.start()
        first = []
        for a in range(n):
            first.append(copy(a, 0, me, sibling, src=x_refs[a]))
            first += [copy(a, 1 + j, me, (*chip, c_), src=x_refs[a]) for j, chip in enumerate(chips)]
        for cp in first:
            cp.start()
        passed = []
        for j, chip in enumerate(chips):
            for a in range(n):
                copy(a, 1 + j, (*chip, c_), me).wait_recv()
                passed.append(copy(a, 4 + j, (*chip, c_), sibling))
                passed[-1].start()
        for a in range(n):
            copy(a, 0, sibling, me).wait_recv()
        for j, chip in enumerate(chips):
            for a in range(n):
                copy(a, 4 + j, (*chip, 1 - c_), me).wait_recv()
        for cp in first + passed:
            cp.wait_send()
        for cp in mine:
            cp.wait()

    out_shape = [jax.ShapeDtypeStruct((N_DEV,) + x.shape, x.dtype) for x in xs]
    sems = [pltpu.SemaphoreType.DMA((7 * n,)), pltpu.SemaphoreType.DMA((7 * n,)), pltpu.SemaphoreType.DMA((n,))]
    if sequencer_id is not None:
        return _on_sequencer(body, out_shape, sems, xs, after, sequencer_id, name)
    return pl.pallas_call(
        body, name=name, out_shape=out_shape, in_specs=[_ANY] * n, out_specs=[_ANY] * n, scratch_shapes=sems)(*xs)


def _sibling_exchange(gs, name, sequencer_id=None, after=None):
    n = len(gs)

    def body(*refs):
        g_refs, p_refs = refs[:n], refs[n:2 * n]
        send_sems, recv_sems = refs[2 * n:]
        x_, y_, c_ = lax.axis_index("x"), lax.axis_index("y"), lax.axis_index("c")
        if sequencer_id is not None:
            barrier = pltpu.get_barrier_semaphore()
            pl.semaphore_signal(barrier, inc=1, device_id=(x_, y_, 1 - c_), device_id_type=MESH)
            pl.semaphore_wait(barrier, 1)
        copies = [pltpu.make_async_remote_copy(
            src_ref=g_refs[a].at[2 * k + (1 - c_)], dst_ref=p_refs[a].at[k], send_sem=send_sems.at[4 * a + k],
            recv_sem=recv_sems.at[4 * a + k], device_id=(x_, y_, 1 - c_), device_id_type=MESH)
            for a in range(n) for k in range(4)]
        for cp in copies:
            cp.start()
        for cp in copies:
            cp.wait()

    out_shape = [jax.ShapeDtypeStruct((4,) + g.shape[1:], g.dtype) for g in gs]
    sems = [pltpu.SemaphoreType.DMA((4 * n,)), pltpu.SemaphoreType.DMA((4 * n,))]
    if sequencer_id is not None:
        return _on_sequencer(body, out_shape, sems, gs, after, sequencer_id, name)
    return pl.pallas_call(
        body, name=name, out_shape=out_shape, in_specs=[_ANY] * n, out_specs=[_ANY] * n, scratch_shapes=sems)(*gs)


def _slab_tiles(rows, cols):
    if rows % 8 == 0:
        return _row_tile(rows), cols
    return rows, 2 * LANE


def _pair_sums(g, p, route, name):
    _, rows, cols = g.shape
    tr, tc = _slab_tiles(rows, cols)

    def body(route_ref, g_ref, p_ref, t_ref):
        t_ref[...] = (g_ref[...] + p_ref[...]).astype(BF16)

    return pl.pallas_call(
        body, name=name, out_shape=jax.ShapeDtypeStruct((3, rows, cols), BF16),
        grid_spec=pltpu.PrefetchScalarGridSpec(
            num_scalar_prefetch=1, grid=(3, rows // tr, cols // tc),
            in_specs=[pl.BlockSpec((1, tr, tc), lambda r, i, j, route_ref: (2 * route_ref[1 + r] + route_ref[0], i, j)),
                      pl.BlockSpec((1, tr, tc), lambda r, i, j, route_ref: (route_ref[1 + r], i, j))],
            out_specs=pl.BlockSpec((1, tr, tc), lambda r, i, j, route_ref: (r, i, j))),
        compiler_params=_params(("parallel", "parallel", "parallel")),
    )(route, g, p)


def _chip_exchange(ts, name, sequencer_id=None, after=None):
    n = len(ts)

    def body(*refs):
        t_refs, l_refs = refs[:n], refs[n:2 * n]
        send_sems, recv_sems = refs[2 * n:]
        x_, y_, c_ = lax.axis_index("x"), lax.axis_index("y"), lax.axis_index("c")
        chips = [(1 - x_, y_), (x_, 1 - y_), (1 - x_, 1 - y_)]
        if sequencer_id is not None:
            barrier = pltpu.get_barrier_semaphore()
            for px, py in chips:
                pl.semaphore_signal(barrier, inc=1, device_id=(px, py, c_), device_id_type=MESH)
            pl.semaphore_wait(barrier, len(chips))
        copies = [pltpu.make_async_remote_copy(
            src_ref=t_refs[a].at[r], dst_ref=l_refs[a].at[r], send_sem=send_sems.at[3 * a + r],
            recv_sem=recv_sems.at[3 * a + r], device_id=(px, py, c_), device_id_type=MESH)
            for a in range(n) for r, (px, py) in enumerate(chips)]
        for cp in copies:
            cp.start()
        for cp in copies:
            cp.wait()

    out_shape = [jax.ShapeDtypeStruct((3,) + t.shape[1:], t.dtype) for t in ts]
    sems = [pltpu.SemaphoreType.DMA((3 * n,)), pltpu.SemaphoreType.DMA((3 * n,))]
    if sequencer_id is not None:
        return _on_sequencer(body, out_shape, sems, ts, after, sequencer_id, name)
    return pl.pallas_call(
        body, name=name, out_shape=out_shape, in_specs=[_ANY] * n, out_specs=[_ANY] * n, scratch_shapes=sems)(*ts)


def _reduce_adamw(gs, ps, landed, place, w, m, v, name):
    layers, rows, cols = w.shape
    assert layers == DEPTH == 2
    tr, tc = _slab_tiles(rows, cols)
    nr, nc = rows // tr, cols // tc
    spec = pl.BlockSpec((1, tr, tc), lambda l, i, j, place_ref: (l, i, j))

    def own(layer, which):
        pi, pj = (nr - 1, nc - 1) if layer == 0 else (0, 0)

        def index(l, i, j, place_ref):
            lead = 0 if which is None else place_ref[which]
            return lead, jnp.where(l == layer, i, pi), jnp.where(l == layer, j, pj)

        return pl.BlockSpec((3 if which is None else 1, tr, tc), index)

    def body(place_ref, g0_ref, p0_ref, l0_ref, g1_ref, p1_ref, l1_ref, w_ref, m_ref, v_ref,
             g_ref, d_ref, nm_ref, nv_ref):
        def update(own_ref, sib_ref, l_ref):
            g = own_ref[0] + sib_ref[0] + l_ref[0].astype(F32) + l_ref[1].astype(F32) + l_ref[2].astype(F32)
            g_ref[0] = g
            d_ref[0], nm_ref[0], nv_ref[0] = _adamw_math(w_ref[0], g, m_ref[0], v_ref[0])

        @pl.when(pl.program_id(0) == 0)
        def _():
            update(g0_ref, p0_ref, l0_ref)

        @pl.when(pl.program_id(0) == 1)
        def _():
            update(g1_ref, p1_ref, l1_ref)

    out = jax.ShapeDtypeStruct(w.shape, F32)
    return pl.pallas_call(
        body, name=name, out_shape=(out, out, out, out),
        grid_spec=pltpu.PrefetchScalarGridSpec(
            num_scalar_prefetch=1, grid=(DEPTH, nr, nc),
            in_specs=[own(0, 0), own(0, 1), own(0, None), own(1, 0), own(1, 1), own(1, None), spec, spec, spec],
            out_specs=(spec, spec, spec, spec)),
        compiler_params=_params(("arbitrary", "arbitrary", "arbitrary")),
    )(place, gs[0], ps[0], landed[0], gs[1], ps[1], landed[1], w, m, v)


def _pack(pieces, row_multiple, dtype, cols=D, rows=None):
    flat = jnp.concatenate([p.astype(dtype).reshape(-1) for p in pieces])
    if rows is None:
        rows = -(-flat.shape[0] // cols)
        rows = -(-rows // row_multiple) * row_multiple
    flat = jnp.pad(flat, (0, rows * cols - flat.shape[0]))
    return flat.reshape(rows, cols)


def _unpack(flat, shapes, lead=()):
    out, off = [], 0
    for shp in shapes:
        n = 1
        for s_ in shp:
            n *= s_
        out.append(lax.slice_in_dim(flat, off, off + n, axis=len(lead)).reshape(lead + tuple(shp)))
        off += n
    return out


WIN_STRIDE = 704
WIN_ROWS = 720
Z_TURN = 1544


def _window(wt, me, name):
    padded = jnp.pad(wt, ((0, 0), (0, WIN_ROWS - IN_SHARD), (0, 0)))

    def body(me_ref, x_ref, o_ref):
        o_ref[0] = pltpu.roll(x_ref[0], me_ref[0], axis=0).astype(BF16)

    spec = pl.BlockSpec((1, WIN_ROWS, D), lambda l, me_ref: (l, 0, 0))
    return pl.pallas_call(
        body, name=name, out_shape=jax.ShapeDtypeStruct((DEPTH, WIN_ROWS, D), BF16),
        grid_spec=pltpu.PrefetchScalarGridSpec(num_scalar_prefetch=1, grid=(DEPTH,), in_specs=[spec], out_specs=spec),
        compiler_params=_params(("parallel",)),
    )(me, padded)


def _z_rows_from_windows(win):
    over = WIN_ROWS - WIN_STRIDE
    pieces = [(0, win[0][0:WIN_STRIDE])]
    for d in range(1, N_DEV):
        base = WIN_STRIDE * d
        pieces.append((base, win[d - 1][WIN_STRIDE:WIN_ROWS] + win[d][0:over]))
        pieces.append((base + over, win[d][over:WIN_STRIDE]))
    pieces.append((WIN_STRIDE * N_DEV, win[N_DEV - 1][WIN_STRIDE:WIN_ROWS]))

    def rows(a, b):
        out = []
        for start, arr in pieces:
            lo, hi = max(a, start), min(b, start + arr.shape[0])
            if lo < hi:
                out.append(arr[lo - start:hi - start])
        return out

    pad = jnp.zeros((NZ - IN_COLS, win.shape[-1]), win.dtype)
    return jnp.concatenate(rows(Z_TURN, IN_COLS) + rows(0, Z_TURN) + [pad], axis=0)


def _in_rows_from_z(wt):
    return jnp.concatenate([wt[Z_Q:Z_Q + 1536], wt[Z_F:Z_F + 8], wt[Z_PC:Z_PC + 1024], wt[Z_G:Z_G + 3072]], axis=0)


def _pad_rows(v, rows=8):
    return jnp.pad(v, ((0, rows - v.shape[0]), (0, 0)))


def _layer_fwd(l, x, wts, gvec, mod):
    tag = f"l{l}"
    h = _prenorm_fwd(x, gvec, mod, 0, 0, 1, f"prenorm_mix_{tag}")
    z = _matmul(h, wts["w_in_t"], "nt", f"in_proj_{tag}", tn=1152)
    qa, ka, va, kat = _attn_prep(z, wts["b_f"], f"attn_prep_{tag}")
    qa = wts["arrive"](qa)
    o, lse = _attn_fwd(qa, ka, va, f"attn_{tag}")
    br_b = _pool_fwd(z, wts["wp_bd"], wts["pool_scale"], f"pool_{tag}")
    br_c = _conv_fwd(z, wts["conv_w"], f"conv_{tag}")
    pa = _matmul(o, wts["wa"], "nn", f"proj_a_{tag}")
    pb = _matmul(br_b, wts["wb"], "nn", f"proj_b_{tag}")
    gates = [(z, Z_G + k * D) for k in range(3)]
    pc, merged = _matmul(br_c, wts["wc"], "nn", f"proj_c_merge_{tag}", tm=512, tn=512,
                         extra=gates + [(pa, 0), (pb, 0)], epilogue=_merge_epilogue, out_dtypes=(F32, BF16))
    y = _matmul(merged, wts["w_out"], "nn", f"out_proj_{tag}")
    x1 = _postnorm_fwd(x, y, gvec, mod, 1, 2, f"postnorm_mix_{tag}")
    h2 = _prenorm_fwd(x1, gvec, mod, 2, 3, 4, f"prenorm_ff_{tag}")
    a, r = _matmul(h2, wts["w_ff1"], "nn", f"ff1_{tag}", b_col_shards=True, epilogue=_relu2_epilogue,
                   out_dtypes=(F32, BF16))
    y2 = _matmul(r, wts["w_ff2"], "nn", f"ff2_{tag}")
    x2 = _postnorm_fwd(x1, y2, gvec, mod, 3, 5, f"postnorm_ff_{tag}")
    saved = dict(x=x, h=h, z=z, qa=qa, ka=ka, va=va, kat=kat, o=o, lse=lse, br_b=br_b, br_c=br_c, pa=pa, pb=pb, pc=pc,
                 merged=merged, y=y, x1=x1, h2=h2, a=a, r=r, y2=y2)
    return x2, saved


def _ffn_bwd(l, dx2, sv, wts, gvec, mod, midpoint):
    tag = f"l{l}"
    dy2, red_post_ff = _postnorm_bwd(sv["y2"], gvec, mod, dx2, 3, 5, f"postnorm_ff_bwd_{tag}")
    dy2 = midpoint(dy2)
    da = _matmul(dy2, wts["w_ff2"], "nt", f"ff2_dx_{tag}", extra=[(sv["a"], 0)], epilogue=_relu2_bwd_epilogue,
                 out_dtypes=(BF16,))[0]
    d_w_ff2 = _matmul(sv["r"], dy2, "tn", f"ff2_dw_{tag}")
    dh2 = _matmul(da, wts["w_ff1"], "nt", f"ff1_dx_{tag}", b_col_shards=True)
    d_w_ff1 = _matmul(sv["h2"], da, "tn", f"ff1_dw_{tag}", out_col_shards=True)
    dx1, red_pre_ff = _prenorm_bwd(sv["x1"], gvec, mod, dh2, dx2, 2, 4, f"prenorm_ff_bwd_{tag}")
    return dx1, [d_w_ff1, d_w_ff2.reshape(N_DEV, D_FF // N_DEV, D)], (red_pre_ff, red_post_ff)


def _mixer_bwd(l, dx1, sv, wts, gvec, mod, ffn_reds, midpoint):
    tag = f"l{l}"
    red_pre_ff, red_post_ff = ffn_reds
    dy, red_post_mix = _postnorm_bwd(sv["y"], gvec, mod, dx1, 1, 2, f"postnorm_mix_bwd_{tag}")
    gates = [(sv["z"], Z_G + k * D) for k in range(3)]
    dpa, dpb, dpc, *dgl = _matmul(dy, wts["w_out"], "nt", f"out_proj_dx_{tag}", tm=512, tn=512,
                                  extra=gates + [(sv["pa"], 0), (sv["pb"], 0), (sv["pc"], 0)],
                                  epilogue=_merge_bwd_epilogue, out_dtypes=(BF16,) * 6)
    d_w_out = _matmul(sv["merged"], dy, "tn", f"out_proj_dw_{tag}")
    dpa = midpoint(dpa)
    do = _matmul(dpa, wts["wa"], "nt", f"proj_a_dx_{tag}")
    dbr_b = _matmul(dpb, wts["wb"], "nt", f"proj_b_dx_{tag}")
    dbr_c = _matmul(dpc, wts["wc"], "nt", f"proj_c_dx_{tag}")
    d_wa = _matmul(sv["o"], dpa, "tn", f"proj_a_dw_{tag}")
    d_wb = _matmul(sv["br_b"], dpb, "tn", f"proj_b_dw_{tag}")
    d_wc = _matmul(sv["br_c"], dpc, "tn", f"proj_c_dw_{tag}")
    d_w_branch = jnp.concatenate([d_wa, d_wb, d_wc], axis=0)

    dpu, d_wp_bd, red_pool = _pool_bwd(sv["z"], wts["wp_bd"], wts["pool_scale"], dbr_b, f"pool_bwd_{tag}")
    dconv, red_conv = _conv_bwd(sv["z"], wts["conv_w"], dbr_c, f"conv_bwd_{tag}")
    qa2, doa = _attn_bwd_prep(sv["qa"], sv["o"], sv["lse"], do, f"attn_bwd_prep_{tag}")
    dqt, dka, dva = _attn_bwd(qa2, sv["ka"], sv["va"], sv["kat"], doa, f"attn_bwd_{tag}")
    dq, dk, dv, dfl, red_f = _attn_bwd_post(sv["z"], wts["b_f"], dqt, dka, dva, f"attn_bwd_post_{tag}")
    dz = jnp.concatenate([dpu, dconv, *dgl, dq, dk, dv, dfl], axis=1)
    dh = _matmul(dz, wts["w_in_t"], "nn", f"in_proj_dx_{tag}", tk=1152)
    d_w_in_t = _matmul(dz, sv["h"], "tn", f"in_proj_dw_{tag}", tm=1152)
    dx0, red_pre_mix = _prenorm_bwd(sv["x"], gvec, mod, dh, dx1, 0, 1, f"prenorm_mix_bwd_{tag}")

    rows = D // N_DEV
    big = [_in_rows_from_z(d_w_in_t).reshape(N_DEV, IN_SHARD, D), d_w_branch.reshape(N_DEV, rows, D),
           d_w_out.reshape(N_DEV, rows, D)]
    d_w_pool = jnp.stack([d_wp_bd[64 * g:64 * (g + 1), 64 * g:64 * (g + 1)] for g in range(4)])
    small = dict(
        mod=jnp.stack([red_pre_mix[0], red_pre_mix[1], red_post_mix[0], red_pre_ff[0], red_pre_ff[1], red_post_ff[0]]),
        g_mix_pre=red_pre_mix[2], g_mix_post=red_post_mix[1], g_ff_pre=red_pre_ff[2], g_ff_post=red_post_ff[1],
        b_f=red_f[0, 0:8], w_pool=d_w_pool, pool_scale=red_pool[0], conv_w=red_conv[0:3])
    return dx0, big, small


SMALL_KEYS = ["mod", "g_mix_pre", "g_mix_post", "g_ff_pre", "g_ff_post", "b_f", "w_pool", "pool_scale", "conv_w"]
SMALL_SHAPES = [(DEPTH, 6 * D), (DEPTH, D), (DEPTH, D), (DEPTH, D), (DEPTH, D), (DEPTH, 8), (DEPTH, 4, 64, 64),
                (DEPTH, POOL_W), (DEPTH, 3, CONV_W)]


def kernel(x, c, w_ada, b_ada, g_mix_pre, g_mix_post, g_ff_pre, g_ff_post, w_in, b_f, w_pool, pool_scale, conv_w, w_branch, w_out, w_ff1, w_ff2, loss_target, m_w_ada, m_b_ada, m_g_mix_pre, m_g_mix_post, m_g_ff_pre, m_g_ff_post, m_w_in, m_b_f, m_w_pool, m_pool_scale, m_conv_w, m_w_branch, m_w_out, m_w_ff1, m_w_ff2, v_w_ada, v_b_ada, v_g_mix_pre, v_g_mix_post, v_g_ff_pre, v_g_ff_post, v_w_in, v_b_f, v_w_pool, v_pool_scale, v_conv_w, v_w_branch, v_w_out, v_w_ff1, v_w_ff2):
    ix, iy, ic = lax.axis_index("x"), lax.axis_index("y"), lax.axis_index("c")
    me = 4 * ix + 2 * iy + ic
    route = jnp.stack([ic, 2 * (1 - ix) + iy, 2 * ix + (1 - iy), 2 * (1 - ix) + (1 - iy)]).astype(jnp.int32)
    place = jnp.stack([me, 2 * ix + iy]).astype(jnp.int32)
    wt_in, mt_in, vt_in = (jnp.transpose(a, (0, 2, 1)) for a in (w_in, m_w_in, v_w_in))

    c_all = _all_gather([_pad_rows(c)], "gather_c")[0][:, 0, :]
    c_pad = _pad_rows(c_all, ADA_ROWS)
    b_cols = lax.dynamic_slice_in_dim(b_ada, me * ADA_COLS, ADA_COLS, axis=1)
    b_cols = jnp.broadcast_to(b_cols[:, None, :], (DEPTH, 8, ADA_COLS))
    mod_part = _ada_fwd(c_pad, w_ada, b_cols, "ada_fwd")
    mod_all = _all_gather([mod_part.reshape(DEPTH * ADA_ROWS, ADA_COLS)], "gather_mod")[0]
    mod_all = mod_all.reshape(N_DEV, DEPTH, ADA_ROWS, ADA_COLS)
    mod_mine = lax.dynamic_index_in_dim(mod_all, me, axis=2, keepdims=False)
    mod_mine = jnp.transpose(mod_mine, (1, 0, 2)).reshape(DEPTH, 6, D)

    cw_cols = CONV_W // N_DEV
    cw_send = jnp.pad(conv_w.reshape(DEPTH * 3, cw_cols), ((0, 8 - DEPTH * 3), (0, LANE - cw_cols)))
    win_in = _window(wt_in, place[0:1], "w_in_window")
    send = [[w[l].astype(BF16) for w in (win_in, w_branch, w_out, w_ff1, w_ff2)] for l in range(DEPTH)]
    first = _all_gather(send[0][:1], "gather_weights_l0_in", sequencer_id=1, after=mod_all)
    rest = _all_gather(send[0][1:] + [cw_send], "gather_weights_l0_rest", sequencer_id=2, after=first[0])
    first1 = _all_gather(send[1][:1], "gather_weights_l1_in", sequencer_id=3, after=first[0])
    rest1 = _all_gather(send[1][1:], "gather_weights_l1_rest", sequencer_id=12, after=first[0])
    gathered = [first + rest[:4], first1 + rest1]
    cw_all = rest[4][:, :DEPTH * 3, :cw_cols].reshape(N_DEV, DEPTH, 3, cw_cols)

    def first_operands(l, p_in):
        wp_bd = jnp.zeros((POOL_W, POOL_W), F32)
        for g in range(4):
            wp_bd = wp_bd.at[64 * g:64 * (g + 1), 64 * g:64 * (g + 1)].set(w_pool[l, g])
        return dict(w_in_t=_z_rows_from_windows(p_in), wp_bd=wp_bd.astype(BF16),
                    pool_scale=_pad_rows(pool_scale[l][None, :]), b_f=_pad_rows(jnp.pad(b_f[l], (0, LANE - 8))[None, :]))

    def rest_operands(l, rest):
        p_br, p_out, p_ff1, p_ff2 = rest
        w_br_full = p_br.reshape(D, D)
        cw_full = jnp.transpose(cw_all[:, l], (1, 0, 2)).reshape(3, CONV_W)
        return dict(wa=w_br_full[0:A_WIDTH], wb=w_br_full[A_WIDTH:A_WIDTH + POOL_W], wc=w_br_full[A_WIDTH + POOL_W:],
                    w_out=p_out.reshape(D, D), w_ff1=p_ff1, w_ff2=p_ff2.reshape(D_FF, D), conv_w=_pad_rows(cw_full))

    xs = x[0]
    saved, layers = [], []
    for l in range(DEPTH):
        p_in, rest = gathered[l][0], gathered[l][1:5]
        if l > 0:
            xs, p_in = lax.optimization_barrier((xs, p_in))
        wts = first_operands(l, p_in)

        def arrive(t, l=l, rest=rest, wts=wts):
            if l > 0:
                t, rest = lax.optimization_barrier((t, rest))
            wts.update(rest_operands(l, rest))
            return t

        wts["arrive"] = arrive
        gvec = _pad_rows(jnp.stack([g_mix_pre[l], g_mix_post[l], g_ff_pre[l], g_ff_post[l]]))
        layers.append((wts, gvec, _pad_rows(mod_mine[l])))
        xs, sv = _layer_fwd(l, xs, *layers[l])
        saved.append(sv)
    dx, loss_part = _loss_head(xs, loss_target[0], "loss_head")
    loss = lax.psum(loss_part[0, 0], ("x", "y", "c"))
    small_grads = [None] * DEPTH
    mine, sibs, landed = ({} for _ in range(3))
    seq_id = iter(range(4, 4 + 4 * DEPTH))
    last = [gathered[DEPTH - 1][1]]

    def start(group, grads):
        mine[group] = grads
        sibs[group] = _sibling_exchange(grads, f"rs_sibling_{group}", sequencer_id=next(seq_id), after=last[0])
        last[0] = sibs[group][0]

    def finish(group, later):
        later, (grads, sib) = lax.optimization_barrier((later, (mine[group], sibs[group])))
        sends = [_pair_sums(g, p, route, f"rs_pair_sums_{group}_{k}") for k, (g, p) in enumerate(zip(grads, sib))]
        later, sends = lax.optimization_barrier((later, sends))
        landed[group] = _chip_exchange(sends, f"rs_chips_{group}", sequencer_id=next(seq_id), after=last[0])
        last[0] = landed[group][0]
        return later

    pending = None
    for l in reversed(range(DEPTH)):
        hook = (lambda da: da) if pending is None else functools.partial(finish, pending)
        dx, ffn_grads, ffn_reds = _ffn_bwd(l, dx, saved[l], *layers[l], hook)
        start(f"ffn_l{l}", ffn_grads)
        dx, mix_grads, small_grads[l] = _mixer_bwd(l, dx, saved[l], *layers[l], ffn_reds,
                                                   functools.partial(finish, f"ffn_l{l}"))
        start(f"mix_l{l}", mix_grads)
        pending = f"mix_l{l}"
    grad_x = dx[None]

    big_w = [wt_in, w_branch, w_out, w_ff1, w_ff2]
    big_m = [mt_in, m_w_branch, m_w_out, m_w_ff1, m_w_ff2]
    big_v = [vt_in, v_w_branch, v_w_out, v_w_ff1, v_w_ff2]
    where = [("mix", 0), ("mix", 1), ("mix", 2), ("ffn", 0), ("ffn", 1)]

    def reduce_and_update(k):
        group, at = where[k]
        return _reduce_adamw([mine[f"{group}_l{l}"][at] for l in range(DEPTH)],
                             [sibs[f"{group}_l{l}"][at] for l in range(DEPTH)],
                             [landed[f"{group}_l{l}"][at] for l in range(DEPTH)], place, big_w[k], big_m[k], big_v[k],
                             f"rs_sum_adamw_{k}")

    big_res = {k: list(reduce_and_update(k)) for k in (3, 4)}
    big_res[3][0] = finish(pending, big_res[3][0])

    small = {k: jnp.stack([small_grads[l][k] for l in range(DEPTH)]) for k in SMALL_KEYS}
    small_all = _all_gather([_pack([small[k] for k in SMALL_KEYS], 8, F32)], "gather_small")[0]
    dmod_all = small_all[:, 0:DEPTH * 6, :].reshape(N_DEV, DEPTH, 6 * D)
    summed = _unpack(_sum_slabs(small_all, "sum_small").reshape(-1), SMALL_SHAPES)
    sg = dict(zip(SMALL_KEYS, summed))
    dmod_cols = lax.dynamic_slice_in_dim(dmod_all, me * ADA_COLS, ADA_COLS, axis=2)
    dmod_cols = jnp.pad(jnp.transpose(dmod_cols, (1, 0, 2)), ((0, 0), (0, ADA_ROWS - N_DEV), (0, 0)))
    g_w_ada = _ada_bwd(c_pad, dmod_cols, "ada_bwd")
    g_conv_w = lax.dynamic_slice_in_dim(sg["conv_w"], me * (CONV_W // N_DEV), CONV_W // N_DEV, axis=2)

    ada_out = [g_w_ada] + list(_adamw(w_ada, g_w_ada, m_w_ada, v_w_ada, "adamw_ada"))
    rest_w = [b_ada, g_mix_pre, g_mix_post, g_ff_pre, g_ff_post, b_f, w_pool, pool_scale, conv_w]
    rest_m = [m_b_ada, m_g_mix_pre, m_g_mix_post, m_g_ff_pre, m_g_ff_post, m_b_f, m_w_pool, m_pool_scale, m_conv_w]
    rest_v = [v_b_ada, v_g_mix_pre, v_g_mix_post, v_g_ff_pre, v_g_ff_post, v_b_f, v_w_pool, v_pool_scale, v_conv_w]
    rest_g = [sg["mod"], sg["g_mix_pre"], sg["g_mix_post"], sg["g_ff_pre"], sg["g_ff_post"], sg["b_f"],
              sg["w_pool"], sg["pool_scale"], g_conv_w]
    rest_shapes = [a.shape for a in rest_w]
    upd = _adamw(_pack(rest_w, 8, F32)[None], _pack(rest_g, 8, F32)[None], _pack(rest_m, 8, F32)[None],
                 _pack(rest_v, 8, F32)[None], "adamw_rest")
    rest_out = [rest_g] + [_unpack(arr.reshape(-1), rest_shapes) for arr in upd]
    rest_out = [[ada_out[which]] + rest_out[which] for which in range(4)]

    landed[pending], rest_out = lax.optimization_barrier((landed[pending], rest_out))
    big_res.update({k: reduce_and_update(k) for k in (0, 1, 2)})
    big_out = [[jnp.transpose(big_res[k][which], (0, 2, 1)) if k == 0 else big_res[k][which] for k in range(5)]
               for which in range(4)]

    def ordered(k):
        r, b = rest_out[k], big_out[k]
        return [r[0], r[1], r[2], r[3], r[4], r[5], b[0], r[6], r[7], r[8], r[9], b[1], b[2], b[3], b[4]]

    return (loss, grad_x, *ordered(0), *ordered(1), *ordered(2), *ordered(3))
```

```python
import functools

import jax
import jax.numpy as jnp
from jax import lax
from jax.experimental import pallas as pl
from jax.experimental.pallas import tpu as pltpu
from jax.experimental.pallas import tpu_sc as plsc

F32 = jnp.float32
BF16 = jnp.bfloat16

N_DEV = 8
D = 1024
S = 2048
DEPTH = 2
D_FF = 4 * D
A_WIDTH = 512
HEAD_DIM = 64
N_PAIR = 4
POOL_W = 256
CONV_W = 256
IN_COLS = 5640
ADA_COLS = 6 * D // N_DEV
IN_SHARD = IN_COLS // N_DEV
RMS_EPS = 1e-6
NEG_INF = -1e30
ATT_SCALE = HEAD_DIM ** -0.5

NZ = 5760
Z_PC = 0
Z_G = 1024
Z_Q = 4096
Z_K = 4608
Z_V = 5120
Z_F = 5632

LR, B1, B2, EPS, WD, STEP = 0.001, 0.9, 0.999, 1e-08, 0.01, 10

LANE = 128
VMEM_LIMIT_BYTES = 48 * 1024 * 1024
TS = 512
TQ = 256
TQ_FWD = 512
HEADS_PER_STEP = 4
HEADS_PER_STEP_FWD = 8


def _params(sem=None):
    return pltpu.CompilerParams(dimension_semantics=sem, vmem_limit_bytes=VMEM_LIMIT_BYTES)


def _pick(n, target):
    best = None
    for t in range(LANE, min(n, target) + 1, LANE):
        if n % t == 0:
            best = t
    return n if best is None else best


def _matmul(a, b, mode, name, out_dtype=F32, tm=1024, tn=1024, tk=1024, b_col_shards=False, out_col_shards=False,
            extra=(), epilogue=None, out_dtypes=None):
    if b_col_shards:
        shards, b_rows, shard_cols = b.shape
        b_shape = (b_rows, shards * shard_cols)
    else:
        b_shape = b.shape
    if mode == "nn":
        (m, k), (k2, n) = a.shape, b_shape
    elif mode == "nt":
        (m, k), (n, k2) = a.shape, b_shape
    else:
        (k, m), (k2, n) = a.shape, b_shape
    assert k == k2, (a.shape, b.shape, mode)
    tm, tn, tk = _pick(m, tm), _pick(n, tn), _pick(k, tk)
    if b_col_shards and mode == "nn":
        tn = shard_cols
    per_step = 1
    if b_col_shards and mode == "nt":
        per_step = max(1, tk // shard_cols)
        tk = per_step * shard_cols
    if out_col_shards:
        tn = n // N_DEV
    nk = k // tk
    if mode == "nn":
        a_spec = pl.BlockSpec((tm, tk), lambda i, j, kk: (i, kk))
        b_spec = (pl.BlockSpec((None, tk, tn), lambda i, j, kk: (j, kk, 0)) if b_col_shards else
                  pl.BlockSpec((tk, tn), lambda i, j, kk: (kk, j)))
        dims = (((1,), (0,)), ((), ()))
    elif mode == "nt":
        a_spec = pl.BlockSpec((tm, tk), lambda i, j, kk: (i, kk))
        b_spec = (pl.BlockSpec((per_step, tn, shard_cols), lambda i, j, kk: (kk, j, 0)) if b_col_shards else
                  pl.BlockSpec((tn, tk), lambda i, j, kk: (j, kk)))
        dims = (((1,), (1,)), ((), ()))
    else:
        assert not b_col_shards
        a_spec = pl.BlockSpec((tk, tm), lambda i, j, kk: (kk, i))
        b_spec = pl.BlockSpec((tk, tn), lambda i, j, kk: (kk, j))
        dims = (((0,), (0,)), ((), ()))
    if out_col_shards:
        out_shape = jax.ShapeDtypeStruct((N_DEV, m, tn), out_dtype)
        out_spec = pl.BlockSpec((None, tm, tn), lambda i, j, kk: (j, i, 0))
    else:
        out_shape = jax.ShapeDtypeStruct((m, n), out_dtype)
        out_spec = pl.BlockSpec((tm, tn), lambda i, j, kk: (i, j))

    n_extra = len(extra)
    extra_specs = [pl.BlockSpec((tm, tn), lambda i, j, kk, off=off: (i, j + off // tn)) for _, off in extra]
    if epilogue is not None:
        assert not out_col_shards and all(off % tn == 0 for _, off in extra)
        out_shape = [jax.ShapeDtypeStruct((m, n), dt) for dt in out_dtypes]
        out_spec = [pl.BlockSpec((tm, tn), lambda i, j, kk: (i, j)) for _ in out_dtypes]

    def product(a_ref, b_ref):
        if b_col_shards and mode == "nt":
            b_tile = jnp.concatenate([b_ref[s] for s in range(per_step)], axis=1) if per_step > 1 else b_ref[0]
        else:
            b_tile = b_ref[...]
        return lax.dot_general(a_ref[...].astype(BF16), b_tile.astype(BF16), dims, preferred_element_type=F32)

    def write(acc, extra_refs, o_refs):
        if epilogue is None:
            o_refs[0][...] = acc.astype(out_dtype)
        else:
            for o_ref, tile in zip(o_refs, epilogue(acc, *[r[...] for r in extra_refs])):
                o_ref[...] = tile.astype(o_ref.dtype)

    def body_one_pass(a_ref, b_ref, *refs):
        write(product(a_ref, b_ref), refs[:n_extra], refs[n_extra:])

    def body(a_ref, b_ref, *refs):
        acc_ref = refs[-1]
        kk = pl.program_id(2)

        @pl.when(kk == 0)
        def _():
            acc_ref[...] = product(a_ref, b_ref)

        @pl.when(kk > 0)
        def _():
            acc_ref[...] += product(a_ref, b_ref)

        @pl.when(kk == nk - 1)
        def _():
            write(acc_ref[...], refs[:n_extra], refs[n_extra:-1])

    return pl.pallas_call(
        body_one_pass if nk == 1 else body, name=name,
        out_shape=out_shape,
        grid=(m // tm, n // tn, nk),
        in_specs=[a_spec, b_spec] + extra_specs,
        out_specs=out_spec,
        scratch_shapes=[] if nk == 1 else [pltpu.VMEM((tm, tn), F32)],
        compiler_params=_params(("parallel", "parallel", "arbitrary")),
    )(a, b, *[x for x, _ in extra])


def _row_spec(width=D, col=0):
    return pl.BlockSpec((TS, width), lambda i: (i, col))


def _vec_spec(rows=8, width=D):
    return pl.BlockSpec((rows, width), lambda i: (0, 0))


def _rms(x):
    return lax.rsqrt(jnp.mean(x * x, axis=-1, keepdims=True) + RMS_EPS)


def _prenorm_fwd(x, gvec, mod, g_row, shift_row, scale_row, name):
    def body(x_ref, g_ref, mod_ref, h_ref):
        xv = x_ref[...]
        y = xv * _rms(xv) * g_ref[g_row:g_row + 1, :]
        h = y * (1.0 + mod_ref[scale_row:scale_row + 1, :]) + mod_ref[shift_row:shift_row + 1, :]
        h_ref[...] = h.astype(BF16)

    return pl.pallas_call(
        body, name=name, out_shape=jax.ShapeDtypeStruct((S, D), BF16), grid=(S // TS,),
        in_specs=[_row_spec(), _vec_spec(), _vec_spec()], out_specs=_row_spec(),
        compiler_params=_params(("parallel",)),
    )(x, gvec, mod)


def _prenorm_bwd(x, gvec, mod, dh, dres, g_row, scale_row, name):
    def body(x_ref, g_ref, mod_ref, dh_ref, dres_ref, dx_ref, red_ref):
        i = pl.program_id(0)

        @pl.when(i == 0)
        def _():
            red_ref[...] = jnp.zeros_like(red_ref)

        xv = x_ref[...]
        g = g_ref[g_row:g_row + 1, :]
        r = _rms(xv)
        n = xv * r
        yg = n * g
        dhv = dh_ref[...]
        dyg = dhv * (1.0 + mod_ref[scale_row:scale_row + 1, :])
        dn = dyg * g
        dx = r * (dn - n * jnp.mean(dn * n, axis=-1, keepdims=True))
        dx_ref[...] = dres_ref[...] + dx
        red_ref[0:1, :] += jnp.sum(dhv, axis=0, keepdims=True)
        red_ref[1:2, :] += jnp.sum(dhv * yg, axis=0, keepdims=True)
        red_ref[2:3, :] += jnp.sum(dyg * n, axis=0, keepdims=True)

    return pl.pallas_call(
        body, name=name,
        out_shape=(jax.ShapeDtypeStruct((S, D), F32), jax.ShapeDtypeStruct((8, D), F32)),
        grid=(S // TS,),
        in_specs=[_row_spec(), _vec_spec(), _vec_spec(), _row_spec(), _row_spec()],
        out_specs=(_row_spec(), _vec_spec()),
        compiler_params=_params(("arbitrary",)),
    )(x, gvec, mod, dh, dres)


def _postnorm_fwd(x, y, gvec, mod, g_row, gate_row, name):
    def body(x_ref, y_ref, g_ref, mod_ref, o_ref):
        yv = y_ref[...]
        yn = yv * _rms(yv) * g_ref[g_row:g_row + 1, :]
        o_ref[...] = x_ref[...] + mod_ref[gate_row:gate_row + 1, :] * yn

    return pl.pallas_call(
        body, name=name, out_shape=jax.ShapeDtypeStruct((S, D), F32), grid=(S // TS,),
        in_specs=[_row_spec(), _row_spec(), _vec_spec(), _vec_spec()], out_specs=_row_spec(),
        compiler_params=_params(("parallel",)),
    )(x, y, gvec, mod)


def _postnorm_bwd(y, gvec, mod, dxo, g_row, gate_row, name):
    def body(y_ref, g_ref, mod_ref, dxo_ref, dy_ref, red_ref):
        i = pl.program_id(0)

        @pl.when(i == 0)
        def _():
            red_ref[...] = jnp.zeros_like(red_ref)

        yv = y_ref[...]
        g = g_ref[g_row:g_row + 1, :]
        r = _rms(yv)
        n = yv * r
        dxo = dxo_ref[...]
        dyn = dxo * mod_ref[gate_row:gate_row + 1, :]
        dn = dyn * g
        dy = r * (dn - n * jnp.mean(dn * n, axis=-1, keepdims=True))
        dy_ref[...] = dy.astype(BF16)
        red_ref[0:1, :] += jnp.sum(dxo * (n * g), axis=0, keepdims=True)
        red_ref[1:2, :] += jnp.sum(dyn * n, axis=0, keepdims=True)

    return pl.pallas_call(
        body, name=name,
        out_shape=(jax.ShapeDtypeStruct((S, D), BF16), jax.ShapeDtypeStruct((8, D), F32)),
        grid=(S // TS,),
        in_specs=[_row_spec(), _vec_spec(), _vec_spec(), _row_spec()],
        out_specs=(_row_spec(), _vec_spec()),
        compiler_params=_params(("arbitrary",)),
    )(y, gvec, mod, dxo)


def _loss_head(xf, target, name):
    def body(x_ref, t_ref, dx_ref, loss_ref):
        i = pl.program_id(0)

        @pl.when(i == 0)
        def _():
            loss_ref[...] = jnp.zeros_like(loss_ref)

        e = x_ref[...] - t_ref[...]
        dx_ref[...] = e / float(D)
        per_tok = jnp.mean(e * e, axis=-1, keepdims=True)
        loss_ref[0:1, 0:1] += 0.5 * jnp.sum(per_tok, axis=0, keepdims=True)

    return pl.pallas_call(
        body, name=name,
        out_shape=(jax.ShapeDtypeStruct((S, D), F32), jax.ShapeDtypeStruct((8, LANE), F32)),
        grid=(S // TS,),
        in_specs=[_row_spec(), _row_spec()],
        out_specs=(_row_spec(), pl.BlockSpec((8, LANE), lambda i: (0, 0))),
        compiler_params=_params(("arbitrary",)),
    )(xf, target)


def _relu2_epilogue(a):
    t = jnp.maximum(a, 0.0)
    return a, t * t


def _relu2_bwd_epilogue(dr, a):
    return (dr * (2.0 * jnp.maximum(a, 0.0)),)


def _merge_epilogue(pc, g0, g1, g2, pa, pb):
    return pc, jax.nn.sigmoid(g0) * pa + jax.nn.sigmoid(g1) * pb + jax.nn.sigmoid(g2) * pc


def _merge_bwd_epilogue(dm, g0, g1, g2, pa, pb, pc):
    sg = [jax.nn.sigmoid(g) for g in (g0, g1, g2)]
    return tuple(dm * s for s in sg) + tuple(dm * p * (s * (1.0 - s)) for p, s in zip((pa, pb, pc), sg))


def _shift_down(x, k, row):
    return jnp.where(row >= k, pltpu.roll(x, k, axis=0), 0.0)


def _shift_up(x, k, row):
    n = x.shape[0]
    return jnp.where(row < n - k, pltpu.roll(x, n - k, axis=0), 0.0)


def _cumsum_rows(x, row, reverse=False):
    shift = _shift_up if reverse else _shift_down
    k = 1
    while k < x.shape[0]:
        x = x + shift(x, k, row)
        k *= 2
    return x


def _full_spec(shape, idx=(0, 0)):
    return pl.BlockSpec(shape, lambda i: idx)


def _pool_window_select(lane, a2, a4, a8, a16):
    return jnp.where(lane < 64, a2, jnp.where(lane < 128, a4, jnp.where(lane < 192, a8, a16)))


def _pool_p(u, row, lane):
    t2 = u + _shift_down(u, 1, row)
    t4 = t2 + _shift_down(t2, 2, row)
    t8 = t4 + _shift_down(t4, 4, row)
    t16 = t8 + _shift_down(t8, 8, row)
    tw = _pool_window_select(lane, t2, t4, t8, t16)
    cnt = jnp.minimum((row + 1).astype(F32), _pool_window_select(lane, 2.0, 4.0, 8.0, 16.0))
    return tw / cnt - u, cnt


def _pool_fwd(z, wp_bd, pscale, name):
    def body(u_ref, w_ref, s_ref, o_ref):
        row = lax.broadcasted_iota(jnp.int32, (S, POOL_W), 0)
        lane = lax.broadcasted_iota(jnp.int32, (S, POOL_W), 1)
        p, _ = _pool_p(u_ref[...], row, lane)
        y = jnp.dot(p.astype(BF16), w_ref[...], preferred_element_type=F32)
        o_ref[...] = y * s_ref[0:1, :]

    return pl.pallas_call(
        body, name=name, out_shape=jax.ShapeDtypeStruct((S, POOL_W), F32), grid=(1,),
        in_specs=[_full_spec((S, POOL_W), (0, Z_PC // POOL_W)), _full_spec((POOL_W, POOL_W)), _full_spec((8, POOL_W))],
        out_specs=_full_spec((S, POOL_W)),
        compiler_params=_params(("arbitrary",)),
    )(z, wp_bd, pscale)


def _pool_bwd(z, wp_bd, pscale, dbr, name):
    def body(u_ref, w_ref, s_ref, dbr_ref, du_ref, dw_ref, red_ref):
        row = lax.broadcasted_iota(jnp.int32, (S, POOL_W), 0)
        lane = lax.broadcasted_iota(jnp.int32, (S, POOL_W), 1)
        p, cnt = _pool_p(u_ref[...], row, lane)
        pb = p.astype(BF16)
        y = jnp.dot(pb, w_ref[...], preferred_element_type=F32)
        dbr = dbr_ref[...]
        red_ref[...] = jnp.zeros_like(red_ref)
        red_ref[0:1, :] = jnp.sum(dbr * y, axis=0, keepdims=True)
        dy = (dbr * s_ref[0:1, :]).astype(BF16)
        dw_ref[...] = lax.dot_general(pb, dy, (((0,), (0,)), ((), ())), preferred_element_type=F32)
        dp = lax.dot_general(dy, w_ref[...], (((1,), (1,)), ((), ())), preferred_element_type=F32)
        g = dp / cnt
        a2 = g + _shift_up(g, 1, row)
        a4 = a2 + _shift_up(a2, 2, row)
        a8 = a4 + _shift_up(a4, 4, row)
        a16 = a8 + _shift_up(a8, 8, row)
        du_ref[...] = (_pool_window_select(lane, a2, a4, a8, a16) - dp).astype(BF16)

    return pl.pallas_call(
        body, name=name,
        out_shape=(jax.ShapeDtypeStruct((S, POOL_W), BF16), jax.ShapeDtypeStruct((POOL_W, POOL_W), F32),
                   jax.ShapeDtypeStruct((8, POOL_W), F32)),
        grid=(1,),
        in_specs=[_full_spec((S, POOL_W), (0, Z_PC // POOL_W)), _full_spec((POOL_W, POOL_W)), _full_spec((8, POOL_W)),
                  _full_spec((S, POOL_W))],
        out_specs=(_full_spec((S, POOL_W)), _full_spec((POOL_W, POOL_W)), _full_spec((8, POOL_W))),
        compiler_params=_params(("arbitrary",)),
    )(z, wp_bd, pscale, dbr)


def _conv_specs():
    base = Z_PC // CONV_W
    return [_full_spec((S, CONV_W), (0, base + 1)), _full_spec((S, CONV_W), (0, base + 2)),
            _full_spec((S, CONV_W), (0, base + 3)), _full_spec((8, CONV_W))]


def _conv_fwd(z, cw, name):
    def body(h_ref, b_ref, c_ref, w_ref, o_ref):
        row = lax.broadcasted_iota(jnp.int32, (S, CONV_W), 0)
        u = c_ref[...] * h_ref[...]
        y = (w_ref[0:1, :] * _shift_down(u, 2, row) + w_ref[1:2, :] * _shift_down(u, 1, row) + w_ref[2:3, :] * u)
        o_ref[...] = b_ref[...] * y

    return pl.pallas_call(
        body, name=name, out_shape=jax.ShapeDtypeStruct((S, CONV_W), F32), grid=(1,),
        in_specs=_conv_specs(), out_specs=_full_spec((S, CONV_W)),
        compiler_params=_params(("arbitrary",)),
    )(z, z, z, cw)


def _conv_bwd(z, cw, dbr, name):
    def body(h_ref, b_ref, c_ref, w_ref, dbr_ref, d_ref, red_ref):
        row = lax.broadcasted_iota(jnp.int32, (S, CONV_W), 0)
        h, cg = h_ref[...], c_ref[...]
        u = cg * h
        u1 = _shift_down(u, 1, row)
        u2 = _shift_down(u, 2, row)
        y = w_ref[0:1, :] * u2 + w_ref[1:2, :] * u1 + w_ref[2:3, :] * u
        dbr = dbr_ref[...]
        dy = dbr * b_ref[...]
        du = w_ref[2:3, :] * dy + w_ref[1:2, :] * _shift_up(dy, 1, row) + w_ref[0:1, :] * _shift_up(dy, 2, row)
        d_ref[:, 0:CONV_W] = (du * cg).astype(BF16)
        d_ref[:, CONV_W:2 * CONV_W] = (dbr * y).astype(BF16)
        d_ref[:, 2 * CONV_W:3 * CONV_W] = (du * h).astype(BF16)
        red_ref[...] = jnp.zeros_like(red_ref)
        red_ref[0:1, :] = jnp.sum(dy * u2, axis=0, keepdims=True)
        red_ref[1:2, :] = jnp.sum(dy * u1, axis=0, keepdims=True)
        red_ref[2:3, :] = jnp.sum(dy * u, axis=0, keepdims=True)

    return pl.pallas_call(
        body, name=name,
        out_shape=(jax.ShapeDtypeStruct((S, 3 * CONV_W), BF16), jax.ShapeDtypeStruct((8, CONV_W), F32)),
        grid=(1,),
        in_specs=_conv_specs() + [_full_spec((S, CONV_W))],
        out_specs=(_full_spec((S, 3 * CONV_W)), _full_spec((8, CONV_W))),
        compiler_params=_params(("arbitrary",)),
    )(z, z, z, cw, dbr)


_NT = (((1,), (1,)), ((), ()))
_TN = (((0,), (0,)), ((), ()))
N_HEAD = 2 * N_PAIR


def _split3(x):
    hi = x.astype(BF16).astype(F32)
    mid = (x - hi).astype(BF16).astype(F32)
    lo = (x - hi - mid).astype(BF16).astype(F32)
    return hi, mid, lo


def _spare(lane, e, k):
    return lane == 64 * (1 - e) + k


def _spare3(lane, e, k):
    base = 64 * (1 - e) + k
    return (lane >= base) & (lane < base + 3)


def _put3(lane, e, k, pieces, rest):
    out = rest
    for n, piece in enumerate(pieces):
        out = jnp.where(_spare(lane, e, k + n), piece, out)
    return out


def _attn_prep(z, bf, name):
    def body(q_ref, k_ref, v_ref, f_ref, b_ref, qa_ref, ka_ref, va_ref, kat_ref, cum_ref):
        p = pl.program_id(0)
        row = lax.broadcasted_iota(jnp.int32, (S, LANE), 0)
        lane = lax.broadcasted_iota(jnp.int32, (S, LANE), 1)

        @pl.when(p == 0)
        def _():
            xv = f_ref[...] + b_ref[0:1, :]
            ls = jnp.minimum(xv, 0.0) - jnp.log(1.0 + jnp.exp(-jnp.abs(xv)))
            cum_ref[...] = _cumsum_rows(jnp.where(lane < N_HEAD, ls, 0.0), row)

        cum = cum_ref[...]
        q, k, v = q_ref[...], k_ref[...], v_ref[...]
        for e in range(2):
            head = (lane >= 64) if e else (lane < 64)
            f = jnp.sum(jnp.where(lane == 2 * p + e, cum, 0.0), axis=1, keepdims=True)
            pieces = _split3(f)
            qa = jnp.where(head, q * ATT_SCALE, _put3(lane, e, 0, pieces, jnp.where(_spare3(lane, e, 3), 1.0, 0.0)))
            ones = jnp.where(_spare3(lane, e, 0) | _spare3(lane, e, 6), 1.0, 0.0)
            ka = jnp.where(head, k, _put3(lane, e, 3, [-x for x in pieces], ones))
            va = jnp.where(head, v, jnp.where(_spare3(lane, e, 0), 1.0, 0.0))
            qa_ref[e] = qa.astype(BF16)
            ka_ref[e] = ka.astype(BF16)
            va_ref[e] = va.astype(BF16)
            kat_ref[e] = ka.T.astype(BF16)

    qb, kb, vb = Z_Q // LANE, Z_K // LANE, Z_V // LANE
    heads = jax.ShapeDtypeStruct((N_HEAD, S, LANE), BF16)
    pair = pl.BlockSpec((2, S, LANE), lambda p: (p, 0, 0))
    return pl.pallas_call(
        body, name=name,
        out_shape=(heads, heads, heads, jax.ShapeDtypeStruct((N_HEAD, LANE, S), BF16)),
        grid=(N_PAIR,),
        in_specs=[pl.BlockSpec((S, LANE), lambda p: (0, qb + p)), pl.BlockSpec((S, LANE), lambda p: (0, kb + p)),
                  pl.BlockSpec((S, LANE), lambda p: (0, vb + p)), pl.BlockSpec((S, LANE), lambda p: (0, Z_F // LANE)),
                  pl.BlockSpec((8, LANE), lambda p: (0, 0))],
        out_specs=(pair, pair, pair, pl.BlockSpec((2, LANE, S), lambda p: (p, 0, 0))),
        scratch_shapes=[pltpu.VMEM((S, LANE), F32)],
        compiler_params=_params(("arbitrary",)),
    )(z, z, z, z, bf)


def _attn_bwd_prep(qa, o, lse, do, name):
    def body(qa_ref, o_ref, lse_ref, do_ref, qa2_ref, doa_ref):
        lane = lax.broadcasted_iota(jnp.int32, (S, LANE), 1)
        dov, ov, lsev = do_ref[...], o_ref[...], lse_ref[...]
        for e in range(2):
            head = (lane >= 64) if e else (lane < 64)
            dsum = jnp.sum(jnp.where(head, dov * ov, 0.0), axis=1, keepdims=True)
            doa_ref[e] = jnp.where(head, dov, _put3(lane, e, 0, [-x for x in _split3(dsum)], 0.0)).astype(BF16)
            lse_col = lsev[:, 64 * e:64 * e + 1]
            qa2_ref[e] = _put3(lane, e, 6, [-x for x in _split3(lse_col)], qa_ref[e].astype(F32)).astype(BF16)

    heads = jax.ShapeDtypeStruct((N_HEAD, S, LANE), BF16)
    pair = pl.BlockSpec((2, S, LANE), lambda p: (p, 0, 0))
    cols = pl.BlockSpec((S, LANE), lambda p: (0, p))
    return pl.pallas_call(
        body, name=name, out_shape=(heads, heads), grid=(N_PAIR,),
        in_specs=[pair, cols, cols, cols], out_specs=(pair, pair),
        compiler_params=_params(("parallel",)),
    )(qa, o, lse, do)


def _attn_bwd_post(z, bf, dqt, dka, dva, name):
    def body(f_ref, b_ref, dqt_ref, dk_ref, dv_ref, dq_out, dk_out, dv_out, dfl_ref, red_ref, dcum_ref):
        p = pl.program_id(0)

        @pl.when(p == 0)
        def _():
            dcum_ref[...] = jnp.zeros_like(dcum_ref)

        row = lax.broadcasted_iota(jnp.int32, (S, LANE), 0)
        lane = lax.broadcasted_iota(jnp.int32, (S, LANE), 1)
        dqa = [dqt_ref[e].T for e in range(2)]
        dq_out[...] = (jnp.where(lane < 64, dqa[0], dqa[1]) * ATT_SCALE).astype(BF16)
        dk_out[...] = jnp.where(lane < 64, dk_ref[0], dk_ref[1]).astype(BF16)
        dv_out[...] = jnp.where(lane < 64, dv_ref[0], dv_ref[1]).astype(BF16)
        for e in range(2):
            d_query = jnp.sum(jnp.where(_spare(lane, e, 0), dqa[e], 0.0), axis=1, keepdims=True)
            d_key = jnp.sum(jnp.where(_spare(lane, e, 3), dk_ref[e], 0.0), axis=1, keepdims=True)
            dcum_ref[...] += jnp.where(lane == 2 * p + e, d_query - d_key, 0.0)

        @pl.when(p == N_PAIR - 1)
        def _():
            dls = _cumsum_rows(dcum_ref[...], row, reverse=True)
            xv = f_ref[...] + b_ref[0:1, :]
            dx = jnp.where(lane < N_HEAD, dls * jax.nn.sigmoid(-xv), 0.0)
            dfl_ref[...] = dx.astype(BF16)
            red_ref[...] = jnp.zeros_like(red_ref)
            red_ref[0:1, :] = jnp.sum(dx, axis=0, keepdims=True)

    wide = jax.ShapeDtypeStruct((S, N_PAIR * LANE), BF16)
    cols = pl.BlockSpec((S, LANE), lambda p: (0, p))
    pair = pl.BlockSpec((2, S, LANE), lambda p: (p, 0, 0))
    return pl.pallas_call(
        body, name=name,
        out_shape=(wide, wide, wide, jax.ShapeDtypeStruct((S, LANE), BF16), jax.ShapeDtypeStruct((8, LANE), F32)),
        grid=(N_PAIR,),
        in_specs=[pl.BlockSpec((S, LANE), lambda p: (0, Z_F // LANE)), pl.BlockSpec((8, LANE), lambda p: (0, 0)),
                  pl.BlockSpec((2, LANE, S), lambda p: (p, 0, 0)), pair, pair],
        out_specs=(cols, cols, cols, pl.BlockSpec((S, LANE), lambda p: (0, 0)), pl.BlockSpec((8, LANE), lambda p: (0, 0))),
        scratch_shapes=[pltpu.VMEM((S, LANE), F32)],
        compiler_params=_params(("arbitrary",)),
    )(z, bf, dqt, dka, dva)


def _attn_fwd(qa, ka, va, name):
    tq, tk = TQ_FWD, TQ
    ratio = tq // tk

    def body(qa_ref, ka_ref, va_ref, o_ref, lse_ref):
        i = pl.program_id(1)
        lane = lax.broadcasted_iota(jnp.int32, (tq, LANE), 1)
        row = lax.broadcasted_iota(jnp.int32, (tq, tk), 0)
        col = lax.broadcasted_iota(jnp.int32, (tq, tk), 1)
        nh = HEADS_PER_STEP_FWD
        qs = [qa_ref[h] for h in range(nh)]

        def block(j, carry, masked):
            off = pl.multiple_of(j * tk, tk)
            out = []
            for h in range(nh):
                m, acc = carry[h]
                s = lax.dot_general(qs[h], ka_ref[h, pl.ds(off, tk), :], _NT, preferred_element_type=F32)
                if masked:
                    s = jnp.where(col + (j - ratio * i) * tk > row, NEG_INF, s)
                mn = jnp.maximum(m, jnp.max(s, axis=1, keepdims=True))
                p = jnp.exp(s - mn).astype(BF16)
                acc = jnp.exp(m - mn) * acc + jnp.dot(p, va_ref[h, pl.ds(off, tk), :], preferred_element_type=F32)
                out.append((mn, acc))
            return tuple(out)

        init = (jnp.full((tq, 1), NEG_INF, F32), jnp.zeros((tq, LANE), F32))
        carry = lax.fori_loop(0, ratio * i, lambda j, c: block(j, c, False), (init,) * nh)
        for d in range(ratio):
            carry = block(ratio * i + d, carry, True)
        res = []
        for h in range(nh):
            m, acc = carry[h]
            l = jnp.sum(jnp.where(_spare(lane, h % 2, 0), acc, 0.0), axis=1, keepdims=True)
            res.append((acc / l, m + jnp.log(l)))
        for g in range(nh // 2):
            o_ref[:, g * LANE:(g + 1) * LANE] = jnp.where(lane < 64, res[2 * g][0], res[2 * g + 1][0])
            lse_ref[:, g * LANE:(g + 1) * LANE] = jnp.where(lane < 64, res[2 * g][1], res[2 * g + 1][1])

    nh = HEADS_PER_STEP_FWD
    out = jax.ShapeDtypeStruct((S, N_PAIR * LANE), F32)
    wide = pl.BlockSpec((tq, 64 * nh), lambda p, i: (i, p))
    return pl.pallas_call(
        body, name=name, out_shape=(out, out), grid=(N_HEAD // nh, S // tq),
        in_specs=[pl.BlockSpec((nh, tq, LANE), lambda p, i: (p, i, 0)), pl.BlockSpec((nh, S, LANE), lambda p, i: (p, 0, 0)),
                  pl.BlockSpec((nh, S, LANE), lambda p, i: (p, 0, 0))],
        out_specs=(wide, wide),
        compiler_params=_params(("parallel", "parallel")),
    )(qa, ka, va)


def _attn_bwd(qa2, ka, va, kat, doa, name):
    nq = S // TQ

    def body(qa_ref, ka_ref, va_ref, kat_ref, doa_ref, dqt_ref, dk_ref, dv_ref):
        j = pl.program_id(1)

        @pl.when(j == 0)
        def _():
            dqt_ref[...] = jnp.zeros_like(dqt_ref)

        key = lax.broadcasted_iota(jnp.int32, (TQ, TQ), 0)
        qry = lax.broadcasted_iota(jnp.int32, (TQ, TQ), 1)
        nh = HEADS_PER_STEP
        kav, vav, katv = ([ref[h] for h in range(nh)] for ref in (ka_ref, va_ref, kat_ref))

        def block(i, carry, masked):
            off = pl.multiple_of(i * TQ, TQ)
            out = []
            for h in range(nh):
                dk_acc, dv_acc = carry[h]
                qav = qa_ref[h, pl.ds(off, TQ), :]
                doav = doa_ref[h, pl.ds(off, TQ), :]
                s_t = lax.dot_general(kav[h], qav, _NT, preferred_element_type=F32)
                if masked:
                    s_t = jnp.where(key > qry, NEG_INF, s_t)
                p_t = jnp.exp(s_t)
                ds_t = p_t * lax.dot_general(vav[h], doav, _NT, preferred_element_type=F32)
                dsb = ds_t.astype(BF16)
                dv_acc = dv_acc + jnp.dot(p_t.astype(BF16), doav, preferred_element_type=F32)
                dk_acc = dk_acc + jnp.dot(dsb, qav, preferred_element_type=F32)
                dqt_ref[h, :, pl.ds(off, TQ)] += jnp.dot(katv[h], dsb, preferred_element_type=F32)
                out.append((dk_acc, dv_acc))
            return tuple(out)

        zero = (jnp.zeros((TQ, LANE), F32), jnp.zeros((TQ, LANE), F32))
        carry = block(j, (zero,) * nh, True)
        carry = lax.fori_loop(j + 1, nq, lambda i, c: block(i, c, False), carry)
        for h in range(nh):
            dk_ref[h], dv_ref[h] = carry[h]

    nh = HEADS_PER_STEP
    full = pl.BlockSpec((nh, S, LANE), lambda p, j: (p, 0, 0))
    blk = pl.BlockSpec((nh, TQ, LANE), lambda p, j: (p, j, 0))
    acc = jax.ShapeDtypeStruct((N_HEAD, S, LANE), F32)
    return pl.pallas_call(
        body, name=name,
        out_shape=(jax.ShapeDtypeStruct((N_HEAD, LANE, S), F32), acc, acc),
        grid=(N_HEAD // nh, nq),
        in_specs=[full, blk, blk, pl.BlockSpec((nh, LANE, TQ), lambda p, j: (p, 0, j)), full],
        out_specs=(pl.BlockSpec((nh, LANE, S), lambda p, j: (p, 0, 0)), blk, blk),
        compiler_params=_params(("arbitrary", "arbitrary")),
    )(qa2, ka, va, kat, doa)


ADA_ROWS = 16


def _ada_fwd(c_pad, w_ada, b_cols, name):
    def body(c_ref, w_ref, b_ref, o_ref):
        cv = c_ref[...]
        sc = (cv * jax.nn.sigmoid(cv)).astype(BF16)
        o_ref[0] = jnp.dot(sc, w_ref[0].astype(BF16), preferred_element_type=F32) + b_ref[0, 0:1, :]

    return pl.pallas_call(
        body, name=name, out_shape=jax.ShapeDtypeStruct((DEPTH, ADA_ROWS, ADA_COLS), F32), grid=(DEPTH,),
        in_specs=[pl.BlockSpec((ADA_ROWS, D), lambda l: (0, 0)), pl.BlockSpec((1, D, ADA_COLS), lambda l: (l, 0, 0)),
                  pl.BlockSpec((1, 8, ADA_COLS), lambda l: (l, 0, 0))],
        out_specs=pl.BlockSpec((1, ADA_ROWS, ADA_COLS), lambda l: (l, 0, 0)),
        compiler_params=_params(("parallel",)),
    )(c_pad, w_ada, b_cols)


def _ada_bwd(c_pad, dmod_cols, name):
    def body(c_ref, d_ref, o_ref):
        cv = c_ref[...]
        sc = (cv * jax.nn.sigmoid(cv)).astype(BF16)
        o_ref[0] = lax.dot_general(sc, d_ref[0].astype(BF16), _TN, preferred_element_type=F32)

    return pl.pallas_call(
        body, name=name, out_shape=jax.ShapeDtypeStruct((DEPTH, D, ADA_COLS), F32), grid=(DEPTH,),
        in_specs=[pl.BlockSpec((ADA_ROWS, D), lambda l: (0, 0)), pl.BlockSpec((1, ADA_ROWS, ADA_COLS), lambda l: (l, 0, 0))],
        out_specs=pl.BlockSpec((1, D, ADA_COLS), lambda l: (l, 0, 0)),
        compiler_params=_params(("parallel",)),
    )(c_pad, dmod_cols)


def _adamw_math(w, g, m, v):
    m = B1 * m + (1.0 - B1) * g
    v = B2 * v + (1.0 - B2) * (g * g)
    m_hat = m / (1.0 - B1 ** STEP)
    v_hat = v / (1.0 - B2 ** STEP)
    delta = -LR * (m_hat / (jnp.sqrt(v_hat) + EPS) + WD * w)
    return delta, m, v


def _row_tile(rows, target=256):
    best = 8
    for t in range(8, min(rows, target) + 1, 8):
        if rows % t == 0:
            best = t
    return best


def _adamw(w, g, m, v, name):
    layers, rows, cols = w.shape
    tr = _row_tile(rows)
    spec = pl.BlockSpec((1, tr, cols), lambda l, i: (l, i, 0))

    def body(w_ref, g_ref, m_ref, v_ref, d_ref, nm_ref, nv_ref):
        d_ref[...], nm_ref[...], nv_ref[...] = _adamw_math(w_ref[...], g_ref[...], m_ref[...], v_ref[...])

    out = jax.ShapeDtypeStruct(w.shape, F32)
    return pl.pallas_call(
        body, name=name, out_shape=(out, out, out), grid=(layers, rows // tr),
        in_specs=[spec] * 4, out_specs=(spec,) * 3, compiler_params=_params(("parallel", "parallel")),
    )(w, g, m, v)


def _sum_slabs(x, name):
    n, rows, _ = x.shape
    tr = _row_tile(rows)

    def body(x_ref, o_ref):
        acc = x_ref[0]
        for k in range(1, n):
            acc = acc + x_ref[k]
        o_ref[...] = acc

    return pl.pallas_call(
        body, name=name, out_shape=jax.ShapeDtypeStruct((rows, D), F32), grid=(rows // tr,),
        in_specs=[pl.BlockSpec((n, tr, D), lambda i: (0, i, 0))], out_specs=pl.BlockSpec((tr, D), lambda i: (i, 0)),
        compiler_params=_params(("parallel",)),
    )(x)


_ANY = pl.BlockSpec(memory_space=pl.ANY)
MESH = pl.DeviceIdType.MESH


def _on_sequencer(body, out_shape, sems, operands, after, sequencer_id, name):
    n = len(operands)

    def ordered_body(*refs):
        body(*refs[:n], *refs[n + 1:])

    extra = [] if after is None else [after]
    return pl.kernel(
        body if after is None else ordered_body, out_type=out_shape,
        mesh=plsc.ScalarSubcoreMesh(axis_name="sequencer", num_cores=1), scratch_types=sems,
        compiler_params=pltpu.CompilerParams(collective_id=sequencer_id), name=name)(*operands, *extra)


def _all_gather(xs, name, sequencer_id=None, after=None):
    n = len(xs)

    def body(*refs):
        x_refs, out_refs = refs[:n], refs[n:2 * n]
        send_sems, recv_sems, local_sems = refs[2 * n:]
        x_, y_, c_ = lax.axis_index("x"), lax.axis_index("y"), lax.axis_index("c")
        me, sibling = (x_, y_, c_), (x_, y_, 1 - c_)
        chips = [(1 - x_, y_), (x_, 1 - y_), (1 - x_, 1 - y_)]
        if sequencer_id is not None:
            barrier = pltpu.get_barrier_semaphore()
            peers = [sibling] + [(*chip, pc) for chip in chips for pc in (c_, 1 - c_)]
            for peer in peers:
                pl.semaphore_signal(barrier, inc=1, device_id=peer, device_id_type=MESH)
            pl.semaphore_wait(barrier, len(peers))

        def slot(a, px, py, pc):
            return out_refs[a].at[4 * px + 2 * py + pc]

        def copy(a, k, block, to, src=None):
            return pltpu.make_async_remote_copy(
                src_ref=slot(a, *block) if src is None else src, dst_ref=slot(a, *block),
                send_sem=send_sems.at[7 * a + k], recv_sem=recv_sems.at[7 * a + k], device_id=to, device_id_type=MESH)

        mine = [pltpu.make_async_copy(x_refs[a], slot(a, *me), local_sems.at[a]) for a in range(n)]
        for cp in mine:
            cp.start()
        first = []
        for a in range(n):
            first.append(copy(a, 0, me, sibling, src=x_refs[a]))
            first += [copy(a, 1 + j, me, (*chip, c_), src=x_refs[a]) for j, chip in enumerate(chips)]
        for cp in first:
            cp.start()
        passed = []
        for j, chip in enumerate(chips):
            for a in range(n):
                copy(a, 1 + j, (*chip, c_), me).wait_recv()
                passed.append(copy(a, 4 + j, (*chip, c_), sibling))
                passed[-1].start()
        for a in range(n):
            copy(a, 0, sibling, me).wait_recv()
        for j, chip in enumerate(chips):
            for a in range(n):
                copy(a, 4 + j, (*chip, 1 - c_), me).wait_recv()
        for cp in first + passed:
            cp.wait_send()
        for cp in mine:
            cp.wait()

    out_shape = [jax.ShapeDtypeStruct((N_DEV,) + x.shape, x.dtype) for x in xs]
    sems = [pltpu.SemaphoreType.DMA((7 * n,)), pltpu.SemaphoreType.DMA((7 * n,)), pltpu.SemaphoreType.DMA((n,))]
    if sequencer_id is not None:
        return _on_sequencer(body, out_shape, sems, xs, after, sequencer_id, name)
    return pl.pallas_call(
        body, name=name, out_shape=out_shape, in_specs=[_ANY] * n, out_specs=[_ANY] * n, scratch_shapes=sems)(*xs)


def _sibling_exchange(gs, name, sequencer_id=None, after=None):
    n = len(gs)

    def body(*refs):
        g_refs, p_refs = refs[:n], refs[n:2 * n]
        send_sems, recv_sems = refs[2 * n:]
        x_, y_, c_ = lax.axis_index("x"), lax.axis_index("y"), lax.axis_index("c")
        if sequencer_id is not None:
            barrier = pltpu.get_barrier_semaphore()
            pl.semaphore_signal(barrier, inc=1, device_id=(x_, y_, 1 - c_), device_id_type=MESH)
            pl.semaphore_wait(barrier, 1)
        copies = [pltpu.make_async_remote_copy(
            src_ref=g_refs[a].at[2 * k + (1 - c_)], dst_ref=p_refs[a].at[k], send_sem=send_sems.at[4 * a + k],
            recv_sem=recv_sems.at[4 * a + k], device_id=(x_, y_, 1 - c_), device_id_type=MESH)
            for a in range(n) for k in range(4)]
        for cp in copies:
            cp.start()
        for cp in copies:
            cp.wait()

    out_shape = [jax.ShapeDtypeStruct((4,) + g.shape[1:], g.dtype) for g in gs]
    sems = [pltpu.SemaphoreType.DMA((4 * n,)), pltpu.SemaphoreType.DMA((4 * n,))]
    if sequencer_id is not None:
        return _on_sequencer(body, out_shape, sems, gs, after, sequencer_id, name)
    return pl.pallas_call(
        body, name=name, out_shape=out_shape, in_specs=[_ANY] * n, out_specs=[_ANY] * n, scratch_shapes=sems)(*gs)


def _slab_tiles(rows, cols):
    if rows % 8 == 0:
        return _row_tile(rows), cols
    return rows, 2 * LANE


def _pair_sums(g, p, route, name):
    _, rows, cols = g.shape
    tr, tc = _slab_tiles(rows, cols)

    def body(route_ref, g_ref, p_ref, t_ref):
        t_ref[...] = (g_ref[...] + p_ref[...]).astype(BF16)

    return pl.pallas_call(
        body, name=name, out_shape=jax.ShapeDtypeStruct((3, rows, cols), BF16),
        grid_spec=pltpu.PrefetchScalarGridSpec(
            num_scalar_prefetch=1, grid=(3, rows // tr, cols // tc),
            in_specs=[pl.BlockSpec((1, tr, tc), lambda r, i, j, route_ref: (2 * route_ref[1 + r] + route_ref[0], i, j)),
                      pl.BlockSpec((1, tr, tc), lambda r, i, j, route_ref: (route_ref[1 + r], i, j))],
            out_specs=pl.BlockSpec((1, tr, tc), lambda r, i, j, route_ref: (r, i, j))),
        compiler_params=_params(("parallel", "parallel", "parallel")),
    )(route, g, p)


def _chip_exchange(ts, name, sequencer_id=None, after=None):
    n = len(ts)

    def body(*refs):
        t_refs, l_refs = refs[:n], refs[n:2 * n]
        send_sems, recv_sems = refs[2 * n:]
        x_, y_, c_ = lax.axis_index("x"), lax.axis_index("y"), lax.axis_index("c")
        chips = [(1 - x_, y_), (x_, 1 - y_), (1 - x_, 1 - y_)]
        if sequencer_id is not None:
            barrier = pltpu.get_barrier_semaphore()
            for px, py in chips:
                pl.semaphore_signal(barrier, inc=1, device_id=(px, py, c_), device_id_type=MESH)
            pl.semaphore_wait(barrier, len(chips))
        copies = [pltpu.make_async_remote_copy(
            src_ref=t_refs[a].at[r], dst_ref=l_refs[a].at[r], send_sem=send_sems.at[3 * a + r],
            recv_sem=recv_sems.at[3 * a + r], device_id=(px, py, c_), device_id_type=MESH)
            for a in range(n) for r, (px, py) in enumerate(chips)]
        for cp in copies:
            cp.start()
        for cp in copies:
            cp.wait()

    out_shape = [jax.ShapeDtypeStruct((3,) + t.shape[1:], t.dtype) for t in ts]
    sems = [pltpu.SemaphoreType.DMA((3 * n,)), pltpu.SemaphoreType.DMA((3 * n,))]
    if sequencer_id is not None:
        return _on_sequencer(body, out_shape, sems, ts, after, sequencer_id, name)
    return pl.pallas_call(
        body, name=name, out_shape=out_shape, in_specs=[_ANY] * n, out_specs=[_ANY] * n, scratch_shapes=sems)(*ts)


def _reduce_adamw(gs, ps, landed, place, w, m, v, name):
    layers, rows, cols = w.shape
    assert layers == DEPTH == 2
    tr, tc = _slab_tiles(rows, cols)
    nr, nc = rows // tr, cols // tc
    spec = pl.BlockSpec((1, tr, tc), lambda l, i, j, place_ref: (l, i, j))

    def own(layer, which):
        pi, pj = (nr - 1, nc - 1) if layer == 0 else (0, 0)

        def index(l, i, j, place_ref):
            lead = 0 if which is None else place_ref[which]
            return lead, jnp.where(l == layer, i, pi), jnp.where(l == layer, j, pj)

        return pl.BlockSpec((3 if which is None else 1, tr, tc), index)

    def body(place_ref, g0_ref, p0_ref, l0_ref, g1_ref, p1_ref, l1_ref, w_ref, m_ref, v_ref,
             g_ref, d_ref, nm_ref, nv_ref):
        def update(own_ref, sib_ref, l_ref):
            g = own_ref[0] + sib_ref[0] + l_ref[0].astype(F32) + l_ref[1].astype(F32) + l_ref[2].astype(F32)
            g_ref[0] = g
            d_ref[0], nm_ref[0], nv_ref[0] = _adamw_math(w_ref[0], g, m_ref[0], v_ref[0])

        @pl.when(pl.program_id(0) == 0)
        def _():
            update(g0_ref, p0_ref, l0_ref)

        @pl.when(pl.program_id(0) == 1)
        def _():
            update(g1_ref, p1_ref, l1_ref)

    out = jax.ShapeDtypeStruct(w.shape, F32)
    return pl.pallas_call(
        body, name=name, out_shape=(out, out, out, out),
        grid_spec=pltpu.PrefetchScalarGridSpec(
            num_scalar_prefetch=1, grid=(DEPTH, nr, nc),
            in_specs=[own(0, 0), own(0, 1), own(0, None), own(1, 0), own(1, 1), own(1, None), spec, spec, spec],
            out_specs=(spec, spec, spec, spec)),
        compiler_params=_params(("arbitrary", "arbitrary", "arbitrary")),
    )(place, gs[0], ps[0], landed[0], gs[1], ps[1], landed[1], w, m, v)


def _pack(pieces, row_multiple, dtype, cols=D, rows=None):
    flat = jnp.concatenate([p.astype(dtype).reshape(-1) for p in pieces])
    if rows is None:
        rows = -(-flat.shape[0] // cols)
        rows = -(-rows // row_multiple) * row_multiple
    flat = jnp.pad(flat, (0, rows * cols - flat.shape[0]))
    return flat.reshape(rows, cols)


def _unpack(flat, shapes, lead=()):
    out, off = [], 0
    for shp in shapes:
        n = 1
        for s_ in shp:
            n *= s_
        out.append(lax.slice_in_dim(flat, off, off + n, axis=len(lead)).reshape(lead + tuple(shp)))
        off += n
    return out


WIN_STRIDE = 704
WIN_ROWS = 720
Z_TURN = 1544


def _window(wt, me, name):
    padded = jnp.pad(wt, ((0, 0), (0, WIN_ROWS - IN_SHARD), (0, 0)))

    def body(me_ref, x_ref, o_ref):
        o_ref[0] = pltpu.roll(x_ref[0], me_ref[0], axis=0).astype(BF16)

    spec = pl.BlockSpec((1, WIN_ROWS, D), lambda l, me_ref: (l, 0, 0))
    return pl.pallas_call(
        body, name=name, out_shape=jax.ShapeDtypeStruct((DEPTH, WIN_ROWS, D), BF16),
        grid_spec=pltpu.PrefetchScalarGridSpec(num_scalar_prefetch=1, grid=(DEPTH,), in_specs=[spec], out_specs=spec),
        compiler_params=_params(("parallel",)),
    )(me, padded)


def _z_rows_from_windows(win):
    over = WIN_ROWS - WIN_STRIDE
    pieces = [(0, win[0][0:WIN_STRIDE])]
    for d in range(1, N_DEV):
        base = WIN_STRIDE * d
        pieces.append((base, win[d - 1][WIN_STRIDE:WIN_ROWS] + win[d][0:over]))
        pieces.append((base + over, win[d][over:WIN_STRIDE]))
    pieces.append((WIN_STRIDE * N_DEV, win[N_DEV - 1][WIN_STRIDE:WIN_ROWS]))

    def rows(a, b):
        out = []
        for start, arr in pieces:
            lo, hi = max(a, start), min(b, start + arr.shape[0])
            if lo < hi:
                out.append(arr[lo - start:hi - start])
        return out

    pad = jnp.zeros((NZ - IN_COLS, win.shape[-1]), win.dtype)
    return jnp.concatenate(rows(Z_TURN, IN_COLS) + rows(0, Z_TURN) + [pad], axis=0)


def _in_rows_from_z(wt):
    return jnp.concatenate([wt[Z_Q:Z_Q + 1536], wt[Z_F:Z_F + 8], wt[Z_PC:Z_PC + 1024], wt[Z_G:Z_G + 3072]], axis=0)


def _pad_rows(v, rows=8):
    return jnp.pad(v, ((0, rows - v.shape[0]), (0, 0)))


def _layer_fwd(l, x, wts, gvec, mod):
    tag = f"l{l}"
    h = _prenorm_fwd(x, gvec, mod, 0, 0, 1, f"prenorm_mix_{tag}")
    z = _matmul(h, wts["w_in_t"], "nt", f"in_proj_{tag}", tn=1152)
    qa, ka, va, kat = _attn_prep(z, wts["b_f"], f"attn_prep_{tag}")
    qa = wts["arrive"](qa)
    o, lse = _attn_fwd(qa, ka, va, f"attn_{tag}")
    br_b = _pool_fwd(z, wts["wp_bd"], wts["pool_scale"], f"pool_{tag}")
    br_c = _conv_fwd(z, wts["conv_w"], f"conv_{tag}")
    pa = _matmul(o, wts["wa"], "nn", f"proj_a_{tag}")
    pb = _matmul(br_b, wts["wb"], "nn", f"proj_b_{tag}")
    gates = [(z, Z_G + k * D) for k in range(3)]
    pc, merged = _matmul(br_c, wts["wc"], "nn", f"proj_c_merge_{tag}", tm=512, tn=512,
                         extra=gates + [(pa, 0), (pb, 0)], epilogue=_merge_epilogue, out_dtypes=(F32, BF16))
    y = _matmul(merged, wts["w_out"], "nn", f"out_proj_{tag}")
    x1 = _postnorm_fwd(x, y, gvec, mod, 1, 2, f"postnorm_mix_{tag}")
    h2 = _prenorm_fwd(x1, gvec, mod, 2, 3, 4, f"prenorm_ff_{tag}")
    a, r = _matmul(h2, wts["w_ff1"], "nn", f"ff1_{tag}", b_col_shards=True, epilogue=_relu2_epilogue,
                   out_dtypes=(F32, BF16))
    y2 = _matmul(r, wts["w_ff2"], "nn", f"ff2_{tag}")
    x2 = _postnorm_fwd(x1, y2, gvec, mod, 3, 5, f"postnorm_ff_{tag}")
    saved = dict(x=x, h=h, z=z, qa=qa, ka=ka, va=va, kat=kat, o=o, lse=lse, br_b=br_b, br_c=br_c, pa=pa, pb=pb, pc=pc,
                 merged=merged, y=y, x1=x1, h2=h2, a=a, r=r, y2=y2)
    return x2, saved


def _ffn_bwd(l, dx2, sv, wts, gvec, mod, midpoint):
    tag = f"l{l}"
    dy2, red_post_ff = _postnorm_bwd(sv["y2"], gvec, mod, dx2, 3, 5, f"postnorm_ff_bwd_{tag}")
    dy2 = midpoint(dy2)
    da = _matmul(dy2, wts["w_ff2"], "nt", f"ff2_dx_{tag}", extra=[(sv["a"], 0)], epilogue=_relu2_bwd_epilogue,
                 out_dtypes=(BF16,))[0]
    d_w_ff2 = _matmul(sv["r"], dy2, "tn", f"ff2_dw_{tag}")
    dh2 = _matmul(da, wts["w_ff1"], "nt", f"ff1_dx_{tag}", b_col_shards=True)
    d_w_ff1 = _matmul(sv["h2"], da, "tn", f"ff1_dw_{tag}", out_col_shards=True)
    dx1, red_pre_ff = _prenorm_bwd(sv["x1"], gvec, mod, dh2, dx2, 2, 4, f"prenorm_ff_bwd_{tag}")
    return dx1, [d_w_ff1, d_w_ff2.reshape(N_DEV, D_FF // N_DEV, D)], (red_pre_ff, red_post_ff)


def _mixer_bwd(l, dx1, sv, wts, gvec, mod, ffn_reds, midpoint):
    tag = f"l{l}"
    red_pre_ff, red_post_ff = ffn_reds
    dy, red_post_mix = _postnorm_bwd(sv["y"], gvec, mod, dx1, 1, 2, f"postnorm_mix_bwd_{tag}")
    gates = [(sv["z"], Z_G + k * D) for k in range(3)]
    dpa, dpb, dpc, *dgl = _matmul(dy, wts["w_out"], "nt", f"out_proj_dx_{tag}", tm=512, tn=512,
                                  extra=gates + [(sv["pa"], 0), (sv["pb"], 0), (sv["pc"], 0)],
                                  epilogue=_merge_bwd_epilogue, out_dtypes=(BF16,) * 6)
    d_w_out = _matmul(sv["merged"], dy, "tn", f"out_proj_dw_{tag}")
    dpa = midpoint(dpa)
    do = _matmul(dpa, wts["wa"], "nt", f"proj_a_dx_{tag}")
    dbr_b = _matmul(dpb, wts["wb"], "nt", f"proj_b_dx_{tag}")
    dbr_c = _matmul(dpc, wts["wc"], "nt", f"proj_c_dx_{tag}")
    d_wa = _matmul(sv["o"], dpa, "tn", f"proj_a_dw_{tag}")
    d_wb = _matmul(sv["br_b"], dpb, "tn", f"proj_b_dw_{tag}")
    d_wc = _matmul(sv["br_c"], dpc, "tn", f"proj_c_dw_{tag}")
    d_w_branch = jnp.concatenate([d_wa, d_wb, d_wc], axis=0)

    dpu, d_wp_bd, red_pool = _pool_bwd(sv["z"], wts["wp_bd"], wts["pool_scale"], dbr_b, f"pool_bwd_{tag}")
    dconv, red_conv = _conv_bwd(sv["z"], wts["conv_w"], dbr_c, f"conv_bwd_{tag}")
    qa2, doa = _attn_bwd_prep(sv["qa"], sv["o"], sv["lse"], do, f"attn_bwd_prep_{tag}")
    dqt, dka, dva = _attn_bwd(qa2, sv["ka"], sv["va"], sv["kat"], doa, f"attn_bwd_{tag}")
    dq, dk, dv, dfl, red_f = _attn_bwd_post(sv["z"], wts["b_f"], dqt, dka, dva, f"attn_bwd_post_{tag}")
    dz = jnp.concatenate([dpu, dconv, *dgl, dq, dk, dv, dfl], axis=1)
    dh = _matmul(dz, wts["w_in_t"], "nn", f"in_proj_dx_{tag}", tk=1920)
    d_w_in_t = _matmul(dz, sv["h"], "tn", f"in_proj_dw_{tag}", tm=1152)
    dx0, red_pre_mix = _prenorm_bwd(sv["x"], gvec, mod, dh, dx1, 0, 1, f"prenorm_mix_bwd_{tag}")

    rows = D // N_DEV
    big = [_in_rows_from_z(d_w_in_t).reshape(N_DEV, IN_SHARD, D), d_w_branch.reshape(N_DEV, rows, D),
           d_w_out.reshape(N_DEV, rows, D)]
    d_w_pool = jnp.stack([d_wp_bd[64 * g:64 * (g + 1), 64 * g:64 * (g + 1)] for g in range(4)])
    small = dict(
        mod=jnp.stack([red_pre_mix[0], red_pre_mix[1], red_post_mix[0], red_pre_ff[0], red_pre_ff[1], red_post_ff[0]]),
        g_mix_pre=red_pre_mix[2], g_mix_post=red_post_mix[1], g_ff_pre=red_pre_ff[2], g_ff_post=red_post_ff[1],
        b_f=red_f[0, 0:8], w_pool=d_w_pool, pool_scale=red_pool[0], conv_w=red_conv[0:3])
    return dx0, big, small


SMALL_KEYS = ["mod", "g_mix_pre", "g_mix_post", "g_ff_pre", "g_ff_post", "b_f", "w_pool", "pool_scale", "conv_w"]
SMALL_SHAPES = [(DEPTH, 6 * D), (DEPTH, D), (DEPTH, D), (DEPTH, D), (DEPTH, D), (DEPTH, 8), (DEPTH, 4, 64, 64),
                (DEPTH, POOL_W), (DEPTH, 3, CONV_W)]


def kernel(x, c, w_ada, b_ada, g_mix_pre, g_mix_post, g_ff_pre, g_ff_post, w_in, b_f, w_pool, pool_scale, conv_w, w_branch, w_out, w_ff1, w_ff2, loss_target, m_w_ada, m_b_ada, m_g_mix_pre, m_g_mix_post, m_g_ff_pre, m_g_ff_post, m_w_in, m_b_f, m_w_pool, m_pool_scale, m_conv_w, m_w_branch, m_w_out, m_w_ff1, m_w_ff2, v_w_ada, v_b_ada, v_g_mix_pre, v_g_mix_post, v_g_ff_pre, v_g_ff_post, v_w_in, v_b_f, v_w_pool, v_pool_scale, v_conv_w, v_w_branch, v_w_out, v_w_ff1, v_w_ff2):
    ix, iy, ic = lax.axis_index("x"), lax.axis_index("y"), lax.axis_index("c")
    me = 4 * ix + 2 * iy + ic
    route = jnp.stack([ic, 2 * (1 - ix) + iy, 2 * ix + (1 - iy), 2 * (1 - ix) + (1 - iy)]).astype(jnp.int32)
    place = jnp.stack([me, 2 * ix + iy]).astype(jnp.int32)
    wt_in, mt_in, vt_in = (jnp.transpose(a, (0, 2, 1)) for a in (w_in, m_w_in, v_w_in))

    c_all = _all_gather([_pad_rows(c)], "gather_c")[0][:, 0, :]
    c_pad = _pad_rows(c_all, ADA_ROWS)
    b_cols = lax.dynamic_slice_in_dim(b_ada, me * ADA_COLS, ADA_COLS, axis=1)
    b_cols = jnp.broadcast_to(b_cols[:, None, :], (DEPTH, 8, ADA_COLS))
    mod_part = _ada_fwd(c_pad, w_ada, b_cols, "ada_fwd")
    mod_all = _all_gather([mod_part.reshape(DEPTH * ADA_ROWS, ADA_COLS)], "gather_mod")[0]
    mod_all = mod_all.reshape(N_DEV, DEPTH, ADA_ROWS, ADA_COLS)
    mod_mine = lax.dynamic_index_in_dim(mod_all, me, axis=2, keepdims=False)
    mod_mine = jnp.transpose(mod_mine, (1, 0, 2)).reshape(DEPTH, 6, D)

    cw_cols = CONV_W // N_DEV
    cw_send = jnp.pad(conv_w.reshape(DEPTH * 3, cw_cols), ((0, 8 - DEPTH * 3), (0, LANE - cw_cols)))
    win_in = _window(wt_in, place[0:1], "w_in_window")
    send = [[w[l].astype(BF16) for w in (win_in, w_branch, w_out, w_ff1, w_ff2)] for l in range(DEPTH)]
    first = _all_gather(send[0][:1], "gather_weights_l0_in", sequencer_id=1, after=mod_all)
    rest = _all_gather(send[0][1:] + [cw_send], "gather_weights_l0_rest", sequencer_id=2, after=first[0])
    first1 = _all_gather(send[1][:1], "gather_weights_l1_in", sequencer_id=3, after=first[0])
    rest1 = _all_gather(send[1][1:], "gather_weights_l1_rest", sequencer_id=12, after=first[0])
    gathered = [first + rest[:4], first1 + rest1]
    cw_all = rest[4][:, :DEPTH * 3, :cw_cols].reshape(N_DEV, DEPTH, 3, cw_cols)

    def first_operands(l, p_in):
        wp_bd = jnp.zeros((POOL_W, POOL_W), F32)
        for g in range(4):
            wp_bd = wp_bd.at[64 * g:64 * (g + 1), 64 * g:64 * (g + 1)].set(w_pool[l, g])
        return dict(w_in_t=_z_rows_from_windows(p_in), wp_bd=wp_bd.astype(BF16),
                    pool_scale=_pad_rows(pool_scale[l][None, :]), b_f=_pad_rows(jnp.pad(b_f[l], (0, LANE - 8))[None, :]))

    def rest_operands(l, rest):
        p_br, p_out, p_ff1, p_ff2 = rest
        w_br_full = p_br.reshape(D, D)
        cw_full = jnp.transpose(cw_all[:, l], (1, 0, 2)).reshape(3, CONV_W)
        return dict(wa=w_br_full[0:A_WIDTH], wb=w_br_full[A_WIDTH:A_WIDTH + POOL_W], wc=w_br_full[A_WIDTH + POOL_W:],
                    w_out=p_out.reshape(D, D), w_ff1=p_ff1, w_ff2=p_ff2.reshape(D_FF, D), conv_w=_pad_rows(cw_full))

    xs = x[0]
    saved, layers = [], []
    for l in range(DEPTH):
        p_in, rest = gathered[l][0], gathered[l][1:5]
        if l > 0:
            xs, p_in = lax.optimization_barrier((xs, p_in))
        wts = first_operands(l, p_in)

        def arrive(t, l=l, rest=rest, wts=wts):
            if l > 0:
                t, rest = lax.optimization_barrier((t, rest))
            wts.update(rest_operands(l, rest))
            return t

        wts["arrive"] = arrive
        gvec = _pad_rows(jnp.stack([g_mix_pre[l], g_mix_post[l], g_ff_pre[l], g_ff_post[l]]))
        layers.append((wts, gvec, _pad_rows(mod_mine[l])))
        xs, sv = _layer_fwd(l, xs, *layers[l])
        saved.append(sv)
    dx, loss_part = _loss_head(xs, loss_target[0], "loss_head")
    loss = lax.psum(loss_part[0, 0], ("x", "y", "c"))
    small_grads = [None] * DEPTH
    mine, sibs, landed = ({} for _ in range(3))
    seq_id = iter(range(4, 4 + 4 * DEPTH))
    last = [gathered[DEPTH - 1][1]]

    def start(group, grads):
        mine[group] = grads
        sibs[group] = _sibling_exchange(grads, f"rs_sibling_{group}", sequencer_id=next(seq_id), after=last[0])
        last[0] = sibs[group][0]

    def finish(group, later):
        later, (grads, sib) = lax.optimization_barrier((later, (mine[group], sibs[group])))
        sends = [_pair_sums(g, p, route, f"rs_pair_sums_{group}_{k}") for k, (g, p) in enumerate(zip(grads, sib))]
        later, sends = lax.optimization_barrier((later, sends))
        landed[group] = _chip_exchange(sends, f"rs_chips_{group}", sequencer_id=next(seq_id), after=last[0])
        last[0] = landed[group][0]
        return later

    pending = None
    for l in reversed(range(DEPTH)):
        hook = (lambda da: da) if pending is None else functools.partial(finish, pending)
        dx, ffn_grads, ffn_reds = _ffn_bwd(l, dx, saved[l], *layers[l], hook)
        start(f"ffn_l{l}", ffn_grads)
        dx, mix_grads, small_grads[l] = _mixer_bwd(l, dx, saved[l], *layers[l], ffn_reds,
                                                   functools.partial(finish, f"ffn_l{l}"))
        start(f"mix_l{l}", mix_grads)
        pending = f"mix_l{l}"
    grad_x = dx[None]

    big_w = [wt_in, w_branch, w_out, w_ff1, w_ff2]
    big_m = [mt_in, m_w_branch, m_w_out, m_w_ff1, m_w_ff2]
    big_v = [vt_in, v_w_branch, v_w_out, v_w_ff1, v_w_ff2]
    where = [("mix", 0), ("mix", 1), ("mix", 2), ("ffn", 0), ("ffn", 1)]

    def reduce_and_update(k):
        group, at = where[k]
        return _reduce_adamw([mine[f"{group}_l{l}"][at] for l in range(DEPTH)],
                             [sibs[f"{group}_l{l}"][at] for l in range(DEPTH)],
                             [landed[f"{group}_l{l}"][at] for l in range(DEPTH)], place, big_w[k], big_m[k], big_v[k],
                             f"rs_sum_adamw_{k}")

    big_res = {k: list(reduce_and_update(k)) for k in (3, 4)}
    big_res[3][0] = finish(pending, big_res[3][0])

    small = {k: jnp.stack([small_grads[l][k] for l in range(DEPTH)]) for k in SMALL_KEYS}
    small_all = _all_gather([_pack([small[k] for k in SMALL_KEYS], 8, F32)], "gather_small")[0]
    dmod_all = small_all[:, 0:DEPTH * 6, :].reshape(N_DEV, DEPTH, 6 * D)
    summed = _unpack(_sum_slabs(small_all, "sum_small").reshape(-1), SMALL_SHAPES)
    sg = dict(zip(SMALL_KEYS, summed))
    dmod_cols = lax.dynamic_slice_in_dim(dmod_all, me * ADA_COLS, ADA_COLS, axis=2)
    dmod_cols = jnp.pad(jnp.transpose(dmod_cols, (1, 0, 2)), ((0, 0), (0, ADA_ROWS - N_DEV), (0, 0)))
    g_w_ada = _ada_bwd(c_pad, dmod_cols, "ada_bwd")
    g_conv_w = lax.dynamic_slice_in_dim(sg["conv_w"], me * (CONV_W // N_DEV), CONV_W // N_DEV, axis=2)

    ada_out = [g_w_ada] + list(_adamw(w_ada, g_w_ada, m_w_ada, v_w_ada, "adamw_ada"))
    rest_w = [b_ada, g_mix_pre, g_mix_post, g_ff_pre, g_ff_post, b_f, w_pool, pool_scale, conv_w]
    rest_m = [m_b_ada, m_g_mix_pre, m_g_mix_post, m_g_ff_pre, m_g_ff_post, m_b_f, m_w_pool, m_pool_scale, m_conv_w]
    rest_v = [v_b_ada, v_g_mix_pre, v_g_mix_post, v_g_ff_pre, v_g_ff_post, v_b_f, v_w_pool, v_pool_scale, v_conv_w]
    rest_g = [sg["mod"], sg["g_mix_pre"], sg["g_mix_post"], sg["g_ff_pre"], sg["g_ff_post"], sg["b_f"],
              sg["w_pool"], sg["pool_scale"], g_conv_w]
    rest_shapes = [a.shape for a in rest_w]
    upd = _adamw(_pack(rest_w, 8, F32)[None], _pack(rest_g, 8, F32)[None], _pack(rest_m, 8, F32)[None],
                 _pack(rest_v, 8, F32)[None], "adamw_rest")
    rest_out = [rest_g] + [_unpack(arr.reshape(-1), rest_shapes) for arr in upd]
    rest_out = [[ada_out[which]] + rest_out[which] for which in range(4)]

    landed[pending], rest_out = lax.optimization_barrier((landed[pending], rest_out))
    big_res.update({k: reduce_and_update(k) for k in (0, 1, 2)})
    big_out = [[jnp.transpose(big_res[k][which], (0, 2, 1)) if k == 0 else big_res[k][which] for k in range(5)]
               for which in range(4)]

    def ordered(k):
        r, b = rest_out[k], big_out[k]
        return [r[0], r[1], r[2], r[3], r[4], r[5], b[0], r[6], r[7], r[8], r[9], b[1], b[2], b[3], b[4]]

    return (loss, grad_x, *ordered(0), *ordered(1), *ordered(2), *ordered(3))
```

```python
import functools

import jax
import jax.numpy as jnp
from jax import lax
from jax.experimental import pallas as pl
from jax.experimental.pallas import tpu as pltpu
from jax.experimental.pallas import tpu_sc as plsc

F32 = jnp.float32
BF16 = jnp.bfloat16

N_DEV = 8
D = 1024
S = 2048
DEPTH = 2
D_FF = 4 * D
A_WIDTH = 512
HEAD_DIM = 64
N_PAIR = 4
POOL_W = 256
CONV_W = 256
IN_COLS = 5640
ADA_COLS = 6 * D // N_DEV
IN_SHARD = IN_COLS // N_DEV
RMS_EPS = 1e-6
NEG_INF = -1e30
ATT_SCALE = HEAD_DIM ** -0.5

NZ = 5760
Z_PC = 0
Z_G = 1024
Z_Q = 4096
Z_K = 4608
Z_V = 5120
Z_F = 5632

LR, B1, B2, EPS, WD, STEP = 0.001, 0.9, 0.999, 1e-08, 0.01, 10

LANE = 128
VMEM_LIMIT_BYTES = 48 * 1024 * 1024
TS = 512
TQ = 256
TQ_FWD = 512
HEADS_PER_STEP = 4
HEADS_PER_STEP_FWD = 8


def _params(sem=None):
    return pltpu.CompilerParams(dimension_semantics=sem, vmem_limit_bytes=VMEM_LIMIT_BYTES)


def _pick(n, target):
    best = None
    for t in range(LANE, min(n, target) + 1, LANE):
        if n % t == 0:
            best = t
    return n if best is None else best


def _matmul(a, b, mode, name, out_dtype=F32, tm=2048, tn=1024, tk=2048, b_col_shards=False, out_col_shards=False,
            extra=(), epilogue=None, out_dtypes=None):
    if b_col_shards:
        shards, b_rows, shard_cols = b.shape
        b_shape = (b_rows, shards * shard_cols)
    else:
        b_shape = b.shape
    if mode == "nn":
        (m, k), (k2, n) = a.shape, b_shape
    elif mode == "nt":
        (m, k), (n, k2) = a.shape, b_shape
    else:
        (k, m), (k2, n) = a.shape, b_shape
    assert k == k2, (a.shape, b.shape, mode)
    tm, tn, tk = _pick(m, tm), _pick(n, tn), _pick(k, tk)
    if b_col_shards and mode == "nn":
        tn = shard_cols
    per_step = 1
    if b_col_shards and mode == "nt":
        per_step = max(1, min(tk, 1024) // shard_cols)
        tk = per_step * shard_cols
    if out_col_shards:
        tn = n // N_DEV
    nk = k // tk
    if mode == "nn":
        a_spec = pl.BlockSpec((tm, tk), lambda i, j, kk: (i, kk))
        b_spec = (pl.BlockSpec((None, tk, tn), lambda i, j, kk: (j, kk, 0)) if b_col_shards else
                  pl.BlockSpec((tk, tn), lambda i, j, kk: (kk, j)))
        dims = (((1,), (0,)), ((), ()))
    elif mode == "nt":
        a_spec = pl.BlockSpec((tm, tk), lambda i, j, kk: (i, kk))
        b_spec = (pl.BlockSpec((per_step, tn, shard_cols), lambda i, j, kk: (kk, j, 0)) if b_col_shards else
                  pl.BlockSpec((tn, tk), lambda i, j, kk: (j, kk)))
        dims = (((1,), (1,)), ((), ()))
    else:
        assert not b_col_shards
        a_spec = pl.BlockSpec((tk, tm), lambda i, j, kk: (kk, i))
        b_spec = pl.BlockSpec((tk, tn), lambda i, j, kk: (kk, j))
        dims = (((0,), (0,)), ((), ()))
    if out_col_shards:
        out_shape = jax.ShapeDtypeStruct((N_DEV, m, tn), out_dtype)
        out_spec = pl.BlockSpec((None, tm, tn), lambda i, j, kk: (j, i, 0))
    else:
        out_shape = jax.ShapeDtypeStruct((m, n), out_dtype)
        out_spec = pl.BlockSpec((tm, tn), lambda i, j, kk: (i, j))

    n_extra = len(extra)
    extra_specs = [pl.BlockSpec((tm, tn), lambda i, j, kk, off=off: (i, j + off // tn)) for _, off in extra]
    if epilogue is not None:
        assert not out_col_shards and all(off % tn == 0 for _, off in extra)
        out_shape = [jax.ShapeDtypeStruct((m, n), dt) for dt in out_dtypes]
        out_spec = [pl.BlockSpec((tm, tn), lambda i, j, kk: (i, j)) for _ in out_dtypes]

    def product(a_ref, b_ref):
        if b_col_shards and mode == "nt":
            b_tile = jnp.concatenate([b_ref[s] for s in range(per_step)], axis=1) if per_step > 1 else b_ref[0]
        else:
            b_tile = b_ref[...]
        return lax.dot_general(a_ref[...].astype(BF16), b_tile.astype(BF16), dims, preferred_element_type=F32)

    def write(acc, extra_refs, o_refs):
        if epilogue is None:
            o_refs[0][...] = acc.astype(out_dtype)
        else:
            for o_ref, tile in zip(o_refs, epilogue(acc, *[r[...] for r in extra_refs])):
                o_ref[...] = tile.astype(o_ref.dtype)

    def body_one_pass(a_ref, b_ref, *refs):
        write(product(a_ref, b_ref), refs[:n_extra], refs[n_extra:])

    def body(a_ref, b_ref, *refs):
        acc_ref = refs[-1]
        kk = pl.program_id(2)

        @pl.when(kk == 0)
        def _():
            acc_ref[...] = product(a_ref, b_ref)

        @pl.when(kk > 0)
        def _():
            acc_ref[...] += product(a_ref, b_ref)

        @pl.when(kk == nk - 1)
        def _():
            write(acc_ref[...], refs[:n_extra], refs[n_extra:-1])

    return pl.pallas_call(
        body_one_pass if nk == 1 else body, name=name,
        out_shape=out_shape,
        grid=(m // tm, n // tn, nk),
        in_specs=[a_spec, b_spec] + extra_specs,
        out_specs=out_spec,
        scratch_shapes=[] if nk == 1 else [pltpu.VMEM((tm, tn), F32)],
        compiler_params=_params(("parallel", "parallel", "arbitrary")),
    )(a, b, *[x for x, _ in extra])


def _row_spec(width=D, col=0):
    return pl.BlockSpec((TS, width), lambda i: (i, col))


def _vec_spec(rows=8, width=D):
    return pl.BlockSpec((rows, width), lambda i: (0, 0))


def _rms(x):
    return lax.rsqrt(jnp.mean(x * x, axis=-1, keepdims=True) + RMS_EPS)


def _prenorm_fwd(x, gvec, mod, g_row, shift_row, scale_row, name):
    def body(x_ref, g_ref, mod_ref, h_ref):
        xv = x_ref[...]
        y = xv * _rms(xv) * g_ref[g_row:g_row + 1, :]
        h = y * (1.0 + mod_ref[scale_row:scale_row + 1, :]) + mod_ref[shift_row:shift_row + 1, :]
        h_ref[...] = h.astype(BF16)

    return pl.pallas_call(
        body, name=name, out_shape=jax.ShapeDtypeStruct((S, D), BF16), grid=(S // TS,),
        in_specs=[_row_spec(), _vec_spec(), _vec_spec()], out_specs=_row_spec(),
        compiler_params=_params(("parallel",)),
    )(x, gvec, mod)


def _prenorm_bwd(x, gvec, mod, dh, dres, g_row, scale_row, name):
    def body(x_ref, g_ref, mod_ref, dh_ref, dres_ref, dx_ref, red_ref):
        i = pl.program_id(0)

        @pl.when(i == 0)
        def _():
            red_ref[...] = jnp.zeros_like(red_ref)

        xv = x_ref[...]
        g = g_ref[g_row:g_row + 1, :]
        r = _rms(xv)
        n = xv * r
        yg = n * g
        dhv = dh_ref[...]
        dyg = dhv * (1.0 + mod_ref[scale_row:scale_row + 1, :])
        dn = dyg * g
        dx = r * (dn - n * jnp.mean(dn * n, axis=-1, keepdims=True))
        dx_ref[...] = dres_ref[...] + dx
        red_ref[0:1, :] += jnp.sum(dhv, axis=0, keepdims=True)
        red_ref[1:2, :] += jnp.sum(dhv * yg, axis=0, keepdims=True)
        red_ref[2:3, :] += jnp.sum(dyg * n, axis=0, keepdims=True)

    return pl.pallas_call(
        body, name=name,
        out_shape=(jax.ShapeDtypeStruct((S, D), F32), jax.ShapeDtypeStruct((8, D), F32)),
        grid=(S // TS,),
        in_specs=[_row_spec(), _vec_spec(), _vec_spec(), _row_spec(), _row_spec()],
        out_specs=(_row_spec(), _vec_spec()),
        compiler_params=_params(("arbitrary",)),
    )(x, gvec, mod, dh, dres)


def _postnorm_fwd(x, y, gvec, mod, g_row, gate_row, name):
    def body(x_ref, y_ref, g_ref, mod_ref, o_ref):
        yv = y_ref[...]
        yn = yv * _rms(yv) * g_ref[g_row:g_row + 1, :]
        o_ref[...] = x_ref[...] + mod_ref[gate_row:gate_row + 1, :] * yn

    return pl.pallas_call(
        body, name=name, out_shape=jax.ShapeDtypeStruct((S, D), F32), grid=(S // TS,),
        in_specs=[_row_spec(), _row_spec(), _vec_spec(), _vec_spec()], out_specs=_row_spec(),
        compiler_params=_params(("parallel",)),
    )(x, y, gvec, mod)


def _postnorm_bwd(y, gvec, mod, dxo, g_row, gate_row, name):
    def body(y_ref, g_ref, mod_ref, dxo_ref, dy_ref, red_ref):
        i = pl.program_id(0)

        @pl.when(i == 0)
        def _():
            red_ref[...] = jnp.zeros_like(red_ref)

        yv = y_ref[...]
        g = g_ref[g_row:g_row + 1, :]
        r = _rms(yv)
        n = yv * r
        dxo = dxo_ref[...]
        dyn = dxo * mod_ref[gate_row:gate_row + 1, :]
        dn = dyn * g
        dy = r * (dn - n * jnp.mean(dn * n, axis=-1, keepdims=True))
        dy_ref[...] = dy.astype(BF16)
        red_ref[0:1, :] += jnp.sum(dxo * (n * g), axis=0, keepdims=True)
        red_ref[1:2, :] += jnp.sum(dyn * n, axis=0, keepdims=True)

    return pl.pallas_call(
        body, name=name,
        out_shape=(jax.ShapeDtypeStruct((S, D), BF16), jax.ShapeDtypeStruct((8, D), F32)),
        grid=(S // TS,),
        in_specs=[_row_spec(), _vec_spec(), _vec_spec(), _row_spec()],
        out_specs=(_row_spec(), _vec_spec()),
        compiler_params=_params(("arbitrary",)),
    )(y, gvec, mod, dxo)


def _loss_head(xf, target, name):
    def body(x_ref, t_ref, dx_ref, loss_ref):
        i = pl.program_id(0)

        @pl.when(i == 0)
        def _():
            loss_ref[...] = jnp.zeros_like(loss_ref)

        e = x_ref[...] - t_ref[...]
        dx_ref[...] = e / float(D)
        per_tok = jnp.mean(e * e, axis=-1, keepdims=True)
        loss_ref[0:1, 0:1] += 0.5 * jnp.sum(per_tok, axis=0, keepdims=True)

    return pl.pallas_call(
        body, name=name,
        out_shape=(jax.ShapeDtypeStruct((S, D), F32), jax.ShapeDtypeStruct((8, LANE), F32)),
        grid=(S // TS,),
        in_specs=[_row_spec(), _row_spec()],
        out_specs=(_row_spec(), pl.BlockSpec((8, LANE), lambda i: (0, 0))),
        compiler_params=_params(("arbitrary",)),
    )(xf, target)


def _relu2_epilogue(a):
    t = jnp.maximum(a, 0.0)
    return a, t * t


def _relu2_bwd_epilogue(dr, a):
    return (dr * (2.0 * jnp.maximum(a, 0.0)),)


def _merge_epilogue(pc, g0, g1, g2, pa, pb):
    return pc, jax.nn.sigmoid(g0) * pa + jax.nn.sigmoid(g1) * pb + jax.nn.sigmoid(g2) * pc


def _merge_bwd_epilogue(dm, g0, g1, g2, pa, pb, pc):
    sg = [jax.nn.sigmoid(g) for g in (g0, g1, g2)]
    return tuple(dm * s for s in sg) + tuple(dm * p * (s * (1.0 - s)) for p, s in zip((pa, pb, pc), sg))


def _shift_down(x, k, row):
    return jnp.where(row >= k, pltpu.roll(x, k, axis=0), 0.0)


def _shift_up(x, k, row):
    n = x.shape[0]
    return jnp.where(row < n - k, pltpu.roll(x, n - k, axis=0), 0.0)


def _cumsum_rows(x, row, reverse=False):
    shift = _shift_up if reverse else _shift_down
    k = 1
    while k < x.shape[0]:
        x = x + shift(x, k, row)
        k *= 2
    return x


def _full_spec(shape, idx=(0, 0)):
    return pl.BlockSpec(shape, lambda i: idx)


def _pool_window_select(lane, a2, a4, a8, a16):
    return jnp.where(lane < 64, a2, jnp.where(lane < 128, a4, jnp.where(lane < 192, a8, a16)))


def _pool_p(u, row, lane):
    t2 = u + _shift_down(u, 1, row)
    t4 = t2 + _shift_down(t2, 2, row)
    t8 = t4 + _shift_down(t4, 4, row)
    t16 = t8 + _shift_down(t8, 8, row)
    tw = _pool_window_select(lane, t2, t4, t8, t16)
    cnt = jnp.minimum((row + 1).astype(F32), _pool_window_select(lane, 2.0, 4.0, 8.0, 16.0))
    return tw / cnt - u, cnt


def _pool_fwd(z, wp_bd, pscale, name):
    def body(u_ref, w_ref, s_ref, o_ref):
        row = lax.broadcasted_iota(jnp.int32, (S, POOL_W), 0)
        lane = lax.broadcasted_iota(jnp.int32, (S, POOL_W), 1)
        p, _ = _pool_p(u_ref[...], row, lane)
        y = jnp.dot(p.astype(BF16), w_ref[...], preferred_element_type=F32)
        o_ref[...] = y * s_ref[0:1, :]

    return pl.pallas_call(
        body, name=name, out_shape=jax.ShapeDtypeStruct((S, POOL_W), F32), grid=(1,),
        in_specs=[_full_spec((S, POOL_W), (0, Z_PC // POOL_W)), _full_spec((POOL_W, POOL_W)), _full_spec((8, POOL_W))],
        out_specs=_full_spec((S, POOL_W)),
        compiler_params=_params(("arbitrary",)),
    )(z, wp_bd, pscale)


def _pool_bwd(z, wp_bd, pscale, dbr, name):
    def body(u_ref, w_ref, s_ref, dbr_ref, du_ref, dw_ref, red_ref):
        row = lax.broadcasted_iota(jnp.int32, (S, POOL_W), 0)
        lane = lax.broadcasted_iota(jnp.int32, (S, POOL_W), 1)
        p, cnt = _pool_p(u_ref[...], row, lane)
        pb = p.astype(BF16)
        y = jnp.dot(pb, w_ref[...], preferred_element_type=F32)
        dbr = dbr_ref[...]
        red_ref[...] = jnp.zeros_like(red_ref)
        red_ref[0:1, :] = jnp.sum(dbr * y, axis=0, keepdims=True)
        dy = (dbr * s_ref[0:1, :]).astype(BF16)
        dw_ref[...] = lax.dot_general(pb, dy, (((0,), (0,)), ((), ())), preferred_element_type=F32)
        dp = lax.dot_general(dy, w_ref[...], (((1,), (1,)), ((), ())), preferred_element_type=F32)
        g = dp / cnt
        a2 = g + _shift_up(g, 1, row)
        a4 = a2 + _shift_up(a2, 2, row)
        a8 = a4 + _shift_up(a4, 4, row)
        a16 = a8 + _shift_up(a8, 8, row)
        du_ref[...] = (_pool_window_select(lane, a2, a4, a8, a16) - dp).astype(BF16)

    return pl.pallas_call(
        body, name=name,
        out_shape=(jax.ShapeDtypeStruct((S, POOL_W), BF16), jax.ShapeDtypeStruct((POOL_W, POOL_W), F32),
                   jax.ShapeDtypeStruct((8, POOL_W), F32)),
        grid=(1,),
        in_specs=[_full_spec((S, POOL_W), (0, Z_PC // POOL_W)), _full_spec((POOL_W, POOL_W)), _full_spec((8, POOL_W)),
                  _full_spec((S, POOL_W))],
        out_specs=(_full_spec((S, POOL_W)), _full_spec((POOL_W, POOL_W)), _full_spec((8, POOL_W))),
        compiler_params=_params(("arbitrary",)),
    )(z, wp_bd, pscale, dbr)


def _conv_specs():
    base = Z_PC // CONV_W
    return [_full_spec((S, CONV_W), (0, base + 1)), _full_spec((S, CONV_W), (0, base + 2)),
            _full_spec((S, CONV_W), (0, base + 3)), _full_spec((8, CONV_W))]


def _conv_fwd(z, cw, name):
    def body(h_ref, b_ref, c_ref, w_ref, o_ref):
        row = lax.broadcasted_iota(jnp.int32, (S, CONV_W), 0)
        u = c_ref[...] * h_ref[...]
        y = (w_ref[0:1, :] * _shift_down(u, 2, row) + w_ref[1:2, :] * _shift_down(u, 1, row) + w_ref[2:3, :] * u)
        o_ref[...] = b_ref[...] * y

    return pl.pallas_call(
        body, name=name, out_shape=jax.ShapeDtypeStruct((S, CONV_W), F32), grid=(1,),
        in_specs=_conv_specs(), out_specs=_full_spec((S, CONV_W)),
        compiler_params=_params(("arbitrary",)),
    )(z, z, z, cw)


def _conv_bwd(z, cw, dbr, name):
    def body(h_ref, b_ref, c_ref, w_ref, dbr_ref, d_ref, red_ref):
        row = lax.broadcasted_iota(jnp.int32, (S, CONV_W), 0)
        h, cg = h_ref[...], c_ref[...]
        u = cg * h
        u1 = _shift_down(u, 1, row)
        u2 = _shift_down(u, 2, row)
        y = w_ref[0:1, :] * u2 + w_ref[1:2, :] * u1 + w_ref[2:3, :] * u
        dbr = dbr_ref[...]
        dy = dbr * b_ref[...]
        du = w_ref[2:3, :] * dy + w_ref[1:2, :] * _shift_up(dy, 1, row) + w_ref[0:1, :] * _shift_up(dy, 2, row)
        d_ref[:, 0:CONV_W] = (du * cg).astype(BF16)
        d_ref[:, CONV_W:2 * CONV_W] = (dbr * y).astype(BF16)
        d_ref[:, 2 * CONV_W:3 * CONV_W] = (du * h).astype(BF16)
        red_ref[...] = jnp.zeros_like(red_ref)
        red_ref[0:1, :] = jnp.sum(dy * u2, axis=0, keepdims=True)
        red_ref[1:2, :] = jnp.sum(dy * u1, axis=0, keepdims=True)
        red_ref[2:3, :] = jnp.sum(dy * u, axis=0, keepdims=True)

    return pl.pallas_call(
        body, name=name,
        out_shape=(jax.ShapeDtypeStruct((S, 3 * CONV_W), BF16), jax.ShapeDtypeStruct((8, CONV_W), F32)),
        grid=(1,),
        in_specs=_conv_specs() + [_full_spec((S, CONV_W))],
        out_specs=(_full_spec((S, 3 * CONV_W)), _full_spec((8, CONV_W))),
        compiler_params=_params(("arbitrary",)),
    )(z, z, z, cw, dbr)


_NT = (((1,), (1,)), ((), ()))
_TN = (((0,), (0,)), ((), ()))
N_HEAD = 2 * N_PAIR


def _split3(x):
    hi = x.astype(BF16).astype(F32)
    mid = (x - hi).astype(BF16).astype(F32)
    lo = (x - hi - mid).astype(BF16).astype(F32)
    return hi, mid, lo


def _spare(lane, e, k):
    return lane == 64 * (1 - e) + k


def _spare3(lane, e, k):
    base = 64 * (1 - e) + k
    return (lane >= base) & (lane < base + 3)


def _put3(lane, e, k, pieces, rest):
    out = rest
    for n, piece in enumerate(pieces):
        out = jnp.where(_spare(lane, e, k + n), piece, out)
    return out


def _attn_prep(z, bf, name):
    def body(q_ref, k_ref, v_ref, f_ref, b_ref, qa_ref, ka_ref, va_ref, kat_ref, cum_ref):
        p = pl.program_id(0)
        row = lax.broadcasted_iota(jnp.int32, (S, LANE), 0)
        lane = lax.broadcasted_iota(jnp.int32, (S, LANE), 1)

        @pl.when(p == 0)
        def _():
            xv = f_ref[...] + b_ref[0:1, :]
            ls = jnp.minimum(xv, 0.0) - jnp.log(1.0 + jnp.exp(-jnp.abs(xv)))
            cum_ref[...] = _cumsum_rows(jnp.where(lane < N_HEAD, ls, 0.0), row)

        cum = cum_ref[...]
        q, k, v = q_ref[...], k_ref[...], v_ref[...]
        for e in range(2):
            head = (lane >= 64) if e else (lane < 64)
            f = jnp.sum(jnp.where(lane == 2 * p + e, cum, 0.0), axis=1, keepdims=True)
            pieces = _split3(f)
            qa = jnp.where(head, q * ATT_SCALE, _put3(lane, e, 0, pieces, jnp.where(_spare3(lane, e, 3), 1.0, 0.0)))
            ones = jnp.where(_spare3(lane, e, 0) | _spare3(lane, e, 6), 1.0, 0.0)
            ka = jnp.where(head, k, _put3(lane, e, 3, [-x for x in pieces], ones))
            va = jnp.where(head, v, jnp.where(_spare3(lane, e, 0), 1.0, 0.0))
            qa_ref[e] = qa.astype(BF16)
            ka_ref[e] = ka.astype(BF16)
            va_ref[e] = va.astype(BF16)
            kat_ref[e] = ka.T.astype(BF16)

    qb, kb, vb = Z_Q // LANE, Z_K // LANE, Z_V // LANE
    heads = jax.ShapeDtypeStruct((N_HEAD, S, LANE), BF16)
    pair = pl.BlockSpec((2, S, LANE), lambda p: (p, 0, 0))
    return pl.pallas_call(
        body, name=name,
        out_shape=(heads, heads, heads, jax.ShapeDtypeStruct((N_HEAD, LANE, S), BF16)),
        grid=(N_PAIR,),
        in_specs=[pl.BlockSpec((S, LANE), lambda p: (0, qb + p)), pl.BlockSpec((S, LANE), lambda p: (0, kb + p)),
                  pl.BlockSpec((S, LANE), lambda p: (0, vb + p)), pl.BlockSpec((S, LANE), lambda p: (0, Z_F // LANE)),
                  pl.BlockSpec((8, LANE), lambda p: (0, 0))],
        out_specs=(pair, pair, pair, pl.BlockSpec((2, LANE, S), lambda p: (p, 0, 0))),
        scratch_shapes=[pltpu.VMEM((S, LANE), F32)],
        compiler_params=_params(("arbitrary",)),
    )(z, z, z, z, bf)


def _attn_bwd_prep(qa, o, lse, do, name):
    def body(qa_ref, o_ref, lse_ref, do_ref, qa2_ref, doa_ref):
        lane = lax.broadcasted_iota(jnp.int32, (S, LANE), 1)
        dov, ov, lsev = do_ref[...], o_ref[...], lse_ref[...]
        for e in range(2):
            head = (lane >= 64) if e else (lane < 64)
            dsum = jnp.sum(jnp.where(head, dov * ov, 0.0), axis=1, keepdims=True)
            doa_ref[e] = jnp.where(head, dov, _put3(lane, e, 0, [-x for x in _split3(dsum)], 0.0)).astype(BF16)
            lse_col = lsev[:, 64 * e:64 * e + 1]
            qa2_ref[e] = _put3(lane, e, 6, [-x for x in _split3(lse_col)], qa_ref[e].astype(F32)).astype(BF16)

    heads = jax.ShapeDtypeStruct((N_HEAD, S, LANE), BF16)
    pair = pl.BlockSpec((2, S, LANE), lambda p: (p, 0, 0))
    cols = pl.BlockSpec((S, LANE), lambda p: (0, p))
    return pl.pallas_call(
        body, name=name, out_shape=(heads, heads), grid=(N_PAIR,),
        in_specs=[pair, cols, cols, cols], out_specs=(pair, pair),
        compiler_params=_params(("parallel",)),
    )(qa, o, lse, do)


def _attn_bwd_post(z, bf, dqt, dka, dva, name):
    def body(f_ref, b_ref, dqt_ref, dk_ref, dv_ref, dq_out, dk_out, dv_out, dfl_ref, red_ref, dcum_ref):
        p = pl.program_id(0)

        @pl.when(p == 0)
        def _():
            dcum_ref[...] = jnp.zeros_like(dcum_ref)

        row = lax.broadcasted_iota(jnp.int32, (S, LANE), 0)
        lane = lax.broadcasted_iota(jnp.int32, (S, LANE), 1)
        dqa = [dqt_ref[e].T for e in range(2)]
        dq_out[...] = (jnp.where(lane < 64, dqa[0], dqa[1]) * ATT_SCALE).astype(BF16)
        dk_out[...] = jnp.where(lane < 64, dk_ref[0], dk_ref[1]).astype(BF16)
        dv_out[...] = jnp.where(lane < 64, dv_ref[0], dv_ref[1]).astype(BF16)
        for e in range(2):
            d_query = jnp.sum(jnp.where(_spare(lane, e, 0), dqa[e], 0.0), axis=1, keepdims=True)
            d_key = jnp.sum(jnp.where(_spare(lane, e, 3), dk_ref[e], 0.0), axis=1, keepdims=True)
            dcum_ref[...] += jnp.where(lane == 2 * p + e, d_query - d_key, 0.0)

        @pl.when(p == N_PAIR - 1)
        def _():
            dls = _cumsum_rows(dcum_ref[...], row, reverse=True)
            xv = f_ref[...] + b_ref[0:1, :]
            dx = jnp.where(lane < N_HEAD, dls * jax.nn.sigmoid(-xv), 0.0)
            dfl_ref[...] = dx.astype(BF16)
            red_ref[...] = jnp.zeros_like(red_ref)
            red_ref[0:1, :] = jnp.sum(dx, axis=0, keepdims=True)

    wide = jax.ShapeDtypeStruct((S, N_PAIR * LANE), BF16)
    cols = pl.BlockSpec((S, LANE), lambda p: (0, p))
    pair = pl.BlockSpec((2, S, LANE), lambda p: (p, 0, 0))
    return pl.pallas_call(
        body, name=name,
        out_shape=(wide, wide, wide, jax.ShapeDtypeStruct((S, LANE), BF16), jax.ShapeDtypeStruct((8, LANE), F32)),
        grid=(N_PAIR,),
        in_specs=[pl.BlockSpec((S, LANE), lambda p: (0, Z_F // LANE)), pl.BlockSpec((8, LANE), lambda p: (0, 0)),
                  pl.BlockSpec((2, LANE, S), lambda p: (p, 0, 0)), pair, pair],
        out_specs=(cols, cols, cols, pl.BlockSpec((S, LANE), lambda p: (0, 0)), pl.BlockSpec((8, LANE), lambda p: (0, 0))),
        scratch_shapes=[pltpu.VMEM((S, LANE), F32)],
        compiler_params=_params(("arbitrary",)),
    )(z, bf, dqt, dka, dva)


def _attn_fwd(qa, ka, va, name):
    tq, tk = TQ_FWD, TQ
    ratio = tq // tk

    def body(qa_ref, ka_ref, va_ref, o_ref, lse_ref):
        i = pl.program_id(1)
        lane = lax.broadcasted_iota(jnp.int32, (tq, LANE), 1)
        row = lax.broadcasted_iota(jnp.int32, (tq, tk), 0)
        col = lax.broadcasted_iota(jnp.int32, (tq, tk), 1)
        nh = HEADS_PER_STEP_FWD
        qs = [qa_ref[h] for h in range(nh)]

        def block(j, carry, masked):
            off = pl.multiple_of(j * tk, tk)
            out = []
            for h in range(nh):
                m, acc = carry[h]
                s = lax.dot_general(qs[h], ka_ref[h, pl.ds(off, tk), :], _NT, preferred_element_type=F32)
                if masked:
                    s = jnp.where(col + (j - ratio * i) * tk > row, NEG_INF, s)
                mn = jnp.maximum(m, jnp.max(s, axis=1, keepdims=True))
                p = jnp.exp(s - mn).astype(BF16)
                acc = jnp.exp(m - mn) * acc + jnp.dot(p, va_ref[h, pl.ds(off, tk), :], preferred_element_type=F32)
                out.append((mn, acc))
            return tuple(out)

        init = (jnp.full((tq, 1), NEG_INF, F32), jnp.zeros((tq, LANE), F32))
        carry = lax.fori_loop(0, ratio * i, lambda j, c: block(j, c, False), (init,) * nh)
        for d in range(ratio):
            carry = block(ratio * i + d, carry, True)
        res = []
        for h in range(nh):
            m, acc = carry[h]
            l = jnp.sum(jnp.where(_spare(lane, h % 2, 0), acc, 0.0), axis=1, keepdims=True)
            res.append((acc / l, m + jnp.log(l)))
        for g in range(nh // 2):
            o_ref[:, g * LANE:(g + 1) * LANE] = jnp.where(lane < 64, res[2 * g][0], res[2 * g + 1][0])
            lse_ref[:, g * LANE:(g + 1) * LANE] = jnp.where(lane < 64, res[2 * g][1], res[2 * g + 1][1])

    nh = HEADS_PER_STEP_FWD
    out = jax.ShapeDtypeStruct((S, N_PAIR * LANE), F32)
    wide = pl.BlockSpec((tq, 64 * nh), lambda p, i: (i, p))
    return pl.pallas_call(
        body, name=name, out_shape=(out, out), grid=(N_HEAD // nh, S // tq),
        in_specs=[pl.BlockSpec((nh, tq, LANE), lambda p, i: (p, i, 0)), pl.BlockSpec((nh, S, LANE), lambda p, i: (p, 0, 0)),
                  pl.BlockSpec((nh, S, LANE), lambda p, i: (p, 0, 0))],
        out_specs=(wide, wide),
        compiler_params=_params(("parallel", "parallel")),
    )(qa, ka, va)


def _attn_bwd(qa2, ka, va, kat, doa, name):
    nq = S // TQ

    def body(qa_ref, ka_ref, va_ref, kat_ref, doa_ref, dqt_ref, dk_ref, dv_ref):
        j = pl.program_id(1)

        @pl.when(j == 0)
        def _():
            dqt_ref[...] = jnp.zeros_like(dqt_ref)

        key = lax.broadcasted_iota(jnp.int32, (TQ, TQ), 0)
        qry = lax.broadcasted_iota(jnp.int32, (TQ, TQ), 1)
        nh = HEADS_PER_STEP
        kav, vav, katv = ([ref[h] for h in range(nh)] for ref in (ka_ref, va_ref, kat_ref))

        def block(i, carry, masked):
            off = pl.multiple_of(i * TQ, TQ)
            out = []
            for h in range(nh):
                dk_acc, dv_acc = carry[h]
                qav = qa_ref[h, pl.ds(off, TQ), :]
                doav = doa_ref[h, pl.ds(off, TQ), :]
                s_t = lax.dot_general(kav[h], qav, _NT, preferred_element_type=F32)
                if masked:
                    s_t = jnp.where(key > qry, NEG_INF, s_t)
                p_t = jnp.exp(s_t)
                ds_t = p_t * lax.dot_general(vav[h], doav, _NT, preferred_element_type=F32)
                dsb = ds_t.astype(BF16)
                dv_acc = dv_acc + jnp.dot(p_t.astype(BF16), doav, preferred_element_type=F32)
                dk_acc = dk_acc + jnp.dot(dsb, qav, preferred_element_type=F32)
                dqt_ref[h, :, pl.ds(off, TQ)] += jnp.dot(katv[h], dsb, preferred_element_type=F32)
                out.append((dk_acc, dv_acc))
            return tuple(out)

        zero = (jnp.zeros((TQ, LANE), F32), jnp.zeros((TQ, LANE), F32))
        carry = block(j, (zero,) * nh, True)
        carry = lax.fori_loop(j + 1, nq, lambda i, c: block(i, c, False), carry)
        for h in range(nh):
            dk_ref[h], dv_ref[h] = carry[h]

    nh = HEADS_PER_STEP
    full = pl.BlockSpec((nh, S, LANE), lambda p, j: (p, 0, 0))
    blk = pl.BlockSpec((nh, TQ, LANE), lambda p, j: (p, j, 0))
    acc = jax.ShapeDtypeStruct((N_HEAD, S, LANE), F32)
    return pl.pallas_call(
        body, name=name,
        out_shape=(jax.ShapeDtypeStruct((N_HEAD, LANE, S), F32), acc, acc),
        grid=(N_HEAD // nh, nq),
        in_specs=[full, blk, blk, pl.BlockSpec((nh, LANE, TQ), lambda p, j: (p, 0, j)), full],
        out_specs=(pl.BlockSpec((nh, LANE, S), lambda p, j: (p, 0, 0)), blk, blk),
        compiler_params=_params(("arbitrary", "arbitrary")),
    )(qa2, ka, va, kat, doa)


ADA_ROWS = 16


def _ada_fwd(c_pad, w_ada, b_cols, name):
    def body(c_ref, w_ref, b_ref, o_ref):
        cv = c_ref[...]
        sc = (cv * jax.nn.sigmoid(cv)).astype(BF16)
        o_ref[0] = jnp.dot(sc, w_ref[0].astype(BF16), preferred_element_type=F32) + b_ref[0, 0:1, :]

    return pl.pallas_call(
        body, name=name, out_shape=jax.ShapeDtypeStruct((DEPTH, ADA_ROWS, ADA_COLS), F32), grid=(DEPTH,),
        in_specs=[pl.BlockSpec((ADA_ROWS, D), lambda l: (0, 0)), pl.BlockSpec((1, D, ADA_COLS), lambda l: (l, 0, 0)),
                  pl.BlockSpec((1, 8, ADA_COLS), lambda l: (l, 0, 0))],
        out_specs=pl.BlockSpec((1, ADA_ROWS, ADA_COLS), lambda l: (l, 0, 0)),
        compiler_params=_params(("parallel",)),
    )(c_pad, w_ada, b_cols)


def _ada_bwd(c_pad, dmod_cols, name):
    def body(c_ref, d_ref, o_ref):
        cv = c_ref[...]
        sc = (cv * jax.nn.sigmoid(cv)).astype(BF16)
        o_ref[0] = lax.dot_general(sc, d_ref[0].astype(BF16), _TN, preferred_element_type=F32)

    return pl.pallas_call(
        body, name=name, out_shape=jax.ShapeDtypeStruct((DEPTH, D, ADA_COLS), F32), grid=(DEPTH,),
        in_specs=[pl.BlockSpec((ADA_ROWS, D), lambda l: (0, 0)), pl.BlockSpec((1, ADA_ROWS, ADA_COLS), lambda l: (l, 0, 0))],
        out_specs=pl.BlockSpec((1, D, ADA_COLS), lambda l: (l, 0, 0)),
        compiler_params=_params(("parallel",)),
    )(c_pad, dmod_cols)


def _adamw_math(w, g, m, v):
    m = B1 * m + (1.0 - B1) * g
    v = B2 * v + (1.0 - B2) * (g * g)
    m_hat = m / (1.0 - B1 ** STEP)
    v_hat = v / (1.0 - B2 ** STEP)
    delta = -LR * (m_hat / (jnp.sqrt(v_hat) + EPS) + WD * w)
    return delta, m, v


def _row_tile(rows, target=256):
    best = 8
    for t in range(8, min(rows, target) + 1, 8):
        if rows % t == 0:
            best = t
    return best


def _adamw(w, g, m, v, name):
    layers, rows, cols = w.shape
    tr = _row_tile(rows)
    spec = pl.BlockSpec((1, tr, cols), lambda l, i: (l, i, 0))

    def body(w_ref, g_ref, m_ref, v_ref, d_ref, nm_ref, nv_ref):
        d_ref[...], nm_ref[...], nv_ref[...] = _adamw_math(w_ref[...], g_ref[...], m_ref[...], v_ref[...])

    out = jax.ShapeDtypeStruct(w.shape, F32)
    return pl.pallas_call(
        body, name=name, out_shape=(out, out, out), grid=(layers, rows // tr),
        in_specs=[spec] * 4, out_specs=(spec,) * 3, compiler_params=_params(("parallel", "parallel")),
    )(w, g, m, v)


def _sum_slabs(x, name):
    n, rows, _ = x.shape
    tr = _row_tile(rows)

    def body(x_ref, o_ref):
        acc = x_ref[0]
        for k in range(1, n):
            acc = acc + x_ref[k]
        o_ref[...] = acc

    return pl.pallas_call(
        body, name=name, out_shape=jax.ShapeDtypeStruct((rows, D), F32), grid=(rows // tr,),
        in_specs=[pl.BlockSpec((n, tr, D), lambda i: (0, i, 0))], out_specs=pl.BlockSpec((tr, D), lambda i: (i, 0)),
        compiler_params=_params(("parallel",)),
    )(x)


_ANY = pl.BlockSpec(memory_space=pl.ANY)
MESH = pl.DeviceIdType.MESH


def _on_sequencer(body, out_shape, sems, operands, after, sequencer_id, name):
    n = len(operands)

    def ordered_body(*refs):
        body(*refs[:n], *refs[n + 1:])

    extra = [] if after is None else [after]
    return pl.kernel(
        body if after is None else ordered_body, out_type=out_shape,
        mesh=plsc.ScalarSubcoreMesh(axis_name="sequencer", num_cores=1), scratch_types=sems,
        compiler_params=pltpu.CompilerParams(collective_id=sequencer_id), name=name)(*operands, *extra)


def _all_gather(xs, name, sequencer_id=None, after=None):
    n = len(xs)

    def body(*refs):
        x_refs, out_refs = refs[:n], refs[n:2 * n]
        send_sems, recv_sems, local_sems = refs[2 * n:]
        x_, y_, c_ = lax.axis_index("x"), lax.axis_index("y"), lax.axis_index("c")
        me, sibling = (x_, y_, c_), (x_, y_, 1 - c_)
        chips = [(1 - x_, y_), (x_, 1 - y_), (1 - x_, 1 - y_)]
        if sequencer_id is not None:
            barrier = pltpu.get_barrier_semaphore()
            peers = [sibling] + [(*chip, pc) for chip in chips for pc in (c_, 1 - c_)]
            for peer in peers:
                pl.semaphore_signal(barrier, inc=1, device_id=peer, device_id_type=MESH)
            pl.semaphore_wait(barrier, len(peers))

        def slot(a, px, py, pc):
            return out_refs[a].at[4 * px + 2 * py + pc]

        def copy(a, k, block, to, src=None):
            return pltpu.make_async_remote_copy(
                src_ref=slot(a, *block) if src is None else src, dst_ref=slot(a, *block),
                send_sem=send_sems.at[7 * a + k], recv_sem=recv_sems.at[7 * a + k], device_id=to, device_id_type=MESH)

        mine = [pltpu.make_async_copy(x_refs[a], slot(a, *me), local_sems.at[a]) for a in range(n)]
        for cp in mine:
            cp.start()
        first = []
        for a in range(n):
            first.append(copy(a, 0, me, sibling, src=x_refs[a]))
            first += [copy(a, 1 + j, me, (*chip, c_), src=x_refs[a]) for j, chip in enumerate(chips)]
        for cp in first:
            cp.start()
        passed = []
        for j, chip in enumerate(chips):
            for a in range(n):
                copy(a, 1 + j, (*chip, c_), me).wait_recv()
                passed.append(copy(a, 4 + j, (*chip, c_), sibling))
                passed[-1].start()
        for a in range(n):
            copy(a, 0, sibling, me).wait_recv()
        for j, chip in enumerate(chips):
            for a in range(n):
                copy(a, 4 + j, (*chip, 1 - c_), me).wait_recv()
        for cp in first + passed:
            cp.wait_send()
        for cp in mine:
            cp.wait()

    out_shape = [jax.ShapeDtypeStruct((N_DEV,) + x.shape, x.dtype) for x in xs]
    sems = [pltpu.SemaphoreType.DMA((7 * n,)), pltpu.SemaphoreType.DMA((7 * n,)), pltpu.SemaphoreType.DMA((n,))]
    if sequencer_id is not None:
        return _on_sequencer(body, out_shape, sems, xs, after, sequencer_id, name)
    return pl.pallas_call(
        body, name=name, out_shape=out_shape, in_specs=[_ANY] * n, out_specs=[_ANY] * n, scratch_shapes=sems)(*xs)


def _sibling_exchange(gs, name, sequencer_id=None, after=None):
    n = len(gs)

    def body(*refs):
        g_refs, p_refs = refs[:n], refs[n:2 * n]
        send_sems, recv_sems = refs[2 * n:]
        x_, y_, c_ = lax.axis_index("x"), lax.axis_index("y"), lax.axis_index("c")
        if sequencer_id is not None:
            barrier = pltpu.get_barrier_semaphore()
            pl.semaphore_signal(barrier, inc=1, device_id=(x_, y_, 1 - c_), device_id_type=MESH)
            pl.semaphore_wait(barrier, 1)
        copies = [pltpu.make_async_remote_copy(
            src_ref=g_refs[a].at[2 * k + (1 - c_)], dst_ref=p_refs[a].at[k], send_sem=send_sems.at[4 * a + k],
            recv_sem=recv_sems.at[4 * a + k], device_id=(x_, y_, 1 - c_), device_id_type=MESH)
            for a in range(n) for k in range(4)]
        for cp in copies:
            cp.start()
        for cp in copies:
            cp.wait()

    out_shape = [jax.ShapeDtypeStruct((4,) + g.shape[1:], g.dtype) for g in gs]
    sems = [pltpu.SemaphoreType.DMA((4 * n,)), pltpu.SemaphoreType.DMA((4 * n,))]
    if sequencer_id is not None:
        return _on_sequencer(body, out_shape, sems, gs, after, sequencer_id, name)
    return pl.pallas_call(
        body, name=name, out_shape=out_shape, in_specs=[_ANY] * n, out_specs=[_ANY] * n, scratch_shapes=sems)(*gs)


def _slab_tiles(rows, cols):
    if rows % 8 == 0:
        return _row_tile(rows), cols
    return rows, 2 * LANE


def _pair_sums(g, p, route, name):
    _, rows, cols = g.shape
    tr, tc = _slab_tiles(rows, cols)

    def body(route_ref, g_ref, p_ref, t_ref):
        t_ref[...] = (g_ref[...] + p_ref[...]).astype(BF16)

    return pl.pallas_call(
        body, name=name, out_shape=jax.ShapeDtypeStruct((3, rows, cols), BF16),
        grid_spec=pltpu.PrefetchScalarGridSpec(
            num_scalar_prefetch=1, grid=(3, rows // tr, cols // tc),
            in_specs=[pl.BlockSpec((1, tr, tc), lambda r, i, j, route_ref: (2 * route_ref[1 + r] + route_ref[0], i, j)),
                      pl.BlockSpec((1, tr, tc), lambda r, i, j, route_ref: (route_ref[1 + r], i, j))],
            out_specs=pl.BlockSpec((1, tr, tc), lambda r, i, j, route_ref: (r, i, j))),
        compiler_params=_params(("parallel", "parallel", "parallel")),
    )(route, g, p)


def _chip_exchange(ts, name, sequencer_id=None, after=None):
    n = len(ts)

    def body(*refs):
        t_refs, l_refs = refs[:n], refs[n:2 * n]
        send_sems, recv_sems = refs[2 * n:]
        x_, y_, c_ = lax.axis_index("x"), lax.axis_index("y"), lax.axis_index("c")
        chips = [(1 - x_, y_), (x_, 1 - y_), (1 - x_, 1 - y_)]
        if sequencer_id is not None:
            barrier = pltpu.get_barrier_semaphore()
            for px, py in chips:
                pl.semaphore_signal(barrier, inc=1, device_id=(px, py, c_), device_id_type=MESH)
            pl.semaphore_wait(barrier, len(chips))
        copies = [pltpu.make_async_remote_copy(
            src_ref=t_refs[a].at[r], dst_ref=l_refs[a].at[r], send_sem=send_sems.at[3 * a + r],
            recv_sem=recv_sems.at[3 * a + r], device_id=(px, py, c_), device_id_type=MESH)
            for a in range(n) for r, (px, py) in enumerate(chips)]
        for cp in copies:
            cp.start()
        for cp in copies:
            cp.wait()

    out_shape = [jax.ShapeDtypeStruct((3,) + t.shape[1:], t.dtype) for t in ts]
    sems = [pltpu.SemaphoreType.DMA((3 * n,)), pltpu.SemaphoreType.DMA((3 * n,))]
    if sequencer_id is not None:
        return _on_sequencer(body, out_shape, sems, ts, after, sequencer_id, name)
    return pl.pallas_call(
        body, name=name, out_shape=out_shape, in_specs=[_ANY] * n, out_specs=[_ANY] * n, scratch_shapes=sems)(*ts)


def _reduce_adamw(gs, ps, landed, place, w, m, v, name):
    layers, rows, cols = w.shape
    assert layers == DEPTH == 2
    tr, tc = _slab_tiles(rows, cols)
    nr, nc = rows // tr, cols // tc
    spec = pl.BlockSpec((1, tr, tc), lambda l, i, j, place_ref: (l, i, j))

    def own(layer, which):
        pi, pj = (nr - 1, nc - 1) if layer == 0 else (0, 0)

        def index(l, i, j, place_ref):
            lead = 0 if which is None else place_ref[which]
            return lead, jnp.where(l == layer, i, pi), jnp.where(l == layer, j, pj)

        return pl.BlockSpec((3 if which is None else 1, tr, tc), index)

    def body(place_ref, g0_ref, p0_ref, l0_ref, g1_ref, p1_ref, l1_ref, w_ref, m_ref, v_ref,
             g_ref, d_ref, nm_ref, nv_ref):
        def update(own_ref, sib_ref, l_ref):
            g = own_ref[0] + sib_ref[0] + l_ref[0].astype(F32) + l_ref[1].astype(F32) + l_ref[2].astype(F32)
            g_ref[0] = g
            d_ref[0], nm_ref[0], nv_ref[0] = _adamw_math(w_ref[0], g, m_ref[0], v_ref[0])

        @pl.when(pl.program_id(0) == 0)
        def _():
            update(g0_ref, p0_ref, l0_ref)

        @pl.when(pl.program_id(0) == 1)
        def _():
            update(g1_ref, p1_ref, l1_ref)

    out = jax.ShapeDtypeStruct(w.shape, F32)
    return pl.pallas_call(
        body, name=name, out_shape=(out, out, out, out),
        grid_spec=pltpu.PrefetchScalarGridSpec(
            num_scalar_prefetch=1, grid=(DEPTH, nr, nc),
            in_specs=[own(0, 0), own(0, 1), own(0, None), own(1, 0), own(1, 1), own(1, None), spec, spec, spec],
            out_specs=(spec, spec, spec, spec)),
        compiler_params=_params(("arbitrary", "arbitrary", "arbitrary")),
    )(place, gs[0], ps[0], landed[0], gs[1], ps[1], landed[1], w, m, v)


def _pack(pieces, row_multiple, dtype, cols=D, rows=None):
    flat = jnp.concatenate([p.astype(dtype).reshape(-1) for p in pieces])
    if rows is None:
        rows = -(-flat.shape[0] // cols)
        rows = -(-rows // row_multiple) * row_multiple
    flat = jnp.pad(flat, (0, rows * cols - flat.shape[0]))
    return flat.reshape(rows, cols)


def _unpack(flat, shapes, lead=()):
    out, off = [], 0
    for shp in shapes:
        n = 1
        for s_ in shp:
            n *= s_
        out.append(lax.slice_in_dim(flat, off, off + n, axis=len(lead)).reshape(lead + tuple(shp)))
        off += n
    return out


WIN_STRIDE = 704
WIN_ROWS = 720
Z_TURN = 1544


def _window(wt, me, name):
    padded = jnp.pad(wt, ((0, 0), (0, WIN_ROWS - IN_SHARD), (0, 0)))

    def body(me_ref, x_ref, o_ref):
        o_ref[0] = pltpu.roll(x_ref[0], me_ref[0], axis=0).astype(BF16)

    spec = pl.BlockSpec((1, WIN_ROWS, D), lambda l, me_ref: (l, 0, 0))
    return pl.pallas_call(
        body, name=name, out_shape=jax.ShapeDtypeStruct((DEPTH, WIN_ROWS, D), BF16),
        grid_spec=pltpu.PrefetchScalarGridSpec(num_scalar_prefetch=1, grid=(DEPTH,), in_specs=[spec], out_specs=spec),
        compiler_params=_params(("parallel",)),
    )(me, padded)


def _z_rows_from_windows(win):
    over = WIN_ROWS - WIN_STRIDE
    pieces = [(0, win[0][0:WIN_STRIDE])]
    for d in range(1, N_DEV):
        base = WIN_STRIDE * d
        pieces.append((base, win[d - 1][WIN_STRIDE:WIN_ROWS] + win[d][0:over]))
        pieces.append((base + over, win[d][over:WIN_STRIDE]))
    pieces.append((WIN_STRIDE * N_DEV, win[N_DEV - 1][WIN_STRIDE:WIN_ROWS]))

    def rows(a, b):
        out = []
        for start, arr in pieces:
            lo, hi = max(a, start), min(b, start + arr.shape[0])
            if lo < hi:
                out.append(arr[lo - start:hi - start])
        return out

    pad = jnp.zeros((NZ - IN_COLS, win.shape[-1]), win.dtype)
    return jnp.concatenate(rows(Z_TURN, IN_COLS) + rows(0, Z_TURN) + [pad], axis=0)


def _in_rows_from_z(wt):
    return jnp.concatenate([wt[Z_Q:Z_Q + 1536], wt[Z_F:Z_F + 8], wt[Z_PC:Z_PC + 1024], wt[Z_G:Z_G + 3072]], axis=0)


def _pad_rows(v, rows=8):
    return jnp.pad(v, ((0, rows - v.shape[0]), (0, 0)))


def _layer_fwd(l, x, wts, gvec, mod):
    tag = f"l{l}"
    h = _prenorm_fwd(x, gvec, mod, 0, 0, 1, f"prenorm_mix_{tag}")
    z = _matmul(h, wts["w_in_t"], "nt", f"in_proj_{tag}", tn=1152)
    qa, ka, va, kat = _attn_prep(z, wts["b_f"], f"attn_prep_{tag}")
    qa = wts["arrive"](qa)
    o, lse = _attn_fwd(qa, ka, va, f"attn_{tag}")
    br_b = _pool_fwd(z, wts["wp_bd"], wts["pool_scale"], f"pool_{tag}")
    br_c = _conv_fwd(z, wts["conv_w"], f"conv_{tag}")
    pa = _matmul(o, wts["wa"], "nn", f"proj_a_{tag}")
    pb = _matmul(br_b, wts["wb"], "nn", f"proj_b_{tag}")
    gates = [(z, Z_G + k * D) for k in range(3)]
    pc, merged = _matmul(br_c, wts["wc"], "nn", f"proj_c_merge_{tag}", tm=512, tn=512,
                         extra=gates + [(pa, 0), (pb, 0)], epilogue=_merge_epilogue, out_dtypes=(F32, BF16))
    y = _matmul(merged, wts["w_out"], "nn", f"out_proj_{tag}")
    x1 = _postnorm_fwd(x, y, gvec, mod, 1, 2, f"postnorm_mix_{tag}")
    h2 = _prenorm_fwd(x1, gvec, mod, 2, 3, 4, f"prenorm_ff_{tag}")
    a, r = _matmul(h2, wts["w_ff1"], "nn", f"ff1_{tag}", b_col_shards=True, epilogue=_relu2_epilogue,
                   out_dtypes=(F32, BF16))
    y2 = _matmul(r, wts["w_ff2"], "nn", f"ff2_{tag}", tk=1024)
    x2 = _postnorm_fwd(x1, y2, gvec, mod, 3, 5, f"postnorm_ff_{tag}")
    saved = dict(x=x, h=h, z=z, qa=qa, ka=ka, va=va, kat=kat, o=o, lse=lse, br_b=br_b, br_c=br_c, pa=pa, pb=pb, pc=pc,
                 merged=merged, y=y, x1=x1, h2=h2, a=a, r=r, y2=y2)
    return x2, saved


def _ffn_bwd(l, dx2, sv, wts, gvec, mod, midpoint):
    tag = f"l{l}"
    dy2, red_post_ff = _postnorm_bwd(sv["y2"], gvec, mod, dx2, 3, 5, f"postnorm_ff_bwd_{tag}")
    dy2 = midpoint(dy2)
    da = _matmul(dy2, wts["w_ff2"], "nt", f"ff2_dx_{tag}", extra=[(sv["a"], 0)], epilogue=_relu2_bwd_epilogue,
                 out_dtypes=(BF16,))[0]
    d_w_ff2 = _matmul(sv["r"], dy2, "tn", f"ff2_dw_{tag}")
    dh2 = _matmul(da, wts["w_ff1"], "nt", f"ff1_dx_{tag}", b_col_shards=True)
    d_w_ff1 = _matmul(sv["h2"], da, "tn", f"ff1_dw_{tag}", out_col_shards=True)
    dx1, red_pre_ff = _prenorm_bwd(sv["x1"], gvec, mod, dh2, dx2, 2, 4, f"prenorm_ff_bwd_{tag}")
    return dx1, [d_w_ff1, d_w_ff2.reshape(N_DEV, D_FF // N_DEV, D)], (red_pre_ff, red_post_ff)


def _mixer_bwd(l, dx1, sv, wts, gvec, mod, ffn_reds, midpoint):
    tag = f"l{l}"
    red_pre_ff, red_post_ff = ffn_reds
    dy, red_post_mix = _postnorm_bwd(sv["y"], gvec, mod, dx1, 1, 2, f"postnorm_mix_bwd_{tag}")
    gates = [(sv["z"], Z_G + k * D) for k in range(3)]
    dpa, dpb, dpc, *dgl = _matmul(dy, wts["w_out"], "nt", f"out_proj_dx_{tag}", tm=512, tn=512,
                                  extra=gates + [(sv["pa"], 0), (sv["pb"], 0), (sv["pc"], 0)],
                                  epilogue=_merge_bwd_epilogue, out_dtypes=(BF16,) * 6)
    d_w_out = _matmul(sv["merged"], dy, "tn", f"out_proj_dw_{tag}")
    dpa = midpoint(dpa)
    do = _matmul(dpa, wts["wa"], "nt", f"proj_a_dx_{tag}")
    dbr_b = _matmul(dpb, wts["wb"], "nt", f"proj_b_dx_{tag}")
    dbr_c = _matmul(dpc, wts["wc"], "nt", f"proj_c_dx_{tag}")
    d_wa = _matmul(sv["o"], dpa, "tn", f"proj_a_dw_{tag}")
    d_wb = _matmul(sv["br_b"], dpb, "tn", f"proj_b_dw_{tag}")
    d_wc = _matmul(sv["br_c"], dpc, "tn", f"proj_c_dw_{tag}")
    d_w_branch = jnp.concatenate([d_wa, d_wb, d_wc], axis=0)

    dpu, d_wp_bd, red_pool = _pool_bwd(sv["z"], wts["wp_bd"], wts["pool_scale"], dbr_b, f"pool_bwd_{tag}")
    dconv, red_conv = _conv_bwd(sv["z"], wts["conv_w"], dbr_c, f"conv_bwd_{tag}")
    qa2, doa = _attn_bwd_prep(sv["qa"], sv["o"], sv["lse"], do, f"attn_bwd_prep_{tag}")
    dqt, dka, dva = _attn_bwd(qa2, sv["ka"], sv["va"], sv["kat"], doa, f"attn_bwd_{tag}")
    dq, dk, dv, dfl, red_f = _attn_bwd_post(sv["z"], wts["b_f"], dqt, dka, dva, f"attn_bwd_post_{tag}")
    dz = jnp.concatenate([dpu, dconv, *dgl, dq, dk, dv, dfl], axis=1)
    dh = _matmul(dz, wts["w_in_t"], "nn", f"in_proj_dx_{tag}", tm=1024, tk=1920)
    d_w_in_t = _matmul(dz, sv["h"], "tn", f"in_proj_dw_{tag}", tm=1152)
    dx0, red_pre_mix = _prenorm_bwd(sv["x"], gvec, mod, dh, dx1, 0, 1, f"prenorm_mix_bwd_{tag}")

    rows = D // N_DEV
    big = [_in_rows_from_z(d_w_in_t).reshape(N_DEV, IN_SHARD, D), d_w_branch.reshape(N_DEV, rows, D),
           d_w_out.reshape(N_DEV, rows, D)]
    d_w_pool = jnp.stack([d_wp_bd[64 * g:64 * (g + 1), 64 * g:64 * (g + 1)] for g in range(4)])
    small = dict(
        mod=jnp.stack([red_pre_mix[0], red_pre_mix[1], red_post_mix[0], red_pre_ff[0], red_pre_ff[1], red_post_ff[0]]),
        g_mix_pre=red_pre_mix[2], g_mix_post=red_post_mix[1], g_ff_pre=red_pre_ff[2], g_ff_post=red_post_ff[1],
        b_f=red_f[0, 0:8], w_pool=d_w_pool, pool_scale=red_pool[0], conv_w=red_conv[0:3])
    return dx0, big, small


SMALL_KEYS = ["mod", "g_mix_pre", "g_mix_post", "g_ff_pre", "g_ff_post", "b_f", "w_pool", "pool_scale", "conv_w"]
SMALL_SHAPES = [(DEPTH, 6 * D), (DEPTH, D), (DEPTH, D), (DEPTH, D), (DEPTH, D), (DEPTH, 8), (DEPTH, 4, 64, 64),
                (DEPTH, POOL_W), (DEPTH, 3, CONV_W)]


def kernel(x, c, w_ada, b_ada, g_mix_pre, g_mix_post, g_ff_pre, g_ff_post, w_in, b_f, w_pool, pool_scale, conv_w, w_branch, w_out, w_ff1, w_ff2, loss_target, m_w_ada, m_b_ada, m_g_mix_pre, m_g_mix_post, m_g_ff_pre, m_g_ff_post, m_w_in, m_b_f, m_w_pool, m_pool_scale, m_conv_w, m_w_branch, m_w_out, m_w_ff1, m_w_ff2, v_w_ada, v_b_ada, v_g_mix_pre, v_g_mix_post, v_g_ff_pre, v_g_ff_post, v_w_in, v_b_f, v_w_pool, v_pool_scale, v_conv_w, v_w_branch, v_w_out, v_w_ff1, v_w_ff2):
    ix, iy, ic = lax.axis_index("x"), lax.axis_index("y"), lax.axis_index("c")
    me = 4 * ix + 2 * iy + ic
    route = jnp.stack([ic, 2 * (1 - ix) + iy, 2 * ix + (1 - iy), 2 * (1 - ix) + (1 - iy)]).astype(jnp.int32)
    place = jnp.stack([me, 2 * ix + iy]).astype(jnp.int32)
    wt_in, mt_in, vt_in = (jnp.transpose(a, (0, 2, 1)) for a in (w_in, m_w_in, v_w_in))

    c_all = _all_gather([_pad_rows(c)], "gather_c")[0][:, 0, :]
    c_pad = _pad_rows(c_all, ADA_ROWS)
    b_cols = lax.dynamic_slice_in_dim(b_ada, me * ADA_COLS, ADA_COLS, axis=1)
    b_cols = jnp.broadcast_to(b_cols[:, None, :], (DEPTH, 8, ADA_COLS))
    mod_part = _ada_fwd(c_pad, w_ada, b_cols, "ada_fwd")
    mod_all = _all_gather([mod_part.reshape(DEPTH * ADA_ROWS, ADA_COLS)], "gather_mod")[0]
    mod_all = mod_all.reshape(N_DEV, DEPTH, ADA_ROWS, ADA_COLS)
    mod_mine = lax.dynamic_index_in_dim(mod_all, me, axis=2, keepdims=False)
    mod_mine = jnp.transpose(mod_mine, (1, 0, 2)).reshape(DEPTH, 6, D)

    cw_cols = CONV_W // N_DEV
    cw_send = jnp.pad(conv_w.reshape(DEPTH * 3, cw_cols), ((0, 8 - DEPTH * 3), (0, LANE - cw_cols)))
    win_in = _window(wt_in, place[0:1], "w_in_window")
    send = [[w[l].astype(BF16) for w in (win_in, w_branch, w_out, w_ff1, w_ff2)] for l in range(DEPTH)]
    first = _all_gather(send[0][:1], "gather_weights_l0_in", sequencer_id=1, after=mod_all)
    rest = _all_gather(send[0][1:] + [cw_send], "gather_weights_l0_rest", sequencer_id=2, after=first[0])
    first1 = _all_gather(send[1][:1], "gather_weights_l1_in", sequencer_id=3, after=first[0])
    rest1 = _all_gather(send[1][1:], "gather_weights_l1_rest", sequencer_id=12, after=first[0])
    gathered = [first + rest[:4], first1 + rest1]
    cw_all = rest[4][:, :DEPTH * 3, :cw_cols].reshape(N_DEV, DEPTH, 3, cw_cols)

    def first_operands(l, p_in):
        wp_bd = jnp.zeros((POOL_W, POOL_W), F32)
        for g in range(4):
            wp_bd = wp_bd.at[64 * g:64 * (g + 1), 64 * g:64 * (g + 1)].set(w_pool[l, g])
        return dict(w_in_t=_z_rows_from_windows(p_in), wp_bd=wp_bd.astype(BF16),
                    pool_scale=_pad_rows(pool_scale[l][None, :]), b_f=_pad_rows(jnp.pad(b_f[l], (0, LANE - 8))[None, :]))

    def rest_operands(l, rest):
        p_br, p_out, p_ff1, p_ff2 = rest
        w_br_full = p_br.reshape(D, D)
        cw_full = jnp.transpose(cw_all[:, l], (1, 0, 2)).reshape(3, CONV_W)
        return dict(wa=w_br_full[0:A_WIDTH], wb=w_br_full[A_WIDTH:A_WIDTH + POOL_W], wc=w_br_full[A_WIDTH + POOL_W:],
                    w_out=p_out.reshape(D, D), w_ff1=p_ff1, w_ff2=p_ff2.reshape(D_FF, D), conv_w=_pad_rows(cw_full))

    xs = x[0]
    saved, layers = [], []
    for l in range(DEPTH):
        p_in, rest = gathered[l][0], gathered[l][1:5]
        if l > 0:
            xs, p_in = lax.optimization_barrier((xs, p_in))
        wts = first_operands(l, p_in)

        def arrive(t, l=l, rest=rest, wts=wts):
            if l > 0:
                t, rest = lax.optimization_barrier((t, rest))
            wts.update(rest_operands(l, rest))
            return t

        wts["arrive"] = arrive
        gvec = _pad_rows(jnp.stack([g_mix_pre[l], g_mix_post[l], g_ff_pre[l], g_ff_post[l]]))
        layers.append((wts, gvec, _pad_rows(mod_mine[l])))
        xs, sv = _layer_fwd(l, xs, *layers[l])
        saved.append(sv)
    dx, loss_part = _loss_head(xs, loss_target[0], "loss_head")
    loss = lax.psum(loss_part[0, 0], ("x", "y", "c"))
    small_grads = [None] * DEPTH
    mine, sibs, landed = ({} for _ in range(3))
    seq_id = iter(range(4, 4 + 4 * DEPTH))
    last = [gathered[DEPTH - 1][1]]

    def start(group, grads):
        mine[group] = grads
        sibs[group] = _sibling_exchange(grads, f"rs_sibling_{group}", sequencer_id=next(seq_id), after=last[0])
        last[0] = sibs[group][0]

    def finish(group, later):
        later, (grads, sib) = lax.optimization_barrier((later, (mine[group], sibs[group])))
        sends = [_pair_sums(g, p, route, f"rs_pair_sums_{group}_{k}") for k, (g, p) in enumerate(zip(grads, sib))]
        later, sends = lax.optimization_barrier((later, sends))
        landed[group] = _chip_exchange(sends, f"rs_chips_{group}", sequencer_id=next(seq_id), after=last[0])
        last[0] = landed[group][0]
        return later

    pending = None
    for l in reversed(range(DEPTH)):
        hook = (lambda da: da) if pending is None else functools.partial(finish, pending)
        dx, ffn_grads, ffn_reds = _ffn_bwd(l, dx, saved[l], *layers[l], hook)
        start(f"ffn_l{l}", ffn_grads)
        dx, mix_grads, small_grads[l] = _mixer_bwd(l, dx, saved[l], *layers[l], ffn_reds,
                                                   functools.partial(finish, f"ffn_l{l}"))
        start(f"mix_l{l}", mix_grads)
        pending = f"mix_l{l}"
    grad_x = dx[None]

    big_w = [wt_in, w_branch, w_out, w_ff1, w_ff2]
    big_m = [mt_in, m_w_branch, m_w_out, m_w_ff1, m_w_ff2]
    big_v = [vt_in, v_w_branch, v_w_out, v_w_ff1, v_w_ff2]
    where = [("mix", 0), ("mix", 1), ("mix", 2), ("ffn", 0), ("ffn", 1)]

    def reduce_and_update(k):
        group, at = where[k]
        return _reduce_adamw([mine[f"{group}_l{l}"][at] for l in range(DEPTH)],
                             [sibs[f"{group}_l{l}"][at] for l in range(DEPTH)],
                             [landed[f"{group}_l{l}"][at] for l in range(DEPTH)], place, big_w[k], big_m[k], big_v[k],
                             f"rs_sum_adamw_{k}")

    big_res = {k: list(reduce_and_update(k)) for k in (3, 4)}
    big_res[3][0] = finish(pending, big_res[3][0])

    small = {k: jnp.stack([small_grads[l][k] for l in range(DEPTH)]) for k in SMALL_KEYS}
    small_all = _all_gather([_pack([small[k] for k in SMALL_KEYS], 8, F32)], "gather_small")[0]
    dmod_all = small_all[:, 0:DEPTH * 6, :].reshape(N_DEV, DEPTH, 6 * D)
    summed = _unpack(_sum_slabs(small_all, "sum_small").reshape(-1), SMALL_SHAPES)
    sg = dict(zip(SMALL_KEYS, summed))
    dmod_cols = lax.dynamic_slice_in_dim(dmod_all, me * ADA_COLS, ADA_COLS, axis=2)
    dmod_cols = jnp.pad(jnp.transpose(dmod_cols, (1, 0, 2)), ((0, 0), (0, ADA_ROWS - N_DEV), (0, 0)))
    g_w_ada = _ada_bwd(c_pad, dmod_cols, "ada_bwd")
    g_conv_w = lax.dynamic_slice_in_dim(sg["conv_w"], me * (CONV_W // N_DEV), CONV_W // N_DEV, axis=2)

    ada_out = [g_w_ada] + list(_adamw(w_ada, g_w_ada, m_w_ada, v_w_ada, "adamw_ada"))
    rest_w = [b_ada, g_mix_pre, g_mix_post, g_ff_pre, g_ff_post, b_f, w_pool, pool_scale, conv_w]
    rest_m = [m_b_ada, m_g_mix_pre, m_g_mix_post, m_g_ff_pre, m_g_ff_post, m_b_f, m_w_pool, m_pool_scale, m_conv_w]
    rest_v = [v_b_ada, v_g_mix_pre, v_g_mix_post, v_g_ff_pre, v_g_ff_post, v_b_f, v_w_pool, v_pool_scale, v_conv_w]
    rest_g = [sg["mod"], sg["g_mix_pre"], sg["g_mix_post"], sg["g_ff_pre"], sg["g_ff_post"], sg["b_f"],
              sg["w_pool"], sg["pool_scale"], g_conv_w]
    rest_shapes = [a.shape for a in rest_w]
    upd = _adamw(_pack(rest_w, 8, F32)[None], _pack(rest_g, 8, F32)[None], _pack(rest_m, 8, F32)[None],
                 _pack(rest_v, 8, F32)[None], "adamw_rest")
    rest_out = [rest_g] + [_unpack(arr.reshape(-1), rest_shapes) for arr in upd]
    rest_out = [[ada_out[which]] + rest_out[which] for which in range(4)]

    landed[pending], rest_out = lax.optimization_barrier((landed[pending], rest_out))
    big_res.update({k: reduce_and_update(k) for k in (0, 1, 2)})
    big_out = [[jnp.transpose(big_res[k][which], (0, 2, 1)) if k == 0 else big_res[k][which] for k in range(5)]
               for which in range(4)]

    def ordered(k):
        r, b = rest_out[k], big_out[k]
        return [r[0], r[1], r[2], r[3], r[4], r[5], b[0], r[6], r[7], r[8], r[9], b[1], b[2], b[3], b[4]]

    return (loss, grad_x, *ordered(0), *ordered(1), *ordered(2), *ordered(3))
```

```python
import functools

import jax
import jax.numpy as jnp
from jax import lax
from jax.experimental import pallas as pl
from jax.experimental.pallas import tpu as pltpu
from jax.experimental.pallas import tpu_sc as plsc

F32 = jnp.float32
BF16 = jnp.bfloat16

N_DEV = 8
D = 1024
S = 2048
DEPTH = 2
D_FF = 4 * D
A_WIDTH = 512
HEAD_DIM = 64
N_PAIR = 4
POOL_W = 256
CONV_W = 256
IN_COLS = 5640
ADA_COLS = 6 * D // N_DEV
IN_SHARD = IN_COLS // N_DEV
RMS_EPS = 1e-6
NEG_INF = -1e30
ATT_SCALE = HEAD_DIM ** -0.5

NZ = 5760
Z_PC = 0
Z_G = 1024
Z_Q = 4096
Z_K = 4608
Z_V = 5120
Z_F = 5632

LR, B1, B2, EPS, WD, STEP = 0.001, 0.9, 0.999, 1e-08, 0.01, 10

LANE = 128
VMEM_LIMIT_BYTES = 48 * 1024 * 1024
TS = 512
TQ = 256
TQ_FWD = 512
HEADS_PER_STEP = 8
HEADS_PER_STEP_FWD = 8


def _params(sem=None):
    return pltpu.CompilerParams(dimension_semantics=sem, vmem_limit_bytes=VMEM_LIMIT_BYTES)


def _pick(n, target):
    best = None
    for t in range(LANE, min(n, target) + 1, LANE):
        if n % t == 0:
            best = t
    return n if best is None else best


def _matmul(a, b, mode, name, out_dtype=F32, tm=2048, tn=1024, tk=2048, b_col_shards=False, out_col_shards=False,
            extra=(), epilogue=None, out_dtypes=None):
    if b_col_shards:
        shards, b_rows, shard_cols = b.shape
        b_shape = (b_rows, shards * shard_cols)
    else:
        b_shape = b.shape
    if mode == "nn":
        (m, k), (k2, n) = a.shape, b_shape
    elif mode == "nt":
        (m, k), (n, k2) = a.shape, b_shape
    else:
        (k, m), (k2, n) = a.shape, b_shape
    assert k == k2, (a.shape, b.shape, mode)
    tm, tn, tk = _pick(m, tm), _pick(n, tn), _pick(k, tk)
    if b_col_shards and mode == "nn":
        tn = shard_cols
    per_step = 1
    if b_col_shards and mode == "nt":
        per_step = max(1, min(tk, 1024) // shard_cols)
        tk = per_step * shard_cols
    if out_col_shards:
        tn = n // N_DEV
    nk = k // tk
    if mode == "nn":
        a_spec = pl.BlockSpec((tm, tk), lambda i, j, kk: (i, kk))
        b_spec = (pl.BlockSpec((None, tk, tn), lambda i, j, kk: (j, kk, 0)) if b_col_shards else
                  pl.BlockSpec((tk, tn), lambda i, j, kk: (kk, j)))
        dims = (((1,), (0,)), ((), ()))
    elif mode == "nt":
        a_spec = pl.BlockSpec((tm, tk), lambda i, j, kk: (i, kk))
        b_spec = (pl.BlockSpec((per_step, tn, shard_cols), lambda i, j, kk: (kk, j, 0)) if b_col_shards else
                  pl.BlockSpec((tn, tk), lambda i, j, kk: (j, kk)))
        dims = (((1,), (1,)), ((), ()))
    else:
        assert not b_col_shards
        a_spec = pl.BlockSpec((tk, tm), lambda i, j, kk: (kk, i))
        b_spec = pl.BlockSpec((tk, tn), lambda i, j, kk: (kk, j))
        dims = (((0,), (0,)), ((), ()))
    if out_col_shards:
        out_shape = jax.ShapeDtypeStruct((N_DEV, m, tn), out_dtype)
        out_spec = pl.BlockSpec((None, tm, tn), lambda i, j, kk: (j, i, 0))
    else:
        out_shape = jax.ShapeDtypeStruct((m, n), out_dtype)
        out_spec = pl.BlockSpec((tm, tn), lambda i, j, kk: (i, j))

    n_extra = len(extra)
    extra_specs = [pl.BlockSpec((tm, tn), lambda i, j, kk, off=off: (i, j + off // tn)) for _, off in extra]
    if epilogue is not None:
        assert not out_col_shards and all(off % tn == 0 for _, off in extra)
        out_shape = [jax.ShapeDtypeStruct((m, n), dt) for dt in out_dtypes]
        out_spec = [pl.BlockSpec((tm, tn), lambda i, j, kk: (i, j)) for _ in out_dtypes]

    def product(a_ref, b_ref):
        if b_col_shards and mode == "nt":
            b_tile = jnp.concatenate([b_ref[s] for s in range(per_step)], axis=1) if per_step > 1 else b_ref[0]
        else:
            b_tile = b_ref[...]
        return lax.dot_general(a_ref[...].astype(BF16), b_tile.astype(BF16), dims, preferred_element_type=F32)

    def write(acc, extra_refs, o_refs):
        if epilogue is None:
            o_refs[0][...] = acc.astype(out_dtype)
        else:
            for o_ref, tile in zip(o_refs, epilogue(acc, *[r[...] for r in extra_refs])):
                o_ref[...] = tile.astype(o_ref.dtype)

    def body_one_pass(a_ref, b_ref, *refs):
        write(product(a_ref, b_ref), refs[:n_extra], refs[n_extra:])

    def body(a_ref, b_ref, *refs):
        acc_ref = refs[-1]
        kk = pl.program_id(2)

        @pl.when(kk == 0)
        def _():
            acc_ref[...] = product(a_ref, b_ref)

        @pl.when(kk > 0)
        def _():
            acc_ref[...] += product(a_ref, b_ref)

        @pl.when(kk == nk - 1)
        def _():
            write(acc_ref[...], refs[:n_extra], refs[n_extra:-1])

    return pl.pallas_call(
        body_one_pass if nk == 1 else body, name=name,
        out_shape=out_shape,
        grid=(m // tm, n // tn, nk),
        in_specs=[a_spec, b_spec] + extra_specs,
        out_specs=out_spec,
        scratch_shapes=[] if nk == 1 else [pltpu.VMEM((tm, tn), F32)],
        compiler_params=_params(("parallel", "parallel", "arbitrary")),
    )(a, b, *[x for x, _ in extra])


def _row_spec(width=D, col=0):
    return pl.BlockSpec((TS, width), lambda i: (i, col))


def _vec_spec(rows=8, width=D):
    return pl.BlockSpec((rows, width), lambda i: (0, 0))


def _rms(x):
    return lax.rsqrt(jnp.mean(x * x, axis=-1, keepdims=True) + RMS_EPS)


def _prenorm_fwd(x, gvec, mod, g_row, shift_row, scale_row, name):
    def body(x_ref, g_ref, mod_ref, h_ref):
        xv = x_ref[...]
        y = xv * _rms(xv) * g_ref[g_row:g_row + 1, :]
        h = y * (1.0 + mod_ref[scale_row:scale_row + 1, :]) + mod_ref[shift_row:shift_row + 1, :]
        h_ref[...] = h.astype(BF16)

    return pl.pallas_call(
        body, name=name, out_shape=jax.ShapeDtypeStruct((S, D), BF16), grid=(S // TS,),
        in_specs=[_row_spec(), _vec_spec(), _vec_spec()], out_specs=_row_spec(),
        compiler_params=_params(("parallel",)),
    )(x, gvec, mod)


def _prenorm_bwd(x, gvec, mod, dh, dres, g_row, scale_row, name):
    def body(x_ref, g_ref, mod_ref, dh_ref, dres_ref, dx_ref, red_ref):
        i = pl.program_id(0)

        @pl.when(i == 0)
        def _():
            red_ref[...] = jnp.zeros_like(red_ref)

        xv = x_ref[...]
        g = g_ref[g_row:g_row + 1, :]
        r = _rms(xv)
        n = xv * r
        yg = n * g
        dhv = dh_ref[...]
        dyg = dhv * (1.0 + mod_ref[scale_row:scale_row + 1, :])
        dn = dyg * g
        dx = r * (dn - n * jnp.mean(dn * n, axis=-1, keepdims=True))
        dx_ref[...] = dres_ref[...] + dx
        red_ref[0:1, :] += jnp.sum(dhv, axis=0, keepdims=True)
        red_ref[1:2, :] += jnp.sum(dhv * yg, axis=0, keepdims=True)
        red_ref[2:3, :] += jnp.sum(dyg * n, axis=0, keepdims=True)

    return pl.pallas_call(
        body, name=name,
        out_shape=(jax.ShapeDtypeStruct((S, D), F32), jax.ShapeDtypeStruct((8, D), F32)),
        grid=(S // TS,),
        in_specs=[_row_spec(), _vec_spec(), _vec_spec(), _row_spec(), _row_spec()],
        out_specs=(_row_spec(), _vec_spec()),
        compiler_params=_params(("arbitrary",)),
    )(x, gvec, mod, dh, dres)


def _postnorm_fwd(x, y, gvec, mod, g_row, gate_row, name):
    def body(x_ref, y_ref, g_ref, mod_ref, o_ref):
        yv = y_ref[...]
        yn = yv * _rms(yv) * g_ref[g_row:g_row + 1, :]
        o_ref[...] = x_ref[...] + mod_ref[gate_row:gate_row + 1, :] * yn

    return pl.pallas_call(
        body, name=name, out_shape=jax.ShapeDtypeStruct((S, D), F32), grid=(S // TS,),
        in_specs=[_row_spec(), _row_spec(), _vec_spec(), _vec_spec()], out_specs=_row_spec(),
        compiler_params=_params(("parallel",)),
    )(x, y, gvec, mod)


def _postnorm_bwd(y, gvec, mod, dxo, g_row, gate_row, name):
    def body(y_ref, g_ref, mod_ref, dxo_ref, dy_ref, red_ref):
        i = pl.program_id(0)

        @pl.when(i == 0)
        def _():
            red_ref[...] = jnp.zeros_like(red_ref)

        yv = y_ref[...]
        g = g_ref[g_row:g_row + 1, :]
        r = _rms(yv)
        n = yv * r
        dxo = dxo_ref[...]
        dyn = dxo * mod_ref[gate_row:gate_row + 1, :]
        dn = dyn * g
        dy = r * (dn - n * jnp.mean(dn * n, axis=-1, keepdims=True))
        dy_ref[...] = dy.astype(BF16)
        red_ref[0:1, :] += jnp.sum(dxo * (n * g), axis=0, keepdims=True)
        red_ref[1:2, :] += jnp.sum(dyn * n, axis=0, keepdims=True)

    return pl.pallas_call(
        body, name=name,
        out_shape=(jax.ShapeDtypeStruct((S, D), BF16), jax.ShapeDtypeStruct((8, D), F32)),
        grid=(S // TS,),
        in_specs=[_row_spec(), _vec_spec(), _vec_spec(), _row_spec()],
        out_specs=(_row_spec(), _vec_spec()),
        compiler_params=_params(("arbitrary",)),
    )(y, gvec, mod, dxo)


def _loss_head(xf, target, name):
    def body(x_ref, t_ref, dx_ref, loss_ref):
        i = pl.program_id(0)

        @pl.when(i == 0)
        def _():
            loss_ref[...] = jnp.zeros_like(loss_ref)

        e = x_ref[...] - t_ref[...]
        dx_ref[...] = e / float(D)
        per_tok = jnp.mean(e * e, axis=-1, keepdims=True)
        loss_ref[0:1, 0:1] += 0.5 * jnp.sum(per_tok, axis=0, keepdims=True)

    return pl.pallas_call(
        body, name=name,
        out_shape=(jax.ShapeDtypeStruct((S, D), F32), jax.ShapeDtypeStruct((8, LANE), F32)),
        grid=(S // TS,),
        in_specs=[_row_spec(), _row_spec()],
        out_specs=(_row_spec(), pl.BlockSpec((8, LANE), lambda i: (0, 0))),
        compiler_params=_params(("arbitrary",)),
    )(xf, target)


def _relu2_epilogue(a):
    t = jnp.maximum(a, 0.0)
    return a, t * t


def _relu2_bwd_epilogue(dr, a):
    return (dr * (2.0 * jnp.maximum(a, 0.0)),)


def _merge_epilogue(pc, g0, g1, g2, pa, pb):
    return pc, jax.nn.sigmoid(g0) * pa + jax.nn.sigmoid(g1) * pb + jax.nn.sigmoid(g2) * pc


def _merge_bwd_epilogue(dm, g0, g1, g2, pa, pb, pc):
    sg = [jax.nn.sigmoid(g) for g in (g0, g1, g2)]
    return tuple(dm * s for s in sg) + tuple(dm * p * (s * (1.0 - s)) for p, s in zip((pa, pb, pc), sg))


def _shift_down(x, k, row):
    return jnp.where(row >= k, pltpu.roll(x, k, axis=0), 0.0)


def _shift_up(x, k, row):
    n = x.shape[0]
    return jnp.where(row < n - k, pltpu.roll(x, n - k, axis=0), 0.0)


def _cumsum_rows(x, row, reverse=False):
    shift = _shift_up if reverse else _shift_down
    k = 1
    while k < x.shape[0]:
        x = x + shift(x, k, row)
        k *= 2
    return x


def _full_spec(shape, idx=(0, 0)):
    return pl.BlockSpec(shape, lambda i: idx)


def _pool_window_select(lane, a2, a4, a8, a16):
    return jnp.where(lane < 64, a2, jnp.where(lane < 128, a4, jnp.where(lane < 192, a8, a16)))


def _pool_p(u, row, lane):
    t2 = u + _shift_down(u, 1, row)
    t4 = t2 + _shift_down(t2, 2, row)
    t8 = t4 + _shift_down(t4, 4, row)
    t16 = t8 + _shift_down(t8, 8, row)
    tw = _pool_window_select(lane, t2, t4, t8, t16)
    cnt = jnp.minimum((row + 1).astype(F32), _pool_window_select(lane, 2.0, 4.0, 8.0, 16.0))
    return tw / cnt - u, cnt


def _pool_fwd(z, wp_bd, pscale, name):
    def body(u_ref, w_ref, s_ref, o_ref):
        row = lax.broadcasted_iota(jnp.int32, (S, POOL_W), 0)
        lane = lax.broadcasted_iota(jnp.int32, (S, POOL_W), 1)
        p, _ = _pool_p(u_ref[...], row, lane)
        y = jnp.dot(p.astype(BF16), w_ref[...], preferred_element_type=F32)
        o_ref[...] = y * s_ref[0:1, :]

    return pl.pallas_call(
        body, name=name, out_shape=jax.ShapeDtypeStruct((S, POOL_W), F32), grid=(1,),
        in_specs=[_full_spec((S, POOL_W), (0, Z_PC // POOL_W)), _full_spec((POOL_W, POOL_W)), _full_spec((8, POOL_W))],
        out_specs=_full_spec((S, POOL_W)),
        compiler_params=_params(("arbitrary",)),
    )(z, wp_bd, pscale)


def _pool_bwd(z, wp_bd, pscale, dbr, name):
    def body(u_ref, w_ref, s_ref, dbr_ref, du_ref, dw_ref, red_ref):
        row = lax.broadcasted_iota(jnp.int32, (S, POOL_W), 0)
        lane = lax.broadcasted_iota(jnp.int32, (S, POOL_W), 1)
        p, cnt = _pool_p(u_ref[...], row, lane)
        pb = p.astype(BF16)
        y = jnp.dot(pb, w_ref[...], preferred_element_type=F32)
        dbr = dbr_ref[...]
        red_ref[...] = jnp.zeros_like(red_ref)
        red_ref[0:1, :] = jnp.sum(dbr * y, axis=0, keepdims=True)
        dy = (dbr * s_ref[0:1, :]).astype(BF16)
        dw_ref[...] = lax.dot_general(pb, dy, (((0,), (0,)), ((), ())), preferred_element_type=F32)
        dp = lax.dot_general(dy, w_ref[...], (((1,), (1,)), ((), ())), preferred_element_type=F32)
        g = dp / cnt
        a2 = g + _shift_up(g, 1, row)
        a4 = a2 + _shift_up(a2, 2, row)
        a8 = a4 + _shift_up(a4, 4, row)
        a16 = a8 + _shift_up(a8, 8, row)
        du_ref[...] = (_pool_window_select(lane, a2, a4, a8, a16) - dp).astype(BF16)

    return pl.pallas_call(
        body, name=name,
        out_shape=(jax.ShapeDtypeStruct((S, POOL_W), BF16), jax.ShapeDtypeStruct((POOL_W, POOL_W), F32),
                   jax.ShapeDtypeStruct((8, POOL_W), F32)),
        grid=(1,),
        in_specs=[_full_spec((S, POOL_W), (0, Z_PC // POOL_W)), _full_spec((POOL_W, POOL_W)), _full_spec((8, POOL_W)),
                  _full_spec((S, POOL_W))],
        out_specs=(_full_spec((S, POOL_W)), _full_spec((POOL_W, POOL_W)), _full_spec((8, POOL_W))),
        compiler_params=_params(("arbitrary",)),
    )(z, wp_bd, pscale, dbr)


def _conv_specs():
    base = Z_PC // CONV_W
    return [_full_spec((S, CONV_W), (0, base + 1)), _full_spec((S, CONV_W), (0, base + 2)),
            _full_spec((S, CONV_W), (0, base + 3)), _full_spec((8, CONV_W))]


def _conv_fwd(z, cw, name):
    def body(h_ref, b_ref, c_ref, w_ref, o_ref):
        row = lax.broadcasted_iota(jnp.int32, (S, CONV_W), 0)
        u = c_ref[...] * h_ref[...]
        y = (w_ref[0:1, :] * _shift_down(u, 2, row) + w_ref[1:2, :] * _shift_down(u, 1, row) + w_ref[2:3, :] * u)
        o_ref[...] = b_ref[...] * y

    return pl.pallas_call(
        body, name=name, out_shape=jax.ShapeDtypeStruct((S, CONV_W), F32), grid=(1,),
        in_specs=_conv_specs(), out_specs=_full_spec((S, CONV_W)),
        compiler_params=_params(("arbitrary",)),
    )(z, z, z, cw)


def _conv_bwd(z, cw, dbr, name):
    def body(h_ref, b_ref, c_ref, w_ref, dbr_ref, d_ref, red_ref):
        row = lax.broadcasted_iota(jnp.int32, (S, CONV_W), 0)
        h, cg = h_ref[...], c_ref[...]
        u = cg * h
        u1 = _shift_down(u, 1, row)
        u2 = _shift_down(u, 2, row)
        y = w_ref[0:1, :] * u2 + w_ref[1:2, :] * u1 + w_ref[2:3, :] * u
        dbr = dbr_ref[...]
        dy = dbr * b_ref[...]
        du = w_ref[2:3, :] * dy + w_ref[1:2, :] * _shift_up(dy, 1, row) + w_ref[0:1, :] * _shift_up(dy, 2, row)
        d_ref[:, 0:CONV_W] = (du * cg).astype(BF16)
        d_ref[:, CONV_W:2 * CONV_W] = (dbr * y).astype(BF16)
        d_ref[:, 2 * CONV_W:3 * CONV_W] = (du * h).astype(BF16)
        red_ref[...] = jnp.zeros_like(red_ref)
        red_ref[0:1, :] = jnp.sum(dy * u2, axis=0, keepdims=True)
        red_ref[1:2, :] = jnp.sum(dy * u1, axis=0, keepdims=True)
        red_ref[2:3, :] = jnp.sum(dy * u, axis=0, keepdims=True)

    return pl.pallas_call(
        body, name=name,
        out_shape=(jax.ShapeDtypeStruct((S, 3 * CONV_W), BF16), jax.ShapeDtypeStruct((8, CONV_W), F32)),
        grid=(1,),
        in_specs=_conv_specs() + [_full_spec((S, CONV_W))],
        out_specs=(_full_spec((S, 3 * CONV_W)), _full_spec((8, CONV_W))),
        compiler_params=_params(("arbitrary",)),
    )(z, z, z, cw, dbr)


_NT = (((1,), (1,)), ((), ()))
_TN = (((0,), (0,)), ((), ()))
N_HEAD = 2 * N_PAIR


def _split3(x):
    hi = x.astype(BF16).astype(F32)
    mid = (x - hi).astype(BF16).astype(F32)
    lo = (x - hi - mid).astype(BF16).astype(F32)
    return hi, mid, lo


def _spare(lane, e, k):
    return lane == 64 * (1 - e) + k


def _spare3(lane, e, k):
    base = 64 * (1 - e) + k
    return (lane >= base) & (lane < base + 3)


def _put3(lane, e, k, pieces, rest):
    out = rest
    for n, piece in enumerate(pieces):
        out = jnp.where(_spare(lane, e, k + n), piece, out)
    return out


def _attn_prep(z, bf, name):
    def body(q_ref, k_ref, v_ref, f_ref, b_ref, qa_ref, ka_ref, va_ref, kat_ref, cum_ref):
        p = pl.program_id(0)
        row = lax.broadcasted_iota(jnp.int32, (S, LANE), 0)
        lane = lax.broadcasted_iota(jnp.int32, (S, LANE), 1)

        @pl.when(p == 0)
        def _():
            xv = f_ref[...] + b_ref[0:1, :]
            ls = jnp.minimum(xv, 0.0) - jnp.log(1.0 + jnp.exp(-jnp.abs(xv)))
            cum_ref[...] = _cumsum_rows(jnp.where(lane < N_HEAD, ls, 0.0), row)

        cum = cum_ref[...]
        q, k, v = q_ref[...], k_ref[...], v_ref[...]
        for e in range(2):
            head = (lane >= 64) if e else (lane < 64)
            f = jnp.sum(jnp.where(lane == 2 * p + e, cum, 0.0), axis=1, keepdims=True)
            pieces = _split3(f)
            qa = jnp.where(head, q * ATT_SCALE, _put3(lane, e, 0, pieces, jnp.where(_spare3(lane, e, 3), 1.0, 0.0)))
            ones = jnp.where(_spare3(lane, e, 0) | _spare3(lane, e, 6), 1.0, 0.0)
            ka = jnp.where(head, k, _put3(lane, e, 3, [-x for x in pieces], ones))
            va = jnp.where(head, v, jnp.where(_spare3(lane, e, 0), 1.0, 0.0))
            qa_ref[e] = qa.astype(BF16)
            ka_ref[e] = ka.astype(BF16)
            va_ref[e] = va.astype(BF16)
            kat_ref[e] = ka.T.astype(BF16)

    qb, kb, vb = Z_Q // LANE, Z_K // LANE, Z_V // LANE
    heads = jax.ShapeDtypeStruct((N_HEAD, S, LANE), BF16)
    pair = pl.BlockSpec((2, S, LANE), lambda p: (p, 0, 0))
    return pl.pallas_call(
        body, name=name,
        out_shape=(heads, heads, heads, jax.ShapeDtypeStruct((N_HEAD, LANE, S), BF16)),
        grid=(N_PAIR,),
        in_specs=[pl.BlockSpec((S, LANE), lambda p: (0, qb + p)), pl.BlockSpec((S, LANE), lambda p: (0, kb + p)),
                  pl.BlockSpec((S, LANE), lambda p: (0, vb + p)), pl.BlockSpec((S, LANE), lambda p: (0, Z_F // LANE)),
                  pl.BlockSpec((8, LANE), lambda p: (0, 0))],
        out_specs=(pair, pair, pair, pl.BlockSpec((2, LANE, S), lambda p: (p, 0, 0))),
        scratch_shapes=[pltpu.VMEM((S, LANE), F32)],
        compiler_params=_params(("arbitrary",)),
    )(z, z, z, z, bf)


def _attn_bwd_prep(qa, o, lse, do, name):
    def body(qa_ref, o_ref, lse_ref, do_ref, qa2_ref, doa_ref):
        lane = lax.broadcasted_iota(jnp.int32, (S, LANE), 1)
        dov, ov, lsev = do_ref[...], o_ref[...], lse_ref[...]
        for e in range(2):
            head = (lane >= 64) if e else (lane < 64)
            dsum = jnp.sum(jnp.where(head, dov * ov, 0.0), axis=1, keepdims=True)
            doa_ref[e] = jnp.where(head, dov, _put3(lane, e, 0, [-x for x in _split3(dsum)], 0.0)).astype(BF16)
            lse_col = lsev[:, 64 * e:64 * e + 1]
            qa2_ref[e] = _put3(lane, e, 6, [-x for x in _split3(lse_col)], qa_ref[e].astype(F32)).astype(BF16)

    heads = jax.ShapeDtypeStruct((N_HEAD, S, LANE), BF16)
    pair = pl.BlockSpec((2, S, LANE), lambda p: (p, 0, 0))
    cols = pl.BlockSpec((S, LANE), lambda p: (0, p))
    return pl.pallas_call(
        body, name=name, out_shape=(heads, heads), grid=(N_PAIR,),
        in_specs=[pair, cols, cols, cols], out_specs=(pair, pair),
        compiler_params=_params(("parallel",)),
    )(qa, o, lse, do)


def _attn_bwd_post(z, bf, dqt, dka, dva, name):
    def body(f_ref, b_ref, dqt_ref, dk_ref, dv_ref, dq_out, dk_out, dv_out, dfl_ref, red_ref, dcum_ref):
        p = pl.program_id(0)

        @pl.when(p == 0)
        def _():
            dcum_ref[...] = jnp.zeros_like(dcum_ref)

        row = lax.broadcasted_iota(jnp.int32, (S, LANE), 0)
        lane = lax.broadcasted_iota(jnp.int32, (S, LANE), 1)
        dqa = [dqt_ref[e].T for e in range(2)]
        dq_out[...] = (jnp.where(lane < 64, dqa[0], dqa[1]) * ATT_SCALE).astype(BF16)
        dk_out[...] = jnp.where(lane < 64, dk_ref[0], dk_ref[1]).astype(BF16)
        dv_out[...] = jnp.where(lane < 64, dv_ref[0], dv_ref[1]).astype(BF16)
        for e in range(2):
            d_query = jnp.sum(jnp.where(_spare(lane, e, 0), dqa[e], 0.0), axis=1, keepdims=True)
            d_key = jnp.sum(jnp.where(_spare(lane, e, 3), dk_ref[e], 0.0), axis=1, keepdims=True)
            dcum_ref[...] += jnp.where(lane == 2 * p + e, d_query - d_key, 0.0)

        @pl.when(p == N_PAIR - 1)
        def _():
            dls = _cumsum_rows(dcum_ref[...], row, reverse=True)
            xv = f_ref[...] + b_ref[0:1, :]
            dx = jnp.where(lane < N_HEAD, dls * jax.nn.sigmoid(-xv), 0.0)
            dfl_ref[...] = dx.astype(BF16)
            red_ref[...] = jnp.zeros_like(red_ref)
            red_ref[0:1, :] = jnp.sum(dx, axis=0, keepdims=True)

    wide = jax.ShapeDtypeStruct((S, N_PAIR * LANE), BF16)
    cols = pl.BlockSpec((S, LANE), lambda p: (0, p))
    pair = pl.BlockSpec((2, S, LANE), lambda p: (p, 0, 0))
    return pl.pallas_call(
        body, name=name,
        out_shape=(wide, wide, wide, jax.ShapeDtypeStruct((S, LANE), BF16), jax.ShapeDtypeStruct((8, LANE), F32)),
        grid=(N_PAIR,),
        in_specs=[pl.BlockSpec((S, LANE), lambda p: (0, Z_F // LANE)), pl.BlockSpec((8, LANE), lambda p: (0, 0)),
                  pl.BlockSpec((2, LANE, S), lambda p: (p, 0, 0)), pair, pair],
        out_specs=(cols, cols, cols, pl.BlockSpec((S, LANE), lambda p: (0, 0)), pl.BlockSpec((8, LANE), lambda p: (0, 0))),
        scratch_shapes=[pltpu.VMEM((S, LANE), F32)],
        compiler_params=_params(("arbitrary",)),
    )(z, bf, dqt, dka, dva)


def _attn_fwd(qa, ka, va, name):
    tq, tk = TQ_FWD, TQ
    ratio = tq // tk

    def body(qa_ref, ka_ref, va_ref, o_ref, lse_ref):
        i = pl.program_id(1)
        lane = lax.broadcasted_iota(jnp.int32, (tq, LANE), 1)
        row = lax.broadcasted_iota(jnp.int32, (tq, tk), 0)
        col = lax.broadcasted_iota(jnp.int32, (tq, tk), 1)
        nh = HEADS_PER_STEP_FWD
        qs = [qa_ref[h] for h in range(nh)]

        def block(j, carry, masked):
            off = pl.multiple_of(j * tk, tk)
            out = []
            for h in range(nh):
                m, acc = carry[h]
                s = lax.dot_general(qs[h], ka_ref[h, pl.ds(off, tk), :], _NT, preferred_element_type=F32)
                if masked:
                    s = jnp.where(col + (j - ratio * i) * tk > row, NEG_INF, s)
                mn = jnp.maximum(m, jnp.max(s, axis=1, keepdims=True))
                p = jnp.exp(s - mn).astype(BF16)
                acc = jnp.exp(m - mn) * acc + jnp.dot(p, va_ref[h, pl.ds(off, tk), :], preferred_element_type=F32)
                out.append((mn, acc))
            return tuple(out)

        init = (jnp.full((tq, 1), NEG_INF, F32), jnp.zeros((tq, LANE), F32))
        carry = lax.fori_loop(0, ratio * i, lambda j, c: block(j, c, False), (init,) * nh)
        for d in range(ratio):
            carry = block(ratio * i + d, carry, True)
        res = []
        for h in range(nh):
            m, acc = carry[h]
            l = jnp.sum(jnp.where(_spare(lane, h % 2, 0), acc, 0.0), axis=1, keepdims=True)
            res.append((acc / l, m + jnp.log(l)))
        for g in range(nh // 2):
            o_ref[:, g * LANE:(g + 1) * LANE] = jnp.where(lane < 64, res[2 * g][0], res[2 * g + 1][0])
            lse_ref[:, g * LANE:(g + 1) * LANE] = jnp.where(lane < 64, res[2 * g][1], res[2 * g + 1][1])

    nh = HEADS_PER_STEP_FWD
    out = jax.ShapeDtypeStruct((S, N_PAIR * LANE), F32)
    wide = pl.BlockSpec((tq, 64 * nh), lambda p, i: (i, p))
    return pl.pallas_call(
        body, name=name, out_shape=(out, out), grid=(N_HEAD // nh, S // tq),
        in_specs=[pl.BlockSpec((nh, tq, LANE), lambda p, i: (p, i, 0)), pl.BlockSpec((nh, S, LANE), lambda p, i: (p, 0, 0)),
                  pl.BlockSpec((nh, S, LANE), lambda p, i: (p, 0, 0))],
        out_specs=(wide, wide),
        compiler_params=_params(("parallel", "parallel")),
    )(qa, ka, va)


def _attn_bwd(qa2, ka, va, kat, doa, name):
    nq = S // TQ

    def body(qa_ref, ka_ref, va_ref, kat_ref, doa_ref, dqt_ref, dk_ref, dv_ref):
        j = pl.program_id(1)

        @pl.when(j == 0)
        def _():
            dqt_ref[...] = jnp.zeros_like(dqt_ref)

        key = lax.broadcasted_iota(jnp.int32, (TQ, TQ), 0)
        qry = lax.broadcasted_iota(jnp.int32, (TQ, TQ), 1)
        nh = HEADS_PER_STEP
        kav, vav, katv = ([ref[h] for h in range(nh)] for ref in (ka_ref, va_ref, kat_ref))

        def block(i, carry, masked):
            off = pl.multiple_of(i * TQ, TQ)
            out = []
            for h in range(nh):
                dk_acc, dv_acc = carry[h]
                qav = qa_ref[h, pl.ds(off, TQ), :]
                doav = doa_ref[h, pl.ds(off, TQ), :]
                s_t = lax.dot_general(kav[h], qav, _NT, preferred_element_type=F32)
                if masked:
                    s_t = jnp.where(key > qry, NEG_INF, s_t)
                p_t = jnp.exp(s_t)
                ds_t = p_t * lax.dot_general(vav[h], doav, _NT, preferred_element_type=F32)
                dsb = ds_t.astype(BF16)
                dv_acc = dv_acc + jnp.dot(p_t.astype(BF16), doav, preferred_element_type=F32)
                dk_acc = dk_acc + jnp.dot(dsb, qav, preferred_element_type=F32)
                dqt_ref[h, :, pl.ds(off, TQ)] += jnp.dot(katv[h], dsb, preferred_element_type=F32)
                out.append((dk_acc, dv_acc))
            return tuple(out)

        zero = (jnp.zeros((TQ, LANE), F32), jnp.zeros((TQ, LANE), F32))
        carry = block(j, (zero,) * nh, True)
        carry = lax.fori_loop(j + 1, nq, lambda i, c: block(i, c, False), carry)
        for h in range(nh):
            dk_ref[h], dv_ref[h] = carry[h]

    nh = HEADS_PER_STEP
    full = pl.BlockSpec((nh, S, LANE), lambda p, j: (p, 0, 0))
    blk = pl.BlockSpec((nh, TQ, LANE), lambda p, j: (p, j, 0))
    acc = jax.ShapeDtypeStruct((N_HEAD, S, LANE), F32)
    return pl.pallas_call(
        body, name=name,
        out_shape=(jax.ShapeDtypeStruct((N_HEAD, LANE, S), F32), acc, acc),
        grid=(N_HEAD // nh, nq),
        in_specs=[full, blk, blk, pl.BlockSpec((nh, LANE, TQ), lambda p, j: (p, 0, j)), full],
        out_specs=(pl.BlockSpec((nh, LANE, S), lambda p, j: (p, 0, 0)), blk, blk),
        compiler_params=_params(("arbitrary", "arbitrary")),
    )(qa2, ka, va, kat, doa)


ADA_ROWS = 16


def _ada_fwd(c_pad, w_ada, b_cols, name):
    def body(c_ref, w_ref, b_ref, o_ref):
        cv = c_ref[...]
        sc = (cv * jax.nn.sigmoid(cv)).astype(BF16)
        o_ref[0] = jnp.dot(sc, w_ref[0].astype(BF16), preferred_element_type=F32) + b_ref[0, 0:1, :]

    return pl.pallas_call(
        body, name=name, out_shape=jax.ShapeDtypeStruct((DEPTH, ADA_ROWS, ADA_COLS), F32), grid=(DEPTH,),
        in_specs=[pl.BlockSpec((ADA_ROWS, D), lambda l: (0, 0)), pl.BlockSpec((1, D, ADA_COLS), lambda l: (l, 0, 0)),
                  pl.BlockSpec((1, 8, ADA_COLS), lambda l: (l, 0, 0))],
        out_specs=pl.BlockSpec((1, ADA_ROWS, ADA_COLS), lambda l: (l, 0, 0)),
        compiler_params=_params(("parallel",)),
    )(c_pad, w_ada, b_cols)


def _ada_bwd(c_pad, dmod_cols, name):
    def body(c_ref, d_ref, o_ref):
        cv = c_ref[...]
        sc = (cv * jax.nn.sigmoid(cv)).astype(BF16)
        o_ref[0] = lax.dot_general(sc, d_ref[0].astype(BF16), _TN, preferred_element_type=F32)

    return pl.pallas_call(
        body, name=name, out_shape=jax.ShapeDtypeStruct((DEPTH, D, ADA_COLS), F32), grid=(DEPTH,),
        in_specs=[pl.BlockSpec((ADA_ROWS, D), lambda l: (0, 0)), pl.BlockSpec((1, ADA_ROWS, ADA_COLS), lambda l: (l, 0, 0))],
        out_specs=pl.BlockSpec((1, D, ADA_COLS), lambda l: (l, 0, 0)),
        compiler_params=_params(("parallel",)),
    )(c_pad, dmod_cols)


def _adamw_math(w, g, m, v):
    m = B1 * m + (1.0 - B1) * g
    v = B2 * v + (1.0 - B2) * (g * g)
    m_hat = m / (1.0 - B1 ** STEP)
    v_hat = v / (1.0 - B2 ** STEP)
    delta = -LR * (m_hat / (jnp.sqrt(v_hat) + EPS) + WD * w)
    return delta, m, v


def _row_tile(rows, target=256):
    best = 8
    for t in range(8, min(rows, target) + 1, 8):
        if rows % t == 0:
            best = t
    return best


def _adamw(w, g, m, v, name):
    layers, rows, cols = w.shape
    tr = _row_tile(rows)
    spec = pl.BlockSpec((1, tr, cols), lambda l, i: (l, i, 0))

    def body(w_ref, g_ref, m_ref, v_ref, d_ref, nm_ref, nv_ref):
        d_ref[...], nm_ref[...], nv_ref[...] = _adamw_math(w_ref[...], g_ref[...], m_ref[...], v_ref[...])

    out = jax.ShapeDtypeStruct(w.shape, F32)
    return pl.pallas_call(
        body, name=name, out_shape=(out, out, out), grid=(layers, rows // tr),
        in_specs=[spec] * 4, out_specs=(spec,) * 3, compiler_params=_params(("parallel", "parallel")),
    )(w, g, m, v)


def _sum_slabs(x, name):
    n, rows, _ = x.shape
    tr = _row_tile(rows)

    def body(x_ref, o_ref):
        acc = x_ref[0]
        for k in range(1, n):
            acc = acc + x_ref[k]
        o_ref[...] = acc

    return pl.pallas_call(
        body, name=name, out_shape=jax.ShapeDtypeStruct((rows, D), F32), grid=(rows // tr,),
        in_specs=[pl.BlockSpec((n, tr, D), lambda i: (0, i, 0))], out_specs=pl.BlockSpec((tr, D), lambda i: (i, 0)),
        compiler_params=_params(("parallel",)),
    )(x)


_ANY = pl.BlockSpec(memory_space=pl.ANY)
MESH = pl.DeviceIdType.MESH


def _on_sequencer(body, out_shape, sems, operands, after, sequencer_id, name):
    n = len(operands)

    def ordered_body(*refs):
        body(*refs[:n], *refs[n + 1:])

    extra = [] if after is None else [after]
    return pl.kernel(
        body if after is None else ordered_body, out_type=out_shape,
        mesh=plsc.ScalarSubcoreMesh(axis_name="sequencer", num_cores=1), scratch_types=sems,
        compiler_params=pltpu.CompilerParams(collective_id=sequencer_id), name=name)(*operands, *extra)


def _all_gather(xs, name, sequencer_id=None, after=None):
    n = len(xs)

    def body(*refs):
        x_refs, out_refs = refs[:n], refs[n:2 * n]
        send_sems, recv_sems, local_sems = refs[2 * n:]
        x_, y_, c_ = lax.axis_index("x"), lax.axis_index("y"), lax.axis_index("c")
        me, sibling = (x_, y_, c_), (x_, y_, 1 - c_)
        chips = [(1 - x_, y_), (x_, 1 - y_), (1 - x_, 1 - y_)]
        if sequencer_id is not None:
            barrier = pltpu.get_barrier_semaphore()
            peers = [sibling] + [(*chip, pc) for chip in chips for pc in (c_, 1 - c_)]
            for peer in peers:
                pl.semaphore_signal(barrier, inc=1, device_id=peer, device_id_type=MESH)
            pl.semaphore_wait(barrier, len(peers))

        def slot(a, px, py, pc):
            return out_refs[a].at[4 * px + 2 * py + pc]

        def copy(a, k, block, to, src=None):
            return pltpu.make_async_remote_copy(
                src_ref=slot(a, *block) if src is None else src, dst_ref=slot(a, *block),
                send_sem=send_sems.at[7 * a + k], recv_sem=recv_sems.at[7 * a + k], device_id=to, device_id_type=MESH)

        mine = [pltpu.make_async_copy(x_refs[a], slot(a, *me), local_sems.at[a]) for a in range(n)]
        for cp in mine:
            cp.start()
        first = []
        for a in range(n):
            first.append(copy(a, 0, me, sibling, src=x_refs[a]))
            first += [copy(a, 1 + j, me, (*chip, c_), src=x_refs[a]) for j, chip in enumerate(chips)]
        for cp in first:
            cp.start()
        passed = []
        for j, chip in enumerate(chips):
            for a in range(n):
                copy(a, 1 + j, (*chip, c_), me).wait_recv()
                passed.append(copy(a, 4 + j, (*chip, c_), sibling))
                passed[-1].start()
        for a in range(n):
            copy(a, 0, sibling, me).wait_recv()
        for j, chip in enumerate(chips):
            for a in range(n):
                copy(a, 4 + j, (*chip, 1 - c_), me).wait_recv()
        for cp in first + passed:
            cp.wait_send()
        for cp in mine:
            cp.wait()

    out_shape = [jax.ShapeDtypeStruct((N_DEV,) + x.shape, x.dtype) for x in xs]
    sems = [pltpu.SemaphoreType.DMA((7 * n,)), pltpu.SemaphoreType.DMA((7 * n,)), pltpu.SemaphoreType.DMA((n,))]
    if sequencer_id is not None:
        return _on_sequencer(body, out_shape, sems, xs, after, sequencer_id, name)
    return pl.pallas_call(
        body, name=name, out_shape=out_shape, in_specs=[_ANY] * n, out_specs=[_ANY] * n, scratch_shapes=sems)(*xs)


def _sibling_exchange(gs, name, sequencer_id=None, after=None):
    n = len(gs)

    def body(*refs):
        g_refs, p_refs = refs[:n], refs[n:2 * n]
        send_sems, recv_sems = refs[2 * n:]
        x_, y_, c_ = lax.axis_index("x"), lax.axis_index("y"), lax.axis_index("c")
        if sequencer_id is not None:
            barrier = pltpu.get_barrier_semaphore()
            pl.semaphore_signal(barrier, inc=1, device_id=(x_, y_, 1 - c_), device_id_type=MESH)
            pl.semaphore_wait(barrier, 1)
        copies = [pltpu.make_async_remote_copy(
            src_ref=g_refs[a].at[2 * k + (1 - c_)], dst_ref=p_refs[a].at[k], send_sem=send_sems.at[4 * a + k],
            recv_sem=recv_sems.at[4 * a + k], device_id=(x_, y_, 1 - c_), device_id_type=MESH)
            for a in range(n) for k in range(4)]
        for cp in copies:
            cp.start()
        for cp in copies:
            cp.wait()

    out_shape = [jax.ShapeDtypeStruct((4,) + g.shape[1:], g.dtype) for g in gs]
    sems = [pltpu.SemaphoreType.DMA((4 * n,)), pltpu.SemaphoreType.DMA((4 * n,))]
    if sequencer_id is not None:
        return _on_sequencer(body, out_shape, sems, gs, after, sequencer_id, name)
    return pl.pallas_call(
        body, name=name, out_shape=out_shape, in_specs=[_ANY] * n, out_specs=[_ANY] * n, scratch_shapes=sems)(*gs)


def _slab_tiles(rows, cols):
    if rows % 8 == 0:
        return _row_tile(rows), cols
    return rows, 2 * LANE


def _pair_sums(g, p, route, name):
    _, rows, cols = g.shape
    tr, tc = _slab_tiles(rows, cols)

    def body(route_ref, g_ref, p_ref, t_ref):
        t_ref[...] = (g_ref[...] + p_ref[...]).astype(BF16)

    return pl.pallas_call(
        body, name=name, out_shape=jax.ShapeDtypeStruct((3, rows, cols), BF16),
        grid_spec=pltpu.PrefetchScalarGridSpec(
            num_scalar_prefetch=1, grid=(3, rows // tr, cols // tc),
            in_specs=[pl.BlockSpec((1, tr, tc), lambda r, i, j, route_ref: (2 * route_ref[1 + r] + route_ref[0], i, j)),
                      pl.BlockSpec((1, tr, tc), lambda r, i, j, route_ref: (route_ref[1 + r], i, j))],
            out_specs=pl.BlockSpec((1, tr, tc), lambda r, i, j, route_ref: (r, i, j))),
        compiler_params=_params(("parallel", "parallel", "parallel")),
    )(route, g, p)


def _chip_exchange(ts, name, sequencer_id=None, after=None):
    n = len(ts)

    def body(*refs):
        t_refs, l_refs = refs[:n], refs[n:2 * n]
        send_sems, recv_sems = refs[2 * n:]
        x_, y_, c_ = lax.axis_index("x"), lax.axis_index("y"), lax.axis_index("c")
        chips = [(1 - x_, y_), (x_, 1 - y_), (1 - x_, 1 - y_)]
        if sequencer_id is not None:
            barrier = pltpu.get_barrier_semaphore()
            for px, py in chips:
                pl.semaphore_signal(barrier, inc=1, device_id=(px, py, c_), device_id_type=MESH)
            pl.semaphore_wait(barrier, len(chips))
        copies = [pltpu.make_async_remote_copy(
            src_ref=t_refs[a].at[r], dst_ref=l_refs[a].at[r], send_sem=send_sems.at[3 * a + r],
            recv_sem=recv_sems.at[3 * a + r], device_id=(px, py, c_), device_id_type=MESH)
            for a in range(n) for r, (px, py) in enumerate(chips)]
        for cp in copies:
            cp.start()
        for cp in copies:
            cp.wait()

    out_shape = [jax.ShapeDtypeStruct((3,) + t.shape[1:], t.dtype) for t in ts]
    sems = [pltpu.SemaphoreType.DMA((3 * n,)), pltpu.SemaphoreType.DMA((3 * n,))]
    if sequencer_id is not None:
        return _on_sequencer(body, out_shape, sems, ts, after, sequencer_id, name)
    return pl.pallas_call(
        body, name=name, out_shape=out_shape, in_specs=[_ANY] * n, out_specs=[_ANY] * n, scratch_shapes=sems)(*ts)


def _reduce_adamw(gs, ps, landed, place, w, m, v, name):
    layers, rows, cols = w.shape
    assert layers == DEPTH == 2
    tr, tc = _slab_tiles(rows, cols)
    nr, nc = rows // tr, cols // tc
    spec = pl.BlockSpec((1, tr, tc), lambda l, i, j, place_ref: (l, i, j))

    def own(layer, which):
        pi, pj = (nr - 1, nc - 1) if layer == 0 else (0, 0)

        def index(l, i, j, place_ref):
            lead = 0 if which is None else place_ref[which]
            return lead, jnp.where(l == layer, i, pi), jnp.where(l == layer, j, pj)

        return pl.BlockSpec((3 if which is None else 1, tr, tc), index)

    def body(place_ref, g0_ref, p0_ref, l0_ref, g1_ref, p1_ref, l1_ref, w_ref, m_ref, v_ref,
             g_ref, d_ref, nm_ref, nv_ref):
        def update(own_ref, sib_ref, l_ref):
            g = own_ref[0] + sib_ref[0] + l_ref[0].astype(F32) + l_ref[1].astype(F32) + l_ref[2].astype(F32)
            g_ref[0] = g
            d_ref[0], nm_ref[0], nv_ref[0] = _adamw_math(w_ref[0], g, m_ref[0], v_ref[0])

        @pl.when(pl.program_id(0) == 0)
        def _():
            update(g0_ref, p0_ref, l0_ref)

        @pl.when(pl.program_id(0) == 1)
        def _():
            update(g1_ref, p1_ref, l1_ref)

    out = jax.ShapeDtypeStruct(w.shape, F32)
    return pl.pallas_call(
        body, name=name, out_shape=(out, out, out, out),
        grid_spec=pltpu.PrefetchScalarGridSpec(
            num_scalar_prefetch=1, grid=(DEPTH, nr, nc),
            in_specs=[own(0, 0), own(0, 1), own(0, None), own(1, 0), own(1, 1), own(1, None), spec, spec, spec],
            out_specs=(spec, spec, spec, spec)),
        compiler_params=_params(("arbitrary", "arbitrary", "arbitrary")),
    )(place, gs[0], ps[0], landed[0], gs[1], ps[1], landed[1], w, m, v)


def _pack(pieces, row_multiple, dtype, cols=D, rows=None):
    flat = jnp.concatenate([p.astype(dtype).reshape(-1) for p in pieces])
    if rows is None:
        rows = -(-flat.shape[0] // cols)
        rows = -(-rows // row_multiple) * row_multiple
    flat = jnp.pad(flat, (0, rows * cols - flat.shape[0]))
    return flat.reshape(rows, cols)


def _unpack(flat, shapes, lead=()):
    out, off = [], 0
    for shp in shapes:
        n = 1
        for s_ in shp:
            n *= s_
        out.append(lax.slice_in_dim(flat, off, off + n, axis=len(lead)).reshape(lead + tuple(shp)))
        off += n
    return out


WIN_STRIDE = 704
WIN_ROWS = 720
Z_TURN = 1544


def _window(wt, me, name):
    padded = jnp.pad(wt, ((0, 0), (0, WIN_ROWS - IN_SHARD), (0, 0)))

    def body(me_ref, x_ref, o_ref):
        o_ref[0] = pltpu.roll(x_ref[0], me_ref[0], axis=0).astype(BF16)

    spec = pl.BlockSpec((1, WIN_ROWS, D), lambda l, me_ref: (l, 0, 0))
    return pl.pallas_call(
        body, name=name, out_shape=jax.ShapeDtypeStruct((DEPTH, WIN_ROWS, D), BF16),
        grid_spec=pltpu.PrefetchScalarGridSpec(num_scalar_prefetch=1, grid=(DEPTH,), in_specs=[spec], out_specs=spec),
        compiler_params=_params(("parallel",)),
    )(me, padded)


def _z_rows_from_windows(win):
    over = WIN_ROWS - WIN_STRIDE
    pieces = [(0, win[0][0:WIN_STRIDE])]
    for d in range(1, N_DEV):
        base = WIN_STRIDE * d
        pieces.append((base, win[d - 1][WIN_STRIDE:WIN_ROWS] + win[d][0:over]))
        pieces.append((base + over, win[d][over:WIN_STRIDE]))
    pieces.append((WIN_STRIDE * N_DEV, win[N_DEV - 1][WIN_STRIDE:WIN_ROWS]))

    def rows(a, b):
        out = []
        for start, arr in pieces:
            lo, hi = max(a, start), min(b, start + arr.shape[0])
            if lo < hi:
                out.append(arr[lo - start:hi - start])
        return out

    pad = jnp.zeros((NZ - IN_COLS, win.shape[-1]), win.dtype)
    return jnp.concatenate(rows(Z_TURN, IN_COLS) + rows(0, Z_TURN) + [pad], axis=0)


def _in_rows_from_z(wt):
    return jnp.concatenate([wt[Z_Q:Z_Q + 1536], wt[Z_F:Z_F + 8], wt[Z_PC:Z_PC + 1024], wt[Z_G:Z_G + 3072]], axis=0)


def _pad_rows(v, rows=8):
    return jnp.pad(v, ((0, rows - v.shape[0]), (0, 0)))


def _layer_fwd(l, x, wts, gvec, mod):
    tag = f"l{l}"
    h = _prenorm_fwd(x, gvec, mod, 0, 0, 1, f"prenorm_mix_{tag}")
    z = _matmul(h, wts["w_in_t"], "nt", f"in_proj_{tag}", tn=1152)
    qa, ka, va, kat = _attn_prep(z, wts["b_f"], f"attn_prep_{tag}")
    qa = wts["arrive"](qa)
    o, lse = _attn_fwd(qa, ka, va, f"attn_{tag}")
    br_b = _pool_fwd(z, wts["wp_bd"], wts["pool_scale"], f"pool_{tag}")
    br_c = _conv_fwd(z, wts["conv_w"], f"conv_{tag}")
    pa = _matmul(o, wts["wa"], "nn", f"proj_a_{tag}")
    pb = _matmul(br_b, wts["wb"], "nn", f"proj_b_{tag}")
    gates = [(z, Z_G + k * D) for k in range(3)]
    pc, merged = _matmul(br_c, wts["wc"], "nn", f"proj_c_merge_{tag}", tm=512, tn=512,
                         extra=gates + [(pa, 0), (pb, 0)], epilogue=_merge_epilogue, out_dtypes=(F32, BF16))
    y = _matmul(merged, wts["w_out"], "nn", f"out_proj_{tag}")
    x1 = _postnorm_fwd(x, y, gvec, mod, 1, 2, f"postnorm_mix_{tag}")
    h2 = _prenorm_fwd(x1, gvec, mod, 2, 3, 4, f"prenorm_ff_{tag}")
    a, r = _matmul(h2, wts["w_ff1"], "nn", f"ff1_{tag}", b_col_shards=True, epilogue=_relu2_epilogue,
                   out_dtypes=(F32, BF16))
    y2 = _matmul(r, wts["w_ff2"], "nn", f"ff2_{tag}", tk=1024)
    x2 = _postnorm_fwd(x1, y2, gvec, mod, 3, 5, f"postnorm_ff_{tag}")
    saved = dict(x=x, h=h, z=z, qa=qa, ka=ka, va=va, kat=kat, o=o, lse=lse, br_b=br_b, br_c=br_c, pa=pa, pb=pb, pc=pc,
                 merged=merged, y=y, x1=x1, h2=h2, a=a, r=r, y2=y2)
    return x2, saved


def _ffn_bwd(l, dx2, sv, wts, gvec, mod, midpoint):
    tag = f"l{l}"
    dy2, red_post_ff = _postnorm_bwd(sv["y2"], gvec, mod, dx2, 3, 5, f"postnorm_ff_bwd_{tag}")
    dy2 = midpoint(dy2)
    da = _matmul(dy2, wts["w_ff2"], "nt", f"ff2_dx_{tag}", extra=[(sv["a"], 0)], epilogue=_relu2_bwd_epilogue,
                 out_dtypes=(BF16,))[0]
    d_w_ff2 = _matmul(sv["r"], dy2, "tn", f"ff2_dw_{tag}")
    dh2 = _matmul(da, wts["w_ff1"], "nt", f"ff1_dx_{tag}", b_col_shards=True)
    d_w_ff1 = _matmul(sv["h2"], da, "tn", f"ff1_dw_{tag}", out_col_shards=True)
    dx1, red_pre_ff = _prenorm_bwd(sv["x1"], gvec, mod, dh2, dx2, 2, 4, f"prenorm_ff_bwd_{tag}")
    return dx1, [d_w_ff1, d_w_ff2.reshape(N_DEV, D_FF // N_DEV, D)], (red_pre_ff, red_post_ff)


def _mixer_bwd(l, dx1, sv, wts, gvec, mod, ffn_reds, midpoint):
    tag = f"l{l}"
    red_pre_ff, red_post_ff = ffn_reds
    dy, red_post_mix = _postnorm_bwd(sv["y"], gvec, mod, dx1, 1, 2, f"postnorm_mix_bwd_{tag}")
    gates = [(sv["z"], Z_G + k * D) for k in range(3)]
    dpa, dpb, dpc, *dgl = _matmul(dy, wts["w_out"], "nt", f"out_proj_dx_{tag}", tm=512, tn=512,
                                  extra=gates + [(sv["pa"], 0), (sv["pb"], 0), (sv["pc"], 0)],
                                  epilogue=_merge_bwd_epilogue, out_dtypes=(BF16,) * 6)
    d_w_out = _matmul(sv["merged"], dy, "tn", f"out_proj_dw_{tag}")
    dpa = midpoint(dpa)
    do = _matmul(dpa, wts["wa"], "nt", f"proj_a_dx_{tag}")
    dbr_b = _matmul(dpb, wts["wb"], "nt", f"proj_b_dx_{tag}")
    dbr_c = _matmul(dpc, wts["wc"], "nt", f"proj_c_dx_{tag}")
    d_wa = _matmul(sv["o"], dpa, "tn", f"proj_a_dw_{tag}")
    d_wb = _matmul(sv["br_b"], dpb, "tn", f"proj_b_dw_{tag}")
    d_wc = _matmul(sv["br_c"], dpc, "tn", f"proj_c_dw_{tag}")
    d_w_branch = jnp.concatenate([d_wa, d_wb, d_wc], axis=0)

    dpu, d_wp_bd, red_pool = _pool_bwd(sv["z"], wts["wp_bd"], wts["pool_scale"], dbr_b, f"pool_bwd_{tag}")
    dconv, red_conv = _conv_bwd(sv["z"], wts["conv_w"], dbr_c, f"conv_bwd_{tag}")
    qa2, doa = _attn_bwd_prep(sv["qa"], sv["o"], sv["lse"], do, f"attn_bwd_prep_{tag}")
    dqt, dka, dva = _attn_bwd(qa2, sv["ka"], sv["va"], sv["kat"], doa, f"attn_bwd_{tag}")
    dq, dk, dv, dfl, red_f = _attn_bwd_post(sv["z"], wts["b_f"], dqt, dka, dva, f"attn_bwd_post_{tag}")
    dz = jnp.concatenate([dpu, dconv, *dgl, dq, dk, dv, dfl], axis=1)
    dh = _matmul(dz, wts["w_in_t"], "nn", f"in_proj_dx_{tag}", tm=1024, tk=1920)
    d_w_in_t = _matmul(dz, sv["h"], "tn", f"in_proj_dw_{tag}", tm=1152)
    dx0, red_pre_mix = _prenorm_bwd(sv["x"], gvec, mod, dh, dx1, 0, 1, f"prenorm_mix_bwd_{tag}")

    rows = D // N_DEV
    big = [_in_rows_from_z(d_w_in_t).reshape(N_DEV, IN_SHARD, D), d_w_branch.reshape(N_DEV, rows, D),
           d_w_out.reshape(N_DEV, rows, D)]
    d_w_pool = jnp.stack([d_wp_bd[64 * g:64 * (g + 1), 64 * g:64 * (g + 1)] for g in range(4)])
    small = dict(
        mod=jnp.stack([red_pre_mix[0], red_pre_mix[1], red_post_mix[0], red_pre_ff[0], red_pre_ff[1], red_post_ff[0]]),
        g_mix_pre=red_pre_mix[2], g_mix_post=red_post_mix[1], g_ff_pre=red_pre_ff[2], g_ff_post=red_post_ff[1],
        b_f=red_f[0, 0:8], w_pool=d_w_pool, pool_scale=red_pool[0], conv_w=red_conv[0:3])
    return dx0, big, small


SMALL_KEYS = ["mod", "g_mix_pre", "g_mix_post", "g_ff_pre", "g_ff_post", "b_f", "w_pool", "pool_scale", "conv_w"]
SMALL_SHAPES = [(DEPTH, 6 * D), (DEPTH, D), (DEPTH, D), (DEPTH, D), (DEPTH, D), (DEPTH, 8), (DEPTH, 4, 64, 64),
                (DEPTH, POOL_W), (DEPTH, 3, CONV_W)]


def kernel(x, c, w_ada, b_ada, g_mix_pre, g_mix_post, g_ff_pre, g_ff_post, w_in, b_f, w_pool, pool_scale, conv_w, w_branch, w_out, w_ff1, w_ff2, loss_target, m_w_ada, m_b_ada, m_g_mix_pre, m_g_mix_post, m_g_ff_pre, m_g_ff_post, m_w_in, m_b_f, m_w_pool, m_pool_scale, m_conv_w, m_w_branch, m_w_out, m_w_ff1, m_w_ff2, v_w_ada, v_b_ada, v_g_mix_pre, v_g_mix_post, v_g_ff_pre, v_g_ff_post, v_w_in, v_b_f, v_w_pool, v_pool_scale, v_conv_w, v_w_branch, v_w_out, v_w_ff1, v_w_ff2):
    ix, iy, ic = lax.axis_index("x"), lax.axis_index("y"), lax.axis_index("c")
    me = 4 * ix + 2 * iy + ic
    route = jnp.stack([ic, 2 * (1 - ix) + iy, 2 * ix + (1 - iy), 2 * (1 - ix) + (1 - iy)]).astype(jnp.int32)
    place = jnp.stack([me, 2 * ix + iy]).astype(jnp.int32)
    wt_in, mt_in, vt_in = (jnp.transpose(a, (0, 2, 1)) for a in (w_in, m_w_in, v_w_in))

    c_all = _all_gather([_pad_rows(c)], "gather_c")[0][:, 0, :]
    c_pad = _pad_rows(c_all, ADA_ROWS)
    b_cols = lax.dynamic_slice_in_dim(b_ada, me * ADA_COLS, ADA_COLS, axis=1)
    b_cols = jnp.broadcast_to(b_cols[:, None, :], (DEPTH, 8, ADA_COLS))
    mod_part = _ada_fwd(c_pad, w_ada, b_cols, "ada_fwd")
    mod_all = _all_gather([mod_part.reshape(DEPTH * ADA_ROWS, ADA_COLS)], "gather_mod")[0]
    mod_all = mod_all.reshape(N_DEV, DEPTH, ADA_ROWS, ADA_COLS)
    mod_mine = lax.dynamic_index_in_dim(mod_all, me, axis=2, keepdims=False)
    mod_mine = jnp.transpose(mod_mine, (1, 0, 2)).reshape(DEPTH, 6, D)

    cw_cols = CONV_W // N_DEV
    cw_send = jnp.pad(conv_w.reshape(DEPTH * 3, cw_cols), ((0, 8 - DEPTH * 3), (0, LANE - cw_cols)))
    win_in = _window(wt_in, place[0:1], "w_in_window")
    send = [[w[l].astype(BF16) for w in (win_in, w_branch, w_out, w_ff1, w_ff2)] for l in range(DEPTH)]
    first = _all_gather(send[0][:1], "gather_weights_l0_in", sequencer_id=1, after=mod_all)
    rest = _all_gather(send[0][1:] + [cw_send], "gather_weights_l0_rest", sequencer_id=2, after=first[0])
    first1 = _all_gather(send[1][:1], "gather_weights_l1_in", sequencer_id=3, after=first[0])
    rest1 = _all_gather(send[1][1:], "gather_weights_l1_rest", sequencer_id=12, after=first[0])
    gathered = [first + rest[:4], first1 + rest1]
    cw_all = rest[4][:, :DEPTH * 3, :cw_cols].reshape(N_DEV, DEPTH, 3, cw_cols)

    def first_operands(l, p_in):
        wp_bd = jnp.zeros((POOL_W, POOL_W), F32)
        for g in range(4):
            wp_bd = wp_bd.at[64 * g:64 * (g + 1), 64 * g:64 * (g + 1)].set(w_pool[l, g])
        return dict(w_in_t=_z_rows_from_windows(p_in), wp_bd=wp_bd.astype(BF16),
                    pool_scale=_pad_rows(pool_scale[l][None, :]), b_f=_pad_rows(jnp.pad(b_f[l], (0, LANE - 8))[None, :]))

    def rest_operands(l, rest):
        p_br, p_out, p_ff1, p_ff2 = rest
        w_br_full = p_br.reshape(D, D)
        cw_full = jnp.transpose(cw_all[:, l], (1, 0, 2)).reshape(3, CONV_W)
        return dict(wa=w_br_full[0:A_WIDTH], wb=w_br_full[A_WIDTH:A_WIDTH + POOL_W], wc=w_br_full[A_WIDTH + POOL_W:],
                    w_out=p_out.reshape(D, D), w_ff1=p_ff1, w_ff2=p_ff2.reshape(D_FF, D), conv_w=_pad_rows(cw_full))

    xs = x[0]
    saved, layers = [], []
    for l in range(DEPTH):
        p_in, rest = gathered[l][0], gathered[l][1:5]
        if l > 0:
            xs, p_in = lax.optimization_barrier((xs, p_in))
        wts = first_operands(l, p_in)

        def arrive(t, l=l, rest=rest, wts=wts):
            if l > 0:
                t, rest = lax.optimization_barrier((t, rest))
            wts.update(rest_operands(l, rest))
            return t

        wts["arrive"] = arrive
        gvec = _pad_rows(jnp.stack([g_mix_pre[l], g_mix_post[l], g_ff_pre[l], g_ff_post[l]]))
        layers.append((wts, gvec, _pad_rows(mod_mine[l])))
        xs, sv = _layer_fwd(l, xs, *layers[l])
        saved.append(sv)
    dx, loss_part = _loss_head(xs, loss_target[0], "loss_head")
    small_grads = [None] * DEPTH
    mine, sibs, landed = ({} for _ in range(3))
    seq_id = iter(range(4, 4 + 4 * DEPTH))
    last = [gathered[DEPTH - 1][1]]

    def start(group, grads):
        mine[group] = grads
        sibs[group] = _sibling_exchange(grads, f"rs_sibling_{group}", sequencer_id=next(seq_id), after=last[0])
        last[0] = sibs[group][0]

    def finish(group, later):
        later, (grads, sib) = lax.optimization_barrier((later, (mine[group], sibs[group])))
        sends = [_pair_sums(g, p, route, f"rs_pair_sums_{group}_{k}") for k, (g, p) in enumerate(zip(grads, sib))]
        later, sends = lax.optimization_barrier((later, sends))
        landed[group] = _chip_exchange(sends, f"rs_chips_{group}", sequencer_id=next(seq_id), after=last[0])
        last[0] = landed[group][0]
        return later

    pending = None
    for l in reversed(range(DEPTH)):
        hook = (lambda da: da) if pending is None else functools.partial(finish, pending)
        dx, ffn_grads, ffn_reds = _ffn_bwd(l, dx, saved[l], *layers[l], hook)
        start(f"ffn_l{l}", ffn_grads)
        dx, mix_grads, small_grads[l] = _mixer_bwd(l, dx, saved[l], *layers[l], ffn_reds,
                                                   functools.partial(finish, f"ffn_l{l}"))
        start(f"mix_l{l}", mix_grads)
        pending = f"mix_l{l}"
    grad_x = dx[None]

    big_w = [wt_in, w_branch, w_out, w_ff1, w_ff2]
    big_m = [mt_in, m_w_branch, m_w_out, m_w_ff1, m_w_ff2]
    big_v = [vt_in, v_w_branch, v_w_out, v_w_ff1, v_w_ff2]
    where = [("mix", 0), ("mix", 1), ("mix", 2), ("ffn", 0), ("ffn", 1)]

    def reduce_and_update(k):
        group, at = where[k]
        return _reduce_adamw([mine[f"{group}_l{l}"][at] for l in range(DEPTH)],
                             [sibs[f"{group}_l{l}"][at] for l in range(DEPTH)],
                             [landed[f"{group}_l{l}"][at] for l in range(DEPTH)], place, big_w[k], big_m[k], big_v[k],
                             f"rs_sum_adamw_{k}")

    big_res = {k: list(reduce_and_update(k)) for k in (3, 4)}
    big_res[3][0] = finish(pending, big_res[3][0])

    small = {k: jnp.stack([small_grads[l][k] for l in range(DEPTH)]) for k in SMALL_KEYS}
    payload = _pack([small[k] for k in SMALL_KEYS] + [loss_part[0:1, 0:1]], 8, F32)
    small_all = _all_gather([payload], "gather_small")[0]
    dmod_all = small_all[:, 0:DEPTH * 6, :].reshape(N_DEV, DEPTH, 6 * D)
    summed = _unpack(_sum_slabs(small_all, "sum_small").reshape(-1), SMALL_SHAPES + [(1, 1)])
    sg = dict(zip(SMALL_KEYS, summed))
    loss = summed[-1][0, 0]
    dmod_cols = lax.dynamic_slice_in_dim(dmod_all, me * ADA_COLS, ADA_COLS, axis=2)
    dmod_cols = jnp.pad(jnp.transpose(dmod_cols, (1, 0, 2)), ((0, 0), (0, ADA_ROWS - N_DEV), (0, 0)))
    g_w_ada = _ada_bwd(c_pad, dmod_cols, "ada_bwd")
    g_conv_w = lax.dynamic_slice_in_dim(sg["conv_w"], me * (CONV_W // N_DEV), CONV_W // N_DEV, axis=2)

    ada_out = [g_w_ada] + list(_adamw(w_ada, g_w_ada, m_w_ada, v_w_ada, "adamw_ada"))
    rest_w = [b_ada, g_mix_pre, g_mix_post, g_ff_pre, g_ff_post, b_f, w_pool, pool_scale, conv_w]
    rest_m = [m_b_ada, m_g_mix_pre, m_g_mix_post, m_g_ff_pre, m_g_ff_post, m_b_f, m_w_pool, m_pool_scale, m_conv_w]
    rest_v = [v_b_ada, v_g_mix_pre, v_g_mix_post, v_g_ff_pre, v_g_ff_post, v_b_f, v_w_pool, v_pool_scale, v_conv_w]
    rest_g = [sg["mod"], sg["g_mix_pre"], sg["g_mix_post"], sg["g_ff_pre"], sg["g_ff_post"], sg["b_f"],
              sg["w_pool"], sg["pool_scale"], g_conv_w]
    rest_shapes = [a.shape for a in rest_w]
    upd = _adamw(_pack(rest_w, 8, F32)[None], _pack(rest_g, 8, F32)[None], _pack(rest_m, 8, F32)[None],
                 _pack(rest_v, 8, F32)[None], "adamw_rest")
    rest_out = [rest_g] + [_unpack(arr.reshape(-1), rest_shapes) for arr in upd]
    rest_out = [[ada_out[which]] + rest_out[which] for which in range(4)]

    landed[pending], rest_out = lax.optimization_barrier((landed[pending], rest_out))
    big_res.update({k: reduce_and_update(k) for k in (0, 1, 2)})
    big_out = [[jnp.transpose(big_res[k][which], (0, 2, 1)) if k == 0 else big_res[k][which] for k in range(5)]
               for which in range(4)]

    def ordered(k):
        r, b = rest_out[k], big_out[k]
        return [r[0], r[1], r[2], r[3], r[4], r[5], b[0], r[6], r[7], r[8], r[9], b[1], b[2], b[3], b[4]]

    return (loss, grad_x, *ordered(0), *ordered(1), *ordered(2), *ordered(3))
```

```python
import functools

import jax
import jax.numpy as jnp
from jax import lax
from jax.experimental import pallas as pl
from jax.experimental.pallas import tpu as pltpu
from jax.experimental.pallas import tpu_sc as plsc

F32 = jnp.float32
BF16 = jnp.bfloat16

N_DEV = 8
D = 1024
S = 2048
DEPTH = 2
D_FF = 4 * D
A_WIDTH = 512
HEAD_DIM = 64
N_PAIR = 4
POOL_W = 256
CONV_W = 256
IN_COLS = 5640
ADA_COLS = 6 * D // N_DEV
IN_SHARD = IN_COLS // N_DEV
RMS_EPS = 1e-6
NEG_INF = -1e30
ATT_SCALE = HEAD_DIM ** -0.5

NZ = 5760
Z_PC = 0
Z_G = 1024
Z_Q = 4096
Z_K = 4608
Z_V = 5120
Z_F = 5632

LR, B1, B2, EPS, WD, STEP = 0.001, 0.9, 0.999, 1e-08, 0.01, 10

LANE = 128
VMEM_LIMIT_BYTES = 48 * 1024 * 1024
TS = 512
TQ = 256
TQ_FWD = 512
HEADS_PER_STEP = 8
HEADS_PER_STEP_FWD = 8


def _params(sem=None):
    return pltpu.CompilerParams(dimension_semantics=sem, vmem_limit_bytes=VMEM_LIMIT_BYTES)


def _pick(n, target):
    best = None
    for t in range(LANE, min(n, target) + 1, LANE):
        if n % t == 0:
            best = t
    return n if best is None else best


def _matmul(a, b, mode, name, out_dtype=F32, tm=2048, tn=1024, tk=2048, b_col_shards=False, out_col_shards=False,
            extra=(), epilogue=None, out_dtypes=None):
    if b_col_shards:
        shards, b_rows, shard_cols = b.shape
        b_shape = (b_rows, shards * shard_cols)
    else:
        b_shape = b.shape
    if mode == "nn":
        (m, k), (k2, n) = a.shape, b_shape
    elif mode == "nt":
        (m, k), (n, k2) = a.shape, b_shape
    else:
        (k, m), (k2, n) = a.shape, b_shape
    assert k == k2, (a.shape, b.shape, mode)
    tm, tn, tk = _pick(m, tm), _pick(n, tn), _pick(k, tk)
    if b_col_shards and mode == "nn":
        tn = shard_cols
    per_step = 1
    if b_col_shards and mode == "nt":
        per_step = max(1, min(tk, 1024) // shard_cols)
        tk = per_step * shard_cols
    if out_col_shards:
        tn = n // N_DEV
    nk = k // tk
    if mode == "nn":
        a_spec = pl.BlockSpec((tm, tk), lambda i, j, kk: (i, kk))
        b_spec = (pl.BlockSpec((None, tk, tn), lambda i, j, kk: (j, kk, 0)) if b_col_shards else
                  pl.BlockSpec((tk, tn), lambda i, j, kk: (kk, j)))
        dims = (((1,), (0,)), ((), ()))
    elif mode == "nt":
        a_spec = pl.BlockSpec((tm, tk), lambda i, j, kk: (i, kk))
        b_spec = (pl.BlockSpec((per_step, tn, shard_cols), lambda i, j, kk: (kk, j, 0)) if b_col_shards else
                  pl.BlockSpec((tn, tk), lambda i, j, kk: (j, kk)))
        dims = (((1,), (1,)), ((), ()))
    else:
        assert not b_col_shards
        a_spec = pl.BlockSpec((tk, tm), lambda i, j, kk: (kk, i))
        b_spec = pl.BlockSpec((tk, tn), lambda i, j, kk: (kk, j))
        dims = (((0,), (0,)), ((), ()))
    if out_col_shards:
        out_shape = jax.ShapeDtypeStruct((N_DEV, m, tn), out_dtype)
        out_spec = pl.BlockSpec((None, tm, tn), lambda i, j, kk: (j, i, 0))
    else:
        out_shape = jax.ShapeDtypeStruct((m, n), out_dtype)
        out_spec = pl.BlockSpec((tm, tn), lambda i, j, kk: (i, j))

    n_extra = len(extra)
    extra_specs = [pl.BlockSpec((tm, tn), lambda i, j, kk, off=off: (i, j + off // tn)) for _, off in extra]
    if epilogue is not None:
        assert not out_col_shards and all(off % tn == 0 for _, off in extra)
        out_shape = [jax.ShapeDtypeStruct((m, n), dt) for dt in out_dtypes]
        out_spec = [pl.BlockSpec((tm, tn), lambda i, j, kk: (i, j)) for _ in out_dtypes]

    def product(a_ref, b_ref):
        if b_col_shards and mode == "nt":
            b_tile = jnp.concatenate([b_ref[s] for s in range(per_step)], axis=1) if per_step > 1 else b_ref[0]
        else:
            b_tile = b_ref[...]
        return lax.dot_general(a_ref[...].astype(BF16), b_tile.astype(BF16), dims, preferred_element_type=F32)

    def write(acc, extra_refs, o_refs):
        if epilogue is None:
            o_refs[0][...] = acc.astype(out_dtype)
        else:
            for o_ref, tile in zip(o_refs, epilogue(acc, *[r[...] for r in extra_refs])):
                o_ref[...] = tile.astype(o_ref.dtype)

    def body_one_pass(a_ref, b_ref, *refs):
        write(product(a_ref, b_ref), refs[:n_extra], refs[n_extra:])

    def body(a_ref, b_ref, *refs):
        acc_ref = refs[-1]
        kk = pl.program_id(2)

        @pl.when(kk == 0)
        def _():
            acc_ref[...] = product(a_ref, b_ref)

        @pl.when(kk > 0)
        def _():
            acc_ref[...] += product(a_ref, b_ref)

        @pl.when(kk == nk - 1)
        def _():
            write(acc_ref[...], refs[:n_extra], refs[n_extra:-1])

    return pl.pallas_call(
        body_one_pass if nk == 1 else body, name=name,
        out_shape=out_shape,
        grid=(m // tm, n // tn, nk),
        in_specs=[a_spec, b_spec] + extra_specs,
        out_specs=out_spec,
        scratch_shapes=[] if nk == 1 else [pltpu.VMEM((tm, tn), F32)],
        compiler_params=_params(("parallel", "parallel", "arbitrary")),
    )(a, b, *[x for x, _ in extra])


def _row_spec(width=D, col=0):
    return pl.BlockSpec((TS, width), lambda i: (i, col))


def _vec_spec(rows=8, width=D):
    return pl.BlockSpec((rows, width), lambda i: (0, 0))


def _rms(x):
    return lax.rsqrt(jnp.mean(x * x, axis=-1, keepdims=True) + RMS_EPS)


def _prenorm_fwd(x, gvec, mod, g_row, shift_row, scale_row, name):
    def body(x_ref, g_ref, mod_ref, h_ref):
        xv = x_ref[...]
        y = xv * _rms(xv) * g_ref[g_row:g_row + 1, :]
        h = y * (1.0 + mod_ref[scale_row:scale_row + 1, :]) + mod_ref[shift_row:shift_row + 1, :]
        h_ref[...] = h.astype(BF16)

    return pl.pallas_call(
        body, name=name, out_shape=jax.ShapeDtypeStruct((S, D), BF16), grid=(S // TS,),
        in_specs=[_row_spec(), _vec_spec(), _vec_spec()], out_specs=_row_spec(),
        compiler_params=_params(("parallel",)),
    )(x, gvec, mod)


def _prenorm_bwd(x, gvec, mod, dh, dres, g_row, scale_row, name):
    def body(x_ref, g_ref, mod_ref, dh_ref, dres_ref, dx_ref, red_ref):
        i = pl.program_id(0)

        @pl.when(i == 0)
        def _():
            red_ref[...] = jnp.zeros_like(red_ref)

        xv = x_ref[...]
        g = g_ref[g_row:g_row + 1, :]
        r = _rms(xv)
        n = xv * r
        yg = n * g
        dhv = dh_ref[...]
        dyg = dhv * (1.0 + mod_ref[scale_row:scale_row + 1, :])
        dn = dyg * g
        dx = r * (dn - n * jnp.mean(dn * n, axis=-1, keepdims=True))
        dx_ref[...] = dres_ref[...] + dx
        red_ref[0:1, :] += jnp.sum(dhv, axis=0, keepdims=True)
        red_ref[1:2, :] += jnp.sum(dhv * yg, axis=0, keepdims=True)
        red_ref[2:3, :] += jnp.sum(dyg * n, axis=0, keepdims=True)

    return pl.pallas_call(
        body, name=name,
        out_shape=(jax.ShapeDtypeStruct((S, D), F32), jax.ShapeDtypeStruct((8, D), F32)),
        grid=(S // TS,),
        in_specs=[_row_spec(), _vec_spec(), _vec_spec(), _row_spec(), _row_spec()],
        out_specs=(_row_spec(), _vec_spec()),
        compiler_params=_params(("arbitrary",)),
    )(x, gvec, mod, dh, dres)


def _postnorm_fwd(x, y, gvec, mod, g_row, gate_row, name):
    def body(x_ref, y_ref, g_ref, mod_ref, o_ref):
        yv = y_ref[...]
        yn = yv * _rms(yv) * g_ref[g_row:g_row + 1, :]
        o_ref[...] = x_ref[...] + mod_ref[gate_row:gate_row + 1, :] * yn

    return pl.pallas_call(
        body, name=name, out_shape=jax.ShapeDtypeStruct((S, D), F32), grid=(S // TS,),
        in_specs=[_row_spec(), _row_spec(), _vec_spec(), _vec_spec()], out_specs=_row_spec(),
        compiler_params=_params(("parallel",)),
    )(x, y, gvec, mod)


def _postnorm_bwd(y, gvec, mod, dxo, g_row, gate_row, name):
    def body(y_ref, g_ref, mod_ref, dxo_ref, dy_ref, red_ref):
        i = pl.program_id(0)

        @pl.when(i == 0)
        def _():
            red_ref[...] = jnp.zeros_like(red_ref)

        yv = y_ref[...]
        g = g_ref[g_row:g_row + 1, :]
        r = _rms(yv)
        n = yv * r
        dxo = dxo_ref[...]
        dyn = dxo * mod_ref[gate_row:gate_row + 1, :]
        dn = dyn * g
        dy = r * (dn - n * jnp.mean(dn * n, axis=-1, keepdims=True))
        dy_ref[...] = dy.astype(BF16)
        red_ref[0:1, :] += jnp.sum(dxo * (n * g), axis=0, keepdims=True)
        red_ref[1:2, :] += jnp.sum(dyn * n, axis=0, keepdims=True)

    return pl.pallas_call(
        body, name=name,
        out_shape=(jax.ShapeDtypeStruct((S, D), BF16), jax.ShapeDtypeStruct((8, D), F32)),
        grid=(S // TS,),
        in_specs=[_row_spec(), _vec_spec(), _vec_spec(), _row_spec()],
        out_specs=(_row_spec(), _vec_spec()),
        compiler_params=_params(("arbitrary",)),
    )(y, gvec, mod, dxo)


def _loss_head(xf, target, name):
    def body(x_ref, t_ref, dx_ref, loss_ref):
        i = pl.program_id(0)

        @pl.when(i == 0)
        def _():
            loss_ref[...] = jnp.zeros_like(loss_ref)

        e = x_ref[...] - t_ref[...]
        dx_ref[...] = e / float(D)
        per_tok = jnp.mean(e * e, axis=-1, keepdims=True)
        loss_ref[0:1, 0:1] += 0.5 * jnp.sum(per_tok, axis=0, keepdims=True)

    return pl.pallas_call(
        body, name=name,
        out_shape=(jax.ShapeDtypeStruct((S, D), F32), jax.ShapeDtypeStruct((8, LANE), F32)),
        grid=(S // TS,),
        in_specs=[_row_spec(), _row_spec()],
        out_specs=(_row_spec(), pl.BlockSpec((8, LANE), lambda i: (0, 0))),
        compiler_params=_params(("arbitrary",)),
    )(xf, target)


def _relu2_epilogue(a):
    t = jnp.maximum(a, 0.0)
    return a, t * t


def _relu2_bwd_epilogue(dr, a):
    return (dr * (2.0 * jnp.maximum(a, 0.0)),)


def _merge_epilogue(pc, g0, g1, g2, pa, pb):
    return pc, jax.nn.sigmoid(g0) * pa + jax.nn.sigmoid(g1) * pb + jax.nn.sigmoid(g2) * pc


def _merge_bwd_epilogue(dm, g0, g1, g2, pa, pb, pc):
    sg = [jax.nn.sigmoid(g) for g in (g0, g1, g2)]
    return tuple(dm * s for s in sg) + tuple(dm * p * (s * (1.0 - s)) for p, s in zip((pa, pb, pc), sg))


def _shift_down(x, k, row):
    return jnp.where(row >= k, pltpu.roll(x, k, axis=0), 0.0)


def _shift_up(x, k, row):
    n = x.shape[0]
    return jnp.where(row < n - k, pltpu.roll(x, n - k, axis=0), 0.0)


def _cumsum_rows(x, row, reverse=False):
    shift = _shift_up if reverse else _shift_down
    k = 1
    while k < x.shape[0]:
        x = x + shift(x, k, row)
        k *= 2
    return x


def _full_spec(shape, idx=(0, 0)):
    return pl.BlockSpec(shape, lambda i: idx)


def _pool_window_select(lane, a2, a4, a8, a16):
    return jnp.where(lane < 64, a2, jnp.where(lane < 128, a4, jnp.where(lane < 192, a8, a16)))


def _pool_p(u, row, lane):
    t2 = u + _shift_down(u, 1, row)
    t4 = t2 + _shift_down(t2, 2, row)
    t8 = t4 + _shift_down(t4, 4, row)
    t16 = t8 + _shift_down(t8, 8, row)
    tw = _pool_window_select(lane, t2, t4, t8, t16)
    cnt = jnp.minimum((row + 1).astype(F32), _pool_window_select(lane, 2.0, 4.0, 8.0, 16.0))
    return tw / cnt - u, cnt


def _pool_fwd(z, wp_bd, pscale, name):
    def body(u_ref, w_ref, s_ref, o_ref):
        row = lax.broadcasted_iota(jnp.int32, (S, POOL_W), 0)
        lane = lax.broadcasted_iota(jnp.int32, (S, POOL_W), 1)
        p, _ = _pool_p(u_ref[...], row, lane)
        y = jnp.dot(p.astype(BF16), w_ref[...], preferred_element_type=F32)
        o_ref[...] = y * s_ref[0:1, :]

    return pl.pallas_call(
        body, name=name, out_shape=jax.ShapeDtypeStruct((S, POOL_W), F32), grid=(1,),
        in_specs=[_full_spec((S, POOL_W), (0, Z_PC // POOL_W)), _full_spec((POOL_W, POOL_W)), _full_spec((8, POOL_W))],
        out_specs=_full_spec((S, POOL_W)),
        compiler_params=_params(("arbitrary",)),
    )(z, wp_bd, pscale)


def _pool_bwd(z, wp_bd, pscale, dbr, name):
    def body(u_ref, w_ref, s_ref, dbr_ref, du_ref, dw_ref, red_ref):
        row = lax.broadcasted_iota(jnp.int32, (S, POOL_W), 0)
        lane = lax.broadcasted_iota(jnp.int32, (S, POOL_W), 1)
        p, cnt = _pool_p(u_ref[...], row, lane)
        pb = p.astype(BF16)
        y = jnp.dot(pb, w_ref[...], preferred_element_type=F32)
        dbr = dbr_ref[...]
        red_ref[...] = jnp.zeros_like(red_ref)
        red_ref[0:1, :] = jnp.sum(dbr * y, axis=0, keepdims=True)
        dy = (dbr * s_ref[0:1, :]).astype(BF16)
        dw_ref[...] = lax.dot_general(pb, dy, (((0,), (0,)), ((), ())), preferred_element_type=F32)
        dp = lax.dot_general(dy, w_ref[...], (((1,), (1,)), ((), ())), preferred_element_type=F32)
        g = dp / cnt
        a2 = g + _shift_up(g, 1, row)
        a4 = a2 + _shift_up(a2, 2, row)
        a8 = a4 + _shift_up(a4, 4, row)
        a16 = a8 + _shift_up(a8, 8, row)
        du_ref[...] = (_pool_window_select(lane, a2, a4, a8, a16) - dp).astype(BF16)

    return pl.pallas_call(
        body, name=name,
        out_shape=(jax.ShapeDtypeStruct((S, POOL_W), BF16), jax.ShapeDtypeStruct((POOL_W, POOL_W), F32),
                   jax.ShapeDtypeStruct((8, POOL_W), F32)),
        grid=(1,),
        in_specs=[_full_spec((S, POOL_W), (0, Z_PC // POOL_W)), _full_spec((POOL_W, POOL_W)), _full_spec((8, POOL_W)),
                  _full_spec((S, POOL_W))],
        out_specs=(_full_spec((S, POOL_W)), _full_spec((POOL_W, POOL_W)), _full_spec((8, POOL_W))),
        compiler_params=_params(("arbitrary",)),
    )(z, wp_bd, pscale, dbr)


def _conv_specs():
    base = Z_PC // CONV_W
    return [_full_spec((S, CONV_W), (0, base + 1)), _full_spec((S, CONV_W), (0, base + 2)),
            _full_spec((S, CONV_W), (0, base + 3)), _full_spec((8, CONV_W))]


def _conv_fwd(z, cw, name):
    def body(h_ref, b_ref, c_ref, w_ref, o_ref):
        row = lax.broadcasted_iota(jnp.int32, (S, CONV_W), 0)
        u = c_ref[...] * h_ref[...]
        y = (w_ref[0:1, :] * _shift_down(u, 2, row) + w_ref[1:2, :] * _shift_down(u, 1, row) + w_ref[2:3, :] * u)
        o_ref[...] = b_ref[...] * y

    return pl.pallas_call(
        body, name=name, out_shape=jax.ShapeDtypeStruct((S, CONV_W), F32), grid=(1,),
        in_specs=_conv_specs(), out_specs=_full_spec((S, CONV_W)),
        compiler_params=_params(("arbitrary",)),
    )(z, z, z, cw)


def _conv_bwd(z, cw, dbr, name):
    def body(h_ref, b_ref, c_ref, w_ref, dbr_ref, d_ref, red_ref):
        row = lax.broadcasted_iota(jnp.int32, (S, CONV_W), 0)
        h, cg = h_ref[...], c_ref[...]
        u = cg * h
        u1 = _shift_down(u, 1, row)
        u2 = _shift_down(u, 2, row)
        y = w_ref[0:1, :] * u2 + w_ref[1:2, :] * u1 + w_ref[2:3, :] * u
        dbr = dbr_ref[...]
        dy = dbr * b_ref[...]
        du = w_ref[2:3, :] * dy + w_ref[1:2, :] * _shift_up(dy, 1, row) + w_ref[0:1, :] * _shift_up(dy, 2, row)
        d_ref[:, 0:CONV_W] = (du * cg).astype(BF16)
        d_ref[:, CONV_W:2 * CONV_W] = (dbr * y).astype(BF16)
        d_ref[:, 2 * CONV_W:3 * CONV_W] = (du * h).astype(BF16)
        red_ref[...] = jnp.zeros_like(red_ref)
        red_ref[0:1, :] = jnp.sum(dy * u2, axis=0, keepdims=True)
        red_ref[1:2, :] = jnp.sum(dy * u1, axis=0, keepdims=True)
        red_ref[2:3, :] = jnp.sum(dy * u, axis=0, keepdims=True)

    return pl.pallas_call(
        body, name=name,
        out_shape=(jax.ShapeDtypeStruct((S, 3 * CONV_W), BF16), jax.ShapeDtypeStruct((8, CONV_W), F32)),
        grid=(1,),
        in_specs=_conv_specs() + [_full_spec((S, CONV_W))],
        out_specs=(_full_spec((S, 3 * CONV_W)), _full_spec((8, CONV_W))),
        compiler_params=_params(("arbitrary",)),
    )(z, z, z, cw, dbr)


_NT = (((1,), (1,)), ((), ()))
_TN = (((0,), (0,)), ((), ()))
N_HEAD = 2 * N_PAIR


def _split3(x):
    hi = x.astype(BF16).astype(F32)
    mid = (x - hi).astype(BF16).astype(F32)
    lo = (x - hi - mid).astype(BF16).astype(F32)
    return hi, mid, lo


def _spare(lane, e, k):
    return lane == 64 * (1 - e) + k


def _spare3(lane, e, k):
    base = 64 * (1 - e) + k
    return (lane >= base) & (lane < base + 3)


def _put3(lane, e, k, pieces, rest):
    out = rest
    for n, piece in enumerate(pieces):
        out = jnp.where(_spare(lane, e, k + n), piece, out)
    return out


def _attn_prep(z, bf, name):
    def body(q_ref, k_ref, v_ref, f_ref, b_ref, qa_ref, ka_ref, va_ref, kat_ref, cum_ref):
        p = pl.program_id(0)
        row = lax.broadcasted_iota(jnp.int32, (S, LANE), 0)
        lane = lax.broadcasted_iota(jnp.int32, (S, LANE), 1)

        @pl.when(p == 0)
        def _():
            xv = f_ref[...] + b_ref[0:1, :]
            ls = jnp.minimum(xv, 0.0) - jnp.log(1.0 + jnp.exp(-jnp.abs(xv)))
            cum_ref[...] = _cumsum_rows(jnp.where(lane < N_HEAD, ls, 0.0), row)

        cum = cum_ref[...]
        q, k, v = q_ref[...], k_ref[...], v_ref[...]
        for e in range(2):
            head = (lane >= 64) if e else (lane < 64)
            f = jnp.sum(jnp.where(lane == 2 * p + e, cum, 0.0), axis=1, keepdims=True)
            pieces = _split3(f)
            qa = jnp.where(head, q * ATT_SCALE, _put3(lane, e, 0, pieces, jnp.where(_spare3(lane, e, 3), 1.0, 0.0)))
            ones = jnp.where(_spare3(lane, e, 0) | _spare3(lane, e, 6), 1.0, 0.0)
            ka = jnp.where(head, k, _put3(lane, e, 3, [-x for x in pieces], ones))
            va = jnp.where(head, v, jnp.where(_spare3(lane, e, 0), 1.0, 0.0))
            qa_ref[e] = qa.astype(BF16)
            ka_ref[e] = ka.astype(BF16)
            va_ref[e] = va.astype(BF16)
            kat_ref[e] = ka.T.astype(BF16)

    qb, kb, vb = Z_Q // LANE, Z_K // LANE, Z_V // LANE
    heads = jax.ShapeDtypeStruct((N_HEAD, S, LANE), BF16)
    pair = pl.BlockSpec((2, S, LANE), lambda p: (p, 0, 0))
    return pl.pallas_call(
        body, name=name,
        out_shape=(heads, heads, heads, jax.ShapeDtypeStruct((N_HEAD, LANE, S), BF16)),
        grid=(N_PAIR,),
        in_specs=[pl.BlockSpec((S, LANE), lambda p: (0, qb + p)), pl.BlockSpec((S, LANE), lambda p: (0, kb + p)),
                  pl.BlockSpec((S, LANE), lambda p: (0, vb + p)), pl.BlockSpec((S, LANE), lambda p: (0, Z_F // LANE)),
                  pl.BlockSpec((8, LANE), lambda p: (0, 0))],
        out_specs=(pair, pair, pair, pl.BlockSpec((2, LANE, S), lambda p: (p, 0, 0))),
        scratch_shapes=[pltpu.VMEM((S, LANE), F32)],
        compiler_params=_params(("arbitrary",)),
    )(z, z, z, z, bf)


def _attn_bwd_prep(qa, o, lse, do, name):
    def body(qa_ref, o_ref, lse_ref, do_ref, qa2_ref, doa_ref):
        lane = lax.broadcasted_iota(jnp.int32, (S, LANE), 1)
        dov, ov, lsev = do_ref[...], o_ref[...], lse_ref[...]
        for e in range(2):
            head = (lane >= 64) if e else (lane < 64)
            dsum = jnp.sum(jnp.where(head, dov * ov, 0.0), axis=1, keepdims=True)
            doa_ref[e] = jnp.where(head, dov, _put3(lane, e, 0, [-x for x in _split3(dsum)], 0.0)).astype(BF16)
            lse_col = lsev[:, 64 * e:64 * e + 1]
            qa2_ref[e] = _put3(lane, e, 6, [-x for x in _split3(lse_col)], qa_ref[e].astype(F32)).astype(BF16)

    heads = jax.ShapeDtypeStruct((N_HEAD, S, LANE), BF16)
    pair = pl.BlockSpec((2, S, LANE), lambda p: (p, 0, 0))
    cols = pl.BlockSpec((S, LANE), lambda p: (0, p))
    return pl.pallas_call(
        body, name=name, out_shape=(heads, heads), grid=(N_PAIR,),
        in_specs=[pair, cols, cols, cols], out_specs=(pair, pair),
        compiler_params=_params(("parallel",)),
    )(qa, o, lse, do)


def _attn_bwd_post(z, bf, dqt, dka, dva, name):
    def body(f_ref, b_ref, dqt_ref, dk_ref, dv_ref, dq_out, dk_out, dv_out, dfl_ref, red_ref, dcum_ref):
        p = pl.program_id(0)

        @pl.when(p == 0)
        def _():
            dcum_ref[...] = jnp.zeros_like(dcum_ref)

        row = lax.broadcasted_iota(jnp.int32, (S, LANE), 0)
        lane = lax.broadcasted_iota(jnp.int32, (S, LANE), 1)
        dqa = [dqt_ref[e].T for e in range(2)]
        dq_out[...] = (jnp.where(lane < 64, dqa[0], dqa[1]) * ATT_SCALE).astype(BF16)
        dk_out[...] = jnp.where(lane < 64, dk_ref[0], dk_ref[1]).astype(BF16)
        dv_out[...] = jnp.where(lane < 64, dv_ref[0], dv_ref[1]).astype(BF16)
        for e in range(2):
            d_query = jnp.sum(jnp.where(_spare(lane, e, 0), dqa[e], 0.0), axis=1, keepdims=True)
            d_key = jnp.sum(jnp.where(_spare(lane, e, 3), dk_ref[e], 0.0), axis=1, keepdims=True)
            dcum_ref[...] += jnp.where(lane == 2 * p + e, d_query - d_key, 0.0)

        @pl.when(p == N_PAIR - 1)
        def _():
            dls = _cumsum_rows(dcum_ref[...], row, reverse=True)
            xv = f_ref[...] + b_ref[0:1, :]
            dx = jnp.where(lane < N_HEAD, dls * jax.nn.sigmoid(-xv), 0.0)
            dfl_ref[...] = dx.astype(BF16)
            red_ref[...] = jnp.zeros_like(red_ref)
            red_ref[0:1, :] = jnp.sum(dx, axis=0, keepdims=True)

    wide = jax.ShapeDtypeStruct((S, N_PAIR * LANE), BF16)
    cols = pl.BlockSpec((S, LANE), lambda p: (0, p))
    pair = pl.BlockSpec((2, S, LANE), lambda p: (p, 0, 0))
    return pl.pallas_call(
        body, name=name,
        out_shape=(wide, wide, wide, jax.ShapeDtypeStruct((S, LANE), BF16), jax.ShapeDtypeStruct((8, LANE), F32)),
        grid=(N_PAIR,),
        in_specs=[pl.BlockSpec((S, LANE), lambda p: (0, Z_F // LANE)), pl.BlockSpec((8, LANE), lambda p: (0, 0)),
                  pl.BlockSpec((2, LANE, S), lambda p: (p, 0, 0)), pair, pair],
        out_specs=(cols, cols, cols, pl.BlockSpec((S, LANE), lambda p: (0, 0)), pl.BlockSpec((8, LANE), lambda p: (0, 0))),
        scratch_shapes=[pltpu.VMEM((S, LANE), F32)],
        compiler_params=_params(("arbitrary",)),
    )(z, bf, dqt, dka, dva)


def _attn_fwd(qa, ka, va, name):
    tq, tk = TQ_FWD, TQ
    ratio = tq // tk

    def body(qa_ref, ka_ref, va_ref, o_ref, lse_ref):
        i = pl.program_id(1)
        lane = lax.broadcasted_iota(jnp.int32, (tq, LANE), 1)
        row = lax.broadcasted_iota(jnp.int32, (tq, tk), 0)
        col = lax.broadcasted_iota(jnp.int32, (tq, tk), 1)
        nh = HEADS_PER_STEP_FWD
        qs = [qa_ref[h] for h in range(nh)]

        def block(j, carry, masked):
            off = pl.multiple_of(j * tk, tk)
            out = []
            for h in range(nh):
                m, acc = carry[h]
                s = lax.dot_general(qs[h], ka_ref[h, pl.ds(off, tk), :], _NT, preferred_element_type=F32)
                if masked:
                    s = jnp.where(col + (j - ratio * i) * tk > row, NEG_INF, s)
                mn = jnp.maximum(m, jnp.max(s, axis=1, keepdims=True))
                p = jnp.exp(s - mn).astype(BF16)
                acc = jnp.exp(m - mn) * acc + jnp.dot(p, va_ref[h, pl.ds(off, tk), :], preferred_element_type=F32)
                out.append((mn, acc))
            return tuple(out)

        init = (jnp.full((tq, 1), NEG_INF, F32), jnp.zeros((tq, LANE), F32))
        carry = lax.fori_loop(0, ratio * i, lambda j, c: block(j, c, False), (init,) * nh)
        for d in range(ratio):
            carry = block(ratio * i + d, carry, True)
        res = []
        for h in range(nh):
            m, acc = carry[h]
            l = jnp.sum(jnp.where(_spare(lane, h % 2, 0), acc, 0.0), axis=1, keepdims=True)
            res.append((acc / l, m + jnp.log(l)))
        for g in range(nh // 2):
            o_ref[:, g * LANE:(g + 1) * LANE] = jnp.where(lane < 64, res[2 * g][0], res[2 * g + 1][0])
            lse_ref[:, g * LANE:(g + 1) * LANE] = jnp.where(lane < 64, res[2 * g][1], res[2 * g + 1][1])

    nh = HEADS_PER_STEP_FWD
    out = jax.ShapeDtypeStruct((S, N_PAIR * LANE), F32)
    wide = pl.BlockSpec((tq, 64 * nh), lambda p, i: (i, p))
    return pl.pallas_call(
        body, name=name, out_shape=(out, out), grid=(N_HEAD // nh, S // tq),
        in_specs=[pl.BlockSpec((nh, tq, LANE), lambda p, i: (p, i, 0)), pl.BlockSpec((nh, S, LANE), lambda p, i: (p, 0, 0)),
                  pl.BlockSpec((nh, S, LANE), lambda p, i: (p, 0, 0))],
        out_specs=(wide, wide),
        compiler_params=_params(("parallel", "parallel")),
    )(qa, ka, va)


def _attn_bwd(qa2, ka, va, kat, doa, name):
    nq = S // TQ

    def body(qa_ref, ka_ref, va_ref, kat_ref, doa_ref, dqt_ref, dk_ref, dv_ref):
        j = pl.program_id(1)

        @pl.when(j == 0)
        def _():
            dqt_ref[...] = jnp.zeros_like(dqt_ref)

        key = lax.broadcasted_iota(jnp.int32, (TQ, TQ), 0)
        qry = lax.broadcasted_iota(jnp.int32, (TQ, TQ), 1)
        nh = HEADS_PER_STEP
        kav, vav, katv = ([ref[h] for h in range(nh)] for ref in (ka_ref, va_ref, kat_ref))

        def block(i, carry, masked):
            off = pl.multiple_of(i * TQ, TQ)
            out = []
            for h in range(nh):
                dk_acc, dv_acc = carry[h]
                qav = qa_ref[h, pl.ds(off, TQ), :]
                doav = doa_ref[h, pl.ds(off, TQ), :]
                s_t = lax.dot_general(kav[h], qav, _NT, preferred_element_type=F32)
                if masked:
                    s_t = jnp.where(key > qry, NEG_INF, s_t)
                p_t = jnp.exp(s_t)
                ds_t = p_t * lax.dot_general(vav[h], doav, _NT, preferred_element_type=F32)
                dsb = ds_t.astype(BF16)
                dv_acc = dv_acc + jnp.dot(p_t.astype(BF16), doav, preferred_element_type=F32)
                dk_acc = dk_acc + jnp.dot(dsb, qav, preferred_element_type=F32)
                dqt_ref[h, :, pl.ds(off, TQ)] += jnp.dot(katv[h], dsb, preferred_element_type=F32)
                out.append((dk_acc, dv_acc))
            return tuple(out)

        zero = (jnp.zeros((TQ, LANE), F32), jnp.zeros((TQ, LANE), F32))
        carry = block(j, (zero,) * nh, True)
        carry = lax.fori_loop(j + 1, nq, lambda i, c: block(i, c, False), carry)
        for h in range(nh):
            dk_ref[h], dv_ref[h] = carry[h]

    nh = HEADS_PER_STEP
    full = pl.BlockSpec((nh, S, LANE), lambda p, j: (p, 0, 0))
    blk = pl.BlockSpec((nh, TQ, LANE), lambda p, j: (p, j, 0))
    acc = jax.ShapeDtypeStruct((N_HEAD, S, LANE), F32)
    return pl.pallas_call(
        body, name=name,
        out_shape=(jax.ShapeDtypeStruct((N_HEAD, LANE, S), F32), acc, acc),
        grid=(N_HEAD // nh, nq),
        in_specs=[full, blk, blk, pl.BlockSpec((nh, LANE, TQ), lambda p, j: (p, 0, j)), full],
        out_specs=(pl.BlockSpec((nh, LANE, S), lambda p, j: (p, 0, 0)), blk, blk),
        compiler_params=_params(("arbitrary", "arbitrary")),
    )(qa2, ka, va, kat, doa)


ADA_ROWS = 16


def _ada_fwd(c_pad, w_ada, b_cols, name):
    def body(c_ref, w_ref, b_ref, o_ref):
        cv = c_ref[...]
        sc = (cv * jax.nn.sigmoid(cv)).astype(BF16)
        o_ref[0] = jnp.dot(sc, w_ref[0].astype(BF16), preferred_element_type=F32) + b_ref[0, 0:1, :]

    return pl.pallas_call(
        body, name=name, out_shape=jax.ShapeDtypeStruct((DEPTH, ADA_ROWS, ADA_COLS), F32), grid=(DEPTH,),
        in_specs=[pl.BlockSpec((ADA_ROWS, D), lambda l: (0, 0)), pl.BlockSpec((1, D, ADA_COLS), lambda l: (l, 0, 0)),
                  pl.BlockSpec((1, 8, ADA_COLS), lambda l: (l, 0, 0))],
        out_specs=pl.BlockSpec((1, ADA_ROWS, ADA_COLS), lambda l: (l, 0, 0)),
        compiler_params=_params(("parallel",)),
    )(c_pad, w_ada, b_cols)


def _ada_bwd(c_pad, dmod_cols, name):
    def body(c_ref, d_ref, o_ref):
        cv = c_ref[...]
        sc = (cv * jax.nn.sigmoid(cv)).astype(BF16)
        o_ref[0] = lax.dot_general(sc, d_ref[0].astype(BF16), _TN, preferred_element_type=F32)

    return pl.pallas_call(
        body, name=name, out_shape=jax.ShapeDtypeStruct((DEPTH, D, ADA_COLS), F32), grid=(DEPTH,),
        in_specs=[pl.BlockSpec((ADA_ROWS, D), lambda l: (0, 0)), pl.BlockSpec((1, ADA_ROWS, ADA_COLS), lambda l: (l, 0, 0))],
        out_specs=pl.BlockSpec((1, D, ADA_COLS), lambda l: (l, 0, 0)),
        compiler_params=_params(("parallel",)),
    )(c_pad, dmod_cols)


def _adamw_math(w, g, m, v):
    m = B1 * m + (1.0 - B1) * g
    v = B2 * v + (1.0 - B2) * (g * g)
    m_hat = m / (1.0 - B1 ** STEP)
    v_hat = v / (1.0 - B2 ** STEP)
    delta = -LR * (m_hat / (jnp.sqrt(v_hat) + EPS) + WD * w)
    return delta, m, v


def _row_tile(rows, target=256):
    best = 8
    for t in range(8, min(rows, target) + 1, 8):
        if rows % t == 0:
            best = t
    return best


def _adamw(w, g, m, v, name):
    layers, rows, cols = w.shape
    tr = _row_tile(rows)
    spec = pl.BlockSpec((1, tr, cols), lambda l, i: (l, i, 0))

    def body(w_ref, g_ref, m_ref, v_ref, d_ref, nm_ref, nv_ref):
        d_ref[...], nm_ref[...], nv_ref[...] = _adamw_math(w_ref[...], g_ref[...], m_ref[...], v_ref[...])

    out = jax.ShapeDtypeStruct(w.shape, F32)
    return pl.pallas_call(
        body, name=name, out_shape=(out, out, out), grid=(layers, rows // tr),
        in_specs=[spec] * 4, out_specs=(spec,) * 3, compiler_params=_params(("parallel", "parallel")),
    )(w, g, m, v)


def _sum_slabs(x, name):
    n, rows, _ = x.shape
    tr = _row_tile(rows)

    def body(x_ref, o_ref):
        acc = x_ref[0]
        for k in range(1, n):
            acc = acc + x_ref[k]
        o_ref[...] = acc

    return pl.pallas_call(
        body, name=name, out_shape=jax.ShapeDtypeStruct((rows, D), F32), grid=(rows // tr,),
        in_specs=[pl.BlockSpec((n, tr, D), lambda i: (0, i, 0))], out_specs=pl.BlockSpec((tr, D), lambda i: (i, 0)),
        compiler_params=_params(("parallel",)),
    )(x)


_ANY = pl.BlockSpec(memory_space=pl.ANY)
MESH = pl.DeviceIdType.MESH


def _on_sequencer(body, out_shape, sems, operands, after, sequencer_id, name):
    n = len(operands)

    def ordered_body(*refs):
        body(*refs[:n], *refs[n + 1:])

    extra = [] if after is None else [after]
    return pl.kernel(
        body if after is None else ordered_body, out_type=out_shape,
        mesh=plsc.ScalarSubcoreMesh(axis_name="sequencer", num_cores=1), scratch_types=sems,
        compiler_params=pltpu.CompilerParams(collective_id=sequencer_id), name=name)(*operands, *extra)


def _all_gather(xs, name, sequencer_id=None, after=None):
    n = len(xs)

    def body(*refs):
        x_refs, out_refs = refs[:n], refs[n:2 * n]
        send_sems, recv_sems, local_sems = refs[2 * n:]
        x_, y_, c_ = lax.axis_index("x"), lax.axis_index("y"), lax.axis_index("c")
        me, sibling = (x_, y_, c_), (x_, y_, 1 - c_)
        chips = [(1 - x_, y_), (x_, 1 - y_), (1 - x_, 1 - y_)]
        if sequencer_id is not None:
            barrier = pltpu.get_barrier_semaphore()
            peers = [sibling] + [(*chip, pc) for chip in chips for pc in (c_, 1 - c_)]
            for peer in peers:
                pl.semaphore_signal(barrier, inc=1, device_id=peer, device_id_type=MESH)
            pl.semaphore_wait(barrier, len(peers))

        def slot(a, px, py, pc):
            return out_refs[a].at[4 * px + 2 * py + pc]

        def copy(a, k, block, to, src=None):
            return pltpu.make_async_remote_copy(
                src_ref=slot(a, *block) if src is None else src, dst_ref=slot(a, *block),
                send_sem=send_sems.at[7 * a + k], recv_sem=recv_sems.at[7 * a + k], device_id=to, device_id_type=MESH)

        mine = [pltpu.make_async_copy(x_refs[a], slot(a, *me), local_sems.at[a]) for a in range(n)]
        for cp in mine:
            cp.start()
        first = []
        for a in range(n):
            first.append(copy(a, 0, me, sibling, src=x_refs[a]))
            first += [copy(a, 1 + j, me, (*chip, c_), src=x_refs[a]) for j, chip in enumerate(chips)]
        for cp in first:
            cp.start()
        passed = []
        for j, chip in enumerate(chips):
            for a in range(n):
                copy(a, 1 + j, (*chip, c_), me).wait_recv()
                passed.append(copy(a, 4 + j, (*chip, c_), sibling))
                passed[-1].start()
        for a in range(n):
            copy(a, 0, sibling, me).wait_recv()
        for j, chip in enumerate(chips):
            for a in range(n):
                copy(a, 4 + j, (*chip, 1 - c_), me).wait_recv()
        for cp in first + passed:
            cp.wait_send()
        for cp in mine:
            cp.wait()

    out_shape = [jax.ShapeDtypeStruct((N_DEV,) + x.shape, x.dtype) for x in xs]
    sems = [pltpu.SemaphoreType.DMA((7 * n,)), pltpu.SemaphoreType.DMA((7 * n,)), pltpu.SemaphoreType.DMA((n,))]
    if sequencer_id is not None:
        return _on_sequencer(body, out_shape, sems, xs, after, sequencer_id, name)
    return pl.pallas_call(
        body, name=name, out_shape=out_shape, in_specs=[_ANY] * n, out_specs=[_ANY] * n, scratch_shapes=sems)(*xs)


def _sibling_exchange(gs, name, sequencer_id=None, after=None):
    n = len(gs)

    def body(*refs):
        g_refs, p_refs = refs[:n], refs[n:2 * n]
        send_sems, recv_sems = refs[2 * n:]
        x_, y_, c_ = lax.axis_index("x"), lax.axis_index("y"), lax.axis_index("c")
        if sequencer_id is not None:
            barrier = pltpu.get_barrier_semaphore()
            pl.semaphore_signal(barrier, inc=1, device_id=(x_, y_, 1 - c_), device_id_type=MESH)
            pl.semaphore_wait(barrier, 1)
        copies = [pltpu.make_async_remote_copy(
            src_ref=g_refs[a].at[2 * k + (1 - c_)], dst_ref=p_refs[a].at[k], send_sem=send_sems.at[4 * a + k],
            recv_sem=recv_sems.at[4 * a + k], device_id=(x_, y_, 1 - c_), device_id_type=MESH)
            for a in range(n) for k in range(4)]
        for cp in copies:
            cp.start()
        for cp in copies:
            cp.wait()

    out_shape = [jax.ShapeDtypeStruct((4,) + g.shape[1:], g.dtype) for g in gs]
    sems = [pltpu.SemaphoreType.DMA((4 * n,)), pltpu.SemaphoreType.DMA((4 * n,))]
    if sequencer_id is not None:
        return _on_sequencer(body, out_shape, sems, gs, after, sequencer_id, name)
    return pl.pallas_call(
        body, name=name, out_shape=out_shape, in_specs=[_ANY] * n, out_specs=[_ANY] * n, scratch_shapes=sems)(*gs)


def _slab_tiles(rows, cols):
    if rows % 8 == 0:
        return _row_tile(rows), cols
    return rows, 2 * LANE


def _pair_sums(g, p, route, name):
    _, rows, cols = g.shape
    tr, tc = _slab_tiles(rows, cols)

    def body(route_ref, g_ref, p_ref, t_ref):
        t_ref[...] = (g_ref[...] + p_ref[...]).astype(BF16)

    return pl.pallas_call(
        body, name=name, out_shape=jax.ShapeDtypeStruct((3, rows, cols), BF16),
        grid_spec=pltpu.PrefetchScalarGridSpec(
            num_scalar_prefetch=1, grid=(3, rows // tr, cols // tc),
            in_specs=[pl.BlockSpec((1, tr, tc), lambda r, i, j, route_ref: (2 * route_ref[1 + r] + route_ref[0], i, j)),
                      pl.BlockSpec((1, tr, tc), lambda r, i, j, route_ref: (route_ref[1 + r], i, j))],
            out_specs=pl.BlockSpec((1, tr, tc), lambda r, i, j, route_ref: (r, i, j))),
        compiler_params=_params(("parallel", "parallel", "parallel")),
    )(route, g, p)


def _chip_exchange(ts, name, sequencer_id=None, after=None):
    n = len(ts)

    def body(*refs):
        t_refs, l_refs = refs[:n], refs[n:2 * n]
        send_sems, recv_sems = refs[2 * n:]
        x_, y_, c_ = lax.axis_index("x"), lax.axis_index("y"), lax.axis_index("c")
        chips = [(1 - x_, y_), (x_, 1 - y_), (1 - x_, 1 - y_)]
        if sequencer_id is not None:
            barrier = pltpu.get_barrier_semaphore()
            for px, py in chips:
                pl.semaphore_signal(barrier, inc=1, device_id=(px, py, c_), device_id_type=MESH)
            pl.semaphore_wait(barrier, len(chips))
        copies = [pltpu.make_async_remote_copy(
            src_ref=t_refs[a].at[r], dst_ref=l_refs[a].at[r], send_sem=send_sems.at[3 * a + r],
            recv_sem=recv_sems.at[3 * a + r], device_id=(px, py, c_), device_id_type=MESH)
            for a in range(n) for r, (px, py) in enumerate(chips)]
        for cp in copies:
            cp.start()
        for cp in copies:
            cp.wait()

    out_shape = [jax.ShapeDtypeStruct((3,) + t.shape[1:], t.dtype) for t in ts]
    sems = [pltpu.SemaphoreType.DMA((3 * n,)), pltpu.SemaphoreType.DMA((3 * n,))]
    if sequencer_id is not None:
        return _on_sequencer(body, out_shape, sems, ts, after, sequencer_id, name)
    return pl.pallas_call(
        body, name=name, out_shape=out_shape, in_specs=[_ANY] * n, out_specs=[_ANY] * n, scratch_shapes=sems)(*ts)


def _reduce_adamw(gs, ps, landed, place, w, m, v, name):
    layers, rows, cols = w.shape
    assert layers == DEPTH == 2
    tr, tc = _slab_tiles(rows, cols)
    nr, nc = rows // tr, cols // tc
    spec = pl.BlockSpec((1, tr, tc), lambda l, i, j, place_ref: (l, i, j))

    def own(layer, which):
        pi, pj = (nr - 1, nc - 1) if layer == 0 else (0, 0)

        def index(l, i, j, place_ref):
            lead = 0 if which is None else place_ref[which]
            return lead, jnp.where(l == layer, i, pi), jnp.where(l == layer, j, pj)

        return pl.BlockSpec((3 if which is None else 1, tr, tc), index)

    def body(place_ref, g0_ref, p0_ref, l0_ref, g1_ref, p1_ref, l1_ref, w_ref, m_ref, v_ref,
             g_ref, d_ref, nm_ref, nv_ref):
        def update(own_ref, sib_ref, l_ref):
            g = own_ref[0] + sib_ref[0] + l_ref[0].astype(F32) + l_ref[1].astype(F32) + l_ref[2].astype(F32)
            g_ref[0] = g
            d_ref[0], nm_ref[0], nv_ref[0] = _adamw_math(w_ref[0], g, m_ref[0], v_ref[0])

        @pl.when(pl.program_id(0) == 0)
        def _():
            update(g0_ref, p0_ref, l0_ref)

        @pl.when(pl.program_id(0) == 1)
        def _():
            update(g1_ref, p1_ref, l1_ref)

    out = jax.ShapeDtypeStruct(w.shape, F32)
    return pl.pallas_call(
        body, name=name, out_shape=(out, out, out, out),
        grid_spec=pltpu.PrefetchScalarGridSpec(
            num_scalar_prefetch=1, grid=(DEPTH, nr, nc),
            in_specs=[own(0, 0), own(0, 1), own(0, None), own(1, 0), own(1, 1), own(1, None), spec, spec, spec],
            out_specs=(spec, spec, spec, spec)),
        compiler_params=_params(("arbitrary", "arbitrary", "arbitrary")),
    )(place, gs[0], ps[0], landed[0], gs[1], ps[1], landed[1], w, m, v)


def _pack(pieces, row_multiple, dtype, cols=D, rows=None):
    flat = jnp.concatenate([p.astype(dtype).reshape(-1) for p in pieces])
    if rows is None:
        rows = -(-flat.shape[0] // cols)
        rows = -(-rows // row_multiple) * row_multiple
    flat = jnp.pad(flat, (0, rows * cols - flat.shape[0]))
    return flat.reshape(rows, cols)


def _unpack(flat, shapes, lead=()):
    out, off = [], 0
    for shp in shapes:
        n = 1
        for s_ in shp:
            n *= s_
        out.append(lax.slice_in_dim(flat, off, off + n, axis=len(lead)).reshape(lead + tuple(shp)))
        off += n
    return out


WIN_STRIDE = 704
WIN_ROWS = 720
Z_TURN = 1544


def _window(wt, me, name):
    padded = jnp.pad(wt, ((0, 0), (0, WIN_ROWS - IN_SHARD), (0, 0)))

    def body(me_ref, x_ref, o_ref):
        o_ref[0] = pltpu.roll(x_ref[0], me_ref[0], axis=0).astype(BF16)

    spec = pl.BlockSpec((1, WIN_ROWS, D), lambda l, me_ref: (l, 0, 0))
    return pl.pallas_call(
        body, name=name, out_shape=jax.ShapeDtypeStruct((DEPTH, WIN_ROWS, D), BF16),
        grid_spec=pltpu.PrefetchScalarGridSpec(num_scalar_prefetch=1, grid=(DEPTH,), in_specs=[spec], out_specs=spec),
        compiler_params=_params(("parallel",)),
    )(me, padded)


def _z_rows_from_windows(win):
    over = WIN_ROWS - WIN_STRIDE
    pieces = [(0, win[0][0:WIN_STRIDE])]
    for d in range(1, N_DEV):
        base = WIN_STRIDE * d
        pieces.append((base, win[d - 1][WIN_STRIDE:WIN_ROWS] + win[d][0:over]))
        pieces.append((base + over, win[d][over:WIN_STRIDE]))
    pieces.append((WIN_STRIDE * N_DEV, win[N_DEV - 1][WIN_STRIDE:WIN_ROWS]))

    def rows(a, b):
        out = []
        for start, arr in pieces:
            lo, hi = max(a, start), min(b, start + arr.shape[0])
            if lo < hi:
                out.append(arr[lo - start:hi - start])
        return out

    pad = jnp.zeros((NZ - IN_COLS, win.shape[-1]), win.dtype)
    return jnp.concatenate(rows(Z_TURN, IN_COLS) + rows(0, Z_TURN) + [pad], axis=0)


def _in_rows_from_z(wt):
    return jnp.concatenate([wt[Z_Q:Z_Q + 1536], wt[Z_F:Z_F + 8], wt[Z_PC:Z_PC + 1024], wt[Z_G:Z_G + 3072]], axis=0)


def _pad_rows(v, rows=8):
    return jnp.pad(v, ((0, rows - v.shape[0]), (0, 0)))


def _layer_fwd(l, x, wts, gvec, mod):
    tag = f"l{l}"
    h = _prenorm_fwd(x, gvec, mod, 0, 0, 1, f"prenorm_mix_{tag}")
    z = _matmul(h, wts["w_in_t"], "nt", f"in_proj_{tag}", tn=1152)
    qa, ka, va, kat = _attn_prep(z, wts["b_f"], f"attn_prep_{tag}")
    qa = wts["arrive"](qa)
    o, lse = _attn_fwd(qa, ka, va, f"attn_{tag}")
    br_b = _pool_fwd(z, wts["wp_bd"], wts["pool_scale"], f"pool_{tag}")
    br_c = _conv_fwd(z, wts["conv_w"], f"conv_{tag}")
    pa = _matmul(o, wts["wa"], "nn", f"proj_a_{tag}", out_dtype=BF16)
    pb = _matmul(br_b, wts["wb"], "nn", f"proj_b_{tag}", out_dtype=BF16)
    gates = [(z, Z_G + k * D) for k in range(3)]
    pc, merged = _matmul(br_c, wts["wc"], "nn", f"proj_c_merge_{tag}", tm=512, tn=512,
                         extra=gates + [(pa, 0), (pb, 0)], epilogue=_merge_epilogue, out_dtypes=(BF16, BF16))
    y = _matmul(merged, wts["w_out"], "nn", f"out_proj_{tag}")
    x1 = _postnorm_fwd(x, y, gvec, mod, 1, 2, f"postnorm_mix_{tag}")
    h2 = _prenorm_fwd(x1, gvec, mod, 2, 3, 4, f"prenorm_ff_{tag}")
    a, r = _matmul(h2, wts["w_ff1"], "nn", f"ff1_{tag}", b_col_shards=True, epilogue=_relu2_epilogue,
                   out_dtypes=(BF16, BF16))
    y2 = _matmul(r, wts["w_ff2"], "nn", f"ff2_{tag}", tk=1024)
    x2 = _postnorm_fwd(x1, y2, gvec, mod, 3, 5, f"postnorm_ff_{tag}")
    saved = dict(x=x, h=h, z=z, qa=qa, ka=ka, va=va, kat=kat, o=o, lse=lse, br_b=br_b, br_c=br_c, pa=pa, pb=pb, pc=pc,
                 merged=merged, y=y, x1=x1, h2=h2, a=a, r=r, y2=y2)
    return x2, saved


def _ffn_bwd(l, dx2, sv, wts, gvec, mod, midpoint):
    tag = f"l{l}"
    dy2, red_post_ff = _postnorm_bwd(sv["y2"], gvec, mod, dx2, 3, 5, f"postnorm_ff_bwd_{tag}")
    dy2 = midpoint(dy2)
    da = _matmul(dy2, wts["w_ff2"], "nt", f"ff2_dx_{tag}", extra=[(sv["a"], 0)], epilogue=_relu2_bwd_epilogue,
                 out_dtypes=(BF16,))[0]
    d_w_ff2 = _matmul(sv["r"], dy2, "tn", f"ff2_dw_{tag}")
    dh2 = _matmul(da, wts["w_ff1"], "nt", f"ff1_dx_{tag}", b_col_shards=True)
    d_w_ff1 = _matmul(sv["h2"], da, "tn", f"ff1_dw_{tag}", out_col_shards=True)
    dx1, red_pre_ff = _prenorm_bwd(sv["x1"], gvec, mod, dh2, dx2, 2, 4, f"prenorm_ff_bwd_{tag}")
    return dx1, [d_w_ff1, d_w_ff2.reshape(N_DEV, D_FF // N_DEV, D)], (red_pre_ff, red_post_ff)


def _mixer_bwd(l, dx1, sv, wts, gvec, mod, ffn_reds, midpoint):
    tag = f"l{l}"
    red_pre_ff, red_post_ff = ffn_reds
    dy, red_post_mix = _postnorm_bwd(sv["y"], gvec, mod, dx1, 1, 2, f"postnorm_mix_bwd_{tag}")
    gates = [(sv["z"], Z_G + k * D) for k in range(3)]
    dpa, dpb, dpc, *dgl = _matmul(dy, wts["w_out"], "nt", f"out_proj_dx_{tag}", tm=512, tn=512,
                                  extra=gates + [(sv["pa"], 0), (sv["pb"], 0), (sv["pc"], 0)],
                                  epilogue=_merge_bwd_epilogue, out_dtypes=(BF16,) * 6)
    d_w_out = _matmul(sv["merged"], dy, "tn", f"out_proj_dw_{tag}")
    dpa = midpoint(dpa)
    do = _matmul(dpa, wts["wa"], "nt", f"proj_a_dx_{tag}")
    dbr_b = _matmul(dpb, wts["wb"], "nt", f"proj_b_dx_{tag}")
    dbr_c = _matmul(dpc, wts["wc"], "nt", f"proj_c_dx_{tag}")
    d_wa = _matmul(sv["o"], dpa, "tn", f"proj_a_dw_{tag}")
    d_wb = _matmul(sv["br_b"], dpb, "tn", f"proj_b_dw_{tag}")
    d_wc = _matmul(sv["br_c"], dpc, "tn", f"proj_c_dw_{tag}")
    d_w_branch = jnp.concatenate([d_wa, d_wb, d_wc], axis=0)

    dpu, d_wp_bd, red_pool = _pool_bwd(sv["z"], wts["wp_bd"], wts["pool_scale"], dbr_b, f"pool_bwd_{tag}")
    dconv, red_conv = _conv_bwd(sv["z"], wts["conv_w"], dbr_c, f"conv_bwd_{tag}")
    qa2, doa = _attn_bwd_prep(sv["qa"], sv["o"], sv["lse"], do, f"attn_bwd_prep_{tag}")
    dqt, dka, dva = _attn_bwd(qa2, sv["ka"], sv["va"], sv["kat"], doa, f"attn_bwd_{tag}")
    dq, dk, dv, dfl, red_f = _attn_bwd_post(sv["z"], wts["b_f"], dqt, dka, dva, f"attn_bwd_post_{tag}")
    dz = jnp.concatenate([dpu, dconv, *dgl, dq, dk, dv, dfl], axis=1)
    dh = _matmul(dz, wts["w_in_t"], "nn", f"in_proj_dx_{tag}", tm=1024, tk=1920)
    d_w_in_t = _matmul(dz, sv["h"], "tn", f"in_proj_dw_{tag}", tm=1152)
    dx0, red_pre_mix = _prenorm_bwd(sv["x"], gvec, mod, dh, dx1, 0, 1, f"prenorm_mix_bwd_{tag}")

    rows = D // N_DEV
    big = [_in_rows_from_z(d_w_in_t).reshape(N_DEV, IN_SHARD, D), d_w_branch.reshape(N_DEV, rows, D),
           d_w_out.reshape(N_DEV, rows, D)]
    d_w_pool = jnp.stack([d_wp_bd[64 * g:64 * (g + 1), 64 * g:64 * (g + 1)] for g in range(4)])
    small = dict(
        mod=jnp.stack([red_pre_mix[0], red_pre_mix[1], red_post_mix[0], red_pre_ff[0], red_pre_ff[1], red_post_ff[0]]),
        g_mix_pre=red_pre_mix[2], g_mix_post=red_post_mix[1], g_ff_pre=red_pre_ff[2], g_ff_post=red_post_ff[1],
        b_f=red_f[0, 0:8], w_pool=d_w_pool, pool_scale=red_pool[0], conv_w=red_conv[0:3])
    return dx0, big, small


SMALL_KEYS = ["mod", "g_mix_pre", "g_mix_post", "g_ff_pre", "g_ff_post", "b_f", "w_pool", "pool_scale", "conv_w"]
SMALL_SHAPES = [(DEPTH, 6 * D), (DEPTH, D), (DEPTH, D), (DEPTH, D), (DEPTH, D), (DEPTH, 8), (DEPTH, 4, 64, 64),
                (DEPTH, POOL_W), (DEPTH, 3, CONV_W)]


def kernel(x, c, w_ada, b_ada, g_mix_pre, g_mix_post, g_ff_pre, g_ff_post, w_in, b_f, w_pool, pool_scale, conv_w, w_branch, w_out, w_ff1, w_ff2, loss_target, m_w_ada, m_b_ada, m_g_mix_pre, m_g_mix_post, m_g_ff_pre, m_g_ff_post, m_w_in, m_b_f, m_w_pool, m_pool_scale, m_conv_w, m_w_branch, m_w_out, m_w_ff1, m_w_ff2, v_w_ada, v_b_ada, v_g_mix_pre, v_g_mix_post, v_g_ff_pre, v_g_ff_post, v_w_in, v_b_f, v_w_pool, v_pool_scale, v_conv_w, v_w_branch, v_w_out, v_w_ff1, v_w_ff2):
    ix, iy, ic = lax.axis_index("x"), lax.axis_index("y"), lax.axis_index("c")
    me = 4 * ix + 2 * iy + ic
    route = jnp.stack([ic, 2 * (1 - ix) + iy, 2 * ix + (1 - iy), 2 * (1 - ix) + (1 - iy)]).astype(jnp.int32)
    place = jnp.stack([me, 2 * ix + iy]).astype(jnp.int32)
    wt_in, mt_in, vt_in = (jnp.transpose(a, (0, 2, 1)) for a in (w_in, m_w_in, v_w_in))

    c_all = _all_gather([_pad_rows(c)], "gather_c")[0][:, 0, :]
    c_pad = _pad_rows(c_all, ADA_ROWS)
    b_cols = lax.dynamic_slice_in_dim(b_ada, me * ADA_COLS, ADA_COLS, axis=1)
    b_cols = jnp.broadcast_to(b_cols[:, None, :], (DEPTH, 8, ADA_COLS))
    mod_part = _ada_fwd(c_pad, w_ada, b_cols, "ada_fwd")
    mod_all = _all_gather([mod_part.reshape(DEPTH * ADA_ROWS, ADA_COLS)], "gather_mod")[0]
    mod_all = mod_all.reshape(N_DEV, DEPTH, ADA_ROWS, ADA_COLS)
    mod_mine = lax.dynamic_index_in_dim(mod_all, me, axis=2, keepdims=False)
    mod_mine = jnp.transpose(mod_mine, (1, 0, 2)).reshape(DEPTH, 6, D)

    cw_cols = CONV_W // N_DEV
    cw_send = jnp.pad(conv_w.reshape(DEPTH * 3, cw_cols), ((0, 8 - DEPTH * 3), (0, LANE - cw_cols)))
    win_in = _window(wt_in, place[0:1], "w_in_window")
    send = [[w[l].astype(BF16) for w in (win_in, w_branch, w_out, w_ff1, w_ff2)] for l in range(DEPTH)]
    first = _all_gather(send[0][:1], "gather_weights_l0_in", sequencer_id=1, after=mod_all)
    rest = _all_gather(send[0][1:] + [cw_send], "gather_weights_l0_rest", sequencer_id=2, after=first[0])
    first1 = _all_gather(send[1][:1], "gather_weights_l1_in", sequencer_id=3, after=first[0])
    rest1 = _all_gather(send[1][1:], "gather_weights_l1_rest", sequencer_id=12, after=first[0])
    gathered = [first + rest[:4], first1 + rest1]
    cw_all = rest[4][:, :DEPTH * 3, :cw_cols].reshape(N_DEV, DEPTH, 3, cw_cols)

    def first_operands(l, p_in):
        wp_bd = jnp.zeros((POOL_W, POOL_W), F32)
        for g in range(4):
            wp_bd = wp_bd.at[64 * g:64 * (g + 1), 64 * g:64 * (g + 1)].set(w_pool[l, g])
        return dict(w_in_t=_z_rows_from_windows(p_in), wp_bd=wp_bd.astype(BF16),
                    pool_scale=_pad_rows(pool_scale[l][None, :]), b_f=_pad_rows(jnp.pad(b_f[l], (0, LANE - 8))[None, :]))

    def rest_operands(l, rest):
        p_br, p_out, p_ff1, p_ff2 = rest
        w_br_full = p_br.reshape(D, D)
        cw_full = jnp.transpose(cw_all[:, l], (1, 0, 2)).reshape(3, CONV_W)
        return dict(wa=w_br_full[0:A_WIDTH], wb=w_br_full[A_WIDTH:A_WIDTH + POOL_W], wc=w_br_full[A_WIDTH + POOL_W:],
                    w_out=p_out.reshape(D, D), w_ff1=p_ff1, w_ff2=p_ff2.reshape(D_FF, D), conv_w=_pad_rows(cw_full))

    xs = x[0]
    saved, layers = [], []
    for l in range(DEPTH):
        p_in, rest = gathered[l][0], gathered[l][1:5]
        if l > 0:
            xs, p_in = lax.optimization_barrier((xs, p_in))
        wts = first_operands(l, p_in)

        def arrive(t, l=l, rest=rest, wts=wts):
            if l > 0:
                t, rest = lax.optimization_barrier((t, rest))
            wts.update(rest_operands(l, rest))
            return t

        wts["arrive"] = arrive
        gvec = _pad_rows(jnp.stack([g_mix_pre[l], g_mix_post[l], g_ff_pre[l], g_ff_post[l]]))
        layers.append((wts, gvec, _pad_rows(mod_mine[l])))
        xs, sv = _layer_fwd(l, xs, *layers[l])
        saved.append(sv)
    dx, loss_part = _loss_head(xs, loss_target[0], "loss_head")
    small_grads = [None] * DEPTH
    mine, sibs, landed = ({} for _ in range(3))
    seq_id = iter(range(4, 4 + 4 * DEPTH))
    last = [gathered[DEPTH - 1][1]]

    def start(group, grads):
        mine[group] = grads
        sibs[group] = _sibling_exchange(grads, f"rs_sibling_{group}", sequencer_id=next(seq_id), after=last[0])
        last[0] = sibs[group][0]

    def finish(group, later):
        later, (grads, sib) = lax.optimization_barrier((later, (mine[group], sibs[group])))
        sends = [_pair_sums(g, p, route, f"rs_pair_sums_{group}_{k}") for k, (g, p) in enumerate(zip(grads, sib))]
        later, sends = lax.optimization_barrier((later, sends))
        landed[group] = _chip_exchange(sends, f"rs_chips_{group}", sequencer_id=next(seq_id), after=last[0])
        last[0] = landed[group][0]
        return later

    pending = None
    for l in reversed(range(DEPTH)):
        hook = (lambda da: da) if pending is None else functools.partial(finish, pending)
        dx, ffn_grads, ffn_reds = _ffn_bwd(l, dx, saved[l], *layers[l], hook)
        start(f"ffn_l{l}", ffn_grads)
        dx, mix_grads, small_grads[l] = _mixer_bwd(l, dx, saved[l], *layers[l], ffn_reds,
                                                   functools.partial(finish, f"ffn_l{l}"))
        start(f"mix_l{l}", mix_grads)
        pending = f"mix_l{l}"
    grad_x = dx[None]

    big_w = [wt_in, w_branch, w_out, w_ff1, w_ff2]
    big_m = [mt_in, m_w_branch, m_w_out, m_w_ff1, m_w_ff2]
    big_v = [vt_in, v_w_branch, v_w_out, v_w_ff1, v_w_ff2]
    where = [("mix", 0), ("mix", 1), ("mix", 2), ("ffn", 0), ("ffn", 1)]

    def reduce_and_update(k):
        group, at = where[k]
        return _reduce_adamw([mine[f"{group}_l{l}"][at] for l in range(DEPTH)],
                             [sibs[f"{group}_l{l}"][at] for l in range(DEPTH)],
                             [landed[f"{group}_l{l}"][at] for l in range(DEPTH)], place, big_w[k], big_m[k], big_v[k],
                             f"rs_sum_adamw_{k}")

    big_res = {k: list(reduce_and_update(k)) for k in (3, 4)}
    big_res[3][0] = finish(pending, big_res[3][0])

    small = {k: jnp.stack([small_grads[l][k] for l in range(DEPTH)]) for k in SMALL_KEYS}
    payload = _pack([small[k] for k in SMALL_KEYS] + [loss_part[0:1, 0:1]], 8, F32)
    small_all = _all_gather([payload], "gather_small")[0]
    dmod_all = small_all[:, 0:DEPTH * 6, :].reshape(N_DEV, DEPTH, 6 * D)
    summed = _unpack(_sum_slabs(small_all, "sum_small").reshape(-1), SMALL_SHAPES + [(1, 1)])
    sg = dict(zip(SMALL_KEYS, summed))
    loss = summed[-1][0, 0]
    dmod_cols = lax.dynamic_slice_in_dim(dmod_all, me * ADA_COLS, ADA_COLS, axis=2)
    dmod_cols = jnp.pad(jnp.transpose(dmod_cols, (1, 0, 2)), ((0, 0), (0, ADA_ROWS - N_DEV), (0, 0)))
    g_w_ada = _ada_bwd(c_pad, dmod_cols, "ada_bwd")
    g_conv_w = lax.dynamic_slice_in_dim(sg["conv_w"], me * (CONV_W // N_DEV), CONV_W // N_DEV, axis=2)

    ada_out = [g_w_ada] + list(_adamw(w_ada, g_w_ada, m_w_ada, v_w_ada, "adamw_ada"))
    rest_w = [b_ada, g_mix_pre, g_mix_post, g_ff_pre, g_ff_post, b_f, w_pool, pool_scale, conv_w]
    rest_m = [m_b_ada, m_g_mix_pre, m_g_mix_post, m_g_ff_pre, m_g_ff_post, m_b_f, m_w_pool, m_pool_scale, m_conv_w]
    rest_v = [v_b_ada, v_g_mix_pre, v_g_mix_post, v_g_ff_pre, v_g_ff_post, v_b_f, v_w_pool, v_pool_scale, v_conv_w]
    rest_g = [sg["mod"], sg["g_mix_pre"], sg["g_mix_post"], sg["g_ff_pre"], sg["g_ff_post"], sg["b_f"],
              sg["w_pool"], sg["pool_scale"], g_conv_w]
    rest_shapes = [a.shape for a in rest_w]
    upd = _adamw(_pack(rest_w, 8, F32)[None], _pack(rest_g, 8, F32)[None], _pack(rest_m, 8, F32)[None],
                 _pack(rest_v, 8, F32)[None], "adamw_rest")
    rest_out = [rest_g] + [_unpack(arr.reshape(-1), rest_shapes) for arr in upd]
    rest_out = [[ada_out[which]] + rest_out[which] for which in range(4)]

    landed[pending], rest_out = lax.optimization_barrier((landed[pending], rest_out))
    big_res.update({k: reduce_and_update(k) for k in (0, 1, 2)})
    big_out = [[jnp.transpose(big_res[k][which], (0, 2, 1)) if k == 0 else big_res[k][which] for k in range(5)]
               for which in range(4)]

    def ordered(k):
        r, b = rest_out[k], big_out[k]
        return [r[0], r[1], r[2], r[3], r[4], r[5], b[0], r[6], r[7], r[8], r[9], b[1], b[2], b[3], b[4]]

    return (loss, grad_x, *ordered(0), *ordered(1), *ordered(2), *ordered(3))
```

```python
import functools

import jax
import jax.numpy as jnp
from jax import lax
from jax.experimental import pallas as pl
from jax.experimental.pallas import tpu as pltpu
from jax.experimental.pallas import tpu_sc as plsc

F32 = jnp.float32
BF16 = jnp.bfloat16
GRAD_DTYPE = BF16

N_DEV = 8
D = 1024
S = 2048
DEPTH = 2
D_FF = 4 * D
A_WIDTH = 512
HEAD_DIM = 64
N_PAIR = 4
POOL_W = 256
CONV_W = 256
IN_COLS = 5640
ADA_COLS = 6 * D // N_DEV
IN_SHARD = IN_COLS // N_DEV
RMS_EPS = 1e-6
NEG_INF = -1e30
ATT_SCALE = HEAD_DIM ** -0.5

NZ = 5760
Z_PC = 0
Z_G = 1024
Z_Q = 4096
Z_K = 4608
Z_V = 5120
Z_F = 5632

LR, B1, B2, EPS, WD, STEP = 0.001, 0.9, 0.999, 1e-08, 0.01, 10

LANE = 128
VMEM_LIMIT_BYTES = 48 * 1024 * 1024
TS = 512
TQ = 256
TQ_FWD = 512
HEADS_PER_STEP = 8
HEADS_PER_STEP_FWD = 8


def _params(sem=None):
    return pltpu.CompilerParams(dimension_semantics=sem, vmem_limit_bytes=VMEM_LIMIT_BYTES)


def _pick(n, target):
    best = None
    for t in range(LANE, min(n, target) + 1, LANE):
        if n % t == 0:
            best = t
    return n if best is None else best


def _matmul(a, b, mode, name, out_dtype=F32, tm=2048, tn=1024, tk=2048, b_col_shards=False, out_col_shards=False,
            extra=(), epilogue=None, out_dtypes=None):
    if b_col_shards:
        shards, b_rows, shard_cols = b.shape
        b_shape = (b_rows, shards * shard_cols)
    else:
        b_shape = b.shape
    if mode == "nn":
        (m, k), (k2, n) = a.shape, b_shape
    elif mode == "nt":
        (m, k), (n, k2) = a.shape, b_shape
    else:
        (k, m), (k2, n) = a.shape, b_shape
    assert k == k2, (a.shape, b.shape, mode)
    tm, tn, tk = _pick(m, tm), _pick(n, tn), _pick(k, tk)
    if b_col_shards and mode == "nn":
        tn = shard_cols
    per_step = 1
    if b_col_shards and mode == "nt":
        per_step = max(1, min(tk, 1024) // shard_cols)
        tk = per_step * shard_cols
    if out_col_shards:
        tn = n // N_DEV
    nk = k // tk
    if mode == "nn":
        a_spec = pl.BlockSpec((tm, tk), lambda i, j, kk: (i, kk))
        b_spec = (pl.BlockSpec((None, tk, tn), lambda i, j, kk: (j, kk, 0)) if b_col_shards else
                  pl.BlockSpec((tk, tn), lambda i, j, kk: (kk, j)))
        dims = (((1,), (0,)), ((), ()))
    elif mode == "nt":
        a_spec = pl.BlockSpec((tm, tk), lambda i, j, kk: (i, kk))
        b_spec = (pl.BlockSpec((per_step, tn, shard_cols), lambda i, j, kk: (kk, j, 0)) if b_col_shards else
                  pl.BlockSpec((tn, tk), lambda i, j, kk: (j, kk)))
        dims = (((1,), (1,)), ((), ()))
    else:
        assert not b_col_shards
        a_spec = pl.BlockSpec((tk, tm), lambda i, j, kk: (kk, i))
        b_spec = pl.BlockSpec((tk, tn), lambda i, j, kk: (kk, j))
        dims = (((0,), (0,)), ((), ()))
    if out_col_shards:
        out_shape = jax.ShapeDtypeStruct((N_DEV, m, tn), out_dtype)
        out_spec = pl.BlockSpec((None, tm, tn), lambda i, j, kk: (j, i, 0))
    else:
        out_shape = jax.ShapeDtypeStruct((m, n), out_dtype)
        out_spec = pl.BlockSpec((tm, tn), lambda i, j, kk: (i, j))

    n_extra = len(extra)
    extra_specs = [pl.BlockSpec((tm, tn), lambda i, j, kk, off=off: (i, j + off // tn)) for _, off in extra]
    if epilogue is not None:
        assert not out_col_shards and all(off % tn == 0 for _, off in extra)
        out_shape = [jax.ShapeDtypeStruct((m, n), dt) for dt in out_dtypes]
        out_spec = [pl.BlockSpec((tm, tn), lambda i, j, kk: (i, j)) for _ in out_dtypes]

    def product(a_ref, b_ref):
        if b_col_shards and mode == "nt":
            b_tile = jnp.concatenate([b_ref[s] for s in range(per_step)], axis=1) if per_step > 1 else b_ref[0]
        else:
            b_tile = b_ref[...]
        return lax.dot_general(a_ref[...].astype(BF16), b_tile.astype(BF16), dims, preferred_element_type=F32)

    def write(acc, extra_refs, o_refs):
        if epilogue is None:
            o_refs[0][...] = acc.astype(out_dtype)
        else:
            for o_ref, tile in zip(o_refs, epilogue(acc, *[r[...] for r in extra_refs])):
                o_ref[...] = tile.astype(o_ref.dtype)

    def body_one_pass(a_ref, b_ref, *refs):
        write(product(a_ref, b_ref), refs[:n_extra], refs[n_extra:])

    def body(a_ref, b_ref, *refs):
        acc_ref = refs[-1]
        kk = pl.program_id(2)

        @pl.when(kk == 0)
        def _():
            acc_ref[...] = product(a_ref, b_ref)

        @pl.when(kk > 0)
        def _():
            acc_ref[...] += product(a_ref, b_ref)

        @pl.when(kk == nk - 1)
        def _():
            write(acc_ref[...], refs[:n_extra], refs[n_extra:-1])

    return pl.pallas_call(
        body_one_pass if nk == 1 else body, name=name,
        out_shape=out_shape,
        grid=(m // tm, n // tn, nk),
        in_specs=[a_spec, b_spec] + extra_specs,
        out_specs=out_spec,
        scratch_shapes=[] if nk == 1 else [pltpu.VMEM((tm, tn), F32)],
        compiler_params=_params(("parallel", "parallel", "arbitrary")),
    )(a, b, *[x for x, _ in extra])


def _row_spec(width=D, col=0):
    return pl.BlockSpec((TS, width), lambda i: (i, col))


def _vec_spec(rows=8, width=D):
    return pl.BlockSpec((rows, width), lambda i: (0, 0))


def _rms(x):
    return lax.rsqrt(jnp.mean(x * x, axis=-1, keepdims=True) + RMS_EPS)


def _prenorm_fwd(x, gvec, mod, g_row, shift_row, scale_row, name):
    def body(x_ref, g_ref, mod_ref, h_ref):
        xv = x_ref[...]
        y = xv * _rms(xv) * g_ref[g_row:g_row + 1, :]
        h = y * (1.0 + mod_ref[scale_row:scale_row + 1, :]) + mod_ref[shift_row:shift_row + 1, :]
        h_ref[...] = h.astype(BF16)

    return pl.pallas_call(
        body, name=name, out_shape=jax.ShapeDtypeStruct((S, D), BF16), grid=(S // TS,),
        in_specs=[_row_spec(), _vec_spec(), _vec_spec()], out_specs=_row_spec(),
        compiler_params=_params(("parallel",)),
    )(x, gvec, mod)


def _prenorm_bwd(x, gvec, mod, dh, dres, g_row, scale_row, name):
    def body(x_ref, g_ref, mod_ref, dh_ref, dres_ref, dx_ref, red_ref):
        i = pl.program_id(0)

        @pl.when(i == 0)
        def _():
            red_ref[...] = jnp.zeros_like(red_ref)

        xv = x_ref[...]
        g = g_ref[g_row:g_row + 1, :]
        r = _rms(xv)
        n = xv * r
        yg = n * g
        dhv = dh_ref[...]
        dyg = dhv * (1.0 + mod_ref[scale_row:scale_row + 1, :])
        dn = dyg * g
        dx = r * (dn - n * jnp.mean(dn * n, axis=-1, keepdims=True))
        dx_ref[...] = dres_ref[...] + dx
        red_ref[0:1, :] += jnp.sum(dhv, axis=0, keepdims=True)
        red_ref[1:2, :] += jnp.sum(dhv * yg, axis=0, keepdims=True)
        red_ref[2:3, :] += jnp.sum(dyg * n, axis=0, keepdims=True)

    return pl.pallas_call(
        body, name=name,
        out_shape=(jax.ShapeDtypeStruct((S, D), F32), jax.ShapeDtypeStruct((8, D), F32)),
        grid=(S // TS,),
        in_specs=[_row_spec(), _vec_spec(), _vec_spec(), _row_spec(), _row_spec()],
        out_specs=(_row_spec(), _vec_spec()),
        compiler_params=_params(("arbitrary",)),
    )(x, gvec, mod, dh, dres)


def _postnorm_fwd(x, y, gvec, mod, g_row, gate_row, name):
    def body(x_ref, y_ref, g_ref, mod_ref, o_ref):
        yv = y_ref[...]
        yn = yv * _rms(yv) * g_ref[g_row:g_row + 1, :]
        o_ref[...] = x_ref[...] + mod_ref[gate_row:gate_row + 1, :] * yn

    return pl.pallas_call(
        body, name=name, out_shape=jax.ShapeDtypeStruct((S, D), F32), grid=(S // TS,),
        in_specs=[_row_spec(), _row_spec(), _vec_spec(), _vec_spec()], out_specs=_row_spec(),
        compiler_params=_params(("parallel",)),
    )(x, y, gvec, mod)


def _postnorm_bwd(y, gvec, mod, dxo, g_row, gate_row, name):
    def body(y_ref, g_ref, mod_ref, dxo_ref, dy_ref, red_ref):
        i = pl.program_id(0)

        @pl.when(i == 0)
        def _():
            red_ref[...] = jnp.zeros_like(red_ref)

        yv = y_ref[...]
        g = g_ref[g_row:g_row + 1, :]
        r = _rms(yv)
        n = yv * r
        dxo = dxo_ref[...]
        dyn = dxo * mod_ref[gate_row:gate_row + 1, :]
        dn = dyn * g
        dy = r * (dn - n * jnp.mean(dn * n, axis=-1, keepdims=True))
        dy_ref[...] = dy.astype(BF16)
        red_ref[0:1, :] += jnp.sum(dxo * (n * g), axis=0, keepdims=True)
        red_ref[1:2, :] += jnp.sum(dyn * n, axis=0, keepdims=True)

    return pl.pallas_call(
        body, name=name,
        out_shape=(jax.ShapeDtypeStruct((S, D), BF16), jax.ShapeDtypeStruct((8, D), F32)),
        grid=(S // TS,),
        in_specs=[_row_spec(), _vec_spec(), _vec_spec(), _row_spec()],
        out_specs=(_row_spec(), _vec_spec()),
        compiler_params=_params(("arbitrary",)),
    )(y, gvec, mod, dxo)


def _loss_head(xf, target, name):
    def body(x_ref, t_ref, dx_ref, loss_ref):
        i = pl.program_id(0)

        @pl.when(i == 0)
        def _():
            loss_ref[...] = jnp.zeros_like(loss_ref)

        e = x_ref[...] - t_ref[...]
        dx_ref[...] = e / float(D)
        per_tok = jnp.mean(e * e, axis=-1, keepdims=True)
        loss_ref[0:1, 0:1] += 0.5 * jnp.sum(per_tok, axis=0, keepdims=True)

    return pl.pallas_call(
        body, name=name,
        out_shape=(jax.ShapeDtypeStruct((S, D), F32), jax.ShapeDtypeStruct((8, LANE), F32)),
        grid=(S // TS,),
        in_specs=[_row_spec(), _row_spec()],
        out_specs=(_row_spec(), pl.BlockSpec((8, LANE), lambda i: (0, 0))),
        compiler_params=_params(("arbitrary",)),
    )(xf, target)


def _relu2_epilogue(a):
    t = jnp.maximum(a, 0.0)
    return a, t * t


def _relu2_bwd_epilogue(dr, a):
    return (dr * (2.0 * jnp.maximum(a, 0.0)),)


def _merge_epilogue(pc, g0, g1, g2, pa, pb):
    return pc, jax.nn.sigmoid(g0) * pa + jax.nn.sigmoid(g1) * pb + jax.nn.sigmoid(g2) * pc


def _merge_bwd_epilogue(dm, g0, g1, g2, pa, pb, pc):
    sg = [jax.nn.sigmoid(g) for g in (g0, g1, g2)]
    return tuple(dm * s for s in sg) + tuple(dm * p * (s * (1.0 - s)) for p, s in zip((pa, pb, pc), sg))


def _shift_down(x, k, row):
    return jnp.where(row >= k, pltpu.roll(x, k, axis=0), 0.0)


def _shift_up(x, k, row):
    n = x.shape[0]
    return jnp.where(row < n - k, pltpu.roll(x, n - k, axis=0), 0.0)


def _cumsum_rows(x, row, reverse=False):
    shift = _shift_up if reverse else _shift_down
    k = 1
    while k < x.shape[0]:
        x = x + shift(x, k, row)
        k *= 2
    return x


def _full_spec(shape, idx=(0, 0)):
    return pl.BlockSpec(shape, lambda i: idx)


def _pool_window_select(lane, a2, a4, a8, a16):
    return jnp.where(lane < 64, a2, jnp.where(lane < 128, a4, jnp.where(lane < 192, a8, a16)))


def _pool_p(u, row, lane):
    t2 = u + _shift_down(u, 1, row)
    t4 = t2 + _shift_down(t2, 2, row)
    t8 = t4 + _shift_down(t4, 4, row)
    t16 = t8 + _shift_down(t8, 8, row)
    tw = _pool_window_select(lane, t2, t4, t8, t16)
    cnt = jnp.minimum((row + 1).astype(F32), _pool_window_select(lane, 2.0, 4.0, 8.0, 16.0))
    return tw / cnt - u, cnt


def _pool_fwd(z, wp_bd, pscale, name):
    def body(u_ref, w_ref, s_ref, o_ref):
        row = lax.broadcasted_iota(jnp.int32, (S, POOL_W), 0)
        lane = lax.broadcasted_iota(jnp.int32, (S, POOL_W), 1)
        p, _ = _pool_p(u_ref[...], row, lane)
        y = jnp.dot(p.astype(BF16), w_ref[...], preferred_element_type=F32)
        o_ref[...] = y * s_ref[0:1, :]

    return pl.pallas_call(
        body, name=name, out_shape=jax.ShapeDtypeStruct((S, POOL_W), F32), grid=(1,),
        in_specs=[_full_spec((S, POOL_W), (0, Z_PC // POOL_W)), _full_spec((POOL_W, POOL_W)), _full_spec((8, POOL_W))],
        out_specs=_full_spec((S, POOL_W)),
        compiler_params=_params(("arbitrary",)),
    )(z, wp_bd, pscale)


def _pool_bwd(z, wp_bd, pscale, dbr, name):
    def body(u_ref, w_ref, s_ref, dbr_ref, du_ref, dw_ref, red_ref):
        row = lax.broadcasted_iota(jnp.int32, (S, POOL_W), 0)
        lane = lax.broadcasted_iota(jnp.int32, (S, POOL_W), 1)
        p, cnt = _pool_p(u_ref[...], row, lane)
        pb = p.astype(BF16)
        y = jnp.dot(pb, w_ref[...], preferred_element_type=F32)
        dbr = dbr_ref[...]
        red_ref[...] = jnp.zeros_like(red_ref)
        red_ref[0:1, :] = jnp.sum(dbr * y, axis=0, keepdims=True)
        dy = (dbr * s_ref[0:1, :]).astype(BF16)
        dw_ref[...] = lax.dot_general(pb, dy, (((0,), (0,)), ((), ())), preferred_element_type=F32)
        dp = lax.dot_general(dy, w_ref[...], (((1,), (1,)), ((), ())), preferred_element_type=F32)
        g = dp / cnt
        a2 = g + _shift_up(g, 1, row)
        a4 = a2 + _shift_up(a2, 2, row)
        a8 = a4 + _shift_up(a4, 4, row)
        a16 = a8 + _shift_up(a8, 8, row)
        du_ref[...] = (_pool_window_select(lane, a2, a4, a8, a16) - dp).astype(BF16)

    return pl.pallas_call(
        body, name=name,
        out_shape=(jax.ShapeDtypeStruct((S, POOL_W), BF16), jax.ShapeDtypeStruct((POOL_W, POOL_W), F32),
                   jax.ShapeDtypeStruct((8, POOL_W), F32)),
        grid=(1,),
        in_specs=[_full_spec((S, POOL_W), (0, Z_PC // POOL_W)), _full_spec((POOL_W, POOL_W)), _full_spec((8, POOL_W)),
                  _full_spec((S, POOL_W))],
        out_specs=(_full_spec((S, POOL_W)), _full_spec((POOL_W, POOL_W)), _full_spec((8, POOL_W))),
        compiler_params=_params(("arbitrary",)),
    )(z, wp_bd, pscale, dbr)


def _conv_specs():
    base = Z_PC // CONV_W
    return [_full_spec((S, CONV_W), (0, base + 1)), _full_spec((S, CONV_W), (0, base + 2)),
            _full_spec((S, CONV_W), (0, base + 3)), _full_spec((8, CONV_W))]


def _conv_fwd(z, cw, name):
    def body(h_ref, b_ref, c_ref, w_ref, o_ref):
        row = lax.broadcasted_iota(jnp.int32, (S, CONV_W), 0)
        u = c_ref[...] * h_ref[...]
        y = (w_ref[0:1, :] * _shift_down(u, 2, row) + w_ref[1:2, :] * _shift_down(u, 1, row) + w_ref[2:3, :] * u)
        o_ref[...] = b_ref[...] * y

    return pl.pallas_call(
        body, name=name, out_shape=jax.ShapeDtypeStruct((S, CONV_W), F32), grid=(1,),
        in_specs=_conv_specs(), out_specs=_full_spec((S, CONV_W)),
        compiler_params=_params(("arbitrary",)),
    )(z, z, z, cw)


def _conv_bwd(z, cw, dbr, name):
    def body(h_ref, b_ref, c_ref, w_ref, dbr_ref, d_ref, red_ref):
        row = lax.broadcasted_iota(jnp.int32, (S, CONV_W), 0)
        h, cg = h_ref[...], c_ref[...]
        u = cg * h
        u1 = _shift_down(u, 1, row)
        u2 = _shift_down(u, 2, row)
        y = w_ref[0:1, :] * u2 + w_ref[1:2, :] * u1 + w_ref[2:3, :] * u
        dbr = dbr_ref[...]
        dy = dbr * b_ref[...]
        du = w_ref[2:3, :] * dy + w_ref[1:2, :] * _shift_up(dy, 1, row) + w_ref[0:1, :] * _shift_up(dy, 2, row)
        d_ref[:, 0:CONV_W] = (du * cg).astype(BF16)
        d_ref[:, CONV_W:2 * CONV_W] = (dbr * y).astype(BF16)
        d_ref[:, 2 * CONV_W:3 * CONV_W] = (du * h).astype(BF16)
        red_ref[...] = jnp.zeros_like(red_ref)
        red_ref[0:1, :] = jnp.sum(dy * u2, axis=0, keepdims=True)
        red_ref[1:2, :] = jnp.sum(dy * u1, axis=0, keepdims=True)
        red_ref[2:3, :] = jnp.sum(dy * u, axis=0, keepdims=True)

    return pl.pallas_call(
        body, name=name,
        out_shape=(jax.ShapeDtypeStruct((S, 3 * CONV_W), BF16), jax.ShapeDtypeStruct((8, CONV_W), F32)),
        grid=(1,),
        in_specs=_conv_specs() + [_full_spec((S, CONV_W))],
        out_specs=(_full_spec((S, 3 * CONV_W)), _full_spec((8, CONV_W))),
        compiler_params=_params(("arbitrary",)),
    )(z, z, z, cw, dbr)


_NT = (((1,), (1,)), ((), ()))
_TN = (((0,), (0,)), ((), ()))
N_HEAD = 2 * N_PAIR


def _split3(x):
    hi = x.astype(BF16).astype(F32)
    mid = (x - hi).astype(BF16).astype(F32)
    lo = (x - hi - mid).astype(BF16).astype(F32)
    return hi, mid, lo


def _spare(lane, e, k):
    return lane == 64 * (1 - e) + k


def _spare3(lane, e, k):
    base = 64 * (1 - e) + k
    return (lane >= base) & (lane < base + 3)


def _put3(lane, e, k, pieces, rest):
    out = rest
    for n, piece in enumerate(pieces):
        out = jnp.where(_spare(lane, e, k + n), piece, out)
    return out


def _attn_prep(z, bf, name):
    def body(q_ref, k_ref, v_ref, f_ref, b_ref, qa_ref, ka_ref, va_ref, kat_ref, cum_ref):
        p = pl.program_id(0)
        row = lax.broadcasted_iota(jnp.int32, (S, LANE), 0)
        lane = lax.broadcasted_iota(jnp.int32, (S, LANE), 1)

        @pl.when(p == 0)
        def _():
            xv = f_ref[...] + b_ref[0:1, :]
            ls = jnp.minimum(xv, 0.0) - jnp.log(1.0 + jnp.exp(-jnp.abs(xv)))
            cum_ref[...] = _cumsum_rows(jnp.where(lane < N_HEAD, ls, 0.0), row)

        cum = cum_ref[...]
        q, k, v = q_ref[...], k_ref[...], v_ref[...]
        for e in range(2):
            head = (lane >= 64) if e else (lane < 64)
            f = jnp.sum(jnp.where(lane == 2 * p + e, cum, 0.0), axis=1, keepdims=True)
            pieces = _split3(f)
            qa = jnp.where(head, q * ATT_SCALE, _put3(lane, e, 0, pieces, jnp.where(_spare3(lane, e, 3), 1.0, 0.0)))
            ones = jnp.where(_spare3(lane, e, 0) | _spare3(lane, e, 6), 1.0, 0.0)
            ka = jnp.where(head, k, _put3(lane, e, 3, [-x for x in pieces], ones))
            va = jnp.where(head, v, jnp.where(_spare3(lane, e, 0), 1.0, 0.0))
            qa_ref[e] = qa.astype(BF16)
            ka_ref[e] = ka.astype(BF16)
            va_ref[e] = va.astype(BF16)
            kat_ref[e] = ka.T.astype(BF16)

    qb, kb, vb = Z_Q // LANE, Z_K // LANE, Z_V // LANE
    heads = jax.ShapeDtypeStruct((N_HEAD, S, LANE), BF16)
    pair = pl.BlockSpec((2, S, LANE), lambda p: (p, 0, 0))
    return pl.pallas_call(
        body, name=name,
        out_shape=(heads, heads, heads, jax.ShapeDtypeStruct((N_HEAD, LANE, S), BF16)),
        grid=(N_PAIR,),
        in_specs=[pl.BlockSpec((S, LANE), lambda p: (0, qb + p)), pl.BlockSpec((S, LANE), lambda p: (0, kb + p)),
                  pl.BlockSpec((S, LANE), lambda p: (0, vb + p)), pl.BlockSpec((S, LANE), lambda p: (0, Z_F // LANE)),
                  pl.BlockSpec((8, LANE), lambda p: (0, 0))],
        out_specs=(pair, pair, pair, pl.BlockSpec((2, LANE, S), lambda p: (p, 0, 0))),
        scratch_shapes=[pltpu.VMEM((S, LANE), F32)],
        compiler_params=_params(("arbitrary",)),
    )(z, z, z, z, bf)


def _attn_bwd_prep(qa, o, lse, do, name):
    def body(qa_ref, o_ref, lse_ref, do_ref, qa2_ref, doa_ref):
        lane = lax.broadcasted_iota(jnp.int32, (S, LANE), 1)
        dov, ov, lsev = do_ref[...], o_ref[...], lse_ref[...]
        for e in range(2):
            head = (lane >= 64) if e else (lane < 64)
            dsum = jnp.sum(jnp.where(head, dov * ov, 0.0), axis=1, keepdims=True)
            doa_ref[e] = jnp.where(head, dov, _put3(lane, e, 0, [-x for x in _split3(dsum)], 0.0)).astype(BF16)
            lse_col = lsev[:, 64 * e:64 * e + 1]
            qa2_ref[e] = _put3(lane, e, 6, [-x for x in _split3(lse_col)], qa_ref[e].astype(F32)).astype(BF16)

    heads = jax.ShapeDtypeStruct((N_HEAD, S, LANE), BF16)
    pair = pl.BlockSpec((2, S, LANE), lambda p: (p, 0, 0))
    cols = pl.BlockSpec((S, LANE), lambda p: (0, p))
    return pl.pallas_call(
        body, name=name, out_shape=(heads, heads), grid=(N_PAIR,),
        in_specs=[pair, cols, cols, cols], out_specs=(pair, pair),
        compiler_params=_params(("parallel",)),
    )(qa, o, lse, do)


def _attn_bwd_post(z, bf, dqt, dka, dva, name):
    def body(f_ref, b_ref, dqt_ref, dk_ref, dv_ref, dq_out, dk_out, dv_out, dfl_ref, red_ref, dcum_ref):
        p = pl.program_id(0)

        @pl.when(p == 0)
        def _():
            dcum_ref[...] = jnp.zeros_like(dcum_ref)

        row = lax.broadcasted_iota(jnp.int32, (S, LANE), 0)
        lane = lax.broadcasted_iota(jnp.int32, (S, LANE), 1)
        dqa = [dqt_ref[e].T for e in range(2)]
        dq_out[...] = (jnp.where(lane < 64, dqa[0], dqa[1]) * ATT_SCALE).astype(BF16)
        dk_out[...] = jnp.where(lane < 64, dk_ref[0], dk_ref[1]).astype(BF16)
        dv_out[...] = jnp.where(lane < 64, dv_ref[0], dv_ref[1]).astype(BF16)
        for e in range(2):
            d_query = jnp.sum(jnp.where(_spare(lane, e, 0), dqa[e], 0.0), axis=1, keepdims=True)
            d_key = jnp.sum(jnp.where(_spare(lane, e, 3), dk_ref[e], 0.0), axis=1, keepdims=True)
            dcum_ref[...] += jnp.where(lane == 2 * p + e, d_query - d_key, 0.0)

        @pl.when(p == N_PAIR - 1)
        def _():
            dls = _cumsum_rows(dcum_ref[...], row, reverse=True)
            xv = f_ref[...] + b_ref[0:1, :]
            dx = jnp.where(lane < N_HEAD, dls * jax.nn.sigmoid(-xv), 0.0)
            dfl_ref[...] = dx.astype(BF16)
            red_ref[...] = jnp.zeros_like(red_ref)
            red_ref[0:1, :] = jnp.sum(dx, axis=0, keepdims=True)

    wide = jax.ShapeDtypeStruct((S, N_PAIR * LANE), BF16)
    cols = pl.BlockSpec((S, LANE), lambda p: (0, p))
    pair = pl.BlockSpec((2, S, LANE), lambda p: (p, 0, 0))
    return pl.pallas_call(
        body, name=name,
        out_shape=(wide, wide, wide, jax.ShapeDtypeStruct((S, LANE), BF16), jax.ShapeDtypeStruct((8, LANE), F32)),
        grid=(N_PAIR,),
        in_specs=[pl.BlockSpec((S, LANE), lambda p: (0, Z_F // LANE)), pl.BlockSpec((8, LANE), lambda p: (0, 0)),
                  pl.BlockSpec((2, LANE, S), lambda p: (p, 0, 0)), pair, pair],
        out_specs=(cols, cols, cols, pl.BlockSpec((S, LANE), lambda p: (0, 0)), pl.BlockSpec((8, LANE), lambda p: (0, 0))),
        scratch_shapes=[pltpu.VMEM((S, LANE), F32)],
        compiler_params=_params(("arbitrary",)),
    )(z, bf, dqt, dka, dva)


def _attn_fwd(qa, ka, va, name):
    tq, tk = TQ_FWD, TQ
    ratio = tq // tk

    def body(qa_ref, ka_ref, va_ref, o_ref, lse_ref):
        i = pl.program_id(1)
        lane = lax.broadcasted_iota(jnp.int32, (tq, LANE), 1)
        row = lax.broadcasted_iota(jnp.int32, (tq, tk), 0)
        col = lax.broadcasted_iota(jnp.int32, (tq, tk), 1)
        nh = HEADS_PER_STEP_FWD
        qs = [qa_ref[h] for h in range(nh)]

        def block(j, carry, masked):
            off = pl.multiple_of(j * tk, tk)
            out = []
            for h in range(nh):
                m, acc = carry[h]
                s = lax.dot_general(qs[h], ka_ref[h, pl.ds(off, tk), :], _NT, preferred_element_type=F32)
                if masked:
                    s = jnp.where(col + (j - ratio * i) * tk > row, NEG_INF, s)
                mn = jnp.maximum(m, jnp.max(s, axis=1, keepdims=True))
                p = jnp.exp(s - mn).astype(BF16)
                acc = jnp.exp(m - mn) * acc + jnp.dot(p, va_ref[h, pl.ds(off, tk), :], preferred_element_type=F32)
                out.append((mn, acc))
            return tuple(out)

        init = (jnp.full((tq, 1), NEG_INF, F32), jnp.zeros((tq, LANE), F32))
        carry = lax.fori_loop(0, ratio * i, lambda j, c: block(j, c, False), (init,) * nh)
        for d in range(ratio):
            carry = block(ratio * i + d, carry, True)
        res = []
        for h in range(nh):
            m, acc = carry[h]
            l = jnp.sum(jnp.where(_spare(lane, h % 2, 0), acc, 0.0), axis=1, keepdims=True)
            res.append((acc / l, m + jnp.log(l)))
        for g in range(nh // 2):
            o_ref[:, g * LANE:(g + 1) * LANE] = jnp.where(lane < 64, res[2 * g][0], res[2 * g + 1][0])
            lse_ref[:, g * LANE:(g + 1) * LANE] = jnp.where(lane < 64, res[2 * g][1], res[2 * g + 1][1])

    nh = HEADS_PER_STEP_FWD
    out = jax.ShapeDtypeStruct((S, N_PAIR * LANE), F32)
    wide = pl.BlockSpec((tq, 64 * nh), lambda p, i: (i, p))
    return pl.pallas_call(
        body, name=name, out_shape=(out, out), grid=(N_HEAD // nh, S // tq),
        in_specs=[pl.BlockSpec((nh, tq, LANE), lambda p, i: (p, i, 0)), pl.BlockSpec((nh, S, LANE), lambda p, i: (p, 0, 0)),
                  pl.BlockSpec((nh, S, LANE), lambda p, i: (p, 0, 0))],
        out_specs=(wide, wide),
        compiler_params=_params(("parallel", "parallel")),
    )(qa, ka, va)


def _attn_bwd(qa2, ka, va, kat, doa, name):
    nq = S // TQ

    def body(qa_ref, ka_ref, va_ref, kat_ref, doa_ref, dqt_ref, dk_ref, dv_ref):
        j = pl.program_id(1)

        @pl.when(j == 0)
        def _():
            dqt_ref[...] = jnp.zeros_like(dqt_ref)

        key = lax.broadcasted_iota(jnp.int32, (TQ, TQ), 0)
        qry = lax.broadcasted_iota(jnp.int32, (TQ, TQ), 1)
        nh = HEADS_PER_STEP
        kav, vav, katv = ([ref[h] for h in range(nh)] for ref in (ka_ref, va_ref, kat_ref))

        def block(i, carry, masked):
            off = pl.multiple_of(i * TQ, TQ)
            out = []
            for h in range(nh):
                dk_acc, dv_acc = carry[h]
                qav = qa_ref[h, pl.ds(off, TQ), :]
                doav = doa_ref[h, pl.ds(off, TQ), :]
                s_t = lax.dot_general(kav[h], qav, _NT, preferred_element_type=F32)
                if masked:
                    s_t = jnp.where(key > qry, NEG_INF, s_t)
                p_t = jnp.exp(s_t)
                ds_t = p_t * lax.dot_general(vav[h], doav, _NT, preferred_element_type=F32)
                dsb = ds_t.astype(BF16)
                dv_acc = dv_acc + jnp.dot(p_t.astype(BF16), doav, preferred_element_type=F32)
                dk_acc = dk_acc + jnp.dot(dsb, qav, preferred_element_type=F32)
                dqt_ref[h, :, pl.ds(off, TQ)] += jnp.dot(katv[h], dsb, preferred_element_type=F32)
                out.append((dk_acc, dv_acc))
            return tuple(out)

        zero = (jnp.zeros((TQ, LANE), F32), jnp.zeros((TQ, LANE), F32))
        carry = block(j, (zero,) * nh, True)
        carry = lax.fori_loop(j + 1, nq, lambda i, c: block(i, c, False), carry)
        for h in range(nh):
            dk_ref[h], dv_ref[h] = carry[h]

    nh = HEADS_PER_STEP
    full = pl.BlockSpec((nh, S, LANE), lambda p, j: (p, 0, 0))
    blk = pl.BlockSpec((nh, TQ, LANE), lambda p, j: (p, j, 0))
    acc = jax.ShapeDtypeStruct((N_HEAD, S, LANE), F32)
    return pl.pallas_call(
        body, name=name,
        out_shape=(jax.ShapeDtypeStruct((N_HEAD, LANE, S), F32), acc, acc),
        grid=(N_HEAD // nh, nq),
        in_specs=[full, blk, blk, pl.BlockSpec((nh, LANE, TQ), lambda p, j: (p, 0, j)), full],
        out_specs=(pl.BlockSpec((nh, LANE, S), lambda p, j: (p, 0, 0)), blk, blk),
        compiler_params=_params(("arbitrary", "arbitrary")),
    )(qa2, ka, va, kat, doa)


ADA_ROWS = 16


def _ada_fwd(c_pad, w_ada, b_cols, name):
    def body(c_ref, w_ref, b_ref, o_ref):
        cv = c_ref[...]
        sc = (cv * jax.nn.sigmoid(cv)).astype(BF16)
        o_ref[0] = jnp.dot(sc, w_ref[0].astype(BF16), preferred_element_type=F32) + b_ref[0, 0:1, :]

    return pl.pallas_call(
        body, name=name, out_shape=jax.ShapeDtypeStruct((DEPTH, ADA_ROWS, ADA_COLS), F32), grid=(DEPTH,),
        in_specs=[pl.BlockSpec((ADA_ROWS, D), lambda l: (0, 0)), pl.BlockSpec((1, D, ADA_COLS), lambda l: (l, 0, 0)),
                  pl.BlockSpec((1, 8, ADA_COLS), lambda l: (l, 0, 0))],
        out_specs=pl.BlockSpec((1, ADA_ROWS, ADA_COLS), lambda l: (l, 0, 0)),
        compiler_params=_params(("parallel",)),
    )(c_pad, w_ada, b_cols)


def _ada_bwd(c_pad, dmod_cols, name):
    def body(c_ref, d_ref, o_ref):
        cv = c_ref[...]
        sc = (cv * jax.nn.sigmoid(cv)).astype(BF16)
        o_ref[0] = lax.dot_general(sc, d_ref[0].astype(BF16), _TN, preferred_element_type=F32)

    return pl.pallas_call(
        body, name=name, out_shape=jax.ShapeDtypeStruct((DEPTH, D, ADA_COLS), F32), grid=(DEPTH,),
        in_specs=[pl.BlockSpec((ADA_ROWS, D), lambda l: (0, 0)), pl.BlockSpec((1, ADA_ROWS, ADA_COLS), lambda l: (l, 0, 0))],
        out_specs=pl.BlockSpec((1, D, ADA_COLS), lambda l: (l, 0, 0)),
        compiler_params=_params(("parallel",)),
    )(c_pad, dmod_cols)


def _adamw_math(w, g, m, v):
    m = B1 * m + (1.0 - B1) * g
    v = B2 * v + (1.0 - B2) * (g * g)
    m_hat = m / (1.0 - B1 ** STEP)
    v_hat = v / (1.0 - B2 ** STEP)
    delta = -LR * (m_hat / (jnp.sqrt(v_hat) + EPS) + WD * w)
    return delta, m, v


def _row_tile(rows, target=256):
    best = 8
    for t in range(8, min(rows, target) + 1, 8):
        if rows % t == 0:
            best = t
    return best


def _adamw(w, g, m, v, name):
    layers, rows, cols = w.shape
    tr = _row_tile(rows)
    spec = pl.BlockSpec((1, tr, cols), lambda l, i: (l, i, 0))

    def body(w_ref, g_ref, m_ref, v_ref, d_ref, nm_ref, nv_ref):
        d_ref[...], nm_ref[...], nv_ref[...] = _adamw_math(w_ref[...], g_ref[...], m_ref[...], v_ref[...])

    out = jax.ShapeDtypeStruct(w.shape, F32)
    return pl.pallas_call(
        body, name=name, out_shape=(out, out, out), grid=(layers, rows // tr),
        in_specs=[spec] * 4, out_specs=(spec,) * 3, compiler_params=_params(("parallel", "parallel")),
    )(w, g, m, v)


def _sum_slabs(x, name):
    n, rows, _ = x.shape
    tr = _row_tile(rows)

    def body(x_ref, o_ref):
        acc = x_ref[0]
        for k in range(1, n):
            acc = acc + x_ref[k]
        o_ref[...] = acc

    return pl.pallas_call(
        body, name=name, out_shape=jax.ShapeDtypeStruct((rows, D), F32), grid=(rows // tr,),
        in_specs=[pl.BlockSpec((n, tr, D), lambda i: (0, i, 0))], out_specs=pl.BlockSpec((tr, D), lambda i: (i, 0)),
        compiler_params=_params(("parallel",)),
    )(x)


_ANY = pl.BlockSpec(memory_space=pl.ANY)
MESH = pl.DeviceIdType.MESH


def _on_sequencer(body, out_shape, sems, operands, after, sequencer_id, name):
    n = len(operands)

    def ordered_body(*refs):
        body(*refs[:n], *refs[n + 1:])

    extra = [] if after is None else [after]
    return pl.kernel(
        body if after is None else ordered_body, out_type=out_shape,
        mesh=plsc.ScalarSubcoreMesh(axis_name="sequencer", num_cores=1), scratch_types=sems,
        compiler_params=pltpu.CompilerParams(collective_id=sequencer_id), name=name)(*operands, *extra)


def _all_gather(xs, name, sequencer_id=None, after=None):
    n = len(xs)

    def body(*refs):
        x_refs, out_refs = refs[:n], refs[n:2 * n]
        send_sems, recv_sems, local_sems = refs[2 * n:]
        x_, y_, c_ = lax.axis_index("x"), lax.axis_index("y"), lax.axis_index("c")
        me, sibling = (x_, y_, c_), (x_, y_, 1 - c_)
        chips = [(1 - x_, y_), (x_, 1 - y_), (1 - x_, 1 - y_)]
        if sequencer_id is not None:
            barrier = pltpu.get_barrier_semaphore()
            peers = [sibling] + [(*chip, pc) for chip in chips for pc in (c_, 1 - c_)]
            for peer in peers:
                pl.semaphore_signal(barrier, inc=1, device_id=peer, device_id_type=MESH)
            pl.semaphore_wait(barrier, len(peers))

        def slot(a, px, py, pc):
            return out_refs[a].at[4 * px + 2 * py + pc]

        def copy(a, k, block, to, src=None):
            return pltpu.make_async_remote_copy(
                src_ref=slot(a, *block) if src is None else src, dst_ref=slot(a, *block),
                send_sem=send_sems.at[7 * a + k], recv_sem=recv_sems.at[7 * a + k], device_id=to, device_id_type=MESH)

        mine = [pltpu.make_async_copy(x_refs[a], slot(a, *me), local_sems.at[a]) for a in range(n)]
        for cp in mine:
            cp.start()
        first = []
        for a in range(n):
            first.append(copy(a, 0, me, sibling, src=x_refs[a]))
            first += [copy(a, 1 + j, me, (*chip, c_), src=x_refs[a]) for j, chip in enumerate(chips)]
        for cp in first:
            cp.start()
        passed = []
        for j, chip in enumerate(chips):
            for a in range(n):
                copy(a, 1 + j, (*chip, c_), me).wait_recv()
                passed.append(copy(a, 4 + j, (*chip, c_), sibling))
                passed[-1].start()
        for a in range(n):
            copy(a, 0, sibling, me).wait_recv()
        for j, chip in enumerate(chips):
            for a in range(n):
                copy(a, 4 + j, (*chip, 1 - c_), me).wait_recv()
        for cp in first + passed:
            cp.wait_send()
        for cp in mine:
            cp.wait()

    out_shape = [jax.ShapeDtypeStruct((N_DEV,) + x.shape, x.dtype) for x in xs]
    sems = [pltpu.SemaphoreType.DMA((7 * n,)), pltpu.SemaphoreType.DMA((7 * n,)), pltpu.SemaphoreType.DMA((n,))]
    if sequencer_id is not None:
        return _on_sequencer(body, out_shape, sems, xs, after, sequencer_id, name)
    return pl.pallas_call(
        body, name=name, out_shape=out_shape, in_specs=[_ANY] * n, out_specs=[_ANY] * n, scratch_shapes=sems)(*xs)


def _sibling_exchange(gs, name, sequencer_id=None, after=None):
    n = len(gs)

    def body(*refs):
        g_refs, p_refs = refs[:n], refs[n:2 * n]
        send_sems, recv_sems = refs[2 * n:]
        x_, y_, c_ = lax.axis_index("x"), lax.axis_index("y"), lax.axis_index("c")
        if sequencer_id is not None:
            barrier = pltpu.get_barrier_semaphore()
            pl.semaphore_signal(barrier, inc=1, device_id=(x_, y_, 1 - c_), device_id_type=MESH)
            pl.semaphore_wait(barrier, 1)
        copies = [pltpu.make_async_remote_copy(
            src_ref=g_refs[a].at[2 * k + (1 - c_)], dst_ref=p_refs[a].at[k], send_sem=send_sems.at[4 * a + k],
            recv_sem=recv_sems.at[4 * a + k], device_id=(x_, y_, 1 - c_), device_id_type=MESH)
            for a in range(n) for k in range(4)]
        for cp in copies:
            cp.start()
        for cp in copies:
            cp.wait()

    out_shape = [jax.ShapeDtypeStruct((4,) + g.shape[1:], g.dtype) for g in gs]
    sems = [pltpu.SemaphoreType.DMA((4 * n,)), pltpu.SemaphoreType.DMA((4 * n,))]
    if sequencer_id is not None:
        return _on_sequencer(body, out_shape, sems, gs, after, sequencer_id, name)
    return pl.pallas_call(
        body, name=name, out_shape=out_shape, in_specs=[_ANY] * n, out_specs=[_ANY] * n, scratch_shapes=sems)(*gs)


def _slab_tiles(rows, cols):
    if rows % 8 == 0:
        return _row_tile(rows), cols
    return rows, 2 * LANE


def _pair_sums(g, p, route, name):
    _, rows, cols = g.shape
    tr, tc = _slab_tiles(rows, cols)

    def body(route_ref, g_ref, p_ref, t_ref):
        t_ref[...] = (g_ref[...].astype(F32) + p_ref[...].astype(F32)).astype(BF16)

    return pl.pallas_call(
        body, name=name, out_shape=jax.ShapeDtypeStruct((3, rows, cols), BF16),
        grid_spec=pltpu.PrefetchScalarGridSpec(
            num_scalar_prefetch=1, grid=(3, rows // tr, cols // tc),
            in_specs=[pl.BlockSpec((1, tr, tc), lambda r, i, j, route_ref: (2 * route_ref[1 + r] + route_ref[0], i, j)),
                      pl.BlockSpec((1, tr, tc), lambda r, i, j, route_ref: (route_ref[1 + r], i, j))],
            out_specs=pl.BlockSpec((1, tr, tc), lambda r, i, j, route_ref: (r, i, j))),
        compiler_params=_params(("parallel", "parallel", "parallel")),
    )(route, g, p)


def _chip_exchange(ts, name, sequencer_id=None, after=None):
    n = len(ts)

    def body(*refs):
        t_refs, l_refs = refs[:n], refs[n:2 * n]
        send_sems, recv_sems = refs[2 * n:]
        x_, y_, c_ = lax.axis_index("x"), lax.axis_index("y"), lax.axis_index("c")
        chips = [(1 - x_, y_), (x_, 1 - y_), (1 - x_, 1 - y_)]
        if sequencer_id is not None:
            barrier = pltpu.get_barrier_semaphore()
            for px, py in chips:
                pl.semaphore_signal(barrier, inc=1, device_id=(px, py, c_), device_id_type=MESH)
            pl.semaphore_wait(barrier, len(chips))
        copies = [pltpu.make_async_remote_copy(
            src_ref=t_refs[a].at[r], dst_ref=l_refs[a].at[r], send_sem=send_sems.at[3 * a + r],
            recv_sem=recv_sems.at[3 * a + r], device_id=(px, py, c_), device_id_type=MESH)
            for a in range(n) for r, (px, py) in enumerate(chips)]
        for cp in copies:
            cp.start()
        for cp in copies:
            cp.wait()

    out_shape = [jax.ShapeDtypeStruct((3,) + t.shape[1:], t.dtype) for t in ts]
    sems = [pltpu.SemaphoreType.DMA((3 * n,)), pltpu.SemaphoreType.DMA((3 * n,))]
    if sequencer_id is not None:
        return _on_sequencer(body, out_shape, sems, ts, after, sequencer_id, name)
    return pl.pallas_call(
        body, name=name, out_shape=out_shape, in_specs=[_ANY] * n, out_specs=[_ANY] * n, scratch_shapes=sems)(*ts)


def _reduce_adamw(gs, ps, landed, place, w, m, v, name):
    layers, rows, cols = w.shape
    assert layers == DEPTH == 2
    tr, tc = _slab_tiles(rows, cols)
    nr, nc = rows // tr, cols // tc
    spec = pl.BlockSpec((1, tr, tc), lambda l, i, j, place_ref: (l, i, j))

    def own(layer, which):
        pi, pj = (nr - 1, nc - 1) if layer == 0 else (0, 0)

        def index(l, i, j, place_ref):
            lead = 0 if which is None else place_ref[which]
            return lead, jnp.where(l == layer, i, pi), jnp.where(l == layer, j, pj)

        return pl.BlockSpec((3 if which is None else 1, tr, tc), index)

    def body(place_ref, g0_ref, p0_ref, l0_ref, g1_ref, p1_ref, l1_ref, w_ref, m_ref, v_ref,
             g_ref, d_ref, nm_ref, nv_ref):
        def update(own_ref, sib_ref, l_ref):
            g = (own_ref[0].astype(F32) + sib_ref[0].astype(F32) + l_ref[0].astype(F32) + l_ref[1].astype(F32)
                 + l_ref[2].astype(F32))
            g_ref[0] = g
            d_ref[0], nm_ref[0], nv_ref[0] = _adamw_math(w_ref[0], g, m_ref[0], v_ref[0])

        @pl.when(pl.program_id(0) == 0)
        def _():
            update(g0_ref, p0_ref, l0_ref)

        @pl.when(pl.program_id(0) == 1)
        def _():
            update(g1_ref, p1_ref, l1_ref)

    out = jax.ShapeDtypeStruct(w.shape, F32)
    return pl.pallas_call(
        body, name=name, out_shape=(out, out, out, out),
        grid_spec=pltpu.PrefetchScalarGridSpec(
            num_scalar_prefetch=1, grid=(DEPTH, nr, nc),
            in_specs=[own(0, 0), own(0, 1), own(0, None), own(1, 0), own(1, 1), own(1, None), spec, spec, spec],
            out_specs=(spec, spec, spec, spec)),
        compiler_params=_params(("arbitrary", "arbitrary", "arbitrary")),
    )(place, gs[0], ps[0], landed[0], gs[1], ps[1], landed[1], w, m, v)


def _pack(pieces, row_multiple, dtype, cols=D, rows=None):
    flat = jnp.concatenate([p.astype(dtype).reshape(-1) for p in pieces])
    if rows is None:
        rows = -(-flat.shape[0] // cols)
        rows = -(-rows // row_multiple) * row_multiple
    flat = jnp.pad(flat, (0, rows * cols - flat.shape[0]))
    return flat.reshape(rows, cols)


def _unpack(flat, shapes, lead=()):
    out, off = [], 0
    for shp in shapes:
        n = 1
        for s_ in shp:
            n *= s_
        out.append(lax.slice_in_dim(flat, off, off + n, axis=len(lead)).reshape(lead + tuple(shp)))
        off += n
    return out


WIN_STRIDE = 704
WIN_ROWS = 720
Z_TURN = 1544


def _window(wt, me, name):
    padded = jnp.pad(wt, ((0, 0), (0, WIN_ROWS - IN_SHARD), (0, 0)))

    def body(me_ref, x_ref, o_ref):
        o_ref[0] = pltpu.roll(x_ref[0], me_ref[0], axis=0).astype(BF16)

    spec = pl.BlockSpec((1, WIN_ROWS, D), lambda l, me_ref: (l, 0, 0))
    return pl.pallas_call(
        body, name=name, out_shape=jax.ShapeDtypeStruct((DEPTH, WIN_ROWS, D), BF16),
        grid_spec=pltpu.PrefetchScalarGridSpec(num_scalar_prefetch=1, grid=(DEPTH,), in_specs=[spec], out_specs=spec),
        compiler_params=_params(("parallel",)),
    )(me, padded)


def _z_rows_from_windows(win):
    over = WIN_ROWS - WIN_STRIDE
    pieces = [(0, win[0][0:WIN_STRIDE])]
    for d in range(1, N_DEV):
        base = WIN_STRIDE * d
        pieces.append((base, win[d - 1][WIN_STRIDE:WIN_ROWS] + win[d][0:over]))
        pieces.append((base + over, win[d][over:WIN_STRIDE]))
    pieces.append((WIN_STRIDE * N_DEV, win[N_DEV - 1][WIN_STRIDE:WIN_ROWS]))

    def rows(a, b):
        out = []
        for start, arr in pieces:
            lo, hi = max(a, start), min(b, start + arr.shape[0])
            if lo < hi:
                out.append(arr[lo - start:hi - start])
        return out

    pad = jnp.zeros((NZ - IN_COLS, win.shape[-1]), win.dtype)
    return jnp.concatenate(rows(Z_TURN, IN_COLS) + rows(0, Z_TURN) + [pad], axis=0)


def _in_rows_from_z(wt):
    return jnp.concatenate([wt[Z_Q:Z_Q + 1536], wt[Z_F:Z_F + 8], wt[Z_PC:Z_PC + 1024], wt[Z_G:Z_G + 3072]], axis=0)


def _pad_rows(v, rows=8):
    return jnp.pad(v, ((0, rows - v.shape[0]), (0, 0)))


def _layer_fwd(l, x, wts, gvec, mod):
    tag = f"l{l}"
    h = _prenorm_fwd(x, gvec, mod, 0, 0, 1, f"prenorm_mix_{tag}")
    z = _matmul(h, wts["w_in_t"], "nt", f"in_proj_{tag}", tn=1152)
    qa, ka, va, kat = _attn_prep(z, wts["b_f"], f"attn_prep_{tag}")
    qa = wts["arrive"](qa)
    o, lse = _attn_fwd(qa, ka, va, f"attn_{tag}")
    br_b = _pool_fwd(z, wts["wp_bd"], wts["pool_scale"], f"pool_{tag}")
    br_c = _conv_fwd(z, wts["conv_w"], f"conv_{tag}")
    pa = _matmul(o, wts["wa"], "nn", f"proj_a_{tag}", out_dtype=BF16)
    pb = _matmul(br_b, wts["wb"], "nn", f"proj_b_{tag}", out_dtype=BF16)
    gates = [(z, Z_G + k * D) for k in range(3)]
    pc, merged = _matmul(br_c, wts["wc"], "nn", f"proj_c_merge_{tag}", tm=512, tn=512,
                         extra=gates + [(pa, 0), (pb, 0)], epilogue=_merge_epilogue, out_dtypes=(BF16, BF16))
    y = _matmul(merged, wts["w_out"], "nn", f"out_proj_{tag}")
    x1 = _postnorm_fwd(x, y, gvec, mod, 1, 2, f"postnorm_mix_{tag}")
    h2 = _prenorm_fwd(x1, gvec, mod, 2, 3, 4, f"prenorm_ff_{tag}")
    a, r = _matmul(h2, wts["w_ff1"], "nn", f"ff1_{tag}", b_col_shards=True, epilogue=_relu2_epilogue,
                   out_dtypes=(BF16, BF16))
    y2 = _matmul(r, wts["w_ff2"], "nn", f"ff2_{tag}", tk=1024)
    x2 = _postnorm_fwd(x1, y2, gvec, mod, 3, 5, f"postnorm_ff_{tag}")
    saved = dict(x=x, h=h, z=z, qa=qa, ka=ka, va=va, kat=kat, o=o, lse=lse, br_b=br_b, br_c=br_c, pa=pa, pb=pb, pc=pc,
                 merged=merged, y=y, x1=x1, h2=h2, a=a, r=r, y2=y2)
    return x2, saved


def _ffn_bwd(l, dx2, sv, wts, gvec, mod, midpoint):
    tag = f"l{l}"
    dy2, red_post_ff = _postnorm_bwd(sv["y2"], gvec, mod, dx2, 3, 5, f"postnorm_ff_bwd_{tag}")
    dy2 = midpoint(dy2)
    da = _matmul(dy2, wts["w_ff2"], "nt", f"ff2_dx_{tag}", extra=[(sv["a"], 0)], epilogue=_relu2_bwd_epilogue,
                 out_dtypes=(BF16,))[0]
    d_w_ff2 = _matmul(sv["r"], dy2, "tn", f"ff2_dw_{tag}", out_dtype=GRAD_DTYPE)
    dh2 = _matmul(da, wts["w_ff1"], "nt", f"ff1_dx_{tag}", b_col_shards=True)
    d_w_ff1 = _matmul(sv["h2"], da, "tn", f"ff1_dw_{tag}", out_dtype=GRAD_DTYPE, out_col_shards=True)
    dx1, red_pre_ff = _prenorm_bwd(sv["x1"], gvec, mod, dh2, dx2, 2, 4, f"prenorm_ff_bwd_{tag}")
    return dx1, [d_w_ff1, d_w_ff2.reshape(N_DEV, D_FF // N_DEV, D)], (red_pre_ff, red_post_ff)


def _mixer_bwd(l, dx1, sv, wts, gvec, mod, ffn_reds, midpoint):
    tag = f"l{l}"
    red_pre_ff, red_post_ff = ffn_reds
    dy, red_post_mix = _postnorm_bwd(sv["y"], gvec, mod, dx1, 1, 2, f"postnorm_mix_bwd_{tag}")
    gates = [(sv["z"], Z_G + k * D) for k in range(3)]
    dpa, dpb, dpc, *dgl = _matmul(dy, wts["w_out"], "nt", f"out_proj_dx_{tag}", tm=512, tn=512,
                                  extra=gates + [(sv["pa"], 0), (sv["pb"], 0), (sv["pc"], 0)],
                                  epilogue=_merge_bwd_epilogue, out_dtypes=(BF16,) * 6)
    d_w_out = _matmul(sv["merged"], dy, "tn", f"out_proj_dw_{tag}", out_dtype=GRAD_DTYPE)
    dpa = midpoint(dpa)
    do = _matmul(dpa, wts["wa"], "nt", f"proj_a_dx_{tag}")
    dbr_b = _matmul(dpb, wts["wb"], "nt", f"proj_b_dx_{tag}")
    dbr_c = _matmul(dpc, wts["wc"], "nt", f"proj_c_dx_{tag}")
    d_wa = _matmul(sv["o"], dpa, "tn", f"proj_a_dw_{tag}", out_dtype=GRAD_DTYPE)
    d_wb = _matmul(sv["br_b"], dpb, "tn", f"proj_b_dw_{tag}", out_dtype=GRAD_DTYPE)
    d_wc = _matmul(sv["br_c"], dpc, "tn", f"proj_c_dw_{tag}", out_dtype=GRAD_DTYPE)
    d_w_branch = jnp.concatenate([d_wa, d_wb, d_wc], axis=0)

    dpu, d_wp_bd, red_pool = _pool_bwd(sv["z"], wts["wp_bd"], wts["pool_scale"], dbr_b, f"pool_bwd_{tag}")
    dconv, red_conv = _conv_bwd(sv["z"], wts["conv_w"], dbr_c, f"conv_bwd_{tag}")
    qa2, doa = _attn_bwd_prep(sv["qa"], sv["o"], sv["lse"], do, f"attn_bwd_prep_{tag}")
    dqt, dka, dva = _attn_bwd(qa2, sv["ka"], sv["va"], sv["kat"], doa, f"attn_bwd_{tag}")
    dq, dk, dv, dfl, red_f = _attn_bwd_post(sv["z"], wts["b_f"], dqt, dka, dva, f"attn_bwd_post_{tag}")
    dz = jnp.concatenate([dpu, dconv, *dgl, dq, dk, dv, dfl], axis=1)
    dh = _matmul(dz, wts["w_in_t"], "nn", f"in_proj_dx_{tag}", tm=1024, tk=1920)
    d_w_in_t = _matmul(dz, sv["h"], "tn", f"in_proj_dw_{tag}", out_dtype=GRAD_DTYPE, tm=1152)
    dx0, red_pre_mix = _prenorm_bwd(sv["x"], gvec, mod, dh, dx1, 0, 1, f"prenorm_mix_bwd_{tag}")

    rows = D // N_DEV
    big = [_in_rows_from_z(d_w_in_t).reshape(N_DEV, IN_SHARD, D), d_w_branch.reshape(N_DEV, rows, D),
           d_w_out.reshape(N_DEV, rows, D)]
    d_w_pool = jnp.stack([d_wp_bd[64 * g:64 * (g + 1), 64 * g:64 * (g + 1)] for g in range(4)])
    small = dict(
        mod=jnp.stack([red_pre_mix[0], red_pre_mix[1], red_post_mix[0], red_pre_ff[0], red_pre_ff[1], red_post_ff[0]]),
        g_mix_pre=red_pre_mix[2], g_mix_post=red_post_mix[1], g_ff_pre=red_pre_ff[2], g_ff_post=red_post_ff[1],
        b_f=red_f[0, 0:8], w_pool=d_w_pool, pool_scale=red_pool[0], conv_w=red_conv[0:3])
    return dx0, big, small


SMALL_KEYS = ["mod", "g_mix_pre", "g_mix_post", "g_ff_pre", "g_ff_post", "b_f", "w_pool", "pool_scale", "conv_w"]
SMALL_SHAPES = [(DEPTH, 6 * D), (DEPTH, D), (DEPTH, D), (DEPTH, D), (DEPTH, D), (DEPTH, 8), (DEPTH, 4, 64, 64),
                (DEPTH, POOL_W), (DEPTH, 3, CONV_W)]


def kernel(x, c, w_ada, b_ada, g_mix_pre, g_mix_post, g_ff_pre, g_ff_post, w_in, b_f, w_pool, pool_scale, conv_w, w_branch, w_out, w_ff1, w_ff2, loss_target, m_w_ada, m_b_ada, m_g_mix_pre, m_g_mix_post, m_g_ff_pre, m_g_ff_post, m_w_in, m_b_f, m_w_pool, m_pool_scale, m_conv_w, m_w_branch, m_w_out, m_w_ff1, m_w_ff2, v_w_ada, v_b_ada, v_g_mix_pre, v_g_mix_post, v_g_ff_pre, v_g_ff_post, v_w_in, v_b_f, v_w_pool, v_pool_scale, v_conv_w, v_w_branch, v_w_out, v_w_ff1, v_w_ff2):
    ix, iy, ic = lax.axis_index("x"), lax.axis_index("y"), lax.axis_index("c")
    me = 4 * ix + 2 * iy + ic
    route = jnp.stack([ic, 2 * (1 - ix) + iy, 2 * ix + (1 - iy), 2 * (1 - ix) + (1 - iy)]).astype(jnp.int32)
    place = jnp.stack([me, 2 * ix + iy]).astype(jnp.int32)
    wt_in, mt_in, vt_in = (jnp.transpose(a, (0, 2, 1)) for a in (w_in, m_w_in, v_w_in))

    c_all = _all_gather([_pad_rows(c)], "gather_c")[0][:, 0, :]
    c_pad = _pad_rows(c_all, ADA_ROWS)
    b_cols = lax.dynamic_slice_in_dim(b_ada, me * ADA_COLS, ADA_COLS, axis=1)
    b_cols = jnp.broadcast_to(b_cols[:, None, :], (DEPTH, 8, ADA_COLS))
    mod_part = _ada_fwd(c_pad, w_ada, b_cols, "ada_fwd")
    mod_all = _all_gather([mod_part.reshape(DEPTH * ADA_ROWS, ADA_COLS)], "gather_mod")[0]
    mod_all = mod_all.reshape(N_DEV, DEPTH, ADA_ROWS, ADA_COLS)
    mod_mine = lax.dynamic_index_in_dim(mod_all, me, axis=2, keepdims=False)
    mod_mine = jnp.transpose(mod_mine, (1, 0, 2)).reshape(DEPTH, 6, D)

    cw_cols = CONV_W // N_DEV
    cw_send = jnp.pad(conv_w.reshape(DEPTH * 3, cw_cols), ((0, 8 - DEPTH * 3), (0, LANE - cw_cols)))
    win_in = _window(wt_in, place[0:1], "w_in_window")
    send = [[w[l].astype(BF16) for w in (win_in, w_branch, w_out, w_ff1, w_ff2)] for l in range(DEPTH)]
    first = _all_gather(send[0][:1], "gather_weights_l0_in", sequencer_id=1, after=mod_all)
    rest = _all_gather(send[0][1:] + [cw_send], "gather_weights_l0_rest", sequencer_id=2, after=first[0])
    first1 = _all_gather(send[1][:1], "gather_weights_l1_in", sequencer_id=3, after=first[0])
    rest1 = _all_gather(send[1][1:], "gather_weights_l1_rest", sequencer_id=12, after=first[0])
    gathered = [first + rest[:4], first1 + rest1]
    cw_all = rest[4][:, :DEPTH * 3, :cw_cols].reshape(N_DEV, DEPTH, 3, cw_cols)

    def first_operands(l, p_in):
        wp_bd = jnp.zeros((POOL_W, POOL_W), F32)
        for g in range(4):
            wp_bd = wp_bd.at[64 * g:64 * (g + 1), 64 * g:64 * (g + 1)].set(w_pool[l, g])
        return dict(w_in_t=_z_rows_from_windows(p_in), wp_bd=wp_bd.astype(BF16),
                    pool_scale=_pad_rows(pool_scale[l][None, :]), b_f=_pad_rows(jnp.pad(b_f[l], (0, LANE - 8))[None, :]))

    def rest_operands(l, rest):
        p_br, p_out, p_ff1, p_ff2 = rest
        w_br_full = p_br.reshape(D, D)
        cw_full = jnp.transpose(cw_all[:, l], (1, 0, 2)).reshape(3, CONV_W)
        return dict(wa=w_br_full[0:A_WIDTH], wb=w_br_full[A_WIDTH:A_WIDTH + POOL_W], wc=w_br_full[A_WIDTH + POOL_W:],
                    w_out=p_out.reshape(D, D), w_ff1=p_ff1, w_ff2=p_ff2.reshape(D_FF, D), conv_w=_pad_rows(cw_full))

    xs = x[0]
    saved, layers = [], []
    for l in range(DEPTH):
        p_in, rest = gathered[l][0], gathered[l][1:5]
        if l > 0:
            xs, p_in = lax.optimization_barrier((xs, p_in))
        wts = first_operands(l, p_in)

        def arrive(t, l=l, rest=rest, wts=wts):
            if l > 0:
                t, rest = lax.optimization_barrier((t, rest))
            wts.update(rest_operands(l, rest))
            return t

        wts["arrive"] = arrive
        gvec = _pad_rows(jnp.stack([g_mix_pre[l], g_mix_post[l], g_ff_pre[l], g_ff_post[l]]))
        layers.append((wts, gvec, _pad_rows(mod_mine[l])))
        xs, sv = _layer_fwd(l, xs, *layers[l])
        saved.append(sv)
    dx, loss_part = _loss_head(xs, loss_target[0], "loss_head")
    small_grads = [None] * DEPTH
    mine, sibs, landed = ({} for _ in range(3))
    seq_id = iter(range(4, 4 + 4 * DEPTH))
    last = [gathered[DEPTH - 1][1]]

    def start(group, grads):
        mine[group] = grads
        sibs[group] = _sibling_exchange(grads, f"rs_sibling_{group}", sequencer_id=next(seq_id), after=last[0])
        last[0] = sibs[group][0]

    def finish(group, later):
        later, (grads, sib) = lax.optimization_barrier((later, (mine[group], sibs[group])))
        sends = [_pair_sums(g, p, route, f"rs_pair_sums_{group}_{k}") for k, (g, p) in enumerate(zip(grads, sib))]
        later, sends = lax.optimization_barrier((later, sends))
        landed[group] = _chip_exchange(sends, f"rs_chips_{group}", sequencer_id=next(seq_id), after=last[0])
        last[0] = landed[group][0]
        return later

    pending = None
    for l in reversed(range(DEPTH)):
        hook = (lambda da: da) if pending is None else functools.partial(finish, pending)
        dx, ffn_grads, ffn_reds = _ffn_bwd(l, dx, saved[l], *layers[l], hook)
        start(f"ffn_l{l}", ffn_grads)
        dx, mix_grads, small_grads[l] = _mixer_bwd(l, dx, saved[l], *layers[l], ffn_reds,
                                                   functools.partial(finish, f"ffn_l{l}"))
        start(f"mix_l{l}", mix_grads)
        pending = f"mix_l{l}"
    grad_x = dx[None]

    big_w = [wt_in, w_branch, w_out, w_ff1, w_ff2]
    big_m = [mt_in, m_w_branch, m_w_out, m_w_ff1, m_w_ff2]
    big_v = [vt_in, v_w_branch, v_w_out, v_w_ff1, v_w_ff2]
    where = [("mix", 0), ("mix", 1), ("mix", 2), ("ffn", 0), ("ffn", 1)]

    def reduce_and_update(k):
        group, at = where[k]
        return _reduce_adamw([mine[f"{group}_l{l}"][at] for l in range(DEPTH)],
                             [sibs[f"{group}_l{l}"][at] for l in range(DEPTH)],
                             [landed[f"{group}_l{l}"][at] for l in range(DEPTH)], place, big_w[k], big_m[k], big_v[k],
                             f"rs_sum_adamw_{k}")

    big_res = {k: list(reduce_and_update(k)) for k in (3, 4)}
    big_res[3][0] = finish(pending, big_res[3][0])

    small = {k: jnp.stack([small_grads[l][k] for l in range(DEPTH)]) for k in SMALL_KEYS}
    payload = _pack([small[k] for k in SMALL_KEYS] + [loss_part[0:1, 0:1]], 8, F32)
    small_all = _all_gather([payload], "gather_small")[0]
    dmod_all = small_all[:, 0:DEPTH * 6, :].reshape(N_DEV, DEPTH, 6 * D)
    summed = _unpack(_sum_slabs(small_all, "sum_small").reshape(-1), SMALL_SHAPES + [(1, 1)])
    sg = dict(zip(SMALL_KEYS, summed))
    loss = summed[-1][0, 0]
    dmod_cols = lax.dynamic_slice_in_dim(dmod_all, me * ADA_COLS, ADA_COLS, axis=2)
    dmod_cols = jnp.pad(jnp.transpose(dmod_cols, (1, 0, 2)), ((0, 0), (0, ADA_ROWS - N_DEV), (0, 0)))
    g_w_ada = _ada_bwd(c_pad, dmod_cols, "ada_bwd")
    g_conv_w = lax.dynamic_slice_in_dim(sg["conv_w"], me * (CONV_W // N_DEV), CONV_W // N_DEV, axis=2)

    ada_out = [g_w_ada] + list(_adamw(w_ada, g_w_ada, m_w_ada, v_w_ada, "adamw_ada"))
    rest_w = [b_ada, g_mix_pre, g_mix_post, g_ff_pre, g_ff_post, b_f, w_pool, pool_scale, conv_w]
    rest_m = [m_b_ada, m_g_mix_pre, m_g_mix_post, m_g_ff_pre, m_g_ff_post, m_b_f, m_w_pool, m_pool_scale, m_conv_w]
    rest_v = [v_b_ada, v_g_mix_pre, v_g_mix_post, v_g_ff_pre, v_g_ff_post, v_b_f, v_w_pool, v_pool_scale, v_conv_w]
    rest_g = [sg["mod"], sg["g_mix_pre"], sg["g_mix_post"], sg["g_ff_pre"], sg["g_ff_post"], sg["b_f"],
              sg["w_pool"], sg["pool_scale"], g_conv_w]
    rest_shapes = [a.shape for a in rest_w]
    upd = _adamw(_pack(rest_w, 8, F32)[None], _pack(rest_g, 8, F32)[None], _pack(rest_m, 8, F32)[None],
                 _pack(rest_v, 8, F32)[None], "adamw_rest")
    rest_out = [rest_g] + [_unpack(arr.reshape(-1), rest_shapes) for arr in upd]
    rest_out = [[ada_out[which]] + rest_out[which] for which in range(4)]

    landed[pending], rest_out = lax.optimization_barrier((landed[pending], rest_out))
    big_res.update({k: reduce_and_update(k) for k in (0, 1, 2)})
    big_out = [[jnp.transpose(big_res[k][which], (0, 2, 1)) if k == 0 else big_res[k][which] for k in range(5)]
               for which in range(4)]

    def ordered(k):
        r, b = rest_out[k], big_out[k]
        return [r[0], r[1], r[2], r[3], r[4], r[5], b[0], r[6], r[7], r[8], r[9], b[1], b[2], b[3], b[4]]

    return (loss, grad_x, *ordered(0), *ordered(1), *ordered(2), *ordered(3))
```

```python
import functools

import jax
import jax.numpy as jnp
from jax import lax
from jax.experimental import pallas as pl
from jax.experimental.pallas import tpu as pltpu
from jax.experimental.pallas import tpu_sc as plsc

F32 = jnp.float32
BF16 = jnp.bfloat16
GRAD_DTYPE = BF16

N_DEV = 8
D = 1024
S = 2048
DEPTH = 2
D_FF = 4 * D
A_WIDTH = 512
HEAD_DIM = 64
N_PAIR = 4
POOL_W = 256
CONV_W = 256
IN_COLS = 5640
ADA_COLS = 6 * D // N_DEV
IN_SHARD = IN_COLS // N_DEV
RMS_EPS = 1e-6
NEG_INF = -1e30
ATT_SCALE = HEAD_DIM ** -0.5

NZ = 5760
Z_PC = 0
Z_G = 1024
Z_Q = 4096
Z_K = 4608
Z_V = 5120
Z_F = 5632

LR, B1, B2, EPS, WD, STEP = 0.001, 0.9, 0.999, 1e-08, 0.01, 10

LANE = 128
VMEM_LIMIT_BYTES = 48 * 1024 * 1024
TS = 512
TQ = 256
TQ_FWD = 512
HEADS_PER_STEP = 8
HEADS_PER_STEP_FWD = 8


def _params(sem=None):
    return pltpu.CompilerParams(dimension_semantics=sem, vmem_limit_bytes=VMEM_LIMIT_BYTES)


def _pick(n, target):
    best = None
    for t in range(LANE, min(n, target) + 1, LANE):
        if n % t == 0:
            best = t
    return n if best is None else best


def _matmul(a, b, mode, name, out_dtype=F32, tm=2048, tn=1024, tk=2048, b_col_shards=False, out_col_shards=False,
            extra=(), epilogue=None, out_dtypes=None):
    if b_col_shards:
        shards, b_rows, shard_cols = b.shape
        b_shape = (b_rows, shards * shard_cols)
    else:
        b_shape = b.shape
    if mode == "nn":
        (m, k), (k2, n) = a.shape, b_shape
    elif mode == "nt":
        (m, k), (n, k2) = a.shape, b_shape
    else:
        (k, m), (k2, n) = a.shape, b_shape
    assert k == k2, (a.shape, b.shape, mode)
    tm, tn, tk = _pick(m, tm), _pick(n, tn), _pick(k, tk)
    if b_col_shards and mode == "nn":
        tn = shard_cols
    per_step = 1
    if b_col_shards and mode == "nt":
        per_step = max(1, min(tk, 1024) // shard_cols)
        tk = per_step * shard_cols
    if out_col_shards:
        tn = n // N_DEV
    nk = k // tk
    if mode == "nn":
        a_spec = pl.BlockSpec((tm, tk), lambda i, j, kk: (i, kk))
        b_spec = (pl.BlockSpec((None, tk, tn), lambda i, j, kk: (j, kk, 0)) if b_col_shards else
                  pl.BlockSpec((tk, tn), lambda i, j, kk: (kk, j)))
        dims = (((1,), (0,)), ((), ()))
    elif mode == "nt":
        a_spec = pl.BlockSpec((tm, tk), lambda i, j, kk: (i, kk))
        b_spec = (pl.BlockSpec((per_step, tn, shard_cols), lambda i, j, kk: (kk, j, 0)) if b_col_shards else
                  pl.BlockSpec((tn, tk), lambda i, j, kk: (j, kk)))
        dims = (((1,), (1,)), ((), ()))
    else:
        assert not b_col_shards
        a_spec = pl.BlockSpec((tk, tm), lambda i, j, kk: (kk, i))
        b_spec = pl.BlockSpec((tk, tn), lambda i, j, kk: (kk, j))
        dims = (((0,), (0,)), ((), ()))
    if out_col_shards:
        out_shape = jax.ShapeDtypeStruct((N_DEV, m, tn), out_dtype)
        out_spec = pl.BlockSpec((None, tm, tn), lambda i, j, kk: (j, i, 0))
    else:
        out_shape = jax.ShapeDtypeStruct((m, n), out_dtype)
        out_spec = pl.BlockSpec((tm, tn), lambda i, j, kk: (i, j))

    n_extra = len(extra)
    extra_specs = [pl.BlockSpec((tm, tn), lambda i, j, kk, off=off: (i, j + off // tn)) for _, off in extra]
    if epilogue is not None:
        assert not out_col_shards and all(off % tn == 0 for _, off in extra)
        out_shape = [jax.ShapeDtypeStruct((m, n), dt) for dt in out_dtypes]
        out_spec = [pl.BlockSpec((tm, tn), lambda i, j, kk: (i, j)) for _ in out_dtypes]

    def product(a_ref, b_ref):
        if b_col_shards and mode == "nt":
            b_tile = jnp.concatenate([b_ref[s] for s in range(per_step)], axis=1) if per_step > 1 else b_ref[0]
        else:
            b_tile = b_ref[...]
        return lax.dot_general(a_ref[...].astype(BF16), b_tile.astype(BF16), dims, preferred_element_type=F32)

    def write(acc, extra_refs, o_refs):
        if epilogue is None:
            o_refs[0][...] = acc.astype(out_dtype)
        else:
            for o_ref, tile in zip(o_refs, epilogue(acc, *[r[...] for r in extra_refs])):
                o_ref[...] = tile.astype(o_ref.dtype)

    def body_one_pass(a_ref, b_ref, *refs):
        write(product(a_ref, b_ref), refs[:n_extra], refs[n_extra:])

    def body(a_ref, b_ref, *refs):
        acc_ref = refs[-1]
        kk = pl.program_id(2)

        @pl.when(kk == 0)
        def _():
            acc_ref[...] = product(a_ref, b_ref)

        @pl.when(kk > 0)
        def _():
            acc_ref[...] += product(a_ref, b_ref)

        @pl.when(kk == nk - 1)
        def _():
            write(acc_ref[...], refs[:n_extra], refs[n_extra:-1])

    return pl.pallas_call(
        body_one_pass if nk == 1 else body, name=name,
        out_shape=out_shape,
        grid=(m // tm, n // tn, nk),
        in_specs=[a_spec, b_spec] + extra_specs,
        out_specs=out_spec,
        scratch_shapes=[] if nk == 1 else [pltpu.VMEM((tm, tn), F32)],
        compiler_params=_params(("parallel", "parallel", "arbitrary")),
    )(a, b, *[x for x, _ in extra])


def _row_spec(width=D, col=0):
    return pl.BlockSpec((TS, width), lambda i: (i, col))


def _vec_spec(rows=8, width=D):
    return pl.BlockSpec((rows, width), lambda i: (0, 0))


def _rms(x):
    return lax.rsqrt(jnp.mean(x * x, axis=-1, keepdims=True) + RMS_EPS)


def _prenorm_fwd(x, gvec, mod, g_row, shift_row, scale_row, name):
    def body(x_ref, g_ref, mod_ref, h_ref):
        xv = x_ref[...]
        y = xv * _rms(xv) * g_ref[g_row:g_row + 1, :]
        h = y * (1.0 + mod_ref[scale_row:scale_row + 1, :]) + mod_ref[shift_row:shift_row + 1, :]
        h_ref[...] = h.astype(BF16)

    return pl.pallas_call(
        body, name=name, out_shape=jax.ShapeDtypeStruct((S, D), BF16), grid=(S // TS,),
        in_specs=[_row_spec(), _vec_spec(), _vec_spec()], out_specs=_row_spec(),
        compiler_params=_params(("parallel",)),
    )(x, gvec, mod)


def _prenorm_bwd(x, gvec, mod, dh, dres, g_row, scale_row, name):
    def body(x_ref, g_ref, mod_ref, dh_ref, dres_ref, dx_ref, red_ref):
        i = pl.program_id(0)

        @pl.when(i == 0)
        def _():
            red_ref[...] = jnp.zeros_like(red_ref)

        xv = x_ref[...]
        g = g_ref[g_row:g_row + 1, :]
        r = _rms(xv)
        n = xv * r
        yg = n * g
        dhv = dh_ref[...]
        dyg = dhv * (1.0 + mod_ref[scale_row:scale_row + 1, :])
        dn = dyg * g
        dx = r * (dn - n * jnp.mean(dn * n, axis=-1, keepdims=True))
        dx_ref[...] = dres_ref[...] + dx
        red_ref[0:1, :] += jnp.sum(dhv, axis=0, keepdims=True)
        red_ref[1:2, :] += jnp.sum(dhv * yg, axis=0, keepdims=True)
        red_ref[2:3, :] += jnp.sum(dyg * n, axis=0, keepdims=True)

    return pl.pallas_call(
        body, name=name,
        out_shape=(jax.ShapeDtypeStruct((S, D), F32), jax.ShapeDtypeStruct((8, D), F32)),
        grid=(S // TS,),
        in_specs=[_row_spec(), _vec_spec(), _vec_spec(), _row_spec(), _row_spec()],
        out_specs=(_row_spec(), _vec_spec()),
        compiler_params=_params(("arbitrary",)),
    )(x, gvec, mod, dh, dres)


def _postnorm_fwd(x, y, gvec, mod, g_row, gate_row, name):
    def body(x_ref, y_ref, g_ref, mod_ref, o_ref):
        yv = y_ref[...]
        yn = yv * _rms(yv) * g_ref[g_row:g_row + 1, :]
        o_ref[...] = x_ref[...] + mod_ref[gate_row:gate_row + 1, :] * yn

    return pl.pallas_call(
        body, name=name, out_shape=jax.ShapeDtypeStruct((S, D), F32), grid=(S // TS,),
        in_specs=[_row_spec(), _row_spec(), _vec_spec(), _vec_spec()], out_specs=_row_spec(),
        compiler_params=_params(("parallel",)),
    )(x, y, gvec, mod)


def _postnorm_bwd(y, gvec, mod, dxo, g_row, gate_row, name):
    def body(y_ref, g_ref, mod_ref, dxo_ref, dy_ref, red_ref):
        i = pl.program_id(0)

        @pl.when(i == 0)
        def _():
            red_ref[...] = jnp.zeros_like(red_ref)

        yv = y_ref[...]
        g = g_ref[g_row:g_row + 1, :]
        r = _rms(yv)
        n = yv * r
        dxo = dxo_ref[...]
        dyn = dxo * mod_ref[gate_row:gate_row + 1, :]
        dn = dyn * g
        dy = r * (dn - n * jnp.mean(dn * n, axis=-1, keepdims=True))
        dy_ref[...] = dy.astype(BF16)
        red_ref[0:1, :] += jnp.sum(dxo * (n * g), axis=0, keepdims=True)
        red_ref[1:2, :] += jnp.sum(dyn * n, axis=0, keepdims=True)

    return pl.pallas_call(
        body, name=name,
        out_shape=(jax.ShapeDtypeStruct((S, D), BF16), jax.ShapeDtypeStruct((8, D), F32)),
        grid=(S // TS,),
        in_specs=[_row_spec(), _vec_spec(), _vec_spec(), _row_spec()],
        out_specs=(_row_spec(), _vec_spec()),
        compiler_params=_params(("arbitrary",)),
    )(y, gvec, mod, dxo)


def _loss_head(xf, target, name):
    def body(x_ref, t_ref, dx_ref, loss_ref):
        i = pl.program_id(0)

        @pl.when(i == 0)
        def _():
            loss_ref[...] = jnp.zeros_like(loss_ref)

        e = x_ref[...] - t_ref[...]
        dx_ref[...] = e / float(D)
        per_tok = jnp.mean(e * e, axis=-1, keepdims=True)
        loss_ref[0:1, 0:1] += 0.5 * jnp.sum(per_tok, axis=0, keepdims=True)

    return pl.pallas_call(
        body, name=name,
        out_shape=(jax.ShapeDtypeStruct((S, D), F32), jax.ShapeDtypeStruct((8, LANE), F32)),
        grid=(S // TS,),
        in_specs=[_row_spec(), _row_spec()],
        out_specs=(_row_spec(), pl.BlockSpec((8, LANE), lambda i: (0, 0))),
        compiler_params=_params(("arbitrary",)),
    )(xf, target)


def _relu2_epilogue(a):
    t = jnp.maximum(a, 0.0)
    return a, t * t


def _relu2_bwd_epilogue(dr, a):
    return (dr * (2.0 * jnp.maximum(a, 0.0)),)


def _merge_epilogue(pc, g0, g1, g2, pa, pb):
    return pc, jax.nn.sigmoid(g0) * pa + jax.nn.sigmoid(g1) * pb + jax.nn.sigmoid(g2) * pc


def _merge_bwd_epilogue(dm, g0, g1, g2, pa, pb, pc):
    sg = [jax.nn.sigmoid(g) for g in (g0, g1, g2)]
    return tuple(dm * s for s in sg) + tuple(dm * p * (s * (1.0 - s)) for p, s in zip((pa, pb, pc), sg))


def _shift_down(x, k, row):
    return jnp.where(row >= k, pltpu.roll(x, k, axis=0), 0.0)


def _shift_up(x, k, row):
    n = x.shape[0]
    return jnp.where(row < n - k, pltpu.roll(x, n - k, axis=0), 0.0)


def _cumsum_rows(x, row, reverse=False):
    shift = _shift_up if reverse else _shift_down
    k = 1
    while k < x.shape[0]:
        x = x + shift(x, k, row)
        k *= 2
    return x


def _full_spec(shape, idx=(0, 0)):
    return pl.BlockSpec(shape, lambda i: idx)


def _pool_window_select(lane, a2, a4, a8, a16):
    return jnp.where(lane < 64, a2, jnp.where(lane < 128, a4, jnp.where(lane < 192, a8, a16)))


def _pool_p(u, row, lane):
    t2 = u + _shift_down(u, 1, row)
    t4 = t2 + _shift_down(t2, 2, row)
    t8 = t4 + _shift_down(t4, 4, row)
    t16 = t8 + _shift_down(t8, 8, row)
    tw = _pool_window_select(lane, t2, t4, t8, t16)
    cnt = jnp.minimum((row + 1).astype(F32), _pool_window_select(lane, 2.0, 4.0, 8.0, 16.0))
    return tw / cnt - u, cnt


def _pool_fwd(z, wp_bd, pscale, name):
    def body(u_ref, w_ref, s_ref, o_ref):
        row = lax.broadcasted_iota(jnp.int32, (S, POOL_W), 0)
        lane = lax.broadcasted_iota(jnp.int32, (S, POOL_W), 1)
        p, _ = _pool_p(u_ref[...], row, lane)
        y = jnp.dot(p.astype(BF16), w_ref[...], preferred_element_type=F32)
        o_ref[...] = y * s_ref[0:1, :]

    return pl.pallas_call(
        body, name=name, out_shape=jax.ShapeDtypeStruct((S, POOL_W), F32), grid=(1,),
        in_specs=[_full_spec((S, POOL_W), (0, Z_PC // POOL_W)), _full_spec((POOL_W, POOL_W)), _full_spec((8, POOL_W))],
        out_specs=_full_spec((S, POOL_W)),
        compiler_params=_params(("arbitrary",)),
    )(z, wp_bd, pscale)


def _pool_bwd(z, wp_bd, pscale, dbr, name):
    def body(u_ref, w_ref, s_ref, dbr_ref, du_ref, dw_ref, red_ref):
        row = lax.broadcasted_iota(jnp.int32, (S, POOL_W), 0)
        lane = lax.broadcasted_iota(jnp.int32, (S, POOL_W), 1)
        p, cnt = _pool_p(u_ref[...], row, lane)
        pb = p.astype(BF16)
        y = jnp.dot(pb, w_ref[...], preferred_element_type=F32)
        dbr = dbr_ref[...]
        red_ref[...] = jnp.zeros_like(red_ref)
        red_ref[0:1, :] = jnp.sum(dbr * y, axis=0, keepdims=True)
        dy = (dbr * s_ref[0:1, :]).astype(BF16)
        dw_ref[...] = lax.dot_general(pb, dy, (((0,), (0,)), ((), ())), preferred_element_type=F32)
        dp = lax.dot_general(dy, w_ref[...], (((1,), (1,)), ((), ())), preferred_element_type=F32)
        g = dp / cnt
        a2 = g + _shift_up(g, 1, row)
        a4 = a2 + _shift_up(a2, 2, row)
        a8 = a4 + _shift_up(a4, 4, row)
        a16 = a8 + _shift_up(a8, 8, row)
        du_ref[...] = (_pool_window_select(lane, a2, a4, a8, a16) - dp).astype(BF16)

    return pl.pallas_call(
        body, name=name,
        out_shape=(jax.ShapeDtypeStruct((S, POOL_W), BF16), jax.ShapeDtypeStruct((POOL_W, POOL_W), F32),
                   jax.ShapeDtypeStruct((8, POOL_W), F32)),
        grid=(1,),
        in_specs=[_full_spec((S, POOL_W), (0, Z_PC // POOL_W)), _full_spec((POOL_W, POOL_W)), _full_spec((8, POOL_W)),
                  _full_spec((S, POOL_W))],
        out_specs=(_full_spec((S, POOL_W)), _full_spec((POOL_W, POOL_W)), _full_spec((8, POOL_W))),
        compiler_params=_params(("arbitrary",)),
    )(z, wp_bd, pscale, dbr)


def _conv_specs():
    base = Z_PC // CONV_W
    return [_full_spec((S, CONV_W), (0, base + 1)), _full_spec((S, CONV_W), (0, base + 2)),
            _full_spec((S, CONV_W), (0, base + 3)), _full_spec((8, CONV_W))]


def _conv_fwd(z, cw, name):
    def body(h_ref, b_ref, c_ref, w_ref, o_ref):
        row = lax.broadcasted_iota(jnp.int32, (S, CONV_W), 0)
        u = c_ref[...] * h_ref[...]
        y = (w_ref[0:1, :] * _shift_down(u, 2, row) + w_ref[1:2, :] * _shift_down(u, 1, row) + w_ref[2:3, :] * u)
        o_ref[...] = b_ref[...] * y

    return pl.pallas_call(
        body, name=name, out_shape=jax.ShapeDtypeStruct((S, CONV_W), F32), grid=(1,),
        in_specs=_conv_specs(), out_specs=_full_spec((S, CONV_W)),
        compiler_params=_params(("arbitrary",)),
    )(z, z, z, cw)


def _conv_bwd(z, cw, dbr, name):
    def body(h_ref, b_ref, c_ref, w_ref, dbr_ref, d_ref, red_ref):
        row = lax.broadcasted_iota(jnp.int32, (S, CONV_W), 0)
        h, cg = h_ref[...], c_ref[...]
        u = cg * h
        u1 = _shift_down(u, 1, row)
        u2 = _shift_down(u, 2, row)
        y = w_ref[0:1, :] * u2 + w_ref[1:2, :] * u1 + w_ref[2:3, :] * u
        dbr = dbr_ref[...]
        dy = dbr * b_ref[...]
        du = w_ref[2:3, :] * dy + w_ref[1:2, :] * _shift_up(dy, 1, row) + w_ref[0:1, :] * _shift_up(dy, 2, row)
        d_ref[:, 0:CONV_W] = (du * cg).astype(BF16)
        d_ref[:, CONV_W:2 * CONV_W] = (dbr * y).astype(BF16)
        d_ref[:, 2 * CONV_W:3 * CONV_W] = (du * h).astype(BF16)
        red_ref[...] = jnp.zeros_like(red_ref)
        red_ref[0:1, :] = jnp.sum(dy * u2, axis=0, keepdims=True)
        red_ref[1:2, :] = jnp.sum(dy * u1, axis=0, keepdims=True)
        red_ref[2:3, :] = jnp.sum(dy * u, axis=0, keepdims=True)

    return pl.pallas_call(
        body, name=name,
        out_shape=(jax.ShapeDtypeStruct((S, 3 * CONV_W), BF16), jax.ShapeDtypeStruct((8, CONV_W), F32)),
        grid=(1,),
        in_specs=_conv_specs() + [_full_spec((S, CONV_W))],
        out_specs=(_full_spec((S, 3 * CONV_W)), _full_spec((8, CONV_W))),
        compiler_params=_params(("arbitrary",)),
    )(z, z, z, cw, dbr)


_NT = (((1,), (1,)), ((), ()))
_TN = (((0,), (0,)), ((), ()))
N_HEAD = 2 * N_PAIR


def _split3(x):
    hi = x.astype(BF16).astype(F32)
    mid = (x - hi).astype(BF16).astype(F32)
    lo = (x - hi - mid).astype(BF16).astype(F32)
    return hi, mid, lo


def _spare(lane, e, k):
    return lane == 64 * (1 - e) + k


def _spare3(lane, e, k):
    base = 64 * (1 - e) + k
    return (lane >= base) & (lane < base + 3)


def _put3(lane, e, k, pieces, rest):
    out = rest
    for n, piece in enumerate(pieces):
        out = jnp.where(_spare(lane, e, k + n), piece, out)
    return out


def _attn_prep(z, bf, name):
    def body(q_ref, k_ref, v_ref, f_ref, b_ref, qa_ref, ka_ref, va_ref, kat_ref, cum_ref):
        p = pl.program_id(0)
        row = lax.broadcasted_iota(jnp.int32, (S, LANE), 0)
        lane = lax.broadcasted_iota(jnp.int32, (S, LANE), 1)

        @pl.when(p == 0)
        def _():
            xv = f_ref[...] + b_ref[0:1, :]
            ls = jnp.minimum(xv, 0.0) - jnp.log(1.0 + jnp.exp(-jnp.abs(xv)))
            cum_ref[...] = _cumsum_rows(jnp.where(lane < N_HEAD, ls, 0.0), row)

        cum = cum_ref[...]
        q, k, v = q_ref[...], k_ref[...], v_ref[...]
        for e in range(2):
            head = (lane >= 64) if e else (lane < 64)
            f = jnp.sum(jnp.where(lane == 2 * p + e, cum, 0.0), axis=1, keepdims=True)
            pieces = _split3(f)
            qa = jnp.where(head, q * ATT_SCALE, _put3(lane, e, 0, pieces, jnp.where(_spare3(lane, e, 3), 1.0, 0.0)))
            ones = jnp.where(_spare3(lane, e, 0) | _spare3(lane, e, 6), 1.0, 0.0)
            ka = jnp.where(head, k, _put3(lane, e, 3, [-x for x in pieces], ones))
            va = jnp.where(head, v, jnp.where(_spare3(lane, e, 0), 1.0, 0.0))
            qa_ref[e] = qa.astype(BF16)
            ka_ref[e] = ka.astype(BF16)
            va_ref[e] = va.astype(BF16)
            kat_ref[e] = ka.T.astype(BF16)

    qb, kb, vb = Z_Q // LANE, Z_K // LANE, Z_V // LANE
    heads = jax.ShapeDtypeStruct((N_HEAD, S, LANE), BF16)
    pair = pl.BlockSpec((2, S, LANE), lambda p: (p, 0, 0))
    return pl.pallas_call(
        body, name=name,
        out_shape=(heads, heads, heads, jax.ShapeDtypeStruct((N_HEAD, LANE, S), BF16)),
        grid=(N_PAIR,),
        in_specs=[pl.BlockSpec((S, LANE), lambda p: (0, qb + p)), pl.BlockSpec((S, LANE), lambda p: (0, kb + p)),
                  pl.BlockSpec((S, LANE), lambda p: (0, vb + p)), pl.BlockSpec((S, LANE), lambda p: (0, Z_F // LANE)),
                  pl.BlockSpec((8, LANE), lambda p: (0, 0))],
        out_specs=(pair, pair, pair, pl.BlockSpec((2, LANE, S), lambda p: (p, 0, 0))),
        scratch_shapes=[pltpu.VMEM((S, LANE), F32)],
        compiler_params=_params(("arbitrary",)),
    )(z, z, z, z, bf)


def _attn_bwd_prep(qa, o, lse, do, name):
    def body(qa_ref, o_ref, lse_ref, do_ref, qa2_ref, doa_ref):
        lane = lax.broadcasted_iota(jnp.int32, (S, LANE), 1)
        dov, ov, lsev = do_ref[...], o_ref[...], lse_ref[...]
        for e in range(2):
            head = (lane >= 64) if e else (lane < 64)
            dsum = jnp.sum(jnp.where(head, dov * ov, 0.0), axis=1, keepdims=True)
            doa_ref[e] = jnp.where(head, dov, _put3(lane, e, 0, [-x for x in _split3(dsum)], 0.0)).astype(BF16)
            lse_col = lsev[:, 64 * e:64 * e + 1]
            qa2_ref[e] = _put3(lane, e, 6, [-x for x in _split3(lse_col)], qa_ref[e].astype(F32)).astype(BF16)

    heads = jax.ShapeDtypeStruct((N_HEAD, S, LANE), BF16)
    pair = pl.BlockSpec((2, S, LANE), lambda p: (p, 0, 0))
    cols = pl.BlockSpec((S, LANE), lambda p: (0, p))
    return pl.pallas_call(
        body, name=name, out_shape=(heads, heads), grid=(N_PAIR,),
        in_specs=[pair, cols, cols, cols], out_specs=(pair, pair),
        compiler_params=_params(("parallel",)),
    )(qa, o, lse, do)


def _attn_bwd_post(z, bf, dqt, dka, dva, name):
    def body(f_ref, b_ref, dqt_ref, dk_ref, dv_ref, dq_out, dk_out, dv_out, dfl_ref, red_ref, dcum_ref):
        p = pl.program_id(0)

        @pl.when(p == 0)
        def _():
            dcum_ref[...] = jnp.zeros_like(dcum_ref)

        row = lax.broadcasted_iota(jnp.int32, (S, LANE), 0)
        lane = lax.broadcasted_iota(jnp.int32, (S, LANE), 1)
        dqa = [dqt_ref[e].T for e in range(2)]
        dq_out[...] = (jnp.where(lane < 64, dqa[0], dqa[1]) * ATT_SCALE).astype(BF16)
        dk_out[...] = jnp.where(lane < 64, dk_ref[0], dk_ref[1]).astype(BF16)
        dv_out[...] = jnp.where(lane < 64, dv_ref[0], dv_ref[1]).astype(BF16)
        for e in range(2):
            d_query = jnp.sum(jnp.where(_spare(lane, e, 0), dqa[e], 0.0), axis=1, keepdims=True)
            d_key = jnp.sum(jnp.where(_spare(lane, e, 3), dk_ref[e], 0.0), axis=1, keepdims=True)
            dcum_ref[...] += jnp.where(lane == 2 * p + e, d_query - d_key, 0.0)

        @pl.when(p == N_PAIR - 1)
        def _():
            dls = _cumsum_rows(dcum_ref[...], row, reverse=True)
            xv = f_ref[...] + b_ref[0:1, :]
            dx = jnp.where(lane < N_HEAD, dls * jax.nn.sigmoid(-xv), 0.0)
            dfl_ref[...] = dx.astype(BF16)
            red_ref[...] = jnp.zeros_like(red_ref)
            red_ref[0:1, :] = jnp.sum(dx, axis=0, keepdims=True)

    wide = jax.ShapeDtypeStruct((S, N_PAIR * LANE), BF16)
    cols = pl.BlockSpec((S, LANE), lambda p: (0, p))
    pair = pl.BlockSpec((2, S, LANE), lambda p: (p, 0, 0))
    return pl.pallas_call(
        body, name=name,
        out_shape=(wide, wide, wide, jax.ShapeDtypeStruct((S, LANE), BF16), jax.ShapeDtypeStruct((8, LANE), F32)),
        grid=(N_PAIR,),
        in_specs=[pl.BlockSpec((S, LANE), lambda p: (0, Z_F // LANE)), pl.BlockSpec((8, LANE), lambda p: (0, 0)),
                  pl.BlockSpec((2, LANE, S), lambda p: (p, 0, 0)), pair, pair],
        out_specs=(cols, cols, cols, pl.BlockSpec((S, LANE), lambda p: (0, 0)), pl.BlockSpec((8, LANE), lambda p: (0, 0))),
        scratch_shapes=[pltpu.VMEM((S, LANE), F32)],
        compiler_params=_params(("arbitrary",)),
    )(z, bf, dqt, dka, dva)


def _attn_fwd(qa, ka, va, name):
    tq, tk = TQ_FWD, TQ
    ratio = tq // tk

    def body(qa_ref, ka_ref, va_ref, o_ref, lse_ref):
        i = pl.program_id(1)
        lane = lax.broadcasted_iota(jnp.int32, (tq, LANE), 1)
        row = lax.broadcasted_iota(jnp.int32, (tq, tk), 0)
        col = lax.broadcasted_iota(jnp.int32, (tq, tk), 1)
        nh = HEADS_PER_STEP_FWD
        qs = [qa_ref[h] for h in range(nh)]

        def block(j, carry, masked):
            off = pl.multiple_of(j * tk, tk)
            out = []
            for h in range(nh):
                m, acc = carry[h]
                s = lax.dot_general(qs[h], ka_ref[h, pl.ds(off, tk), :], _NT, preferred_element_type=F32)
                if masked:
                    s = jnp.where(col + (j - ratio * i) * tk > row, NEG_INF, s)
                mn = jnp.maximum(m, jnp.max(s, axis=1, keepdims=True))
                p = jnp.exp(s - mn).astype(BF16)
                acc = jnp.exp(m - mn) * acc + jnp.dot(p, va_ref[h, pl.ds(off, tk), :], preferred_element_type=F32)
                out.append((mn, acc))
            return tuple(out)

        init = (jnp.full((tq, 1), NEG_INF, F32), jnp.zeros((tq, LANE), F32))
        carry = lax.fori_loop(0, ratio * i, lambda j, c: block(j, c, False), (init,) * nh)
        for d in range(ratio):
            carry = block(ratio * i + d, carry, True)
        res = []
        for h in range(nh):
            m, acc = carry[h]
            l = jnp.sum(jnp.where(_spare(lane, h % 2, 0), acc, 0.0), axis=1, keepdims=True)
            res.append((acc / l, m + jnp.log(l)))
        for g in range(nh // 2):
            o_ref[:, g * LANE:(g + 1) * LANE] = jnp.where(lane < 64, res[2 * g][0], res[2 * g + 1][0])
            lse_ref[:, g * LANE:(g + 1) * LANE] = jnp.where(lane < 64, res[2 * g][1], res[2 * g + 1][1])

    nh = HEADS_PER_STEP_FWD
    out = jax.ShapeDtypeStruct((S, N_PAIR * LANE), F32)
    wide = pl.BlockSpec((tq, 64 * nh), lambda p, i: (i, p))
    return pl.pallas_call(
        body, name=name, out_shape=(out, out), grid=(N_HEAD // nh, S // tq),
        in_specs=[pl.BlockSpec((nh, tq, LANE), lambda p, i: (p, i, 0)), pl.BlockSpec((nh, S, LANE), lambda p, i: (p, 0, 0)),
                  pl.BlockSpec((nh, S, LANE), lambda p, i: (p, 0, 0))],
        out_specs=(wide, wide),
        compiler_params=_params(("parallel", "parallel")),
    )(qa, ka, va)


def _attn_bwd(qa2, ka, va, kat, doa, name):
    nq = S // TQ

    def body(qa_ref, ka_ref, va_ref, kat_ref, doa_ref, dqt_ref, dk_ref, dv_ref):
        j = pl.program_id(1)

        @pl.when(j == 0)
        def _():
            dqt_ref[...] = jnp.zeros_like(dqt_ref)

        key = lax.broadcasted_iota(jnp.int32, (TQ, TQ), 0)
        qry = lax.broadcasted_iota(jnp.int32, (TQ, TQ), 1)
        nh = HEADS_PER_STEP
        kav, vav, katv = ([ref[h] for h in range(nh)] for ref in (ka_ref, va_ref, kat_ref))

        def block(i, carry, masked):
            off = pl.multiple_of(i * TQ, TQ)
            out = []
            for h in range(nh):
                dk_acc, dv_acc = carry[h]
                qav = qa_ref[h, pl.ds(off, TQ), :]
                doav = doa_ref[h, pl.ds(off, TQ), :]
                s_t = lax.dot_general(kav[h], qav, _NT, preferred_element_type=F32)
                if masked:
                    s_t = jnp.where(key > qry, NEG_INF, s_t)
                p_t = jnp.exp(s_t)
                ds_t = p_t * lax.dot_general(vav[h], doav, _NT, preferred_element_type=F32)
                dsb = ds_t.astype(BF16)
                dv_acc = dv_acc + jnp.dot(p_t.astype(BF16), doav, preferred_element_type=F32)
                dk_acc = dk_acc + jnp.dot(dsb, qav, preferred_element_type=F32)
                dqt_ref[h, :, pl.ds(off, TQ)] += jnp.dot(katv[h], dsb, preferred_element_type=F32)
                out.append((dk_acc, dv_acc))
            return tuple(out)

        zero = (jnp.zeros((TQ, LANE), F32), jnp.zeros((TQ, LANE), F32))
        carry = block(j, (zero,) * nh, True)
        carry = lax.fori_loop(j + 1, nq, lambda i, c: block(i, c, False), carry)
        for h in range(nh):
            dk_ref[h], dv_ref[h] = carry[h]

    nh = HEADS_PER_STEP
    full = pl.BlockSpec((nh, S, LANE), lambda p, j: (p, 0, 0))
    blk = pl.BlockSpec((nh, TQ, LANE), lambda p, j: (p, j, 0))
    acc = jax.ShapeDtypeStruct((N_HEAD, S, LANE), F32)
    return pl.pallas_call(
        body, name=name,
        out_shape=(jax.ShapeDtypeStruct((N_HEAD, LANE, S), F32), acc, acc),
        grid=(N_HEAD // nh, nq),
        in_specs=[full, blk, blk, pl.BlockSpec((nh, LANE, TQ), lambda p, j: (p, 0, j)), full],
        out_specs=(pl.BlockSpec((nh, LANE, S), lambda p, j: (p, 0, 0)), blk, blk),
        compiler_params=_params(("arbitrary", "arbitrary")),
    )(qa2, ka, va, kat, doa)


ADA_ROWS = 16


def _ada_fwd(c_pad, w_ada, b_cols, name):
    def body(c_ref, w_ref, b_ref, o_ref):
        cv = c_ref[...]
        sc = (cv * jax.nn.sigmoid(cv)).astype(BF16)
        o_ref[0] = jnp.dot(sc, w_ref[0].astype(BF16), preferred_element_type=F32) + b_ref[0, 0:1, :]

    return pl.pallas_call(
        body, name=name, out_shape=jax.ShapeDtypeStruct((DEPTH, ADA_ROWS, ADA_COLS), F32), grid=(DEPTH,),
        in_specs=[pl.BlockSpec((ADA_ROWS, D), lambda l: (0, 0)), pl.BlockSpec((1, D, ADA_COLS), lambda l: (l, 0, 0)),
                  pl.BlockSpec((1, 8, ADA_COLS), lambda l: (l, 0, 0))],
        out_specs=pl.BlockSpec((1, ADA_ROWS, ADA_COLS), lambda l: (l, 0, 0)),
        compiler_params=_params(("parallel",)),
    )(c_pad, w_ada, b_cols)


def _ada_bwd(c_pad, dmod_cols, name):
    def body(c_ref, d_ref, o_ref):
        cv = c_ref[...]
        sc = (cv * jax.nn.sigmoid(cv)).astype(BF16)
        o_ref[0] = lax.dot_general(sc, d_ref[0].astype(BF16), _TN, preferred_element_type=F32)

    return pl.pallas_call(
        body, name=name, out_shape=jax.ShapeDtypeStruct((DEPTH, D, ADA_COLS), F32), grid=(DEPTH,),
        in_specs=[pl.BlockSpec((ADA_ROWS, D), lambda l: (0, 0)), pl.BlockSpec((1, ADA_ROWS, ADA_COLS), lambda l: (l, 0, 0))],
        out_specs=pl.BlockSpec((1, D, ADA_COLS), lambda l: (l, 0, 0)),
        compiler_params=_params(("parallel",)),
    )(c_pad, dmod_cols)


def _adamw_math(w, g, m, v):
    m = B1 * m + (1.0 - B1) * g
    v = B2 * v + (1.0 - B2) * (g * g)
    m_hat = m / (1.0 - B1 ** STEP)
    v_hat = v / (1.0 - B2 ** STEP)
    delta = -LR * (m_hat / (jnp.sqrt(v_hat) + EPS) + WD * w)
    return delta, m, v


def _row_tile(rows, target=256):
    best = 8
    for t in range(8, min(rows, target) + 1, 8):
        if rows % t == 0:
            best = t
    return best


def _adamw(w, g, m, v, name):
    layers, rows, cols = w.shape
    tr = _row_tile(rows)
    spec = pl.BlockSpec((1, tr, cols), lambda l, i: (l, i, 0))

    def body(w_ref, g_ref, m_ref, v_ref, d_ref, nm_ref, nv_ref):
        d_ref[...], nm_ref[...], nv_ref[...] = _adamw_math(w_ref[...], g_ref[...], m_ref[...], v_ref[...])

    out = jax.ShapeDtypeStruct(w.shape, F32)
    return pl.pallas_call(
        body, name=name, out_shape=(out, out, out), grid=(layers, rows // tr),
        in_specs=[spec] * 4, out_specs=(spec,) * 3, compiler_params=_params(("parallel", "parallel")),
    )(w, g, m, v)


def _sum_slabs(x, name):
    n, rows, _ = x.shape
    tr = _row_tile(rows)

    def body(x_ref, o_ref):
        acc = x_ref[0]
        for k in range(1, n):
            acc = acc + x_ref[k]
        o_ref[...] = acc

    return pl.pallas_call(
        body, name=name, out_shape=jax.ShapeDtypeStruct((rows, D), F32), grid=(rows // tr,),
        in_specs=[pl.BlockSpec((n, tr, D), lambda i: (0, i, 0))], out_specs=pl.BlockSpec((tr, D), lambda i: (i, 0)),
        compiler_params=_params(("parallel",)),
    )(x)


_ANY = pl.BlockSpec(memory_space=pl.ANY)
MESH = pl.DeviceIdType.MESH


def _on_sequencer(body, out_shape, sems, operands, after, sequencer_id, name):
    n = len(operands)

    def ordered_body(*refs):
        body(*refs[:n], *refs[n + 1:])

    extra = [] if after is None else [after]
    return pl.kernel(
        body if after is None else ordered_body, out_type=out_shape,
        mesh=plsc.ScalarSubcoreMesh(axis_name="sequencer", num_cores=1), scratch_types=sems,
        compiler_params=pltpu.CompilerParams(collective_id=sequencer_id), name=name)(*operands, *extra)


def _all_gather(xs, name, sequencer_id=None, after=None):
    n = len(xs)

    def body(*refs):
        x_refs, out_refs = refs[:n], refs[n:2 * n]
        send_sems, recv_sems, local_sems = refs[2 * n:]
        x_, y_, c_ = lax.axis_index("x"), lax.axis_index("y"), lax.axis_index("c")
        me, sibling = (x_, y_, c_), (x_, y_, 1 - c_)
        chips = [(1 - x_, y_), (x_, 1 - y_), (1 - x_, 1 - y_)]
        if sequencer_id is not None:
            barrier = pltpu.get_barrier_semaphore()
            peers = [sibling] + [(*chip, pc) for chip in chips for pc in (c_, 1 - c_)]
            for peer in peers:
                pl.semaphore_signal(barrier, inc=1, device_id=peer, device_id_type=MESH)
            pl.semaphore_wait(barrier, len(peers))

        def slot(a, px, py, pc):
            return out_refs[a].at[4 * px + 2 * py + pc]

        def copy(a, k, block, to, src=None):
            return pltpu.make_async_remote_copy(
                src_ref=slot(a, *block) if src is None else src, dst_ref=slot(a, *block),
                send_sem=send_sems.at[7 * a + k], recv_sem=recv_sems.at[7 * a + k], device_id=to, device_id_type=MESH)

        mine = [pltpu.make_async_copy(x_refs[a], slot(a, *me), local_sems.at[a]) for a in range(n)]
        for cp in mine:
            cp.start()
        first = []
        for a in range(n):
            first.append(copy(a, 0, me, sibling, src=x_refs[a]))
            first += [copy(a, 1 + j, me, (*chip, c_), src=x_refs[a]) for j, chip in enumerate(chips)]
        for cp in first:
            cp.start()
        passed = []
        for j, chip in enumerate(chips):
            for a in range(n):
                copy(a, 1 + j, (*chip, c_), me).wait_recv()
                passed.append(copy(a, 4 + j, (*chip, c_), sibling))
                passed[-1].start()
        for a in range(n):
            copy(a, 0, sibling, me).wait_recv()
        for j, chip in enumerate(chips):
            for a in range(n):
                copy(a, 4 + j, (*chip, 1 - c_), me).wait_recv()
        for cp in first + passed:
            cp.wait_send()
        for cp in mine:
            cp.wait()

    out_shape = [jax.ShapeDtypeStruct((N_DEV,) + x.shape, x.dtype) for x in xs]
    sems = [pltpu.SemaphoreType.DMA((7 * n,)), pltpu.SemaphoreType.DMA((7 * n,)), pltpu.SemaphoreType.DMA((n,))]
    if sequencer_id is not None:
        return _on_sequencer(body, out_shape, sems, xs, after, sequencer_id, name)
    return pl.pallas_call(
        body, name=name, out_shape=out_shape, in_specs=[_ANY] * n, out_specs=[_ANY] * n, scratch_shapes=sems)(*xs)


def _sibling_exchange(gs, name, sequencer_id=None, after=None):
    n = len(gs)

    def body(*refs):
        g_refs, p_refs = refs[:n], refs[n:2 * n]
        send_sems, recv_sems = refs[2 * n:]
        x_, y_, c_ = lax.axis_index("x"), lax.axis_index("y"), lax.axis_index("c")
        if sequencer_id is not None:
            barrier = pltpu.get_barrier_semaphore()
            pl.semaphore_signal(barrier, inc=1, device_id=(x_, y_, 1 - c_), device_id_type=MESH)
            pl.semaphore_wait(barrier, 1)
        copies = [pltpu.make_async_remote_copy(
            src_ref=g_refs[a].at[2 * k + (1 - c_)], dst_ref=p_refs[a].at[k], send_sem=send_sems.at[4 * a + k],
            recv_sem=recv_sems.at[4 * a + k], device_id=(x_, y_, 1 - c_), device_id_type=MESH)
            for a in range(n) for k in range(4)]
        for cp in copies:
            cp.start()
        for cp in copies:
            cp.wait()

    out_shape = [jax.ShapeDtypeStruct((4,) + g.shape[1:], g.dtype) for g in gs]
    sems = [pltpu.SemaphoreType.DMA((4 * n,)), pltpu.SemaphoreType.DMA((4 * n,))]
    if sequencer_id is not None:
        return _on_sequencer(body, out_shape, sems, gs, after, sequencer_id, name)
    return pl.pallas_call(
        body, name=name, out_shape=out_shape, in_specs=[_ANY] * n, out_specs=[_ANY] * n, scratch_shapes=sems)(*gs)


def _slab_tiles(rows, cols):
    if rows % 8 == 0:
        return _row_tile(rows), cols
    return rows, 2 * LANE


def _pair_sums(g, p, route, name):
    _, rows, cols = g.shape
    tr, tc = _slab_tiles(rows, cols)

    def body(route_ref, g_ref, p_ref, t_ref):
        t_ref[...] = (g_ref[...].astype(F32) + p_ref[...].astype(F32)).astype(BF16)

    return pl.pallas_call(
        body, name=name, out_shape=jax.ShapeDtypeStruct((3, rows, cols), BF16),
        grid_spec=pltpu.PrefetchScalarGridSpec(
            num_scalar_prefetch=1, grid=(3, rows // tr, cols // tc),
            in_specs=[pl.BlockSpec((1, tr, tc), lambda r, i, j, route_ref: (2 * route_ref[1 + r] + route_ref[0], i, j)),
                      pl.BlockSpec((1, tr, tc), lambda r, i, j, route_ref: (route_ref[1 + r], i, j))],
            out_specs=pl.BlockSpec((1, tr, tc), lambda r, i, j, route_ref: (r, i, j))),
        compiler_params=_params(("parallel", "parallel", "parallel")),
    )(route, g, p)


def _chip_exchange(ts, name, sequencer_id=None, after=None):
    n = len(ts)

    def body(*refs):
        t_refs, l_refs = refs[:n], refs[n:2 * n]
        send_sems, recv_sems = refs[2 * n:]
        x_, y_, c_ = lax.axis_index("x"), lax.axis_index("y"), lax.axis_index("c")
        chips = [(1 - x_, y_), (x_, 1 - y_), (1 - x_, 1 - y_)]
        if sequencer_id is not None:
            barrier = pltpu.get_barrier_semaphore()
            for px, py in chips:
                pl.semaphore_signal(barrier, inc=1, device_id=(px, py, c_), device_id_type=MESH)
            pl.semaphore_wait(barrier, len(chips))
        copies = [pltpu.make_async_remote_copy(
            src_ref=t_refs[a].at[r], dst_ref=l_refs[a].at[r], send_sem=send_sems.at[3 * a + r],
            recv_sem=recv_sems.at[3 * a + r], device_id=(px, py, c_), device_id_type=MESH)
            for a in range(n) for r, (px, py) in enumerate(chips)]
        for cp in copies:
            cp.start()
        for cp in copies:
            cp.wait()

    out_shape = [jax.ShapeDtypeStruct((3,) + t.shape[1:], t.dtype) for t in ts]
    sems = [pltpu.SemaphoreType.DMA((3 * n,)), pltpu.SemaphoreType.DMA((3 * n,))]
    if sequencer_id is not None:
        return _on_sequencer(body, out_shape, sems, ts, after, sequencer_id, name)
    return pl.pallas_call(
        body, name=name, out_shape=out_shape, in_specs=[_ANY] * n, out_specs=[_ANY] * n, scratch_shapes=sems)(*ts)


def _reduce_adamw(gs, ps, landed, place, w, m, v, name):
    layers, rows, cols = w.shape
    assert layers == DEPTH == 2
    tr, tc = _slab_tiles(rows, cols)
    nr, nc = rows // tr, cols // tc
    spec = pl.BlockSpec((1, tr, tc), lambda l, i, j, place_ref: (l, i, j))

    def own(layer, which):
        pi, pj = (nr - 1, nc - 1) if layer == 0 else (0, 0)

        def index(l, i, j, place_ref):
            lead = 0 if which is None else place_ref[which]
            return lead, jnp.where(l == layer, i, pi), jnp.where(l == layer, j, pj)

        return pl.BlockSpec((3 if which is None else 1, tr, tc), index)

    def body(place_ref, g0_ref, p0_ref, l0_ref, g1_ref, p1_ref, l1_ref, w_ref, m_ref, v_ref,
             g_ref, d_ref, nm_ref, nv_ref):
        def update(own_ref, sib_ref, l_ref):
            g = (own_ref[0].astype(F32) + sib_ref[0].astype(F32) + l_ref[0].astype(F32) + l_ref[1].astype(F32)
                 + l_ref[2].astype(F32))
            g_ref[0] = g
            d_ref[0], nm_ref[0], nv_ref[0] = _adamw_math(w_ref[0], g, m_ref[0], v_ref[0])

        @pl.when(pl.program_id(0) == 0)
        def _():
            update(g0_ref, p0_ref, l0_ref)

        @pl.when(pl.program_id(0) == 1)
        def _():
            update(g1_ref, p1_ref, l1_ref)

    out = jax.ShapeDtypeStruct(w.shape, F32)
    return pl.pallas_call(
        body, name=name, out_shape=(out, out, out, out),
        grid_spec=pltpu.PrefetchScalarGridSpec(
            num_scalar_prefetch=1, grid=(DEPTH, nr, nc),
            in_specs=[own(0, 0), own(0, 1), own(0, None), own(1, 0), own(1, 1), own(1, None), spec, spec, spec],
            out_specs=(spec, spec, spec, spec)),
        compiler_params=_params(("arbitrary", "arbitrary", "arbitrary")),
    )(place, gs[0], ps[0], landed[0], gs[1], ps[1], landed[1], w, m, v)


def _pack(pieces, row_multiple, dtype, cols=D, rows=None):
    flat = jnp.concatenate([p.astype(dtype).reshape(-1) for p in pieces])
    if rows is None:
        rows = -(-flat.shape[0] // cols)
        rows = -(-rows // row_multiple) * row_multiple
    flat = jnp.pad(flat, (0, rows * cols - flat.shape[0]))
    return flat.reshape(rows, cols)


def _unpack(flat, shapes, lead=()):
    out, off = [], 0
    for shp in shapes:
        n = 1
        for s_ in shp:
            n *= s_
        out.append(lax.slice_in_dim(flat, off, off + n, axis=len(lead)).reshape(lead + tuple(shp)))
        off += n
    return out


WIN_STRIDE = 704
WIN_ROWS = 720
Z_TURN = 1544


def _window(wt, me, name):
    padded = jnp.pad(wt, ((0, 0), (0, WIN_ROWS - IN_SHARD), (0, 0)))

    def body(me_ref, x_ref, o_ref):
        o_ref[0] = pltpu.roll(x_ref[0], me_ref[0], axis=0).astype(BF16)

    spec = pl.BlockSpec((1, WIN_ROWS, D), lambda l, me_ref: (l, 0, 0))
    return pl.pallas_call(
        body, name=name, out_shape=jax.ShapeDtypeStruct((DEPTH, WIN_ROWS, D), BF16),
        grid_spec=pltpu.PrefetchScalarGridSpec(num_scalar_prefetch=1, grid=(DEPTH,), in_specs=[spec], out_specs=spec),
        compiler_params=_params(("parallel",)),
    )(me, padded)


def _z_rows_from_windows(win):
    over = WIN_ROWS - WIN_STRIDE
    pieces = [(0, win[0][0:WIN_STRIDE])]
    for d in range(1, N_DEV):
        base = WIN_STRIDE * d
        pieces.append((base, win[d - 1][WIN_STRIDE:WIN_ROWS] + win[d][0:over]))
        pieces.append((base + over, win[d][over:WIN_STRIDE]))
    pieces.append((WIN_STRIDE * N_DEV, win[N_DEV - 1][WIN_STRIDE:WIN_ROWS]))

    def rows(a, b):
        out = []
        for start, arr in pieces:
            lo, hi = max(a, start), min(b, start + arr.shape[0])
            if lo < hi:
                out.append(arr[lo - start:hi - start])
        return out

    pad = jnp.zeros((NZ - IN_COLS, win.shape[-1]), win.dtype)
    return jnp.concatenate(rows(Z_TURN, IN_COLS) + rows(0, Z_TURN) + [pad], axis=0)


def _in_rows_from_z(wt):
    return jnp.concatenate([wt[Z_Q:Z_Q + 1536], wt[Z_F:Z_F + 8], wt[Z_PC:Z_PC + 1024], wt[Z_G:Z_G + 3072]], axis=0)


def _pad_rows(v, rows=8):
    return jnp.pad(v, ((0, rows - v.shape[0]), (0, 0)))


def _layer_fwd(l, x, wts, gvec, mod):
    tag = f"l{l}"
    h = _prenorm_fwd(x, gvec, mod, 0, 0, 1, f"prenorm_mix_{tag}")
    z = _matmul(h, wts["w_in_t"], "nt", f"in_proj_{tag}", tn=1152)
    qa, ka, va, kat = _attn_prep(z, wts["b_f"], f"attn_prep_{tag}")
    qa = wts["arrive"](qa)
    o, lse = _attn_fwd(qa, ka, va, f"attn_{tag}")
    br_b = _pool_fwd(z, wts["wp_bd"], wts["pool_scale"], f"pool_{tag}")
    br_c = _conv_fwd(z, wts["conv_w"], f"conv_{tag}")
    pa = _matmul(o, wts["wa"], "nn", f"proj_a_{tag}", out_dtype=BF16)
    pb = _matmul(br_b, wts["wb"], "nn", f"proj_b_{tag}", out_dtype=BF16)
    gates = [(z, Z_G + k * D) for k in range(3)]
    pc, merged = _matmul(br_c, wts["wc"], "nn", f"proj_c_merge_{tag}", tm=1024, tn=512,
                         extra=gates + [(pa, 0), (pb, 0)], epilogue=_merge_epilogue, out_dtypes=(BF16, BF16))
    y = _matmul(merged, wts["w_out"], "nn", f"out_proj_{tag}")
    x1 = _postnorm_fwd(x, y, gvec, mod, 1, 2, f"postnorm_mix_{tag}")
    h2 = _prenorm_fwd(x1, gvec, mod, 2, 3, 4, f"prenorm_ff_{tag}")
    a, r = _matmul(h2, wts["w_ff1"], "nn", f"ff1_{tag}", b_col_shards=True, epilogue=_relu2_epilogue,
                   out_dtypes=(BF16, BF16))
    y2 = _matmul(r, wts["w_ff2"], "nn", f"ff2_{tag}", tk=1024)
    x2 = _postnorm_fwd(x1, y2, gvec, mod, 3, 5, f"postnorm_ff_{tag}")
    saved = dict(x=x, h=h, z=z, qa=qa, ka=ka, va=va, kat=kat, o=o, lse=lse, br_b=br_b, br_c=br_c, pa=pa, pb=pb, pc=pc,
                 merged=merged, y=y, x1=x1, h2=h2, a=a, r=r, y2=y2)
    return x2, saved


def _ffn_bwd(l, dx2, sv, wts, gvec, mod, midpoint):
    tag = f"l{l}"
    dy2, red_post_ff = _postnorm_bwd(sv["y2"], gvec, mod, dx2, 3, 5, f"postnorm_ff_bwd_{tag}")
    dy2 = midpoint(dy2)
    da = _matmul(dy2, wts["w_ff2"], "nt", f"ff2_dx_{tag}", extra=[(sv["a"], 0)], epilogue=_relu2_bwd_epilogue,
                 out_dtypes=(BF16,))[0]
    d_w_ff2 = _matmul(sv["r"], dy2, "tn", f"ff2_dw_{tag}", out_dtype=GRAD_DTYPE)
    dh2 = _matmul(da, wts["w_ff1"], "nt", f"ff1_dx_{tag}", b_col_shards=True)
    d_w_ff1 = _matmul(sv["h2"], da, "tn", f"ff1_dw_{tag}", out_dtype=GRAD_DTYPE, out_col_shards=True)
    dx1, red_pre_ff = _prenorm_bwd(sv["x1"], gvec, mod, dh2, dx2, 2, 4, f"prenorm_ff_bwd_{tag}")
    return dx1, [d_w_ff1, d_w_ff2.reshape(N_DEV, D_FF // N_DEV, D)], (red_pre_ff, red_post_ff)


def _mixer_bwd(l, dx1, sv, wts, gvec, mod, ffn_reds, midpoint):
    tag = f"l{l}"
    red_pre_ff, red_post_ff = ffn_reds
    dy, red_post_mix = _postnorm_bwd(sv["y"], gvec, mod, dx1, 1, 2, f"postnorm_mix_bwd_{tag}")
    gates = [(sv["z"], Z_G + k * D) for k in range(3)]
    dpa, dpb, dpc, *dgl = _matmul(dy, wts["w_out"], "nt", f"out_proj_dx_{tag}", tm=1024, tn=512,
                                  extra=gates + [(sv["pa"], 0), (sv["pb"], 0), (sv["pc"], 0)],
                                  epilogue=_merge_bwd_epilogue, out_dtypes=(BF16,) * 6)
    d_w_out = _matmul(sv["merged"], dy, "tn", f"out_proj_dw_{tag}", out_dtype=GRAD_DTYPE)
    dpa = midpoint(dpa)
    do = _matmul(dpa, wts["wa"], "nt", f"proj_a_dx_{tag}")
    dbr_b = _matmul(dpb, wts["wb"], "nt", f"proj_b_dx_{tag}")
    dbr_c = _matmul(dpc, wts["wc"], "nt", f"proj_c_dx_{tag}")
    d_wa = _matmul(sv["o"], dpa, "tn", f"proj_a_dw_{tag}", out_dtype=GRAD_DTYPE)
    d_wb = _matmul(sv["br_b"], dpb, "tn", f"proj_b_dw_{tag}", out_dtype=GRAD_DTYPE)
    d_wc = _matmul(sv["br_c"], dpc, "tn", f"proj_c_dw_{tag}", out_dtype=GRAD_DTYPE)
    d_w_branch = jnp.concatenate([d_wa, d_wb, d_wc], axis=0)

    dpu, d_wp_bd, red_pool = _pool_bwd(sv["z"], wts["wp_bd"], wts["pool_scale"], dbr_b, f"pool_bwd_{tag}")
    dconv, red_conv = _conv_bwd(sv["z"], wts["conv_w"], dbr_c, f"conv_bwd_{tag}")
    qa2, doa = _attn_bwd_prep(sv["qa"], sv["o"], sv["lse"], do, f"attn_bwd_prep_{tag}")
    dqt, dka, dva = _attn_bwd(qa2, sv["ka"], sv["va"], sv["kat"], doa, f"attn_bwd_{tag}")
    dq, dk, dv, dfl, red_f = _attn_bwd_post(sv["z"], wts["b_f"], dqt, dka, dva, f"attn_bwd_post_{tag}")
    dz = jnp.concatenate([dpu, dconv, *dgl, dq, dk, dv, dfl], axis=1)
    dh = _matmul(dz, wts["w_in_t"], "nn", f"in_proj_dx_{tag}", tm=1024, tk=1920)
    d_w_in_t = _matmul(dz, sv["h"], "tn", f"in_proj_dw_{tag}", out_dtype=GRAD_DTYPE, tm=1152)
    dx0, red_pre_mix = _prenorm_bwd(sv["x"], gvec, mod, dh, dx1, 0, 1, f"prenorm_mix_bwd_{tag}")

    rows = D // N_DEV
    big = [_in_rows_from_z(d_w_in_t).reshape(N_DEV, IN_SHARD, D), d_w_branch.reshape(N_DEV, rows, D),
           d_w_out.reshape(N_DEV, rows, D)]
    d_w_pool = jnp.stack([d_wp_bd[64 * g:64 * (g + 1), 64 * g:64 * (g + 1)] for g in range(4)])
    small = dict(
        mod=jnp.stack([red_pre_mix[0], red_pre_mix[1], red_post_mix[0], red_pre_ff[0], red_pre_ff[1], red_post_ff[0]]),
        g_mix_pre=red_pre_mix[2], g_mix_post=red_post_mix[1], g_ff_pre=red_pre_ff[2], g_ff_post=red_post_ff[1],
        b_f=red_f[0, 0:8], w_pool=d_w_pool, pool_scale=red_pool[0], conv_w=red_conv[0:3])
    return dx0, big, small


SMALL_KEYS = ["mod", "g_mix_pre", "g_mix_post", "g_ff_pre", "g_ff_post", "b_f", "w_pool", "pool_scale", "conv_w"]
SMALL_SHAPES = [(DEPTH, 6 * D), (DEPTH, D), (DEPTH, D), (DEPTH, D), (DEPTH, D), (DEPTH, 8), (DEPTH, 4, 64, 64),
                (DEPTH, POOL_W), (DEPTH, 3, CONV_W)]


def kernel(x, c, w_ada, b_ada, g_mix_pre, g_mix_post, g_ff_pre, g_ff_post, w_in, b_f, w_pool, pool_scale, conv_w, w_branch, w_out, w_ff1, w_ff2, loss_target, m_w_ada, m_b_ada, m_g_mix_pre, m_g_mix_post, m_g_ff_pre, m_g_ff_post, m_w_in, m_b_f, m_w_pool, m_pool_scale, m_conv_w, m_w_branch, m_w_out, m_w_ff1, m_w_ff2, v_w_ada, v_b_ada, v_g_mix_pre, v_g_mix_post, v_g_ff_pre, v_g_ff_post, v_w_in, v_b_f, v_w_pool, v_pool_scale, v_conv_w, v_w_branch, v_w_out, v_w_ff1, v_w_ff2):
    ix, iy, ic = lax.axis_index("x"), lax.axis_index("y"), lax.axis_index("c")
    me = 4 * ix + 2 * iy + ic
    route = jnp.stack([ic, 2 * (1 - ix) + iy, 2 * ix + (1 - iy), 2 * (1 - ix) + (1 - iy)]).astype(jnp.int32)
    place = jnp.stack([me, 2 * ix + iy]).astype(jnp.int32)
    wt_in, mt_in, vt_in = (jnp.transpose(a, (0, 2, 1)) for a in (w_in, m_w_in, v_w_in))

    c_all = _all_gather([_pad_rows(c)], "gather_c")[0][:, 0, :]
    c_pad = _pad_rows(c_all, ADA_ROWS)
    b_cols = lax.dynamic_slice_in_dim(b_ada, me * ADA_COLS, ADA_COLS, axis=1)
    b_cols = jnp.broadcast_to(b_cols[:, None, :], (DEPTH, 8, ADA_COLS))
    mod_part = _ada_fwd(c_pad, w_ada, b_cols, "ada_fwd")
    mod_all = _all_gather([mod_part.reshape(DEPTH * ADA_ROWS, ADA_COLS)], "gather_mod")[0]
    mod_all = mod_all.reshape(N_DEV, DEPTH, ADA_ROWS, ADA_COLS)
    mod_mine = lax.dynamic_index_in_dim(mod_all, me, axis=2, keepdims=False)
    mod_mine = jnp.transpose(mod_mine, (1, 0, 2)).reshape(DEPTH, 6, D)

    cw_cols = CONV_W // N_DEV
    cw_send = jnp.pad(conv_w.reshape(DEPTH * 3, cw_cols), ((0, 8 - DEPTH * 3), (0, LANE - cw_cols)))
    win_in = _window(wt_in, place[0:1], "w_in_window")
    send = [[w[l].astype(BF16) for w in (win_in, w_branch, w_out, w_ff1, w_ff2)] for l in range(DEPTH)]
    first = _all_gather(send[0][:1], "gather_weights_l0_in", sequencer_id=1, after=mod_all)
    rest = _all_gather(send[0][1:] + [cw_send], "gather_weights_l0_rest", sequencer_id=2, after=first[0])
    first1 = _all_gather(send[1][:1], "gather_weights_l1_in", sequencer_id=3, after=first[0])
    rest1 = _all_gather(send[1][1:], "gather_weights_l1_rest", sequencer_id=12, after=first[0])
    first, (mt_in, vt_in) = lax.optimization_barrier((first, (mt_in, vt_in)))
    gathered = [first + rest[:4], first1 + rest1]
    cw_all = rest[4][:, :DEPTH * 3, :cw_cols].reshape(N_DEV, DEPTH, 3, cw_cols)

    def first_operands(l, p_in):
        wp_bd = jnp.zeros((POOL_W, POOL_W), F32)
        for g in range(4):
            wp_bd = wp_bd.at[64 * g:64 * (g + 1), 64 * g:64 * (g + 1)].set(w_pool[l, g])
        return dict(w_in_t=_z_rows_from_windows(p_in), wp_bd=wp_bd.astype(BF16),
                    pool_scale=_pad_rows(pool_scale[l][None, :]), b_f=_pad_rows(jnp.pad(b_f[l], (0, LANE - 8))[None, :]))

    def rest_operands(l, rest):
        p_br, p_out, p_ff1, p_ff2 = rest
        w_br_full = p_br.reshape(D, D)
        cw_full = jnp.transpose(cw_all[:, l], (1, 0, 2)).reshape(3, CONV_W)
        return dict(wa=w_br_full[0:A_WIDTH], wb=w_br_full[A_WIDTH:A_WIDTH + POOL_W], wc=w_br_full[A_WIDTH + POOL_W:],
                    w_out=p_out.reshape(D, D), w_ff1=p_ff1, w_ff2=p_ff2.reshape(D_FF, D), conv_w=_pad_rows(cw_full))

    xs = x[0]
    saved, layers = [], []
    for l in range(DEPTH):
        p_in, rest = gathered[l][0], gathered[l][1:5]
        if l > 0:
            xs, p_in = lax.optimization_barrier((xs, p_in))
        wts = first_operands(l, p_in)

        def arrive(t, l=l, rest=rest, wts=wts):
            if l > 0:
                t, rest = lax.optimization_barrier((t, rest))
            wts.update(rest_operands(l, rest))
            return t

        wts["arrive"] = arrive
        gvec = _pad_rows(jnp.stack([g_mix_pre[l], g_mix_post[l], g_ff_pre[l], g_ff_post[l]]))
        layers.append((wts, gvec, _pad_rows(mod_mine[l])))
        xs, sv = _layer_fwd(l, xs, *layers[l])
        saved.append(sv)
    dx, loss_part = _loss_head(xs, loss_target[0], "loss_head")
    small_grads = [None] * DEPTH
    mine, sibs, landed = ({} for _ in range(3))
    seq_id = iter(range(4, 4 + 4 * DEPTH))
    last = [gathered[DEPTH - 1][1]]

    def start(group, grads):
        mine[group] = grads
        sibs[group] = _sibling_exchange(grads, f"rs_sibling_{group}", sequencer_id=next(seq_id), after=last[0])
        last[0] = sibs[group][0]

    def finish(group, later):
        later, (grads, sib) = lax.optimization_barrier((later, (mine[group], sibs[group])))
        sends = [_pair_sums(g, p, route, f"rs_pair_sums_{group}_{k}") for k, (g, p) in enumerate(zip(grads, sib))]
        later, sends = lax.optimization_barrier((later, sends))
        landed[group] = _chip_exchange(sends, f"rs_chips_{group}", sequencer_id=next(seq_id), after=last[0])
        last[0] = landed[group][0]
        return later

    pending = None
    for l in reversed(range(DEPTH)):
        hook = (lambda da: da) if pending is None else functools.partial(finish, pending)
        dx, ffn_grads, ffn_reds = _ffn_bwd(l, dx, saved[l], *layers[l], hook)
        start(f"ffn_l{l}", ffn_grads)
        dx, mix_grads, small_grads[l] = _mixer_bwd(l, dx, saved[l], *layers[l], ffn_reds,
                                                   functools.partial(finish, f"ffn_l{l}"))
        start(f"mix_l{l}", mix_grads)
        pending = f"mix_l{l}"
    grad_x = dx[None]

    big_w = [wt_in, w_branch, w_out, w_ff1, w_ff2]
    big_m = [mt_in, m_w_branch, m_w_out, m_w_ff1, m_w_ff2]
    big_v = [vt_in, v_w_branch, v_w_out, v_w_ff1, v_w_ff2]
    where = [("mix", 0), ("mix", 1), ("mix", 2), ("ffn", 0), ("ffn", 1)]

    def reduce_and_update(k):
        group, at = where[k]
        return _reduce_adamw([mine[f"{group}_l{l}"][at] for l in range(DEPTH)],
                             [sibs[f"{group}_l{l}"][at] for l in range(DEPTH)],
                             [landed[f"{group}_l{l}"][at] for l in range(DEPTH)], place, big_w[k], big_m[k], big_v[k],
                             f"rs_sum_adamw_{k}")

    big_res = {k: list(reduce_and_update(k)) for k in (3, 4)}
    big_res[3][0] = finish(pending, big_res[3][0])

    small = {k: jnp.stack([small_grads[l][k] for l in range(DEPTH)]) for k in SMALL_KEYS}
    payload = _pack([small[k] for k in SMALL_KEYS] + [loss_part[0:1, 0:1]], 8, F32)
    small_all = _all_gather([payload], "gather_small")[0]
    dmod_all = small_all[:, 0:DEPTH * 6, :].reshape(N_DEV, DEPTH, 6 * D)
    summed = _unpack(_sum_slabs(small_all, "sum_small").reshape(-1), SMALL_SHAPES + [(1, 1)])
    sg = dict(zip(SMALL_KEYS, summed))
    loss = summed[-1][0, 0]
    dmod_cols = lax.dynamic_slice_in_dim(dmod_all, me * ADA_COLS, ADA_COLS, axis=2)
    dmod_cols = jnp.pad(jnp.transpose(dmod_cols, (1, 0, 2)), ((0, 0), (0, ADA_ROWS - N_DEV), (0, 0)))
    g_w_ada = _ada_bwd(c_pad, dmod_cols, "ada_bwd")
    g_conv_w = lax.dynamic_slice_in_dim(sg["conv_w"], me * (CONV_W // N_DEV), CONV_W // N_DEV, axis=2)

    ada_out = [g_w_ada] + list(_adamw(w_ada, g_w_ada, m_w_ada, v_w_ada, "adamw_ada"))
    rest_w = [b_ada, g_mix_pre, g_mix_post, g_ff_pre, g_ff_post, b_f, w_pool, pool_scale, conv_w]
    rest_m = [m_b_ada, m_g_mix_pre, m_g_mix_post, m_g_ff_pre, m_g_ff_post, m_b_f, m_w_pool, m_pool_scale, m_conv_w]
    rest_v = [v_b_ada, v_g_mix_pre, v_g_mix_post, v_g_ff_pre, v_g_ff_post, v_b_f, v_w_pool, v_pool_scale, v_conv_w]
    rest_g = [sg["mod"], sg["g_mix_pre"], sg["g_mix_post"], sg["g_ff_pre"], sg["g_ff_post"], sg["b_f"],
              sg["w_pool"], sg["pool_scale"], g_conv_w]
    rest_shapes = [a.shape for a in rest_w]
    upd = _adamw(_pack(rest_w, 8, F32)[None], _pack(rest_g, 8, F32)[None], _pack(rest_m, 8, F32)[None],
                 _pack(rest_v, 8, F32)[None], "adamw_rest")
    rest_out = [rest_g] + [_unpack(arr.reshape(-1), rest_shapes) for arr in upd]
    rest_out = [[ada_out[which]] + rest_out[which] for which in range(4)]

    landed[pending], rest_out = lax.optimization_barrier((landed[pending], rest_out))
    big_res.update({k: reduce_and_update(k) for k in (0, 1, 2)})
    big_out = [[jnp.transpose(big_res[k][which], (0, 2, 1)) if k == 0 else big_res[k][which] for k in range(5)]
               for which in range(4)]

    def ordered(k):
        r, b = rest_out[k], big_out[k]
        return [r[0], r[1], r[2], r[3], r[4], r[5], b[0], r[6], r[7], r[8], r[9], b[1], b[2], b[3], b[4]]

    return (loss, grad_x, *ordered(0), *ordered(1), *ordered(2), *ordered(3))
```

```python
import functools

import jax
import jax.numpy as jnp
from jax import lax
from jax.experimental import pallas as pl
from jax.experimental.pallas import tpu as pltpu
from jax.experimental.pallas import tpu_sc as plsc

F32 = jnp.float32
BF16 = jnp.bfloat16
GRAD_DTYPE = BF16

N_DEV = 8
D = 1024
S = 2048
DEPTH = 2
D_FF = 4 * D
A_WIDTH = 512
HEAD_DIM = 64
N_PAIR = 4
POOL_W = 256
CONV_W = 256
IN_COLS = 5640
ADA_COLS = 6 * D // N_DEV
IN_SHARD = IN_COLS // N_DEV
RMS_EPS = 1e-6
NEG_INF = -1e30
ATT_SCALE = HEAD_DIM ** -0.5

NZ = 5760
Z_PC = 0
Z_G = 1024
Z_Q = 4096
Z_K = 4608
Z_V = 5120
Z_F = 5632

LR, B1, B2, EPS, WD, STEP = 0.001, 0.9, 0.999, 1e-08, 0.01, 10

LANE = 128
VMEM_LIMIT_BYTES = 48 * 1024 * 1024
TS = 1024
TQ = 256
TQ_FWD = 512
HEADS_PER_STEP = 8
HEADS_PER_STEP_FWD = 8


def _params(sem=None):
    return pltpu.CompilerParams(dimension_semantics=sem, vmem_limit_bytes=VMEM_LIMIT_BYTES)


def _pick(n, target):
    best = None
    for t in range(LANE, min(n, target) + 1, LANE):
        if n % t == 0:
            best = t
    return n if best is None else best


def _matmul(a, b, mode, name, out_dtype=F32, tm=2048, tn=1024, tk=2048, b_col_shards=False, out_col_shards=False,
            extra=(), epilogue=None, out_dtypes=None):
    if b_col_shards:
        shards, b_rows, shard_cols = b.shape
        b_shape = (b_rows, shards * shard_cols)
    else:
        b_shape = b.shape
    if mode == "nn":
        (m, k), (k2, n) = a.shape, b_shape
    elif mode == "nt":
        (m, k), (n, k2) = a.shape, b_shape
    else:
        (k, m), (k2, n) = a.shape, b_shape
    assert k == k2, (a.shape, b.shape, mode)
    tm, tn, tk = _pick(m, tm), _pick(n, tn), _pick(k, tk)
    if b_col_shards and mode == "nn":
        tn = shard_cols
    per_step = 1
    if b_col_shards and mode == "nt":
        per_step = max(1, min(tk, 1024) // shard_cols)
        tk = per_step * shard_cols
    if out_col_shards:
        tn = n // N_DEV
    nk = k // tk
    if mode == "nn":
        a_spec = pl.BlockSpec((tm, tk), lambda i, j, kk: (i, kk))
        b_spec = (pl.BlockSpec((None, tk, tn), lambda i, j, kk: (j, kk, 0)) if b_col_shards else
                  pl.BlockSpec((tk, tn), lambda i, j, kk: (kk, j)))
        dims = (((1,), (0,)), ((), ()))
    elif mode == "nt":
        a_spec = pl.BlockSpec((tm, tk), lambda i, j, kk: (i, kk))
        b_spec = (pl.BlockSpec((per_step, tn, shard_cols), lambda i, j, kk: (kk, j, 0)) if b_col_shards else
                  pl.BlockSpec((tn, tk), lambda i, j, kk: (j, kk)))
        dims = (((1,), (1,)), ((), ()))
    else:
        assert not b_col_shards
        a_spec = pl.BlockSpec((tk, tm), lambda i, j, kk: (kk, i))
        b_spec = pl.BlockSpec((tk, tn), lambda i, j, kk: (kk, j))
        dims = (((0,), (0,)), ((), ()))
    if out_col_shards:
        out_shape = jax.ShapeDtypeStruct((N_DEV, m, tn), out_dtype)
        out_spec = pl.BlockSpec((None, tm, tn), lambda i, j, kk: (j, i, 0))
    else:
        out_shape = jax.ShapeDtypeStruct((m, n), out_dtype)
        out_spec = pl.BlockSpec((tm, tn), lambda i, j, kk: (i, j))

    n_extra = len(extra)
    extra_specs = [pl.BlockSpec((tm, tn), lambda i, j, kk, off=off: (i, j + off // tn)) for _, off in extra]
    if epilogue is not None:
        assert not out_col_shards and all(off % tn == 0 for _, off in extra)
        out_shape = [jax.ShapeDtypeStruct((m, n), dt) for dt in out_dtypes]
        out_spec = [pl.BlockSpec((tm, tn), lambda i, j, kk: (i, j)) for _ in out_dtypes]

    def product(a_ref, b_ref):
        if b_col_shards and mode == "nt":
            b_tile = jnp.concatenate([b_ref[s] for s in range(per_step)], axis=1) if per_step > 1 else b_ref[0]
        else:
            b_tile = b_ref[...]
        return lax.dot_general(a_ref[...].astype(BF16), b_tile.astype(BF16), dims, preferred_element_type=F32)

    def write(acc, extra_refs, o_refs):
        if epilogue is None:
            o_refs[0][...] = acc.astype(out_dtype)
        else:
            for o_ref, tile in zip(o_refs, epilogue(acc, *[r[...] for r in extra_refs])):
                o_ref[...] = tile.astype(o_ref.dtype)

    def body_one_pass(a_ref, b_ref, *refs):
        write(product(a_ref, b_ref), refs[:n_extra], refs[n_extra:])

    def body(a_ref, b_ref, *refs):
        acc_ref = refs[-1]
        kk = pl.program_id(2)

        @pl.when(kk == 0)
        def _():
            acc_ref[...] = product(a_ref, b_ref)

        @pl.when(kk > 0)
        def _():
            acc_ref[...] += product(a_ref, b_ref)

        @pl.when(kk == nk - 1)
        def _():
            write(acc_ref[...], refs[:n_extra], refs[n_extra:-1])

    return pl.pallas_call(
        body_one_pass if nk == 1 else body, name=name,
        out_shape=out_shape,
        grid=(m // tm, n // tn, nk),
        in_specs=[a_spec, b_spec] + extra_specs,
        out_specs=out_spec,
        scratch_shapes=[] if nk == 1 else [pltpu.VMEM((tm, tn), F32)],
        compiler_params=_params(("parallel", "parallel", "arbitrary")),
    )(a, b, *[x for x, _ in extra])


def _row_spec(width=D, col=0):
    return pl.BlockSpec((TS, width), lambda i: (i, col))


def _vec_spec(rows=8, width=D):
    return pl.BlockSpec((rows, width), lambda i: (0, 0))


def _rms(x):
    return lax.rsqrt(jnp.mean(x * x, axis=-1, keepdims=True) + RMS_EPS)


def _prenorm_fwd(x, gvec, mod, g_row, shift_row, scale_row, name):
    def body(x_ref, g_ref, mod_ref, h_ref):
        xv = x_ref[...]
        y = xv * _rms(xv) * g_ref[g_row:g_row + 1, :]
        h = y * (1.0 + mod_ref[scale_row:scale_row + 1, :]) + mod_ref[shift_row:shift_row + 1, :]
        h_ref[...] = h.astype(BF16)

    return pl.pallas_call(
        body, name=name, out_shape=jax.ShapeDtypeStruct((S, D), BF16), grid=(S // TS,),
        in_specs=[_row_spec(), _vec_spec(), _vec_spec()], out_specs=_row_spec(),
        compiler_params=_params(("parallel",)),
    )(x, gvec, mod)


def _prenorm_bwd(x, gvec, mod, dh, dres, g_row, scale_row, name):
    def body(x_ref, g_ref, mod_ref, dh_ref, dres_ref, dx_ref, red_ref):
        i = pl.program_id(0)

        @pl.when(i == 0)
        def _():
            red_ref[...] = jnp.zeros_like(red_ref)

        xv = x_ref[...]
        g = g_ref[g_row:g_row + 1, :]
        r = _rms(xv)
        n = xv * r
        yg = n * g
        dhv = dh_ref[...]
        dyg = dhv * (1.0 + mod_ref[scale_row:scale_row + 1, :])
        dn = dyg * g
        dx = r * (dn - n * jnp.mean(dn * n, axis=-1, keepdims=True))
        dx_ref[...] = dres_ref[...] + dx
        red_ref[0:1, :] += jnp.sum(dhv, axis=0, keepdims=True)
        red_ref[1:2, :] += jnp.sum(dhv * yg, axis=0, keepdims=True)
        red_ref[2:3, :] += jnp.sum(dyg * n, axis=0, keepdims=True)

    return pl.pallas_call(
        body, name=name,
        out_shape=(jax.ShapeDtypeStruct((S, D), F32), jax.ShapeDtypeStruct((8, D), F32)),
        grid=(S // TS,),
        in_specs=[_row_spec(), _vec_spec(), _vec_spec(), _row_spec(), _row_spec()],
        out_specs=(_row_spec(), _vec_spec()),
        compiler_params=_params(("arbitrary",)),
    )(x, gvec, mod, dh, dres)


def _postnorm_fwd(x, y, gvec, mod, g_row, gate_row, name):
    def body(x_ref, y_ref, g_ref, mod_ref, o_ref):
        yv = y_ref[...]
        yn = yv * _rms(yv) * g_ref[g_row:g_row + 1, :]
        o_ref[...] = x_ref[...] + mod_ref[gate_row:gate_row + 1, :] * yn

    return pl.pallas_call(
        body, name=name, out_shape=jax.ShapeDtypeStruct((S, D), F32), grid=(S // TS,),
        in_specs=[_row_spec(), _row_spec(), _vec_spec(), _vec_spec()], out_specs=_row_spec(),
        compiler_params=_params(("parallel",)),
    )(x, y, gvec, mod)


def _postnorm_bwd(y, gvec, mod, dxo, g_row, gate_row, name):
    def body(y_ref, g_ref, mod_ref, dxo_ref, dy_ref, red_ref):
        i = pl.program_id(0)

        @pl.when(i == 0)
        def _():
            red_ref[...] = jnp.zeros_like(red_ref)

        yv = y_ref[...]
        g = g_ref[g_row:g_row + 1, :]
        r = _rms(yv)
        n = yv * r
        dxo = dxo_ref[...]
        dyn = dxo * mod_ref[gate_row:gate_row + 1, :]
        dn = dyn * g
        dy = r * (dn - n * jnp.mean(dn * n, axis=-1, keepdims=True))
        dy_ref[...] = dy.astype(BF16)
        red_ref[0:1, :] += jnp.sum(dxo * (n * g), axis=0, keepdims=True)
        red_ref[1:2, :] += jnp.sum(dyn * n, axis=0, keepdims=True)

    return pl.pallas_call(
        body, name=name,
        out_shape=(jax.ShapeDtypeStruct((S, D), BF16), jax.ShapeDtypeStruct((8, D), F32)),
        grid=(S // TS,),
        in_specs=[_row_spec(), _vec_spec(), _vec_spec(), _row_spec()],
        out_specs=(_row_spec(), _vec_spec()),
        compiler_params=_params(("arbitrary",)),
    )(y, gvec, mod, dxo)


def _loss_head(xf, target, name):
    def body(x_ref, t_ref, dx_ref, loss_ref):
        i = pl.program_id(0)

        @pl.when(i == 0)
        def _():
            loss_ref[...] = jnp.zeros_like(loss_ref)

        e = x_ref[...] - t_ref[...]
        dx_ref[...] = e / float(D)
        per_tok = jnp.mean(e * e, axis=-1, keepdims=True)
        loss_ref[0:1, 0:1] += 0.5 * jnp.sum(per_tok, axis=0, keepdims=True)

    return pl.pallas_call(
        body, name=name,
        out_shape=(jax.ShapeDtypeStruct((S, D), F32), jax.ShapeDtypeStruct((8, LANE), F32)),
        grid=(S // TS,),
        in_specs=[_row_spec(), _row_spec()],
        out_specs=(_row_spec(), pl.BlockSpec((8, LANE), lambda i: (0, 0))),
        compiler_params=_params(("arbitrary",)),
    )(xf, target)


def _relu2_epilogue(a):
    t = jnp.maximum(a, 0.0)
    return a, t * t


def _relu2_bwd_epilogue(dr, a):
    return (dr * (2.0 * jnp.maximum(a, 0.0)),)


def _merge_epilogue(pc, g0, g1, g2, pa, pb):
    return pc, jax.nn.sigmoid(g0) * pa + jax.nn.sigmoid(g1) * pb + jax.nn.sigmoid(g2) * pc


def _merge_bwd_epilogue(dm, g0, g1, g2, pa, pb, pc):
    sg = [jax.nn.sigmoid(g) for g in (g0, g1, g2)]
    return tuple(dm * s for s in sg) + tuple(dm * p * (s * (1.0 - s)) for p, s in zip((pa, pb, pc), sg))


def _shift_down(x, k, row):
    return jnp.where(row >= k, pltpu.roll(x, k, axis=0), 0.0)


def _shift_up(x, k, row):
    n = x.shape[0]
    return jnp.where(row < n - k, pltpu.roll(x, n - k, axis=0), 0.0)


def _cumsum_rows(x, row, reverse=False):
    shift = _shift_up if reverse else _shift_down
    k = 1
    while k < x.shape[0]:
        x = x + shift(x, k, row)
        k *= 2
    return x


def _full_spec(shape, idx=(0, 0)):
    return pl.BlockSpec(shape, lambda i: idx)


def _pool_window_select(lane, a2, a4, a8, a16):
    return jnp.where(lane < 64, a2, jnp.where(lane < 128, a4, jnp.where(lane < 192, a8, a16)))


def _pool_p(u, row, lane):
    t2 = u + _shift_down(u, 1, row)
    t4 = t2 + _shift_down(t2, 2, row)
    t8 = t4 + _shift_down(t4, 4, row)
    t16 = t8 + _shift_down(t8, 8, row)
    tw = _pool_window_select(lane, t2, t4, t8, t16)
    cnt = jnp.minimum((row + 1).astype(F32), _pool_window_select(lane, 2.0, 4.0, 8.0, 16.0))
    return tw / cnt - u, cnt


def _pool_fwd(z, wp_bd, pscale, name):
    def body(u_ref, w_ref, s_ref, o_ref):
        row = lax.broadcasted_iota(jnp.int32, (S, POOL_W), 0)
        lane = lax.broadcasted_iota(jnp.int32, (S, POOL_W), 1)
        p, _ = _pool_p(u_ref[...], row, lane)
        y = jnp.dot(p.astype(BF16), w_ref[...], preferred_element_type=F32)
        o_ref[...] = y * s_ref[0:1, :]

    return pl.pallas_call(
        body, name=name, out_shape=jax.ShapeDtypeStruct((S, POOL_W), F32), grid=(1,),
        in_specs=[_full_spec((S, POOL_W), (0, Z_PC // POOL_W)), _full_spec((POOL_W, POOL_W)), _full_spec((8, POOL_W))],
        out_specs=_full_spec((S, POOL_W)),
        compiler_params=_params(("arbitrary",)),
    )(z, wp_bd, pscale)


def _pool_bwd(z, wp_bd, pscale, dbr, name):
    def body(u_ref, w_ref, s_ref, dbr_ref, du_ref, dw_ref, red_ref):
        row = lax.broadcasted_iota(jnp.int32, (S, POOL_W), 0)
        lane = lax.broadcasted_iota(jnp.int32, (S, POOL_W), 1)
        p, cnt = _pool_p(u_ref[...], row, lane)
        pb = p.astype(BF16)
        y = jnp.dot(pb, w_ref[...], preferred_element_type=F32)
        dbr = dbr_ref[...]
        red_ref[...] = jnp.zeros_like(red_ref)
        red_ref[0:1, :] = jnp.sum(dbr * y, axis=0, keepdims=True)
        dy = (dbr * s_ref[0:1, :]).astype(BF16)
        dw_ref[...] = lax.dot_general(pb, dy, (((0,), (0,)), ((), ())), preferred_element_type=F32)
        dp = lax.dot_general(dy, w_ref[...], (((1,), (1,)), ((), ())), preferred_element_type=F32)
        g = dp / cnt
        a2 = g + _shift_up(g, 1, row)
        a4 = a2 + _shift_up(a2, 2, row)
        a8 = a4 + _shift_up(a4, 4, row)
        a16 = a8 + _shift_up(a8, 8, row)
        du_ref[...] = (_pool_window_select(lane, a2, a4, a8, a16) - dp).astype(BF16)

    return pl.pallas_call(
        body, name=name,
        out_shape=(jax.ShapeDtypeStruct((S, POOL_W), BF16), jax.ShapeDtypeStruct((POOL_W, POOL_W), F32),
                   jax.ShapeDtypeStruct((8, POOL_W), F32)),
        grid=(1,),
        in_specs=[_full_spec((S, POOL_W), (0, Z_PC // POOL_W)), _full_spec((POOL_W, POOL_W)), _full_spec((8, POOL_W)),
                  _full_spec((S, POOL_W))],
        out_specs=(_full_spec((S, POOL_W)), _full_spec((POOL_W, POOL_W)), _full_spec((8, POOL_W))),
        compiler_params=_params(("arbitrary",)),
    )(z, wp_bd, pscale, dbr)


def _conv_specs():
    base = Z_PC // CONV_W
    return [_full_spec((S, CONV_W), (0, base + 1)), _full_spec((S, CONV_W), (0, base + 2)),
            _full_spec((S, CONV_W), (0, base + 3)), _full_spec((8, CONV_W))]


def _conv_fwd(z, cw, name):
    def body(h_ref, b_ref, c_ref, w_ref, o_ref):
        row = lax.broadcasted_iota(jnp.int32, (S, CONV_W), 0)
        u = c_ref[...] * h_ref[...]
        y = (w_ref[0:1, :] * _shift_down(u, 2, row) + w_ref[1:2, :] * _shift_down(u, 1, row) + w_ref[2:3, :] * u)
        o_ref[...] = b_ref[...] * y

    return pl.pallas_call(
        body, name=name, out_shape=jax.ShapeDtypeStruct((S, CONV_W), F32), grid=(1,),
        in_specs=_conv_specs(), out_specs=_full_spec((S, CONV_W)),
        compiler_params=_params(("arbitrary",)),
    )(z, z, z, cw)


def _conv_bwd(z, cw, dbr, name):
    def body(h_ref, b_ref, c_ref, w_ref, dbr_ref, d_ref, red_ref):
        row = lax.broadcasted_iota(jnp.int32, (S, CONV_W), 0)
        h, cg = h_ref[...], c_ref[...]
        u = cg * h
        u1 = _shift_down(u, 1, row)
        u2 = _shift_down(u, 2, row)
        y = w_ref[0:1, :] * u2 + w_ref[1:2, :] * u1 + w_ref[2:3, :] * u
        dbr = dbr_ref[...]
        dy = dbr * b_ref[...]
        du = w_ref[2:3, :] * dy + w_ref[1:2, :] * _shift_up(dy, 1, row) + w_ref[0:1, :] * _shift_up(dy, 2, row)
        d_ref[:, 0:CONV_W] = (du * cg).astype(BF16)
        d_ref[:, CONV_W:2 * CONV_W] = (dbr * y).astype(BF16)
        d_ref[:, 2 * CONV_W:3 * CONV_W] = (du * h).astype(BF16)
        red_ref[...] = jnp.zeros_like(red_ref)
        red_ref[0:1, :] = jnp.sum(dy * u2, axis=0, keepdims=True)
        red_ref[1:2, :] = jnp.sum(dy * u1, axis=0, keepdims=True)
        red_ref[2:3, :] = jnp.sum(dy * u, axis=0, keepdims=True)

    return pl.pallas_call(
        body, name=name,
        out_shape=(jax.ShapeDtypeStruct((S, 3 * CONV_W), BF16), jax.ShapeDtypeStruct((8, CONV_W), F32)),
        grid=(1,),
        in_specs=_conv_specs() + [_full_spec((S, CONV_W))],
        out_specs=(_full_spec((S, 3 * CONV_W)), _full_spec((8, CONV_W))),
        compiler_params=_params(("arbitrary",)),
    )(z, z, z, cw, dbr)


_NT = (((1,), (1,)), ((), ()))
_TN = (((0,), (0,)), ((), ()))
N_HEAD = 2 * N_PAIR


def _split3(x):
    hi = x.astype(BF16).astype(F32)
    mid = (x - hi).astype(BF16).astype(F32)
    lo = (x - hi - mid).astype(BF16).astype(F32)
    return hi, mid, lo


def _spare(lane, e, k):
    return lane == 64 * (1 - e) + k


def _spare3(lane, e, k):
    base = 64 * (1 - e) + k
    return (lane >= base) & (lane < base + 3)


def _put3(lane, e, k, pieces, rest):
    out = rest
    for n, piece in enumerate(pieces):
        out = jnp.where(_spare(lane, e, k + n), piece, out)
    return out


def _attn_prep(z, bf, name):
    def body(q_ref, k_ref, v_ref, f_ref, b_ref, qa_ref, ka_ref, va_ref, kat_ref, cum_ref):
        p = pl.program_id(0)
        row = lax.broadcasted_iota(jnp.int32, (S, LANE), 0)
        lane = lax.broadcasted_iota(jnp.int32, (S, LANE), 1)

        @pl.when(p == 0)
        def _():
            xv = f_ref[...] + b_ref[0:1, :]
            ls = jnp.minimum(xv, 0.0) - jnp.log(1.0 + jnp.exp(-jnp.abs(xv)))
            cum_ref[...] = _cumsum_rows(jnp.where(lane < N_HEAD, ls, 0.0), row)

        cum = cum_ref[...]
        q, k, v = q_ref[...], k_ref[...], v_ref[...]
        for e in range(2):
            head = (lane >= 64) if e else (lane < 64)
            f = jnp.sum(jnp.where(lane == 2 * p + e, cum, 0.0), axis=1, keepdims=True)
            pieces = _split3(f)
            qa = jnp.where(head, q * ATT_SCALE, _put3(lane, e, 0, pieces, jnp.where(_spare3(lane, e, 3), 1.0, 0.0)))
            ones = jnp.where(_spare3(lane, e, 0) | _spare3(lane, e, 6), 1.0, 0.0)
            ka = jnp.where(head, k, _put3(lane, e, 3, [-x for x in pieces], ones))
            va = jnp.where(head, v, jnp.where(_spare3(lane, e, 0), 1.0, 0.0))
            qa_ref[e] = qa.astype(BF16)
            ka_ref[e] = ka.astype(BF16)
            va_ref[e] = va.astype(BF16)
            kat_ref[e] = ka.T.astype(BF16)

    qb, kb, vb = Z_Q // LANE, Z_K // LANE, Z_V // LANE
    heads = jax.ShapeDtypeStruct((N_HEAD, S, LANE), BF16)
    pair = pl.BlockSpec((2, S, LANE), lambda p: (p, 0, 0))
    return pl.pallas_call(
        body, name=name,
        out_shape=(heads, heads, heads, jax.ShapeDtypeStruct((N_HEAD, LANE, S), BF16)),
        grid=(N_PAIR,),
        in_specs=[pl.BlockSpec((S, LANE), lambda p: (0, qb + p)), pl.BlockSpec((S, LANE), lambda p: (0, kb + p)),
                  pl.BlockSpec((S, LANE), lambda p: (0, vb + p)), pl.BlockSpec((S, LANE), lambda p: (0, Z_F // LANE)),
                  pl.BlockSpec((8, LANE), lambda p: (0, 0))],
        out_specs=(pair, pair, pair, pl.BlockSpec((2, LANE, S), lambda p: (p, 0, 0))),
        scratch_shapes=[pltpu.VMEM((S, LANE), F32)],
        compiler_params=_params(("arbitrary",)),
    )(z, z, z, z, bf)


def _attn_bwd_prep(qa, o, lse, do, name):
    def body(qa_ref, o_ref, lse_ref, do_ref, qa2_ref, doa_ref):
        lane = lax.broadcasted_iota(jnp.int32, (S, LANE), 1)
        dov, ov, lsev = do_ref[...], o_ref[...], lse_ref[...]
        for e in range(2):
            head = (lane >= 64) if e else (lane < 64)
            dsum = jnp.sum(jnp.where(head, dov * ov, 0.0), axis=1, keepdims=True)
            doa_ref[e] = jnp.where(head, dov, _put3(lane, e, 0, [-x for x in _split3(dsum)], 0.0)).astype(BF16)
            lse_col = lsev[:, 64 * e:64 * e + 1]
            qa2_ref[e] = _put3(lane, e, 6, [-x for x in _split3(lse_col)], qa_ref[e].astype(F32)).astype(BF16)

    heads = jax.ShapeDtypeStruct((N_HEAD, S, LANE), BF16)
    pair = pl.BlockSpec((2, S, LANE), lambda p: (p, 0, 0))
    cols = pl.BlockSpec((S, LANE), lambda p: (0, p))
    return pl.pallas_call(
        body, name=name, out_shape=(heads, heads), grid=(N_PAIR,),
        in_specs=[pair, cols, cols, cols], out_specs=(pair, pair),
        compiler_params=_params(("parallel",)),
    )(qa, o, lse, do)


def _attn_bwd_post(z, bf, dqt, dka, dva, name):
    def body(f_ref, b_ref, dqt_ref, dk_ref, dv_ref, dq_out, dk_out, dv_out, dfl_ref, red_ref, dcum_ref):
        p = pl.program_id(0)

        @pl.when(p == 0)
        def _():
            dcum_ref[...] = jnp.zeros_like(dcum_ref)

        row = lax.broadcasted_iota(jnp.int32, (S, LANE), 0)
        lane = lax.broadcasted_iota(jnp.int32, (S, LANE), 1)
        dqa = [dqt_ref[e].T for e in range(2)]
        dq_out[...] = (jnp.where(lane < 64, dqa[0], dqa[1]) * ATT_SCALE).astype(BF16)
        dk_out[...] = jnp.where(lane < 64, dk_ref[0], dk_ref[1]).astype(BF16)
        dv_out[...] = jnp.where(lane < 64, dv_ref[0], dv_ref[1]).astype(BF16)
        for e in range(2):
            at = 64 * (1 - e)
            d_query = dqa[e][:, at:at + 1]
            d_key = dk_ref[e][:, at + 3:at + 4]
            dcum_ref[...] += jnp.where(lane == 2 * p + e, d_query - d_key, 0.0)

        @pl.when(p == N_PAIR - 1)
        def _():
            dls = _cumsum_rows(dcum_ref[...], row, reverse=True)
            xv = f_ref[...] + b_ref[0:1, :]
            dx = jnp.where(lane < N_HEAD, dls * jax.nn.sigmoid(-xv), 0.0)
            dfl_ref[...] = dx.astype(BF16)
            red_ref[...] = jnp.zeros_like(red_ref)
            red_ref[0:1, :] = jnp.sum(dx, axis=0, keepdims=True)

    wide = jax.ShapeDtypeStruct((S, N_PAIR * LANE), BF16)
    cols = pl.BlockSpec((S, LANE), lambda p: (0, p))
    pair = pl.BlockSpec((2, S, LANE), lambda p: (p, 0, 0))
    return pl.pallas_call(
        body, name=name,
        out_shape=(wide, wide, wide, jax.ShapeDtypeStruct((S, LANE), BF16), jax.ShapeDtypeStruct((8, LANE), F32)),
        grid=(N_PAIR,),
        in_specs=[pl.BlockSpec((S, LANE), lambda p: (0, Z_F // LANE)), pl.BlockSpec((8, LANE), lambda p: (0, 0)),
                  pl.BlockSpec((2, LANE, S), lambda p: (p, 0, 0)), pair, pair],
        out_specs=(cols, cols, cols, pl.BlockSpec((S, LANE), lambda p: (0, 0)), pl.BlockSpec((8, LANE), lambda p: (0, 0))),
        scratch_shapes=[pltpu.VMEM((S, LANE), F32)],
        compiler_params=_params(("arbitrary",)),
    )(z, bf, dqt, dka, dva)


def _attn_fwd(qa, ka, va, name):
    tq, tk = TQ_FWD, TQ
    ratio = tq // tk

    def body(qa_ref, ka_ref, va_ref, o_ref, lse_ref):
        i = pl.program_id(1)
        lane = lax.broadcasted_iota(jnp.int32, (tq, LANE), 1)
        row = lax.broadcasted_iota(jnp.int32, (tq, tk), 0)
        col = lax.broadcasted_iota(jnp.int32, (tq, tk), 1)
        nh = HEADS_PER_STEP_FWD
        qs = [qa_ref[h] for h in range(nh)]

        def block(j, carry, masked):
            off = pl.multiple_of(j * tk, tk)
            out = []
            for h in range(nh):
                m, acc = carry[h]
                s = lax.dot_general(qs[h], ka_ref[h, pl.ds(off, tk), :], _NT, preferred_element_type=F32)
                if masked:
                    s = jnp.where(col + (j - ratio * i) * tk > row, NEG_INF, s)
                mn = jnp.maximum(m, jnp.max(s, axis=1, keepdims=True))
                p = jnp.exp(s - mn).astype(BF16)
                acc = jnp.exp(m - mn) * acc + jnp.dot(p, va_ref[h, pl.ds(off, tk), :], preferred_element_type=F32)
                out.append((mn, acc))
            return tuple(out)

        init = (jnp.full((tq, 1), NEG_INF, F32), jnp.zeros((tq, LANE), F32))
        carry = lax.fori_loop(0, ratio * i, lambda j, c: block(j, c, False), (init,) * nh)
        for d in range(ratio):
            carry = block(ratio * i + d, carry, True)
        res = []
        for h in range(nh):
            m, acc = carry[h]
            at = 64 * (1 - h % 2)
            l = acc[:, at:at + 1]
            res.append((acc / l, m + jnp.log(l)))
        for g in range(nh // 2):
            o_ref[:, g * LANE:(g + 1) * LANE] = jnp.where(lane < 64, res[2 * g][0], res[2 * g + 1][0])
            lse_ref[:, g * LANE:(g + 1) * LANE] = jnp.where(lane < 64, res[2 * g][1], res[2 * g + 1][1])

    nh = HEADS_PER_STEP_FWD
    out = jax.ShapeDtypeStruct((S, N_PAIR * LANE), F32)
    wide = pl.BlockSpec((tq, 64 * nh), lambda p, i: (i, p))
    return pl.pallas_call(
        body, name=name, out_shape=(out, out), grid=(N_HEAD // nh, S // tq),
        in_specs=[pl.BlockSpec((nh, tq, LANE), lambda p, i: (p, i, 0)), pl.BlockSpec((nh, S, LANE), lambda p, i: (p, 0, 0)),
                  pl.BlockSpec((nh, S, LANE), lambda p, i: (p, 0, 0))],
        out_specs=(wide, wide),
        compiler_params=_params(("parallel", "parallel")),
    )(qa, ka, va)


def _attn_bwd(qa2, ka, va, kat, doa, name):
    nq = S // TQ

    def body(qa_ref, ka_ref, va_ref, kat_ref, doa_ref, dqt_ref, dk_ref, dv_ref):
        j = pl.program_id(1)

        @pl.when(j == 0)
        def _():
            dqt_ref[...] = jnp.zeros_like(dqt_ref)

        key = lax.broadcasted_iota(jnp.int32, (TQ, TQ), 0)
        qry = lax.broadcasted_iota(jnp.int32, (TQ, TQ), 1)
        nh = HEADS_PER_STEP
        kav, vav, katv = ([ref[h] for h in range(nh)] for ref in (ka_ref, va_ref, kat_ref))

        def block(i, carry, masked):
            off = pl.multiple_of(i * TQ, TQ)
            out = []
            for h in range(nh):
                dk_acc, dv_acc = carry[h]
                qav = qa_ref[h, pl.ds(off, TQ), :]
                doav = doa_ref[h, pl.ds(off, TQ), :]
                s_t = lax.dot_general(kav[h], qav, _NT, preferred_element_type=F32)
                if masked:
                    s_t = jnp.where(key > qry, NEG_INF, s_t)
                p_t = jnp.exp(s_t)
                ds_t = p_t * lax.dot_general(vav[h], doav, _NT, preferred_element_type=F32)
                dsb = ds_t.astype(BF16)
                dv_acc = dv_acc + jnp.dot(p_t.astype(BF16), doav, preferred_element_type=F32)
                dk_acc = dk_acc + jnp.dot(dsb, qav, preferred_element_type=F32)
                dqt_ref[h, :, pl.ds(off, TQ)] += jnp.dot(katv[h], dsb, preferred_element_type=F32)
                out.append((dk_acc, dv_acc))
            return tuple(out)

        zero = (jnp.zeros((TQ, LANE), F32), jnp.zeros((TQ, LANE), F32))
        carry = block(j, (zero,) * nh, True)
        carry = lax.fori_loop(j + 1, nq, lambda i, c: block(i, c, False), carry)
        for h in range(nh):
            dk_ref[h], dv_ref[h] = carry[h]

    nh = HEADS_PER_STEP
    full = pl.BlockSpec((nh, S, LANE), lambda p, j: (p, 0, 0))
    blk = pl.BlockSpec((nh, TQ, LANE), lambda p, j: (p, j, 0))
    acc = jax.ShapeDtypeStruct((N_HEAD, S, LANE), F32)
    return pl.pallas_call(
        body, name=name,
        out_shape=(jax.ShapeDtypeStruct((N_HEAD, LANE, S), F32), acc, acc),
        grid=(N_HEAD // nh, nq),
        in_specs=[full, blk, blk, pl.BlockSpec((nh, LANE, TQ), lambda p, j: (p, 0, j)), full],
        out_specs=(pl.BlockSpec((nh, LANE, S), lambda p, j: (p, 0, 0)), blk, blk),
        compiler_params=_params(("arbitrary", "arbitrary")),
    )(qa2, ka, va, kat, doa)


ADA_ROWS = 16


def _ada_fwd(c_pad, w_ada, b_cols, name):
    def body(c_ref, w_ref, b_ref, o_ref):
        cv = c_ref[...]
        sc = (cv * jax.nn.sigmoid(cv)).astype(BF16)
        o_ref[0] = jnp.dot(sc, w_ref[0].astype(BF16), preferred_element_type=F32) + b_ref[0, 0:1, :]

    return pl.pallas_call(
        body, name=name, out_shape=jax.ShapeDtypeStruct((DEPTH, ADA_ROWS, ADA_COLS), F32), grid=(DEPTH,),
        in_specs=[pl.BlockSpec((ADA_ROWS, D), lambda l: (0, 0)), pl.BlockSpec((1, D, ADA_COLS), lambda l: (l, 0, 0)),
                  pl.BlockSpec((1, 8, ADA_COLS), lambda l: (l, 0, 0))],
        out_specs=pl.BlockSpec((1, ADA_ROWS, ADA_COLS), lambda l: (l, 0, 0)),
        compiler_params=_params(("parallel",)),
    )(c_pad, w_ada, b_cols)


def _ada_bwd(c_pad, dmod_cols, name):
    def body(c_ref, d_ref, o_ref):
        cv = c_ref[...]
        sc = (cv * jax.nn.sigmoid(cv)).astype(BF16)
        o_ref[0] = lax.dot_general(sc, d_ref[0].astype(BF16), _TN, preferred_element_type=F32)

    return pl.pallas_call(
        body, name=name, out_shape=jax.ShapeDtypeStruct((DEPTH, D, ADA_COLS), F32), grid=(DEPTH,),
        in_specs=[pl.BlockSpec((ADA_ROWS, D), lambda l: (0, 0)), pl.BlockSpec((1, ADA_ROWS, ADA_COLS), lambda l: (l, 0, 0))],
        out_specs=pl.BlockSpec((1, D, ADA_COLS), lambda l: (l, 0, 0)),
        compiler_params=_params(("parallel",)),
    )(c_pad, dmod_cols)


def _adamw_math(w, g, m, v):
    m = B1 * m + (1.0 - B1) * g
    v = B2 * v + (1.0 - B2) * (g * g)
    m_hat = m / (1.0 - B1 ** STEP)
    v_hat = v / (1.0 - B2 ** STEP)
    delta = -LR * (m_hat / (jnp.sqrt(v_hat) + EPS) + WD * w)
    return delta, m, v


def _row_tile(rows, target=256):
    best = 8
    for t in range(8, min(rows, target) + 1, 8):
        if rows % t == 0:
            best = t
    return best


def _adamw(w, g, m, v, name):
    layers, rows, cols = w.shape
    tr = _row_tile(rows)
    spec = pl.BlockSpec((1, tr, cols), lambda l, i: (l, i, 0))

    def body(w_ref, g_ref, m_ref, v_ref, d_ref, nm_ref, nv_ref):
        d_ref[...], nm_ref[...], nv_ref[...] = _adamw_math(w_ref[...], g_ref[...], m_ref[...], v_ref[...])

    out = jax.ShapeDtypeStruct(w.shape, F32)
    return pl.pallas_call(
        body, name=name, out_shape=(out, out, out), grid=(layers, rows // tr),
        in_specs=[spec] * 4, out_specs=(spec,) * 3, compiler_params=_params(("parallel", "parallel")),
    )(w, g, m, v)


def _sum_slabs(x, name):
    n, rows, _ = x.shape
    tr = _row_tile(rows)

    def body(x_ref, o_ref):
        acc = x_ref[0]
        for k in range(1, n):
            acc = acc + x_ref[k]
        o_ref[...] = acc

    return pl.pallas_call(
        body, name=name, out_shape=jax.ShapeDtypeStruct((rows, D), F32), grid=(rows // tr,),
        in_specs=[pl.BlockSpec((n, tr, D), lambda i: (0, i, 0))], out_specs=pl.BlockSpec((tr, D), lambda i: (i, 0)),
        compiler_params=_params(("parallel",)),
    )(x)


_ANY = pl.BlockSpec(memory_space=pl.ANY)
MESH = pl.DeviceIdType.MESH


def _on_sequencer(body, out_shape, sems, operands, after, sequencer_id, name):
    n = len(operands)

    def ordered_body(*refs):
        body(*refs[:n], *refs[n + 1:])

    extra = [] if after is None else [after]
    return pl.kernel(
        body if after is None else ordered_body, out_type=out_shape,
        mesh=plsc.ScalarSubcoreMesh(axis_name="sequencer", num_cores=1), scratch_types=sems,
        compiler_params=pltpu.CompilerParams(collective_id=sequencer_id), name=name)(*operands, *extra)


def _all_gather(xs, name, sequencer_id=None, after=None):
    n = len(xs)

    def body(*refs):
        x_refs, out_refs = refs[:n], refs[n:2 * n]
        send_sems, recv_sems, local_sems = refs[2 * n:]
        x_, y_, c_ = lax.axis_index("x"), lax.axis_index("y"), lax.axis_index("c")
        me, sibling = (x_, y_, c_), (x_, y_, 1 - c_)
        chips = [(1 - x_, y_), (x_, 1 - y_), (1 - x_, 1 - y_)]
        if sequencer_id is not None:
            barrier = pltpu.get_barrier_semaphore()
            peers = [sibling] + [(*chip, pc) for chip in chips for pc in (c_, 1 - c_)]
            for peer in peers:
                pl.semaphore_signal(barrier, inc=1, device_id=peer, device_id_type=MESH)
            pl.semaphore_wait(barrier, len(peers))

        def slot(a, px, py, pc):
            return out_refs[a].at[4 * px + 2 * py + pc]

        def copy(a, k, block, to, src=None):
            return pltpu.make_async_remote_copy(
                src_ref=slot(a, *block) if src is None else src, dst_ref=slot(a, *block),
                send_sem=send_sems.at[7 * a + k], recv_sem=recv_sems.at[7 * a + k], device_id=to, device_id_type=MESH)

        mine = [pltpu.make_async_copy(x_refs[a], slot(a, *me), local_sems.at[a]) for a in range(n)]
        for cp in mine:
            cp.start()
        first = []
        for a in range(n):
            first.append(copy(a, 0, me, sibling, src=x_refs[a]))
            first += [copy(a, 1 + j, me, (*chip, c_), src=x_refs[a]) for j, chip in enumerate(chips)]
        for cp in first:
            cp.start()
        passed = []
        for j, chip in enumerate(chips):
            for a in range(n):
                copy(a, 1 + j, (*chip, c_), me).wait_recv()
                passed.append(copy(a, 4 + j, (*chip, c_), sibling))
                passed[-1].start()
        for a in range(n):
            copy(a, 0, sibling, me).wait_recv()
        for j, chip in enumerate(chips):
            for a in range(n):
                copy(a, 4 + j, (*chip, 1 - c_), me).wait_recv()
        for cp in first + passed:
            cp.wait_send()
        for cp in mine:
            cp.wait()

    out_shape = [jax.ShapeDtypeStruct((N_DEV,) + x.shape, x.dtype) for x in xs]
    sems = [pltpu.SemaphoreType.DMA((7 * n,)), pltpu.SemaphoreType.DMA((7 * n,)), pltpu.SemaphoreType.DMA((n,))]
    if sequencer_id is not None:
        return _on_sequencer(body, out_shape, sems, xs, after, sequencer_id, name)
    return pl.pallas_call(
        body, name=name, out_shape=out_shape, in_specs=[_ANY] * n, out_specs=[_ANY] * n, scratch_shapes=sems)(*xs)


def _sibling_exchange(gs, name, sequencer_id=None, after=None):
    n = len(gs)

    def body(*refs):
        g_refs, p_refs = refs[:n], refs[n:2 * n]
        send_sems, recv_sems = refs[2 * n:]
        x_, y_, c_ = lax.axis_index("x"), lax.axis_index("y"), lax.axis_index("c")
        if sequencer_id is not None:
            barrier = pltpu.get_barrier_semaphore()
            pl.semaphore_signal(barrier, inc=1, device_id=(x_, y_, 1 - c_), device_id_type=MESH)
            pl.semaphore_wait(barrier, 1)
        copies = [pltpu.make_async_remote_copy(
            src_ref=g_refs[a].at[2 * k + (1 - c_)], dst_ref=p_refs[a].at[k], send_sem=send_sems.at[4 * a + k],
            recv_sem=recv_sems.at[4 * a + k], device_id=(x_, y_, 1 - c_), device_id_type=MESH)
            for a in range(n) for k in range(4)]
        for cp in copies:
            cp.start()
        for cp in copies:
            cp.wait()

    out_shape = [jax.ShapeDtypeStruct((4,) + g.shape[1:], g.dtype) for g in gs]
    sems = [pltpu.SemaphoreType.DMA((4 * n,)), pltpu.SemaphoreType.DMA((4 * n,))]
    if sequencer_id is not None:
        return _on_sequencer(body, out_shape, sems, gs, after, sequencer_id, name)
    return pl.pallas_call(
        body, name=name, out_shape=out_shape, in_specs=[_ANY] * n, out_specs=[_ANY] * n, scratch_shapes=sems)(*gs)


def _slab_tiles(rows, cols):
    if rows % 8 == 0:
        return _row_tile(rows), cols
    return rows, 2 * LANE


def _pair_sums(g, p, route, name):
    _, rows, cols = g.shape
    tr, tc = _slab_tiles(rows, cols)

    def body(route_ref, g_ref, p_ref, t_ref):
        t_ref[...] = (g_ref[...].astype(F32) + p_ref[...].astype(F32)).astype(BF16)

    return pl.pallas_call(
        body, name=name, out_shape=jax.ShapeDtypeStruct((3, rows, cols), BF16),
        grid_spec=pltpu.PrefetchScalarGridSpec(
            num_scalar_prefetch=1, grid=(3, rows // tr, cols // tc),
            in_specs=[pl.BlockSpec((1, tr, tc), lambda r, i, j, route_ref: (2 * route_ref[1 + r] + route_ref[0], i, j)),
                      pl.BlockSpec((1, tr, tc), lambda r, i, j, route_ref: (route_ref[1 + r], i, j))],
            out_specs=pl.BlockSpec((1, tr, tc), lambda r, i, j, route_ref: (r, i, j))),
        compiler_params=_params(("parallel", "parallel", "parallel")),
    )(route, g, p)


def _chip_exchange(ts, name, sequencer_id=None, after=None):
    n = len(ts)

    def body(*refs):
        t_refs, l_refs = refs[:n], refs[n:2 * n]
        send_sems, recv_sems = refs[2 * n:]
        x_, y_, c_ = lax.axis_index("x"), lax.axis_index("y"), lax.axis_index("c")
        chips = [(1 - x_, y_), (x_, 1 - y_), (1 - x_, 1 - y_)]
        if sequencer_id is not None:
            barrier = pltpu.get_barrier_semaphore()
            for px, py in chips:
                pl.semaphore_signal(barrier, inc=1, device_id=(px, py, c_), device_id_type=MESH)
            pl.semaphore_wait(barrier, len(chips))
        copies = [pltpu.make_async_remote_copy(
            src_ref=t_refs[a].at[r], dst_ref=l_refs[a].at[r], send_sem=send_sems.at[3 * a + r],
            recv_sem=recv_sems.at[3 * a + r], device_id=(px, py, c_), device_id_type=MESH)
            for a in range(n) for r, (px, py) in enumerate(chips)]
        for cp in copies:
            cp.start()
        for cp in copies:
            cp.wait()

    out_shape = [jax.ShapeDtypeStruct((3,) + t.shape[1:], t.dtype) for t in ts]
    sems = [pltpu.SemaphoreType.DMA((3 * n,)), pltpu.SemaphoreType.DMA((3 * n,))]
    if sequencer_id is not None:
        return _on_sequencer(body, out_shape, sems, ts, after, sequencer_id, name)
    return pl.pallas_call(
        body, name=name, out_shape=out_shape, in_specs=[_ANY] * n, out_specs=[_ANY] * n, scratch_shapes=sems)(*ts)


def _reduce_adamw(gs, ps, landed, place, w, m, v, name):
    layers, rows, cols = w.shape
    assert layers == DEPTH == 2
    tr, tc = _slab_tiles(rows, cols)
    nr, nc = rows // tr, cols // tc
    spec = pl.BlockSpec((1, tr, tc), lambda l, i, j, place_ref: (l, i, j))

    def own(layer, which):
        pi, pj = (nr - 1, nc - 1) if layer == 0 else (0, 0)

        def index(l, i, j, place_ref):
            lead = 0 if which is None else place_ref[which]
            return lead, jnp.where(l == layer, i, pi), jnp.where(l == layer, j, pj)

        return pl.BlockSpec((3 if which is None else 1, tr, tc), index)

    def body(place_ref, g0_ref, p0_ref, l0_ref, g1_ref, p1_ref, l1_ref, w_ref, m_ref, v_ref,
             g_ref, d_ref, nm_ref, nv_ref):
        def update(own_ref, sib_ref, l_ref):
            g = (own_ref[0].astype(F32) + sib_ref[0].astype(F32) + l_ref[0].astype(F32) + l_ref[1].astype(F32)
                 + l_ref[2].astype(F32))
            g_ref[0] = g
            d_ref[0], nm_ref[0], nv_ref[0] = _adamw_math(w_ref[0], g, m_ref[0], v_ref[0])

        @pl.when(pl.program_id(0) == 0)
        def _():
            update(g0_ref, p0_ref, l0_ref)

        @pl.when(pl.program_id(0) == 1)
        def _():
            update(g1_ref, p1_ref, l1_ref)

    out = jax.ShapeDtypeStruct(w.shape, F32)
    return pl.pallas_call(
        body, name=name, out_shape=(out, out, out, out),
        grid_spec=pltpu.PrefetchScalarGridSpec(
            num_scalar_prefetch=1, grid=(DEPTH, nr, nc),
            in_specs=[own(0, 0), own(0, 1), own(0, None), own(1, 0), own(1, 1), own(1, None), spec, spec, spec],
            out_specs=(spec, spec, spec, spec)),
        compiler_params=_params(("arbitrary", "arbitrary", "arbitrary")),
    )(place, gs[0], ps[0], landed[0], gs[1], ps[1], landed[1], w, m, v)


def _pack(pieces, row_multiple, dtype, cols=D, rows=None):
    flat = jnp.concatenate([p.astype(dtype).reshape(-1) for p in pieces])
    if rows is None:
        rows = -(-flat.shape[0] // cols)
        rows = -(-rows // row_multiple) * row_multiple
    flat = jnp.pad(flat, (0, rows * cols - flat.shape[0]))
    return flat.reshape(rows, cols)


def _unpack(flat, shapes, lead=()):
    out, off = [], 0
    for shp in shapes:
        n = 1
        for s_ in shp:
            n *= s_
        out.append(lax.slice_in_dim(flat, off, off + n, axis=len(lead)).reshape(lead + tuple(shp)))
        off += n
    return out


WIN_STRIDE = 704
WIN_ROWS = 720
Z_TURN = 1544


def _window(wt, me, name):
    padded = jnp.pad(wt, ((0, 0), (0, WIN_ROWS - IN_SHARD), (0, 0)))

    def body(me_ref, x_ref, o_ref):
        o_ref[0] = pltpu.roll(x_ref[0], me_ref[0], axis=0).astype(BF16)

    spec = pl.BlockSpec((1, WIN_ROWS, D), lambda l, me_ref: (l, 0, 0))
    return pl.pallas_call(
        body, name=name, out_shape=jax.ShapeDtypeStruct((DEPTH, WIN_ROWS, D), BF16),
        grid_spec=pltpu.PrefetchScalarGridSpec(num_scalar_prefetch=1, grid=(DEPTH,), in_specs=[spec], out_specs=spec),
        compiler_params=_params(("parallel",)),
    )(me, padded)


def _z_rows_from_windows(win):
    over = WIN_ROWS - WIN_STRIDE
    pieces = [(0, win[0][0:WIN_STRIDE])]
    for d in range(1, N_DEV):
        base = WIN_STRIDE * d
        pieces.append((base, win[d - 1][WIN_STRIDE:WIN_ROWS] + win[d][0:over]))
        pieces.append((base + over, win[d][over:WIN_STRIDE]))
    pieces.append((WIN_STRIDE * N_DEV, win[N_DEV - 1][WIN_STRIDE:WIN_ROWS]))

    def rows(a, b):
        out = []
        for start, arr in pieces:
            lo, hi = max(a, start), min(b, start + arr.shape[0])
            if lo < hi:
                out.append(arr[lo - start:hi - start])
        return out

    pad = jnp.zeros((NZ - IN_COLS, win.shape[-1]), win.dtype)
    return jnp.concatenate(rows(Z_TURN, IN_COLS) + rows(0, Z_TURN) + [pad], axis=0)


def _in_rows_from_z(wt):
    return jnp.concatenate([wt[Z_Q:Z_Q + 1536], wt[Z_F:Z_F + 8], wt[Z_PC:Z_PC + 1024], wt[Z_G:Z_G + 3072]], axis=0)


def _pad_rows(v, rows=8):
    return jnp.pad(v, ((0, rows - v.shape[0]), (0, 0)))


def _layer_fwd(l, x, wts, gvec, mod):
    tag = f"l{l}"
    h = _prenorm_fwd(x, gvec, mod, 0, 0, 1, f"prenorm_mix_{tag}")
    z = _matmul(h, wts["w_in_t"], "nt", f"in_proj_{tag}", tn=1152)
    qa, ka, va, kat = _attn_prep(z, wts["b_f"], f"attn_prep_{tag}")
    qa = wts["arrive"](qa)
    o, lse = _attn_fwd(qa, ka, va, f"attn_{tag}")
    br_b = _pool_fwd(z, wts["wp_bd"], wts["pool_scale"], f"pool_{tag}")
    br_c = _conv_fwd(z, wts["conv_w"], f"conv_{tag}")
    pa = _matmul(o, wts["wa"], "nn", f"proj_a_{tag}", out_dtype=BF16)
    pb = _matmul(br_b, wts["wb"], "nn", f"proj_b_{tag}", out_dtype=BF16)
    gates = [(z, Z_G + k * D) for k in range(3)]
    pc, merged = _matmul(br_c, wts["wc"], "nn", f"proj_c_merge_{tag}", tm=1024, tn=512,
                         extra=gates + [(pa, 0), (pb, 0)], epilogue=_merge_epilogue, out_dtypes=(BF16, BF16))
    y = _matmul(merged, wts["w_out"], "nn", f"out_proj_{tag}")
    x1 = _postnorm_fwd(x, y, gvec, mod, 1, 2, f"postnorm_mix_{tag}")
    h2 = _prenorm_fwd(x1, gvec, mod, 2, 3, 4, f"prenorm_ff_{tag}")
    a, r = _matmul(h2, wts["w_ff1"], "nn", f"ff1_{tag}", b_col_shards=True, epilogue=_relu2_epilogue,
                   out_dtypes=(BF16, BF16))
    y2 = _matmul(r, wts["w_ff2"], "nn", f"ff2_{tag}", tk=1024)
    x2 = _postnorm_fwd(x1, y2, gvec, mod, 3, 5, f"postnorm_ff_{tag}")
    saved = dict(x=x, h=h, z=z, qa=qa, ka=ka, va=va, kat=kat, o=o, lse=lse, br_b=br_b, br_c=br_c, pa=pa, pb=pb, pc=pc,
                 merged=merged, y=y, x1=x1, h2=h2, a=a, r=r, y2=y2)
    return x2, saved


def _ffn_bwd(l, dx2, sv, wts, gvec, mod, midpoint):
    tag = f"l{l}"
    dy2, red_post_ff = _postnorm_bwd(sv["y2"], gvec, mod, dx2, 3, 5, f"postnorm_ff_bwd_{tag}")
    dy2 = midpoint(dy2)
    da = _matmul(dy2, wts["w_ff2"], "nt", f"ff2_dx_{tag}", extra=[(sv["a"], 0)], epilogue=_relu2_bwd_epilogue,
                 out_dtypes=(BF16,))[0]
    d_w_ff2 = _matmul(sv["r"], dy2, "tn", f"ff2_dw_{tag}", out_dtype=GRAD_DTYPE)
    dh2 = _matmul(da, wts["w_ff1"], "nt", f"ff1_dx_{tag}", b_col_shards=True)
    d_w_ff1 = _matmul(sv["h2"], da, "tn", f"ff1_dw_{tag}", out_dtype=GRAD_DTYPE, out_col_shards=True)
    dx1, red_pre_ff = _prenorm_bwd(sv["x1"], gvec, mod, dh2, dx2, 2, 4, f"prenorm_ff_bwd_{tag}")
    return dx1, [d_w_ff1, d_w_ff2.reshape(N_DEV, D_FF // N_DEV, D)], (red_pre_ff, red_post_ff)


def _mixer_bwd(l, dx1, sv, wts, gvec, mod, ffn_reds, midpoint):
    tag = f"l{l}"
    red_pre_ff, red_post_ff = ffn_reds
    dy, red_post_mix = _postnorm_bwd(sv["y"], gvec, mod, dx1, 1, 2, f"postnorm_mix_bwd_{tag}")
    gates = [(sv["z"], Z_G + k * D) for k in range(3)]
    dpa, dpb, dpc, *dgl = _matmul(dy, wts["w_out"], "nt", f"out_proj_dx_{tag}", tm=1024, tn=512,
                                  extra=gates + [(sv["pa"], 0), (sv["pb"], 0), (sv["pc"], 0)],
                                  epilogue=_merge_bwd_epilogue, out_dtypes=(BF16,) * 6)
    d_w_out = _matmul(sv["merged"], dy, "tn", f"out_proj_dw_{tag}", out_dtype=GRAD_DTYPE)
    dpa = midpoint(dpa)
    do = _matmul(dpa, wts["wa"], "nt", f"proj_a_dx_{tag}")
    dbr_b = _matmul(dpb, wts["wb"], "nt", f"proj_b_dx_{tag}")
    dbr_c = _matmul(dpc, wts["wc"], "nt", f"proj_c_dx_{tag}")
    d_wa = _matmul(sv["o"], dpa, "tn", f"proj_a_dw_{tag}", out_dtype=GRAD_DTYPE)
    d_wb = _matmul(sv["br_b"], dpb, "tn", f"proj_b_dw_{tag}", out_dtype=GRAD_DTYPE)
    d_wc = _matmul(sv["br_c"], dpc, "tn", f"proj_c_dw_{tag}", out_dtype=GRAD_DTYPE)
    d_w_branch = jnp.concatenate([d_wa, d_wb, d_wc], axis=0)

    dpu, d_wp_bd, red_pool = _pool_bwd(sv["z"], wts["wp_bd"], wts["pool_scale"], dbr_b, f"pool_bwd_{tag}")
    dconv, red_conv = _conv_bwd(sv["z"], wts["conv_w"], dbr_c, f"conv_bwd_{tag}")
    qa2, doa = _attn_bwd_prep(sv["qa"], sv["o"], sv["lse"], do, f"attn_bwd_prep_{tag}")
    dqt, dka, dva = _attn_bwd(qa2, sv["ka"], sv["va"], sv["kat"], doa, f"attn_bwd_{tag}")
    dq, dk, dv, dfl, red_f = _attn_bwd_post(sv["z"], wts["b_f"], dqt, dka, dva, f"attn_bwd_post_{tag}")
    dz = jnp.concatenate([dpu, dconv, *dgl, dq, dk, dv, dfl], axis=1)
    dh = _matmul(dz, wts["w_in_t"], "nn", f"in_proj_dx_{tag}", tm=1024, tk=1920)
    d_w_in_t = _matmul(dz, sv["h"], "tn", f"in_proj_dw_{tag}", out_dtype=GRAD_DTYPE, tm=1152)
    dx0, red_pre_mix = _prenorm_bwd(sv["x"], gvec, mod, dh, dx1, 0, 1, f"prenorm_mix_bwd_{tag}")

    rows = D // N_DEV
    big = [_in_rows_from_z(d_w_in_t).reshape(N_DEV, IN_SHARD, D), d_w_branch.reshape(N_DEV, rows, D),
           d_w_out.reshape(N_DEV, rows, D)]
    d_w_pool = jnp.stack([d_wp_bd[64 * g:64 * (g + 1), 64 * g:64 * (g + 1)] for g in range(4)])
    small = dict(
        mod=jnp.stack([red_pre_mix[0], red_pre_mix[1], red_post_mix[0], red_pre_ff[0], red_pre_ff[1], red_post_ff[0]]),
        g_mix_pre=red_pre_mix[2], g_mix_post=red_post_mix[1], g_ff_pre=red_pre_ff[2], g_ff_post=red_post_ff[1],
        b_f=red_f[0, 0:8], w_pool=d_w_pool, pool_scale=red_pool[0], conv_w=red_conv[0:3])
    return dx0, big, small


SMALL_KEYS = ["mod", "g_mix_pre", "g_mix_post", "g_ff_pre", "g_ff_post", "b_f", "w_pool", "pool_scale", "conv_w"]
SMALL_SHAPES = [(DEPTH, 6 * D), (DEPTH, D), (DEPTH, D), (DEPTH, D), (DEPTH, D), (DEPTH, 8), (DEPTH, 4, 64, 64),
                (DEPTH, POOL_W), (DEPTH, 3, CONV_W)]


def kernel(x, c, w_ada, b_ada, g_mix_pre, g_mix_post, g_ff_pre, g_ff_post, w_in, b_f, w_pool, pool_scale, conv_w, w_branch, w_out, w_ff1, w_ff2, loss_target, m_w_ada, m_b_ada, m_g_mix_pre, m_g_mix_post, m_g_ff_pre, m_g_ff_post, m_w_in, m_b_f, m_w_pool, m_pool_scale, m_conv_w, m_w_branch, m_w_out, m_w_ff1, m_w_ff2, v_w_ada, v_b_ada, v_g_mix_pre, v_g_mix_post, v_g_ff_pre, v_g_ff_post, v_w_in, v_b_f, v_w_pool, v_pool_scale, v_conv_w, v_w_branch, v_w_out, v_w_ff1, v_w_ff2):
    ix, iy, ic = lax.axis_index("x"), lax.axis_index("y"), lax.axis_index("c")
    me = 4 * ix + 2 * iy + ic
    route = jnp.stack([ic, 2 * (1 - ix) + iy, 2 * ix + (1 - iy), 2 * (1 - ix) + (1 - iy)]).astype(jnp.int32)
    place = jnp.stack([me, 2 * ix + iy]).astype(jnp.int32)
    wt_in, mt_in, vt_in = (jnp.transpose(a, (0, 2, 1)) for a in (w_in, m_w_in, v_w_in))

    c_all = _all_gather([_pad_rows(c)], "gather_c")[0][:, 0, :]
    c_pad = _pad_rows(c_all, ADA_ROWS)
    b_cols = lax.dynamic_slice_in_dim(b_ada, me * ADA_COLS, ADA_COLS, axis=1)
    b_cols = jnp.broadcast_to(b_cols[:, None, :], (DEPTH, 8, ADA_COLS))
    mod_part = _ada_fwd(c_pad, w_ada, b_cols, "ada_fwd")
    mod_all = _all_gather([mod_part.reshape(DEPTH * ADA_ROWS, ADA_COLS)], "gather_mod")[0]
    mod_all = mod_all.reshape(N_DEV, DEPTH, ADA_ROWS, ADA_COLS)
    mod_mine = lax.dynamic_index_in_dim(mod_all, me, axis=2, keepdims=False)
    mod_mine = jnp.transpose(mod_mine, (1, 0, 2)).reshape(DEPTH, 6, D)

    cw_cols = CONV_W // N_DEV
    cw_send = jnp.pad(conv_w.reshape(DEPTH * 3, cw_cols), ((0, 8 - DEPTH * 3), (0, LANE - cw_cols)))
    win_in = _window(wt_in, place[0:1], "w_in_window")
    send = [[w[l].astype(BF16) for w in (win_in, w_branch, w_out, w_ff1, w_ff2)] for l in range(DEPTH)]
    first = _all_gather(send[0][:1], "gather_weights_l0_in", sequencer_id=1, after=mod_all)
    rest = _all_gather(send[0][1:] + [cw_send], "gather_weights_l0_rest", sequencer_id=2, after=first[0])
    first1 = _all_gather(send[1][:1], "gather_weights_l1_in", sequencer_id=3, after=first[0])
    rest1 = _all_gather(send[1][1:], "gather_weights_l1_rest", sequencer_id=12, after=first[0])
    first, (mt_in, vt_in) = lax.optimization_barrier((first, (mt_in, vt_in)))
    gathered = [first + rest[:4], first1 + rest1]
    cw_all = rest[4][:, :DEPTH * 3, :cw_cols].reshape(N_DEV, DEPTH, 3, cw_cols)

    def first_operands(l, p_in):
        wp_bd = jnp.zeros((POOL_W, POOL_W), F32)
        for g in range(4):
            wp_bd = wp_bd.at[64 * g:64 * (g + 1), 64 * g:64 * (g + 1)].set(w_pool[l, g])
        return dict(w_in_t=_z_rows_from_windows(p_in), wp_bd=wp_bd.astype(BF16),
                    pool_scale=_pad_rows(pool_scale[l][None, :]), b_f=_pad_rows(jnp.pad(b_f[l], (0, LANE - 8))[None, :]))

    def rest_operands(l, rest):
        p_br, p_out, p_ff1, p_ff2 = rest
        w_br_full = p_br.reshape(D, D)
        cw_full = jnp.transpose(cw_all[:, l], (1, 0, 2)).reshape(3, CONV_W)
        return dict(wa=w_br_full[0:A_WIDTH], wb=w_br_full[A_WIDTH:A_WIDTH + POOL_W], wc=w_br_full[A_WIDTH + POOL_W:],
                    w_out=p_out.reshape(D, D), w_ff1=p_ff1, w_ff2=p_ff2.reshape(D_FF, D), conv_w=_pad_rows(cw_full))

    xs = x[0]
    saved, layers = [], []
    for l in range(DEPTH):
        p_in, rest = gathered[l][0], gathered[l][1:5]
        if l > 0:
            xs, p_in = lax.optimization_barrier((xs, p_in))
        wts = first_operands(l, p_in)

        def arrive(t, l=l, rest=rest, wts=wts):
            if l > 0:
                t, rest = lax.optimization_barrier((t, rest))
            wts.update(rest_operands(l, rest))
            return t

        wts["arrive"] = arrive
        gvec = _pad_rows(jnp.stack([g_mix_pre[l], g_mix_post[l], g_ff_pre[l], g_ff_post[l]]))
        layers.append((wts, gvec, _pad_rows(mod_mine[l])))
        xs, sv = _layer_fwd(l, xs, *layers[l])
        saved.append(sv)
    dx, loss_part = _loss_head(xs, loss_target[0], "loss_head")
    small_grads = [None] * DEPTH
    mine, sibs, landed = ({} for _ in range(3))
    seq_id = iter(range(4, 4 + 4 * DEPTH))
    last = [gathered[DEPTH - 1][1]]

    def start(group, grads):
        mine[group] = grads
        sibs[group] = _sibling_exchange(grads, f"rs_sibling_{group}", sequencer_id=next(seq_id), after=last[0])
        last[0] = sibs[group][0]

    def finish(group, later):
        later, (grads, sib) = lax.optimization_barrier((later, (mine[group], sibs[group])))
        sends = [_pair_sums(g, p, route, f"rs_pair_sums_{group}_{k}") for k, (g, p) in enumerate(zip(grads, sib))]
        later, sends = lax.optimization_barrier((later, sends))
        landed[group] = _chip_exchange(sends, f"rs_chips_{group}", sequencer_id=next(seq_id), after=last[0])
        last[0] = landed[group][0]
        return later

    pending = None
    for l in reversed(range(DEPTH)):
        hook = (lambda da: da) if pending is None else functools.partial(finish, pending)
        dx, ffn_grads, ffn_reds = _ffn_bwd(l, dx, saved[l], *layers[l], hook)
        start(f"ffn_l{l}", ffn_grads)
        dx, mix_grads, small_grads[l] = _mixer_bwd(l, dx, saved[l], *layers[l], ffn_reds,
                                                   functools.partial(finish, f"ffn_l{l}"))
        start(f"mix_l{l}", mix_grads)
        pending = f"mix_l{l}"
    grad_x = dx[None]

    big_w = [wt_in, w_branch, w_out, w_ff1, w_ff2]
    big_m = [mt_in, m_w_branch, m_w_out, m_w_ff1, m_w_ff2]
    big_v = [vt_in, v_w_branch, v_w_out, v_w_ff1, v_w_ff2]
    where = [("mix", 0), ("mix", 1), ("mix", 2), ("ffn", 0), ("ffn", 1)]

    def reduce_and_update(k):
        group, at = where[k]
        return _reduce_adamw([mine[f"{group}_l{l}"][at] for l in range(DEPTH)],
                             [sibs[f"{group}_l{l}"][at] for l in range(DEPTH)],
                             [landed[f"{group}_l{l}"][at] for l in range(DEPTH)], place, big_w[k], big_m[k], big_v[k],
                             f"rs_sum_adamw_{k}")

    big_res = {k: list(reduce_and_update(k)) for k in (3, 4)}
    big_res[3][0] = finish(pending, big_res[3][0])

    small = {k: jnp.stack([small_grads[l][k] for l in range(DEPTH)]) for k in SMALL_KEYS}
    payload = _pack([small[k] for k in SMALL_KEYS] + [loss_part[0:1, 0:1]], 8, F32)
    small_all = _all_gather([payload], "gather_small")[0]
    dmod_all = small_all[:, 0:DEPTH * 6, :].reshape(N_DEV, DEPTH, 6 * D)
    summed = _unpack(_sum_slabs(small_all, "sum_small").reshape(-1), SMALL_SHAPES + [(1, 1)])
    sg = dict(zip(SMALL_KEYS, summed))
    loss = summed[-1][0, 0]
    dmod_cols = lax.dynamic_slice_in_dim(dmod_all, me * ADA_COLS, ADA_COLS, axis=2)
    dmod_cols = jnp.pad(jnp.transpose(dmod_cols, (1, 0, 2)), ((0, 0), (0, ADA_ROWS - N_DEV), (0, 0)))
    g_w_ada = _ada_bwd(c_pad, dmod_cols, "ada_bwd")
    g_conv_w = lax.dynamic_slice_in_dim(sg["conv_w"], me * (CONV_W // N_DEV), CONV_W // N_DEV, axis=2)

    ada_out = [g_w_ada] + list(_adamw(w_ada, g_w_ada, m_w_ada, v_w_ada, "adamw_ada"))
    rest_w = [b_ada, g_mix_pre, g_mix_post, g_ff_pre, g_ff_post, b_f, w_pool, pool_scale, conv_w]
    rest_m = [m_b_ada, m_g_mix_pre, m_g_mix_post, m_g_ff_pre, m_g_ff_post, m_b_f, m_w_pool, m_pool_scale, m_conv_w]
    rest_v = [v_b_ada, v_g_mix_pre, v_g_mix_post, v_g_ff_pre, v_g_ff_post, v_b_f, v_w_pool, v_pool_scale, v_conv_w]
    rest_g = [sg["mod"], sg["g_mix_pre"], sg["g_mix_post"], sg["g_ff_pre"], sg["g_ff_post"], sg["b_f"],
              sg["w_pool"], sg["pool_scale"], g_conv_w]
    rest_shapes = [a.shape for a in rest_w]
    upd = _adamw(_pack(rest_w, 8, F32)[None], _pack(rest_g, 8, F32)[None], _pack(rest_m, 8, F32)[None],
                 _pack(rest_v, 8, F32)[None], "adamw_rest")
    rest_out = [rest_g] + [_unpack(arr.reshape(-1), rest_shapes) for arr in upd]
    rest_out = [[ada_out[which]] + rest_out[which] for which in range(4)]

    landed[pending], rest_out = lax.optimization_barrier((landed[pending], rest_out))
    big_res.update({k: reduce_and_update(k) for k in (0, 1, 2)})
    big_out = [[jnp.transpose(big_res[k][which], (0, 2, 1)) if k == 0 else big_res[k][which] for k in range(5)]
               for which in range(4)]

    def ordered(k):
        r, b = rest_out[k], big_out[k]
        return [r[0], r[1], r[2], r[3], r[4], r[5], b[0], r[6], r[7], r[8], r[9], b[1], b[2], b[3], b[4]]

    return (loss, grad_x, *ordered(0), *ordered(1), *ordered(2), *ordered(3))
```

```python
import functools

import jax
import jax.numpy as jnp
from jax import lax
from jax.experimental import pallas as pl
from jax.experimental.pallas import tpu as pltpu
from jax.experimental.pallas import tpu_sc as plsc

F32 = jnp.float32
BF16 = jnp.bfloat16
GRAD_DTYPE = BF16

N_DEV = 8
D = 1024
S = 2048
DEPTH = 2
D_FF = 4 * D
A_WIDTH = 512
HEAD_DIM = 64
N_PAIR = 4
POOL_W = 256
CONV_W = 256
IN_COLS = 5640
ADA_COLS = 6 * D // N_DEV
IN_SHARD = IN_COLS // N_DEV
RMS_EPS = 1e-6
NEG_INF = -1e30
ATT_SCALE = HEAD_DIM ** -0.5

NZ = 5760
Z_PC = 0
Z_G = 1024
Z_Q = 4096
Z_K = 4608
Z_V = 5120
Z_F = 5632

LR, B1, B2, EPS, WD, STEP = 0.001, 0.9, 0.999, 1e-08, 0.01, 10

LANE = 128
VMEM_LIMIT_BYTES = 48 * 1024 * 1024
TS = 512
TQ = 256
TQ_FWD = 512
HEADS_PER_STEP = 8
HEADS_PER_STEP_FWD = 8


def _params(sem=None):
    return pltpu.CompilerParams(dimension_semantics=sem, vmem_limit_bytes=VMEM_LIMIT_BYTES)


def _pick(n, target):
    best = None
    for t in range(LANE, min(n, target) + 1, LANE):
        if n % t == 0:
            best = t
    return n if best is None else best


def _matmul(a, b, mode, name, out_dtype=F32, tm=2048, tn=1024, tk=2048, b_col_shards=False, out_col_shards=False,
            extra=(), epilogue=None, out_dtypes=None):
    if b_col_shards:
        shards, b_rows, shard_cols = b.shape
        b_shape = (b_rows, shards * shard_cols)
    else:
        b_shape = b.shape
    if mode == "nn":
        (m, k), (k2, n) = a.shape, b_shape
    elif mode == "nt":
        (m, k), (n, k2) = a.shape, b_shape
    else:
        (k, m), (k2, n) = a.shape, b_shape
    assert k == k2, (a.shape, b.shape, mode)
    tm, tn, tk = _pick(m, tm), _pick(n, tn), _pick(k, tk)
    if b_col_shards and mode == "nn":
        tn = shard_cols
    per_step = 1
    if b_col_shards and mode == "nt":
        per_step = max(1, min(tk, 1024) // shard_cols)
        tk = per_step * shard_cols
    if out_col_shards:
        tn = n // N_DEV
    nk = k // tk
    if mode == "nn":
        a_spec = pl.BlockSpec((tm, tk), lambda i, j, kk: (i, kk))
        b_spec = (pl.BlockSpec((None, tk, tn), lambda i, j, kk: (j, kk, 0)) if b_col_shards else
                  pl.BlockSpec((tk, tn), lambda i, j, kk: (kk, j)))
        dims = (((1,), (0,)), ((), ()))
    elif mode == "nt":
        a_spec = pl.BlockSpec((tm, tk), lambda i, j, kk: (i, kk))
        b_spec = (pl.BlockSpec((per_step, tn, shard_cols), lambda i, j, kk: (kk, j, 0)) if b_col_shards else
                  pl.BlockSpec((tn, tk), lambda i, j, kk: (j, kk)))
        dims = (((1,), (1,)), ((), ()))
    else:
        assert not b_col_shards
        a_spec = pl.BlockSpec((tk, tm), lambda i, j, kk: (kk, i))
        b_spec = pl.BlockSpec((tk, tn), lambda i, j, kk: (kk, j))
        dims = (((0,), (0,)), ((), ()))
    if out_col_shards:
        out_shape = jax.ShapeDtypeStruct((N_DEV, m, tn), out_dtype)
        out_spec = pl.BlockSpec((None, tm, tn), lambda i, j, kk: (j, i, 0))
    else:
        out_shape = jax.ShapeDtypeStruct((m, n), out_dtype)
        out_spec = pl.BlockSpec((tm, tn), lambda i, j, kk: (i, j))

    n_extra = len(extra)
    extra_specs = [pl.BlockSpec((tm, tn), lambda i, j, kk, off=off: (i, j + off // tn)) for _, off in extra]
    if epilogue is not None:
        assert not out_col_shards and all(off % tn == 0 for _, off in extra)
        out_shape = [jax.ShapeDtypeStruct((m, n), dt) for dt in out_dtypes]
        out_spec = [pl.BlockSpec((tm, tn), lambda i, j, kk: (i, j)) for _ in out_dtypes]

    def product(a_ref, b_ref):
        if b_col_shards and mode == "nt":
            b_tile = jnp.concatenate([b_ref[s] for s in range(per_step)], axis=1) if per_step > 1 else b_ref[0]
        else:
            b_tile = b_ref[...]
        return lax.dot_general(a_ref[...].astype(BF16), b_tile.astype(BF16), dims, preferred_element_type=F32)

    def write(acc, extra_refs, o_refs):
        if epilogue is None:
            o_refs[0][...] = acc.astype(out_dtype)
        else:
            for o_ref, tile in zip(o_refs, epilogue(acc, *[r[...] for r in extra_refs])):
                o_ref[...] = tile.astype(o_ref.dtype)

    def body_one_pass(a_ref, b_ref, *refs):
        write(product(a_ref, b_ref), refs[:n_extra], refs[n_extra:])

    def body(a_ref, b_ref, *refs):
        acc_ref = refs[-1]
        kk = pl.program_id(2)

        @pl.when(kk == 0)
        def _():
            acc_ref[...] = product(a_ref, b_ref)

        @pl.when(kk > 0)
        def _():
            acc_ref[...] += product(a_ref, b_ref)

        @pl.when(kk == nk - 1)
        def _():
            write(acc_ref[...], refs[:n_extra], refs[n_extra:-1])

    return pl.pallas_call(
        body_one_pass if nk == 1 else body, name=name,
        out_shape=out_shape,
        grid=(m // tm, n // tn, nk),
        in_specs=[a_spec, b_spec] + extra_specs,
        out_specs=out_spec,
        scratch_shapes=[] if nk == 1 else [pltpu.VMEM((tm, tn), F32)],
        compiler_params=_params(("parallel", "parallel", "arbitrary")),
    )(a, b, *[x for x, _ in extra])


def _row_spec(width=D, col=0):
    return pl.BlockSpec((TS, width), lambda i: (i, col))


def _vec_spec(rows=8, width=D):
    return pl.BlockSpec((rows, width), lambda i: (0, 0))


def _rms(x):
    return lax.rsqrt(jnp.mean(x * x, axis=-1, keepdims=True) + RMS_EPS)


def _prenorm_fwd(x, gvec, mod, g_row, shift_row, scale_row, name):
    def body(x_ref, g_ref, mod_ref, h_ref):
        xv = x_ref[...]
        y = xv * _rms(xv) * g_ref[g_row:g_row + 1, :]
        h = y * (1.0 + mod_ref[scale_row:scale_row + 1, :]) + mod_ref[shift_row:shift_row + 1, :]
        h_ref[...] = h.astype(BF16)

    return pl.pallas_call(
        body, name=name, out_shape=jax.ShapeDtypeStruct((S, D), BF16), grid=(S // TS,),
        in_specs=[_row_spec(), _vec_spec(), _vec_spec()], out_specs=_row_spec(),
        compiler_params=_params(("parallel",)),
    )(x, gvec, mod)


def _prenorm_bwd(x, gvec, mod, dh, dres, g_row, scale_row, name):
    def body(x_ref, g_ref, mod_ref, dh_ref, dres_ref, dx_ref, red_ref):
        i = pl.program_id(0)

        @pl.when(i == 0)
        def _():
            red_ref[...] = jnp.zeros_like(red_ref)

        xv = x_ref[...]
        g = g_ref[g_row:g_row + 1, :]
        r = _rms(xv)
        n = xv * r
        yg = n * g
        dhv = dh_ref[...]
        dyg = dhv * (1.0 + mod_ref[scale_row:scale_row + 1, :])
        dn = dyg * g
        dx = r * (dn - n * jnp.mean(dn * n, axis=-1, keepdims=True))
        dx_ref[...] = dres_ref[...] + dx
        red_ref[0:1, :] += jnp.sum(dhv, axis=0, keepdims=True)
        red_ref[1:2, :] += jnp.sum(dhv * yg, axis=0, keepdims=True)
        red_ref[2:3, :] += jnp.sum(dyg * n, axis=0, keepdims=True)

    return pl.pallas_call(
        body, name=name,
        out_shape=(jax.ShapeDtypeStruct((S, D), F32), jax.ShapeDtypeStruct((8, D), F32)),
        grid=(S // TS,),
        in_specs=[_row_spec(), _vec_spec(), _vec_spec(), _row_spec(), _row_spec()],
        out_specs=(_row_spec(), _vec_spec()),
        compiler_params=_params(("arbitrary",)),
    )(x, gvec, mod, dh, dres)


def _postnorm_fwd(x, y, gvec, mod, g_row, gate_row, name):
    def body(x_ref, y_ref, g_ref, mod_ref, o_ref):
        yv = y_ref[...]
        yn = yv * _rms(yv) * g_ref[g_row:g_row + 1, :]
        o_ref[...] = x_ref[...] + mod_ref[gate_row:gate_row + 1, :] * yn

    return pl.pallas_call(
        body, name=name, out_shape=jax.ShapeDtypeStruct((S, D), F32), grid=(S // TS,),
        in_specs=[_row_spec(), _row_spec(), _vec_spec(), _vec_spec()], out_specs=_row_spec(),
        compiler_params=_params(("parallel",)),
    )(x, y, gvec, mod)


def _postnorm_bwd(y, gvec, mod, dxo, g_row, gate_row, name):
    def body(y_ref, g_ref, mod_ref, dxo_ref, dy_ref, red_ref):
        i = pl.program_id(0)

        @pl.when(i == 0)
        def _():
            red_ref[...] = jnp.zeros_like(red_ref)

        yv = y_ref[...]
        g = g_ref[g_row:g_row + 1, :]
        r = _rms(yv)
        n = yv * r
        dxo = dxo_ref[...]
        dyn = dxo * mod_ref[gate_row:gate_row + 1, :]
        dn = dyn * g
        dy = r * (dn - n * jnp.mean(dn * n, axis=-1, keepdims=True))
        dy_ref[...] = dy.astype(BF16)
        red_ref[0:1, :] += jnp.sum(dxo * (n * g), axis=0, keepdims=True)
        red_ref[1:2, :] += jnp.sum(dyn * n, axis=0, keepdims=True)

    return pl.pallas_call(
        body, name=name,
        out_shape=(jax.ShapeDtypeStruct((S, D), BF16), jax.ShapeDtypeStruct((8, D), F32)),
        grid=(S // TS,),
        in_specs=[_row_spec(), _vec_spec(), _vec_spec(), _row_spec()],
        out_specs=(_row_spec(), _vec_spec()),
        compiler_params=_params(("arbitrary",)),
    )(y, gvec, mod, dxo)


def _concat_columns(pieces, name):
    widths = [p.shape[1] for p in pieces]
    offsets = [sum(widths[:k]) for k in range(len(widths))]

    def body(*refs):
        o_ref = refs[-1]
        for ref, off, w in zip(refs[:-1], offsets, widths):
            o_ref[:, off:off + w] = ref[...]

    return pl.pallas_call(
        body, name=name, out_shape=jax.ShapeDtypeStruct((S, sum(widths)), pieces[0].dtype), grid=(S // TS,),
        in_specs=[_row_spec(w) for w in widths], out_specs=_row_spec(sum(widths)),
        compiler_params=_params(("parallel",)),
    )(*pieces)


def _loss_head(xf, target, name):
    def body(x_ref, t_ref, dx_ref, loss_ref):
        i = pl.program_id(0)

        @pl.when(i == 0)
        def _():
            loss_ref[...] = jnp.zeros_like(loss_ref)

        e = x_ref[...] - t_ref[...]
        dx_ref[...] = e / float(D)
        per_tok = jnp.mean(e * e, axis=-1, keepdims=True)
        loss_ref[0:1, 0:1] += 0.5 * jnp.sum(per_tok, axis=0, keepdims=True)

    return pl.pallas_call(
        body, name=name,
        out_shape=(jax.ShapeDtypeStruct((S, D), F32), jax.ShapeDtypeStruct((8, LANE), F32)),
        grid=(S // TS,),
        in_specs=[_row_spec(), _row_spec()],
        out_specs=(_row_spec(), pl.BlockSpec((8, LANE), lambda i: (0, 0))),
        compiler_params=_params(("arbitrary",)),
    )(xf, target)


def _relu2_epilogue(a):
    t = jnp.maximum(a, 0.0)
    return a, t * t


def _relu2_bwd_epilogue(dr, a):
    return (dr * (2.0 * jnp.maximum(a, 0.0)),)


def _merge_epilogue(pc, g0, g1, g2, pa, pb):
    return pc, jax.nn.sigmoid(g0) * pa + jax.nn.sigmoid(g1) * pb + jax.nn.sigmoid(g2) * pc


def _merge_bwd_epilogue(dm, g0, g1, g2, pa, pb, pc):
    sg = [jax.nn.sigmoid(g) for g in (g0, g1, g2)]
    return tuple(dm * s for s in sg) + tuple(dm * p * (s * (1.0 - s)) for p, s in zip((pa, pb, pc), sg))


def _shift_down(x, k, row):
    return jnp.where(row >= k, pltpu.roll(x, k, axis=0), 0.0)


def _shift_up(x, k, row):
    n = x.shape[0]
    return jnp.where(row < n - k, pltpu.roll(x, n - k, axis=0), 0.0)


def _cumsum_rows(x, row, reverse=False):
    shift = _shift_up if reverse else _shift_down
    k = 1
    while k < x.shape[0]:
        x = x + shift(x, k, row)
        k *= 2
    return x


def _full_spec(shape, idx=(0, 0)):
    return pl.BlockSpec(shape, lambda i: idx)


def _pool_window_select(lane, a2, a4, a8, a16):
    return jnp.where(lane < 64, a2, jnp.where(lane < 128, a4, jnp.where(lane < 192, a8, a16)))


def _pool_p(u, row, lane):
    t2 = u + _shift_down(u, 1, row)
    t4 = t2 + _shift_down(t2, 2, row)
    t8 = t4 + _shift_down(t4, 4, row)
    t16 = t8 + _shift_down(t8, 8, row)
    tw = _pool_window_select(lane, t2, t4, t8, t16)
    cnt = jnp.minimum((row + 1).astype(F32), _pool_window_select(lane, 2.0, 4.0, 8.0, 16.0))
    return tw / cnt - u, cnt


def _pool_fwd(z, wp_bd, pscale, name):
    def body(u_ref, w_ref, s_ref, o_ref):
        row = lax.broadcasted_iota(jnp.int32, (S, POOL_W), 0)
        lane = lax.broadcasted_iota(jnp.int32, (S, POOL_W), 1)
        p, _ = _pool_p(u_ref[...], row, lane)
        y = jnp.dot(p.astype(BF16), w_ref[...], preferred_element_type=F32)
        o_ref[...] = y * s_ref[0:1, :]

    return pl.pallas_call(
        body, name=name, out_shape=jax.ShapeDtypeStruct((S, POOL_W), F32), grid=(1,),
        in_specs=[_full_spec((S, POOL_W), (0, Z_PC // POOL_W)), _full_spec((POOL_W, POOL_W)), _full_spec((8, POOL_W))],
        out_specs=_full_spec((S, POOL_W)),
        compiler_params=_params(("arbitrary",)),
    )(z, wp_bd, pscale)


def _pool_bwd(z, wp_bd, pscale, dbr, name):
    def body(u_ref, w_ref, s_ref, dbr_ref, du_ref, dw_ref, red_ref):
        row = lax.broadcasted_iota(jnp.int32, (S, POOL_W), 0)
        lane = lax.broadcasted_iota(jnp.int32, (S, POOL_W), 1)
        p, cnt = _pool_p(u_ref[...], row, lane)
        pb = p.astype(BF16)
        y = jnp.dot(pb, w_ref[...], preferred_element_type=F32)
        dbr = dbr_ref[...]
        red_ref[...] = jnp.zeros_like(red_ref)
        red_ref[0:1, :] = jnp.sum(dbr * y, axis=0, keepdims=True)
        dy = (dbr * s_ref[0:1, :]).astype(BF16)
        dw_ref[...] = lax.dot_general(pb, dy, (((0,), (0,)), ((), ())), preferred_element_type=F32)
        dp = lax.dot_general(dy, w_ref[...], (((1,), (1,)), ((), ())), preferred_element_type=F32)
        g = dp / cnt
        a2 = g + _shift_up(g, 1, row)
        a4 = a2 + _shift_up(a2, 2, row)
        a8 = a4 + _shift_up(a4, 4, row)
        a16 = a8 + _shift_up(a8, 8, row)
        du_ref[...] = (_pool_window_select(lane, a2, a4, a8, a16) - dp).astype(BF16)

    return pl.pallas_call(
        body, name=name,
        out_shape=(jax.ShapeDtypeStruct((S, POOL_W), BF16), jax.ShapeDtypeStruct((POOL_W, POOL_W), F32),
                   jax.ShapeDtypeStruct((8, POOL_W), F32)),
        grid=(1,),
        in_specs=[_full_spec((S, POOL_W), (0, Z_PC // POOL_W)), _full_spec((POOL_W, POOL_W)), _full_spec((8, POOL_W)),
                  _full_spec((S, POOL_W))],
        out_specs=(_full_spec((S, POOL_W)), _full_spec((POOL_W, POOL_W)), _full_spec((8, POOL_W))),
        compiler_params=_params(("arbitrary",)),
    )(z, wp_bd, pscale, dbr)


def _conv_specs():
    base = Z_PC // CONV_W
    return [_full_spec((S, CONV_W), (0, base + 1)), _full_spec((S, CONV_W), (0, base + 2)),
            _full_spec((S, CONV_W), (0, base + 3)), _full_spec((8, CONV_W))]


def _conv_fwd(z, cw, name):
    def body(h_ref, b_ref, c_ref, w_ref, o_ref):
        row = lax.broadcasted_iota(jnp.int32, (S, CONV_W), 0)
        u = c_ref[...] * h_ref[...]
        y = (w_ref[0:1, :] * _shift_down(u, 2, row) + w_ref[1:2, :] * _shift_down(u, 1, row) + w_ref[2:3, :] * u)
        o_ref[...] = b_ref[...] * y

    return pl.pallas_call(
        body, name=name, out_shape=jax.ShapeDtypeStruct((S, CONV_W), F32), grid=(1,),
        in_specs=_conv_specs(), out_specs=_full_spec((S, CONV_W)),
        compiler_params=_params(("arbitrary",)),
    )(z, z, z, cw)


def _conv_bwd(z, cw, dbr, name):
    def body(h_ref, b_ref, c_ref, w_ref, dbr_ref, d_ref, red_ref):
        row = lax.broadcasted_iota(jnp.int32, (S, CONV_W), 0)
        h, cg = h_ref[...], c_ref[...]
        u = cg * h
        u1 = _shift_down(u, 1, row)
        u2 = _shift_down(u, 2, row)
        y = w_ref[0:1, :] * u2 + w_ref[1:2, :] * u1 + w_ref[2:3, :] * u
        dbr = dbr_ref[...]
        dy = dbr * b_ref[...]
        du = w_ref[2:3, :] * dy + w_ref[1:2, :] * _shift_up(dy, 1, row) + w_ref[0:1, :] * _shift_up(dy, 2, row)
        d_ref[:, 0:CONV_W] = (du * cg).astype(BF16)
        d_ref[:, CONV_W:2 * CONV_W] = (dbr * y).astype(BF16)
        d_ref[:, 2 * CONV_W:3 * CONV_W] = (du * h).astype(BF16)
        red_ref[...] = jnp.zeros_like(red_ref)
        red_ref[0:1, :] = jnp.sum(dy * u2, axis=0, keepdims=True)
        red_ref[1:2, :] = jnp.sum(dy * u1, axis=0, keepdims=True)
        red_ref[2:3, :] = jnp.sum(dy * u, axis=0, keepdims=True)

    return pl.pallas_call(
        body, name=name,
        out_shape=(jax.ShapeDtypeStruct((S, 3 * CONV_W), BF16), jax.ShapeDtypeStruct((8, CONV_W), F32)),
        grid=(1,),
        in_specs=_conv_specs() + [_full_spec((S, CONV_W))],
        out_specs=(_full_spec((S, 3 * CONV_W)), _full_spec((8, CONV_W))),
        compiler_params=_params(("arbitrary",)),
    )(z, z, z, cw, dbr)


_NT = (((1,), (1,)), ((), ()))
_TN = (((0,), (0,)), ((), ()))
N_HEAD = 2 * N_PAIR


def _split3(x):
    hi = x.astype(BF16).astype(F32)
    mid = (x - hi).astype(BF16).astype(F32)
    lo = (x - hi - mid).astype(BF16).astype(F32)
    return hi, mid, lo


def _spare(lane, e, k):
    return lane == 64 * (1 - e) + k


def _spare3(lane, e, k):
    base = 64 * (1 - e) + k
    return (lane >= base) & (lane < base + 3)


def _put3(lane, e, k, pieces, rest):
    out = rest
    for n, piece in enumerate(pieces):
        out = jnp.where(_spare(lane, e, k + n), piece, out)
    return out


def _attn_prep(z, bf, name):
    def body(q_ref, k_ref, v_ref, f_ref, b_ref, qa_ref, ka_ref, va_ref, kat_ref, cum_ref):
        p = pl.program_id(0)
        row = lax.broadcasted_iota(jnp.int32, (S, LANE), 0)
        lane = lax.broadcasted_iota(jnp.int32, (S, LANE), 1)

        @pl.when(p == 0)
        def _():
            xv = f_ref[...] + b_ref[0:1, :]
            ls = jnp.minimum(xv, 0.0) - jnp.log(1.0 + jnp.exp(-jnp.abs(xv)))
            cum_ref[...] = _cumsum_rows(jnp.where(lane < N_HEAD, ls, 0.0), row)

        cum = cum_ref[...]
        q, k, v = q_ref[...], k_ref[...], v_ref[...]
        for e in range(2):
            head = (lane >= 64) if e else (lane < 64)
            f = jnp.sum(jnp.where(lane == 2 * p + e, cum, 0.0), axis=1, keepdims=True)
            pieces = _split3(f)
            qa = jnp.where(head, q * ATT_SCALE, _put3(lane, e, 0, pieces, jnp.where(_spare3(lane, e, 3), 1.0, 0.0)))
            ones = jnp.where(_spare3(lane, e, 0) | _spare3(lane, e, 6), 1.0, 0.0)
            ka = jnp.where(head, k, _put3(lane, e, 3, [-x for x in pieces], ones))
            va = jnp.where(head, v, jnp.where(_spare3(lane, e, 0), 1.0, 0.0))
            qa_ref[e] = qa.astype(BF16)
            ka_ref[e] = ka.astype(BF16)
            va_ref[e] = va.astype(BF16)
            kat_ref[e] = ka.T.astype(BF16)

    qb, kb, vb = Z_Q // LANE, Z_K // LANE, Z_V // LANE
    heads = jax.ShapeDtypeStruct((N_HEAD, S, LANE), BF16)
    pair = pl.BlockSpec((2, S, LANE), lambda p: (p, 0, 0))
    return pl.pallas_call(
        body, name=name,
        out_shape=(heads, heads, heads, jax.ShapeDtypeStruct((N_HEAD, LANE, S), BF16)),
        grid=(N_PAIR,),
        in_specs=[pl.BlockSpec((S, LANE), lambda p: (0, qb + p)), pl.BlockSpec((S, LANE), lambda p: (0, kb + p)),
                  pl.BlockSpec((S, LANE), lambda p: (0, vb + p)), pl.BlockSpec((S, LANE), lambda p: (0, Z_F // LANE)),
                  pl.BlockSpec((8, LANE), lambda p: (0, 0))],
        out_specs=(pair, pair, pair, pl.BlockSpec((2, LANE, S), lambda p: (p, 0, 0))),
        scratch_shapes=[pltpu.VMEM((S, LANE), F32)],
        compiler_params=_params(("arbitrary",)),
    )(z, z, z, z, bf)


def _attn_bwd_prep(qa, o, lse, do, name):
    def body(qa_ref, o_ref, lse_ref, do_ref, qa2_ref, doa_ref):
        lane = lax.broadcasted_iota(jnp.int32, (S, LANE), 1)
        dov, ov, lsev = do_ref[...], o_ref[...], lse_ref[...]
        for e in range(2):
            head = (lane >= 64) if e else (lane < 64)
            dsum = jnp.sum(jnp.where(head, dov * ov, 0.0), axis=1, keepdims=True)
            doa_ref[e] = jnp.where(head, dov, _put3(lane, e, 0, [-x for x in _split3(dsum)], 0.0)).astype(BF16)
            lse_col = lsev[:, 64 * e:64 * e + 1]
            qa2_ref[e] = _put3(lane, e, 6, [-x for x in _split3(lse_col)], qa_ref[e].astype(F32)).astype(BF16)

    heads = jax.ShapeDtypeStruct((N_HEAD, S, LANE), BF16)
    pair = pl.BlockSpec((2, S, LANE), lambda p: (p, 0, 0))
    cols = pl.BlockSpec((S, LANE), lambda p: (0, p))
    return pl.pallas_call(
        body, name=name, out_shape=(heads, heads), grid=(N_PAIR,),
        in_specs=[pair, cols, cols, cols], out_specs=(pair, pair),
        compiler_params=_params(("parallel",)),
    )(qa, o, lse, do)


def _attn_bwd_post(z, bf, dqt, dka, dva, name):
    def body(f_ref, b_ref, dqt_ref, dk_ref, dv_ref, dq_out, dk_out, dv_out, dfl_ref, red_ref, dcum_ref):
        p = pl.program_id(0)

        @pl.when(p == 0)
        def _():
            dcum_ref[...] = jnp.zeros_like(dcum_ref)

        row = lax.broadcasted_iota(jnp.int32, (S, LANE), 0)
        lane = lax.broadcasted_iota(jnp.int32, (S, LANE), 1)
        dqa = [dqt_ref[e].T for e in range(2)]
        dq_out[...] = (jnp.where(lane < 64, dqa[0], dqa[1]) * ATT_SCALE).astype(BF16)
        dk_out[...] = jnp.where(lane < 64, dk_ref[0], dk_ref[1]).astype(BF16)
        dv_out[...] = jnp.where(lane < 64, dv_ref[0], dv_ref[1]).astype(BF16)
        for e in range(2):
            d_query = jnp.sum(jnp.where(_spare(lane, e, 0), dqa[e], 0.0), axis=1, keepdims=True)
            d_key = jnp.sum(jnp.where(_spare(lane, e, 3), dk_ref[e], 0.0), axis=1, keepdims=True)
            dcum_ref[...] += jnp.where(lane == 2 * p + e, d_query - d_key, 0.0)

        @pl.when(p == N_PAIR - 1)
        def _():
            dls = _cumsum_rows(dcum_ref[...], row, reverse=True)
            xv = f_ref[...] + b_ref[0:1, :]
            dx = jnp.where(lane < N_HEAD, dls * jax.nn.sigmoid(-xv), 0.0)
            dfl_ref[...] = dx.astype(BF16)
            red_ref[...] = jnp.zeros_like(red_ref)
            red_ref[0:1, :] = jnp.sum(dx, axis=0, keepdims=True)

    wide = jax.ShapeDtypeStruct((S, N_PAIR * LANE), BF16)
    cols = pl.BlockSpec((S, LANE), lambda p: (0, p))
    pair = pl.BlockSpec((2, S, LANE), lambda p: (p, 0, 0))
    return pl.pallas_call(
        body, name=name,
        out_shape=(wide, wide, wide, jax.ShapeDtypeStruct((S, LANE), BF16), jax.ShapeDtypeStruct((8, LANE), F32)),
        grid=(N_PAIR,),
        in_specs=[pl.BlockSpec((S, LANE), lambda p: (0, Z_F // LANE)), pl.BlockSpec((8, LANE), lambda p: (0, 0)),
                  pl.BlockSpec((2, LANE, S), lambda p: (p, 0, 0)), pair, pair],
        out_specs=(cols, cols, cols, pl.BlockSpec((S, LANE), lambda p: (0, 0)), pl.BlockSpec((8, LANE), lambda p: (0, 0))),
        scratch_shapes=[pltpu.VMEM((S, LANE), F32)],
        compiler_params=_params(("arbitrary",)),
    )(z, bf, dqt, dka, dva)


def _attn_fwd(qa, ka, va, name):
    tq, tk = TQ_FWD, TQ
    ratio = tq // tk

    def body(qa_ref, ka_ref, va_ref, o_ref, lse_ref):
        i = pl.program_id(1)
        lane = lax.broadcasted_iota(jnp.int32, (tq, LANE), 1)
        row = lax.broadcasted_iota(jnp.int32, (tq, tk), 0)
        col = lax.broadcasted_iota(jnp.int32, (tq, tk), 1)
        nh = HEADS_PER_STEP_FWD
        qs = [qa_ref[h] for h in range(nh)]

        def block(j, carry, masked):
            off = pl.multiple_of(j * tk, tk)
            out = []
            for h in range(nh):
                m, acc = carry[h]
                s = lax.dot_general(qs[h], ka_ref[h, pl.ds(off, tk), :], _NT, preferred_element_type=F32)
                if masked:
                    s = jnp.where(col + (j - ratio * i) * tk > row, NEG_INF, s)
                mn = jnp.maximum(m, jnp.max(s, axis=1, keepdims=True))
                p = jnp.exp(s - mn).astype(BF16)
                acc = jnp.exp(m - mn) * acc + jnp.dot(p, va_ref[h, pl.ds(off, tk), :], preferred_element_type=F32)
                out.append((mn, acc))
            return tuple(out)

        init = (jnp.full((tq, 1), NEG_INF, F32), jnp.zeros((tq, LANE), F32))
        carry = lax.fori_loop(0, ratio * i, lambda j, c: block(j, c, False), (init,) * nh)
        for d in range(ratio):
            carry = block(ratio * i + d, carry, True)
        res = []
        for h in range(nh):
            m, acc = carry[h]
            l = jnp.sum(jnp.where(_spare(lane, h % 2, 0), acc, 0.0), axis=1, keepdims=True)
            res.append((acc / l, m + jnp.log(l)))
        for g in range(nh // 2):
            o_ref[:, g * LANE:(g + 1) * LANE] = jnp.where(lane < 64, res[2 * g][0], res[2 * g + 1][0])
            lse_ref[:, g * LANE:(g + 1) * LANE] = jnp.where(lane < 64, res[2 * g][1], res[2 * g + 1][1])

    nh = HEADS_PER_STEP_FWD
    out = jax.ShapeDtypeStruct((S, N_PAIR * LANE), F32)
    wide = pl.BlockSpec((tq, 64 * nh), lambda p, i: (i, p))
    return pl.pallas_call(
        body, name=name, out_shape=(out, out), grid=(N_HEAD // nh, S // tq),
        in_specs=[pl.BlockSpec((nh, tq, LANE), lambda p, i: (p, i, 0)), pl.BlockSpec((nh, S, LANE), lambda p, i: (p, 0, 0)),
                  pl.BlockSpec((nh, S, LANE), lambda p, i: (p, 0, 0))],
        out_specs=(wide, wide),
        compiler_params=_params(("parallel", "parallel")),
    )(qa, ka, va)


def _attn_bwd(qa2, ka, va, kat, doa, name):
    nq = S // TQ

    def body(qa_ref, ka_ref, va_ref, kat_ref, doa_ref, dqt_ref, dk_ref, dv_ref):
        j = pl.program_id(1)

        @pl.when(j == 0)
        def _():
            dqt_ref[...] = jnp.zeros_like(dqt_ref)

        key = lax.broadcasted_iota(jnp.int32, (TQ, TQ), 0)
        qry = lax.broadcasted_iota(jnp.int32, (TQ, TQ), 1)
        nh = HEADS_PER_STEP
        kav, vav, katv = ([ref[h] for h in range(nh)] for ref in (ka_ref, va_ref, kat_ref))

        def block(i, carry, masked):
            off = pl.multiple_of(i * TQ, TQ)
            out = []
            for h in range(nh):
                dk_acc, dv_acc = carry[h]
                qav = qa_ref[h, pl.ds(off, TQ), :]
                doav = doa_ref[h, pl.ds(off, TQ), :]
                s_t = lax.dot_general(kav[h], qav, _NT, preferred_element_type=F32)
                if masked:
                    s_t = jnp.where(key > qry, NEG_INF, s_t)
                p_t = jnp.exp(s_t)
                ds_t = p_t * lax.dot_general(vav[h], doav, _NT, preferred_element_type=F32)
                dsb = ds_t.astype(BF16)
                dv_acc = dv_acc + jnp.dot(p_t.astype(BF16), doav, preferred_element_type=F32)
                dk_acc = dk_acc + jnp.dot(dsb, qav, preferred_element_type=F32)
                dqt_ref[h, :, pl.ds(off, TQ)] += jnp.dot(katv[h], dsb, preferred_element_type=F32)
                out.append((dk_acc, dv_acc))
            return tuple(out)

        zero = (jnp.zeros((TQ, LANE), F32), jnp.zeros((TQ, LANE), F32))
        carry = block(j, (zero,) * nh, True)
        carry = lax.fori_loop(j + 1, nq, lambda i, c: block(i, c, False), carry)
        for h in range(nh):
            dk_ref[h], dv_ref[h] = carry[h]

    nh = HEADS_PER_STEP
    full = pl.BlockSpec((nh, S, LANE), lambda p, j: (p, 0, 0))
    blk = pl.BlockSpec((nh, TQ, LANE), lambda p, j: (p, j, 0))
    acc = jax.ShapeDtypeStruct((N_HEAD, S, LANE), F32)
    return pl.pallas_call(
        body, name=name,
        out_shape=(jax.ShapeDtypeStruct((N_HEAD, LANE, S), F32), acc, acc),
        grid=(N_HEAD // nh, nq),
        in_specs=[full, blk, blk, pl.BlockSpec((nh, LANE, TQ), lambda p, j: (p, 0, j)), full],
        out_specs=(pl.BlockSpec((nh, LANE, S), lambda p, j: (p, 0, 0)), blk, blk),
        compiler_params=_params(("arbitrary", "arbitrary")),
    )(qa2, ka, va, kat, doa)


ADA_ROWS = 16


def _ada_fwd(c_pad, w_ada, b_cols, name):
    def body(c_ref, w_ref, b_ref, o_ref):
        cv = c_ref[...]
        sc = (cv * jax.nn.sigmoid(cv)).astype(BF16)
        o_ref[0] = jnp.dot(sc, w_ref[0].astype(BF16), preferred_element_type=F32) + b_ref[0, 0:1, :]

    return pl.pallas_call(
        body, name=name, out_shape=jax.ShapeDtypeStruct((DEPTH, ADA_ROWS, ADA_COLS), F32), grid=(DEPTH,),
        in_specs=[pl.BlockSpec((ADA_ROWS, D), lambda l: (0, 0)), pl.BlockSpec((1, D, ADA_COLS), lambda l: (l, 0, 0)),
                  pl.BlockSpec((1, 8, ADA_COLS), lambda l: (l, 0, 0))],
        out_specs=pl.BlockSpec((1, ADA_ROWS, ADA_COLS), lambda l: (l, 0, 0)),
        compiler_params=_params(("parallel",)),
    )(c_pad, w_ada, b_cols)


def _ada_bwd(c_pad, dmod_cols, name):
    def body(c_ref, d_ref, o_ref):
        cv = c_ref[...]
        sc = (cv * jax.nn.sigmoid(cv)).astype(BF16)
        o_ref[0] = lax.dot_general(sc, d_ref[0].astype(BF16), _TN, preferred_element_type=F32)

    return pl.pallas_call(
        body, name=name, out_shape=jax.ShapeDtypeStruct((DEPTH, D, ADA_COLS), F32), grid=(DEPTH,),
        in_specs=[pl.BlockSpec((ADA_ROWS, D), lambda l: (0, 0)), pl.BlockSpec((1, ADA_ROWS, ADA_COLS), lambda l: (l, 0, 0))],
        out_specs=pl.BlockSpec((1, D, ADA_COLS), lambda l: (l, 0, 0)),
        compiler_params=_params(("parallel",)),
    )(c_pad, dmod_cols)


def _adamw_math(w, g, m, v):
    m = B1 * m + (1.0 - B1) * g
    v = B2 * v + (1.0 - B2) * (g * g)
    m_hat = m / (1.0 - B1 ** STEP)
    v_hat = v / (1.0 - B2 ** STEP)
    delta = -LR * (m_hat / (jnp.sqrt(v_hat) + EPS) + WD * w)
    return delta, m, v


def _row_tile(rows, target=256):
    best = 8
    for t in range(8, min(rows, target) + 1, 8):
        if rows % t == 0:
            best = t
    return best


def _adamw(w, g, m, v, name):
    layers, rows, cols = w.shape
    tr = _row_tile(rows)
    spec = pl.BlockSpec((1, tr, cols), lambda l, i: (l, i, 0))

    def body(w_ref, g_ref, m_ref, v_ref, d_ref, nm_ref, nv_ref):
        d_ref[...], nm_ref[...], nv_ref[...] = _adamw_math(w_ref[...], g_ref[...], m_ref[...], v_ref[...])

    out = jax.ShapeDtypeStruct(w.shape, F32)
    return pl.pallas_call(
        body, name=name, out_shape=(out, out, out), grid=(layers, rows // tr),
        in_specs=[spec] * 4, out_specs=(spec,) * 3, compiler_params=_params(("parallel", "parallel")),
    )(w, g, m, v)


def _sum_slabs(x, name):
    n, rows, _ = x.shape
    tr = _row_tile(rows)

    def body(x_ref, o_ref):
        acc = x_ref[0]
        for k in range(1, n):
            acc = acc + x_ref[k]
        o_ref[...] = acc

    return pl.pallas_call(
        body, name=name, out_shape=jax.ShapeDtypeStruct((rows, D), F32), grid=(rows // tr,),
        in_specs=[pl.BlockSpec((n, tr, D), lambda i: (0, i, 0))], out_specs=pl.BlockSpec((tr, D), lambda i: (i, 0)),
        compiler_params=_params(("parallel",)),
    )(x)


_ANY = pl.BlockSpec(memory_space=pl.ANY)
MESH = pl.DeviceIdType.MESH


def _on_sequencer(body, out_shape, sems, operands, after, sequencer_id, name):
    n = len(operands)

    def ordered_body(*refs):
        body(*refs[:n], *refs[n + 1:])

    extra = [] if after is None else [after]
    return pl.kernel(
        body if after is None else ordered_body, out_type=out_shape,
        mesh=plsc.ScalarSubcoreMesh(axis_name="sequencer", num_cores=1), scratch_types=sems,
        compiler_params=pltpu.CompilerParams(collective_id=sequencer_id), name=name)(*operands, *extra)


def _all_gather(xs, name, sequencer_id=None, after=None):
    n = len(xs)

    def body(*refs):
        x_refs, out_refs = refs[:n], refs[n:2 * n]
        send_sems, recv_sems, local_sems = refs[2 * n:]
        x_, y_, c_ = lax.axis_index("x"), lax.axis_index("y"), lax.axis_index("c")
        me, sibling = (x_, y_, c_), (x_, y_, 1 - c_)
        chips = [(1 - x_, y_), (x_, 1 - y_), (1 - x_, 1 - y_)]
        if sequencer_id is not None:
            barrier = pltpu.get_barrier_semaphore()
            peers = [sibling] + [(*chip, pc) for chip in chips for pc in (c_, 1 - c_)]
            for peer in peers:
                pl.semaphore_signal(barrier, inc=1, device_id=peer, device_id_type=MESH)
            pl.semaphore_wait(barrier, len(peers))

        def slot(a, px, py, pc):
            return out_refs[a].at[4 * px + 2 * py + pc]

        def copy(a, k, block, to, src=None):
            return pltpu.make_async_remote_copy(
                src_ref=slot(a, *block) if src is None else src, dst_ref=slot(a, *block),
                send_sem=send_sems.at[7 * a + k], recv_sem=recv_sems.at[7 * a + k], device_id=to, device_id_type=MESH)

        mine = [pltpu.make_async_copy(x_refs[a], slot(a, *me), local_sems.at[a]) for a in range(n)]
        for cp in mine:
            cp.start()
        first = []
        for a in range(n):
            first.append(copy(a, 0, me, sibling, src=x_refs[a]))
            first += [copy(a, 1 + j, me, (*chip, c_), src=x_refs[a]) for j, chip in enumerate(chips)]
        for cp in first:
            cp.start()
        passed = []
        for j, chip in enumerate(chips):
            for a in range(n):
                copy(a, 1 + j, (*chip, c_), me).wait_recv()
                passed.append(copy(a, 4 + j, (*chip, c_), sibling))
                passed[-1].start()
        for a in range(n):
            copy(a, 0, sibling, me).wait_recv()
        for j, chip in enumerate(chips):
            for a in range(n):
                copy(a, 4 + j, (*chip, 1 - c_), me).wait_recv()
        for cp in first + passed:
            cp.wait_send()
        for cp in mine:
            cp.wait()

    out_shape = [jax.ShapeDtypeStruct((N_DEV,) + x.shape, x.dtype) for x in xs]
    sems = [pltpu.SemaphoreType.DMA((7 * n,)), pltpu.SemaphoreType.DMA((7 * n,)), pltpu.SemaphoreType.DMA((n,))]
    if sequencer_id is not None:
        return _on_sequencer(body, out_shape, sems, xs, after, sequencer_id, name)
    return pl.pallas_call(
        body, name=name, out_shape=out_shape, in_specs=[_ANY] * n, out_specs=[_ANY] * n, scratch_shapes=sems)(*xs)


def _sibling_exchange(gs, name, sequencer_id=None, after=None):
    n = len(gs)

    def body(*refs):
        g_refs, p_refs = refs[:n], refs[n:2 * n]
        send_sems, recv_sems = refs[2 * n:]
        x_, y_, c_ = lax.axis_index("x"), lax.axis_index("y"), lax.axis_index("c")
        if sequencer_id is not None:
            barrier = pltpu.get_barrier_semaphore()
            pl.semaphore_signal(barrier, inc=1, device_id=(x_, y_, 1 - c_), device_id_type=MESH)
            pl.semaphore_wait(barrier, 1)
        copies = [pltpu.make_async_remote_copy(
            src_ref=g_refs[a].at[2 * k + (1 - c_)], dst_ref=p_refs[a].at[k], send_sem=send_sems.at[4 * a + k],
            recv_sem=recv_sems.at[4 * a + k], device_id=(x_, y_, 1 - c_), device_id_type=MESH)
            for a in range(n) for k in range(4)]
        for cp in copies:
            cp.start()
        for cp in copies:
            cp.wait()

    out_shape = [jax.ShapeDtypeStruct((4,) + g.shape[1:], g.dtype) for g in gs]
    sems = [pltpu.SemaphoreType.DMA((4 * n,)), pltpu.SemaphoreType.DMA((4 * n,))]
    if sequencer_id is not None:
        return _on_sequencer(body, out_shape, sems, gs, after, sequencer_id, name)
    return pl.pallas_call(
        body, name=name, out_shape=out_shape, in_specs=[_ANY] * n, out_specs=[_ANY] * n, scratch_shapes=sems)(*gs)


def _slab_tiles(rows, cols):
    if rows % 8 == 0:
        return _row_tile(rows), cols
    return rows, 2 * LANE


def _pair_sums(g, p, route, name):
    _, rows, cols = g.shape
    tr, tc = _slab_tiles(rows, cols)

    def body(route_ref, g_ref, p_ref, t_ref):
        t_ref[...] = (g_ref[...].astype(F32) + p_ref[...].astype(F32)).astype(BF16)

    return pl.pallas_call(
        body, name=name, out_shape=jax.ShapeDtypeStruct((3, rows, cols), BF16),
        grid_spec=pltpu.PrefetchScalarGridSpec(
            num_scalar_prefetch=1, grid=(3, rows // tr, cols // tc),
            in_specs=[pl.BlockSpec((1, tr, tc), lambda r, i, j, route_ref: (2 * route_ref[1 + r] + route_ref[0], i, j)),
                      pl.BlockSpec((1, tr, tc), lambda r, i, j, route_ref: (route_ref[1 + r], i, j))],
            out_specs=pl.BlockSpec((1, tr, tc), lambda r, i, j, route_ref: (r, i, j))),
        compiler_params=_params(("parallel", "parallel", "parallel")),
    )(route, g, p)


def _chip_exchange(ts, name, sequencer_id=None, after=None):
    n = len(ts)

    def body(*refs):
        t_refs, l_refs = refs[:n], refs[n:2 * n]
        send_sems, recv_sems = refs[2 * n:]
        x_, y_, c_ = lax.axis_index("x"), lax.axis_index("y"), lax.axis_index("c")
        chips = [(1 - x_, y_), (x_, 1 - y_), (1 - x_, 1 - y_)]
        if sequencer_id is not None:
            barrier = pltpu.get_barrier_semaphore()
            for px, py in chips:
                pl.semaphore_signal(barrier, inc=1, device_id=(px, py, c_), device_id_type=MESH)
            pl.semaphore_wait(barrier, len(chips))
        copies = [pltpu.make_async_remote_copy(
            src_ref=t_refs[a].at[r], dst_ref=l_refs[a].at[r], send_sem=send_sems.at[3 * a + r],
            recv_sem=recv_sems.at[3 * a + r], device_id=(px, py, c_), device_id_type=MESH)
            for a in range(n) for r, (px, py) in enumerate(chips)]
        for cp in copies:
            cp.start()
        for cp in copies:
            cp.wait()

    out_shape = [jax.ShapeDtypeStruct((3,) + t.shape[1:], t.dtype) for t in ts]
    sems = [pltpu.SemaphoreType.DMA((3 * n,)), pltpu.SemaphoreType.DMA((3 * n,))]
    if sequencer_id is not None:
        return _on_sequencer(body, out_shape, sems, ts, after, sequencer_id, name)
    return pl.pallas_call(
        body, name=name, out_shape=out_shape, in_specs=[_ANY] * n, out_specs=[_ANY] * n, scratch_shapes=sems)(*ts)


def _reduce_adamw(gs, ps, landed, place, w, m, v, name):
    layers, rows, cols = w.shape
    assert layers == DEPTH == 2
    tr, tc = _slab_tiles(rows, cols)
    nr, nc = rows // tr, cols // tc
    spec = pl.BlockSpec((1, tr, tc), lambda l, i, j, place_ref: (l, i, j))

    def own(layer, which):
        pi, pj = (nr - 1, nc - 1) if layer == 0 else (0, 0)

        def index(l, i, j, place_ref):
            lead = 0 if which is None else place_ref[which]
            return lead, jnp.where(l == layer, i, pi), jnp.where(l == layer, j, pj)

        return pl.BlockSpec((3 if which is None else 1, tr, tc), index)

    def body(place_ref, g0_ref, p0_ref, l0_ref, g1_ref, p1_ref, l1_ref, w_ref, m_ref, v_ref,
             g_ref, d_ref, nm_ref, nv_ref):
        def update(own_ref, sib_ref, l_ref):
            g = (own_ref[0].astype(F32) + sib_ref[0].astype(F32) + l_ref[0].astype(F32) + l_ref[1].astype(F32)
                 + l_ref[2].astype(F32))
            g_ref[0] = g
            d_ref[0], nm_ref[0], nv_ref[0] = _adamw_math(w_ref[0], g, m_ref[0], v_ref[0])

        @pl.when(pl.program_id(0) == 0)
        def _():
            update(g0_ref, p0_ref, l0_ref)

        @pl.when(pl.program_id(0) == 1)
        def _():
            update(g1_ref, p1_ref, l1_ref)

    out = jax.ShapeDtypeStruct(w.shape, F32)
    return pl.pallas_call(
        body, name=name, out_shape=(out, out, out, out),
        grid_spec=pltpu.PrefetchScalarGridSpec(
            num_scalar_prefetch=1, grid=(DEPTH, nr, nc),
            in_specs=[own(0, 0), own(0, 1), own(0, None), own(1, 0), own(1, 1), own(1, None), spec, spec, spec],
            out_specs=(spec, spec, spec, spec)),
        compiler_params=_params(("arbitrary", "arbitrary", "arbitrary")),
    )(place, gs[0], ps[0], landed[0], gs[1], ps[1], landed[1], w, m, v)


def _pack(pieces, row_multiple, dtype, cols=D, rows=None):
    flat = jnp.concatenate([p.astype(dtype).reshape(-1) for p in pieces])
    if rows is None:
        rows = -(-flat.shape[0] // cols)
        rows = -(-rows // row_multiple) * row_multiple
    flat = jnp.pad(flat, (0, rows * cols - flat.shape[0]))
    return flat.reshape(rows, cols)


def _unpack(flat, shapes, lead=()):
    out, off = [], 0
    for shp in shapes:
        n = 1
        for s_ in shp:
            n *= s_
        out.append(lax.slice_in_dim(flat, off, off + n, axis=len(lead)).reshape(lead + tuple(shp)))
        off += n
    return out


WIN_STRIDE = 704
WIN_ROWS = 720
Z_TURN = 1544


def _window(wt, me, name):
    padded = jnp.pad(wt, ((0, 0), (0, WIN_ROWS - IN_SHARD), (0, 0)))

    def body(me_ref, x_ref, o_ref):
        o_ref[0] = pltpu.roll(x_ref[0], me_ref[0], axis=0).astype(BF16)

    spec = pl.BlockSpec((1, WIN_ROWS, D), lambda l, me_ref: (l, 0, 0))
    return pl.pallas_call(
        body, name=name, out_shape=jax.ShapeDtypeStruct((DEPTH, WIN_ROWS, D), BF16),
        grid_spec=pltpu.PrefetchScalarGridSpec(num_scalar_prefetch=1, grid=(DEPTH,), in_specs=[spec], out_specs=spec),
        compiler_params=_params(("parallel",)),
    )(me, padded)


def _z_rows_from_windows(win):
    over = WIN_ROWS - WIN_STRIDE
    pieces = [(0, win[0][0:WIN_STRIDE])]
    for d in range(1, N_DEV):
        base = WIN_STRIDE * d
        pieces.append((base, win[d - 1][WIN_STRIDE:WIN_ROWS] + win[d][0:over]))
        pieces.append((base + over, win[d][over:WIN_STRIDE]))
    pieces.append((WIN_STRIDE * N_DEV, win[N_DEV - 1][WIN_STRIDE:WIN_ROWS]))

    def rows(a, b):
        out = []
        for start, arr in pieces:
            lo, hi = max(a, start), min(b, start + arr.shape[0])
            if lo < hi:
                out.append(arr[lo - start:hi - start])
        return out

    pad = jnp.zeros((NZ - IN_COLS, win.shape[-1]), win.dtype)
    return jnp.concatenate(rows(Z_TURN, IN_COLS) + rows(0, Z_TURN) + [pad], axis=0)


def _in_rows_from_z(wt):
    return jnp.concatenate([wt[Z_Q:Z_Q + 1536], wt[Z_F:Z_F + 8], wt[Z_PC:Z_PC + 1024], wt[Z_G:Z_G + 3072]], axis=0)


def _pad_rows(v, rows=8):
    return jnp.pad(v, ((0, rows - v.shape[0]), (0, 0)))


def _layer_fwd(l, x, wts, gvec, mod):
    tag = f"l{l}"
    h = _prenorm_fwd(x, gvec, mod, 0, 0, 1, f"prenorm_mix_{tag}")
    z = _matmul(h, wts["w_in_t"], "nt", f"in_proj_{tag}", tn=1152)
    qa, ka, va, kat = _attn_prep(z, wts["b_f"], f"attn_prep_{tag}")
    qa = wts["arrive"](qa)
    o, lse = _attn_fwd(qa, ka, va, f"attn_{tag}")
    br_b = _pool_fwd(z, wts["wp_bd"], wts["pool_scale"], f"pool_{tag}")
    br_c = _conv_fwd(z, wts["conv_w"], f"conv_{tag}")
    pa = _matmul(o, wts["wa"], "nn", f"proj_a_{tag}", out_dtype=BF16)
    pb = _matmul(br_b, wts["wb"], "nn", f"proj_b_{tag}", out_dtype=BF16)
    gates = [(z, Z_G + k * D) for k in range(3)]
    pc, merged = _matmul(br_c, wts["wc"], "nn", f"proj_c_merge_{tag}", tm=1024, tn=512,
                         extra=gates + [(pa, 0), (pb, 0)], epilogue=_merge_epilogue, out_dtypes=(BF16, BF16))
    y = _matmul(merged, wts["w_out"], "nn", f"out_proj_{tag}")
    x1 = _postnorm_fwd(x, y, gvec, mod, 1, 2, f"postnorm_mix_{tag}")
    h2 = _prenorm_fwd(x1, gvec, mod, 2, 3, 4, f"prenorm_ff_{tag}")
    a, r = _matmul(h2, wts["w_ff1"], "nn", f"ff1_{tag}", b_col_shards=True, epilogue=_relu2_epilogue,
                   out_dtypes=(BF16, BF16))
    y2 = _matmul(r, wts["w_ff2"], "nn", f"ff2_{tag}", tk=1024)
    x2 = _postnorm_fwd(x1, y2, gvec, mod, 3, 5, f"postnorm_ff_{tag}")
    saved = dict(x=x, h=h, z=z, qa=qa, ka=ka, va=va, kat=kat, o=o, lse=lse, br_b=br_b, br_c=br_c, pa=pa, pb=pb, pc=pc,
                 merged=merged, y=y, x1=x1, h2=h2, a=a, r=r, y2=y2)
    return x2, saved


def _ffn_bwd(l, dx2, sv, wts, gvec, mod, midpoint):
    tag = f"l{l}"
    dy2, red_post_ff = _postnorm_bwd(sv["y2"], gvec, mod, dx2, 3, 5, f"postnorm_ff_bwd_{tag}")
    dy2 = midpoint(dy2)
    da = _matmul(dy2, wts["w_ff2"], "nt", f"ff2_dx_{tag}", extra=[(sv["a"], 0)], epilogue=_relu2_bwd_epilogue,
                 out_dtypes=(BF16,))[0]
    d_w_ff2 = _matmul(sv["r"], dy2, "tn", f"ff2_dw_{tag}", out_dtype=GRAD_DTYPE)
    dh2 = _matmul(da, wts["w_ff1"], "nt", f"ff1_dx_{tag}", b_col_shards=True)
    d_w_ff1 = _matmul(sv["h2"], da, "tn", f"ff1_dw_{tag}", out_dtype=GRAD_DTYPE, out_col_shards=True)
    dx1, red_pre_ff = _prenorm_bwd(sv["x1"], gvec, mod, dh2, dx2, 2, 4, f"prenorm_ff_bwd_{tag}")
    return dx1, [d_w_ff1, d_w_ff2.reshape(N_DEV, D_FF // N_DEV, D)], (red_pre_ff, red_post_ff)


def _mixer_bwd(l, dx1, sv, wts, gvec, mod, ffn_reds, midpoint):
    tag = f"l{l}"
    red_pre_ff, red_post_ff = ffn_reds
    dy, red_post_mix = _postnorm_bwd(sv["y"], gvec, mod, dx1, 1, 2, f"postnorm_mix_bwd_{tag}")
    gates = [(sv["z"], Z_G + k * D) for k in range(3)]
    dpa, dpb, dpc, *dgl = _matmul(dy, wts["w_out"], "nt", f"out_proj_dx_{tag}", tm=1024, tn=512,
                                  extra=gates + [(sv["pa"], 0), (sv["pb"], 0), (sv["pc"], 0)],
                                  epilogue=_merge_bwd_epilogue, out_dtypes=(BF16,) * 6)
    d_w_out = _matmul(sv["merged"], dy, "tn", f"out_proj_dw_{tag}", out_dtype=GRAD_DTYPE)
    dpa = midpoint(dpa)
    do = _matmul(dpa, wts["wa"], "nt", f"proj_a_dx_{tag}")
    dbr_b = _matmul(dpb, wts["wb"], "nt", f"proj_b_dx_{tag}")
    dbr_c = _matmul(dpc, wts["wc"], "nt", f"proj_c_dx_{tag}")
    d_wa = _matmul(sv["o"], dpa, "tn", f"proj_a_dw_{tag}", out_dtype=GRAD_DTYPE)
    d_wb = _matmul(sv["br_b"], dpb, "tn", f"proj_b_dw_{tag}", out_dtype=GRAD_DTYPE)
    d_wc = _matmul(sv["br_c"], dpc, "tn", f"proj_c_dw_{tag}", out_dtype=GRAD_DTYPE)
    d_w_branch = jnp.concatenate([d_wa, d_wb, d_wc], axis=0)

    dpu, d_wp_bd, red_pool = _pool_bwd(sv["z"], wts["wp_bd"], wts["pool_scale"], dbr_b, f"pool_bwd_{tag}")
    dconv, red_conv = _conv_bwd(sv["z"], wts["conv_w"], dbr_c, f"conv_bwd_{tag}")
    qa2, doa = _attn_bwd_prep(sv["qa"], sv["o"], sv["lse"], do, f"attn_bwd_prep_{tag}")
    dqt, dka, dva = _attn_bwd(qa2, sv["ka"], sv["va"], sv["kat"], doa, f"attn_bwd_{tag}")
    dq, dk, dv, dfl, red_f = _attn_bwd_post(sv["z"], wts["b_f"], dqt, dka, dva, f"attn_bwd_post_{tag}")
    dz = _concat_columns([dpu, dconv, *dgl, dq, dk, dv, dfl], f"dz_{tag}")
    dh = _matmul(dz, wts["w_in_t"], "nn", f"in_proj_dx_{tag}", tm=1024, tk=1920)
    d_w_in_t = _matmul(dz, sv["h"], "tn", f"in_proj_dw_{tag}", out_dtype=GRAD_DTYPE, tm=1152)
    dx0, red_pre_mix = _prenorm_bwd(sv["x"], gvec, mod, dh, dx1, 0, 1, f"prenorm_mix_bwd_{tag}")

    rows = D // N_DEV
    big = [_in_rows_from_z(d_w_in_t).reshape(N_DEV, IN_SHARD, D), d_w_branch.reshape(N_DEV, rows, D),
           d_w_out.reshape(N_DEV, rows, D)]
    d_w_pool = jnp.stack([d_wp_bd[64 * g:64 * (g + 1), 64 * g:64 * (g + 1)] for g in range(4)])
    small = dict(
        mod=jnp.stack([red_pre_mix[0], red_pre_mix[1], red_post_mix[0], red_pre_ff[0], red_pre_ff[1], red_post_ff[0]]),
        g_mix_pre=red_pre_mix[2], g_mix_post=red_post_mix[1], g_ff_pre=red_pre_ff[2], g_ff_post=red_post_ff[1],
        b_f=red_f[0, 0:8], w_pool=d_w_pool, pool_scale=red_pool[0], conv_w=red_conv[0:3])
    return dx0, big, small


SMALL_KEYS = ["mod", "g_mix_pre", "g_mix_post", "g_ff_pre", "g_ff_post", "b_f", "w_pool", "pool_scale", "conv_w"]
SMALL_SHAPES = [(DEPTH, 6 * D), (DEPTH, D), (DEPTH, D), (DEPTH, D), (DEPTH, D), (DEPTH, 8), (DEPTH, 4, 64, 64),
                (DEPTH, POOL_W), (DEPTH, 3, CONV_W)]


def kernel(x, c, w_ada, b_ada, g_mix_pre, g_mix_post, g_ff_pre, g_ff_post, w_in, b_f, w_pool, pool_scale, conv_w, w_branch, w_out, w_ff1, w_ff2, loss_target, m_w_ada, m_b_ada, m_g_mix_pre, m_g_mix_post, m_g_ff_pre, m_g_ff_post, m_w_in, m_b_f, m_w_pool, m_pool_scale, m_conv_w, m_w_branch, m_w_out, m_w_ff1, m_w_ff2, v_w_ada, v_b_ada, v_g_mix_pre, v_g_mix_post, v_g_ff_pre, v_g_ff_post, v_w_in, v_b_f, v_w_pool, v_pool_scale, v_conv_w, v_w_branch, v_w_out, v_w_ff1, v_w_ff2):
    ix, iy, ic = lax.axis_index("x"), lax.axis_index("y"), lax.axis_index("c")
    me = 4 * ix + 2 * iy + ic
    route = jnp.stack([ic, 2 * (1 - ix) + iy, 2 * ix + (1 - iy), 2 * (1 - ix) + (1 - iy)]).astype(jnp.int32)
    place = jnp.stack([me, 2 * ix + iy]).astype(jnp.int32)
    wt_in, mt_in, vt_in = (jnp.transpose(a, (0, 2, 1)) for a in (w_in, m_w_in, v_w_in))

    c_all = _all_gather([_pad_rows(c)], "gather_c")[0][:, 0, :]
    c_pad = _pad_rows(c_all, ADA_ROWS)
    b_cols = lax.dynamic_slice_in_dim(b_ada, me * ADA_COLS, ADA_COLS, axis=1)
    b_cols = jnp.broadcast_to(b_cols[:, None, :], (DEPTH, 8, ADA_COLS))
    mod_part = _ada_fwd(c_pad, w_ada, b_cols, "ada_fwd")
    mod_all = _all_gather([mod_part.reshape(DEPTH * ADA_ROWS, ADA_COLS)], "gather_mod")[0]
    mod_all = mod_all.reshape(N_DEV, DEPTH, ADA_ROWS, ADA_COLS)
    mod_mine = lax.dynamic_index_in_dim(mod_all, me, axis=2, keepdims=False)
    mod_mine = jnp.transpose(mod_mine, (1, 0, 2)).reshape(DEPTH, 6, D)

    cw_cols = CONV_W // N_DEV
    cw_send = jnp.pad(conv_w.reshape(DEPTH * 3, cw_cols), ((0, 8 - DEPTH * 3), (0, LANE - cw_cols)))
    win_in = _window(wt_in, place[0:1], "w_in_window")
    send = [[w[l].astype(BF16) for w in (win_in, w_branch, w_out, w_ff1, w_ff2)] for l in range(DEPTH)]
    first = _all_gather(send[0][:1], "gather_weights_l0_in", sequencer_id=1, after=mod_all)
    rest = _all_gather(send[0][1:] + [cw_send], "gather_weights_l0_rest", sequencer_id=2, after=first[0])
    first1 = _all_gather(send[1][:1], "gather_weights_l1_in", sequencer_id=3, after=first[0])
    rest1 = _all_gather(send[1][1:], "gather_weights_l1_rest", sequencer_id=12, after=first[0])
    first, (mt_in, vt_in) = lax.optimization_barrier((first, (mt_in, vt_in)))
    gathered = [first + rest[:4], first1 + rest1]
    cw_all = rest[4][:, :DEPTH * 3, :cw_cols].reshape(N_DEV, DEPTH, 3, cw_cols)

    def first_operands(l, p_in):
        wp_bd = jnp.zeros((POOL_W, POOL_W), F32)
        for g in range(4):
            wp_bd = wp_bd.at[64 * g:64 * (g + 1), 64 * g:64 * (g + 1)].set(w_pool[l, g])
        return dict(w_in_t=_z_rows_from_windows(p_in), wp_bd=wp_bd.astype(BF16),
                    pool_scale=_pad_rows(pool_scale[l][None, :]), b_f=_pad_rows(jnp.pad(b_f[l], (0, LANE - 8))[None, :]))

    def rest_operands(l, rest):
        p_br, p_out, p_ff1, p_ff2 = rest
        w_br_full = p_br.reshape(D, D)
        cw_full = jnp.transpose(cw_all[:, l], (1, 0, 2)).reshape(3, CONV_W)
        return dict(wa=w_br_full[0:A_WIDTH], wb=w_br_full[A_WIDTH:A_WIDTH + POOL_W], wc=w_br_full[A_WIDTH + POOL_W:],
                    w_out=p_out.reshape(D, D), w_ff1=p_ff1, w_ff2=p_ff2.reshape(D_FF, D), conv_w=_pad_rows(cw_full))

    xs = x[0]
    saved, layers = [], []
    for l in range(DEPTH):
        p_in, rest = gathered[l][0], gathered[l][1:5]
        if l > 0:
            xs, p_in = lax.optimization_barrier((xs, p_in))
        wts = first_operands(l, p_in)

        def arrive(t, l=l, rest=rest, wts=wts):
            if l > 0:
                t, rest = lax.optimization_barrier((t, rest))
            wts.update(rest_operands(l, rest))
            return t

        wts["arrive"] = arrive
        gvec = _pad_rows(jnp.stack([g_mix_pre[l], g_mix_post[l], g_ff_pre[l], g_ff_post[l]]))
        layers.append((wts, gvec, _pad_rows(mod_mine[l])))
        xs, sv = _layer_fwd(l, xs, *layers[l])
        saved.append(sv)
    dx, loss_part = _loss_head(xs, loss_target[0], "loss_head")
    small_grads = [None] * DEPTH
    mine, sibs, landed = ({} for _ in range(3))
    seq_id = iter(range(4, 4 + 4 * DEPTH))
    last = [gathered[DEPTH - 1][1]]

    def start(group, grads):
        mine[group] = grads
        sibs[group] = _sibling_exchange(grads, f"rs_sibling_{group}", sequencer_id=next(seq_id), after=last[0])
        last[0] = sibs[group][0]

    def finish(group, later):
        later, (grads, sib) = lax.optimization_barrier((later, (mine[group], sibs[group])))
        sends = [_pair_sums(g, p, route, f"rs_pair_sums_{group}_{k}") for k, (g, p) in enumerate(zip(grads, sib))]
        later, sends = lax.optimization_barrier((later, sends))
        landed[group] = _chip_exchange(sends, f"rs_chips_{group}", sequencer_id=next(seq_id), after=last[0])
        last[0] = landed[group][0]
        return later

    pending = None
    for l in reversed(range(DEPTH)):
        hook = (lambda da: da) if pending is None else functools.partial(finish, pending)
        dx, ffn_grads, ffn_reds = _ffn_bwd(l, dx, saved[l], *layers[l], hook)
        start(f"ffn_l{l}", ffn_grads)
        dx, mix_grads, small_grads[l] = _mixer_bwd(l, dx, saved[l], *layers[l], ffn_reds,
                                                   functools.partial(finish, f"ffn_l{l}"))
        start(f"mix_l{l}", mix_grads)
        pending = f"mix_l{l}"
    grad_x = dx[None]

    big_w = [wt_in, w_branch, w_out, w_ff1, w_ff2]
    big_m = [mt_in, m_w_branch, m_w_out, m_w_ff1, m_w_ff2]
    big_v = [vt_in, v_w_branch, v_w_out, v_w_ff1, v_w_ff2]
    where = [("mix", 0), ("mix", 1), ("mix", 2), ("ffn", 0), ("ffn", 1)]

    def reduce_and_update(k):
        group, at = where[k]
        return _reduce_adamw([mine[f"{group}_l{l}"][at] for l in range(DEPTH)],
                             [sibs[f"{group}_l{l}"][at] for l in range(DEPTH)],
                             [landed[f"{group}_l{l}"][at] for l in range(DEPTH)], place, big_w[k], big_m[k], big_v[k],
                             f"rs_sum_adamw_{k}")

    big_res = {k: list(reduce_and_update(k)) for k in (3, 4)}
    big_res[3][0] = finish(pending, big_res[3][0])

    small = {k: jnp.stack([small_grads[l][k] for l in range(DEPTH)]) for k in SMALL_KEYS}
    payload = _pack([small[k] for k in SMALL_KEYS] + [loss_part[0:1, 0:1]], 8, F32)
    small_all = _all_gather([payload], "gather_small")[0]
    dmod_all = small_all[:, 0:DEPTH * 6, :].reshape(N_DEV, DEPTH, 6 * D)
    summed = _unpack(_sum_slabs(small_all, "sum_small").reshape(-1), SMALL_SHAPES + [(1, 1)])
    sg = dict(zip(SMALL_KEYS, summed))
    loss = summed[-1][0, 0]
    dmod_cols = lax.dynamic_slice_in_dim(dmod_all, me * ADA_COLS, ADA_COLS, axis=2)
    dmod_cols = jnp.pad(jnp.transpose(dmod_cols, (1, 0, 2)), ((0, 0), (0, ADA_ROWS - N_DEV), (0, 0)))
    g_w_ada = _ada_bwd(c_pad, dmod_cols, "ada_bwd")
    g_conv_w = lax.dynamic_slice_in_dim(sg["conv_w"], me * (CONV_W // N_DEV), CONV_W // N_DEV, axis=2)

    ada_out = [g_w_ada] + list(_adamw(w_ada, g_w_ada, m_w_ada, v_w_ada, "adamw_ada"))
    rest_w = [b_ada, g_mix_pre, g_mix_post, g_ff_pre, g_ff_post, b_f, w_pool, pool_scale, conv_w]
    rest_m = [m_b_ada, m_g_mix_pre, m_g_mix_post, m_g_ff_pre, m_g_ff_post, m_b_f, m_w_pool, m_pool_scale, m_conv_w]
    rest_v = [v_b_ada, v_g_mix_pre, v_g_mix_post, v_g_ff_pre, v_g_ff_post, v_b_f, v_w_pool, v_pool_scale, v_conv_w]
    rest_g = [sg["mod"], sg["g_mix_pre"], sg["g_mix_post"], sg["g_ff_pre"], sg["g_ff_post"], sg["b_f"],
              sg["w_pool"], sg["pool_scale"], g_conv_w]
    rest_shapes = [a.shape for a in rest_w]
    upd = _adamw(_pack(rest_w, 8, F32)[None], _pack(rest_g, 8, F32)[None], _pack(rest_m, 8, F32)[None],
                 _pack(rest_v, 8, F32)[None], "adamw_rest")
    rest_out = [rest_g] + [_unpack(arr.reshape(-1), rest_shapes) for arr in upd]
    rest_out = [[ada_out[which]] + rest_out[which] for which in range(4)]

    landed[pending], rest_out = lax.optimization_barrier((landed[pending], rest_out))
    big_res.update({k: reduce_and_update(k) for k in (0, 1, 2)})
    big_out = [[jnp.transpose(big_res[k][which], (0, 2, 1)) if k == 0 else big_res[k][which] for k in range(5)]
               for which in range(4)]

    def ordered(k):
        r, b = rest_out[k], big_out[k]
        return [r[0], r[1], r[2], r[3], r[4], r[5], b[0], r[6], r[7], r[8], r[9], b[1], b[2], b[3], b[4]]

    return (loss, grad_x, *ordered(0), *ordered(1), *ordered(2), *ordered(3))
```

```python
import functools

import jax
import jax.numpy as jnp
from jax import lax
from jax.experimental import pallas as pl
from jax.experimental.pallas import tpu as pltpu
from jax.experimental.pallas import tpu_sc as plsc

F32 = jnp.float32
BF16 = jnp.bfloat16
GRAD_DTYPE = BF16

N_DEV = 8
D = 1024
S = 2048
DEPTH = 2
D_FF = 4 * D
A_WIDTH = 512
HEAD_DIM = 64
N_PAIR = 4
POOL_W = 256
CONV_W = 256
IN_COLS = 5640
ADA_COLS = 6 * D // N_DEV
IN_SHARD = IN_COLS // N_DEV
RMS_EPS = 1e-6
NEG_INF = -1e30
ATT_SCALE = HEAD_DIM ** -0.5

NZ = 5760
Z_PC = 0
Z_G = 1024
Z_Q = 4096
Z_K = 4608
Z_V = 5120
Z_F = 5632

LR, B1, B2, EPS, WD, STEP = 0.001, 0.9, 0.999, 1e-08, 0.01, 10

LANE = 128
VMEM_LIMIT_BYTES = 48 * 1024 * 1024
TS = 512
TQ = 256
TQ_FWD = 512
HEADS_PER_STEP = 8
HEADS_PER_STEP_FWD = 8


def _params(sem=None):
    return pltpu.CompilerParams(dimension_semantics=sem, vmem_limit_bytes=VMEM_LIMIT_BYTES)


def _pick(n, target):
    best = None
    for t in range(LANE, min(n, target) + 1, LANE):
        if n % t == 0:
            best = t
    return n if best is None else best


def _matmul(a, b, mode, name, out_dtype=F32, tm=2048, tn=1024, tk=2048, b_col_shards=False, out_col_shards=False,
            extra=(), vec_extra=(), epilogue=None, out_dtypes=None):
    if b_col_shards:
        shards, b_rows, shard_cols = b.shape
        b_shape = (b_rows, shards * shard_cols)
    else:
        b_shape = b.shape
    if mode == "nn":
        (m, k), (k2, n) = a.shape, b_shape
    elif mode == "nt":
        (m, k), (n, k2) = a.shape, b_shape
    else:
        (k, m), (k2, n) = a.shape, b_shape
    assert k == k2, (a.shape, b.shape, mode)
    tm, tn, tk = _pick(m, tm), _pick(n, tn), _pick(k, tk)
    if b_col_shards and mode == "nn":
        tn = shard_cols
    per_step = 1
    if b_col_shards and mode == "nt":
        per_step = max(1, min(tk, 1024) // shard_cols)
        tk = per_step * shard_cols
    if out_col_shards:
        tn = n // N_DEV
    nk = k // tk
    if mode == "nn":
        a_spec = pl.BlockSpec((tm, tk), lambda i, j, kk: (i, kk))
        b_spec = (pl.BlockSpec((None, tk, tn), lambda i, j, kk: (j, kk, 0)) if b_col_shards else
                  pl.BlockSpec((tk, tn), lambda i, j, kk: (kk, j)))
        dims = (((1,), (0,)), ((), ()))
    elif mode == "nt":
        a_spec = pl.BlockSpec((tm, tk), lambda i, j, kk: (i, kk))
        b_spec = (pl.BlockSpec((per_step, tn, shard_cols), lambda i, j, kk: (kk, j, 0)) if b_col_shards else
                  pl.BlockSpec((tn, tk), lambda i, j, kk: (j, kk)))
        dims = (((1,), (1,)), ((), ()))
    else:
        assert not b_col_shards
        a_spec = pl.BlockSpec((tk, tm), lambda i, j, kk: (kk, i))
        b_spec = pl.BlockSpec((tk, tn), lambda i, j, kk: (kk, j))
        dims = (((0,), (0,)), ((), ()))
    if out_col_shards:
        out_shape = jax.ShapeDtypeStruct((N_DEV, m, tn), out_dtype)
        out_spec = pl.BlockSpec((None, tm, tn), lambda i, j, kk: (j, i, 0))
    else:
        out_shape = jax.ShapeDtypeStruct((m, n), out_dtype)
        out_spec = pl.BlockSpec((tm, tn), lambda i, j, kk: (i, j))

    n_extra = len(extra) + len(vec_extra)
    extra_specs = [pl.BlockSpec((tm, tn), lambda i, j, kk, off=off: (i, j + off // tn)) for _, off in extra]
    extra_specs += [pl.BlockSpec((8, tn), lambda i, j, kk: (0, j)) for _ in vec_extra]
    if epilogue is not None:
        assert not out_col_shards and all(off % tn == 0 for _, off in extra)
        out_shape = [jax.ShapeDtypeStruct((m, n), dt) for dt in out_dtypes]
        out_spec = [pl.BlockSpec((tm, tn), lambda i, j, kk: (i, j)) for _ in out_dtypes]

    def product(a_ref, b_ref):
        if b_col_shards and mode == "nt":
            b_tile = jnp.concatenate([b_ref[s] for s in range(per_step)], axis=1) if per_step > 1 else b_ref[0]
        else:
            b_tile = b_ref[...]
        return lax.dot_general(a_ref[...].astype(BF16), b_tile.astype(BF16), dims, preferred_element_type=F32)

    def write(acc, extra_refs, o_refs):
        if epilogue is None:
            o_refs[0][...] = acc.astype(out_dtype)
        else:
            for o_ref, tile in zip(o_refs, epilogue(acc, *[r[...] for r in extra_refs])):
                o_ref[...] = tile.astype(o_ref.dtype)

    def body_one_pass(a_ref, b_ref, *refs):
        write(product(a_ref, b_ref), refs[:n_extra], refs[n_extra:])

    def body(a_ref, b_ref, *refs):
        acc_ref = refs[-1]
        kk = pl.program_id(2)

        @pl.when(kk == 0)
        def _():
            acc_ref[...] = product(a_ref, b_ref)

        @pl.when(kk > 0)
        def _():
            acc_ref[...] += product(a_ref, b_ref)

        @pl.when(kk == nk - 1)
        def _():
            write(acc_ref[...], refs[:n_extra], refs[n_extra:-1])

    return pl.pallas_call(
        body_one_pass if nk == 1 else body, name=name,
        out_shape=out_shape,
        grid=(m // tm, n // tn, nk),
        in_specs=[a_spec, b_spec] + extra_specs,
        out_specs=out_spec,
        scratch_shapes=[] if nk == 1 else [pltpu.VMEM((tm, tn), F32)],
        compiler_params=_params(("parallel", "parallel", "arbitrary")),
    )(a, b, *[x for x, _ in extra], *vec_extra)


def _row_spec(width=D, col=0):
    return pl.BlockSpec((TS, width), lambda i: (i, col))


def _vec_spec(rows=8, width=D):
    return pl.BlockSpec((rows, width), lambda i: (0, 0))


def _rms(x):
    return lax.rsqrt(jnp.mean(x * x, axis=-1, keepdims=True) + RMS_EPS)


def _prenorm_fwd(x, gvec, mod, g_row, shift_row, scale_row, name):
    def body(x_ref, g_ref, mod_ref, h_ref):
        xv = x_ref[...]
        y = xv * _rms(xv) * g_ref[g_row:g_row + 1, :]
        h = y * (1.0 + mod_ref[scale_row:scale_row + 1, :]) + mod_ref[shift_row:shift_row + 1, :]
        h_ref[...] = h.astype(BF16)

    return pl.pallas_call(
        body, name=name, out_shape=jax.ShapeDtypeStruct((S, D), BF16), grid=(S // TS,),
        in_specs=[_row_spec(), _vec_spec(), _vec_spec()], out_specs=_row_spec(),
        compiler_params=_params(("parallel",)),
    )(x, gvec, mod)


def _prenorm_bwd(x, gvec, mod, dh, dres, g_row, scale_row, name):
    def body(x_ref, g_ref, mod_ref, dh_ref, dres_ref, dx_ref, red_ref):
        i = pl.program_id(0)

        @pl.when(i == 0)
        def _():
            red_ref[...] = jnp.zeros_like(red_ref)

        xv = x_ref[...]
        g = g_ref[g_row:g_row + 1, :]
        r = _rms(xv)
        n = xv * r
        yg = n * g
        dhv = dh_ref[...]
        dyg = dhv * (1.0 + mod_ref[scale_row:scale_row + 1, :])
        dn = dyg * g
        dx = r * (dn - n * jnp.mean(dn * n, axis=-1, keepdims=True))
        dx_ref[...] = dres_ref[...] + dx
        red_ref[0:1, :] += jnp.sum(dhv, axis=0, keepdims=True)
        red_ref[1:2, :] += jnp.sum(dhv * yg, axis=0, keepdims=True)
        red_ref[2:3, :] += jnp.sum(dyg * n, axis=0, keepdims=True)

    return pl.pallas_call(
        body, name=name,
        out_shape=(jax.ShapeDtypeStruct((S, D), F32), jax.ShapeDtypeStruct((8, D), F32)),
        grid=(S // TS,),
        in_specs=[_row_spec(), _vec_spec(), _vec_spec(), _row_spec(), _row_spec()],
        out_specs=(_row_spec(), _vec_spec()),
        compiler_params=_params(("arbitrary",)),
    )(x, gvec, mod, dh, dres)


def _postnorm_fwd(x, y, gvec, mod, g_row, gate_row, name):
    def body(x_ref, y_ref, g_ref, mod_ref, o_ref):
        yv = y_ref[...]
        yn = yv * _rms(yv) * g_ref[g_row:g_row + 1, :]
        o_ref[...] = x_ref[...] + mod_ref[gate_row:gate_row + 1, :] * yn

    return pl.pallas_call(
        body, name=name, out_shape=jax.ShapeDtypeStruct((S, D), F32), grid=(S // TS,),
        in_specs=[_row_spec(), _row_spec(), _vec_spec(), _vec_spec()], out_specs=_row_spec(),
        compiler_params=_params(("parallel",)),
    )(x, y, gvec, mod)


def _postnorm_bwd(y, gvec, mod, dxo, g_row, gate_row, name):
    def body(y_ref, g_ref, mod_ref, dxo_ref, dy_ref, red_ref):
        i = pl.program_id(0)

        @pl.when(i == 0)
        def _():
            red_ref[...] = jnp.zeros_like(red_ref)

        yv = y_ref[...]
        g = g_ref[g_row:g_row + 1, :]
        r = _rms(yv)
        n = yv * r
        dxo = dxo_ref[...]
        dyn = dxo * mod_ref[gate_row:gate_row + 1, :]
        dn = dyn * g
        dy = r * (dn - n * jnp.mean(dn * n, axis=-1, keepdims=True))
        dy_ref[...] = dy.astype(BF16)
        red_ref[0:1, :] += jnp.sum(dxo * (n * g), axis=0, keepdims=True)
        red_ref[1:2, :] += jnp.sum(dyn * n, axis=0, keepdims=True)

    return pl.pallas_call(
        body, name=name,
        out_shape=(jax.ShapeDtypeStruct((S, D), BF16), jax.ShapeDtypeStruct((8, D), F32)),
        grid=(S // TS,),
        in_specs=[_row_spec(), _vec_spec(), _vec_spec(), _row_spec()],
        out_specs=(_row_spec(), _vec_spec()),
        compiler_params=_params(("arbitrary",)),
    )(y, gvec, mod, dxo)


def _concat_columns(pieces, name):
    widths = [p.shape[1] for p in pieces]
    offsets = [sum(widths[:k]) for k in range(len(widths))]

    def body(*refs):
        o_ref = refs[-1]
        for ref, off, w in zip(refs[:-1], offsets, widths):
            o_ref[:, off:off + w] = ref[...]

    return pl.pallas_call(
        body, name=name, out_shape=jax.ShapeDtypeStruct((S, sum(widths)), pieces[0].dtype), grid=(S // TS,),
        in_specs=[_row_spec(w) for w in widths], out_specs=_row_spec(sum(widths)),
        compiler_params=_params(("parallel",)),
    )(*pieces)


def _loss_head(xf, target, name):
    def body(x_ref, t_ref, dx_ref, loss_ref):
        i = pl.program_id(0)

        @pl.when(i == 0)
        def _():
            loss_ref[...] = jnp.zeros_like(loss_ref)

        e = x_ref[...] - t_ref[...]
        dx_ref[...] = e / float(D)
        per_tok = jnp.mean(e * e, axis=-1, keepdims=True)
        loss_ref[0:1, 0:1] += 0.5 * jnp.sum(per_tok, axis=0, keepdims=True)

    return pl.pallas_call(
        body, name=name,
        out_shape=(jax.ShapeDtypeStruct((S, D), F32), jax.ShapeDtypeStruct((8, LANE), F32)),
        grid=(S // TS,),
        in_specs=[_row_spec(), _row_spec()],
        out_specs=(_row_spec(), pl.BlockSpec((8, LANE), lambda i: (0, 0))),
        compiler_params=_params(("arbitrary",)),
    )(xf, target)


def _relu2_epilogue(a):
    t = jnp.maximum(a, 0.0)
    return a, t * t


def _relu2_bwd_epilogue(dr, a):
    return (dr * (2.0 * jnp.maximum(a, 0.0)),)


def _merge_epilogue(pc, g0, g1, g2, pa, pb):
    return pc, jax.nn.sigmoid(g0) * pa + jax.nn.sigmoid(g1) * pb + jax.nn.sigmoid(g2) * pc


def _postnorm_epilogue(g_row, gate_row):
    def epilogue(y, x, gvec, mod):
        yn = y * _rms(y) * gvec[g_row:g_row + 1, :]
        return y, x + mod[gate_row:gate_row + 1, :] * yn

    return epilogue


def _merge_bwd_epilogue(dm, g0, g1, g2, pa, pb, pc):
    sg = [jax.nn.sigmoid(g) for g in (g0, g1, g2)]
    return tuple(dm * s for s in sg) + tuple(dm * p * (s * (1.0 - s)) for p, s in zip((pa, pb, pc), sg))


def _shift_down(x, k, row):
    return jnp.where(row >= k, pltpu.roll(x, k, axis=0), 0.0)


def _shift_up(x, k, row):
    n = x.shape[0]
    return jnp.where(row < n - k, pltpu.roll(x, n - k, axis=0), 0.0)


def _cumsum_rows(x, row, reverse=False):
    shift = _shift_up if reverse else _shift_down
    k = 1
    while k < x.shape[0]:
        x = x + shift(x, k, row)
        k *= 2
    return x


def _full_spec(shape, idx=(0, 0)):
    return pl.BlockSpec(shape, lambda i: idx)


def _pool_window_select(lane, a2, a4, a8, a16):
    return jnp.where(lane < 64, a2, jnp.where(lane < 128, a4, jnp.where(lane < 192, a8, a16)))


def _pool_p(u, row, lane):
    t2 = u + _shift_down(u, 1, row)
    t4 = t2 + _shift_down(t2, 2, row)
    t8 = t4 + _shift_down(t4, 4, row)
    t16 = t8 + _shift_down(t8, 8, row)
    tw = _pool_window_select(lane, t2, t4, t8, t16)
    cnt = jnp.minimum((row + 1).astype(F32), _pool_window_select(lane, 2.0, 4.0, 8.0, 16.0))
    return tw / cnt - u, cnt


def _pool_fwd(z, wp_bd, pscale, name):
    def body(u_ref, w_ref, s_ref, o_ref):
        row = lax.broadcasted_iota(jnp.int32, (S, POOL_W), 0)
        lane = lax.broadcasted_iota(jnp.int32, (S, POOL_W), 1)
        p, _ = _pool_p(u_ref[...], row, lane)
        y = jnp.dot(p.astype(BF16), w_ref[...], preferred_element_type=F32)
        o_ref[...] = y * s_ref[0:1, :]

    return pl.pallas_call(
        body, name=name, out_shape=jax.ShapeDtypeStruct((S, POOL_W), F32), grid=(1,),
        in_specs=[_full_spec((S, POOL_W), (0, Z_PC // POOL_W)), _full_spec((POOL_W, POOL_W)), _full_spec((8, POOL_W))],
        out_specs=_full_spec((S, POOL_W)),
        compiler_params=_params(("arbitrary",)),
    )(z, wp_bd, pscale)


def _pool_bwd(z, wp_bd, pscale, dbr, name):
    def body(u_ref, w_ref, s_ref, dbr_ref, du_ref, dw_ref, red_ref):
        row = lax.broadcasted_iota(jnp.int32, (S, POOL_W), 0)
        lane = lax.broadcasted_iota(jnp.int32, (S, POOL_W), 1)
        p, cnt = _pool_p(u_ref[...], row, lane)
        pb = p.astype(BF16)
        y = jnp.dot(pb, w_ref[...], preferred_element_type=F32)
        dbr = dbr_ref[...]
        red_ref[...] = jnp.zeros_like(red_ref)
        red_ref[0:1, :] = jnp.sum(dbr * y, axis=0, keepdims=True)
        dy = (dbr * s_ref[0:1, :]).astype(BF16)
        dw_ref[...] = lax.dot_general(pb, dy, (((0,), (0,)), ((), ())), preferred_element_type=F32)
        dp = lax.dot_general(dy, w_ref[...], (((1,), (1,)), ((), ())), preferred_element_type=F32)
        g = dp / cnt
        a2 = g + _shift_up(g, 1, row)
        a4 = a2 + _shift_up(a2, 2, row)
        a8 = a4 + _shift_up(a4, 4, row)
        a16 = a8 + _shift_up(a8, 8, row)
        du_ref[...] = (_pool_window_select(lane, a2, a4, a8, a16) - dp).astype(BF16)

    return pl.pallas_call(
        body, name=name,
        out_shape=(jax.ShapeDtypeStruct((S, POOL_W), BF16), jax.ShapeDtypeStruct((POOL_W, POOL_W), F32),
                   jax.ShapeDtypeStruct((8, POOL_W), F32)),
        grid=(1,),
        in_specs=[_full_spec((S, POOL_W), (0, Z_PC // POOL_W)), _full_spec((POOL_W, POOL_W)), _full_spec((8, POOL_W)),
                  _full_spec((S, POOL_W))],
        out_specs=(_full_spec((S, POOL_W)), _full_spec((POOL_W, POOL_W)), _full_spec((8, POOL_W))),
        compiler_params=_params(("arbitrary",)),
    )(z, wp_bd, pscale, dbr)


def _conv_specs():
    base = Z_PC // CONV_W
    return [_full_spec((S, CONV_W), (0, base + 1)), _full_spec((S, CONV_W), (0, base + 2)),
            _full_spec((S, CONV_W), (0, base + 3)), _full_spec((8, CONV_W))]


def _conv_fwd(z, cw, name):
    def body(h_ref, b_ref, c_ref, w_ref, o_ref):
        row = lax.broadcasted_iota(jnp.int32, (S, CONV_W), 0)
        u = c_ref[...] * h_ref[...]
        y = (w_ref[0:1, :] * _shift_down(u, 2, row) + w_ref[1:2, :] * _shift_down(u, 1, row) + w_ref[2:3, :] * u)
        o_ref[...] = b_ref[...] * y

    return pl.pallas_call(
        body, name=name, out_shape=jax.ShapeDtypeStruct((S, CONV_W), F32), grid=(1,),
        in_specs=_conv_specs(), out_specs=_full_spec((S, CONV_W)),
        compiler_params=_params(("arbitrary",)),
    )(z, z, z, cw)


def _conv_bwd(z, cw, dbr, name):
    def body(h_ref, b_ref, c_ref, w_ref, dbr_ref, d_ref, red_ref):
        row = lax.broadcasted_iota(jnp.int32, (S, CONV_W), 0)
        h, cg = h_ref[...], c_ref[...]
        u = cg * h
        u1 = _shift_down(u, 1, row)
        u2 = _shift_down(u, 2, row)
        y = w_ref[0:1, :] * u2 + w_ref[1:2, :] * u1 + w_ref[2:3, :] * u
        dbr = dbr_ref[...]
        dy = dbr * b_ref[...]
        du = w_ref[2:3, :] * dy + w_ref[1:2, :] * _shift_up(dy, 1, row) + w_ref[0:1, :] * _shift_up(dy, 2, row)
        d_ref[:, 0:CONV_W] = (du * cg).astype(BF16)
        d_ref[:, CONV_W:2 * CONV_W] = (dbr * y).astype(BF16)
        d_ref[:, 2 * CONV_W:3 * CONV_W] = (du * h).astype(BF16)
        red_ref[...] = jnp.zeros_like(red_ref)
        red_ref[0:1, :] = jnp.sum(dy * u2, axis=0, keepdims=True)
        red_ref[1:2, :] = jnp.sum(dy * u1, axis=0, keepdims=True)
        red_ref[2:3, :] = jnp.sum(dy * u, axis=0, keepdims=True)

    return pl.pallas_call(
        body, name=name,
        out_shape=(jax.ShapeDtypeStruct((S, 3 * CONV_W), BF16), jax.ShapeDtypeStruct((8, CONV_W), F32)),
        grid=(1,),
        in_specs=_conv_specs() + [_full_spec((S, CONV_W))],
        out_specs=(_full_spec((S, 3 * CONV_W)), _full_spec((8, CONV_W))),
        compiler_params=_params(("arbitrary",)),
    )(z, z, z, cw, dbr)


_NT = (((1,), (1,)), ((), ()))
_TN = (((0,), (0,)), ((), ()))
N_HEAD = 2 * N_PAIR


def _split3(x):
    hi = x.astype(BF16).astype(F32)
    mid = (x - hi).astype(BF16).astype(F32)
    lo = (x - hi - mid).astype(BF16).astype(F32)
    return hi, mid, lo


def _spare(lane, e, k):
    return lane == 64 * (1 - e) + k


def _spare3(lane, e, k):
    base = 64 * (1 - e) + k
    return (lane >= base) & (lane < base + 3)


def _put3(lane, e, k, pieces, rest):
    out = rest
    for n, piece in enumerate(pieces):
        out = jnp.where(_spare(lane, e, k + n), piece, out)
    return out


def _attn_prep(z, bf, name):
    def body(q_ref, k_ref, v_ref, f_ref, b_ref, qa_ref, ka_ref, va_ref, kat_ref, cum_ref):
        p = pl.program_id(0)
        row = lax.broadcasted_iota(jnp.int32, (S, LANE), 0)
        lane = lax.broadcasted_iota(jnp.int32, (S, LANE), 1)

        @pl.when(p == 0)
        def _():
            xv = f_ref[...] + b_ref[0:1, :]
            ls = jnp.minimum(xv, 0.0) - jnp.log(1.0 + jnp.exp(-jnp.abs(xv)))
            cum_ref[...] = _cumsum_rows(jnp.where(lane < N_HEAD, ls, 0.0), row)

        cum = cum_ref[...]
        q, k, v = q_ref[...], k_ref[...], v_ref[...]
        for e in range(2):
            head = (lane >= 64) if e else (lane < 64)
            f = jnp.sum(jnp.where(lane == 2 * p + e, cum, 0.0), axis=1, keepdims=True)
            pieces = _split3(f)
            qa = jnp.where(head, q * ATT_SCALE, _put3(lane, e, 0, pieces, jnp.where(_spare3(lane, e, 3), 1.0, 0.0)))
            ones = jnp.where(_spare3(lane, e, 0) | _spare3(lane, e, 6), 1.0, 0.0)
            ka = jnp.where(head, k, _put3(lane, e, 3, [-x for x in pieces], ones))
            va = jnp.where(head, v, jnp.where(_spare3(lane, e, 0), 1.0, 0.0))
            qa_ref[e] = qa.astype(BF16)
            ka_ref[e] = ka.astype(BF16)
            va_ref[e] = va.astype(BF16)
            kat_ref[e] = ka.T.astype(BF16)

    qb, kb, vb = Z_Q // LANE, Z_K // LANE, Z_V // LANE
    heads = jax.ShapeDtypeStruct((N_HEAD, S, LANE), BF16)
    pair = pl.BlockSpec((2, S, LANE), lambda p: (p, 0, 0))
    return pl.pallas_call(
        body, name=name,
        out_shape=(heads, heads, heads, jax.ShapeDtypeStruct((N_HEAD, LANE, S), BF16)),
        grid=(N_PAIR,),
        in_specs=[pl.BlockSpec((S, LANE), lambda p: (0, qb + p)), pl.BlockSpec((S, LANE), lambda p: (0, kb + p)),
                  pl.BlockSpec((S, LANE), lambda p: (0, vb + p)), pl.BlockSpec((S, LANE), lambda p: (0, Z_F // LANE)),
                  pl.BlockSpec((8, LANE), lambda p: (0, 0))],
        out_specs=(pair, pair, pair, pl.BlockSpec((2, LANE, S), lambda p: (p, 0, 0))),
        scratch_shapes=[pltpu.VMEM((S, LANE), F32)],
        compiler_params=_params(("arbitrary",)),
    )(z, z, z, z, bf)


def _attn_bwd_prep(qa, o, lse, do, name):
    def body(qa_ref, o_ref, lse_ref, do_ref, qa2_ref, doa_ref):
        lane = lax.broadcasted_iota(jnp.int32, (S, LANE), 1)
        dov, ov, lsev = do_ref[...], o_ref[...], lse_ref[...]
        for e in range(2):
            head = (lane >= 64) if e else (lane < 64)
            dsum = jnp.sum(jnp.where(head, dov * ov, 0.0), axis=1, keepdims=True)
            doa_ref[e] = jnp.where(head, dov, _put3(lane, e, 0, [-x for x in _split3(dsum)], 0.0)).astype(BF16)
            lse_col = lsev[:, 64 * e:64 * e + 1]
            qa2_ref[e] = _put3(lane, e, 6, [-x for x in _split3(lse_col)], qa_ref[e].astype(F32)).astype(BF16)

    heads = jax.ShapeDtypeStruct((N_HEAD, S, LANE), BF16)
    pair = pl.BlockSpec((2, S, LANE), lambda p: (p, 0, 0))
    cols = pl.BlockSpec((S, LANE), lambda p: (0, p))
    return pl.pallas_call(
        body, name=name, out_shape=(heads, heads), grid=(N_PAIR,),
        in_specs=[pair, cols, cols, cols], out_specs=(pair, pair),
        compiler_params=_params(("parallel",)),
    )(qa, o, lse, do)


def _attn_bwd_post(z, bf, dqt, dka, dva, name):
    def body(f_ref, b_ref, dqt_ref, dk_ref, dv_ref, dq_out, dk_out, dv_out, dfl_ref, red_ref, dcum_ref):
        p = pl.program_id(0)

        @pl.when(p == 0)
        def _():
            dcum_ref[...] = jnp.zeros_like(dcum_ref)

        row = lax.broadcasted_iota(jnp.int32, (S, LANE), 0)
        lane = lax.broadcasted_iota(jnp.int32, (S, LANE), 1)
        dqa = [dqt_ref[e].T for e in range(2)]
        dq_out[...] = (jnp.where(lane < 64, dqa[0], dqa[1]) * ATT_SCALE).astype(BF16)
        dk_out[...] = jnp.where(lane < 64, dk_ref[0], dk_ref[1]).astype(BF16)
        dv_out[...] = jnp.where(lane < 64, dv_ref[0], dv_ref[1]).astype(BF16)
        for e in range(2):
            d_query = jnp.sum(jnp.where(_spare(lane, e, 0), dqa[e], 0.0), axis=1, keepdims=True)
            d_key = jnp.sum(jnp.where(_spare(lane, e, 3), dk_ref[e], 0.0), axis=1, keepdims=True)
            dcum_ref[...] += jnp.where(lane == 2 * p + e, d_query - d_key, 0.0)

        @pl.when(p == N_PAIR - 1)
        def _():
            dls = _cumsum_rows(dcum_ref[...], row, reverse=True)
            xv = f_ref[...] + b_ref[0:1, :]
            dx = jnp.where(lane < N_HEAD, dls * jax.nn.sigmoid(-xv), 0.0)
            dfl_ref[...] = dx.astype(BF16)
            red_ref[...] = jnp.zeros_like(red_ref)
            red_ref[0:1, :] = jnp.sum(dx, axis=0, keepdims=True)

    wide = jax.ShapeDtypeStruct((S, N_PAIR * LANE), BF16)
    cols = pl.BlockSpec((S, LANE), lambda p: (0, p))
    pair = pl.BlockSpec((2, S, LANE), lambda p: (p, 0, 0))
    return pl.pallas_call(
        body, name=name,
        out_shape=(wide, wide, wide, jax.ShapeDtypeStruct((S, LANE), BF16), jax.ShapeDtypeStruct((8, LANE), F32)),
        grid=(N_PAIR,),
        in_specs=[pl.BlockSpec((S, LANE), lambda p: (0, Z_F // LANE)), pl.BlockSpec((8, LANE), lambda p: (0, 0)),
                  pl.BlockSpec((2, LANE, S), lambda p: (p, 0, 0)), pair, pair],
        out_specs=(cols, cols, cols, pl.BlockSpec((S, LANE), lambda p: (0, 0)), pl.BlockSpec((8, LANE), lambda p: (0, 0))),
        scratch_shapes=[pltpu.VMEM((S, LANE), F32)],
        compiler_params=_params(("arbitrary",)),
    )(z, bf, dqt, dka, dva)


def _attn_fwd(qa, ka, va, name):
    tq, tk = TQ_FWD, TQ
    ratio = tq // tk

    def body(qa_ref, ka_ref, va_ref, o_ref, lse_ref):
        i = pl.program_id(1)
        lane = lax.broadcasted_iota(jnp.int32, (tq, LANE), 1)
        row = lax.broadcasted_iota(jnp.int32, (tq, tk), 0)
        col = lax.broadcasted_iota(jnp.int32, (tq, tk), 1)
        nh = HEADS_PER_STEP_FWD
        qs = [qa_ref[h] for h in range(nh)]

        def block(j, carry, masked):
            off = pl.multiple_of(j * tk, tk)
            out = []
            for h in range(nh):
                m, acc = carry[h]
                s = lax.dot_general(qs[h], ka_ref[h, pl.ds(off, tk), :], _NT, preferred_element_type=F32)
                if masked:
                    s = jnp.where(col + (j - ratio * i) * tk > row, NEG_INF, s)
                mn = jnp.maximum(m, jnp.max(s, axis=1, keepdims=True))
                p = jnp.exp(s - mn).astype(BF16)
                acc = jnp.exp(m - mn) * acc + jnp.dot(p, va_ref[h, pl.ds(off, tk), :], preferred_element_type=F32)
                out.append((mn, acc))
            return tuple(out)

        init = (jnp.full((tq, 1), NEG_INF, F32), jnp.zeros((tq, LANE), F32))
        carry = lax.fori_loop(0, ratio * i, lambda j, c: block(j, c, False), (init,) * nh)
        for d in range(ratio):
            carry = block(ratio * i + d, carry, True)
        res = []
        for h in range(nh):
            m, acc = carry[h]
            l = jnp.sum(jnp.where(_spare(lane, h % 2, 0), acc, 0.0), axis=1, keepdims=True)
            res.append((acc / l, m + jnp.log(l)))
        for g in range(nh // 2):
            o_ref[:, g * LANE:(g + 1) * LANE] = jnp.where(lane < 64, res[2 * g][0], res[2 * g + 1][0])
            lse_ref[:, g * LANE:(g + 1) * LANE] = jnp.where(lane < 64, res[2 * g][1], res[2 * g + 1][1])

    nh = HEADS_PER_STEP_FWD
    out = jax.ShapeDtypeStruct((S, N_PAIR * LANE), F32)
    wide = pl.BlockSpec((tq, 64 * nh), lambda p, i: (i, p))
    return pl.pallas_call(
        body, name=name, out_shape=(out, out), grid=(N_HEAD // nh, S // tq),
        in_specs=[pl.BlockSpec((nh, tq, LANE), lambda p, i: (p, i, 0)), pl.BlockSpec((nh, S, LANE), lambda p, i: (p, 0, 0)),
                  pl.BlockSpec((nh, S, LANE), lambda p, i: (p, 0, 0))],
        out_specs=(wide, wide),
        compiler_params=_params(("parallel", "parallel")),
    )(qa, ka, va)


def _attn_bwd(qa2, ka, va, kat, doa, name):
    nq = S // TQ

    def body(qa_ref, ka_ref, va_ref, kat_ref, doa_ref, dqt_ref, dk_ref, dv_ref):
        j = pl.program_id(1)

        @pl.when(j == 0)
        def _():
            dqt_ref[...] = jnp.zeros_like(dqt_ref)

        key = lax.broadcasted_iota(jnp.int32, (TQ, TQ), 0)
        qry = lax.broadcasted_iota(jnp.int32, (TQ, TQ), 1)
        nh = HEADS_PER_STEP
        kav, vav, katv = ([ref[h] for h in range(nh)] for ref in (ka_ref, va_ref, kat_ref))

        def block(i, carry, masked):
            off = pl.multiple_of(i * TQ, TQ)
            out = []
            for h in range(nh):
                dk_acc, dv_acc = carry[h]
                qav = qa_ref[h, pl.ds(off, TQ), :]
                doav = doa_ref[h, pl.ds(off, TQ), :]
                s_t = lax.dot_general(kav[h], qav, _NT, preferred_element_type=F32)
                if masked:
                    s_t = jnp.where(key > qry, NEG_INF, s_t)
                p_t = jnp.exp(s_t)
                ds_t = p_t * lax.dot_general(vav[h], doav, _NT, preferred_element_type=F32)
                dsb = ds_t.astype(BF16)
                dv_acc = dv_acc + jnp.dot(p_t.astype(BF16), doav, preferred_element_type=F32)
                dk_acc = dk_acc + jnp.dot(dsb, qav, preferred_element_type=F32)
                dqt_ref[h, :, pl.ds(off, TQ)] += jnp.dot(katv[h], dsb, preferred_element_type=F32)
                out.append((dk_acc, dv_acc))
            return tuple(out)

        zero = (jnp.zeros((TQ, LANE), F32), jnp.zeros((TQ, LANE), F32))
        carry = block(j, (zero,) * nh, True)
        carry = lax.fori_loop(j + 1, nq, lambda i, c: block(i, c, False), carry)
        for h in range(nh):
            dk_ref[h], dv_ref[h] = carry[h]

    nh = HEADS_PER_STEP
    full = pl.BlockSpec((nh, S, LANE), lambda p, j: (p, 0, 0))
    blk = pl.BlockSpec((nh, TQ, LANE), lambda p, j: (p, j, 0))
    acc = jax.ShapeDtypeStruct((N_HEAD, S, LANE), F32)
    return pl.pallas_call(
        body, name=name,
        out_shape=(jax.ShapeDtypeStruct((N_HEAD, LANE, S), F32), acc, acc),
        grid=(N_HEAD // nh, nq),
        in_specs=[full, blk, blk, pl.BlockSpec((nh, LANE, TQ), lambda p, j: (p, 0, j)), full],
        out_specs=(pl.BlockSpec((nh, LANE, S), lambda p, j: (p, 0, 0)), blk, blk),
        compiler_params=_params(("arbitrary", "arbitrary")),
    )(qa2, ka, va, kat, doa)


ADA_ROWS = 16


def _ada_fwd(c_pad, w_ada, b_cols, name):
    def body(c_ref, w_ref, b_ref, o_ref):
        cv = c_ref[...]
        sc = (cv * jax.nn.sigmoid(cv)).astype(BF16)
        o_ref[0] = jnp.dot(sc, w_ref[0].astype(BF16), preferred_element_type=F32) + b_ref[0, 0:1, :]

    return pl.pallas_call(
        body, name=name, out_shape=jax.ShapeDtypeStruct((DEPTH, ADA_ROWS, ADA_COLS), F32), grid=(DEPTH,),
        in_specs=[pl.BlockSpec((ADA_ROWS, D), lambda l: (0, 0)), pl.BlockSpec((1, D, ADA_COLS), lambda l: (l, 0, 0)),
                  pl.BlockSpec((1, 8, ADA_COLS), lambda l: (l, 0, 0))],
        out_specs=pl.BlockSpec((1, ADA_ROWS, ADA_COLS), lambda l: (l, 0, 0)),
        compiler_params=_params(("parallel",)),
    )(c_pad, w_ada, b_cols)


def _ada_bwd(c_pad, dmod_cols, name):
    def body(c_ref, d_ref, o_ref):
        cv = c_ref[...]
        sc = (cv * jax.nn.sigmoid(cv)).astype(BF16)
        o_ref[0] = lax.dot_general(sc, d_ref[0].astype(BF16), _TN, preferred_element_type=F32)

    return pl.pallas_call(
        body, name=name, out_shape=jax.ShapeDtypeStruct((DEPTH, D, ADA_COLS), F32), grid=(DEPTH,),
        in_specs=[pl.BlockSpec((ADA_ROWS, D), lambda l: (0, 0)), pl.BlockSpec((1, ADA_ROWS, ADA_COLS), lambda l: (l, 0, 0))],
        out_specs=pl.BlockSpec((1, D, ADA_COLS), lambda l: (l, 0, 0)),
        compiler_params=_params(("parallel",)),
    )(c_pad, dmod_cols)


def _adamw_math(w, g, m, v):
    m = B1 * m + (1.0 - B1) * g
    v = B2 * v + (1.0 - B2) * (g * g)
    m_hat = m / (1.0 - B1 ** STEP)
    v_hat = v / (1.0 - B2 ** STEP)
    delta = -LR * (m_hat / (jnp.sqrt(v_hat) + EPS) + WD * w)
    return delta, m, v


def _row_tile(rows, target=256):
    best = 8
    for t in range(8, min(rows, target) + 1, 8):
        if rows % t == 0:
            best = t
    return best


def _adamw(w, g, m, v, name):
    layers, rows, cols = w.shape
    tr = _row_tile(rows)
    spec = pl.BlockSpec((1, tr, cols), lambda l, i: (l, i, 0))

    def body(w_ref, g_ref, m_ref, v_ref, d_ref, nm_ref, nv_ref):
        d_ref[...], nm_ref[...], nv_ref[...] = _adamw_math(w_ref[...], g_ref[...], m_ref[...], v_ref[...])

    out = jax.ShapeDtypeStruct(w.shape, F32)
    return pl.pallas_call(
        body, name=name, out_shape=(out, out, out), grid=(layers, rows // tr),
        in_specs=[spec] * 4, out_specs=(spec,) * 3, compiler_params=_params(("parallel", "parallel")),
    )(w, g, m, v)


def _sum_slabs(x, name):
    n, rows, _ = x.shape
    tr = _row_tile(rows)

    def body(x_ref, o_ref):
        acc = x_ref[0]
        for k in range(1, n):
            acc = acc + x_ref[k]
        o_ref[...] = acc

    return pl.pallas_call(
        body, name=name, out_shape=jax.ShapeDtypeStruct((rows, D), F32), grid=(rows // tr,),
        in_specs=[pl.BlockSpec((n, tr, D), lambda i: (0, i, 0))], out_specs=pl.BlockSpec((tr, D), lambda i: (i, 0)),
        compiler_params=_params(("parallel",)),
    )(x)


_ANY = pl.BlockSpec(memory_space=pl.ANY)
MESH = pl.DeviceIdType.MESH


def _on_sequencer(body, out_shape, sems, operands, after, sequencer_id, name):
    n = len(operands)

    def ordered_body(*refs):
        body(*refs[:n], *refs[n + 1:])

    extra = [] if after is None else [after]
    return pl.kernel(
        body if after is None else ordered_body, out_type=out_shape,
        mesh=plsc.ScalarSubcoreMesh(axis_name="sequencer", num_cores=1), scratch_types=sems,
        compiler_params=pltpu.CompilerParams(collective_id=sequencer_id), name=name)(*operands, *extra)


def _all_gather(xs, name, sequencer_id=None, after=None):
    n = len(xs)

    def body(*refs):
        x_refs, out_refs = refs[:n], refs[n:2 * n]
        send_sems, recv_sems, local_sems = refs[2 * n:]
        x_, y_, c_ = lax.axis_index("x"), lax.axis_index("y"), lax.axis_index("c")
        me, sibling = (x_, y_, c_), (x_, y_, 1 - c_)
        chips = [(1 - x_, y_), (x_, 1 - y_), (1 - x_, 1 - y_)]
        if sequencer_id is not None:
            barrier = pltpu.get_barrier_semaphore()
            peers = [sibling] + [(*chip, pc) for chip in chips for pc in (c_, 1 - c_)]
            for peer in peers:
                pl.semaphore_signal(barrier, inc=1, device_id=peer, device_id_type=MESH)
            pl.semaphore_wait(barrier, len(peers))

        def slot(a, px, py, pc):
            return out_refs[a].at[4 * px + 2 * py + pc]

        def copy(a, k, block, to, src=None):
            return pltpu.make_async_remote_copy(
                src_ref=slot(a, *block) if src is None else src, dst_ref=slot(a, *block),
                send_sem=send_sems.at[7 * a + k], recv_sem=recv_sems.at[7 * a + k], device_id=to, device_id_type=MESH)

        mine = [pltpu.make_async_copy(x_refs[a], slot(a, *me), local_sems.at[a]) for a in range(n)]
        for cp in mine:
            cp.start()
        first = []
        for a in range(n):
            first.append(copy(a, 0, me, sibling, src=x_refs[a]))
            first += [copy(a, 1 + j, me, (*chip, c_), src=x_refs[a]) for j, chip in enumerate(chips)]
        for cp in first:
            cp.start()
        passed = []
        for j, chip in enumerate(chips):
            for a in range(n):
                copy(a, 1 + j, (*chip, c_), me).wait_recv()
                passed.append(copy(a, 4 + j, (*chip, c_), sibling))
                passed[-1].start()
        for a in range(n):
            copy(a, 0, sibling, me).wait_recv()
        for j, chip in enumerate(chips):
            for a in range(n):
                copy(a, 4 + j, (*chip, 1 - c_), me).wait_recv()
        for cp in first + passed:
            cp.wait_send()
        for cp in mine:
            cp.wait()

    out_shape = [jax.ShapeDtypeStruct((N_DEV,) + x.shape, x.dtype) for x in xs]
    sems = [pltpu.SemaphoreType.DMA((7 * n,)), pltpu.SemaphoreType.DMA((7 * n,)), pltpu.SemaphoreType.DMA((n,))]
    if sequencer_id is not None:
        return _on_sequencer(body, out_shape, sems, xs, after, sequencer_id, name)
    return pl.pallas_call(
        body, name=name, out_shape=out_shape, in_specs=[_ANY] * n, out_specs=[_ANY] * n, scratch_shapes=sems)(*xs)


def _sibling_exchange(gs, name, sequencer_id=None, after=None):
    n = len(gs)

    def body(*refs):
        g_refs, p_refs = refs[:n], refs[n:2 * n]
        send_sems, recv_sems = refs[2 * n:]
        x_, y_, c_ = lax.axis_index("x"), lax.axis_index("y"), lax.axis_index("c")
        if sequencer_id is not None:
            barrier = pltpu.get_barrier_semaphore()
            pl.semaphore_signal(barrier, inc=1, device_id=(x_, y_, 1 - c_), device_id_type=MESH)
            pl.semaphore_wait(barrier, 1)
        copies = [pltpu.make_async_remote_copy(
            src_ref=g_refs[a].at[2 * k + (1 - c_)], dst_ref=p_refs[a].at[k], send_sem=send_sems.at[4 * a + k],
            recv_sem=recv_sems.at[4 * a + k], device_id=(x_, y_, 1 - c_), device_id_type=MESH)
            for a in range(n) for k in range(4)]
        for cp in copies:
            cp.start()
        for cp in copies:
            cp.wait()

    out_shape = [jax.ShapeDtypeStruct((4,) + g.shape[1:], g.dtype) for g in gs]
    sems = [pltpu.SemaphoreType.DMA((4 * n,)), pltpu.SemaphoreType.DMA((4 * n,))]
    if sequencer_id is not None:
        return _on_sequencer(body, out_shape, sems, gs, after, sequencer_id, name)
    return pl.pallas_call(
        body, name=name, out_shape=out_shape, in_specs=[_ANY] * n, out_specs=[_ANY] * n, scratch_shapes=sems)(*gs)


def _slab_tiles(rows, cols):
    if rows % 8 == 0:
        return _row_tile(rows), cols
    return rows, 2 * LANE


def _pair_sums(g, p, route, name):
    _, rows, cols = g.shape
    tr, tc = _slab_tiles(rows, cols)

    def body(route_ref, g_ref, p_ref, t_ref):
        t_ref[...] = (g_ref[...].astype(F32) + p_ref[...].astype(F32)).astype(BF16)

    return pl.pallas_call(
        body, name=name, out_shape=jax.ShapeDtypeStruct((3, rows, cols), BF16),
        grid_spec=pltpu.PrefetchScalarGridSpec(
            num_scalar_prefetch=1, grid=(3, rows // tr, cols // tc),
            in_specs=[pl.BlockSpec((1, tr, tc), lambda r, i, j, route_ref: (2 * route_ref[1 + r] + route_ref[0], i, j)),
                      pl.BlockSpec((1, tr, tc), lambda r, i, j, route_ref: (route_ref[1 + r], i, j))],
            out_specs=pl.BlockSpec((1, tr, tc), lambda r, i, j, route_ref: (r, i, j))),
        compiler_params=_params(("parallel", "parallel", "parallel")),
    )(route, g, p)


def _chip_exchange(ts, name, sequencer_id=None, after=None):
    n = len(ts)

    def body(*refs):
        t_refs, l_refs = refs[:n], refs[n:2 * n]
        send_sems, recv_sems = refs[2 * n:]
        x_, y_, c_ = lax.axis_index("x"), lax.axis_index("y"), lax.axis_index("c")
        chips = [(1 - x_, y_), (x_, 1 - y_), (1 - x_, 1 - y_)]
        if sequencer_id is not None:
            barrier = pltpu.get_barrier_semaphore()
            for px, py in chips:
                pl.semaphore_signal(barrier, inc=1, device_id=(px, py, c_), device_id_type=MESH)
            pl.semaphore_wait(barrier, len(chips))
        copies = [pltpu.make_async_remote_copy(
            src_ref=t_refs[a].at[r], dst_ref=l_refs[a].at[r], send_sem=send_sems.at[3 * a + r],
            recv_sem=recv_sems.at[3 * a + r], device_id=(px, py, c_), device_id_type=MESH)
            for a in range(n) for r, (px, py) in enumerate(chips)]
        for cp in copies:
            cp.start()
        for cp in copies:
            cp.wait()

    out_shape = [jax.ShapeDtypeStruct((3,) + t.shape[1:], t.dtype) for t in ts]
    sems = [pltpu.SemaphoreType.DMA((3 * n,)), pltpu.SemaphoreType.DMA((3 * n,))]
    if sequencer_id is not None:
        return _on_sequencer(body, out_shape, sems, ts, after, sequencer_id, name)
    return pl.pallas_call(
        body, name=name, out_shape=out_shape, in_specs=[_ANY] * n, out_specs=[_ANY] * n, scratch_shapes=sems)(*ts)


def _reduce_adamw(gs, ps, landed, place, w, m, v, name):
    layers, rows, cols = w.shape
    assert layers == DEPTH == 2
    tr, tc = _slab_tiles(rows, cols)
    nr, nc = rows // tr, cols // tc
    spec = pl.BlockSpec((1, tr, tc), lambda l, i, j, place_ref: (l, i, j))

    def own(layer, which):
        pi, pj = (nr - 1, nc - 1) if layer == 0 else (0, 0)

        def index(l, i, j, place_ref):
            lead = 0 if which is None else place_ref[which]
            return lead, jnp.where(l == layer, i, pi), jnp.where(l == layer, j, pj)

        return pl.BlockSpec((3 if which is None else 1, tr, tc), index)

    def body(place_ref, g0_ref, p0_ref, l0_ref, g1_ref, p1_ref, l1_ref, w_ref, m_ref, v_ref,
             g_ref, d_ref, nm_ref, nv_ref):
        def update(own_ref, sib_ref, l_ref):
            g = (own_ref[0].astype(F32) + sib_ref[0].astype(F32) + l_ref[0].astype(F32) + l_ref[1].astype(F32)
                 + l_ref[2].astype(F32))
            g_ref[0] = g
            d_ref[0], nm_ref[0], nv_ref[0] = _adamw_math(w_ref[0], g, m_ref[0], v_ref[0])

        @pl.when(pl.program_id(0) == 0)
        def _():
            update(g0_ref, p0_ref, l0_ref)

        @pl.when(pl.program_id(0) == 1)
        def _():
            update(g1_ref, p1_ref, l1_ref)

    out = jax.ShapeDtypeStruct(w.shape, F32)
    return pl.pallas_call(
        body, name=name, out_shape=(out, out, out, out),
        grid_spec=pltpu.PrefetchScalarGridSpec(
            num_scalar_prefetch=1, grid=(DEPTH, nr, nc),
            in_specs=[own(0, 0), own(0, 1), own(0, None), own(1, 0), own(1, 1), own(1, None), spec, spec, spec],
            out_specs=(spec, spec, spec, spec)),
        compiler_params=_params(("arbitrary", "arbitrary", "arbitrary")),
    )(place, gs[0], ps[0], landed[0], gs[1], ps[1], landed[1], w, m, v)


def _pack(pieces, row_multiple, dtype, cols=D, rows=None):
    flat = jnp.concatenate([p.astype(dtype).reshape(-1) for p in pieces])
    if rows is None:
        rows = -(-flat.shape[0] // cols)
        rows = -(-rows // row_multiple) * row_multiple
    flat = jnp.pad(flat, (0, rows * cols - flat.shape[0]))
    return flat.reshape(rows, cols)


def _unpack(flat, shapes, lead=()):
    out, off = [], 0
    for shp in shapes:
        n = 1
        for s_ in shp:
            n *= s_
        out.append(lax.slice_in_dim(flat, off, off + n, axis=len(lead)).reshape(lead + tuple(shp)))
        off += n
    return out


WIN_STRIDE = 704
WIN_ROWS = 720
Z_TURN = 1544


def _window(wt, me, name):
    padded = jnp.pad(wt, ((0, 0), (0, WIN_ROWS - IN_SHARD), (0, 0)))

    def body(me_ref, x_ref, o_ref):
        o_ref[0] = pltpu.roll(x_ref[0], me_ref[0], axis=0).astype(BF16)

    spec = pl.BlockSpec((1, WIN_ROWS, D), lambda l, me_ref: (l, 0, 0))
    return pl.pallas_call(
        body, name=name, out_shape=jax.ShapeDtypeStruct((DEPTH, WIN_ROWS, D), BF16),
        grid_spec=pltpu.PrefetchScalarGridSpec(num_scalar_prefetch=1, grid=(DEPTH,), in_specs=[spec], out_specs=spec),
        compiler_params=_params(("parallel",)),
    )(me, padded)


def _z_rows_from_windows(win):
    over = WIN_ROWS - WIN_STRIDE
    pieces = [(0, win[0][0:WIN_STRIDE])]
    for d in range(1, N_DEV):
        base = WIN_STRIDE * d
        pieces.append((base, win[d - 1][WIN_STRIDE:WIN_ROWS] + win[d][0:over]))
        pieces.append((base + over, win[d][over:WIN_STRIDE]))
    pieces.append((WIN_STRIDE * N_DEV, win[N_DEV - 1][WIN_STRIDE:WIN_ROWS]))

    def rows(a, b):
        out = []
        for start, arr in pieces:
            lo, hi = max(a, start), min(b, start + arr.shape[0])
            if lo < hi:
                out.append(arr[lo - start:hi - start])
        return out

    pad = jnp.zeros((NZ - IN_COLS, win.shape[-1]), win.dtype)
    return jnp.concatenate(rows(Z_TURN, IN_COLS) + rows(0, Z_TURN) + [pad], axis=0)


def _in_rows_from_z(wt):
    return jnp.concatenate([wt[Z_Q:Z_Q + 1536], wt[Z_F:Z_F + 8], wt[Z_PC:Z_PC + 1024], wt[Z_G:Z_G + 3072]], axis=0)


def _pad_rows(v, rows=8):
    return jnp.pad(v, ((0, rows - v.shape[0]), (0, 0)))


def _layer_fwd(l, x, wts, gvec, mod):
    tag = f"l{l}"
    h = _prenorm_fwd(x, gvec, mod, 0, 0, 1, f"prenorm_mix_{tag}")
    z = _matmul(h, wts["w_in_t"], "nt", f"in_proj_{tag}", tn=1152)
    qa, ka, va, kat = _attn_prep(z, wts["b_f"], f"attn_prep_{tag}")
    qa = wts["arrive"](qa)
    o, lse = _attn_fwd(qa, ka, va, f"attn_{tag}")
    br_b = _pool_fwd(z, wts["wp_bd"], wts["pool_scale"], f"pool_{tag}")
    br_c = _conv_fwd(z, wts["conv_w"], f"conv_{tag}")
    pa = _matmul(o, wts["wa"], "nn", f"proj_a_{tag}", out_dtype=BF16)
    pb = _matmul(br_b, wts["wb"], "nn", f"proj_b_{tag}", out_dtype=BF16)
    gates = [(z, Z_G + k * D) for k in range(3)]
    pc, merged = _matmul(br_c, wts["wc"], "nn", f"proj_c_merge_{tag}", tm=1024, tn=512,
                         extra=gates + [(pa, 0), (pb, 0)], epilogue=_merge_epilogue, out_dtypes=(BF16, BF16))
    y, x1 = _matmul(merged, wts["w_out"], "nn", f"out_proj_{tag}", tm=1024, tn=D, extra=[(x, 0)],
                    vec_extra=[gvec, mod], epilogue=_postnorm_epilogue(1, 2), out_dtypes=(F32, F32))
    h2 = _prenorm_fwd(x1, gvec, mod, 2, 3, 4, f"prenorm_ff_{tag}")
    a, r = _matmul(h2, wts["w_ff1"], "nn", f"ff1_{tag}", b_col_shards=True, epilogue=_relu2_epilogue,
                   out_dtypes=(BF16, BF16))
    y2, x2 = _matmul(r, wts["w_ff2"], "nn", f"ff2_{tag}", tm=1024, tn=D, tk=1024, extra=[(x1, 0)],
                     vec_extra=[gvec, mod], epilogue=_postnorm_epilogue(3, 5), out_dtypes=(F32, F32))
    saved = dict(x=x, h=h, z=z, qa=qa, ka=ka, va=va, kat=kat, o=o, lse=lse, br_b=br_b, br_c=br_c, pa=pa, pb=pb, pc=pc,
                 merged=merged, y=y, x1=x1, h2=h2, a=a, r=r, y2=y2)
    return x2, saved


def _ffn_bwd(l, dx2, sv, wts, gvec, mod, midpoint):
    tag = f"l{l}"
    dy2, red_post_ff = _postnorm_bwd(sv["y2"], gvec, mod, dx2, 3, 5, f"postnorm_ff_bwd_{tag}")
    dy2 = midpoint(dy2)
    da = _matmul(dy2, wts["w_ff2"], "nt", f"ff2_dx_{tag}", extra=[(sv["a"], 0)], epilogue=_relu2_bwd_epilogue,
                 out_dtypes=(BF16,))[0]
    d_w_ff2 = _matmul(sv["r"], dy2, "tn", f"ff2_dw_{tag}", out_dtype=GRAD_DTYPE)
    dh2 = _matmul(da, wts["w_ff1"], "nt", f"ff1_dx_{tag}", b_col_shards=True)
    d_w_ff1 = _matmul(sv["h2"], da, "tn", f"ff1_dw_{tag}", out_dtype=GRAD_DTYPE, out_col_shards=True)
    dx1, red_pre_ff = _prenorm_bwd(sv["x1"], gvec, mod, dh2, dx2, 2, 4, f"prenorm_ff_bwd_{tag}")
    return dx1, [d_w_ff1, d_w_ff2.reshape(N_DEV, D_FF // N_DEV, D)], (red_pre_ff, red_post_ff)


def _mixer_bwd(l, dx1, sv, wts, gvec, mod, ffn_reds, midpoint):
    tag = f"l{l}"
    red_pre_ff, red_post_ff = ffn_reds
    dy, red_post_mix = _postnorm_bwd(sv["y"], gvec, mod, dx1, 1, 2, f"postnorm_mix_bwd_{tag}")
    gates = [(sv["z"], Z_G + k * D) for k in range(3)]
    dpa, dpb, dpc, *dgl = _matmul(dy, wts["w_out"], "nt", f"out_proj_dx_{tag}", tm=1024, tn=512,
                                  extra=gates + [(sv["pa"], 0), (sv["pb"], 0), (sv["pc"], 0)],
                                  epilogue=_merge_bwd_epilogue, out_dtypes=(BF16,) * 6)
    d_w_out = _matmul(sv["merged"], dy, "tn", f"out_proj_dw_{tag}", out_dtype=GRAD_DTYPE)
    dpa = midpoint(dpa)
    do = _matmul(dpa, wts["wa"], "nt", f"proj_a_dx_{tag}")
    dbr_b = _matmul(dpb, wts["wb"], "nt", f"proj_b_dx_{tag}")
    dbr_c = _matmul(dpc, wts["wc"], "nt", f"proj_c_dx_{tag}")
    d_wa = _matmul(sv["o"], dpa, "tn", f"proj_a_dw_{tag}", out_dtype=GRAD_DTYPE)
    d_wb = _matmul(sv["br_b"], dpb, "tn", f"proj_b_dw_{tag}", out_dtype=GRAD_DTYPE)
    d_wc = _matmul(sv["br_c"], dpc, "tn", f"proj_c_dw_{tag}", out_dtype=GRAD_DTYPE)
    d_w_branch = jnp.concatenate([d_wa, d_wb, d_wc], axis=0)

    dpu, d_wp_bd, red_pool = _pool_bwd(sv["z"], wts["wp_bd"], wts["pool_scale"], dbr_b, f"pool_bwd_{tag}")
    dconv, red_conv = _conv_bwd(sv["z"], wts["conv_w"], dbr_c, f"conv_bwd_{tag}")
    qa2, doa = _attn_bwd_prep(sv["qa"], sv["o"], sv["lse"], do, f"attn_bwd_prep_{tag}")
    dqt, dka, dva = _attn_bwd(qa2, sv["ka"], sv["va"], sv["kat"], doa, f"attn_bwd_{tag}")
    dq, dk, dv, dfl, red_f = _attn_bwd_post(sv["z"], wts["b_f"], dqt, dka, dva, f"attn_bwd_post_{tag}")
    dz = _concat_columns([dpu, dconv, *dgl, dq, dk, dv, dfl], f"dz_{tag}")
    dh = _matmul(dz, wts["w_in_t"], "nn", f"in_proj_dx_{tag}", tm=1024, tk=1920)
    d_w_in_t = _matmul(dz, sv["h"], "tn", f"in_proj_dw_{tag}", out_dtype=GRAD_DTYPE, tm=1152)
    dx0, red_pre_mix = _prenorm_bwd(sv["x"], gvec, mod, dh, dx1, 0, 1, f"prenorm_mix_bwd_{tag}")

    rows = D // N_DEV
    big = [_in_rows_from_z(d_w_in_t).reshape(N_DEV, IN_SHARD, D), d_w_branch.reshape(N_DEV, rows, D),
           d_w_out.reshape(N_DEV, rows, D)]
    d_w_pool = jnp.stack([d_wp_bd[64 * g:64 * (g + 1), 64 * g:64 * (g + 1)] for g in range(4)])
    small = dict(
        mod=jnp.stack([red_pre_mix[0], red_pre_mix[1], red_post_mix[0], red_pre_ff[0], red_pre_ff[1], red_post_ff[0]]),
        g_mix_pre=red_pre_mix[2], g_mix_post=red_post_mix[1], g_ff_pre=red_pre_ff[2], g_ff_post=red_post_ff[1],
        b_f=red_f[0, 0:8], w_pool=d_w_pool, pool_scale=red_pool[0], conv_w=red_conv[0:3])
    return dx0, big, small


SMALL_KEYS = ["mod", "g_mix_pre", "g_mix_post", "g_ff_pre", "g_ff_post", "b_f", "w_pool", "pool_scale", "conv_w"]
SMALL_SHAPES = [(DEPTH, 6 * D), (DEPTH, D), (DEPTH, D), (DEPTH, D), (DEPTH, D), (DEPTH, 8), (DEPTH, 4, 64, 64),
                (DEPTH, POOL_W), (DEPTH, 3, CONV_W)]


def kernel(x, c, w_ada, b_ada, g_mix_pre, g_mix_post, g_ff_pre, g_ff_post, w_in, b_f, w_pool, pool_scale, conv_w, w_branch, w_out, w_ff1, w_ff2, loss_target, m_w_ada, m_b_ada, m_g_mix_pre, m_g_mix_post, m_g_ff_pre, m_g_ff_post, m_w_in, m_b_f, m_w_pool, m_pool_scale, m_conv_w, m_w_branch, m_w_out, m_w_ff1, m_w_ff2, v_w_ada, v_b_ada, v_g_mix_pre, v_g_mix_post, v_g_ff_pre, v_g_ff_post, v_w_in, v_b_f, v_w_pool, v_pool_scale, v_conv_w, v_w_branch, v_w_out, v_w_ff1, v_w_ff2):
    ix, iy, ic = lax.axis_index("x"), lax.axis_index("y"), lax.axis_index("c")
    me = 4 * ix + 2 * iy + ic
    route = jnp.stack([ic, 2 * (1 - ix) + iy, 2 * ix + (1 - iy), 2 * (1 - ix) + (1 - iy)]).astype(jnp.int32)
    place = jnp.stack([me, 2 * ix + iy]).astype(jnp.int32)
    wt_in, mt_in, vt_in = (jnp.transpose(a, (0, 2, 1)) for a in (w_in, m_w_in, v_w_in))

    c_all = _all_gather([_pad_rows(c)], "gather_c")[0][:, 0, :]
    c_pad = _pad_rows(c_all, ADA_ROWS)
    b_cols = lax.dynamic_slice_in_dim(b_ada, me * ADA_COLS, ADA_COLS, axis=1)
    b_cols = jnp.broadcast_to(b_cols[:, None, :], (DEPTH, 8, ADA_COLS))
    mod_part = _ada_fwd(c_pad, w_ada, b_cols, "ada_fwd")
    mod_all = _all_gather([mod_part.reshape(DEPTH * ADA_ROWS, ADA_COLS)], "gather_mod")[0]
    mod_all = mod_all.reshape(N_DEV, DEPTH, ADA_ROWS, ADA_COLS)
    mod_mine = lax.dynamic_index_in_dim(mod_all, me, axis=2, keepdims=False)
    mod_mine = jnp.transpose(mod_mine, (1, 0, 2)).reshape(DEPTH, 6, D)

    cw_cols = CONV_W // N_DEV
    cw_send = jnp.pad(conv_w.reshape(DEPTH * 3, cw_cols), ((0, 8 - DEPTH * 3), (0, LANE - cw_cols)))
    win_in = _window(wt_in, place[0:1], "w_in_window")
    send = [[w[l].astype(BF16) for w in (win_in, w_branch, w_out, w_ff1, w_ff2)] for l in range(DEPTH)]
    first = _all_gather(send[0][:1], "gather_weights_l0_in", sequencer_id=1, after=mod_all)
    rest = _all_gather(send[0][1:] + [cw_send], "gather_weights_l0_rest", sequencer_id=2, after=first[0])
    first1 = _all_gather(send[1][:1], "gather_weights_l1_in", sequencer_id=3, after=first[0])
    rest1 = _all_gather(send[1][1:], "gather_weights_l1_rest", sequencer_id=12, after=first[0])
    first, (mt_in, vt_in) = lax.optimization_barrier((first, (mt_in, vt_in)))
    gathered = [first + rest[:4], first1 + rest1]
    cw_all = rest[4][:, :DEPTH * 3, :cw_cols].reshape(N_DEV, DEPTH, 3, cw_cols)

    def first_operands(l, p_in):
        wp_bd = jnp.zeros((POOL_W, POOL_W), F32)
        for g in range(4):
            wp_bd = wp_bd.at[64 * g:64 * (g + 1), 64 * g:64 * (g + 1)].set(w_pool[l, g])
        return dict(w_in_t=_z_rows_from_windows(p_in), wp_bd=wp_bd.astype(BF16),
                    pool_scale=_pad_rows(pool_scale[l][None, :]), b_f=_pad_rows(jnp.pad(b_f[l], (0, LANE - 8))[None, :]))

    def rest_operands(l, rest):
        p_br, p_out, p_ff1, p_ff2 = rest
        w_br_full = p_br.reshape(D, D)
        cw_full = jnp.transpose(cw_all[:, l], (1, 0, 2)).reshape(3, CONV_W)
        return dict(wa=w_br_full[0:A_WIDTH], wb=w_br_full[A_WIDTH:A_WIDTH + POOL_W], wc=w_br_full[A_WIDTH + POOL_W:],
                    w_out=p_out.reshape(D, D), w_ff1=p_ff1, w_ff2=p_ff2.reshape(D_FF, D), conv_w=_pad_rows(cw_full))

    xs = x[0]
    saved, layers = [], []
    for l in range(DEPTH):
        p_in, rest = gathered[l][0], gathered[l][1:5]
        if l > 0:
            xs, p_in = lax.optimization_barrier((xs, p_in))
        wts = first_operands(l, p_in)

        def arrive(t, l=l, rest=rest, wts=wts):
            if l > 0:
                t, rest = lax.optimization_barrier((t, rest))
            wts.update(rest_operands(l, rest))
            return t

        wts["arrive"] = arrive
        gvec = _pad_rows(jnp.stack([g_mix_pre[l], g_mix_post[l], g_ff_pre[l], g_ff_post[l]]))
        layers.append((wts, gvec, _pad_rows(mod_mine[l])))
        xs, sv = _layer_fwd(l, xs, *layers[l])
        saved.append(sv)
    dx, loss_part = _loss_head(xs, loss_target[0], "loss_head")
    small_grads = [None] * DEPTH
    mine, sibs, landed = ({} for _ in range(3))
    seq_id = iter(range(4, 4 + 4 * DEPTH))
    last = [gathered[DEPTH - 1][1]]

    def start(group, grads):
        mine[group] = grads
        sibs[group] = _sibling_exchange(grads, f"rs_sibling_{group}", sequencer_id=next(seq_id), after=last[0])
        last[0] = sibs[group][0]

    def finish(group, later):
        later, (grads, sib) = lax.optimization_barrier((later, (mine[group], sibs[group])))
        sends = [_pair_sums(g, p, route, f"rs_pair_sums_{group}_{k}") for k, (g, p) in enumerate(zip(grads, sib))]
        later, sends = lax.optimization_barrier((later, sends))
        landed[group] = _chip_exchange(sends, f"rs_chips_{group}", sequencer_id=next(seq_id), after=last[0])
        last[0] = landed[group][0]
        return later

    pending = None
    for l in reversed(range(DEPTH)):
        hook = (lambda da: da) if pending is None else functools.partial(finish, pending)
        dx, ffn_grads, ffn_reds = _ffn_bwd(l, dx, saved[l], *layers[l], hook)
        start(f"ffn_l{l}", ffn_grads)
        dx, mix_grads, small_grads[l] = _mixer_bwd(l, dx, saved[l], *layers[l], ffn_reds,
                                                   functools.partial(finish, f"ffn_l{l}"))
        start(f"mix_l{l}", mix_grads)
        pending = f"mix_l{l}"
    grad_x = dx[None]

    big_w = [wt_in, w_branch, w_out, w_ff1, w_ff2]
    big_m = [mt_in, m_w_branch, m_w_out, m_w_ff1, m_w_ff2]
    big_v = [vt_in, v_w_branch, v_w_out, v_w_ff1, v_w_ff2]
    where = [("mix", 0), ("mix", 1), ("mix", 2), ("ffn", 0), ("ffn", 1)]

    def reduce_and_update(k):
        group, at = where[k]
        return _reduce_adamw([mine[f"{group}_l{l}"][at] for l in range(DEPTH)],
                             [sibs[f"{group}_l{l}"][at] for l in range(DEPTH)],
                             [landed[f"{group}_l{l}"][at] for l in range(DEPTH)], place, big_w[k], big_m[k], big_v[k],
                             f"rs_sum_adamw_{k}")

    big_res = {k: list(reduce_and_update(k)) for k in (3, 4)}
    big_res[3][0] = finish(pending, big_res[3][0])

    small = {k: jnp.stack([small_grads[l][k] for l in range(DEPTH)]) for k in SMALL_KEYS}
    payload = _pack([small[k] for k in SMALL_KEYS] + [loss_part[0:1, 0:1]], 8, F32)
    small_all = _all_gather([payload], "gather_small")[0]
    dmod_all = small_all[:, 0:DEPTH * 6, :].reshape(N_DEV, DEPTH, 6 * D)
    summed = _unpack(_sum_slabs(small_all, "sum_small").reshape(-1), SMALL_SHAPES + [(1, 1)])
    sg = dict(zip(SMALL_KEYS, summed))
    loss = summed[-1][0, 0]
    dmod_cols = lax.dynamic_slice_in_dim(dmod_all, me * ADA_COLS, ADA_COLS, axis=2)
    dmod_cols = jnp.pad(jnp.transpose(dmod_cols, (1, 0, 2)), ((0, 0), (0, ADA_ROWS - N_DEV), (0, 0)))
    g_w_ada = _ada_bwd(c_pad, dmod_cols, "ada_bwd")
    g_conv_w = lax.dynamic_slice_in_dim(sg["conv_w"], me * (CONV_W // N_DEV), CONV_W // N_DEV, axis=2)

    ada_out = [g_w_ada] + list(_adamw(w_ada, g_w_ada, m_w_ada, v_w_ada, "adamw_ada"))
    rest_w = [b_ada, g_mix_pre, g_mix_post, g_ff_pre, g_ff_post, b_f, w_pool, pool_scale, conv_w]
    rest_m = [m_b_ada, m_g_mix_pre, m_g_mix_post, m_g_ff_pre, m_g_ff_post, m_b_f, m_w_pool, m_pool_scale, m_conv_w]
    rest_v = [v_b_ada, v_g_mix_pre, v_g_mix_post, v_g_ff_pre, v_g_ff_post, v_b_f, v_w_pool, v_pool_scale, v_conv_w]
    rest_g = [sg["mod"], sg["g_mix_pre"], sg["g_mix_post"], sg["g_ff_pre"], sg["g_ff_post"], sg["b_f"],
              sg["w_pool"], sg["pool_scale"], g_conv_w]
    rest_shapes = [a.shape for a in rest_w]
    upd = _adamw(_pack(rest_w, 8, F32)[None], _pack(rest_g, 8, F32)[None], _pack(rest_m, 8, F32)[None],
                 _pack(rest_v, 8, F32)[None], "adamw_rest")
    rest_out = [rest_g] + [_unpack(arr.reshape(-1), rest_shapes) for arr in upd]
    rest_out = [[ada_out[which]] + rest_out[which] for which in range(4)]

    landed[pending], rest_out = lax.optimization_barrier((landed[pending], rest_out))
    big_res.update({k: reduce_and_update(k) for k in (0, 1, 2)})
    big_out = [[jnp.transpose(big_res[k][which], (0, 2, 1)) if k == 0 else big_res[k][which] for k in range(5)]
               for which in range(4)]

    def ordered(k):
        r, b = rest_out[k], big_out[k]
        return [r[0], r[1], r[2], r[3], r[4], r[5], b[0], r[6], r[7], r[8], r[9], b[1], b[2], b[3], b[4]]

    return (loss, grad_x, *ordered(0), *ordered(1), *ordered(2), *ordered(3))
```

```python
import functools

import jax
import jax.numpy as jnp
from jax import lax
from jax.experimental import pallas as pl
from jax.experimental.pallas import tpu as pltpu
from jax.experimental.pallas import tpu_sc as plsc

F32 = jnp.float32
BF16 = jnp.bfloat16
GRAD_DTYPE = BF16

N_DEV = 8
D = 1024
S = 2048
DEPTH = 2
D_FF = 4 * D
A_WIDTH = 512
HEAD_DIM = 64
N_PAIR = 4
POOL_W = 256
CONV_W = 256
IN_COLS = 5640
ADA_COLS = 6 * D // N_DEV
IN_SHARD = IN_COLS // N_DEV
RMS_EPS = 1e-6
NEG_INF = -1e30
ATT_SCALE = HEAD_DIM ** -0.5

NZ = 5760
Z_PC = 0
Z_G = 1024
Z_Q = 4096
Z_K = 4608
Z_V = 5120
Z_F = 5632

LR, B1, B2, EPS, WD, STEP = 0.001, 0.9, 0.999, 1e-08, 0.01, 10

LANE = 128
VMEM_LIMIT_BYTES = 48 * 1024 * 1024
TS = 512
TQ = 256
TQ_FWD = 512
HEADS_PER_STEP = 8
HEADS_PER_STEP_FWD = 8


def _params(sem=None):
    return pltpu.CompilerParams(dimension_semantics=sem, vmem_limit_bytes=VMEM_LIMIT_BYTES)


def _pick(n, target):
    best = None
    for t in range(LANE, min(n, target) + 1, LANE):
        if n % t == 0:
            best = t
    return n if best is None else best


def _matmul(a, b, mode, name, out_dtype=F32, tm=2048, tn=1024, tk=2048, b_col_shards=False, out_col_shards=False,
            extra=(), vec_extra=(), epilogue=None, out_dtypes=None, prologue=None, prologue_vecs=()):
    if b_col_shards:
        shards, b_rows, shard_cols = b.shape
        b_shape = (b_rows, shards * shard_cols)
    else:
        b_shape = b.shape
    if mode == "nn":
        (m, k), (k2, n) = a.shape, b_shape
    elif mode == "nt":
        (m, k), (n, k2) = a.shape, b_shape
    else:
        (k, m), (k2, n) = a.shape, b_shape
    assert k == k2, (a.shape, b.shape, mode)
    tm, tn, tk = _pick(m, tm), _pick(n, tn), _pick(k, tk)
    if b_col_shards and mode == "nn":
        tn = shard_cols
    per_step = 1
    if b_col_shards and mode == "nt":
        per_step = max(1, min(tk, 1024) // shard_cols)
        tk = per_step * shard_cols
    if out_col_shards:
        tn = n // N_DEV
    nk = k // tk
    if mode == "nn":
        a_spec = pl.BlockSpec((tm, tk), lambda i, j, kk: (i, kk))
        b_spec = (pl.BlockSpec((None, tk, tn), lambda i, j, kk: (j, kk, 0)) if b_col_shards else
                  pl.BlockSpec((tk, tn), lambda i, j, kk: (kk, j)))
        dims = (((1,), (0,)), ((), ()))
    elif mode == "nt":
        a_spec = pl.BlockSpec((tm, tk), lambda i, j, kk: (i, kk))
        b_spec = (pl.BlockSpec((per_step, tn, shard_cols), lambda i, j, kk: (kk, j, 0)) if b_col_shards else
                  pl.BlockSpec((tn, tk), lambda i, j, kk: (j, kk)))
        dims = (((1,), (1,)), ((), ()))
    else:
        assert not b_col_shards
        a_spec = pl.BlockSpec((tk, tm), lambda i, j, kk: (kk, i))
        b_spec = pl.BlockSpec((tk, tn), lambda i, j, kk: (kk, j))
        dims = (((0,), (0,)), ((), ()))
    if out_col_shards:
        out_shape = jax.ShapeDtypeStruct((N_DEV, m, tn), out_dtype)
        out_spec = pl.BlockSpec((None, tm, tn), lambda i, j, kk: (j, i, 0))
    else:
        out_shape = jax.ShapeDtypeStruct((m, n), out_dtype)
        out_spec = pl.BlockSpec((tm, tn), lambda i, j, kk: (i, j))

    n_extra = len(extra) + len(vec_extra)
    extra_specs = [pl.BlockSpec((tm, tn), lambda i, j, kk, off=off: (i, j + off // tn)) for _, off in extra]
    extra_specs += [pl.BlockSpec((8, tn), lambda i, j, kk: (0, j)) for _ in vec_extra]
    if epilogue is not None:
        assert not out_col_shards and all(off % tn == 0 for _, off in extra)
        out_shape = [jax.ShapeDtypeStruct((m, n), dt) for dt in out_dtypes]
        out_spec = [pl.BlockSpec((tm, tn), lambda i, j, kk: (i, j)) for _ in out_dtypes]

    def product(a_ref, b_ref):
        if b_col_shards and mode == "nt":
            b_tile = jnp.concatenate([b_ref[s] for s in range(per_step)], axis=1) if per_step > 1 else b_ref[0]
        else:
            b_tile = b_ref[...]
        return lax.dot_general(a_ref[...].astype(BF16), b_tile.astype(BF16), dims, preferred_element_type=F32)

    def write(acc, extra_refs, o_refs):
        if epilogue is None:
            o_refs[0][...] = acc.astype(out_dtype)
        else:
            for o_ref, tile in zip(o_refs, epilogue(acc, *[r[...] for r in extra_refs])):
                o_ref[...] = tile.astype(o_ref.dtype)

    def body_one_pass(a_ref, b_ref, *refs):
        write(product(a_ref, b_ref), refs[:n_extra], refs[n_extra:])

    if prologue is not None:
        assert nk == 1 and mode in ("nn", "nt")
        n_pro = len(prologue_vecs)
        outs = out_shape if isinstance(out_shape, list) else [out_shape]
        out_specs_all = (out_spec if isinstance(out_spec, list) else [out_spec]) + [
            pl.BlockSpec((tm, tk), lambda i, j, kk: (i, 0))]

        def body_prologue(a_ref, b_ref, *refs):
            pro_refs, rest = refs[:n_pro], refs[n_pro:]
            left_out, left_ref = rest[-2], rest[-1]

            @pl.when(pl.program_id(1) == 0)
            def _():
                left = prologue(a_ref[...], *[r[...] for r in pro_refs]).astype(BF16)
                left_ref[...] = left
                left_out[...] = left

            write(product(left_ref, b_ref), rest[:n_extra], rest[n_extra:-2])

        return pl.pallas_call(
            body_prologue, name=name,
            out_shape=outs + [jax.ShapeDtypeStruct((m, k), BF16)],
            grid=(m // tm, n // tn, nk),
            in_specs=[a_spec, b_spec] + [pl.BlockSpec((8, tk), lambda i, j, kk: (0, 0)) for _ in prologue_vecs] + extra_specs,
            out_specs=out_specs_all,
            scratch_shapes=[pltpu.VMEM((tm, tk), BF16)],
            compiler_params=_params(("parallel", "arbitrary", "arbitrary")),
        )(a, b, *prologue_vecs, *[x for x, _ in extra], *vec_extra)

    def body(a_ref, b_ref, *refs):
        acc_ref = refs[-1]
        kk = pl.program_id(2)

        @pl.when(kk == 0)
        def _():
            acc_ref[...] = product(a_ref, b_ref)

        @pl.when(kk > 0)
        def _():
            acc_ref[...] += product(a_ref, b_ref)

        @pl.when(kk == nk - 1)
        def _():
            write(acc_ref[...], refs[:n_extra], refs[n_extra:-1])

    return pl.pallas_call(
        body_one_pass if nk == 1 else body, name=name,
        out_shape=out_shape,
        grid=(m // tm, n // tn, nk),
        in_specs=[a_spec, b_spec] + extra_specs,
        out_specs=out_spec,
        scratch_shapes=[] if nk == 1 else [pltpu.VMEM((tm, tn), F32)],
        compiler_params=_params(("parallel", "parallel", "arbitrary")),
    )(a, b, *[x for x, _ in extra], *vec_extra)


def _row_spec(width=D, col=0):
    return pl.BlockSpec((TS, width), lambda i: (i, col))


def _vec_spec(rows=8, width=D):
    return pl.BlockSpec((rows, width), lambda i: (0, 0))


def _rms(x):
    return lax.rsqrt(jnp.mean(x * x, axis=-1, keepdims=True) + RMS_EPS)


def _prenorm_bwd(x, gvec, mod, dh, dres, g_row, scale_row, name):
    def body(x_ref, g_ref, mod_ref, dh_ref, dres_ref, dx_ref, red_ref):
        i = pl.program_id(0)

        @pl.when(i == 0)
        def _():
            red_ref[...] = jnp.zeros_like(red_ref)

        xv = x_ref[...]
        g = g_ref[g_row:g_row + 1, :]
        r = _rms(xv)
        n = xv * r
        yg = n * g
        dhv = dh_ref[...]
        dyg = dhv * (1.0 + mod_ref[scale_row:scale_row + 1, :])
        dn = dyg * g
        dx = r * (dn - n * jnp.mean(dn * n, axis=-1, keepdims=True))
        dx_ref[...] = dres_ref[...] + dx
        red_ref[0:1, :] += jnp.sum(dhv, axis=0, keepdims=True)
        red_ref[1:2, :] += jnp.sum(dhv * yg, axis=0, keepdims=True)
        red_ref[2:3, :] += jnp.sum(dyg * n, axis=0, keepdims=True)

    return pl.pallas_call(
        body, name=name,
        out_shape=(jax.ShapeDtypeStruct((S, D), F32), jax.ShapeDtypeStruct((8, D), F32)),
        grid=(S // TS,),
        in_specs=[_row_spec(), _vec_spec(), _vec_spec(), _row_spec(), _row_spec()],
        out_specs=(_row_spec(), _vec_spec()),
        compiler_params=_params(("arbitrary",)),
    )(x, gvec, mod, dh, dres)


def _postnorm_bwd(y, gvec, mod, dxo, g_row, gate_row, name):
    def body(y_ref, g_ref, mod_ref, dxo_ref, dy_ref, red_ref):
        i = pl.program_id(0)

        @pl.when(i == 0)
        def _():
            red_ref[...] = jnp.zeros_like(red_ref)

        yv = y_ref[...]
        g = g_ref[g_row:g_row + 1, :]
        r = _rms(yv)
        n = yv * r
        dxo = dxo_ref[...]
        dyn = dxo * mod_ref[gate_row:gate_row + 1, :]
        dn = dyn * g
        dy = r * (dn - n * jnp.mean(dn * n, axis=-1, keepdims=True))
        dy_ref[...] = dy.astype(BF16)
        red_ref[0:1, :] += jnp.sum(dxo * (n * g), axis=0, keepdims=True)
        red_ref[1:2, :] += jnp.sum(dyn * n, axis=0, keepdims=True)

    return pl.pallas_call(
        body, name=name,
        out_shape=(jax.ShapeDtypeStruct((S, D), BF16), jax.ShapeDtypeStruct((8, D), F32)),
        grid=(S // TS,),
        in_specs=[_row_spec(), _vec_spec(), _vec_spec(), _row_spec()],
        out_specs=(_row_spec(), _vec_spec()),
        compiler_params=_params(("arbitrary",)),
    )(y, gvec, mod, dxo)


def _concat_columns(pieces, name):
    widths = [p.shape[1] for p in pieces]
    offsets = [sum(widths[:k]) for k in range(len(widths))]

    def body(*refs):
        o_ref = refs[-1]
        for ref, off, w in zip(refs[:-1], offsets, widths):
            o_ref[:, off:off + w] = ref[...]

    return pl.pallas_call(
        body, name=name, out_shape=jax.ShapeDtypeStruct((S, sum(widths)), pieces[0].dtype), grid=(S // TS,),
        in_specs=[_row_spec(w) for w in widths], out_specs=_row_spec(sum(widths)),
        compiler_params=_params(("parallel",)),
    )(*pieces)


def _loss_head(xf, target, name):
    def body(x_ref, t_ref, dx_ref, loss_ref):
        i = pl.program_id(0)

        @pl.when(i == 0)
        def _():
            loss_ref[...] = jnp.zeros_like(loss_ref)

        e = x_ref[...] - t_ref[...]
        dx_ref[...] = e / float(D)
        per_tok = jnp.mean(e * e, axis=-1, keepdims=True)
        loss_ref[0:1, 0:1] += 0.5 * jnp.sum(per_tok, axis=0, keepdims=True)

    return pl.pallas_call(
        body, name=name,
        out_shape=(jax.ShapeDtypeStruct((S, D), F32), jax.ShapeDtypeStruct((8, LANE), F32)),
        grid=(S // TS,),
        in_specs=[_row_spec(), _row_spec()],
        out_specs=(_row_spec(), pl.BlockSpec((8, LANE), lambda i: (0, 0))),
        compiler_params=_params(("arbitrary",)),
    )(xf, target)


def _relu2_epilogue(a):
    t = jnp.maximum(a, 0.0)
    return a, t * t


def _relu2_bwd_epilogue(dr, a):
    return (dr * (2.0 * jnp.maximum(a, 0.0)),)


def _merge_epilogue(pc, g0, g1, g2, pa, pb):
    return pc, jax.nn.sigmoid(g0) * pa + jax.nn.sigmoid(g1) * pb + jax.nn.sigmoid(g2) * pc


def _prenorm_prologue(g_row, shift_row, scale_row):
    def prologue(x, gvec, mod):
        y = x * _rms(x) * gvec[g_row:g_row + 1, :]
        return y * (1.0 + mod[scale_row:scale_row + 1, :]) + mod[shift_row:shift_row + 1, :]

    return prologue


def _postnorm_epilogue(g_row, gate_row):
    def epilogue(y, x, gvec, mod):
        yn = y * _rms(y) * gvec[g_row:g_row + 1, :]
        return y, x + mod[gate_row:gate_row + 1, :] * yn

    return epilogue


def _merge_bwd_epilogue(dm, g0, g1, g2, pa, pb, pc):
    sg = [jax.nn.sigmoid(g) for g in (g0, g1, g2)]
    return tuple(dm * s for s in sg) + tuple(dm * p * (s * (1.0 - s)) for p, s in zip((pa, pb, pc), sg))


def _shift_down(x, k, row):
    return jnp.where(row >= k, pltpu.roll(x, k, axis=0), 0.0)


def _shift_up(x, k, row):
    n = x.shape[0]
    return jnp.where(row < n - k, pltpu.roll(x, n - k, axis=0), 0.0)


def _cumsum_rows(x, row, reverse=False):
    shift = _shift_up if reverse else _shift_down
    k = 1
    while k < x.shape[0]:
        x = x + shift(x, k, row)
        k *= 2
    return x


def _full_spec(shape, idx=(0, 0)):
    return pl.BlockSpec(shape, lambda i: idx)


def _pool_window_select(lane, a2, a4, a8, a16):
    return jnp.where(lane < 64, a2, jnp.where(lane < 128, a4, jnp.where(lane < 192, a8, a16)))


def _pool_p(u, row, lane):
    t2 = u + _shift_down(u, 1, row)
    t4 = t2 + _shift_down(t2, 2, row)
    t8 = t4 + _shift_down(t4, 4, row)
    t16 = t8 + _shift_down(t8, 8, row)
    tw = _pool_window_select(lane, t2, t4, t8, t16)
    cnt = jnp.minimum((row + 1).astype(F32), _pool_window_select(lane, 2.0, 4.0, 8.0, 16.0))
    return tw / cnt - u, cnt


def _pool_fwd(z, wp_bd, pscale, name):
    def body(u_ref, w_ref, s_ref, o_ref):
        row = lax.broadcasted_iota(jnp.int32, (S, POOL_W), 0)
        lane = lax.broadcasted_iota(jnp.int32, (S, POOL_W), 1)
        p, _ = _pool_p(u_ref[...], row, lane)
        y = jnp.dot(p.astype(BF16), w_ref[...], preferred_element_type=F32)
        o_ref[...] = y * s_ref[0:1, :]

    return pl.pallas_call(
        body, name=name, out_shape=jax.ShapeDtypeStruct((S, POOL_W), F32), grid=(1,),
        in_specs=[_full_spec((S, POOL_W), (0, Z_PC // POOL_W)), _full_spec((POOL_W, POOL_W)), _full_spec((8, POOL_W))],
        out_specs=_full_spec((S, POOL_W)),
        compiler_params=_params(("arbitrary",)),
    )(z, wp_bd, pscale)


def _pool_bwd(z, wp_bd, pscale, dbr, name):
    def body(u_ref, w_ref, s_ref, dbr_ref, du_ref, dw_ref, red_ref):
        row = lax.broadcasted_iota(jnp.int32, (S, POOL_W), 0)
        lane = lax.broadcasted_iota(jnp.int32, (S, POOL_W), 1)
        p, cnt = _pool_p(u_ref[...], row, lane)
        pb = p.astype(BF16)
        y = jnp.dot(pb, w_ref[...], preferred_element_type=F32)
        dbr = dbr_ref[...]
        red_ref[...] = jnp.zeros_like(red_ref)
        red_ref[0:1, :] = jnp.sum(dbr * y, axis=0, keepdims=True)
        dy = (dbr * s_ref[0:1, :]).astype(BF16)
        dw_ref[...] = lax.dot_general(pb, dy, (((0,), (0,)), ((), ())), preferred_element_type=F32)
        dp = lax.dot_general(dy, w_ref[...], (((1,), (1,)), ((), ())), preferred_element_type=F32)
        g = dp / cnt
        a2 = g + _shift_up(g, 1, row)
        a4 = a2 + _shift_up(a2, 2, row)
        a8 = a4 + _shift_up(a4, 4, row)
        a16 = a8 + _shift_up(a8, 8, row)
        du_ref[...] = (_pool_window_select(lane, a2, a4, a8, a16) - dp).astype(BF16)

    return pl.pallas_call(
        body, name=name,
        out_shape=(jax.ShapeDtypeStruct((S, POOL_W), BF16), jax.ShapeDtypeStruct((POOL_W, POOL_W), F32),
                   jax.ShapeDtypeStruct((8, POOL_W), F32)),
        grid=(1,),
        in_specs=[_full_spec((S, POOL_W), (0, Z_PC // POOL_W)), _full_spec((POOL_W, POOL_W)), _full_spec((8, POOL_W)),
                  _full_spec((S, POOL_W))],
        out_specs=(_full_spec((S, POOL_W)), _full_spec((POOL_W, POOL_W)), _full_spec((8, POOL_W))),
        compiler_params=_params(("arbitrary",)),
    )(z, wp_bd, pscale, dbr)


def _conv_specs():
    base = Z_PC // CONV_W
    return [_full_spec((S, CONV_W), (0, base + 1)), _full_spec((S, CONV_W), (0, base + 2)),
            _full_spec((S, CONV_W), (0, base + 3)), _full_spec((8, CONV_W))]


def _conv_fwd(z, cw, name):
    def body(h_ref, b_ref, c_ref, w_ref, o_ref):
        row = lax.broadcasted_iota(jnp.int32, (S, CONV_W), 0)
        u = c_ref[...] * h_ref[...]
        y = (w_ref[0:1, :] * _shift_down(u, 2, row) + w_ref[1:2, :] * _shift_down(u, 1, row) + w_ref[2:3, :] * u)
        o_ref[...] = b_ref[...] * y

    return pl.pallas_call(
        body, name=name, out_shape=jax.ShapeDtypeStruct((S, CONV_W), F32), grid=(1,),
        in_specs=_conv_specs(), out_specs=_full_spec((S, CONV_W)),
        compiler_params=_params(("arbitrary",)),
    )(z, z, z, cw)


def _conv_bwd(z, cw, dbr, name):
    def body(h_ref, b_ref, c_ref, w_ref, dbr_ref, d_ref, red_ref):
        row = lax.broadcasted_iota(jnp.int32, (S, CONV_W), 0)
        h, cg = h_ref[...], c_ref[...]
        u = cg * h
        u1 = _shift_down(u, 1, row)
        u2 = _shift_down(u, 2, row)
        y = w_ref[0:1, :] * u2 + w_ref[1:2, :] * u1 + w_ref[2:3, :] * u
        dbr = dbr_ref[...]
        dy = dbr * b_ref[...]
        du = w_ref[2:3, :] * dy + w_ref[1:2, :] * _shift_up(dy, 1, row) + w_ref[0:1, :] * _shift_up(dy, 2, row)
        d_ref[:, 0:CONV_W] = (du * cg).astype(BF16)
        d_ref[:, CONV_W:2 * CONV_W] = (dbr * y).astype(BF16)
        d_ref[:, 2 * CONV_W:3 * CONV_W] = (du * h).astype(BF16)
        red_ref[...] = jnp.zeros_like(red_ref)
        red_ref[0:1, :] = jnp.sum(dy * u2, axis=0, keepdims=True)
        red_ref[1:2, :] = jnp.sum(dy * u1, axis=0, keepdims=True)
        red_ref[2:3, :] = jnp.sum(dy * u, axis=0, keepdims=True)

    return pl.pallas_call(
        body, name=name,
        out_shape=(jax.ShapeDtypeStruct((S, 3 * CONV_W), BF16), jax.ShapeDtypeStruct((8, CONV_W), F32)),
        grid=(1,),
        in_specs=_conv_specs() + [_full_spec((S, CONV_W))],
        out_specs=(_full_spec((S, 3 * CONV_W)), _full_spec((8, CONV_W))),
        compiler_params=_params(("arbitrary",)),
    )(z, z, z, cw, dbr)


_NT = (((1,), (1,)), ((), ()))
_TN = (((0,), (0,)), ((), ()))
N_HEAD = 2 * N_PAIR


def _split3(x):
    hi = x.astype(BF16).astype(F32)
    mid = (x - hi).astype(BF16).astype(F32)
    lo = (x - hi - mid).astype(BF16).astype(F32)
    return hi, mid, lo


def _spare(lane, e, k):
    return lane == 64 * (1 - e) + k


def _spare3(lane, e, k):
    base = 64 * (1 - e) + k
    return (lane >= base) & (lane < base + 3)


def _put3(lane, e, k, pieces, rest):
    out = rest
    for n, piece in enumerate(pieces):
        out = jnp.where(_spare(lane, e, k + n), piece, out)
    return out


def _attn_prep(z, bf, name):
    def body(q_ref, k_ref, v_ref, f_ref, b_ref, qa_ref, ka_ref, va_ref, kat_ref, cum_ref):
        p = pl.program_id(0)
        row = lax.broadcasted_iota(jnp.int32, (S, LANE), 0)
        lane = lax.broadcasted_iota(jnp.int32, (S, LANE), 1)

        @pl.when(p == 0)
        def _():
            xv = f_ref[...] + b_ref[0:1, :]
            ls = jnp.minimum(xv, 0.0) - jnp.log(1.0 + jnp.exp(-jnp.abs(xv)))
            cum_ref[...] = _cumsum_rows(jnp.where(lane < N_HEAD, ls, 0.0), row)

        cum = cum_ref[...]
        q, k, v = q_ref[...], k_ref[...], v_ref[...]
        for e in range(2):
            head = (lane >= 64) if e else (lane < 64)
            f = jnp.sum(jnp.where(lane == 2 * p + e, cum, 0.0), axis=1, keepdims=True)
            pieces = _split3(f)
            qa = jnp.where(head, q * ATT_SCALE, _put3(lane, e, 0, pieces, jnp.where(_spare3(lane, e, 3), 1.0, 0.0)))
            ones = jnp.where(_spare3(lane, e, 0) | _spare3(lane, e, 6), 1.0, 0.0)
            ka = jnp.where(head, k, _put3(lane, e, 3, [-x for x in pieces], ones))
            va = jnp.where(head, v, jnp.where(_spare3(lane, e, 0), 1.0, 0.0))
            qa_ref[e] = qa.astype(BF16)
            ka_ref[e] = ka.astype(BF16)
            va_ref[e] = va.astype(BF16)
            kat_ref[e] = ka.T.astype(BF16)

    qb, kb, vb = Z_Q // LANE, Z_K // LANE, Z_V // LANE
    heads = jax.ShapeDtypeStruct((N_HEAD, S, LANE), BF16)
    pair = pl.BlockSpec((2, S, LANE), lambda p: (p, 0, 0))
    return pl.pallas_call(
        body, name=name,
        out_shape=(heads, heads, heads, jax.ShapeDtypeStruct((N_HEAD, LANE, S), BF16)),
        grid=(N_PAIR,),
        in_specs=[pl.BlockSpec((S, LANE), lambda p: (0, qb + p)), pl.BlockSpec((S, LANE), lambda p: (0, kb + p)),
                  pl.BlockSpec((S, LANE), lambda p: (0, vb + p)), pl.BlockSpec((S, LANE), lambda p: (0, Z_F // LANE)),
                  pl.BlockSpec((8, LANE), lambda p: (0, 0))],
        out_specs=(pair, pair, pair, pl.BlockSpec((2, LANE, S), lambda p: (p, 0, 0))),
        scratch_shapes=[pltpu.VMEM((S, LANE), F32)],
        compiler_params=_params(("arbitrary",)),
    )(z, z, z, z, bf)


def _attn_bwd_prep(qa, o, lse, do, name):
    def body(qa_ref, o_ref, lse_ref, do_ref, qa2_ref, doa_ref):
        lane = lax.broadcasted_iota(jnp.int32, (S, LANE), 1)
        dov, ov, lsev = do_ref[...], o_ref[...], lse_ref[...]
        for e in range(2):
            head = (lane >= 64) if e else (lane < 64)
            dsum = jnp.sum(jnp.where(head, dov * ov, 0.0), axis=1, keepdims=True)
            doa_ref[e] = jnp.where(head, dov, _put3(lane, e, 0, [-x for x in _split3(dsum)], 0.0)).astype(BF16)
            lse_col = lsev[:, 64 * e:64 * e + 1]
            qa2_ref[e] = _put3(lane, e, 6, [-x for x in _split3(lse_col)], qa_ref[e].astype(F32)).astype(BF16)

    heads = jax.ShapeDtypeStruct((N_HEAD, S, LANE), BF16)
    pair = pl.BlockSpec((2, S, LANE), lambda p: (p, 0, 0))
    cols = pl.BlockSpec((S, LANE), lambda p: (0, p))
    return pl.pallas_call(
        body, name=name, out_shape=(heads, heads), grid=(N_PAIR,),
        in_specs=[pair, cols, cols, cols], out_specs=(pair, pair),
        compiler_params=_params(("parallel",)),
    )(qa, o, lse, do)


def _attn_bwd_post(z, bf, dqt, dka, dva, name):
    def body(f_ref, b_ref, dqt_ref, dk_ref, dv_ref, dq_out, dk_out, dv_out, dfl_ref, red_ref, dcum_ref):
        p = pl.program_id(0)

        @pl.when(p == 0)
        def _():
            dcum_ref[...] = jnp.zeros_like(dcum_ref)

        row = lax.broadcasted_iota(jnp.int32, (S, LANE), 0)
        lane = lax.broadcasted_iota(jnp.int32, (S, LANE), 1)
        dqa = [dqt_ref[e].T for e in range(2)]
        dq_out[...] = (jnp.where(lane < 64, dqa[0], dqa[1]) * ATT_SCALE).astype(BF16)
        dk_out[...] = jnp.where(lane < 64, dk_ref[0], dk_ref[1]).astype(BF16)
        dv_out[...] = jnp.where(lane < 64, dv_ref[0], dv_ref[1]).astype(BF16)
        for e in range(2):
            d_query = jnp.sum(jnp.where(_spare(lane, e, 0), dqa[e], 0.0), axis=1, keepdims=True)
            d_key = jnp.sum(jnp.where(_spare(lane, e, 3), dk_ref[e], 0.0), axis=1, keepdims=True)
            dcum_ref[...] += jnp.where(lane == 2 * p + e, d_query - d_key, 0.0)

        @pl.when(p == N_PAIR - 1)
        def _():
            dls = _cumsum_rows(dcum_ref[...], row, reverse=True)
            xv = f_ref[...] + b_ref[0:1, :]
            dx = jnp.where(lane < N_HEAD, dls * jax.nn.sigmoid(-xv), 0.0)
            dfl_ref[...] = dx.astype(BF16)
            red_ref[...] = jnp.zeros_like(red_ref)
            red_ref[0:1, :] = jnp.sum(dx, axis=0, keepdims=True)

    wide = jax.ShapeDtypeStruct((S, N_PAIR * LANE), BF16)
    cols = pl.BlockSpec((S, LANE), lambda p: (0, p))
    pair = pl.BlockSpec((2, S, LANE), lambda p: (p, 0, 0))
    return pl.pallas_call(
        body, name=name,
        out_shape=(wide, wide, wide, jax.ShapeDtypeStruct((S, LANE), BF16), jax.ShapeDtypeStruct((8, LANE), F32)),
        grid=(N_PAIR,),
        in_specs=[pl.BlockSpec((S, LANE), lambda p: (0, Z_F // LANE)), pl.BlockSpec((8, LANE), lambda p: (0, 0)),
                  pl.BlockSpec((2, LANE, S), lambda p: (p, 0, 0)), pair, pair],
        out_specs=(cols, cols, cols, pl.BlockSpec((S, LANE), lambda p: (0, 0)), pl.BlockSpec((8, LANE), lambda p: (0, 0))),
        scratch_shapes=[pltpu.VMEM((S, LANE), F32)],
        compiler_params=_params(("arbitrary",)),
    )(z, bf, dqt, dka, dva)


def _attn_fwd(qa, ka, va, name):
    tq, tk = TQ_FWD, TQ
    ratio = tq // tk

    def body(qa_ref, ka_ref, va_ref, o_ref, lse_ref):
        i = pl.program_id(1)
        lane = lax.broadcasted_iota(jnp.int32, (tq, LANE), 1)
        row = lax.broadcasted_iota(jnp.int32, (tq, tk), 0)
        col = lax.broadcasted_iota(jnp.int32, (tq, tk), 1)
        nh = HEADS_PER_STEP_FWD
        qs = [qa_ref[h] for h in range(nh)]

        def block(j, carry, masked):
            off = pl.multiple_of(j * tk, tk)
            out = []
            for h in range(nh):
                m, acc = carry[h]
                s = lax.dot_general(qs[h], ka_ref[h, pl.ds(off, tk), :], _NT, preferred_element_type=F32)
                if masked:
                    s = jnp.where(col + (j - ratio * i) * tk > row, NEG_INF, s)
                mn = jnp.maximum(m, jnp.max(s, axis=1, keepdims=True))
                p = jnp.exp(s - mn).astype(BF16)
                acc = jnp.exp(m - mn) * acc + jnp.dot(p, va_ref[h, pl.ds(off, tk), :], preferred_element_type=F32)
                out.append((mn, acc))
            return tuple(out)

        init = (jnp.full((tq, 1), NEG_INF, F32), jnp.zeros((tq, LANE), F32))
        carry = lax.fori_loop(0, ratio * i, lambda j, c: block(j, c, False), (init,) * nh)
        for d in range(ratio):
            carry = block(ratio * i + d, carry, True)
        res = []
        for h in range(nh):
            m, acc = carry[h]
            l = jnp.sum(jnp.where(_spare(lane, h % 2, 0), acc, 0.0), axis=1, keepdims=True)
            res.append((acc / l, m + jnp.log(l)))
        for g in range(nh // 2):
            o_ref[:, g * LANE:(g + 1) * LANE] = jnp.where(lane < 64, res[2 * g][0], res[2 * g + 1][0])
            lse_ref[:, g * LANE:(g + 1) * LANE] = jnp.where(lane < 64, res[2 * g][1], res[2 * g + 1][1])

    nh = HEADS_PER_STEP_FWD
    out = jax.ShapeDtypeStruct((S, N_PAIR * LANE), F32)
    wide = pl.BlockSpec((tq, 64 * nh), lambda p, i: (i, p))
    return pl.pallas_call(
        body, name=name, out_shape=(out, out), grid=(N_HEAD // nh, S // tq),
        in_specs=[pl.BlockSpec((nh, tq, LANE), lambda p, i: (p, i, 0)), pl.BlockSpec((nh, S, LANE), lambda p, i: (p, 0, 0)),
                  pl.BlockSpec((nh, S, LANE), lambda p, i: (p, 0, 0))],
        out_specs=(wide, wide),
        compiler_params=_params(("parallel", "parallel")),
    )(qa, ka, va)


def _attn_bwd(qa2, ka, va, kat, doa, name):
    nq = S // TQ

    def body(qa_ref, ka_ref, va_ref, kat_ref, doa_ref, dqt_ref, dk_ref, dv_ref):
        j = pl.program_id(1)

        @pl.when(j == 0)
        def _():
            dqt_ref[...] = jnp.zeros_like(dqt_ref)

        key = lax.broadcasted_iota(jnp.int32, (TQ, TQ), 0)
        qry = lax.broadcasted_iota(jnp.int32, (TQ, TQ), 1)
        nh = HEADS_PER_STEP
        kav, vav, katv = ([ref[h] for h in range(nh)] for ref in (ka_ref, va_ref, kat_ref))

        def block(i, carry, masked):
            off = pl.multiple_of(i * TQ, TQ)
            out = []
            for h in range(nh):
                dk_acc, dv_acc = carry[h]
                qav = qa_ref[h, pl.ds(off, TQ), :]
                doav = doa_ref[h, pl.ds(off, TQ), :]
                s_t = lax.dot_general(kav[h], qav, _NT, preferred_element_type=F32)
                if masked:
                    s_t = jnp.where(key > qry, NEG_INF, s_t)
                p_t = jnp.exp(s_t)
                ds_t = p_t * lax.dot_general(vav[h], doav, _NT, preferred_element_type=F32)
                dsb = ds_t.astype(BF16)
                dv_acc = dv_acc + jnp.dot(p_t.astype(BF16), doav, preferred_element_type=F32)
                dk_acc = dk_acc + jnp.dot(dsb, qav, preferred_element_type=F32)
                dqt_ref[h, :, pl.ds(off, TQ)] += jnp.dot(katv[h], dsb, preferred_element_type=F32)
                out.append((dk_acc, dv_acc))
            return tuple(out)

        zero = (jnp.zeros((TQ, LANE), F32), jnp.zeros((TQ, LANE), F32))
        carry = block(j, (zero,) * nh, True)
        carry = lax.fori_loop(j + 1, nq, lambda i, c: block(i, c, False), carry)
        for h in range(nh):
            dk_ref[h], dv_ref[h] = carry[h]

    nh = HEADS_PER_STEP
    full = pl.BlockSpec((nh, S, LANE), lambda p, j: (p, 0, 0))
    blk = pl.BlockSpec((nh, TQ, LANE), lambda p, j: (p, j, 0))
    acc = jax.ShapeDtypeStruct((N_HEAD, S, LANE), F32)
    return pl.pallas_call(
        body, name=name,
        out_shape=(jax.ShapeDtypeStruct((N_HEAD, LANE, S), F32), acc, acc),
        grid=(N_HEAD // nh, nq),
        in_specs=[full, blk, blk, pl.BlockSpec((nh, LANE, TQ), lambda p, j: (p, 0, j)), full],
        out_specs=(pl.BlockSpec((nh, LANE, S), lambda p, j: (p, 0, 0)), blk, blk),
        compiler_params=_params(("arbitrary", "arbitrary")),
    )(qa2, ka, va, kat, doa)


ADA_ROWS = 16


def _ada_fwd(c_pad, w_ada, b_cols, name):
    def body(c_ref, w_ref, b_ref, o_ref):
        cv = c_ref[...]
        sc = (cv * jax.nn.sigmoid(cv)).astype(BF16)
        o_ref[0] = jnp.dot(sc, w_ref[0].astype(BF16), preferred_element_type=F32) + b_ref[0, 0:1, :]

    return pl.pallas_call(
        body, name=name, out_shape=jax.ShapeDtypeStruct((DEPTH, ADA_ROWS, ADA_COLS), F32), grid=(DEPTH,),
        in_specs=[pl.BlockSpec((ADA_ROWS, D), lambda l: (0, 0)), pl.BlockSpec((1, D, ADA_COLS), lambda l: (l, 0, 0)),
                  pl.BlockSpec((1, 8, ADA_COLS), lambda l: (l, 0, 0))],
        out_specs=pl.BlockSpec((1, ADA_ROWS, ADA_COLS), lambda l: (l, 0, 0)),
        compiler_params=_params(("parallel",)),
    )(c_pad, w_ada, b_cols)


def _ada_bwd(c_pad, dmod_cols, name):
    def body(c_ref, d_ref, o_ref):
        cv = c_ref[...]
        sc = (cv * jax.nn.sigmoid(cv)).astype(BF16)
        o_ref[0] = lax.dot_general(sc, d_ref[0].astype(BF16), _TN, preferred_element_type=F32)

    return pl.pallas_call(
        body, name=name, out_shape=jax.ShapeDtypeStruct((DEPTH, D, ADA_COLS), F32), grid=(DEPTH,),
        in_specs=[pl.BlockSpec((ADA_ROWS, D), lambda l: (0, 0)), pl.BlockSpec((1, ADA_ROWS, ADA_COLS), lambda l: (l, 0, 0))],
        out_specs=pl.BlockSpec((1, D, ADA_COLS), lambda l: (l, 0, 0)),
        compiler_params=_params(("parallel",)),
    )(c_pad, dmod_cols)


def _adamw_math(w, g, m, v):
    m = B1 * m + (1.0 - B1) * g
    v = B2 * v + (1.0 - B2) * (g * g)
    m_hat = m / (1.0 - B1 ** STEP)
    v_hat = v / (1.0 - B2 ** STEP)
    delta = -LR * (m_hat / (jnp.sqrt(v_hat) + EPS) + WD * w)
    return delta, m, v


def _row_tile(rows, target=256):
    best = 8
    for t in range(8, min(rows, target) + 1, 8):
        if rows % t == 0:
            best = t
    return best


def _adamw(w, g, m, v, name):
    layers, rows, cols = w.shape
    tr = _row_tile(rows)
    spec = pl.BlockSpec((1, tr, cols), lambda l, i: (l, i, 0))

    def body(w_ref, g_ref, m_ref, v_ref, d_ref, nm_ref, nv_ref):
        d_ref[...], nm_ref[...], nv_ref[...] = _adamw_math(w_ref[...], g_ref[...], m_ref[...], v_ref[...])

    out = jax.ShapeDtypeStruct(w.shape, F32)
    return pl.pallas_call(
        body, name=name, out_shape=(out, out, out), grid=(layers, rows // tr),
        in_specs=[spec] * 4, out_specs=(spec,) * 3, compiler_params=_params(("parallel", "parallel")),
    )(w, g, m, v)


def _sum_slabs(x, name):
    n, rows, _ = x.shape
    tr = _row_tile(rows)

    def body(x_ref, o_ref):
        acc = x_ref[0]
        for k in range(1, n):
            acc = acc + x_ref[k]
        o_ref[...] = acc

    return pl.pallas_call(
        body, name=name, out_shape=jax.ShapeDtypeStruct((rows, D), F32), grid=(rows // tr,),
        in_specs=[pl.BlockSpec((n, tr, D), lambda i: (0, i, 0))], out_specs=pl.BlockSpec((tr, D), lambda i: (i, 0)),
        compiler_params=_params(("parallel",)),
    )(x)


_ANY = pl.BlockSpec(memory_space=pl.ANY)
MESH = pl.DeviceIdType.MESH


def _on_sequencer(body, out_shape, sems, operands, after, sequencer_id, name):
    n = len(operands)

    def ordered_body(*refs):
        body(*refs[:n], *refs[n + 1:])

    extra = [] if after is None else [after]
    return pl.kernel(
        body if after is None else ordered_body, out_type=out_shape,
        mesh=plsc.ScalarSubcoreMesh(axis_name="sequencer", num_cores=1), scratch_types=sems,
        compiler_params=pltpu.CompilerParams(collective_id=sequencer_id), name=name)(*operands, *extra)


def _all_gather(xs, name, sequencer_id=None, after=None):
    n = len(xs)

    def body(*refs):
        x_refs, out_refs = refs[:n], refs[n:2 * n]
        send_sems, recv_sems, local_sems = refs[2 * n:]
        x_, y_, c_ = lax.axis_index("x"), lax.axis_index("y"), lax.axis_index("c")
        me, sibling = (x_, y_, c_), (x_, y_, 1 - c_)
        chips = [(1 - x_, y_), (x_, 1 - y_), (1 - x_, 1 - y_)]
        if sequencer_id is not None:
            barrier = pltpu.get_barrier_semaphore()
            peers = [sibling] + [(*chip, pc) for chip in chips for pc in (c_, 1 - c_)]
            for peer in peers:
                pl.semaphore_signal(barrier, inc=1, device_id=peer, device_id_type=MESH)
            pl.semaphore_wait(barrier, len(peers))

        def slot(a, px, py, pc):
            return out_refs[a].at[4 * px + 2 * py + pc]

        def copy(a, k, block, to, src=None):
            return pltpu.make_async_remote_copy(
                src_ref=slot(a, *block) if src is None else src, dst_ref=slot(a, *block),
                send_sem=send_sems.at[7 * a + k], recv_sem=recv_sems.at[7 * a + k], device_id=to, device_id_type=MESH)

        mine = [pltpu.make_async_copy(x_refs[a], slot(a, *me), local_sems.at[a]) for a in range(n)]
        for cp in mine:
            cp.start()
        first = []
        for a in range(n):
            first.append(copy(a, 0, me, sibling, src=x_refs[a]))
            first += [copy(a, 1 + j, me, (*chip, c_), src=x_refs[a]) for j, chip in enumerate(chips)]
        for cp in first:
            cp.start()
        passed = []
        for j, chip in enumerate(chips):
            for a in range(n):
                copy(a, 1 + j, (*chip, c_), me).wait_recv()
                passed.append(copy(a, 4 + j, (*chip, c_), sibling))
                passed[-1].start()
        for a in range(n):
            copy(a, 0, sibling, me).wait_recv()
        for j, chip in enumerate(chips):
            for a in range(n):
                copy(a, 4 + j, (*chip, 1 - c_), me).wait_recv()
        for cp in first + passed:
            cp.wait_send()
        for cp in mine:
            cp.wait()

    out_shape = [jax.ShapeDtypeStruct((N_DEV,) + x.shape, x.dtype) for x in xs]
    sems = [pltpu.SemaphoreType.DMA((7 * n,)), pltpu.SemaphoreType.DMA((7 * n,)), pltpu.SemaphoreType.DMA((n,))]
    if sequencer_id is not None:
        return _on_sequencer(body, out_shape, sems, xs, after, sequencer_id, name)
    return pl.pallas_call(
        body, name=name, out_shape=out_shape, in_specs=[_ANY] * n, out_specs=[_ANY] * n, scratch_shapes=sems)(*xs)


def _sibling_exchange(gs, name, sequencer_id=None, after=None):
    n = len(gs)

    def body(*refs):
        g_refs, p_refs = refs[:n], refs[n:2 * n]
        send_sems, recv_sems = refs[2 * n:]
        x_, y_, c_ = lax.axis_index("x"), lax.axis_index("y"), lax.axis_index("c")
        if sequencer_id is not None:
            barrier = pltpu.get_barrier_semaphore()
            pl.semaphore_signal(barrier, inc=1, device_id=(x_, y_, 1 - c_), device_id_type=MESH)
            pl.semaphore_wait(barrier, 1)
        copies = [pltpu.make_async_remote_copy(
            src_ref=g_refs[a].at[2 * k + (1 - c_)], dst_ref=p_refs[a].at[k], send_sem=send_sems.at[4 * a + k],
            recv_sem=recv_sems.at[4 * a + k], device_id=(x_, y_, 1 - c_), device_id_type=MESH)
            for a in range(n) for k in range(4)]
        for cp in copies:
            cp.start()
        for cp in copies:
            cp.wait()

    out_shape = [jax.ShapeDtypeStruct((4,) + g.shape[1:], g.dtype) for g in gs]
    sems = [pltpu.SemaphoreType.DMA((4 * n,)), pltpu.SemaphoreType.DMA((4 * n,))]
    if sequencer_id is not None:
        return _on_sequencer(body, out_shape, sems, gs, after, sequencer_id, name)
    return pl.pallas_call(
        body, name=name, out_shape=out_shape, in_specs=[_ANY] * n, out_specs=[_ANY] * n, scratch_shapes=sems)(*gs)


def _slab_tiles(rows, cols):
    if rows % 8 == 0:
        return _row_tile(rows), cols
    return rows, 2 * LANE


def _pair_sums(g, p, route, name):
    _, rows, cols = g.shape
    tr, tc = _slab_tiles(rows, cols)

    def body(route_ref, g_ref, p_ref, t_ref):
        t_ref[...] = (g_ref[...].astype(F32) + p_ref[...].astype(F32)).astype(BF16)

    return pl.pallas_call(
        body, name=name, out_shape=jax.ShapeDtypeStruct((3, rows, cols), BF16),
        grid_spec=pltpu.PrefetchScalarGridSpec(
            num_scalar_prefetch=1, grid=(3, rows // tr, cols // tc),
            in_specs=[pl.BlockSpec((1, tr, tc), lambda r, i, j, route_ref: (2 * route_ref[1 + r] + route_ref[0], i, j)),
                      pl.BlockSpec((1, tr, tc), lambda r, i, j, route_ref: (route_ref[1 + r], i, j))],
            out_specs=pl.BlockSpec((1, tr, tc), lambda r, i, j, route_ref: (r, i, j))),
        compiler_params=_params(("parallel", "parallel", "parallel")),
    )(route, g, p)


def _chip_exchange(ts, name, sequencer_id=None, after=None):
    n = len(ts)

    def body(*refs):
        t_refs, l_refs = refs[:n], refs[n:2 * n]
        send_sems, recv_sems = refs[2 * n:]
        x_, y_, c_ = lax.axis_index("x"), lax.axis_index("y"), lax.axis_index("c")
        chips = [(1 - x_, y_), (x_, 1 - y_), (1 - x_, 1 - y_)]
        if sequencer_id is not None:
            barrier = pltpu.get_barrier_semaphore()
            for px, py in chips:
                pl.semaphore_signal(barrier, inc=1, device_id=(px, py, c_), device_id_type=MESH)
            pl.semaphore_wait(barrier, len(chips))
        copies = [pltpu.make_async_remote_copy(
            src_ref=t_refs[a].at[r], dst_ref=l_refs[a].at[r], send_sem=send_sems.at[3 * a + r],
            recv_sem=recv_sems.at[3 * a + r], device_id=(px, py, c_), device_id_type=MESH)
            for a in range(n) for r, (px, py) in enumerate(chips)]
        for cp in copies:
            cp.start()
        for cp in copies:
            cp.wait()

    out_shape = [jax.ShapeDtypeStruct((3,) + t.shape[1:], t.dtype) for t in ts]
    sems = [pltpu.SemaphoreType.DMA((3 * n,)), pltpu.SemaphoreType.DMA((3 * n,))]
    if sequencer_id is not None:
        return _on_sequencer(body, out_shape, sems, ts, after, sequencer_id, name)
    return pl.pallas_call(
        body, name=name, out_shape=out_shape, in_specs=[_ANY] * n, out_specs=[_ANY] * n, scratch_shapes=sems)(*ts)


def _reduce_adamw(gs, ps, landed, place, w, m, v, name):
    layers, rows, cols = w.shape
    assert layers == DEPTH == 2
    tr, tc = _slab_tiles(rows, cols)
    nr, nc = rows // tr, cols // tc
    spec = pl.BlockSpec((1, tr, tc), lambda l, i, j, place_ref: (l, i, j))

    def own(layer, which):
        pi, pj = (nr - 1, nc - 1) if layer == 0 else (0, 0)

        def index(l, i, j, place_ref):
            lead = 0 if which is None else place_ref[which]
            return lead, jnp.where(l == layer, i, pi), jnp.where(l == layer, j, pj)

        return pl.BlockSpec((3 if which is None else 1, tr, tc), index)

    def body(place_ref, g0_ref, p0_ref, l0_ref, g1_ref, p1_ref, l1_ref, w_ref, m_ref, v_ref,
             g_ref, d_ref, nm_ref, nv_ref):
        def update(own_ref, sib_ref, l_ref):
            g = (own_ref[0].astype(F32) + sib_ref[0].astype(F32) + l_ref[0].astype(F32) + l_ref[1].astype(F32)
                 + l_ref[2].astype(F32))
            g_ref[0] = g
            d_ref[0], nm_ref[0], nv_ref[0] = _adamw_math(w_ref[0], g, m_ref[0], v_ref[0])

        @pl.when(pl.program_id(0) == 0)
        def _():
            update(g0_ref, p0_ref, l0_ref)

        @pl.when(pl.program_id(0) == 1)
        def _():
            update(g1_ref, p1_ref, l1_ref)

    out = jax.ShapeDtypeStruct(w.shape, F32)
    return pl.pallas_call(
        body, name=name, out_shape=(out, out, out, out),
        grid_spec=pltpu.PrefetchScalarGridSpec(
            num_scalar_prefetch=1, grid=(DEPTH, nr, nc),
            in_specs=[own(0, 0), own(0, 1), own(0, None), own(1, 0), own(1, 1), own(1, None), spec, spec, spec],
            out_specs=(spec, spec, spec, spec)),
        compiler_params=_params(("arbitrary", "arbitrary", "arbitrary")),
    )(place, gs[0], ps[0], landed[0], gs[1], ps[1], landed[1], w, m, v)


def _pack(pieces, row_multiple, dtype, cols=D, rows=None):
    flat = jnp.concatenate([p.astype(dtype).reshape(-1) for p in pieces])
    if rows is None:
        rows = -(-flat.shape[0] // cols)
        rows = -(-rows // row_multiple) * row_multiple
    flat = jnp.pad(flat, (0, rows * cols - flat.shape[0]))
    return flat.reshape(rows, cols)


def _unpack(flat, shapes, lead=()):
    out, off = [], 0
    for shp in shapes:
        n = 1
        for s_ in shp:
            n *= s_
        out.append(lax.slice_in_dim(flat, off, off + n, axis=len(lead)).reshape(lead + tuple(shp)))
        off += n
    return out


WIN_STRIDE = 704
WIN_ROWS = 720
Z_TURN = 1544


def _window(wt, me, name):
    padded = jnp.pad(wt, ((0, 0), (0, WIN_ROWS - IN_SHARD), (0, 0)))

    def body(me_ref, x_ref, o_ref):
        o_ref[0] = pltpu.roll(x_ref[0], me_ref[0], axis=0).astype(BF16)

    spec = pl.BlockSpec((1, WIN_ROWS, D), lambda l, me_ref: (l, 0, 0))
    return pl.pallas_call(
        body, name=name, out_shape=jax.ShapeDtypeStruct((DEPTH, WIN_ROWS, D), BF16),
        grid_spec=pltpu.PrefetchScalarGridSpec(num_scalar_prefetch=1, grid=(DEPTH,), in_specs=[spec], out_specs=spec),
        compiler_params=_params(("parallel",)),
    )(me, padded)


def _z_rows_from_windows(win):
    over = WIN_ROWS - WIN_STRIDE
    pieces = [(0, win[0][0:WIN_STRIDE])]
    for d in range(1, N_DEV):
        base = WIN_STRIDE * d
        pieces.append((base, win[d - 1][WIN_STRIDE:WIN_ROWS] + win[d][0:over]))
        pieces.append((base + over, win[d][over:WIN_STRIDE]))
    pieces.append((WIN_STRIDE * N_DEV, win[N_DEV - 1][WIN_STRIDE:WIN_ROWS]))

    def rows(a, b):
        out = []
        for start, arr in pieces:
            lo, hi = max(a, start), min(b, start + arr.shape[0])
            if lo < hi:
                out.append(arr[lo - start:hi - start])
        return out

    pad = jnp.zeros((NZ - IN_COLS, win.shape[-1]), win.dtype)
    return jnp.concatenate(rows(Z_TURN, IN_COLS) + rows(0, Z_TURN) + [pad], axis=0)


def _in_rows_from_z(wt):
    return jnp.concatenate([wt[Z_Q:Z_Q + 1536], wt[Z_F:Z_F + 8], wt[Z_PC:Z_PC + 1024], wt[Z_G:Z_G + 3072]], axis=0)


def _pad_rows(v, rows=8):
    return jnp.pad(v, ((0, rows - v.shape[0]), (0, 0)))


def _layer_fwd(l, x, wts, gvec, mod):
    tag = f"l{l}"
    z, h = _matmul(x, wts["w_in_t"], "nt", f"in_proj_{tag}", tm=1024, tn=1152, prologue=_prenorm_prologue(0, 0, 1),
                   prologue_vecs=[gvec, mod])
    qa, ka, va, kat = _attn_prep(z, wts["b_f"], f"attn_prep_{tag}")
    qa = wts["arrive"](qa)
    o, lse = _attn_fwd(qa, ka, va, f"attn_{tag}")
    br_b = _pool_fwd(z, wts["wp_bd"], wts["pool_scale"], f"pool_{tag}")
    br_c = _conv_fwd(z, wts["conv_w"], f"conv_{tag}")
    pa = _matmul(o, wts["wa"], "nn", f"proj_a_{tag}", out_dtype=BF16)
    pb = _matmul(br_b, wts["wb"], "nn", f"proj_b_{tag}", out_dtype=BF16)
    gates = [(z, Z_G + k * D) for k in range(3)]
    pc, merged = _matmul(br_c, wts["wc"], "nn", f"proj_c_merge_{tag}", tm=1024, tn=512,
                         extra=gates + [(pa, 0), (pb, 0)], epilogue=_merge_epilogue, out_dtypes=(BF16, BF16))
    y, x1 = _matmul(merged, wts["w_out"], "nn", f"out_proj_{tag}", tm=1024, tn=D, extra=[(x, 0)],
                    vec_extra=[gvec, mod], epilogue=_postnorm_epilogue(1, 2), out_dtypes=(F32, F32))
    a, r, h2 = _matmul(x1, wts["w_ff1"], "nn", f"ff1_{tag}", b_col_shards=True, epilogue=_relu2_epilogue,
                       out_dtypes=(BF16, BF16), prologue=_prenorm_prologue(2, 3, 4), prologue_vecs=[gvec, mod])
    y2, x2 = _matmul(r, wts["w_ff2"], "nn", f"ff2_{tag}", tm=1024, tn=D, tk=1024, extra=[(x1, 0)],
                     vec_extra=[gvec, mod], epilogue=_postnorm_epilogue(3, 5), out_dtypes=(F32, F32))
    saved = dict(x=x, h=h, z=z, qa=qa, ka=ka, va=va, kat=kat, o=o, lse=lse, br_b=br_b, br_c=br_c, pa=pa, pb=pb, pc=pc,
                 merged=merged, y=y, x1=x1, h2=h2, a=a, r=r, y2=y2)
    return x2, saved


def _ffn_bwd(l, dx2, sv, wts, gvec, mod, midpoint):
    tag = f"l{l}"
    dy2, red_post_ff = _postnorm_bwd(sv["y2"], gvec, mod, dx2, 3, 5, f"postnorm_ff_bwd_{tag}")
    dy2 = midpoint(dy2)
    da = _matmul(dy2, wts["w_ff2"], "nt", f"ff2_dx_{tag}", extra=[(sv["a"], 0)], epilogue=_relu2_bwd_epilogue,
                 out_dtypes=(BF16,))[0]
    d_w_ff2 = _matmul(sv["r"], dy2, "tn", f"ff2_dw_{tag}", out_dtype=GRAD_DTYPE)
    dh2 = _matmul(da, wts["w_ff1"], "nt", f"ff1_dx_{tag}", b_col_shards=True)
    d_w_ff1 = _matmul(sv["h2"], da, "tn", f"ff1_dw_{tag}", out_dtype=GRAD_DTYPE, out_col_shards=True)
    dx1, red_pre_ff = _prenorm_bwd(sv["x1"], gvec, mod, dh2, dx2, 2, 4, f"prenorm_ff_bwd_{tag}")
    return dx1, [d_w_ff1, d_w_ff2.reshape(N_DEV, D_FF // N_DEV, D)], (red_pre_ff, red_post_ff)


def _mixer_bwd(l, dx1, sv, wts, gvec, mod, ffn_reds, midpoint):
    tag = f"l{l}"
    red_pre_ff, red_post_ff = ffn_reds
    dy, red_post_mix = _postnorm_bwd(sv["y"], gvec, mod, dx1, 1, 2, f"postnorm_mix_bwd_{tag}")
    gates = [(sv["z"], Z_G + k * D) for k in range(3)]
    dpa, dpb, dpc, *dgl = _matmul(dy, wts["w_out"], "nt", f"out_proj_dx_{tag}", tm=1024, tn=512,
                                  extra=gates + [(sv["pa"], 0), (sv["pb"], 0), (sv["pc"], 0)],
                                  epilogue=_merge_bwd_epilogue, out_dtypes=(BF16,) * 6)
    d_w_out = _matmul(sv["merged"], dy, "tn", f"out_proj_dw_{tag}", out_dtype=GRAD_DTYPE)
    dpa = midpoint(dpa)
    do = _matmul(dpa, wts["wa"], "nt", f"proj_a_dx_{tag}")
    dbr_b = _matmul(dpb, wts["wb"], "nt", f"proj_b_dx_{tag}")
    dbr_c = _matmul(dpc, wts["wc"], "nt", f"proj_c_dx_{tag}")
    d_wa = _matmul(sv["o"], dpa, "tn", f"proj_a_dw_{tag}", out_dtype=GRAD_DTYPE)
    d_wb = _matmul(sv["br_b"], dpb, "tn", f"proj_b_dw_{tag}", out_dtype=GRAD_DTYPE)
    d_wc = _matmul(sv["br_c"], dpc, "tn", f"proj_c_dw_{tag}", out_dtype=GRAD_DTYPE)
    d_w_branch = jnp.concatenate([d_wa, d_wb, d_wc], axis=0)

    dpu, d_wp_bd, red_pool = _pool_bwd(sv["z"], wts["wp_bd"], wts["pool_scale"], dbr_b, f"pool_bwd_{tag}")
    dconv, red_conv = _conv_bwd(sv["z"], wts["conv_w"], dbr_c, f"conv_bwd_{tag}")
    qa2, doa = _attn_bwd_prep(sv["qa"], sv["o"], sv["lse"], do, f"attn_bwd_prep_{tag}")
    dqt, dka, dva = _attn_bwd(qa2, sv["ka"], sv["va"], sv["kat"], doa, f"attn_bwd_{tag}")
    dq, dk, dv, dfl, red_f = _attn_bwd_post(sv["z"], wts["b_f"], dqt, dka, dva, f"attn_bwd_post_{tag}")
    dz = _concat_columns([dpu, dconv, *dgl, dq, dk, dv, dfl], f"dz_{tag}")
    dh = _matmul(dz, wts["w_in_t"], "nn", f"in_proj_dx_{tag}", tm=1024, tk=1920)
    d_w_in_t = _matmul(dz, sv["h"], "tn", f"in_proj_dw_{tag}", out_dtype=GRAD_DTYPE, tm=1152)
    dx0, red_pre_mix = _prenorm_bwd(sv["x"], gvec, mod, dh, dx1, 0, 1, f"prenorm_mix_bwd_{tag}")

    rows = D // N_DEV
    big = [_in_rows_from_z(d_w_in_t).reshape(N_DEV, IN_SHARD, D), d_w_branch.reshape(N_DEV, rows, D),
           d_w_out.reshape(N_DEV, rows, D)]
    d_w_pool = jnp.stack([d_wp_bd[64 * g:64 * (g + 1), 64 * g:64 * (g + 1)] for g in range(4)])
    small = dict(
        mod=jnp.stack([red_pre_mix[0], red_pre_mix[1], red_post_mix[0], red_pre_ff[0], red_pre_ff[1], red_post_ff[0]]),
        g_mix_pre=red_pre_mix[2], g_mix_post=red_post_mix[1], g_ff_pre=red_pre_ff[2], g_ff_post=red_post_ff[1],
        b_f=red_f[0, 0:8], w_pool=d_w_pool, pool_scale=red_pool[0], conv_w=red_conv[0:3])
    return dx0, big, small


SMALL_KEYS = ["mod", "g_mix_pre", "g_mix_post", "g_ff_pre", "g_ff_post", "b_f", "w_pool", "pool_scale", "conv_w"]
SMALL_SHAPES = [(DEPTH, 6 * D), (DEPTH, D), (DEPTH, D), (DEPTH, D), (DEPTH, D), (DEPTH, 8), (DEPTH, 4, 64, 64),
                (DEPTH, POOL_W), (DEPTH, 3, CONV_W)]


def kernel(x, c, w_ada, b_ada, g_mix_pre, g_mix_post, g_ff_pre, g_ff_post, w_in, b_f, w_pool, pool_scale, conv_w, w_branch, w_out, w_ff1, w_ff2, loss_target, m_w_ada, m_b_ada, m_g_mix_pre, m_g_mix_post, m_g_ff_pre, m_g_ff_post, m_w_in, m_b_f, m_w_pool, m_pool_scale, m_conv_w, m_w_branch, m_w_out, m_w_ff1, m_w_ff2, v_w_ada, v_b_ada, v_g_mix_pre, v_g_mix_post, v_g_ff_pre, v_g_ff_post, v_w_in, v_b_f, v_w_pool, v_pool_scale, v_conv_w, v_w_branch, v_w_out, v_w_ff1, v_w_ff2):
    ix, iy, ic = lax.axis_index("x"), lax.axis_index("y"), lax.axis_index("c")
    me = 4 * ix + 2 * iy + ic
    route = jnp.stack([ic, 2 * (1 - ix) + iy, 2 * ix + (1 - iy), 2 * (1 - ix) + (1 - iy)]).astype(jnp.int32)
    place = jnp.stack([me, 2 * ix + iy]).astype(jnp.int32)
    wt_in, mt_in, vt_in = (jnp.transpose(a, (0, 2, 1)) for a in (w_in, m_w_in, v_w_in))

    c_all = _all_gather([_pad_rows(c)], "gather_c")[0][:, 0, :]
    c_pad = _pad_rows(c_all, ADA_ROWS)
    b_cols = lax.dynamic_slice_in_dim(b_ada, me * ADA_COLS, ADA_COLS, axis=1)
    b_cols = jnp.broadcast_to(b_cols[:, None, :], (DEPTH, 8, ADA_COLS))
    mod_part = _ada_fwd(c_pad, w_ada, b_cols, "ada_fwd")
    mod_all = _all_gather([mod_part.reshape(DEPTH * ADA_ROWS, ADA_COLS)], "gather_mod")[0]
    mod_all = mod_all.reshape(N_DEV, DEPTH, ADA_ROWS, ADA_COLS)
    mod_mine = lax.dynamic_index_in_dim(mod_all, me, axis=2, keepdims=False)
    mod_mine = jnp.transpose(mod_mine, (1, 0, 2)).reshape(DEPTH, 6, D)

    cw_cols = CONV_W // N_DEV
    cw_send = jnp.pad(conv_w.reshape(DEPTH * 3, cw_cols), ((0, 8 - DEPTH * 3), (0, LANE - cw_cols)))
    win_in = _window(wt_in, place[0:1], "w_in_window")
    send = [[w[l].astype(BF16) for w in (win_in, w_branch, w_out, w_ff1, w_ff2)] for l in range(DEPTH)]
    first = _all_gather(send[0][:1], "gather_weights_l0_in", sequencer_id=1, after=mod_all)
    rest = _all_gather(send[0][1:] + [cw_send], "gather_weights_l0_rest", sequencer_id=2, after=first[0])
    first1 = _all_gather(send[1][:1], "gather_weights_l1_in", sequencer_id=3, after=first[0])
    rest1 = _all_gather(send[1][1:], "gather_weights_l1_rest", sequencer_id=12, after=first[0])
    first, (mt_in, vt_in) = lax.optimization_barrier((first, (mt_in, vt_in)))
    gathered = [first + rest[:4], first1 + rest1]
    cw_all = rest[4][:, :DEPTH * 3, :cw_cols].reshape(N_DEV, DEPTH, 3, cw_cols)

    def first_operands(l, p_in):
        wp_bd = jnp.zeros((POOL_W, POOL_W), F32)
        for g in range(4):
            wp_bd = wp_bd.at[64 * g:64 * (g + 1), 64 * g:64 * (g + 1)].set(w_pool[l, g])
        return dict(w_in_t=_z_rows_from_windows(p_in), wp_bd=wp_bd.astype(BF16),
                    pool_scale=_pad_rows(pool_scale[l][None, :]), b_f=_pad_rows(jnp.pad(b_f[l], (0, LANE - 8))[None, :]))

    def rest_operands(l, rest):
        p_br, p_out, p_ff1, p_ff2 = rest
        w_br_full = p_br.reshape(D, D)
        cw_full = jnp.transpose(cw_all[:, l], (1, 0, 2)).reshape(3, CONV_W)
        return dict(wa=w_br_full[0:A_WIDTH], wb=w_br_full[A_WIDTH:A_WIDTH + POOL_W], wc=w_br_full[A_WIDTH + POOL_W:],
                    w_out=p_out.reshape(D, D), w_ff1=p_ff1, w_ff2=p_ff2.reshape(D_FF, D), conv_w=_pad_rows(cw_full))

    xs = x[0]
    saved, layers = [], []
    for l in range(DEPTH):
        p_in, rest = gathered[l][0], gathered[l][1:5]
        if l > 0:
            xs, p_in = lax.optimization_barrier((xs, p_in))
        wts = first_operands(l, p_in)

        def arrive(t, l=l, rest=rest, wts=wts):
            if l > 0:
                t, rest = lax.optimization_barrier((t, rest))
            wts.update(rest_operands(l, rest))
            return t

        wts["arrive"] = arrive
        gvec = _pad_rows(jnp.stack([g_mix_pre[l], g_mix_post[l], g_ff_pre[l], g_ff_post[l]]))
        layers.append((wts, gvec, _pad_rows(mod_mine[l])))
        xs, sv = _layer_fwd(l, xs, *layers[l])
        saved.append(sv)
    dx, loss_part = _loss_head(xs, loss_target[0], "loss_head")
    small_grads = [None] * DEPTH
    mine, sibs, landed = ({} for _ in range(3))
    seq_id = iter(range(4, 4 + 4 * DEPTH))
    last = [gathered[DEPTH - 1][1]]

    def start(group, grads):
        mine[group] = grads
        sibs[group] = _sibling_exchange(grads, f"rs_sibling_{group}", sequencer_id=next(seq_id), after=last[0])
        last[0] = sibs[group][0]

    def finish(group, later):
        later, (grads, sib) = lax.optimization_barrier((later, (mine[group], sibs[group])))
        sends = [_pair_sums(g, p, route, f"rs_pair_sums_{group}_{k}") for k, (g, p) in enumerate(zip(grads, sib))]
        later, sends = lax.optimization_barrier((later, sends))
        landed[group] = _chip_exchange(sends, f"rs_chips_{group}", sequencer_id=next(seq_id), after=last[0])
        last[0] = landed[group][0]
        return later

    pending = None
    for l in reversed(range(DEPTH)):
        hook = (lambda da: da) if pending is None else functools.partial(finish, pending)
        dx, ffn_grads, ffn_reds = _ffn_bwd(l, dx, saved[l], *layers[l], hook)
        start(f"ffn_l{l}", ffn_grads)
        dx, mix_grads, small_grads[l] = _mixer_bwd(l, dx, saved[l], *layers[l], ffn_reds,
                                                   functools.partial(finish, f"ffn_l{l}"))
        start(f"mix_l{l}", mix_grads)
        pending = f"mix_l{l}"
    grad_x = dx[None]

    big_w = [wt_in, w_branch, w_out, w_ff1, w_ff2]
    big_m = [mt_in, m_w_branch, m_w_out, m_w_ff1, m_w_ff2]
    big_v = [vt_in, v_w_branch, v_w_out, v_w_ff1, v_w_ff2]
    where = [("mix", 0), ("mix", 1), ("mix", 2), ("ffn", 0), ("ffn", 1)]

    def reduce_and_update(k):
        group, at = where[k]
        return _reduce_adamw([mine[f"{group}_l{l}"][at] for l in range(DEPTH)],
                             [sibs[f"{group}_l{l}"][at] for l in range(DEPTH)],
                             [landed[f"{group}_l{l}"][at] for l in range(DEPTH)], place, big_w[k], big_m[k], big_v[k],
                             f"rs_sum_adamw_{k}")

    big_res = {k: list(reduce_and_update(k)) for k in (3, 4)}
    big_res[3][0] = finish(pending, big_res[3][0])

    small = {k: jnp.stack([small_grads[l][k] for l in range(DEPTH)]) for k in SMALL_KEYS}
    payload = _pack([small[k] for k in SMALL_KEYS] + [loss_part[0:1, 0:1]], 8, F32)
    small_all = _all_gather([payload], "gather_small")[0]
    dmod_all = small_all[:, 0:DEPTH * 6, :].reshape(N_DEV, DEPTH, 6 * D)
    summed = _unpack(_sum_slabs(small_all, "sum_small").reshape(-1), SMALL_SHAPES + [(1, 1)])
    sg = dict(zip(SMALL_KEYS, summed))
    loss = summed[-1][0, 0]
    dmod_cols = lax.dynamic_slice_in_dim(dmod_all, me * ADA_COLS, ADA_COLS, axis=2)
    dmod_cols = jnp.pad(jnp.transpose(dmod_cols, (1, 0, 2)), ((0, 0), (0, ADA_ROWS - N_DEV), (0, 0)))
    g_w_ada = _ada_bwd(c_pad, dmod_cols, "ada_bwd")
    g_conv_w = lax.dynamic_slice_in_dim(sg["conv_w"], me * (CONV_W // N_DEV), CONV_W // N_DEV, axis=2)

    ada_out = [g_w_ada] + list(_adamw(w_ada, g_w_ada, m_w_ada, v_w_ada, "adamw_ada"))
    rest_w = [b_ada, g_mix_pre, g_mix_post, g_ff_pre, g_ff_post, b_f, w_pool, pool_scale, conv_w]
    rest_m = [m_b_ada, m_g_mix_pre, m_g_mix_post, m_g_ff_pre, m_g_ff_post, m_b_f, m_w_pool, m_pool_scale, m_conv_w]
    rest_v = [v_b_ada, v_g_mix_pre, v_g_mix_post, v_g_ff_pre, v_g_ff_post, v_b_f, v_w_pool, v_pool_scale, v_conv_w]
    rest_g = [sg["mod"], sg["g_mix_pre"], sg["g_mix_post"], sg["g_ff_pre"], sg["g_ff_post"], sg["b_f"],
              sg["w_pool"], sg["pool_scale"], g_conv_w]
    rest_shapes = [a.shape for a in rest_w]
    upd = _adamw(_pack(rest_w, 8, F32)[None], _pack(rest_g, 8, F32)[None], _pack(rest_m, 8, F32)[None],
                 _pack(rest_v, 8, F32)[None], "adamw_rest")
    rest_out = [rest_g] + [_unpack(arr.reshape(-1), rest_shapes) for arr in upd]
    rest_out = [[ada_out[which]] + rest_out[which] for which in range(4)]

    landed[pending], rest_out = lax.optimization_barrier((landed[pending], rest_out))
    big_res.update({k: reduce_and_update(k) for k in (0, 1, 2)})
    big_out = [[jnp.transpose(big_res[k][which], (0, 2, 1)) if k == 0 else big_res[k][which] for k in range(5)]
               for which in range(4)]

    def ordered(k):
        r, b = rest_out[k], big_out[k]
        return [r[0], r[1], r[2], r[3], r[4], r[5], b[0], r[6], r[7], r[8], r[9], b[1], b[2], b[3], b[4]]

    return (loss, grad_x, *ordered(0), *ordered(1), *ordered(2), *ordered(3))
```

```python
import functools

import jax
import jax.numpy as jnp
from jax import lax
from jax.experimental import pallas as pl
from jax.experimental.pallas import tpu as pltpu
from jax.experimental.pallas import tpu_sc as plsc

F32 = jnp.float32
BF16 = jnp.bfloat16
GRAD_DTYPE = BF16

N_DEV = 8
D = 1024
S = 2048
DEPTH = 2
D_FF = 4 * D
A_WIDTH = 512
HEAD_DIM = 64
N_PAIR = 4
POOL_W = 256
CONV_W = 256
IN_COLS = 5640
ADA_COLS = 6 * D // N_DEV
IN_SHARD = IN_COLS // N_DEV
RMS_EPS = 1e-6
NEG_INF = -1e30
ATT_SCALE = HEAD_DIM ** -0.5

NZ = 5760
Z_PC = 0
Z_G = 1024
Z_Q = 4096
Z_K = 4608
Z_V = 5120
Z_F = 5632

LR, B1, B2, EPS, WD, STEP = 0.001, 0.9, 0.999, 1e-08, 0.01, 10

LANE = 128
VMEM_LIMIT_BYTES = 48 * 1024 * 1024
TS = 512
TQ = 256
TQ_FWD = 512
HEADS_PER_STEP = 8
HEADS_PER_STEP_FWD = 8


def _params(sem=None):
    return pltpu.CompilerParams(dimension_semantics=sem, vmem_limit_bytes=VMEM_LIMIT_BYTES)


def _pick(n, target):
    best = None
    for t in range(LANE, min(n, target) + 1, LANE):
        if n % t == 0:
            best = t
    return n if best is None else best


def _matmul(a, b, mode, name, out_dtype=F32, tm=2048, tn=1024, tk=2048, b_col_shards=False, out_col_shards=False,
            extra=(), vec_extra=(), epilogue=None, out_dtypes=None, n_row_sums=0, prologue=None, prologue_vecs=()):
    if b_col_shards:
        shards, b_rows, shard_cols = b.shape
        b_shape = (b_rows, shards * shard_cols)
    else:
        b_shape = b.shape
    if mode == "nn":
        (m, k), (k2, n) = a.shape, b_shape
    elif mode == "nt":
        (m, k), (n, k2) = a.shape, b_shape
    else:
        (k, m), (k2, n) = a.shape, b_shape
    assert k == k2, (a.shape, b.shape, mode)
    tm, tn, tk = _pick(m, tm), _pick(n, tn), _pick(k, tk)
    if b_col_shards and mode == "nn":
        tn = shard_cols
    per_step = 1
    if b_col_shards and mode == "nt":
        per_step = max(1, min(tk, 1024) // shard_cols)
        tk = per_step * shard_cols
    if out_col_shards:
        tn = n // N_DEV
    nk = k // tk
    if mode == "nn":
        a_spec = pl.BlockSpec((tm, tk), lambda i, j, kk: (i, kk))
        b_spec = (pl.BlockSpec((None, tk, tn), lambda i, j, kk: (j, kk, 0)) if b_col_shards else
                  pl.BlockSpec((tk, tn), lambda i, j, kk: (kk, j)))
        dims = (((1,), (0,)), ((), ()))
    elif mode == "nt":
        a_spec = pl.BlockSpec((tm, tk), lambda i, j, kk: (i, kk))
        b_spec = (pl.BlockSpec((per_step, tn, shard_cols), lambda i, j, kk: (kk, j, 0)) if b_col_shards else
                  pl.BlockSpec((tn, tk), lambda i, j, kk: (j, kk)))
        dims = (((1,), (1,)), ((), ()))
    else:
        assert not b_col_shards
        a_spec = pl.BlockSpec((tk, tm), lambda i, j, kk: (kk, i))
        b_spec = pl.BlockSpec((tk, tn), lambda i, j, kk: (kk, j))
        dims = (((0,), (0,)), ((), ()))
    if out_col_shards:
        out_shape = jax.ShapeDtypeStruct((N_DEV, m, tn), out_dtype)
        out_spec = pl.BlockSpec((None, tm, tn), lambda i, j, kk: (j, i, 0))
    else:
        out_shape = jax.ShapeDtypeStruct((m, n), out_dtype)
        out_spec = pl.BlockSpec((tm, tn), lambda i, j, kk: (i, j))

    n_extra = len(extra) + len(vec_extra)
    extra_specs = [pl.BlockSpec((tm, tn), lambda i, j, kk, off=off: (i, j + off // tn)) for _, off in extra]
    extra_specs += [pl.BlockSpec((8, tn), lambda i, j, kk: (0, j)) for _ in vec_extra]
    if epilogue is not None:
        assert not out_col_shards and all(off % tn == 0 for _, off in extra)
        out_shape = [jax.ShapeDtypeStruct((m, n), dt) for dt in out_dtypes]
        out_spec = [pl.BlockSpec((tm, tn), lambda i, j, kk: (i, j)) for _ in out_dtypes]
        for at in range(len(out_dtypes) - n_row_sums, len(out_dtypes)):
            out_shape[at] = jax.ShapeDtypeStruct((8 * (m // tm), n), out_dtypes[at])
            out_spec[at] = pl.BlockSpec((8, tn), lambda i, j, kk: (i, j))

    def product(a_ref, b_ref):
        if b_col_shards and mode == "nt":
            b_tile = jnp.concatenate([b_ref[s] for s in range(per_step)], axis=1) if per_step > 1 else b_ref[0]
        else:
            b_tile = b_ref[...]
        return lax.dot_general(a_ref[...].astype(BF16), b_tile.astype(BF16), dims, preferred_element_type=F32)

    def write(acc, extra_refs, o_refs):
        if epilogue is None:
            o_refs[0][...] = acc.astype(out_dtype)
        else:
            for o_ref, tile in zip(o_refs, epilogue(acc, *[r[...] for r in extra_refs])):
                o_ref[...] = tile.astype(o_ref.dtype)

    def body_one_pass(a_ref, b_ref, *refs):
        write(product(a_ref, b_ref), refs[:n_extra], refs[n_extra:])

    if prologue is not None:
        assert nk == 1 and mode in ("nn", "nt")
        n_pro = len(prologue_vecs)
        outs = out_shape if isinstance(out_shape, list) else [out_shape]
        out_specs_all = (out_spec if isinstance(out_spec, list) else [out_spec]) + [
            pl.BlockSpec((tm, tk), lambda i, j, kk: (i, 0))]

        def body_prologue(a_ref, b_ref, *refs):
            pro_refs, rest = refs[:n_pro], refs[n_pro:]
            left_out, left_ref = rest[-2], rest[-1]

            @pl.when(pl.program_id(1) == 0)
            def _():
                left = prologue(a_ref[...], *[r[...] for r in pro_refs]).astype(BF16)
                left_ref[...] = left
                left_out[...] = left

            write(product(left_ref, b_ref), rest[:n_extra], rest[n_extra:-2])

        return pl.pallas_call(
            body_prologue, name=name,
            out_shape=outs + [jax.ShapeDtypeStruct((m, k), BF16)],
            grid=(m // tm, n // tn, nk),
            in_specs=[a_spec, b_spec] + [pl.BlockSpec((8, tk), lambda i, j, kk: (0, 0)) for _ in prologue_vecs] + extra_specs,
            out_specs=out_specs_all,
            scratch_shapes=[pltpu.VMEM((tm, tk), BF16)],
            compiler_params=_params(("parallel", "arbitrary", "arbitrary")),
        )(a, b, *prologue_vecs, *[x for x, _ in extra], *vec_extra)

    def body(a_ref, b_ref, *refs):
        acc_ref = refs[-1]
        kk = pl.program_id(2)

        @pl.when(kk == 0)
        def _():
            acc_ref[...] = product(a_ref, b_ref)

        @pl.when(kk > 0)
        def _():
            acc_ref[...] += product(a_ref, b_ref)

        @pl.when(kk == nk - 1)
        def _():
            write(acc_ref[...], refs[:n_extra], refs[n_extra:-1])

    return pl.pallas_call(
        body_one_pass if nk == 1 else body, name=name,
        out_shape=out_shape,
        grid=(m // tm, n // tn, nk),
        in_specs=[a_spec, b_spec] + extra_specs,
        out_specs=out_spec,
        scratch_shapes=[] if nk == 1 else [pltpu.VMEM((tm, tn), F32)],
        compiler_params=_params(("parallel", "parallel", "arbitrary")),
    )(a, b, *[x for x, _ in extra], *vec_extra)


def _row_spec(width=D, col=0):
    return pl.BlockSpec((TS, width), lambda i: (i, col))


def _vec_spec(rows=8, width=D):
    return pl.BlockSpec((rows, width), lambda i: (0, 0))


def _rms(x):
    return lax.rsqrt(jnp.mean(x * x, axis=-1, keepdims=True) + RMS_EPS)


def _postnorm_bwd(y, gvec, mod, dxo, g_row, gate_row, name):
    def body(y_ref, g_ref, mod_ref, dxo_ref, dy_ref, red_ref):
        i = pl.program_id(0)

        @pl.when(i == 0)
        def _():
            red_ref[...] = jnp.zeros_like(red_ref)

        yv = y_ref[...]
        g = g_ref[g_row:g_row + 1, :]
        r = _rms(yv)
        n = yv * r
        dxo = dxo_ref[...]
        dyn = dxo * mod_ref[gate_row:gate_row + 1, :]
        dn = dyn * g
        dy = r * (dn - n * jnp.mean(dn * n, axis=-1, keepdims=True))
        dy_ref[...] = dy.astype(BF16)
        red_ref[0:1, :] += jnp.sum(dxo * (n * g), axis=0, keepdims=True)
        red_ref[1:2, :] += jnp.sum(dyn * n, axis=0, keepdims=True)

    return pl.pallas_call(
        body, name=name,
        out_shape=(jax.ShapeDtypeStruct((S, D), BF16), jax.ShapeDtypeStruct((8, D), F32)),
        grid=(S // TS,),
        in_specs=[_row_spec(), _vec_spec(), _vec_spec(), _row_spec()],
        out_specs=(_row_spec(), _vec_spec()),
        compiler_params=_params(("arbitrary",)),
    )(y, gvec, mod, dxo)


def _concat_columns(pieces, name):
    widths = [p.shape[1] for p in pieces]
    offsets = [sum(widths[:k]) for k in range(len(widths))]

    def body(*refs):
        o_ref = refs[-1]
        for ref, off, w in zip(refs[:-1], offsets, widths):
            o_ref[:, off:off + w] = ref[...]

    return pl.pallas_call(
        body, name=name, out_shape=jax.ShapeDtypeStruct((S, sum(widths)), pieces[0].dtype), grid=(S // TS,),
        in_specs=[_row_spec(w) for w in widths], out_specs=_row_spec(sum(widths)),
        compiler_params=_params(("parallel",)),
    )(*pieces)


def _loss_head(xf, target, name):
    def body(x_ref, t_ref, dx_ref, loss_ref):
        i = pl.program_id(0)

        @pl.when(i == 0)
        def _():
            loss_ref[...] = jnp.zeros_like(loss_ref)

        e = x_ref[...] - t_ref[...]
        dx_ref[...] = e / float(D)
        per_tok = jnp.mean(e * e, axis=-1, keepdims=True)
        loss_ref[0:1, 0:1] += 0.5 * jnp.sum(per_tok, axis=0, keepdims=True)

    return pl.pallas_call(
        body, name=name,
        out_shape=(jax.ShapeDtypeStruct((S, D), F32), jax.ShapeDtypeStruct((8, LANE), F32)),
        grid=(S // TS,),
        in_specs=[_row_spec(), _row_spec()],
        out_specs=(_row_spec(), pl.BlockSpec((8, LANE), lambda i: (0, 0))),
        compiler_params=_params(("arbitrary",)),
    )(xf, target)


def _relu2_epilogue(a):
    t = jnp.maximum(a, 0.0)
    return a, t * t


def _relu2_bwd_epilogue(dr, a):
    return (dr * (2.0 * jnp.maximum(a, 0.0)),)


def _merge_epilogue(pc, g0, g1, g2, pa, pb):
    return pc, jax.nn.sigmoid(g0) * pa + jax.nn.sigmoid(g1) * pb + jax.nn.sigmoid(g2) * pc


def _prenorm_prologue(g_row, shift_row, scale_row):
    def prologue(x, gvec, mod):
        y = x * _rms(x) * gvec[g_row:g_row + 1, :]
        return y * (1.0 + mod[scale_row:scale_row + 1, :]) + mod[shift_row:shift_row + 1, :]

    return prologue


def _rows8(*rows):
    sub = lax.broadcasted_iota(jnp.int32, (8, rows[0].shape[1]), 0)
    out = jnp.zeros((8, rows[0].shape[1]), F32)
    for k, r in enumerate(rows):
        out = jnp.where(sub == k, r, out)
    return out


def _prenorm_bwd_epilogue(g_row, scale_row):
    def epilogue(dh, x, dres, gvec, mod):
        g = gvec[g_row:g_row + 1, :]
        r = _rms(x)
        n = x * r
        dyg = dh * (1.0 + mod[scale_row:scale_row + 1, :])
        dn = dyg * g
        dx = r * (dn - n * jnp.mean(dn * n, axis=-1, keepdims=True))
        sums = _rows8(jnp.sum(dh, axis=0, keepdims=True), jnp.sum(dh * (n * g), axis=0, keepdims=True),
                      jnp.sum(dyg * n, axis=0, keepdims=True))
        return dres + dx, sums

    return epilogue


def _postnorm_epilogue(g_row, gate_row):
    def epilogue(y, x, gvec, mod):
        yn = y * _rms(y) * gvec[g_row:g_row + 1, :]
        return y, x + mod[gate_row:gate_row + 1, :] * yn

    return epilogue


def _merge_bwd_epilogue(dm, g0, g1, g2, pa, pb, pc):
    sg = [jax.nn.sigmoid(g) for g in (g0, g1, g2)]
    return tuple(dm * s for s in sg) + tuple(dm * p * (s * (1.0 - s)) for p, s in zip((pa, pb, pc), sg))


def _shift_down(x, k, row):
    return jnp.where(row >= k, pltpu.roll(x, k, axis=0), 0.0)


def _shift_up(x, k, row):
    n = x.shape[0]
    return jnp.where(row < n - k, pltpu.roll(x, n - k, axis=0), 0.0)


def _cumsum_rows(x, row, reverse=False):
    shift = _shift_up if reverse else _shift_down
    k = 1
    while k < x.shape[0]:
        x = x + shift(x, k, row)
        k *= 2
    return x


def _full_spec(shape, idx=(0, 0)):
    return pl.BlockSpec(shape, lambda i: idx)


def _pool_window_select(lane, a2, a4, a8, a16):
    return jnp.where(lane < 64, a2, jnp.where(lane < 128, a4, jnp.where(lane < 192, a8, a16)))


def _pool_p(u, row, lane):
    t2 = u + _shift_down(u, 1, row)
    t4 = t2 + _shift_down(t2, 2, row)
    t8 = t4 + _shift_down(t4, 4, row)
    t16 = t8 + _shift_down(t8, 8, row)
    tw = _pool_window_select(lane, t2, t4, t8, t16)
    cnt = jnp.minimum((row + 1).astype(F32), _pool_window_select(lane, 2.0, 4.0, 8.0, 16.0))
    return tw / cnt - u, cnt


def _pool_fwd(z, wp_bd, pscale, name):
    def body(u_ref, w_ref, s_ref, o_ref):
        row = lax.broadcasted_iota(jnp.int32, (S, POOL_W), 0)
        lane = lax.broadcasted_iota(jnp.int32, (S, POOL_W), 1)
        p, _ = _pool_p(u_ref[...], row, lane)
        y = jnp.dot(p.astype(BF16), w_ref[...], preferred_element_type=F32)
        o_ref[...] = y * s_ref[0:1, :]

    return pl.pallas_call(
        body, name=name, out_shape=jax.ShapeDtypeStruct((S, POOL_W), F32), grid=(1,),
        in_specs=[_full_spec((S, POOL_W), (0, Z_PC // POOL_W)), _full_spec((POOL_W, POOL_W)), _full_spec((8, POOL_W))],
        out_specs=_full_spec((S, POOL_W)),
        compiler_params=_params(("arbitrary",)),
    )(z, wp_bd, pscale)


def _pool_bwd(z, wp_bd, pscale, dbr, name):
    def body(u_ref, w_ref, s_ref, dbr_ref, du_ref, dw_ref, red_ref):
        row = lax.broadcasted_iota(jnp.int32, (S, POOL_W), 0)
        lane = lax.broadcasted_iota(jnp.int32, (S, POOL_W), 1)
        p, cnt = _pool_p(u_ref[...], row, lane)
        pb = p.astype(BF16)
        y = jnp.dot(pb, w_ref[...], preferred_element_type=F32)
        dbr = dbr_ref[...]
        red_ref[...] = jnp.zeros_like(red_ref)
        red_ref[0:1, :] = jnp.sum(dbr * y, axis=0, keepdims=True)
        dy = (dbr * s_ref[0:1, :]).astype(BF16)
        dw_ref[...] = lax.dot_general(pb, dy, (((0,), (0,)), ((), ())), preferred_element_type=F32)
        dp = lax.dot_general(dy, w_ref[...], (((1,), (1,)), ((), ())), preferred_element_type=F32)
        g = dp / cnt
        a2 = g + _shift_up(g, 1, row)
        a4 = a2 + _shift_up(a2, 2, row)
        a8 = a4 + _shift_up(a4, 4, row)
        a16 = a8 + _shift_up(a8, 8, row)
        du_ref[...] = (_pool_window_select(lane, a2, a4, a8, a16) - dp).astype(BF16)

    return pl.pallas_call(
        body, name=name,
        out_shape=(jax.ShapeDtypeStruct((S, POOL_W), BF16), jax.ShapeDtypeStruct((POOL_W, POOL_W), F32),
                   jax.ShapeDtypeStruct((8, POOL_W), F32)),
        grid=(1,),
        in_specs=[_full_spec((S, POOL_W), (0, Z_PC // POOL_W)), _full_spec((POOL_W, POOL_W)), _full_spec((8, POOL_W)),
                  _full_spec((S, POOL_W))],
        out_specs=(_full_spec((S, POOL_W)), _full_spec((POOL_W, POOL_W)), _full_spec((8, POOL_W))),
        compiler_params=_params(("arbitrary",)),
    )(z, wp_bd, pscale, dbr)


def _conv_specs():
    base = Z_PC // CONV_W
    return [_full_spec((S, CONV_W), (0, base + 1)), _full_spec((S, CONV_W), (0, base + 2)),
            _full_spec((S, CONV_W), (0, base + 3)), _full_spec((8, CONV_W))]


def _conv_fwd(z, cw, name):
    def body(h_ref, b_ref, c_ref, w_ref, o_ref):
        row = lax.broadcasted_iota(jnp.int32, (S, CONV_W), 0)
        u = c_ref[...] * h_ref[...]
        y = (w_ref[0:1, :] * _shift_down(u, 2, row) + w_ref[1:2, :] * _shift_down(u, 1, row) + w_ref[2:3, :] * u)
        o_ref[...] = b_ref[...] * y

    return pl.pallas_call(
        body, name=name, out_shape=jax.ShapeDtypeStruct((S, CONV_W), F32), grid=(1,),
        in_specs=_conv_specs(), out_specs=_full_spec((S, CONV_W)),
        compiler_params=_params(("arbitrary",)),
    )(z, z, z, cw)


def _conv_bwd(z, cw, dbr, name):
    def body(h_ref, b_ref, c_ref, w_ref, dbr_ref, d_ref, red_ref):
        row = lax.broadcasted_iota(jnp.int32, (S, CONV_W), 0)
        h, cg = h_ref[...], c_ref[...]
        u = cg * h
        u1 = _shift_down(u, 1, row)
        u2 = _shift_down(u, 2, row)
        y = w_ref[0:1, :] * u2 + w_ref[1:2, :] * u1 + w_ref[2:3, :] * u
        dbr = dbr_ref[...]
        dy = dbr * b_ref[...]
        du = w_ref[2:3, :] * dy + w_ref[1:2, :] * _shift_up(dy, 1, row) + w_ref[0:1, :] * _shift_up(dy, 2, row)
        d_ref[:, 0:CONV_W] = (du * cg).astype(BF16)
        d_ref[:, CONV_W:2 * CONV_W] = (dbr * y).astype(BF16)
        d_ref[:, 2 * CONV_W:3 * CONV_W] = (du * h).astype(BF16)
        red_ref[...] = jnp.zeros_like(red_ref)
        red_ref[0:1, :] = jnp.sum(dy * u2, axis=0, keepdims=True)
        red_ref[1:2, :] = jnp.sum(dy * u1, axis=0, keepdims=True)
        red_ref[2:3, :] = jnp.sum(dy * u, axis=0, keepdims=True)

    return pl.pallas_call(
        body, name=name,
        out_shape=(jax.ShapeDtypeStruct((S, 3 * CONV_W), BF16), jax.ShapeDtypeStruct((8, CONV_W), F32)),
        grid=(1,),
        in_specs=_conv_specs() + [_full_spec((S, CONV_W))],
        out_specs=(_full_spec((S, 3 * CONV_W)), _full_spec((8, CONV_W))),
        compiler_params=_params(("arbitrary",)),
    )(z, z, z, cw, dbr)


_NT = (((1,), (1,)), ((), ()))
_TN = (((0,), (0,)), ((), ()))
N_HEAD = 2 * N_PAIR


def _split3(x):
    hi = x.astype(BF16).astype(F32)
    mid = (x - hi).astype(BF16).astype(F32)
    lo = (x - hi - mid).astype(BF16).astype(F32)
    return hi, mid, lo


def _spare(lane, e, k):
    return lane == 64 * (1 - e) + k


def _spare3(lane, e, k):
    base = 64 * (1 - e) + k
    return (lane >= base) & (lane < base + 3)


def _put3(lane, e, k, pieces, rest):
    out = rest
    for n, piece in enumerate(pieces):
        out = jnp.where(_spare(lane, e, k + n), piece, out)
    return out


def _attn_prep(z, bf, name):
    def body(q_ref, k_ref, v_ref, f_ref, b_ref, qa_ref, ka_ref, va_ref, kat_ref, cum_ref):
        p = pl.program_id(0)
        row = lax.broadcasted_iota(jnp.int32, (S, LANE), 0)
        lane = lax.broadcasted_iota(jnp.int32, (S, LANE), 1)

        @pl.when(p == 0)
        def _():
            xv = f_ref[...] + b_ref[0:1, :]
            ls = jnp.minimum(xv, 0.0) - jnp.log(1.0 + jnp.exp(-jnp.abs(xv)))
            cum_ref[...] = _cumsum_rows(jnp.where(lane < N_HEAD, ls, 0.0), row)

        cum = cum_ref[...]
        q, k, v = q_ref[...], k_ref[...], v_ref[...]
        for e in range(2):
            head = (lane >= 64) if e else (lane < 64)
            f = jnp.sum(jnp.where(lane == 2 * p + e, cum, 0.0), axis=1, keepdims=True)
            pieces = _split3(f)
            qa = jnp.where(head, q * ATT_SCALE, _put3(lane, e, 0, pieces, jnp.where(_spare3(lane, e, 3), 1.0, 0.0)))
            ones = jnp.where(_spare3(lane, e, 0) | _spare3(lane, e, 6), 1.0, 0.0)
            ka = jnp.where(head, k, _put3(lane, e, 3, [-x for x in pieces], ones))
            va = jnp.where(head, v, jnp.where(_spare3(lane, e, 0), 1.0, 0.0))
            qa_ref[e] = qa.astype(BF16)
            ka_ref[e] = ka.astype(BF16)
            va_ref[e] = va.astype(BF16)
            kat_ref[e] = ka.T.astype(BF16)

    qb, kb, vb = Z_Q // LANE, Z_K // LANE, Z_V // LANE
    heads = jax.ShapeDtypeStruct((N_HEAD, S, LANE), BF16)
    pair = pl.BlockSpec((2, S, LANE), lambda p: (p, 0, 0))
    return pl.pallas_call(
        body, name=name,
        out_shape=(heads, heads, heads, jax.ShapeDtypeStruct((N_HEAD, LANE, S), BF16)),
        grid=(N_PAIR,),
        in_specs=[pl.BlockSpec((S, LANE), lambda p: (0, qb + p)), pl.BlockSpec((S, LANE), lambda p: (0, kb + p)),
                  pl.BlockSpec((S, LANE), lambda p: (0, vb + p)), pl.BlockSpec((S, LANE), lambda p: (0, Z_F // LANE)),
                  pl.BlockSpec((8, LANE), lambda p: (0, 0))],
        out_specs=(pair, pair, pair, pl.BlockSpec((2, LANE, S), lambda p: (p, 0, 0))),
        scratch_shapes=[pltpu.VMEM((S, LANE), F32)],
        compiler_params=_params(("arbitrary",)),
    )(z, z, z, z, bf)


def _attn_bwd_prep(qa, o, lse, do, name):
    def body(qa_ref, o_ref, lse_ref, do_ref, qa2_ref, doa_ref):
        lane = lax.broadcasted_iota(jnp.int32, (S, LANE), 1)
        dov, ov, lsev = do_ref[...], o_ref[...], lse_ref[...]
        for e in range(2):
            head = (lane >= 64) if e else (lane < 64)
            dsum = jnp.sum(jnp.where(head, dov * ov, 0.0), axis=1, keepdims=True)
            doa_ref[e] = jnp.where(head, dov, _put3(lane, e, 0, [-x for x in _split3(dsum)], 0.0)).astype(BF16)
            lse_col = lsev[:, 64 * e:64 * e + 1]
            qa2_ref[e] = _put3(lane, e, 6, [-x for x in _split3(lse_col)], qa_ref[e].astype(F32)).astype(BF16)

    heads = jax.ShapeDtypeStruct((N_HEAD, S, LANE), BF16)
    pair = pl.BlockSpec((2, S, LANE), lambda p: (p, 0, 0))
    cols = pl.BlockSpec((S, LANE), lambda p: (0, p))
    return pl.pallas_call(
        body, name=name, out_shape=(heads, heads), grid=(N_PAIR,),
        in_specs=[pair, cols, cols, cols], out_specs=(pair, pair),
        compiler_params=_params(("parallel",)),
    )(qa, o, lse, do)


def _attn_bwd_post(z, bf, dqt, dka, dva, name):
    def body(f_ref, b_ref, dqt_ref, dk_ref, dv_ref, dq_out, dk_out, dv_out, dfl_ref, red_ref, dcum_ref):
        p = pl.program_id(0)

        @pl.when(p == 0)
        def _():
            dcum_ref[...] = jnp.zeros_like(dcum_ref)

        row = lax.broadcasted_iota(jnp.int32, (S, LANE), 0)
        lane = lax.broadcasted_iota(jnp.int32, (S, LANE), 1)
        dqa = [dqt_ref[e].T for e in range(2)]
        dq_out[...] = (jnp.where(lane < 64, dqa[0], dqa[1]) * ATT_SCALE).astype(BF16)
        dk_out[...] = jnp.where(lane < 64, dk_ref[0], dk_ref[1]).astype(BF16)
        dv_out[...] = jnp.where(lane < 64, dv_ref[0], dv_ref[1]).astype(BF16)
        for e in range(2):
            d_query = jnp.sum(jnp.where(_spare(lane, e, 0), dqa[e], 0.0), axis=1, keepdims=True)
            d_key = jnp.sum(jnp.where(_spare(lane, e, 3), dk_ref[e], 0.0), axis=1, keepdims=True)
            dcum_ref[...] += jnp.where(lane == 2 * p + e, d_query - d_key, 0.0)

        @pl.when(p == N_PAIR - 1)
        def _():
            dls = _cumsum_rows(dcum_ref[...], row, reverse=True)
            xv = f_ref[...] + b_ref[0:1, :]
            dx = jnp.where(lane < N_HEAD, dls * jax.nn.sigmoid(-xv), 0.0)
            dfl_ref[...] = dx.astype(BF16)
            red_ref[...] = jnp.zeros_like(red_ref)
            red_ref[0:1, :] = jnp.sum(dx, axis=0, keepdims=True)

    wide = jax.ShapeDtypeStruct((S, N_PAIR * LANE), BF16)
    cols = pl.BlockSpec((S, LANE), lambda p: (0, p))
    pair = pl.BlockSpec((2, S, LANE), lambda p: (p, 0, 0))
    return pl.pallas_call(
        body, name=name,
        out_shape=(wide, wide, wide, jax.ShapeDtypeStruct((S, LANE), BF16), jax.ShapeDtypeStruct((8, LANE), F32)),
        grid=(N_PAIR,),
        in_specs=[pl.BlockSpec((S, LANE), lambda p: (0, Z_F // LANE)), pl.BlockSpec((8, LANE), lambda p: (0, 0)),
                  pl.BlockSpec((2, LANE, S), lambda p: (p, 0, 0)), pair, pair],
        out_specs=(cols, cols, cols, pl.BlockSpec((S, LANE), lambda p: (0, 0)), pl.BlockSpec((8, LANE), lambda p: (0, 0))),
        scratch_shapes=[pltpu.VMEM((S, LANE), F32)],
        compiler_params=_params(("arbitrary",)),
    )(z, bf, dqt, dka, dva)


def _attn_fwd(qa, ka, va, name):
    tq, tk = TQ_FWD, TQ
    ratio = tq // tk

    def body(qa_ref, ka_ref, va_ref, o_ref, lse_ref):
        i = pl.program_id(1)
        lane = lax.broadcasted_iota(jnp.int32, (tq, LANE), 1)
        row = lax.broadcasted_iota(jnp.int32, (tq, tk), 0)
        col = lax.broadcasted_iota(jnp.int32, (tq, tk), 1)
        nh = HEADS_PER_STEP_FWD
        qs = [qa_ref[h] for h in range(nh)]

        def block(j, carry, masked):
            off = pl.multiple_of(j * tk, tk)
            out = []
            for h in range(nh):
                m, acc = carry[h]
                s = lax.dot_general(qs[h], ka_ref[h, pl.ds(off, tk), :], _NT, preferred_element_type=F32)
                if masked:
                    s = jnp.where(col + (j - ratio * i) * tk > row, NEG_INF, s)
                mn = jnp.maximum(m, jnp.max(s, axis=1, keepdims=True))
                p = jnp.exp(s - mn).astype(BF16)
                acc = jnp.exp(m - mn) * acc + jnp.dot(p, va_ref[h, pl.ds(off, tk), :], preferred_element_type=F32)
                out.append((mn, acc))
            return tuple(out)

        init = (jnp.full((tq, 1), NEG_INF, F32), jnp.zeros((tq, LANE), F32))
        carry = lax.fori_loop(0, ratio * i, lambda j, c: block(j, c, False), (init,) * nh)
        for d in range(ratio):
            carry = block(ratio * i + d, carry, True)
        res = []
        for h in range(nh):
            m, acc = carry[h]
            l = jnp.sum(jnp.where(_spare(lane, h % 2, 0), acc, 0.0), axis=1, keepdims=True)
            res.append((acc / l, m + jnp.log(l)))
        for g in range(nh // 2):
            o_ref[:, g * LANE:(g + 1) * LANE] = jnp.where(lane < 64, res[2 * g][0], res[2 * g + 1][0])
            lse_ref[:, g * LANE:(g + 1) * LANE] = jnp.where(lane < 64, res[2 * g][1], res[2 * g + 1][1])

    nh = HEADS_PER_STEP_FWD
    out = jax.ShapeDtypeStruct((S, N_PAIR * LANE), F32)
    wide = pl.BlockSpec((tq, 64 * nh), lambda p, i: (i, p))
    return pl.pallas_call(
        body, name=name, out_shape=(out, out), grid=(N_HEAD // nh, S // tq),
        in_specs=[pl.BlockSpec((nh, tq, LANE), lambda p, i: (p, i, 0)), pl.BlockSpec((nh, S, LANE), lambda p, i: (p, 0, 0)),
                  pl.BlockSpec((nh, S, LANE), lambda p, i: (p, 0, 0))],
        out_specs=(wide, wide),
        compiler_params=_params(("parallel", "parallel")),
    )(qa, ka, va)


def _attn_bwd(qa2, ka, va, kat, doa, name):
    nq = S // TQ

    def body(qa_ref, ka_ref, va_ref, kat_ref, doa_ref, dqt_ref, dk_ref, dv_ref):
        j = pl.program_id(1)

        @pl.when(j == 0)
        def _():
            dqt_ref[...] = jnp.zeros_like(dqt_ref)

        key = lax.broadcasted_iota(jnp.int32, (TQ, TQ), 0)
        qry = lax.broadcasted_iota(jnp.int32, (TQ, TQ), 1)
        nh = HEADS_PER_STEP
        kav, vav, katv = ([ref[h] for h in range(nh)] for ref in (ka_ref, va_ref, kat_ref))

        def block(i, carry, masked):
            off = pl.multiple_of(i * TQ, TQ)
            out = []
            for h in range(nh):
                dk_acc, dv_acc = carry[h]
                qav = qa_ref[h, pl.ds(off, TQ), :]
                doav = doa_ref[h, pl.ds(off, TQ), :]
                s_t = lax.dot_general(kav[h], qav, _NT, preferred_element_type=F32)
                if masked:
                    s_t = jnp.where(key > qry, NEG_INF, s_t)
                p_t = jnp.exp(s_t)
                ds_t = p_t * lax.dot_general(vav[h], doav, _NT, preferred_element_type=F32)
                dsb = ds_t.astype(BF16)
                dv_acc = dv_acc + jnp.dot(p_t.astype(BF16), doav, preferred_element_type=F32)
                dk_acc = dk_acc + jnp.dot(dsb, qav, preferred_element_type=F32)
                dqt_ref[h, :, pl.ds(off, TQ)] += jnp.dot(katv[h], dsb, preferred_element_type=F32)
                out.append((dk_acc, dv_acc))
            return tuple(out)

        zero = (jnp.zeros((TQ, LANE), F32), jnp.zeros((TQ, LANE), F32))
        carry = block(j, (zero,) * nh, True)
        carry = lax.fori_loop(j + 1, nq, lambda i, c: block(i, c, False), carry)
        for h in range(nh):
            dk_ref[h], dv_ref[h] = carry[h]

    nh = HEADS_PER_STEP
    full = pl.BlockSpec((nh, S, LANE), lambda p, j: (p, 0, 0))
    blk = pl.BlockSpec((nh, TQ, LANE), lambda p, j: (p, j, 0))
    acc = jax.ShapeDtypeStruct((N_HEAD, S, LANE), F32)
    return pl.pallas_call(
        body, name=name,
        out_shape=(jax.ShapeDtypeStruct((N_HEAD, LANE, S), F32), acc, acc),
        grid=(N_HEAD // nh, nq),
        in_specs=[full, blk, blk, pl.BlockSpec((nh, LANE, TQ), lambda p, j: (p, 0, j)), full],
        out_specs=(pl.BlockSpec((nh, LANE, S), lambda p, j: (p, 0, 0)), blk, blk),
        compiler_params=_params(("arbitrary", "arbitrary")),
    )(qa2, ka, va, kat, doa)


ADA_ROWS = 16


def _ada_fwd(c_pad, w_ada, b_cols, name):
    def body(c_ref, w_ref, b_ref, o_ref):
        cv = c_ref[...]
        sc = (cv * jax.nn.sigmoid(cv)).astype(BF16)
        o_ref[0] = jnp.dot(sc, w_ref[0].astype(BF16), preferred_element_type=F32) + b_ref[0, 0:1, :]

    return pl.pallas_call(
        body, name=name, out_shape=jax.ShapeDtypeStruct((DEPTH, ADA_ROWS, ADA_COLS), F32), grid=(DEPTH,),
        in_specs=[pl.BlockSpec((ADA_ROWS, D), lambda l: (0, 0)), pl.BlockSpec((1, D, ADA_COLS), lambda l: (l, 0, 0)),
                  pl.BlockSpec((1, 8, ADA_COLS), lambda l: (l, 0, 0))],
        out_specs=pl.BlockSpec((1, ADA_ROWS, ADA_COLS), lambda l: (l, 0, 0)),
        compiler_params=_params(("parallel",)),
    )(c_pad, w_ada, b_cols)


def _ada_bwd(c_pad, dmod_cols, name):
    def body(c_ref, d_ref, o_ref):
        cv = c_ref[...]
        sc = (cv * jax.nn.sigmoid(cv)).astype(BF16)
        o_ref[0] = lax.dot_general(sc, d_ref[0].astype(BF16), _TN, preferred_element_type=F32)

    return pl.pallas_call(
        body, name=name, out_shape=jax.ShapeDtypeStruct((DEPTH, D, ADA_COLS), F32), grid=(DEPTH,),
        in_specs=[pl.BlockSpec((ADA_ROWS, D), lambda l: (0, 0)), pl.BlockSpec((1, ADA_ROWS, ADA_COLS), lambda l: (l, 0, 0))],
        out_specs=pl.BlockSpec((1, D, ADA_COLS), lambda l: (l, 0, 0)),
        compiler_params=_params(("parallel",)),
    )(c_pad, dmod_cols)


def _adamw_math(w, g, m, v):
    m = B1 * m + (1.0 - B1) * g
    v = B2 * v + (1.0 - B2) * (g * g)
    m_hat = m / (1.0 - B1 ** STEP)
    v_hat = v / (1.0 - B2 ** STEP)
    delta = -LR * (m_hat / (jnp.sqrt(v_hat) + EPS) + WD * w)
    return delta, m, v


def _row_tile(rows, target=256):
    best = 8
    for t in range(8, min(rows, target) + 1, 8):
        if rows % t == 0:
            best = t
    return best


def _adamw(w, g, m, v, name):
    layers, rows, cols = w.shape
    tr = _row_tile(rows)
    spec = pl.BlockSpec((1, tr, cols), lambda l, i: (l, i, 0))

    def body(w_ref, g_ref, m_ref, v_ref, d_ref, nm_ref, nv_ref):
        d_ref[...], nm_ref[...], nv_ref[...] = _adamw_math(w_ref[...], g_ref[...], m_ref[...], v_ref[...])

    out = jax.ShapeDtypeStruct(w.shape, F32)
    return pl.pallas_call(
        body, name=name, out_shape=(out, out, out), grid=(layers, rows // tr),
        in_specs=[spec] * 4, out_specs=(spec,) * 3, compiler_params=_params(("parallel", "parallel")),
    )(w, g, m, v)


def _sum_slabs(x, name):
    n, rows, _ = x.shape
    tr = _row_tile(rows)

    def body(x_ref, o_ref):
        acc = x_ref[0]
        for k in range(1, n):
            acc = acc + x_ref[k]
        o_ref[...] = acc

    return pl.pallas_call(
        body, name=name, out_shape=jax.ShapeDtypeStruct((rows, D), F32), grid=(rows // tr,),
        in_specs=[pl.BlockSpec((n, tr, D), lambda i: (0, i, 0))], out_specs=pl.BlockSpec((tr, D), lambda i: (i, 0)),
        compiler_params=_params(("parallel",)),
    )(x)


_ANY = pl.BlockSpec(memory_space=pl.ANY)
MESH = pl.DeviceIdType.MESH


def _on_sequencer(body, out_shape, sems, operands, after, sequencer_id, name):
    n = len(operands)

    def ordered_body(*refs):
        body(*refs[:n], *refs[n + 1:])

    extra = [] if after is None else [after]
    return pl.kernel(
        body if after is None else ordered_body, out_type=out_shape,
        mesh=plsc.ScalarSubcoreMesh(axis_name="sequencer", num_cores=1), scratch_types=sems,
        compiler_params=pltpu.CompilerParams(collective_id=sequencer_id), name=name)(*operands, *extra)


def _all_gather(xs, name, sequencer_id=None, after=None):
    n = len(xs)

    def body(*refs):
        x_refs, out_refs = refs[:n], refs[n:2 * n]
        send_sems, recv_sems, local_sems = refs[2 * n:]
        x_, y_, c_ = lax.axis_index("x"), lax.axis_index("y"), lax.axis_index("c")
        me, sibling = (x_, y_, c_), (x_, y_, 1 - c_)
        chips = [(1 - x_, y_), (x_, 1 - y_), (1 - x_, 1 - y_)]
        if sequencer_id is not None:
            barrier = pltpu.get_barrier_semaphore()
            peers = [sibling] + [(*chip, pc) for chip in chips for pc in (c_, 1 - c_)]
            for peer in peers:
                pl.semaphore_signal(barrier, inc=1, device_id=peer, device_id_type=MESH)
            pl.semaphore_wait(barrier, len(peers))

        def slot(a, px, py, pc):
            return out_refs[a].at[4 * px + 2 * py + pc]

        def copy(a, k, block, to, src=None):
            return pltpu.make_async_remote_copy(
                src_ref=slot(a, *block) if src is None else src, dst_ref=slot(a, *block),
                send_sem=send_sems.at[7 * a + k], recv_sem=recv_sems.at[7 * a + k], device_id=to, device_id_type=MESH)

        mine = [pltpu.make_async_copy(x_refs[a], slot(a, *me), local_sems.at[a]) for a in range(n)]
        for cp in mine:
            cp.start()
        first = []
        for a in range(n):
            first.append(copy(a, 0, me, sibling, src=x_refs[a]))
            first += [copy(a, 1 + j, me, (*chip, c_), src=x_refs[a]) for j, chip in enumerate(chips)]
        for cp in first:
            cp.start()
        passed = []
        for j, chip in enumerate(chips):
            for a in range(n):
                copy(a, 1 + j, (*chip, c_), me).wait_recv()
                passed.append(copy(a, 4 + j, (*chip, c_), sibling))
                passed[-1].start()
        for a in range(n):
            copy(a, 0, sibling, me).wait_recv()
        for j, chip in enumerate(chips):
            for a in range(n):
                copy(a, 4 + j, (*chip, 1 - c_), me).wait_recv()
        for cp in first + passed:
            cp.wait_send()
        for cp in mine:
            cp.wait()

    out_shape = [jax.ShapeDtypeStruct((N_DEV,) + x.shape, x.dtype) for x in xs]
    sems = [pltpu.SemaphoreType.DMA((7 * n,)), pltpu.SemaphoreType.DMA((7 * n,)), pltpu.SemaphoreType.DMA((n,))]
    if sequencer_id is not None:
        return _on_sequencer(body, out_shape, sems, xs, after, sequencer_id, name)
    return pl.pallas_call(
        body, name=name, out_shape=out_shape, in_specs=[_ANY] * n, out_specs=[_ANY] * n, scratch_shapes=sems)(*xs)


def _sibling_exchange(gs, name, sequencer_id=None, after=None):
    n = len(gs)

    def body(*refs):
        g_refs, p_refs = refs[:n], refs[n:2 * n]
        send_sems, recv_sems = refs[2 * n:]
        x_, y_, c_ = lax.axis_index("x"), lax.axis_index("y"), lax.axis_index("c")
        if sequencer_id is not None:
            barrier = pltpu.get_barrier_semaphore()
            pl.semaphore_signal(barrier, inc=1, device_id=(x_, y_, 1 - c_), device_id_type=MESH)
            pl.semaphore_wait(barrier, 1)
        copies = [pltpu.make_async_remote_copy(
            src_ref=g_refs[a].at[2 * k + (1 - c_)], dst_ref=p_refs[a].at[k], send_sem=send_sems.at[4 * a + k],
            recv_sem=recv_sems.at[4 * a + k], device_id=(x_, y_, 1 - c_), device_id_type=MESH)
            for a in range(n) for k in range(4)]
        for cp in copies:
            cp.start()
        for cp in copies:
            cp.wait()

    out_shape = [jax.ShapeDtypeStruct((4,) + g.shape[1:], g.dtype) for g in gs]
    sems = [pltpu.SemaphoreType.DMA((4 * n,)), pltpu.SemaphoreType.DMA((4 * n,))]
    if sequencer_id is not None:
        return _on_sequencer(body, out_shape, sems, gs, after, sequencer_id, name)
    return pl.pallas_call(
        body, name=name, out_shape=out_shape, in_specs=[_ANY] * n, out_specs=[_ANY] * n, scratch_shapes=sems)(*gs)


def _slab_tiles(rows, cols):
    if rows % 8 == 0:
        return _row_tile(rows), cols
    return rows, 2 * LANE


def _pair_sums(g, p, route, name):
    _, rows, cols = g.shape
    tr, tc = _slab_tiles(rows, cols)

    def body(route_ref, g_ref, p_ref, t_ref):
        t_ref[...] = (g_ref[...].astype(F32) + p_ref[...].astype(F32)).astype(BF16)

    return pl.pallas_call(
        body, name=name, out_shape=jax.ShapeDtypeStruct((3, rows, cols), BF16),
        grid_spec=pltpu.PrefetchScalarGridSpec(
            num_scalar_prefetch=1, grid=(3, rows // tr, cols // tc),
            in_specs=[pl.BlockSpec((1, tr, tc), lambda r, i, j, route_ref: (2 * route_ref[1 + r] + route_ref[0], i, j)),
                      pl.BlockSpec((1, tr, tc), lambda r, i, j, route_ref: (route_ref[1 + r], i, j))],
            out_specs=pl.BlockSpec((1, tr, tc), lambda r, i, j, route_ref: (r, i, j))),
        compiler_params=_params(("parallel", "parallel", "parallel")),
    )(route, g, p)


def _chip_exchange(ts, name, sequencer_id=None, after=None):
    n = len(ts)

    def body(*refs):
        t_refs, l_refs = refs[:n], refs[n:2 * n]
        send_sems, recv_sems = refs[2 * n:]
        x_, y_, c_ = lax.axis_index("x"), lax.axis_index("y"), lax.axis_index("c")
        chips = [(1 - x_, y_), (x_, 1 - y_), (1 - x_, 1 - y_)]
        if sequencer_id is not None:
            barrier = pltpu.get_barrier_semaphore()
            for px, py in chips:
                pl.semaphore_signal(barrier, inc=1, device_id=(px, py, c_), device_id_type=MESH)
            pl.semaphore_wait(barrier, len(chips))
        copies = [pltpu.make_async_remote_copy(
            src_ref=t_refs[a].at[r], dst_ref=l_refs[a].at[r], send_sem=send_sems.at[3 * a + r],
            recv_sem=recv_sems.at[3 * a + r], device_id=(px, py, c_), device_id_type=MESH)
            for a in range(n) for r, (px, py) in enumerate(chips)]
        for cp in copies:
            cp.start()
        for cp in copies:
            cp.wait()

    out_shape = [jax.ShapeDtypeStruct((3,) + t.shape[1:], t.dtype) for t in ts]
    sems = [pltpu.SemaphoreType.DMA((3 * n,)), pltpu.SemaphoreType.DMA((3 * n,))]
    if sequencer_id is not None:
        return _on_sequencer(body, out_shape, sems, ts, after, sequencer_id, name)
    return pl.pallas_call(
        body, name=name, out_shape=out_shape, in_specs=[_ANY] * n, out_specs=[_ANY] * n, scratch_shapes=sems)(*ts)


def _reduce_adamw(gs, ps, landed, place, w, m, v, name):
    layers, rows, cols = w.shape
    assert layers == DEPTH == 2
    tr, tc = _slab_tiles(rows, cols)
    nr, nc = rows // tr, cols // tc
    spec = pl.BlockSpec((1, tr, tc), lambda l, i, j, place_ref: (l, i, j))

    def own(layer, which):
        pi, pj = (nr - 1, nc - 1) if layer == 0 else (0, 0)

        def index(l, i, j, place_ref):
            lead = 0 if which is None else place_ref[which]
            return lead, jnp.where(l == layer, i, pi), jnp.where(l == layer, j, pj)

        return pl.BlockSpec((3 if which is None else 1, tr, tc), index)

    def body(place_ref, g0_ref, p0_ref, l0_ref, g1_ref, p1_ref, l1_ref, w_ref, m_ref, v_ref,
             g_ref, d_ref, nm_ref, nv_ref):
        def update(own_ref, sib_ref, l_ref):
            g = (own_ref[0].astype(F32) + sib_ref[0].astype(F32) + l_ref[0].astype(F32) + l_ref[1].astype(F32)
                 + l_ref[2].astype(F32))
            g_ref[0] = g
            d_ref[0], nm_ref[0], nv_ref[0] = _adamw_math(w_ref[0], g, m_ref[0], v_ref[0])

        @pl.when(pl.program_id(0) == 0)
        def _():
            update(g0_ref, p0_ref, l0_ref)

        @pl.when(pl.program_id(0) == 1)
        def _():
            update(g1_ref, p1_ref, l1_ref)

    out = jax.ShapeDtypeStruct(w.shape, F32)
    return pl.pallas_call(
        body, name=name, out_shape=(out, out, out, out),
        grid_spec=pltpu.PrefetchScalarGridSpec(
            num_scalar_prefetch=1, grid=(DEPTH, nr, nc),
            in_specs=[own(0, 0), own(0, 1), own(0, None), own(1, 0), own(1, 1), own(1, None), spec, spec, spec],
            out_specs=(spec, spec, spec, spec)),
        compiler_params=_params(("arbitrary", "arbitrary", "arbitrary")),
    )(place, gs[0], ps[0], landed[0], gs[1], ps[1], landed[1], w, m, v)


def _pack(pieces, row_multiple, dtype, cols=D, rows=None):
    flat = jnp.concatenate([p.astype(dtype).reshape(-1) for p in pieces])
    if rows is None:
        rows = -(-flat.shape[0] // cols)
        rows = -(-rows // row_multiple) * row_multiple
    flat = jnp.pad(flat, (0, rows * cols - flat.shape[0]))
    return flat.reshape(rows, cols)


def _unpack(flat, shapes, lead=()):
    out, off = [], 0
    for shp in shapes:
        n = 1
        for s_ in shp:
            n *= s_
        out.append(lax.slice_in_dim(flat, off, off + n, axis=len(lead)).reshape(lead + tuple(shp)))
        off += n
    return out


WIN_STRIDE = 704
WIN_ROWS = 720
Z_TURN = 1544


def _window(wt, me, name):
    padded = jnp.pad(wt, ((0, 0), (0, WIN_ROWS - IN_SHARD), (0, 0)))

    def body(me_ref, x_ref, o_ref):
        o_ref[0] = pltpu.roll(x_ref[0], me_ref[0], axis=0).astype(BF16)

    spec = pl.BlockSpec((1, WIN_ROWS, D), lambda l, me_ref: (l, 0, 0))
    return pl.pallas_call(
        body, name=name, out_shape=jax.ShapeDtypeStruct((DEPTH, WIN_ROWS, D), BF16),
        grid_spec=pltpu.PrefetchScalarGridSpec(num_scalar_prefetch=1, grid=(DEPTH,), in_specs=[spec], out_specs=spec),
        compiler_params=_params(("parallel",)),
    )(me, padded)


def _z_rows_from_windows(win):
    over = WIN_ROWS - WIN_STRIDE
    pieces = [(0, win[0][0:WIN_STRIDE])]
    for d in range(1, N_DEV):
        base = WIN_STRIDE * d
        pieces.append((base, win[d - 1][WIN_STRIDE:WIN_ROWS] + win[d][0:over]))
        pieces.append((base + over, win[d][over:WIN_STRIDE]))
    pieces.append((WIN_STRIDE * N_DEV, win[N_DEV - 1][WIN_STRIDE:WIN_ROWS]))

    def rows(a, b):
        out = []
        for start, arr in pieces:
            lo, hi = max(a, start), min(b, start + arr.shape[0])
            if lo < hi:
                out.append(arr[lo - start:hi - start])
        return out

    pad = jnp.zeros((NZ - IN_COLS, win.shape[-1]), win.dtype)
    return jnp.concatenate(rows(Z_TURN, IN_COLS) + rows(0, Z_TURN) + [pad], axis=0)


def _in_rows_from_z(wt):
    return jnp.concatenate([wt[Z_Q:Z_Q + 1536], wt[Z_F:Z_F + 8], wt[Z_PC:Z_PC + 1024], wt[Z_G:Z_G + 3072]], axis=0)


def _pad_rows(v, rows=8):
    return jnp.pad(v, ((0, rows - v.shape[0]), (0, 0)))


def _layer_fwd(l, x, wts, gvec, mod):
    tag = f"l{l}"
    z, h = _matmul(x, wts["w_in_t"], "nt", f"in_proj_{tag}", tm=1024, tn=1152, prologue=_prenorm_prologue(0, 0, 1),
                   prologue_vecs=[gvec, mod])
    qa, ka, va, kat = _attn_prep(z, wts["b_f"], f"attn_prep_{tag}")
    qa = wts["arrive"](qa)
    o, lse = _attn_fwd(qa, ka, va, f"attn_{tag}")
    br_b = _pool_fwd(z, wts["wp_bd"], wts["pool_scale"], f"pool_{tag}")
    br_c = _conv_fwd(z, wts["conv_w"], f"conv_{tag}")
    pa = _matmul(o, wts["wa"], "nn", f"proj_a_{tag}", out_dtype=BF16)
    pb = _matmul(br_b, wts["wb"], "nn", f"proj_b_{tag}", out_dtype=BF16)
    gates = [(z, Z_G + k * D) for k in range(3)]
    pc, merged = _matmul(br_c, wts["wc"], "nn", f"proj_c_merge_{tag}", tm=1024, tn=512,
                         extra=gates + [(pa, 0), (pb, 0)], epilogue=_merge_epilogue, out_dtypes=(BF16, BF16))
    y, x1 = _matmul(merged, wts["w_out"], "nn", f"out_proj_{tag}", tm=1024, tn=D, extra=[(x, 0)],
                    vec_extra=[gvec, mod], epilogue=_postnorm_epilogue(1, 2), out_dtypes=(F32, F32))
    a, r, h2 = _matmul(x1, wts["w_ff1"], "nn", f"ff1_{tag}", b_col_shards=True, epilogue=_relu2_epilogue,
                       out_dtypes=(BF16, BF16), prologue=_prenorm_prologue(2, 3, 4), prologue_vecs=[gvec, mod])
    y2, x2 = _matmul(r, wts["w_ff2"], "nn", f"ff2_{tag}", tm=1024, tn=D, tk=1024, extra=[(x1, 0)],
                     vec_extra=[gvec, mod], epilogue=_postnorm_epilogue(3, 5), out_dtypes=(F32, F32))
    saved = dict(x=x, h=h, z=z, qa=qa, ka=ka, va=va, kat=kat, o=o, lse=lse, br_b=br_b, br_c=br_c, pa=pa, pb=pb, pc=pc,
                 merged=merged, y=y, x1=x1, h2=h2, a=a, r=r, y2=y2)
    return x2, saved


def _ffn_bwd(l, dx2, sv, wts, gvec, mod, midpoint):
    tag = f"l{l}"
    dy2, red_post_ff = _postnorm_bwd(sv["y2"], gvec, mod, dx2, 3, 5, f"postnorm_ff_bwd_{tag}")
    dy2 = midpoint(dy2)
    da = _matmul(dy2, wts["w_ff2"], "nt", f"ff2_dx_{tag}", extra=[(sv["a"], 0)], epilogue=_relu2_bwd_epilogue,
                 out_dtypes=(BF16,))[0]
    d_w_ff2 = _matmul(sv["r"], dy2, "tn", f"ff2_dw_{tag}", out_dtype=GRAD_DTYPE)
    dx1, sums = _matmul(da, wts["w_ff1"], "nt", f"ff1_dx_{tag}", tm=1024, tn=D, b_col_shards=True,
                        extra=[(sv["x1"], 0), (dx2, 0)], vec_extra=[gvec, mod], epilogue=_prenorm_bwd_epilogue(2, 4),
                        out_dtypes=(F32, F32), n_row_sums=1)
    red_pre_ff = jnp.sum(sums.reshape(-1, 8, D), axis=0)
    d_w_ff1 = _matmul(sv["h2"], da, "tn", f"ff1_dw_{tag}", out_dtype=GRAD_DTYPE, out_col_shards=True)
    return dx1, [d_w_ff1, d_w_ff2.reshape(N_DEV, D_FF // N_DEV, D)], (red_pre_ff, red_post_ff)


def _mixer_bwd(l, dx1, sv, wts, gvec, mod, ffn_reds, midpoint):
    tag = f"l{l}"
    red_pre_ff, red_post_ff = ffn_reds
    dy, red_post_mix = _postnorm_bwd(sv["y"], gvec, mod, dx1, 1, 2, f"postnorm_mix_bwd_{tag}")
    gates = [(sv["z"], Z_G + k * D) for k in range(3)]
    dpa, dpb, dpc, *dgl = _matmul(dy, wts["w_out"], "nt", f"out_proj_dx_{tag}", tm=1024, tn=512,
                                  extra=gates + [(sv["pa"], 0), (sv["pb"], 0), (sv["pc"], 0)],
                                  epilogue=_merge_bwd_epilogue, out_dtypes=(BF16,) * 6)
    d_w_out = _matmul(sv["merged"], dy, "tn", f"out_proj_dw_{tag}", out_dtype=GRAD_DTYPE)
    dpa = midpoint(dpa)
    do = _matmul(dpa, wts["wa"], "nt", f"proj_a_dx_{tag}")
    dbr_b = _matmul(dpb, wts["wb"], "nt", f"proj_b_dx_{tag}")
    dbr_c = _matmul(dpc, wts["wc"], "nt", f"proj_c_dx_{tag}")
    d_wa = _matmul(sv["o"], dpa, "tn", f"proj_a_dw_{tag}", out_dtype=GRAD_DTYPE)
    d_wb = _matmul(sv["br_b"], dpb, "tn", f"proj_b_dw_{tag}", out_dtype=GRAD_DTYPE)
    d_wc = _matmul(sv["br_c"], dpc, "tn", f"proj_c_dw_{tag}", out_dtype=GRAD_DTYPE)
    d_w_branch = jnp.concatenate([d_wa, d_wb, d_wc], axis=0)

    dpu, d_wp_bd, red_pool = _pool_bwd(sv["z"], wts["wp_bd"], wts["pool_scale"], dbr_b, f"pool_bwd_{tag}")
    dconv, red_conv = _conv_bwd(sv["z"], wts["conv_w"], dbr_c, f"conv_bwd_{tag}")
    qa2, doa = _attn_bwd_prep(sv["qa"], sv["o"], sv["lse"], do, f"attn_bwd_prep_{tag}")
    dqt, dka, dva = _attn_bwd(qa2, sv["ka"], sv["va"], sv["kat"], doa, f"attn_bwd_{tag}")
    dq, dk, dv, dfl, red_f = _attn_bwd_post(sv["z"], wts["b_f"], dqt, dka, dva, f"attn_bwd_post_{tag}")
    dz = _concat_columns([dpu, dconv, *dgl, dq, dk, dv, dfl], f"dz_{tag}")
    dx0, sums = _matmul(dz, wts["w_in_t"], "nn", f"in_proj_dx_{tag}", tm=1024, tn=D, tk=1152,
                        extra=[(sv["x"], 0), (dx1, 0)], vec_extra=[gvec, mod], epilogue=_prenorm_bwd_epilogue(0, 1),
                        out_dtypes=(F32, F32), n_row_sums=1)
    red_pre_mix = jnp.sum(sums.reshape(-1, 8, D), axis=0)
    d_w_in_t = _matmul(dz, sv["h"], "tn", f"in_proj_dw_{tag}", out_dtype=GRAD_DTYPE, tm=1152)

    rows = D // N_DEV
    big = [_in_rows_from_z(d_w_in_t).reshape(N_DEV, IN_SHARD, D), d_w_branch.reshape(N_DEV, rows, D),
           d_w_out.reshape(N_DEV, rows, D)]
    d_w_pool = jnp.stack([d_wp_bd[64 * g:64 * (g + 1), 64 * g:64 * (g + 1)] for g in range(4)])
    small = dict(
        mod=jnp.stack([red_pre_mix[0], red_pre_mix[1], red_post_mix[0], red_pre_ff[0], red_pre_ff[1], red_post_ff[0]]),
        g_mix_pre=red_pre_mix[2], g_mix_post=red_post_mix[1], g_ff_pre=red_pre_ff[2], g_ff_post=red_post_ff[1],
        b_f=red_f[0, 0:8], w_pool=d_w_pool, pool_scale=red_pool[0], conv_w=red_conv[0:3])
    return dx0, big, small


SMALL_KEYS = ["mod", "g_mix_pre", "g_mix_post", "g_ff_pre", "g_ff_post", "b_f", "w_pool", "pool_scale", "conv_w"]
SMALL_SHAPES = [(DEPTH, 6 * D), (DEPTH, D), (DEPTH, D), (DEPTH, D), (DEPTH, D), (DEPTH, 8), (DEPTH, 4, 64, 64),
                (DEPTH, POOL_W), (DEPTH, 3, CONV_W)]


def kernel(x, c, w_ada, b_ada, g_mix_pre, g_mix_post, g_ff_pre, g_ff_post, w_in, b_f, w_pool, pool_scale, conv_w, w_branch, w_out, w_ff1, w_ff2, loss_target, m_w_ada, m_b_ada, m_g_mix_pre, m_g_mix_post, m_g_ff_pre, m_g_ff_post, m_w_in, m_b_f, m_w_pool, m_pool_scale, m_conv_w, m_w_branch, m_w_out, m_w_ff1, m_w_ff2, v_w_ada, v_b_ada, v_g_mix_pre, v_g_mix_post, v_g_ff_pre, v_g_ff_post, v_w_in, v_b_f, v_w_pool, v_pool_scale, v_conv_w, v_w_branch, v_w_out, v_w_ff1, v_w_ff2):
    ix, iy, ic = lax.axis_index("x"), lax.axis_index("y"), lax.axis_index("c")
    me = 4 * ix + 2 * iy + ic
    route = jnp.stack([ic, 2 * (1 - ix) + iy, 2 * ix + (1 - iy), 2 * (1 - ix) + (1 - iy)]).astype(jnp.int32)
    place = jnp.stack([me, 2 * ix + iy]).astype(jnp.int32)
    wt_in, mt_in, vt_in = (jnp.transpose(a, (0, 2, 1)) for a in (w_in, m_w_in, v_w_in))

    c_all = _all_gather([_pad_rows(c)], "gather_c")[0][:, 0, :]
    c_pad = _pad_rows(c_all, ADA_ROWS)
    b_cols = lax.dynamic_slice_in_dim(b_ada, me * ADA_COLS, ADA_COLS, axis=1)
    b_cols = jnp.broadcast_to(b_cols[:, None, :], (DEPTH, 8, ADA_COLS))
    mod_part = _ada_fwd(c_pad, w_ada, b_cols, "ada_fwd")
    mod_all = _all_gather([mod_part.reshape(DEPTH * ADA_ROWS, ADA_COLS)], "gather_mod")[0]
    mod_all = mod_all.reshape(N_DEV, DEPTH, ADA_ROWS, ADA_COLS)
    mod_mine = lax.dynamic_index_in_dim(mod_all, me, axis=2, keepdims=False)
    mod_mine = jnp.transpose(mod_mine, (1, 0, 2)).reshape(DEPTH, 6, D)

    cw_cols = CONV_W // N_DEV
    cw_send = jnp.pad(conv_w.reshape(DEPTH * 3, cw_cols), ((0, 8 - DEPTH * 3), (0, LANE - cw_cols)))
    win_in = _window(wt_in, place[0:1], "w_in_window")
    send = [[w[l].astype(BF16) for w in (win_in, w_branch, w_out, w_ff1, w_ff2)] for l in range(DEPTH)]
    first = _all_gather(send[0][:1], "gather_weights_l0_in", sequencer_id=1, after=mod_all)
    rest = _all_gather(send[0][1:] + [cw_send], "gather_weights_l0_rest", sequencer_id=2, after=first[0])
    first1 = _all_gather(send[1][:1], "gather_weights_l1_in", sequencer_id=3, after=first[0])
    rest1 = _all_gather(send[1][1:], "gather_weights_l1_rest", sequencer_id=12, after=first[0])
    first, (mt_in, vt_in) = lax.optimization_barrier((first, (mt_in, vt_in)))
    gathered = [first + rest[:4], first1 + rest1]
    cw_all = rest[4][:, :DEPTH * 3, :cw_cols].reshape(N_DEV, DEPTH, 3, cw_cols)

    def first_operands(l, p_in):
        wp_bd = jnp.zeros((POOL_W, POOL_W), F32)
        for g in range(4):
            wp_bd = wp_bd.at[64 * g:64 * (g + 1), 64 * g:64 * (g + 1)].set(w_pool[l, g])
        return dict(w_in_t=_z_rows_from_windows(p_in), wp_bd=wp_bd.astype(BF16),
                    pool_scale=_pad_rows(pool_scale[l][None, :]), b_f=_pad_rows(jnp.pad(b_f[l], (0, LANE - 8))[None, :]))

    def rest_operands(l, rest):
        p_br, p_out, p_ff1, p_ff2 = rest
        w_br_full = p_br.reshape(D, D)
        cw_full = jnp.transpose(cw_all[:, l], (1, 0, 2)).reshape(3, CONV_W)
        return dict(wa=w_br_full[0:A_WIDTH], wb=w_br_full[A_WIDTH:A_WIDTH + POOL_W], wc=w_br_full[A_WIDTH + POOL_W:],
                    w_out=p_out.reshape(D, D), w_ff1=p_ff1, w_ff2=p_ff2.reshape(D_FF, D), conv_w=_pad_rows(cw_full))

    xs = x[0]
    saved, layers = [], []
    for l in range(DEPTH):
        p_in, rest = gathered[l][0], gathered[l][1:5]
        if l > 0:
            xs, p_in = lax.optimization_barrier((xs, p_in))
        wts = first_operands(l, p_in)

        def arrive(t, l=l, rest=rest, wts=wts):
            if l > 0:
                t, rest = lax.optimization_barrier((t, rest))
            wts.update(rest_operands(l, rest))
            return t

        wts["arrive"] = arrive
        gvec = _pad_rows(jnp.stack([g_mix_pre[l], g_mix_post[l], g_ff_pre[l], g_ff_post[l]]))
        layers.append((wts, gvec, _pad_rows(mod_mine[l])))
        xs, sv = _layer_fwd(l, xs, *layers[l])
        saved.append(sv)
    dx, loss_part = _loss_head(xs, loss_target[0], "loss_head")
    small_grads = [None] * DEPTH
    mine, sibs, landed = ({} for _ in range(3))
    seq_id = iter(range(4, 4 + 4 * DEPTH))
    last = [gathered[DEPTH - 1][1]]

    def start(group, grads):
        mine[group] = grads
        sibs[group] = _sibling_exchange(grads, f"rs_sibling_{group}", sequencer_id=next(seq_id), after=last[0])
        last[0] = sibs[group][0]

    def finish(group, later):
        later, (grads, sib) = lax.optimization_barrier((later, (mine[group], sibs[group])))
        sends = [_pair_sums(g, p, route, f"rs_pair_sums_{group}_{k}") for k, (g, p) in enumerate(zip(grads, sib))]
        later, sends = lax.optimization_barrier((later, sends))
        landed[group] = _chip_exchange(sends, f"rs_chips_{group}", sequencer_id=next(seq_id), after=last[0])
        last[0] = landed[group][0]
        return later

    pending = None
    for l in reversed(range(DEPTH)):
        hook = (lambda da: da) if pending is None else functools.partial(finish, pending)
        dx, ffn_grads, ffn_reds = _ffn_bwd(l, dx, saved[l], *layers[l], hook)
        start(f"ffn_l{l}", ffn_grads)
        dx, mix_grads, small_grads[l] = _mixer_bwd(l, dx, saved[l], *layers[l], ffn_reds,
                                                   functools.partial(finish, f"ffn_l{l}"))
        start(f"mix_l{l}", mix_grads)
        pending = f"mix_l{l}"
    grad_x = dx[None]

    big_w = [wt_in, w_branch, w_out, w_ff1, w_ff2]
    big_m = [mt_in, m_w_branch, m_w_out, m_w_ff1, m_w_ff2]
    big_v = [vt_in, v_w_branch, v_w_out, v_w_ff1, v_w_ff2]
    where = [("mix", 0), ("mix", 1), ("mix", 2), ("ffn", 0), ("ffn", 1)]

    def reduce_and_update(k):
        group, at = where[k]
        return _reduce_adamw([mine[f"{group}_l{l}"][at] for l in range(DEPTH)],
                             [sibs[f"{group}_l{l}"][at] for l in range(DEPTH)],
                             [landed[f"{group}_l{l}"][at] for l in range(DEPTH)], place, big_w[k], big_m[k], big_v[k],
                             f"rs_sum_adamw_{k}")

    big_res = {k: list(reduce_and_update(k)) for k in (3, 4)}
    big_res[3][0] = finish(pending, big_res[3][0])

    small = {k: jnp.stack([small_grads[l][k] for l in range(DEPTH)]) for k in SMALL_KEYS}
    payload = _pack([small[k] for k in SMALL_KEYS] + [loss_part[0:1, 0:1]], 8, F32)
    small_all = _all_gather([payload], "gather_small")[0]
    dmod_all = small_all[:, 0:DEPTH * 6, :].reshape(N_DEV, DEPTH, 6 * D)
    summed = _unpack(_sum_slabs(small_all, "sum_small").reshape(-1), SMALL_SHAPES + [(1, 1)])
    sg = dict(zip(SMALL_KEYS, summed))
    loss = summed[-1][0, 0]
    dmod_cols = lax.dynamic_slice_in_dim(dmod_all, me * ADA_COLS, ADA_COLS, axis=2)
    dmod_cols = jnp.pad(jnp.transpose(dmod_cols, (1, 0, 2)), ((0, 0), (0, ADA_ROWS - N_DEV), (0, 0)))
    g_w_ada = _ada_bwd(c_pad, dmod_cols, "ada_bwd")
    g_conv_w = lax.dynamic_slice_in_dim(sg["conv_w"], me * (CONV_W // N_DEV), CONV_W // N_DEV, axis=2)

    ada_out = [g_w_ada] + list(_adamw(w_ada, g_w_ada, m_w_ada, v_w_ada, "adamw_ada"))
    rest_w = [b_ada, g_mix_pre, g_mix_post, g_ff_pre, g_ff_post, b_f, w_pool, pool_scale, conv_w]
    rest_m = [m_b_ada, m_g_mix_pre, m_g_mix_post, m_g_ff_pre, m_g_ff_post, m_b_f, m_w_pool, m_pool_scale, m_conv_w]
    rest_v = [v_b_ada, v_g_mix_pre, v_g_mix_post, v_g_ff_pre, v_g_ff_post, v_b_f, v_w_pool, v_pool_scale, v_conv_w]
    rest_g = [sg["mod"], sg["g_mix_pre"], sg["g_mix_post"], sg["g_ff_pre"], sg["g_ff_post"], sg["b_f"],
              sg["w_pool"], sg["pool_scale"], g_conv_w]
    rest_shapes = [a.shape for a in rest_w]
    upd = _adamw(_pack(rest_w, 8, F32)[None], _pack(rest_g, 8, F32)[None], _pack(rest_m, 8, F32)[None],
                 _pack(rest_v, 8, F32)[None], "adamw_rest")
    rest_out = [rest_g] + [_unpack(arr.reshape(-1), rest_shapes) for arr in upd]
    rest_out = [[ada_out[which]] + rest_out[which] for which in range(4)]

    landed[pending], rest_out = lax.optimization_barrier((landed[pending], rest_out))
    big_res.update({k: reduce_and_update(k) for k in (0, 1, 2)})
    big_out = [[jnp.transpose(big_res[k][which], (0, 2, 1)) if k == 0 else big_res[k][which] for k in range(5)]
               for which in range(4)]

    def ordered(k):
        r, b = rest_out[k], big_out[k]
        return [r[0], r[1], r[2], r[3], r[4], r[5], b[0], r[6], r[7], r[8], r[9], b[1], b[2], b[3], b[4]]

    return (loss, grad_x, *ordered(0), *ordered(1), *ordered(2), *ordered(3))
```

```python
import functools

import jax
import jax.numpy as jnp
from jax import lax
from jax.experimental import pallas as pl
from jax.experimental.pallas import tpu as pltpu
from jax.experimental.pallas import tpu_sc as plsc

F32 = jnp.float32
BF16 = jnp.bfloat16
GRAD_DTYPE = BF16

N_DEV = 8
D = 1024
S = 2048
DEPTH = 2
D_FF = 4 * D
A_WIDTH = 512
HEAD_DIM = 64
N_PAIR = 4
POOL_W = 256
CONV_W = 256
IN_COLS = 5640
ADA_COLS = 6 * D // N_DEV
IN_SHARD = IN_COLS // N_DEV
RMS_EPS = 1e-6
NEG_INF = -1e30
ATT_SCALE = HEAD_DIM ** -0.5

NZ = 5760
Z_PC = 0
Z_G = 1024
Z_Q = 4096
Z_K = 4608
Z_V = 5120
Z_F = 5632

LR, B1, B2, EPS, WD, STEP = 0.001, 0.9, 0.999, 1e-08, 0.01, 10

LANE = 128
VMEM_LIMIT_BYTES = 48 * 1024 * 1024
TS = 512
TQ = 256
TQ_FWD = 512
HEADS_PER_STEP = 8
HEADS_PER_STEP_FWD = 8


def _params(sem=None):
    return pltpu.CompilerParams(dimension_semantics=sem, vmem_limit_bytes=VMEM_LIMIT_BYTES)


def _pick(n, target):
    best = None
    for t in range(LANE, min(n, target) + 1, LANE):
        if n % t == 0:
            best = t
    return n if best is None else best


def _matmul(a, b, mode, name, out_dtype=F32, tm=2048, tn=1024, tk=2048, b_col_shards=False, out_col_shards=False,
            extra=(), vec_extra=(), epilogue=None, out_dtypes=None, n_row_sums=0, prologue=None, prologue_tiles=(), prologue_vecs=(),
            prologue_sums=False):
    if b_col_shards:
        shards, b_rows, shard_cols = b.shape
        b_shape = (b_rows, shards * shard_cols)
    else:
        b_shape = b.shape
    if mode == "nn":
        (m, k), (k2, n) = a.shape, b_shape
    elif mode == "nt":
        (m, k), (n, k2) = a.shape, b_shape
    else:
        (k, m), (k2, n) = a.shape, b_shape
    assert k == k2, (a.shape, b.shape, mode)
    tm, tn, tk = _pick(m, tm), _pick(n, tn), _pick(k, tk)
    if b_col_shards and mode == "nn":
        tn = shard_cols
    per_step = 1
    if b_col_shards and mode == "nt":
        per_step = max(1, min(tk, 1024) // shard_cols)
        tk = per_step * shard_cols
    if out_col_shards:
        tn = n // N_DEV
    nk = k // tk
    if mode == "nn":
        a_spec = pl.BlockSpec((tm, tk), lambda i, j, kk: (i, kk))
        b_spec = (pl.BlockSpec((None, tk, tn), lambda i, j, kk: (j, kk, 0)) if b_col_shards else
                  pl.BlockSpec((tk, tn), lambda i, j, kk: (kk, j)))
        dims = (((1,), (0,)), ((), ()))
    elif mode == "nt":
        a_spec = pl.BlockSpec((tm, tk), lambda i, j, kk: (i, kk))
        b_spec = (pl.BlockSpec((per_step, tn, shard_cols), lambda i, j, kk: (kk, j, 0)) if b_col_shards else
                  pl.BlockSpec((tn, tk), lambda i, j, kk: (j, kk)))
        dims = (((1,), (1,)), ((), ()))
    else:
        assert not b_col_shards
        a_spec = pl.BlockSpec((tk, tm), lambda i, j, kk: (kk, i))
        b_spec = pl.BlockSpec((tk, tn), lambda i, j, kk: (kk, j))
        dims = (((0,), (0,)), ((), ()))
    if out_col_shards:
        out_shape = jax.ShapeDtypeStruct((N_DEV, m, tn), out_dtype)
        out_spec = pl.BlockSpec((None, tm, tn), lambda i, j, kk: (j, i, 0))
    else:
        out_shape = jax.ShapeDtypeStruct((m, n), out_dtype)
        out_spec = pl.BlockSpec((tm, tn), lambda i, j, kk: (i, j))

    n_extra = len(extra) + len(vec_extra)
    extra_specs = [pl.BlockSpec((tm, tn), lambda i, j, kk, off=off: (i, j + off // tn)) for _, off in extra]
    extra_specs += [pl.BlockSpec((8, tn), lambda i, j, kk: (0, j)) for _ in vec_extra]
    if epilogue is not None:
        assert not out_col_shards and all(off % tn == 0 for _, off in extra)
        out_shape = [jax.ShapeDtypeStruct((m, n), dt) for dt in out_dtypes]
        out_spec = [pl.BlockSpec((tm, tn), lambda i, j, kk: (i, j)) for _ in out_dtypes]
        for at in range(len(out_dtypes) - n_row_sums, len(out_dtypes)):
            out_shape[at] = jax.ShapeDtypeStruct((8 * (m // tm), n), out_dtypes[at])
            out_spec[at] = pl.BlockSpec((8, tn), lambda i, j, kk: (i, j))

    def product(a_ref, b_ref):
        if b_col_shards and mode == "nt":
            b_tile = jnp.concatenate([b_ref[s] for s in range(per_step)], axis=1) if per_step > 1 else b_ref[0]
        else:
            b_tile = b_ref[...]
        return lax.dot_general(a_ref[...].astype(BF16), b_tile.astype(BF16), dims, preferred_element_type=F32)

    def write(acc, extra_refs, o_refs):
        if epilogue is None:
            o_refs[0][...] = acc.astype(out_dtype)
        else:
            for o_ref, tile in zip(o_refs, epilogue(acc, *[r[...] for r in extra_refs])):
                o_ref[...] = tile.astype(o_ref.dtype)

    def body_one_pass(a_ref, b_ref, *refs):
        write(product(a_ref, b_ref), refs[:n_extra], refs[n_extra:])

    if prologue is not None:
        assert nk == 1 and mode in ("nn", "nt")
        n_pro = len(prologue_tiles) + len(prologue_vecs)
        sums = 1 if prologue_sums else 0
        outs = out_shape if isinstance(out_shape, list) else [out_shape]
        out_specs_all = (out_spec if isinstance(out_spec, list) else [out_spec]) + [
            pl.BlockSpec((tm, tk), lambda i, j, kk: (i, 0))]
        outs = outs + [jax.ShapeDtypeStruct((m, k), BF16)]
        if sums:
            outs.append(jax.ShapeDtypeStruct((8 * (m // tm), k), F32))
            out_specs_all.append(pl.BlockSpec((8, tk), lambda i, j, kk: (i, 0)))

        def body_prologue(a_ref, b_ref, *refs):
            pro_refs, rest = refs[:n_pro], refs[n_pro:]
            left_ref = rest[-1]
            left_out = rest[-2 - sums]

            @pl.when(pl.program_id(1) == 0)
            def _():
                made = prologue(a_ref[...], *[r[...] for r in pro_refs])
                left = (made[0] if sums else made).astype(BF16)
                left_ref[...] = left
                left_out[...] = left
                if sums:
                    rest[-2][...] = made[1]

            write(product(left_ref, b_ref), rest[:n_extra], rest[n_extra:-2 - sums])

        pro_specs = [a_spec for _ in prologue_tiles] + [pl.BlockSpec((8, tk), lambda i, j, kk: (0, 0)) for _ in prologue_vecs]
        return pl.pallas_call(
            body_prologue, name=name, out_shape=outs, grid=(m // tm, n // tn, nk),
            in_specs=[a_spec, b_spec] + pro_specs + extra_specs,
            out_specs=out_specs_all,
            scratch_shapes=[pltpu.VMEM((tm, tk), BF16)],
            compiler_params=_params(("parallel", "arbitrary", "arbitrary")),
        )(a, b, *prologue_tiles, *prologue_vecs, *[x for x, _ in extra], *vec_extra)

    def body(a_ref, b_ref, *refs):
        acc_ref = refs[-1]
        kk = pl.program_id(2)

        @pl.when(kk == 0)
        def _():
            acc_ref[...] = product(a_ref, b_ref)

        @pl.when(kk > 0)
        def _():
            acc_ref[...] += product(a_ref, b_ref)

        @pl.when(kk == nk - 1)
        def _():
            write(acc_ref[...], refs[:n_extra], refs[n_extra:-1])

    return pl.pallas_call(
        body_one_pass if nk == 1 else body, name=name,
        out_shape=out_shape,
        grid=(m // tm, n // tn, nk),
        in_specs=[a_spec, b_spec] + extra_specs,
        out_specs=out_spec,
        scratch_shapes=[] if nk == 1 else [pltpu.VMEM((tm, tn), F32)],
        compiler_params=_params(("parallel", "parallel", "arbitrary")),
    )(a, b, *[x for x, _ in extra], *vec_extra)


def _row_spec(width=D, col=0):
    return pl.BlockSpec((TS, width), lambda i: (i, col))


def _vec_spec(rows=8, width=D):
    return pl.BlockSpec((rows, width), lambda i: (0, 0))


def _rms(x):
    return lax.rsqrt(jnp.mean(x * x, axis=-1, keepdims=True) + RMS_EPS)


def _concat_columns(pieces, name):
    widths = [p.shape[1] for p in pieces]
    offsets = [sum(widths[:k]) for k in range(len(widths))]

    def body(*refs):
        o_ref = refs[-1]
        for ref, off, w in zip(refs[:-1], offsets, widths):
            o_ref[:, off:off + w] = ref[...]

    return pl.pallas_call(
        body, name=name, out_shape=jax.ShapeDtypeStruct((S, sum(widths)), pieces[0].dtype), grid=(S // TS,),
        in_specs=[_row_spec(w) for w in widths], out_specs=_row_spec(sum(widths)),
        compiler_params=_params(("parallel",)),
    )(*pieces)


def _loss_head(xf, target, name):
    def body(x_ref, t_ref, dx_ref, loss_ref):
        i = pl.program_id(0)

        @pl.when(i == 0)
        def _():
            loss_ref[...] = jnp.zeros_like(loss_ref)

        e = x_ref[...] - t_ref[...]
        dx_ref[...] = e / float(D)
        per_tok = jnp.mean(e * e, axis=-1, keepdims=True)
        loss_ref[0:1, 0:1] += 0.5 * jnp.sum(per_tok, axis=0, keepdims=True)

    return pl.pallas_call(
        body, name=name,
        out_shape=(jax.ShapeDtypeStruct((S, D), F32), jax.ShapeDtypeStruct((8, LANE), F32)),
        grid=(S // TS,),
        in_specs=[_row_spec(), _row_spec()],
        out_specs=(_row_spec(), pl.BlockSpec((8, LANE), lambda i: (0, 0))),
        compiler_params=_params(("arbitrary",)),
    )(xf, target)


def _relu2_epilogue(a):
    t = jnp.maximum(a, 0.0)
    return a, t * t


def _relu2_bwd_epilogue(dr, a):
    return (dr * (2.0 * jnp.maximum(a, 0.0)),)


def _merge_epilogue(pc, g0, g1, g2, pa, pb):
    return pc, jax.nn.sigmoid(g0) * pa + jax.nn.sigmoid(g1) * pb + jax.nn.sigmoid(g2) * pc


def _prenorm_prologue(g_row, shift_row, scale_row):
    def prologue(x, gvec, mod):
        y = x * _rms(x) * gvec[g_row:g_row + 1, :]
        return y * (1.0 + mod[scale_row:scale_row + 1, :]) + mod[shift_row:shift_row + 1, :]

    return prologue


def _rows8(*rows):
    sub = lax.broadcasted_iota(jnp.int32, (8, rows[0].shape[1]), 0)
    out = jnp.zeros((8, rows[0].shape[1]), F32)
    for k, r in enumerate(rows):
        out = jnp.where(sub == k, r, out)
    return out


def _postnorm_bwd_prologue(g_row, gate_row):
    def prologue(y, dxo, gvec, mod):
        g = gvec[g_row:g_row + 1, :]
        r = _rms(y)
        n = y * r
        dyn = dxo * mod[gate_row:gate_row + 1, :]
        dn = dyn * g
        dy = r * (dn - n * jnp.mean(dn * n, axis=-1, keepdims=True))
        return dy, _rows8(jnp.sum(dxo * (n * g), axis=0, keepdims=True), jnp.sum(dyn * n, axis=0, keepdims=True))

    return prologue


def _prenorm_bwd_epilogue(g_row, scale_row):
    def epilogue(dh, x, dres, gvec, mod):
        g = gvec[g_row:g_row + 1, :]
        r = _rms(x)
        n = x * r
        dyg = dh * (1.0 + mod[scale_row:scale_row + 1, :])
        dn = dyg * g
        dx = r * (dn - n * jnp.mean(dn * n, axis=-1, keepdims=True))
        sums = _rows8(jnp.sum(dh, axis=0, keepdims=True), jnp.sum(dh * (n * g), axis=0, keepdims=True),
                      jnp.sum(dyg * n, axis=0, keepdims=True))
        return dres + dx, sums

    return epilogue


def _postnorm_epilogue(g_row, gate_row):
    def epilogue(y, x, gvec, mod):
        yn = y * _rms(y) * gvec[g_row:g_row + 1, :]
        return y, x + mod[gate_row:gate_row + 1, :] * yn

    return epilogue


def _merge_bwd_epilogue(dm, g0, g1, g2, pa, pb, pc):
    sg = [jax.nn.sigmoid(g) for g in (g0, g1, g2)]
    return tuple(dm * s for s in sg) + tuple(dm * p * (s * (1.0 - s)) for p, s in zip((pa, pb, pc), sg))


def _shift_down(x, k, row):
    return jnp.where(row >= k, pltpu.roll(x, k, axis=0), 0.0)


def _shift_up(x, k, row):
    n = x.shape[0]
    return jnp.where(row < n - k, pltpu.roll(x, n - k, axis=0), 0.0)


def _cumsum_rows(x, row, reverse=False):
    shift = _shift_up if reverse else _shift_down
    k = 1
    while k < x.shape[0]:
        x = x + shift(x, k, row)
        k *= 2
    return x


def _full_spec(shape, idx=(0, 0)):
    return pl.BlockSpec(shape, lambda i: idx)


def _pool_window_select(lane, a2, a4, a8, a16):
    return jnp.where(lane < 64, a2, jnp.where(lane < 128, a4, jnp.where(lane < 192, a8, a16)))


def _pool_p(u, row, lane):
    t2 = u + _shift_down(u, 1, row)
    t4 = t2 + _shift_down(t2, 2, row)
    t8 = t4 + _shift_down(t4, 4, row)
    t16 = t8 + _shift_down(t8, 8, row)
    tw = _pool_window_select(lane, t2, t4, t8, t16)
    cnt = jnp.minimum((row + 1).astype(F32), _pool_window_select(lane, 2.0, 4.0, 8.0, 16.0))
    return tw / cnt - u, cnt


def _pool_fwd(z, wp_bd, pscale, name):
    def body(u_ref, w_ref, s_ref, o_ref):
        row = lax.broadcasted_iota(jnp.int32, (S, POOL_W), 0)
        lane = lax.broadcasted_iota(jnp.int32, (S, POOL_W), 1)
        p, _ = _pool_p(u_ref[...], row, lane)
        y = jnp.dot(p.astype(BF16), w_ref[...], preferred_element_type=F32)
        o_ref[...] = y * s_ref[0:1, :]

    return pl.pallas_call(
        body, name=name, out_shape=jax.ShapeDtypeStruct((S, POOL_W), F32), grid=(1,),
        in_specs=[_full_spec((S, POOL_W), (0, Z_PC // POOL_W)), _full_spec((POOL_W, POOL_W)), _full_spec((8, POOL_W))],
        out_specs=_full_spec((S, POOL_W)),
        compiler_params=_params(("arbitrary",)),
    )(z, wp_bd, pscale)


def _pool_bwd(z, wp_bd, pscale, dbr, name):
    def body(u_ref, w_ref, s_ref, dbr_ref, du_ref, dw_ref, red_ref):
        row = lax.broadcasted_iota(jnp.int32, (S, POOL_W), 0)
        lane = lax.broadcasted_iota(jnp.int32, (S, POOL_W), 1)
        p, cnt = _pool_p(u_ref[...], row, lane)
        pb = p.astype(BF16)
        y = jnp.dot(pb, w_ref[...], preferred_element_type=F32)
        dbr = dbr_ref[...]
        red_ref[...] = jnp.zeros_like(red_ref)
        red_ref[0:1, :] = jnp.sum(dbr * y, axis=0, keepdims=True)
        dy = (dbr * s_ref[0:1, :]).astype(BF16)
        dw_ref[...] = lax.dot_general(pb, dy, (((0,), (0,)), ((), ())), preferred_element_type=F32)
        dp = lax.dot_general(dy, w_ref[...], (((1,), (1,)), ((), ())), preferred_element_type=F32)
        g = dp / cnt
        a2 = g + _shift_up(g, 1, row)
        a4 = a2 + _shift_up(a2, 2, row)
        a8 = a4 + _shift_up(a4, 4, row)
        a16 = a8 + _shift_up(a8, 8, row)
        du_ref[...] = (_pool_window_select(lane, a2, a4, a8, a16) - dp).astype(BF16)

    return pl.pallas_call(
        body, name=name,
        out_shape=(jax.ShapeDtypeStruct((S, POOL_W), BF16), jax.ShapeDtypeStruct((POOL_W, POOL_W), F32),
                   jax.ShapeDtypeStruct((8, POOL_W), F32)),
        grid=(1,),
        in_specs=[_full_spec((S, POOL_W), (0, Z_PC // POOL_W)), _full_spec((POOL_W, POOL_W)), _full_spec((8, POOL_W)),
                  _full_spec((S, POOL_W))],
        out_specs=(_full_spec((S, POOL_W)), _full_spec((POOL_W, POOL_W)), _full_spec((8, POOL_W))),
        compiler_params=_params(("arbitrary",)),
    )(z, wp_bd, pscale, dbr)


def _conv_specs():
    base = Z_PC // CONV_W
    return [_full_spec((S, CONV_W), (0, base + 1)), _full_spec((S, CONV_W), (0, base + 2)),
            _full_spec((S, CONV_W), (0, base + 3)), _full_spec((8, CONV_W))]


def _conv_fwd(z, cw, name):
    def body(h_ref, b_ref, c_ref, w_ref, o_ref):
        row = lax.broadcasted_iota(jnp.int32, (S, CONV_W), 0)
        u = c_ref[...] * h_ref[...]
        y = (w_ref[0:1, :] * _shift_down(u, 2, row) + w_ref[1:2, :] * _shift_down(u, 1, row) + w_ref[2:3, :] * u)
        o_ref[...] = b_ref[...] * y

    return pl.pallas_call(
        body, name=name, out_shape=jax.ShapeDtypeStruct((S, CONV_W), F32), grid=(1,),
        in_specs=_conv_specs(), out_specs=_full_spec((S, CONV_W)),
        compiler_params=_params(("arbitrary",)),
    )(z, z, z, cw)


def _conv_bwd(z, cw, dbr, name):
    def body(h_ref, b_ref, c_ref, w_ref, dbr_ref, d_ref, red_ref):
        row = lax.broadcasted_iota(jnp.int32, (S, CONV_W), 0)
        h, cg = h_ref[...], c_ref[...]
        u = cg * h
        u1 = _shift_down(u, 1, row)
        u2 = _shift_down(u, 2, row)
        y = w_ref[0:1, :] * u2 + w_ref[1:2, :] * u1 + w_ref[2:3, :] * u
        dbr = dbr_ref[...]
        dy = dbr * b_ref[...]
        du = w_ref[2:3, :] * dy + w_ref[1:2, :] * _shift_up(dy, 1, row) + w_ref[0:1, :] * _shift_up(dy, 2, row)
        d_ref[:, 0:CONV_W] = (du * cg).astype(BF16)
        d_ref[:, CONV_W:2 * CONV_W] = (dbr * y).astype(BF16)
        d_ref[:, 2 * CONV_W:3 * CONV_W] = (du * h).astype(BF16)
        red_ref[...] = jnp.zeros_like(red_ref)
        red_ref[0:1, :] = jnp.sum(dy * u2, axis=0, keepdims=True)
        red_ref[1:2, :] = jnp.sum(dy * u1, axis=0, keepdims=True)
        red_ref[2:3, :] = jnp.sum(dy * u, axis=0, keepdims=True)

    return pl.pallas_call(
        body, name=name,
        out_shape=(jax.ShapeDtypeStruct((S, 3 * CONV_W), BF16), jax.ShapeDtypeStruct((8, CONV_W), F32)),
        grid=(1,),
        in_specs=_conv_specs() + [_full_spec((S, CONV_W))],
        out_specs=(_full_spec((S, 3 * CONV_W)), _full_spec((8, CONV_W))),
        compiler_params=_params(("arbitrary",)),
    )(z, z, z, cw, dbr)


_NT = (((1,), (1,)), ((), ()))
_TN = (((0,), (0,)), ((), ()))
N_HEAD = 2 * N_PAIR


def _split3(x):
    hi = x.astype(BF16).astype(F32)
    mid = (x - hi).astype(BF16).astype(F32)
    lo = (x - hi - mid).astype(BF16).astype(F32)
    return hi, mid, lo


def _spare(lane, e, k):
    return lane == 64 * (1 - e) + k


def _spare3(lane, e, k):
    base = 64 * (1 - e) + k
    return (lane >= base) & (lane < base + 3)


def _put3(lane, e, k, pieces, rest):
    out = rest
    for n, piece in enumerate(pieces):
        out = jnp.where(_spare(lane, e, k + n), piece, out)
    return out


def _attn_prep(z, bf, name):
    def body(q_ref, k_ref, v_ref, f_ref, b_ref, qa_ref, ka_ref, va_ref, kat_ref, cum_ref):
        p = pl.program_id(0)
        row = lax.broadcasted_iota(jnp.int32, (S, LANE), 0)
        lane = lax.broadcasted_iota(jnp.int32, (S, LANE), 1)

        @pl.when(p == 0)
        def _():
            xv = f_ref[...] + b_ref[0:1, :]
            ls = jnp.minimum(xv, 0.0) - jnp.log(1.0 + jnp.exp(-jnp.abs(xv)))
            cum_ref[...] = _cumsum_rows(jnp.where(lane < N_HEAD, ls, 0.0), row)

        cum = cum_ref[...]
        q, k, v = q_ref[...], k_ref[...], v_ref[...]
        for e in range(2):
            head = (lane >= 64) if e else (lane < 64)
            f = jnp.sum(jnp.where(lane == 2 * p + e, cum, 0.0), axis=1, keepdims=True)
            pieces = _split3(f)
            qa = jnp.where(head, q * ATT_SCALE, _put3(lane, e, 0, pieces, jnp.where(_spare3(lane, e, 3), 1.0, 0.0)))
            ones = jnp.where(_spare3(lane, e, 0) | _spare3(lane, e, 6), 1.0, 0.0)
            ka = jnp.where(head, k, _put3(lane, e, 3, [-x for x in pieces], ones))
            va = jnp.where(head, v, jnp.where(_spare3(lane, e, 0), 1.0, 0.0))
            qa_ref[e] = qa.astype(BF16)
            ka_ref[e] = ka.astype(BF16)
            va_ref[e] = va.astype(BF16)
            kat_ref[e] = ka.T.astype(BF16)

    qb, kb, vb = Z_Q // LANE, Z_K // LANE, Z_V // LANE
    heads = jax.ShapeDtypeStruct((N_HEAD, S, LANE), BF16)
    pair = pl.BlockSpec((2, S, LANE), lambda p: (p, 0, 0))
    return pl.pallas_call(
        body, name=name,
        out_shape=(heads, heads, heads, jax.ShapeDtypeStruct((N_HEAD, LANE, S), BF16)),
        grid=(N_PAIR,),
        in_specs=[pl.BlockSpec((S, LANE), lambda p: (0, qb + p)), pl.BlockSpec((S, LANE), lambda p: (0, kb + p)),
                  pl.BlockSpec((S, LANE), lambda p: (0, vb + p)), pl.BlockSpec((S, LANE), lambda p: (0, Z_F // LANE)),
                  pl.BlockSpec((8, LANE), lambda p: (0, 0))],
        out_specs=(pair, pair, pair, pl.BlockSpec((2, LANE, S), lambda p: (p, 0, 0))),
        scratch_shapes=[pltpu.VMEM((S, LANE), F32)],
        compiler_params=_params(("arbitrary",)),
    )(z, z, z, z, bf)


def _attn_bwd_prep(qa, o, lse, do, name):
    def body(qa_ref, o_ref, lse_ref, do_ref, qa2_ref, doa_ref):
        lane = lax.broadcasted_iota(jnp.int32, (S, LANE), 1)
        dov, ov, lsev = do_ref[...], o_ref[...], lse_ref[...]
        for e in range(2):
            head = (lane >= 64) if e else (lane < 64)
            dsum = jnp.sum(jnp.where(head, dov * ov, 0.0), axis=1, keepdims=True)
            doa_ref[e] = jnp.where(head, dov, _put3(lane, e, 0, [-x for x in _split3(dsum)], 0.0)).astype(BF16)
            lse_col = lsev[:, 64 * e:64 * e + 1]
            qa2_ref[e] = _put3(lane, e, 6, [-x for x in _split3(lse_col)], qa_ref[e].astype(F32)).astype(BF16)

    heads = jax.ShapeDtypeStruct((N_HEAD, S, LANE), BF16)
    pair = pl.BlockSpec((2, S, LANE), lambda p: (p, 0, 0))
    cols = pl.BlockSpec((S, LANE), lambda p: (0, p))
    return pl.pallas_call(
        body, name=name, out_shape=(heads, heads), grid=(N_PAIR,),
        in_specs=[pair, cols, cols, cols], out_specs=(pair, pair),
        compiler_params=_params(("parallel",)),
    )(qa, o, lse, do)


def _attn_bwd_post(z, bf, dqt, dka, dva, name):
    def body(f_ref, b_ref, dqt_ref, dk_ref, dv_ref, dq_out, dk_out, dv_out, dfl_ref, red_ref, dcum_ref):
        p = pl.program_id(0)

        @pl.when(p == 0)
        def _():
            dcum_ref[...] = jnp.zeros_like(dcum_ref)

        row = lax.broadcasted_iota(jnp.int32, (S, LANE), 0)
        lane = lax.broadcasted_iota(jnp.int32, (S, LANE), 1)
        dqa = [dqt_ref[e].T for e in range(2)]
        dq_out[...] = (jnp.where(lane < 64, dqa[0], dqa[1]) * ATT_SCALE).astype(BF16)
        dk_out[...] = jnp.where(lane < 64, dk_ref[0], dk_ref[1]).astype(BF16)
        dv_out[...] = jnp.where(lane < 64, dv_ref[0], dv_ref[1]).astype(BF16)
        for e in range(2):
            d_query = jnp.sum(jnp.where(_spare(lane, e, 0), dqa[e], 0.0), axis=1, keepdims=True)
            d_key = jnp.sum(jnp.where(_spare(lane, e, 3), dk_ref[e], 0.0), axis=1, keepdims=True)
            dcum_ref[...] += jnp.where(lane == 2 * p + e, d_query - d_key, 0.0)

        @pl.when(p == N_PAIR - 1)
        def _():
            dls = _cumsum_rows(dcum_ref[...], row, reverse=True)
            xv = f_ref[...] + b_ref[0:1, :]
            dx = jnp.where(lane < N_HEAD, dls * jax.nn.sigmoid(-xv), 0.0)
            dfl_ref[...] = dx.astype(BF16)
            red_ref[...] = jnp.zeros_like(red_ref)
            red_ref[0:1, :] = jnp.sum(dx, axis=0, keepdims=True)

    wide = jax.ShapeDtypeStruct((S, N_PAIR * LANE), BF16)
    cols = pl.BlockSpec((S, LANE), lambda p: (0, p))
    pair = pl.BlockSpec((2, S, LANE), lambda p: (p, 0, 0))
    return pl.pallas_call(
        body, name=name,
        out_shape=(wide, wide, wide, jax.ShapeDtypeStruct((S, LANE), BF16), jax.ShapeDtypeStruct((8, LANE), F32)),
        grid=(N_PAIR,),
        in_specs=[pl.BlockSpec((S, LANE), lambda p: (0, Z_F // LANE)), pl.BlockSpec((8, LANE), lambda p: (0, 0)),
                  pl.BlockSpec((2, LANE, S), lambda p: (p, 0, 0)), pair, pair],
        out_specs=(cols, cols, cols, pl.BlockSpec((S, LANE), lambda p: (0, 0)), pl.BlockSpec((8, LANE), lambda p: (0, 0))),
        scratch_shapes=[pltpu.VMEM((S, LANE), F32)],
        compiler_params=_params(("arbitrary",)),
    )(z, bf, dqt, dka, dva)


def _attn_fwd(qa, ka, va, name):
    tq, tk = TQ_FWD, TQ
    ratio = tq // tk

    def body(qa_ref, ka_ref, va_ref, o_ref, lse_ref):
        i = pl.program_id(1)
        lane = lax.broadcasted_iota(jnp.int32, (tq, LANE), 1)
        row = lax.broadcasted_iota(jnp.int32, (tq, tk), 0)
        col = lax.broadcasted_iota(jnp.int32, (tq, tk), 1)
        nh = HEADS_PER_STEP_FWD
        qs = [qa_ref[h] for h in range(nh)]

        def block(j, carry, masked):
            off = pl.multiple_of(j * tk, tk)
            out = []
            for h in range(nh):
                m, acc = carry[h]
                s = lax.dot_general(qs[h], ka_ref[h, pl.ds(off, tk), :], _NT, preferred_element_type=F32)
                if masked:
                    s = jnp.where(col + (j - ratio * i) * tk > row, NEG_INF, s)
                mn = jnp.maximum(m, jnp.max(s, axis=1, keepdims=True))
                p = jnp.exp(s - mn).astype(BF16)
                acc = jnp.exp(m - mn) * acc + jnp.dot(p, va_ref[h, pl.ds(off, tk), :], preferred_element_type=F32)
                out.append((mn, acc))
            return tuple(out)

        init = (jnp.full((tq, 1), NEG_INF, F32), jnp.zeros((tq, LANE), F32))
        carry = lax.fori_loop(0, ratio * i, lambda j, c: block(j, c, False), (init,) * nh)
        for d in range(ratio):
            carry = block(ratio * i + d, carry, True)
        res = []
        for h in range(nh):
            m, acc = carry[h]
            l = jnp.sum(jnp.where(_spare(lane, h % 2, 0), acc, 0.0), axis=1, keepdims=True)
            res.append((acc / l, m + jnp.log(l)))
        for g in range(nh // 2):
            o_ref[:, g * LANE:(g + 1) * LANE] = jnp.where(lane < 64, res[2 * g][0], res[2 * g + 1][0])
            lse_ref[:, g * LANE:(g + 1) * LANE] = jnp.where(lane < 64, res[2 * g][1], res[2 * g + 1][1])

    nh = HEADS_PER_STEP_FWD
    out = jax.ShapeDtypeStruct((S, N_PAIR * LANE), F32)
    wide = pl.BlockSpec((tq, 64 * nh), lambda p, i: (i, p))
    return pl.pallas_call(
        body, name=name, out_shape=(out, out), grid=(N_HEAD // nh, S // tq),
        in_specs=[pl.BlockSpec((nh, tq, LANE), lambda p, i: (p, i, 0)), pl.BlockSpec((nh, S, LANE), lambda p, i: (p, 0, 0)),
                  pl.BlockSpec((nh, S, LANE), lambda p, i: (p, 0, 0))],
        out_specs=(wide, wide),
        compiler_params=_params(("parallel", "parallel")),
    )(qa, ka, va)


def _attn_bwd(qa2, ka, va, kat, doa, name):
    nq = S // TQ

    def body(qa_ref, ka_ref, va_ref, kat_ref, doa_ref, dqt_ref, dk_ref, dv_ref):
        j = pl.program_id(1)

        @pl.when(j == 0)
        def _():
            dqt_ref[...] = jnp.zeros_like(dqt_ref)

        key = lax.broadcasted_iota(jnp.int32, (TQ, TQ), 0)
        qry = lax.broadcasted_iota(jnp.int32, (TQ, TQ), 1)
        nh = HEADS_PER_STEP
        kav, vav, katv = ([ref[h] for h in range(nh)] for ref in (ka_ref, va_ref, kat_ref))

        def block(i, carry, masked):
            off = pl.multiple_of(i * TQ, TQ)
            out = []
            for h in range(nh):
                dk_acc, dv_acc = carry[h]
                qav = qa_ref[h, pl.ds(off, TQ), :]
                doav = doa_ref[h, pl.ds(off, TQ), :]
                s_t = lax.dot_general(kav[h], qav, _NT, preferred_element_type=F32)
                if masked:
                    s_t = jnp.where(key > qry, NEG_INF, s_t)
                p_t = jnp.exp(s_t)
                ds_t = p_t * lax.dot_general(vav[h], doav, _NT, preferred_element_type=F32)
                dsb = ds_t.astype(BF16)
                dv_acc = dv_acc + jnp.dot(p_t.astype(BF16), doav, preferred_element_type=F32)
                dk_acc = dk_acc + jnp.dot(dsb, qav, preferred_element_type=F32)
                dqt_ref[h, :, pl.ds(off, TQ)] += jnp.dot(katv[h], dsb, preferred_element_type=F32)
                out.append((dk_acc, dv_acc))
            return tuple(out)

        zero = (jnp.zeros((TQ, LANE), F32), jnp.zeros((TQ, LANE), F32))
        carry = block(j, (zero,) * nh, True)
        carry = lax.fori_loop(j + 1, nq, lambda i, c: block(i, c, False), carry)
        for h in range(nh):
            dk_ref[h], dv_ref[h] = carry[h]

    nh = HEADS_PER_STEP
    full = pl.BlockSpec((nh, S, LANE), lambda p, j: (p, 0, 0))
    blk = pl.BlockSpec((nh, TQ, LANE), lambda p, j: (p, j, 0))
    acc = jax.ShapeDtypeStruct((N_HEAD, S, LANE), F32)
    return pl.pallas_call(
        body, name=name,
        out_shape=(jax.ShapeDtypeStruct((N_HEAD, LANE, S), F32), acc, acc),
        grid=(N_HEAD // nh, nq),
        in_specs=[full, blk, blk, pl.BlockSpec((nh, LANE, TQ), lambda p, j: (p, 0, j)), full],
        out_specs=(pl.BlockSpec((nh, LANE, S), lambda p, j: (p, 0, 0)), blk, blk),
        compiler_params=_params(("arbitrary", "arbitrary")),
    )(qa2, ka, va, kat, doa)


ADA_ROWS = 16


def _ada_fwd(c_pad, w_ada, b_cols, name):
    def body(c_ref, w_ref, b_ref, o_ref):
        cv = c_ref[...]
        sc = (cv * jax.nn.sigmoid(cv)).astype(BF16)
        o_ref[0] = jnp.dot(sc, w_ref[0].astype(BF16), preferred_element_type=F32) + b_ref[0, 0:1, :]

    return pl.pallas_call(
        body, name=name, out_shape=jax.ShapeDtypeStruct((DEPTH, ADA_ROWS, ADA_COLS), F32), grid=(DEPTH,),
        in_specs=[pl.BlockSpec((ADA_ROWS, D), lambda l: (0, 0)), pl.BlockSpec((1, D, ADA_COLS), lambda l: (l, 0, 0)),
                  pl.BlockSpec((1, 8, ADA_COLS), lambda l: (l, 0, 0))],
        out_specs=pl.BlockSpec((1, ADA_ROWS, ADA_COLS), lambda l: (l, 0, 0)),
        compiler_params=_params(("parallel",)),
    )(c_pad, w_ada, b_cols)


def _ada_bwd(c_pad, dmod_cols, name):
    def body(c_ref, d_ref, o_ref):
        cv = c_ref[...]
        sc = (cv * jax.nn.sigmoid(cv)).astype(BF16)
        o_ref[0] = lax.dot_general(sc, d_ref[0].astype(BF16), _TN, preferred_element_type=F32)

    return pl.pallas_call(
        body, name=name, out_shape=jax.ShapeDtypeStruct((DEPTH, D, ADA_COLS), F32), grid=(DEPTH,),
        in_specs=[pl.BlockSpec((ADA_ROWS, D), lambda l: (0, 0)), pl.BlockSpec((1, ADA_ROWS, ADA_COLS), lambda l: (l, 0, 0))],
        out_specs=pl.BlockSpec((1, D, ADA_COLS), lambda l: (l, 0, 0)),
        compiler_params=_params(("parallel",)),
    )(c_pad, dmod_cols)


def _adamw_math(w, g, m, v):
    m = B1 * m + (1.0 - B1) * g
    v = B2 * v + (1.0 - B2) * (g * g)
    m_hat = m / (1.0 - B1 ** STEP)
    v_hat = v / (1.0 - B2 ** STEP)
    delta = -LR * (m_hat / (jnp.sqrt(v_hat) + EPS) + WD * w)
    return delta, m, v


def _row_tile(rows, target=256):
    best = 8
    for t in range(8, min(rows, target) + 1, 8):
        if rows % t == 0:
            best = t
    return best


def _adamw(w, g, m, v, name):
    layers, rows, cols = w.shape
    tr = _row_tile(rows)
    spec = pl.BlockSpec((1, tr, cols), lambda l, i: (l, i, 0))

    def body(w_ref, g_ref, m_ref, v_ref, d_ref, nm_ref, nv_ref):
        d_ref[...], nm_ref[...], nv_ref[...] = _adamw_math(w_ref[...], g_ref[...], m_ref[...], v_ref[...])

    out = jax.ShapeDtypeStruct(w.shape, F32)
    return pl.pallas_call(
        body, name=name, out_shape=(out, out, out), grid=(layers, rows // tr),
        in_specs=[spec] * 4, out_specs=(spec,) * 3, compiler_params=_params(("parallel", "parallel")),
    )(w, g, m, v)


def _sum_slabs(x, name):
    n, rows, _ = x.shape
    tr = _row_tile(rows)

    def body(x_ref, o_ref):
        acc = x_ref[0]
        for k in range(1, n):
            acc = acc + x_ref[k]
        o_ref[...] = acc

    return pl.pallas_call(
        body, name=name, out_shape=jax.ShapeDtypeStruct((rows, D), F32), grid=(rows // tr,),
        in_specs=[pl.BlockSpec((n, tr, D), lambda i: (0, i, 0))], out_specs=pl.BlockSpec((tr, D), lambda i: (i, 0)),
        compiler_params=_params(("parallel",)),
    )(x)


_ANY = pl.BlockSpec(memory_space=pl.ANY)
MESH = pl.DeviceIdType.MESH


def _on_sequencer(body, out_shape, sems, operands, after, sequencer_id, name):
    n = len(operands)

    def ordered_body(*refs):
        body(*refs[:n], *refs[n + 1:])

    extra = [] if after is None else [after]
    return pl.kernel(
        body if after is None else ordered_body, out_type=out_shape,
        mesh=plsc.ScalarSubcoreMesh(axis_name="sequencer", num_cores=1), scratch_types=sems,
        compiler_params=pltpu.CompilerParams(collective_id=sequencer_id), name=name)(*operands, *extra)


def _all_gather(xs, name, sequencer_id=None, after=None):
    n = len(xs)

    def body(*refs):
        x_refs, out_refs = refs[:n], refs[n:2 * n]
        send_sems, recv_sems, local_sems = refs[2 * n:]
        x_, y_, c_ = lax.axis_index("x"), lax.axis_index("y"), lax.axis_index("c")
        me, sibling = (x_, y_, c_), (x_, y_, 1 - c_)
        chips = [(1 - x_, y_), (x_, 1 - y_), (1 - x_, 1 - y_)]
        if sequencer_id is not None:
            barrier = pltpu.get_barrier_semaphore()
            peers = [sibling] + [(*chip, pc) for chip in chips for pc in (c_, 1 - c_)]
            for peer in peers:
                pl.semaphore_signal(barrier, inc=1, device_id=peer, device_id_type=MESH)
            pl.semaphore_wait(barrier, len(peers))

        def slot(a, px, py, pc):
            return out_refs[a].at[4 * px + 2 * py + pc]

        def copy(a, k, block, to, src=None):
            return pltpu.make_async_remote_copy(
                src_ref=slot(a, *block) if src is None else src, dst_ref=slot(a, *block),
                send_sem=send_sems.at[7 * a + k], recv_sem=recv_sems.at[7 * a + k], device_id=to, device_id_type=MESH)

        mine = [pltpu.make_async_copy(x_refs[a], slot(a, *me), local_sems.at[a]) for a in range(n)]
        for cp in mine:
            cp.start()
        first = []
        for a in range(n):
            first.append(copy(a, 0, me, sibling, src=x_refs[a]))
            first += [copy(a, 1 + j, me, (*chip, c_), src=x_refs[a]) for j, chip in enumerate(chips)]
        for cp in first:
            cp.start()
        passed = []
        for j, chip in enumerate(chips):
            for a in range(n):
                copy(a, 1 + j, (*chip, c_), me).wait_recv()
                passed.append(copy(a, 4 + j, (*chip, c_), sibling))
                passed[-1].start()
        for a in range(n):
            copy(a, 0, sibling, me).wait_recv()
        for j, chip in enumerate(chips):
            for a in range(n):
                copy(a, 4 + j, (*chip, 1 - c_), me).wait_recv()
        for cp in first + passed:
            cp.wait_send()
        for cp in mine:
            cp.wait()

    out_shape = [jax.ShapeDtypeStruct((N_DEV,) + x.shape, x.dtype) for x in xs]
    sems = [pltpu.SemaphoreType.DMA((7 * n,)), pltpu.SemaphoreType.DMA((7 * n,)), pltpu.SemaphoreType.DMA((n,))]
    if sequencer_id is not None:
        return _on_sequencer(body, out_shape, sems, xs, after, sequencer_id, name)
    return pl.pallas_call(
        body, name=name, out_shape=out_shape, in_specs=[_ANY] * n, out_specs=[_ANY] * n, scratch_shapes=sems)(*xs)


def _sibling_exchange(gs, name, sequencer_id=None, after=None):
    n = len(gs)

    def body(*refs):
        g_refs, p_refs = refs[:n], refs[n:2 * n]
        send_sems, recv_sems = refs[2 * n:]
        x_, y_, c_ = lax.axis_index("x"), lax.axis_index("y"), lax.axis_index("c")
        if sequencer_id is not None:
            barrier = pltpu.get_barrier_semaphore()
            pl.semaphore_signal(barrier, inc=1, device_id=(x_, y_, 1 - c_), device_id_type=MESH)
            pl.semaphore_wait(barrier, 1)
        copies = [pltpu.make_async_remote_copy(
            src_ref=g_refs[a].at[2 * k + (1 - c_)], dst_ref=p_refs[a].at[k], send_sem=send_sems.at[4 * a + k],
            recv_sem=recv_sems.at[4 * a + k], device_id=(x_, y_, 1 - c_), device_id_type=MESH)
            for a in range(n) for k in range(4)]
        for cp in copies:
            cp.start()
        for cp in copies:
            cp.wait()

    out_shape = [jax.ShapeDtypeStruct((4,) + g.shape[1:], g.dtype) for g in gs]
    sems = [pltpu.SemaphoreType.DMA((4 * n,)), pltpu.SemaphoreType.DMA((4 * n,))]
    if sequencer_id is not None:
        return _on_sequencer(body, out_shape, sems, gs, after, sequencer_id, name)
    return pl.pallas_call(
        body, name=name, out_shape=out_shape, in_specs=[_ANY] * n, out_specs=[_ANY] * n, scratch_shapes=sems)(*gs)


def _slab_tiles(rows, cols):
    if rows % 8 == 0:
        return _row_tile(rows), cols
    return rows, 2 * LANE


def _pair_sums(g, p, route, name):
    _, rows, cols = g.shape
    tr, tc = _slab_tiles(rows, cols)

    def body(route_ref, g_ref, p_ref, t_ref):
        t_ref[...] = (g_ref[...].astype(F32) + p_ref[...].astype(F32)).astype(BF16)

    return pl.pallas_call(
        body, name=name, out_shape=jax.ShapeDtypeStruct((3, rows, cols), BF16),
        grid_spec=pltpu.PrefetchScalarGridSpec(
            num_scalar_prefetch=1, grid=(3, rows // tr, cols // tc),
            in_specs=[pl.BlockSpec((1, tr, tc), lambda r, i, j, route_ref: (2 * route_ref[1 + r] + route_ref[0], i, j)),
                      pl.BlockSpec((1, tr, tc), lambda r, i, j, route_ref: (route_ref[1 + r], i, j))],
            out_specs=pl.BlockSpec((1, tr, tc), lambda r, i, j, route_ref: (r, i, j))),
        compiler_params=_params(("parallel", "parallel", "parallel")),
    )(route, g, p)


def _chip_exchange(ts, name, sequencer_id=None, after=None):
    n = len(ts)

    def body(*refs):
        t_refs, l_refs = refs[:n], refs[n:2 * n]
        send_sems, recv_sems = refs[2 * n:]
        x_, y_, c_ = lax.axis_index("x"), lax.axis_index("y"), lax.axis_index("c")
        chips = [(1 - x_, y_), (x_, 1 - y_), (1 - x_, 1 - y_)]
        if sequencer_id is not None:
            barrier = pltpu.get_barrier_semaphore()
            for px, py in chips:
                pl.semaphore_signal(barrier, inc=1, device_id=(px, py, c_), device_id_type=MESH)
            pl.semaphore_wait(barrier, len(chips))
        copies = [pltpu.make_async_remote_copy(
            src_ref=t_refs[a].at[r], dst_ref=l_refs[a].at[r], send_sem=send_sems.at[3 * a + r],
            recv_sem=recv_sems.at[3 * a + r], device_id=(px, py, c_), device_id_type=MESH)
            for a in range(n) for r, (px, py) in enumerate(chips)]
        for cp in copies:
            cp.start()
        for cp in copies:
            cp.wait()

    out_shape = [jax.ShapeDtypeStruct((3,) + t.shape[1:], t.dtype) for t in ts]
    sems = [pltpu.SemaphoreType.DMA((3 * n,)), pltpu.SemaphoreType.DMA((3 * n,))]
    if sequencer_id is not None:
        return _on_sequencer(body, out_shape, sems, ts, after, sequencer_id, name)
    return pl.pallas_call(
        body, name=name, out_shape=out_shape, in_specs=[_ANY] * n, out_specs=[_ANY] * n, scratch_shapes=sems)(*ts)


def _reduce_adamw(gs, ps, landed, place, w, m, v, name):
    layers, rows, cols = w.shape
    assert layers == DEPTH == 2
    tr, tc = _slab_tiles(rows, cols)
    nr, nc = rows // tr, cols // tc
    spec = pl.BlockSpec((1, tr, tc), lambda l, i, j, place_ref: (l, i, j))

    def own(layer, which):
        pi, pj = (nr - 1, nc - 1) if layer == 0 else (0, 0)

        def index(l, i, j, place_ref):
            lead = 0 if which is None else place_ref[which]
            return lead, jnp.where(l == layer, i, pi), jnp.where(l == layer, j, pj)

        return pl.BlockSpec((3 if which is None else 1, tr, tc), index)

    def body(place_ref, g0_ref, p0_ref, l0_ref, g1_ref, p1_ref, l1_ref, w_ref, m_ref, v_ref,
             g_ref, d_ref, nm_ref, nv_ref):
        def update(own_ref, sib_ref, l_ref):
            g = (own_ref[0].astype(F32) + sib_ref[0].astype(F32) + l_ref[0].astype(F32) + l_ref[1].astype(F32)
                 + l_ref[2].astype(F32))
            g_ref[0] = g
            d_ref[0], nm_ref[0], nv_ref[0] = _adamw_math(w_ref[0], g, m_ref[0], v_ref[0])

        @pl.when(pl.program_id(0) == 0)
        def _():
            update(g0_ref, p0_ref, l0_ref)

        @pl.when(pl.program_id(0) == 1)
        def _():
            update(g1_ref, p1_ref, l1_ref)

    out = jax.ShapeDtypeStruct(w.shape, F32)
    return pl.pallas_call(
        body, name=name, out_shape=(out, out, out, out),
        grid_spec=pltpu.PrefetchScalarGridSpec(
            num_scalar_prefetch=1, grid=(DEPTH, nr, nc),
            in_specs=[own(0, 0), own(0, 1), own(0, None), own(1, 0), own(1, 1), own(1, None), spec, spec, spec],
            out_specs=(spec, spec, spec, spec)),
        compiler_params=_params(("arbitrary", "arbitrary", "arbitrary")),
    )(place, gs[0], ps[0], landed[0], gs[1], ps[1], landed[1], w, m, v)


def _pack(pieces, row_multiple, dtype, cols=D, rows=None):
    flat = jnp.concatenate([p.astype(dtype).reshape(-1) for p in pieces])
    if rows is None:
        rows = -(-flat.shape[0] // cols)
        rows = -(-rows // row_multiple) * row_multiple
    flat = jnp.pad(flat, (0, rows * cols - flat.shape[0]))
    return flat.reshape(rows, cols)


def _unpack(flat, shapes, lead=()):
    out, off = [], 0
    for shp in shapes:
        n = 1
        for s_ in shp:
            n *= s_
        out.append(lax.slice_in_dim(flat, off, off + n, axis=len(lead)).reshape(lead + tuple(shp)))
        off += n
    return out


WIN_STRIDE = 704
WIN_ROWS = 720
Z_TURN = 1544


def _window(wt, me, name):
    padded = jnp.pad(wt, ((0, 0), (0, WIN_ROWS - IN_SHARD), (0, 0)))

    def body(me_ref, x_ref, o_ref):
        o_ref[0] = pltpu.roll(x_ref[0], me_ref[0], axis=0).astype(BF16)

    spec = pl.BlockSpec((1, WIN_ROWS, D), lambda l, me_ref: (l, 0, 0))
    return pl.pallas_call(
        body, name=name, out_shape=jax.ShapeDtypeStruct((DEPTH, WIN_ROWS, D), BF16),
        grid_spec=pltpu.PrefetchScalarGridSpec(num_scalar_prefetch=1, grid=(DEPTH,), in_specs=[spec], out_specs=spec),
        compiler_params=_params(("parallel",)),
    )(me, padded)


def _z_rows_from_windows(win):
    over = WIN_ROWS - WIN_STRIDE
    pieces = [(0, win[0][0:WIN_STRIDE])]
    for d in range(1, N_DEV):
        base = WIN_STRIDE * d
        pieces.append((base, win[d - 1][WIN_STRIDE:WIN_ROWS] + win[d][0:over]))
        pieces.append((base + over, win[d][over:WIN_STRIDE]))
    pieces.append((WIN_STRIDE * N_DEV, win[N_DEV - 1][WIN_STRIDE:WIN_ROWS]))

    def rows(a, b):
        out = []
        for start, arr in pieces:
            lo, hi = max(a, start), min(b, start + arr.shape[0])
            if lo < hi:
                out.append(arr[lo - start:hi - start])
        return out

    pad = jnp.zeros((NZ - IN_COLS, win.shape[-1]), win.dtype)
    return jnp.concatenate(rows(Z_TURN, IN_COLS) + rows(0, Z_TURN) + [pad], axis=0)


def _in_rows_from_z(wt):
    return jnp.concatenate([wt[Z_Q:Z_Q + 1536], wt[Z_F:Z_F + 8], wt[Z_PC:Z_PC + 1024], wt[Z_G:Z_G + 3072]], axis=0)


def _pad_rows(v, rows=8):
    return jnp.pad(v, ((0, rows - v.shape[0]), (0, 0)))


def _layer_fwd(l, x, wts, gvec, mod):
    tag = f"l{l}"
    z, h = _matmul(x, wts["w_in_t"], "nt", f"in_proj_{tag}", tm=1024, tn=1152, prologue=_prenorm_prologue(0, 0, 1),
                   prologue_vecs=[gvec, mod])
    qa, ka, va, kat = _attn_prep(z, wts["b_f"], f"attn_prep_{tag}")
    qa = wts["arrive"](qa)
    o, lse = _attn_fwd(qa, ka, va, f"attn_{tag}")
    br_b = _pool_fwd(z, wts["wp_bd"], wts["pool_scale"], f"pool_{tag}")
    br_c = _conv_fwd(z, wts["conv_w"], f"conv_{tag}")
    pa = _matmul(o, wts["wa"], "nn", f"proj_a_{tag}", out_dtype=BF16)
    pb = _matmul(br_b, wts["wb"], "nn", f"proj_b_{tag}", out_dtype=BF16)
    gates = [(z, Z_G + k * D) for k in range(3)]
    pc, merged = _matmul(br_c, wts["wc"], "nn", f"proj_c_merge_{tag}", tm=1024, tn=512,
                         extra=gates + [(pa, 0), (pb, 0)], epilogue=_merge_epilogue, out_dtypes=(BF16, BF16))
    y, x1 = _matmul(merged, wts["w_out"], "nn", f"out_proj_{tag}", tm=1024, tn=D, extra=[(x, 0)],
                    vec_extra=[gvec, mod], epilogue=_postnorm_epilogue(1, 2), out_dtypes=(F32, F32))
    a, r, h2 = _matmul(x1, wts["w_ff1"], "nn", f"ff1_{tag}", b_col_shards=True, epilogue=_relu2_epilogue,
                       out_dtypes=(BF16, BF16), prologue=_prenorm_prologue(2, 3, 4), prologue_vecs=[gvec, mod])
    y2, x2 = _matmul(r, wts["w_ff2"], "nn", f"ff2_{tag}", tm=1024, tn=D, tk=1024, extra=[(x1, 0)],
                     vec_extra=[gvec, mod], epilogue=_postnorm_epilogue(3, 5), out_dtypes=(F32, F32))
    saved = dict(x=x, h=h, z=z, qa=qa, ka=ka, va=va, kat=kat, o=o, lse=lse, br_b=br_b, br_c=br_c, pa=pa, pb=pb, pc=pc,
                 merged=merged, y=y, x1=x1, h2=h2, a=a, r=r, y2=y2)
    return x2, saved


def _ffn_bwd(l, dx2, sv, wts, gvec, mod, midpoint):
    tag = f"l{l}"
    dx2 = midpoint(dx2)
    da, dy2, sums = _matmul(sv["y2"], wts["w_ff2"], "nt", f"ff2_dx_{tag}", tm=1024, extra=[(sv["a"], 0)],
                            epilogue=_relu2_bwd_epilogue, out_dtypes=(BF16,), prologue=_postnorm_bwd_prologue(3, 5),
                            prologue_tiles=[dx2], prologue_vecs=[gvec, mod], prologue_sums=True)
    red_post_ff = jnp.sum(sums.reshape(-1, 8, D), axis=0)
    d_w_ff2 = _matmul(sv["r"], dy2, "tn", f"ff2_dw_{tag}", out_dtype=GRAD_DTYPE)
    dx1, sums = _matmul(da, wts["w_ff1"], "nt", f"ff1_dx_{tag}", tm=1024, tn=D, b_col_shards=True,
                        extra=[(sv["x1"], 0), (dx2, 0)], vec_extra=[gvec, mod], epilogue=_prenorm_bwd_epilogue(2, 4),
                        out_dtypes=(F32, F32), n_row_sums=1)
    red_pre_ff = jnp.sum(sums.reshape(-1, 8, D), axis=0)
    d_w_ff1 = _matmul(sv["h2"], da, "tn", f"ff1_dw_{tag}", out_dtype=GRAD_DTYPE, out_col_shards=True)
    return dx1, [d_w_ff1, d_w_ff2.reshape(N_DEV, D_FF // N_DEV, D)], (red_pre_ff, red_post_ff)


def _mixer_bwd(l, dx1, sv, wts, gvec, mod, ffn_reds, midpoint):
    tag = f"l{l}"
    red_pre_ff, red_post_ff = ffn_reds
    gates = [(sv["z"], Z_G + k * D) for k in range(3)]
    dpa, dpb, dpc, *dgl, dy, sums = _matmul(
        sv["y"], wts["w_out"], "nt", f"out_proj_dx_{tag}", tm=512, tn=512,
        extra=gates + [(sv["pa"], 0), (sv["pb"], 0), (sv["pc"], 0)], epilogue=_merge_bwd_epilogue,
        out_dtypes=(BF16,) * 6, prologue=_postnorm_bwd_prologue(1, 2), prologue_tiles=[dx1], prologue_vecs=[gvec, mod],
        prologue_sums=True)
    red_post_mix = jnp.sum(sums.reshape(-1, 8, D), axis=0)
    d_w_out = _matmul(sv["merged"], dy, "tn", f"out_proj_dw_{tag}", out_dtype=GRAD_DTYPE)
    dpa = midpoint(dpa)
    do = _matmul(dpa, wts["wa"], "nt", f"proj_a_dx_{tag}")
    dbr_b = _matmul(dpb, wts["wb"], "nt", f"proj_b_dx_{tag}")
    dbr_c = _matmul(dpc, wts["wc"], "nt", f"proj_c_dx_{tag}")
    d_wa = _matmul(sv["o"], dpa, "tn", f"proj_a_dw_{tag}", out_dtype=GRAD_DTYPE)
    d_wb = _matmul(sv["br_b"], dpb, "tn", f"proj_b_dw_{tag}", out_dtype=GRAD_DTYPE)
    d_wc = _matmul(sv["br_c"], dpc, "tn", f"proj_c_dw_{tag}", out_dtype=GRAD_DTYPE)
    d_w_branch = jnp.concatenate([d_wa, d_wb, d_wc], axis=0)

    dpu, d_wp_bd, red_pool = _pool_bwd(sv["z"], wts["wp_bd"], wts["pool_scale"], dbr_b, f"pool_bwd_{tag}")
    dconv, red_conv = _conv_bwd(sv["z"], wts["conv_w"], dbr_c, f"conv_bwd_{tag}")
    qa2, doa = _attn_bwd_prep(sv["qa"], sv["o"], sv["lse"], do, f"attn_bwd_prep_{tag}")
    dqt, dka, dva = _attn_bwd(qa2, sv["ka"], sv["va"], sv["kat"], doa, f"attn_bwd_{tag}")
    dq, dk, dv, dfl, red_f = _attn_bwd_post(sv["z"], wts["b_f"], dqt, dka, dva, f"attn_bwd_post_{tag}")
    dz = _concat_columns([dpu, dconv, *dgl, dq, dk, dv, dfl], f"dz_{tag}")
    dx0, sums = _matmul(dz, wts["w_in_t"], "nn", f"in_proj_dx_{tag}", tm=1024, tn=D, tk=1152,
                        extra=[(sv["x"], 0), (dx1, 0)], vec_extra=[gvec, mod], epilogue=_prenorm_bwd_epilogue(0, 1),
                        out_dtypes=(F32, F32), n_row_sums=1)
    red_pre_mix = jnp.sum(sums.reshape(-1, 8, D), axis=0)
    d_w_in_t = _matmul(dz, sv["h"], "tn", f"in_proj_dw_{tag}", out_dtype=GRAD_DTYPE, tm=1152)

    rows = D // N_DEV
    big = [_in_rows_from_z(d_w_in_t).reshape(N_DEV, IN_SHARD, D), d_w_branch.reshape(N_DEV, rows, D),
           d_w_out.reshape(N_DEV, rows, D)]
    d_w_pool = jnp.stack([d_wp_bd[64 * g:64 * (g + 1), 64 * g:64 * (g + 1)] for g in range(4)])
    small = dict(
        mod=jnp.stack([red_pre_mix[0], red_pre_mix[1], red_post_mix[0], red_pre_ff[0], red_pre_ff[1], red_post_ff[0]]),
        g_mix_pre=red_pre_mix[2], g_mix_post=red_post_mix[1], g_ff_pre=red_pre_ff[2], g_ff_post=red_post_ff[1],
        b_f=red_f[0, 0:8], w_pool=d_w_pool, pool_scale=red_pool[0], conv_w=red_conv[0:3])
    return dx0, big, small


SMALL_KEYS = ["mod", "g_mix_pre", "g_mix_post", "g_ff_pre", "g_ff_post", "b_f", "w_pool", "pool_scale", "conv_w"]
SMALL_SHAPES = [(DEPTH, 6 * D), (DEPTH, D), (DEPTH, D), (DEPTH, D), (DEPTH, D), (DEPTH, 8), (DEPTH, 4, 64, 64),
                (DEPTH, POOL_W), (DEPTH, 3, CONV_W)]


def kernel(x, c, w_ada, b_ada, g_mix_pre, g_mix_post, g_ff_pre, g_ff_post, w_in, b_f, w_pool, pool_scale, conv_w, w_branch, w_out, w_ff1, w_ff2, loss_target, m_w_ada, m_b_ada, m_g_mix_pre, m_g_mix_post, m_g_ff_pre, m_g_ff_post, m_w_in, m_b_f, m_w_pool, m_pool_scale, m_conv_w, m_w_branch, m_w_out, m_w_ff1, m_w_ff2, v_w_ada, v_b_ada, v_g_mix_pre, v_g_mix_post, v_g_ff_pre, v_g_ff_post, v_w_in, v_b_f, v_w_pool, v_pool_scale, v_conv_w, v_w_branch, v_w_out, v_w_ff1, v_w_ff2):
    ix, iy, ic = lax.axis_index("x"), lax.axis_index("y"), lax.axis_index("c")
    me = 4 * ix + 2 * iy + ic
    route = jnp.stack([ic, 2 * (1 - ix) + iy, 2 * ix + (1 - iy), 2 * (1 - ix) + (1 - iy)]).astype(jnp.int32)
    place = jnp.stack([me, 2 * ix + iy]).astype(jnp.int32)
    wt_in, mt_in, vt_in = (jnp.transpose(a, (0, 2, 1)) for a in (w_in, m_w_in, v_w_in))

    c_all = _all_gather([_pad_rows(c)], "gather_c")[0][:, 0, :]
    c_pad = _pad_rows(c_all, ADA_ROWS)
    b_cols = lax.dynamic_slice_in_dim(b_ada, me * ADA_COLS, ADA_COLS, axis=1)
    b_cols = jnp.broadcast_to(b_cols[:, None, :], (DEPTH, 8, ADA_COLS))
    mod_part = _ada_fwd(c_pad, w_ada, b_cols, "ada_fwd")
    mod_all = _all_gather([mod_part.reshape(DEPTH * ADA_ROWS, ADA_COLS)], "gather_mod")[0]
    mod_all = mod_all.reshape(N_DEV, DEPTH, ADA_ROWS, ADA_COLS)
    mod_mine = lax.dynamic_index_in_dim(mod_all, me, axis=2, keepdims=False)
    mod_mine = jnp.transpose(mod_mine, (1, 0, 2)).reshape(DEPTH, 6, D)

    cw_cols = CONV_W // N_DEV
    cw_send = jnp.pad(conv_w.reshape(DEPTH * 3, cw_cols), ((0, 8 - DEPTH * 3), (0, LANE - cw_cols)))
    win_in = _window(wt_in, place[0:1], "w_in_window")
    send = [[w[l].astype(BF16) for w in (win_in, w_branch, w_out, w_ff1, w_ff2)] for l in range(DEPTH)]
    first = _all_gather(send[0][:1], "gather_weights_l0_in", sequencer_id=1, after=mod_all)
    rest = _all_gather(send[0][1:] + [cw_send], "gather_weights_l0_rest", sequencer_id=2, after=first[0])
    first1 = _all_gather(send[1][:1], "gather_weights_l1_in", sequencer_id=3, after=first[0])
    rest1 = _all_gather(send[1][1:], "gather_weights_l1_rest", sequencer_id=12, after=first[0])
    first, (mt_in, vt_in) = lax.optimization_barrier((first, (mt_in, vt_in)))
    gathered = [first + rest[:4], first1 + rest1]
    cw_all = rest[4][:, :DEPTH * 3, :cw_cols].reshape(N_DEV, DEPTH, 3, cw_cols)

    def first_operands(l, p_in):
        wp_bd = jnp.zeros((POOL_W, POOL_W), F32)
        for g in range(4):
            wp_bd = wp_bd.at[64 * g:64 * (g + 1), 64 * g:64 * (g + 1)].set(w_pool[l, g])
        return dict(w_in_t=_z_rows_from_windows(p_in), wp_bd=wp_bd.astype(BF16),
                    pool_scale=_pad_rows(pool_scale[l][None, :]), b_f=_pad_rows(jnp.pad(b_f[l], (0, LANE - 8))[None, :]))

    def rest_operands(l, rest):
        p_br, p_out, p_ff1, p_ff2 = rest
        w_br_full = p_br.reshape(D, D)
        cw_full = jnp.transpose(cw_all[:, l], (1, 0, 2)).reshape(3, CONV_W)
        return dict(wa=w_br_full[0:A_WIDTH], wb=w_br_full[A_WIDTH:A_WIDTH + POOL_W], wc=w_br_full[A_WIDTH + POOL_W:],
                    w_out=p_out.reshape(D, D), w_ff1=p_ff1, w_ff2=p_ff2.reshape(D_FF, D), conv_w=_pad_rows(cw_full))

    xs = x[0]
    saved, layers = [], []
    for l in range(DEPTH):
        p_in, rest = gathered[l][0], gathered[l][1:5]
        if l > 0:
            xs, p_in = lax.optimization_barrier((xs, p_in))
        wts = first_operands(l, p_in)

        def arrive(t, l=l, rest=rest, wts=wts):
            if l > 0:
                t, rest = lax.optimization_barrier((t, rest))
            wts.update(rest_operands(l, rest))
            return t

        wts["arrive"] = arrive
        gvec = _pad_rows(jnp.stack([g_mix_pre[l], g_mix_post[l], g_ff_pre[l], g_ff_post[l]]))
        layers.append((wts, gvec, _pad_rows(mod_mine[l])))
        xs, sv = _layer_fwd(l, xs, *layers[l])
        saved.append(sv)
    dx, loss_part = _loss_head(xs, loss_target[0], "loss_head")
    small_grads = [None] * DEPTH
    mine, sibs, landed = ({} for _ in range(3))
    seq_id = iter(range(4, 4 + 4 * DEPTH))
    last = [gathered[DEPTH - 1][1]]

    def start(group, grads):
        mine[group] = grads
        sibs[group] = _sibling_exchange(grads, f"rs_sibling_{group}", sequencer_id=next(seq_id), after=last[0])
        last[0] = sibs[group][0]

    def finish(group, later):
        later, (grads, sib) = lax.optimization_barrier((later, (mine[group], sibs[group])))
        sends = [_pair_sums(g, p, route, f"rs_pair_sums_{group}_{k}") for k, (g, p) in enumerate(zip(grads, sib))]
        later, sends = lax.optimization_barrier((later, sends))
        landed[group] = _chip_exchange(sends, f"rs_chips_{group}", sequencer_id=next(seq_id), after=last[0])
        last[0] = landed[group][0]
        return later

    pending = None
    for l in reversed(range(DEPTH)):
        hook = (lambda da: da) if pending is None else functools.partial(finish, pending)
        dx, ffn_grads, ffn_reds = _ffn_bwd(l, dx, saved[l], *layers[l], hook)
        start(f"ffn_l{l}", ffn_grads)
        dx, mix_grads, small_grads[l] = _mixer_bwd(l, dx, saved[l], *layers[l], ffn_reds,
                                                   functools.partial(finish, f"ffn_l{l}"))
        start(f"mix_l{l}", mix_grads)
        pending = f"mix_l{l}"
    grad_x = dx[None]

    big_w = [wt_in, w_branch, w_out, w_ff1, w_ff2]
    big_m = [mt_in, m_w_branch, m_w_out, m_w_ff1, m_w_ff2]
    big_v = [vt_in, v_w_branch, v_w_out, v_w_ff1, v_w_ff2]
    where = [("mix", 0), ("mix", 1), ("mix", 2), ("ffn", 0), ("ffn", 1)]

    def reduce_and_update(k):
        group, at = where[k]
        return _reduce_adamw([mine[f"{group}_l{l}"][at] for l in range(DEPTH)],
                             [sibs[f"{group}_l{l}"][at] for l in range(DEPTH)],
                             [landed[f"{group}_l{l}"][at] for l in range(DEPTH)], place, big_w[k], big_m[k], big_v[k],
                             f"rs_sum_adamw_{k}")

    big_res = {k: list(reduce_and_update(k)) for k in (3, 4)}
    big_res[3][0] = finish(pending, big_res[3][0])

    small = {k: jnp.stack([small_grads[l][k] for l in range(DEPTH)]) for k in SMALL_KEYS}
    payload = _pack([small[k] for k in SMALL_KEYS] + [loss_part[0:1, 0:1]], 8, F32)
    small_all = _all_gather([payload], "gather_small")[0]
    dmod_all = small_all[:, 0:DEPTH * 6, :].reshape(N_DEV, DEPTH, 6 * D)
    summed = _unpack(_sum_slabs(small_all, "sum_small").reshape(-1), SMALL_SHAPES + [(1, 1)])
    sg = dict(zip(SMALL_KEYS, summed))
    loss = summed[-1][0, 0]
    dmod_cols = lax.dynamic_slice_in_dim(dmod_all, me * ADA_COLS, ADA_COLS, axis=2)
    dmod_cols = jnp.pad(jnp.transpose(dmod_cols, (1, 0, 2)), ((0, 0), (0, ADA_ROWS - N_DEV), (0, 0)))
    g_w_ada = _ada_bwd(c_pad, dmod_cols, "ada_bwd")
    g_conv_w = lax.dynamic_slice_in_dim(sg["conv_w"], me * (CONV_W // N_DEV), CONV_W // N_DEV, axis=2)

    ada_out = [g_w_ada] + list(_adamw(w_ada, g_w_ada, m_w_ada, v_w_ada, "adamw_ada"))
    rest_w = [b_ada, g_mix_pre, g_mix_post, g_ff_pre, g_ff_post, b_f, w_pool, pool_scale, conv_w]
    rest_m = [m_b_ada, m_g_mix_pre, m_g_mix_post, m_g_ff_pre, m_g_ff_post, m_b_f, m_w_pool, m_pool_scale, m_conv_w]
    rest_v = [v_b_ada, v_g_mix_pre, v_g_mix_post, v_g_ff_pre, v_g_ff_post, v_b_f, v_w_pool, v_pool_scale, v_conv_w]
    rest_g = [sg["mod"], sg["g_mix_pre"], sg["g_mix_post"], sg["g_ff_pre"], sg["g_ff_post"], sg["b_f"],
              sg["w_pool"], sg["pool_scale"], g_conv_w]
    rest_shapes = [a.shape for a in rest_w]
    upd = _adamw(_pack(rest_w, 8, F32)[None], _pack(rest_g, 8, F32)[None], _pack(rest_m, 8, F32)[None],
                 _pack(rest_v, 8, F32)[None], "adamw_rest")
    rest_out = [rest_g] + [_unpack(arr.reshape(-1), rest_shapes) for arr in upd]
    rest_out = [[ada_out[which]] + rest_out[which] for which in range(4)]

    landed[pending], rest_out = lax.optimization_barrier((landed[pending], rest_out))
    big_res.update({k: reduce_and_update(k) for k in (0, 1, 2)})
    big_out = [[jnp.transpose(big_res[k][which], (0, 2, 1)) if k == 0 else big_res[k][which] for k in range(5)]
               for which in range(4)]

    def ordered(k):
        r, b = rest_out[k], big_out[k]
        return [r[0], r[1], r[2], r[3], r[4], r[5], b[0], r[6], r[7], r[8], r[9], b[1], b[2], b[3], b[4]]

    return (loss, grad_x, *ordered(0), *ordered(1), *ordered(2), *ordered(3))
```

```python
import functools

import jax
import jax.numpy as jnp
from jax import lax
from jax.experimental import pallas as pl
from jax.experimental.pallas import tpu as pltpu
from jax.experimental.pallas import tpu_sc as plsc

F32 = jnp.float32
BF16 = jnp.bfloat16
GRAD_DTYPE = BF16

N_DEV = 8
D = 1024
S = 2048
DEPTH = 2
D_FF = 4 * D
A_WIDTH = 512
HEAD_DIM = 64
N_PAIR = 4
POOL_W = 256
CONV_W = 256
IN_COLS = 5640
ADA_COLS = 6 * D // N_DEV
IN_SHARD = IN_COLS // N_DEV
RMS_EPS = 1e-6
NEG_INF = -1e30
ATT_SCALE = HEAD_DIM ** -0.5

NZ = 5760
Z_PC = 0
Z_G = 1024
Z_Q = 4096
Z_K = 4608
Z_V = 5120
Z_F = 5632

LR, B1, B2, EPS, WD, STEP = 0.001, 0.9, 0.999, 1e-08, 0.01, 10

LANE = 128
VMEM_LIMIT_BYTES = 48 * 1024 * 1024
TS = 512
TQ = 256
TQ_FWD = 512
HEADS_PER_STEP = 8
HEADS_PER_STEP_FWD = 8


def _params(sem=None):
    return pltpu.CompilerParams(dimension_semantics=sem, vmem_limit_bytes=VMEM_LIMIT_BYTES)


def _in_hbm(*operands):
    return [pltpu.with_memory_space_constraint(x, pltpu.HBM) for x in operands]


def _pick(n, target):
    best = None
    for t in range(LANE, min(n, target) + 1, LANE):
        if n % t == 0:
            best = t
    return n if best is None else best


def _matmul(a, b, mode, name, out_dtype=F32, tm=2048, tn=1024, tk=2048, b_col_shards=False, out_col_shards=False,
            extra=(), vec_extra=(), epilogue=None, out_dtypes=None, n_row_sums=0, prologue=None, prologue_tiles=(), prologue_vecs=(),
            prologue_sums=False):
    if b_col_shards:
        shards, b_rows, shard_cols = b.shape
        b_shape = (b_rows, shards * shard_cols)
    else:
        b_shape = b.shape
    if mode == "nn":
        (m, k), (k2, n) = a.shape, b_shape
    elif mode == "nt":
        (m, k), (n, k2) = a.shape, b_shape
    else:
        (k, m), (k2, n) = a.shape, b_shape
    assert k == k2, (a.shape, b.shape, mode)
    tm, tn, tk = _pick(m, tm), _pick(n, tn), _pick(k, tk)
    if b_col_shards and mode == "nn":
        tn = shard_cols
    per_step = 1
    if b_col_shards and mode == "nt":
        per_step = max(1, min(tk, 1024) // shard_cols)
        tk = per_step * shard_cols
    if out_col_shards:
        tn = n // N_DEV
    nk = k // tk
    if mode == "nn":
        a_spec = pl.BlockSpec((tm, tk), lambda i, j, kk: (i, kk))
        b_spec = (pl.BlockSpec((None, tk, tn), lambda i, j, kk: (j, kk, 0)) if b_col_shards else
                  pl.BlockSpec((tk, tn), lambda i, j, kk: (kk, j)))
        dims = (((1,), (0,)), ((), ()))
    elif mode == "nt":
        a_spec = pl.BlockSpec((tm, tk), lambda i, j, kk: (i, kk))
        b_spec = (pl.BlockSpec((per_step, tn, shard_cols), lambda i, j, kk: (kk, j, 0)) if b_col_shards else
                  pl.BlockSpec((tn, tk), lambda i, j, kk: (j, kk)))
        dims = (((1,), (1,)), ((), ()))
    else:
        assert not b_col_shards
        a_spec = pl.BlockSpec((tk, tm), lambda i, j, kk: (kk, i))
        b_spec = pl.BlockSpec((tk, tn), lambda i, j, kk: (kk, j))
        dims = (((0,), (0,)), ((), ()))
    if out_col_shards:
        out_shape = jax.ShapeDtypeStruct((N_DEV, m, tn), out_dtype)
        out_spec = pl.BlockSpec((None, tm, tn), lambda i, j, kk: (j, i, 0))
    else:
        out_shape = jax.ShapeDtypeStruct((m, n), out_dtype)
        out_spec = pl.BlockSpec((tm, tn), lambda i, j, kk: (i, j))

    n_extra = len(extra) + len(vec_extra)
    extra_specs = [pl.BlockSpec((tm, tn), lambda i, j, kk, off=off: (i, j + off // tn)) for _, off in extra]
    extra_specs += [pl.BlockSpec((8, tn), lambda i, j, kk: (0, j)) for _ in vec_extra]
    if epilogue is not None:
        assert not out_col_shards and all(off % tn == 0 for _, off in extra)
        out_shape = [jax.ShapeDtypeStruct((m, n), dt) for dt in out_dtypes]
        out_spec = [pl.BlockSpec((tm, tn), lambda i, j, kk: (i, j)) for _ in out_dtypes]
        for at in range(len(out_dtypes) - n_row_sums, len(out_dtypes)):
            out_shape[at] = jax.ShapeDtypeStruct((8 * (m // tm), n), out_dtypes[at])
            out_spec[at] = pl.BlockSpec((8, tn), lambda i, j, kk: (i, j))

    def product(a_ref, b_ref):
        if b_col_shards and mode == "nt":
            b_tile = jnp.concatenate([b_ref[s] for s in range(per_step)], axis=1) if per_step > 1 else b_ref[0]
        else:
            b_tile = b_ref[...]
        return lax.dot_general(a_ref[...].astype(BF16), b_tile.astype(BF16), dims, preferred_element_type=F32)

    def write(acc, extra_refs, o_refs):
        if epilogue is None:
            o_refs[0][...] = acc.astype(out_dtype)
        else:
            for o_ref, tile in zip(o_refs, epilogue(acc, *[r[...] for r in extra_refs])):
                o_ref[...] = tile.astype(o_ref.dtype)

    def body_one_pass(a_ref, b_ref, *refs):
        write(product(a_ref, b_ref), refs[:n_extra], refs[n_extra:])

    if prologue is not None:
        assert nk == 1 and mode in ("nn", "nt")
        n_pro = len(prologue_tiles) + len(prologue_vecs)
        sums = 1 if prologue_sums else 0
        outs = out_shape if isinstance(out_shape, list) else [out_shape]
        out_specs_all = (out_spec if isinstance(out_spec, list) else [out_spec]) + [
            pl.BlockSpec((tm, tk), lambda i, j, kk: (i, 0))]
        outs = outs + [jax.ShapeDtypeStruct((m, k), BF16)]
        if sums:
            outs.append(jax.ShapeDtypeStruct((8 * (m // tm), k), F32))
            out_specs_all.append(pl.BlockSpec((8, tk), lambda i, j, kk: (i, 0)))

        def body_prologue(a_ref, b_ref, *refs):
            pro_refs, rest = refs[:n_pro], refs[n_pro:]
            left_ref = rest[-1]
            left_out = rest[-2 - sums]

            @pl.when(pl.program_id(1) == 0)
            def _():
                made = prologue(a_ref[...], *[r[...] for r in pro_refs])
                left = (made[0] if sums else made).astype(BF16)
                left_ref[...] = left
                left_out[...] = left
                if sums:
                    rest[-2][...] = made[1]

            write(product(left_ref, b_ref), rest[:n_extra], rest[n_extra:-2 - sums])

        pro_specs = [a_spec for _ in prologue_tiles] + [pl.BlockSpec((8, tk), lambda i, j, kk: (0, 0)) for _ in prologue_vecs]
        return pl.pallas_call(
            body_prologue, name=name, out_shape=outs, grid=(m // tm, n // tn, nk),
            in_specs=[a_spec, b_spec] + pro_specs + extra_specs,
            out_specs=out_specs_all,
            scratch_shapes=[pltpu.VMEM((tm, tk), BF16)],
            compiler_params=_params(("parallel", "arbitrary", "arbitrary")),
        )(*_in_hbm(a, b, *prologue_tiles, *prologue_vecs, *[x for x, _ in extra], *vec_extra))

    def body(a_ref, b_ref, *refs):
        acc_ref = refs[-1]
        kk = pl.program_id(2)

        @pl.when(kk == 0)
        def _():
            acc_ref[...] = product(a_ref, b_ref)

        @pl.when(kk > 0)
        def _():
            acc_ref[...] += product(a_ref, b_ref)

        @pl.when(kk == nk - 1)
        def _():
            write(acc_ref[...], refs[:n_extra], refs[n_extra:-1])

    return pl.pallas_call(
        body_one_pass if nk == 1 else body, name=name,
        out_shape=out_shape,
        grid=(m // tm, n // tn, nk),
        in_specs=[a_spec, b_spec] + extra_specs,
        out_specs=out_spec,
        scratch_shapes=[] if nk == 1 else [pltpu.VMEM((tm, tn), F32)],
        compiler_params=_params(("parallel", "parallel", "arbitrary")),
    )(*_in_hbm(a, b, *[x for x, _ in extra], *vec_extra))


def _row_spec(width=D, col=0):
    return pl.BlockSpec((TS, width), lambda i: (i, col))


def _vec_spec(rows=8, width=D):
    return pl.BlockSpec((rows, width), lambda i: (0, 0))


def _rms(x):
    return lax.rsqrt(jnp.mean(x * x, axis=-1, keepdims=True) + RMS_EPS)


def _concat_columns(pieces, name):
    widths = [p.shape[1] for p in pieces]
    offsets = [sum(widths[:k]) for k in range(len(widths))]

    def body(*refs):
        o_ref = refs[-1]
        for ref, off, w in zip(refs[:-1], offsets, widths):
            o_ref[:, off:off + w] = ref[...]

    return pl.pallas_call(
        body, name=name, out_shape=jax.ShapeDtypeStruct((S, sum(widths)), pieces[0].dtype), grid=(S // TS,),
        in_specs=[_row_spec(w) for w in widths], out_specs=_row_spec(sum(widths)),
        compiler_params=_params(("parallel",)),
    )(*_in_hbm(*pieces))


def _loss_head(xf, target, name):
    def body(x_ref, t_ref, dx_ref, loss_ref):
        i = pl.program_id(0)

        @pl.when(i == 0)
        def _():
            loss_ref[...] = jnp.zeros_like(loss_ref)

        e = x_ref[...] - t_ref[...]
        dx_ref[...] = e / float(D)
        per_tok = jnp.mean(e * e, axis=-1, keepdims=True)
        loss_ref[0:1, 0:1] += 0.5 * jnp.sum(per_tok, axis=0, keepdims=True)

    return pl.pallas_call(
        body, name=name,
        out_shape=(jax.ShapeDtypeStruct((S, D), F32), jax.ShapeDtypeStruct((8, LANE), F32)),
        grid=(S // TS,),
        in_specs=[_row_spec(), _row_spec()],
        out_specs=(_row_spec(), pl.BlockSpec((8, LANE), lambda i: (0, 0))),
        compiler_params=_params(("arbitrary",)),
    )(*_in_hbm(xf, target))


def _relu2_epilogue(a):
    t = jnp.maximum(a, 0.0)
    return a, t * t


def _relu2_bwd_epilogue(dr, a):
    return (dr * (2.0 * jnp.maximum(a, 0.0)),)


def _merge_epilogue(pc, g0, g1, g2, pa, pb):
    return pc, jax.nn.sigmoid(g0) * pa + jax.nn.sigmoid(g1) * pb + jax.nn.sigmoid(g2) * pc


def _prenorm_prologue(g_row, shift_row, scale_row):
    def prologue(x, gvec, mod):
        y = x * _rms(x) * gvec[g_row:g_row + 1, :]
        return y * (1.0 + mod[scale_row:scale_row + 1, :]) + mod[shift_row:shift_row + 1, :]

    return prologue


def _rows8(*rows):
    sub = lax.broadcasted_iota(jnp.int32, (8, rows[0].shape[1]), 0)
    out = jnp.zeros((8, rows[0].shape[1]), F32)
    for k, r in enumerate(rows):
        out = jnp.where(sub == k, r, out)
    return out


def _postnorm_bwd_prologue(g_row, gate_row):
    def prologue(y, dxo, gvec, mod):
        g = gvec[g_row:g_row + 1, :]
        r = _rms(y)
        n = y * r
        dyn = dxo * mod[gate_row:gate_row + 1, :]
        dn = dyn * g
        dy = r * (dn - n * jnp.mean(dn * n, axis=-1, keepdims=True))
        return dy, _rows8(jnp.sum(dxo * (n * g), axis=0, keepdims=True), jnp.sum(dyn * n, axis=0, keepdims=True))

    return prologue


def _prenorm_bwd_epilogue(g_row, scale_row):
    def epilogue(dh, x, dres, gvec, mod):
        g = gvec[g_row:g_row + 1, :]
        r = _rms(x)
        n = x * r
        dyg = dh * (1.0 + mod[scale_row:scale_row + 1, :])
        dn = dyg * g
        dx = r * (dn - n * jnp.mean(dn * n, axis=-1, keepdims=True))
        sums = _rows8(jnp.sum(dh, axis=0, keepdims=True), jnp.sum(dh * (n * g), axis=0, keepdims=True),
                      jnp.sum(dyg * n, axis=0, keepdims=True))
        return dres + dx, sums

    return epilogue


def _postnorm_epilogue(g_row, gate_row):
    def epilogue(y, x, gvec, mod):
        yn = y * _rms(y) * gvec[g_row:g_row + 1, :]
        return y, x + mod[gate_row:gate_row + 1, :] * yn

    return epilogue


def _merge_bwd_epilogue(dm, g0, g1, g2, pa, pb, pc):
    sg = [jax.nn.sigmoid(g) for g in (g0, g1, g2)]
    return tuple(dm * s for s in sg) + tuple(dm * p * (s * (1.0 - s)) for p, s in zip((pa, pb, pc), sg))


def _shift_down(x, k, row):
    return jnp.where(row >= k, pltpu.roll(x, k, axis=0), 0.0)


def _shift_up(x, k, row):
    n = x.shape[0]
    return jnp.where(row < n - k, pltpu.roll(x, n - k, axis=0), 0.0)


def _cumsum_rows(x, row, reverse=False):
    shift = _shift_up if reverse else _shift_down
    k = 1
    while k < x.shape[0]:
        x = x + shift(x, k, row)
        k *= 2
    return x


def _full_spec(shape, idx=(0, 0)):
    return pl.BlockSpec(shape, lambda i: idx)


def _pool_window_select(lane, a2, a4, a8, a16):
    return jnp.where(lane < 64, a2, jnp.where(lane < 128, a4, jnp.where(lane < 192, a8, a16)))


def _pool_p(u, row, lane):
    t2 = u + _shift_down(u, 1, row)
    t4 = t2 + _shift_down(t2, 2, row)
    t8 = t4 + _shift_down(t4, 4, row)
    t16 = t8 + _shift_down(t8, 8, row)
    tw = _pool_window_select(lane, t2, t4, t8, t16)
    cnt = jnp.minimum((row + 1).astype(F32), _pool_window_select(lane, 2.0, 4.0, 8.0, 16.0))
    return tw / cnt - u, cnt


def _pool_fwd(z, wp_bd, pscale, name):
    def body(u_ref, w_ref, s_ref, o_ref):
        row = lax.broadcasted_iota(jnp.int32, (S, POOL_W), 0)
        lane = lax.broadcasted_iota(jnp.int32, (S, POOL_W), 1)
        p, _ = _pool_p(u_ref[...], row, lane)
        y = jnp.dot(p.astype(BF16), w_ref[...], preferred_element_type=F32)
        o_ref[...] = y * s_ref[0:1, :]

    return pl.pallas_call(
        body, name=name, out_shape=jax.ShapeDtypeStruct((S, POOL_W), F32), grid=(1,),
        in_specs=[_full_spec((S, POOL_W), (0, Z_PC // POOL_W)), _full_spec((POOL_W, POOL_W)), _full_spec((8, POOL_W))],
        out_specs=_full_spec((S, POOL_W)),
        compiler_params=_params(("arbitrary",)),
    )(*_in_hbm(z, wp_bd, pscale))


def _pool_bwd(z, wp_bd, pscale, dbr, name):
    def body(u_ref, w_ref, s_ref, dbr_ref, du_ref, dw_ref, red_ref):
        row = lax.broadcasted_iota(jnp.int32, (S, POOL_W), 0)
        lane = lax.broadcasted_iota(jnp.int32, (S, POOL_W), 1)
        p, cnt = _pool_p(u_ref[...], row, lane)
        pb = p.astype(BF16)
        y = jnp.dot(pb, w_ref[...], preferred_element_type=F32)
        dbr = dbr_ref[...]
        red_ref[...] = jnp.zeros_like(red_ref)
        red_ref[0:1, :] = jnp.sum(dbr * y, axis=0, keepdims=True)
        dy = (dbr * s_ref[0:1, :]).astype(BF16)
        dw_ref[...] = lax.dot_general(pb, dy, (((0,), (0,)), ((), ())), preferred_element_type=F32)
        dp = lax.dot_general(dy, w_ref[...], (((1,), (1,)), ((), ())), preferred_element_type=F32)
        g = dp / cnt
        a2 = g + _shift_up(g, 1, row)
        a4 = a2 + _shift_up(a2, 2, row)
        a8 = a4 + _shift_up(a4, 4, row)
        a16 = a8 + _shift_up(a8, 8, row)
        du_ref[...] = (_pool_window_select(lane, a2, a4, a8, a16) - dp).astype(BF16)

    return pl.pallas_call(
        body, name=name,
        out_shape=(jax.ShapeDtypeStruct((S, POOL_W), BF16), jax.ShapeDtypeStruct((POOL_W, POOL_W), F32),
                   jax.ShapeDtypeStruct((8, POOL_W), F32)),
        grid=(1,),
        in_specs=[_full_spec((S, POOL_W), (0, Z_PC // POOL_W)), _full_spec((POOL_W, POOL_W)), _full_spec((8, POOL_W)),
                  _full_spec((S, POOL_W))],
        out_specs=(_full_spec((S, POOL_W)), _full_spec((POOL_W, POOL_W)), _full_spec((8, POOL_W))),
        compiler_params=_params(("arbitrary",)),
    )(*_in_hbm(z, wp_bd, pscale, dbr))


def _conv_specs():
    base = Z_PC // CONV_W
    return [_full_spec((S, CONV_W), (0, base + 1)), _full_spec((S, CONV_W), (0, base + 2)),
            _full_spec((S, CONV_W), (0, base + 3)), _full_spec((8, CONV_W))]


def _conv_fwd(z, cw, name):
    def body(h_ref, b_ref, c_ref, w_ref, o_ref):
        row = lax.broadcasted_iota(jnp.int32, (S, CONV_W), 0)
        u = c_ref[...] * h_ref[...]
        y = (w_ref[0:1, :] * _shift_down(u, 2, row) + w_ref[1:2, :] * _shift_down(u, 1, row) + w_ref[2:3, :] * u)
        o_ref[...] = b_ref[...] * y

    return pl.pallas_call(
        body, name=name, out_shape=jax.ShapeDtypeStruct((S, CONV_W), F32), grid=(1,),
        in_specs=_conv_specs(), out_specs=_full_spec((S, CONV_W)),
        compiler_params=_params(("arbitrary",)),
    )(*_in_hbm(z, z, z, cw))


def _conv_bwd(z, cw, dbr, name):
    def body(h_ref, b_ref, c_ref, w_ref, dbr_ref, d_ref, red_ref):
        row = lax.broadcasted_iota(jnp.int32, (S, CONV_W), 0)
        h, cg = h_ref[...], c_ref[...]
        u = cg * h
        u1 = _shift_down(u, 1, row)
        u2 = _shift_down(u, 2, row)
        y = w_ref[0:1, :] * u2 + w_ref[1:2, :] * u1 + w_ref[2:3, :] * u
        dbr = dbr_ref[...]
        dy = dbr * b_ref[...]
        du = w_ref[2:3, :] * dy + w_ref[1:2, :] * _shift_up(dy, 1, row) + w_ref[0:1, :] * _shift_up(dy, 2, row)
        d_ref[:, 0:CONV_W] = (du * cg).astype(BF16)
        d_ref[:, CONV_W:2 * CONV_W] = (dbr * y).astype(BF16)
        d_ref[:, 2 * CONV_W:3 * CONV_W] = (du * h).astype(BF16)
        red_ref[...] = jnp.zeros_like(red_ref)
        red_ref[0:1, :] = jnp.sum(dy * u2, axis=0, keepdims=True)
        red_ref[1:2, :] = jnp.sum(dy * u1, axis=0, keepdims=True)
        red_ref[2:3, :] = jnp.sum(dy * u, axis=0, keepdims=True)

    return pl.pallas_call(
        body, name=name,
        out_shape=(jax.ShapeDtypeStruct((S, 3 * CONV_W), BF16), jax.ShapeDtypeStruct((8, CONV_W), F32)),
        grid=(1,),
        in_specs=_conv_specs() + [_full_spec((S, CONV_W))],
        out_specs=(_full_spec((S, 3 * CONV_W)), _full_spec((8, CONV_W))),
        compiler_params=_params(("arbitrary",)),
    )(*_in_hbm(z, z, z, cw, dbr))


_NT = (((1,), (1,)), ((), ()))
_TN = (((0,), (0,)), ((), ()))
N_HEAD = 2 * N_PAIR


def _split3(x):
    hi = x.astype(BF16).astype(F32)
    mid = (x - hi).astype(BF16).astype(F32)
    lo = (x - hi - mid).astype(BF16).astype(F32)
    return hi, mid, lo


def _spare(lane, e, k):
    return lane == 64 * (1 - e) + k


def _spare3(lane, e, k):
    base = 64 * (1 - e) + k
    return (lane >= base) & (lane < base + 3)


def _put3(lane, e, k, pieces, rest):
    out = rest
    for n, piece in enumerate(pieces):
        out = jnp.where(_spare(lane, e, k + n), piece, out)
    return out


def _attn_prep(z, bf, name):
    def body(q_ref, k_ref, v_ref, f_ref, b_ref, qa_ref, ka_ref, va_ref, kat_ref, cum_ref):
        p = pl.program_id(0)
        row = lax.broadcasted_iota(jnp.int32, (S, LANE), 0)
        lane = lax.broadcasted_iota(jnp.int32, (S, LANE), 1)

        @pl.when(p == 0)
        def _():
            xv = f_ref[...] + b_ref[0:1, :]
            ls = jnp.minimum(xv, 0.0) - jnp.log(1.0 + jnp.exp(-jnp.abs(xv)))
            cum_ref[...] = _cumsum_rows(jnp.where(lane < N_HEAD, ls, 0.0), row)

        cum = cum_ref[...]
        q, k, v = q_ref[...], k_ref[...], v_ref[...]
        for e in range(2):
            head = (lane >= 64) if e else (lane < 64)
            f = jnp.sum(jnp.where(lane == 2 * p + e, cum, 0.0), axis=1, keepdims=True)
            pieces = _split3(f)
            qa = jnp.where(head, q * ATT_SCALE, _put3(lane, e, 0, pieces, jnp.where(_spare3(lane, e, 3), 1.0, 0.0)))
            ones = jnp.where(_spare3(lane, e, 0) | _spare3(lane, e, 6), 1.0, 0.0)
            ka = jnp.where(head, k, _put3(lane, e, 3, [-x for x in pieces], ones))
            va = jnp.where(head, v, jnp.where(_spare3(lane, e, 0), 1.0, 0.0))
            qa_ref[e] = qa.astype(BF16)
            ka_ref[e] = ka.astype(BF16)
            va_ref[e] = va.astype(BF16)
            kat_ref[e] = ka.T.astype(BF16)

    qb, kb, vb = Z_Q // LANE, Z_K // LANE, Z_V // LANE
    heads = jax.ShapeDtypeStruct((N_HEAD, S, LANE), BF16)
    pair = pl.BlockSpec((2, S, LANE), lambda p: (p, 0, 0))
    return pl.pallas_call(
        body, name=name,
        out_shape=(heads, heads, heads, jax.ShapeDtypeStruct((N_HEAD, LANE, S), BF16)),
        grid=(N_PAIR,),
        in_specs=[pl.BlockSpec((S, LANE), lambda p: (0, qb + p)), pl.BlockSpec((S, LANE), lambda p: (0, kb + p)),
                  pl.BlockSpec((S, LANE), lambda p: (0, vb + p)), pl.BlockSpec((S, LANE), lambda p: (0, Z_F // LANE)),
                  pl.BlockSpec((8, LANE), lambda p: (0, 0))],
        out_specs=(pair, pair, pair, pl.BlockSpec((2, LANE, S), lambda p: (p, 0, 0))),
        scratch_shapes=[pltpu.VMEM((S, LANE), F32)],
        compiler_params=_params(("arbitrary",)),
    )(*_in_hbm(z, z, z, z, bf))


def _attn_bwd_prep(qa, o, lse, do, name):
    def body(qa_ref, o_ref, lse_ref, do_ref, qa2_ref, doa_ref):
        lane = lax.broadcasted_iota(jnp.int32, (S, LANE), 1)
        dov, ov, lsev = do_ref[...], o_ref[...], lse_ref[...]
        for e in range(2):
            head = (lane >= 64) if e else (lane < 64)
            dsum = jnp.sum(jnp.where(head, dov * ov, 0.0), axis=1, keepdims=True)
            doa_ref[e] = jnp.where(head, dov, _put3(lane, e, 0, [-x for x in _split3(dsum)], 0.0)).astype(BF16)
            lse_col = lsev[:, 64 * e:64 * e + 1]
            qa2_ref[e] = _put3(lane, e, 6, [-x for x in _split3(lse_col)], qa_ref[e].astype(F32)).astype(BF16)

    heads = jax.ShapeDtypeStruct((N_HEAD, S, LANE), BF16)
    pair = pl.BlockSpec((2, S, LANE), lambda p: (p, 0, 0))
    cols = pl.BlockSpec((S, LANE), lambda p: (0, p))
    return pl.pallas_call(
        body, name=name, out_shape=(heads, heads), grid=(N_PAIR,),
        in_specs=[pair, cols, cols, cols], out_specs=(pair, pair),
        compiler_params=_params(("parallel",)),
    )(*_in_hbm(qa, o, lse, do))


def _attn_bwd_post(z, bf, dqt, dka, dva, name):
    def body(f_ref, b_ref, dqt_ref, dk_ref, dv_ref, dq_out, dk_out, dv_out, dfl_ref, red_ref, dcum_ref):
        p = pl.program_id(0)

        @pl.when(p == 0)
        def _():
            dcum_ref[...] = jnp.zeros_like(dcum_ref)

        row = lax.broadcasted_iota(jnp.int32, (S, LANE), 0)
        lane = lax.broadcasted_iota(jnp.int32, (S, LANE), 1)
        dqa = [dqt_ref[e].T for e in range(2)]
        dq_out[...] = (jnp.where(lane < 64, dqa[0], dqa[1]) * ATT_SCALE).astype(BF16)
        dk_out[...] = jnp.where(lane < 64, dk_ref[0], dk_ref[1]).astype(BF16)
        dv_out[...] = jnp.where(lane < 64, dv_ref[0], dv_ref[1]).astype(BF16)
        for e in range(2):
            d_query = jnp.sum(jnp.where(_spare(lane, e, 0), dqa[e], 0.0), axis=1, keepdims=True)
            d_key = jnp.sum(jnp.where(_spare(lane, e, 3), dk_ref[e], 0.0), axis=1, keepdims=True)
            dcum_ref[...] += jnp.where(lane == 2 * p + e, d_query - d_key, 0.0)

        @pl.when(p == N_PAIR - 1)
        def _():
            dls = _cumsum_rows(dcum_ref[...], row, reverse=True)
            xv = f_ref[...] + b_ref[0:1, :]
            dx = jnp.where(lane < N_HEAD, dls * jax.nn.sigmoid(-xv), 0.0)
            dfl_ref[...] = dx.astype(BF16)
            red_ref[...] = jnp.zeros_like(red_ref)
            red_ref[0:1, :] = jnp.sum(dx, axis=0, keepdims=True)

    wide = jax.ShapeDtypeStruct((S, N_PAIR * LANE), BF16)
    cols = pl.BlockSpec((S, LANE), lambda p: (0, p))
    pair = pl.BlockSpec((2, S, LANE), lambda p: (p, 0, 0))
    return pl.pallas_call(
        body, name=name,
        out_shape=(wide, wide, wide, jax.ShapeDtypeStruct((S, LANE), BF16), jax.ShapeDtypeStruct((8, LANE), F32)),
        grid=(N_PAIR,),
        in_specs=[pl.BlockSpec((S, LANE), lambda p: (0, Z_F // LANE)), pl.BlockSpec((8, LANE), lambda p: (0, 0)),
                  pl.BlockSpec((2, LANE, S), lambda p: (p, 0, 0)), pair, pair],
        out_specs=(cols, cols, cols, pl.BlockSpec((S, LANE), lambda p: (0, 0)), pl.BlockSpec((8, LANE), lambda p: (0, 0))),
        scratch_shapes=[pltpu.VMEM((S, LANE), F32)],
        compiler_params=_params(("arbitrary",)),
    )(*_in_hbm(z, bf, dqt, dka, dva))


def _attn_fwd(qa, ka, va, name):
    tq, tk = TQ_FWD, TQ
    ratio = tq // tk

    def body(qa_ref, ka_ref, va_ref, o_ref, lse_ref):
        i = pl.program_id(1)
        lane = lax.broadcasted_iota(jnp.int32, (tq, LANE), 1)
        row = lax.broadcasted_iota(jnp.int32, (tq, tk), 0)
        col = lax.broadcasted_iota(jnp.int32, (tq, tk), 1)
        nh = HEADS_PER_STEP_FWD
        qs = [qa_ref[h] for h in range(nh)]

        def block(j, carry, masked):
            off = pl.multiple_of(j * tk, tk)
            out = []
            for h in range(nh):
                m, acc = carry[h]
                s = lax.dot_general(qs[h], ka_ref[h, pl.ds(off, tk), :], _NT, preferred_element_type=F32)
                if masked:
                    s = jnp.where(col + (j - ratio * i) * tk > row, NEG_INF, s)
                mn = jnp.maximum(m, jnp.max(s, axis=1, keepdims=True))
                p = jnp.exp(s - mn).astype(BF16)
                acc = jnp.exp(m - mn) * acc + jnp.dot(p, va_ref[h, pl.ds(off, tk), :], preferred_element_type=F32)
                out.append((mn, acc))
            return tuple(out)

        init = (jnp.full((tq, 1), NEG_INF, F32), jnp.zeros((tq, LANE), F32))
        carry = lax.fori_loop(0, ratio * i, lambda j, c: block(j, c, False), (init,) * nh)
        for d in range(ratio):
            carry = block(ratio * i + d, carry, True)
        res = []
        for h in range(nh):
            m, acc = carry[h]
            l = jnp.sum(jnp.where(_spare(lane, h % 2, 0), acc, 0.0), axis=1, keepdims=True)
            res.append((acc / l, m + jnp.log(l)))
        for g in range(nh // 2):
            o_ref[:, g * LANE:(g + 1) * LANE] = jnp.where(lane < 64, res[2 * g][0], res[2 * g + 1][0])
            lse_ref[:, g * LANE:(g + 1) * LANE] = jnp.where(lane < 64, res[2 * g][1], res[2 * g + 1][1])

    nh = HEADS_PER_STEP_FWD
    out = jax.ShapeDtypeStruct((S, N_PAIR * LANE), F32)
    wide = pl.BlockSpec((tq, 64 * nh), lambda p, i: (i, p))
    return pl.pallas_call(
        body, name=name, out_shape=(out, out), grid=(N_HEAD // nh, S // tq),
        in_specs=[pl.BlockSpec((nh, tq, LANE), lambda p, i: (p, i, 0)), pl.BlockSpec((nh, S, LANE), lambda p, i: (p, 0, 0)),
                  pl.BlockSpec((nh, S, LANE), lambda p, i: (p, 0, 0))],
        out_specs=(wide, wide),
        compiler_params=_params(("parallel", "parallel")),
    )(*_in_hbm(qa, ka, va))


def _attn_bwd(qa2, ka, va, kat, doa, name):
    nq = S // TQ

    def body(qa_ref, ka_ref, va_ref, kat_ref, doa_ref, dqt_ref, dk_ref, dv_ref):
        j = pl.program_id(1)

        @pl.when(j == 0)
        def _():
            dqt_ref[...] = jnp.zeros_like(dqt_ref)

        key = lax.broadcasted_iota(jnp.int32, (TQ, TQ), 0)
        qry = lax.broadcasted_iota(jnp.int32, (TQ, TQ), 1)
        nh = HEADS_PER_STEP
        kav, vav, katv = ([ref[h] for h in range(nh)] for ref in (ka_ref, va_ref, kat_ref))

        def block(i, carry, masked):
            off = pl.multiple_of(i * TQ, TQ)
            out = []
            for h in range(nh):
                dk_acc, dv_acc = carry[h]
                qav = qa_ref[h, pl.ds(off, TQ), :]
                doav = doa_ref[h, pl.ds(off, TQ), :]
                s_t = lax.dot_general(kav[h], qav, _NT, preferred_element_type=F32)
                if masked:
                    s_t = jnp.where(key > qry, NEG_INF, s_t)
                p_t = jnp.exp(s_t)
                ds_t = p_t * lax.dot_general(vav[h], doav, _NT, preferred_element_type=F32)
                dsb = ds_t.astype(BF16)
                dv_acc = dv_acc + jnp.dot(p_t.astype(BF16), doav, preferred_element_type=F32)
                dk_acc = dk_acc + jnp.dot(dsb, qav, preferred_element_type=F32)
                dqt_ref[h, :, pl.ds(off, TQ)] += jnp.dot(katv[h], dsb, preferred_element_type=F32)
                out.append((dk_acc, dv_acc))
            return tuple(out)

        zero = (jnp.zeros((TQ, LANE), F32), jnp.zeros((TQ, LANE), F32))
        carry = block(j, (zero,) * nh, True)
        carry = lax.fori_loop(j + 1, nq, lambda i, c: block(i, c, False), carry)
        for h in range(nh):
            dk_ref[h], dv_ref[h] = carry[h]

    nh = HEADS_PER_STEP
    full = pl.BlockSpec((nh, S, LANE), lambda p, j: (p, 0, 0))
    blk = pl.BlockSpec((nh, TQ, LANE), lambda p, j: (p, j, 0))
    acc = jax.ShapeDtypeStruct((N_HEAD, S, LANE), F32)
    return pl.pallas_call(
        body, name=name,
        out_shape=(jax.ShapeDtypeStruct((N_HEAD, LANE, S), F32), acc, acc),
        grid=(N_HEAD // nh, nq),
        in_specs=[full, blk, blk, pl.BlockSpec((nh, LANE, TQ), lambda p, j: (p, 0, j)), full],
        out_specs=(pl.BlockSpec((nh, LANE, S), lambda p, j: (p, 0, 0)), blk, blk),
        compiler_params=_params(("arbitrary", "arbitrary")),
    )(*_in_hbm(qa2, ka, va, kat, doa))


ADA_ROWS = 16


def _ada_fwd(c_pad, w_ada, b_cols, name):
    def body(c_ref, w_ref, b_ref, o_ref):
        cv = c_ref[...]
        sc = (cv * jax.nn.sigmoid(cv)).astype(BF16)
        o_ref[0] = jnp.dot(sc, w_ref[0].astype(BF16), preferred_element_type=F32) + b_ref[0, 0:1, :]

    return pl.pallas_call(
        body, name=name, out_shape=jax.ShapeDtypeStruct((DEPTH, ADA_ROWS, ADA_COLS), F32), grid=(DEPTH,),
        in_specs=[pl.BlockSpec((ADA_ROWS, D), lambda l: (0, 0)), pl.BlockSpec((1, D, ADA_COLS), lambda l: (l, 0, 0)),
                  pl.BlockSpec((1, 8, ADA_COLS), lambda l: (l, 0, 0))],
        out_specs=pl.BlockSpec((1, ADA_ROWS, ADA_COLS), lambda l: (l, 0, 0)),
        compiler_params=_params(("parallel",)),
    )(c_pad, w_ada, b_cols)


def _ada_bwd(c_pad, dmod_cols, name):
    def body(c_ref, d_ref, o_ref):
        cv = c_ref[...]
        sc = (cv * jax.nn.sigmoid(cv)).astype(BF16)
        o_ref[0] = lax.dot_general(sc, d_ref[0].astype(BF16), _TN, preferred_element_type=F32)

    return pl.pallas_call(
        body, name=name, out_shape=jax.ShapeDtypeStruct((DEPTH, D, ADA_COLS), F32), grid=(DEPTH,),
        in_specs=[pl.BlockSpec((ADA_ROWS, D), lambda l: (0, 0)), pl.BlockSpec((1, ADA_ROWS, ADA_COLS), lambda l: (l, 0, 0))],
        out_specs=pl.BlockSpec((1, D, ADA_COLS), lambda l: (l, 0, 0)),
        compiler_params=_params(("parallel",)),
    )(c_pad, dmod_cols)


def _adamw_math(w, g, m, v):
    m = B1 * m + (1.0 - B1) * g
    v = B2 * v + (1.0 - B2) * (g * g)
    m_hat = m / (1.0 - B1 ** STEP)
    v_hat = v / (1.0 - B2 ** STEP)
    delta = -LR * (m_hat / (jnp.sqrt(v_hat) + EPS) + WD * w)
    return delta, m, v


def _row_tile(rows, target=256):
    best = 8
    for t in range(8, min(rows, target) + 1, 8):
        if rows % t == 0:
            best = t
    return best


def _adamw(w, g, m, v, name):
    layers, rows, cols = w.shape
    tr = _row_tile(rows)
    spec = pl.BlockSpec((1, tr, cols), lambda l, i: (l, i, 0))

    def body(w_ref, g_ref, m_ref, v_ref, d_ref, nm_ref, nv_ref):
        d_ref[...], nm_ref[...], nv_ref[...] = _adamw_math(w_ref[...], g_ref[...], m_ref[...], v_ref[...])

    out = jax.ShapeDtypeStruct(w.shape, F32)
    return pl.pallas_call(
        body, name=name, out_shape=(out, out, out), grid=(layers, rows // tr),
        in_specs=[spec] * 4, out_specs=(spec,) * 3, compiler_params=_params(("parallel", "parallel")),
    )(*_in_hbm(w, g, m, v))


def _sum_slabs(x, name):
    n, rows, _ = x.shape
    tr = _row_tile(rows)

    def body(x_ref, o_ref):
        acc = x_ref[0]
        for k in range(1, n):
            acc = acc + x_ref[k]
        o_ref[...] = acc

    return pl.pallas_call(
        body, name=name, out_shape=jax.ShapeDtypeStruct((rows, D), F32), grid=(rows // tr,),
        in_specs=[pl.BlockSpec((n, tr, D), lambda i: (0, i, 0))], out_specs=pl.BlockSpec((tr, D), lambda i: (i, 0)),
        compiler_params=_params(("parallel",)),
    )(x)


_ANY = pl.BlockSpec(memory_space=pl.ANY)
MESH = pl.DeviceIdType.MESH


def _on_sequencer(body, out_shape, sems, operands, after, sequencer_id, name):
    n = len(operands)

    def ordered_body(*refs):
        body(*refs[:n], *refs[n + 1:])

    extra = [] if after is None else [after]
    return pl.kernel(
        body if after is None else ordered_body, out_type=out_shape,
        mesh=plsc.ScalarSubcoreMesh(axis_name="sequencer", num_cores=1), scratch_types=sems,
        compiler_params=pltpu.CompilerParams(collective_id=sequencer_id), name=name)(*operands, *extra)


def _all_gather(xs, name, sequencer_id=None, after=None):
    n = len(xs)

    def body(*refs):
        x_refs, out_refs = refs[:n], refs[n:2 * n]
        send_sems, recv_sems, local_sems = refs[2 * n:]
        x_, y_, c_ = lax.axis_index("x"), lax.axis_index("y"), lax.axis_index("c")
        me, sibling = (x_, y_, c_), (x_, y_, 1 - c_)
        chips = [(1 - x_, y_), (x_, 1 - y_), (1 - x_, 1 - y_)]
        if sequencer_id is not None:
            barrier = pltpu.get_barrier_semaphore()
            peers = [sibling] + [(*chip, pc) for chip in chips for pc in (c_, 1 - c_)]
            for peer in peers:
                pl.semaphore_signal(barrier, inc=1, device_id=peer, device_id_type=MESH)
            pl.semaphore_wait(barrier, len(peers))

        def slot(a, px, py, pc):
            return out_refs[a].at[4 * px + 2 * py + pc]

        def copy(a, k, block, to, src=None):
            return pltpu.make_async_remote_copy(
                src_ref=slot(a, *block) if src is None else src, dst_ref=slot(a, *block),
                send_sem=send_sems.at[7 * a + k], recv_sem=recv_sems.at[7 * a + k], device_id=to, device_id_type=MESH)

        mine = [pltpu.make_async_copy(x_refs[a], slot(a, *me), local_sems.at[a]) for a in range(n)]
        for cp in mine:
            cp.start()
        first = []
        for a in range(n):
            first.append(copy(a, 0, me, sibling, src=x_refs[a]))
            first += [copy(a, 1 + j, me, (*chip, c_), src=x_refs[a]) for j, chip in enumerate(chips)]
        for cp in first:
            cp.start()
        passed = []
        for j, chip in enumerate(chips):
            for a in range(n):
                copy(a, 1 + j, (*chip, c_), me).wait_recv()
                passed.append(copy(a, 4 + j, (*chip, c_), sibling))
                passed[-1].start()
        for a in range(n):
            copy(a, 0, sibling, me).wait_recv()
        for j, chip in enumerate(chips):
            for a in range(n):
                copy(a, 4 + j, (*chip, 1 - c_), me).wait_recv()
        for cp in first + passed:
            cp.wait_send()
        for cp in mine:
            cp.wait()

    out_shape = [jax.ShapeDtypeStruct((N_DEV,) + x.shape, x.dtype) for x in xs]
    sems = [pltpu.SemaphoreType.DMA((7 * n,)), pltpu.SemaphoreType.DMA((7 * n,)), pltpu.SemaphoreType.DMA((n,))]
    if sequencer_id is not None:
        return _on_sequencer(body, out_shape, sems, xs, after, sequencer_id, name)
    return pl.pallas_call(
        body, name=name, out_shape=out_shape, in_specs=[_ANY] * n, out_specs=[_ANY] * n, scratch_shapes=sems)(*xs)


def _sibling_exchange(gs, name, sequencer_id=None, after=None):
    n = len(gs)

    def body(*refs):
        g_refs, p_refs = refs[:n], refs[n:2 * n]
        send_sems, recv_sems = refs[2 * n:]
        x_, y_, c_ = lax.axis_index("x"), lax.axis_index("y"), lax.axis_index("c")
        if sequencer_id is not None:
            barrier = pltpu.get_barrier_semaphore()
            pl.semaphore_signal(barrier, inc=1, device_id=(x_, y_, 1 - c_), device_id_type=MESH)
            pl.semaphore_wait(barrier, 1)
        copies = [pltpu.make_async_remote_copy(
            src_ref=g_refs[a].at[2 * k + (1 - c_)], dst_ref=p_refs[a].at[k], send_sem=send_sems.at[4 * a + k],
            recv_sem=recv_sems.at[4 * a + k], device_id=(x_, y_, 1 - c_), device_id_type=MESH)
            for a in range(n) for k in range(4)]
        for cp in copies:
            cp.start()
        for cp in copies:
            cp.wait()

    out_shape = [jax.ShapeDtypeStruct((4,) + g.shape[1:], g.dtype) for g in gs]
    sems = [pltpu.SemaphoreType.DMA((4 * n,)), pltpu.SemaphoreType.DMA((4 * n,))]
    if sequencer_id is not None:
        return _on_sequencer(body, out_shape, sems, gs, after, sequencer_id, name)
    return pl.pallas_call(
        body, name=name, out_shape=out_shape, in_specs=[_ANY] * n, out_specs=[_ANY] * n, scratch_shapes=sems)(*gs)


def _slab_tiles(rows, cols):
    if rows % 8 == 0:
        return _row_tile(rows), cols
    return rows, 2 * LANE


def _pair_sums(g, p, route, name):
    _, rows, cols = g.shape
    tr, tc = _slab_tiles(rows, cols)

    def body(route_ref, g_ref, p_ref, t_ref):
        t_ref[...] = (g_ref[...].astype(F32) + p_ref[...].astype(F32)).astype(BF16)

    return pl.pallas_call(
        body, name=name, out_shape=jax.ShapeDtypeStruct((3, rows, cols), BF16),
        grid_spec=pltpu.PrefetchScalarGridSpec(
            num_scalar_prefetch=1, grid=(3, rows // tr, cols // tc),
            in_specs=[pl.BlockSpec((1, tr, tc), lambda r, i, j, route_ref: (2 * route_ref[1 + r] + route_ref[0], i, j)),
                      pl.BlockSpec((1, tr, tc), lambda r, i, j, route_ref: (route_ref[1 + r], i, j))],
            out_specs=pl.BlockSpec((1, tr, tc), lambda r, i, j, route_ref: (r, i, j))),
        compiler_params=_params(("parallel", "parallel", "parallel")),
    )(route, *_in_hbm(g, p))


def _chip_exchange(ts, name, sequencer_id=None, after=None):
    n = len(ts)

    def body(*refs):
        t_refs, l_refs = refs[:n], refs[n:2 * n]
        send_sems, recv_sems = refs[2 * n:]
        x_, y_, c_ = lax.axis_index("x"), lax.axis_index("y"), lax.axis_index("c")
        chips = [(1 - x_, y_), (x_, 1 - y_), (1 - x_, 1 - y_)]
        if sequencer_id is not None:
            barrier = pltpu.get_barrier_semaphore()
            for px, py in chips:
                pl.semaphore_signal(barrier, inc=1, device_id=(px, py, c_), device_id_type=MESH)
            pl.semaphore_wait(barrier, len(chips))
        copies = [pltpu.make_async_remote_copy(
            src_ref=t_refs[a].at[r], dst_ref=l_refs[a].at[r], send_sem=send_sems.at[3 * a + r],
            recv_sem=recv_sems.at[3 * a + r], device_id=(px, py, c_), device_id_type=MESH)
            for a in range(n) for r, (px, py) in enumerate(chips)]
        for cp in copies:
            cp.start()
        for cp in copies:
            cp.wait()

    out_shape = [jax.ShapeDtypeStruct((3,) + t.shape[1:], t.dtype) for t in ts]
    sems = [pltpu.SemaphoreType.DMA((3 * n,)), pltpu.SemaphoreType.DMA((3 * n,))]
    if sequencer_id is not None:
        return _on_sequencer(body, out_shape, sems, ts, after, sequencer_id, name)
    return pl.pallas_call(
        body, name=name, out_shape=out_shape, in_specs=[_ANY] * n, out_specs=[_ANY] * n, scratch_shapes=sems)(*ts)


def _reduce_adamw(gs, ps, landed, place, w, m, v, name):
    layers, rows, cols = w.shape
    assert layers == DEPTH == 2
    tr, tc = _slab_tiles(rows, cols)
    nr, nc = rows // tr, cols // tc
    spec = pl.BlockSpec((1, tr, tc), lambda l, i, j, place_ref: (l, i, j))

    def own(layer, which):
        pi, pj = (nr - 1, nc - 1) if layer == 0 else (0, 0)

        def index(l, i, j, place_ref):
            lead = 0 if which is None else place_ref[which]
            return lead, jnp.where(l == layer, i, pi), jnp.where(l == layer, j, pj)

        return pl.BlockSpec((3 if which is None else 1, tr, tc), index)

    def body(place_ref, g0_ref, p0_ref, l0_ref, g1_ref, p1_ref, l1_ref, w_ref, m_ref, v_ref,
             g_ref, d_ref, nm_ref, nv_ref):
        def update(own_ref, sib_ref, l_ref):
            g = (own_ref[0].astype(F32) + sib_ref[0].astype(F32) + l_ref[0].astype(F32) + l_ref[1].astype(F32)
                 + l_ref[2].astype(F32))
            g_ref[0] = g
            d_ref[0], nm_ref[0], nv_ref[0] = _adamw_math(w_ref[0], g, m_ref[0], v_ref[0])

        @pl.when(pl.program_id(0) == 0)
        def _():
            update(g0_ref, p0_ref, l0_ref)

        @pl.when(pl.program_id(0) == 1)
        def _():
            update(g1_ref, p1_ref, l1_ref)

    out = jax.ShapeDtypeStruct(w.shape, F32)
    return pl.pallas_call(
        body, name=name, out_shape=(out, out, out, out),
        grid_spec=pltpu.PrefetchScalarGridSpec(
            num_scalar_prefetch=1, grid=(DEPTH, nr, nc),
            in_specs=[own(0, 0), own(0, 1), own(0, None), own(1, 0), own(1, 1), own(1, None), spec, spec, spec],
            out_specs=(spec, spec, spec, spec)),
        compiler_params=_params(("arbitrary", "arbitrary", "arbitrary")),
    )(place, *_in_hbm(gs[0], ps[0], landed[0], gs[1], ps[1], landed[1], w, m, v))


def _pack(pieces, row_multiple, dtype, cols=D, rows=None):
    flat = jnp.concatenate([p.astype(dtype).reshape(-1) for p in pieces])
    if rows is None:
        rows = -(-flat.shape[0] // cols)
        rows = -(-rows // row_multiple) * row_multiple
    flat = jnp.pad(flat, (0, rows * cols - flat.shape[0]))
    return flat.reshape(rows, cols)


def _unpack(flat, shapes, lead=()):
    out, off = [], 0
    for shp in shapes:
        n = 1
        for s_ in shp:
            n *= s_
        out.append(lax.slice_in_dim(flat, off, off + n, axis=len(lead)).reshape(lead + tuple(shp)))
        off += n
    return out


WIN_STRIDE = 704
WIN_ROWS = 720
Z_TURN = 1544


def _window(wt, me, name):
    padded = jnp.pad(wt, ((0, 0), (0, WIN_ROWS - IN_SHARD), (0, 0)))

    def body(me_ref, x_ref, o_ref):
        o_ref[0] = pltpu.roll(x_ref[0], me_ref[0], axis=0).astype(BF16)

    spec = pl.BlockSpec((1, WIN_ROWS, D), lambda l, me_ref: (l, 0, 0))
    return pl.pallas_call(
        body, name=name, out_shape=jax.ShapeDtypeStruct((DEPTH, WIN_ROWS, D), BF16),
        grid_spec=pltpu.PrefetchScalarGridSpec(num_scalar_prefetch=1, grid=(DEPTH,), in_specs=[spec], out_specs=spec),
        compiler_params=_params(("parallel",)),
    )(me, padded)


def _z_rows_from_windows(win):
    over = WIN_ROWS - WIN_STRIDE
    pieces = [(0, win[0][0:WIN_STRIDE])]
    for d in range(1, N_DEV):
        base = WIN_STRIDE * d
        pieces.append((base, win[d - 1][WIN_STRIDE:WIN_ROWS] + win[d][0:over]))
        pieces.append((base + over, win[d][over:WIN_STRIDE]))
    pieces.append((WIN_STRIDE * N_DEV, win[N_DEV - 1][WIN_STRIDE:WIN_ROWS]))

    def rows(a, b):
        out = []
        for start, arr in pieces:
            lo, hi = max(a, start), min(b, start + arr.shape[0])
            if lo < hi:
                out.append(arr[lo - start:hi - start])
        return out

    pad = jnp.zeros((NZ - IN_COLS, win.shape[-1]), win.dtype)
    return jnp.concatenate(rows(Z_TURN, IN_COLS) + rows(0, Z_TURN) + [pad], axis=0)


def _in_rows_from_z(wt):
    return jnp.concatenate([wt[Z_Q:Z_Q + 1536], wt[Z_F:Z_F + 8], wt[Z_PC:Z_PC + 1024], wt[Z_G:Z_G + 3072]], axis=0)


def _pad_rows(v, rows=8):
    return jnp.pad(v, ((0, rows - v.shape[0]), (0, 0)))


def _layer_fwd(l, x, wts, gvec, mod):
    tag = f"l{l}"
    z, h = _matmul(x, wts["w_in_t"], "nt", f"in_proj_{tag}", tm=1024, tn=1152, prologue=_prenorm_prologue(0, 0, 1),
                   prologue_vecs=[gvec, mod])
    qa, ka, va, kat = _attn_prep(z, wts["b_f"], f"attn_prep_{tag}")
    qa = wts["arrive"](qa)
    o, lse = _attn_fwd(qa, ka, va, f"attn_{tag}")
    br_b = _pool_fwd(z, wts["wp_bd"], wts["pool_scale"], f"pool_{tag}")
    br_c = _conv_fwd(z, wts["conv_w"], f"conv_{tag}")
    pa = _matmul(o, wts["wa"], "nn", f"proj_a_{tag}", out_dtype=BF16)
    pb = _matmul(br_b, wts["wb"], "nn", f"proj_b_{tag}", out_dtype=BF16)
    gates = [(z, Z_G + k * D) for k in range(3)]
    pc, merged = _matmul(br_c, wts["wc"], "nn", f"proj_c_merge_{tag}", tm=1024, tn=512,
                         extra=gates + [(pa, 0), (pb, 0)], epilogue=_merge_epilogue, out_dtypes=(BF16, BF16))
    y, x1 = _matmul(merged, wts["w_out"], "nn", f"out_proj_{tag}", tm=1024, tn=D, extra=[(x, 0)],
                    vec_extra=[gvec, mod], epilogue=_postnorm_epilogue(1, 2), out_dtypes=(F32, F32))
    a, r, h2 = _matmul(x1, wts["w_ff1"], "nn", f"ff1_{tag}", b_col_shards=True, epilogue=_relu2_epilogue,
                       out_dtypes=(BF16, BF16), prologue=_prenorm_prologue(2, 3, 4), prologue_vecs=[gvec, mod])
    y2, x2 = _matmul(r, wts["w_ff2"], "nn", f"ff2_{tag}", tm=1024, tn=D, tk=1024, extra=[(x1, 0)],
                     vec_extra=[gvec, mod], epilogue=_postnorm_epilogue(3, 5), out_dtypes=(F32, F32))
    saved = dict(x=x, h=h, z=z, qa=qa, ka=ka, va=va, kat=kat, o=o, lse=lse, br_b=br_b, br_c=br_c, pa=pa, pb=pb, pc=pc,
                 merged=merged, y=y, x1=x1, h2=h2, a=a, r=r, y2=y2)
    return x2, saved


def _ffn_bwd(l, dx2, sv, wts, gvec, mod, midpoint):
    tag = f"l{l}"
    dx2 = midpoint(dx2)
    da, dy2, sums = _matmul(sv["y2"], wts["w_ff2"], "nt", f"ff2_dx_{tag}", tm=1024, extra=[(sv["a"], 0)],
                            epilogue=_relu2_bwd_epilogue, out_dtypes=(BF16,), prologue=_postnorm_bwd_prologue(3, 5),
                            prologue_tiles=[dx2], prologue_vecs=[gvec, mod], prologue_sums=True)
    red_post_ff = jnp.sum(sums.reshape(-1, 8, D), axis=0)
    d_w_ff2 = _matmul(sv["r"], dy2, "tn", f"ff2_dw_{tag}", out_dtype=GRAD_DTYPE)
    dx1, sums = _matmul(da, wts["w_ff1"], "nt", f"ff1_dx_{tag}", tm=1024, tn=D, b_col_shards=True,
                        extra=[(sv["x1"], 0), (dx2, 0)], vec_extra=[gvec, mod], epilogue=_prenorm_bwd_epilogue(2, 4),
                        out_dtypes=(F32, F32), n_row_sums=1)
    red_pre_ff = jnp.sum(sums.reshape(-1, 8, D), axis=0)
    d_w_ff1 = _matmul(sv["h2"], da, "tn", f"ff1_dw_{tag}", out_dtype=GRAD_DTYPE, out_col_shards=True)
    return dx1, [d_w_ff1, d_w_ff2.reshape(N_DEV, D_FF // N_DEV, D)], (red_pre_ff, red_post_ff)


def _mixer_bwd(l, dx1, sv, wts, gvec, mod, ffn_reds, midpoint):
    tag = f"l{l}"
    red_pre_ff, red_post_ff = ffn_reds
    gates = [(sv["z"], Z_G + k * D) for k in range(3)]
    dpa, dpb, dpc, *dgl, dy, sums = _matmul(
        sv["y"], wts["w_out"], "nt", f"out_proj_dx_{tag}", tm=512, tn=512,
        extra=gates + [(sv["pa"], 0), (sv["pb"], 0), (sv["pc"], 0)], epilogue=_merge_bwd_epilogue,
        out_dtypes=(BF16,) * 6, prologue=_postnorm_bwd_prologue(1, 2), prologue_tiles=[dx1], prologue_vecs=[gvec, mod],
        prologue_sums=True)
    red_post_mix = jnp.sum(sums.reshape(-1, 8, D), axis=0)
    d_w_out = _matmul(sv["merged"], dy, "tn", f"out_proj_dw_{tag}", out_dtype=GRAD_DTYPE)
    dpa = midpoint(dpa)
    do = _matmul(dpa, wts["wa"], "nt", f"proj_a_dx_{tag}")
    dbr_b = _matmul(dpb, wts["wb"], "nt", f"proj_b_dx_{tag}")
    dbr_c = _matmul(dpc, wts["wc"], "nt", f"proj_c_dx_{tag}")
    d_wa = _matmul(sv["o"], dpa, "tn", f"proj_a_dw_{tag}", out_dtype=GRAD_DTYPE)
    d_wb = _matmul(sv["br_b"], dpb, "tn", f"proj_b_dw_{tag}", out_dtype=GRAD_DTYPE)
    d_wc = _matmul(sv["br_c"], dpc, "tn", f"proj_c_dw_{tag}", out_dtype=GRAD_DTYPE)
    d_w_branch = jnp.concatenate([d_wa, d_wb, d_wc], axis=0)

    dpu, d_wp_bd, red_pool = _pool_bwd(sv["z"], wts["wp_bd"], wts["pool_scale"], dbr_b, f"pool_bwd_{tag}")
    dconv, red_conv = _conv_bwd(sv["z"], wts["conv_w"], dbr_c, f"conv_bwd_{tag}")
    qa2, doa = _attn_bwd_prep(sv["qa"], sv["o"], sv["lse"], do, f"attn_bwd_prep_{tag}")
    dqt, dka, dva = _attn_bwd(qa2, sv["ka"], sv["va"], sv["kat"], doa, f"attn_bwd_{tag}")
    dq, dk, dv, dfl, red_f = _attn_bwd_post(sv["z"], wts["b_f"], dqt, dka, dva, f"attn_bwd_post_{tag}")
    dz = _concat_columns([dpu, dconv, *dgl, dq, dk, dv, dfl], f"dz_{tag}")
    dx0, sums = _matmul(dz, wts["w_in_t"], "nn", f"in_proj_dx_{tag}", tm=1024, tn=D, tk=1152,
                        extra=[(sv["x"], 0), (dx1, 0)], vec_extra=[gvec, mod], epilogue=_prenorm_bwd_epilogue(0, 1),
                        out_dtypes=(F32, F32), n_row_sums=1)
    red_pre_mix = jnp.sum(sums.reshape(-1, 8, D), axis=0)
    d_w_in_t = _matmul(dz, sv["h"], "tn", f"in_proj_dw_{tag}", out_dtype=GRAD_DTYPE, tm=1152)

    rows = D // N_DEV
    big = [_in_rows_from_z(d_w_in_t).reshape(N_DEV, IN_SHARD, D), d_w_branch.reshape(N_DEV, rows, D),
           d_w_out.reshape(N_DEV, rows, D)]
    d_w_pool = jnp.stack([d_wp_bd[64 * g:64 * (g + 1), 64 * g:64 * (g + 1)] for g in range(4)])
    small = dict(
        mod=jnp.stack([red_pre_mix[0], red_pre_mix[1], red_post_mix[0], red_pre_ff[0], red_pre_ff[1], red_post_ff[0]]),
        g_mix_pre=red_pre_mix[2], g_mix_post=red_post_mix[1], g_ff_pre=red_pre_ff[2], g_ff_post=red_post_ff[1],
        b_f=red_f[0, 0:8], w_pool=d_w_pool, pool_scale=red_pool[0], conv_w=red_conv[0:3])
    return dx0, big, small


SMALL_KEYS = ["mod", "g_mix_pre", "g_mix_post", "g_ff_pre", "g_ff_post", "b_f", "w_pool", "pool_scale", "conv_w"]
SMALL_SHAPES = [(DEPTH, 6 * D), (DEPTH, D), (DEPTH, D), (DEPTH, D), (DEPTH, D), (DEPTH, 8), (DEPTH, 4, 64, 64),
                (DEPTH, POOL_W), (DEPTH, 3, CONV_W)]


def kernel(x, c, w_ada, b_ada, g_mix_pre, g_mix_post, g_ff_pre, g_ff_post, w_in, b_f, w_pool, pool_scale, conv_w, w_branch, w_out, w_ff1, w_ff2, loss_target, m_w_ada, m_b_ada, m_g_mix_pre, m_g_mix_post, m_g_ff_pre, m_g_ff_post, m_w_in, m_b_f, m_w_pool, m_pool_scale, m_conv_w, m_w_branch, m_w_out, m_w_ff1, m_w_ff2, v_w_ada, v_b_ada, v_g_mix_pre, v_g_mix_post, v_g_ff_pre, v_g_ff_post, v_w_in, v_b_f, v_w_pool, v_pool_scale, v_conv_w, v_w_branch, v_w_out, v_w_ff1, v_w_ff2):
    ix, iy, ic = lax.axis_index("x"), lax.axis_index("y"), lax.axis_index("c")
    me = 4 * ix + 2 * iy + ic
    route = jnp.stack([ic, 2 * (1 - ix) + iy, 2 * ix + (1 - iy), 2 * (1 - ix) + (1 - iy)]).astype(jnp.int32)
    place = jnp.stack([me, 2 * ix + iy]).astype(jnp.int32)
    wt_in, mt_in, vt_in = (jnp.transpose(a, (0, 2, 1)) for a in (w_in, m_w_in, v_w_in))

    c_all = _all_gather([_pad_rows(c)], "gather_c")[0][:, 0, :]
    c_pad = _pad_rows(c_all, ADA_ROWS)
    b_cols = lax.dynamic_slice_in_dim(b_ada, me * ADA_COLS, ADA_COLS, axis=1)
    b_cols = jnp.broadcast_to(b_cols[:, None, :], (DEPTH, 8, ADA_COLS))
    mod_part = _ada_fwd(c_pad, w_ada, b_cols, "ada_fwd")
    mod_all = _all_gather([mod_part.reshape(DEPTH * ADA_ROWS, ADA_COLS)], "gather_mod")[0]
    mod_all = mod_all.reshape(N_DEV, DEPTH, ADA_ROWS, ADA_COLS)
    mod_mine = lax.dynamic_index_in_dim(mod_all, me, axis=2, keepdims=False)
    mod_mine = jnp.transpose(mod_mine, (1, 0, 2)).reshape(DEPTH, 6, D)

    cw_cols = CONV_W // N_DEV
    cw_send = jnp.pad(conv_w.reshape(DEPTH * 3, cw_cols), ((0, 8 - DEPTH * 3), (0, LANE - cw_cols)))
    win_in = _window(wt_in, place[0:1], "w_in_window")
    send = [[w[l].astype(BF16) for w in (win_in, w_branch, w_out, w_ff1, w_ff2)] for l in range(DEPTH)]
    first = _all_gather(send[0][:1], "gather_weights_l0_in", sequencer_id=1, after=mod_all)
    rest = _all_gather(send[0][1:] + [cw_send], "gather_weights_l0_rest", sequencer_id=2, after=first[0])
    first1 = _all_gather(send[1][:1], "gather_weights_l1_in", sequencer_id=3, after=first[0])
    rest1 = _all_gather(send[1][1:], "gather_weights_l1_rest", sequencer_id=12, after=first[0])
    first, (mt_in, vt_in) = lax.optimization_barrier((first, (mt_in, vt_in)))
    gathered = [first + rest[:4], first1 + rest1]
    cw_all = rest[4][:, :DEPTH * 3, :cw_cols].reshape(N_DEV, DEPTH, 3, cw_cols)

    def first_operands(l, p_in):
        wp_bd = jnp.zeros((POOL_W, POOL_W), F32)
        for g in range(4):
            wp_bd = wp_bd.at[64 * g:64 * (g + 1), 64 * g:64 * (g + 1)].set(w_pool[l, g])
        return dict(w_in_t=_z_rows_from_windows(p_in), wp_bd=wp_bd.astype(BF16),
                    pool_scale=_pad_rows(pool_scale[l][None, :]), b_f=_pad_rows(jnp.pad(b_f[l], (0, LANE - 8))[None, :]))

    def rest_operands(l, rest):
        p_br, p_out, p_ff1, p_ff2 = rest
        w_br_full = p_br.reshape(D, D)
        cw_full = jnp.transpose(cw_all[:, l], (1, 0, 2)).reshape(3, CONV_W)
        return dict(wa=w_br_full[0:A_WIDTH], wb=w_br_full[A_WIDTH:A_WIDTH + POOL_W], wc=w_br_full[A_WIDTH + POOL_W:],
                    w_out=p_out.reshape(D, D), w_ff1=p_ff1, w_ff2=p_ff2.reshape(D_FF, D), conv_w=_pad_rows(cw_full))

    xs = x[0]
    saved, layers = [], []
    for l in range(DEPTH):
        p_in, rest = gathered[l][0], gathered[l][1:5]
        if l > 0:
            xs, p_in = lax.optimization_barrier((xs, p_in))
        wts = first_operands(l, p_in)

        def arrive(t, l=l, rest=rest, wts=wts):
            if l > 0:
                t, rest = lax.optimization_barrier((t, rest))
            wts.update(rest_operands(l, rest))
            return t

        wts["arrive"] = arrive
        gvec = _pad_rows(jnp.stack([g_mix_pre[l], g_mix_post[l], g_ff_pre[l], g_ff_post[l]]))
        layers.append((wts, gvec, _pad_rows(mod_mine[l])))
        xs, sv = _layer_fwd(l, xs, *layers[l])
        saved.append(sv)
    dx, loss_part = _loss_head(xs, loss_target[0], "loss_head")
    small_grads = [None] * DEPTH
    mine, sibs, landed = ({} for _ in range(3))
    seq_id = iter(range(4, 4 + 4 * DEPTH))
    last = [gathered[DEPTH - 1][1]]

    def start(group, grads):
        mine[group] = grads
        sibs[group] = _sibling_exchange(grads, f"rs_sibling_{group}", sequencer_id=next(seq_id), after=last[0])
        last[0] = sibs[group][0]

    def finish(group, later):
        later, (grads, sib) = lax.optimization_barrier((later, (mine[group], sibs[group])))
        sends = [_pair_sums(g, p, route, f"rs_pair_sums_{group}_{k}") for k, (g, p) in enumerate(zip(grads, sib))]
        later, sends = lax.optimization_barrier((later, sends))
        landed[group] = _chip_exchange(sends, f"rs_chips_{group}", sequencer_id=next(seq_id), after=last[0])
        last[0] = landed[group][0]
        return later

    pending = None
    for l in reversed(range(DEPTH)):
        hook = (lambda da: da) if pending is None else functools.partial(finish, pending)
        dx, ffn_grads, ffn_reds = _ffn_bwd(l, dx, saved[l], *layers[l], hook)
        start(f"ffn_l{l}", ffn_grads)
        dx, mix_grads, small_grads[l] = _mixer_bwd(l, dx, saved[l], *layers[l], ffn_reds,
                                                   functools.partial(finish, f"ffn_l{l}"))
        start(f"mix_l{l}", mix_grads)
        pending = f"mix_l{l}"
    grad_x = dx[None]

    big_w = [wt_in, w_branch, w_out, w_ff1, w_ff2]
    big_m = [mt_in, m_w_branch, m_w_out, m_w_ff1, m_w_ff2]
    big_v = [vt_in, v_w_branch, v_w_out, v_w_ff1, v_w_ff2]
    where = [("mix", 0), ("mix", 1), ("mix", 2), ("ffn", 0), ("ffn", 1)]

    def reduce_and_update(k):
        group, at = where[k]
        return _reduce_adamw([mine[f"{group}_l{l}"][at] for l in range(DEPTH)],
                             [sibs[f"{group}_l{l}"][at] for l in range(DEPTH)],
                             [landed[f"{group}_l{l}"][at] for l in range(DEPTH)], place, big_w[k], big_m[k], big_v[k],
                             f"rs_sum_adamw_{k}")

    big_res = {k: list(reduce_and_update(k)) for k in (3, 4)}
    big_res[3][0] = finish(pending, big_res[3][0])

    small = {k: jnp.stack([small_grads[l][k] for l in range(DEPTH)]) for k in SMALL_KEYS}
    payload = _pack([small[k] for k in SMALL_KEYS] + [loss_part[0:1, 0:1]], 8, F32)
    small_all = _all_gather([payload], "gather_small")[0]
    dmod_all = small_all[:, 0:DEPTH * 6, :].reshape(N_DEV, DEPTH, 6 * D)
    summed = _unpack(_sum_slabs(small_all, "sum_small").reshape(-1), SMALL_SHAPES + [(1, 1)])
    sg = dict(zip(SMALL_KEYS, summed))
    loss = summed[-1][0, 0]
    dmod_cols = lax.dynamic_slice_in_dim(dmod_all, me * ADA_COLS, ADA_COLS, axis=2)
    dmod_cols = jnp.pad(jnp.transpose(dmod_cols, (1, 0, 2)), ((0, 0), (0, ADA_ROWS - N_DEV), (0, 0)))
    g_w_ada = _ada_bwd(c_pad, dmod_cols, "ada_bwd")
    g_conv_w = lax.dynamic_slice_in_dim(sg["conv_w"], me * (CONV_W // N_DEV), CONV_W // N_DEV, axis=2)

    ada_out = [g_w_ada] + list(_adamw(w_ada, g_w_ada, m_w_ada, v_w_ada, "adamw_ada"))
    rest_w = [b_ada, g_mix_pre, g_mix_post, g_ff_pre, g_ff_post, b_f, w_pool, pool_scale, conv_w]
    rest_m = [m_b_ada, m_g_mix_pre, m_g_mix_post, m_g_ff_pre, m_g_ff_post, m_b_f, m_w_pool, m_pool_scale, m_conv_w]
    rest_v = [v_b_ada, v_g_mix_pre, v_g_mix_post, v_g_ff_pre, v_g_ff_post, v_b_f, v_w_pool, v_pool_scale, v_conv_w]
    rest_g = [sg["mod"], sg["g_mix_pre"], sg["g_mix_post"], sg["g_ff_pre"], sg["g_ff_post"], sg["b_f"],
              sg["w_pool"], sg["pool_scale"], g_conv_w]
    rest_shapes = [a.shape for a in rest_w]
    upd = _adamw(_pack(rest_w, 8, F32)[None], _pack(rest_g, 8, F32)[None], _pack(rest_m, 8, F32)[None],
                 _pack(rest_v, 8, F32)[None], "adamw_rest")
    rest_out = [rest_g] + [_unpack(arr.reshape(-1), rest_shapes) for arr in upd]
    rest_out = [[ada_out[which]] + rest_out[which] for which in range(4)]

    landed[pending], rest_out = lax.optimization_barrier((landed[pending], rest_out))
    big_res.update({k: reduce_and_update(k) for k in (0, 1, 2)})
    big_out = [[jnp.transpose(big_res[k][which], (0, 2, 1)) if k == 0 else big_res[k][which] for k in range(5)]
               for which in range(4)]

    def ordered(k):
        r, b = rest_out[k], big_out[k]
        return [r[0], r[1], r[2], r[3], r[4], r[5], b[0], r[6], r[7], r[8], r[9], b[1], b[2], b[3], b[4]]

    return (loss, grad_x, *ordered(0), *ordered(1), *ordered(2), *ordered(3))
```

```python
import functools

import jax
import jax.numpy as jnp
from jax import lax
from jax.experimental import pallas as pl
from jax.experimental.pallas import tpu as pltpu
from jax.experimental.pallas import tpu_sc as plsc

F32 = jnp.float32
BF16 = jnp.bfloat16
GRAD_DTYPE = BF16

N_DEV = 8
D = 1024
S = 2048
DEPTH = 2
D_FF = 4 * D
A_WIDTH = 512
HEAD_DIM = 64
N_PAIR = 4
POOL_W = 256
CONV_W = 256
IN_COLS = 5640
ADA_COLS = 6 * D // N_DEV
IN_SHARD = IN_COLS // N_DEV
RMS_EPS = 1e-6
NEG_INF = -1e30
ATT_SCALE = HEAD_DIM ** -0.5

NZ = 5760
Z_PC = 0
Z_G = 1024
Z_Q = 4096
Z_K = 4608
Z_V = 5120
Z_F = 5632

LR, B1, B2, EPS, WD, STEP = 0.001, 0.9, 0.999, 1e-08, 0.01, 10

LANE = 128
VMEM_LIMIT_BYTES = 48 * 1024 * 1024
TS = 512
TQ = 256
TQ_FWD = 512
HEADS_PER_STEP = 8
HEADS_PER_STEP_FWD = 8


def _params(sem=None):
    return pltpu.CompilerParams(dimension_semantics=sem, vmem_limit_bytes=VMEM_LIMIT_BYTES)


def _pick(n, target):
    best = None
    for t in range(LANE, min(n, target) + 1, LANE):
        if n % t == 0:
            best = t
    return n if best is None else best


def _matmul(a, b, mode, name, out_dtype=F32, tm=2048, tn=1024, tk=2048, b_col_shards=False, out_col_shards=False,
            extra=(), vec_extra=(), epilogue=None, out_dtypes=None, n_row_sums=0, prologue=None, prologue_tiles=(), prologue_vecs=(),
            prologue_sums=False):
    if b_col_shards:
        shards, b_rows, shard_cols = b.shape
        b_shape = (b_rows, shards * shard_cols)
    else:
        b_shape = b.shape
    if mode == "nn":
        (m, k), (k2, n) = a.shape, b_shape
    elif mode == "nt":
        (m, k), (n, k2) = a.shape, b_shape
    else:
        (k, m), (k2, n) = a.shape, b_shape
    assert k == k2, (a.shape, b.shape, mode)
    tm, tn, tk = _pick(m, tm), _pick(n, tn), _pick(k, tk)
    if b_col_shards and mode == "nn":
        tn = shard_cols
    per_step = 1
    if b_col_shards and mode == "nt":
        per_step = max(1, min(tk, 1024) // shard_cols)
        tk = per_step * shard_cols
    if out_col_shards:
        tn = n // N_DEV
    nk = k // tk
    if mode == "nn":
        a_spec = pl.BlockSpec((tm, tk), lambda i, j, kk: (i, kk))
        b_spec = (pl.BlockSpec((None, tk, tn), lambda i, j, kk: (j, kk, 0)) if b_col_shards else
                  pl.BlockSpec((tk, tn), lambda i, j, kk: (kk, j)))
        dims = (((1,), (0,)), ((), ()))
    elif mode == "nt":
        a_spec = pl.BlockSpec((tm, tk), lambda i, j, kk: (i, kk))
        b_spec = (pl.BlockSpec((per_step, tn, shard_cols), lambda i, j, kk: (kk, j, 0)) if b_col_shards else
                  pl.BlockSpec((tn, tk), lambda i, j, kk: (j, kk)))
        dims = (((1,), (1,)), ((), ()))
    else:
        assert not b_col_shards
        a_spec = pl.BlockSpec((tk, tm), lambda i, j, kk: (kk, i))
        b_spec = pl.BlockSpec((tk, tn), lambda i, j, kk: (kk, j))
        dims = (((0,), (0,)), ((), ()))
    if out_col_shards:
        out_shape = jax.ShapeDtypeStruct((N_DEV, m, tn), out_dtype)
        out_spec = pl.BlockSpec((None, tm, tn), lambda i, j, kk: (j, i, 0))
    else:
        out_shape = jax.ShapeDtypeStruct((m, n), out_dtype)
        out_spec = pl.BlockSpec((tm, tn), lambda i, j, kk: (i, j))

    n_extra = len(extra) + len(vec_extra)
    extra_specs = [pl.BlockSpec((tm, tn), lambda i, j, kk, off=off: (i, j + off // tn)) for _, off in extra]
    extra_specs += [pl.BlockSpec((8, tn), lambda i, j, kk: (0, j)) for _ in vec_extra]
    if epilogue is not None:
        assert not out_col_shards and all(off % tn == 0 for _, off in extra)
        out_shape = [jax.ShapeDtypeStruct((m, n), dt) for dt in out_dtypes]
        out_spec = [pl.BlockSpec((tm, tn), lambda i, j, kk: (i, j)) for _ in out_dtypes]
        for at in range(len(out_dtypes) - n_row_sums, len(out_dtypes)):
            out_shape[at] = jax.ShapeDtypeStruct((8 * (m // tm), n), out_dtypes[at])
            out_spec[at] = pl.BlockSpec((8, tn), lambda i, j, kk: (i, j))

    def product(a_ref, b_ref):
        if b_col_shards and mode == "nt":
            b_tile = jnp.concatenate([b_ref[s] for s in range(per_step)], axis=1) if per_step > 1 else b_ref[0]
        else:
            b_tile = b_ref[...]
        return lax.dot_general(a_ref[...].astype(BF16), b_tile.astype(BF16), dims, preferred_element_type=F32)

    def write(acc, extra_refs, o_refs):
        if epilogue is None:
            o_refs[0][...] = acc.astype(out_dtype)
        else:
            for o_ref, tile in zip(o_refs, epilogue(acc, *[r[...] for r in extra_refs])):
                o_ref[...] = tile.astype(o_ref.dtype)

    def body_one_pass(a_ref, b_ref, *refs):
        write(product(a_ref, b_ref), refs[:n_extra], refs[n_extra:])

    if prologue is not None:
        assert nk == 1 and mode in ("nn", "nt")
        n_pro = len(prologue_tiles) + len(prologue_vecs)
        sums = 1 if prologue_sums else 0
        outs = out_shape if isinstance(out_shape, list) else [out_shape]
        out_specs_all = (out_spec if isinstance(out_spec, list) else [out_spec]) + [
            pl.BlockSpec((tm, tk), lambda i, j, kk: (i, 0))]
        outs = outs + [jax.ShapeDtypeStruct((m, k), BF16)]
        if sums:
            outs.append(jax.ShapeDtypeStruct((8 * (m // tm), k), F32))
            out_specs_all.append(pl.BlockSpec((8, tk), lambda i, j, kk: (i, 0)))

        def body_prologue(a_ref, b_ref, *refs):
            pro_refs, rest = refs[:n_pro], refs[n_pro:]
            left_ref = rest[-1]
            left_out = rest[-2 - sums]

            @pl.when(pl.program_id(1) == 0)
            def _():
                made = prologue(a_ref[...], *[r[...] for r in pro_refs])
                left = (made[0] if sums else made).astype(BF16)
                left_ref[...] = left
                left_out[...] = left
                if sums:
                    rest[-2][...] = made[1]

            write(product(left_ref, b_ref), rest[:n_extra], rest[n_extra:-2 - sums])

        pro_specs = [a_spec for _ in prologue_tiles] + [pl.BlockSpec((8, tk), lambda i, j, kk: (0, 0)) for _ in prologue_vecs]
        return pl.pallas_call(
            body_prologue, name=name, out_shape=outs, grid=(m // tm, n // tn, nk),
            in_specs=[a_spec, b_spec] + pro_specs + extra_specs,
            out_specs=out_specs_all,
            scratch_shapes=[pltpu.VMEM((tm, tk), BF16)],
            compiler_params=_params(("parallel", "arbitrary", "arbitrary")),
        )(a, b, *prologue_tiles, *prologue_vecs, *[x for x, _ in extra], *vec_extra)

    def body(a_ref, b_ref, *refs):
        acc_ref = refs[-1]
        kk = pl.program_id(2)

        @pl.when(kk == 0)
        def _():
            acc_ref[...] = product(a_ref, b_ref)

        @pl.when(kk > 0)
        def _():
            acc_ref[...] += product(a_ref, b_ref)

        @pl.when(kk == nk - 1)
        def _():
            write(acc_ref[...], refs[:n_extra], refs[n_extra:-1])

    return pl.pallas_call(
        body_one_pass if nk == 1 else body, name=name,
        out_shape=out_shape,
        grid=(m // tm, n // tn, nk),
        in_specs=[a_spec, b_spec] + extra_specs,
        out_specs=out_spec,
        scratch_shapes=[] if nk == 1 else [pltpu.VMEM((tm, tn), F32)],
        compiler_params=_params(("parallel", "parallel", "arbitrary")),
    )(a, b, *[x for x, _ in extra], *vec_extra)


def _row_spec(width=D, col=0):
    return pl.BlockSpec((TS, width), lambda i: (i, col))


def _vec_spec(rows=8, width=D):
    return pl.BlockSpec((rows, width), lambda i: (0, 0))


def _rms(x):
    return lax.rsqrt(jnp.mean(x * x, axis=-1, keepdims=True) + RMS_EPS)


def _concat_columns(pieces, name):
    widths = [p.shape[1] for p in pieces]
    offsets = [sum(widths[:k]) for k in range(len(widths))]

    def body(*refs):
        o_ref = refs[-1]
        for ref, off, w in zip(refs[:-1], offsets, widths):
            o_ref[:, off:off + w] = ref[...]

    return pl.pallas_call(
        body, name=name, out_shape=jax.ShapeDtypeStruct((S, sum(widths)), pieces[0].dtype), grid=(S // TS,),
        in_specs=[_row_spec(w) for w in widths], out_specs=_row_spec(sum(widths)),
        compiler_params=_params(("parallel",)),
    )(*pieces)


def _loss_head(xf, target, name):
    def body(x_ref, t_ref, dx_ref, loss_ref):
        i = pl.program_id(0)

        @pl.when(i == 0)
        def _():
            loss_ref[...] = jnp.zeros_like(loss_ref)

        e = x_ref[...] - t_ref[...]
        dx_ref[...] = e / float(D)
        per_tok = jnp.mean(e * e, axis=-1, keepdims=True)
        loss_ref[0:1, 0:1] += 0.5 * jnp.sum(per_tok, axis=0, keepdims=True)

    return pl.pallas_call(
        body, name=name,
        out_shape=(jax.ShapeDtypeStruct((S, D), F32), jax.ShapeDtypeStruct((8, LANE), F32)),
        grid=(S // TS,),
        in_specs=[_row_spec(), _row_spec()],
        out_specs=(_row_spec(), pl.BlockSpec((8, LANE), lambda i: (0, 0))),
        compiler_params=_params(("arbitrary",)),
    )(xf, target)


def _relu2_epilogue(a):
    t = jnp.maximum(a, 0.0)
    return a, t * t


def _relu2_bwd_epilogue(dr, a):
    return (dr * (2.0 * jnp.maximum(a, 0.0)),)


def _merge_epilogue(pc, g0, g1, g2, pa, pb):
    return pc, jax.nn.sigmoid(g0) * pa + jax.nn.sigmoid(g1) * pb + jax.nn.sigmoid(g2) * pc


def _prenorm_prologue(g_row, shift_row, scale_row):
    def prologue(x, gvec, mod):
        y = x * _rms(x) * gvec[g_row:g_row + 1, :]
        return y * (1.0 + mod[scale_row:scale_row + 1, :]) + mod[shift_row:shift_row + 1, :]

    return prologue


def _rows8(*rows):
    sub = lax.broadcasted_iota(jnp.int32, (8, rows[0].shape[1]), 0)
    out = jnp.zeros((8, rows[0].shape[1]), F32)
    for k, r in enumerate(rows):
        out = jnp.where(sub == k, r, out)
    return out


def _postnorm_bwd_prologue(g_row, gate_row):
    def prologue(y, dxo, gvec, mod):
        g = gvec[g_row:g_row + 1, :]
        r = _rms(y)
        n = y * r
        dyn = dxo * mod[gate_row:gate_row + 1, :]
        dn = dyn * g
        dy = r * (dn - n * jnp.mean(dn * n, axis=-1, keepdims=True))
        return dy, _rows8(jnp.sum(dxo * (n * g), axis=0, keepdims=True), jnp.sum(dyn * n, axis=0, keepdims=True))

    return prologue


def _prenorm_bwd_epilogue(g_row, scale_row):
    def epilogue(dh, x, dres, gvec, mod):
        g = gvec[g_row:g_row + 1, :]
        r = _rms(x)
        n = x * r
        dyg = dh * (1.0 + mod[scale_row:scale_row + 1, :])
        dn = dyg * g
        dx = r * (dn - n * jnp.mean(dn * n, axis=-1, keepdims=True))
        sums = _rows8(jnp.sum(dh, axis=0, keepdims=True), jnp.sum(dh * (n * g), axis=0, keepdims=True),
                      jnp.sum(dyg * n, axis=0, keepdims=True))
        return dres + dx, sums

    return epilogue


def _postnorm_epilogue(g_row, gate_row):
    def epilogue(y, x, gvec, mod):
        yn = y * _rms(y) * gvec[g_row:g_row + 1, :]
        return y, x + mod[gate_row:gate_row + 1, :] * yn

    return epilogue


def _merge_bwd_epilogue(dm, g0, g1, g2, pa, pb, pc):
    sg = [jax.nn.sigmoid(g) for g in (g0, g1, g2)]
    return tuple(dm * s for s in sg) + tuple(dm * p * (s * (1.0 - s)) for p, s in zip((pa, pb, pc), sg))


def _shift_down(x, k, row):
    return jnp.where(row >= k, pltpu.roll(x, k, axis=0), 0.0)


def _shift_up(x, k, row):
    n = x.shape[0]
    return jnp.where(row < n - k, pltpu.roll(x, n - k, axis=0), 0.0)


def _cumsum_rows(x, row, reverse=False):
    shift = _shift_up if reverse else _shift_down
    k = 1
    while k < x.shape[0]:
        x = x + shift(x, k, row)
        k *= 2
    return x


def _full_spec(shape, idx=(0, 0)):
    return pl.BlockSpec(shape, lambda i: idx)


def _pool_window_select(lane, a2, a4, a8, a16):
    return jnp.where(lane < 64, a2, jnp.where(lane < 128, a4, jnp.where(lane < 192, a8, a16)))


def _pool_p(u, row, lane):
    t2 = u + _shift_down(u, 1, row)
    t4 = t2 + _shift_down(t2, 2, row)
    t8 = t4 + _shift_down(t4, 4, row)
    t16 = t8 + _shift_down(t8, 8, row)
    tw = _pool_window_select(lane, t2, t4, t8, t16)
    cnt = jnp.minimum((row + 1).astype(F32), _pool_window_select(lane, 2.0, 4.0, 8.0, 16.0))
    return tw / cnt - u, cnt


def _pool_fwd(z, wp_bd, pscale, name):
    def body(u_ref, w_ref, s_ref, o_ref):
        row = lax.broadcasted_iota(jnp.int32, (S, POOL_W), 0)
        lane = lax.broadcasted_iota(jnp.int32, (S, POOL_W), 1)
        p, _ = _pool_p(u_ref[...], row, lane)
        y = jnp.dot(p.astype(BF16), w_ref[...], preferred_element_type=F32)
        o_ref[...] = y * s_ref[0:1, :]

    return pl.pallas_call(
        body, name=name, out_shape=jax.ShapeDtypeStruct((S, POOL_W), F32), grid=(1,),
        in_specs=[_full_spec((S, POOL_W), (0, Z_PC // POOL_W)), _full_spec((POOL_W, POOL_W)), _full_spec((8, POOL_W))],
        out_specs=_full_spec((S, POOL_W)),
        compiler_params=_params(("arbitrary",)),
    )(z, wp_bd, pscale)


def _pool_bwd(z, wp_bd, pscale, dbr, name):
    def body(u_ref, w_ref, s_ref, dbr_ref, du_ref, dw_ref, red_ref):
        row = lax.broadcasted_iota(jnp.int32, (S, POOL_W), 0)
        lane = lax.broadcasted_iota(jnp.int32, (S, POOL_W), 1)
        p, cnt = _pool_p(u_ref[...], row, lane)
        pb = p.astype(BF16)
        y = jnp.dot(pb, w_ref[...], preferred_element_type=F32)
        dbr = dbr_ref[...]
        red_ref[...] = jnp.zeros_like(red_ref)
        red_ref[0:1, :] = jnp.sum(dbr * y, axis=0, keepdims=True)
        dy = (dbr * s_ref[0:1, :]).astype(BF16)
        dw_ref[...] = lax.dot_general(pb, dy, (((0,), (0,)), ((), ())), preferred_element_type=F32)
        dp = lax.dot_general(dy, w_ref[...], (((1,), (1,)), ((), ())), preferred_element_type=F32)
        g = dp / cnt
        a2 = g + _shift_up(g, 1, row)
        a4 = a2 + _shift_up(a2, 2, row)
        a8 = a4 + _shift_up(a4, 4, row)
        a16 = a8 + _shift_up(a8, 8, row)
        du_ref[...] = (_pool_window_select(lane, a2, a4, a8, a16) - dp).astype(BF16)

    return pl.pallas_call(
        body, name=name,
        out_shape=(jax.ShapeDtypeStruct((S, POOL_W), BF16), jax.ShapeDtypeStruct((POOL_W, POOL_W), F32),
                   jax.ShapeDtypeStruct((8, POOL_W), F32)),
        grid=(1,),
        in_specs=[_full_spec((S, POOL_W), (0, Z_PC // POOL_W)), _full_spec((POOL_W, POOL_W)), _full_spec((8, POOL_W)),
                  _full_spec((S, POOL_W))],
        out_specs=(_full_spec((S, POOL_W)), _full_spec((POOL_W, POOL_W)), _full_spec((8, POOL_W))),
        compiler_params=_params(("arbitrary",)),
    )(z, wp_bd, pscale, dbr)


def _conv_specs():
    base = Z_PC // CONV_W
    return [_full_spec((S, CONV_W), (0, base + 1)), _full_spec((S, CONV_W), (0, base + 2)),
            _full_spec((S, CONV_W), (0, base + 3)), _full_spec((8, CONV_W))]


def _conv_fwd(z, cw, name):
    def body(h_ref, b_ref, c_ref, w_ref, o_ref):
        row = lax.broadcasted_iota(jnp.int32, (S, CONV_W), 0)
        u = c_ref[...] * h_ref[...]
        y = (w_ref[0:1, :] * _shift_down(u, 2, row) + w_ref[1:2, :] * _shift_down(u, 1, row) + w_ref[2:3, :] * u)
        o_ref[...] = b_ref[...] * y

    return pl.pallas_call(
        body, name=name, out_shape=jax.ShapeDtypeStruct((S, CONV_W), F32), grid=(1,),
        in_specs=_conv_specs(), out_specs=_full_spec((S, CONV_W)),
        compiler_params=_params(("arbitrary",)),
    )(z, z, z, cw)


def _conv_bwd(z, cw, dbr, name):
    def body(h_ref, b_ref, c_ref, w_ref, dbr_ref, d_ref, red_ref):
        row = lax.broadcasted_iota(jnp.int32, (S, CONV_W), 0)
        h, cg = h_ref[...], c_ref[...]
        u = cg * h
        u1 = _shift_down(u, 1, row)
        u2 = _shift_down(u, 2, row)
        y = w_ref[0:1, :] * u2 + w_ref[1:2, :] * u1 + w_ref[2:3, :] * u
        dbr = dbr_ref[...]
        dy = dbr * b_ref[...]
        du = w_ref[2:3, :] * dy + w_ref[1:2, :] * _shift_up(dy, 1, row) + w_ref[0:1, :] * _shift_up(dy, 2, row)
        d_ref[:, 0:CONV_W] = (du * cg).astype(BF16)
        d_ref[:, CONV_W:2 * CONV_W] = (dbr * y).astype(BF16)
        d_ref[:, 2 * CONV_W:3 * CONV_W] = (du * h).astype(BF16)
        red_ref[...] = jnp.zeros_like(red_ref)
        red_ref[0:1, :] = jnp.sum(dy * u2, axis=0, keepdims=True)
        red_ref[1:2, :] = jnp.sum(dy * u1, axis=0, keepdims=True)
        red_ref[2:3, :] = jnp.sum(dy * u, axis=0, keepdims=True)

    return pl.pallas_call(
        body, name=name,
        out_shape=(jax.ShapeDtypeStruct((S, 3 * CONV_W), BF16), jax.ShapeDtypeStruct((8, CONV_W), F32)),
        grid=(1,),
        in_specs=_conv_specs() + [_full_spec((S, CONV_W))],
        out_specs=(_full_spec((S, 3 * CONV_W)), _full_spec((8, CONV_W))),
        compiler_params=_params(("arbitrary",)),
    )(z, z, z, cw, dbr)


_NT = (((1,), (1,)), ((), ()))
_TN = (((0,), (0,)), ((), ()))
N_HEAD = 2 * N_PAIR


def _split3(x):
    hi = x.astype(BF16).astype(F32)
    mid = (x - hi).astype(BF16).astype(F32)
    lo = (x - hi - mid).astype(BF16).astype(F32)
    return hi, mid, lo


def _spare(lane, e, k):
    return lane == 64 * (1 - e) + k


def _spare3(lane, e, k):
    base = 64 * (1 - e) + k
    return (lane >= base) & (lane < base + 3)


def _put3(lane, e, k, pieces, rest):
    out = rest
    for n, piece in enumerate(pieces):
        out = jnp.where(_spare(lane, e, k + n), piece, out)
    return out


def _attn_prep(z, bf, name):
    def body(q_ref, k_ref, v_ref, f_ref, b_ref, qa_ref, ka_ref, va_ref, kat_ref, cum_ref):
        p = pl.program_id(0)
        row = lax.broadcasted_iota(jnp.int32, (S, LANE), 0)
        lane = lax.broadcasted_iota(jnp.int32, (S, LANE), 1)

        @pl.when(p == 0)
        def _():
            xv = f_ref[...] + b_ref[0:1, :]
            ls = jnp.minimum(xv, 0.0) - jnp.log(1.0 + jnp.exp(-jnp.abs(xv)))
            cum_ref[...] = _cumsum_rows(jnp.where(lane < N_HEAD, ls, 0.0), row)

        cum = cum_ref[...]
        q, k, v = q_ref[...], k_ref[...], v_ref[...]
        for e in range(2):
            head = (lane >= 64) if e else (lane < 64)
            f = jnp.sum(jnp.where(lane == 2 * p + e, cum, 0.0), axis=1, keepdims=True)
            pieces = _split3(f)
            qa = jnp.where(head, q * ATT_SCALE, _put3(lane, e, 0, pieces, jnp.where(_spare3(lane, e, 3), 1.0, 0.0)))
            ones = jnp.where(_spare3(lane, e, 0) | _spare3(lane, e, 6), 1.0, 0.0)
            ka = jnp.where(head, k, _put3(lane, e, 3, [-x for x in pieces], ones))
            va = jnp.where(head, v, jnp.where(_spare3(lane, e, 0), 1.0, 0.0))
            qa_ref[e] = qa.astype(BF16)
            ka_ref[e] = ka.astype(BF16)
            va_ref[e] = va.astype(BF16)
            kat_ref[e] = ka.T.astype(BF16)

    qb, kb, vb = Z_Q // LANE, Z_K // LANE, Z_V // LANE
    heads = jax.ShapeDtypeStruct((N_HEAD, S, LANE), BF16)
    pair = pl.BlockSpec((2, S, LANE), lambda p: (p, 0, 0))
    return pl.pallas_call(
        body, name=name,
        out_shape=(heads, heads, heads, jax.ShapeDtypeStruct((N_HEAD, LANE, S), BF16)),
        grid=(N_PAIR,),
        in_specs=[pl.BlockSpec((S, LANE), lambda p: (0, qb + p)), pl.BlockSpec((S, LANE), lambda p: (0, kb + p)),
                  pl.BlockSpec((S, LANE), lambda p: (0, vb + p)), pl.BlockSpec((S, LANE), lambda p: (0, Z_F // LANE)),
                  pl.BlockSpec((8, LANE), lambda p: (0, 0))],
        out_specs=(pair, pair, pair, pl.BlockSpec((2, LANE, S), lambda p: (p, 0, 0))),
        scratch_shapes=[pltpu.VMEM((S, LANE), F32)],
        compiler_params=_params(("arbitrary",)),
    )(z, z, z, z, bf)


def _attn_bwd_prep(qa, o, lse, do, name):
    def body(qa_ref, o_ref, lse_ref, do_ref, qa2_ref, doa_ref):
        lane = lax.broadcasted_iota(jnp.int32, (S, LANE), 1)
        dov, ov, lsev = do_ref[...], o_ref[...], lse_ref[...]
        for e in range(2):
            head = (lane >= 64) if e else (lane < 64)
            dsum = jnp.sum(jnp.where(head, dov * ov, 0.0), axis=1, keepdims=True)
            doa_ref[e] = jnp.where(head, dov, _put3(lane, e, 0, [-x for x in _split3(dsum)], 0.0)).astype(BF16)
            lse_col = lsev[:, 64 * e:64 * e + 1]
            qa2_ref[e] = _put3(lane, e, 6, [-x for x in _split3(lse_col)], qa_ref[e].astype(F32)).astype(BF16)

    heads = jax.ShapeDtypeStruct((N_HEAD, S, LANE), BF16)
    pair = pl.BlockSpec((2, S, LANE), lambda p: (p, 0, 0))
    cols = pl.BlockSpec((S, LANE), lambda p: (0, p))
    return pl.pallas_call(
        body, name=name, out_shape=(heads, heads), grid=(N_PAIR,),
        in_specs=[pair, cols, cols, cols], out_specs=(pair, pair),
        compiler_params=_params(("parallel",)),
    )(qa, o, lse, do)


def _attn_bwd_post(z, bf, dqt, dka, dva, name):
    def body(f_ref, b_ref, dqt_ref, dk_ref, dv_ref, dq_out, dk_out, dv_out, dfl_ref, red_ref, dcum_ref):
        p = pl.program_id(0)

        @pl.when(p == 0)
        def _():
            dcum_ref[...] = jnp.zeros_like(dcum_ref)

        row = lax.broadcasted_iota(jnp.int32, (S, LANE), 0)
        lane = lax.broadcasted_iota(jnp.int32, (S, LANE), 1)
        dqa = [dqt_ref[e].T for e in range(2)]
        dq_out[...] = (jnp.where(lane < 64, dqa[0], dqa[1]) * ATT_SCALE).astype(BF16)
        dk_out[...] = jnp.where(lane < 64, dk_ref[0], dk_ref[1]).astype(BF16)
        dv_out[...] = jnp.where(lane < 64, dv_ref[0], dv_ref[1]).astype(BF16)
        for e in range(2):
            d_query = jnp.sum(jnp.where(_spare(lane, e, 0), dqa[e], 0.0), axis=1, keepdims=True)
            d_key = jnp.sum(jnp.where(_spare(lane, e, 3), dk_ref[e], 0.0), axis=1, keepdims=True)
            dcum_ref[...] += jnp.where(lane == 2 * p + e, d_query - d_key, 0.0)

        @pl.when(p == N_PAIR - 1)
        def _():
            dls = _cumsum_rows(dcum_ref[...], row, reverse=True)
            xv = f_ref[...] + b_ref[0:1, :]
            dx = jnp.where(lane < N_HEAD, dls * jax.nn.sigmoid(-xv), 0.0)
            dfl_ref[...] = dx.astype(BF16)
            red_ref[...] = jnp.zeros_like(red_ref)
            red_ref[0:1, :] = jnp.sum(dx, axis=0, keepdims=True)

    wide = jax.ShapeDtypeStruct((S, N_PAIR * LANE), BF16)
    cols = pl.BlockSpec((S, LANE), lambda p: (0, p))
    pair = pl.BlockSpec((2, S, LANE), lambda p: (p, 0, 0))
    return pl.pallas_call(
        body, name=name,
        out_shape=(wide, wide, wide, jax.ShapeDtypeStruct((S, LANE), BF16), jax.ShapeDtypeStruct((8, LANE), F32)),
        grid=(N_PAIR,),
        in_specs=[pl.BlockSpec((S, LANE), lambda p: (0, Z_F // LANE)), pl.BlockSpec((8, LANE), lambda p: (0, 0)),
                  pl.BlockSpec((2, LANE, S), lambda p: (p, 0, 0)), pair, pair],
        out_specs=(cols, cols, cols, pl.BlockSpec((S, LANE), lambda p: (0, 0)), pl.BlockSpec((8, LANE), lambda p: (0, 0))),
        scratch_shapes=[pltpu.VMEM((S, LANE), F32)],
        compiler_params=_params(("arbitrary",)),
    )(z, bf, dqt, dka, dva)


def _attn_fwd(qa, ka, va, name):
    tq, tk = TQ_FWD, TQ
    ratio = tq // tk

    def body(qa_ref, ka_ref, va_ref, o_ref, lse_ref):
        i = pl.program_id(1)
        lane = lax.broadcasted_iota(jnp.int32, (tq, LANE), 1)
        row = lax.broadcasted_iota(jnp.int32, (tq, tk), 0)
        col = lax.broadcasted_iota(jnp.int32, (tq, tk), 1)
        nh = HEADS_PER_STEP_FWD
        qs = [qa_ref[h] for h in range(nh)]

        def block(j, carry, masked):
            off = pl.multiple_of(j * tk, tk)
            out = []
            for h in range(nh):
                m, acc = carry[h]
                s = lax.dot_general(qs[h], ka_ref[h, pl.ds(off, tk), :], _NT, preferred_element_type=F32)
                if masked:
                    s = jnp.where(col + (j - ratio * i) * tk > row, NEG_INF, s)
                mn = jnp.maximum(m, jnp.max(s, axis=1, keepdims=True))
                p = jnp.exp(s - mn).astype(BF16)
                acc = jnp.exp(m - mn) * acc + jnp.dot(p, va_ref[h, pl.ds(off, tk), :], preferred_element_type=F32)
                out.append((mn, acc))
            return tuple(out)

        init = (jnp.full((tq, 1), NEG_INF, F32), jnp.zeros((tq, LANE), F32))
        carry = lax.fori_loop(0, ratio * i, lambda j, c: block(j, c, False), (init,) * nh)
        for d in range(ratio):
            carry = block(ratio * i + d, carry, True)
        res = []
        for h in range(nh):
            m, acc = carry[h]
            l = jnp.sum(jnp.where(_spare(lane, h % 2, 0), acc, 0.0), axis=1, keepdims=True)
            res.append((acc / l, m + jnp.log(l)))
        for g in range(nh // 2):
            o_ref[:, g * LANE:(g + 1) * LANE] = jnp.where(lane < 64, res[2 * g][0], res[2 * g + 1][0])
            lse_ref[:, g * LANE:(g + 1) * LANE] = jnp.where(lane < 64, res[2 * g][1], res[2 * g + 1][1])

    nh = HEADS_PER_STEP_FWD
    out = jax.ShapeDtypeStruct((S, N_PAIR * LANE), F32)
    wide = pl.BlockSpec((tq, 64 * nh), lambda p, i: (i, p))
    return pl.pallas_call(
        body, name=name, out_shape=(out, out), grid=(N_HEAD // nh, S // tq),
        in_specs=[pl.BlockSpec((nh, tq, LANE), lambda p, i: (p, i, 0)), pl.BlockSpec((nh, S, LANE), lambda p, i: (p, 0, 0)),
                  pl.BlockSpec((nh, S, LANE), lambda p, i: (p, 0, 0))],
        out_specs=(wide, wide),
        compiler_params=_params(("parallel", "parallel")),
    )(qa, ka, va)


def _attn_bwd(qa2, ka, va, kat, doa, name):
    nq = S // TQ

    def body(qa_ref, ka_ref, va_ref, kat_ref, doa_ref, dqt_ref, dk_ref, dv_ref):
        j = pl.program_id(1)

        @pl.when(j == 0)
        def _():
            dqt_ref[...] = jnp.zeros_like(dqt_ref)

        key = lax.broadcasted_iota(jnp.int32, (TQ, TQ), 0)
        qry = lax.broadcasted_iota(jnp.int32, (TQ, TQ), 1)
        nh = HEADS_PER_STEP
        kav, vav, katv = ([ref[h] for h in range(nh)] for ref in (ka_ref, va_ref, kat_ref))

        def block(i, carry, masked):
            off = pl.multiple_of(i * TQ, TQ)
            out = []
            for h in range(nh):
                dk_acc, dv_acc = carry[h]
                qav = qa_ref[h, pl.ds(off, TQ), :]
                doav = doa_ref[h, pl.ds(off, TQ), :]
                s_t = lax.dot_general(kav[h], qav, _NT, preferred_element_type=F32)
                if masked:
                    s_t = jnp.where(key > qry, NEG_INF, s_t)
                p_t = jnp.exp(s_t)
                ds_t = p_t * lax.dot_general(vav[h], doav, _NT, preferred_element_type=F32)
                dsb = ds_t.astype(BF16)
                dv_acc = dv_acc + jnp.dot(p_t.astype(BF16), doav, preferred_element_type=F32)
                dk_acc = dk_acc + jnp.dot(dsb, qav, preferred_element_type=F32)
                dqt_ref[h, :, pl.ds(off, TQ)] += jnp.dot(katv[h], dsb, preferred_element_type=F32)
                out.append((dk_acc, dv_acc))
            return tuple(out)

        zero = (jnp.zeros((TQ, LANE), F32), jnp.zeros((TQ, LANE), F32))
        carry = block(j, (zero,) * nh, True)
        carry = lax.fori_loop(j + 1, nq, lambda i, c: block(i, c, False), carry)
        for h in range(nh):
            dk_ref[h], dv_ref[h] = carry[h]

    nh = HEADS_PER_STEP
    full = pl.BlockSpec((nh, S, LANE), lambda p, j: (p, 0, 0))
    blk = pl.BlockSpec((nh, TQ, LANE), lambda p, j: (p, j, 0))
    acc = jax.ShapeDtypeStruct((N_HEAD, S, LANE), F32)
    return pl.pallas_call(
        body, name=name,
        out_shape=(jax.ShapeDtypeStruct((N_HEAD, LANE, S), F32), acc, acc),
        grid=(N_HEAD // nh, nq),
        in_specs=[full, blk, blk, pl.BlockSpec((nh, LANE, TQ), lambda p, j: (p, 0, j)), full],
        out_specs=(pl.BlockSpec((nh, LANE, S), lambda p, j: (p, 0, 0)), blk, blk),
        compiler_params=_params(("arbitrary", "arbitrary")),
    )(qa2, ka, va, kat, doa)


ADA_ROWS = 16


def _ada_fwd(c_pad, w_ada, b_cols, name):
    def body(c_ref, w_ref, b_ref, o_ref):
        cv = c_ref[...]
        sc = (cv * jax.nn.sigmoid(cv)).astype(BF16)
        o_ref[0] = jnp.dot(sc, w_ref[0].astype(BF16), preferred_element_type=F32) + b_ref[0, 0:1, :]

    return pl.pallas_call(
        body, name=name, out_shape=jax.ShapeDtypeStruct((DEPTH, ADA_ROWS, ADA_COLS), F32), grid=(DEPTH,),
        in_specs=[pl.BlockSpec((ADA_ROWS, D), lambda l: (0, 0)), pl.BlockSpec((1, D, ADA_COLS), lambda l: (l, 0, 0)),
                  pl.BlockSpec((1, 8, ADA_COLS), lambda l: (l, 0, 0))],
        out_specs=pl.BlockSpec((1, ADA_ROWS, ADA_COLS), lambda l: (l, 0, 0)),
        compiler_params=_params(("parallel",)),
    )(c_pad, w_ada, b_cols)


def _ada_bwd(c_pad, dmod_cols, name):
    def body(c_ref, d_ref, o_ref):
        cv = c_ref[...]
        sc = (cv * jax.nn.sigmoid(cv)).astype(BF16)
        o_ref[0] = lax.dot_general(sc, d_ref[0].astype(BF16), _TN, preferred_element_type=F32)

    return pl.pallas_call(
        body, name=name, out_shape=jax.ShapeDtypeStruct((DEPTH, D, ADA_COLS), F32), grid=(DEPTH,),
        in_specs=[pl.BlockSpec((ADA_ROWS, D), lambda l: (0, 0)), pl.BlockSpec((1, ADA_ROWS, ADA_COLS), lambda l: (l, 0, 0))],
        out_specs=pl.BlockSpec((1, D, ADA_COLS), lambda l: (l, 0, 0)),
        compiler_params=_params(("parallel",)),
    )(c_pad, dmod_cols)


def _adamw_math(w, g, m, v):
    m = B1 * m + (1.0 - B1) * g
    v = B2 * v + (1.0 - B2) * (g * g)
    m_hat = m / (1.0 - B1 ** STEP)
    v_hat = v / (1.0 - B2 ** STEP)
    delta = -LR * (m_hat / (jnp.sqrt(v_hat) + EPS) + WD * w)
    return delta, m, v


def _row_tile(rows, target=256):
    best = 8
    for t in range(8, min(rows, target) + 1, 8):
        if rows % t == 0:
            best = t
    return best


def _adamw(w, g, m, v, name):
    layers, rows, cols = w.shape
    tr = _row_tile(rows)
    spec = pl.BlockSpec((1, tr, cols), lambda l, i: (l, i, 0))

    def body(w_ref, g_ref, m_ref, v_ref, d_ref, nm_ref, nv_ref):
        d_ref[...], nm_ref[...], nv_ref[...] = _adamw_math(w_ref[...], g_ref[...], m_ref[...], v_ref[...])

    out = jax.ShapeDtypeStruct(w.shape, F32)
    return pl.pallas_call(
        body, name=name, out_shape=(out, out, out), grid=(layers, rows // tr),
        in_specs=[spec] * 4, out_specs=(spec,) * 3, compiler_params=_params(("parallel", "parallel")),
    )(w, g, m, v)


def _sum_slabs(x, name):
    n, rows, _ = x.shape
    tr = _row_tile(rows)

    def body(x_ref, o_ref):
        acc = x_ref[0]
        for k in range(1, n):
            acc = acc + x_ref[k]
        o_ref[...] = acc

    return pl.pallas_call(
        body, name=name, out_shape=jax.ShapeDtypeStruct((rows, D), F32), grid=(rows // tr,),
        in_specs=[pl.BlockSpec((n, tr, D), lambda i: (0, i, 0))], out_specs=pl.BlockSpec((tr, D), lambda i: (i, 0)),
        compiler_params=_params(("parallel",)),
    )(x)


_ANY = pl.BlockSpec(memory_space=pl.ANY)
MESH = pl.DeviceIdType.MESH


def _on_sequencer(body, out_shape, sems, operands, after, sequencer_id, name):
    n = len(operands)

    def ordered_body(*refs):
        body(*refs[:n], *refs[n + 1:])

    extra = [] if after is None else [after]
    return pl.kernel(
        body if after is None else ordered_body, out_type=out_shape,
        mesh=plsc.ScalarSubcoreMesh(axis_name="sequencer", num_cores=1), scratch_types=sems,
        compiler_params=pltpu.CompilerParams(collective_id=sequencer_id), name=name)(*operands, *extra)


def _all_gather(xs, name, sequencer_id=None, after=None):
    n = len(xs)

    def body(*refs):
        x_refs, out_refs = refs[:n], refs[n:2 * n]
        send_sems, recv_sems, local_sems = refs[2 * n:]
        x_, y_, c_ = lax.axis_index("x"), lax.axis_index("y"), lax.axis_index("c")
        me, sibling = (x_, y_, c_), (x_, y_, 1 - c_)
        chips = [(1 - x_, y_), (x_, 1 - y_), (1 - x_, 1 - y_)]
        if sequencer_id is not None:
            barrier = pltpu.get_barrier_semaphore()
            peers = [sibling] + [(*chip, pc) for chip in chips for pc in (c_, 1 - c_)]
            for peer in peers:
                pl.semaphore_signal(barrier, inc=1, device_id=peer, device_id_type=MESH)
            pl.semaphore_wait(barrier, len(peers))

        def slot(a, px, py, pc):
            return out_refs[a].at[4 * px + 2 * py + pc]

        def copy(a, k, block, to, src=None):
            return pltpu.make_async_remote_copy(
                src_ref=slot(a, *block) if src is None else src, dst_ref=slot(a, *block),
                send_sem=send_sems.at[7 * a + k], recv_sem=recv_sems.at[7 * a + k], device_id=to, device_id_type=MESH)

        mine = [pltpu.make_async_copy(x_refs[a], slot(a, *me), local_sems.at[a]) for a in range(n)]
        for cp in mine:
            cp.start()
        first = []
        for a in range(n):
            first.append(copy(a, 0, me, sibling, src=x_refs[a]))
            first += [copy(a, 1 + j, me, (*chip, c_), src=x_refs[a]) for j, chip in enumerate(chips)]
        for cp in first:
            cp.start()
        passed = []
        for j, chip in enumerate(chips):
            for a in range(n):
                copy(a, 1 + j, (*chip, c_), me).wait_recv()
                passed.append(copy(a, 4 + j, (*chip, c_), sibling))
                passed[-1].start()
        for a in range(n):
            copy(a, 0, sibling, me).wait_recv()
        for j, chip in enumerate(chips):
            for a in range(n):
                copy(a, 4 + j, (*chip, 1 - c_), me).wait_recv()
        for cp in first + passed:
            cp.wait_send()
        for cp in mine:
            cp.wait()

    out_shape = [jax.ShapeDtypeStruct((N_DEV,) + x.shape, x.dtype) for x in xs]
    sems = [pltpu.SemaphoreType.DMA((7 * n,)), pltpu.SemaphoreType.DMA((7 * n,)), pltpu.SemaphoreType.DMA((n,))]
    if sequencer_id is not None:
        return _on_sequencer(body, out_shape, sems, xs, after, sequencer_id, name)
    return pl.pallas_call(
        body, name=name, out_shape=out_shape, in_specs=[_ANY] * n, out_specs=[_ANY] * n, scratch_shapes=sems)(*xs)


def _sibling_exchange(gs, name, sequencer_id=None, after=None):
    n = len(gs)

    def body(*refs):
        g_refs, p_refs = refs[:n], refs[n:2 * n]
        send_sems, recv_sems = refs[2 * n:]
        x_, y_, c_ = lax.axis_index("x"), lax.axis_index("y"), lax.axis_index("c")
        if sequencer_id is not None:
            barrier = pltpu.get_barrier_semaphore()
            pl.semaphore_signal(barrier, inc=1, device_id=(x_, y_, 1 - c_), device_id_type=MESH)
            pl.semaphore_wait(barrier, 1)
        copies = [pltpu.make_async_remote_copy(
            src_ref=g_refs[a].at[2 * k + (1 - c_)], dst_ref=p_refs[a].at[k], send_sem=send_sems.at[4 * a + k],
            recv_sem=recv_sems.at[4 * a + k], device_id=(x_, y_, 1 - c_), device_id_type=MESH)
            for a in range(n) for k in range(4)]
        for cp in copies:
            cp.start()
        for cp in copies:
            cp.wait()

    out_shape = [jax.ShapeDtypeStruct((4,) + g.shape[1:], g.dtype) for g in gs]
    sems = [pltpu.SemaphoreType.DMA((4 * n,)), pltpu.SemaphoreType.DMA((4 * n,))]
    if sequencer_id is not None:
        return _on_sequencer(body, out_shape, sems, gs, after, sequencer_id, name)
    return pl.pallas_call(
        body, name=name, out_shape=out_shape, in_specs=[_ANY] * n, out_specs=[_ANY] * n, scratch_shapes=sems)(*gs)


def _slab_tiles(rows, cols):
    if rows % 8 == 0:
        return _row_tile(rows), cols
    return rows, 2 * LANE


def _pair_sums(g, p, route, name):
    _, rows, cols = g.shape
    tr, tc = _slab_tiles(rows, cols)

    def body(route_ref, g_ref, p_ref, t_ref):
        t_ref[...] = (g_ref[...].astype(F32) + p_ref[...].astype(F32)).astype(BF16)

    return pl.pallas_call(
        body, name=name, out_shape=jax.ShapeDtypeStruct((3, rows, cols), BF16),
        grid_spec=pltpu.PrefetchScalarGridSpec(
            num_scalar_prefetch=1, grid=(3, rows // tr, cols // tc),
            in_specs=[pl.BlockSpec((1, tr, tc), lambda r, i, j, route_ref: (2 * route_ref[1 + r] + route_ref[0], i, j)),
                      pl.BlockSpec((1, tr, tc), lambda r, i, j, route_ref: (route_ref[1 + r], i, j))],
            out_specs=pl.BlockSpec((1, tr, tc), lambda r, i, j, route_ref: (r, i, j))),
        compiler_params=_params(("parallel", "parallel", "parallel")),
    )(route, g, p)


def _chip_exchange(ts, name, sequencer_id=None, after=None):
    n = len(ts)

    def body(*refs):
        t_refs, l_refs = refs[:n], refs[n:2 * n]
        send_sems, recv_sems = refs[2 * n:]
        x_, y_, c_ = lax.axis_index("x"), lax.axis_index("y"), lax.axis_index("c")
        chips = [(1 - x_, y_), (x_, 1 - y_), (1 - x_, 1 - y_)]
        if sequencer_id is not None:
            barrier = pltpu.get_barrier_semaphore()
            for px, py in chips:
                pl.semaphore_signal(barrier, inc=1, device_id=(px, py, c_), device_id_type=MESH)
            pl.semaphore_wait(barrier, len(chips))
        copies = [pltpu.make_async_remote_copy(
            src_ref=t_refs[a].at[r], dst_ref=l_refs[a].at[r], send_sem=send_sems.at[3 * a + r],
            recv_sem=recv_sems.at[3 * a + r], device_id=(px, py, c_), device_id_type=MESH)
            for a in range(n) for r, (px, py) in enumerate(chips)]
        for cp in copies:
            cp.start()
        for cp in copies:
            cp.wait()

    out_shape = [jax.ShapeDtypeStruct((3,) + t.shape[1:], t.dtype) for t in ts]
    sems = [pltpu.SemaphoreType.DMA((3 * n,)), pltpu.SemaphoreType.DMA((3 * n,))]
    if sequencer_id is not None:
        return _on_sequencer(body, out_shape, sems, ts, after, sequencer_id, name)
    return pl.pallas_call(
        body, name=name, out_shape=out_shape, in_specs=[_ANY] * n, out_specs=[_ANY] * n, scratch_shapes=sems)(*ts)


def _reduce_adamw(gs, ps, landed, place, w, m, v, name):
    layers, rows, cols = w.shape
    assert layers == DEPTH == 2
    tr, tc = _slab_tiles(rows, cols)
    nr, nc = rows // tr, cols // tc
    spec = pl.BlockSpec((1, tr, tc), lambda l, i, j, place_ref: (l, i, j))

    def own(layer, which):
        pi, pj = (nr - 1, nc - 1) if layer == 0 else (0, 0)

        def index(l, i, j, place_ref):
            lead = 0 if which is None else place_ref[which]
            return lead, jnp.where(l == layer, i, pi), jnp.where(l == layer, j, pj)

        return pl.BlockSpec((3 if which is None else 1, tr, tc), index)

    def body(place_ref, g0_ref, p0_ref, l0_ref, g1_ref, p1_ref, l1_ref, w_ref, m_ref, v_ref,
             g_ref, d_ref, nm_ref, nv_ref):
        def update(own_ref, sib_ref, l_ref):
            g = (own_ref[0].astype(F32) + sib_ref[0].astype(F32) + l_ref[0].astype(F32) + l_ref[1].astype(F32)
                 + l_ref[2].astype(F32))
            g_ref[0] = g
            d_ref[0], nm_ref[0], nv_ref[0] = _adamw_math(w_ref[0], g, m_ref[0], v_ref[0])

        @pl.when(pl.program_id(0) == 0)
        def _():
            update(g0_ref, p0_ref, l0_ref)

        @pl.when(pl.program_id(0) == 1)
        def _():
            update(g1_ref, p1_ref, l1_ref)

    out = jax.ShapeDtypeStruct(w.shape, F32)
    return pl.pallas_call(
        body, name=name, out_shape=(out, out, out, out),
        grid_spec=pltpu.PrefetchScalarGridSpec(
            num_scalar_prefetch=1, grid=(DEPTH, nr, nc),
            in_specs=[own(0, 0), own(0, 1), own(0, None), own(1, 0), own(1, 1), own(1, None), spec, spec, spec],
            out_specs=(spec, spec, spec, spec)),
        compiler_params=_params(("arbitrary", "arbitrary", "arbitrary")),
    )(place, gs[0], ps[0], landed[0], gs[1], ps[1], landed[1], w, m, v)


def _pack(pieces, row_multiple, dtype, cols=D, rows=None):
    flat = jnp.concatenate([p.astype(dtype).reshape(-1) for p in pieces])
    if rows is None:
        rows = -(-flat.shape[0] // cols)
        rows = -(-rows // row_multiple) * row_multiple
    flat = jnp.pad(flat, (0, rows * cols - flat.shape[0]))
    return flat.reshape(rows, cols)


def _unpack(flat, shapes, lead=()):
    out, off = [], 0
    for shp in shapes:
        n = 1
        for s_ in shp:
            n *= s_
        out.append(lax.slice_in_dim(flat, off, off + n, axis=len(lead)).reshape(lead + tuple(shp)))
        off += n
    return out


WIN_STRIDE = 704
WIN_ROWS = 720
Z_TURN = 1544


def _window(wt, me, name):
    padded = jnp.pad(wt, ((0, 0), (0, WIN_ROWS - IN_SHARD), (0, 0)))

    def body(me_ref, x_ref, o_ref):
        o_ref[0] = pltpu.roll(x_ref[0], me_ref[0], axis=0).astype(BF16)

    spec = pl.BlockSpec((1, WIN_ROWS, D), lambda l, me_ref: (l, 0, 0))
    return pl.pallas_call(
        body, name=name, out_shape=jax.ShapeDtypeStruct((DEPTH, WIN_ROWS, D), BF16),
        grid_spec=pltpu.PrefetchScalarGridSpec(num_scalar_prefetch=1, grid=(DEPTH,), in_specs=[spec], out_specs=spec),
        compiler_params=_params(("parallel",)),
    )(me, padded)


def _z_rows_from_windows(win):
    over = WIN_ROWS - WIN_STRIDE
    pieces = [(0, win[0][0:WIN_STRIDE])]
    for d in range(1, N_DEV):
        base = WIN_STRIDE * d
        pieces.append((base, win[d - 1][WIN_STRIDE:WIN_ROWS] + win[d][0:over]))
        pieces.append((base + over, win[d][over:WIN_STRIDE]))
    pieces.append((WIN_STRIDE * N_DEV, win[N_DEV - 1][WIN_STRIDE:WIN_ROWS]))

    def rows(a, b):
        out = []
        for start, arr in pieces:
            lo, hi = max(a, start), min(b, start + arr.shape[0])
            if lo < hi:
                out.append(arr[lo - start:hi - start])
        return out

    pad = jnp.zeros((NZ - IN_COLS, win.shape[-1]), win.dtype)
    return jnp.concatenate(rows(Z_TURN, IN_COLS) + rows(0, Z_TURN) + [pad], axis=0)


def _in_rows_from_z(wt):
    return jnp.concatenate([wt[Z_Q:Z_Q + 1536], wt[Z_F:Z_F + 8], wt[Z_PC:Z_PC + 1024], wt[Z_G:Z_G + 3072]], axis=0)


def _pad_rows(v, rows=8):
    return jnp.pad(v, ((0, rows - v.shape[0]), (0, 0)))


def _layer_fwd(l, x, wts, gvec, mod):
    tag = f"l{l}"
    z, h = _matmul(x, wts["w_in_t"], "nt", f"in_proj_{tag}", tm=1024, tn=1152, prologue=_prenorm_prologue(0, 0, 1),
                   prologue_vecs=[gvec, mod])
    qa, ka, va, kat = _attn_prep(z, wts["b_f"], f"attn_prep_{tag}")
    qa = wts["arrive"](qa)
    o, lse = _attn_fwd(qa, ka, va, f"attn_{tag}")
    br_b = _pool_fwd(z, wts["wp_bd"], wts["pool_scale"], f"pool_{tag}")
    br_c = _conv_fwd(z, wts["conv_w"], f"conv_{tag}")
    pa = _matmul(o, wts["wa"], "nn", f"proj_a_{tag}", out_dtype=BF16)
    pb = _matmul(br_b, wts["wb"], "nn", f"proj_b_{tag}", out_dtype=BF16)
    gates = [(z, Z_G + k * D) for k in range(3)]
    pc, merged = _matmul(br_c, wts["wc"], "nn", f"proj_c_merge_{tag}", tm=1024, tn=512,
                         extra=gates + [(pa, 0), (pb, 0)], epilogue=_merge_epilogue, out_dtypes=(BF16, BF16))
    y, x1 = _matmul(merged, wts["w_out"], "nn", f"out_proj_{tag}", tm=1024, tn=D, extra=[(x, 0)],
                    vec_extra=[gvec, mod], epilogue=_postnorm_epilogue(1, 2), out_dtypes=(F32, F32))
    a, r, h2 = _matmul(x1, wts["w_ff1"], "nn", f"ff1_{tag}", b_col_shards=True, epilogue=_relu2_epilogue,
                       out_dtypes=(BF16, BF16), prologue=_prenorm_prologue(2, 3, 4), prologue_vecs=[gvec, mod])
    y2, x2 = _matmul(r, wts["w_ff2"], "nn", f"ff2_{tag}", tm=1024, tn=D, tk=1024, extra=[(x1, 0)],
                     vec_extra=[gvec, mod], epilogue=_postnorm_epilogue(3, 5), out_dtypes=(F32, F32))
    saved = dict(x=x, h=h, z=z, qa=qa, ka=ka, va=va, kat=kat, o=o, lse=lse, br_b=br_b, br_c=br_c, pa=pa, pb=pb, pc=pc,
                 merged=merged, y=y, x1=x1, h2=h2, a=a, r=r, y2=y2)
    return x2, saved


def _ffn_bwd(l, dx2, sv, wts, gvec, mod, midpoint):
    tag = f"l{l}"
    dx2 = midpoint(dx2)
    da, dy2, sums = _matmul(sv["y2"], wts["w_ff2"], "nt", f"ff2_dx_{tag}", tm=1024, extra=[(sv["a"], 0)],
                            epilogue=_relu2_bwd_epilogue, out_dtypes=(BF16,), prologue=_postnorm_bwd_prologue(3, 5),
                            prologue_tiles=[dx2], prologue_vecs=[gvec, mod], prologue_sums=True)
    red_post_ff = jnp.sum(sums.reshape(-1, 8, D), axis=0)
    d_w_ff2 = _matmul(sv["r"], dy2, "tn", f"ff2_dw_{tag}", out_dtype=GRAD_DTYPE)
    dx1, sums = _matmul(da, wts["w_ff1"], "nt", f"ff1_dx_{tag}", tm=1024, tn=D, b_col_shards=True,
                        extra=[(sv["x1"], 0), (dx2, 0)], vec_extra=[gvec, mod], epilogue=_prenorm_bwd_epilogue(2, 4),
                        out_dtypes=(F32, F32), n_row_sums=1)
    red_pre_ff = jnp.sum(sums.reshape(-1, 8, D), axis=0)
    d_w_ff1 = _matmul(sv["h2"], da, "tn", f"ff1_dw_{tag}", out_dtype=GRAD_DTYPE, out_col_shards=True)
    return dx1, [d_w_ff1, d_w_ff2.reshape(N_DEV, D_FF // N_DEV, D)], (red_pre_ff, red_post_ff)


def _mixer_bwd(l, dx1, sv, wts, gvec, mod, ffn_reds, midpoint):
    tag = f"l{l}"
    red_pre_ff, red_post_ff = ffn_reds
    gates = [(sv["z"], Z_G + k * D) for k in range(3)]
    dpa, dpb, dpc, *dgl, dy, sums = _matmul(
        sv["y"], wts["w_out"], "nt", f"out_proj_dx_{tag}", tm=512, tn=512,
        extra=gates + [(sv["pa"], 0), (sv["pb"], 0), (sv["pc"], 0)], epilogue=_merge_bwd_epilogue,
        out_dtypes=(BF16,) * 6, prologue=_postnorm_bwd_prologue(1, 2), prologue_tiles=[dx1], prologue_vecs=[gvec, mod],
        prologue_sums=True)
    red_post_mix = jnp.sum(sums.reshape(-1, 8, D), axis=0)
    d_w_out = _matmul(sv["merged"], dy, "tn", f"out_proj_dw_{tag}", out_dtype=GRAD_DTYPE)
    dpa = midpoint(dpa)
    do = _matmul(dpa, wts["wa"], "nt", f"proj_a_dx_{tag}")
    dbr_b = _matmul(dpb, wts["wb"], "nt", f"proj_b_dx_{tag}")
    dbr_c = _matmul(dpc, wts["wc"], "nt", f"proj_c_dx_{tag}")
    d_wa = _matmul(sv["o"], dpa, "tn", f"proj_a_dw_{tag}", out_dtype=GRAD_DTYPE)
    d_wb = _matmul(sv["br_b"], dpb, "tn", f"proj_b_dw_{tag}", out_dtype=GRAD_DTYPE)
    d_wc = _matmul(sv["br_c"], dpc, "tn", f"proj_c_dw_{tag}", out_dtype=GRAD_DTYPE)
    d_w_branch = jnp.concatenate([d_wa, d_wb, d_wc], axis=0)

    dpu, d_wp_bd, red_pool = _pool_bwd(sv["z"], wts["wp_bd"], wts["pool_scale"], dbr_b, f"pool_bwd_{tag}")
    dconv, red_conv = _conv_bwd(sv["z"], wts["conv_w"], dbr_c, f"conv_bwd_{tag}")
    qa2, doa = _attn_bwd_prep(sv["qa"], sv["o"], sv["lse"], do, f"attn_bwd_prep_{tag}")
    dqt, dka, dva = _attn_bwd(qa2, sv["ka"], sv["va"], sv["kat"], doa, f"attn_bwd_{tag}")
    dq, dk, dv, dfl, red_f = _attn_bwd_post(sv["z"], wts["b_f"], dqt, dka, dva, f"attn_bwd_post_{tag}")
    dz = _concat_columns([dpu, dconv, *dgl, dq, dk, dv, dfl], f"dz_{tag}")
    dx0, sums = _matmul(dz, wts["w_in_t"], "nn", f"in_proj_dx_{tag}", tm=1024, tn=D, tk=1152,
                        extra=[(sv["x"], 0), (dx1, 0)], vec_extra=[gvec, mod], epilogue=_prenorm_bwd_epilogue(0, 1),
                        out_dtypes=(F32, F32), n_row_sums=1)
    red_pre_mix = jnp.sum(sums.reshape(-1, 8, D), axis=0)
    d_w_in_t = _matmul(dz, sv["h"], "tn", f"in_proj_dw_{tag}", out_dtype=GRAD_DTYPE, tm=1152)

    rows = D // N_DEV
    big = [_in_rows_from_z(d_w_in_t).reshape(N_DEV, IN_SHARD, D), d_w_branch.reshape(N_DEV, rows, D),
           d_w_out.reshape(N_DEV, rows, D)]
    d_w_pool = jnp.stack([d_wp_bd[64 * g:64 * (g + 1), 64 * g:64 * (g + 1)] for g in range(4)])
    small = dict(
        mod=jnp.stack([red_pre_mix[0], red_pre_mix[1], red_post_mix[0], red_pre_ff[0], red_pre_ff[1], red_post_ff[0]]),
        g_mix_pre=red_pre_mix[2], g_mix_post=red_post_mix[1], g_ff_pre=red_pre_ff[2], g_ff_post=red_post_ff[1],
        b_f=red_f[0, 0:8], w_pool=d_w_pool, pool_scale=red_pool[0], conv_w=red_conv[0:3])
    return dx0, big, small


SMALL_KEYS = ["mod", "g_mix_pre", "g_mix_post", "g_ff_pre", "g_ff_post", "b_f", "w_pool", "pool_scale", "conv_w"]
SMALL_SHAPES = [(DEPTH, 6 * D), (DEPTH, D), (DEPTH, D), (DEPTH, D), (DEPTH, D), (DEPTH, 8), (DEPTH, 4, 64, 64),
                (DEPTH, POOL_W), (DEPTH, 3, CONV_W)]


def kernel(x, c, w_ada, b_ada, g_mix_pre, g_mix_post, g_ff_pre, g_ff_post, w_in, b_f, w_pool, pool_scale, conv_w, w_branch, w_out, w_ff1, w_ff2, loss_target, m_w_ada, m_b_ada, m_g_mix_pre, m_g_mix_post, m_g_ff_pre, m_g_ff_post, m_w_in, m_b_f, m_w_pool, m_pool_scale, m_conv_w, m_w_branch, m_w_out, m_w_ff1, m_w_ff2, v_w_ada, v_b_ada, v_g_mix_pre, v_g_mix_post, v_g_ff_pre, v_g_ff_post, v_w_in, v_b_f, v_w_pool, v_pool_scale, v_conv_w, v_w_branch, v_w_out, v_w_ff1, v_w_ff2):
    ix, iy, ic = lax.axis_index("x"), lax.axis_index("y"), lax.axis_index("c")
    me = 4 * ix + 2 * iy + ic
    route = jnp.stack([ic, 2 * (1 - ix) + iy, 2 * ix + (1 - iy), 2 * (1 - ix) + (1 - iy)]).astype(jnp.int32)
    place = jnp.stack([me, 2 * ix + iy]).astype(jnp.int32)
    wt_in, mt_in, vt_in = (jnp.transpose(a, (0, 2, 1)) for a in (w_in, m_w_in, v_w_in))

    win_in = _window(wt_in, place[0:1], "w_in_window")
    c, win_in = lax.optimization_barrier((c, win_in))

    c_all = _all_gather([_pad_rows(c)], "gather_c")[0][:, 0, :]
    c_pad = _pad_rows(c_all, ADA_ROWS)
    b_cols = lax.dynamic_slice_in_dim(b_ada, me * ADA_COLS, ADA_COLS, axis=1)
    b_cols = jnp.broadcast_to(b_cols[:, None, :], (DEPTH, 8, ADA_COLS))
    mod_part = _ada_fwd(c_pad, w_ada, b_cols, "ada_fwd")
    mod_all = _all_gather([mod_part.reshape(DEPTH * ADA_ROWS, ADA_COLS)], "gather_mod")[0]
    mod_all = mod_all.reshape(N_DEV, DEPTH, ADA_ROWS, ADA_COLS)
    mod_mine = lax.dynamic_index_in_dim(mod_all, me, axis=2, keepdims=False)
    mod_mine = jnp.transpose(mod_mine, (1, 0, 2)).reshape(DEPTH, 6, D)

    cw_cols = CONV_W // N_DEV
    cw_send = jnp.pad(conv_w.reshape(DEPTH * 3, cw_cols), ((0, 8 - DEPTH * 3), (0, LANE - cw_cols)))
    send = [[w[l].astype(BF16) for w in (win_in, w_branch, w_out, w_ff1, w_ff2)] for l in range(DEPTH)]
    first = _all_gather(send[0][:1], "gather_weights_l0_in", sequencer_id=1, after=mod_all)
    rest = _all_gather(send[0][1:] + [cw_send], "gather_weights_l0_rest", sequencer_id=2, after=first[0])
    first1 = _all_gather(send[1][:1], "gather_weights_l1_in", sequencer_id=3, after=first[0])
    rest1 = _all_gather(send[1][1:], "gather_weights_l1_rest", sequencer_id=12, after=first[0])
    first, (mt_in, vt_in) = lax.optimization_barrier((first, (mt_in, vt_in)))
    gathered = [first + rest[:4], first1 + rest1]
    cw_all = rest[4][:, :DEPTH * 3, :cw_cols].reshape(N_DEV, DEPTH, 3, cw_cols)

    def first_operands(l, p_in):
        wp_bd = jnp.zeros((POOL_W, POOL_W), F32)
        for g in range(4):
            wp_bd = wp_bd.at[64 * g:64 * (g + 1), 64 * g:64 * (g + 1)].set(w_pool[l, g])
        return dict(w_in_t=_z_rows_from_windows(p_in), wp_bd=wp_bd.astype(BF16),
                    pool_scale=_pad_rows(pool_scale[l][None, :]), b_f=_pad_rows(jnp.pad(b_f[l], (0, LANE - 8))[None, :]))

    def rest_operands(l, rest):
        p_br, p_out, p_ff1, p_ff2 = rest
        w_br_full = p_br.reshape(D, D)
        cw_full = jnp.transpose(cw_all[:, l], (1, 0, 2)).reshape(3, CONV_W)
        return dict(wa=w_br_full[0:A_WIDTH], wb=w_br_full[A_WIDTH:A_WIDTH + POOL_W], wc=w_br_full[A_WIDTH + POOL_W:],
                    w_out=p_out.reshape(D, D), w_ff1=p_ff1, w_ff2=p_ff2.reshape(D_FF, D), conv_w=_pad_rows(cw_full))

    xs = x[0]
    saved, layers = [], []
    for l in range(DEPTH):
        p_in, rest = gathered[l][0], gathered[l][1:5]
        if l > 0:
            xs, p_in = lax.optimization_barrier((xs, p_in))
        wts = first_operands(l, p_in)

        def arrive(t, l=l, rest=rest, wts=wts):
            if l > 0:
                t, rest = lax.optimization_barrier((t, rest))
            wts.update(rest_operands(l, rest))
            return t

        wts["arrive"] = arrive
        gvec = _pad_rows(jnp.stack([g_mix_pre[l], g_mix_post[l], g_ff_pre[l], g_ff_post[l]]))
        layers.append((wts, gvec, _pad_rows(mod_mine[l])))
        xs, sv = _layer_fwd(l, xs, *layers[l])
        saved.append(sv)
    dx, loss_part = _loss_head(xs, loss_target[0], "loss_head")
    small_grads = [None] * DEPTH
    mine, sibs, landed = ({} for _ in range(3))
    seq_id = iter(range(4, 4 + 4 * DEPTH))
    last = [gathered[DEPTH - 1][1]]

    def start(group, grads):
        mine[group] = grads
        sibs[group] = _sibling_exchange(grads, f"rs_sibling_{group}", sequencer_id=next(seq_id), after=last[0])
        last[0] = sibs[group][0]

    def finish(group, later):
        later, (grads, sib) = lax.optimization_barrier((later, (mine[group], sibs[group])))
        sends = [_pair_sums(g, p, route, f"rs_pair_sums_{group}_{k}") for k, (g, p) in enumerate(zip(grads, sib))]
        later, sends = lax.optimization_barrier((later, sends))
        landed[group] = _chip_exchange(sends, f"rs_chips_{group}", sequencer_id=next(seq_id), after=last[0])
        last[0] = landed[group][0]
        return later

    pending = None
    for l in reversed(range(DEPTH)):
        hook = (lambda da: da) if pending is None else functools.partial(finish, pending)
        dx, ffn_grads, ffn_reds = _ffn_bwd(l, dx, saved[l], *layers[l], hook)
        start(f"ffn_l{l}", ffn_grads)
        dx, mix_grads, small_grads[l] = _mixer_bwd(l, dx, saved[l], *layers[l], ffn_reds,
                                                   functools.partial(finish, f"ffn_l{l}"))
        start(f"mix_l{l}", mix_grads)
        pending = f"mix_l{l}"
    grad_x = dx[None]

    big_w = [wt_in, w_branch, w_out, w_ff1, w_ff2]
    big_m = [mt_in, m_w_branch, m_w_out, m_w_ff1, m_w_ff2]
    big_v = [vt_in, v_w_branch, v_w_out, v_w_ff1, v_w_ff2]
    where = [("mix", 0), ("mix", 1), ("mix", 2), ("ffn", 0), ("ffn", 1)]

    def reduce_and_update(k):
        group, at = where[k]
        return _reduce_adamw([mine[f"{group}_l{l}"][at] for l in range(DEPTH)],
                             [sibs[f"{group}_l{l}"][at] for l in range(DEPTH)],
                             [landed[f"{group}_l{l}"][at] for l in range(DEPTH)], place, big_w[k], big_m[k], big_v[k],
                             f"rs_sum_adamw_{k}")

    big_res = {k: list(reduce_and_update(k)) for k in (3, 4)}
    big_res[3][0] = finish(pending, big_res[3][0])

    small = {k: jnp.stack([small_grads[l][k] for l in range(DEPTH)]) for k in SMALL_KEYS}
    payload = _pack([small[k] for k in SMALL_KEYS] + [loss_part[0:1, 0:1]], 8, F32)
    small_all = _all_gather([payload], "gather_small")[0]
    dmod_all = small_all[:, 0:DEPTH * 6, :].reshape(N_DEV, DEPTH, 6 * D)
    summed = _unpack(_sum_slabs(small_all, "sum_small").reshape(-1), SMALL_SHAPES + [(1, 1)])
    sg = dict(zip(SMALL_KEYS, summed))
    loss = summed[-1][0, 0]
    dmod_cols = lax.dynamic_slice_in_dim(dmod_all, me * ADA_COLS, ADA_COLS, axis=2)
    dmod_cols = jnp.pad(jnp.transpose(dmod_cols, (1, 0, 2)), ((0, 0), (0, ADA_ROWS - N_DEV), (0, 0)))
    g_w_ada = _ada_bwd(c_pad, dmod_cols, "ada_bwd")
    g_conv_w = lax.dynamic_slice_in_dim(sg["conv_w"], me * (CONV_W // N_DEV), CONV_W // N_DEV, axis=2)

    ada_out = [g_w_ada] + list(_adamw(w_ada, g_w_ada, m_w_ada, v_w_ada, "adamw_ada"))
    rest_w = [b_ada, g_mix_pre, g_mix_post, g_ff_pre, g_ff_post, b_f, w_pool, pool_scale, conv_w]
    rest_m = [m_b_ada, m_g_mix_pre, m_g_mix_post, m_g_ff_pre, m_g_ff_post, m_b_f, m_w_pool, m_pool_scale, m_conv_w]
    rest_v = [v_b_ada, v_g_mix_pre, v_g_mix_post, v_g_ff_pre, v_g_ff_post, v_b_f, v_w_pool, v_pool_scale, v_conv_w]
    rest_g = [sg["mod"], sg["g_mix_pre"], sg["g_mix_post"], sg["g_ff_pre"], sg["g_ff_post"], sg["b_f"],
              sg["w_pool"], sg["pool_scale"], g_conv_w]
    rest_shapes = [a.shape for a in rest_w]
    upd = _adamw(_pack(rest_w, 8, F32)[None], _pack(rest_g, 8, F32)[None], _pack(rest_m, 8, F32)[None],
                 _pack(rest_v, 8, F32)[None], "adamw_rest")
    rest_out = [rest_g] + [_unpack(arr.reshape(-1), rest_shapes) for arr in upd]
    rest_out = [[ada_out[which]] + rest_out[which] for which in range(4)]

    landed[pending], rest_out = lax.optimization_barrier((landed[pending], rest_out))
    big_res.update({k: reduce_and_update(k) for k in (0, 1, 2)})
    big_out = [[jnp.transpose(big_res[k][which], (0, 2, 1)) if k == 0 else big_res[k][which] for k in range(5)]
               for which in range(4)]

    def ordered(k):
        r, b = rest_out[k], big_out[k]
        return [r[0], r[1], r[2], r[3], r[4], r[5], b[0], r[6], r[7], r[8], r[9], b[1], b[2], b[3], b[4]]

    return (loss, grad_x, *ordered(0), *ordered(1), *ordered(2), *ordered(3))
```

```python
import functools

import jax
import jax.numpy as jnp
from jax import lax
from jax.experimental import pallas as pl
from jax.experimental.pallas import tpu as pltpu
from jax.experimental.pallas import tpu_sc as plsc

F32 = jnp.float32
BF16 = jnp.bfloat16
GRAD_DTYPE = BF16

N_DEV = 8
D = 1024
S = 2048
DEPTH = 2
D_FF = 4 * D
A_WIDTH = 512
HEAD_DIM = 64
N_PAIR = 4
POOL_W = 256
CONV_W = 256
IN_COLS = 5640
ADA_COLS = 6 * D // N_DEV
IN_SHARD = IN_COLS // N_DEV
RMS_EPS = 1e-6
NEG_INF = -1e30
ATT_SCALE = HEAD_DIM ** -0.5

NZ = 5760
Z_PC = 0
Z_G = 1024
Z_Q = 4096
Z_K = 4608
Z_V = 5120
Z_F = 5632

LR, B1, B2, EPS, WD, STEP = 0.001, 0.9, 0.999, 1e-08, 0.01, 10

LANE = 128
VMEM_LIMIT_BYTES = 48 * 1024 * 1024
TS = 512
TQ = 512
TQ_FWD = 512
HEADS_PER_STEP = 8
HEADS_PER_STEP_FWD = 8


def _params(sem=None):
    return pltpu.CompilerParams(dimension_semantics=sem, vmem_limit_bytes=VMEM_LIMIT_BYTES)


def _pick(n, target):
    best = None
    for t in range(LANE, min(n, target) + 1, LANE):
        if n % t == 0:
            best = t
    return n if best is None else best


def _matmul(a, b, mode, name, out_dtype=F32, tm=2048, tn=1024, tk=2048, b_col_shards=False, out_col_shards=False,
            extra=(), vec_extra=(), epilogue=None, out_dtypes=None, n_row_sums=0, prologue=None, prologue_tiles=(), prologue_vecs=(),
            prologue_sums=False):
    if b_col_shards:
        shards, b_rows, shard_cols = b.shape
        b_shape = (b_rows, shards * shard_cols)
    else:
        b_shape = b.shape
    if mode == "nn":
        (m, k), (k2, n) = a.shape, b_shape
    elif mode == "nt":
        (m, k), (n, k2) = a.shape, b_shape
    else:
        (k, m), (k2, n) = a.shape, b_shape
    assert k == k2, (a.shape, b.shape, mode)
    tm, tn, tk = _pick(m, tm), _pick(n, tn), _pick(k, tk)
    if b_col_shards and mode == "nn":
        tn = shard_cols
    per_step = 1
    if b_col_shards and mode == "nt":
        per_step = max(1, min(tk, 1024) // shard_cols)
        tk = per_step * shard_cols
    if out_col_shards:
        tn = n // N_DEV
    nk = k // tk
    if mode == "nn":
        a_spec = pl.BlockSpec((tm, tk), lambda i, j, kk: (i, kk))
        b_spec = (pl.BlockSpec((None, tk, tn), lambda i, j, kk: (j, kk, 0)) if b_col_shards else
                  pl.BlockSpec((tk, tn), lambda i, j, kk: (kk, j)))
        dims = (((1,), (0,)), ((), ()))
    elif mode == "nt":
        a_spec = pl.BlockSpec((tm, tk), lambda i, j, kk: (i, kk))
        b_spec = (pl.BlockSpec((per_step, tn, shard_cols), lambda i, j, kk: (kk, j, 0)) if b_col_shards else
                  pl.BlockSpec((tn, tk), lambda i, j, kk: (j, kk)))
        dims = (((1,), (1,)), ((), ()))
    else:
        assert not b_col_shards
        a_spec = pl.BlockSpec((tk, tm), lambda i, j, kk: (kk, i))
        b_spec = pl.BlockSpec((tk, tn), lambda i, j, kk: (kk, j))
        dims = (((0,), (0,)), ((), ()))
    if out_col_shards:
        out_shape = jax.ShapeDtypeStruct((N_DEV, m, tn), out_dtype)
        out_spec = pl.BlockSpec((None, tm, tn), lambda i, j, kk: (j, i, 0))
    else:
        out_shape = jax.ShapeDtypeStruct((m, n), out_dtype)
        out_spec = pl.BlockSpec((tm, tn), lambda i, j, kk: (i, j))

    n_extra = len(extra) + len(vec_extra)
    extra_specs = [pl.BlockSpec((tm, tn), lambda i, j, kk, off=off: (i, j + off // tn)) for _, off in extra]
    extra_specs += [pl.BlockSpec((8, tn), lambda i, j, kk: (0, j)) for _ in vec_extra]
    if epilogue is not None:
        assert not out_col_shards and all(off % tn == 0 for _, off in extra)
        out_shape = [jax.ShapeDtypeStruct((m, n), dt) for dt in out_dtypes]
        out_spec = [pl.BlockSpec((tm, tn), lambda i, j, kk: (i, j)) for _ in out_dtypes]
        for at in range(len(out_dtypes) - n_row_sums, len(out_dtypes)):
            out_shape[at] = jax.ShapeDtypeStruct((8 * (m // tm), n), out_dtypes[at])
            out_spec[at] = pl.BlockSpec((8, tn), lambda i, j, kk: (i, j))

    def product(a_ref, b_ref):
        if b_col_shards and mode == "nt":
            b_tile = jnp.concatenate([b_ref[s] for s in range(per_step)], axis=1) if per_step > 1 else b_ref[0]
        else:
            b_tile = b_ref[...]
        return lax.dot_general(a_ref[...].astype(BF16), b_tile.astype(BF16), dims, preferred_element_type=F32)

    def write(acc, extra_refs, o_refs):
        if epilogue is None:
            o_refs[0][...] = acc.astype(out_dtype)
        else:
            for o_ref, tile in zip(o_refs, epilogue(acc, *[r[...] for r in extra_refs])):
                o_ref[...] = tile.astype(o_ref.dtype)

    def body_one_pass(a_ref, b_ref, *refs):
        write(product(a_ref, b_ref), refs[:n_extra], refs[n_extra:])

    if prologue is not None:
        assert nk == 1 and mode in ("nn", "nt")
        n_pro = len(prologue_tiles) + len(prologue_vecs)
        sums = 1 if prologue_sums else 0
        outs = out_shape if isinstance(out_shape, list) else [out_shape]
        out_specs_all = (out_spec if isinstance(out_spec, list) else [out_spec]) + [
            pl.BlockSpec((tm, tk), lambda i, j, kk: (i, 0))]
        outs = outs + [jax.ShapeDtypeStruct((m, k), BF16)]
        if sums:
            outs.append(jax.ShapeDtypeStruct((8 * (m // tm), k), F32))
            out_specs_all.append(pl.BlockSpec((8, tk), lambda i, j, kk: (i, 0)))

        def body_prologue(a_ref, b_ref, *refs):
            pro_refs, rest = refs[:n_pro], refs[n_pro:]
            left_ref = rest[-1]
            left_out = rest[-2 - sums]

            @pl.when(pl.program_id(1) == 0)
            def _():
                made = prologue(a_ref[...], *[r[...] for r in pro_refs])
                left = (made[0] if sums else made).astype(BF16)
                left_ref[...] = left
                left_out[...] = left
                if sums:
                    rest[-2][...] = made[1]

            write(product(left_ref, b_ref), rest[:n_extra], rest[n_extra:-2 - sums])

        pro_specs = [a_spec for _ in prologue_tiles] + [pl.BlockSpec((8, tk), lambda i, j, kk: (0, 0)) for _ in prologue_vecs]
        return pl.pallas_call(
            body_prologue, name=name, out_shape=outs, grid=(m // tm, n // tn, nk),
            in_specs=[a_spec, b_spec] + pro_specs + extra_specs,
            out_specs=out_specs_all,
            scratch_shapes=[pltpu.VMEM((tm, tk), BF16)],
            compiler_params=_params(("parallel", "arbitrary", "arbitrary")),
        )(a, b, *prologue_tiles, *prologue_vecs, *[x for x, _ in extra], *vec_extra)

    def body(a_ref, b_ref, *refs):
        acc_ref = refs[-1]
        kk = pl.program_id(2)

        @pl.when(kk == 0)
        def _():
            acc_ref[...] = product(a_ref, b_ref)

        @pl.when(kk > 0)
        def _():
            acc_ref[...] += product(a_ref, b_ref)

        @pl.when(kk == nk - 1)
        def _():
            write(acc_ref[...], refs[:n_extra], refs[n_extra:-1])

    return pl.pallas_call(
        body_one_pass if nk == 1 else body, name=name,
        out_shape=out_shape,
        grid=(m // tm, n // tn, nk),
        in_specs=[a_spec, b_spec] + extra_specs,
        out_specs=out_spec,
        scratch_shapes=[] if nk == 1 else [pltpu.VMEM((tm, tn), F32)],
        compiler_params=_params(("parallel", "parallel", "arbitrary")),
    )(a, b, *[x for x, _ in extra], *vec_extra)


def _row_spec(width=D, col=0):
    return pl.BlockSpec((TS, width), lambda i: (i, col))


def _vec_spec(rows=8, width=D):
    return pl.BlockSpec((rows, width), lambda i: (0, 0))


def _rms(x):
    return lax.rsqrt(jnp.mean(x * x, axis=-1, keepdims=True) + RMS_EPS)


def _concat_columns(pieces, name):
    widths = [p.shape[1] for p in pieces]
    offsets = [sum(widths[:k]) for k in range(len(widths))]

    def body(*refs):
        o_ref = refs[-1]
        for ref, off, w in zip(refs[:-1], offsets, widths):
            o_ref[:, off:off + w] = ref[...]

    return pl.pallas_call(
        body, name=name, out_shape=jax.ShapeDtypeStruct((S, sum(widths)), pieces[0].dtype), grid=(S // TS,),
        in_specs=[_row_spec(w) for w in widths], out_specs=_row_spec(sum(widths)),
        compiler_params=_params(("parallel",)),
    )(*pieces)


def _loss_head(xf, target, name):
    def body(x_ref, t_ref, dx_ref, loss_ref):
        i = pl.program_id(0)

        @pl.when(i == 0)
        def _():
            loss_ref[...] = jnp.zeros_like(loss_ref)

        e = x_ref[...] - t_ref[...]
        dx_ref[...] = e / float(D)
        per_tok = jnp.mean(e * e, axis=-1, keepdims=True)
        loss_ref[0:1, 0:1] += 0.5 * jnp.sum(per_tok, axis=0, keepdims=True)

    return pl.pallas_call(
        body, name=name,
        out_shape=(jax.ShapeDtypeStruct((S, D), F32), jax.ShapeDtypeStruct((8, LANE), F32)),
        grid=(S // TS,),
        in_specs=[_row_spec(), _row_spec()],
        out_specs=(_row_spec(), pl.BlockSpec((8, LANE), lambda i: (0, 0))),
        compiler_params=_params(("arbitrary",)),
    )(xf, target)


def _relu2_epilogue(a):
    t = jnp.maximum(a, 0.0)
    return a, t * t


def _relu2_bwd_epilogue(dr, a):
    return (dr * (2.0 * jnp.maximum(a, 0.0)),)


def _merge_epilogue(pc, g0, g1, g2, pa, pb):
    return pc, jax.nn.sigmoid(g0) * pa + jax.nn.sigmoid(g1) * pb + jax.nn.sigmoid(g2) * pc


def _prenorm_prologue(g_row, shift_row, scale_row):
    def prologue(x, gvec, mod):
        y = x * _rms(x) * gvec[g_row:g_row + 1, :]
        return y * (1.0 + mod[scale_row:scale_row + 1, :]) + mod[shift_row:shift_row + 1, :]

    return prologue


def _rows8(*rows):
    sub = lax.broadcasted_iota(jnp.int32, (8, rows[0].shape[1]), 0)
    out = jnp.zeros((8, rows[0].shape[1]), F32)
    for k, r in enumerate(rows):
        out = jnp.where(sub == k, r, out)
    return out


def _postnorm_bwd_prologue(g_row, gate_row):
    def prologue(y, dxo, gvec, mod):
        g = gvec[g_row:g_row + 1, :]
        r = _rms(y)
        n = y * r
        dyn = dxo * mod[gate_row:gate_row + 1, :]
        dn = dyn * g
        dy = r * (dn - n * jnp.mean(dn * n, axis=-1, keepdims=True))
        return dy, _rows8(jnp.sum(dxo * (n * g), axis=0, keepdims=True), jnp.sum(dyn * n, axis=0, keepdims=True))

    return prologue


def _prenorm_bwd_epilogue(g_row, scale_row):
    def epilogue(dh, x, dres, gvec, mod):
        g = gvec[g_row:g_row + 1, :]
        r = _rms(x)
        n = x * r
        dyg = dh * (1.0 + mod[scale_row:scale_row + 1, :])
        dn = dyg * g
        dx = r * (dn - n * jnp.mean(dn * n, axis=-1, keepdims=True))
        sums = _rows8(jnp.sum(dh, axis=0, keepdims=True), jnp.sum(dh * (n * g), axis=0, keepdims=True),
                      jnp.sum(dyg * n, axis=0, keepdims=True))
        return dres + dx, sums

    return epilogue


def _postnorm_epilogue(g_row, gate_row):
    def epilogue(y, x, gvec, mod):
        yn = y * _rms(y) * gvec[g_row:g_row + 1, :]
        return y, x + mod[gate_row:gate_row + 1, :] * yn

    return epilogue


def _merge_bwd_epilogue(dm, g0, g1, g2, pa, pb, pc):
    sg = [jax.nn.sigmoid(g) for g in (g0, g1, g2)]
    return tuple(dm * s for s in sg) + tuple(dm * p * (s * (1.0 - s)) for p, s in zip((pa, pb, pc), sg))


def _shift_down(x, k, row):
    return jnp.where(row >= k, pltpu.roll(x, k, axis=0), 0.0)


def _shift_up(x, k, row):
    n = x.shape[0]
    return jnp.where(row < n - k, pltpu.roll(x, n - k, axis=0), 0.0)


def _cumsum_rows(x, row, reverse=False):
    shift = _shift_up if reverse else _shift_down
    k = 1
    while k < x.shape[0]:
        x = x + shift(x, k, row)
        k *= 2
    return x


def _full_spec(shape, idx=(0, 0)):
    return pl.BlockSpec(shape, lambda i: idx)


def _pool_window_select(lane, a2, a4, a8, a16):
    return jnp.where(lane < 64, a2, jnp.where(lane < 128, a4, jnp.where(lane < 192, a8, a16)))


def _pool_p(u, row, lane):
    t2 = u + _shift_down(u, 1, row)
    t4 = t2 + _shift_down(t2, 2, row)
    t8 = t4 + _shift_down(t4, 4, row)
    t16 = t8 + _shift_down(t8, 8, row)
    tw = _pool_window_select(lane, t2, t4, t8, t16)
    cnt = jnp.minimum((row + 1).astype(F32), _pool_window_select(lane, 2.0, 4.0, 8.0, 16.0))
    return tw / cnt - u, cnt


def _pool_fwd(z, wp_bd, pscale, name):
    def body(u_ref, w_ref, s_ref, o_ref):
        row = lax.broadcasted_iota(jnp.int32, (S, POOL_W), 0)
        lane = lax.broadcasted_iota(jnp.int32, (S, POOL_W), 1)
        p, _ = _pool_p(u_ref[...], row, lane)
        y = jnp.dot(p.astype(BF16), w_ref[...], preferred_element_type=F32)
        o_ref[...] = y * s_ref[0:1, :]

    return pl.pallas_call(
        body, name=name, out_shape=jax.ShapeDtypeStruct((S, POOL_W), F32), grid=(1,),
        in_specs=[_full_spec((S, POOL_W), (0, Z_PC // POOL_W)), _full_spec((POOL_W, POOL_W)), _full_spec((8, POOL_W))],
        out_specs=_full_spec((S, POOL_W)),
        compiler_params=_params(("arbitrary",)),
    )(z, wp_bd, pscale)


def _pool_bwd(z, wp_bd, pscale, dbr, name):
    def body(u_ref, w_ref, s_ref, dbr_ref, du_ref, dw_ref, red_ref):
        row = lax.broadcasted_iota(jnp.int32, (S, POOL_W), 0)
        lane = lax.broadcasted_iota(jnp.int32, (S, POOL_W), 1)
        p, cnt = _pool_p(u_ref[...], row, lane)
        pb = p.astype(BF16)
        y = jnp.dot(pb, w_ref[...], preferred_element_type=F32)
        dbr = dbr_ref[...]
        red_ref[...] = jnp.zeros_like(red_ref)
        red_ref[0:1, :] = jnp.sum(dbr * y, axis=0, keepdims=True)
        dy = (dbr * s_ref[0:1, :]).astype(BF16)
        dw_ref[...] = lax.dot_general(pb, dy, (((0,), (0,)), ((), ())), preferred_element_type=F32)
        dp = lax.dot_general(dy, w_ref[...], (((1,), (1,)), ((), ())), preferred_element_type=F32)
        g = dp / cnt
        a2 = g + _shift_up(g, 1, row)
        a4 = a2 + _shift_up(a2, 2, row)
        a8 = a4 + _shift_up(a4, 4, row)
        a16 = a8 + _shift_up(a8, 8, row)
        du_ref[...] = (_pool_window_select(lane, a2, a4, a8, a16) - dp).astype(BF16)

    return pl.pallas_call(
        body, name=name,
        out_shape=(jax.ShapeDtypeStruct((S, POOL_W), BF16), jax.ShapeDtypeStruct((POOL_W, POOL_W), F32),
                   jax.ShapeDtypeStruct((8, POOL_W), F32)),
        grid=(1,),
        in_specs=[_full_spec((S, POOL_W), (0, Z_PC // POOL_W)), _full_spec((POOL_W, POOL_W)), _full_spec((8, POOL_W)),
                  _full_spec((S, POOL_W))],
        out_specs=(_full_spec((S, POOL_W)), _full_spec((POOL_W, POOL_W)), _full_spec((8, POOL_W))),
        compiler_params=_params(("arbitrary",)),
    )(z, wp_bd, pscale, dbr)


def _conv_specs():
    base = Z_PC // CONV_W
    return [_full_spec((S, CONV_W), (0, base + 1)), _full_spec((S, CONV_W), (0, base + 2)),
            _full_spec((S, CONV_W), (0, base + 3)), _full_spec((8, CONV_W))]


def _conv_fwd(z, cw, name):
    def body(h_ref, b_ref, c_ref, w_ref, o_ref):
        row = lax.broadcasted_iota(jnp.int32, (S, CONV_W), 0)
        u = c_ref[...] * h_ref[...]
        y = (w_ref[0:1, :] * _shift_down(u, 2, row) + w_ref[1:2, :] * _shift_down(u, 1, row) + w_ref[2:3, :] * u)
        o_ref[...] = b_ref[...] * y

    return pl.pallas_call(
        body, name=name, out_shape=jax.ShapeDtypeStruct((S, CONV_W), F32), grid=(1,),
        in_specs=_conv_specs(), out_specs=_full_spec((S, CONV_W)),
        compiler_params=_params(("arbitrary",)),
    )(z, z, z, cw)


def _conv_bwd(z, cw, dbr, name):
    def body(h_ref, b_ref, c_ref, w_ref, dbr_ref, d_ref, red_ref):
        row = lax.broadcasted_iota(jnp.int32, (S, CONV_W), 0)
        h, cg = h_ref[...], c_ref[...]
        u = cg * h
        u1 = _shift_down(u, 1, row)
        u2 = _shift_down(u, 2, row)
        y = w_ref[0:1, :] * u2 + w_ref[1:2, :] * u1 + w_ref[2:3, :] * u
        dbr = dbr_ref[...]
        dy = dbr * b_ref[...]
        du = w_ref[2:3, :] * dy + w_ref[1:2, :] * _shift_up(dy, 1, row) + w_ref[0:1, :] * _shift_up(dy, 2, row)
        d_ref[:, 0:CONV_W] = (du * cg).astype(BF16)
        d_ref[:, CONV_W:2 * CONV_W] = (dbr * y).astype(BF16)
        d_ref[:, 2 * CONV_W:3 * CONV_W] = (du * h).astype(BF16)
        red_ref[...] = jnp.zeros_like(red_ref)
        red_ref[0:1, :] = jnp.sum(dy * u2, axis=0, keepdims=True)
        red_ref[1:2, :] = jnp.sum(dy * u1, axis=0, keepdims=True)
        red_ref[2:3, :] = jnp.sum(dy * u, axis=0, keepdims=True)

    return pl.pallas_call(
        body, name=name,
        out_shape=(jax.ShapeDtypeStruct((S, 3 * CONV_W), BF16), jax.ShapeDtypeStruct((8, CONV_W), F32)),
        grid=(1,),
        in_specs=_conv_specs() + [_full_spec((S, CONV_W))],
        out_specs=(_full_spec((S, 3 * CONV_W)), _full_spec((8, CONV_W))),
        compiler_params=_params(("arbitrary",)),
    )(z, z, z, cw, dbr)


_NT = (((1,), (1,)), ((), ()))
_TN = (((0,), (0,)), ((), ()))
N_HEAD = 2 * N_PAIR


def _split3(x):
    hi = x.astype(BF16).astype(F32)
    mid = (x - hi).astype(BF16).astype(F32)
    lo = (x - hi - mid).astype(BF16).astype(F32)
    return hi, mid, lo


def _spare(lane, e, k):
    return lane == 64 * (1 - e) + k


def _spare3(lane, e, k):
    base = 64 * (1 - e) + k
    return (lane >= base) & (lane < base + 3)


def _put3(lane, e, k, pieces, rest):
    out = rest
    for n, piece in enumerate(pieces):
        out = jnp.where(_spare(lane, e, k + n), piece, out)
    return out


def _attn_prep(z, bf, name):
    def body(q_ref, k_ref, v_ref, f_ref, b_ref, qa_ref, ka_ref, va_ref, kat_ref, cum_ref):
        p = pl.program_id(0)
        row = lax.broadcasted_iota(jnp.int32, (S, LANE), 0)
        lane = lax.broadcasted_iota(jnp.int32, (S, LANE), 1)

        @pl.when(p == 0)
        def _():
            xv = f_ref[...] + b_ref[0:1, :]
            ls = jnp.minimum(xv, 0.0) - jnp.log(1.0 + jnp.exp(-jnp.abs(xv)))
            cum_ref[...] = _cumsum_rows(jnp.where(lane < N_HEAD, ls, 0.0), row)

        cum = cum_ref[...]
        q, k, v = q_ref[...], k_ref[...], v_ref[...]
        for e in range(2):
            head = (lane >= 64) if e else (lane < 64)
            f = jnp.sum(jnp.where(lane == 2 * p + e, cum, 0.0), axis=1, keepdims=True)
            pieces = _split3(f)
            qa = jnp.where(head, q * ATT_SCALE, _put3(lane, e, 0, pieces, jnp.where(_spare3(lane, e, 3), 1.0, 0.0)))
            ones = jnp.where(_spare3(lane, e, 0) | _spare3(lane, e, 6), 1.0, 0.0)
            ka = jnp.where(head, k, _put3(lane, e, 3, [-x for x in pieces], ones))
            va = jnp.where(head, v, jnp.where(_spare3(lane, e, 0), 1.0, 0.0))
            qa_ref[e] = qa.astype(BF16)
            ka_ref[e] = ka.astype(BF16)
            va_ref[e] = va.astype(BF16)
            kat_ref[e] = ka.T.astype(BF16)

    qb, kb, vb = Z_Q // LANE, Z_K // LANE, Z_V // LANE
    heads = jax.ShapeDtypeStruct((N_HEAD, S, LANE), BF16)
    pair = pl.BlockSpec((2, S, LANE), lambda p: (p, 0, 0))
    return pl.pallas_call(
        body, name=name,
        out_shape=(heads, heads, heads, jax.ShapeDtypeStruct((N_HEAD, LANE, S), BF16)),
        grid=(N_PAIR,),
        in_specs=[pl.BlockSpec((S, LANE), lambda p: (0, qb + p)), pl.BlockSpec((S, LANE), lambda p: (0, kb + p)),
                  pl.BlockSpec((S, LANE), lambda p: (0, vb + p)), pl.BlockSpec((S, LANE), lambda p: (0, Z_F // LANE)),
                  pl.BlockSpec((8, LANE), lambda p: (0, 0))],
        out_specs=(pair, pair, pair, pl.BlockSpec((2, LANE, S), lambda p: (p, 0, 0))),
        scratch_shapes=[pltpu.VMEM((S, LANE), F32)],
        compiler_params=_params(("arbitrary",)),
    )(z, z, z, z, bf)


def _attn_bwd_prep(qa, o, lse, do, name):
    def body(qa_ref, o_ref, lse_ref, do_ref, qa2_ref, doa_ref):
        lane = lax.broadcasted_iota(jnp.int32, (S, LANE), 1)
        dov, ov, lsev = do_ref[...], o_ref[...], lse_ref[...]
        for e in range(2):
            head = (lane >= 64) if e else (lane < 64)
            dsum = jnp.sum(jnp.where(head, dov * ov, 0.0), axis=1, keepdims=True)
            doa_ref[e] = jnp.where(head, dov, _put3(lane, e, 0, [-x for x in _split3(dsum)], 0.0)).astype(BF16)
            lse_col = lsev[:, 64 * e:64 * e + 1]
            qa2_ref[e] = _put3(lane, e, 6, [-x for x in _split3(lse_col)], qa_ref[e].astype(F32)).astype(BF16)

    heads = jax.ShapeDtypeStruct((N_HEAD, S, LANE), BF16)
    pair = pl.BlockSpec((2, S, LANE), lambda p: (p, 0, 0))
    cols = pl.BlockSpec((S, LANE), lambda p: (0, p))
    return pl.pallas_call(
        body, name=name, out_shape=(heads, heads), grid=(N_PAIR,),
        in_specs=[pair, cols, cols, cols], out_specs=(pair, pair),
        compiler_params=_params(("parallel",)),
    )(qa, o, lse, do)


def _attn_bwd_post(z, bf, dqt, dka, dva, name):
    def body(f_ref, b_ref, dqt_ref, dk_ref, dv_ref, dq_out, dk_out, dv_out, dfl_ref, red_ref, dcum_ref):
        p = pl.program_id(0)

        @pl.when(p == 0)
        def _():
            dcum_ref[...] = jnp.zeros_like(dcum_ref)

        row = lax.broadcasted_iota(jnp.int32, (S, LANE), 0)
        lane = lax.broadcasted_iota(jnp.int32, (S, LANE), 1)
        dqa = [dqt_ref[e].T for e in range(2)]
        dq_out[...] = (jnp.where(lane < 64, dqa[0], dqa[1]) * ATT_SCALE).astype(BF16)
        dk_out[...] = jnp.where(lane < 64, dk_ref[0], dk_ref[1]).astype(BF16)
        dv_out[...] = jnp.where(lane < 64, dv_ref[0], dv_ref[1]).astype(BF16)
        for e in range(2):
            d_query = jnp.sum(jnp.where(_spare(lane, e, 0), dqa[e], 0.0), axis=1, keepdims=True)
            d_key = jnp.sum(jnp.where(_spare(lane, e, 3), dk_ref[e], 0.0), axis=1, keepdims=True)
            dcum_ref[...] += jnp.where(lane == 2 * p + e, d_query - d_key, 0.0)

        @pl.when(p == N_PAIR - 1)
        def _():
            dls = _cumsum_rows(dcum_ref[...], row, reverse=True)
            xv = f_ref[...] + b_ref[0:1, :]
            dx = jnp.where(lane < N_HEAD, dls * jax.nn.sigmoid(-xv), 0.0)
            dfl_ref[...] = dx.astype(BF16)
            red_ref[...] = jnp.zeros_like(red_ref)
            red_ref[0:1, :] = jnp.sum(dx, axis=0, keepdims=True)

    wide = jax.ShapeDtypeStruct((S, N_PAIR * LANE), BF16)
    cols = pl.BlockSpec((S, LANE), lambda p: (0, p))
    pair = pl.BlockSpec((2, S, LANE), lambda p: (p, 0, 0))
    return pl.pallas_call(
        body, name=name,
        out_shape=(wide, wide, wide, jax.ShapeDtypeStruct((S, LANE), BF16), jax.ShapeDtypeStruct((8, LANE), F32)),
        grid=(N_PAIR,),
        in_specs=[pl.BlockSpec((S, LANE), lambda p: (0, Z_F // LANE)), pl.BlockSpec((8, LANE), lambda p: (0, 0)),
                  pl.BlockSpec((2, LANE, S), lambda p: (p, 0, 0)), pair, pair],
        out_specs=(cols, cols, cols, pl.BlockSpec((S, LANE), lambda p: (0, 0)), pl.BlockSpec((8, LANE), lambda p: (0, 0))),
        scratch_shapes=[pltpu.VMEM((S, LANE), F32)],
        compiler_params=_params(("arbitrary",)),
    )(z, bf, dqt, dka, dva)


def _attn_fwd(qa, ka, va, name):
    tq, tk = TQ_FWD, TQ
    ratio = tq // tk

    def body(qa_ref, ka_ref, va_ref, o_ref, lse_ref):
        i = pl.program_id(1)
        lane = lax.broadcasted_iota(jnp.int32, (tq, LANE), 1)
        row = lax.broadcasted_iota(jnp.int32, (tq, tk), 0)
        col = lax.broadcasted_iota(jnp.int32, (tq, tk), 1)
        nh = HEADS_PER_STEP_FWD
        qs = [qa_ref[h] for h in range(nh)]

        def block(j, carry, masked):
            off = pl.multiple_of(j * tk, tk)
            out = []
            for h in range(nh):
                m, acc = carry[h]
                s = lax.dot_general(qs[h], ka_ref[h, pl.ds(off, tk), :], _NT, preferred_element_type=F32)
                if masked:
                    s = jnp.where(col + (j - ratio * i) * tk > row, NEG_INF, s)
                mn = jnp.maximum(m, jnp.max(s, axis=1, keepdims=True))
                p = jnp.exp(s - mn).astype(BF16)
                acc = jnp.exp(m - mn) * acc + jnp.dot(p, va_ref[h, pl.ds(off, tk), :], preferred_element_type=F32)
                out.append((mn, acc))
            return tuple(out)

        init = (jnp.full((tq, 1), NEG_INF, F32), jnp.zeros((tq, LANE), F32))
        carry = lax.fori_loop(0, ratio * i, lambda j, c: block(j, c, False), (init,) * nh)
        for d in range(ratio):
            carry = block(ratio * i + d, carry, True)
        res = []
        for h in range(nh):
            m, acc = carry[h]
            l = jnp.sum(jnp.where(_spare(lane, h % 2, 0), acc, 0.0), axis=1, keepdims=True)
            res.append((acc / l, m + jnp.log(l)))
        for g in range(nh // 2):
            o_ref[:, g * LANE:(g + 1) * LANE] = jnp.where(lane < 64, res[2 * g][0], res[2 * g + 1][0])
            lse_ref[:, g * LANE:(g + 1) * LANE] = jnp.where(lane < 64, res[2 * g][1], res[2 * g + 1][1])

    nh = HEADS_PER_STEP_FWD
    out = jax.ShapeDtypeStruct((S, N_PAIR * LANE), F32)
    wide = pl.BlockSpec((tq, 64 * nh), lambda p, i: (i, p))
    return pl.pallas_call(
        body, name=name, out_shape=(out, out), grid=(N_HEAD // nh, S // tq),
        in_specs=[pl.BlockSpec((nh, tq, LANE), lambda p, i: (p, i, 0)), pl.BlockSpec((nh, S, LANE), lambda p, i: (p, 0, 0)),
                  pl.BlockSpec((nh, S, LANE), lambda p, i: (p, 0, 0))],
        out_specs=(wide, wide),
        compiler_params=_params(("parallel", "parallel")),
    )(qa, ka, va)


def _attn_bwd(qa2, ka, va, kat, doa, name):
    nq = S // TQ

    def body(qa_ref, ka_ref, va_ref, kat_ref, doa_ref, dqt_ref, dk_ref, dv_ref):
        j = pl.program_id(1)

        @pl.when(j == 0)
        def _():
            dqt_ref[...] = jnp.zeros_like(dqt_ref)

        key = lax.broadcasted_iota(jnp.int32, (TQ, TQ), 0)
        qry = lax.broadcasted_iota(jnp.int32, (TQ, TQ), 1)
        nh = HEADS_PER_STEP
        kav, vav, katv = ([ref[h] for h in range(nh)] for ref in (ka_ref, va_ref, kat_ref))

        def block(i, carry, masked):
            off = pl.multiple_of(i * TQ, TQ)
            out = []
            for h in range(nh):
                dk_acc, dv_acc = carry[h]
                qav = qa_ref[h, pl.ds(off, TQ), :]
                doav = doa_ref[h, pl.ds(off, TQ), :]
                s_t = lax.dot_general(kav[h], qav, _NT, preferred_element_type=F32)
                if masked:
                    s_t = jnp.where(key > qry, NEG_INF, s_t)
                p_t = jnp.exp(s_t)
                ds_t = p_t * lax.dot_general(vav[h], doav, _NT, preferred_element_type=F32)
                dsb = ds_t.astype(BF16)
                dv_acc = dv_acc + jnp.dot(p_t.astype(BF16), doav, preferred_element_type=F32)
                dk_acc = dk_acc + jnp.dot(dsb, qav, preferred_element_type=F32)
                dqt_ref[h, :, pl.ds(off, TQ)] += jnp.dot(katv[h], dsb, preferred_element_type=F32)
                out.append((dk_acc, dv_acc))
            return tuple(out)

        zero = (jnp.zeros((TQ, LANE), F32), jnp.zeros((TQ, LANE), F32))
        carry = block(j, (zero,) * nh, True)
        carry = lax.fori_loop(j + 1, nq, lambda i, c: block(i, c, False), carry)
        for h in range(nh):
            dk_ref[h], dv_ref[h] = carry[h]

    nh = HEADS_PER_STEP
    full = pl.BlockSpec((nh, S, LANE), lambda p, j: (p, 0, 0))
    blk = pl.BlockSpec((nh, TQ, LANE), lambda p, j: (p, j, 0))
    acc = jax.ShapeDtypeStruct((N_HEAD, S, LANE), F32)
    return pl.pallas_call(
        body, name=name,
        out_shape=(jax.ShapeDtypeStruct((N_HEAD, LANE, S), F32), acc, acc),
        grid=(N_HEAD // nh, nq),
        in_specs=[full, blk, blk, pl.BlockSpec((nh, LANE, TQ), lambda p, j: (p, 0, j)), full],
        out_specs=(pl.BlockSpec((nh, LANE, S), lambda p, j: (p, 0, 0)), blk, blk),
        compiler_params=_params(("arbitrary", "arbitrary")),
    )(qa2, ka, va, kat, doa)


ADA_ROWS = 16


def _ada_fwd(c_pad, w_ada, b_cols, name):
    def body(c_ref, w_ref, b_ref, o_ref):
        cv = c_ref[...]
        sc = (cv * jax.nn.sigmoid(cv)).astype(BF16)
        o_ref[0] = jnp.dot(sc, w_ref[0].astype(BF16), preferred_element_type=F32) + b_ref[0, 0:1, :]

    return pl.pallas_call(
        body, name=name, out_shape=jax.ShapeDtypeStruct((DEPTH, ADA_ROWS, ADA_COLS), F32), grid=(DEPTH,),
        in_specs=[pl.BlockSpec((ADA_ROWS, D), lambda l: (0, 0)), pl.BlockSpec((1, D, ADA_COLS), lambda l: (l, 0, 0)),
                  pl.BlockSpec((1, 8, ADA_COLS), lambda l: (l, 0, 0))],
        out_specs=pl.BlockSpec((1, ADA_ROWS, ADA_COLS), lambda l: (l, 0, 0)),
        compiler_params=_params(("parallel",)),
    )(c_pad, w_ada, b_cols)


def _ada_bwd(c_pad, dmod_cols, name):
    def body(c_ref, d_ref, o_ref):
        cv = c_ref[...]
        sc = (cv * jax.nn.sigmoid(cv)).astype(BF16)
        o_ref[0] = lax.dot_general(sc, d_ref[0].astype(BF16), _TN, preferred_element_type=F32)

    return pl.pallas_call(
        body, name=name, out_shape=jax.ShapeDtypeStruct((DEPTH, D, ADA_COLS), F32), grid=(DEPTH,),
        in_specs=[pl.BlockSpec((ADA_ROWS, D), lambda l: (0, 0)), pl.BlockSpec((1, ADA_ROWS, ADA_COLS), lambda l: (l, 0, 0))],
        out_specs=pl.BlockSpec((1, D, ADA_COLS), lambda l: (l, 0, 0)),
        compiler_params=_params(("parallel",)),
    )(c_pad, dmod_cols)


def _adamw_math(w, g, m, v):
    m = B1 * m + (1.0 - B1) * g
    v = B2 * v + (1.0 - B2) * (g * g)
    m_hat = m / (1.0 - B1 ** STEP)
    v_hat = v / (1.0 - B2 ** STEP)
    delta = -LR * (m_hat / (jnp.sqrt(v_hat) + EPS) + WD * w)
    return delta, m, v


def _row_tile(rows, target=256):
    best = 8
    for t in range(8, min(rows, target) + 1, 8):
        if rows % t == 0:
            best = t
    return best


def _adamw(w, g, m, v, name):
    layers, rows, cols = w.shape
    tr = _row_tile(rows)
    spec = pl.BlockSpec((1, tr, cols), lambda l, i: (l, i, 0))

    def body(w_ref, g_ref, m_ref, v_ref, d_ref, nm_ref, nv_ref):
        d_ref[...], nm_ref[...], nv_ref[...] = _adamw_math(w_ref[...], g_ref[...], m_ref[...], v_ref[...])

    out = jax.ShapeDtypeStruct(w.shape, F32)
    return pl.pallas_call(
        body, name=name, out_shape=(out, out, out), grid=(layers, rows // tr),
        in_specs=[spec] * 4, out_specs=(spec,) * 3, compiler_params=_params(("parallel", "parallel")),
    )(w, g, m, v)


def _sum_slabs(x, name):
    n, rows, _ = x.shape
    tr = _row_tile(rows)

    def body(x_ref, o_ref):
        acc = x_ref[0]
        for k in range(1, n):
            acc = acc + x_ref[k]
        o_ref[...] = acc

    return pl.pallas_call(
        body, name=name, out_shape=jax.ShapeDtypeStruct((rows, D), F32), grid=(rows // tr,),
        in_specs=[pl.BlockSpec((n, tr, D), lambda i: (0, i, 0))], out_specs=pl.BlockSpec((tr, D), lambda i: (i, 0)),
        compiler_params=_params(("parallel",)),
    )(x)


_ANY = pl.BlockSpec(memory_space=pl.ANY)
MESH = pl.DeviceIdType.MESH


def _on_sequencer(body, out_shape, sems, operands, after, sequencer_id, name):
    n = len(operands)

    def ordered_body(*refs):
        body(*refs[:n], *refs[n + 1:])

    extra = [] if after is None else [after]
    return pl.kernel(
        body if after is None else ordered_body, out_type=out_shape,
        mesh=plsc.ScalarSubcoreMesh(axis_name="sequencer", num_cores=1), scratch_types=sems,
        compiler_params=pltpu.CompilerParams(collective_id=sequencer_id), name=name)(*operands, *extra)


def _all_gather(xs, name, sequencer_id=None, after=None):
    n = len(xs)

    def body(*refs):
        x_refs, out_refs = refs[:n], refs[n:2 * n]
        send_sems, recv_sems, local_sems = refs[2 * n:]
        x_, y_, c_ = lax.axis_index("x"), lax.axis_index("y"), lax.axis_index("c")
        me, sibling = (x_, y_, c_), (x_, y_, 1 - c_)
        chips = [(1 - x_, y_), (x_, 1 - y_), (1 - x_, 1 - y_)]
        if sequencer_id is not None:
            barrier = pltpu.get_barrier_semaphore()
            peers = [sibling] + [(*chip, pc) for chip in chips for pc in (c_, 1 - c_)]
            for peer in peers:
                pl.semaphore_signal(barrier, inc=1, device_id=peer, device_id_type=MESH)
            pl.semaphore_wait(barrier, len(peers))

        def slot(a, px, py, pc):
            return out_refs[a].at[4 * px + 2 * py + pc]

        def copy(a, k, block, to, src=None):
            return pltpu.make_async_remote_copy(
                src_ref=slot(a, *block) if src is None else src, dst_ref=slot(a, *block),
                send_sem=send_sems.at[7 * a + k], recv_sem=recv_sems.at[7 * a + k], device_id=to, device_id_type=MESH)

        mine = [pltpu.make_async_copy(x_refs[a], slot(a, *me), local_sems.at[a]) for a in range(n)]
        for cp in mine:
            cp.start()
        first = []
        for a in range(n):
            first.append(copy(a, 0, me, sibling, src=x_refs[a]))
            first += [copy(a, 1 + j, me, (*chip, c_), src=x_refs[a]) for j, chip in enumerate(chips)]
        for cp in first:
            cp.start()
        passed = []
        for j, chip in enumerate(chips):
            for a in range(n):
                copy(a, 1 + j, (*chip, c_), me).wait_recv()
                passed.append(copy(a, 4 + j, (*chip, c_), sibling))
                passed[-1].start()
        for a in range(n):
            copy(a, 0, sibling, me).wait_recv()
        for j, chip in enumerate(chips):
            for a in range(n):
                copy(a, 4 + j, (*chip, 1 - c_), me).wait_recv()
        for cp in first + passed:
            cp.wait_send()
        for cp in mine:
            cp.wait()

    out_shape = [jax.ShapeDtypeStruct((N_DEV,) + x.shape, x.dtype) for x in xs]
    sems = [pltpu.SemaphoreType.DMA((7 * n,)), pltpu.SemaphoreType.DMA((7 * n,)), pltpu.SemaphoreType.DMA((n,))]
    if sequencer_id is not None:
        return _on_sequencer(body, out_shape, sems, xs, after, sequencer_id, name)
    return pl.pallas_call(
        body, name=name, out_shape=out_shape, in_specs=[_ANY] * n, out_specs=[_ANY] * n, scratch_shapes=sems)(*xs)


def _sibling_exchange(gs, name, sequencer_id=None, after=None):
    n = len(gs)

    def body(*refs):
        g_refs, p_refs = refs[:n], refs[n:2 * n]
        send_sems, recv_sems = refs[2 * n:]
        x_, y_, c_ = lax.axis_index("x"), lax.axis_index("y"), lax.axis_index("c")
        if sequencer_id is not None:
            barrier = pltpu.get_barrier_semaphore()
            pl.semaphore_signal(barrier, inc=1, device_id=(x_, y_, 1 - c_), device_id_type=MESH)
            pl.semaphore_wait(barrier, 1)
        copies = [pltpu.make_async_remote_copy(
            src_ref=g_refs[a].at[2 * k + (1 - c_)], dst_ref=p_refs[a].at[k], send_sem=send_sems.at[4 * a + k],
            recv_sem=recv_sems.at[4 * a + k], device_id=(x_, y_, 1 - c_), device_id_type=MESH)
            for a in range(n) for k in range(4)]
        for cp in copies:
            cp.start()
        for cp in copies:
            cp.wait()

    out_shape = [jax.ShapeDtypeStruct((4,) + g.shape[1:], g.dtype) for g in gs]
    sems = [pltpu.SemaphoreType.DMA((4 * n,)), pltpu.SemaphoreType.DMA((4 * n,))]
    if sequencer_id is not None:
        return _on_sequencer(body, out_shape, sems, gs, after, sequencer_id, name)
    return pl.pallas_call(
        body, name=name, out_shape=out_shape, in_specs=[_ANY] * n, out_specs=[_ANY] * n, scratch_shapes=sems)(*gs)


def _slab_tiles(rows, cols):
    if rows % 8 == 0:
        return _row_tile(rows), cols
    return rows, 2 * LANE


def _pair_sums(g, p, route, name):
    _, rows, cols = g.shape
    tr, tc = _slab_tiles(rows, cols)

    def body(route_ref, g_ref, p_ref, t_ref):
        t_ref[...] = (g_ref[...].astype(F32) + p_ref[...].astype(F32)).astype(BF16)

    return pl.pallas_call(
        body, name=name, out_shape=jax.ShapeDtypeStruct((3, rows, cols), BF16),
        grid_spec=pltpu.PrefetchScalarGridSpec(
            num_scalar_prefetch=1, grid=(3, rows // tr, cols // tc),
            in_specs=[pl.BlockSpec((1, tr, tc), lambda r, i, j, route_ref: (2 * route_ref[1 + r] + route_ref[0], i, j)),
                      pl.BlockSpec((1, tr, tc), lambda r, i, j, route_ref: (route_ref[1 + r], i, j))],
            out_specs=pl.BlockSpec((1, tr, tc), lambda r, i, j, route_ref: (r, i, j))),
        compiler_params=_params(("parallel", "parallel", "parallel")),
    )(route, g, p)


def _chip_exchange(ts, name, sequencer_id=None, after=None):
    n = len(ts)

    def body(*refs):
        t_refs, l_refs = refs[:n], refs[n:2 * n]
        send_sems, recv_sems = refs[2 * n:]
        x_, y_, c_ = lax.axis_index("x"), lax.axis_index("y"), lax.axis_index("c")
        chips = [(1 - x_, y_), (x_, 1 - y_), (1 - x_, 1 - y_)]
        if sequencer_id is not None:
            barrier = pltpu.get_barrier_semaphore()
            for px, py in chips:
                pl.semaphore_signal(barrier, inc=1, device_id=(px, py, c_), device_id_type=MESH)
            pl.semaphore_wait(barrier, len(chips))
        copies = [pltpu.make_async_remote_copy(
            src_ref=t_refs[a].at[r], dst_ref=l_refs[a].at[r], send_sem=send_sems.at[3 * a + r],
            recv_sem=recv_sems.at[3 * a + r], device_id=(px, py, c_), device_id_type=MESH)
            for a in range(n) for r, (px, py) in enumerate(chips)]
        for cp in copies:
            cp.start()
        for cp in copies:
            cp.wait()

    out_shape = [jax.ShapeDtypeStruct((3,) + t.shape[1:], t.dtype) for t in ts]
    sems = [pltpu.SemaphoreType.DMA((3 * n,)), pltpu.SemaphoreType.DMA((3 * n,))]
    if sequencer_id is not None:
        return _on_sequencer(body, out_shape, sems, ts, after, sequencer_id, name)
    return pl.pallas_call(
        body, name=name, out_shape=out_shape, in_specs=[_ANY] * n, out_specs=[_ANY] * n, scratch_shapes=sems)(*ts)


def _reduce_adamw(gs, ps, landed, place, w, m, v, name):
    layers, rows, cols = w.shape
    assert layers == DEPTH == 2
    tr, tc = _slab_tiles(rows, cols)
    nr, nc = rows // tr, cols // tc
    spec = pl.BlockSpec((1, tr, tc), lambda l, i, j, place_ref: (l, i, j))

    def own(layer, which):
        pi, pj = (nr - 1, nc - 1) if layer == 0 else (0, 0)

        def index(l, i, j, place_ref):
            lead = 0 if which is None else place_ref[which]
            return lead, jnp.where(l == layer, i, pi), jnp.where(l == layer, j, pj)

        return pl.BlockSpec((3 if which is None else 1, tr, tc), index)

    def body(place_ref, g0_ref, p0_ref, l0_ref, g1_ref, p1_ref, l1_ref, w_ref, m_ref, v_ref,
             g_ref, d_ref, nm_ref, nv_ref):
        def update(own_ref, sib_ref, l_ref):
            g = (own_ref[0].astype(F32) + sib_ref[0].astype(F32) + l_ref[0].astype(F32) + l_ref[1].astype(F32)
                 + l_ref[2].astype(F32))
            g_ref[0] = g
            d_ref[0], nm_ref[0], nv_ref[0] = _adamw_math(w_ref[0], g, m_ref[0], v_ref[0])

        @pl.when(pl.program_id(0) == 0)
        def _():
            update(g0_ref, p0_ref, l0_ref)

        @pl.when(pl.program_id(0) == 1)
        def _():
            update(g1_ref, p1_ref, l1_ref)

    out = jax.ShapeDtypeStruct(w.shape, F32)
    return pl.pallas_call(
        body, name=name, out_shape=(out, out, out, out),
        grid_spec=pltpu.PrefetchScalarGridSpec(
            num_scalar_prefetch=1, grid=(DEPTH, nr, nc),
            in_specs=[own(0, 0), own(0, 1), own(0, None), own(1, 0), own(1, 1), own(1, None), spec, spec, spec],
            out_specs=(spec, spec, spec, spec)),
        compiler_params=_params(("arbitrary", "arbitrary", "arbitrary")),
    )(place, gs[0], ps[0], landed[0], gs[1], ps[1], landed[1], w, m, v)


def _pack(pieces, row_multiple, dtype, cols=D, rows=None):
    flat = jnp.concatenate([p.astype(dtype).reshape(-1) for p in pieces])
    if rows is None:
        rows = -(-flat.shape[0] // cols)
        rows = -(-rows // row_multiple) * row_multiple
    flat = jnp.pad(flat, (0, rows * cols - flat.shape[0]))
    return flat.reshape(rows, cols)


def _unpack(flat, shapes, lead=()):
    out, off = [], 0
    for shp in shapes:
        n = 1
        for s_ in shp:
            n *= s_
        out.append(lax.slice_in_dim(flat, off, off + n, axis=len(lead)).reshape(lead + tuple(shp)))
        off += n
    return out


WIN_STRIDE = 704
WIN_ROWS = 720
Z_TURN = 1544


def _window(wt, me, name):
    padded = jnp.pad(wt, ((0, 0), (0, WIN_ROWS - IN_SHARD), (0, 0)))

    def body(me_ref, x_ref, o_ref):
        o_ref[0] = pltpu.roll(x_ref[0], me_ref[0], axis=0).astype(BF16)

    spec = pl.BlockSpec((1, WIN_ROWS, D), lambda l, me_ref: (l, 0, 0))
    return pl.pallas_call(
        body, name=name, out_shape=jax.ShapeDtypeStruct((DEPTH, WIN_ROWS, D), BF16),
        grid_spec=pltpu.PrefetchScalarGridSpec(num_scalar_prefetch=1, grid=(DEPTH,), in_specs=[spec], out_specs=spec),
        compiler_params=_params(("parallel",)),
    )(me, padded)


def _z_rows_from_windows(win):
    over = WIN_ROWS - WIN_STRIDE
    pieces = [(0, win[0][0:WIN_STRIDE])]
    for d in range(1, N_DEV):
        base = WIN_STRIDE * d
        pieces.append((base, win[d - 1][WIN_STRIDE:WIN_ROWS] + win[d][0:over]))
        pieces.append((base + over, win[d][over:WIN_STRIDE]))
    pieces.append((WIN_STRIDE * N_DEV, win[N_DEV - 1][WIN_STRIDE:WIN_ROWS]))

    def rows(a, b):
        out = []
        for start, arr in pieces:
            lo, hi = max(a, start), min(b, start + arr.shape[0])
            if lo < hi:
                out.append(arr[lo - start:hi - start])
        return out

    pad = jnp.zeros((NZ - IN_COLS, win.shape[-1]), win.dtype)
    return jnp.concatenate(rows(Z_TURN, IN_COLS) + rows(0, Z_TURN) + [pad], axis=0)


def _in_rows_from_z(wt):
    return jnp.concatenate([wt[Z_Q:Z_Q + 1536], wt[Z_F:Z_F + 8], wt[Z_PC:Z_PC + 1024], wt[Z_G:Z_G + 3072]], axis=0)


def _pad_rows(v, rows=8):
    return jnp.pad(v, ((0, rows - v.shape[0]), (0, 0)))


def _layer_fwd(l, x, wts, gvec, mod):
    tag = f"l{l}"
    z, h = _matmul(x, wts["w_in_t"], "nt", f"in_proj_{tag}", tm=1024, tn=1152, prologue=_prenorm_prologue(0, 0, 1),
                   prologue_vecs=[gvec, mod])
    qa, ka, va, kat = _attn_prep(z, wts["b_f"], f"attn_prep_{tag}")
    qa = wts["arrive"](qa)
    o, lse = _attn_fwd(qa, ka, va, f"attn_{tag}")
    br_b = _pool_fwd(z, wts["wp_bd"], wts["pool_scale"], f"pool_{tag}")
    br_c = _conv_fwd(z, wts["conv_w"], f"conv_{tag}")
    pa = _matmul(o, wts["wa"], "nn", f"proj_a_{tag}", out_dtype=BF16)
    pb = _matmul(br_b, wts["wb"], "nn", f"proj_b_{tag}", out_dtype=BF16)
    gates = [(z, Z_G + k * D) for k in range(3)]
    pc, merged = _matmul(br_c, wts["wc"], "nn", f"proj_c_merge_{tag}", tm=1024, tn=512,
                         extra=gates + [(pa, 0), (pb, 0)], epilogue=_merge_epilogue, out_dtypes=(BF16, BF16))
    y, x1 = _matmul(merged, wts["w_out"], "nn", f"out_proj_{tag}", tm=1024, tn=D, extra=[(x, 0)],
                    vec_extra=[gvec, mod], epilogue=_postnorm_epilogue(1, 2), out_dtypes=(F32, F32))
    a, r, h2 = _matmul(x1, wts["w_ff1"], "nn", f"ff1_{tag}", b_col_shards=True, epilogue=_relu2_epilogue,
                       out_dtypes=(BF16, BF16), prologue=_prenorm_prologue(2, 3, 4), prologue_vecs=[gvec, mod])
    y2, x2 = _matmul(r, wts["w_ff2"], "nn", f"ff2_{tag}", tm=1024, tn=D, tk=1024, extra=[(x1, 0)],
                     vec_extra=[gvec, mod], epilogue=_postnorm_epilogue(3, 5), out_dtypes=(F32, F32))
    saved = dict(x=x, h=h, z=z, qa=qa, ka=ka, va=va, kat=kat, o=o, lse=lse, br_b=br_b, br_c=br_c, pa=pa, pb=pb, pc=pc,
                 merged=merged, y=y, x1=x1, h2=h2, a=a, r=r, y2=y2)
    return x2, saved


def _ffn_bwd(l, dx2, sv, wts, gvec, mod, midpoint):
    tag = f"l{l}"
    dx2 = midpoint(dx2)
    da, dy2, sums = _matmul(sv["y2"], wts["w_ff2"], "nt", f"ff2_dx_{tag}", tm=1024, extra=[(sv["a"], 0)],
                            epilogue=_relu2_bwd_epilogue, out_dtypes=(BF16,), prologue=_postnorm_bwd_prologue(3, 5),
                            prologue_tiles=[dx2], prologue_vecs=[gvec, mod], prologue_sums=True)
    red_post_ff = jnp.sum(sums.reshape(-1, 8, D), axis=0)
    d_w_ff2 = _matmul(sv["r"], dy2, "tn", f"ff2_dw_{tag}", out_dtype=GRAD_DTYPE)
    dx1, sums = _matmul(da, wts["w_ff1"], "nt", f"ff1_dx_{tag}", tm=1024, tn=D, b_col_shards=True,
                        extra=[(sv["x1"], 0), (dx2, 0)], vec_extra=[gvec, mod], epilogue=_prenorm_bwd_epilogue(2, 4),
                        out_dtypes=(F32, F32), n_row_sums=1)
    red_pre_ff = jnp.sum(sums.reshape(-1, 8, D), axis=0)
    d_w_ff1 = _matmul(sv["h2"], da, "tn", f"ff1_dw_{tag}", out_dtype=GRAD_DTYPE, out_col_shards=True)
    return dx1, [d_w_ff1, d_w_ff2.reshape(N_DEV, D_FF // N_DEV, D)], (red_pre_ff, red_post_ff)


def _mixer_bwd(l, dx1, sv, wts, gvec, mod, ffn_reds, midpoint):
    tag = f"l{l}"
    red_pre_ff, red_post_ff = ffn_reds
    gates = [(sv["z"], Z_G + k * D) for k in range(3)]
    dpa, dpb, dpc, *dgl, dy, sums = _matmul(
        sv["y"], wts["w_out"], "nt", f"out_proj_dx_{tag}", tm=512, tn=512,
        extra=gates + [(sv["pa"], 0), (sv["pb"], 0), (sv["pc"], 0)], epilogue=_merge_bwd_epilogue,
        out_dtypes=(BF16,) * 6, prologue=_postnorm_bwd_prologue(1, 2), prologue_tiles=[dx1], prologue_vecs=[gvec, mod],
        prologue_sums=True)
    red_post_mix = jnp.sum(sums.reshape(-1, 8, D), axis=0)
    d_w_out = _matmul(sv["merged"], dy, "tn", f"out_proj_dw_{tag}", out_dtype=GRAD_DTYPE)
    dpa = midpoint(dpa)
    do = _matmul(dpa, wts["wa"], "nt", f"proj_a_dx_{tag}")
    dbr_b = _matmul(dpb, wts["wb"], "nt", f"proj_b_dx_{tag}")
    dbr_c = _matmul(dpc, wts["wc"], "nt", f"proj_c_dx_{tag}")
    d_wa = _matmul(sv["o"], dpa, "tn", f"proj_a_dw_{tag}", out_dtype=GRAD_DTYPE)
    d_wb = _matmul(sv["br_b"], dpb, "tn", f"proj_b_dw_{tag}", out_dtype=GRAD_DTYPE)
    d_wc = _matmul(sv["br_c"], dpc, "tn", f"proj_c_dw_{tag}", out_dtype=GRAD_DTYPE)
    d_w_branch = jnp.concatenate([d_wa, d_wb, d_wc], axis=0)

    dpu, d_wp_bd, red_pool = _pool_bwd(sv["z"], wts["wp_bd"], wts["pool_scale"], dbr_b, f"pool_bwd_{tag}")
    dconv, red_conv = _conv_bwd(sv["z"], wts["conv_w"], dbr_c, f"conv_bwd_{tag}")
    qa2, doa = _attn_bwd_prep(sv["qa"], sv["o"], sv["lse"], do, f"attn_bwd_prep_{tag}")
    dqt, dka, dva = _attn_bwd(qa2, sv["ka"], sv["va"], sv["kat"], doa, f"attn_bwd_{tag}")
    dq, dk, dv, dfl, red_f = _attn_bwd_post(sv["z"], wts["b_f"], dqt, dka, dva, f"attn_bwd_post_{tag}")
    dz = _concat_columns([dpu, dconv, *dgl, dq, dk, dv, dfl], f"dz_{tag}")
    dx0, sums = _matmul(dz, wts["w_in_t"], "nn", f"in_proj_dx_{tag}", tm=1024, tn=D, tk=1152,
                        extra=[(sv["x"], 0), (dx1, 0)], vec_extra=[gvec, mod], epilogue=_prenorm_bwd_epilogue(0, 1),
                        out_dtypes=(F32, F32), n_row_sums=1)
    red_pre_mix = jnp.sum(sums.reshape(-1, 8, D), axis=0)
    d_w_in_t = _matmul(dz, sv["h"], "tn", f"in_proj_dw_{tag}", out_dtype=GRAD_DTYPE, tm=1152)

    rows = D // N_DEV
    big = [_in_rows_from_z(d_w_in_t).reshape(N_DEV, IN_SHARD, D), d_w_branch.reshape(N_DEV, rows, D),
           d_w_out.reshape(N_DEV, rows, D)]
    d_w_pool = jnp.stack([d_wp_bd[64 * g:64 * (g + 1), 64 * g:64 * (g + 1)] for g in range(4)])
    small = dict(
        mod=jnp.stack([red_pre_mix[0], red_pre_mix[1], red_post_mix[0], red_pre_ff[0], red_pre_ff[1], red_post_ff[0]]),
        g_mix_pre=red_pre_mix[2], g_mix_post=red_post_mix[1], g_ff_pre=red_pre_ff[2], g_ff_post=red_post_ff[1],
        b_f=red_f[0, 0:8], w_pool=d_w_pool, pool_scale=red_pool[0], conv_w=red_conv[0:3])
    return dx0, big, small


SMALL_KEYS = ["mod", "g_mix_pre", "g_mix_post", "g_ff_pre", "g_ff_post", "b_f", "w_pool", "pool_scale", "conv_w"]
SMALL_SHAPES = [(DEPTH, 6 * D), (DEPTH, D), (DEPTH, D), (DEPTH, D), (DEPTH, D), (DEPTH, 8), (DEPTH, 4, 64, 64),
                (DEPTH, POOL_W), (DEPTH, 3, CONV_W)]


def kernel(x, c, w_ada, b_ada, g_mix_pre, g_mix_post, g_ff_pre, g_ff_post, w_in, b_f, w_pool, pool_scale, conv_w, w_branch, w_out, w_ff1, w_ff2, loss_target, m_w_ada, m_b_ada, m_g_mix_pre, m_g_mix_post, m_g_ff_pre, m_g_ff_post, m_w_in, m_b_f, m_w_pool, m_pool_scale, m_conv_w, m_w_branch, m_w_out, m_w_ff1, m_w_ff2, v_w_ada, v_b_ada, v_g_mix_pre, v_g_mix_post, v_g_ff_pre, v_g_ff_post, v_w_in, v_b_f, v_w_pool, v_pool_scale, v_conv_w, v_w_branch, v_w_out, v_w_ff1, v_w_ff2):
    ix, iy, ic = lax.axis_index("x"), lax.axis_index("y"), lax.axis_index("c")
    me = 4 * ix + 2 * iy + ic
    route = jnp.stack([ic, 2 * (1 - ix) + iy, 2 * ix + (1 - iy), 2 * (1 - ix) + (1 - iy)]).astype(jnp.int32)
    place = jnp.stack([me, 2 * ix + iy]).astype(jnp.int32)
    wt_in, mt_in, vt_in = (jnp.transpose(a, (0, 2, 1)) for a in (w_in, m_w_in, v_w_in))

    c_all = _all_gather([_pad_rows(c)], "gather_c")[0][:, 0, :]
    c_pad = _pad_rows(c_all, ADA_ROWS)
    b_cols = lax.dynamic_slice_in_dim(b_ada, me * ADA_COLS, ADA_COLS, axis=1)
    b_cols = jnp.broadcast_to(b_cols[:, None, :], (DEPTH, 8, ADA_COLS))
    mod_part = _ada_fwd(c_pad, w_ada, b_cols, "ada_fwd")
    mod_all = _all_gather([mod_part.reshape(DEPTH * ADA_ROWS, ADA_COLS)], "gather_mod")[0]
    mod_all = mod_all.reshape(N_DEV, DEPTH, ADA_ROWS, ADA_COLS)
    mod_mine = lax.dynamic_index_in_dim(mod_all, me, axis=2, keepdims=False)
    mod_mine = jnp.transpose(mod_mine, (1, 0, 2)).reshape(DEPTH, 6, D)

    cw_cols = CONV_W // N_DEV
    cw_send = jnp.pad(conv_w.reshape(DEPTH * 3, cw_cols), ((0, 8 - DEPTH * 3), (0, LANE - cw_cols)))
    win_in = _window(wt_in, place[0:1], "w_in_window")
    send = [[w[l].astype(BF16) for w in (win_in, w_branch, w_out, w_ff1, w_ff2)] for l in range(DEPTH)]
    first = _all_gather(send[0][:1], "gather_weights_l0_in", sequencer_id=1, after=mod_all)
    rest = _all_gather(send[0][1:] + [cw_send], "gather_weights_l0_rest", sequencer_id=2, after=first[0])
    first1 = _all_gather(send[1][:1], "gather_weights_l1_in", sequencer_id=3, after=first[0])
    rest1 = _all_gather(send[1][1:], "gather_weights_l1_rest", sequencer_id=12, after=first[0])
    first, (mt_in, vt_in) = lax.optimization_barrier((first, (mt_in, vt_in)))
    gathered = [first + rest[:4], first1 + rest1]
    cw_all = rest[4][:, :DEPTH * 3, :cw_cols].reshape(N_DEV, DEPTH, 3, cw_cols)

    def first_operands(l, p_in):
        wp_bd = jnp.zeros((POOL_W, POOL_W), F32)
        for g in range(4):
            wp_bd = wp_bd.at[64 * g:64 * (g + 1), 64 * g:64 * (g + 1)].set(w_pool[l, g])
        return dict(w_in_t=_z_rows_from_windows(p_in), wp_bd=wp_bd.astype(BF16),
                    pool_scale=_pad_rows(pool_scale[l][None, :]), b_f=_pad_rows(jnp.pad(b_f[l], (0, LANE - 8))[None, :]))

    def rest_operands(l, rest):
        p_br, p_out, p_ff1, p_ff2 = rest
        w_br_full = p_br.reshape(D, D)
        cw_full = jnp.transpose(cw_all[:, l], (1, 0, 2)).reshape(3, CONV_W)
        return dict(wa=w_br_full[0:A_WIDTH], wb=w_br_full[A_WIDTH:A_WIDTH + POOL_W], wc=w_br_full[A_WIDTH + POOL_W:],
                    w_out=p_out.reshape(D, D), w_ff1=p_ff1, w_ff2=p_ff2.reshape(D_FF, D), conv_w=_pad_rows(cw_full))

    xs = x[0]
    saved, layers = [], []
    for l in range(DEPTH):
        p_in, rest = gathered[l][0], gathered[l][1:5]
        if l > 0:
            xs, p_in = lax.optimization_barrier((xs, p_in))
        wts = first_operands(l, p_in)

        def arrive(t, l=l, rest=rest, wts=wts):
            if l > 0:
                t, rest = lax.optimization_barrier((t, rest))
            wts.update(rest_operands(l, rest))
            return t

        wts["arrive"] = arrive
        gvec = _pad_rows(jnp.stack([g_mix_pre[l], g_mix_post[l], g_ff_pre[l], g_ff_post[l]]))
        layers.append((wts, gvec, _pad_rows(mod_mine[l])))
        xs, sv = _layer_fwd(l, xs, *layers[l])
        saved.append(sv)
    dx, loss_part = _loss_head(xs, loss_target[0], "loss_head")
    small_grads = [None] * DEPTH
    mine, sibs, landed = ({} for _ in range(3))
    seq_id = iter(range(4, 4 + 4 * DEPTH))
    last = [gathered[DEPTH - 1][1]]

    def start(group, grads):
        mine[group] = grads
        sibs[group] = _sibling_exchange(grads, f"rs_sibling_{group}", sequencer_id=next(seq_id), after=last[0])
        last[0] = sibs[group][0]

    def finish(group, later):
        later, (grads, sib) = lax.optimization_barrier((later, (mine[group], sibs[group])))
        sends = [_pair_sums(g, p, route, f"rs_pair_sums_{group}_{k}") for k, (g, p) in enumerate(zip(grads, sib))]
        later, sends = lax.optimization_barrier((later, sends))
        landed[group] = _chip_exchange(sends, f"rs_chips_{group}", sequencer_id=next(seq_id), after=last[0])
        last[0] = landed[group][0]
        return later

    pending = None
    for l in reversed(range(DEPTH)):
        hook = (lambda da: da) if pending is None else functools.partial(finish, pending)
        dx, ffn_grads, ffn_reds = _ffn_bwd(l, dx, saved[l], *layers[l], hook)
        start(f"ffn_l{l}", ffn_grads)
        dx, mix_grads, small_grads[l] = _mixer_bwd(l, dx, saved[l], *layers[l], ffn_reds,
                                                   functools.partial(finish, f"ffn_l{l}"))
        start(f"mix_l{l}", mix_grads)
        pending = f"mix_l{l}"
    grad_x = dx[None]

    big_w = [wt_in, w_branch, w_out, w_ff1, w_ff2]
    big_m = [mt_in, m_w_branch, m_w_out, m_w_ff1, m_w_ff2]
    big_v = [vt_in, v_w_branch, v_w_out, v_w_ff1, v_w_ff2]
    where = [("mix", 0), ("mix", 1), ("mix", 2), ("ffn", 0), ("ffn", 1)]

    def reduce_and_update(k):
        group, at = where[k]
        return _reduce_adamw([mine[f"{group}_l{l}"][at] for l in range(DEPTH)],
                             [sibs[f"{group}_l{l}"][at] for l in range(DEPTH)],
                             [landed[f"{group}_l{l}"][at] for l in range(DEPTH)], place, big_w[k], big_m[k], big_v[k],
                             f"rs_sum_adamw_{k}")

    big_res = {k: list(reduce_and_update(k)) for k in (3, 4)}
    big_res[3][0] = finish(pending, big_res[3][0])

    small = {k: jnp.stack([small_grads[l][k] for l in range(DEPTH)]) for k in SMALL_KEYS}
    payload = _pack([small[k] for k in SMALL_KEYS] + [loss_part[0:1, 0:1]], 8, F32)
    small_all = _all_gather([payload], "gather_small")[0]
    dmod_all = small_all[:, 0:DEPTH * 6, :].reshape(N_DEV, DEPTH, 6 * D)
    summed = _unpack(_sum_slabs(small_all, "sum_small").reshape(-1), SMALL_SHAPES + [(1, 1)])
    sg = dict(zip(SMALL_KEYS, summed))
    loss = summed[-1][0, 0]
    dmod_cols = lax.dynamic_slice_in_dim(dmod_all, me * ADA_COLS, ADA_COLS, axis=2)
    dmod_cols = jnp.pad(jnp.transpose(dmod_cols, (1, 0, 2)), ((0, 0), (0, ADA_ROWS - N_DEV), (0, 0)))
    g_w_ada = _ada_bwd(c_pad, dmod_cols, "ada_bwd")
    g_conv_w = lax.dynamic_slice_in_dim(sg["conv_w"], me * (CONV_W // N_DEV), CONV_W // N_DEV, axis=2)

    ada_out = [g_w_ada] + list(_adamw(w_ada, g_w_ada, m_w_ada, v_w_ada, "adamw_ada"))
    rest_w = [b_ada, g_mix_pre, g_mix_post, g_ff_pre, g_ff_post, b_f, w_pool, pool_scale, conv_w]
    rest_m = [m_b_ada, m_g_mix_pre, m_g_mix_post, m_g_ff_pre, m_g_ff_post, m_b_f, m_w_pool, m_pool_scale, m_conv_w]
    rest_v = [v_b_ada, v_g_mix_pre, v_g_mix_post, v_g_ff_pre, v_g_ff_post, v_b_f, v_w_pool, v_pool_scale, v_conv_w]
    rest_g = [sg["mod"], sg["g_mix_pre"], sg["g_mix_post"], sg["g_ff_pre"], sg["g_ff_post"], sg["b_f"],
              sg["w_pool"], sg["pool_scale"], g_conv_w]
    rest_shapes = [a.shape for a in rest_w]
    upd = _adamw(_pack(rest_w, 8, F32)[None], _pack(rest_g, 8, F32)[None], _pack(rest_m, 8, F32)[None],
                 _pack(rest_v, 8, F32)[None], "adamw_rest")
    rest_out = [rest_g] + [_unpack(arr.reshape(-1), rest_shapes) for arr in upd]
    rest_out = [[ada_out[which]] + rest_out[which] for which in range(4)]

    landed[pending], rest_out = lax.optimization_barrier((landed[pending], rest_out))
    big_res.update({k: reduce_and_update(k) for k in (0, 1, 2)})
    big_out = [[jnp.transpose(big_res[k][which], (0, 2, 1)) if k == 0 else big_res[k][which] for k in range(5)]
               for which in range(4)]

    def ordered(k):
        r, b = rest_out[k], big_out[k]
        return [r[0], r[1], r[2], r[3], r[4], r[5], b[0], r[6], r[7], r[8], r[9], b[1], b[2], b[3], b[4]]

    return (loss, grad_x, *ordered(0), *ordered(1), *ordered(2), *ordered(3))
```

```python
import functools

import jax
import jax.numpy as jnp
from jax import lax
from jax.experimental import pallas as pl
from jax.experimental.pallas import tpu as pltpu
from jax.experimental.pallas import tpu_sc as plsc

F32 = jnp.float32
BF16 = jnp.bfloat16
GRAD_DTYPE = BF16

N_DEV = 8
D = 1024
S = 2048
DEPTH = 2
D_FF = 4 * D
A_WIDTH = 512
HEAD_DIM = 64
N_PAIR = 4
POOL_W = 256
CONV_W = 256
IN_COLS = 5640
ADA_COLS = 6 * D // N_DEV
IN_SHARD = IN_COLS // N_DEV
RMS_EPS = 1e-6
NEG_INF = -1e30
ATT_SCALE = HEAD_DIM ** -0.5

NZ = 5760
Z_PC = 0
Z_G = 1024
Z_Q = 4096
Z_K = 4608
Z_V = 5120
Z_F = 5632

LR, B1, B2, EPS, WD, STEP = 0.001, 0.9, 0.999, 1e-08, 0.01, 10

LANE = 128
VMEM_LIMIT_BYTES = 48 * 1024 * 1024
TS = 512
TQ = 512
TQ_FWD = 512
HEADS_PER_STEP = 8
HEADS_PER_STEP_FWD = 8


def _params(sem=None):
    return pltpu.CompilerParams(dimension_semantics=sem, vmem_limit_bytes=VMEM_LIMIT_BYTES)


def _pick(n, target):
    best = None
    for t in range(LANE, min(n, target) + 1, LANE):
        if n % t == 0:
            best = t
    return n if best is None else best


def _matmul(a, b, mode, name, out_dtype=F32, tm=2048, tn=1024, tk=2048, b_col_shards=False, out_col_shards=False,
            extra=(), vec_extra=(), epilogue=None, out_dtypes=None, n_row_sums=0, prologue=None, prologue_tiles=(), prologue_vecs=(),
            prologue_sums=False):
    if b_col_shards:
        shards, b_rows, shard_cols = b.shape
        b_shape = (b_rows, shards * shard_cols)
    else:
        b_shape = b.shape
    if mode == "nn":
        (m, k), (k2, n) = a.shape, b_shape
    elif mode == "nt":
        (m, k), (n, k2) = a.shape, b_shape
    else:
        (k, m), (k2, n) = a.shape, b_shape
    assert k == k2, (a.shape, b.shape, mode)
    tm, tn, tk = _pick(m, tm), _pick(n, tn), _pick(k, tk)
    if b_col_shards and mode == "nn":
        tn = shard_cols
    per_step = 1
    if b_col_shards and mode == "nt":
        per_step = max(1, min(tk, 1024) // shard_cols)
        tk = per_step * shard_cols
    if out_col_shards:
        tn = n // N_DEV
    nk = k // tk
    if mode == "nn":
        a_spec = pl.BlockSpec((tm, tk), lambda i, j, kk: (i, kk))
        b_spec = (pl.BlockSpec((None, tk, tn), lambda i, j, kk: (j, kk, 0)) if b_col_shards else
                  pl.BlockSpec((tk, tn), lambda i, j, kk: (kk, j)))
        dims = (((1,), (0,)), ((), ()))
    elif mode == "nt":
        a_spec = pl.BlockSpec((tm, tk), lambda i, j, kk: (i, kk))
        b_spec = (pl.BlockSpec((per_step, tn, shard_cols), lambda i, j, kk: (kk, j, 0)) if b_col_shards else
                  pl.BlockSpec((tn, tk), lambda i, j, kk: (j, kk)))
        dims = (((1,), (1,)), ((), ()))
    else:
        assert not b_col_shards
        a_spec = pl.BlockSpec((tk, tm), lambda i, j, kk: (kk, i))
        b_spec = pl.BlockSpec((tk, tn), lambda i, j, kk: (kk, j))
        dims = (((0,), (0,)), ((), ()))
    if out_col_shards:
        out_shape = jax.ShapeDtypeStruct((N_DEV, m, tn), out_dtype)
        out_spec = pl.BlockSpec((None, tm, tn), lambda i, j, kk: (j, i, 0))
    else:
        out_shape = jax.ShapeDtypeStruct((m, n), out_dtype)
        out_spec = pl.BlockSpec((tm, tn), lambda i, j, kk: (i, j))

    n_extra = len(extra) + len(vec_extra)
    extra_specs = [pl.BlockSpec((tm, tn), lambda i, j, kk, off=off: (i, j + off // tn)) for _, off in extra]
    extra_specs += [pl.BlockSpec((8, tn), lambda i, j, kk: (0, j)) for _ in vec_extra]
    if epilogue is not None:
        assert not out_col_shards and all(off % tn == 0 for _, off in extra)
        out_shape = [jax.ShapeDtypeStruct((m, n), dt) for dt in out_dtypes]
        out_spec = [pl.BlockSpec((tm, tn), lambda i, j, kk: (i, j)) for _ in out_dtypes]
        for at in range(len(out_dtypes) - n_row_sums, len(out_dtypes)):
            out_shape[at] = jax.ShapeDtypeStruct((8 * (m // tm), n), out_dtypes[at])
            out_spec[at] = pl.BlockSpec((8, tn), lambda i, j, kk: (i, j))

    def product(a_ref, b_ref):
        if b_col_shards and mode == "nt":
            b_tile = jnp.concatenate([b_ref[s] for s in range(per_step)], axis=1) if per_step > 1 else b_ref[0]
        else:
            b_tile = b_ref[...]
        return lax.dot_general(a_ref[...].astype(BF16), b_tile.astype(BF16), dims, preferred_element_type=F32)

    def write(acc, extra_refs, o_refs):
        if epilogue is None:
            o_refs[0][...] = acc.astype(out_dtype)
        else:
            for o_ref, tile in zip(o_refs, epilogue(acc, *[r[...] for r in extra_refs])):
                o_ref[...] = tile.astype(o_ref.dtype)

    def body_one_pass(a_ref, b_ref, *refs):
        write(product(a_ref, b_ref), refs[:n_extra], refs[n_extra:])

    if prologue is not None:
        assert nk == 1 and mode in ("nn", "nt")
        n_pro = len(prologue_tiles) + len(prologue_vecs)
        sums = 1 if prologue_sums else 0
        outs = out_shape if isinstance(out_shape, list) else [out_shape]
        out_specs_all = (out_spec if isinstance(out_spec, list) else [out_spec]) + [
            pl.BlockSpec((tm, tk), lambda i, j, kk: (i, 0))]
        outs = outs + [jax.ShapeDtypeStruct((m, k), BF16)]
        if sums:
            outs.append(jax.ShapeDtypeStruct((8 * (m // tm), k), F32))
            out_specs_all.append(pl.BlockSpec((8, tk), lambda i, j, kk: (i, 0)))

        def body_prologue(a_ref, b_ref, *refs):
            pro_refs, rest = refs[:n_pro], refs[n_pro:]
            left_ref = rest[-1]
            left_out = rest[-2 - sums]

            @pl.when(pl.program_id(1) == 0)
            def _():
                made = prologue(a_ref[...], *[r[...] for r in pro_refs])
                left = (made[0] if sums else made).astype(BF16)
                left_ref[...] = left
                left_out[...] = left
                if sums:
                    rest[-2][...] = made[1]

            write(product(left_ref, b_ref), rest[:n_extra], rest[n_extra:-2 - sums])

        pro_specs = [a_spec for _ in prologue_tiles] + [pl.BlockSpec((8, tk), lambda i, j, kk: (0, 0)) for _ in prologue_vecs]
        return pl.pallas_call(
            body_prologue, name=name, out_shape=outs, grid=(m // tm, n // tn, nk),
            in_specs=[a_spec, b_spec] + pro_specs + extra_specs,
            out_specs=out_specs_all,
            scratch_shapes=[pltpu.VMEM((tm, tk), BF16)],
            compiler_params=_params(("parallel", "arbitrary", "arbitrary")),
        )(a, b, *prologue_tiles, *prologue_vecs, *[x for x, _ in extra], *vec_extra)

    def body(a_ref, b_ref, *refs):
        acc_ref = refs[-1]
        kk = pl.program_id(2)

        @pl.when(kk == 0)
        def _():
            acc_ref[...] = product(a_ref, b_ref)

        @pl.when(kk > 0)
        def _():
            acc_ref[...] += product(a_ref, b_ref)

        @pl.when(kk == nk - 1)
        def _():
            write(acc_ref[...], refs[:n_extra], refs[n_extra:-1])

    return pl.pallas_call(
        body_one_pass if nk == 1 else body, name=name,
        out_shape=out_shape,
        grid=(m // tm, n // tn, nk),
        in_specs=[a_spec, b_spec] + extra_specs,
        out_specs=out_spec,
        scratch_shapes=[] if nk == 1 else [pltpu.VMEM((tm, tn), F32)],
        compiler_params=_params(("parallel", "parallel", "arbitrary")),
    )(a, b, *[x for x, _ in extra], *vec_extra)


def _row_spec(width=D, col=0):
    return pl.BlockSpec((TS, width), lambda i: (i, col))


def _vec_spec(rows=8, width=D):
    return pl.BlockSpec((rows, width), lambda i: (0, 0))


def _rms(x):
    return lax.rsqrt(jnp.mean(x * x, axis=-1, keepdims=True) + RMS_EPS)


def _concat_columns(pieces, name):
    widths = [p.shape[1] for p in pieces]
    offsets = [sum(widths[:k]) for k in range(len(widths))]

    def body(*refs):
        o_ref = refs[-1]
        for ref, off, w in zip(refs[:-1], offsets, widths):
            o_ref[:, off:off + w] = ref[...]

    return pl.pallas_call(
        body, name=name, out_shape=jax.ShapeDtypeStruct((S, sum(widths)), pieces[0].dtype), grid=(S // TS,),
        in_specs=[_row_spec(w) for w in widths], out_specs=_row_spec(sum(widths)),
        compiler_params=_params(("parallel",)),
    )(*pieces)


def _loss_head(xf, target, name):
    def body(x_ref, t_ref, dx_ref, loss_ref):
        i = pl.program_id(0)

        @pl.when(i == 0)
        def _():
            loss_ref[...] = jnp.zeros_like(loss_ref)

        e = x_ref[...] - t_ref[...]
        dx_ref[...] = e / float(D)
        per_tok = jnp.mean(e * e, axis=-1, keepdims=True)
        loss_ref[0:1, 0:1] += 0.5 * jnp.sum(per_tok, axis=0, keepdims=True)

    return pl.pallas_call(
        body, name=name,
        out_shape=(jax.ShapeDtypeStruct((S, D), F32), jax.ShapeDtypeStruct((8, LANE), F32)),
        grid=(S // TS,),
        in_specs=[_row_spec(), _row_spec()],
        out_specs=(_row_spec(), pl.BlockSpec((8, LANE), lambda i: (0, 0))),
        compiler_params=_params(("arbitrary",)),
    )(xf, target)


def _relu2_epilogue(a):
    t = jnp.maximum(a, 0.0)
    return a, t * t


def _relu2_bwd_epilogue(dr, a):
    return (dr * (2.0 * jnp.maximum(a, 0.0)),)


def _merge_epilogue(pc, g0, g1, g2, pa, pb):
    return pc, jax.nn.sigmoid(g0) * pa + jax.nn.sigmoid(g1) * pb + jax.nn.sigmoid(g2) * pc


def _prenorm_prologue(g_row, shift_row, scale_row):
    def prologue(x, gvec, mod):
        y = x * _rms(x) * gvec[g_row:g_row + 1, :]
        return y * (1.0 + mod[scale_row:scale_row + 1, :]) + mod[shift_row:shift_row + 1, :]

    return prologue


def _rows8(*rows):
    sub = lax.broadcasted_iota(jnp.int32, (8, rows[0].shape[1]), 0)
    out = jnp.zeros((8, rows[0].shape[1]), F32)
    for k, r in enumerate(rows):
        out = jnp.where(sub == k, r, out)
    return out


def _postnorm_bwd_prologue(g_row, gate_row):
    def prologue(y, dxo, gvec, mod):
        g = gvec[g_row:g_row + 1, :]
        r = _rms(y)
        n = y * r
        dyn = dxo * mod[gate_row:gate_row + 1, :]
        dn = dyn * g
        dy = r * (dn - n * jnp.mean(dn * n, axis=-1, keepdims=True))
        return dy, _rows8(jnp.sum(dxo * (n * g), axis=0, keepdims=True), jnp.sum(dyn * n, axis=0, keepdims=True))

    return prologue


def _prenorm_bwd_epilogue(g_row, scale_row):
    def epilogue(dh, x, dres, gvec, mod):
        g = gvec[g_row:g_row + 1, :]
        r = _rms(x)
        n = x * r
        dyg = dh * (1.0 + mod[scale_row:scale_row + 1, :])
        dn = dyg * g
        dx = r * (dn - n * jnp.mean(dn * n, axis=-1, keepdims=True))
        sums = _rows8(jnp.sum(dh, axis=0, keepdims=True), jnp.sum(dh * (n * g), axis=0, keepdims=True),
                      jnp.sum(dyg * n, axis=0, keepdims=True))
        return dres + dx, sums

    return epilogue


def _postnorm_epilogue(g_row, gate_row):
    def epilogue(y, x, gvec, mod):
        yn = y * _rms(y) * gvec[g_row:g_row + 1, :]
        return y, x + mod[gate_row:gate_row + 1, :] * yn

    return epilogue


def _merge_bwd_epilogue(dm, g0, g1, g2, pa, pb, pc):
    sg = [jax.nn.sigmoid(g) for g in (g0, g1, g2)]
    return tuple(dm * s for s in sg) + tuple(dm * p * (s * (1.0 - s)) for p, s in zip((pa, pb, pc), sg))


def _shift_down(x, k, row):
    return jnp.where(row >= k, pltpu.roll(x, k, axis=0), 0.0)


def _shift_up(x, k, row):
    n = x.shape[0]
    return jnp.where(row < n - k, pltpu.roll(x, n - k, axis=0), 0.0)


def _cumsum_rows(x, row, reverse=False):
    shift = _shift_up if reverse else _shift_down
    k = 1
    while k < x.shape[0]:
        x = x + shift(x, k, row)
        k *= 2
    return x


def _full_spec(shape, idx=(0, 0)):
    return pl.BlockSpec(shape, lambda i: idx)


def _pool_window_select(lane, a2, a4, a8, a16):
    return jnp.where(lane < 64, a2, jnp.where(lane < 128, a4, jnp.where(lane < 192, a8, a16)))


def _pool_p(u, row, lane):
    t2 = u + _shift_down(u, 1, row)
    t4 = t2 + _shift_down(t2, 2, row)
    t8 = t4 + _shift_down(t4, 4, row)
    t16 = t8 + _shift_down(t8, 8, row)
    tw = _pool_window_select(lane, t2, t4, t8, t16)
    cnt = jnp.minimum((row + 1).astype(F32), _pool_window_select(lane, 2.0, 4.0, 8.0, 16.0))
    return tw / cnt - u, cnt


def _pool_fwd(z, wp_bd, pscale, name):
    def body(u_ref, w_ref, s_ref, o_ref):
        row = lax.broadcasted_iota(jnp.int32, (S, POOL_W), 0)
        lane = lax.broadcasted_iota(jnp.int32, (S, POOL_W), 1)
        p, _ = _pool_p(u_ref[...], row, lane)
        y = jnp.dot(p.astype(BF16), w_ref[...], preferred_element_type=F32)
        o_ref[...] = y * s_ref[0:1, :]

    return pl.pallas_call(
        body, name=name, out_shape=jax.ShapeDtypeStruct((S, POOL_W), F32), grid=(1,),
        in_specs=[_full_spec((S, POOL_W), (0, Z_PC // POOL_W)), _full_spec((POOL_W, POOL_W)), _full_spec((8, POOL_W))],
        out_specs=_full_spec((S, POOL_W)),
        compiler_params=_params(("arbitrary",)),
    )(z, wp_bd, pscale)


def _pool_bwd(z, wp_bd, pscale, dbr, name):
    def body(u_ref, w_ref, s_ref, dbr_ref, du_ref, dw_ref, red_ref):
        row = lax.broadcasted_iota(jnp.int32, (S, POOL_W), 0)
        lane = lax.broadcasted_iota(jnp.int32, (S, POOL_W), 1)
        p, cnt = _pool_p(u_ref[...], row, lane)
        pb = p.astype(BF16)
        y = jnp.dot(pb, w_ref[...], preferred_element_type=F32)
        dbr = dbr_ref[...]
        red_ref[...] = jnp.zeros_like(red_ref)
        red_ref[0:1, :] = jnp.sum(dbr * y, axis=0, keepdims=True)
        dy = (dbr * s_ref[0:1, :]).astype(BF16)
        dw_ref[...] = lax.dot_general(pb, dy, (((0,), (0,)), ((), ())), preferred_element_type=F32)
        dp = lax.dot_general(dy, w_ref[...], (((1,), (1,)), ((), ())), preferred_element_type=F32)
        g = dp / cnt
        a2 = g + _shift_up(g, 1, row)
        a4 = a2 + _shift_up(a2, 2, row)
        a8 = a4 + _shift_up(a4, 4, row)
        a16 = a8 + _shift_up(a8, 8, row)
        du_ref[...] = (_pool_window_select(lane, a2, a4, a8, a16) - dp).astype(BF16)

    return pl.pallas_call(
        body, name=name,
        out_shape=(jax.ShapeDtypeStruct((S, POOL_W), BF16), jax.ShapeDtypeStruct((POOL_W, POOL_W), F32),
                   jax.ShapeDtypeStruct((8, POOL_W), F32)),
        grid=(1,),
        in_specs=[_full_spec((S, POOL_W), (0, Z_PC // POOL_W)), _full_spec((POOL_W, POOL_W)), _full_spec((8, POOL_W)),
                  _full_spec((S, POOL_W))],
        out_specs=(_full_spec((S, POOL_W)), _full_spec((POOL_W, POOL_W)), _full_spec((8, POOL_W))),
        compiler_params=_params(("arbitrary",)),
    )(z, wp_bd, pscale, dbr)


def _conv_specs():
    base = Z_PC // CONV_W
    return [_full_spec((S, CONV_W), (0, base + 1)), _full_spec((S, CONV_W), (0, base + 2)),
            _full_spec((S, CONV_W), (0, base + 3)), _full_spec((8, CONV_W))]


def _conv_fwd(z, cw, name):
    def body(h_ref, b_ref, c_ref, w_ref, o_ref):
        row = lax.broadcasted_iota(jnp.int32, (S, CONV_W), 0)
        u = c_ref[...] * h_ref[...]
        y = (w_ref[0:1, :] * _shift_down(u, 2, row) + w_ref[1:2, :] * _shift_down(u, 1, row) + w_ref[2:3, :] * u)
        o_ref[...] = b_ref[...] * y

    return pl.pallas_call(
        body, name=name, out_shape=jax.ShapeDtypeStruct((S, CONV_W), F32), grid=(1,),
        in_specs=_conv_specs(), out_specs=_full_spec((S, CONV_W)),
        compiler_params=_params(("arbitrary",)),
    )(z, z, z, cw)


def _conv_bwd(z, cw, dbr, name):
    def body(h_ref, b_ref, c_ref, w_ref, dbr_ref, d_ref, red_ref):
        row = lax.broadcasted_iota(jnp.int32, (S, CONV_W), 0)
        h, cg = h_ref[...], c_ref[...]
        u = cg * h
        u1 = _shift_down(u, 1, row)
        u2 = _shift_down(u, 2, row)
        y = w_ref[0:1, :] * u2 + w_ref[1:2, :] * u1 + w_ref[2:3, :] * u
        dbr = dbr_ref[...]
        dy = dbr * b_ref[...]
        du = w_ref[2:3, :] * dy + w_ref[1:2, :] * _shift_up(dy, 1, row) + w_ref[0:1, :] * _shift_up(dy, 2, row)
        d_ref[:, 0:CONV_W] = (du * cg).astype(BF16)
        d_ref[:, CONV_W:2 * CONV_W] = (dbr * y).astype(BF16)
        d_ref[:, 2 * CONV_W:3 * CONV_W] = (du * h).astype(BF16)
        red_ref[...] = jnp.zeros_like(red_ref)
        red_ref[0:1, :] = jnp.sum(dy * u2, axis=0, keepdims=True)
        red_ref[1:2, :] = jnp.sum(dy * u1, axis=0, keepdims=True)
        red_ref[2:3, :] = jnp.sum(dy * u, axis=0, keepdims=True)

    return pl.pallas_call(
        body, name=name,
        out_shape=(jax.ShapeDtypeStruct((S, 3 * CONV_W), BF16), jax.ShapeDtypeStruct((8, CONV_W), F32)),
        grid=(1,),
        in_specs=_conv_specs() + [_full_spec((S, CONV_W))],
        out_specs=(_full_spec((S, 3 * CONV_W)), _full_spec((8, CONV_W))),
        compiler_params=_params(("arbitrary",)),
    )(z, z, z, cw, dbr)


_NT = (((1,), (1,)), ((), ()))
_TN = (((0,), (0,)), ((), ()))
N_HEAD = 2 * N_PAIR


def _split3(x):
    hi = x.astype(BF16).astype(F32)
    mid = (x - hi).astype(BF16).astype(F32)
    lo = (x - hi - mid).astype(BF16).astype(F32)
    return hi, mid, lo


def _spare(lane, e, k):
    return lane == 64 * (1 - e) + k


def _spare3(lane, e, k):
    base = 64 * (1 - e) + k
    return (lane >= base) & (lane < base + 3)


def _put3(lane, e, k, pieces, rest):
    out = rest
    for n, piece in enumerate(pieces):
        out = jnp.where(_spare(lane, e, k + n), piece, out)
    return out


def _attn_prep(z, bf, name):
    def body(q_ref, k_ref, v_ref, f_ref, b_ref, qa_ref, ka_ref, va_ref, kat_ref, cum_ref):
        p = pl.program_id(0)
        row = lax.broadcasted_iota(jnp.int32, (S, LANE), 0)
        lane = lax.broadcasted_iota(jnp.int32, (S, LANE), 1)

        @pl.when(p == 0)
        def _():
            xv = f_ref[...] + b_ref[0:1, :]
            ls = jnp.minimum(xv, 0.0) - jnp.log(1.0 + jnp.exp(-jnp.abs(xv)))
            cum_ref[...] = _cumsum_rows(jnp.where(lane < N_HEAD, ls, 0.0), row)

        cum = cum_ref[...]
        q, k, v = q_ref[...], k_ref[...], v_ref[...]
        for e in range(2):
            head = (lane >= 64) if e else (lane < 64)
            f = jnp.sum(jnp.where(lane == 2 * p + e, cum, 0.0), axis=1, keepdims=True)
            pieces = _split3(f)
            qa = jnp.where(head, q * ATT_SCALE, _put3(lane, e, 0, pieces, jnp.where(_spare3(lane, e, 3), 1.0, 0.0)))
            ones = jnp.where(_spare3(lane, e, 0) | _spare3(lane, e, 6), 1.0, 0.0)
            ka = jnp.where(head, k, _put3(lane, e, 3, [-x for x in pieces], ones))
            va = jnp.where(head, v, jnp.where(_spare3(lane, e, 0), 1.0, 0.0))
            qa_ref[e] = qa.astype(BF16)
            ka_ref[e] = ka.astype(BF16)
            va_ref[e] = va.astype(BF16)
            kat_ref[e] = ka.T.astype(BF16)

    qb, kb, vb = Z_Q // LANE, Z_K // LANE, Z_V // LANE
    heads = jax.ShapeDtypeStruct((N_HEAD, S, LANE), BF16)
    pair = pl.BlockSpec((2, S, LANE), lambda p: (p, 0, 0))
    return pl.pallas_call(
        body, name=name,
        out_shape=(heads, heads, heads, jax.ShapeDtypeStruct((N_HEAD, LANE, S), BF16)),
        grid=(N_PAIR,),
        in_specs=[pl.BlockSpec((S, LANE), lambda p: (0, qb + p)), pl.BlockSpec((S, LANE), lambda p: (0, kb + p)),
                  pl.BlockSpec((S, LANE), lambda p: (0, vb + p)), pl.BlockSpec((S, LANE), lambda p: (0, Z_F // LANE)),
                  pl.BlockSpec((8, LANE), lambda p: (0, 0))],
        out_specs=(pair, pair, pair, pl.BlockSpec((2, LANE, S), lambda p: (p, 0, 0))),
        scratch_shapes=[pltpu.VMEM((S, LANE), F32)],
        compiler_params=_params(("arbitrary",)),
    )(z, z, z, z, bf)


def _attn_bwd_prep(qa, o, lse, do, name):
    def body(qa_ref, o_ref, lse_ref, do_ref, qa2_ref, doa_ref):
        lane = lax.broadcasted_iota(jnp.int32, (S, LANE), 1)
        dov, ov, lsev = do_ref[...], o_ref[...], lse_ref[...]
        for e in range(2):
            head = (lane >= 64) if e else (lane < 64)
            dsum = jnp.sum(jnp.where(head, dov * ov, 0.0), axis=1, keepdims=True)
            doa_ref[e] = jnp.where(head, dov, _put3(lane, e, 0, [-x for x in _split3(dsum)], 0.0)).astype(BF16)
            lse_col = lsev[:, 64 * e:64 * e + 1]
            qa2_ref[e] = _put3(lane, e, 6, [-x for x in _split3(lse_col)], qa_ref[e].astype(F32)).astype(BF16)

    heads = jax.ShapeDtypeStruct((N_HEAD, S, LANE), BF16)
    pair = pl.BlockSpec((2, S, LANE), lambda p: (p, 0, 0))
    cols = pl.BlockSpec((S, LANE), lambda p: (0, p))
    return pl.pallas_call(
        body, name=name, out_shape=(heads, heads), grid=(N_PAIR,),
        in_specs=[pair, cols, cols, cols], out_specs=(pair, pair),
        compiler_params=_params(("parallel",)),
    )(qa, o, lse, do)


def _attn_bwd_post(z, bf, dqt, dka, dva, name):
    def body(f_ref, b_ref, dqt_ref, dk_ref, dv_ref, dq_out, dk_out, dv_out, dfl_ref, red_ref, dcum_ref):
        p = pl.program_id(0)

        @pl.when(p == 0)
        def _():
            dcum_ref[...] = jnp.zeros_like(dcum_ref)

        row = lax.broadcasted_iota(jnp.int32, (S, LANE), 0)
        lane = lax.broadcasted_iota(jnp.int32, (S, LANE), 1)
        dqa = [dqt_ref[e].T for e in range(2)]
        dq_out[...] = (jnp.where(lane < 64, dqa[0], dqa[1]) * ATT_SCALE).astype(BF16)
        dk_out[...] = jnp.where(lane < 64, dk_ref[0], dk_ref[1]).astype(BF16)
        dv_out[...] = jnp.where(lane < 64, dv_ref[0], dv_ref[1]).astype(BF16)
        for e in range(2):
            d_query = jnp.sum(jnp.where(_spare(lane, e, 0), dqa[e], 0.0), axis=1, keepdims=True)
            d_key = jnp.sum(jnp.where(_spare(lane, e, 3), dk_ref[e], 0.0), axis=1, keepdims=True)
            dcum_ref[...] += jnp.where(lane == 2 * p + e, d_query - d_key, 0.0)

        @pl.when(p == N_PAIR - 1)
        def _():
            dls = _cumsum_rows(dcum_ref[...], row, reverse=True)
            xv = f_ref[...] + b_ref[0:1, :]
            dx = jnp.where(lane < N_HEAD, dls * jax.nn.sigmoid(-xv), 0.0)
            dfl_ref[...] = dx.astype(BF16)
            red_ref[...] = jnp.zeros_like(red_ref)
            red_ref[0:1, :] = jnp.sum(dx, axis=0, keepdims=True)

    wide = jax.ShapeDtypeStruct((S, N_PAIR * LANE), BF16)
    cols = pl.BlockSpec((S, LANE), lambda p: (0, p))
    pair = pl.BlockSpec((2, S, LANE), lambda p: (p, 0, 0))
    return pl.pallas_call(
        body, name=name,
        out_shape=(wide, wide, wide, jax.ShapeDtypeStruct((S, LANE), BF16), jax.ShapeDtypeStruct((8, LANE), F32)),
        grid=(N_PAIR,),
        in_specs=[pl.BlockSpec((S, LANE), lambda p: (0, Z_F // LANE)), pl.BlockSpec((8, LANE), lambda p: (0, 0)),
                  pl.BlockSpec((2, LANE, S), lambda p: (p, 0, 0)), pair, pair],
        out_specs=(cols, cols, cols, pl.BlockSpec((S, LANE), lambda p: (0, 0)), pl.BlockSpec((8, LANE), lambda p: (0, 0))),
        scratch_shapes=[pltpu.VMEM((S, LANE), F32)],
        compiler_params=_params(("arbitrary",)),
    )(z, bf, dqt, dka, dva)


def _attn_fwd(qa, ka, va, name):
    tq, tk = TQ_FWD, TQ
    ratio = tq // tk

    def body(qa_ref, ka_ref, va_ref, o_ref, lse_ref):
        i = pl.program_id(1)
        lane = lax.broadcasted_iota(jnp.int32, (tq, LANE), 1)
        row = lax.broadcasted_iota(jnp.int32, (tq, tk), 0)
        col = lax.broadcasted_iota(jnp.int32, (tq, tk), 1)
        nh = HEADS_PER_STEP_FWD
        qs = [qa_ref[h] for h in range(nh)]

        def block(j, carry, masked):
            off = pl.multiple_of(j * tk, tk)
            out = []
            for h in range(nh):
                m, acc = carry[h]
                s = lax.dot_general(qs[h], ka_ref[h, pl.ds(off, tk), :], _NT, preferred_element_type=F32)
                if masked:
                    s = jnp.where(col + (j - ratio * i) * tk > row, NEG_INF, s)
                mn = jnp.maximum(m, jnp.max(s, axis=1, keepdims=True))
                p = jnp.exp(s - mn).astype(BF16)
                acc = jnp.exp(m - mn) * acc + jnp.dot(p, va_ref[h, pl.ds(off, tk), :], preferred_element_type=F32)
                out.append((mn, acc))
            return tuple(out)

        init = (jnp.full((tq, 1), NEG_INF, F32), jnp.zeros((tq, LANE), F32))
        carry = lax.fori_loop(0, ratio * i, lambda j, c: block(j, c, False), (init,) * nh)
        for d in range(ratio):
            carry = block(ratio * i + d, carry, True)
        res = []
        for h in range(nh):
            m, acc = carry[h]
            l = jnp.sum(jnp.where(_spare(lane, h % 2, 0), acc, 0.0), axis=1, keepdims=True)
            res.append((acc / l, m + jnp.log(l)))
        for g in range(nh // 2):
            o_ref[:, g * LANE:(g + 1) * LANE] = jnp.where(lane < 64, res[2 * g][0], res[2 * g + 1][0])
            lse_ref[:, g * LANE:(g + 1) * LANE] = jnp.where(lane < 64, res[2 * g][1], res[2 * g + 1][1])

    nh = HEADS_PER_STEP_FWD
    out = jax.ShapeDtypeStruct((S, N_PAIR * LANE), F32)
    wide = pl.BlockSpec((tq, 64 * nh), lambda p, i: (i, p))
    return pl.pallas_call(
        body, name=name, out_shape=(out, out), grid=(N_HEAD // nh, S // tq),
        in_specs=[pl.BlockSpec((nh, tq, LANE), lambda p, i: (p, i, 0)), pl.BlockSpec((nh, S, LANE), lambda p, i: (p, 0, 0)),
                  pl.BlockSpec((nh, S, LANE), lambda p, i: (p, 0, 0))],
        out_specs=(wide, wide),
        compiler_params=_params(("parallel", "parallel")),
    )(qa, ka, va)


def _attn_bwd(qa2, ka, va, kat, doa, name):
    nq = S // TQ

    def body(qa_ref, ka_ref, va_ref, kat_ref, doa_ref, dqt_ref, dk_ref, dv_ref):
        j = pl.program_id(1)

        @pl.when(j == 0)
        def _():
            dqt_ref[...] = jnp.zeros_like(dqt_ref)

        key = lax.broadcasted_iota(jnp.int32, (TQ, TQ), 0)
        qry = lax.broadcasted_iota(jnp.int32, (TQ, TQ), 1)
        nh = HEADS_PER_STEP
        kav, vav, katv = ([ref[h] for h in range(nh)] for ref in (ka_ref, va_ref, kat_ref))

        def block(i, carry, masked):
            off = pl.multiple_of(i * TQ, TQ)
            out = []
            for h in range(nh):
                dk_acc, dv_acc = carry[h]
                qav = qa_ref[h, pl.ds(off, TQ), :]
                doav = doa_ref[h, pl.ds(off, TQ), :]
                s_t = lax.dot_general(kav[h], qav, _NT, preferred_element_type=F32)
                if masked:
                    s_t = jnp.where(key > qry, NEG_INF, s_t)
                p_t = jnp.exp(s_t)
                ds_t = p_t * lax.dot_general(vav[h], doav, _NT, preferred_element_type=F32)
                dsb = ds_t.astype(BF16)
                dv_acc = dv_acc + jnp.dot(p_t.astype(BF16), doav, preferred_element_type=F32)
                dk_acc = dk_acc + jnp.dot(dsb, qav, preferred_element_type=F32)
                dqt_ref[h, :, pl.ds(off, TQ)] += jnp.dot(katv[h], dsb, preferred_element_type=F32)
                out.append((dk_acc, dv_acc))
            return tuple(out)

        zero = (jnp.zeros((TQ, LANE), F32), jnp.zeros((TQ, LANE), F32))
        carry = block(j, (zero,) * nh, True)
        carry = lax.fori_loop(j + 1, nq, lambda i, c: block(i, c, False), carry)
        for h in range(nh):
            dk_ref[h], dv_ref[h] = carry[h]

    nh = HEADS_PER_STEP
    full = pl.BlockSpec((nh, S, LANE), lambda p, j: (p, 0, 0))
    blk = pl.BlockSpec((nh, TQ, LANE), lambda p, j: (p, j, 0))
    acc = jax.ShapeDtypeStruct((N_HEAD, S, LANE), F32)
    return pl.pallas_call(
        body, name=name,
        out_shape=(jax.ShapeDtypeStruct((N_HEAD, LANE, S), F32), acc, acc),
        grid=(N_HEAD // nh, nq),
        in_specs=[full, blk, blk, pl.BlockSpec((nh, LANE, TQ), lambda p, j: (p, 0, j)), full],
        out_specs=(pl.BlockSpec((nh, LANE, S), lambda p, j: (p, 0, 0)), blk, blk),
        compiler_params=_params(("arbitrary", "arbitrary")),
    )(qa2, ka, va, kat, doa)


ADA_ROWS = 16


def _ada_fwd(c_pad, w_ada, b_cols, name):
    def body(c_ref, w_ref, b_ref, o_ref):
        cv = c_ref[...]
        sc = (cv * jax.nn.sigmoid(cv)).astype(BF16)
        o_ref[0] = jnp.dot(sc, w_ref[0].astype(BF16), preferred_element_type=F32) + b_ref[0, 0:1, :]

    return pl.pallas_call(
        body, name=name, out_shape=jax.ShapeDtypeStruct((DEPTH, ADA_ROWS, ADA_COLS), F32), grid=(DEPTH,),
        in_specs=[pl.BlockSpec((ADA_ROWS, D), lambda l: (0, 0)), pl.BlockSpec((1, D, ADA_COLS), lambda l: (l, 0, 0)),
                  pl.BlockSpec((1, 8, ADA_COLS), lambda l: (l, 0, 0))],
        out_specs=pl.BlockSpec((1, ADA_ROWS, ADA_COLS), lambda l: (l, 0, 0)),
        compiler_params=_params(("parallel",)),
    )(c_pad, w_ada, b_cols)


def _ada_bwd(c_pad, dmod_cols, name):
    def body(c_ref, d_ref, o_ref):
        cv = c_ref[...]
        sc = (cv * jax.nn.sigmoid(cv)).astype(BF16)
        o_ref[0] = lax.dot_general(sc, d_ref[0].astype(BF16), _TN, preferred_element_type=F32)

    return pl.pallas_call(
        body, name=name, out_shape=jax.ShapeDtypeStruct((DEPTH, D, ADA_COLS), F32), grid=(DEPTH,),
        in_specs=[pl.BlockSpec((ADA_ROWS, D), lambda l: (0, 0)), pl.BlockSpec((1, ADA_ROWS, ADA_COLS), lambda l: (l, 0, 0))],
        out_specs=pl.BlockSpec((1, D, ADA_COLS), lambda l: (l, 0, 0)),
        compiler_params=_params(("parallel",)),
    )(c_pad, dmod_cols)


def _adamw_math(w, g, m, v):
    m = B1 * m + (1.0 - B1) * g
    v = B2 * v + (1.0 - B2) * (g * g)
    m_hat = m / (1.0 - B1 ** STEP)
    v_hat = v / (1.0 - B2 ** STEP)
    delta = -LR * (m_hat / (jnp.sqrt(v_hat) + EPS) + WD * w)
    return delta, m, v


def _row_tile(rows, target=256):
    best = 8
    for t in range(8, min(rows, target) + 1, 8):
        if rows % t == 0:
            best = t
    return best


def _adamw(w, g, m, v, name):
    layers, rows, cols = w.shape
    tr = _row_tile(rows)
    spec = pl.BlockSpec((1, tr, cols), lambda l, i: (l, i, 0))

    def body(w_ref, g_ref, m_ref, v_ref, d_ref, nm_ref, nv_ref):
        d_ref[...], nm_ref[...], nv_ref[...] = _adamw_math(w_ref[...], g_ref[...], m_ref[...], v_ref[...])

    out = jax.ShapeDtypeStruct(w.shape, F32)
    return pl.pallas_call(
        body, name=name, out_shape=(out, out, out), grid=(layers, rows // tr),
        in_specs=[spec] * 4, out_specs=(spec,) * 3, compiler_params=_params(("parallel", "parallel")),
    )(w, g, m, v)


def _sum_slabs(x, name):
    n, rows, _ = x.shape
    tr = _row_tile(rows)

    def body(x_ref, o_ref):
        acc = x_ref[0]
        for k in range(1, n):
            acc = acc + x_ref[k]
        o_ref[...] = acc

    return pl.pallas_call(
        body, name=name, out_shape=jax.ShapeDtypeStruct((rows, D), F32), grid=(rows // tr,),
        in_specs=[pl.BlockSpec((n, tr, D), lambda i: (0, i, 0))], out_specs=pl.BlockSpec((tr, D), lambda i: (i, 0)),
        compiler_params=_params(("parallel",)),
    )(x)


_ANY = pl.BlockSpec(memory_space=pl.ANY)
MESH = pl.DeviceIdType.MESH


def _on_sequencer(body, out_shape, sems, operands, after, sequencer_id, name):
    n = len(operands)

    def ordered_body(*refs):
        body(*refs[:n], *refs[n + 1:])

    extra = [] if after is None else [after]
    return pl.kernel(
        body if after is None else ordered_body, out_type=out_shape,
        mesh=plsc.ScalarSubcoreMesh(axis_name="sequencer", num_cores=1), scratch_types=sems,
        compiler_params=pltpu.CompilerParams(collective_id=sequencer_id), name=name)(*operands, *extra)


def _all_gather(xs, name, sequencer_id=None, after=None):
    n = len(xs)

    def body(*refs):
        x_refs, out_refs = refs[:n], refs[n:2 * n]
        send_sems, recv_sems, local_sems = refs[2 * n:]
        x_, y_, c_ = lax.axis_index("x"), lax.axis_index("y"), lax.axis_index("c")
        me, sibling = (x_, y_, c_), (x_, y_, 1 - c_)
        chips = [(1 - x_, y_), (x_, 1 - y_), (1 - x_, 1 - y_)]
        if sequencer_id is not None:
            barrier = pltpu.get_barrier_semaphore()
            peers = [sibling] + [(*chip, pc) for chip in chips for pc in (c_, 1 - c_)]
            for peer in peers:
                pl.semaphore_signal(barrier, inc=1, device_id=peer, device_id_type=MESH)
            pl.semaphore_wait(barrier, len(peers))

        def slot(a, px, py, pc):
            return out_refs[a].at[4 * px + 2 * py + pc]

        def copy(a, k, block, to, src=None):
            return pltpu.make_async_remote_copy(
                src_ref=slot(a, *block) if src is None else src, dst_ref=slot(a, *block),
                send_sem=send_sems.at[7 * a + k], recv_sem=recv_sems.at[7 * a + k], device_id=to, device_id_type=MESH)

        mine = [pltpu.make_async_copy(x_refs[a], slot(a, *me), local_sems.at[a]) for a in range(n)]
        for cp in mine:
            cp.start()
        first = []
        for a in range(n):
            first.append(copy(a, 0, me, sibling, src=x_refs[a]))
            first += [copy(a, 1 + j, me, (*chip, c_), src=x_refs[a]) for j, chip in enumerate(chips)]
        for cp in first:
            cp.start()
        passed = []
        for j, chip in enumerate(chips):
            for a in range(n):
                copy(a, 1 + j, (*chip, c_), me).wait_recv()
                passed.append(copy(a, 4 + j, (*chip, c_), sibling))
                passed[-1].start()
        for a in range(n):
            copy(a, 0, sibling, me).wait_recv()
        for j, chip in enumerate(chips):
            for a in range(n):
                copy(a, 4 + j, (*chip, 1 - c_), me).wait_recv()
        for cp in first + passed:
            cp.wait_send()
        for cp in mine:
            cp.wait()

    out_shape = [jax.ShapeDtypeStruct((N_DEV,) + x.shape, x.dtype) for x in xs]
    sems = [pltpu.SemaphoreType.DMA((7 * n,)), pltpu.SemaphoreType.DMA((7 * n,)), pltpu.SemaphoreType.DMA((n,))]
    if sequencer_id is not None:
        return _on_sequencer(body, out_shape, sems, xs, after, sequencer_id, name)
    return pl.pallas_call(
        body, name=name, out_shape=out_shape, in_specs=[_ANY] * n, out_specs=[_ANY] * n, scratch_shapes=sems)(*xs)


def _sibling_exchange(gs, name, sequencer_id=None, after=None):
    n = len(gs)

    def body(*refs):
        g_refs, p_refs = refs[:n], refs[n:2 * n]
        send_sems, recv_sems = refs[2 * n:]
        x_, y_, c_ = lax.axis_index("x"), lax.axis_index("y"), lax.axis_index("c")
        if sequencer_id is not None:
            barrier = pltpu.get_barrier_semaphore()
            pl.semaphore_signal(barrier, inc=1, device_id=(x_, y_, 1 - c_), device_id_type=MESH)
            pl.semaphore_wait(barrier, 1)
        copies = [pltpu.make_async_remote_copy(
            src_ref=g_refs[a].at[2 * k + (1 - c_)], dst_ref=p_refs[a].at[k], send_sem=send_sems.at[4 * a + k],
            recv_sem=recv_sems.at[4 * a + k], device_id=(x_, y_, 1 - c_), device_id_type=MESH)
            for a in range(n) for k in range(4)]
        for cp in copies:
            cp.start()
        for cp in copies:
            cp.wait()

    out_shape = [jax.ShapeDtypeStruct((4,) + g.shape[1:], g.dtype) for g in gs]
    sems = [pltpu.SemaphoreType.DMA((4 * n,)), pltpu.SemaphoreType.DMA((4 * n,))]
    if sequencer_id is not None:
        return _on_sequencer(body, out_shape, sems, gs, after, sequencer_id, name)
    return pl.pallas_call(
        body, name=name, out_shape=out_shape, in_specs=[_ANY] * n, out_specs=[_ANY] * n, scratch_shapes=sems)(*gs)


def _slab_tiles(rows, cols):
    if rows % 8 == 0:
        return _row_tile(rows), cols
    return rows, 2 * LANE


def _pair_sums(g, p, route, name):
    _, rows, cols = g.shape
    tr, tc = _slab_tiles(rows, cols)

    def body(route_ref, g_ref, p_ref, t_ref):
        t_ref[...] = (g_ref[...].astype(F32) + p_ref[...].astype(F32)).astype(BF16)

    return pl.pallas_call(
        body, name=name, out_shape=jax.ShapeDtypeStruct((3, rows, cols), BF16),
        grid_spec=pltpu.PrefetchScalarGridSpec(
            num_scalar_prefetch=1, grid=(3, rows // tr, cols // tc),
            in_specs=[pl.BlockSpec((1, tr, tc), lambda r, i, j, route_ref: (2 * route_ref[1 + r] + route_ref[0], i, j)),
                      pl.BlockSpec((1, tr, tc), lambda r, i, j, route_ref: (route_ref[1 + r], i, j))],
            out_specs=pl.BlockSpec((1, tr, tc), lambda r, i, j, route_ref: (r, i, j))),
        compiler_params=_params(("parallel", "parallel", "parallel")),
    )(route, g, p)


def _chip_exchange(ts, name, sequencer_id=None, after=None):
    n = len(ts)

    def body(*refs):
        t_refs, l_refs = refs[:n], refs[n:2 * n]
        send_sems, recv_sems = refs[2 * n:]
        x_, y_, c_ = lax.axis_index("x"), lax.axis_index("y"), lax.axis_index("c")
        chips = [(1 - x_, y_), (x_, 1 - y_), (1 - x_, 1 - y_)]
        if sequencer_id is not None:
            barrier = pltpu.get_barrier_semaphore()
            for px, py in chips:
                pl.semaphore_signal(barrier, inc=1, device_id=(px, py, c_), device_id_type=MESH)
            pl.semaphore_wait(barrier, len(chips))
        copies = [pltpu.make_async_remote_copy(
            src_ref=t_refs[a].at[r], dst_ref=l_refs[a].at[r], send_sem=send_sems.at[3 * a + r],
            recv_sem=recv_sems.at[3 * a + r], device_id=(px, py, c_), device_id_type=MESH)
            for a in range(n) for r, (px, py) in enumerate(chips)]
        for cp in copies:
            cp.start()
        for cp in copies:
            cp.wait()

    out_shape = [jax.ShapeDtypeStruct((3,) + t.shape[1:], t.dtype) for t in ts]
    sems = [pltpu.SemaphoreType.DMA((3 * n,)), pltpu.SemaphoreType.DMA((3 * n,))]
    if sequencer_id is not None:
        return _on_sequencer(body, out_shape, sems, ts, after, sequencer_id, name)
    return pl.pallas_call(
        body, name=name, out_shape=out_shape, in_specs=[_ANY] * n, out_specs=[_ANY] * n, scratch_shapes=sems)(*ts)


def _reduce_adamw(gs, ps, landed, place, w, m, v, name):
    layers, rows, cols = w.shape
    assert layers == DEPTH == 2
    tr, tc = _slab_tiles(rows, cols)
    nr, nc = rows // tr, cols // tc
    spec = pl.BlockSpec((1, tr, tc), lambda l, i, j, place_ref: (l, i, j))

    def own(layer, which):
        pi, pj = (nr - 1, nc - 1) if layer == 0 else (0, 0)

        def index(l, i, j, place_ref):
            lead = 0 if which is None else place_ref[which]
            return lead, jnp.where(l == layer, i, pi), jnp.where(l == layer, j, pj)

        return pl.BlockSpec((3 if which is None else 1, tr, tc), index)

    def body(place_ref, g0_ref, p0_ref, l0_ref, g1_ref, p1_ref, l1_ref, w_ref, m_ref, v_ref,
             g_ref, d_ref, nm_ref, nv_ref):
        def update(own_ref, sib_ref, l_ref):
            g = (own_ref[0].astype(F32) + sib_ref[0].astype(F32) + l_ref[0].astype(F32) + l_ref[1].astype(F32)
                 + l_ref[2].astype(F32))
            g_ref[0] = g
            d_ref[0], nm_ref[0], nv_ref[0] = _adamw_math(w_ref[0], g, m_ref[0], v_ref[0])

        @pl.when(pl.program_id(0) == 0)
        def _():
            update(g0_ref, p0_ref, l0_ref)

        @pl.when(pl.program_id(0) == 1)
        def _():
            update(g1_ref, p1_ref, l1_ref)

    out = jax.ShapeDtypeStruct(w.shape, F32)
    return pl.pallas_call(
        body, name=name, out_shape=(out, out, out, out),
        grid_spec=pltpu.PrefetchScalarGridSpec(
            num_scalar_prefetch=1, grid=(DEPTH, nr, nc),
            in_specs=[own(0, 0), own(0, 1), own(0, None), own(1, 0), own(1, 1), own(1, None), spec, spec, spec],
            out_specs=(spec, spec, spec, spec)),
        compiler_params=_params(("arbitrary", "arbitrary", "arbitrary")),
    )(place, gs[0], ps[0], landed[0], gs[1], ps[1], landed[1], w, m, v)


def _pack(pieces, row_multiple, dtype, cols=D, rows=None):
    flat = jnp.concatenate([p.astype(dtype).reshape(-1) for p in pieces])
    if rows is None:
        rows = -(-flat.shape[0] // cols)
        rows = -(-rows // row_multiple) * row_multiple
    flat = jnp.pad(flat, (0, rows * cols - flat.shape[0]))
    return flat.reshape(rows, cols)


def _unpack(flat, shapes, lead=()):
    out, off = [], 0
    for shp in shapes:
        n = 1
        for s_ in shp:
            n *= s_
        out.append(lax.slice_in_dim(flat, off, off + n, axis=len(lead)).reshape(lead + tuple(shp)))
        off += n
    return out


WIN_STRIDE = 704
WIN_ROWS = 720
Z_TURN = 1544


def _window(wt, me, name):
    padded = jnp.pad(wt, ((0, 0), (0, WIN_ROWS - IN_SHARD), (0, 0)))

    def body(me_ref, x_ref, o_ref):
        o_ref[0] = pltpu.roll(x_ref[0], me_ref[0], axis=0).astype(BF16)

    spec = pl.BlockSpec((1, WIN_ROWS, D), lambda l, me_ref: (l, 0, 0))
    return pl.pallas_call(
        body, name=name, out_shape=jax.ShapeDtypeStruct((DEPTH, WIN_ROWS, D), BF16),
        grid_spec=pltpu.PrefetchScalarGridSpec(num_scalar_prefetch=1, grid=(DEPTH,), in_specs=[spec], out_specs=spec),
        compiler_params=_params(("parallel",)),
    )(me, padded)


def _z_rows_from_windows(win):
    over = WIN_ROWS - WIN_STRIDE
    pieces = [(0, win[0][0:WIN_STRIDE])]
    for d in range(1, N_DEV):
        base = WIN_STRIDE * d
        pieces.append((base, win[d - 1][WIN_STRIDE:WIN_ROWS] + win[d][0:over]))
        pieces.append((base + over, win[d][over:WIN_STRIDE]))
    pieces.append((WIN_STRIDE * N_DEV, win[N_DEV - 1][WIN_STRIDE:WIN_ROWS]))

    def rows(a, b):
        out = []
        for start, arr in pieces:
            lo, hi = max(a, start), min(b, start + arr.shape[0])
            if lo < hi:
                out.append(arr[lo - start:hi - start])
        return out

    pad = jnp.zeros((NZ - IN_COLS, win.shape[-1]), win.dtype)
    return jnp.concatenate(rows(Z_TURN, IN_COLS) + rows(0, Z_TURN) + [pad], axis=0)


def _in_rows_from_z(wt):
    return jnp.concatenate([wt[Z_Q:Z_Q + 1536], wt[Z_F:Z_F + 8], wt[Z_PC:Z_PC + 1024], wt[Z_G:Z_G + 3072]], axis=0)


def _pad_rows(v, rows=8):
    return jnp.pad(v, ((0, rows - v.shape[0]), (0, 0)))


def _layer_fwd(l, x, wts, gvec, mod):
    tag = f"l{l}"
    z, h = _matmul(x, wts["w_in_t"], "nt", f"in_proj_{tag}", tm=1024, tn=1152, prologue=_prenorm_prologue(0, 0, 1),
                   prologue_vecs=[gvec, mod])
    qa, ka, va, kat = _attn_prep(z, wts["b_f"], f"attn_prep_{tag}")
    qa = wts["arrive"](qa)
    o, lse = _attn_fwd(qa, ka, va, f"attn_{tag}")
    br_b = _pool_fwd(z, wts["wp_bd"], wts["pool_scale"], f"pool_{tag}")
    br_c = _conv_fwd(z, wts["conv_w"], f"conv_{tag}")
    pa = _matmul(o, wts["wa"], "nn", f"proj_a_{tag}", out_dtype=BF16)
    pb = _matmul(br_b, wts["wb"], "nn", f"proj_b_{tag}", out_dtype=BF16)
    gates = [(z, Z_G + k * D) for k in range(3)]
    pc, merged = _matmul(br_c, wts["wc"], "nn", f"proj_c_merge_{tag}", tm=1024, tn=512,
                         extra=gates + [(pa, 0), (pb, 0)], epilogue=_merge_epilogue, out_dtypes=(BF16, BF16))
    y, x1 = _matmul(merged, wts["w_out"], "nn", f"out_proj_{tag}", tm=1024, tn=D, extra=[(x, 0)],
                    vec_extra=[gvec, mod], epilogue=_postnorm_epilogue(1, 2), out_dtypes=(F32, F32))
    a, r, h2 = _matmul(x1, wts["w_ff1"], "nn", f"ff1_{tag}", b_col_shards=True, epilogue=_relu2_epilogue,
                       out_dtypes=(BF16, BF16), prologue=_prenorm_prologue(2, 3, 4), prologue_vecs=[gvec, mod])
    y2, x2 = _matmul(r, wts["w_ff2"], "nn", f"ff2_{tag}", tm=1024, tn=D, tk=1024, extra=[(x1, 0)],
                     vec_extra=[gvec, mod], epilogue=_postnorm_epilogue(3, 5), out_dtypes=(F32, F32))
    saved = dict(x=x, h=h, z=z, qa=qa, ka=ka, va=va, kat=kat, o=o, lse=lse, br_b=br_b, br_c=br_c, pa=pa, pb=pb, pc=pc,
                 merged=merged, y=y, x1=x1, h2=h2, a=a, r=r, y2=y2)
    return x2, saved


def _ffn_bwd(l, dx2, sv, wts, gvec, mod, midpoint):
    tag = f"l{l}"
    dx2 = midpoint(dx2)
    da, dy2, sums = _matmul(sv["y2"], wts["w_ff2"], "nt", f"ff2_dx_{tag}", tm=1024, extra=[(sv["a"], 0)],
                            epilogue=_relu2_bwd_epilogue, out_dtypes=(BF16,), prologue=_postnorm_bwd_prologue(3, 5),
                            prologue_tiles=[dx2], prologue_vecs=[gvec, mod], prologue_sums=True)
    red_post_ff = jnp.sum(sums.reshape(-1, 8, D), axis=0)
    d_w_ff2 = _matmul(sv["r"], dy2, "tn", f"ff2_dw_{tag}", out_dtype=GRAD_DTYPE)
    dx1, sums = _matmul(da, wts["w_ff1"], "nt", f"ff1_dx_{tag}", tm=1024, tn=D, b_col_shards=True,
                        extra=[(sv["x1"], 0), (dx2, 0)], vec_extra=[gvec, mod], epilogue=_prenorm_bwd_epilogue(2, 4),
                        out_dtypes=(F32, F32), n_row_sums=1)
    red_pre_ff = jnp.sum(sums.reshape(-1, 8, D), axis=0)
    d_w_ff1 = _matmul(sv["h2"], da, "tn", f"ff1_dw_{tag}", out_dtype=GRAD_DTYPE, out_col_shards=True)
    return dx1, [d_w_ff1, d_w_ff2.reshape(N_DEV, D_FF // N_DEV, D)], (red_pre_ff, red_post_ff)


def _mixer_bwd(l, dx1, sv, wts, gvec, mod, ffn_reds, midpoint):
    tag = f"l{l}"
    red_pre_ff, red_post_ff = ffn_reds
    gates = [(sv["z"], Z_G + k * D) for k in range(3)]
    dpa, dpb, dpc, *dgl, dy, sums = _matmul(
        sv["y"], wts["w_out"], "nt", f"out_proj_dx_{tag}", tm=512, tn=512,
        extra=gates + [(sv["pa"], 0), (sv["pb"], 0), (sv["pc"], 0)], epilogue=_merge_bwd_epilogue,
        out_dtypes=(BF16,) * 6, prologue=_postnorm_bwd_prologue(1, 2), prologue_tiles=[dx1], prologue_vecs=[gvec, mod],
        prologue_sums=True)
    red_post_mix = jnp.sum(sums.reshape(-1, 8, D), axis=0)
    d_w_out = _matmul(sv["merged"], dy, "tn", f"out_proj_dw_{tag}", out_dtype=GRAD_DTYPE)
    dpa = midpoint(dpa)
    do = _matmul(dpa, wts["wa"], "nt", f"proj_a_dx_{tag}")
    dbr_b = _matmul(dpb, wts["wb"], "nt", f"proj_b_dx_{tag}")
    dbr_c = _matmul(dpc, wts["wc"], "nt", f"proj_c_dx_{tag}")
    d_wa = _matmul(sv["o"], dpa, "tn", f"proj_a_dw_{tag}", out_dtype=GRAD_DTYPE)
    d_wb = _matmul(sv["br_b"], dpb, "tn", f"proj_b_dw_{tag}", out_dtype=GRAD_DTYPE)
    d_wc = _matmul(sv["br_c"], dpc, "tn", f"proj_c_dw_{tag}", out_dtype=GRAD_DTYPE)
    d_w_branch = jnp.concatenate([d_wa, d_wb, d_wc], axis=0)

    dpu, d_wp_bd, red_pool = _pool_bwd(sv["z"], wts["wp_bd"], wts["pool_scale"], dbr_b, f"pool_bwd_{tag}")
    dconv, red_conv = _conv_bwd(sv["z"], wts["conv_w"], dbr_c, f"conv_bwd_{tag}")
    qa2, doa = _attn_bwd_prep(sv["qa"], sv["o"], sv["lse"], do, f"attn_bwd_prep_{tag}")
    dqt, dka, dva = _attn_bwd(qa2, sv["ka"], sv["va"], sv["kat"], doa, f"attn_bwd_{tag}")
    dq, dk, dv, dfl, red_f = _attn_bwd_post(sv["z"], wts["b_f"], dqt, dka, dva, f"attn_bwd_post_{tag}")
    dz = _concat_columns([dpu, dconv, *dgl, dq, dk, dv, dfl], f"dz_{tag}")
    dx0, sums = _matmul(dz, wts["w_in_t"], "nn", f"in_proj_dx_{tag}", tm=1024, tn=D, tk=1152,
                        extra=[(sv["x"], 0), (dx1, 0)], vec_extra=[gvec, mod], epilogue=_prenorm_bwd_epilogue(0, 1),
                        out_dtypes=(F32, F32), n_row_sums=1)
    red_pre_mix = jnp.sum(sums.reshape(-1, 8, D), axis=0)
    d_w_in_t = _matmul(dz, sv["h"], "tn", f"in_proj_dw_{tag}", out_dtype=GRAD_DTYPE, tm=1152)

    rows = D // N_DEV
    big = [_in_rows_from_z(d_w_in_t).reshape(N_DEV, IN_SHARD, D), d_w_branch.reshape(N_DEV, rows, D),
           d_w_out.reshape(N_DEV, rows, D)]
    d_w_pool = jnp.stack([d_wp_bd[64 * g:64 * (g + 1), 64 * g:64 * (g + 1)] for g in range(4)])
    small = dict(
        mod=jnp.stack([red_pre_mix[0], red_pre_mix[1], red_post_mix[0], red_pre_ff[0], red_pre_ff[1], red_post_ff[0]]),
        g_mix_pre=red_pre_mix[2], g_mix_post=red_post_mix[1], g_ff_pre=red_pre_ff[2], g_ff_post=red_post_ff[1],
        b_f=red_f[0, 0:8], w_pool=d_w_pool, pool_scale=red_pool[0], conv_w=red_conv[0:3])
    return dx0, big, small


SMALL_KEYS = ["mod", "g_mix_pre", "g_mix_post", "g_ff_pre", "g_ff_post", "b_f", "w_pool", "pool_scale", "conv_w"]
SMALL_SHAPES = [(DEPTH, 6 * D), (DEPTH, D), (DEPTH, D), (DEPTH, D), (DEPTH, D), (DEPTH, 8), (DEPTH, 4, 64, 64),
                (DEPTH, POOL_W), (DEPTH, 3, CONV_W)]


def kernel(x, c, w_ada, b_ada, g_mix_pre, g_mix_post, g_ff_pre, g_ff_post, w_in, b_f, w_pool, pool_scale, conv_w, w_branch, w_out, w_ff1, w_ff2, loss_target, m_w_ada, m_b_ada, m_g_mix_pre, m_g_mix_post, m_g_ff_pre, m_g_ff_post, m_w_in, m_b_f, m_w_pool, m_pool_scale, m_conv_w, m_w_branch, m_w_out, m_w_ff1, m_w_ff2, v_w_ada, v_b_ada, v_g_mix_pre, v_g_mix_post, v_g_ff_pre, v_g_ff_post, v_w_in, v_b_f, v_w_pool, v_pool_scale, v_conv_w, v_w_branch, v_w_out, v_w_ff1, v_w_ff2):
    ix, iy, ic = lax.axis_index("x"), lax.axis_index("y"), lax.axis_index("c")
    me = 4 * ix + 2 * iy + ic
    route = jnp.stack([ic, 2 * (1 - ix) + iy, 2 * ix + (1 - iy), 2 * (1 - ix) + (1 - iy)]).astype(jnp.int32)
    place = jnp.stack([me, 2 * ix + iy]).astype(jnp.int32)
    wt_in, mt_in, vt_in = (jnp.transpose(a, (0, 2, 1)) for a in (w_in, m_w_in, v_w_in))

    c_all = _all_gather([_pad_rows(c)], "gather_c")[0][:, 0, :]
    c_pad = _pad_rows(c_all, ADA_ROWS)
    b_cols = lax.dynamic_slice_in_dim(b_ada, me * ADA_COLS, ADA_COLS, axis=1)
    b_cols = jnp.broadcast_to(b_cols[:, None, :], (DEPTH, 8, ADA_COLS))
    mod_part = _ada_fwd(c_pad, w_ada, b_cols, "ada_fwd")
    mod_all = _all_gather([mod_part.reshape(DEPTH * ADA_ROWS, ADA_COLS)], "gather_mod")[0]
    mod_all = mod_all.reshape(N_DEV, DEPTH, ADA_ROWS, ADA_COLS)
    mod_mine = lax.dynamic_index_in_dim(mod_all, me, axis=2, keepdims=False)
    mod_mine = jnp.transpose(mod_mine, (1, 0, 2)).reshape(DEPTH, 6, D)

    cw_cols = CONV_W // N_DEV
    cw_send = jnp.pad(conv_w.reshape(DEPTH * 3, cw_cols), ((0, 8 - DEPTH * 3), (0, LANE - cw_cols)))
    win_in = _window(wt_in, place[0:1], "w_in_window")
    send = [[w[l].astype(BF16) for w in (win_in, w_branch, w_out, w_ff1, w_ff2)] for l in range(DEPTH)]
    first = _all_gather(send[0][:1], "gather_weights_l0_in", sequencer_id=1)
    rest = _all_gather(send[0][1:] + [cw_send], "gather_weights_l0_rest", sequencer_id=2, after=first[0])
    first1 = _all_gather(send[1][:1], "gather_weights_l1_in", sequencer_id=3, after=first[0])
    rest1 = _all_gather(send[1][1:], "gather_weights_l1_rest", sequencer_id=12, after=first[0])
    first, (mt_in, vt_in) = lax.optimization_barrier((first, (mt_in, vt_in)))
    gathered = [first + rest[:4], first1 + rest1]
    cw_all = rest[4][:, :DEPTH * 3, :cw_cols].reshape(N_DEV, DEPTH, 3, cw_cols)

    def first_operands(l, p_in):
        wp_bd = jnp.zeros((POOL_W, POOL_W), F32)
        for g in range(4):
            wp_bd = wp_bd.at[64 * g:64 * (g + 1), 64 * g:64 * (g + 1)].set(w_pool[l, g])
        return dict(w_in_t=_z_rows_from_windows(p_in), wp_bd=wp_bd.astype(BF16),
                    pool_scale=_pad_rows(pool_scale[l][None, :]), b_f=_pad_rows(jnp.pad(b_f[l], (0, LANE - 8))[None, :]))

    def rest_operands(l, rest):
        p_br, p_out, p_ff1, p_ff2 = rest
        w_br_full = p_br.reshape(D, D)
        cw_full = jnp.transpose(cw_all[:, l], (1, 0, 2)).reshape(3, CONV_W)
        return dict(wa=w_br_full[0:A_WIDTH], wb=w_br_full[A_WIDTH:A_WIDTH + POOL_W], wc=w_br_full[A_WIDTH + POOL_W:],
                    w_out=p_out.reshape(D, D), w_ff1=p_ff1, w_ff2=p_ff2.reshape(D_FF, D), conv_w=_pad_rows(cw_full))

    xs = x[0]
    saved, layers = [], []
    for l in range(DEPTH):
        p_in, rest = gathered[l][0], gathered[l][1:5]
        if l > 0:
            xs, p_in = lax.optimization_barrier((xs, p_in))
        wts = first_operands(l, p_in)

        def arrive(t, l=l, rest=rest, wts=wts):
            if l > 0:
                t, rest = lax.optimization_barrier((t, rest))
            wts.update(rest_operands(l, rest))
            return t

        wts["arrive"] = arrive
        gvec = _pad_rows(jnp.stack([g_mix_pre[l], g_mix_post[l], g_ff_pre[l], g_ff_post[l]]))
        layers.append((wts, gvec, _pad_rows(mod_mine[l])))
        xs, sv = _layer_fwd(l, xs, *layers[l])
        saved.append(sv)
    dx, loss_part = _loss_head(xs, loss_target[0], "loss_head")
    small_grads = [None] * DEPTH
    mine, sibs, landed = ({} for _ in range(3))
    seq_id = iter(range(4, 4 + 4 * DEPTH))
    last = [gathered[DEPTH - 1][1]]

    def start(group, grads):
        mine[group] = grads
        sibs[group] = _sibling_exchange(grads, f"rs_sibling_{group}", sequencer_id=next(seq_id), after=last[0])
        last[0] = sibs[group][0]

    def finish(group, later):
        later, (grads, sib) = lax.optimization_barrier((later, (mine[group], sibs[group])))
        sends = [_pair_sums(g, p, route, f"rs_pair_sums_{group}_{k}") for k, (g, p) in enumerate(zip(grads, sib))]
        later, sends = lax.optimization_barrier((later, sends))
        landed[group] = _chip_exchange(sends, f"rs_chips_{group}", sequencer_id=next(seq_id), after=last[0])
        last[0] = landed[group][0]
        return later

    pending = None
    for l in reversed(range(DEPTH)):
        hook = (lambda da: da) if pending is None else functools.partial(finish, pending)
        dx, ffn_grads, ffn_reds = _ffn_bwd(l, dx, saved[l], *layers[l], hook)
        start(f"ffn_l{l}", ffn_grads)
        dx, mix_grads, small_grads[l] = _mixer_bwd(l, dx, saved[l], *layers[l], ffn_reds,
                                                   functools.partial(finish, f"ffn_l{l}"))
        start(f"mix_l{l}", mix_grads)
        pending = f"mix_l{l}"
    grad_x = dx[None]

    big_w = [wt_in, w_branch, w_out, w_ff1, w_ff2]
    big_m = [mt_in, m_w_branch, m_w_out, m_w_ff1, m_w_ff2]
    big_v = [vt_in, v_w_branch, v_w_out, v_w_ff1, v_w_ff2]
    where = [("mix", 0), ("mix", 1), ("mix", 2), ("ffn", 0), ("ffn", 1)]

    def reduce_and_update(k):
        group, at = where[k]
        return _reduce_adamw([mine[f"{group}_l{l}"][at] for l in range(DEPTH)],
                             [sibs[f"{group}_l{l}"][at] for l in range(DEPTH)],
                             [landed[f"{group}_l{l}"][at] for l in range(DEPTH)], place, big_w[k], big_m[k], big_v[k],
                             f"rs_sum_adamw_{k}")

    big_res = {k: list(reduce_and_update(k)) for k in (3, 4)}
    big_res[3][0] = finish(pending, big_res[3][0])

    small = {k: jnp.stack([small_grads[l][k] for l in range(DEPTH)]) for k in SMALL_KEYS}
    payload = _pack([small[k] for k in SMALL_KEYS] + [loss_part[0:1, 0:1]], 8, F32)
    small_all = _all_gather([payload], "gather_small")[0]
    dmod_all = small_all[:, 0:DEPTH * 6, :].reshape(N_DEV, DEPTH, 6 * D)
    summed = _unpack(_sum_slabs(small_all, "sum_small").reshape(-1), SMALL_SHAPES + [(1, 1)])
    sg = dict(zip(SMALL_KEYS, summed))
    loss = summed[-1][0, 0]
    dmod_cols = lax.dynamic_slice_in_dim(dmod_all, me * ADA_COLS, ADA_COLS, axis=2)
    dmod_cols = jnp.pad(jnp.transpose(dmod_cols, (1, 0, 2)), ((0, 0), (0, ADA_ROWS - N_DEV), (0, 0)))
    g_w_ada = _ada_bwd(c_pad, dmod_cols, "ada_bwd")
    g_conv_w = lax.dynamic_slice_in_dim(sg["conv_w"], me * (CONV_W // N_DEV), CONV_W // N_DEV, axis=2)

    ada_out = [g_w_ada] + list(_adamw(w_ada, g_w_ada, m_w_ada, v_w_ada, "adamw_ada"))
    rest_w = [b_ada, g_mix_pre, g_mix_post, g_ff_pre, g_ff_post, b_f, w_pool, pool_scale, conv_w]
    rest_m = [m_b_ada, m_g_mix_pre, m_g_mix_post, m_g_ff_pre, m_g_ff_post, m_b_f, m_w_pool, m_pool_scale, m_conv_w]
    rest_v = [v_b_ada, v_g_mix_pre, v_g_mix_post, v_g_ff_pre, v_g_ff_post, v_b_f, v_w_pool, v_pool_scale, v_conv_w]
    rest_g = [sg["mod"], sg["g_mix_pre"], sg["g_mix_post"], sg["g_ff_pre"], sg["g_ff_post"], sg["b_f"],
              sg["w_pool"], sg["pool_scale"], g_conv_w]
    rest_shapes = [a.shape for a in rest_w]
    upd = _adamw(_pack(rest_w, 8, F32)[None], _pack(rest_g, 8, F32)[None], _pack(rest_m, 8, F32)[None],
                 _pack(rest_v, 8, F32)[None], "adamw_rest")
    rest_out = [rest_g] + [_unpack(arr.reshape(-1), rest_shapes) for arr in upd]
    rest_out = [[ada_out[which]] + rest_out[which] for which in range(4)]

    landed[pending], rest_out = lax.optimization_barrier((landed[pending], rest_out))
    big_res.update({k: reduce_and_update(k) for k in (0, 1, 2)})
    big_out = [[jnp.transpose(big_res[k][which], (0, 2, 1)) if k == 0 else big_res[k][which] for k in range(5)]
               for which in range(4)]

    def ordered(k):
        r, b = rest_out[k], big_out[k]
        return [r[0], r[1], r[2], r[3], r[4], r[5], b[0], r[6], r[7], r[8], r[9], b[1], b[2], b[3], b[4]]

    return (loss, grad_x, *ordered(0), *ordered(1), *ordered(2), *ordered(3))
```

```python
import functools

import jax
import jax.numpy as jnp
from jax import lax
from jax.experimental import pallas as pl
from jax.experimental.pallas import tpu as pltpu
from jax.experimental.pallas import tpu_sc as plsc

F32 = jnp.float32
BF16 = jnp.bfloat16
GRAD_DTYPE = BF16

N_DEV = 8
D = 1024
S = 2048
DEPTH = 2
D_FF = 4 * D
A_WIDTH = 512
HEAD_DIM = 64
N_PAIR = 4
POOL_W = 256
CONV_W = 256
IN_COLS = 5640
ADA_COLS = 6 * D // N_DEV
IN_SHARD = IN_COLS // N_DEV
RMS_EPS = 1e-6
NEG_INF = -1e30
ATT_SCALE = HEAD_DIM ** -0.5

NZ = 5760
Z_PC = 0
Z_G = 1024
Z_Q = 4096
Z_K = 4608
Z_V = 5120
Z_F = 5632

LR, B1, B2, EPS, WD, STEP = 0.001, 0.9, 0.999, 1e-08, 0.01, 10

LANE = 128
VMEM_LIMIT_BYTES = 48 * 1024 * 1024
TS = 512
TQ = 512
TQ_FWD = 512
HEADS_PER_STEP = 8
HEADS_PER_STEP_FWD = 8


def _params(sem=None):
    return pltpu.CompilerParams(dimension_semantics=sem, vmem_limit_bytes=VMEM_LIMIT_BYTES)


def _pick(n, target):
    best = None
    for t in range(LANE, min(n, target) + 1, LANE):
        if n % t == 0:
            best = t
    return n if best is None else best


def _matmul(a, b, mode, name, out_dtype=F32, tm=2048, tn=1024, tk=2048, b_col_shards=False, out_col_shards=False,
            extra=(), vec_extra=(), epilogue=None, out_dtypes=None, n_row_sums=0, prologue=None, prologue_tiles=(), prologue_vecs=(),
            prologue_sums=False):
    if b_col_shards:
        shards, b_rows, shard_cols = b.shape
        b_shape = (b_rows, shards * shard_cols)
    else:
        b_shape = b.shape
    if mode == "nn":
        (m, k), (k2, n) = a.shape, b_shape
    elif mode == "nt":
        (m, k), (n, k2) = a.shape, b_shape
    else:
        (k, m), (k2, n) = a.shape, b_shape
    assert k == k2, (a.shape, b.shape, mode)
    tm, tn, tk = _pick(m, tm), _pick(n, tn), _pick(k, tk)
    if b_col_shards and mode == "nn":
        tn = shard_cols
    per_step = 1
    if b_col_shards and mode == "nt":
        per_step = max(1, min(tk, 1024) // shard_cols)
        tk = per_step * shard_cols
    if out_col_shards:
        tn = n // N_DEV
    nk = k // tk
    if mode == "nn":
        a_spec = pl.BlockSpec((tm, tk), lambda i, j, kk: (i, kk))
        b_spec = (pl.BlockSpec((None, tk, tn), lambda i, j, kk: (j, kk, 0)) if b_col_shards else
                  pl.BlockSpec((tk, tn), lambda i, j, kk: (kk, j)))
        dims = (((1,), (0,)), ((), ()))
    elif mode == "nt":
        a_spec = pl.BlockSpec((tm, tk), lambda i, j, kk: (i, kk))
        b_spec = (pl.BlockSpec((per_step, tn, shard_cols), lambda i, j, kk: (kk, j, 0)) if b_col_shards else
                  pl.BlockSpec((tn, tk), lambda i, j, kk: (j, kk)))
        dims = (((1,), (1,)), ((), ()))
    else:
        assert not b_col_shards
        a_spec = pl.BlockSpec((tk, tm), lambda i, j, kk: (kk, i))
        b_spec = pl.BlockSpec((tk, tn), lambda i, j, kk: (kk, j))
        dims = (((0,), (0,)), ((), ()))
    if out_col_shards:
        out_shape = jax.ShapeDtypeStruct((N_DEV, m, tn), out_dtype)
        out_spec = pl.BlockSpec((None, tm, tn), lambda i, j, kk: (j, i, 0))
    else:
        out_shape = jax.ShapeDtypeStruct((m, n), out_dtype)
        out_spec = pl.BlockSpec((tm, tn), lambda i, j, kk: (i, j))

    n_extra = len(extra) + len(vec_extra)
    extra_specs = [pl.BlockSpec((tm, tn), lambda i, j, kk, off=off: (i, j + off // tn)) for _, off in extra]
    extra_specs += [pl.BlockSpec((8, tn), lambda i, j, kk: (0, j)) for _ in vec_extra]
    if epilogue is not None:
        assert not out_col_shards and all(off % tn == 0 for _, off in extra)
        out_shape = [jax.ShapeDtypeStruct((m, n), dt) for dt in out_dtypes]
        out_spec = [pl.BlockSpec((tm, tn), lambda i, j, kk: (i, j)) for _ in out_dtypes]
        for at in range(len(out_dtypes) - n_row_sums, len(out_dtypes)):
            out_shape[at] = jax.ShapeDtypeStruct((8 * (m // tm), n), out_dtypes[at])
            out_spec[at] = pl.BlockSpec((8, tn), lambda i, j, kk: (i, j))

    def product(a_ref, b_ref):
        if b_col_shards and mode == "nt":
            b_tile = jnp.concatenate([b_ref[s] for s in range(per_step)], axis=1) if per_step > 1 else b_ref[0]
        else:
            b_tile = b_ref[...]
        return lax.dot_general(a_ref[...].astype(BF16), b_tile.astype(BF16), dims, preferred_element_type=F32)

    def write(acc, extra_refs, o_refs):
        if epilogue is None:
            o_refs[0][...] = acc.astype(out_dtype)
        else:
            for o_ref, tile in zip(o_refs, epilogue(acc, *[r[...] for r in extra_refs])):
                o_ref[...] = tile.astype(o_ref.dtype)

    def body_one_pass(a_ref, b_ref, *refs):
        write(product(a_ref, b_ref), refs[:n_extra], refs[n_extra:])

    if prologue is not None:
        assert nk == 1 and mode in ("nn", "nt")
        n_pro = len(prologue_tiles) + len(prologue_vecs)
        sums = 1 if prologue_sums else 0
        outs = out_shape if isinstance(out_shape, list) else [out_shape]
        out_specs_all = (out_spec if isinstance(out_spec, list) else [out_spec]) + [
            pl.BlockSpec((tm, tk), lambda i, j, kk: (i, 0))]
        outs = outs + [jax.ShapeDtypeStruct((m, k), BF16)]
        if sums:
            outs.append(jax.ShapeDtypeStruct((8 * (m // tm), k), F32))
            out_specs_all.append(pl.BlockSpec((8, tk), lambda i, j, kk: (i, 0)))

        def body_prologue(a_ref, b_ref, *refs):
            pro_refs, rest = refs[:n_pro], refs[n_pro:]
            left_ref = rest[-1]
            left_out = rest[-2 - sums]

            @pl.when(pl.program_id(1) == 0)
            def _():
                made = prologue(a_ref[...], *[r[...] for r in pro_refs])
                left = (made[0] if sums else made).astype(BF16)
                left_ref[...] = left
                left_out[...] = left
                if sums:
                    rest[-2][...] = made[1]

            write(product(left_ref, b_ref), rest[:n_extra], rest[n_extra:-2 - sums])

        pro_specs = [a_spec for _ in prologue_tiles] + [pl.BlockSpec((8, tk), lambda i, j, kk: (0, 0)) for _ in prologue_vecs]
        return pl.pallas_call(
            body_prologue, name=name, out_shape=outs, grid=(m // tm, n // tn, nk),
            in_specs=[a_spec, b_spec] + pro_specs + extra_specs,
            out_specs=out_specs_all,
            scratch_shapes=[pltpu.VMEM((tm, tk), BF16)],
            compiler_params=_params(("parallel", "arbitrary", "arbitrary")),
        )(a, b, *prologue_tiles, *prologue_vecs, *[x for x, _ in extra], *vec_extra)

    def body(a_ref, b_ref, *refs):
        acc_ref = refs[-1]
        kk = pl.program_id(2)

        @pl.when(kk == 0)
        def _():
            acc_ref[...] = product(a_ref, b_ref)

        @pl.when(kk > 0)
        def _():
            acc_ref[...] += product(a_ref, b_ref)

        @pl.when(kk == nk - 1)
        def _():
            write(acc_ref[...], refs[:n_extra], refs[n_extra:-1])

    return pl.pallas_call(
        body_one_pass if nk == 1 else body, name=name,
        out_shape=out_shape,
        grid=(m // tm, n // tn, nk),
        in_specs=[a_spec, b_spec] + extra_specs,
        out_specs=out_spec,
        scratch_shapes=[] if nk == 1 else [pltpu.VMEM((tm, tn), F32)],
        compiler_params=_params(("parallel", "parallel", "arbitrary")),
    )(a, b, *[x for x, _ in extra], *vec_extra)


def _row_spec(width=D, col=0):
    return pl.BlockSpec((TS, width), lambda i: (i, col))


def _vec_spec(rows=8, width=D):
    return pl.BlockSpec((rows, width), lambda i: (0, 0))


def _rms(x):
    return lax.rsqrt(jnp.mean(x * x, axis=-1, keepdims=True) + RMS_EPS)


def _concat_columns(pieces, name):
    widths = [p.shape[1] for p in pieces]
    offsets = [sum(widths[:k]) for k in range(len(widths))]

    def body(*refs):
        o_ref = refs[-1]
        for ref, off, w in zip(refs[:-1], offsets, widths):
            o_ref[:, off:off + w] = ref[...]

    return pl.pallas_call(
        body, name=name, out_shape=jax.ShapeDtypeStruct((S, sum(widths)), pieces[0].dtype), grid=(S // TS,),
        in_specs=[_row_spec(w) for w in widths], out_specs=_row_spec(sum(widths)),
        compiler_params=_params(("parallel",)),
    )(*pieces)


def _loss_head(xf, target, name):
    def body(x_ref, t_ref, dx_ref, loss_ref):
        i = pl.program_id(0)

        @pl.when(i == 0)
        def _():
            loss_ref[...] = jnp.zeros_like(loss_ref)

        e = x_ref[...] - t_ref[...]
        dx_ref[...] = e / float(D)
        per_tok = jnp.mean(e * e, axis=-1, keepdims=True)
        loss_ref[0:1, 0:1] += 0.5 * jnp.sum(per_tok, axis=0, keepdims=True)

    return pl.pallas_call(
        body, name=name,
        out_shape=(jax.ShapeDtypeStruct((S, D), F32), jax.ShapeDtypeStruct((8, LANE), F32)),
        grid=(S // TS,),
        in_specs=[_row_spec(), _row_spec()],
        out_specs=(_row_spec(), pl.BlockSpec((8, LANE), lambda i: (0, 0))),
        compiler_params=_params(("arbitrary",)),
    )(xf, target)


def _relu2_epilogue(a):
    t = jnp.maximum(a, 0.0)
    return a, t * t


def _relu2_bwd_epilogue(dr, a):
    return (dr * (2.0 * jnp.maximum(a, 0.0)),)


def _merge_epilogue(pc, g0, g1, g2, pa, pb):
    return pc, jax.nn.sigmoid(g0) * pa + jax.nn.sigmoid(g1) * pb + jax.nn.sigmoid(g2) * pc


def _prenorm_prologue(g_row, shift_row, scale_row):
    def prologue(x, gvec, mod):
        y = x * _rms(x) * gvec[g_row:g_row + 1, :]
        return y * (1.0 + mod[scale_row:scale_row + 1, :]) + mod[shift_row:shift_row + 1, :]

    return prologue


def _rows8(*rows):
    sub = lax.broadcasted_iota(jnp.int32, (8, rows[0].shape[1]), 0)
    out = jnp.zeros((8, rows[0].shape[1]), F32)
    for k, r in enumerate(rows):
        out = jnp.where(sub == k, r, out)
    return out


def _postnorm_bwd_prologue(g_row, gate_row):
    def prologue(y, dxo, gvec, mod):
        g = gvec[g_row:g_row + 1, :]
        r = _rms(y)
        n = y * r
        dyn = dxo * mod[gate_row:gate_row + 1, :]
        dn = dyn * g
        dy = r * (dn - n * jnp.mean(dn * n, axis=-1, keepdims=True))
        return dy, _rows8(jnp.sum(dxo * (n * g), axis=0, keepdims=True), jnp.sum(dyn * n, axis=0, keepdims=True))

    return prologue


def _prenorm_bwd_epilogue(g_row, scale_row):
    def epilogue(dh, x, dres, gvec, mod):
        g = gvec[g_row:g_row + 1, :]
        r = _rms(x)
        n = x * r
        dyg = dh * (1.0 + mod[scale_row:scale_row + 1, :])
        dn = dyg * g
        dx = r * (dn - n * jnp.mean(dn * n, axis=-1, keepdims=True))
        sums = _rows8(jnp.sum(dh, axis=0, keepdims=True), jnp.sum(dh * (n * g), axis=0, keepdims=True),
                      jnp.sum(dyg * n, axis=0, keepdims=True))
        return dres + dx, sums

    return epilogue


def _postnorm_epilogue(g_row, gate_row):
    def epilogue(y, x, gvec, mod):
        yn = y * _rms(y) * gvec[g_row:g_row + 1, :]
        return y, x + mod[gate_row:gate_row + 1, :] * yn

    return epilogue


def _merge_bwd_epilogue(dm, g0, g1, g2, pa, pb, pc):
    sg = [jax.nn.sigmoid(g) for g in (g0, g1, g2)]
    return tuple(dm * s for s in sg) + tuple(dm * p * (s * (1.0 - s)) for p, s in zip((pa, pb, pc), sg))


def _shift_down(x, k, row):
    return jnp.where(row >= k, pltpu.roll(x, k, axis=0), 0.0)


def _shift_up(x, k, row):
    n = x.shape[0]
    return jnp.where(row < n - k, pltpu.roll(x, n - k, axis=0), 0.0)


def _cumsum_rows(x, row, reverse=False):
    shift = _shift_up if reverse else _shift_down
    k = 1
    while k < x.shape[0]:
        x = x + shift(x, k, row)
        k *= 2
    return x


def _full_spec(shape, idx=(0, 0)):
    return pl.BlockSpec(shape, lambda i: idx)


def _pool_window_select(lane, a2, a4, a8, a16):
    return jnp.where(lane < 64, a2, jnp.where(lane < 128, a4, jnp.where(lane < 192, a8, a16)))


def _pool_p(u, row, lane):
    t2 = u + _shift_down(u, 1, row)
    t4 = t2 + _shift_down(t2, 2, row)
    t8 = t4 + _shift_down(t4, 4, row)
    t16 = t8 + _shift_down(t8, 8, row)
    tw = _pool_window_select(lane, t2, t4, t8, t16)
    cnt = jnp.minimum((row + 1).astype(F32), _pool_window_select(lane, 2.0, 4.0, 8.0, 16.0))
    return tw / cnt - u, cnt


def _pool_fwd(z, wp_bd, pscale, name):
    def body(u_ref, w_ref, s_ref, o_ref):
        row = lax.broadcasted_iota(jnp.int32, (S, POOL_W), 0)
        lane = lax.broadcasted_iota(jnp.int32, (S, POOL_W), 1)
        p, _ = _pool_p(u_ref[...], row, lane)
        y = jnp.dot(p.astype(BF16), w_ref[...], preferred_element_type=F32)
        o_ref[...] = y * s_ref[0:1, :]

    return pl.pallas_call(
        body, name=name, out_shape=jax.ShapeDtypeStruct((S, POOL_W), F32), grid=(1,),
        in_specs=[_full_spec((S, POOL_W), (0, Z_PC // POOL_W)), _full_spec((POOL_W, POOL_W)), _full_spec((8, POOL_W))],
        out_specs=_full_spec((S, POOL_W)),
        compiler_params=_params(("arbitrary",)),
    )(z, wp_bd, pscale)


def _pool_bwd(z, wp_bd, pscale, dbr, name):
    def body(u_ref, w_ref, s_ref, dbr_ref, du_ref, dw_ref, red_ref):
        row = lax.broadcasted_iota(jnp.int32, (S, POOL_W), 0)
        lane = lax.broadcasted_iota(jnp.int32, (S, POOL_W), 1)
        p, cnt = _pool_p(u_ref[...], row, lane)
        pb = p.astype(BF16)
        y = jnp.dot(pb, w_ref[...], preferred_element_type=F32)
        dbr = dbr_ref[...]
        red_ref[...] = jnp.zeros_like(red_ref)
        red_ref[0:1, :] = jnp.sum(dbr * y, axis=0, keepdims=True)
        dy = (dbr * s_ref[0:1, :]).astype(BF16)
        dw_ref[...] = lax.dot_general(pb, dy, (((0,), (0,)), ((), ())), preferred_element_type=F32)
        dp = lax.dot_general(dy, w_ref[...], (((1,), (1,)), ((), ())), preferred_element_type=F32)
        g = dp / cnt
        a2 = g + _shift_up(g, 1, row)
        a4 = a2 + _shift_up(a2, 2, row)
        a8 = a4 + _shift_up(a4, 4, row)
        a16 = a8 + _shift_up(a8, 8, row)
        du_ref[...] = (_pool_window_select(lane, a2, a4, a8, a16) - dp).astype(BF16)

    return pl.pallas_call(
        body, name=name,
        out_shape=(jax.ShapeDtypeStruct((S, POOL_W), BF16), jax.ShapeDtypeStruct((POOL_W, POOL_W), F32),
                   jax.ShapeDtypeStruct((8, POOL_W), F32)),
        grid=(1,),
        in_specs=[_full_spec((S, POOL_W), (0, Z_PC // POOL_W)), _full_spec((POOL_W, POOL_W)), _full_spec((8, POOL_W)),
                  _full_spec((S, POOL_W))],
        out_specs=(_full_spec((S, POOL_W)), _full_spec((POOL_W, POOL_W)), _full_spec((8, POOL_W))),
        compiler_params=_params(("arbitrary",)),
    )(z, wp_bd, pscale, dbr)


def _conv_specs():
    base = Z_PC // CONV_W
    return [_full_spec((S, CONV_W), (0, base + 1)), _full_spec((S, CONV_W), (0, base + 2)),
            _full_spec((S, CONV_W), (0, base + 3)), _full_spec((8, CONV_W))]


def _conv_fwd(z, cw, name):
    def body(h_ref, b_ref, c_ref, w_ref, o_ref):
        row = lax.broadcasted_iota(jnp.int32, (S, CONV_W), 0)
        u = c_ref[...] * h_ref[...]
        y = (w_ref[0:1, :] * _shift_down(u, 2, row) + w_ref[1:2, :] * _shift_down(u, 1, row) + w_ref[2:3, :] * u)
        o_ref[...] = b_ref[...] * y

    return pl.pallas_call(
        body, name=name, out_shape=jax.ShapeDtypeStruct((S, CONV_W), F32), grid=(1,),
        in_specs=_conv_specs(), out_specs=_full_spec((S, CONV_W)),
        compiler_params=_params(("arbitrary",)),
    )(z, z, z, cw)


def _conv_bwd(z, cw, dbr, name):
    def body(h_ref, b_ref, c_ref, w_ref, dbr_ref, d_ref, red_ref):
        row = lax.broadcasted_iota(jnp.int32, (S, CONV_W), 0)
        h, cg = h_ref[...], c_ref[...]
        u = cg * h
        u1 = _shift_down(u, 1, row)
        u2 = _shift_down(u, 2, row)
        y = w_ref[0:1, :] * u2 + w_ref[1:2, :] * u1 + w_ref[2:3, :] * u
        dbr = dbr_ref[...]
        dy = dbr * b_ref[...]
        du = w_ref[2:3, :] * dy + w_ref[1:2, :] * _shift_up(dy, 1, row) + w_ref[0:1, :] * _shift_up(dy, 2, row)
        d_ref[:, 0:CONV_W] = (du * cg).astype(BF16)
        d_ref[:, CONV_W:2 * CONV_W] = (dbr * y).astype(BF16)
        d_ref[:, 2 * CONV_W:3 * CONV_W] = (du * h).astype(BF16)
        red_ref[...] = jnp.zeros_like(red_ref)
        red_ref[0:1, :] = jnp.sum(dy * u2, axis=0, keepdims=True)
        red_ref[1:2, :] = jnp.sum(dy * u1, axis=0, keepdims=True)
        red_ref[2:3, :] = jnp.sum(dy * u, axis=0, keepdims=True)

    return pl.pallas_call(
        body, name=name,
        out_shape=(jax.ShapeDtypeStruct((S, 3 * CONV_W), BF16), jax.ShapeDtypeStruct((8, CONV_W), F32)),
        grid=(1,),
        in_specs=_conv_specs() + [_full_spec((S, CONV_W))],
        out_specs=(_full_spec((S, 3 * CONV_W)), _full_spec((8, CONV_W))),
        compiler_params=_params(("arbitrary",)),
    )(z, z, z, cw, dbr)


_NT = (((1,), (1,)), ((), ()))
_TN = (((0,), (0,)), ((), ()))
N_HEAD = 2 * N_PAIR


def _split3(x):
    hi = x.astype(BF16).astype(F32)
    mid = (x - hi).astype(BF16).astype(F32)
    lo = (x - hi - mid).astype(BF16).astype(F32)
    return hi, mid, lo


def _spare(lane, e, k):
    return lane == 64 * (1 - e) + k


def _spare3(lane, e, k):
    base = 64 * (1 - e) + k
    return (lane >= base) & (lane < base + 3)


def _put3(lane, e, k, pieces, rest):
    out = rest
    for n, piece in enumerate(pieces):
        out = jnp.where(_spare(lane, e, k + n), piece, out)
    return out


def _attn_prep(z, bf, name):
    def body(q_ref, k_ref, v_ref, f_ref, b_ref, qa_ref, ka_ref, va_ref, kat_ref, cum_ref):
        p = pl.program_id(0)
        row = lax.broadcasted_iota(jnp.int32, (S, LANE), 0)
        lane = lax.broadcasted_iota(jnp.int32, (S, LANE), 1)

        @pl.when(p == 0)
        def _():
            xv = f_ref[...] + b_ref[0:1, :]
            ls = jnp.minimum(xv, 0.0) - jnp.log(1.0 + jnp.exp(-jnp.abs(xv)))
            cum_ref[...] = _cumsum_rows(jnp.where(lane < N_HEAD, ls, 0.0), row)

        cum = cum_ref[...]
        q, k, v = q_ref[...], k_ref[...], v_ref[...]
        for e in range(2):
            head = (lane >= 64) if e else (lane < 64)
            f = jnp.sum(jnp.where(lane == 2 * p + e, cum, 0.0), axis=1, keepdims=True)
            pieces = _split3(f)
            qa = jnp.where(head, q * ATT_SCALE, _put3(lane, e, 0, pieces, jnp.where(_spare3(lane, e, 3), 1.0, 0.0)))
            ones = jnp.where(_spare3(lane, e, 0) | _spare3(lane, e, 6), 1.0, 0.0)
            ka = jnp.where(head, k, _put3(lane, e, 3, [-x for x in pieces], ones))
            va = jnp.where(head, v, jnp.where(_spare3(lane, e, 0), 1.0, 0.0))
            qa_ref[e] = qa.astype(BF16)
            ka_ref[e] = ka.astype(BF16)
            va_ref[e] = va.astype(BF16)
            kat_ref[e] = ka.T.astype(BF16)

    qb, kb, vb = Z_Q // LANE, Z_K // LANE, Z_V // LANE
    heads = jax.ShapeDtypeStruct((N_HEAD, S, LANE), BF16)
    pair = pl.BlockSpec((2, S, LANE), lambda p: (p, 0, 0))
    return pl.pallas_call(
        body, name=name,
        out_shape=(heads, heads, heads, jax.ShapeDtypeStruct((N_HEAD, LANE, S), BF16)),
        grid=(N_PAIR,),
        in_specs=[pl.BlockSpec((S, LANE), lambda p: (0, qb + p)), pl.BlockSpec((S, LANE), lambda p: (0, kb + p)),
                  pl.BlockSpec((S, LANE), lambda p: (0, vb + p)), pl.BlockSpec((S, LANE), lambda p: (0, Z_F // LANE)),
                  pl.BlockSpec((8, LANE), lambda p: (0, 0))],
        out_specs=(pair, pair, pair, pl.BlockSpec((2, LANE, S), lambda p: (p, 0, 0))),
        scratch_shapes=[pltpu.VMEM((S, LANE), F32)],
        compiler_params=_params(("arbitrary",)),
    )(z, z, z, z, bf)


def _attn_bwd_prep(qa, o, lse, do, name):
    def body(qa_ref, o_ref, lse_ref, do_ref, qa2_ref, doa_ref):
        lane = lax.broadcasted_iota(jnp.int32, (S, LANE), 1)
        dov, ov, lsev = do_ref[...], o_ref[...], lse_ref[...]
        for e in range(2):
            head = (lane >= 64) if e else (lane < 64)
            dsum = jnp.sum(jnp.where(head, dov * ov, 0.0), axis=1, keepdims=True)
            doa_ref[e] = jnp.where(head, dov, _put3(lane, e, 0, [-x for x in _split3(dsum)], 0.0)).astype(BF16)
            lse_col = lsev[:, 64 * e:64 * e + 1]
            qa2_ref[e] = _put3(lane, e, 6, [-x for x in _split3(lse_col)], qa_ref[e].astype(F32)).astype(BF16)

    heads = jax.ShapeDtypeStruct((N_HEAD, S, LANE), BF16)
    pair = pl.BlockSpec((2, S, LANE), lambda p: (p, 0, 0))
    cols = pl.BlockSpec((S, LANE), lambda p: (0, p))
    return pl.pallas_call(
        body, name=name, out_shape=(heads, heads), grid=(N_PAIR,),
        in_specs=[pair, cols, cols, cols], out_specs=(pair, pair),
        compiler_params=_params(("parallel",)),
    )(qa, o, lse, do)


def _attn_bwd_post(z, bf, dqt, dka, dva, name):
    def body(f_ref, b_ref, dqt_ref, dk_ref, dv_ref, dq_out, dk_out, dv_out, dfl_ref, red_ref, dcum_ref):
        p = pl.program_id(0)

        @pl.when(p == 0)
        def _():
            dcum_ref[...] = jnp.zeros_like(dcum_ref)

        row = lax.broadcasted_iota(jnp.int32, (S, LANE), 0)
        lane = lax.broadcasted_iota(jnp.int32, (S, LANE), 1)
        dqa = [dqt_ref[e].T for e in range(2)]
        dq_out[...] = (jnp.where(lane < 64, dqa[0], dqa[1]) * ATT_SCALE).astype(BF16)
        dk_out[...] = jnp.where(lane < 64, dk_ref[0], dk_ref[1]).astype(BF16)
        dv_out[...] = jnp.where(lane < 64, dv_ref[0], dv_ref[1]).astype(BF16)
        for e in range(2):
            d_query = jnp.sum(jnp.where(_spare(lane, e, 0), dqa[e], 0.0), axis=1, keepdims=True)
            d_key = jnp.sum(jnp.where(_spare(lane, e, 3), dk_ref[e], 0.0), axis=1, keepdims=True)
            dcum_ref[...] += jnp.where(lane == 2 * p + e, d_query - d_key, 0.0)

        @pl.when(p == N_PAIR - 1)
        def _():
            dls = _cumsum_rows(dcum_ref[...], row, reverse=True)
            xv = f_ref[...] + b_ref[0:1, :]
            dx = jnp.where(lane < N_HEAD, dls * jax.nn.sigmoid(-xv), 0.0)
            dfl_ref[...] = dx.astype(BF16)
            red_ref[...] = jnp.zeros_like(red_ref)
            red_ref[0:1, :] = jnp.sum(dx, axis=0, keepdims=True)

    wide = jax.ShapeDtypeStruct((S, N_PAIR * LANE), BF16)
    cols = pl.BlockSpec((S, LANE), lambda p: (0, p))
    pair = pl.BlockSpec((2, S, LANE), lambda p: (p, 0, 0))
    return pl.pallas_call(
        body, name=name,
        out_shape=(wide, wide, wide, jax.ShapeDtypeStruct((S, LANE), BF16), jax.ShapeDtypeStruct((8, LANE), F32)),
        grid=(N_PAIR,),
        in_specs=[pl.BlockSpec((S, LANE), lambda p: (0, Z_F // LANE)), pl.BlockSpec((8, LANE), lambda p: (0, 0)),
                  pl.BlockSpec((2, LANE, S), lambda p: (p, 0, 0)), pair, pair],
        out_specs=(cols, cols, cols, pl.BlockSpec((S, LANE), lambda p: (0, 0)), pl.BlockSpec((8, LANE), lambda p: (0, 0))),
        scratch_shapes=[pltpu.VMEM((S, LANE), F32)],
        compiler_params=_params(("arbitrary",)),
    )(z, bf, dqt, dka, dva)


def _attn_fwd(qa, ka, va, name):
    tq, tk = TQ_FWD, TQ
    ratio = tq // tk

    def body(qa_ref, ka_ref, va_ref, o_ref, lse_ref):
        i = pl.program_id(1)
        lane = lax.broadcasted_iota(jnp.int32, (tq, LANE), 1)
        row = lax.broadcasted_iota(jnp.int32, (tq, tk), 0)
        col = lax.broadcasted_iota(jnp.int32, (tq, tk), 1)
        nh = HEADS_PER_STEP_FWD
        qs = [qa_ref[h] for h in range(nh)]

        def block(j, carry, masked):
            off = pl.multiple_of(j * tk, tk)
            out = []
            for h in range(nh):
                m, acc = carry[h]
                s = lax.dot_general(qs[h], ka_ref[h, pl.ds(off, tk), :], _NT, preferred_element_type=F32)
                if masked:
                    s = jnp.where(col + (j - ratio * i) * tk > row, NEG_INF, s)
                mn = jnp.maximum(m, jnp.max(s, axis=1, keepdims=True))
                p = jnp.exp(s - mn).astype(BF16)
                acc = jnp.exp(m - mn) * acc + jnp.dot(p, va_ref[h, pl.ds(off, tk), :], preferred_element_type=F32)
                out.append((mn, acc))
            return tuple(out)

        init = (jnp.full((tq, 1), NEG_INF, F32), jnp.zeros((tq, LANE), F32))
        carry = lax.fori_loop(0, ratio * i, lambda j, c: block(j, c, False), (init,) * nh)
        for d in range(ratio):
            carry = block(ratio * i + d, carry, True)
        res = []
        for h in range(nh):
            m, acc = carry[h]
            l = jnp.sum(jnp.where(_spare(lane, h % 2, 0), acc, 0.0), axis=1, keepdims=True)
            res.append((acc / l, m + jnp.log(l)))
        for g in range(nh // 2):
            o_ref[:, g * LANE:(g + 1) * LANE] = jnp.where(lane < 64, res[2 * g][0], res[2 * g + 1][0])
            lse_ref[:, g * LANE:(g + 1) * LANE] = jnp.where(lane < 64, res[2 * g][1], res[2 * g + 1][1])

    nh = HEADS_PER_STEP_FWD
    out = jax.ShapeDtypeStruct((S, N_PAIR * LANE), F32)
    wide = pl.BlockSpec((tq, 64 * nh), lambda p, i: (i, p))
    return pl.pallas_call(
        body, name=name, out_shape=(out, out), grid=(N_HEAD // nh, S // tq),
        in_specs=[pl.BlockSpec((nh, tq, LANE), lambda p, i: (p, i, 0)), pl.BlockSpec((nh, S, LANE), lambda p, i: (p, 0, 0)),
                  pl.BlockSpec((nh, S, LANE), lambda p, i: (p, 0, 0))],
        out_specs=(wide, wide),
        compiler_params=_params(("parallel", "parallel")),
    )(qa, ka, va)


def _attn_bwd(qa2, ka, va, kat, doa, name):
    nq = S // TQ

    def body(qa_ref, ka_ref, va_ref, kat_ref, doa_ref, dqt_ref, dk_ref, dv_ref):
        j = pl.program_id(1)

        @pl.when(j == 0)
        def _():
            dqt_ref[...] = jnp.zeros_like(dqt_ref)

        key = lax.broadcasted_iota(jnp.int32, (TQ, TQ), 0)
        qry = lax.broadcasted_iota(jnp.int32, (TQ, TQ), 1)
        nh = HEADS_PER_STEP
        kav, vav, katv = ([ref[h] for h in range(nh)] for ref in (ka_ref, va_ref, kat_ref))

        def block(i, carry, masked):
            off = pl.multiple_of(i * TQ, TQ)
            out = []
            for h in range(nh):
                dk_acc, dv_acc = carry[h]
                qav = qa_ref[h, pl.ds(off, TQ), :]
                doav = doa_ref[h, pl.ds(off, TQ), :]
                s_t = lax.dot_general(kav[h], qav, _NT, preferred_element_type=F32)
                if masked:
                    s_t = jnp.where(key > qry, NEG_INF, s_t)
                p_t = jnp.exp(s_t)
                ds_t = p_t * lax.dot_general(vav[h], doav, _NT, preferred_element_type=F32)
                dsb = ds_t.astype(BF16)
                dv_acc = dv_acc + jnp.dot(p_t.astype(BF16), doav, preferred_element_type=F32)
                dk_acc = dk_acc + jnp.dot(dsb, qav, preferred_element_type=F32)
                dqt_ref[h, :, pl.ds(off, TQ)] += jnp.dot(katv[h], dsb, preferred_element_type=F32)
                out.append((dk_acc, dv_acc))
            return tuple(out)

        zero = (jnp.zeros((TQ, LANE), F32), jnp.zeros((TQ, LANE), F32))
        carry = block(j, (zero,) * nh, True)
        carry = lax.fori_loop(j + 1, nq, lambda i, c: block(i, c, False), carry)
        for h in range(nh):
            dk_ref[h], dv_ref[h] = carry[h]

    nh = HEADS_PER_STEP
    full = pl.BlockSpec((nh, S, LANE), lambda p, j: (p, 0, 0))
    blk = pl.BlockSpec((nh, TQ, LANE), lambda p, j: (p, j, 0))
    acc = jax.ShapeDtypeStruct((N_HEAD, S, LANE), F32)
    return pl.pallas_call(
        body, name=name,
        out_shape=(jax.ShapeDtypeStruct((N_HEAD, LANE, S), F32), acc, acc),
        grid=(N_HEAD // nh, nq),
        in_specs=[full, blk, blk, pl.BlockSpec((nh, LANE, TQ), lambda p, j: (p, 0, j)), full],
        out_specs=(pl.BlockSpec((nh, LANE, S), lambda p, j: (p, 0, 0)), blk, blk),
        compiler_params=_params(("arbitrary", "arbitrary")),
    )(qa2, ka, va, kat, doa)


ADA_ROWS = 16


def _ada_fwd(c_pad, w_ada, b_cols, name):
    def body(c_ref, w_ref, b_ref, o_ref):
        cv = c_ref[...]
        sc = (cv * jax.nn.sigmoid(cv)).astype(BF16)
        o_ref[0] = jnp.dot(sc, w_ref[0].astype(BF16), preferred_element_type=F32) + b_ref[0, 0:1, :]

    return pl.pallas_call(
        body, name=name, out_shape=jax.ShapeDtypeStruct((DEPTH, ADA_ROWS, ADA_COLS), F32), grid=(DEPTH,),
        in_specs=[pl.BlockSpec((ADA_ROWS, D), lambda l: (0, 0)), pl.BlockSpec((1, D, ADA_COLS), lambda l: (l, 0, 0)),
                  pl.BlockSpec((1, 8, ADA_COLS), lambda l: (l, 0, 0))],
        out_specs=pl.BlockSpec((1, ADA_ROWS, ADA_COLS), lambda l: (l, 0, 0)),
        compiler_params=_params(("parallel",)),
    )(c_pad, w_ada, b_cols)


def _ada_bwd(c_pad, dmod_cols, name):
    def body(c_ref, d_ref, o_ref):
        cv = c_ref[...]
        sc = (cv * jax.nn.sigmoid(cv)).astype(BF16)
        o_ref[0] = lax.dot_general(sc, d_ref[0].astype(BF16), _TN, preferred_element_type=F32)

    return pl.pallas_call(
        body, name=name, out_shape=jax.ShapeDtypeStruct((DEPTH, D, ADA_COLS), F32), grid=(DEPTH,),
        in_specs=[pl.BlockSpec((ADA_ROWS, D), lambda l: (0, 0)), pl.BlockSpec((1, ADA_ROWS, ADA_COLS), lambda l: (l, 0, 0))],
        out_specs=pl.BlockSpec((1, D, ADA_COLS), lambda l: (l, 0, 0)),
        compiler_params=_params(("parallel",)),
    )(c_pad, dmod_cols)


def _adamw_math(w, g, m, v):
    m = B1 * m + (1.0 - B1) * g
    v = B2 * v + (1.0 - B2) * (g * g)
    m_hat = m / (1.0 - B1 ** STEP)
    v_hat = v / (1.0 - B2 ** STEP)
    delta = -LR * (m_hat / (jnp.sqrt(v_hat) + EPS) + WD * w)
    return delta, m, v


def _row_tile(rows, target=256):
    best = 8
    for t in range(8, min(rows, target) + 1, 8):
        if rows % t == 0:
            best = t
    return best


def _adamw(w, g, m, v, name):
    layers, rows, cols = w.shape
    tr = _row_tile(rows)
    spec = pl.BlockSpec((1, tr, cols), lambda l, i: (l, i, 0))

    def body(w_ref, g_ref, m_ref, v_ref, d_ref, nm_ref, nv_ref):
        d_ref[...], nm_ref[...], nv_ref[...] = _adamw_math(w_ref[...], g_ref[...], m_ref[...], v_ref[...])

    out = jax.ShapeDtypeStruct(w.shape, F32)
    return pl.pallas_call(
        body, name=name, out_shape=(out, out, out), grid=(layers, rows // tr),
        in_specs=[spec] * 4, out_specs=(spec,) * 3, compiler_params=_params(("parallel", "parallel")),
    )(w, g, m, v)


def _sum_slabs(x, name):
    n, rows, _ = x.shape
    tr = _row_tile(rows)

    def body(x_ref, o_ref):
        acc = x_ref[0]
        for k in range(1, n):
            acc = acc + x_ref[k]
        o_ref[...] = acc

    return pl.pallas_call(
        body, name=name, out_shape=jax.ShapeDtypeStruct((rows, D), F32), grid=(rows // tr,),
        in_specs=[pl.BlockSpec((n, tr, D), lambda i: (0, i, 0))], out_specs=pl.BlockSpec((tr, D), lambda i: (i, 0)),
        compiler_params=_params(("parallel",)),
    )(x)


_ANY = pl.BlockSpec(memory_space=pl.ANY)
MESH = pl.DeviceIdType.MESH


def _on_sequencer(body, out_shape, sems, operands, after, sequencer_id, name):
    n = len(operands)

    def ordered_body(*refs):
        body(*refs[:n], *refs[n + 1:])

    extra = [] if after is None else [after]
    return pl.kernel(
        body if after is None else ordered_body, out_type=out_shape,
        mesh=plsc.ScalarSubcoreMesh(axis_name="sequencer", num_cores=1), scratch_types=sems,
        compiler_params=pltpu.CompilerParams(collective_id=sequencer_id), name=name)(*operands, *extra)


def _all_gather(xs, name, sequencer_id=None, after=None):
    n = len(xs)

    def body(*refs):
        x_refs, out_refs = refs[:n], refs[n:2 * n]
        send_sems, recv_sems, local_sems = refs[2 * n:]
        x_, y_, c_ = lax.axis_index("x"), lax.axis_index("y"), lax.axis_index("c")
        me, sibling = (x_, y_, c_), (x_, y_, 1 - c_)
        chips = [(1 - x_, y_), (x_, 1 - y_), (1 - x_, 1 - y_)]
        if sequencer_id is not None:
            barrier = pltpu.get_barrier_semaphore()
            peers = [sibling] + [(*chip, pc) for chip in chips for pc in (c_, 1 - c_)]
            for peer in peers:
                pl.semaphore_signal(barrier, inc=1, device_id=peer, device_id_type=MESH)
            pl.semaphore_wait(barrier, len(peers))

        def slot(a, px, py, pc):
            return out_refs[a].at[4 * px + 2 * py + pc]

        def copy(a, k, block, to, src=None):
            return pltpu.make_async_remote_copy(
                src_ref=slot(a, *block) if src is None else src, dst_ref=slot(a, *block),
                send_sem=send_sems.at[7 * a + k], recv_sem=recv_sems.at[7 * a + k], device_id=to, device_id_type=MESH)

        mine = [pltpu.make_async_copy(x_refs[a], slot(a, *me), local_sems.at[a]) for a in range(n)]
        for cp in mine:
            cp.start()
        first = []
        for a in range(n):
            first.append(copy(a, 0, me, sibling, src=x_refs[a]))
            first += [copy(a, 1 + j, me, (*chip, c_), src=x_refs[a]) for j, chip in enumerate(chips)]
        for cp in first:
            cp.start()
        passed = []
        for j, chip in enumerate(chips):
            for a in range(n):
                copy(a, 1 + j, (*chip, c_), me).wait_recv()
                passed.append(copy(a, 4 + j, (*chip, c_), sibling))
                passed[-1].start()
        for a in range(n):
            copy(a, 0, sibling, me).wait_recv()
        for j, chip in enumerate(chips):
            for a in range(n):
                copy(a, 4 + j, (*chip, 1 - c_), me).wait_recv()
        for cp in first + passed:
            cp.wait_send()
        for cp in mine:
            cp.wait()

    out_shape = [jax.ShapeDtypeStruct((N_DEV,) + x.shape, x.dtype) for x in xs]
    sems = [pltpu.SemaphoreType.DMA((7 * n,)), pltpu.SemaphoreType.DMA((7 * n,)), pltpu.SemaphoreType.DMA((n,))]
    if sequencer_id is not None:
        return _on_sequencer(body, out_shape, sems, xs, after, sequencer_id, name)
    return pl.pallas_call(
        body, name=name, out_shape=out_shape, in_specs=[_ANY] * n, out_specs=[_ANY] * n, scratch_shapes=sems)(*xs)


def _sibling_exchange(gs, name, sequencer_id=None, after=None):
    n = len(gs)

    def body(*refs):
        g_refs, p_refs = refs[:n], refs[n:2 * n]
        send_sems, recv_sems = refs[2 * n:]
        x_, y_, c_ = lax.axis_index("x"), lax.axis_index("y"), lax.axis_index("c")
        if sequencer_id is not None:
            barrier = pltpu.get_barrier_semaphore()
            pl.semaphore_signal(barrier, inc=1, device_id=(x_, y_, 1 - c_), device_id_type=MESH)
            pl.semaphore_wait(barrier, 1)
        copies = [pltpu.make_async_remote_copy(
            src_ref=g_refs[a].at[2 * k + (1 - c_)], dst_ref=p_refs[a].at[k], send_sem=send_sems.at[4 * a + k],
            recv_sem=recv_sems.at[4 * a + k], device_id=(x_, y_, 1 - c_), device_id_type=MESH)
            for a in range(n) for k in range(4)]
        for cp in copies:
            cp.start()
        for cp in copies:
            cp.wait()

    out_shape = [jax.ShapeDtypeStruct((4,) + g.shape[1:], g.dtype) for g in gs]
    sems = [pltpu.SemaphoreType.DMA((4 * n,)), pltpu.SemaphoreType.DMA((4 * n,))]
    if sequencer_id is not None:
        return _on_sequencer(body, out_shape, sems, gs, after, sequencer_id, name)
    return pl.pallas_call(
        body, name=name, out_shape=out_shape, in_specs=[_ANY] * n, out_specs=[_ANY] * n, scratch_shapes=sems)(*gs)


def _slab_tiles(rows, cols):
    if rows % 8 == 0:
        return _row_tile(rows), cols
    return rows, 2 * LANE


def _pair_sums(g, p, route, name):
    _, rows, cols = g.shape
    tr, tc = _slab_tiles(rows, cols)

    def body(route_ref, g_ref, p_ref, t_ref):
        t_ref[...] = (g_ref[...].astype(F32) + p_ref[...].astype(F32)).astype(BF16)

    return pl.pallas_call(
        body, name=name, out_shape=jax.ShapeDtypeStruct((3, rows, cols), BF16),
        grid_spec=pltpu.PrefetchScalarGridSpec(
            num_scalar_prefetch=1, grid=(3, rows // tr, cols // tc),
            in_specs=[pl.BlockSpec((1, tr, tc), lambda r, i, j, route_ref: (2 * route_ref[1 + r] + route_ref[0], i, j)),
                      pl.BlockSpec((1, tr, tc), lambda r, i, j, route_ref: (route_ref[1 + r], i, j))],
            out_specs=pl.BlockSpec((1, tr, tc), lambda r, i, j, route_ref: (r, i, j))),
        compiler_params=_params(("parallel", "parallel", "parallel")),
    )(route, g, p)


def _chip_exchange(ts, name, sequencer_id=None, after=None):
    n = len(ts)

    def body(*refs):
        t_refs, l_refs = refs[:n], refs[n:2 * n]
        send_sems, recv_sems = refs[2 * n:]
        x_, y_, c_ = lax.axis_index("x"), lax.axis_index("y"), lax.axis_index("c")
        chips = [(1 - x_, y_), (x_, 1 - y_), (1 - x_, 1 - y_)]
        if sequencer_id is not None:
            barrier = pltpu.get_barrier_semaphore()
            for px, py in chips:
                pl.semaphore_signal(barrier, inc=1, device_id=(px, py, c_), device_id_type=MESH)
            pl.semaphore_wait(barrier, len(chips))
        copies = [pltpu.make_async_remote_copy(
            src_ref=t_refs[a].at[r], dst_ref=l_refs[a].at[r], send_sem=send_sems.at[3 * a + r],
            recv_sem=recv_sems.at[3 * a + r], device_id=(px, py, c_), device_id_type=MESH)
            for a in range(n) for r, (px, py) in enumerate(chips)]
        for cp in copies:
            cp.start()
        for cp in copies:
            cp.wait()

    out_shape = [jax.ShapeDtypeStruct((3,) + t.shape[1:], t.dtype) for t in ts]
    sems = [pltpu.SemaphoreType.DMA((3 * n,)), pltpu.SemaphoreType.DMA((3 * n,))]
    if sequencer_id is not None:
        return _on_sequencer(body, out_shape, sems, ts, after, sequencer_id, name)
    return pl.pallas_call(
        body, name=name, out_shape=out_shape, in_specs=[_ANY] * n, out_specs=[_ANY] * n, scratch_shapes=sems)(*ts)


def _reduce_adamw(gs, ps, landed, place, w, m, v, name):
    layers, rows, cols = w.shape
    assert layers == DEPTH == 2
    tr, tc = _slab_tiles(rows, cols)
    nr, nc = rows // tr, cols // tc
    spec = pl.BlockSpec((1, tr, tc), lambda l, i, j, place_ref: (l, i, j))

    def own(layer, which):
        pi, pj = (nr - 1, nc - 1) if layer == 0 else (0, 0)

        def index(l, i, j, place_ref):
            lead = 0 if which is None else place_ref[which]
            return lead, jnp.where(l == layer, i, pi), jnp.where(l == layer, j, pj)

        return pl.BlockSpec((3 if which is None else 1, tr, tc), index)

    def body(place_ref, g0_ref, p0_ref, l0_ref, g1_ref, p1_ref, l1_ref, w_ref, m_ref, v_ref,
             g_ref, d_ref, nm_ref, nv_ref):
        def update(own_ref, sib_ref, l_ref):
            g = (own_ref[0].astype(F32) + sib_ref[0].astype(F32) + l_ref[0].astype(F32) + l_ref[1].astype(F32)
                 + l_ref[2].astype(F32))
            g_ref[0] = g
            d_ref[0], nm_ref[0], nv_ref[0] = _adamw_math(w_ref[0], g, m_ref[0], v_ref[0])

        @pl.when(pl.program_id(0) == 0)
        def _():
            update(g0_ref, p0_ref, l0_ref)

        @pl.when(pl.program_id(0) == 1)
        def _():
            update(g1_ref, p1_ref, l1_ref)

    out = jax.ShapeDtypeStruct(w.shape, F32)
    return pl.pallas_call(
        body, name=name, out_shape=(out, out, out, out),
        grid_spec=pltpu.PrefetchScalarGridSpec(
            num_scalar_prefetch=1, grid=(DEPTH, nr, nc),
            in_specs=[own(0, 0), own(0, 1), own(0, None), own(1, 0), own(1, 1), own(1, None), spec, spec, spec],
            out_specs=(spec, spec, spec, spec)),
        compiler_params=_params(("arbitrary", "arbitrary", "arbitrary")),
    )(place, gs[0], ps[0], landed[0], gs[1], ps[1], landed[1], w, m, v)


def _pack(pieces, row_multiple, dtype, cols=D, rows=None):
    flat = jnp.concatenate([p.astype(dtype).reshape(-1) for p in pieces])
    if rows is None:
        rows = -(-flat.shape[0] // cols)
        rows = -(-rows // row_multiple) * row_multiple
    flat = jnp.pad(flat, (0, rows * cols - flat.shape[0]))
    return flat.reshape(rows, cols)


def _unpack(flat, shapes, lead=()):
    out, off = [], 0
    for shp in shapes:
        n = 1
        for s_ in shp:
            n *= s_
        out.append(lax.slice_in_dim(flat, off, off + n, axis=len(lead)).reshape(lead + tuple(shp)))
        off += n
    return out


WIN_STRIDE = 704
WIN_ROWS = 720
Z_TURN = 1544


def _window(wt, me, name):
    padded = jnp.pad(wt, ((0, 0), (0, WIN_ROWS - IN_SHARD), (0, 0)))

    def body(me_ref, x_ref, o_ref):
        o_ref[0] = pltpu.roll(x_ref[0], me_ref[0], axis=0).astype(BF16)

    spec = pl.BlockSpec((1, WIN_ROWS, D), lambda l, me_ref: (l, 0, 0))
    return pl.pallas_call(
        body, name=name, out_shape=jax.ShapeDtypeStruct((DEPTH, WIN_ROWS, D), BF16),
        grid_spec=pltpu.PrefetchScalarGridSpec(num_scalar_prefetch=1, grid=(DEPTH,), in_specs=[spec], out_specs=spec),
        compiler_params=_params(("parallel",)),
    )(me, padded)


def _z_rows_from_windows(win):
    over = WIN_ROWS - WIN_STRIDE
    pieces = [(0, win[0][0:WIN_STRIDE])]
    for d in range(1, N_DEV):
        base = WIN_STRIDE * d
        pieces.append((base, win[d - 1][WIN_STRIDE:WIN_ROWS] + win[d][0:over]))
        pieces.append((base + over, win[d][over:WIN_STRIDE]))
    pieces.append((WIN_STRIDE * N_DEV, win[N_DEV - 1][WIN_STRIDE:WIN_ROWS]))

    def rows(a, b):
        out = []
        for start, arr in pieces:
            lo, hi = max(a, start), min(b, start + arr.shape[0])
            if lo < hi:
                out.append(arr[lo - start:hi - start])
        return out

    pad = jnp.zeros((NZ - IN_COLS, win.shape[-1]), win.dtype)
    return jnp.concatenate(rows(Z_TURN, IN_COLS) + rows(0, Z_TURN) + [pad], axis=0)


def _in_rows_from_z(wt):
    return jnp.concatenate([wt[Z_Q:Z_Q + 1536], wt[Z_F:Z_F + 8], wt[Z_PC:Z_PC + 1024], wt[Z_G:Z_G + 3072]], axis=0)


def _pad_rows(v, rows=8):
    return jnp.pad(v, ((0, rows - v.shape[0]), (0, 0)))


def _layer_fwd(l, x, wts, gvec, mod):
    tag = f"l{l}"
    z, h = _matmul(x, wts["w_in_t"], "nt", f"in_proj_{tag}", tm=1024, tn=1152, prologue=_prenorm_prologue(0, 0, 1),
                   prologue_vecs=[gvec, mod])
    qa, ka, va, kat = _attn_prep(z, wts["b_f"], f"attn_prep_{tag}")
    qa = wts["arrive"](qa)
    o, lse = _attn_fwd(qa, ka, va, f"attn_{tag}")
    br_b = _pool_fwd(z, wts["wp_bd"], wts["pool_scale"], f"pool_{tag}")
    br_c = _conv_fwd(z, wts["conv_w"], f"conv_{tag}")
    pa = _matmul(o, wts["wa"], "nn", f"proj_a_{tag}", out_dtype=BF16)
    pb = _matmul(br_b, wts["wb"], "nn", f"proj_b_{tag}", out_dtype=BF16)
    gates = [(z, Z_G + k * D) for k in range(3)]
    pc, merged = _matmul(br_c, wts["wc"], "nn", f"proj_c_merge_{tag}", tm=1024, tn=512,
                         extra=gates + [(pa, 0), (pb, 0)], epilogue=_merge_epilogue, out_dtypes=(BF16, BF16))
    y, x1 = _matmul(merged, wts["w_out"], "nn", f"out_proj_{tag}", tm=1024, tn=D, extra=[(x, 0)],
                    vec_extra=[gvec, mod], epilogue=_postnorm_epilogue(1, 2), out_dtypes=(F32, F32))
    a, r, h2 = _matmul(x1, wts["w_ff1"], "nn", f"ff1_{tag}", b_col_shards=True, epilogue=_relu2_epilogue,
                       out_dtypes=(BF16, BF16), prologue=_prenorm_prologue(2, 3, 4), prologue_vecs=[gvec, mod])
    y2, x2 = _matmul(r, wts["w_ff2"], "nn", f"ff2_{tag}", tm=1024, tn=D, tk=1024, extra=[(x1, 0)],
                     vec_extra=[gvec, mod], epilogue=_postnorm_epilogue(3, 5), out_dtypes=(F32, F32))
    saved = dict(x=x, h=h, z=z, qa=qa, ka=ka, va=va, kat=kat, o=o, lse=lse, br_b=br_b, br_c=br_c, pa=pa, pb=pb, pc=pc,
                 merged=merged, y=y, x1=x1, h2=h2, a=a, r=r, y2=y2)
    return x2, saved


def _ffn_bwd(l, dx2, sv, wts, gvec, mod, midpoint):
    tag = f"l{l}"
    dx2 = midpoint(dx2)
    da, dy2, sums = _matmul(sv["y2"], wts["w_ff2"], "nt", f"ff2_dx_{tag}", tm=1024, extra=[(sv["a"], 0)],
                            epilogue=_relu2_bwd_epilogue, out_dtypes=(BF16,), prologue=_postnorm_bwd_prologue(3, 5),
                            prologue_tiles=[dx2], prologue_vecs=[gvec, mod], prologue_sums=True)
    red_post_ff = jnp.sum(sums.reshape(-1, 8, D), axis=0)
    d_w_ff2 = _matmul(sv["r"], dy2, "tn", f"ff2_dw_{tag}", out_dtype=GRAD_DTYPE)
    dx1, sums = _matmul(da, wts["w_ff1"], "nt", f"ff1_dx_{tag}", tm=1024, tn=D, b_col_shards=True,
                        extra=[(sv["x1"], 0), (dx2, 0)], vec_extra=[gvec, mod], epilogue=_prenorm_bwd_epilogue(2, 4),
                        out_dtypes=(F32, F32), n_row_sums=1)
    red_pre_ff = jnp.sum(sums.reshape(-1, 8, D), axis=0)
    d_w_ff1 = _matmul(sv["h2"], da, "tn", f"ff1_dw_{tag}", out_dtype=GRAD_DTYPE, out_col_shards=True)
    return dx1, [d_w_ff1, d_w_ff2.reshape(N_DEV, D_FF // N_DEV, D)], (red_pre_ff, red_post_ff)


def _mixer_bwd(l, dx1, sv, wts, gvec, mod, ffn_reds, midpoint):
    tag = f"l{l}"
    red_pre_ff, red_post_ff = ffn_reds
    gates = [(sv["z"], Z_G + k * D) for k in range(3)]
    dpa, dpb, dpc, *dgl, dy, sums = _matmul(
        sv["y"], wts["w_out"], "nt", f"out_proj_dx_{tag}", tm=512, tn=1024,
        extra=gates + [(sv["pa"], 0), (sv["pb"], 0), (sv["pc"], 0)], epilogue=_merge_bwd_epilogue,
        out_dtypes=(BF16,) * 6, prologue=_postnorm_bwd_prologue(1, 2), prologue_tiles=[dx1], prologue_vecs=[gvec, mod],
        prologue_sums=True)
    red_post_mix = jnp.sum(sums.reshape(-1, 8, D), axis=0)
    d_w_out = _matmul(sv["merged"], dy, "tn", f"out_proj_dw_{tag}", out_dtype=GRAD_DTYPE)
    dpa = midpoint(dpa)
    do = _matmul(dpa, wts["wa"], "nt", f"proj_a_dx_{tag}")
    dbr_b = _matmul(dpb, wts["wb"], "nt", f"proj_b_dx_{tag}")
    dbr_c = _matmul(dpc, wts["wc"], "nt", f"proj_c_dx_{tag}")
    d_wa = _matmul(sv["o"], dpa, "tn", f"proj_a_dw_{tag}", out_dtype=GRAD_DTYPE)
    d_wb = _matmul(sv["br_b"], dpb, "tn", f"proj_b_dw_{tag}", out_dtype=GRAD_DTYPE)
    d_wc = _matmul(sv["br_c"], dpc, "tn", f"proj_c_dw_{tag}", out_dtype=GRAD_DTYPE)
    d_w_branch = jnp.concatenate([d_wa, d_wb, d_wc], axis=0)

    dpu, d_wp_bd, red_pool = _pool_bwd(sv["z"], wts["wp_bd"], wts["pool_scale"], dbr_b, f"pool_bwd_{tag}")
    dconv, red_conv = _conv_bwd(sv["z"], wts["conv_w"], dbr_c, f"conv_bwd_{tag}")
    qa2, doa = _attn_bwd_prep(sv["qa"], sv["o"], sv["lse"], do, f"attn_bwd_prep_{tag}")
    dqt, dka, dva = _attn_bwd(qa2, sv["ka"], sv["va"], sv["kat"], doa, f"attn_bwd_{tag}")
    dq, dk, dv, dfl, red_f = _attn_bwd_post(sv["z"], wts["b_f"], dqt, dka, dva, f"attn_bwd_post_{tag}")
    dz = _concat_columns([dpu, dconv, *dgl, dq, dk, dv, dfl], f"dz_{tag}")
    dx0, sums = _matmul(dz, wts["w_in_t"], "nn", f"in_proj_dx_{tag}", tm=1024, tn=D, tk=1152,
                        extra=[(sv["x"], 0), (dx1, 0)], vec_extra=[gvec, mod], epilogue=_prenorm_bwd_epilogue(0, 1),
                        out_dtypes=(F32, F32), n_row_sums=1)
    red_pre_mix = jnp.sum(sums.reshape(-1, 8, D), axis=0)
    d_w_in_t = _matmul(dz, sv["h"], "tn", f"in_proj_dw_{tag}", out_dtype=GRAD_DTYPE, tm=1152)

    rows = D // N_DEV
    big = [_in_rows_from_z(d_w_in_t).reshape(N_DEV, IN_SHARD, D), d_w_branch.reshape(N_DEV, rows, D),
           d_w_out.reshape(N_DEV, rows, D)]
    d_w_pool = jnp.stack([d_wp_bd[64 * g:64 * (g + 1), 64 * g:64 * (g + 1)] for g in range(4)])
    small = dict(
        mod=jnp.stack([red_pre_mix[0], red_pre_mix[1], red_post_mix[0], red_pre_ff[0], red_pre_ff[1], red_post_ff[0]]),
        g_mix_pre=red_pre_mix[2], g_mix_post=red_post_mix[1], g_ff_pre=red_pre_ff[2], g_ff_post=red_post_ff[1],
        b_f=red_f[0, 0:8], w_pool=d_w_pool, pool_scale=red_pool[0], conv_w=red_conv[0:3])
    return dx0, big, small


SMALL_KEYS = ["mod", "g_mix_pre", "g_mix_post", "g_ff_pre", "g_ff_post", "b_f", "w_pool", "pool_scale", "conv_w"]
SMALL_SHAPES = [(DEPTH, 6 * D), (DEPTH, D), (DEPTH, D), (DEPTH, D), (DEPTH, D), (DEPTH, 8), (DEPTH, 4, 64, 64),
                (DEPTH, POOL_W), (DEPTH, 3, CONV_W)]


def kernel(x, c, w_ada, b_ada, g_mix_pre, g_mix_post, g_ff_pre, g_ff_post, w_in, b_f, w_pool, pool_scale, conv_w, w_branch, w_out, w_ff1, w_ff2, loss_target, m_w_ada, m_b_ada, m_g_mix_pre, m_g_mix_post, m_g_ff_pre, m_g_ff_post, m_w_in, m_b_f, m_w_pool, m_pool_scale, m_conv_w, m_w_branch, m_w_out, m_w_ff1, m_w_ff2, v_w_ada, v_b_ada, v_g_mix_pre, v_g_mix_post, v_g_ff_pre, v_g_ff_post, v_w_in, v_b_f, v_w_pool, v_pool_scale, v_conv_w, v_w_branch, v_w_out, v_w_ff1, v_w_ff2):
    ix, iy, ic = lax.axis_index("x"), lax.axis_index("y"), lax.axis_index("c")
    me = 4 * ix + 2 * iy + ic
    route = jnp.stack([ic, 2 * (1 - ix) + iy, 2 * ix + (1 - iy), 2 * (1 - ix) + (1 - iy)]).astype(jnp.int32)
    place = jnp.stack([me, 2 * ix + iy]).astype(jnp.int32)
    wt_in, mt_in, vt_in = (jnp.transpose(a, (0, 2, 1)) for a in (w_in, m_w_in, v_w_in))

    c_all = _all_gather([_pad_rows(c)], "gather_c")[0][:, 0, :]
    c_pad = _pad_rows(c_all, ADA_ROWS)
    b_cols = lax.dynamic_slice_in_dim(b_ada, me * ADA_COLS, ADA_COLS, axis=1)
    b_cols = jnp.broadcast_to(b_cols[:, None, :], (DEPTH, 8, ADA_COLS))
    mod_part = _ada_fwd(c_pad, w_ada, b_cols, "ada_fwd")
    mod_all = _all_gather([mod_part.reshape(DEPTH * ADA_ROWS, ADA_COLS)], "gather_mod")[0]
    mod_all = mod_all.reshape(N_DEV, DEPTH, ADA_ROWS, ADA_COLS)
    mod_mine = lax.dynamic_index_in_dim(mod_all, me, axis=2, keepdims=False)
    mod_mine = jnp.transpose(mod_mine, (1, 0, 2)).reshape(DEPTH, 6, D)

    cw_cols = CONV_W // N_DEV
    cw_send = jnp.pad(conv_w.reshape(DEPTH * 3, cw_cols), ((0, 8 - DEPTH * 3), (0, LANE - cw_cols)))
    win_in = _window(wt_in, place[0:1], "w_in_window")
    send = [[w[l].astype(BF16) for w in (win_in, w_branch, w_out, w_ff1, w_ff2)] for l in range(DEPTH)]
    first = _all_gather(send[0][:1], "gather_weights_l0_in", sequencer_id=1, after=mod_all)
    rest = _all_gather(send[0][1:] + [cw_send], "gather_weights_l0_rest", sequencer_id=2, after=first[0])
    first1 = _all_gather(send[1][:1], "gather_weights_l1_in", sequencer_id=3, after=first[0])
    rest1 = _all_gather(send[1][1:], "gather_weights_l1_rest", sequencer_id=12, after=first[0])
    first, (mt_in, vt_in) = lax.optimization_barrier((first, (mt_in, vt_in)))
    gathered = [first + rest[:4], first1 + rest1]
    cw_all = rest[4][:, :DEPTH * 3, :cw_cols].reshape(N_DEV, DEPTH, 3, cw_cols)

    def first_operands(l, p_in):
        wp_bd = jnp.zeros((POOL_W, POOL_W), F32)
        for g in range(4):
            wp_bd = wp_bd.at[64 * g:64 * (g + 1), 64 * g:64 * (g + 1)].set(w_pool[l, g])
        return dict(w_in_t=_z_rows_from_windows(p_in), wp_bd=wp_bd.astype(BF16),
                    pool_scale=_pad_rows(pool_scale[l][None, :]), b_f=_pad_rows(jnp.pad(b_f[l], (0, LANE - 8))[None, :]))

    def rest_operands(l, rest):
        p_br, p_out, p_ff1, p_ff2 = rest
        w_br_full = p_br.reshape(D, D)
        cw_full = jnp.transpose(cw_all[:, l], (1, 0, 2)).reshape(3, CONV_W)
        return dict(wa=w_br_full[0:A_WIDTH], wb=w_br_full[A_WIDTH:A_WIDTH + POOL_W], wc=w_br_full[A_WIDTH + POOL_W:],
                    w_out=p_out.reshape(D, D), w_ff1=p_ff1, w_ff2=p_ff2.reshape(D_FF, D), conv_w=_pad_rows(cw_full))

    xs = x[0]
    saved, layers = [], []
    for l in range(DEPTH):
        p_in, rest = gathered[l][0], gathered[l][1:5]
        if l > 0:
            xs, p_in = lax.optimization_barrier((xs, p_in))
        wts = first_operands(l, p_in)

        def arrive(t, l=l, rest=rest, wts=wts):
            if l > 0:
                t, rest = lax.optimization_barrier((t, rest))
            wts.update(rest_operands(l, rest))
            return t

        wts["arrive"] = arrive
        gvec = _pad_rows(jnp.stack([g_mix_pre[l], g_mix_post[l], g_ff_pre[l], g_ff_post[l]]))
        layers.append((wts, gvec, _pad_rows(mod_mine[l])))
        xs, sv = _layer_fwd(l, xs, *layers[l])
        saved.append(sv)
    dx, loss_part = _loss_head(xs, loss_target[0], "loss_head")
    small_grads = [None] * DEPTH
    mine, sibs, landed = ({} for _ in range(3))
    seq_id = iter(range(4, 4 + 4 * DEPTH))
    last = [gathered[DEPTH - 1][1]]

    def start(group, grads):
        mine[group] = grads
        sibs[group] = _sibling_exchange(grads, f"rs_sibling_{group}", sequencer_id=next(seq_id), after=last[0])
        last[0] = sibs[group][0]

    def finish(group, later):
        later, (grads, sib) = lax.optimization_barrier((later, (mine[group], sibs[group])))
        sends = [_pair_sums(g, p, route, f"rs_pair_sums_{group}_{k}") for k, (g, p) in enumerate(zip(grads, sib))]
        later, sends = lax.optimization_barrier((later, sends))
        landed[group] = _chip_exchange(sends, f"rs_chips_{group}", sequencer_id=next(seq_id), after=last[0])
        last[0] = landed[group][0]
        return later

    pending = None
    for l in reversed(range(DEPTH)):
        hook = (lambda da: da) if pending is None else functools.partial(finish, pending)
        dx, ffn_grads, ffn_reds = _ffn_bwd(l, dx, saved[l], *layers[l], hook)
        start(f"ffn_l{l}", ffn_grads)
        dx, mix_grads, small_grads[l] = _mixer_bwd(l, dx, saved[l], *layers[l], ffn_reds,
                                                   functools.partial(finish, f"ffn_l{l}"))
        start(f"mix_l{l}", mix_grads)
        pending = f"mix_l{l}"
    grad_x = dx[None]

    big_w = [wt_in, w_branch, w_out, w_ff1, w_ff2]
    big_m = [mt_in, m_w_branch, m_w_out, m_w_ff1, m_w_ff2]
    big_v = [vt_in, v_w_branch, v_w_out, v_w_ff1, v_w_ff2]
    where = [("mix", 0), ("mix", 1), ("mix", 2), ("ffn", 0), ("ffn", 1)]

    def reduce_and_update(k):
        group, at = where[k]
        return _reduce_adamw([mine[f"{group}_l{l}"][at] for l in range(DEPTH)],
                             [sibs[f"{group}_l{l}"][at] for l in range(DEPTH)],
                             [landed[f"{group}_l{l}"][at] for l in range(DEPTH)], place, big_w[k], big_m[k], big_v[k],
                             f"rs_sum_adamw_{k}")

    big_res = {k: list(reduce_and_update(k)) for k in (3, 4)}
    big_res[3][0] = finish(pending, big_res[3][0])

    small = {k: jnp.stack([small_grads[l][k] for l in range(DEPTH)]) for k in SMALL_KEYS}
    payload = _pack([small[k] for k in SMALL_KEYS] + [loss_part[0:1, 0:1]], 8, F32)
    small_all = _all_gather([payload], "gather_small")[0]
    dmod_all = small_all[:, 0:DEPTH * 6, :].reshape(N_DEV, DEPTH, 6 * D)
    summed = _unpack(_sum_slabs(small_all, "sum_small").reshape(-1), SMALL_SHAPES + [(1, 1)])
    sg = dict(zip(SMALL_KEYS, summed))
    loss = summed[-1][0, 0]
    dmod_cols = lax.dynamic_slice_in_dim(dmod_all, me * ADA_COLS, ADA_COLS, axis=2)
    dmod_cols = jnp.pad(jnp.transpose(dmod_cols, (1, 0, 2)), ((0, 0), (0, ADA_ROWS - N_DEV), (0, 0)))
    g_w_ada = _ada_bwd(c_pad, dmod_cols, "ada_bwd")
    g_conv_w = lax.dynamic_slice_in_dim(sg["conv_w"], me * (CONV_W // N_DEV), CONV_W // N_DEV, axis=2)

    ada_out = [g_w_ada] + list(_adamw(w_ada, g_w_ada, m_w_ada, v_w_ada, "adamw_ada"))
    rest_w = [b_ada, g_mix_pre, g_mix_post, g_ff_pre, g_ff_post, b_f, w_pool, pool_scale, conv_w]
    rest_m = [m_b_ada, m_g_mix_pre, m_g_mix_post, m_g_ff_pre, m_g_ff_post, m_b_f, m_w_pool, m_pool_scale, m_conv_w]
    rest_v = [v_b_ada, v_g_mix_pre, v_g_mix_post, v_g_ff_pre, v_g_ff_post, v_b_f, v_w_pool, v_pool_scale, v_conv_w]
    rest_g = [sg["mod"], sg["g_mix_pre"], sg["g_mix_post"], sg["g_ff_pre"], sg["g_ff_post"], sg["b_f"],
              sg["w_pool"], sg["pool_scale"], g_conv_w]
    rest_shapes = [a.shape for a in rest_w]
    upd = _adamw(_pack(rest_w, 8, F32)[None], _pack(rest_g, 8, F32)[None], _pack(rest_m, 8, F32)[None],
                 _pack(rest_v, 8, F32)[None], "adamw_rest")
    rest_out = [rest_g] + [_unpack(arr.reshape(-1), rest_shapes) for arr in upd]
    rest_out = [[ada_out[which]] + rest_out[which] for which in range(4)]

    landed[pending], rest_out = lax.optimization_barrier((landed[pending], rest_out))
    big_res.update({k: reduce_and_update(k) for k in (0, 1, 2)})
    big_out = [[jnp.transpose(big_res[k][which], (0, 2, 1)) if k == 0 else big_res[k][which] for k in range(5)]
               for which in range(4)]

    def ordered(k):
        r, b = rest_out[k], big_out[k]
        return [r[0], r[1], r[2], r[3], r[4], r[5], b[0], r[6], r[7], r[8], r[9], b[1], b[2], b[3], b[4]]

    return (loss, grad_x, *ordered(0), *ordered(1), *ordered(2), *ordered(3))
```

```python
import functools

import jax
import jax.numpy as jnp
from jax import lax
from jax.experimental import pallas as pl
from jax.experimental.pallas import tpu as pltpu
from jax.experimental.pallas import tpu_sc as plsc

F32 = jnp.float32
BF16 = jnp.bfloat16
GRAD_DTYPE = BF16

N_DEV = 8
D = 1024
S = 2048
DEPTH = 2
D_FF = 4 * D
A_WIDTH = 512
HEAD_DIM = 64
N_PAIR = 4
POOL_W = 256
CONV_W = 256
IN_COLS = 5640
ADA_COLS = 6 * D // N_DEV
IN_SHARD = IN_COLS // N_DEV
RMS_EPS = 1e-6
NEG_INF = -1e30
ATT_SCALE = HEAD_DIM ** -0.5

NZ = 5760
Z_PC = 0
Z_G = 1024
Z_Q = 4096
Z_K = 4608
Z_V = 5120
Z_F = 5632

LR, B1, B2, EPS, WD, STEP = 0.001, 0.9, 0.999, 1e-08, 0.01, 10

LANE = 128
VMEM_LIMIT_BYTES = 48 * 1024 * 1024
TS = 512
TQ = 512
TQ_FWD = 512
HEADS_PER_STEP = 8
HEADS_PER_STEP_FWD = 8


def _params(sem=None):
    return pltpu.CompilerParams(dimension_semantics=sem, vmem_limit_bytes=VMEM_LIMIT_BYTES)


def _pick(n, target):
    best = None
    for t in range(LANE, min(n, target) + 1, LANE):
        if n % t == 0:
            best = t
    return n if best is None else best


def _matmul(a, b, mode, name, out_dtype=F32, tm=2048, tn=1024, tk=2048, b_col_shards=False, out_col_shards=False,
            extra=(), vec_extra=(), epilogue=None, out_dtypes=None, n_row_sums=0, prologue=None, prologue_tiles=(), prologue_vecs=(),
            prologue_sums=False):
    if b_col_shards:
        shards, b_rows, shard_cols = b.shape
        b_shape = (b_rows, shards * shard_cols)
    else:
        b_shape = b.shape
    if mode == "nn":
        (m, k), (k2, n) = a.shape, b_shape
    elif mode == "nt":
        (m, k), (n, k2) = a.shape, b_shape
    else:
        (k, m), (k2, n) = a.shape, b_shape
    assert k == k2, (a.shape, b.shape, mode)
    tm, tn, tk = _pick(m, tm), _pick(n, tn), _pick(k, tk)
    if b_col_shards and mode == "nn":
        tn = shard_cols
    per_step = 1
    if b_col_shards and mode == "nt":
        per_step = max(1, min(tk, 1024) // shard_cols)
        tk = per_step * shard_cols
    if out_col_shards:
        tn = n // N_DEV
    nk = k // tk
    if mode == "nn":
        a_spec = pl.BlockSpec((tm, tk), lambda i, j, kk: (i, kk))
        b_spec = (pl.BlockSpec((None, tk, tn), lambda i, j, kk: (j, kk, 0)) if b_col_shards else
                  pl.BlockSpec((tk, tn), lambda i, j, kk: (kk, j)))
        dims = (((1,), (0,)), ((), ()))
    elif mode == "nt":
        a_spec = pl.BlockSpec((tm, tk), lambda i, j, kk: (i, kk))
        b_spec = (pl.BlockSpec((per_step, tn, shard_cols), lambda i, j, kk: (kk, j, 0)) if b_col_shards else
                  pl.BlockSpec((tn, tk), lambda i, j, kk: (j, kk)))
        dims = (((1,), (1,)), ((), ()))
    else:
        assert not b_col_shards
        a_spec = pl.BlockSpec((tk, tm), lambda i, j, kk: (kk, i))
        b_spec = pl.BlockSpec((tk, tn), lambda i, j, kk: (kk, j))
        dims = (((0,), (0,)), ((), ()))
    if out_col_shards:
        out_shape = jax.ShapeDtypeStruct((N_DEV, m, tn), out_dtype)
        out_spec = pl.BlockSpec((None, tm, tn), lambda i, j, kk: (j, i, 0))
    else:
        out_shape = jax.ShapeDtypeStruct((m, n), out_dtype)
        out_spec = pl.BlockSpec((tm, tn), lambda i, j, kk: (i, j))

    n_extra = len(extra) + len(vec_extra)
    extra_specs = [pl.BlockSpec((tm, tn), lambda i, j, kk, off=off: (i, j + off // tn)) for _, off in extra]
    extra_specs += [pl.BlockSpec((8, tn), lambda i, j, kk: (0, j)) for _ in vec_extra]
    if epilogue is not None:
        assert not out_col_shards and all(off % tn == 0 for _, off in extra)
        out_shape = [jax.ShapeDtypeStruct((m, n), dt) for dt in out_dtypes]
        out_spec = [pl.BlockSpec((tm, tn), lambda i, j, kk: (i, j)) for _ in out_dtypes]
        for at in range(len(out_dtypes) - n_row_sums, len(out_dtypes)):
            out_shape[at] = jax.ShapeDtypeStruct((8 * (m // tm), n), out_dtypes[at])
            out_spec[at] = pl.BlockSpec((8, tn), lambda i, j, kk: (i, j))

    def product(a_ref, b_ref):
        if b_col_shards and mode == "nt":
            b_tile = jnp.concatenate([b_ref[s] for s in range(per_step)], axis=1) if per_step > 1 else b_ref[0]
        else:
            b_tile = b_ref[...]
        return lax.dot_general(a_ref[...].astype(BF16), b_tile.astype(BF16), dims, preferred_element_type=F32)

    def write(acc, extra_refs, o_refs):
        if epilogue is None:
            o_refs[0][...] = acc.astype(out_dtype)
        else:
            for o_ref, tile in zip(o_refs, epilogue(acc, *[r[...] for r in extra_refs])):
                o_ref[...] = tile.astype(o_ref.dtype)

    def body_one_pass(a_ref, b_ref, *refs):
        write(product(a_ref, b_ref), refs[:n_extra], refs[n_extra:])

    if prologue is not None:
        assert nk == 1 and mode in ("nn", "nt")
        n_pro = len(prologue_tiles) + len(prologue_vecs)
        sums = 1 if prologue_sums else 0
        outs = out_shape if isinstance(out_shape, list) else [out_shape]
        out_specs_all = (out_spec if isinstance(out_spec, list) else [out_spec]) + [
            pl.BlockSpec((tm, tk), lambda i, j, kk: (i, 0))]
        outs = outs + [jax.ShapeDtypeStruct((m, k), BF16)]
        if sums:
            outs.append(jax.ShapeDtypeStruct((8 * (m // tm), k), F32))
            out_specs_all.append(pl.BlockSpec((8, tk), lambda i, j, kk: (i, 0)))

        def body_prologue(a_ref, b_ref, *refs):
            pro_refs, rest = refs[:n_pro], refs[n_pro:]
            left_ref = rest[-1]
            left_out = rest[-2 - sums]

            @pl.when(pl.program_id(1) == 0)
            def _():
                made = prologue(a_ref[...], *[r[...] for r in pro_refs])
                left = (made[0] if sums else made).astype(BF16)
                left_ref[...] = left
                left_out[...] = left
                if sums:
                    rest[-2][...] = made[1]

            write(product(left_ref, b_ref), rest[:n_extra], rest[n_extra:-2 - sums])

        pro_specs = [a_spec for _ in prologue_tiles] + [pl.BlockSpec((8, tk), lambda i, j, kk: (0, 0)) for _ in prologue_vecs]
        return pl.pallas_call(
            body_prologue, name=name, out_shape=outs, grid=(m // tm, n // tn, nk),
            in_specs=[a_spec, b_spec] + pro_specs + extra_specs,
            out_specs=out_specs_all,
            scratch_shapes=[pltpu.VMEM((tm, tk), BF16)],
            compiler_params=_params(("parallel", "arbitrary", "arbitrary")),
        )(a, b, *prologue_tiles, *prologue_vecs, *[x for x, _ in extra], *vec_extra)

    def body(a_ref, b_ref, *refs):
        acc_ref = refs[-1]
        kk = pl.program_id(2)

        @pl.when(kk == 0)
        def _():
            acc_ref[...] = product(a_ref, b_ref)

        @pl.when(kk > 0)
        def _():
            acc_ref[...] += product(a_ref, b_ref)

        @pl.when(kk == nk - 1)
        def _():
            write(acc_ref[...], refs[:n_extra], refs[n_extra:-1])

    return pl.pallas_call(
        body_one_pass if nk == 1 else body, name=name,
        out_shape=out_shape,
        grid=(m // tm, n // tn, nk),
        in_specs=[a_spec, b_spec] + extra_specs,
        out_specs=out_spec,
        scratch_shapes=[] if nk == 1 else [pltpu.VMEM((tm, tn), F32)],
        compiler_params=_params(("parallel", "parallel", "arbitrary")),
    )(a, b, *[x for x, _ in extra], *vec_extra)


def _row_spec(width=D, col=0):
    return pl.BlockSpec((TS, width), lambda i: (i, col))


def _vec_spec(rows=8, width=D):
    return pl.BlockSpec((rows, width), lambda i: (0, 0))


def _rms(x):
    return lax.rsqrt(jnp.mean(x * x, axis=-1, keepdims=True) + RMS_EPS)


def _concat_columns(pieces, name):
    widths = [p.shape[1] for p in pieces]
    offsets = [sum(widths[:k]) for k in range(len(widths))]

    def body(*refs):
        o_ref = refs[-1]
        for ref, off, w in zip(refs[:-1], offsets, widths):
            o_ref[:, off:off + w] = ref[...]

    return pl.pallas_call(
        body, name=name, out_shape=jax.ShapeDtypeStruct((S, sum(widths)), pieces[0].dtype), grid=(S // TS,),
        in_specs=[_row_spec(w) for w in widths], out_specs=_row_spec(sum(widths)),
        compiler_params=_params(("parallel",)),
    )(*pieces)


def _loss_head(xf, target, name):
    def body(x_ref, t_ref, dx_ref, loss_ref):
        i = pl.program_id(0)

        @pl.when(i == 0)
        def _():
            loss_ref[...] = jnp.zeros_like(loss_ref)

        e = x_ref[...] - t_ref[...]
        dx_ref[...] = e / float(D)
        per_tok = jnp.mean(e * e, axis=-1, keepdims=True)
        loss_ref[0:1, 0:1] += 0.5 * jnp.sum(per_tok, axis=0, keepdims=True)

    return pl.pallas_call(
        body, name=name,
        out_shape=(jax.ShapeDtypeStruct((S, D), F32), jax.ShapeDtypeStruct((8, LANE), F32)),
        grid=(S // TS,),
        in_specs=[_row_spec(), _row_spec()],
        out_specs=(_row_spec(), pl.BlockSpec((8, LANE), lambda i: (0, 0))),
        compiler_params=_params(("arbitrary",)),
    )(xf, target)


def _relu2_epilogue(a):
    t = jnp.maximum(a, 0.0)
    return a, t * t


def _relu2_bwd_epilogue(dr, a):
    return (dr * (2.0 * jnp.maximum(a, 0.0)),)


def _merge_epilogue(pc, g0, g1, g2, pa, pb):
    return pc, jax.nn.sigmoid(g0) * pa + jax.nn.sigmoid(g1) * pb + jax.nn.sigmoid(g2) * pc


def _prenorm_prologue(g_row, shift_row, scale_row):
    def prologue(x, gvec, mod):
        y = x * _rms(x) * gvec[g_row:g_row + 1, :]
        return y * (1.0 + mod[scale_row:scale_row + 1, :]) + mod[shift_row:shift_row + 1, :]

    return prologue


def _rows8(*rows):
    sub = lax.broadcasted_iota(jnp.int32, (8, rows[0].shape[1]), 0)
    out = jnp.zeros((8, rows[0].shape[1]), F32)
    for k, r in enumerate(rows):
        out = jnp.where(sub == k, r, out)
    return out


def _postnorm_bwd_prologue(g_row, gate_row):
    def prologue(y, dxo, gvec, mod):
        g = gvec[g_row:g_row + 1, :]
        r = _rms(y)
        n = y * r
        dyn = dxo * mod[gate_row:gate_row + 1, :]
        dn = dyn * g
        dy = r * (dn - n * jnp.mean(dn * n, axis=-1, keepdims=True))
        return dy, _rows8(jnp.sum(dxo * (n * g), axis=0, keepdims=True), jnp.sum(dyn * n, axis=0, keepdims=True))

    return prologue


def _prenorm_bwd_epilogue(g_row, scale_row):
    def epilogue(dh, x, dres, gvec, mod):
        g = gvec[g_row:g_row + 1, :]
        r = _rms(x)
        n = x * r
        dyg = dh * (1.0 + mod[scale_row:scale_row + 1, :])
        dn = dyg * g
        dx = r * (dn - n * jnp.mean(dn * n, axis=-1, keepdims=True))
        sums = _rows8(jnp.sum(dh, axis=0, keepdims=True), jnp.sum(dh * (n * g), axis=0, keepdims=True),
                      jnp.sum(dyg * n, axis=0, keepdims=True))
        return dres + dx, sums

    return epilogue


def _postnorm_epilogue(g_row, gate_row):
    def epilogue(y, x, gvec, mod):
        yn = y * _rms(y) * gvec[g_row:g_row + 1, :]
        return y, x + mod[gate_row:gate_row + 1, :] * yn

    return epilogue


def _merge_bwd_epilogue(dm, g0, g1, g2, pa, pb, pc):
    sg = [jax.nn.sigmoid(g) for g in (g0, g1, g2)]
    return tuple(dm * s for s in sg) + tuple(dm * p * (s * (1.0 - s)) for p, s in zip((pa, pb, pc), sg))


def _shift_down(x, k, row):
    return jnp.where(row >= k, pltpu.roll(x, k, axis=0), 0.0)


def _shift_up(x, k, row):
    n = x.shape[0]
    return jnp.where(row < n - k, pltpu.roll(x, n - k, axis=0), 0.0)


def _cumsum_rows(x, row, reverse=False):
    shift = _shift_up if reverse else _shift_down
    k = 1
    while k < x.shape[0]:
        x = x + shift(x, k, row)
        k *= 2
    return x


def _full_spec(shape, idx=(0, 0)):
    return pl.BlockSpec(shape, lambda i: idx)


def _pool_window_select(lane, a2, a4, a8, a16):
    return jnp.where(lane < 64, a2, jnp.where(lane < 128, a4, jnp.where(lane < 192, a8, a16)))


def _pool_p(u, row, lane):
    t2 = u + _shift_down(u, 1, row)
    t4 = t2 + _shift_down(t2, 2, row)
    t8 = t4 + _shift_down(t4, 4, row)
    t16 = t8 + _shift_down(t8, 8, row)
    tw = _pool_window_select(lane, t2, t4, t8, t16)
    cnt = jnp.minimum((row + 1).astype(F32), _pool_window_select(lane, 2.0, 4.0, 8.0, 16.0))
    return tw / cnt - u, cnt


def _pool_fwd(z, wp_bd, pscale, name):
    def body(u_ref, w_ref, s_ref, o_ref):
        row = lax.broadcasted_iota(jnp.int32, (S, POOL_W), 0)
        lane = lax.broadcasted_iota(jnp.int32, (S, POOL_W), 1)
        p, _ = _pool_p(u_ref[...], row, lane)
        y = jnp.dot(p.astype(BF16), w_ref[...], preferred_element_type=F32)
        o_ref[...] = y * s_ref[0:1, :]

    return pl.pallas_call(
        body, name=name, out_shape=jax.ShapeDtypeStruct((S, POOL_W), F32), grid=(1,),
        in_specs=[_full_spec((S, POOL_W), (0, Z_PC // POOL_W)), _full_spec((POOL_W, POOL_W)), _full_spec((8, POOL_W))],
        out_specs=_full_spec((S, POOL_W)),
        compiler_params=_params(("arbitrary",)),
    )(z, wp_bd, pscale)


def _pool_bwd(z, wp_bd, pscale, dbr, name):
    def body(u_ref, w_ref, s_ref, dbr_ref, du_ref, dw_ref, red_ref):
        row = lax.broadcasted_iota(jnp.int32, (S, POOL_W), 0)
        lane = lax.broadcasted_iota(jnp.int32, (S, POOL_W), 1)
        p, cnt = _pool_p(u_ref[...], row, lane)
        pb = p.astype(BF16)
        y = jnp.dot(pb, w_ref[...], preferred_element_type=F32)
        dbr = dbr_ref[...]
        red_ref[...] = jnp.zeros_like(red_ref)
        red_ref[0:1, :] = jnp.sum(dbr * y, axis=0, keepdims=True)
        dy = (dbr * s_ref[0:1, :]).astype(BF16)
        dw_ref[...] = lax.dot_general(pb, dy, (((0,), (0,)), ((), ())), preferred_element_type=F32)
        dp = lax.dot_general(dy, w_ref[...], (((1,), (1,)), ((), ())), preferred_element_type=F32)
        g = dp / cnt
        a2 = g + _shift_up(g, 1, row)
        a4 = a2 + _shift_up(a2, 2, row)
        a8 = a4 + _shift_up(a4, 4, row)
        a16 = a8 + _shift_up(a8, 8, row)
        du_ref[...] = (_pool_window_select(lane, a2, a4, a8, a16) - dp).astype(BF16)

    return pl.pallas_call(
        body, name=name,
        out_shape=(jax.ShapeDtypeStruct((S, POOL_W), BF16), jax.ShapeDtypeStruct((POOL_W, POOL_W), F32),
                   jax.ShapeDtypeStruct((8, POOL_W), F32)),
        grid=(1,),
        in_specs=[_full_spec((S, POOL_W), (0, Z_PC // POOL_W)), _full_spec((POOL_W, POOL_W)), _full_spec((8, POOL_W)),
                  _full_spec((S, POOL_W))],
        out_specs=(_full_spec((S, POOL_W)), _full_spec((POOL_W, POOL_W)), _full_spec((8, POOL_W))),
        compiler_params=_params(("arbitrary",)),
    )(z, wp_bd, pscale, dbr)


def _conv_specs():
    base = Z_PC // CONV_W
    return [_full_spec((S, CONV_W), (0, base + 1)), _full_spec((S, CONV_W), (0, base + 2)),
            _full_spec((S, CONV_W), (0, base + 3)), _full_spec((8, CONV_W))]


def _conv_fwd(z, cw, name):
    def body(h_ref, b_ref, c_ref, w_ref, o_ref):
        row = lax.broadcasted_iota(jnp.int32, (S, CONV_W), 0)
        u = c_ref[...] * h_ref[...]
        y = (w_ref[0:1, :] * _shift_down(u, 2, row) + w_ref[1:2, :] * _shift_down(u, 1, row) + w_ref[2:3, :] * u)
        o_ref[...] = b_ref[...] * y

    return pl.pallas_call(
        body, name=name, out_shape=jax.ShapeDtypeStruct((S, CONV_W), F32), grid=(1,),
        in_specs=_conv_specs(), out_specs=_full_spec((S, CONV_W)),
        compiler_params=_params(("arbitrary",)),
    )(z, z, z, cw)


def _conv_bwd(z, cw, dbr, name):
    def body(h_ref, b_ref, c_ref, w_ref, dbr_ref, d_ref, red_ref):
        row = lax.broadcasted_iota(jnp.int32, (S, CONV_W), 0)
        h, cg = h_ref[...], c_ref[...]
        u = cg * h
        u1 = _shift_down(u, 1, row)
        u2 = _shift_down(u, 2, row)
        y = w_ref[0:1, :] * u2 + w_ref[1:2, :] * u1 + w_ref[2:3, :] * u
        dbr = dbr_ref[...]
        dy = dbr * b_ref[...]
        du = w_ref[2:3, :] * dy + w_ref[1:2, :] * _shift_up(dy, 1, row) + w_ref[0:1, :] * _shift_up(dy, 2, row)
        d_ref[:, 0:CONV_W] = (du * cg).astype(BF16)
        d_ref[:, CONV_W:2 * CONV_W] = (dbr * y).astype(BF16)
        d_ref[:, 2 * CONV_W:3 * CONV_W] = (du * h).astype(BF16)
        red_ref[...] = jnp.zeros_like(red_ref)
        red_ref[0:1, :] = jnp.sum(dy * u2, axis=0, keepdims=True)
        red_ref[1:2, :] = jnp.sum(dy * u1, axis=0, keepdims=True)
        red_ref[2:3, :] = jnp.sum(dy * u, axis=0, keepdims=True)

    return pl.pallas_call(
        body, name=name,
        out_shape=(jax.ShapeDtypeStruct((S, 3 * CONV_W), BF16), jax.ShapeDtypeStruct((8, CONV_W), F32)),
        grid=(1,),
        in_specs=_conv_specs() + [_full_spec((S, CONV_W))],
        out_specs=(_full_spec((S, 3 * CONV_W)), _full_spec((8, CONV_W))),
        compiler_params=_params(("arbitrary",)),
    )(z, z, z, cw, dbr)


_NT = (((1,), (1,)), ((), ()))
_TN = (((0,), (0,)), ((), ()))
N_HEAD = 2 * N_PAIR


def _split3(x):
    hi = x.astype(BF16).astype(F32)
    mid = (x - hi).astype(BF16).astype(F32)
    lo = (x - hi - mid).astype(BF16).astype(F32)
    return hi, mid, lo


def _spare(lane, e, k):
    return lane == 64 * (1 - e) + k


def _spare3(lane, e, k):
    base = 64 * (1 - e) + k
    return (lane >= base) & (lane < base + 3)


def _put3(lane, e, k, pieces, rest):
    out = rest
    for n, piece in enumerate(pieces):
        out = jnp.where(_spare(lane, e, k + n), piece, out)
    return out


def _attn_prep(z, bf, name):
    def body(q_ref, k_ref, v_ref, f_ref, b_ref, qa_ref, ka_ref, va_ref, kat_ref, cum_ref):
        p = pl.program_id(0)
        row = lax.broadcasted_iota(jnp.int32, (S, LANE), 0)
        lane = lax.broadcasted_iota(jnp.int32, (S, LANE), 1)

        @pl.when(p == 0)
        def _():
            xv = f_ref[...] + b_ref[0:1, :]
            ls = jnp.minimum(xv, 0.0) - jnp.log(1.0 + jnp.exp(-jnp.abs(xv)))
            cum_ref[...] = _cumsum_rows(jnp.where(lane < N_HEAD, ls, 0.0), row)

        cum = cum_ref[...]
        q, k, v = q_ref[...], k_ref[...], v_ref[...]
        for e in range(2):
            head = (lane >= 64) if e else (lane < 64)
            f = jnp.sum(jnp.where(lane == 2 * p + e, cum, 0.0), axis=1, keepdims=True)
            pieces = _split3(f)
            qa = jnp.where(head, q * ATT_SCALE, _put3(lane, e, 0, pieces, jnp.where(_spare3(lane, e, 3), 1.0, 0.0)))
            ones = jnp.where(_spare3(lane, e, 0) | _spare3(lane, e, 6), 1.0, 0.0)
            ka = jnp.where(head, k, _put3(lane, e, 3, [-x for x in pieces], ones))
            va = jnp.where(head, v, jnp.where(_spare3(lane, e, 0), 1.0, 0.0))
            qa_ref[e] = qa.astype(BF16)
            ka_ref[e] = ka.astype(BF16)
            va_ref[e] = va.astype(BF16)
            kat_ref[e] = ka.T.astype(BF16)

    qb, kb, vb = Z_Q // LANE, Z_K // LANE, Z_V // LANE
    heads = jax.ShapeDtypeStruct((N_HEAD, S, LANE), BF16)
    pair = pl.BlockSpec((2, S, LANE), lambda p: (p, 0, 0))
    return pl.pallas_call(
        body, name=name,
        out_shape=(heads, heads, heads, jax.ShapeDtypeStruct((N_HEAD, LANE, S), BF16)),
        grid=(N_PAIR,),
        in_specs=[pl.BlockSpec((S, LANE), lambda p: (0, qb + p)), pl.BlockSpec((S, LANE), lambda p: (0, kb + p)),
                  pl.BlockSpec((S, LANE), lambda p: (0, vb + p)), pl.BlockSpec((S, LANE), lambda p: (0, Z_F // LANE)),
                  pl.BlockSpec((8, LANE), lambda p: (0, 0))],
        out_specs=(pair, pair, pair, pl.BlockSpec((2, LANE, S), lambda p: (p, 0, 0))),
        scratch_shapes=[pltpu.VMEM((S, LANE), F32)],
        compiler_params=_params(("arbitrary",)),
    )(z, z, z, z, bf)


def _attn_bwd_prep(qa, o, lse, do, name):
    def body(qa_ref, o_ref, lse_ref, do_ref, qa2_ref, doa_ref):
        lane = lax.broadcasted_iota(jnp.int32, (S, LANE), 1)
        dov, ov, lsev = do_ref[...], o_ref[...], lse_ref[...]
        for e in range(2):
            head = (lane >= 64) if e else (lane < 64)
            dsum = jnp.sum(jnp.where(head, dov * ov, 0.0), axis=1, keepdims=True)
            doa_ref[e] = jnp.where(head, dov, _put3(lane, e, 0, [-x for x in _split3(dsum)], 0.0)).astype(BF16)
            lse_col = lsev[:, 64 * e:64 * e + 1]
            qa2_ref[e] = _put3(lane, e, 6, [-x for x in _split3(lse_col)], qa_ref[e].astype(F32)).astype(BF16)

    heads = jax.ShapeDtypeStruct((N_HEAD, S, LANE), BF16)
    pair = pl.BlockSpec((2, S, LANE), lambda p: (p, 0, 0))
    cols = pl.BlockSpec((S, LANE), lambda p: (0, p))
    return pl.pallas_call(
        body, name=name, out_shape=(heads, heads), grid=(N_PAIR,),
        in_specs=[pair, cols, cols, cols], out_specs=(pair, pair),
        compiler_params=_params(("parallel",)),
    )(qa, o, lse, do)


def _attn_bwd_post(z, bf, dqt, dka, dva, name):
    def body(f_ref, b_ref, dqt_ref, dk_ref, dv_ref, dq_out, dk_out, dv_out, dfl_ref, red_ref, dcum_ref):
        p = pl.program_id(0)

        @pl.when(p == 0)
        def _():
            dcum_ref[...] = jnp.zeros_like(dcum_ref)

        row = lax.broadcasted_iota(jnp.int32, (S, LANE), 0)
        lane = lax.broadcasted_iota(jnp.int32, (S, LANE), 1)
        dqa = [dqt_ref[e].T for e in range(2)]
        dq_out[...] = (jnp.where(lane < 64, dqa[0], dqa[1]) * ATT_SCALE).astype(BF16)
        dk_out[...] = jnp.where(lane < 64, dk_ref[0], dk_ref[1]).astype(BF16)
        dv_out[...] = jnp.where(lane < 64, dv_ref[0], dv_ref[1]).astype(BF16)
        for e in range(2):
            d_query = jnp.sum(jnp.where(_spare(lane, e, 0), dqa[e], 0.0), axis=1, keepdims=True)
            d_key = jnp.sum(jnp.where(_spare(lane, e, 3), dk_ref[e], 0.0), axis=1, keepdims=True)
            dcum_ref[...] += jnp.where(lane == 2 * p + e, d_query - d_key, 0.0)

        @pl.when(p == N_PAIR - 1)
        def _():
            dls = _cumsum_rows(dcum_ref[...], row, reverse=True)
            xv = f_ref[...] + b_ref[0:1, :]
            dx = jnp.where(lane < N_HEAD, dls * jax.nn.sigmoid(-xv), 0.0)
            dfl_ref[...] = dx.astype(BF16)
            red_ref[...] = jnp.zeros_like(red_ref)
            red_ref[0:1, :] = jnp.sum(dx, axis=0, keepdims=True)

    wide = jax.ShapeDtypeStruct((S, N_PAIR * LANE), BF16)
    cols = pl.BlockSpec((S, LANE), lambda p: (0, p))
    pair = pl.BlockSpec((2, S, LANE), lambda p: (p, 0, 0))
    return pl.pallas_call(
        body, name=name,
        out_shape=(wide, wide, wide, jax.ShapeDtypeStruct((S, LANE), BF16), jax.ShapeDtypeStruct((8, LANE), F32)),
        grid=(N_PAIR,),
        in_specs=[pl.BlockSpec((S, LANE), lambda p: (0, Z_F // LANE)), pl.BlockSpec((8, LANE), lambda p: (0, 0)),
                  pl.BlockSpec((2, LANE, S), lambda p: (p, 0, 0)), pair, pair],
        out_specs=(cols, cols, cols, pl.BlockSpec((S, LANE), lambda p: (0, 0)), pl.BlockSpec((8, LANE), lambda p: (0, 0))),
        scratch_shapes=[pltpu.VMEM((S, LANE), F32)],
        compiler_params=_params(("arbitrary",)),
    )(z, bf, dqt, dka, dva)


def _attn_fwd(qa, ka, va, name):
    tq, tk = TQ_FWD, TQ
    ratio = tq // tk

    def body(qa_ref, ka_ref, va_ref, o_ref, lse_ref):
        i = pl.program_id(1)
        lane = lax.broadcasted_iota(jnp.int32, (tq, LANE), 1)
        row = lax.broadcasted_iota(jnp.int32, (tq, tk), 0)
        col = lax.broadcasted_iota(jnp.int32, (tq, tk), 1)
        nh = HEADS_PER_STEP_FWD
        qs = [qa_ref[h] for h in range(nh)]

        def block(j, carry, masked):
            off = pl.multiple_of(j * tk, tk)
            out = []
            for h in range(nh):
                m, acc = carry[h]
                s = lax.dot_general(qs[h], ka_ref[h, pl.ds(off, tk), :], _NT, preferred_element_type=F32)
                if masked:
                    s = jnp.where(col + (j - ratio * i) * tk > row, NEG_INF, s)
                mn = jnp.maximum(m, jnp.max(s, axis=1, keepdims=True))
                p = jnp.exp(s - mn).astype(BF16)
                acc = jnp.exp(m - mn) * acc + jnp.dot(p, va_ref[h, pl.ds(off, tk), :], preferred_element_type=F32)
                out.append((mn, acc))
            return tuple(out)

        init = (jnp.full((tq, 1), NEG_INF, F32), jnp.zeros((tq, LANE), F32))
        carry = lax.fori_loop(0, ratio * i, lambda j, c: block(j, c, False), (init,) * nh)
        for d in range(ratio):
            carry = block(ratio * i + d, carry, True)
        res = []
        for h in range(nh):
            m, acc = carry[h]
            l = jnp.sum(jnp.where(_spare(lane, h % 2, 0), acc, 0.0), axis=1, keepdims=True)
            res.append((acc / l, m + jnp.log(l)))
        for g in range(nh // 2):
            o_ref[:, g * LANE:(g + 1) * LANE] = jnp.where(lane < 64, res[2 * g][0], res[2 * g + 1][0])
            lse_ref[:, g * LANE:(g + 1) * LANE] = jnp.where(lane < 64, res[2 * g][1], res[2 * g + 1][1])

    nh = HEADS_PER_STEP_FWD
    out = jax.ShapeDtypeStruct((S, N_PAIR * LANE), F32)
    wide = pl.BlockSpec((tq, 64 * nh), lambda p, i: (i, p))
    return pl.pallas_call(
        body, name=name, out_shape=(out, out), grid=(N_HEAD // nh, S // tq),
        in_specs=[pl.BlockSpec((nh, tq, LANE), lambda p, i: (p, i, 0)), pl.BlockSpec((nh, S, LANE), lambda p, i: (p, 0, 0)),
                  pl.BlockSpec((nh, S, LANE), lambda p, i: (p, 0, 0))],
        out_specs=(wide, wide),
        compiler_params=_params(("parallel", "parallel")),
    )(qa, ka, va)


def _attn_bwd(qa2, ka, va, kat, doa, name):
    nq = S // TQ

    def body(qa_ref, ka_ref, va_ref, kat_ref, doa_ref, dqt_ref, dk_ref, dv_ref):
        j = pl.program_id(1)

        @pl.when(j == 0)
        def _():
            dqt_ref[...] = jnp.zeros_like(dqt_ref)

        key = lax.broadcasted_iota(jnp.int32, (TQ, TQ), 0)
        qry = lax.broadcasted_iota(jnp.int32, (TQ, TQ), 1)
        nh = HEADS_PER_STEP
        kav, vav, katv = ([ref[h] for h in range(nh)] for ref in (ka_ref, va_ref, kat_ref))

        def block(i, carry, masked):
            off = pl.multiple_of(i * TQ, TQ)
            out = []
            for h in range(nh):
                dk_acc, dv_acc = carry[h]
                qav = qa_ref[h, pl.ds(off, TQ), :]
                doav = doa_ref[h, pl.ds(off, TQ), :]
                s_t = lax.dot_general(kav[h], qav, _NT, preferred_element_type=F32)
                if masked:
                    s_t = jnp.where(key > qry, NEG_INF, s_t)
                p_t = jnp.exp(s_t)
                ds_t = p_t * lax.dot_general(vav[h], doav, _NT, preferred_element_type=F32)
                dsb = ds_t.astype(BF16)
                dv_acc = dv_acc + jnp.dot(p_t.astype(BF16), doav, preferred_element_type=F32)
                dk_acc = dk_acc + jnp.dot(dsb, qav, preferred_element_type=F32)
                dqt_ref[h, :, pl.ds(off, TQ)] += jnp.dot(katv[h], dsb, preferred_element_type=F32)
                out.append((dk_acc, dv_acc))
            return tuple(out)

        zero = (jnp.zeros((TQ, LANE), F32), jnp.zeros((TQ, LANE), F32))
        carry = block(j, (zero,) * nh, True)
        carry = lax.fori_loop(j + 1, nq, lambda i, c: block(i, c, False), carry)
        for h in range(nh):
            dk_ref[h], dv_ref[h] = carry[h]

    nh = HEADS_PER_STEP
    full = pl.BlockSpec((nh, S, LANE), lambda p, j: (p, 0, 0))
    blk = pl.BlockSpec((nh, TQ, LANE), lambda p, j: (p, j, 0))
    acc = jax.ShapeDtypeStruct((N_HEAD, S, LANE), F32)
    return pl.pallas_call(
        body, name=name,
        out_shape=(jax.ShapeDtypeStruct((N_HEAD, LANE, S), F32), acc, acc),
        grid=(N_HEAD // nh, nq),
        in_specs=[full, blk, blk, pl.BlockSpec((nh, LANE, TQ), lambda p, j: (p, 0, j)), full],
        out_specs=(pl.BlockSpec((nh, LANE, S), lambda p, j: (p, 0, 0)), blk, blk),
        compiler_params=_params(("arbitrary", "arbitrary")),
    )(qa2, ka, va, kat, doa)


ADA_ROWS = 16


def _ada_fwd(c_pad, w_ada, b_cols, name):
    def body(c_ref, w_ref, b_ref, o_ref):
        cv = c_ref[...]
        sc = (cv * jax.nn.sigmoid(cv)).astype(BF16)
        o_ref[0] = jnp.dot(sc, w_ref[0].astype(BF16), preferred_element_type=F32) + b_ref[0, 0:1, :]

    return pl.pallas_call(
        body, name=name, out_shape=jax.ShapeDtypeStruct((DEPTH, ADA_ROWS, ADA_COLS), F32), grid=(DEPTH,),
        in_specs=[pl.BlockSpec((ADA_ROWS, D), lambda l: (0, 0)), pl.BlockSpec((1, D, ADA_COLS), lambda l: (l, 0, 0)),
                  pl.BlockSpec((1, 8, ADA_COLS), lambda l: (l, 0, 0))],
        out_specs=pl.BlockSpec((1, ADA_ROWS, ADA_COLS), lambda l: (l, 0, 0)),
        compiler_params=_params(("parallel",)),
    )(c_pad, w_ada, b_cols)


def _ada_bwd(c_pad, dmod_cols, name):
    def body(c_ref, d_ref, o_ref):
        cv = c_ref[...]
        sc = (cv * jax.nn.sigmoid(cv)).astype(BF16)
        o_ref[0] = lax.dot_general(sc, d_ref[0].astype(BF16), _TN, preferred_element_type=F32)

    return pl.pallas_call(
        body, name=name, out_shape=jax.ShapeDtypeStruct((DEPTH, D, ADA_COLS), F32), grid=(DEPTH,),
        in_specs=[pl.BlockSpec((ADA_ROWS, D), lambda l: (0, 0)), pl.BlockSpec((1, ADA_ROWS, ADA_COLS), lambda l: (l, 0, 0))],
        out_specs=pl.BlockSpec((1, D, ADA_COLS), lambda l: (l, 0, 0)),
        compiler_params=_params(("parallel",)),
    )(c_pad, dmod_cols)


def _adamw_math(w, g, m, v):
    m = B1 * m + (1.0 - B1) * g
    v = B2 * v + (1.0 - B2) * (g * g)
    m_hat = m / (1.0 - B1 ** STEP)
    v_hat = v / (1.0 - B2 ** STEP)
    delta = -LR * (m_hat / (jnp.sqrt(v_hat) + EPS) + WD * w)
    return delta, m, v


def _row_tile(rows, target=256):
    best = 8
    for t in range(8, min(rows, target) + 1, 8):
        if rows % t == 0:
            best = t
    return best


def _adamw(w, g, m, v, name):
    layers, rows, cols = w.shape
    tr = _row_tile(rows)
    spec = pl.BlockSpec((1, tr, cols), lambda l, i: (l, i, 0))

    def body(w_ref, g_ref, m_ref, v_ref, d_ref, nm_ref, nv_ref):
        d_ref[...], nm_ref[...], nv_ref[...] = _adamw_math(w_ref[...], g_ref[...], m_ref[...], v_ref[...])

    out = jax.ShapeDtypeStruct(w.shape, F32)
    return pl.pallas_call(
        body, name=name, out_shape=(out, out, out), grid=(layers, rows // tr),
        in_specs=[spec] * 4, out_specs=(spec,) * 3, compiler_params=_params(("parallel", "parallel")),
    )(w, g, m, v)


def _sum_slabs(x, name):
    n, rows, _ = x.shape
    tr = _row_tile(rows)

    def body(x_ref, o_ref):
        acc = x_ref[0]
        for k in range(1, n):
            acc = acc + x_ref[k]
        o_ref[...] = acc

    return pl.pallas_call(
        body, name=name, out_shape=jax.ShapeDtypeStruct((rows, D), F32), grid=(rows // tr,),
        in_specs=[pl.BlockSpec((n, tr, D), lambda i: (0, i, 0))], out_specs=pl.BlockSpec((tr, D), lambda i: (i, 0)),
        compiler_params=_params(("parallel",)),
    )(x)


_ANY = pl.BlockSpec(memory_space=pl.ANY)
MESH = pl.DeviceIdType.MESH


def _on_sequencer(body, out_shape, sems, operands, after, sequencer_id, name):
    n = len(operands)

    def ordered_body(*refs):
        body(*refs[:n], *refs[n + 1:])

    extra = [] if after is None else [after]
    return pl.kernel(
        body if after is None else ordered_body, out_type=out_shape,
        mesh=plsc.ScalarSubcoreMesh(axis_name="sequencer", num_cores=1), scratch_types=sems,
        compiler_params=pltpu.CompilerParams(collective_id=sequencer_id), name=name)(*operands, *extra)


def _all_gather(xs, name, sequencer_id=None, after=None):
    n = len(xs)

    def body(*refs):
        x_refs, out_refs = refs[:n], refs[n:2 * n]
        send_sems, recv_sems, local_sems = refs[2 * n:]
        x_, y_, c_ = lax.axis_index("x"), lax.axis_index("y"), lax.axis_index("c")
        me, sibling = (x_, y_, c_), (x_, y_, 1 - c_)
        chips = [(1 - x_, y_), (x_, 1 - y_), (1 - x_, 1 - y_)]
        if sequencer_id is not None:
            barrier = pltpu.get_barrier_semaphore()
            peers = [sibling] + [(*chip, pc) for chip in chips for pc in (c_, 1 - c_)]
            for peer in peers:
                pl.semaphore_signal(barrier, inc=1, device_id=peer, device_id_type=MESH)
            pl.semaphore_wait(barrier, len(peers))

        def slot(a, px, py, pc):
            return out_refs[a].at[4 * px + 2 * py + pc]

        def copy(a, k, block, to, src=None):
            return pltpu.make_async_remote_copy(
                src_ref=slot(a, *block) if src is None else src, dst_ref=slot(a, *block),
                send_sem=send_sems.at[7 * a + k], recv_sem=recv_sems.at[7 * a + k], device_id=to, device_id_type=MESH)

        mine = [pltpu.make_async_copy(x_refs[a], slot(a, *me), local_sems.at[a]) for a in range(n)]
        for cp in mine:
            cp.start()
        first = []
        for a in range(n):
            first.append(copy(a, 0, me, sibling, src=x_refs[a]))
            first += [copy(a, 1 + j, me, (*chip, c_), src=x_refs[a]) for j, chip in enumerate(chips)]
        for cp in first:
            cp.start()
        passed = []
        for j, chip in enumerate(chips):
            for a in range(n):
                copy(a, 1 + j, (*chip, c_), me).wait_recv()
                passed.append(copy(a, 4 + j, (*chip, c_), sibling))
                passed[-1].start()
        for a in range(n):
            copy(a, 0, sibling, me).wait_recv()
        for j, chip in enumerate(chips):
            for a in range(n):
                copy(a, 4 + j, (*chip, 1 - c_), me).wait_recv()
        for cp in first + passed:
            cp.wait_send()
        for cp in mine:
            cp.wait()

    out_shape = [jax.ShapeDtypeStruct((N_DEV,) + x.shape, x.dtype) for x in xs]
    sems = [pltpu.SemaphoreType.DMA((7 * n,)), pltpu.SemaphoreType.DMA((7 * n,)), pltpu.SemaphoreType.DMA((n,))]
    if sequencer_id is not None:
        return _on_sequencer(body, out_shape, sems, xs, after, sequencer_id, name)
    return pl.pallas_call(
        body, name=name, out_shape=out_shape, in_specs=[_ANY] * n, out_specs=[_ANY] * n, scratch_shapes=sems)(*xs)


def _sibling_exchange(gs, name, sequencer_id=None, after=None):
    n = len(gs)

    def body(*refs):
        g_refs, p_refs = refs[:n], refs[n:2 * n]
        send_sems, recv_sems = refs[2 * n:]
        x_, y_, c_ = lax.axis_index("x"), lax.axis_index("y"), lax.axis_index("c")
        if sequencer_id is not None:
            barrier = pltpu.get_barrier_semaphore()
            pl.semaphore_signal(barrier, inc=1, device_id=(x_, y_, 1 - c_), device_id_type=MESH)
            pl.semaphore_wait(barrier, 1)
        copies = [pltpu.make_async_remote_copy(
            src_ref=g_refs[a].at[2 * k + (1 - c_)], dst_ref=p_refs[a].at[k], send_sem=send_sems.at[4 * a + k],
            recv_sem=recv_sems.at[4 * a + k], device_id=(x_, y_, 1 - c_), device_id_type=MESH)
            for a in range(n) for k in range(4)]
        for cp in copies:
            cp.start()
        for cp in copies:
            cp.wait()

    out_shape = [jax.ShapeDtypeStruct((4,) + g.shape[1:], g.dtype) for g in gs]
    sems = [pltpu.SemaphoreType.DMA((4 * n,)), pltpu.SemaphoreType.DMA((4 * n,))]
    if sequencer_id is not None:
        return _on_sequencer(body, out_shape, sems, gs, after, sequencer_id, name)
    return pl.pallas_call(
        body, name=name, out_shape=out_shape, in_specs=[_ANY] * n, out_specs=[_ANY] * n, scratch_shapes=sems)(*gs)


def _slab_tiles(rows, cols):
    if rows % 8 == 0:
        return _row_tile(rows), cols
    return rows, 2 * LANE


def _pair_sums(g, p, route, name):
    _, rows, cols = g.shape
    tr, tc = _slab_tiles(rows, cols)

    def body(route_ref, g_ref, p_ref, t_ref):
        t_ref[...] = (g_ref[...].astype(F32) + p_ref[...].astype(F32)).astype(BF16)

    return pl.pallas_call(
        body, name=name, out_shape=jax.ShapeDtypeStruct((3, rows, cols), BF16),
        grid_spec=pltpu.PrefetchScalarGridSpec(
            num_scalar_prefetch=1, grid=(3, rows // tr, cols // tc),
            in_specs=[pl.BlockSpec((1, tr, tc), lambda r, i, j, route_ref: (2 * route_ref[1 + r] + route_ref[0], i, j)),
                      pl.BlockSpec((1, tr, tc), lambda r, i, j, route_ref: (route_ref[1 + r], i, j))],
            out_specs=pl.BlockSpec((1, tr, tc), lambda r, i, j, route_ref: (r, i, j))),
        compiler_params=_params(("parallel", "parallel", "parallel")),
    )(route, g, p)


def _chip_exchange(ts, name, sequencer_id=None, after=None):
    n = len(ts)

    def body(*refs):
        t_refs, l_refs = refs[:n], refs[n:2 * n]
        send_sems, recv_sems = refs[2 * n:]
        x_, y_, c_ = lax.axis_index("x"), lax.axis_index("y"), lax.axis_index("c")
        chips = [(1 - x_, y_), (x_, 1 - y_), (1 - x_, 1 - y_)]
        if sequencer_id is not None:
            barrier = pltpu.get_barrier_semaphore()
            for px, py in chips:
                pl.semaphore_signal(barrier, inc=1, device_id=(px, py, c_), device_id_type=MESH)
            pl.semaphore_wait(barrier, len(chips))
        copies = [pltpu.make_async_remote_copy(
            src_ref=t_refs[a].at[r], dst_ref=l_refs[a].at[r], send_sem=send_sems.at[3 * a + r],
            recv_sem=recv_sems.at[3 * a + r], device_id=(px, py, c_), device_id_type=MESH)
            for a in range(n) for r, (px, py) in enumerate(chips)]
        for cp in copies:
            cp.start()
        for cp in copies:
            cp.wait()

    out_shape = [jax.ShapeDtypeStruct((3,) + t.shape[1:], t.dtype) for t in ts]
    sems = [pltpu.SemaphoreType.DMA((3 * n,)), pltpu.SemaphoreType.DMA((3 * n,))]
    if sequencer_id is not None:
        return _on_sequencer(body, out_shape, sems, ts, after, sequencer_id, name)
    return pl.pallas_call(
        body, name=name, out_shape=out_shape, in_specs=[_ANY] * n, out_specs=[_ANY] * n, scratch_shapes=sems)(*ts)


def _reduce_adamw(gs, ps, landed, place, w, m, v, name):
    layers, rows, cols = w.shape
    assert layers == DEPTH == 2
    tr, tc = _slab_tiles(rows, cols)
    nr, nc = rows // tr, cols // tc
    spec = pl.BlockSpec((1, tr, tc), lambda l, i, j, place_ref: (l, i, j))

    def own(layer, which):
        pi, pj = (nr - 1, nc - 1) if layer == 0 else (0, 0)

        def index(l, i, j, place_ref):
            lead = 0 if which is None else place_ref[which]
            return lead, jnp.where(l == layer, i, pi), jnp.where(l == layer, j, pj)

        return pl.BlockSpec((3 if which is None else 1, tr, tc), index)

    def body(place_ref, g0_ref, p0_ref, l0_ref, g1_ref, p1_ref, l1_ref, w_ref, m_ref, v_ref,
             g_ref, d_ref, nm_ref, nv_ref):
        def update(own_ref, sib_ref, l_ref):
            g = (own_ref[0].astype(F32) + sib_ref[0].astype(F32) + l_ref[0].astype(F32) + l_ref[1].astype(F32)
                 + l_ref[2].astype(F32))
            g_ref[0] = g
            d_ref[0], nm_ref[0], nv_ref[0] = _adamw_math(w_ref[0], g, m_ref[0], v_ref[0])

        @pl.when(pl.program_id(0) == 0)
        def _():
            update(g0_ref, p0_ref, l0_ref)

        @pl.when(pl.program_id(0) == 1)
        def _():
            update(g1_ref, p1_ref, l1_ref)

    out = jax.ShapeDtypeStruct(w.shape, F32)
    return pl.pallas_call(
        body, name=name, out_shape=(out, out, out, out),
        grid_spec=pltpu.PrefetchScalarGridSpec(
            num_scalar_prefetch=1, grid=(DEPTH, nr, nc),
            in_specs=[own(0, 0), own(0, 1), own(0, None), own(1, 0), own(1, 1), own(1, None), spec, spec, spec],
            out_specs=(spec, spec, spec, spec)),
        compiler_params=_params(("arbitrary", "arbitrary", "arbitrary")),
    )(place, gs[0], ps[0], landed[0], gs[1], ps[1], landed[1], w, m, v)


def _pack(pieces, row_multiple, dtype, cols=D, rows=None):
    flat = jnp.concatenate([p.astype(dtype).reshape(-1) for p in pieces])
    if rows is None:
        rows = -(-flat.shape[0] // cols)
        rows = -(-rows // row_multiple) * row_multiple
    flat = jnp.pad(flat, (0, rows * cols - flat.shape[0]))
    return flat.reshape(rows, cols)


def _unpack(flat, shapes, lead=()):
    out, off = [], 0
    for shp in shapes:
        n = 1
        for s_ in shp:
            n *= s_
        out.append(lax.slice_in_dim(flat, off, off + n, axis=len(lead)).reshape(lead + tuple(shp)))
        off += n
    return out


WIN_STRIDE = 704
WIN_ROWS = 720
Z_TURN = 1544


def _window(wt, me, name):
    padded = jnp.pad(wt, ((0, 0), (0, WIN_ROWS - IN_SHARD), (0, 0)))

    def body(me_ref, x_ref, o_ref):
        o_ref[0] = pltpu.roll(x_ref[0], me_ref[0], axis=0).astype(BF16)

    spec = pl.BlockSpec((1, WIN_ROWS, D), lambda l, me_ref: (l, 0, 0))
    return pl.pallas_call(
        body, name=name, out_shape=jax.ShapeDtypeStruct((DEPTH, WIN_ROWS, D), BF16),
        grid_spec=pltpu.PrefetchScalarGridSpec(num_scalar_prefetch=1, grid=(DEPTH,), in_specs=[spec], out_specs=spec),
        compiler_params=_params(("parallel",)),
    )(me, padded)


def _z_rows_from_windows(win):
    over = WIN_ROWS - WIN_STRIDE
    pieces = [(0, win[0][0:WIN_STRIDE])]
    for d in range(1, N_DEV):
        base = WIN_STRIDE * d
        pieces.append((base, win[d - 1][WIN_STRIDE:WIN_ROWS] + win[d][0:over]))
        pieces.append((base + over, win[d][over:WIN_STRIDE]))
    pieces.append((WIN_STRIDE * N_DEV, win[N_DEV - 1][WIN_STRIDE:WIN_ROWS]))

    def rows(a, b):
        out = []
        for start, arr in pieces:
            lo, hi = max(a, start), min(b, start + arr.shape[0])
            if lo < hi:
                out.append(arr[lo - start:hi - start])
        return out

    pad = jnp.zeros((NZ - IN_COLS, win.shape[-1]), win.dtype)
    return jnp.concatenate(rows(Z_TURN, IN_COLS) + rows(0, Z_TURN) + [pad], axis=0)


def _in_rows_from_z(wt):
    return jnp.concatenate([wt[Z_Q:Z_Q + 1536], wt[Z_F:Z_F + 8], wt[Z_PC:Z_PC + 1024], wt[Z_G:Z_G + 3072]], axis=0)


def _pad_rows(v, rows=8):
    return jnp.pad(v, ((0, rows - v.shape[0]), (0, 0)))


def _layer_fwd(l, x, wts, gvec, mod):
    tag = f"l{l}"
    z, h = _matmul(x, wts["w_in_t"], "nt", f"in_proj_{tag}", tm=1024, tn=1152, prologue=_prenorm_prologue(0, 0, 1),
                   prologue_vecs=[gvec, mod])
    qa, ka, va, kat = _attn_prep(z, wts["b_f"], f"attn_prep_{tag}")
    qa = wts["arrive"](qa)
    o, lse = _attn_fwd(qa, ka, va, f"attn_{tag}")
    br_b = _pool_fwd(z, wts["wp_bd"], wts["pool_scale"], f"pool_{tag}")
    br_c = _conv_fwd(z, wts["conv_w"], f"conv_{tag}")
    pa = _matmul(o, wts["wa"], "nn", f"proj_a_{tag}", out_dtype=BF16)
    pb = _matmul(br_b, wts["wb"], "nn", f"proj_b_{tag}", out_dtype=BF16)
    gates = [(z, Z_G + k * D) for k in range(3)]
    pc, merged = _matmul(br_c, wts["wc"], "nn", f"proj_c_merge_{tag}", tm=512, tn=1024,
                         extra=gates + [(pa, 0), (pb, 0)], epilogue=_merge_epilogue, out_dtypes=(BF16, BF16))
    y, x1 = _matmul(merged, wts["w_out"], "nn", f"out_proj_{tag}", tm=1024, tn=D, extra=[(x, 0)],
                    vec_extra=[gvec, mod], epilogue=_postnorm_epilogue(1, 2), out_dtypes=(F32, F32))
    a, r, h2 = _matmul(x1, wts["w_ff1"], "nn", f"ff1_{tag}", b_col_shards=True, epilogue=_relu2_epilogue,
                       out_dtypes=(BF16, BF16), prologue=_prenorm_prologue(2, 3, 4), prologue_vecs=[gvec, mod])
    y2, x2 = _matmul(r, wts["w_ff2"], "nn", f"ff2_{tag}", tm=1024, tn=D, tk=1024, extra=[(x1, 0)],
                     vec_extra=[gvec, mod], epilogue=_postnorm_epilogue(3, 5), out_dtypes=(F32, F32))
    saved = dict(x=x, h=h, z=z, qa=qa, ka=ka, va=va, kat=kat, o=o, lse=lse, br_b=br_b, br_c=br_c, pa=pa, pb=pb, pc=pc,
                 merged=merged, y=y, x1=x1, h2=h2, a=a, r=r, y2=y2)
    return x2, saved


def _ffn_bwd(l, dx2, sv, wts, gvec, mod, midpoint):
    tag = f"l{l}"
    dx2 = midpoint(dx2)
    da, dy2, sums = _matmul(sv["y2"], wts["w_ff2"], "nt", f"ff2_dx_{tag}", tm=1024, extra=[(sv["a"], 0)],
                            epilogue=_relu2_bwd_epilogue, out_dtypes=(BF16,), prologue=_postnorm_bwd_prologue(3, 5),
                            prologue_tiles=[dx2], prologue_vecs=[gvec, mod], prologue_sums=True)
    red_post_ff = jnp.sum(sums.reshape(-1, 8, D), axis=0)
    d_w_ff2 = _matmul(sv["r"], dy2, "tn", f"ff2_dw_{tag}", out_dtype=GRAD_DTYPE)
    dx1, sums = _matmul(da, wts["w_ff1"], "nt", f"ff1_dx_{tag}", tm=1024, tn=D, b_col_shards=True,
                        extra=[(sv["x1"], 0), (dx2, 0)], vec_extra=[gvec, mod], epilogue=_prenorm_bwd_epilogue(2, 4),
                        out_dtypes=(F32, F32), n_row_sums=1)
    red_pre_ff = jnp.sum(sums.reshape(-1, 8, D), axis=0)
    d_w_ff1 = _matmul(sv["h2"], da, "tn", f"ff1_dw_{tag}", out_dtype=GRAD_DTYPE, out_col_shards=True)
    return dx1, [d_w_ff1, d_w_ff2.reshape(N_DEV, D_FF // N_DEV, D)], (red_pre_ff, red_post_ff)


def _mixer_bwd(l, dx1, sv, wts, gvec, mod, ffn_reds, midpoint):
    tag = f"l{l}"
    red_pre_ff, red_post_ff = ffn_reds
    gates = [(sv["z"], Z_G + k * D) for k in range(3)]
    dpa, dpb, dpc, *dgl, dy, sums = _matmul(
        sv["y"], wts["w_out"], "nt", f"out_proj_dx_{tag}", tm=512, tn=1024,
        extra=gates + [(sv["pa"], 0), (sv["pb"], 0), (sv["pc"], 0)], epilogue=_merge_bwd_epilogue,
        out_dtypes=(BF16,) * 6, prologue=_postnorm_bwd_prologue(1, 2), prologue_tiles=[dx1], prologue_vecs=[gvec, mod],
        prologue_sums=True)
    red_post_mix = jnp.sum(sums.reshape(-1, 8, D), axis=0)
    d_w_out = _matmul(sv["merged"], dy, "tn", f"out_proj_dw_{tag}", out_dtype=GRAD_DTYPE)
    dpa = midpoint(dpa)
    do = _matmul(dpa, wts["wa"], "nt", f"proj_a_dx_{tag}")
    dbr_b = _matmul(dpb, wts["wb"], "nt", f"proj_b_dx_{tag}")
    dbr_c = _matmul(dpc, wts["wc"], "nt", f"proj_c_dx_{tag}")
    d_wa = _matmul(sv["o"], dpa, "tn", f"proj_a_dw_{tag}", out_dtype=GRAD_DTYPE)
    d_wb = _matmul(sv["br_b"], dpb, "tn", f"proj_b_dw_{tag}", out_dtype=GRAD_DTYPE)
    d_wc = _matmul(sv["br_c"], dpc, "tn", f"proj_c_dw_{tag}", out_dtype=GRAD_DTYPE)
    d_w_branch = jnp.concatenate([d_wa, d_wb, d_wc], axis=0)

    dpu, d_wp_bd, red_pool = _pool_bwd(sv["z"], wts["wp_bd"], wts["pool_scale"], dbr_b, f"pool_bwd_{tag}")
    dconv, red_conv = _conv_bwd(sv["z"], wts["conv_w"], dbr_c, f"conv_bwd_{tag}")
    qa2, doa = _attn_bwd_prep(sv["qa"], sv["o"], sv["lse"], do, f"attn_bwd_prep_{tag}")
    dqt, dka, dva = _attn_bwd(qa2, sv["ka"], sv["va"], sv["kat"], doa, f"attn_bwd_{tag}")
    dq, dk, dv, dfl, red_f = _attn_bwd_post(sv["z"], wts["b_f"], dqt, dka, dva, f"attn_bwd_post_{tag}")
    dz = _concat_columns([dpu, dconv, *dgl, dq, dk, dv, dfl], f"dz_{tag}")
    dx0, sums = _matmul(dz, wts["w_in_t"], "nn", f"in_proj_dx_{tag}", tm=1024, tn=D, tk=1152,
                        extra=[(sv["x"], 0), (dx1, 0)], vec_extra=[gvec, mod], epilogue=_prenorm_bwd_epilogue(0, 1),
                        out_dtypes=(F32, F32), n_row_sums=1)
    red_pre_mix = jnp.sum(sums.reshape(-1, 8, D), axis=0)
    d_w_in_t = _matmul(dz, sv["h"], "tn", f"in_proj_dw_{tag}", out_dtype=GRAD_DTYPE, tm=1152)

    rows = D // N_DEV
    big = [_in_rows_from_z(d_w_in_t).reshape(N_DEV, IN_SHARD, D), d_w_branch.reshape(N_DEV, rows, D),
           d_w_out.reshape(N_DEV, rows, D)]
    d_w_pool = jnp.stack([d_wp_bd[64 * g:64 * (g + 1), 64 * g:64 * (g + 1)] for g in range(4)])
    small = dict(
        mod=jnp.stack([red_pre_mix[0], red_pre_mix[1], red_post_mix[0], red_pre_ff[0], red_pre_ff[1], red_post_ff[0]]),
        g_mix_pre=red_pre_mix[2], g_mix_post=red_post_mix[1], g_ff_pre=red_pre_ff[2], g_ff_post=red_post_ff[1],
        b_f=red_f[0, 0:8], w_pool=d_w_pool, pool_scale=red_pool[0], conv_w=red_conv[0:3])
    return dx0, big, small


SMALL_KEYS = ["mod", "g_mix_pre", "g_mix_post", "g_ff_pre", "g_ff_post", "b_f", "w_pool", "pool_scale", "conv_w"]
SMALL_SHAPES = [(DEPTH, 6 * D), (DEPTH, D), (DEPTH, D), (DEPTH, D), (DEPTH, D), (DEPTH, 8), (DEPTH, 4, 64, 64),
                (DEPTH, POOL_W), (DEPTH, 3, CONV_W)]


def kernel(x, c, w_ada, b_ada, g_mix_pre, g_mix_post, g_ff_pre, g_ff_post, w_in, b_f, w_pool, pool_scale, conv_w, w_branch, w_out, w_ff1, w_ff2, loss_target, m_w_ada, m_b_ada, m_g_mix_pre, m_g_mix_post, m_g_ff_pre, m_g_ff_post, m_w_in, m_b_f, m_w_pool, m_pool_scale, m_conv_w, m_w_branch, m_w_out, m_w_ff1, m_w_ff2, v_w_ada, v_b_ada, v_g_mix_pre, v_g_mix_post, v_g_ff_pre, v_g_ff_post, v_w_in, v_b_f, v_w_pool, v_pool_scale, v_conv_w, v_w_branch, v_w_out, v_w_ff1, v_w_ff2):
    ix, iy, ic = lax.axis_index("x"), lax.axis_index("y"), lax.axis_index("c")
    me = 4 * ix + 2 * iy + ic
    route = jnp.stack([ic, 2 * (1 - ix) + iy, 2 * ix + (1 - iy), 2 * (1 - ix) + (1 - iy)]).astype(jnp.int32)
    place = jnp.stack([me, 2 * ix + iy]).astype(jnp.int32)
    wt_in, mt_in, vt_in = (jnp.transpose(a, (0, 2, 1)) for a in (w_in, m_w_in, v_w_in))

    c_all = _all_gather([_pad_rows(c)], "gather_c")[0][:, 0, :]
    c_pad = _pad_rows(c_all, ADA_ROWS)
    b_cols = lax.dynamic_slice_in_dim(b_ada, me * ADA_COLS, ADA_COLS, axis=1)
    b_cols = jnp.broadcast_to(b_cols[:, None, :], (DEPTH, 8, ADA_COLS))
    mod_part = _ada_fwd(c_pad, w_ada, b_cols, "ada_fwd")
    mod_all = _all_gather([mod_part.reshape(DEPTH * ADA_ROWS, ADA_COLS)], "gather_mod")[0]
    mod_all = mod_all.reshape(N_DEV, DEPTH, ADA_ROWS, ADA_COLS)
    mod_mine = lax.dynamic_index_in_dim(mod_all, me, axis=2, keepdims=False)
    mod_mine = jnp.transpose(mod_mine, (1, 0, 2)).reshape(DEPTH, 6, D)

    cw_cols = CONV_W // N_DEV
    cw_send = jnp.pad(conv_w.reshape(DEPTH * 3, cw_cols), ((0, 8 - DEPTH * 3), (0, LANE - cw_cols)))
    win_in = _window(wt_in, place[0:1], "w_in_window")
    send = [[w[l].astype(BF16) for w in (win_in, w_branch, w_out, w_ff1, w_ff2)] for l in range(DEPTH)]
    first = _all_gather(send[0][:1], "gather_weights_l0_in", sequencer_id=1, after=mod_all)
    rest = _all_gather(send[0][1:] + [cw_send], "gather_weights_l0_rest", sequencer_id=2, after=first[0])
    first1 = _all_gather(send[1][:1], "gather_weights_l1_in", sequencer_id=3, after=first[0])
    rest1 = _all_gather(send[1][1:], "gather_weights_l1_rest", sequencer_id=12, after=first[0])
    first, (mt_in, vt_in) = lax.optimization_barrier((first, (mt_in, vt_in)))
    gathered = [first + rest[:4], first1 + rest1]
    cw_all = rest[4][:, :DEPTH * 3, :cw_cols].reshape(N_DEV, DEPTH, 3, cw_cols)

    def first_operands(l, p_in):
        wp_bd = jnp.zeros((POOL_W, POOL_W), F32)
        for g in range(4):
            wp_bd = wp_bd.at[64 * g:64 * (g + 1), 64 * g:64 * (g + 1)].set(w_pool[l, g])
        return dict(w_in_t=_z_rows_from_windows(p_in), wp_bd=wp_bd.astype(BF16),
                    pool_scale=_pad_rows(pool_scale[l][None, :]), b_f=_pad_rows(jnp.pad(b_f[l], (0, LANE - 8))[None, :]))

    def rest_operands(l, rest):
        p_br, p_out, p_ff1, p_ff2 = rest
        w_br_full = p_br.reshape(D, D)
        cw_full = jnp.transpose(cw_all[:, l], (1, 0, 2)).reshape(3, CONV_W)
        return dict(wa=w_br_full[0:A_WIDTH], wb=w_br_full[A_WIDTH:A_WIDTH + POOL_W], wc=w_br_full[A_WIDTH + POOL_W:],
                    w_out=p_out.reshape(D, D), w_ff1=p_ff1, w_ff2=p_ff2.reshape(D_FF, D), conv_w=_pad_rows(cw_full))

    xs = x[0]
    saved, layers = [], []
    for l in range(DEPTH):
        p_in, rest = gathered[l][0], gathered[l][1:5]
        if l > 0:
            xs, p_in = lax.optimization_barrier((xs, p_in))
        wts = first_operands(l, p_in)

        def arrive(t, l=l, rest=rest, wts=wts):
            if l > 0:
                t, rest = lax.optimization_barrier((t, rest))
            wts.update(rest_operands(l, rest))
            return t

        wts["arrive"] = arrive
        gvec = _pad_rows(jnp.stack([g_mix_pre[l], g_mix_post[l], g_ff_pre[l], g_ff_post[l]]))
        layers.append((wts, gvec, _pad_rows(mod_mine[l])))
        xs, sv = _layer_fwd(l, xs, *layers[l])
        saved.append(sv)
    dx, loss_part = _loss_head(xs, loss_target[0], "loss_head")
    small_grads = [None] * DEPTH
    mine, sibs, landed = ({} for _ in range(3))
    seq_id = iter(range(4, 4 + 4 * DEPTH))
    last = [gathered[DEPTH - 1][1]]

    def start(group, grads):
        mine[group] = grads
        sibs[group] = _sibling_exchange(grads, f"rs_sibling_{group}", sequencer_id=next(seq_id), after=last[0])
        last[0] = sibs[group][0]

    def finish(group, later):
        later, (grads, sib) = lax.optimization_barrier((later, (mine[group], sibs[group])))
        sends = [_pair_sums(g, p, route, f"rs_pair_sums_{group}_{k}") for k, (g, p) in enumerate(zip(grads, sib))]
        later, sends = lax.optimization_barrier((later, sends))
        landed[group] = _chip_exchange(sends, f"rs_chips_{group}", sequencer_id=next(seq_id), after=last[0])
        last[0] = landed[group][0]
        return later

    pending = None
    for l in reversed(range(DEPTH)):
        hook = (lambda da: da) if pending is None else functools.partial(finish, pending)
        dx, ffn_grads, ffn_reds = _ffn_bwd(l, dx, saved[l], *layers[l], hook)
        start(f"ffn_l{l}", ffn_grads)
        dx, mix_grads, small_grads[l] = _mixer_bwd(l, dx, saved[l], *layers[l], ffn_reds,
                                                   functools.partial(finish, f"ffn_l{l}"))
        start(f"mix_l{l}", mix_grads)
        pending = f"mix_l{l}"
    grad_x = dx[None]

    big_w = [wt_in, w_branch, w_out, w_ff1, w_ff2]
    big_m = [mt_in, m_w_branch, m_w_out, m_w_ff1, m_w_ff2]
    big_v = [vt_in, v_w_branch, v_w_out, v_w_ff1, v_w_ff2]
    where = [("mix", 0), ("mix", 1), ("mix", 2), ("ffn", 0), ("ffn", 1)]

    def reduce_and_update(k):
        group, at = where[k]
        return _reduce_adamw([mine[f"{group}_l{l}"][at] for l in range(DEPTH)],
                             [sibs[f"{group}_l{l}"][at] for l in range(DEPTH)],
                             [landed[f"{group}_l{l}"][at] for l in range(DEPTH)], place, big_w[k], big_m[k], big_v[k],
                             f"rs_sum_adamw_{k}")

    big_res = {k: list(reduce_and_update(k)) for k in (3, 4)}
    big_res[3][0] = finish(pending, big_res[3][0])

    small = {k: jnp.stack([small_grads[l][k] for l in range(DEPTH)]) for k in SMALL_KEYS}
    payload = _pack([small[k] for k in SMALL_KEYS] + [loss_part[0:1, 0:1]], 8, F32)
    small_all = _all_gather([payload], "gather_small")[0]
    dmod_all = small_all[:, 0:DEPTH * 6, :].reshape(N_DEV, DEPTH, 6 * D)
    summed = _unpack(_sum_slabs(small_all, "sum_small").reshape(-1), SMALL_SHAPES + [(1, 1)])
    sg = dict(zip(SMALL_KEYS, summed))
    loss = summed[-1][0, 0]
    dmod_cols = lax.dynamic_slice_in_dim(dmod_all, me * ADA_COLS, ADA_COLS, axis=2)
    dmod_cols = jnp.pad(jnp.transpose(dmod_cols, (1, 0, 2)), ((0, 0), (0, ADA_ROWS - N_DEV), (0, 0)))
    g_w_ada = _ada_bwd(c_pad, dmod_cols, "ada_bwd")
    g_conv_w = lax.dynamic_slice_in_dim(sg["conv_w"], me * (CONV_W // N_DEV), CONV_W // N_DEV, axis=2)

    ada_out = [g_w_ada] + list(_adamw(w_ada, g_w_ada, m_w_ada, v_w_ada, "adamw_ada"))
    rest_w = [b_ada, g_mix_pre, g_mix_post, g_ff_pre, g_ff_post, b_f, w_pool, pool_scale, conv_w]
    rest_m = [m_b_ada, m_g_mix_pre, m_g_mix_post, m_g_ff_pre, m_g_ff_post, m_b_f, m_w_pool, m_pool_scale, m_conv_w]
    rest_v = [v_b_ada, v_g_mix_pre, v_g_mix_post, v_g_ff_pre, v_g_ff_post, v_b_f, v_w_pool, v_pool_scale, v_conv_w]
    rest_g = [sg["mod"], sg["g_mix_pre"], sg["g_mix_post"], sg["g_ff_pre"], sg["g_ff_post"], sg["b_f"],
              sg["w_pool"], sg["pool_scale"], g_conv_w]
    rest_shapes = [a.shape for a in rest_w]
    upd = _adamw(_pack(rest_w, 8, F32)[None], _pack(rest_g, 8, F32)[None], _pack(rest_m, 8, F32)[None],
                 _pack(rest_v, 8, F32)[None], "adamw_rest")
    rest_out = [rest_g] + [_unpack(arr.reshape(-1), rest_shapes) for arr in upd]
    rest_out = [[ada_out[which]] + rest_out[which] for which in range(4)]

    landed[pending], rest_out = lax.optimization_barrier((landed[pending], rest_out))
    big_res.update({k: reduce_and_update(k) for k in (0, 1, 2)})
    big_out = [[jnp.transpose(big_res[k][which], (0, 2, 1)) if k == 0 else big_res[k][which] for k in range(5)]
               for which in range(4)]

    def ordered(k):
        r, b = rest_out[k], big_out[k]
        return [r[0], r[1], r[2], r[3], r[4], r[5], b[0], r[6], r[7], r[8], r[9], b[1], b[2], b[3], b[4]]

    return (loss, grad_x, *ordered(0), *ordered(1), *ordered(2), *ordered(3))
```
